```python
import math
import jax, jax.numpy as jnp
from jax import lax
import numpy as np

D_MODEL = 1024
BATCH = 32
SEQ = 2048
DEPTH = 2

HEAD_DIM = 64
A_Q_HEADS = 8
A_KV_HEADS = 2
A_GROUP = A_Q_HEADS // A_KV_HEADS
WINDOW = 128
BLOCK = 128
B_HEADS = 8
A_WIDTH = A_Q_HEADS * HEAD_DIM
A_KV_WIDTH = A_KV_HEADS * HEAD_DIM
B_WIDTH = B_HEADS * HEAD_DIM
ATTN_WIDTH = A_WIDTH + B_WIDTH
ATTN_SPLITS = (A_WIDTH, A_KV_WIDTH, A_KV_WIDTH, B_WIDTH, B_WIDTH, B_WIDTH, B_HEADS, ATTN_WIDTH)
ATTN_IN = sum(ATTN_SPLITS)
REL_BUCKETS = 32
REL_MAX_EXACT = 16
REL_MAX_DIST = 128
LRU_WIDTH = D_MODEL
LRU_BLOCKS = 8
LRU_BLOCK_W = LRU_WIDTH // LRU_BLOCKS
CONV_WIDTH = 4
LRU_C = 8.0
N_ATTN_LAYERS = (DEPTH + 1) // 2
N_LRU_LAYERS = DEPTH // 2
EPS = 1e-6

kernel_name = "hybrid_swa_fox_rglru_adaln"


def rmsnorm(x, g):
    x32 = x.astype(jnp.float32)
    y = x32 * lax.rsqrt(jnp.mean(x32 * x32, axis=-1, keepdims=True) + EPS)
    return (y * g.astype(jnp.float32)).astype(x.dtype)


def t5_causal_bucket(rel):
    n = jnp.maximum(rel, 0)
    nf = jnp.maximum(n, 1).astype(jnp.float32)
    large = REL_MAX_EXACT + (jnp.log(nf / REL_MAX_EXACT) / math.log(REL_MAX_DIST / REL_MAX_EXACT)
                             * (REL_BUCKETS - REL_MAX_EXACT)).astype(jnp.int32)
    large = jnp.minimum(large, REL_BUCKETS - 1)
    return jnp.where(n < REL_MAX_EXACT, n, large)


def swa_sink_attention(q, k, v, sinks, rel_bias):
    B, S = q.shape[0], q.shape[1]
    nb = S // BLOCK
    qb = q.reshape(B, nb, BLOCK, A_KV_HEADS, A_GROUP, HEAD_DIM)
    pad = ((0, 0), (BLOCK, 0), (0, 0), (0, 0))
    kp = jnp.pad(k, pad)[:, :S].reshape(B, nb, BLOCK, A_KV_HEADS, HEAD_DIM)
    vp = jnp.pad(v, pad)[:, :S].reshape(B, nb, BLOCK, A_KV_HEADS, HEAD_DIM)
    kb = jnp.concatenate([kp, k.reshape(B, nb, BLOCK, A_KV_HEADS, HEAD_DIM)], axis=2)
    vb = jnp.concatenate([vp, v.reshape(B, nb, BLOCK, A_KV_HEADS, HEAD_DIM)], axis=2)
    scores = jnp.einsum('bnqhgd,bnkhd->bhgnqk', qb, kb).astype(jnp.float32) * (HEAD_DIM ** -0.5)
    qi = jnp.arange(BLOCK)[:, None]
    kj = jnp.arange(2 * BLOCK)[None, :]
    rel = qi - kj + BLOCK
    bias = rel_bias.astype(jnp.float32)[t5_causal_bucket(rel)]
    bias = jnp.transpose(bias, (2, 0, 1)).reshape(A_KV_HEADS, A_GROUP, 1, BLOCK, 2 * BLOCK)
    valid = (rel >= 0) & (rel < WINDOW)
    first = (jnp.arange(nb)[:, None, None] == 0) & (kj[None] < BLOCK)
    mask = valid[None] & ~first
    logits = jnp.where(mask, scores + bias, -jnp.inf)
    sink = jnp.broadcast_to(sinks.astype(jnp.float32).reshape(1, A_KV_HEADS, A_GROUP, 1, 1, 1),
                            logits.shape[:-1] + (1,))
    probs = jax.nn.softmax(jnp.concatenate([logits, sink], axis=-1), axis=-1)[..., :-1]
    out = jnp.einsum('bhgnqk,bnkhd->bnqhgd', probs.astype(v.dtype), vb)
    return out.reshape(B, S, A_WIDTH)


def forgetting_attention(q, k, v, log_f):
    B, S = q.shape[0], q.shape[1]
    nb = S // BLOCK
    F = jnp.cumsum(log_f, axis=1)
    outs = []
    for n in range(nb):
        q0, kend = n * BLOCK, (n + 1) * BLOCK
        qs = q[:, q0:kend]
        ks, vs = k[:, :kend], v[:, :kend]
        s = jnp.einsum('bqhd,bkhd->bhqk', qs, ks).astype(jnp.float32) * (HEAD_DIM ** -0.5)
        decay = jnp.transpose(F[:, q0:kend], (0, 2, 1))[..., :, None] - jnp.transpose(F[:, :kend], (0, 2, 1))[..., None, :]
        tpos = q0 + jnp.arange(BLOCK)[:, None]
        spos = jnp.arange(kend)[None, :]
        p = jax.nn.softmax(jnp.where(spos <= tpos, s + decay, -jnp.inf), axis=-1)
        outs.append(jnp.einsum('bhqk,bkhd->bqhd', p.astype(v.dtype), vs))
    return jnp.concatenate(outs, axis=1).reshape(B, S, B_WIDTH)


def attention_mixer(h, w_in, sinks, b_f, w_out, rel_bias):
    B, S, _ = h.shape
    proj = h @ w_in
    idx = np.cumsum(ATTN_SPLITS)[:-1].tolist()
    a_q, a_k, a_v, b_q, b_k, b_v, f_logit, gate = jnp.split(proj, idx, axis=-1)
    a_out = swa_sink_attention(a_q.reshape(B, S, A_Q_HEADS, HEAD_DIM),
                               a_k.reshape(B, S, A_KV_HEADS, HEAD_DIM),
                               a_v.reshape(B, S, A_KV_HEADS, HEAD_DIM), sinks, rel_bias)
    log_f = jax.nn.log_sigmoid((f_logit + b_f).astype(jnp.float32))
    b_out = forgetting_attention(b_q.reshape(B, S, B_HEADS, HEAD_DIM),
                                 b_k.reshape(B, S, B_HEADS, HEAD_DIM),
                                 b_v.reshape(B, S, B_HEADS, HEAD_DIM), log_f)
    y = jnp.concatenate([a_out, b_out], axis=-1) * jax.nn.silu(gate)
    return y @ w_out


def rglru_mixer(h, w_in, conv_w, conv_b, w_a, b_a, w_x, b_x, lam, w_out):
    B, S, _ = h.shape
    proj = h @ w_in
    xr, gate = proj[..., :LRU_WIDTH], proj[..., LRU_WIDTH:]
    xp = jnp.pad(xr, ((0, 0), (CONV_WIDTH - 1, 0), (0, 0)))
    xc = conv_b
    for j in range(CONV_WIDTH):
        xc = xc + xp[:, j:j + S] * conv_w[j]
    xblk = xc.reshape(B, S, LRU_BLOCKS, LRU_BLOCK_W)
    r = jax.nn.sigmoid((jnp.einsum('bsnw,nwv->bsnv', xblk, w_a).reshape(B, S, LRU_WIDTH) + b_a).astype(jnp.float32))
    i = jax.nn.sigmoid((jnp.einsum('bsnw,nwv->bsnv', xblk, w_x).reshape(B, S, LRU_WIDTH) + b_x).astype(jnp.float32))
    log_a = -LRU_C * r * jax.nn.softplus(-lam.astype(jnp.float32))
    a = jnp.exp(log_a)
    u = jnp.sqrt(-jnp.expm1(2.0 * log_a)) * (i * xc.astype(jnp.float32))

    def step(state, inp):
        a_t, u_t = inp
        state = a_t * state + u_t
        return state, state

    _, hs = lax.scan(step, jnp.zeros((B, LRU_WIDTH), jnp.float32),
                     (jnp.transpose(a, (1, 0, 2)), jnp.transpose(u, (1, 0, 2))))
    y = jnp.transpose(hs, (1, 0, 2)).astype(h.dtype) * jax.nn.silu(gate)
    return y @ w_out


def _fwd_setup_inputs(seed: int = 0) -> dict:
    key = jax.random.key(seed)
    ks = jax.random.split(key, 22)
    nrm = lambda k, shape, s: jax.random.normal(k, shape, jnp.float32) * s
    a0 = jax.random.uniform(ks[19], (N_LRU_LAYERS, LRU_WIDTH), jnp.float32, 0.9, 0.999) ** (1.0 / LRU_C)
    return {
        "x": nrm(ks[0], (BATCH, SEQ, D_MODEL), 1.0),
        "c": nrm(ks[1], (BATCH, D_MODEL), 1.0),
        "rel_bias": nrm(ks[2], (REL_BUCKETS, A_Q_HEADS), 0.2),
        "norm_g": 1.0 + nrm(ks[3], (DEPTH, D_MODEL), 0.05),
        "ada_w": nrm(ks[4], (DEPTH, D_MODEL, 3 * D_MODEL), 0.3 * D_MODEL ** -0.5),
        "ada_b": nrm(ks[5], (DEPTH, 3 * D_MODEL), 0.02),
        "attn_w_in": nrm(ks[6], (N_ATTN_LAYERS, D_MODEL, ATTN_IN), D_MODEL ** -0.5),
        "attn_sinks": nrm(ks[7], (N_ATTN_LAYERS, A_Q_HEADS), 0.5),
        "attn_b_f": jax.random.uniform(ks[8], (N_ATTN_LAYERS, B_HEADS), jnp.float32, 1.0, 4.0),
        "attn_w_out": nrm(ks[9], (N_ATTN_LAYERS, ATTN_WIDTH, D_MODEL), ATTN_WIDTH ** -0.5),
        "lru_w_in": nrm(ks[10], (N_LRU_LAYERS, D_MODEL, 2 * LRU_WIDTH), D_MODEL ** -0.5),
        "lru_conv_w": nrm(ks[11], (N_LRU_LAYERS, CONV_WIDTH, LRU_WIDTH), CONV_WIDTH ** -0.5),
        "lru_conv_b": nrm(ks[12], (N_LRU_LAYERS, LRU_WIDTH), 0.02),
        "lru_w_a": nrm(ks[13], (N_LRU_LAYERS, LRU_BLOCKS, LRU_BLOCK_W, LRU_BLOCK_W), LRU_BLOCK_W ** -0.5),
        "lru_b_a": nrm(ks[14], (N_LRU_LAYERS, LRU_WIDTH), 0.02),
        "lru_w_x": nrm(ks[15], (N_LRU_LAYERS, LRU_BLOCKS, LRU_BLOCK_W, LRU_BLOCK_W), LRU_BLOCK_W ** -0.5),
        "lru_b_x": nrm(ks[16], (N_LRU_LAYERS, LRU_WIDTH), 0.02),
        "lru_lambda": jnp.log(a0) - jnp.log1p(-a0),
        "lru_w_out": nrm(ks[17], (N_LRU_LAYERS, LRU_WIDTH, D_MODEL), LRU_WIDTH ** -0.5),
        "final_g": 1.0 + nrm(ks[18], (D_MODEL,), 0.05),
    }


def _fwd_reference(x, c, rel_bias, norm_g, ada_w, ada_b, attn_w_in, attn_sinks, attn_b_f, attn_w_out,
              lru_w_in, lru_conv_w, lru_conv_b, lru_w_a, lru_b_a, lru_w_x, lru_b_x, lru_lambda,
              lru_w_out, final_g):
    c_act = jax.nn.silu(c)
    for layer in range(DEPTH):
        mod = c_act @ ada_w[layer] + ada_b[layer]
        shift, scale, gate = jnp.split(mod, 3, axis=-1)
        h = rmsnorm(x, norm_g[layer]) * (1.0 + scale[:, None, :]) + shift[:, None, :]
        if layer % 2 == 0:
            j = layer // 2
            y = attention_mixer(h, attn_w_in[j], attn_sinks[j], attn_b_f[j], attn_w_out[j], rel_bias)
        else:
            j = layer // 2
            y = rglru_mixer(h, lru_w_in[j], lru_conv_w[j], lru_conv_b[j], lru_w_a[j], lru_b_a[j],
                            lru_w_x[j], lru_b_x[j], lru_lambda[j], lru_w_out[j])
        x = x + gate[:, None, :] * y
    return rmsnorm(x, final_g)


import jax as _jax
import jax.numpy as _jnp

TWIN_FORMAT = 'train_step'
FWD_PARAMS = ['x', 'c', 'rel_bias', 'norm_g', 'ada_w', 'ada_b', 'attn_w_in', 'attn_sinks', 'attn_b_f', 'attn_w_out', 'lru_w_in', 'lru_conv_w', 'lru_conv_b', 'lru_w_a', 'lru_b_a', 'lru_w_x', 'lru_b_x', 'lru_lambda', 'lru_w_out', 'final_g']
TWIN_WEIGHTS = ['rel_bias', 'norm_g', 'ada_w', 'ada_b', 'attn_w_in', 'attn_sinks', 'attn_b_f', 'attn_w_out', 'lru_w_in', 'lru_conv_w', 'lru_conv_b', 'lru_w_a', 'lru_b_a', 'lru_w_x', 'lru_b_x', 'lru_lambda', 'lru_w_out', 'final_g']
TWIN_DIFF_INPUT = 'x'
TWIN_INPUTS = ['x', 'c', 'rel_bias', 'norm_g', 'ada_w', 'ada_b', 'attn_w_in', 'attn_sinks', 'attn_b_f', 'attn_w_out', 'lru_w_in', 'lru_conv_w', 'lru_conv_b', 'lru_w_a', 'lru_b_a', 'lru_w_x', 'lru_b_x', 'lru_lambda', 'lru_w_out', 'final_g', 'loss_target', 'm_rel_bias', 'm_norm_g', 'm_ada_w', 'm_ada_b', 'm_attn_w_in', 'm_attn_sinks', 'm_attn_b_f', 'm_attn_w_out', 'm_lru_w_in', 'm_lru_conv_w', 'm_lru_conv_b', 'm_lru_w_a', 'm_lru_b_a', 'm_lru_w_x', 'm_lru_b_x', 'm_lru_lambda', 'm_lru_w_out', 'm_final_g', 'v_rel_bias', 'v_norm_g', 'v_ada_w', 'v_ada_b', 'v_attn_w_in', 'v_attn_sinks', 'v_attn_b_f', 'v_attn_w_out', 'v_lru_w_in', 'v_lru_conv_w', 'v_lru_conv_b', 'v_lru_w_a', 'v_lru_b_a', 'v_lru_w_x', 'v_lru_b_x', 'v_lru_lambda', 'v_lru_w_out', 'v_final_g']
TWIN_OUTPUTS = ['loss', 'grad_x', 'grad_rel_bias', 'grad_norm_g', 'grad_ada_w', 'grad_ada_b', 'grad_attn_w_in', 'grad_attn_sinks', 'grad_attn_b_f', 'grad_attn_w_out', 'grad_lru_w_in', 'grad_lru_conv_w', 'grad_lru_conv_b', 'grad_lru_w_a', 'grad_lru_b_a', 'grad_lru_w_x', 'grad_lru_b_x', 'grad_lru_lambda', 'grad_lru_w_out', 'grad_final_g', 'delta_rel_bias', 'delta_norm_g', 'delta_ada_w', 'delta_ada_b', 'delta_attn_w_in', 'delta_attn_sinks', 'delta_attn_b_f', 'delta_attn_w_out', 'delta_lru_w_in', 'delta_lru_conv_w', 'delta_lru_conv_b', 'delta_lru_w_a', 'delta_lru_b_a', 'delta_lru_w_x', 'delta_lru_b_x', 'delta_lru_lambda', 'delta_lru_w_out', 'delta_final_g', 'new_m_rel_bias', 'new_m_norm_g', 'new_m_ada_w', 'new_m_ada_b', 'new_m_attn_w_in', 'new_m_attn_sinks', 'new_m_attn_b_f', 'new_m_attn_w_out', 'new_m_lru_w_in', 'new_m_lru_conv_w', 'new_m_lru_conv_b', 'new_m_lru_w_a', 'new_m_lru_b_a', 'new_m_lru_w_x', 'new_m_lru_b_x', 'new_m_lru_lambda', 'new_m_lru_w_out', 'new_m_final_g', 'new_v_rel_bias', 'new_v_norm_g', 'new_v_ada_w', 'new_v_ada_b', 'new_v_attn_w_in', 'new_v_attn_sinks', 'new_v_attn_b_f', 'new_v_attn_w_out', 'new_v_lru_w_in', 'new_v_lru_conv_w', 'new_v_lru_conv_b', 'new_v_lru_w_a', 'new_v_lru_b_a', 'new_v_lru_w_x', 'new_v_lru_b_x', 'new_v_lru_lambda', 'new_v_lru_w_out', 'new_v_final_g']
TWIN_LEAF_KINDS = {'loss': 'loss', 'grad_x': 'grad_x', 'grad_rel_bias': 'grad_w', 'grad_norm_g': 'grad_w', 'grad_ada_w': 'grad_w', 'grad_ada_b': 'grad_w', 'grad_attn_w_in': 'grad_w', 'grad_attn_sinks': 'grad_w', 'grad_attn_b_f': 'grad_w', 'grad_attn_w_out': 'grad_w', 'grad_lru_w_in': 'grad_w', 'grad_lru_conv_w': 'grad_w', 'grad_lru_conv_b': 'grad_w', 'grad_lru_w_a': 'grad_w', 'grad_lru_b_a': 'grad_w', 'grad_lru_w_x': 'grad_w', 'grad_lru_b_x': 'grad_w', 'grad_lru_lambda': 'grad_w', 'grad_lru_w_out': 'grad_w', 'grad_final_g': 'grad_w', 'delta_rel_bias': 'delta_w', 'delta_norm_g': 'delta_w', 'delta_ada_w': 'delta_w', 'delta_ada_b': 'delta_w', 'delta_attn_w_in': 'delta_w', 'delta_attn_sinks': 'delta_w', 'delta_attn_b_f': 'delta_w', 'delta_attn_w_out': 'delta_w', 'delta_lru_w_in': 'delta_w', 'delta_lru_conv_w': 'delta_w', 'delta_lru_conv_b': 'delta_w', 'delta_lru_w_a': 'delta_w', 'delta_lru_b_a': 'delta_w', 'delta_lru_w_x': 'delta_w', 'delta_lru_b_x': 'delta_w', 'delta_lru_lambda': 'delta_w', 'delta_lru_w_out': 'delta_w', 'delta_final_g': 'delta_w', 'new_m_rel_bias': 'new_m', 'new_m_norm_g': 'new_m', 'new_m_ada_w': 'new_m', 'new_m_ada_b': 'new_m', 'new_m_attn_w_in': 'new_m', 'new_m_attn_sinks': 'new_m', 'new_m_attn_b_f': 'new_m', 'new_m_attn_w_out': 'new_m', 'new_m_lru_w_in': 'new_m', 'new_m_lru_conv_w': 'new_m', 'new_m_lru_conv_b': 'new_m', 'new_m_lru_w_a': 'new_m', 'new_m_lru_b_a': 'new_m', 'new_m_lru_w_x': 'new_m', 'new_m_lru_b_x': 'new_m', 'new_m_lru_lambda': 'new_m', 'new_m_lru_w_out': 'new_m', 'new_m_final_g': 'new_m', 'new_v_rel_bias': 'new_v', 'new_v_norm_g': 'new_v', 'new_v_ada_w': 'new_v', 'new_v_ada_b': 'new_v', 'new_v_attn_w_in': 'new_v', 'new_v_attn_sinks': 'new_v', 'new_v_attn_b_f': 'new_v', 'new_v_attn_w_out': 'new_v', 'new_v_lru_w_in': 'new_v', 'new_v_lru_conv_w': 'new_v', 'new_v_lru_conv_b': 'new_v', 'new_v_lru_w_a': 'new_v', 'new_v_lru_b_a': 'new_v', 'new_v_lru_w_x': 'new_v', 'new_v_lru_b_x': 'new_v', 'new_v_lru_lambda': 'new_v', 'new_v_lru_w_out': 'new_v', 'new_v_final_g': 'new_v'}


def _forward(args):
    return _fwd_reference(*[args[k] for k in FWD_PARAMS])


def _output_shape():
    out = _jax.eval_shape(lambda: _forward(_fwd_setup_inputs(0)))
    return out.shape, out.dtype

N_MICROBATCH = 1
ADAM_LR = 0.001
ADAM_B1 = 0.9
ADAM_B2 = 0.999
ADAM_EPS = 1e-08
ADAM_WD = 0.01
ADAM_STEP = 10
PER_EXAMPLE_BATCH_AXIS = {'x': 0, 'c': 0, 'loss_target': 0}
SHARED_INPUTS = []
_WEIGHT_DTYPES = {'rel_bias': _jnp.float32, 'norm_g': _jnp.float32, 'ada_w': _jnp.float32, 'ada_b': _jnp.float32, 'attn_w_in': _jnp.float32, 'attn_sinks': _jnp.float32, 'attn_b_f': _jnp.float32, 'attn_w_out': _jnp.float32, 'lru_w_in': _jnp.float32, 'lru_conv_w': _jnp.float32, 'lru_conv_b': _jnp.float32, 'lru_w_a': _jnp.float32, 'lru_b_a': _jnp.float32, 'lru_w_x': _jnp.float32, 'lru_b_x': _jnp.float32, 'lru_lambda': _jnp.float32, 'lru_w_out': _jnp.float32, 'final_g': _jnp.float32}
MOMENT_SCALE = {'rel_bias': 7.613743e-03, 'norm_g': 3.575216e-02, 'ada_w': 1.073820e-01, 'ada_b': 1.806040e-01, 'attn_w_in': 1.136312e-02, 'attn_sinks': 4.552539e-03, 'attn_b_f': 9.236978e-02, 'attn_w_out': 1.162797e-02, 'lru_w_in': 4.885277e-02, 'lru_conv_w': 5.934172e-02, 'lru_conv_b': 3.122780e-01, 'lru_w_a': 7.727527e-03, 'lru_b_a': 1.241150e-02, 'lru_w_x': 1.484605e-02, 'lru_b_x': 3.012938e-02, 'lru_lambda': 3.178959e-02, 'lru_w_out': 5.026595e-02, 'final_g': 6.406185e+01}


def _to_microbatches(a, axis):
    t = _jnp.moveaxis(a, axis, 0)
    t = t.reshape((N_MICROBATCH, t.shape[0] // N_MICROBATCH) + t.shape[1:])
    return _jnp.moveaxis(t, 1, axis + 1)


def setup_inputs(seed: int = 0) -> dict:
    inp = _fwd_setup_inputs(seed)
    key = _jax.random.fold_in(_jax.random.key(seed), 7919)
    shape, _ = _output_shape()
    out = dict(inp)
    out["loss_target"] = _jax.random.normal(_jax.random.fold_in(key, 0), shape, _jnp.float32)
    for i, name in enumerate(TWIN_WEIGHTS):
        w = inp[name].astype(_jnp.float32)
        if MOMENT_SCALE is None:
            s = _jnp.sqrt(_jnp.mean(_jnp.square(w)) + 1e-30)
        else:
            s = MOMENT_SCALE[name]
        km, kv = _jax.random.split(_jax.random.fold_in(key, i + 1))
        out[name] = w
        out["m_" + name] = s * _jax.random.normal(km, w.shape, _jnp.float32)
        out["v_" + name] = (s * s) * _jax.random.uniform(kv, w.shape, _jnp.float32, 0.5, 1.5)
    if N_MICROBATCH > 1:
        for name, axis in PER_EXAMPLE_BATCH_AXIS.items():
            out[name] = _to_microbatches(out[name], axis)
    return {'x': out['x'], 'c': out['c'], 'rel_bias': out['rel_bias'], 'norm_g': out['norm_g'], 'ada_w': out['ada_w'], 'ada_b': out['ada_b'], 'attn_w_in': out['attn_w_in'], 'attn_sinks': out['attn_sinks'], 'attn_b_f': out['attn_b_f'], 'attn_w_out': out['attn_w_out'], 'lru_w_in': out['lru_w_in'], 'lru_conv_w': out['lru_conv_w'], 'lru_conv_b': out['lru_conv_b'], 'lru_w_a': out['lru_w_a'], 'lru_b_a': out['lru_b_a'], 'lru_w_x': out['lru_w_x'], 'lru_b_x': out['lru_b_x'], 'lru_lambda': out['lru_lambda'], 'lru_w_out': out['lru_w_out'], 'final_g': out['final_g'], 'loss_target': out['loss_target'], 'm_rel_bias': out['m_rel_bias'], 'm_norm_g': out['m_norm_g'], 'm_ada_w': out['m_ada_w'], 'm_ada_b': out['m_ada_b'], 'm_attn_w_in': out['m_attn_w_in'], 'm_attn_sinks': out['m_attn_sinks'], 'm_attn_b_f': out['m_attn_b_f'], 'm_attn_w_out': out['m_attn_w_out'], 'm_lru_w_in': out['m_lru_w_in'], 'm_lru_conv_w': out['m_lru_conv_w'], 'm_lru_conv_b': out['m_lru_conv_b'], 'm_lru_w_a': out['m_lru_w_a'], 'm_lru_b_a': out['m_lru_b_a'], 'm_lru_w_x': out['m_lru_w_x'], 'm_lru_b_x': out['m_lru_b_x'], 'm_lru_lambda': out['m_lru_lambda'], 'm_lru_w_out': out['m_lru_w_out'], 'm_final_g': out['m_final_g'], 'v_rel_bias': out['v_rel_bias'], 'v_norm_g': out['v_norm_g'], 'v_ada_w': out['v_ada_w'], 'v_ada_b': out['v_ada_b'], 'v_attn_w_in': out['v_attn_w_in'], 'v_attn_sinks': out['v_attn_sinks'], 'v_attn_b_f': out['v_attn_b_f'], 'v_attn_w_out': out['v_attn_w_out'], 'v_lru_w_in': out['v_lru_w_in'], 'v_lru_conv_w': out['v_lru_conv_w'], 'v_lru_conv_b': out['v_lru_conv_b'], 'v_lru_w_a': out['v_lru_w_a'], 'v_lru_b_a': out['v_lru_b_a'], 'v_lru_w_x': out['v_lru_w_x'], 'v_lru_b_x': out['v_lru_b_x'], 'v_lru_lambda': out['v_lru_lambda'], 'v_lru_w_out': out['v_lru_w_out'], 'v_final_g': out['v_final_g']}


def _loss(weights, diff, rest, loss_target):
    with _jax.named_scope("forward"):
        args = {**rest, TWIN_DIFF_INPUT: diff, **{k: w.astype(_WEIGHT_DTYPES[k]) for k, w in weights.items()}}
        y = _forward(args)
    with _jax.named_scope("loss_head"):
        err = _jnp.square(y.astype(_jnp.float32) - loss_target)
        return 0.5 * _jnp.sum(_jnp.mean(err, axis=-1)) if err.ndim else 0.5 * err


def _adamw(w, g, m, v):
    m = ADAM_B1 * m + (1.0 - ADAM_B1) * g
    v = ADAM_B2 * v + (1.0 - ADAM_B2) * _jnp.square(g)
    m_hat = m / (1.0 - ADAM_B1 ** ADAM_STEP)
    v_hat = v / (1.0 - ADAM_B2 ** ADAM_STEP)
    delta = -ADAM_LR * (m_hat / (_jnp.sqrt(v_hat) + ADAM_EPS) + ADAM_WD * w)
    return delta, m, v


def reference(x, c, rel_bias, norm_g, ada_w, ada_b, attn_w_in, attn_sinks, attn_b_f, attn_w_out, lru_w_in, lru_conv_w, lru_conv_b, lru_w_a, lru_b_a, lru_w_x, lru_b_x, lru_lambda, lru_w_out, final_g, loss_target, m_rel_bias, m_norm_g, m_ada_w, m_ada_b, m_attn_w_in, m_attn_sinks, m_attn_b_f, m_attn_w_out, m_lru_w_in, m_lru_conv_w, m_lru_conv_b, m_lru_w_a, m_lru_b_a, m_lru_w_x, m_lru_b_x, m_lru_lambda, m_lru_w_out, m_final_g, v_rel_bias, v_norm_g, v_ada_w, v_ada_b, v_attn_w_in, v_attn_sinks, v_attn_b_f, v_attn_w_out, v_lru_w_in, v_lru_conv_w, v_lru_conv_b, v_lru_w_a, v_lru_b_a, v_lru_w_x, v_lru_b_x, v_lru_lambda, v_lru_w_out, v_final_g):
    given = dict(x=x, c=c, rel_bias=rel_bias, norm_g=norm_g, ada_w=ada_w, ada_b=ada_b, attn_w_in=attn_w_in, attn_sinks=attn_sinks, attn_b_f=attn_b_f, attn_w_out=attn_w_out, lru_w_in=lru_w_in, lru_conv_w=lru_conv_w, lru_conv_b=lru_conv_b, lru_w_a=lru_w_a, lru_b_a=lru_b_a, lru_w_x=lru_w_x, lru_b_x=lru_b_x, lru_lambda=lru_lambda, lru_w_out=lru_w_out, final_g=final_g, loss_target=loss_target, m_rel_bias=m_rel_bias, m_norm_g=m_norm_g, m_ada_w=m_ada_w, m_ada_b=m_ada_b, m_attn_w_in=m_attn_w_in, m_attn_sinks=m_attn_sinks, m_attn_b_f=m_attn_b_f, m_attn_w_out=m_attn_w_out, m_lru_w_in=m_lru_w_in, m_lru_conv_w=m_lru_conv_w, m_lru_conv_b=m_lru_conv_b, m_lru_w_a=m_lru_w_a, m_lru_b_a=m_lru_b_a, m_lru_w_x=m_lru_w_x, m_lru_b_x=m_lru_b_x, m_lru_lambda=m_lru_lambda, m_lru_w_out=m_lru_w_out, m_final_g=m_final_g, v_rel_bias=v_rel_bias, v_norm_g=v_norm_g, v_ada_w=v_ada_w, v_ada_b=v_ada_b, v_attn_w_in=v_attn_w_in, v_attn_sinks=v_attn_sinks, v_attn_b_f=v_attn_b_f, v_attn_w_out=v_attn_w_out, v_lru_w_in=v_lru_w_in, v_lru_conv_w=v_lru_conv_w, v_lru_conv_b=v_lru_conv_b, v_lru_w_a=v_lru_w_a, v_lru_b_a=v_lru_b_a, v_lru_w_x=v_lru_w_x, v_lru_b_x=v_lru_b_x, v_lru_lambda=v_lru_lambda, v_lru_w_out=v_lru_w_out, v_final_g=v_final_g)
    weights = {n: given[n] for n in TWIN_WEIGHTS}
    shared = {n: given[n] for n in SHARED_INPUTS}
    per_example = {n: given[n] for n in ['x', 'c']}
    grad_fn = _jax.value_and_grad(_loss, argnums=(0, 1))

    def one_microbatch(ex, loss_target):
        ex = dict(ex)
        diff = ex.pop(TWIN_DIFF_INPUT)
        return grad_fn(weights, diff, {**shared, **ex}, loss_target)

    if N_MICROBATCH == 1:
        loss, (grad_w, grad_x) = one_microbatch(per_example, given["loss_target"])
    else:
        def body(carry, xs):
            loss_sum, grad_sum = carry
            l_k, (gw_k, gx_k) = one_microbatch(xs[0], xs[1])
            with _jax.named_scope("update"):
                return (loss_sum + l_k, _jax.tree.map(_jnp.add, grad_sum, gw_k)), gx_k

        init = (_jnp.zeros((), _jnp.float32), _jax.tree.map(_jnp.zeros_like, weights))
        (loss, grad_w), grad_x = _jax.lax.scan(body, init, (per_example, given["loss_target"]))
    with _jax.named_scope("update"):
        delta_w, new_m, new_v = {}, {}, {}
        for n in TWIN_WEIGHTS:
            delta_w[n], new_m[n], new_v[n] = _adamw(weights[n], grad_w[n], given["m_" + n], given["v_" + n])
    return (loss, grad_x, *[grad_w[n] for n in TWIN_WEIGHTS], *[delta_w[n] for n in TWIN_WEIGHTS],
            *[new_m[n] for n in TWIN_WEIGHTS], *[new_v[n] for n in TWIN_WEIGHTS])
```

```python
import functools
import math

import jax
import jax.numpy as jnp
from jax import lax
from jax.experimental import pallas as pl
from jax.experimental.pallas import tpu as pltpu

F32 = jnp.float32
BF16 = jnp.bfloat16
HI = lax.Precision.HIGHEST
MESH = pl.DeviceIdType.MESH

N_DEV = 8
D_MODEL = 1024
HEAD_DIM = 64
N_HEADS = 8
KV_GROUP = 4
BLOCK = 128
REL_BUCKETS = 32
REL_MAX_EXACT = 16
REL_MAX_DIST = 128
LRU_BLOCKS = 8
LRU_BLOCK_W = 128
LRU_C = 8.0
EPS = 1e-6
SCALE = HEAD_DIM ** -0.5
NEG = -1e30

ADAM_LR = 0.001
ADAM_B1 = 0.9
ADAM_B2 = 0.999
ADAM_EPS = 1e-08
ADAM_WD = 0.01
ADAM_STEP = 10

C_BQ, C_BK, C_BV, C_AQ, C_GATE, C_AK, C_AV = 0, 512, 1024, 1536, 2048, 3072, 3200
N_MAIN = 3328
SHARD_W_IN = 417
SHARD_W_PAD = 512

TM = 256
TQ = 256
TC = 256
VMEM_BIG = 56 * 1024 * 1024
VMEM_MID = 40 * 1024 * 1024


def _pallas(body, **kw):
    return pl.pallas_call(body, **kw)


def _cp(sem=None, vmem=None):
    kw = {}
    if sem is not None:
        kw["dimension_semantics"] = sem
    if vmem is not None:
        kw["vmem_limit_bytes"] = vmem
    return pltpu.CompilerParams(**kw)


def _nn(a, b, precision=None):
    return jnp.dot(a, b, preferred_element_type=F32, precision=precision)


def _nt(a, b, precision=None):
    return lax.dot_general(a, b, (((1,), (1,)), ((), ())), preferred_element_type=F32, precision=precision)


def _tn(a, b, precision=None):
    return lax.dot_general(a, b, (((0,), (0,)), ((), ())), preferred_element_type=F32, precision=precision)


def _sigmoid(x):
    return 1.0 / (1.0 + jnp.exp(-x))


def _silu(x):
    return x * _sigmoid(x)


def _dsilu(x):
    s = _sigmoid(x)
    return s * (1.0 + x * (1.0 - s))


def _neg_expm1(x):
    poly = x * (1.0 + x * (0.5 + x * (1.0 / 6.0 + x * (1.0 / 24.0))))
    return -jnp.where(jnp.abs(x) < 0.05, poly, jnp.exp(x) - 1.0)


def _col(tile, idx):
    lane = lax.broadcasted_iota(jnp.int32, tile.shape, 1)
    return jnp.sum(jnp.where(lane == idx, tile, 0.0), axis=1, keepdims=True)


def _row(tile, idx):
    sub = lax.broadcasted_iota(jnp.int32, tile.shape, 0)
    return jnp.sum(jnp.where(sub == idx, tile, 0.0), axis=0, keepdims=True)


def _exchange(name, gathers, scatters):
    ng, n = len(gathers), len(gathers) + len(scatters)
    ins = list(gathers) + list(scatters)

    def body(*refs):
        in_refs, out_refs = refs[:n], refs[n:2 * n]
        send_sems, recv_sems, loc_sems = refs[2 * n:]
        x, y, c = lax.axis_index("x"), lax.axis_index("y"), lax.axis_index("c")
        me = 4 * x + 2 * y + c

        def peer(r):
            px = 1 - x if r & 4 else x
            py = 1 - y if r & 2 else y
            pc = 1 - c if r & 1 else c
            return (px, py, pc), 4 * px + 2 * py + pc

        local, sends, recvs = [], [], []
        for k in range(n):
            mine = in_refs[k] if k < ng else in_refs[k].at[me]
            cp = pltpu.make_async_copy(mine, out_refs[k].at[me], loc_sems.at[k])
            cp.start()
            local.append(cp)
            for r in range(1, N_DEV):
                pid, pidx = peer(r)
                src = in_refs[k] if k < ng else in_refs[k].at[pidx]
                snd = pltpu.make_async_remote_copy(
                    src_ref=src, dst_ref=out_refs[k].at[me], send_sem=send_sems.at[r - 1, k],
                    recv_sem=recv_sems.at[r - 1, k], device_id=pid, device_id_type=MESH)
                snd.start()
                sends.append(snd)
                recvs.append(pltpu.make_async_remote_copy(
                    src_ref=src, dst_ref=out_refs[k].at[pidx], send_sem=send_sems.at[r - 1, k],
                    recv_sem=recv_sems.at[r - 1, k], device_id=pid, device_id_type=MESH))
        for rc in recvs:
            rc.wait_recv()
        for snd in sends:
            snd.wait_send()
        for cp in local:
            cp.wait()

    out_shape = [jax.ShapeDtypeStruct((N_DEV,) + a.shape, a.dtype) for a in gathers]
    out_shape += [jax.ShapeDtypeStruct(a.shape, a.dtype) for a in scatters]
    any_spec = pl.BlockSpec(memory_space=pl.ANY)
    return _pallas(
        body, name=name, out_shape=out_shape,
        in_specs=[any_spec] * n, out_specs=[any_spec] * n,
        scratch_shapes=[pltpu.SemaphoreType.DMA((N_DEV - 1, n)), pltpu.SemaphoreType.DMA((N_DEV - 1, n)),
                        pltpu.SemaphoreType.DMA((n,))],
    )(*ins)


def _ada_mod(c_all, ada_w, ada_b_slice):
    def body(c_ref, w_ref, b_ref, o_ref):
        ca = _silu(c_ref[...])
        for l in range(2):
            o_ref[l] = _nn(ca, w_ref[l], HI) + b_ref[l]

    return _pallas(body, name="ada_mod",
                   out_shape=jax.ShapeDtypeStruct((2, c_all.shape[0], ada_w.shape[2]), F32),
                   compiler_params=_cp(vmem=VMEM_MID))(c_all, ada_w, ada_b_slice)


def _ada_w_grad(c_all, dmod_slice):
    def body(c_ref, d_ref, o_ref):
        ca = _silu(c_ref[...])
        for l in range(2):
            o_ref[l] = _tn(ca, d_ref[l], HI)

    return _pallas(body, name="ada_w_grad",
                   out_shape=jax.ShapeDtypeStruct((2, D_MODEL, dmod_slice.shape[2]), F32),
                   compiler_params=_cp(vmem=VMEM_MID))(c_all, dmod_slice)


def _bucket_onehot():
    qi = jnp.arange(BLOCK)[:, None]
    kj = jnp.arange(2 * BLOCK)[None, :]
    rel = qi - kj + BLOCK
    n = jnp.maximum(rel, 0)
    nf = jnp.maximum(n, 1).astype(F32)
    large = REL_MAX_EXACT + (jnp.log(nf / REL_MAX_EXACT) / math.log(REL_MAX_DIST / REL_MAX_EXACT)
                             * (REL_BUCKETS - REL_MAX_EXACT)).astype(jnp.int32)
    large = jnp.minimum(large, REL_BUCKETS - 1)
    bucket = jnp.where(n < REL_MAX_EXACT, n, large).reshape(1, BLOCK * 2 * BLOCK)
    return (jnp.arange(REL_BUCKETS)[:, None] == bucket).astype(F32)


def _bias_expand(rel_bias_t, onehot):
    def body(r_ref, e_ref, o_ref):
        o_ref[...] = _nn(r_ref[...], e_ref[...], HI)

    return _pallas(body, name="bias_expand",
                   out_shape=jax.ShapeDtypeStruct((N_HEADS, onehot.shape[1]), F32),
                   compiler_params=_cp(vmem=VMEM_MID))(rel_bias_t, onehot)


def _bias_reduce(dbias, onehot):
    def body(d_ref, e_ref, o_ref):
        o_ref[...] = _nt(d_ref[...], e_ref[...], HI)

    return _pallas(body, name="bias_reduce",
                   out_shape=jax.ShapeDtypeStruct((N_HEADS, REL_BUCKETS), F32),
                   compiler_params=_cp(vmem=VMEM_MID))(dbias, onehot)


def _norm_proj(name, x, g, shift, scale, w, seq, out_dtype, wf_t=None):
    t_tok = x.shape[0]
    w3d = w.ndim == 3
    n_out = w.shape[0] * w.shape[2] if w3d else w.shape[1]
    cn = w.shape[2] if w3d else 256

    def body(x_ref, g_ref, sh_ref, sc_ref, w_ref, *rest):
        if wf_t is not None:
            wf_ref, h_ref, o_ref, fl_ref = rest
        else:
            h_ref, o_ref = rest
        xv = x_ref[...]
        rstd = lax.rsqrt(jnp.mean(xv * xv, axis=-1, keepdims=True) + EPS)
        h = (xv * rstd) * g_ref[...] * (1.0 + sc_ref[...]) + sh_ref[...]
        hb = h.astype(BF16)
        h_ref[...] = hb
        for j in range(n_out // cn):
            wj = w_ref[j] if w3d else w_ref[:, j * cn:(j + 1) * cn]
            o_ref[:, j * cn:(j + 1) * cn] = _nn(hb, wj).astype(out_dtype)
        if wf_t is not None:
            fl_ref[...] = _nt(wf_ref[...], hb)

    mod_spec = pl.BlockSpec((None, 1, D_MODEL), lambda i: (i * TM // seq, 0, 0))
    w_spec = (pl.BlockSpec(w.shape, lambda i: (0, 0, 0)) if w3d else pl.BlockSpec(w.shape, lambda i: (0, 0)))
    in_specs = [pl.BlockSpec((TM, D_MODEL), lambda i: (i, 0)), pl.BlockSpec((1, D_MODEL), lambda i: (0, 0)),
                mod_spec, mod_spec, w_spec]
    out_shape = [jax.ShapeDtypeStruct((t_tok, D_MODEL), BF16), jax.ShapeDtypeStruct((t_tok, n_out), out_dtype)]
    out_specs = [pl.BlockSpec((TM, D_MODEL), lambda i: (i, 0)), pl.BlockSpec((TM, n_out), lambda i: (i, 0))]
    args = [x, g, shift, scale, w]
    if wf_t is not None:
        in_specs.append(pl.BlockSpec(wf_t.shape, lambda i: (0, 0)))
        out_shape.append(jax.ShapeDtypeStruct((wf_t.shape[0], t_tok), F32))
        out_specs.append(pl.BlockSpec((wf_t.shape[0], TM), lambda i: (0, i)))
        args.append(wf_t)
    return _pallas(body, name=name, grid=(t_tok // TM,), in_specs=in_specs, out_specs=out_specs,
                   out_shape=out_shape, compiler_params=_cp(("arbitrary",), VMEM_BIG))(*args)


def _fox_prep(fl_t, b_f, seq):
    t_tok = fl_t.shape[1]
    ch = 256

    def body(fl_ref, bf_ref, fr_ref, fc_ref):
        z = fl_ref[...] + bf_ref[...]
        logf = jnp.minimum(z, 0.0) - jnp.log(1.0 + jnp.exp(-jnp.abs(z)))
        ri = lax.broadcasted_iota(jnp.int32, (ch, ch), 0)
        ci = lax.broadcasted_iota(jnp.int32, (ch, ch), 1)
        upper = (ri <= ci).astype(F32)
        eye = (ri == ci).astype(F32)
        carry = jnp.zeros((N_HEADS, 1), F32)
        for k in range(seq // ch):
            fk = _nn(logf[:, k * ch:(k + 1) * ch], upper, HI) + carry
            carry = fk[:, ch - 1:ch]
            fr_ref[:, k * ch:(k + 1) * ch] = fk
            padded = jnp.concatenate([fk, jnp.zeros((128 - N_HEADS, ch), F32)], axis=0)
            fc_ref[k * ch:(k + 1) * ch, :] = _nt(eye, padded, HI)

    return _pallas(
        body, name="fox_prep", grid=(t_tok // seq,),
        in_specs=[pl.BlockSpec((N_HEADS, seq), lambda b: (0, b)), pl.BlockSpec((N_HEADS, 1), lambda b: (0, 0))],
        out_specs=[pl.BlockSpec((N_HEADS, seq), lambda b: (0, b)), pl.BlockSpec((seq, 128), lambda b: (b, 0))],
        out_shape=[jax.ShapeDtypeStruct((N_HEADS, t_tok), F32), jax.ShapeDtypeStruct((t_tok, 128), F32)],
        compiler_params=_cp(("arbitrary",), VMEM_MID))(fl_t, b_f)


def _fox_post(df_row, fl_t, b_f, seq):
    t_tok = fl_t.shape[1]
    ch = 256

    def body(d_ref, fl_ref, bf_ref, o_ref, db_ref):
        @pl.when(pl.program_id(0) == 0)
        def _():
            db_ref[...] = jnp.zeros_like(db_ref)

        z = fl_ref[...] + bf_ref[...]
        sig_neg = 1.0 / (1.0 + jnp.exp(z))
        ri = lax.broadcasted_iota(jnp.int32, (ch, ch), 0)
        ci = lax.broadcasted_iota(jnp.int32, (ch, ch), 1)
        lower = (ri >= ci).astype(F32)
        carry = jnp.zeros((N_HEADS, 1), F32)
        tot = jnp.zeros((N_HEADS, 1), F32)
        for k in reversed(range(seq // ch)):
            dk = _nn(d_ref[:, k * ch:(k + 1) * ch], lower, HI) + carry
            carry = dk[:, 0:1]
            dfl = dk * sig_neg[:, k * ch:(k + 1) * ch]
            o_ref[:, k * ch:(k + 1) * ch] = dfl
            tot = tot + jnp.sum(dfl, axis=1, keepdims=True)
        db_ref[...] += jnp.broadcast_to(tot, db_ref.shape)

    return _pallas(
        body, name="fox_post", grid=(t_tok // seq,),
        in_specs=[pl.BlockSpec((N_HEADS, seq), lambda b: (0, b)), pl.BlockSpec((N_HEADS, seq), lambda b: (0, b)),
                  pl.BlockSpec((N_HEADS, 1), lambda b: (0, 0))],
        out_specs=[pl.BlockSpec((N_HEADS, seq), lambda b: (0, b)), pl.BlockSpec((N_HEADS, 128), lambda b: (0, 0))],
        out_shape=[jax.ShapeDtypeStruct((N_HEADS, t_tok), F32), jax.ShapeDtypeStruct((N_HEADS, 128), F32)],
        compiler_params=_cp(("arbitrary",), VMEM_MID))(df_row, fl_t, b_f)


def _fox_fwd(qkvg, f_row, f_col, seq):
    t_tok = qkvg.shape[0]
    nq = seq // TQ

    def body(q_ref, k_ref, v_ref, fr_ref, fc_ref, o_ref, lse_ref, fk_s):
        i = pl.program_id(1)
        for jj in range(nq):
            fk_s[jj] = fr_ref[:, jj * TQ:(jj + 1) * TQ]
        fcol = fc_ref[...]
        tpos = i * TQ + lax.broadcasted_iota(jnp.int32, (TQ, 1), 0)
        lane = lax.broadcasted_iota(jnp.int32, (TQ, 128), 1)
        lse_tile = jnp.zeros((TQ, 128), F32)
        for p in range(N_HEADS // 2):
            q2 = q_ref[:, 128 * p:128 * (p + 1)]
            qs = [q2[:, :HEAD_DIM], q2[:, HEAD_DIM:]]
            fqs = [_col(fcol, 2 * p + e) for e in range(2)]

            def kblock(j, carry):
                row0 = pl.multiple_of(j * TQ, TQ)
                k2 = k_ref[pl.ds(row0, TQ), 128 * p:128 * (p + 1)]
                v2 = v_ref[pl.ds(row0, TQ), 128 * p:128 * (p + 1)]
                fk8 = fk_s[j]
                spos = j * TQ + lax.broadcasted_iota(jnp.int32, (1, TQ), 1)
                keep = spos <= tpos
                new = []
                for e in range(2):
                    m, l, acc = carry[3 * e:3 * e + 3]
                    kh = k2[:, e * HEAD_DIM:(e + 1) * HEAD_DIM]
                    vh = v2[:, e * HEAD_DIM:(e + 1) * HEAD_DIM]
                    s = _nt(qs[e], kh) * SCALE + (fqs[e] - fk8[2 * p + e:2 * p + e + 1, :])
                    s = jnp.where(keep, s, NEG)
                    m_new = jnp.maximum(m, jnp.max(s, axis=1, keepdims=True))
                    alpha = jnp.exp(m - m_new)
                    pe = jnp.exp(s - m_new)
                    l = alpha * l + jnp.sum(pe, axis=1, keepdims=True)
                    acc = alpha * acc + _nn(pe.astype(BF16), vh)
                    new += [m_new, l, acc]
                return tuple(new)

            init = (jnp.full((TQ, 1), NEG, F32), jnp.zeros((TQ, 1), F32), jnp.zeros((TQ, HEAD_DIM), F32)) * 2
            res = lax.fori_loop(0, i + 1, kblock, init)
            outs = []
            for e in range(2):
                m, l, acc = res[3 * e:3 * e + 3]
                outs.append(acc / l)
                lse_tile = jnp.where(lane == 2 * p + e, m + jnp.log(l), lse_tile)
            o_ref[:, 128 * p:128 * (p + 1)] = jnp.concatenate(outs, axis=1).astype(BF16)
        lse_ref[...] = lse_tile

    return _pallas(
        body, name="fox_fwd", grid=(t_tok // seq, nq),
        in_specs=[pl.BlockSpec((TQ, 512), lambda b, i: (b * nq + i, C_BQ // 512)),
                  pl.BlockSpec((seq, 512), lambda b, i: (b, C_BK // 512)),
                  pl.BlockSpec((seq, 512), lambda b, i: (b, C_BV // 512)),
                  pl.BlockSpec((N_HEADS, seq), lambda b, i: (0, b)),
                  pl.BlockSpec((TQ, 128), lambda b, i: (b * nq + i, 0))],
        out_specs=[pl.BlockSpec((TQ, 512), lambda b, i: (b * nq + i, 0)),
                   pl.BlockSpec((TQ, 128), lambda b, i: (b * nq + i, 0))],
        out_shape=[jax.ShapeDtypeStruct((t_tok, 512), BF16), jax.ShapeDtypeStruct((t_tok, 128), F32)],
        scratch_shapes=[pltpu.VMEM((nq, N_HEADS, TQ), F32)],
        compiler_params=_cp(("arbitrary", "arbitrary"), VMEM_MID))(qkvg, qkvg, qkvg, f_row, f_col)


def _fox_bwd(qkvg, du_b, b_out, lse, f_row, f_col, seq):
    t_tok = qkvg.shape[0]
    nq = seq // TQ

    def body(q_ref, k_ref, v_ref, do_ref, o_ref, lse_ref, fr_ref, fc_ref,
             dq_ref, dk_ref, dv_ref, df_ref, dq_s, dk_s, dv_s, col_s, df_s, fk_s):
        p = pl.program_id(1)
        for jj in range(nq):
            fk_s[jj] = fr_ref[:, jj * TQ:(jj + 1) * TQ]
        eye = (lax.broadcasted_iota(jnp.int32, (TQ, TQ), 0) == lax.broadcasted_iota(jnp.int32, (TQ, TQ), 1)).astype(F32)
        for e in range(2):
            h = 2 * p + e
            lo, hi = e * HEAD_DIM, (e + 1) * HEAD_DIM
            for ii in range(nq):
                rows = slice(ii * TQ, (ii + 1) * TQ)
                do = do_ref[rows, :][:, lo:hi].astype(F32)
                ov = o_ref[rows, :][:, lo:hi].astype(F32)
                col_s[0, rows, :] = jnp.sum(do * ov, axis=1, keepdims=True)
                col_s[1, rows, :] = _col(lse_ref[rows, :], h)
                col_s[2, rows, :] = _col(fc_ref[rows, :], h)
                dq_s[rows, :] = jnp.zeros((TQ, HEAD_DIM), F32)
                df_s[ii] = jnp.zeros((8, TQ), F32)
                col_s[3, rows, :] = jnp.zeros((TQ, 1), F32)

            def kblock(j, _):
                krow = pl.multiple_of(j * TQ, TQ)
                kh = k_ref[pl.ds(krow, TQ), :][:, lo:hi]
                vh = v_ref[pl.ds(krow, TQ), :][:, lo:hi]
                fk = _row(fk_s[j], h)
                spos = j * TQ + lax.broadcasted_iota(jnp.int32, (1, TQ), 1)

                def qblock(i, carry):
                    dk_acc, dv_acc, dfk = carry
                    qrow = pl.multiple_of(i * TQ, TQ)
                    qh = q_ref[pl.ds(qrow, TQ), :][:, lo:hi]
                    doh = do_ref[pl.ds(qrow, TQ), :][:, lo:hi]
                    delta = col_s[0, pl.ds(qrow, TQ), :]
                    lse_q = col_s[1, pl.ds(qrow, TQ), :]
                    fq = col_s[2, pl.ds(qrow, TQ), :]
                    tpos = i * TQ + lax.broadcasted_iota(jnp.int32, (TQ, 1), 0)
                    s = _nt(qh, kh) * SCALE + (fq - fk)
                    pr = jnp.where(spos <= tpos, jnp.exp(s - lse_q), 0.0)
                    dp = _nt(doh, vh)
                    ds = pr * (dp - delta)
                    ds_b = ds.astype(BF16)
                    dv_acc = dv_acc + _tn(pr.astype(BF16), doh)
                    dk_acc = dk_acc + _tn(ds_b, qh)
                    dq_s[pl.ds(qrow, TQ), :] += _nn(ds_b, kh)
                    col_s[3, pl.ds(qrow, TQ), :] += jnp.sum(ds, axis=1, keepdims=True)
                    dfk = dfk + jnp.sum(ds, axis=0, keepdims=True)
                    return dk_acc, dv_acc, dfk

                zero = jnp.zeros((TQ, HEAD_DIM), F32)
                dk_acc, dv_acc, dfk = lax.fori_loop(j, nq, qblock, (zero, zero, jnp.zeros((1, TQ), F32)))
                dk_s[e, pl.ds(krow, TQ), :] = dk_acc * SCALE
                dv_s[e, pl.ds(krow, TQ), :] = dv_acc
                df_s[j] -= jnp.broadcast_to(dfk, (8, TQ))
                return 0

            lax.fori_loop(0, nq, kblock, 0)
            dq_s2 = dq_s[...] * SCALE
            dk_s[2 + e] = dq_s2
            for ii in range(nq):
                dfq = jnp.broadcast_to(col_s[3, ii * TQ:(ii + 1) * TQ, :], (TQ, 128))
                df_ref[e:e + 1, ii * TQ:(ii + 1) * TQ] = _tn(dfq, eye, HI)[0:1, :] + df_s[ii][0:1, :]
        dq_ref[...] = jnp.concatenate([dk_s[2], dk_s[3]], axis=1).astype(BF16)
        dk_ref[...] = jnp.concatenate([dk_s[0], dk_s[1]], axis=1).astype(BF16)
        dv_ref[...] = jnp.concatenate([dv_s[0], dv_s[1]], axis=1).astype(BF16)

    blk = lambda off: pl.BlockSpec((seq, 128), lambda b, p: (b, off // 128 + p))
    out_blk = pl.BlockSpec((seq, 128), lambda b, p: (b, p))
    return _pallas(
        body, name="fox_bwd", grid=(t_tok // seq, N_HEADS // 2),
        in_specs=[blk(C_BQ), blk(C_BK), blk(C_BV), out_blk, out_blk,
                  pl.BlockSpec((seq, 128), lambda b, p: (b, 0)),
                  pl.BlockSpec((N_HEADS, seq), lambda b, p: (0, b)),
                  pl.BlockSpec((seq, 128), lambda b, p: (b, 0))],
        out_specs=[out_blk, out_blk, out_blk, pl.BlockSpec((None, 2, seq), lambda b, p: (p, 0, b))],
        out_shape=[jax.ShapeDtypeStruct((t_tok, 512), BF16)] * 3
        + [jax.ShapeDtypeStruct((N_HEADS // 2, 2, t_tok), F32)],
        scratch_shapes=[pltpu.VMEM((seq, HEAD_DIM), F32), pltpu.VMEM((4, seq, HEAD_DIM), F32),
                        pltpu.VMEM((2, seq, HEAD_DIM), F32), pltpu.VMEM((4, seq, 1), F32),
                        pltpu.VMEM((nq, 8, TQ), F32), pltpu.VMEM((nq, N_HEADS, TQ), F32)],
        compiler_params=_cp(("arbitrary", "arbitrary"), VMEM_BIG))(qkvg, qkvg, qkvg, du_b, b_out, lse, f_row, f_col)


def _swa_scores(q_ref, kp, kc, bias_ref, h, mask_p, mask_c):
    q2 = q_ref[:, 128 * (h // 2):128 * (h // 2 + 1)]
    qh = q2[:, (h % 2) * HEAD_DIM:(h % 2 + 1) * HEAD_DIM]
    hk = h // KV_GROUP
    kph = kp[:, hk * HEAD_DIM:(hk + 1) * HEAD_DIM]
    kch = kc[:, hk * HEAD_DIM:(hk + 1) * HEAD_DIM]
    bias = bias_ref[h]
    sp = jnp.where(mask_p, _nt(qh, kph) * SCALE + bias[:, :BLOCK], NEG)
    sc = jnp.where(mask_c, _nt(qh, kch) * SCALE + bias[:, BLOCK:], NEG)
    return qh, kph, kch, sp, sc


def _swa_masks(n):
    ti = lax.broadcasted_iota(jnp.int32, (BLOCK, BLOCK), 0)
    sj = lax.broadcasted_iota(jnp.int32, (BLOCK, BLOCK), 1)
    return jnp.logical_and(sj > ti, n > 0), sj <= ti


def _swa_fwd(qkvg, bias, sinks, seq):
    t_tok = qkvg.shape[0]
    nb = seq // BLOCK

    def body(sink_ref, q_ref, k_ref, v_ref, bias_ref, o_ref, lse_ref):
        n = pl.program_id(1)
        prev = pl.multiple_of(jnp.maximum(n - 1, 0) * BLOCK, BLOCK)
        cur = pl.multiple_of(n * BLOCK, BLOCK)
        kp, kc = k_ref[pl.ds(prev, BLOCK), :], k_ref[pl.ds(cur, BLOCK), :]
        vp, vc = v_ref[pl.ds(prev, BLOCK), :], v_ref[pl.ds(cur, BLOCK), :]
        mask_p, mask_c = _swa_masks(n)
        lane = lax.broadcasted_iota(jnp.int32, (BLOCK, 128), 1)
        lse_tile = jnp.zeros((BLOCK, 128), F32)
        outs = []
        for h in range(N_HEADS):
            hk = h // KV_GROUP
            _, _, _, sp, sc = _swa_scores(q_ref, kp, kc, bias_ref, h, mask_p, mask_c)
            sink = sink_ref[h]
            m = jnp.maximum(jnp.maximum(jnp.max(sp, axis=1, keepdims=True), jnp.max(sc, axis=1, keepdims=True)), sink)
            pp, pc = jnp.exp(sp - m), jnp.exp(sc - m)
            den = jnp.sum(pp, axis=1, keepdims=True) + jnp.sum(pc, axis=1, keepdims=True) + jnp.exp(sink - m)
            acc = (_nn(pp.astype(BF16), vp[:, hk * HEAD_DIM:(hk + 1) * HEAD_DIM])
                   + _nn(pc.astype(BF16), vc[:, hk * HEAD_DIM:(hk + 1) * HEAD_DIM]))
            outs.append(acc / den)
            lse_tile = jnp.where(lane == h, m + jnp.log(den), lse_tile)
            if h % 2 == 1:
                o_ref[:, 128 * (h // 2):128 * (h // 2 + 1)] = jnp.concatenate(outs, axis=1).astype(BF16)
                outs = []
        lse_ref[...] = lse_tile

    return _pallas(
        body, name="swa_fwd", grid=(t_tok // seq, nb),
        in_specs=[pl.BlockSpec(memory_space=pltpu.SMEM),
                  pl.BlockSpec((BLOCK, 512), lambda b, n: (b * nb + n, C_AQ // 512)),
                  pl.BlockSpec((seq, 128), lambda b, n: (b, C_AK // 128)),
                  pl.BlockSpec((seq, 128), lambda b, n: (b, C_AV // 128)),
                  pl.BlockSpec((N_HEADS, BLOCK, 2 * BLOCK), lambda b, n: (0, 0, 0))],
        out_specs=[pl.BlockSpec((BLOCK, 512), lambda b, n: (b * nb + n, 0)),
                   pl.BlockSpec((BLOCK, 128), lambda b, n: (b * nb + n, 0))],
        out_shape=[jax.ShapeDtypeStruct((t_tok, 512), BF16), jax.ShapeDtypeStruct((t_tok, 128), F32)],
        compiler_params=_cp(("arbitrary", "arbitrary"), VMEM_MID))(sinks, qkvg, qkvg, qkvg, bias)


def _swa_bwd(qkvg, du_a, a_out, lse, bias, sinks, seq):
    t_tok = qkvg.shape[0]
    nb = seq // BLOCK

    def body(sink_ref, q_ref, k_ref, v_ref, do_ref, o_ref, lse_ref, bias_ref,
             dq_ref, dkv_ref, dbias_ref, dsink_ref, kv_s):
        b, n = pl.program_id(0), pl.program_id(1)

        @pl.when(jnp.logical_and(b == 0, n == 0))
        def _():
            dbias_ref[...] = jnp.zeros_like(dbias_ref)
            dsink_ref[...] = jnp.zeros_like(dsink_ref)

        @pl.when(n == 0)
        def _():
            kv_s[...] = jnp.zeros_like(kv_s)

        prev = pl.multiple_of(jnp.maximum(n - 1, 0) * BLOCK, BLOCK)
        cur = pl.multiple_of(n * BLOCK, BLOCK)
        kp, kc = k_ref[pl.ds(prev, BLOCK), :], k_ref[pl.ds(cur, BLOCK), :]
        vp, vc = v_ref[pl.ds(prev, BLOCK), :], v_ref[pl.ds(cur, BLOCK), :]
        mask_p, mask_c = _swa_masks(n)
        lse_tile = lse_ref[...]
        dqs = []
        acc = [[jnp.zeros((BLOCK, HEAD_DIM), F32) for _ in range(4)] for _ in range(2)]
        for h in range(N_HEADS):
            hk = h // KV_GROUP
            qh, kph, kch, sp, sc = _swa_scores(q_ref, kp, kc, bias_ref, h, mask_p, mask_c)
            lo, hi = (h % 2) * HEAD_DIM, (h % 2 + 1) * HEAD_DIM
            do2 = do_ref[:, 128 * (h // 2):128 * (h // 2 + 1)]
            o2 = o_ref[:, 128 * (h // 2):128 * (h // 2 + 1)]
            doh = do2[:, lo:hi]
            delta = jnp.sum(doh.astype(F32) * o2[:, lo:hi].astype(F32), axis=1, keepdims=True)
            lse_h = _col(lse_tile, h)
            pp, pc = jnp.exp(sp - lse_h), jnp.exp(sc - lse_h)
            vph = vp[:, hk * HEAD_DIM:(hk + 1) * HEAD_DIM]
            vch = vc[:, hk * HEAD_DIM:(hk + 1) * HEAD_DIM]
            dsp = pp * (_nt(doh, vph) - delta)
            dsc = pc * (_nt(doh, vch) - delta)
            dbias_ref[h, :, :BLOCK] += dsp
            dbias_ref[h, :, BLOCK:] += dsc
            psink = jnp.exp(sink_ref[h] - lse_h)
            dsink_ref[h:h + 1, :] += jnp.broadcast_to(jnp.sum(-psink * delta, axis=0, keepdims=True), (1, 128))
            dsp_b, dsc_b = dsp.astype(BF16), dsc.astype(BF16)
            dqs.append((_nn(dsp_b, kph) + _nn(dsc_b, kch)) * SCALE)
            acc[hk][0] = acc[hk][0] + _tn(dsp_b, qh)
            acc[hk][1] = acc[hk][1] + _tn(dsc_b, qh)
            acc[hk][2] = acc[hk][2] + _tn(pp.astype(BF16), doh)
            acc[hk][3] = acc[hk][3] + _tn(pc.astype(BF16), doh)
            if h % 2 == 1:
                dq_ref[:, 128 * (h // 2):128 * (h // 2 + 1)] = jnp.concatenate(dqs, axis=1).astype(BF16)
                dqs = []
        upd_p = jnp.concatenate([acc[0][0] * SCALE, acc[1][0] * SCALE, acc[0][2], acc[1][2]], axis=1)
        upd_c = jnp.concatenate([acc[0][1] * SCALE, acc[1][1] * SCALE, acc[0][3], acc[1][3]], axis=1)
        kv_s[pl.ds(prev, BLOCK), :] += upd_p
        kv_s[pl.ds(cur, BLOCK), :] += upd_c

        @pl.when(n == nb - 1)
        def _():
            dkv_ref[...] = kv_s[...].astype(BF16)

    return _pallas(
        body, name="swa_bwd", grid=(t_tok // seq, nb),
        in_specs=[pl.BlockSpec(memory_space=pltpu.SMEM),
                  pl.BlockSpec((BLOCK, 512), lambda b, n: (b * nb + n, C_AQ // 512)),
                  pl.BlockSpec((seq, 128), lambda b, n: (b, C_AK // 128)),
                  pl.BlockSpec((seq, 128), lambda b, n: (b, C_AV // 128)),
                  pl.BlockSpec((BLOCK, 512), lambda b, n: (b * nb + n, 0)),
                  pl.BlockSpec((BLOCK, 512), lambda b, n: (b * nb + n, 0)),
                  pl.BlockSpec((BLOCK, 128), lambda b, n: (b * nb + n, 0)),
                  pl.BlockSpec((N_HEADS, BLOCK, 2 * BLOCK), lambda b, n: (0, 0, 0))],
        out_specs=[pl.BlockSpec((BLOCK, 512), lambda b, n: (b * nb + n, 0)),
                   pl.BlockSpec((seq, 256), lambda b, n: (b, 0)),
                   pl.BlockSpec((N_HEADS, BLOCK, 2 * BLOCK), lambda b, n: (0, 0, 0)),
                   pl.BlockSpec((N_HEADS, 128), lambda b, n: (0, 0))],
        out_shape=[jax.ShapeDtypeStruct((t_tok, 512), BF16), jax.ShapeDtypeStruct((t_tok, 256), BF16),
                   jax.ShapeDtypeStruct((N_HEADS, BLOCK, 2 * BLOCK), F32), jax.ShapeDtypeStruct((N_HEADS, 128), F32)],
        scratch_shapes=[pltpu.VMEM((seq, 256), F32)],
        compiler_params=_cp(("arbitrary", "arbitrary"), VMEM_MID))(sinks, qkvg, qkvg, qkvg, du_a, a_out, lse, bias)


def _out_proj(name, u_parts, gate_arr, gate_blk, w_out, x, gmod, seq):
    t_tok = x.shape[0]
    nu = len(u_parts)

    def body(*refs):
        u_refs = refs[:nu]
        g_ref, w_ref, x_ref, gm_ref, yg_ref, y_ref, xn_ref = refs[nu:]
        u = jnp.concatenate([r[...].astype(F32) for r in u_refs], axis=1) if nu > 1 else u_refs[0][...].astype(F32)
        yg = (u * _silu(g_ref[...].astype(F32))).astype(BF16)
        yg_ref[...] = yg
        y = _nn(yg, w_ref[...])
        y_ref[...] = y.astype(BF16)
        xn_ref[...] = x_ref[...] + gm_ref[...] * y

    row = lambda w: pl.BlockSpec((TM, w), lambda i: (i, 0))
    in_specs = [row(u.shape[1]) for u in u_parts]
    in_specs += [pl.BlockSpec((TM, D_MODEL), lambda i: (i, gate_blk)),
                 pl.BlockSpec((D_MODEL, D_MODEL), lambda i: (0, 0)), row(D_MODEL),
                 pl.BlockSpec((None, 1, D_MODEL), lambda i: (i * TM // seq, 0, 0))]
    return _pallas(
        body, name=name, grid=(t_tok // TM,), in_specs=in_specs,
        out_specs=[row(D_MODEL)] * 3,
        out_shape=[jax.ShapeDtypeStruct((t_tok, D_MODEL), BF16)] * 2 + [jax.ShapeDtypeStruct((t_tok, D_MODEL), F32)],
        compiler_params=_cp(("arbitrary",), VMEM_MID))(*u_parts, gate_arr, w_out, x, gmod)


def _out_proj_bwd(name, dxn, gmod, y, w_out, seq, attn=None):
    t_tok = dxn.shape[0]
    tiles_per_seq = seq // TM

    def body(*refs):
        if attn is None:
            dxn_ref, gm_ref, y_ref, w_ref, dy_ref, dgm_ref, dyg_ref = refs
        else:
            dxn_ref, gm_ref, y_ref, w_ref, a_ref, b_ref, g_ref, dy_ref, dgm_ref, dua_ref, dub_ref, dg_ref = refs
        i = pl.program_id(0)
        dxv = dxn_ref[...]
        dy = (dxv * gm_ref[...]).astype(BF16)
        dy_ref[...] = dy

        @pl.when(i % tiles_per_seq == 0)
        def _():
            dgm_ref[...] = jnp.zeros_like(dgm_ref)

        dgm_ref[...] += jnp.sum(dxv * y_ref[...].astype(F32), axis=0, keepdims=True)
        dyg = _nt(dy, w_ref[...])
        if attn is None:
            dyg_ref[...] = dyg
        else:
            gt = g_ref[...].astype(F32)
            du = dyg * _silu(gt)
            dua_ref[...] = du[:, :512].astype(BF16)
            dub_ref[...] = du[:, 512:].astype(BF16)
            u = jnp.concatenate([a_ref[...].astype(F32), b_ref[...].astype(F32)], axis=1)
            dg_ref[...] = (dyg * u * _dsilu(gt)).astype(BF16)

    row = lambda w: pl.BlockSpec((TM, w), lambda i: (i, 0))
    mod_spec = pl.BlockSpec((None, 1, D_MODEL), lambda i: (i * TM // seq, 0, 0))
    in_specs = [row(D_MODEL), mod_spec, row(D_MODEL), pl.BlockSpec((D_MODEL, D_MODEL), lambda i: (0, 0))]
    out_specs = [row(D_MODEL), mod_spec]
    out_shape = [jax.ShapeDtypeStruct((t_tok, D_MODEL), BF16), jax.ShapeDtypeStruct(gmod.shape, F32)]
    args = [dxn, gmod, y, w_out]
    if attn is None:
        out_specs.append(row(D_MODEL))
        out_shape.append(jax.ShapeDtypeStruct((t_tok, D_MODEL), F32))
    else:
        in_specs += [row(512), row(512), pl.BlockSpec((TM, D_MODEL), lambda i: (i, C_GATE // D_MODEL))]
        out_specs += [row(512), row(512), row(D_MODEL)]
        out_shape += [jax.ShapeDtypeStruct((t_tok, 512), BF16)] * 2 + [jax.ShapeDtypeStruct((t_tok, D_MODEL), BF16)]
        args += list(attn)
    return _pallas(body, name=name, grid=(t_tok // TM,), in_specs=in_specs, out_specs=out_specs,
                   out_shape=out_shape, compiler_params=_cp(("arbitrary",), VMEM_MID))(*args)


def _norm_bwd(name, parts, w, x, g, scale, dxn, seq, rows_part=None):
    t_tok = x.shape[0]
    npart = len(parts)
    w3d = w.ndim == 3
    tiles_per_seq = seq // TM
    nrow_in = 0 if rows_part is None else 2

    def body(*refs):
        p_refs = refs[:npart]
        w_ref, x_ref, g_ref, sc_ref, dxn_ref = refs[npart:npart + 5]
        dx_ref, dss_ref, dg_ref = refs[npart + 5 + nrow_in:]
        i = pl.program_id(0)
        dh = jnp.zeros((TM, D_MODEL), F32)
        if rows_part is not None:
            r_ref, wr_ref = refs[npart + 5:npart + 7]
            dh = dh + _tn(r_ref[...].astype(BF16), wr_ref[...])
        for (arr, off), p_ref in zip(parts, p_refs):
            width = arr.shape[1]
            for j in range(width // 256):
                pj = p_ref[:, j * 256:(j + 1) * 256]
                c0 = off + j * 256
                wj = w_ref[c0 // 256] if w3d else w_ref[:, c0:c0 + 256]
                dh = dh + _nt(pj, wj)
        xv = x_ref[...]
        rstd = lax.rsqrt(jnp.mean(xv * xv, axis=-1, keepdims=True) + EPS)
        xhat = xv * rstd
        gv = g_ref[...]
        nrm = xhat * gv

        @pl.when(i % tiles_per_seq == 0)
        def _():
            dss_ref[...] = jnp.zeros_like(dss_ref)

        @pl.when(i == 0)
        def _():
            dg_ref[...] = jnp.zeros_like(dg_ref)

        dss_ref[0:1, :] += jnp.sum(dh, axis=0, keepdims=True)
        dss_ref[1:2, :] += jnp.sum(dh * nrm, axis=0, keepdims=True)
        dn = dh * (1.0 + sc_ref[...])
        dg_ref[0:1, :] += jnp.sum(dn * xhat, axis=0, keepdims=True)
        dxhat = dn * gv
        dx_ref[...] = rstd * (dxhat - xhat * jnp.mean(dxhat * xhat, axis=-1, keepdims=True)) + dxn_ref[...]

    row = lambda wd: pl.BlockSpec((TM, wd), lambda i: (i, 0))
    w_spec = (pl.BlockSpec(w.shape, lambda i: (0, 0, 0)) if w3d else pl.BlockSpec(w.shape, lambda i: (0, 0)))
    in_specs = [row(a.shape[1]) for a, _ in parts]
    in_specs += [w_spec, row(D_MODEL), pl.BlockSpec((1, D_MODEL), lambda i: (0, 0)),
                 pl.BlockSpec((None, 1, D_MODEL), lambda i: (i * TM // seq, 0, 0)), row(D_MODEL)]
    args = [a for a, _ in parts] + [w, x, g, scale, dxn]
    if rows_part is not None:
        in_specs += [pl.BlockSpec((8, TM), lambda i: (0, i)), pl.BlockSpec((8, D_MODEL), lambda i: (0, 0))]
        args += list(rows_part)
    nseq = t_tok // seq
    return _pallas(
        body, name=name, grid=(t_tok // TM,), in_specs=in_specs,
        out_specs=[row(D_MODEL), pl.BlockSpec((None, 8, D_MODEL), lambda i: (i * TM // seq, 0, 0)),
                   pl.BlockSpec((8, D_MODEL), lambda i: (0, 0))],
        out_shape=[jax.ShapeDtypeStruct((t_tok, D_MODEL), F32), jax.ShapeDtypeStruct((nseq, 8, D_MODEL), F32),
                   jax.ShapeDtypeStruct((8, D_MODEL), F32)],
        compiler_params=_cp(("arbitrary",), VMEM_BIG))(*args)


def _dw(name, a, parts, blocked=None):
    t_tok, ka = a.shape
    tt = 512
    npart = len(parts)
    nt = t_tok // tt

    def body(*refs):
        a_ref = refs[0]
        p_refs = refs[1:1 + npart]
        o_refs = refs[1 + npart:1 + 2 * npart]
        acc_refs = refs[1 + 2 * npart:]
        t = pl.program_id(0)
        av = a_ref[...]
        for p_ref, acc in zip(p_refs, acc_refs):
            upd = _tn(av, p_ref[...])

            @pl.when(t == 0)
            def _():
                acc[...] = upd

            @pl.when(t > 0)
            def _():
                acc[...] += upd

        @pl.when(t == nt - 1)
        def _():
            for o_ref, acc in zip(o_refs, acc_refs):
                if blocked is None:
                    o_ref[...] = acc[...].astype(BF16)
                else:
                    for j in range(o_ref.shape[0]):
                        o_ref[j] = acc[:, j * blocked:(j + 1) * blocked].astype(BF16)

    in_specs = [pl.BlockSpec((tt, ka), lambda t: (t, 0))]
    in_specs += [pl.BlockSpec((tt, p.shape[1]), lambda t: (t, 0)) for p in parts]
    if blocked is None:
        out_shape = [jax.ShapeDtypeStruct((ka, p.shape[1]), BF16) for p in parts]
        out_specs = [pl.BlockSpec((ka, p.shape[1]), lambda t: (0, 0)) for p in parts]
    else:
        out_shape = [jax.ShapeDtypeStruct((p.shape[1] // blocked, ka, blocked), BF16) for p in parts]
        out_specs = [pl.BlockSpec((p.shape[1] // blocked, ka, blocked), lambda t: (0, 0, 0)) for p in parts]
    return _pallas(body, name=name, grid=(nt,), in_specs=in_specs, out_specs=out_specs, out_shape=out_shape,
                   scratch_shapes=[pltpu.VMEM((ka, p.shape[1]), F32) for p in parts],
                   compiler_params=_cp(("arbitrary",), VMEM_BIG))(a, *parts)


def _dw_rows(name, rows_t, h):
    t_tok = h.shape[0]
    tt = 512

    def body(r_ref, h_ref, o_ref):
        @pl.when(pl.program_id(0) == 0)
        def _():
            o_ref[...] = jnp.zeros_like(o_ref)

        o_ref[...] += _nn(r_ref[...].astype(BF16), h_ref[...])

    return _pallas(body, name=name, grid=(t_tok // tt,),
                   in_specs=[pl.BlockSpec((8, tt), lambda t: (0, t)), pl.BlockSpec((tt, D_MODEL), lambda t: (t, 0))],
                   out_specs=pl.BlockSpec((8, D_MODEL), lambda t: (0, 0)),
                   out_shape=jax.ShapeDtypeStruct((8, D_MODEL), F32),
                   compiler_params=_cp(("arbitrary",), VMEM_MID))(rows_t, h)


def _lru_gates(xc, blk, wa_ref, wx_ref, ba_ref, bx_ref, sp):
    cols = slice(blk * LRU_BLOCK_W, (blk + 1) * LRU_BLOCK_W)
    xb = xc[:, cols].astype(BF16)
    r = _sigmoid(_nn(xb, wa_ref[blk].astype(BF16)) + ba_ref[:, cols])
    ig = _sigmoid(_nn(xb, wx_ref[blk].astype(BF16)) + bx_ref[:, cols])
    log_a = -LRU_C * r * sp[:, cols]
    a = jnp.exp(log_a)
    mult = jnp.sqrt(_neg_expm1(2.0 * log_a))
    return xb, r, ig, a, mult


def _softplus_neg(lam):
    return jnp.maximum(-lam, 0.0) + jnp.log(1.0 + jnp.exp(-jnp.abs(lam)))


def _conv_taps(xe_ref, cw_ref, cb_ref):
    xc = cb_ref[...] + xe_ref[8:8 + TC, :] * cw_ref[3:4, :]
    for k in range(1, 4):
        xc = xc + xe_ref[8 - k:8 - k + TC, :] * cw_ref[3 - k:4 - k, :]
    return xc


def _lru_fwd(proj, cw, cb, w_a, b_a, w_x, b_x, lam, seq):
    t_tok = proj.shape[0]
    nc = seq // TC

    def body(x_ref, cw_ref, cb_ref, wa_ref, ba_ref, wx_ref, bx_ref, lam_ref, hs_ref, xe_s, a_s, u_s, h_s):
        c = pl.program_id(1)

        @pl.when(c == 0)
        def _():
            xe_s[0:8, :] = jnp.zeros((8, D_MODEL), F32)
            h_s[...] = jnp.zeros_like(h_s)

        xe_s[8:8 + TC, :] = x_ref[...]
        xc = _conv_taps(xe_s, cw_ref, cb_ref)
        sp = _softplus_neg(lam_ref[...])
        for blk in range(LRU_BLOCKS):
            cols = slice(blk * LRU_BLOCK_W, (blk + 1) * LRU_BLOCK_W)
            _, _, ig, a, mult = _lru_gates(xc, blk, wa_ref, wx_ref, ba_ref, bx_ref, sp)
            a_s[:, cols] = a
            u_s[:, cols] = mult * ig * xc[:, cols]

        def step(t, h):
            h = a_s[pl.ds(t, 1), :] * h + u_s[pl.ds(t, 1), :]
            hs_ref[pl.ds(t, 1), :] = h
            return h

        h_s[0:1, :] = lax.fori_loop(0, TC, step, h_s[0:1, :], unroll=8)
        xe_s[0:8, :] = xe_s[TC:TC + 8, :]

    full = lambda shape: pl.BlockSpec(shape, lambda b, c: (0,) * len(shape))
    return _pallas(
        body, name="lru_fwd", grid=(t_tok // seq, nc),
        in_specs=[pl.BlockSpec((TC, D_MODEL), lambda b, c: (b * nc + c, 0)), full((4, D_MODEL)), full((1, D_MODEL)),
                  full((LRU_BLOCKS, LRU_BLOCK_W, LRU_BLOCK_W)), full((1, D_MODEL)),
                  full((LRU_BLOCKS, LRU_BLOCK_W, LRU_BLOCK_W)), full((1, D_MODEL)), full((1, D_MODEL))],
        out_specs=pl.BlockSpec((TC, D_MODEL), lambda b, c: (b * nc + c, 0)),
        out_shape=jax.ShapeDtypeStruct((t_tok, D_MODEL), F32),
        scratch_shapes=[pltpu.VMEM((TC + 8, D_MODEL), F32), pltpu.VMEM((TC, D_MODEL), F32),
                        pltpu.VMEM((TC, D_MODEL), F32), pltpu.VMEM((8, D_MODEL), F32)],
        compiler_params=_cp(("arbitrary", "arbitrary"), VMEM_MID))(proj, cw, cb, w_a, b_a, w_x, b_x, lam)


def _lru_bwd(proj, hs, dyh, cw, cb, w_a, b_a, w_x, b_x, lam, seq):
    t_tok = proj.shape[0]
    nc = seq // TC

    def body(x_ref, xh_ref, g_ref, hs_ref, hh_ref, dy_ref, cw_ref, cb_ref, wa_ref, ba_ref, wx_ref, bx_ref, lam_ref,
             dp_ref, dcw_ref, dvec_ref, dwa_ref, dwx_ref,
             xe_s, he_s, de_s, a_s, r_s, i_s, m_s, dh_s, carry_s):
        b, cr = pl.program_id(0), pl.program_id(1)
        c = nc - 1 - cr

        @pl.when(jnp.logical_and(b == 0, cr == 0))
        def _():
            dcw_ref[...] = jnp.zeros_like(dcw_ref)
            dvec_ref[...] = jnp.zeros_like(dvec_ref)
            dwa_ref[...] = jnp.zeros_like(dwa_ref)
            dwx_ref[...] = jnp.zeros_like(dwx_ref)

        @pl.when(cr == 0)
        def _():
            carry_s[...] = jnp.zeros_like(carry_s)
            de_s[TC:TC + 8, :] = jnp.zeros((8, D_MODEL), F32)

        first = c == 0
        xe_s[0:8, :] = jnp.where(first, 0.0, xh_ref[...])
        xe_s[8:8 + TC, :] = x_ref[...]
        he_s[0:8, :] = jnp.where(first, 0.0, hh_ref[...])
        he_s[8:8 + TC, :] = hs_ref[...]
        xc = _conv_taps(xe_s, cw_ref, cb_ref)
        lam_v = lam_ref[...]
        sp = _softplus_neg(lam_v)
        for blk in range(LRU_BLOCKS):
            cols = slice(blk * LRU_BLOCK_W, (blk + 1) * LRU_BLOCK_W)
            _, r, ig, a, mult = _lru_gates(xc, blk, wa_ref, wx_ref, ba_ref, bx_ref, sp)
            a_s[:, cols], r_s[:, cols], i_s[:, cols], m_s[:, cols] = a, r, ig, mult

        gt = g_ref[...]
        dyh = dy_ref[...]
        dh_s[...] = dyh * _silu(gt)
        dp_ref[:, D_MODEL:] = (dyh * hs_ref[...] * _dsilu(gt)).astype(BF16)

        def step(k, carry):
            t = TC - 1 - k
            dh = dh_s[pl.ds(t, 1), :] + carry
            dh_s[pl.ds(t, 1), :] = dh
            return a_s[pl.ds(t, 1), :] * dh

        carry_s[0:1, :] = lax.fori_loop(0, TC, step, carry_s[0:1, :], unroll=8)

        hprev = he_s[7:7 + TC, :]
        for blk in range(LRU_BLOCKS):
            cols = slice(blk * LRU_BLOCK_W, (blk + 1) * LRU_BLOCK_W)
            xcb = xc[:, cols]
            a, r, ig, mult, dh = a_s[:, cols], r_s[:, cols], i_s[:, cols], m_s[:, cols], dh_s[:, cols]
            spb = sp[:, cols]
            dmult = dh * ig * xcb
            di = dh * mult * xcb
            dxc = dh * mult * ig
            dla = dh * hprev[:, cols] * a - dmult * (a * a) / jnp.maximum(mult, 1e-20)
            dr = dla * (-LRU_C * spb)
            dsp = jnp.sum(dla * (-LRU_C * r), axis=0, keepdims=True)
            dga = dr * r * (1.0 - r)
            dgx = di * ig * (1.0 - ig)
            dga_b, dgx_b = dga.astype(BF16), dgx.astype(BF16)
            xb = xcb.astype(BF16)
            dxc = dxc + _nt(dga_b, wa_ref[blk].astype(BF16)) + _nt(dgx_b, wx_ref[blk].astype(BF16))
            dwa_ref[blk] += _tn(xb, dga_b)
            dwx_ref[blk] += _tn(xb, dgx_b)
            dvec_ref[1:2, cols] += jnp.sum(dga, axis=0, keepdims=True)
            dvec_ref[2:3, cols] += jnp.sum(dgx, axis=0, keepdims=True)
            dvec_ref[3:4, cols] += dsp * (-1.0 / (1.0 + jnp.exp(lam_v[:, cols])))
            de_s[0:TC, cols] = dxc

        dxc = de_s[0:TC, :]
        dvec_ref[0:1, :] += jnp.sum(dxc, axis=0, keepdims=True)
        dxr = dxc * cw_ref[3:4, :]
        dcw_ref[3:4, :] += jnp.sum(dxc * xe_s[8:8 + TC, :], axis=0, keepdims=True)
        for k in range(1, 4):
            dxr = dxr + de_s[k:k + TC, :] * cw_ref[3 - k:4 - k, :]
            dcw_ref[3 - k:4 - k, :] += jnp.sum(dxc * xe_s[8 - k:8 - k + TC, :], axis=0, keepdims=True)
        dp_ref[:, :D_MODEL] = dxr.astype(BF16)
        de_s[TC:TC + 8, :] = de_s[0:8, :]

    chunk = lambda col: pl.BlockSpec((TC, D_MODEL), lambda b, cr: (b * nc + nc - 1 - cr, col))
    halo = lambda col: pl.BlockSpec(
        (8, D_MODEL), lambda b, cr: (jnp.maximum((b * nc + nc - 1 - cr) * (TC // 8) - 1, 0), col))
    full = lambda shape: pl.BlockSpec(shape, lambda b, cr: (0,) * len(shape))
    wblk = (LRU_BLOCKS, LRU_BLOCK_W, LRU_BLOCK_W)
    return _pallas(
        body, name="lru_bwd", grid=(t_tok // seq, nc),
        in_specs=[chunk(0), halo(0), chunk(1), chunk(0), halo(0), chunk(0),
                  full((4, D_MODEL)), full((1, D_MODEL)), full(wblk), full((1, D_MODEL)), full(wblk),
                  full((1, D_MODEL)), full((1, D_MODEL))],
        out_specs=[pl.BlockSpec((TC, 2 * D_MODEL), lambda b, cr: (b * nc + nc - 1 - cr, 0)),
                   full((8, D_MODEL)), full((8, D_MODEL)), full(wblk), full(wblk)],
        out_shape=[jax.ShapeDtypeStruct((t_tok, 2 * D_MODEL), BF16), jax.ShapeDtypeStruct((8, D_MODEL), F32),
                   jax.ShapeDtypeStruct((8, D_MODEL), F32), jax.ShapeDtypeStruct(wblk, F32),
                   jax.ShapeDtypeStruct(wblk, F32)],
        scratch_shapes=[pltpu.VMEM((TC + 8, D_MODEL), F32), pltpu.VMEM((TC + 8, D_MODEL), F32),
                        pltpu.VMEM((TC + 8, D_MODEL), F32)]
        + [pltpu.VMEM((TC, D_MODEL), F32)] * 5 + [pltpu.VMEM((8, D_MODEL), F32)],
        compiler_params=_cp(("arbitrary", "arbitrary"), VMEM_BIG),
    )(proj, proj, proj, hs, hs, dyh, cw, cb, w_a, b_a, w_x, b_x, lam)


def _final_loss(x, g, target):
    t_tok = x.shape[0]

    def body(x_ref, g_ref, t_ref, dx_ref, loss_ref, dg_ref):
        @pl.when(pl.program_id(0) == 0)
        def _():
            loss_ref[...] = jnp.zeros_like(loss_ref)
            dg_ref[...] = jnp.zeros_like(dg_ref)

        xv = x_ref[...]
        gv = g_ref[...]
        rstd = lax.rsqrt(jnp.mean(xv * xv, axis=-1, keepdims=True) + EPS)
        xhat = xv * rstd
        err = xhat * gv - t_ref[...]
        loss_ref[0:1, :] += jnp.sum(err * err, axis=0, keepdims=True) * (0.5 / D_MODEL)
        dout = err * (1.0 / D_MODEL)
        dg_ref[0:1, :] += jnp.sum(dout * xhat, axis=0, keepdims=True)
        dxhat = dout * gv
        dx_ref[...] = rstd * (dxhat - xhat * jnp.mean(dxhat * xhat, axis=-1, keepdims=True))

    row = pl.BlockSpec((TM, D_MODEL), lambda i: (i, 0))
    acc = pl.BlockSpec((8, D_MODEL), lambda i: (0, 0))
    return _pallas(body, name="final_loss", grid=(t_tok // TM,),
                   in_specs=[row, pl.BlockSpec((1, D_MODEL), lambda i: (0, 0)), row],
                   out_specs=[row, acc, acc],
                   out_shape=[jax.ShapeDtypeStruct((t_tok, D_MODEL), F32)] + [jax.ShapeDtypeStruct((8, D_MODEL), F32)] * 2,
                   compiler_params=_cp(("arbitrary",), VMEM_MID))(x, g, target)


def _adam_math(w, g, m, v):
    m_new = ADAM_B1 * m + (1.0 - ADAM_B1) * g
    v_new = ADAM_B2 * v + (1.0 - ADAM_B2) * (g * g)
    m_hat = m_new / (1.0 - ADAM_B1 ** ADAM_STEP)
    v_hat = v_new / (1.0 - ADAM_B2 ** ADAM_STEP)
    delta = -ADAM_LR * (m_hat / (jnp.sqrt(v_hat) + ADAM_EPS) + ADAM_WD * w)
    return delta, m_new, v_new


def _sum_leading(name, x):
    n, rows, cols = x.shape
    tr = PACK_ROWS if rows % PACK_ROWS == 0 else rows

    def body(x_ref, o_ref):
        acc = x_ref[0]
        for d in range(1, n):
            acc = acc + x_ref[d]
        o_ref[...] = acc

    return _pallas(body, name=name, grid=(rows // tr,),
                   in_specs=[pl.BlockSpec((n, tr, cols), lambda i: (0, i, 0))],
                   out_specs=pl.BlockSpec((tr, cols), lambda i: (i, 0)),
                   out_shape=jax.ShapeDtypeStruct((rows, cols), F32),
                   compiler_params=_cp(("arbitrary",), VMEM_MID))(x)


def _adamw(name, w, m, v, g=None, parts=None):
    rows, cols = w.shape
    tr = rows if rows <= 256 else 256

    def body(*refs):
        w_ref, m_ref, v_ref, g_in, g_ref, d_ref, mo_ref, vo_ref = refs
        if parts is None:
            gv = g_in[...]
        else:
            acc = g_in[0].astype(F32)
            for d in range(1, N_DEV):
                acc = acc + g_in[d].astype(F32)
            gv = acc[:, :cols]
        delta, m_new, v_new = _adam_math(w_ref[...], gv, m_ref[...], v_ref[...])
        g_ref[...] = gv
        d_ref[...] = delta
        mo_ref[...] = m_new
        vo_ref[...] = v_new

    row = pl.BlockSpec((tr, cols), lambda i: (i, 0))
    if parts is None:
        g_spec, g_arg = row, g
    else:
        g_spec, g_arg = pl.BlockSpec((N_DEV, tr, parts.shape[2]), lambda i: (0, i, 0)), parts
    return _pallas(body, name=name, grid=(rows // tr,), in_specs=[row, row, row, g_spec], out_specs=[row] * 4,
                   out_shape=[jax.ShapeDtypeStruct((rows, cols), F32)] * 4,
                   compiler_params=_cp(("arbitrary",), VMEM_MID))(w, m, v, g_arg)


def _pack_rows(arrs):
    rows, meta, total = [], [], 0
    for a in arrs:
        flat = a.reshape(-1)
        nrow = -(-flat.shape[0] // 1024) * 8
        rows.append(jnp.pad(flat, (0, nrow * 128 - flat.shape[0])).reshape(nrow, 128))
        meta.append((a.shape, flat.shape[0], nrow))
        total += nrow
    tail = -total % PACK_ROWS
    if tail:
        rows.append(jnp.zeros((tail, 128), F32))
    return jnp.concatenate(rows, axis=0), meta


def _unpack_rows(packed, meta):
    out, r0 = [], 0
    for shape, size, nrow in meta:
        out.append(packed[r0:r0 + nrow].reshape(-1)[:size].reshape(shape))
        r0 += nrow
    return out


WEIGHTS = ["rel_bias", "norm_g", "ada_w", "ada_b", "attn_w_in", "attn_sinks", "attn_b_f", "attn_w_out", "lru_w_in",
           "lru_conv_w", "lru_conv_b", "lru_w_a", "lru_b_a", "lru_w_x", "lru_b_x", "lru_lambda", "lru_w_out", "final_g"]
BIG = ["ada_w", "attn_w_in", "attn_w_out", "lru_w_in", "lru_w_out"]
PACK_ROWS = 256


def kernel(x, c, rel_bias, norm_g, ada_w, ada_b, attn_w_in, attn_sinks, attn_b_f, attn_w_out, lru_w_in, lru_conv_w, lru_conv_b, lru_w_a, lru_b_a, lru_w_x, lru_b_x, lru_lambda, lru_w_out, final_g, loss_target, m_rel_bias, m_norm_g, m_ada_w, m_ada_b, m_attn_w_in, m_attn_sinks, m_attn_b_f, m_attn_w_out, m_lru_w_in, m_lru_conv_w, m_lru_conv_b, m_lru_w_a, m_lru_b_a, m_lru_w_x, m_lru_b_x, m_lru_lambda, m_lru_w_out, m_final_g, v_rel_bias, v_norm_g, v_ada_w, v_ada_b, v_attn_w_in, v_attn_sinks, v_attn_b_f, v_attn_w_out, v_lru_w_in, v_lru_conv_w, v_lru_conv_b, v_lru_w_a, v_lru_b_a, v_lru_w_x, v_lru_b_x, v_lru_lambda, v_lru_w_out, v_final_g):
    nseq, seq, _ = x.shape
    t_tok = nseq * seq
    me = 4 * lax.axis_index("x") + 2 * lax.axis_index("y") + lax.axis_index("c")
    x0 = x.reshape(t_tok, D_MODEL)
    target = loss_target.reshape(t_tok, D_MODEL)

    w_in_pad = jnp.pad(attn_w_in[0].astype(BF16), ((0, 0), (0, SHARD_W_PAD - SHARD_W_IN)))
    vec_shard = jnp.concatenate([lru_conv_w[0], lru_conv_b, lru_b_a, lru_b_x, lru_lambda], axis=0)
    g_w_in, g_w_out0, g_lru_in, g_w_out1, g_vec, g_c = _exchange(
        "gather_weights",
        [w_in_pad, attn_w_out[0].astype(BF16), lru_w_in[0].astype(BF16), lru_w_out[0].astype(BF16), vec_shard, c], [])
    w_full = jnp.transpose(g_w_in[:, :, :SHARD_W_IN], (1, 0, 2)).reshape(D_MODEL, N_DEV * SHARD_W_IN)
    w_aq, w_ak, w_av = w_full[:, 0:512], w_full[:, 512:640], w_full[:, 640:768]
    w_bq, w_bk, w_bv = w_full[:, 768:1280], w_full[:, 1280:1792], w_full[:, 1792:2304]
    w_f, w_gate = w_full[:, 2304:2312], w_full[:, 2312:3336]
    w_main = jnp.concatenate([w_bq, w_bk, w_bv, w_aq, w_gate, w_ak, w_av], axis=1)
    wf_t = jnp.transpose(w_f)
    w_out0 = g_w_out0.reshape(D_MODEL, D_MODEL)
    w_out1 = g_w_out1.reshape(D_MODEL, D_MODEL)
    vec_full = jnp.transpose(g_vec, (1, 0, 2)).reshape(8, D_MODEL)
    conv_w, conv_b, b_a, b_x, lam = vec_full[0:4], vec_full[4:5], vec_full[5:6], vec_full[6:7], vec_full[7:8]
    c_all = g_c.reshape(N_DEV * nseq, D_MODEL)

    ncol = ada_w.shape[2]
    ada_b_slice = lax.dynamic_slice(ada_b.reshape(2, N_DEV, ncol), (0, me, 0), (2, 1, ncol))
    mod_part = _ada_mod(c_all, ada_w, ada_b_slice)
    (g_mod,) = _exchange("gather_mod", [mod_part], [])
    mine = lax.dynamic_slice(g_mod, (0, 0, me * nseq, 0), (N_DEV, 2, nseq, ncol))
    mod = jnp.transpose(mine, (1, 2, 0, 3)).reshape(2, nseq, 3 * D_MODEL)
    shift = [mod[l, :, 0:D_MODEL].reshape(nseq, 1, D_MODEL) for l in range(2)]
    scale = [mod[l, :, D_MODEL:2 * D_MODEL].reshape(nseq, 1, D_MODEL) for l in range(2)]
    gmod = [mod[l, :, 2 * D_MODEL:].reshape(nseq, 1, D_MODEL) for l in range(2)]

    onehot = _bucket_onehot()
    bias = _bias_expand(jnp.transpose(rel_bias), onehot).reshape(N_HEADS, BLOCK, 2 * BLOCK)
    sinks = attn_sinks.reshape(N_HEADS)
    b_f = attn_b_f.reshape(N_HEADS, 1)
    h0, qkvg, fl_t = _norm_proj("norm_proj0", x0, norm_g[0:1], shift[0], scale[0], w_main, seq, BF16, wf_t=wf_t)
    f_row, f_col = _fox_prep(fl_t, b_f, seq)
    a_out, lse_a = _swa_fwd(qkvg, bias, sinks, seq)
    b_out, lse_b = _fox_fwd(qkvg, f_row, f_col, seq)
    yg0, y0, x1 = _out_proj("out_proj0", [a_out, b_out], qkvg, C_GATE // D_MODEL, w_out0, x0, gmod[0], seq)

    h1, proj1 = _norm_proj("norm_proj1", x1, norm_g[1:2], shift[1], scale[1], g_lru_in, seq, F32)
    hs = _lru_fwd(proj1, conv_w, conv_b, lru_w_a[0], b_a, lru_w_x[0], b_x, lam, seq)
    yg1, y1, x2 = _out_proj("out_proj1", [hs], proj1, 1, w_out1, x1, gmod[1], seq)

    dx2, loss_rows, dfinal_rows = _final_loss(x2, final_g.reshape(1, D_MODEL), target)
    loss = lax.psum(jnp.sum(loss_rows[0]), ("x", "y", "c"))

    dy1, dgm1, dyh = _out_proj_bwd("out_proj1_bwd", dx2, gmod[1], y1, w_out1, seq)
    dproj1, dcw, dvec, dw_a, dw_x = _lru_bwd(proj1, hs, dyh, conv_w, conv_b, lru_w_a[0], b_a, lru_w_x[0], b_x, lam, seq)
    dx1, dss1, dg1 = _norm_bwd("norm1_bwd", [(dproj1, 0)], g_lru_in, x1, norm_g[1:2], scale[1], dx2, seq)
    (p_w_out1,) = _dw("dw_out1", yg1, [dy1])
    (p_lru_in,) = _dw("dw_lru_in", h1, [dproj1], blocked=2 * D_MODEL // N_DEV)

    dy0, dgm0, du_a, du_b, dgate = _out_proj_bwd("out_proj0_bwd", dx1, gmod[0], y0, w_out0, seq,
                                                  attn=(a_out, b_out, qkvg))
    dq_a, dkv_a, dbias, dsink = _swa_bwd(qkvg, du_a, a_out, lse_a, bias, sinks, seq)
    dq_b, dk_b, dv_b, df4 = _fox_bwd(qkvg, du_b, b_out, lse_b, f_row, f_col, seq)
    dfl_t, db_f = _fox_post(df4.reshape(N_HEADS, t_tok), fl_t, b_f, seq)
    parts0 = [(dq_b, C_BQ), (dk_b, C_BK), (dv_b, C_BV), (dq_a, C_AQ), (dgate, C_GATE), (dkv_a, C_AK)]
    dx0, dss0, dg0 = _norm_bwd("norm0_bwd", parts0, w_main, x0, norm_g[0:1], scale[0], dx1, seq,
                               rows_part=(dfl_t, wf_t))
    (p_w_out0,) = _dw("dw_out0", yg0, [dy0])
    pw_bq, pw_bk, pw_bv, pw_aq, pw_gate, pw_akv = _dw("dw_attn_in", h0, [p for p, _ in parts0])
    pw_f = _dw_rows("dw_f", dfl_t, h0)
    dbias_t = _bias_reduce(dbias.reshape(N_HEADS, BLOCK * 2 * BLOCK), onehot)

    p_w_in = jnp.concatenate([pw_aq, pw_akv, pw_bq, pw_bk, pw_bv, jnp.transpose(pw_f).astype(BF16), pw_gate], axis=1)
    p_w_in = jnp.transpose(p_w_in.reshape(D_MODEL, N_DEV, SHARD_W_IN), (1, 0, 2))
    p_w_in = jnp.pad(p_w_in, ((0, 0), (0, 0), (0, SHARD_W_PAD - SHARD_W_IN)))
    rows_out = D_MODEL // N_DEV
    small_partials = [jnp.transpose(dbias_t), jnp.stack([dg0[0], dg1[0]]), dsink[:, 0], db_f[:, 0],
                      dcw[0:4], dvec[0:4], dw_a, dw_x, dfinal_rows[0]]
    gpack, gmeta = _pack_rows(small_partials)
    dmod = jnp.stack([jnp.concatenate([dss[:, 0], dss[:, 1], dgm[:, 0]], axis=1)
                      for dss, dgm in ((dss0, dgm0), (dss1, dgm1))], axis=1)
    g_small, g_dmod, r_w_in, r_w_out0, r_lru_in, r_w_out1 = _exchange(
        "exchange_grads", [gpack, dmod],
        [p_w_in, p_w_out0.reshape(N_DEV, rows_out, D_MODEL), p_lru_in, p_w_out1.reshape(N_DEV, rows_out, D_MODEL)])

    small_sum = _sum_leading("sum_small", g_small)
    d_rel, d_norm_g, d_sinks, d_b_f, d_cw, d_vec, d_wa, d_wx, d_final_g = _unpack_rows(small_sum, gmeta)
    cols = lambda a: lax.dynamic_slice(a, (0, me * LRU_BLOCK_W), (a.shape[0], LRU_BLOCK_W))
    dmod_all = g_dmod.reshape(N_DEV * nseq, 2 * 3 * D_MODEL)
    d_ada_b = _sum_leading("sum_ada_b", dmod_all.reshape(N_DEV * nseq, 2 * 3 * D_MODEL // 128, 128)).reshape(2, 3 * D_MODEL)
    dmod_slice = lax.dynamic_slice(dmod_all.reshape(N_DEV * nseq, 2, N_DEV, ncol), (0, 0, me, 0),
                                   (N_DEV * nseq, 2, 1, ncol)).reshape(N_DEV * nseq, 2, ncol)
    d_ada_w = _ada_w_grad(c_all, jnp.transpose(dmod_slice, (1, 0, 2)))

    given = dict(
        rel_bias=(rel_bias, m_rel_bias, v_rel_bias), norm_g=(norm_g, m_norm_g, v_norm_g),
        ada_w=(ada_w, m_ada_w, v_ada_w), ada_b=(ada_b, m_ada_b, v_ada_b),
        attn_w_in=(attn_w_in, m_attn_w_in, v_attn_w_in), attn_sinks=(attn_sinks, m_attn_sinks, v_attn_sinks),
        attn_b_f=(attn_b_f, m_attn_b_f, v_attn_b_f), attn_w_out=(attn_w_out, m_attn_w_out, v_attn_w_out),
        lru_w_in=(lru_w_in, m_lru_w_in, v_lru_w_in), lru_conv_w=(lru_conv_w, m_lru_conv_w, v_lru_conv_w),
        lru_conv_b=(lru_conv_b, m_lru_conv_b, v_lru_conv_b), lru_w_a=(lru_w_a, m_lru_w_a, v_lru_w_a),
        lru_b_a=(lru_b_a, m_lru_b_a, v_lru_b_a), lru_w_x=(lru_w_x, m_lru_w_x, v_lru_w_x),
        lru_b_x=(lru_b_x, m_lru_b_x, v_lru_b_x), lru_lambda=(lru_lambda, m_lru_lambda, v_lru_lambda),
        lru_w_out=(lru_w_out, m_lru_w_out, v_lru_w_out), final_g=(final_g, m_final_g, v_final_g))
    results = {}

    def big(name, shape2d, g=None, parts=None):
        w, m, v = (a.reshape(shape2d) for a in given[name])
        outs = _adamw("adamw_" + name, w, m, v, g=g, parts=parts)
        results[name] = tuple(o.reshape(given[name][0].shape) for o in outs)

    big("ada_w", (2 * D_MODEL, ncol), g=d_ada_w.reshape(2 * D_MODEL, ncol))
    big("attn_w_in", (D_MODEL, SHARD_W_IN), parts=r_w_in)
    big("attn_w_out", (rows_out, D_MODEL), parts=r_w_out0)
    big("lru_w_in", (D_MODEL, 2 * D_MODEL // N_DEV), parts=r_lru_in)
    big("lru_w_out", (rows_out, D_MODEL), parts=r_w_out1)

    small_grads = dict(
        rel_bias=d_rel, norm_g=d_norm_g, ada_b=d_ada_b, attn_sinks=d_sinks.reshape(1, N_HEADS),
        attn_b_f=d_b_f.reshape(1, N_HEADS), lru_conv_w=cols(d_cw).reshape(1, 4, LRU_BLOCK_W),
        lru_conv_b=cols(d_vec[0:1]), lru_w_a=d_wa.reshape(lru_w_a.shape), lru_b_a=cols(d_vec[1:2]),
        lru_w_x=d_wx.reshape(lru_w_x.shape), lru_b_x=cols(d_vec[2:3]), lru_lambda=cols(d_vec[3:4]),
        final_g=d_final_g)
    small = [n for n in WEIGHTS if n not in BIG]
    wpack, smeta = _pack_rows([given[n][0] for n in small])
    mpack, _ = _pack_rows([given[n][1] for n in small])
    vpack, _ = _pack_rows([given[n][2] for n in small])
    gpack2, _ = _pack_rows([small_grads[n] for n in small])
    packs = _adamw("adamw_small", wpack, mpack, vpack, g=gpack2)
    unpacked = [_unpack_rows(p, smeta) for p in packs]
    for k, n in enumerate(small):
        results[n] = tuple(unpacked[j][k] for j in range(4))

    grad_x = dx0.reshape(x.shape)
    out = [loss, grad_x]
    for j in range(4):
        out += [results[n][j] for n in WEIGHTS]
    return tuple(out)
```

```python
import functools
import math

import jax
import jax.numpy as jnp
from jax import lax
from jax.experimental import pallas as pl
from jax.experimental.pallas import tpu as pltpu

F32 = jnp.float32
BF16 = jnp.bfloat16
HI = lax.Precision.HIGHEST
MESH = pl.DeviceIdType.MESH

N_DEV = 8
D_MODEL = 1024
HEAD_DIM = 64
N_HEADS = 8
KV_GROUP = 4
BLOCK = 128
REL_BUCKETS = 32
REL_MAX_EXACT = 16
REL_MAX_DIST = 128
LRU_BLOCKS = 8
LRU_BLOCK_W = 128
LRU_C = 8.0
EPS = 1e-6
SCALE = HEAD_DIM ** -0.5
NEG = -1e30

ADAM_LR = 0.001
ADAM_B1 = 0.9
ADAM_B2 = 0.999
ADAM_EPS = 1e-08
ADAM_WD = 0.01
ADAM_STEP = 10

C_BQ, C_BK, C_BV, C_AQ, C_GATE, C_AK, C_AV = 0, 512, 1024, 1536, 2048, 3072, 3200
N_MAIN = 3328
SHARD_W_IN = 417
SHARD_W_PAD = 512

TM = 256
TQ = 256
TK = 128
TC = 256
VMEM_BIG = 56 * 1024 * 1024
VMEM_MID = 40 * 1024 * 1024


def _pallas(body, **kw):
    return pl.pallas_call(body, **kw)


def _cp(sem=None, vmem=None):
    kw = {}
    if sem is not None:
        kw["dimension_semantics"] = sem
    if vmem is not None:
        kw["vmem_limit_bytes"] = vmem
    return pltpu.CompilerParams(**kw)


def _nn(a, b, precision=None):
    return jnp.dot(a, b, preferred_element_type=F32, precision=precision)


def _nt(a, b, precision=None):
    return lax.dot_general(a, b, (((1,), (1,)), ((), ())), preferred_element_type=F32, precision=precision)


def _tn(a, b, precision=None):
    return lax.dot_general(a, b, (((0,), (0,)), ((), ())), preferred_element_type=F32, precision=precision)


def _sigmoid(x):
    return 1.0 / (1.0 + jnp.exp(-x))


def _silu(x):
    return x * _sigmoid(x)


def _dsilu(x):
    s = _sigmoid(x)
    return s * (1.0 + x * (1.0 - s))


def _neg_expm1(x):
    poly = x * (1.0 + x * (0.5 + x * (1.0 / 6.0 + x * (1.0 / 24.0))))
    return -jnp.where(jnp.abs(x) < 0.05, poly, jnp.exp(x) - 1.0)


def _col(tile, idx):
    lane = lax.broadcasted_iota(jnp.int32, tile.shape, 1)
    return jnp.sum(jnp.where(lane == idx, tile, 0.0), axis=1, keepdims=True)


def _row(tile, idx):
    sub = lax.broadcasted_iota(jnp.int32, tile.shape, 0)
    return jnp.sum(jnp.where(sub == idx, tile, 0.0), axis=0, keepdims=True)


def _exchange(name, gathers, scatters):
    ng, n = len(gathers), len(gathers) + len(scatters)
    ins = list(gathers) + list(scatters)

    def body(*refs):
        in_refs, out_refs = refs[:n], refs[n:2 * n]
        send_sems, recv_sems, loc_sems = refs[2 * n:]
        x, y, c = lax.axis_index("x"), lax.axis_index("y"), lax.axis_index("c")
        me = 4 * x + 2 * y + c

        def peer(r):
            px = 1 - x if r & 4 else x
            py = 1 - y if r & 2 else y
            pc = 1 - c if r & 1 else c
            return (px, py, pc), 4 * px + 2 * py + pc

        local, sends, recvs = [], [], []
        for k in range(n):
            mine = in_refs[k] if k < ng else in_refs[k].at[me]
            cp = pltpu.make_async_copy(mine, out_refs[k].at[me], loc_sems.at[k])
            cp.start()
            local.append(cp)
            for r in range(1, N_DEV):
                pid, pidx = peer(r)
                src = in_refs[k] if k < ng else in_refs[k].at[pidx]
                snd = pltpu.make_async_remote_copy(
                    src_ref=src, dst_ref=out_refs[k].at[me], send_sem=send_sems.at[r - 1, k],
                    recv_sem=recv_sems.at[r - 1, k], device_id=pid, device_id_type=MESH)
                snd.start()
                sends.append(snd)
                recvs.append(pltpu.make_async_remote_copy(
                    src_ref=src, dst_ref=out_refs[k].at[pidx], send_sem=send_sems.at[r - 1, k],
                    recv_sem=recv_sems.at[r - 1, k], device_id=pid, device_id_type=MESH))
        for rc in recvs:
            rc.wait_recv()
        for snd in sends:
            snd.wait_send()
        for cp in local:
            cp.wait()

    out_shape = [jax.ShapeDtypeStruct((N_DEV,) + a.shape, a.dtype) for a in gathers]
    out_shape += [jax.ShapeDtypeStruct(a.shape, a.dtype) for a in scatters]
    any_spec = pl.BlockSpec(memory_space=pl.ANY)
    return _pallas(
        body, name=name, out_shape=out_shape,
        in_specs=[any_spec] * n, out_specs=[any_spec] * n,
        scratch_shapes=[pltpu.SemaphoreType.DMA((N_DEV - 1, n)), pltpu.SemaphoreType.DMA((N_DEV - 1, n)),
                        pltpu.SemaphoreType.DMA((n,))],
    )(*ins)


def _ada_mod(c_all, ada_w, ada_b_slice):
    def body(c_ref, w_ref, b_ref, o_ref):
        ca = _silu(c_ref[...])
        for l in range(2):
            o_ref[l] = _nn(ca, w_ref[l], HI) + b_ref[l]

    return _pallas(body, name="ada_mod",
                   out_shape=jax.ShapeDtypeStruct((2, c_all.shape[0], ada_w.shape[2]), F32),
                   compiler_params=_cp(vmem=VMEM_MID))(c_all, ada_w, ada_b_slice)


def _ada_w_grad(c_all, dmod_slice):
    def body(c_ref, d_ref, o_ref):
        ca = _silu(c_ref[...])
        for l in range(2):
            o_ref[l] = _tn(ca, d_ref[l], HI)

    return _pallas(body, name="ada_w_grad",
                   out_shape=jax.ShapeDtypeStruct((2, D_MODEL, dmod_slice.shape[2]), F32),
                   compiler_params=_cp(vmem=VMEM_MID))(c_all, dmod_slice)


def _bucket_onehot():
    qi = jnp.arange(BLOCK)[:, None]
    kj = jnp.arange(2 * BLOCK)[None, :]
    rel = qi - kj + BLOCK
    n = jnp.maximum(rel, 0)
    nf = jnp.maximum(n, 1).astype(F32)
    large = REL_MAX_EXACT + (jnp.log(nf / REL_MAX_EXACT) / math.log(REL_MAX_DIST / REL_MAX_EXACT)
                             * (REL_BUCKETS - REL_MAX_EXACT)).astype(jnp.int32)
    large = jnp.minimum(large, REL_BUCKETS - 1)
    bucket = jnp.where(n < REL_MAX_EXACT, n, large).reshape(1, BLOCK * 2 * BLOCK)
    return (jnp.arange(REL_BUCKETS)[:, None] == bucket).astype(F32)


def _bias_expand(rel_bias_t, onehot):
    def body(r_ref, e_ref, o_ref):
        o_ref[...] = _nn(r_ref[...], e_ref[...], HI)

    return _pallas(body, name="bias_expand",
                   out_shape=jax.ShapeDtypeStruct((N_HEADS, onehot.shape[1]), F32),
                   compiler_params=_cp(vmem=VMEM_MID))(rel_bias_t, onehot)


def _bias_reduce(dbias, onehot):
    def body(d_ref, e_ref, o_ref):
        o_ref[...] = _nt(d_ref[...], e_ref[...], HI)

    return _pallas(body, name="bias_reduce",
                   out_shape=jax.ShapeDtypeStruct((N_HEADS, REL_BUCKETS), F32),
                   compiler_params=_cp(vmem=VMEM_MID))(dbias, onehot)


def _norm_proj(name, x, g, shift, scale, w, seq, out_dtype, wf_t=None):
    t_tok = x.shape[0]
    w3d = w.ndim == 3
    n_out = w.shape[0] * w.shape[2] if w3d else w.shape[1]
    cn = w.shape[2] if w3d else 256

    def body(x_ref, g_ref, sh_ref, sc_ref, w_ref, *rest):
        if wf_t is not None:
            wf_ref, h_ref, o_ref, fl_ref = rest
        else:
            h_ref, o_ref = rest
        xv = x_ref[...]
        rstd = lax.rsqrt(jnp.mean(xv * xv, axis=-1, keepdims=True) + EPS)
        h = (xv * rstd) * g_ref[...] * (1.0 + sc_ref[...]) + sh_ref[...]
        hb = h.astype(BF16)
        h_ref[...] = hb
        for j in range(n_out // cn):
            wj = w_ref[j] if w3d else w_ref[:, j * cn:(j + 1) * cn]
            o_ref[:, j * cn:(j + 1) * cn] = _nn(hb, wj).astype(out_dtype)
        if wf_t is not None:
            fl_ref[...] = _nt(wf_ref[...], hb)

    mod_spec = pl.BlockSpec((None, 1, D_MODEL), lambda i: (i * TM // seq, 0, 0))
    w_spec = (pl.BlockSpec(w.shape, lambda i: (0, 0, 0)) if w3d else pl.BlockSpec(w.shape, lambda i: (0, 0)))
    in_specs = [pl.BlockSpec((TM, D_MODEL), lambda i: (i, 0)), pl.BlockSpec((1, D_MODEL), lambda i: (0, 0)),
                mod_spec, mod_spec, w_spec]
    out_shape = [jax.ShapeDtypeStruct((t_tok, D_MODEL), BF16), jax.ShapeDtypeStruct((t_tok, n_out), out_dtype)]
    out_specs = [pl.BlockSpec((TM, D_MODEL), lambda i: (i, 0)), pl.BlockSpec((TM, n_out), lambda i: (i, 0))]
    args = [x, g, shift, scale, w]
    if wf_t is not None:
        in_specs.append(pl.BlockSpec(wf_t.shape, lambda i: (0, 0)))
        out_shape.append(jax.ShapeDtypeStruct((wf_t.shape[0], t_tok), F32))
        out_specs.append(pl.BlockSpec((wf_t.shape[0], TM), lambda i: (0, i)))
        args.append(wf_t)
    return _pallas(body, name=name, grid=(t_tok // TM,), in_specs=in_specs, out_specs=out_specs,
                   out_shape=out_shape, compiler_params=_cp(("arbitrary",), VMEM_BIG))(*args)


def _fox_prep(fl_t, b_f, seq):
    t_tok = fl_t.shape[1]
    ch = 256

    def body(fl_ref, bf_ref, fr_ref, fc_ref):
        z = fl_ref[...] + bf_ref[...]
        logf = jnp.minimum(z, 0.0) - jnp.log(1.0 + jnp.exp(-jnp.abs(z)))
        ri = lax.broadcasted_iota(jnp.int32, (ch, ch), 0)
        ci = lax.broadcasted_iota(jnp.int32, (ch, ch), 1)
        upper = (ri <= ci).astype(F32)
        eye = (ri == ci).astype(F32)
        carry = jnp.zeros((N_HEADS, 1), F32)
        for k in range(seq // ch):
            fk = _nn(logf[:, k * ch:(k + 1) * ch], upper, HI) + carry
            carry = fk[:, ch - 1:ch]
            fr_ref[:, k * ch:(k + 1) * ch] = fk
            padded = jnp.concatenate([fk, jnp.zeros((128 - N_HEADS, ch), F32)], axis=0)
            fc_ref[k * ch:(k + 1) * ch, :] = _nt(eye, padded, HI)

    return _pallas(
        body, name="fox_prep", grid=(t_tok // seq,),
        in_specs=[pl.BlockSpec((N_HEADS, seq), lambda b: (0, b)), pl.BlockSpec((N_HEADS, 1), lambda b: (0, 0))],
        out_specs=[pl.BlockSpec((N_HEADS, seq), lambda b: (0, b)), pl.BlockSpec((seq, 128), lambda b: (b, 0))],
        out_shape=[jax.ShapeDtypeStruct((N_HEADS, t_tok), F32), jax.ShapeDtypeStruct((t_tok, 128), F32)],
        compiler_params=_cp(("arbitrary",), VMEM_MID))(fl_t, b_f)


def _fox_post(df_row, fl_t, b_f, seq):
    t_tok = fl_t.shape[1]
    ch = 256

    def body(d_ref, fl_ref, bf_ref, o_ref, db_ref):
        @pl.when(pl.program_id(0) == 0)
        def _():
            db_ref[...] = jnp.zeros_like(db_ref)

        z = fl_ref[...] + bf_ref[...]
        sig_neg = 1.0 / (1.0 + jnp.exp(z))
        ri = lax.broadcasted_iota(jnp.int32, (ch, ch), 0)
        ci = lax.broadcasted_iota(jnp.int32, (ch, ch), 1)
        lower = (ri >= ci).astype(F32)
        carry = jnp.zeros((N_HEADS, 1), F32)
        tot = jnp.zeros((N_HEADS, 1), F32)
        for k in reversed(range(seq // ch)):
            dk = _nn(d_ref[:, k * ch:(k + 1) * ch], lower, HI) + carry
            carry = dk[:, 0:1]
            dfl = dk * sig_neg[:, k * ch:(k + 1) * ch]
            o_ref[:, k * ch:(k + 1) * ch] = dfl
            tot = tot + jnp.sum(dfl, axis=1, keepdims=True)
        db_ref[...] += jnp.broadcast_to(tot, db_ref.shape)

    return _pallas(
        body, name="fox_post", grid=(t_tok // seq,),
        in_specs=[pl.BlockSpec((N_HEADS, seq), lambda b: (0, b)), pl.BlockSpec((N_HEADS, seq), lambda b: (0, b)),
                  pl.BlockSpec((N_HEADS, 1), lambda b: (0, 0))],
        out_specs=[pl.BlockSpec((N_HEADS, seq), lambda b: (0, b)), pl.BlockSpec((N_HEADS, 128), lambda b: (0, 0))],
        out_shape=[jax.ShapeDtypeStruct((N_HEADS, t_tok), F32), jax.ShapeDtypeStruct((N_HEADS, 128), F32)],
        compiler_params=_cp(("arbitrary",), VMEM_MID))(df_row, fl_t, b_f)


def _eye(n, dtype):
    return (lax.broadcasted_iota(jnp.int32, (n, n), 0) == lax.broadcasted_iota(jnp.int32, (n, n), 1)).astype(dtype)


def _fox_aug(qkvg, f_col, seq):
    t_tok = qkvg.shape[0]
    ta = 256
    nkb = ta // TK

    def body(q_ref, k_ref, v_ref, fc_ref, qa_ref, ka_ref, kt_ref, vt_ref):
        ri = lax.broadcasted_iota(jnp.int32, (128, 128), 0)
        ci = lax.broadcasted_iota(jnp.int32, (128, 128), 1)
        eye = (ri == ci).astype(BF16)
        lane = lax.broadcasted_iota(jnp.int32, (ta, 128), 1)
        ones_q = jnp.where(jnp.logical_and(lane >= 64, lane < 67), 1.0, 0.0)
        ones_k = jnp.where(jnp.logical_and(lane >= 67, lane < 70), 1.0, 0.0)
        fc_tile = fc_ref[...]
        for p in range(N_HEADS // 2):
            q2 = q_ref[:, 128 * p:128 * (p + 1)]
            k2 = k_ref[:, 128 * p:128 * (p + 1)]
            vt = _nt(eye, v_ref[:, 128 * p:128 * (p + 1)]).astype(BF16)
            for kk in range(nkb):
                vt_ref[p, kk] = vt[:, kk * TK:(kk + 1) * TK]
            for e in range(2):
                h = 2 * p + e
                sel = jnp.logical_and(ri == ci + HEAD_DIM * e, ci < HEAD_DIM)
                f = _col(fc_tile, h)
                fh = f.astype(BF16).astype(F32)
                fm = (f - fh).astype(BF16).astype(F32)
                fl = (f - fh - fm).astype(BF16).astype(F32)
                qa = (_nn(q2, jnp.where(sel, SCALE, 0.0).astype(BF16)) + ones_q + jnp.where(lane == 67, fh, 0.0)
                      + jnp.where(lane == 68, fm, 0.0) + jnp.where(lane == 69, fl, 0.0))
                ka = (_nn(k2, jnp.where(sel, 1.0, 0.0).astype(BF16)) + ones_k - jnp.where(lane == 64, fh, 0.0)
                      - jnp.where(lane == 65, fm, 0.0) - jnp.where(lane == 66, fl, 0.0))
                qa_ref[h] = qa.astype(BF16)
                kab = ka.astype(BF16)
                ka_ref[h] = kab
                kt = _nt(eye, kab).astype(BF16)
                for kk in range(nkb):
                    kt_ref[h, kk] = kt[:, kk * TK:(kk + 1) * TK]

    aug = jax.ShapeDtypeStruct((N_HEADS, t_tok, 128), BF16)
    return _pallas(
        body, name="fox_aug", grid=(t_tok // ta,),
        in_specs=[pl.BlockSpec((ta, 512), lambda i: (i, C_BQ // 512)), pl.BlockSpec((ta, 512), lambda i: (i, C_BK // 512)),
                  pl.BlockSpec((ta, 512), lambda i: (i, C_BV // 512)), pl.BlockSpec((ta, 128), lambda i: (i, 0))],
        out_specs=[pl.BlockSpec((N_HEADS, ta, 128), lambda i: (0, i, 0)), pl.BlockSpec((N_HEADS, ta, 128), lambda i: (0, i, 0)),
                   pl.BlockSpec((N_HEADS, nkb, 128, TK), lambda i: (0, i, 0, 0)),
                   pl.BlockSpec((N_HEADS // 2, nkb, 128, TK), lambda i: (0, i, 0, 0))],
        out_shape=[aug, aug, jax.ShapeDtypeStruct((N_HEADS, t_tok // TK, 128, TK), BF16),
                   jax.ShapeDtypeStruct((N_HEADS // 2, t_tok // TK, 128, TK), BF16)],
        compiler_params=_cp(("arbitrary",), VMEM_MID))(qkvg, qkvg, qkvg, f_col)


def _fox_fwd_t(q_aug, k_aug, vt, seq):
    t_tok = q_aug.shape[1]
    nq = seq // TQ
    ratio = TQ // TK

    def body(qa_ref, ka_ref, vt_ref, o_ref, lse_ref):
        i = pl.program_id(1)
        tpos = i * TQ + lax.broadcasted_iota(jnp.int32, (1, TQ), 1)
        eye = _eye(HEAD_DIM, BF16)
        for p in range(N_HEADS // 2):
            outs = []
            for e in range(2):
                h = 2 * p + e
                qa = qa_ref[h]

                def kblock(j, carry, masked):
                    m, l, acc = carry
                    row0 = pl.multiple_of(j * TK, TK)
                    st = _nt(ka_ref[h, pl.ds(row0, TK), :], qa)
                    if masked:
                        spos = j * TK + lax.broadcasted_iota(jnp.int32, (TK, 1), 0)
                        st = jnp.where(spos <= tpos, st, NEG)
                    m_new = jnp.maximum(m, jnp.max(st, axis=0, keepdims=True))
                    alpha = jnp.exp(m - m_new)
                    pe = jnp.exp(st - m_new)
                    l = alpha * l + jnp.sum(pe, axis=0, keepdims=True)
                    acc = alpha * acc + _nn(vt_ref[p, j, e * HEAD_DIM:(e + 1) * HEAD_DIM, :], pe.astype(BF16))
                    return m_new, l, acc

                carry = (jnp.full((1, TQ), NEG, F32), jnp.zeros((1, TQ), F32), jnp.zeros((HEAD_DIM, TQ), F32))
                carry = lax.fori_loop(0, ratio * i, functools.partial(kblock, masked=False), carry)
                for kk in range(ratio):
                    carry = kblock(ratio * i + kk, carry, True)
                m, l, acc = carry
                outs.append(_tn((acc / l).astype(BF16), eye))
                lse_ref[p, e:e + 1, :] = m + jnp.log(l)
            o_ref[:, 128 * p:128 * (p + 1)] = jnp.concatenate(outs, axis=1).astype(BF16)

    return _pallas(
        body, name="fox_fwd", grid=(t_tok // seq, nq),
        in_specs=[pl.BlockSpec((N_HEADS, TQ, 128), lambda b, i: (0, b * nq + i, 0)),
                  pl.BlockSpec((N_HEADS, seq, 128), lambda b, i: (0, b, 0)),
                  pl.BlockSpec((N_HEADS // 2, seq // TK, 128, TK), lambda b, i: (0, b, 0, 0))],
        out_specs=[pl.BlockSpec((TQ, 512), lambda b, i: (b * nq + i, 0)),
                   pl.BlockSpec((N_HEADS // 2, 2, TQ), lambda b, i: (0, 0, b * nq + i))],
        out_shape=[jax.ShapeDtypeStruct((t_tok, 512), BF16), jax.ShapeDtypeStruct((N_HEADS // 2, 2, t_tok), F32)],
        compiler_params=_cp(("arbitrary", "arbitrary"), VMEM_MID))(q_aug, k_aug, vt)


def _fox_bwd_t(q_aug, k_aug, kt, qkvg, du_b, b_out, lse, seq):
    t_tok = qkvg.shape[0]
    nq = seq // TQ
    nkb = seq // TK
    ratio = TQ // TK

    def body(qa_ref, ka_ref, kt_ref, v_ref, do_ref, o_ref, lse_ref, dq_ref, dk_ref, dv_ref, df_ref,
             dqt_s, out_s, row_s, dfk_s):
        ones_b = jnp.ones((8, TQ), BF16)
        ones_f = jnp.ones((8, HEAD_DIM), F32)
        eye = _eye(HEAD_DIM, BF16)
        for e in range(2):
            lo, hi = e * HEAD_DIM, (e + 1) * HEAD_DIM
            for ii in range(nq):
                rows = slice(ii * TQ, (ii + 1) * TQ)
                do = do_ref[rows, :][:, lo:hi].astype(F32)
                ov = o_ref[rows, :][:, lo:hi].astype(F32)
                row_s[ii, 0] = _nt(ones_f, do * ov, HI)
                row_s[ii, 1] = jnp.broadcast_to(lse_ref[e:e + 1, ii * TQ:(ii + 1) * TQ], (8, TQ))
                dqt_s[ii] = jnp.zeros((128, TQ), F32)

            def kblock(j, _):
                krow = pl.multiple_of(j * TK, TK)
                kj = ka_ref[e, pl.ds(krow, TK), :]
                ktj = kt_ref[e, j]
                vj = v_ref[pl.ds(krow, TK), :][:, lo:hi]
                spos = j * TK + lax.broadcasted_iota(jnp.int32, (TK, 1), 0)

                def qblock(i, carry, masked):
                    dk_acc, dv_acc, dfk = carry
                    qrow = pl.multiple_of(i * TQ, TQ)
                    qa = qa_ref[e, pl.ds(qrow, TQ), :]
                    doh = do_ref[pl.ds(qrow, TQ), :][:, lo:hi]
                    pt = jnp.exp(_nt(kj, qa) - row_s[i, 1][0:1, :])
                    if masked:
                        tpos = i * TQ + lax.broadcasted_iota(jnp.int32, (1, TQ), 1)
                        pt = jnp.where(spos <= tpos, pt, 0.0)
                    dst = pt * (_nt(vj, doh) - row_s[i, 0][0:1, :])
                    dst_b = dst.astype(BF16)
                    dv_acc = dv_acc + _nn(pt.astype(BF16), doh)
                    dk_acc = dk_acc + _nn(dst_b, qa)
                    dqt_s[i] += _nn(ktj, dst_b)
                    dfk = dfk + _nt(ones_b, dst_b)
                    return dk_acc, dv_acc, dfk

                i0 = j // ratio
                carry = (jnp.zeros((TK, 128), F32), jnp.zeros((TK, HEAD_DIM), F32), jnp.zeros((8, TK), F32))
                carry = qblock(i0, carry, True)
                dk_acc, dv_acc, dfk = lax.fori_loop(i0 + 1, nq, functools.partial(qblock, masked=False), carry)
                out_s[1, e, pl.ds(krow, TK), :] = dk_acc[:, :HEAD_DIM]
                out_s[2, e, pl.ds(krow, TK), :] = dv_acc
                dfk_s[j] = dfk
                return 0

            lax.fori_loop(0, nkb, kblock, 0)
            for ii in range(nq):
                dqt = dqt_s[ii]
                out_s[0, e, ii * TQ:(ii + 1) * TQ, :] = _tn(dqt[0:HEAD_DIM, :].astype(BF16), eye) * SCALE
                for kk in range(ratio):
                    jj = ii * ratio + kk
                    df_ref[e:e + 1, jj * TK:(jj + 1) * TK] = dqt[67:68, kk * TK:(kk + 1) * TK] - dfk_s[jj][0:1, :]
        for k, ref in enumerate((dq_ref, dk_ref, dv_ref)):
            ref[...] = jnp.concatenate([out_s[k, 0], out_s[k, 1]], axis=1).astype(BF16)

    aug_blk = pl.BlockSpec((2, seq, 128), lambda b, p: (p, b, 0))
    pair_blk = pl.BlockSpec((seq, 128), lambda b, p: (b, p))
    row_blk = pl.BlockSpec((None, 2, seq), lambda b, p: (p, 0, b))
    return _pallas(
        body, name="fox_bwd", grid=(t_tok // seq, N_HEADS // 2),
        in_specs=[aug_blk, aug_blk, pl.BlockSpec((2, nkb, 128, TK), lambda b, p: (p, b, 0, 0)),
                  pl.BlockSpec((seq, 128), lambda b, p: (b, C_BV // 128 + p)), pair_blk, pair_blk, row_blk],
        out_specs=[pair_blk, pair_blk, pair_blk, row_blk],
        out_shape=[jax.ShapeDtypeStruct((t_tok, 512), BF16)] * 3
        + [jax.ShapeDtypeStruct((N_HEADS // 2, 2, t_tok), F32)],
        scratch_shapes=[pltpu.VMEM((nq, 128, TQ), F32), pltpu.VMEM((3, 2, seq, HEAD_DIM), F32),
                        pltpu.VMEM((nq, 2, 8, TQ), F32), pltpu.VMEM((nkb, 8, TK), F32)],
        compiler_params=_cp(("arbitrary", "arbitrary"), VMEM_BIG))(q_aug, k_aug, kt, qkvg, du_b, b_out, lse)


def _fox_fwd(qkvg, f_row, f_col, seq):
    t_tok = qkvg.shape[0]
    nq = seq // TQ

    def body(q_ref, k_ref, v_ref, fr_ref, fc_ref, o_ref, lse_ref, fk_s):
        i = pl.program_id(1)
        for jj in range(nq):
            fk_s[jj] = fr_ref[:, jj * TQ:(jj + 1) * TQ]
        fcol = fc_ref[...]
        tpos = i * TQ + lax.broadcasted_iota(jnp.int32, (TQ, 1), 0)
        lane = lax.broadcasted_iota(jnp.int32, (TQ, 128), 1)
        lse_tile = jnp.zeros((TQ, 128), F32)
        for p in range(N_HEADS // 2):
            q2 = q_ref[:, 128 * p:128 * (p + 1)]
            qs = [q2[:, :HEAD_DIM], q2[:, HEAD_DIM:]]
            fqs = [_col(fcol, 2 * p + e) for e in range(2)]

            def kblock(j, carry):
                row0 = pl.multiple_of(j * TQ, TQ)
                k2 = k_ref[pl.ds(row0, TQ), 128 * p:128 * (p + 1)]
                v2 = v_ref[pl.ds(row0, TQ), 128 * p:128 * (p + 1)]
                fk8 = fk_s[j]
                spos = j * TQ + lax.broadcasted_iota(jnp.int32, (1, TQ), 1)
                keep = spos <= tpos
                new = []
                for e in range(2):
                    m, l, acc = carry[3 * e:3 * e + 3]
                    kh = k2[:, e * HEAD_DIM:(e + 1) * HEAD_DIM]
                    vh = v2[:, e * HEAD_DIM:(e + 1) * HEAD_DIM]
                    s = _nt(qs[e], kh) * SCALE + (fqs[e] - fk8[2 * p + e:2 * p + e + 1, :])
                    s = jnp.where(keep, s, NEG)
                    m_new = jnp.maximum(m, jnp.max(s, axis=1, keepdims=True))
                    alpha = jnp.exp(m - m_new)
                    pe = jnp.exp(s - m_new)
                    l = alpha * l + jnp.sum(pe, axis=1, keepdims=True)
                    acc = alpha * acc + _nn(pe.astype(BF16), vh)
                    new += [m_new, l, acc]
                return tuple(new)

            init = (jnp.full((TQ, 1), NEG, F32), jnp.zeros((TQ, 1), F32), jnp.zeros((TQ, HEAD_DIM), F32)) * 2
            res = lax.fori_loop(0, i + 1, kblock, init)
            outs = []
            for e in range(2):
                m, l, acc = res[3 * e:3 * e + 3]
                outs.append(acc / l)
                lse_tile = jnp.where(lane == 2 * p + e, m + jnp.log(l), lse_tile)
            o_ref[:, 128 * p:128 * (p + 1)] = jnp.concatenate(outs, axis=1).astype(BF16)
        lse_ref[...] = lse_tile

    return _pallas(
        body, name="fox_fwd", grid=(t_tok // seq, nq),
        in_specs=[pl.BlockSpec((TQ, 512), lambda b, i: (b * nq + i, C_BQ // 512)),
                  pl.BlockSpec((seq, 512), lambda b, i: (b, C_BK // 512)),
                  pl.BlockSpec((seq, 512), lambda b, i: (b, C_BV // 512)),
                  pl.BlockSpec((N_HEADS, seq), lambda b, i: (0, b)),
                  pl.BlockSpec((TQ, 128), lambda b, i: (b * nq + i, 0))],
        out_specs=[pl.BlockSpec((TQ, 512), lambda b, i: (b * nq + i, 0)),
                   pl.BlockSpec((TQ, 128), lambda b, i: (b * nq + i, 0))],
        out_shape=[jax.ShapeDtypeStruct((t_tok, 512), BF16), jax.ShapeDtypeStruct((t_tok, 128), F32)],
        scratch_shapes=[pltpu.VMEM((nq, N_HEADS, TQ), F32)],
        compiler_params=_cp(("arbitrary", "arbitrary"), VMEM_MID))(qkvg, qkvg, qkvg, f_row, f_col)


def _fox_bwd(qkvg, du_b, b_out, lse, f_row, f_col, seq):
    t_tok = qkvg.shape[0]
    nq = seq // TQ

    def body(q_ref, k_ref, v_ref, do_ref, o_ref, lse_ref, fr_ref, fc_ref,
             dq_ref, dk_ref, dv_ref, df_ref, dq_s, dk_s, dv_s, col_s, df_s, fk_s):
        p = pl.program_id(1)
        for jj in range(nq):
            fk_s[jj] = fr_ref[:, jj * TQ:(jj + 1) * TQ]
        eye = (lax.broadcasted_iota(jnp.int32, (TQ, TQ), 0) == lax.broadcasted_iota(jnp.int32, (TQ, TQ), 1)).astype(F32)
        for e in range(2):
            h = 2 * p + e
            lo, hi = e * HEAD_DIM, (e + 1) * HEAD_DIM
            for ii in range(nq):
                rows = slice(ii * TQ, (ii + 1) * TQ)
                do = do_ref[rows, :][:, lo:hi].astype(F32)
                ov = o_ref[rows, :][:, lo:hi].astype(F32)
                col_s[0, rows, :] = jnp.sum(do * ov, axis=1, keepdims=True)
                col_s[1, rows, :] = _col(lse_ref[rows, :], h)
                col_s[2, rows, :] = _col(fc_ref[rows, :], h)
                dq_s[rows, :] = jnp.zeros((TQ, HEAD_DIM), F32)
                df_s[ii] = jnp.zeros((8, TQ), F32)
                col_s[3, rows, :] = jnp.zeros((TQ, 1), F32)

            def kblock(j, _):
                krow = pl.multiple_of(j * TQ, TQ)
                kh = k_ref[pl.ds(krow, TQ), :][:, lo:hi]
                vh = v_ref[pl.ds(krow, TQ), :][:, lo:hi]
                fk = _row(fk_s[j], h)
                spos = j * TQ + lax.broadcasted_iota(jnp.int32, (1, TQ), 1)

                def qblock(i, carry):
                    dk_acc, dv_acc, dfk = carry
                    qrow = pl.multiple_of(i * TQ, TQ)
                    qh = q_ref[pl.ds(qrow, TQ), :][:, lo:hi]
                    doh = do_ref[pl.ds(qrow, TQ), :][:, lo:hi]
                    delta = col_s[0, pl.ds(qrow, TQ), :]
                    lse_q = col_s[1, pl.ds(qrow, TQ), :]
                    fq = col_s[2, pl.ds(qrow, TQ), :]
                    tpos = i * TQ + lax.broadcasted_iota(jnp.int32, (TQ, 1), 0)
                    s = _nt(qh, kh) * SCALE + (fq - fk)
                    pr = jnp.where(spos <= tpos, jnp.exp(s - lse_q), 0.0)
                    dp = _nt(doh, vh)
                    ds = pr * (dp - delta)
                    ds_b = ds.astype(BF16)
                    dv_acc = dv_acc + _tn(pr.astype(BF16), doh)
                    dk_acc = dk_acc + _tn(ds_b, qh)
                    dq_s[pl.ds(qrow, TQ), :] += _nn(ds_b, kh)
                    col_s[3, pl.ds(qrow, TQ), :] += jnp.sum(ds, axis=1, keepdims=True)
                    dfk = dfk + jnp.sum(ds, axis=0, keepdims=True)
                    return dk_acc, dv_acc, dfk

                zero = jnp.zeros((TQ, HEAD_DIM), F32)
                dk_acc, dv_acc, dfk = lax.fori_loop(j, nq, qblock, (zero, zero, jnp.zeros((1, TQ), F32)))
                dk_s[e, pl.ds(krow, TQ), :] = dk_acc * SCALE
                dv_s[e, pl.ds(krow, TQ), :] = dv_acc
                df_s[j] -= jnp.broadcast_to(dfk, (8, TQ))
                return 0

            lax.fori_loop(0, nq, kblock, 0)
            dq_s2 = dq_s[...] * SCALE
            dk_s[2 + e] = dq_s2
            for ii in range(nq):
                dfq = jnp.broadcast_to(col_s[3, ii * TQ:(ii + 1) * TQ, :], (TQ, 128))
                df_ref[e:e + 1, ii * TQ:(ii + 1) * TQ] = _tn(dfq, eye, HI)[0:1, :] + df_s[ii][0:1, :]
        dq_ref[...] = jnp.concatenate([dk_s[2], dk_s[3]], axis=1).astype(BF16)
        dk_ref[...] = jnp.concatenate([dk_s[0], dk_s[1]], axis=1).astype(BF16)
        dv_ref[...] = jnp.concatenate([dv_s[0], dv_s[1]], axis=1).astype(BF16)

    blk = lambda off: pl.BlockSpec((seq, 128), lambda b, p: (b, off // 128 + p))
    out_blk = pl.BlockSpec((seq, 128), lambda b, p: (b, p))
    return _pallas(
        body, name="fox_bwd", grid=(t_tok // seq, N_HEADS // 2),
        in_specs=[blk(C_BQ), blk(C_BK), blk(C_BV), out_blk, out_blk,
                  pl.BlockSpec((seq, 128), lambda b, p: (b, 0)),
                  pl.BlockSpec((N_HEADS, seq), lambda b, p: (0, b)),
                  pl.BlockSpec((seq, 128), lambda b, p: (b, 0))],
        out_specs=[out_blk, out_blk, out_blk, pl.BlockSpec((None, 2, seq), lambda b, p: (p, 0, b))],
        out_shape=[jax.ShapeDtypeStruct((t_tok, 512), BF16)] * 3
        + [jax.ShapeDtypeStruct((N_HEADS // 2, 2, t_tok), F32)],
        scratch_shapes=[pltpu.VMEM((seq, HEAD_DIM), F32), pltpu.VMEM((4, seq, HEAD_DIM), F32),
                        pltpu.VMEM((2, seq, HEAD_DIM), F32), pltpu.VMEM((4, seq, 1), F32),
                        pltpu.VMEM((nq, 8, TQ), F32), pltpu.VMEM((nq, N_HEADS, TQ), F32)],
        compiler_params=_cp(("arbitrary", "arbitrary"), VMEM_BIG))(qkvg, qkvg, qkvg, du_b, b_out, lse, f_row, f_col)


def _swa_scores(q_ref, kp, kc, bias_ref, h, mask_p, mask_c):
    q2 = q_ref[:, 128 * (h // 2):128 * (h // 2 + 1)]
    qh = q2[:, (h % 2) * HEAD_DIM:(h % 2 + 1) * HEAD_DIM]
    hk = h // KV_GROUP
    kph = kp[:, hk * HEAD_DIM:(hk + 1) * HEAD_DIM]
    kch = kc[:, hk * HEAD_DIM:(hk + 1) * HEAD_DIM]
    bias = bias_ref[h]
    sp = jnp.where(mask_p, _nt(qh, kph) * SCALE + bias[:, :BLOCK], NEG)
    sc = jnp.where(mask_c, _nt(qh, kch) * SCALE + bias[:, BLOCK:], NEG)
    return qh, kph, kch, sp, sc


def _swa_masks(n):
    ti = lax.broadcasted_iota(jnp.int32, (BLOCK, BLOCK), 0)
    sj = lax.broadcasted_iota(jnp.int32, (BLOCK, BLOCK), 1)
    return jnp.logical_and(sj > ti, n > 0), sj <= ti


def _swa_fwd(qkvg, bias, sinks, seq):
    t_tok = qkvg.shape[0]
    nb = seq // BLOCK

    def body(sink_ref, q_ref, k_ref, v_ref, bias_ref, o_ref, lse_ref):
        n = pl.program_id(1)
        prev = pl.multiple_of(jnp.maximum(n - 1, 0) * BLOCK, BLOCK)
        cur = pl.multiple_of(n * BLOCK, BLOCK)
        kp, kc = k_ref[pl.ds(prev, BLOCK), :], k_ref[pl.ds(cur, BLOCK), :]
        vp, vc = v_ref[pl.ds(prev, BLOCK), :], v_ref[pl.ds(cur, BLOCK), :]
        mask_p, mask_c = _swa_masks(n)
        lane = lax.broadcasted_iota(jnp.int32, (BLOCK, 128), 1)
        lse_tile = jnp.zeros((BLOCK, 128), F32)
        outs = []
        for h in range(N_HEADS):
            hk = h // KV_GROUP
            _, _, _, sp, sc = _swa_scores(q_ref, kp, kc, bias_ref, h, mask_p, mask_c)
            sink = sink_ref[h]
            m = jnp.maximum(jnp.maximum(jnp.max(sp, axis=1, keepdims=True), jnp.max(sc, axis=1, keepdims=True)), sink)
            pp, pc = jnp.exp(sp - m), jnp.exp(sc - m)
            den = jnp.sum(pp, axis=1, keepdims=True) + jnp.sum(pc, axis=1, keepdims=True) + jnp.exp(sink - m)
            acc = (_nn(pp.astype(BF16), vp[:, hk * HEAD_DIM:(hk + 1) * HEAD_DIM])
                   + _nn(pc.astype(BF16), vc[:, hk * HEAD_DIM:(hk + 1) * HEAD_DIM]))
            outs.append(acc / den)
            lse_tile = jnp.where(lane == h, m + jnp.log(den), lse_tile)
            if h % 2 == 1:
                o_ref[:, 128 * (h // 2):128 * (h // 2 + 1)] = jnp.concatenate(outs, axis=1).astype(BF16)
                outs = []
        lse_ref[...] = lse_tile

    return _pallas(
        body, name="swa_fwd", grid=(t_tok // seq, nb),
        in_specs=[pl.BlockSpec(memory_space=pltpu.SMEM),
                  pl.BlockSpec((BLOCK, 512), lambda b, n: (b * nb + n, C_AQ // 512)),
                  pl.BlockSpec((seq, 128), lambda b, n: (b, C_AK // 128)),
                  pl.BlockSpec((seq, 128), lambda b, n: (b, C_AV // 128)),
                  pl.BlockSpec((N_HEADS, BLOCK, 2 * BLOCK), lambda b, n: (0, 0, 0))],
        out_specs=[pl.BlockSpec((BLOCK, 512), lambda b, n: (b * nb + n, 0)),
                   pl.BlockSpec((BLOCK, 128), lambda b, n: (b * nb + n, 0))],
        out_shape=[jax.ShapeDtypeStruct((t_tok, 512), BF16), jax.ShapeDtypeStruct((t_tok, 128), F32)],
        compiler_params=_cp(("arbitrary", "arbitrary"), VMEM_MID))(sinks, qkvg, qkvg, qkvg, bias)


def _swa_bwd(qkvg, du_a, a_out, lse, bias, sinks, seq):
    t_tok = qkvg.shape[0]
    nb = seq // BLOCK

    def body(sink_ref, q_ref, k_ref, v_ref, do_ref, o_ref, lse_ref, bias_ref,
             dq_ref, dkv_ref, dbias_ref, dsink_ref, kv_s):
        b, n = pl.program_id(0), pl.program_id(1)

        @pl.when(jnp.logical_and(b == 0, n == 0))
        def _():
            dbias_ref[...] = jnp.zeros_like(dbias_ref)
            dsink_ref[...] = jnp.zeros_like(dsink_ref)

        @pl.when(n == 0)
        def _():
            kv_s[...] = jnp.zeros_like(kv_s)

        prev = pl.multiple_of(jnp.maximum(n - 1, 0) * BLOCK, BLOCK)
        cur = pl.multiple_of(n * BLOCK, BLOCK)
        kp, kc = k_ref[pl.ds(prev, BLOCK), :], k_ref[pl.ds(cur, BLOCK), :]
        vp, vc = v_ref[pl.ds(prev, BLOCK), :], v_ref[pl.ds(cur, BLOCK), :]
        mask_p, mask_c = _swa_masks(n)
        lse_tile = lse_ref[...]
        dqs = []
        acc = [[jnp.zeros((BLOCK, HEAD_DIM), F32) for _ in range(4)] for _ in range(2)]
        for h in range(N_HEADS):
            hk = h // KV_GROUP
            qh, kph, kch, sp, sc = _swa_scores(q_ref, kp, kc, bias_ref, h, mask_p, mask_c)
            lo, hi = (h % 2) * HEAD_DIM, (h % 2 + 1) * HEAD_DIM
            do2 = do_ref[:, 128 * (h // 2):128 * (h // 2 + 1)]
            o2 = o_ref[:, 128 * (h // 2):128 * (h // 2 + 1)]
            doh = do2[:, lo:hi]
            delta = jnp.sum(doh.astype(F32) * o2[:, lo:hi].astype(F32), axis=1, keepdims=True)
            lse_h = _col(lse_tile, h)
            pp, pc = jnp.exp(sp - lse_h), jnp.exp(sc - lse_h)
            vph = vp[:, hk * HEAD_DIM:(hk + 1) * HEAD_DIM]
            vch = vc[:, hk * HEAD_DIM:(hk + 1) * HEAD_DIM]
            dsp = pp * (_nt(doh, vph) - delta)
            dsc = pc * (_nt(doh, vch) - delta)
            dbias_ref[h, :, :BLOCK] += dsp
            dbias_ref[h, :, BLOCK:] += dsc
            psink = jnp.exp(sink_ref[h] - lse_h)
            dsink_ref[h:h + 1, :] += jnp.broadcast_to(jnp.sum(-psink * delta, axis=0, keepdims=True), (1, 128))
            dsp_b, dsc_b = dsp.astype(BF16), dsc.astype(BF16)
            dqs.append((_nn(dsp_b, kph) + _nn(dsc_b, kch)) * SCALE)
            acc[hk][0] = acc[hk][0] + _tn(dsp_b, qh)
            acc[hk][1] = acc[hk][1] + _tn(dsc_b, qh)
            acc[hk][2] = acc[hk][2] + _tn(pp.astype(BF16), doh)
            acc[hk][3] = acc[hk][3] + _tn(pc.astype(BF16), doh)
            if h % 2 == 1:
                dq_ref[:, 128 * (h // 2):128 * (h // 2 + 1)] = jnp.concatenate(dqs, axis=1).astype(BF16)
                dqs = []
        upd_p = jnp.concatenate([acc[0][0] * SCALE, acc[1][0] * SCALE, acc[0][2], acc[1][2]], axis=1)
        upd_c = jnp.concatenate([acc[0][1] * SCALE, acc[1][1] * SCALE, acc[0][3], acc[1][3]], axis=1)
        kv_s[pl.ds(prev, BLOCK), :] += upd_p
        kv_s[pl.ds(cur, BLOCK), :] += upd_c

        @pl.when(n == nb - 1)
        def _():
            dkv_ref[...] = kv_s[...].astype(BF16)

    return _pallas(
        body, name="swa_bwd", grid=(t_tok // seq, nb),
        in_specs=[pl.BlockSpec(memory_space=pltpu.SMEM),
                  pl.BlockSpec((BLOCK, 512), lambda b, n: (b * nb + n, C_AQ // 512)),
                  pl.BlockSpec((seq, 128), lambda b, n: (b, C_AK // 128)),
                  pl.BlockSpec((seq, 128), lambda b, n: (b, C_AV // 128)),
                  pl.BlockSpec((BLOCK, 512), lambda b, n: (b * nb + n, 0)),
                  pl.BlockSpec((BLOCK, 512), lambda b, n: (b * nb + n, 0)),
                  pl.BlockSpec((BLOCK, 128), lambda b, n: (b * nb + n, 0)),
                  pl.BlockSpec((N_HEADS, BLOCK, 2 * BLOCK), lambda b, n: (0, 0, 0))],
        out_specs=[pl.BlockSpec((BLOCK, 512), lambda b, n: (b * nb + n, 0)),
                   pl.BlockSpec((seq, 256), lambda b, n: (b, 0)),
                   pl.BlockSpec((N_HEADS, BLOCK, 2 * BLOCK), lambda b, n: (0, 0, 0)),
                   pl.BlockSpec((N_HEADS, 128), lambda b, n: (0, 0))],
        out_shape=[jax.ShapeDtypeStruct((t_tok, 512), BF16), jax.ShapeDtypeStruct((t_tok, 256), BF16),
                   jax.ShapeDtypeStruct((N_HEADS, BLOCK, 2 * BLOCK), F32), jax.ShapeDtypeStruct((N_HEADS, 128), F32)],
        scratch_shapes=[pltpu.VMEM((seq, 256), F32)],
        compiler_params=_cp(("arbitrary", "arbitrary"), VMEM_MID))(sinks, qkvg, qkvg, qkvg, du_a, a_out, lse, bias)


def _out_proj(name, u_parts, gate_arr, gate_blk, w_out, x, gmod, seq):
    t_tok = x.shape[0]
    nu = len(u_parts)

    def body(*refs):
        u_refs = refs[:nu]
        g_ref, w_ref, x_ref, gm_ref, yg_ref, y_ref, xn_ref = refs[nu:]
        u = jnp.concatenate([r[...].astype(F32) for r in u_refs], axis=1) if nu > 1 else u_refs[0][...].astype(F32)
        yg = (u * _silu(g_ref[...].astype(F32))).astype(BF16)
        yg_ref[...] = yg
        y = _nn(yg, w_ref[...])
        y_ref[...] = y.astype(BF16)
        xn_ref[...] = x_ref[...] + gm_ref[...] * y

    row = lambda w: pl.BlockSpec((TM, w), lambda i: (i, 0))
    in_specs = [row(u.shape[1]) for u in u_parts]
    in_specs += [pl.BlockSpec((TM, D_MODEL), lambda i: (i, gate_blk)),
                 pl.BlockSpec((D_MODEL, D_MODEL), lambda i: (0, 0)), row(D_MODEL),
                 pl.BlockSpec((None, 1, D_MODEL), lambda i: (i * TM // seq, 0, 0))]
    return _pallas(
        body, name=name, grid=(t_tok // TM,), in_specs=in_specs,
        out_specs=[row(D_MODEL)] * 3,
        out_shape=[jax.ShapeDtypeStruct((t_tok, D_MODEL), BF16)] * 2 + [jax.ShapeDtypeStruct((t_tok, D_MODEL), F32)],
        compiler_params=_cp(("arbitrary",), VMEM_MID))(*u_parts, gate_arr, w_out, x, gmod)


def _out_proj_bwd(name, dxn, gmod, y, w_out, seq, attn=None):
    t_tok = dxn.shape[0]
    tiles_per_seq = seq // TM

    def body(*refs):
        if attn is None:
            dxn_ref, gm_ref, y_ref, w_ref, dy_ref, dgm_ref, dyg_ref = refs
        else:
            dxn_ref, gm_ref, y_ref, w_ref, a_ref, b_ref, g_ref, dy_ref, dgm_ref, dua_ref, dub_ref, dg_ref = refs
        i = pl.program_id(0)
        dxv = dxn_ref[...]
        dy = (dxv * gm_ref[...]).astype(BF16)
        dy_ref[...] = dy

        @pl.when(i % tiles_per_seq == 0)
        def _():
            dgm_ref[...] = jnp.zeros_like(dgm_ref)

        dgm_ref[...] += jnp.sum(dxv * y_ref[...].astype(F32), axis=0, keepdims=True)
        dyg = _nt(dy, w_ref[...])
        if attn is None:
            dyg_ref[...] = dyg
        else:
            gt = g_ref[...].astype(F32)
            du = dyg * _silu(gt)
            dua_ref[...] = du[:, :512].astype(BF16)
            dub_ref[...] = du[:, 512:].astype(BF16)
            u = jnp.concatenate([a_ref[...].astype(F32), b_ref[...].astype(F32)], axis=1)
            dg_ref[...] = (dyg * u * _dsilu(gt)).astype(BF16)

    row = lambda w: pl.BlockSpec((TM, w), lambda i: (i, 0))
    mod_spec = pl.BlockSpec((None, 1, D_MODEL), lambda i: (i * TM // seq, 0, 0))
    in_specs = [row(D_MODEL), mod_spec, row(D_MODEL), pl.BlockSpec((D_MODEL, D_MODEL), lambda i: (0, 0))]
    out_specs = [row(D_MODEL), mod_spec]
    out_shape = [jax.ShapeDtypeStruct((t_tok, D_MODEL), BF16), jax.ShapeDtypeStruct(gmod.shape, F32)]
    args = [dxn, gmod, y, w_out]
    if attn is None:
        out_specs.append(row(D_MODEL))
        out_shape.append(jax.ShapeDtypeStruct((t_tok, D_MODEL), F32))
    else:
        in_specs += [row(512), row(512), pl.BlockSpec((TM, D_MODEL), lambda i: (i, C_GATE // D_MODEL))]
        out_specs += [row(512), row(512), row(D_MODEL)]
        out_shape += [jax.ShapeDtypeStruct((t_tok, 512), BF16)] * 2 + [jax.ShapeDtypeStruct((t_tok, D_MODEL), BF16)]
        args += list(attn)
    return _pallas(body, name=name, grid=(t_tok // TM,), in_specs=in_specs, out_specs=out_specs,
                   out_shape=out_shape, compiler_params=_cp(("arbitrary",), VMEM_MID))(*args)


def _norm_bwd(name, parts, w, x, g, scale, dxn, seq, rows_part=None):
    t_tok = x.shape[0]
    npart = len(parts)
    w3d = w.ndim == 3
    tiles_per_seq = seq // TM
    nrow_in = 0 if rows_part is None else 2

    def body(*refs):
        p_refs = refs[:npart]
        w_ref, x_ref, g_ref, sc_ref, dxn_ref = refs[npart:npart + 5]
        dx_ref, dss_ref, dg_ref = refs[npart + 5 + nrow_in:]
        i = pl.program_id(0)
        dh = jnp.zeros((TM, D_MODEL), F32)
        if rows_part is not None:
            r_ref, wr_ref = refs[npart + 5:npart + 7]
            dh = dh + _tn(r_ref[...].astype(BF16), wr_ref[...])
        for (arr, off), p_ref in zip(parts, p_refs):
            width = arr.shape[1]
            for j in range(width // 256):
                pj = p_ref[:, j * 256:(j + 1) * 256]
                c0 = off + j * 256
                wj = w_ref[c0 // 256] if w3d else w_ref[:, c0:c0 + 256]
                dh = dh + _nt(pj, wj)
        xv = x_ref[...]
        rstd = lax.rsqrt(jnp.mean(xv * xv, axis=-1, keepdims=True) + EPS)
        xhat = xv * rstd
        gv = g_ref[...]
        nrm = xhat * gv

        @pl.when(i % tiles_per_seq == 0)
        def _():
            dss_ref[...] = jnp.zeros_like(dss_ref)

        @pl.when(i == 0)
        def _():
            dg_ref[...] = jnp.zeros_like(dg_ref)

        dss_ref[0:1, :] += jnp.sum(dh, axis=0, keepdims=True)
        dss_ref[1:2, :] += jnp.sum(dh * nrm, axis=0, keepdims=True)
        dn = dh * (1.0 + sc_ref[...])
        dg_ref[0:1, :] += jnp.sum(dn * xhat, axis=0, keepdims=True)
        dxhat = dn * gv
        dx_ref[...] = rstd * (dxhat - xhat * jnp.mean(dxhat * xhat, axis=-1, keepdims=True)) + dxn_ref[...]

    row = lambda wd: pl.BlockSpec((TM, wd), lambda i: (i, 0))
    w_spec = (pl.BlockSpec(w.shape, lambda i: (0, 0, 0)) if w3d else pl.BlockSpec(w.shape, lambda i: (0, 0)))
    in_specs = [row(a.shape[1]) for a, _ in parts]
    in_specs += [w_spec, row(D_MODEL), pl.BlockSpec((1, D_MODEL), lambda i: (0, 0)),
                 pl.BlockSpec((None, 1, D_MODEL), lambda i: (i * TM // seq, 0, 0)), row(D_MODEL)]
    args = [a for a, _ in parts] + [w, x, g, scale, dxn]
    if rows_part is not None:
        in_specs += [pl.BlockSpec((8, TM), lambda i: (0, i)), pl.BlockSpec((8, D_MODEL), lambda i: (0, 0))]
        args += list(rows_part)
    nseq = t_tok // seq
    return _pallas(
        body, name=name, grid=(t_tok // TM,), in_specs=in_specs,
        out_specs=[row(D_MODEL), pl.BlockSpec((None, 8, D_MODEL), lambda i: (i * TM // seq, 0, 0)),
                   pl.BlockSpec((8, D_MODEL), lambda i: (0, 0))],
        out_shape=[jax.ShapeDtypeStruct((t_tok, D_MODEL), F32), jax.ShapeDtypeStruct((nseq, 8, D_MODEL), F32),
                   jax.ShapeDtypeStruct((8, D_MODEL), F32)],
        compiler_params=_cp(("arbitrary",), VMEM_BIG))(*args)


def _dw(name, a, parts, blocked=None):
    t_tok, ka = a.shape
    tt = 512
    npart = len(parts)
    nt = t_tok // tt

    def body(*refs):
        a_ref = refs[0]
        p_refs = refs[1:1 + npart]
        o_refs = refs[1 + npart:1 + 2 * npart]
        acc_refs = refs[1 + 2 * npart:]
        t = pl.program_id(0)
        av = a_ref[...]
        for p_ref, acc in zip(p_refs, acc_refs):
            upd = _tn(av, p_ref[...])

            @pl.when(t == 0)
            def _():
                acc[...] = upd

            @pl.when(t > 0)
            def _():
                acc[...] += upd

        @pl.when(t == nt - 1)
        def _():
            for o_ref, acc in zip(o_refs, acc_refs):
                if blocked is None:
                    o_ref[...] = acc[...].astype(BF16)
                else:
                    for j in range(o_ref.shape[0]):
                        o_ref[j] = acc[:, j * blocked:(j + 1) * blocked].astype(BF16)

    in_specs = [pl.BlockSpec((tt, ka), lambda t: (t, 0))]
    in_specs += [pl.BlockSpec((tt, p.shape[1]), lambda t: (t, 0)) for p in parts]
    if blocked is None:
        out_shape = [jax.ShapeDtypeStruct((ka, p.shape[1]), BF16) for p in parts]
        out_specs = [pl.BlockSpec((ka, p.shape[1]), lambda t: (0, 0)) for p in parts]
    else:
        out_shape = [jax.ShapeDtypeStruct((p.shape[1] // blocked, ka, blocked), BF16) for p in parts]
        out_specs = [pl.BlockSpec((p.shape[1] // blocked, ka, blocked), lambda t: (0, 0, 0)) for p in parts]
    return _pallas(body, name=name, grid=(nt,), in_specs=in_specs, out_specs=out_specs, out_shape=out_shape,
                   scratch_shapes=[pltpu.VMEM((ka, p.shape[1]), F32) for p in parts],
                   compiler_params=_cp(("arbitrary",), VMEM_BIG))(a, *parts)


def _dw_rows(name, rows_t, h):
    t_tok = h.shape[0]
    tt = 512

    def body(r_ref, h_ref, o_ref):
        @pl.when(pl.program_id(0) == 0)
        def _():
            o_ref[...] = jnp.zeros_like(o_ref)

        o_ref[...] += _nn(r_ref[...].astype(BF16), h_ref[...])

    return _pallas(body, name=name, grid=(t_tok // tt,),
                   in_specs=[pl.BlockSpec((8, tt), lambda t: (0, t)), pl.BlockSpec((tt, D_MODEL), lambda t: (t, 0))],
                   out_specs=pl.BlockSpec((8, D_MODEL), lambda t: (0, 0)),
                   out_shape=jax.ShapeDtypeStruct((8, D_MODEL), F32),
                   compiler_params=_cp(("arbitrary",), VMEM_MID))(rows_t, h)


def _lru_gates(xc, blk, wa_ref, wx_ref, ba_ref, bx_ref, sp):
    cols = slice(blk * LRU_BLOCK_W, (blk + 1) * LRU_BLOCK_W)
    xb = xc[:, cols].astype(BF16)
    r = _sigmoid(_nn(xb, wa_ref[blk].astype(BF16)) + ba_ref[:, cols])
    ig = _sigmoid(_nn(xb, wx_ref[blk].astype(BF16)) + bx_ref[:, cols])
    log_a = -LRU_C * r * sp[:, cols]
    a = jnp.exp(log_a)
    mult = jnp.sqrt(_neg_expm1(2.0 * log_a))
    return xb, r, ig, a, mult


def _softplus_neg(lam):
    return jnp.maximum(-lam, 0.0) + jnp.log(1.0 + jnp.exp(-jnp.abs(lam)))


def _conv_taps(xe_ref, cw_ref, cb_ref):
    xc = cb_ref[...] + xe_ref[8:8 + TC, :] * cw_ref[3:4, :]
    for k in range(1, 4):
        xc = xc + xe_ref[8 - k:8 - k + TC, :] * cw_ref[3 - k:4 - k, :]
    return xc


def _lru_fwd(proj, cw, cb, w_a, b_a, w_x, b_x, lam, seq):
    t_tok = proj.shape[0]
    nc = seq // TC

    def body(x_ref, cw_ref, cb_ref, wa_ref, ba_ref, wx_ref, bx_ref, lam_ref, hs_ref, xe_s, a_s, u_s, h_s):
        c = pl.program_id(1)

        @pl.when(c == 0)
        def _():
            xe_s[0:8, :] = jnp.zeros((8, D_MODEL), F32)
            h_s[...] = jnp.zeros_like(h_s)

        xe_s[8:8 + TC, :] = x_ref[...]
        xc = _conv_taps(xe_s, cw_ref, cb_ref)
        sp = _softplus_neg(lam_ref[...])
        for blk in range(LRU_BLOCKS):
            cols = slice(blk * LRU_BLOCK_W, (blk + 1) * LRU_BLOCK_W)
            _, _, ig, a, mult = _lru_gates(xc, blk, wa_ref, wx_ref, ba_ref, bx_ref, sp)
            a_s[:, cols] = a
            u_s[:, cols] = mult * ig * xc[:, cols]

        def step(t, h):
            h = a_s[pl.ds(t, 1), :] * h + u_s[pl.ds(t, 1), :]
            hs_ref[pl.ds(t, 1), :] = h
            return h

        h_s[0:1, :] = lax.fori_loop(0, TC, step, h_s[0:1, :], unroll=8)
        xe_s[0:8, :] = xe_s[TC:TC + 8, :]

    full = lambda shape: pl.BlockSpec(shape, lambda b, c: (0,) * len(shape))
    return _pallas(
        body, name="lru_fwd", grid=(t_tok // seq, nc),
        in_specs=[pl.BlockSpec((TC, D_MODEL), lambda b, c: (b * nc + c, 0)), full((4, D_MODEL)), full((1, D_MODEL)),
                  full((LRU_BLOCKS, LRU_BLOCK_W, LRU_BLOCK_W)), full((1, D_MODEL)),
                  full((LRU_BLOCKS, LRU_BLOCK_W, LRU_BLOCK_W)), full((1, D_MODEL)), full((1, D_MODEL))],
        out_specs=pl.BlockSpec((TC, D_MODEL), lambda b, c: (b * nc + c, 0)),
        out_shape=jax.ShapeDtypeStruct((t_tok, D_MODEL), F32),
        scratch_shapes=[pltpu.VMEM((TC + 8, D_MODEL), F32), pltpu.VMEM((TC, D_MODEL), F32),
                        pltpu.VMEM((TC, D_MODEL), F32), pltpu.VMEM((8, D_MODEL), F32)],
        compiler_params=_cp(("arbitrary", "arbitrary"), VMEM_MID))(proj, cw, cb, w_a, b_a, w_x, b_x, lam)


def _lru_bwd(proj, hs, dyh, cw, cb, w_a, b_a, w_x, b_x, lam, seq):
    t_tok = proj.shape[0]
    nc = seq // TC

    def body(x_ref, xh_ref, g_ref, hs_ref, hh_ref, dy_ref, cw_ref, cb_ref, wa_ref, ba_ref, wx_ref, bx_ref, lam_ref,
             dp_ref, dcw_ref, dvec_ref, dwa_ref, dwx_ref,
             xe_s, he_s, de_s, a_s, r_s, i_s, m_s, dh_s, carry_s):
        b, cr = pl.program_id(0), pl.program_id(1)
        c = nc - 1 - cr

        @pl.when(jnp.logical_and(b == 0, cr == 0))
        def _():
            dcw_ref[...] = jnp.zeros_like(dcw_ref)
            dvec_ref[...] = jnp.zeros_like(dvec_ref)
            dwa_ref[...] = jnp.zeros_like(dwa_ref)
            dwx_ref[...] = jnp.zeros_like(dwx_ref)

        @pl.when(cr == 0)
        def _():
            carry_s[...] = jnp.zeros_like(carry_s)
            de_s[TC:TC + 8, :] = jnp.zeros((8, D_MODEL), F32)

        first = c == 0
        xe_s[0:8, :] = jnp.where(first, 0.0, xh_ref[...])
        xe_s[8:8 + TC, :] = x_ref[...]
        he_s[0:8, :] = jnp.where(first, 0.0, hh_ref[...])
        he_s[8:8 + TC, :] = hs_ref[...]
        xc = _conv_taps(xe_s, cw_ref, cb_ref)
        lam_v = lam_ref[...]
        sp = _softplus_neg(lam_v)
        for blk in range(LRU_BLOCKS):
            cols = slice(blk * LRU_BLOCK_W, (blk + 1) * LRU_BLOCK_W)
            _, r, ig, a, mult = _lru_gates(xc, blk, wa_ref, wx_ref, ba_ref, bx_ref, sp)
            a_s[:, cols], r_s[:, cols], i_s[:, cols], m_s[:, cols] = a, r, ig, mult

        gt = g_ref[...]
        dyh = dy_ref[...]
        dh_s[...] = dyh * _silu(gt)
        dp_ref[:, D_MODEL:] = (dyh * hs_ref[...] * _dsilu(gt)).astype(BF16)

        def step(k, carry):
            t = TC - 1 - k
            dh = dh_s[pl.ds(t, 1), :] + carry
            dh_s[pl.ds(t, 1), :] = dh
            return a_s[pl.ds(t, 1), :] * dh

        carry_s[0:1, :] = lax.fori_loop(0, TC, step, carry_s[0:1, :], unroll=8)

        hprev = he_s[7:7 + TC, :]
        for blk in range(LRU_BLOCKS):
            cols = slice(blk * LRU_BLOCK_W, (blk + 1) * LRU_BLOCK_W)
            xcb = xc[:, cols]
            a, r, ig, mult, dh = a_s[:, cols], r_s[:, cols], i_s[:, cols], m_s[:, cols], dh_s[:, cols]
            spb = sp[:, cols]
            dmult = dh * ig * xcb
            di = dh * mult * xcb
            dxc = dh * mult * ig
            dla = dh * hprev[:, cols] * a - dmult * (a * a) / jnp.maximum(mult, 1e-20)
            dr = dla * (-LRU_C * spb)
            dsp = jnp.sum(dla * (-LRU_C * r), axis=0, keepdims=True)
            dga = dr * r * (1.0 - r)
            dgx = di * ig * (1.0 - ig)
            dga_b, dgx_b = dga.astype(BF16), dgx.astype(BF16)
            xb = xcb.astype(BF16)
            dxc = dxc + _nt(dga_b, wa_ref[blk].astype(BF16)) + _nt(dgx_b, wx_ref[blk].astype(BF16))
            dwa_ref[blk] += _tn(xb, dga_b)
            dwx_ref[blk] += _tn(xb, dgx_b)
            dvec_ref[1:2, cols] += jnp.sum(dga, axis=0, keepdims=True)
            dvec_ref[2:3, cols] += jnp.sum(dgx, axis=0, keepdims=True)
            dvec_ref[3:4, cols] += dsp * (-1.0 / (1.0 + jnp.exp(lam_v[:, cols])))
            de_s[0:TC, cols] = dxc

        dxc = de_s[0:TC, :]
        dvec_ref[0:1, :] += jnp.sum(dxc, axis=0, keepdims=True)
        dxr = dxc * cw_ref[3:4, :]
        dcw_ref[3:4, :] += jnp.sum(dxc * xe_s[8:8 + TC, :], axis=0, keepdims=True)
        for k in range(1, 4):
            dxr = dxr + de_s[k:k + TC, :] * cw_ref[3 - k:4 - k, :]
            dcw_ref[3 - k:4 - k, :] += jnp.sum(dxc * xe_s[8 - k:8 - k + TC, :], axis=0, keepdims=True)
        dp_ref[:, :D_MODEL] = dxr.astype(BF16)
        de_s[TC:TC + 8, :] = de_s[0:8, :]

    chunk = lambda col: pl.BlockSpec((TC, D_MODEL), lambda b, cr: (b * nc + nc - 1 - cr, col))
    halo = lambda col: pl.BlockSpec(
        (8, D_MODEL), lambda b, cr: (jnp.maximum((b * nc + nc - 1 - cr) * (TC // 8) - 1, 0), col))
    full = lambda shape: pl.BlockSpec(shape, lambda b, cr: (0,) * len(shape))
    wblk = (LRU_BLOCKS, LRU_BLOCK_W, LRU_BLOCK_W)
    return _pallas(
        body, name="lru_bwd", grid=(t_tok // seq, nc),
        in_specs=[chunk(0), halo(0), chunk(1), chunk(0), halo(0), chunk(0),
                  full((4, D_MODEL)), full((1, D_MODEL)), full(wblk), full((1, D_MODEL)), full(wblk),
                  full((1, D_MODEL)), full((1, D_MODEL))],
        out_specs=[pl.BlockSpec((TC, 2 * D_MODEL), lambda b, cr: (b * nc + nc - 1 - cr, 0)),
                   full((8, D_MODEL)), full((8, D_MODEL)), full(wblk), full(wblk)],
        out_shape=[jax.ShapeDtypeStruct((t_tok, 2 * D_MODEL), BF16), jax.ShapeDtypeStruct((8, D_MODEL), F32),
                   jax.ShapeDtypeStruct((8, D_MODEL), F32), jax.ShapeDtypeStruct(wblk, F32),
                   jax.ShapeDtypeStruct(wblk, F32)],
        scratch_shapes=[pltpu.VMEM((TC + 8, D_MODEL), F32), pltpu.VMEM((TC + 8, D_MODEL), F32),
                        pltpu.VMEM((TC + 8, D_MODEL), F32)]
        + [pltpu.VMEM((TC, D_MODEL), F32)] * 5 + [pltpu.VMEM((8, D_MODEL), F32)],
        compiler_params=_cp(("arbitrary", "arbitrary"), VMEM_BIG),
    )(proj, proj, proj, hs, hs, dyh, cw, cb, w_a, b_a, w_x, b_x, lam)


def _final_loss(x, g, target):
    t_tok = x.shape[0]

    def body(x_ref, g_ref, t_ref, dx_ref, loss_ref, dg_ref):
        @pl.when(pl.program_id(0) == 0)
        def _():
            loss_ref[...] = jnp.zeros_like(loss_ref)
            dg_ref[...] = jnp.zeros_like(dg_ref)

        xv = x_ref[...]
        gv = g_ref[...]
        rstd = lax.rsqrt(jnp.mean(xv * xv, axis=-1, keepdims=True) + EPS)
        xhat = xv * rstd
        err = xhat * gv - t_ref[...]
        loss_ref[0:1, :] += jnp.sum(err * err, axis=0, keepdims=True) * (0.5 / D_MODEL)
        dout = err * (1.0 / D_MODEL)
        dg_ref[0:1, :] += jnp.sum(dout * xhat, axis=0, keepdims=True)
        dxhat = dout * gv
        dx_ref[...] = rstd * (dxhat - xhat * jnp.mean(dxhat * xhat, axis=-1, keepdims=True))

    row = pl.BlockSpec((TM, D_MODEL), lambda i: (i, 0))
    acc = pl.BlockSpec((8, D_MODEL), lambda i: (0, 0))
    return _pallas(body, name="final_loss", grid=(t_tok // TM,),
                   in_specs=[row, pl.BlockSpec((1, D_MODEL), lambda i: (0, 0)), row],
                   out_specs=[row, acc, acc],
                   out_shape=[jax.ShapeDtypeStruct((t_tok, D_MODEL), F32)] + [jax.ShapeDtypeStruct((8, D_MODEL), F32)] * 2,
                   compiler_params=_cp(("arbitrary",), VMEM_MID))(x, g, target)


def _adam_math(w, g, m, v):
    m_new = ADAM_B1 * m + (1.0 - ADAM_B1) * g
    v_new = ADAM_B2 * v + (1.0 - ADAM_B2) * (g * g)
    m_hat = m_new / (1.0 - ADAM_B1 ** ADAM_STEP)
    v_hat = v_new / (1.0 - ADAM_B2 ** ADAM_STEP)
    delta = -ADAM_LR * (m_hat / (jnp.sqrt(v_hat) + ADAM_EPS) + ADAM_WD * w)
    return delta, m_new, v_new


def _sum_leading(name, x):
    n, rows, cols = x.shape
    tr = PACK_ROWS if rows % PACK_ROWS == 0 else rows

    def body(x_ref, o_ref):
        acc = x_ref[0]
        for d in range(1, n):
            acc = acc + x_ref[d]
        o_ref[...] = acc

    return _pallas(body, name=name, grid=(rows // tr,),
                   in_specs=[pl.BlockSpec((n, tr, cols), lambda i: (0, i, 0))],
                   out_specs=pl.BlockSpec((tr, cols), lambda i: (i, 0)),
                   out_shape=jax.ShapeDtypeStruct((rows, cols), F32),
                   compiler_params=_cp(("arbitrary",), VMEM_MID))(x)


def _adamw(name, w, m, v, g=None, parts=None):
    rows, cols = w.shape
    tr = rows if rows <= 256 else 256

    def body(*refs):
        w_ref, m_ref, v_ref, g_in, g_ref, d_ref, mo_ref, vo_ref = refs
        if parts is None:
            gv = g_in[...]
        else:
            acc = g_in[0].astype(F32)
            for d in range(1, N_DEV):
                acc = acc + g_in[d].astype(F32)
            gv = acc[:, :cols]
        delta, m_new, v_new = _adam_math(w_ref[...], gv, m_ref[...], v_ref[...])
        g_ref[...] = gv
        d_ref[...] = delta
        mo_ref[...] = m_new
        vo_ref[...] = v_new

    row = pl.BlockSpec((tr, cols), lambda i: (i, 0))
    if parts is None:
        g_spec, g_arg = row, g
    else:
        g_spec, g_arg = pl.BlockSpec((N_DEV, tr, parts.shape[2]), lambda i: (0, i, 0)), parts
    return _pallas(body, name=name, grid=(rows // tr,), in_specs=[row, row, row, g_spec], out_specs=[row] * 4,
                   out_shape=[jax.ShapeDtypeStruct((rows, cols), F32)] * 4,
                   compiler_params=_cp(("arbitrary",), VMEM_MID))(w, m, v, g_arg)


def _pack_rows(arrs):
    rows, meta, total = [], [], 0
    for a in arrs:
        flat = a.reshape(-1)
        nrow = -(-flat.shape[0] // 1024) * 8
        rows.append(jnp.pad(flat, (0, nrow * 128 - flat.shape[0])).reshape(nrow, 128))
        meta.append((a.shape, flat.shape[0], nrow))
        total += nrow
    tail = -total % PACK_ROWS
    if tail:
        rows.append(jnp.zeros((tail, 128), F32))
    return jnp.concatenate(rows, axis=0), meta


def _unpack_rows(packed, meta):
    out, r0 = [], 0
    for shape, size, nrow in meta:
        out.append(packed[r0:r0 + nrow].reshape(-1)[:size].reshape(shape))
        r0 += nrow
    return out


WEIGHTS = ["rel_bias", "norm_g", "ada_w", "ada_b", "attn_w_in", "attn_sinks", "attn_b_f", "attn_w_out", "lru_w_in",
           "lru_conv_w", "lru_conv_b", "lru_w_a", "lru_b_a", "lru_w_x", "lru_b_x", "lru_lambda", "lru_w_out", "final_g"]
BIG = ["ada_w", "attn_w_in", "attn_w_out", "lru_w_in", "lru_w_out"]
PACK_ROWS = 256


def kernel(x, c, rel_bias, norm_g, ada_w, ada_b, attn_w_in, attn_sinks, attn_b_f, attn_w_out, lru_w_in, lru_conv_w, lru_conv_b, lru_w_a, lru_b_a, lru_w_x, lru_b_x, lru_lambda, lru_w_out, final_g, loss_target, m_rel_bias, m_norm_g, m_ada_w, m_ada_b, m_attn_w_in, m_attn_sinks, m_attn_b_f, m_attn_w_out, m_lru_w_in, m_lru_conv_w, m_lru_conv_b, m_lru_w_a, m_lru_b_a, m_lru_w_x, m_lru_b_x, m_lru_lambda, m_lru_w_out, m_final_g, v_rel_bias, v_norm_g, v_ada_w, v_ada_b, v_attn_w_in, v_attn_sinks, v_attn_b_f, v_attn_w_out, v_lru_w_in, v_lru_conv_w, v_lru_conv_b, v_lru_w_a, v_lru_b_a, v_lru_w_x, v_lru_b_x, v_lru_lambda, v_lru_w_out, v_final_g):
    nseq, seq, _ = x.shape
    t_tok = nseq * seq
    me = 4 * lax.axis_index("x") + 2 * lax.axis_index("y") + lax.axis_index("c")
    x0 = x.reshape(t_tok, D_MODEL)
    target = loss_target.reshape(t_tok, D_MODEL)

    w_in_pad = jnp.pad(attn_w_in[0].astype(BF16), ((0, 0), (0, SHARD_W_PAD - SHARD_W_IN)))
    vec_shard = jnp.concatenate([lru_conv_w[0], lru_conv_b, lru_b_a, lru_b_x, lru_lambda], axis=0)
    g_w_in, g_w_out0, g_lru_in, g_w_out1, g_vec, g_c = _exchange(
        "gather_weights",
        [w_in_pad, attn_w_out[0].astype(BF16), lru_w_in[0].astype(BF16), lru_w_out[0].astype(BF16), vec_shard, c], [])
    w_full = jnp.transpose(g_w_in[:, :, :SHARD_W_IN], (1, 0, 2)).reshape(D_MODEL, N_DEV * SHARD_W_IN)
    w_aq, w_ak, w_av = w_full[:, 0:512], w_full[:, 512:640], w_full[:, 640:768]
    w_bq, w_bk, w_bv = w_full[:, 768:1280], w_full[:, 1280:1792], w_full[:, 1792:2304]
    w_f, w_gate = w_full[:, 2304:2312], w_full[:, 2312:3336]
    w_main = jnp.concatenate([w_bq, w_bk, w_bv, w_aq, w_gate, w_ak, w_av], axis=1)
    wf_t = jnp.transpose(w_f)
    w_out0 = g_w_out0.reshape(D_MODEL, D_MODEL)
    w_out1 = g_w_out1.reshape(D_MODEL, D_MODEL)
    vec_full = jnp.transpose(g_vec, (1, 0, 2)).reshape(8, D_MODEL)
    conv_w, conv_b, b_a, b_x, lam = vec_full[0:4], vec_full[4:5], vec_full[5:6], vec_full[6:7], vec_full[7:8]
    c_all = g_c.reshape(N_DEV * nseq, D_MODEL)

    ncol = ada_w.shape[2]
    ada_b_slice = lax.dynamic_slice(ada_b.reshape(2, N_DEV, ncol), (0, me, 0), (2, 1, ncol))
    mod_part = _ada_mod(c_all, ada_w, ada_b_slice)
    (g_mod,) = _exchange("gather_mod", [mod_part], [])
    mine = lax.dynamic_slice(g_mod, (0, 0, me * nseq, 0), (N_DEV, 2, nseq, ncol))
    mod = jnp.transpose(mine, (1, 2, 0, 3)).reshape(2, nseq, 3 * D_MODEL)
    shift = [mod[l, :, 0:D_MODEL].reshape(nseq, 1, D_MODEL) for l in range(2)]
    scale = [mod[l, :, D_MODEL:2 * D_MODEL].reshape(nseq, 1, D_MODEL) for l in range(2)]
    gmod = [mod[l, :, 2 * D_MODEL:].reshape(nseq, 1, D_MODEL) for l in range(2)]

    onehot = _bucket_onehot()
    bias = _bias_expand(jnp.transpose(rel_bias), onehot).reshape(N_HEADS, BLOCK, 2 * BLOCK)
    sinks = attn_sinks.reshape(N_HEADS)
    b_f = attn_b_f.reshape(N_HEADS, 1)
    h0, qkvg, fl_t = _norm_proj("norm_proj0", x0, norm_g[0:1], shift[0], scale[0], w_main, seq, BF16, wf_t=wf_t)
    f_row, f_col = _fox_prep(fl_t, b_f, seq)
    a_out, lse_a = _swa_fwd(qkvg, bias, sinks, seq)
    q_aug, k_aug, kt_aug, vt = _fox_aug(qkvg, f_col, seq)
    b_out, lse_b = _fox_fwd_t(q_aug, k_aug, vt, seq)
    yg0, y0, x1 = _out_proj("out_proj0", [a_out, b_out], qkvg, C_GATE // D_MODEL, w_out0, x0, gmod[0], seq)

    h1, proj1 = _norm_proj("norm_proj1", x1, norm_g[1:2], shift[1], scale[1], g_lru_in, seq, F32)
    hs = _lru_fwd(proj1, conv_w, conv_b, lru_w_a[0], b_a, lru_w_x[0], b_x, lam, seq)
    yg1, y1, x2 = _out_proj("out_proj1", [hs], proj1, 1, w_out1, x1, gmod[1], seq)

    dx2, loss_rows, dfinal_rows = _final_loss(x2, final_g.reshape(1, D_MODEL), target)
    loss = lax.psum(jnp.sum(loss_rows[0]), ("x", "y", "c"))

    dy1, dgm1, dyh = _out_proj_bwd("out_proj1_bwd", dx2, gmod[1], y1, w_out1, seq)
    dproj1, dcw, dvec, dw_a, dw_x = _lru_bwd(proj1, hs, dyh, conv_w, conv_b, lru_w_a[0], b_a, lru_w_x[0], b_x, lam, seq)
    dx1, dss1, dg1 = _norm_bwd("norm1_bwd", [(dproj1, 0)], g_lru_in, x1, norm_g[1:2], scale[1], dx2, seq)
    (p_w_out1,) = _dw("dw_out1", yg1, [dy1])
    (p_lru_in,) = _dw("dw_lru_in", h1, [dproj1], blocked=2 * D_MODEL // N_DEV)

    dy0, dgm0, du_a, du_b, dgate = _out_proj_bwd("out_proj0_bwd", dx1, gmod[0], y0, w_out0, seq,
                                                  attn=(a_out, b_out, qkvg))
    dq_a, dkv_a, dbias, dsink = _swa_bwd(qkvg, du_a, a_out, lse_a, bias, sinks, seq)
    dq_b, dk_b, dv_b, df4 = _fox_bwd_t(q_aug, k_aug, kt_aug, qkvg, du_b, b_out, lse_b, seq)
    dfl_t, db_f = _fox_post(df4.reshape(N_HEADS, t_tok), fl_t, b_f, seq)
    parts0 = [(dq_b, C_BQ), (dk_b, C_BK), (dv_b, C_BV), (dq_a, C_AQ), (dgate, C_GATE), (dkv_a, C_AK)]
    dx0, dss0, dg0 = _norm_bwd("norm0_bwd", parts0, w_main, x0, norm_g[0:1], scale[0], dx1, seq,
                               rows_part=(dfl_t, wf_t))
    (p_w_out0,) = _dw("dw_out0", yg0, [dy0])
    pw_bq, pw_bk, pw_bv, pw_aq, pw_gate, pw_akv = _dw("dw_attn_in", h0, [p for p, _ in parts0])
    pw_f = _dw_rows("dw_f", dfl_t, h0)
    dbias_t = _bias_reduce(dbias.reshape(N_HEADS, BLOCK * 2 * BLOCK), onehot)

    p_w_in = jnp.concatenate([pw_aq, pw_akv, pw_bq, pw_bk, pw_bv, jnp.transpose(pw_f).astype(BF16), pw_gate], axis=1)
    p_w_in = jnp.transpose(p_w_in.reshape(D_MODEL, N_DEV, SHARD_W_IN), (1, 0, 2))
    p_w_in = jnp.pad(p_w_in, ((0, 0), (0, 0), (0, SHARD_W_PAD - SHARD_W_IN)))
    rows_out = D_MODEL // N_DEV
    small_partials = [jnp.transpose(dbias_t), jnp.stack([dg0[0], dg1[0]]), dsink[:, 0], db_f[:, 0],
                      dcw[0:4], dvec[0:4], dw_a, dw_x, dfinal_rows[0]]
    gpack, gmeta = _pack_rows(small_partials)
    dmod = jnp.stack([jnp.concatenate([dss[:, 0], dss[:, 1], dgm[:, 0]], axis=1)
                      for dss, dgm in ((dss0, dgm0), (dss1, dgm1))], axis=1)
    g_small, g_dmod, r_w_in, r_w_out0, r_lru_in, r_w_out1 = _exchange(
        "exchange_grads", [gpack, dmod],
        [p_w_in, p_w_out0.reshape(N_DEV, rows_out, D_MODEL), p_lru_in, p_w_out1.reshape(N_DEV, rows_out, D_MODEL)])

    small_sum = _sum_leading("sum_small", g_small)
    d_rel, d_norm_g, d_sinks, d_b_f, d_cw, d_vec, d_wa, d_wx, d_final_g = _unpack_rows(small_sum, gmeta)
    cols = lambda a: lax.dynamic_slice(a, (0, me * LRU_BLOCK_W), (a.shape[0], LRU_BLOCK_W))
    dmod_all = g_dmod.reshape(N_DEV * nseq, 2 * 3 * D_MODEL)
    d_ada_b = _sum_leading("sum_ada_b", dmod_all.reshape(N_DEV * nseq, 2 * 3 * D_MODEL // 128, 128)).reshape(2, 3 * D_MODEL)
    dmod_slice = lax.dynamic_slice(dmod_all.reshape(N_DEV * nseq, 2, N_DEV, ncol), (0, 0, me, 0),
                                   (N_DEV * nseq, 2, 1, ncol)).reshape(N_DEV * nseq, 2, ncol)
    d_ada_w = _ada_w_grad(c_all, jnp.transpose(dmod_slice, (1, 0, 2)))

    given = dict(
        rel_bias=(rel_bias, m_rel_bias, v_rel_bias), norm_g=(norm_g, m_norm_g, v_norm_g),
        ada_w=(ada_w, m_ada_w, v_ada_w), ada_b=(ada_b, m_ada_b, v_ada_b),
        attn_w_in=(attn_w_in, m_attn_w_in, v_attn_w_in), attn_sinks=(attn_sinks, m_attn_sinks, v_attn_sinks),
        attn_b_f=(attn_b_f, m_attn_b_f, v_attn_b_f), attn_w_out=(attn_w_out, m_attn_w_out, v_attn_w_out),
        lru_w_in=(lru_w_in, m_lru_w_in, v_lru_w_in), lru_conv_w=(lru_conv_w, m_lru_conv_w, v_lru_conv_w),
        lru_conv_b=(lru_conv_b, m_lru_conv_b, v_lru_conv_b), lru_w_a=(lru_w_a, m_lru_w_a, v_lru_w_a),
        lru_b_a=(lru_b_a, m_lru_b_a, v_lru_b_a), lru_w_x=(lru_w_x, m_lru_w_x, v_lru_w_x),
        lru_b_x=(lru_b_x, m_lru_b_x, v_lru_b_x), lru_lambda=(lru_lambda, m_lru_lambda, v_lru_lambda),
        lru_w_out=(lru_w_out, m_lru_w_out, v_lru_w_out), final_g=(final_g, m_final_g, v_final_g))
    results = {}

    def big(name, shape2d, g=None, parts=None):
        w, m, v = (a.reshape(shape2d) for a in given[name])
        outs = _adamw("adamw_" + name, w, m, v, g=g, parts=parts)
        results[name] = tuple(o.reshape(given[name][0].shape) for o in outs)

    big("ada_w", (2 * D_MODEL, ncol), g=d_ada_w.reshape(2 * D_MODEL, ncol))
    big("attn_w_in", (D_MODEL, SHARD_W_IN), parts=r_w_in)
    big("attn_w_out", (rows_out, D_MODEL), parts=r_w_out0)
    big("lru_w_in", (D_MODEL, 2 * D_MODEL // N_DEV), parts=r_lru_in)
    big("lru_w_out", (rows_out, D_MODEL), parts=r_w_out1)

    small_grads = dict(
        rel_bias=d_rel, norm_g=d_norm_g, ada_b=d_ada_b, attn_sinks=d_sinks.reshape(1, N_HEADS),
        attn_b_f=d_b_f.reshape(1, N_HEADS), lru_conv_w=cols(d_cw).reshape(1, 4, LRU_BLOCK_W),
        lru_conv_b=cols(d_vec[0:1]), lru_w_a=d_wa.reshape(lru_w_a.shape), lru_b_a=cols(d_vec[1:2]),
        lru_w_x=d_wx.reshape(lru_w_x.shape), lru_b_x=cols(d_vec[2:3]), lru_lambda=cols(d_vec[3:4]),
        final_g=d_final_g)
    small = [n for n in WEIGHTS if n not in BIG]
    wpack, smeta = _pack_rows([given[n][0] for n in small])
    mpack, _ = _pack_rows([given[n][1] for n in small])
    vpack, _ = _pack_rows([given[n][2] for n in small])
    gpack2, _ = _pack_rows([small_grads[n] for n in small])
    packs = _adamw("adamw_small", wpack, mpack, vpack, g=gpack2)
    unpacked = [_unpack_rows(p, smeta) for p in packs]
    for k, n in enumerate(small):
        results[n] = tuple(unpacked[j][k] for j in range(4))

    grad_x = dx0.reshape(x.shape)
    out = [loss, grad_x]
    for j in range(4):
        out += [results[n][j] for n in WEIGHTS]
    return tuple(out)
```

```python
import functools
import math

import jax
import jax.numpy as jnp
from jax import lax
from jax.experimental import pallas as pl
from jax.experimental.pallas import tpu as pltpu

F32 = jnp.float32
BF16 = jnp.bfloat16
HI = lax.Precision.HIGHEST
MESH = pl.DeviceIdType.MESH

N_DEV = 8
D_MODEL = 1024
HEAD_DIM = 64
N_HEADS = 8
KV_GROUP = 4
BLOCK = 128
REL_BUCKETS = 32
REL_MAX_EXACT = 16
REL_MAX_DIST = 128
LRU_BLOCKS = 8
LRU_BLOCK_W = 128
LRU_C = 8.0
EPS = 1e-6
SCALE = HEAD_DIM ** -0.5
NEG = -1e30

ADAM_LR = 0.001
ADAM_B1 = 0.9
ADAM_B2 = 0.999
ADAM_EPS = 1e-08
ADAM_WD = 0.01
ADAM_STEP = 10

C_BQ, C_BK, C_BV, C_AQ, C_GATE, C_AK, C_AV = 0, 512, 1024, 1536, 2048, 3072, 3200
N_MAIN = 3328
SHARD_W_IN = 417
SHARD_W_PAD = 512

TM = 256
TQ = 256
TK = 128
TC = 256
VMEM_BIG = 56 * 1024 * 1024
VMEM_MID = 40 * 1024 * 1024


def _pallas(body, **kw):
    return pl.pallas_call(body, **kw)


def _cp(sem=None, vmem=None):
    kw = {}
    if sem is not None:
        kw["dimension_semantics"] = sem
    if vmem is not None:
        kw["vmem_limit_bytes"] = vmem
    return pltpu.CompilerParams(**kw)


def _nn(a, b, precision=None):
    return jnp.dot(a, b, preferred_element_type=F32, precision=precision)


def _nt(a, b, precision=None):
    return lax.dot_general(a, b, (((1,), (1,)), ((), ())), preferred_element_type=F32, precision=precision)


def _tn(a, b, precision=None):
    return lax.dot_general(a, b, (((0,), (0,)), ((), ())), preferred_element_type=F32, precision=precision)


def _sigmoid(x):
    return 1.0 / (1.0 + jnp.exp(-x))


def _silu(x):
    return x * _sigmoid(x)


def _dsilu(x):
    s = _sigmoid(x)
    return s * (1.0 + x * (1.0 - s))


def _neg_expm1(x):
    poly = x * (1.0 + x * (0.5 + x * (1.0 / 6.0 + x * (1.0 / 24.0))))
    return -jnp.where(jnp.abs(x) < 0.05, poly, jnp.exp(x) - 1.0)


def _col(tile, idx):
    lane = lax.broadcasted_iota(jnp.int32, tile.shape, 1)
    return jnp.sum(jnp.where(lane == idx, tile, 0.0), axis=1, keepdims=True)


def _row(tile, idx):
    sub = lax.broadcasted_iota(jnp.int32, tile.shape, 0)
    return jnp.sum(jnp.where(sub == idx, tile, 0.0), axis=0, keepdims=True)


def _exchange(name, gathers, scatters):
    ng, n = len(gathers), len(gathers) + len(scatters)
    ins = list(gathers) + list(scatters)

    def body(*refs):
        in_refs, out_refs = refs[:n], refs[n:2 * n]
        send_sems, recv_sems, loc_sems = refs[2 * n:]
        x, y, c = lax.axis_index("x"), lax.axis_index("y"), lax.axis_index("c")
        me = 4 * x + 2 * y + c

        def peer(r):
            px = 1 - x if r & 4 else x
            py = 1 - y if r & 2 else y
            pc = 1 - c if r & 1 else c
            return (px, py, pc), 4 * px + 2 * py + pc

        local, sends, recvs = [], [], []
        for k in range(n):
            mine = in_refs[k] if k < ng else in_refs[k].at[me]
            cp = pltpu.make_async_copy(mine, out_refs[k].at[me], loc_sems.at[k])
            cp.start()
            local.append(cp)
            for r in range(1, N_DEV):
                pid, pidx = peer(r)
                src = in_refs[k] if k < ng else in_refs[k].at[pidx]
                snd = pltpu.make_async_remote_copy(
                    src_ref=src, dst_ref=out_refs[k].at[me], send_sem=send_sems.at[r - 1, k],
                    recv_sem=recv_sems.at[r - 1, k], device_id=pid, device_id_type=MESH)
                snd.start()
                sends.append(snd)
                recvs.append(pltpu.make_async_remote_copy(
                    src_ref=src, dst_ref=out_refs[k].at[pidx], send_sem=send_sems.at[r - 1, k],
                    recv_sem=recv_sems.at[r - 1, k], device_id=pid, device_id_type=MESH))
        for rc in recvs:
            rc.wait_recv()
        for snd in sends:
            snd.wait_send()
        for cp in local:
            cp.wait()

    out_shape = [jax.ShapeDtypeStruct((N_DEV,) + a.shape, a.dtype) for a in gathers]
    out_shape += [jax.ShapeDtypeStruct(a.shape, a.dtype) for a in scatters]
    any_spec = pl.BlockSpec(memory_space=pl.ANY)
    return _pallas(
        body, name=name, out_shape=out_shape,
        in_specs=[any_spec] * n, out_specs=[any_spec] * n,
        scratch_shapes=[pltpu.SemaphoreType.DMA((N_DEV - 1, n)), pltpu.SemaphoreType.DMA((N_DEV - 1, n)),
                        pltpu.SemaphoreType.DMA((n,))],
    )(*ins)


def _ada_mod(c_all, ada_w, ada_b_slice):
    def body(c_ref, w_ref, b_ref, o_ref):
        ca = _silu(c_ref[...])
        for l in range(2):
            o_ref[l] = _nn(ca, w_ref[l], HI) + b_ref[l]

    return _pallas(body, name="ada_mod",
                   out_shape=jax.ShapeDtypeStruct((2, c_all.shape[0], ada_w.shape[2]), F32),
                   compiler_params=_cp(vmem=VMEM_MID))(c_all, ada_w, ada_b_slice)


def _ada_w_grad(c_all, dmod_slice):
    def body(c_ref, d_ref, o_ref):
        ca = _silu(c_ref[...])
        for l in range(2):
            o_ref[l] = _tn(ca, d_ref[l], HI)

    return _pallas(body, name="ada_w_grad",
                   out_shape=jax.ShapeDtypeStruct((2, D_MODEL, dmod_slice.shape[2]), F32),
                   compiler_params=_cp(vmem=VMEM_MID))(c_all, dmod_slice)


def _bucket_onehot():
    qi = jnp.arange(BLOCK)[:, None]
    kj = jnp.arange(2 * BLOCK)[None, :]
    rel = qi - kj + BLOCK
    n = jnp.maximum(rel, 0)
    nf = jnp.maximum(n, 1).astype(F32)
    large = REL_MAX_EXACT + (jnp.log(nf / REL_MAX_EXACT) / math.log(REL_MAX_DIST / REL_MAX_EXACT)
                             * (REL_BUCKETS - REL_MAX_EXACT)).astype(jnp.int32)
    large = jnp.minimum(large, REL_BUCKETS - 1)
    bucket = jnp.where(n < REL_MAX_EXACT, n, large).reshape(1, BLOCK * 2 * BLOCK)
    return (jnp.arange(REL_BUCKETS)[:, None] == bucket).astype(F32)


def _bias_expand(rel_bias_t, onehot):
    def body(r_ref, e_ref, o_ref):
        o_ref[...] = _nn(r_ref[...], e_ref[...], HI)

    return _pallas(body, name="bias_expand",
                   out_shape=jax.ShapeDtypeStruct((N_HEADS, onehot.shape[1]), F32),
                   compiler_params=_cp(vmem=VMEM_MID))(rel_bias_t, onehot)


def _bias_reduce(dbias, onehot):
    def body(d_ref, e_ref, o_ref):
        o_ref[...] = _nt(d_ref[...], e_ref[...], HI)

    return _pallas(body, name="bias_reduce",
                   out_shape=jax.ShapeDtypeStruct((N_HEADS, REL_BUCKETS), F32),
                   compiler_params=_cp(vmem=VMEM_MID))(dbias, onehot)


def _norm_proj(name, x, g, shift, scale, w, seq, out_dtype, wf_t=None):
    t_tok = x.shape[0]
    w3d = w.ndim == 3
    n_out = w.shape[0] * w.shape[2] if w3d else w.shape[1]
    cn = w.shape[2] if w3d else 256

    def body(x_ref, g_ref, sh_ref, sc_ref, w_ref, *rest):
        if wf_t is not None:
            wf_ref, h_ref, o_ref, fl_ref = rest
        else:
            h_ref, o_ref = rest
        xv = x_ref[...]
        rstd = lax.rsqrt(jnp.mean(xv * xv, axis=-1, keepdims=True) + EPS)
        h = (xv * rstd) * g_ref[...] * (1.0 + sc_ref[...]) + sh_ref[...]
        hb = h.astype(BF16)
        h_ref[...] = hb
        for j in range(n_out // cn):
            wj = w_ref[j] if w3d else w_ref[:, j * cn:(j + 1) * cn]
            o_ref[:, j * cn:(j + 1) * cn] = _nn(hb, wj).astype(out_dtype)
        if wf_t is not None:
            fl_ref[...] = _nt(wf_ref[...], hb)

    mod_spec = pl.BlockSpec((None, 1, D_MODEL), lambda i: (i * TM // seq, 0, 0))
    w_spec = (pl.BlockSpec(w.shape, lambda i: (0, 0, 0)) if w3d else pl.BlockSpec(w.shape, lambda i: (0, 0)))
    in_specs = [pl.BlockSpec((TM, D_MODEL), lambda i: (i, 0)), pl.BlockSpec((1, D_MODEL), lambda i: (0, 0)),
                mod_spec, mod_spec, w_spec]
    out_shape = [jax.ShapeDtypeStruct((t_tok, D_MODEL), BF16), jax.ShapeDtypeStruct((t_tok, n_out), out_dtype)]
    out_specs = [pl.BlockSpec((TM, D_MODEL), lambda i: (i, 0)), pl.BlockSpec((TM, n_out), lambda i: (i, 0))]
    args = [x, g, shift, scale, w]
    if wf_t is not None:
        in_specs.append(pl.BlockSpec(wf_t.shape, lambda i: (0, 0)))
        out_shape.append(jax.ShapeDtypeStruct((wf_t.shape[0], t_tok), F32))
        out_specs.append(pl.BlockSpec((wf_t.shape[0], TM), lambda i: (0, i)))
        args.append(wf_t)
    return _pallas(body, name=name, grid=(t_tok // TM,), in_specs=in_specs, out_specs=out_specs,
                   out_shape=out_shape, compiler_params=_cp(("arbitrary",), VMEM_BIG))(*args)


def _fox_prep(fl_t, b_f, seq):
    t_tok = fl_t.shape[1]
    ch = 256

    def body(fl_ref, bf_ref, fr_ref, fc_ref):
        z = fl_ref[...] + bf_ref[...]
        logf = jnp.minimum(z, 0.0) - jnp.log(1.0 + jnp.exp(-jnp.abs(z)))
        ri = lax.broadcasted_iota(jnp.int32, (ch, ch), 0)
        ci = lax.broadcasted_iota(jnp.int32, (ch, ch), 1)
        upper = (ri <= ci).astype(F32)
        eye = (ri == ci).astype(F32)
        carry = jnp.zeros((N_HEADS, 1), F32)
        for k in range(seq // ch):
            fk = _nn(logf[:, k * ch:(k + 1) * ch], upper, HI) + carry
            carry = fk[:, ch - 1:ch]
            fr_ref[:, k * ch:(k + 1) * ch] = fk
            padded = jnp.concatenate([fk, jnp.zeros((128 - N_HEADS, ch), F32)], axis=0)
            fc_ref[k * ch:(k + 1) * ch, :] = _nt(eye, padded, HI)

    return _pallas(
        body, name="fox_prep", grid=(t_tok // seq,),
        in_specs=[pl.BlockSpec((N_HEADS, seq), lambda b: (0, b)), pl.BlockSpec((N_HEADS, 1), lambda b: (0, 0))],
        out_specs=[pl.BlockSpec((N_HEADS, seq), lambda b: (0, b)), pl.BlockSpec((seq, 128), lambda b: (b, 0))],
        out_shape=[jax.ShapeDtypeStruct((N_HEADS, t_tok), F32), jax.ShapeDtypeStruct((t_tok, 128), F32)],
        compiler_params=_cp(("arbitrary",), VMEM_MID))(fl_t, b_f)


def _fox_post(df_row, fl_t, b_f, seq):
    t_tok = fl_t.shape[1]
    ch = 256

    def body(d_ref, fl_ref, bf_ref, o_ref, db_ref):
        @pl.when(pl.program_id(0) == 0)
        def _():
            db_ref[...] = jnp.zeros_like(db_ref)

        z = fl_ref[...] + bf_ref[...]
        sig_neg = 1.0 / (1.0 + jnp.exp(z))
        ri = lax.broadcasted_iota(jnp.int32, (ch, ch), 0)
        ci = lax.broadcasted_iota(jnp.int32, (ch, ch), 1)
        lower = (ri >= ci).astype(F32)
        carry = jnp.zeros((N_HEADS, 1), F32)
        tot = jnp.zeros((N_HEADS, 1), F32)
        for k in reversed(range(seq // ch)):
            dk = _nn(d_ref[:, k * ch:(k + 1) * ch], lower, HI) + carry
            carry = dk[:, 0:1]
            dfl = dk * sig_neg[:, k * ch:(k + 1) * ch]
            o_ref[:, k * ch:(k + 1) * ch] = dfl
            tot = tot + jnp.sum(dfl, axis=1, keepdims=True)
        db_ref[...] += jnp.broadcast_to(tot, db_ref.shape)

    return _pallas(
        body, name="fox_post", grid=(t_tok // seq,),
        in_specs=[pl.BlockSpec((N_HEADS, seq), lambda b: (0, b)), pl.BlockSpec((N_HEADS, seq), lambda b: (0, b)),
                  pl.BlockSpec((N_HEADS, 1), lambda b: (0, 0))],
        out_specs=[pl.BlockSpec((N_HEADS, seq), lambda b: (0, b)), pl.BlockSpec((N_HEADS, 128), lambda b: (0, 0))],
        out_shape=[jax.ShapeDtypeStruct((N_HEADS, t_tok), F32), jax.ShapeDtypeStruct((N_HEADS, 128), F32)],
        compiler_params=_cp(("arbitrary",), VMEM_MID))(df_row, fl_t, b_f)


def _eye(n, dtype):
    return (lax.broadcasted_iota(jnp.int32, (n, n), 0) == lax.broadcasted_iota(jnp.int32, (n, n), 1)).astype(dtype)


def _fox_aug(qkvg, f_col, seq):
    t_tok = qkvg.shape[0]
    ta = 256
    nkb = ta // TK

    def body(q_ref, k_ref, v_ref, fc_ref, qa_ref, ka_ref, kt_ref, vt_ref):
        ri = lax.broadcasted_iota(jnp.int32, (128, 128), 0)
        ci = lax.broadcasted_iota(jnp.int32, (128, 128), 1)
        eye = (ri == ci).astype(BF16)
        lane = lax.broadcasted_iota(jnp.int32, (ta, 128), 1)
        ones_q = jnp.where(jnp.logical_and(lane >= 64, lane < 67), 1.0, 0.0)
        ones_k = jnp.where(jnp.logical_and(lane >= 67, lane < 70), 1.0, 0.0)
        fc_tile = fc_ref[...]
        for p in range(N_HEADS // 2):
            q2 = q_ref[:, 128 * p:128 * (p + 1)]
            k2 = k_ref[:, 128 * p:128 * (p + 1)]
            vt = _nt(eye, v_ref[:, 128 * p:128 * (p + 1)]).astype(BF16)
            for kk in range(nkb):
                vt_ref[p, kk] = vt[:, kk * TK:(kk + 1) * TK]
            for e in range(2):
                h = 2 * p + e
                sel = jnp.logical_and(ri == ci + HEAD_DIM * e, ci < HEAD_DIM)
                f = _col(fc_tile, h)
                fh = f.astype(BF16).astype(F32)
                fm = (f - fh).astype(BF16).astype(F32)
                fl = (f - fh - fm).astype(BF16).astype(F32)
                qa = (_nn(q2, jnp.where(sel, SCALE, 0.0).astype(BF16)) + ones_q + jnp.where(lane == 67, fh, 0.0)
                      + jnp.where(lane == 68, fm, 0.0) + jnp.where(lane == 69, fl, 0.0))
                ka = (_nn(k2, jnp.where(sel, 1.0, 0.0).astype(BF16)) + ones_k - jnp.where(lane == 64, fh, 0.0)
                      - jnp.where(lane == 65, fm, 0.0) - jnp.where(lane == 66, fl, 0.0))
                qa_ref[h] = qa.astype(BF16)
                kab = ka.astype(BF16)
                ka_ref[h] = kab
                kt = _nt(eye, kab).astype(BF16)
                for kk in range(nkb):
                    kt_ref[h, kk] = kt[:, kk * TK:(kk + 1) * TK]

    aug = jax.ShapeDtypeStruct((N_HEADS, t_tok, 128), BF16)
    return _pallas(
        body, name="fox_aug", grid=(t_tok // ta,),
        in_specs=[pl.BlockSpec((ta, 512), lambda i: (i, C_BQ // 512)), pl.BlockSpec((ta, 512), lambda i: (i, C_BK // 512)),
                  pl.BlockSpec((ta, 512), lambda i: (i, C_BV // 512)), pl.BlockSpec((ta, 128), lambda i: (i, 0))],
        out_specs=[pl.BlockSpec((N_HEADS, ta, 128), lambda i: (0, i, 0)), pl.BlockSpec((N_HEADS, ta, 128), lambda i: (0, i, 0)),
                   pl.BlockSpec((N_HEADS, nkb, 128, TK), lambda i: (0, i, 0, 0)),
                   pl.BlockSpec((N_HEADS // 2, nkb, 128, TK), lambda i: (0, i, 0, 0))],
        out_shape=[aug, aug, jax.ShapeDtypeStruct((N_HEADS, t_tok // TK, 128, TK), BF16),
                   jax.ShapeDtypeStruct((N_HEADS // 2, t_tok // TK, 128, TK), BF16)],
        compiler_params=_cp(("arbitrary",), VMEM_MID))(qkvg, qkvg, qkvg, f_col)


def _fox_fwd_t(q_aug, k_aug, vt, seq):
    t_tok = q_aug.shape[1]
    nq = seq // TQ
    ratio = TQ // TK

    def body(qa_ref, ka_ref, vt_ref, o_ref, lse_ref, ml_s, acc_s, st_s, p_s, al_s):
        i = pl.program_id(1)
        tpos = i * TQ + lax.broadcasted_iota(jnp.int32, (1, TQ), 1)
        eye = _eye(HEAD_DIM, BF16)
        for h in range(N_HEADS):
            ml_s[0, h] = jnp.full((1, TQ), NEG, F32)
            ml_s[1, h] = jnp.zeros((1, TQ), F32)
            acc_s[h] = jnp.zeros((HEAD_DIM, TQ), F32)
            p_s[1, h] = jnp.zeros((TK, TQ), BF16)
            al_s[1, h] = jnp.ones((1, TQ), F32)

        def scores(j):
            row0 = pl.multiple_of(j * TK, TK)
            for h in range(N_HEADS):
                st_s[j & 1, h] = _nt(ka_ref[h, pl.ds(row0, TK), :], qa_ref[h])

        def softmax(j, masked):
            slot = j & 1
            if masked:
                keep = (j * TK + lax.broadcasted_iota(jnp.int32, (TK, 1), 0)) <= tpos
            for h in range(N_HEADS):
                st = st_s[slot, h]
                if masked:
                    st = jnp.where(keep, st, NEG)
                m = ml_s[0, h]
                m_new = jnp.maximum(m, jnp.max(st, axis=0, keepdims=True))
                alpha = jnp.exp(m - m_new)
                pe = jnp.exp(st - m_new)
                ml_s[0, h] = m_new
                ml_s[1, h] = alpha * ml_s[1, h] + jnp.sum(pe, axis=0, keepdims=True)
                al_s[slot, h] = alpha
                p_s[slot, h] = pe.astype(BF16)

        def values(j):
            slot = j & 1
            jv = jnp.maximum(j, 0)
            for h in range(N_HEADS):
                p, e = divmod(h, 2)
                acc_s[h] = al_s[slot, h] * acc_s[h] + _nn(vt_ref[p, jv, e * HEAD_DIM:(e + 1) * HEAD_DIM, :], p_s[slot, h])

        def step(j, carry):
            values(j - 1)
            softmax(j, False)
            scores(j + 1)
            return carry

        last = ratio * i + ratio - 1
        scores(0)
        lax.fori_loop(0, ratio * i, step, 0)
        for kk in range(ratio):
            j = ratio * i + kk
            values(j - 1)
            softmax(j, True)
            if kk < ratio - 1:
                scores(j + 1)
        values(last)
        for p in range(N_HEADS // 2):
            outs = []
            for e in range(2):
                h = 2 * p + e
                l = ml_s[1, h]
                outs.append(_tn((acc_s[h] / l).astype(BF16), eye))
                lse_ref[p, e:e + 1, :] = ml_s[0, h] + jnp.log(l)
            o_ref[:, 128 * p:128 * (p + 1)] = jnp.concatenate(outs, axis=1).astype(BF16)

    return _pallas(
        body, name="fox_fwd", grid=(t_tok // seq, nq),
        in_specs=[pl.BlockSpec((N_HEADS, TQ, 128), lambda b, i: (0, b * nq + i, 0)),
                  pl.BlockSpec((N_HEADS, seq, 128), lambda b, i: (0, b, 0)),
                  pl.BlockSpec((N_HEADS // 2, seq // TK, 128, TK), lambda b, i: (0, b, 0, 0))],
        out_specs=[pl.BlockSpec((TQ, 512), lambda b, i: (b * nq + i, 0)),
                   pl.BlockSpec((N_HEADS // 2, 2, TQ), lambda b, i: (0, 0, b * nq + i))],
        out_shape=[jax.ShapeDtypeStruct((t_tok, 512), BF16), jax.ShapeDtypeStruct((N_HEADS // 2, 2, t_tok), F32)],
        scratch_shapes=[pltpu.VMEM((2, N_HEADS, 1, TQ), F32), pltpu.VMEM((N_HEADS, HEAD_DIM, TQ), F32),
                        pltpu.VMEM((2, N_HEADS, TK, TQ), F32), pltpu.VMEM((2, N_HEADS, TK, TQ), BF16),
                        pltpu.VMEM((2, N_HEADS, 1, TQ), F32)],
        compiler_params=_cp(("arbitrary", "arbitrary"), VMEM_MID))(q_aug, k_aug, vt)


def _fox_bwd_t(q_aug, k_aug, kt, qkvg, du_b, b_out, lse, seq):
    t_tok = qkvg.shape[0]
    nq = seq // TQ
    nkb = seq // TK
    ratio = TQ // TK
    hg = 4

    def body(qa_ref, ka_ref, kt_ref, v_ref, do_ref, o_ref, lse_ref, dq_ref, dk_ref, dv_ref, df_ref,
             dqt_s, row_s, dfk_s, dk_s, dv_s, dfa_s, st_s, dp_s, pb_s, db_s):
        ones_b = jnp.ones((8, TQ), BF16)
        eye = _eye(HEAD_DIM, BF16)
        lane8 = lax.broadcasted_iota(jnp.int32, (8, 128), 1)
        lane_k = lax.broadcasted_iota(jnp.int32, (TK, 128), 1)
        first = [lane8 < HEAD_DIM, lane8 >= HEAD_DIM]
        for hh in range(hg):
            pp, e = divmod(hh, 2)
            head_lanes = jnp.where(first[e], 1.0, 0.0)
            for ii in range(nq):
                rows = slice(ii * TQ, (ii + 1) * TQ)
                prod = do_ref[rows, 128 * pp:128 * (pp + 1)].astype(F32) * o_ref[rows, 128 * pp:128 * (pp + 1)].astype(F32)
                row_s[hh, ii, 0] = _nt(head_lanes, prod, HI)
                row_s[hh, ii, 1] = jnp.broadcast_to(lse_ref[pp, e:e + 1, ii * TQ:(ii + 1) * TQ], (8, TQ))
                dqt_s[hh, ii] = jnp.zeros((128, TQ), F32)

        def kblock(j, _):
            krow = pl.multiple_of(j * TK, TK)
            spos = j * TK + lax.broadcasted_iota(jnp.int32, (TK, 1), 0)
            for hh in range(hg):
                dk_s[hh] = jnp.zeros((TK, 128), F32)
                dv_s[hh] = jnp.zeros((TK, 128), F32)
                dfa_s[hh] = jnp.zeros((8, TK), F32)

            def scores(i):
                qrow = pl.multiple_of(i * TQ, TQ)
                for hh in range(hg):
                    pp, e = divmod(hh, 2)
                    own = (lane_k < HEAD_DIM) if e == 0 else (lane_k >= HEAD_DIM)
                    v2 = v_ref[pl.ds(krow, TK), 128 * pp:128 * (pp + 1)]
                    vj = jnp.where(own, v2, jnp.zeros_like(v2))
                    st_s[i & 1, hh] = _nt(ka_ref[hh, pl.ds(krow, TK), :], qa_ref[hh, pl.ds(qrow, TQ), :])
                    dp_s[i & 1, hh] = _nt(vj, do_ref[pl.ds(qrow, TQ), 128 * pp:128 * (pp + 1)])

            def elementwise(i, masked):
                slot = i & 1
                if masked:
                    keep = spos <= (i * TQ + lax.broadcasted_iota(jnp.int32, (1, TQ), 1))
                for hh in range(hg):
                    pt = jnp.exp(st_s[slot, hh] - row_s[hh, i, 1][0:1, :])
                    if masked:
                        pt = jnp.where(keep, pt, 0.0)
                    dst = pt * (dp_s[slot, hh] - row_s[hh, i, 0][0:1, :])
                    pb_s[slot, hh] = pt.astype(BF16)
                    db_s[slot, hh] = dst.astype(BF16)

            def grads(i):
                slot = i & 1
                qrow = pl.multiple_of(i * TQ, TQ)
                for hh in range(hg):
                    pp = hh // 2
                    dst_b = db_s[slot, hh]
                    dv_s[hh] += _nn(pb_s[slot, hh], do_ref[pl.ds(qrow, TQ), 128 * pp:128 * (pp + 1)])
                    dk_s[hh] += _nn(dst_b, qa_ref[hh, pl.ds(qrow, TQ), :])
                    dqt_s[hh, i] += _nn(kt_ref[hh, j], dst_b)
                    dfa_s[hh] += _nt(ones_b, dst_b)

            def step(i, carry):
                grads(i - 1)
                elementwise(i, False)
                scores(jnp.minimum(i + 1, nq - 1))
                return carry

            i0 = j // ratio
            scores(i0)
            elementwise(i0, True)
            scores(jnp.minimum(i0 + 1, nq - 1))
            lax.fori_loop(i0 + 1, nq, step, 0)
            grads(nq - 1)
            for pp in range(hg // 2):
                cols = slice(128 * pp, 128 * (pp + 1))
                dk_ref[pl.ds(krow, TK), cols] = jnp.concatenate(
                    [dk_s[2 * pp][:, :HEAD_DIM], dk_s[2 * pp + 1][:, :HEAD_DIM]], axis=1).astype(BF16)
                dv_ref[pl.ds(krow, TK), cols] = jnp.where(lane_k < HEAD_DIM, dv_s[2 * pp], dv_s[2 * pp + 1]).astype(BF16)
            for hh in range(hg):
                dfk_s[hh, j] = dfa_s[hh]
            return 0

        lax.fori_loop(0, nkb, kblock, 0)
        for pp in range(hg // 2):
            for ii in range(nq):
                parts = []
                for e in range(2):
                    dqt = dqt_s[2 * pp + e, ii]
                    parts.append(_tn(dqt[0:HEAD_DIM, :].astype(BF16), eye) * SCALE)
                    for kk in range(ratio):
                        jj = ii * ratio + kk
                        df_ref[pp, e:e + 1, jj * TK:(jj + 1) * TK] = (dqt[67:68, kk * TK:(kk + 1) * TK]
                                                                     - dfk_s[2 * pp + e, jj][0:1, :])
                dq_ref[ii * TQ:(ii + 1) * TQ, 128 * pp:128 * (pp + 1)] = jnp.concatenate(parts, axis=1).astype(BF16)

    aug_blk = pl.BlockSpec((hg, seq, 128), lambda b, g: (g, b, 0))
    pair_blk = pl.BlockSpec((seq, 64 * hg), lambda b, g: (b, g))
    row_blk = pl.BlockSpec((hg // 2, 2, seq), lambda b, g: (g, 0, b))
    return _pallas(
        body, name="fox_bwd", grid=(t_tok // seq, N_HEADS // hg),
        in_specs=[aug_blk, aug_blk, pl.BlockSpec((hg, nkb, 128, TK), lambda b, g: (g, b, 0, 0)),
                  pl.BlockSpec((seq, 64 * hg), lambda b, g: (b, C_BV // (64 * hg) + g)), pair_blk, pair_blk, row_blk],
        out_specs=[pair_blk, pair_blk, pair_blk, row_blk],
        out_shape=[jax.ShapeDtypeStruct((t_tok, 512), BF16)] * 3
        + [jax.ShapeDtypeStruct((N_HEADS // 2, 2, t_tok), F32)],
        scratch_shapes=[pltpu.VMEM((hg, nq, 128, TQ), F32), pltpu.VMEM((hg, nq, 2, 8, TQ), F32),
                        pltpu.VMEM((hg, nkb, 8, TK), F32), pltpu.VMEM((hg, TK, 128), F32),
                        pltpu.VMEM((hg, TK, 128), F32), pltpu.VMEM((hg, 8, TK), F32),
                        pltpu.VMEM((2, hg, TK, TQ), F32), pltpu.VMEM((2, hg, TK, TQ), F32),
                        pltpu.VMEM((2, hg, TK, TQ), BF16), pltpu.VMEM((2, hg, TK, TQ), BF16)],
        compiler_params=_cp(("arbitrary", "arbitrary"), VMEM_BIG))(q_aug, k_aug, kt, qkvg, du_b, b_out, lse)


def _fox_bwd_t_old(q_aug, k_aug, kt, qkvg, du_b, b_out, lse, seq):
    t_tok = qkvg.shape[0]
    nq = seq // TQ
    nkb = seq // TK
    ratio = TQ // TK

    def body(qa_ref, ka_ref, kt_ref, v_ref, do_ref, o_ref, lse_ref, dq_ref, dk_ref, dv_ref, df_ref,
             dqt_s, out_s, row_s, dfk_s):
        ones_b = jnp.ones((8, TQ), BF16)
        ones_f = jnp.ones((8, HEAD_DIM), F32)
        eye = _eye(HEAD_DIM, BF16)
        for e in range(2):
            lo, hi = e * HEAD_DIM, (e + 1) * HEAD_DIM
            for ii in range(nq):
                rows = slice(ii * TQ, (ii + 1) * TQ)
                do = do_ref[rows, :][:, lo:hi].astype(F32)
                ov = o_ref[rows, :][:, lo:hi].astype(F32)
                row_s[ii, 0] = _nt(ones_f, do * ov, HI)
                row_s[ii, 1] = jnp.broadcast_to(lse_ref[e:e + 1, ii * TQ:(ii + 1) * TQ], (8, TQ))
                dqt_s[ii] = jnp.zeros((128, TQ), F32)

            def kblock(j, _):
                krow = pl.multiple_of(j * TK, TK)
                kj = ka_ref[e, pl.ds(krow, TK), :]
                ktj = kt_ref[e, j]
                vj = v_ref[pl.ds(krow, TK), :][:, lo:hi]
                spos = j * TK + lax.broadcasted_iota(jnp.int32, (TK, 1), 0)

                def qblock(i, carry, masked):
                    dk_acc, dv_acc, dfk = carry
                    qrow = pl.multiple_of(i * TQ, TQ)
                    qa = qa_ref[e, pl.ds(qrow, TQ), :]
                    doh = do_ref[pl.ds(qrow, TQ), :][:, lo:hi]
                    pt = jnp.exp(_nt(kj, qa) - row_s[i, 1][0:1, :])
                    if masked:
                        tpos = i * TQ + lax.broadcasted_iota(jnp.int32, (1, TQ), 1)
                        pt = jnp.where(spos <= tpos, pt, 0.0)
                    dst = pt * (_nt(vj, doh) - row_s[i, 0][0:1, :])
                    dst_b = dst.astype(BF16)
                    dv_acc = dv_acc + _nn(pt.astype(BF16), doh)
                    dk_acc = dk_acc + _nn(dst_b, qa)
                    dqt_s[i] += _nn(ktj, dst_b)
                    dfk = dfk + _nt(ones_b, dst_b)
                    return dk_acc, dv_acc, dfk

                i0 = j // ratio
                carry = (jnp.zeros((TK, 128), F32), jnp.zeros((TK, HEAD_DIM), F32), jnp.zeros((8, TK), F32))
                carry = qblock(i0, carry, True)
                dk_acc, dv_acc, dfk = lax.fori_loop(i0 + 1, nq, functools.partial(qblock, masked=False), carry)
                out_s[1, e, pl.ds(krow, TK), :] = dk_acc[:, :HEAD_DIM]
                out_s[2, e, pl.ds(krow, TK), :] = dv_acc
                dfk_s[j] = dfk
                return 0

            lax.fori_loop(0, nkb, kblock, 0)
            for ii in range(nq):
                dqt = dqt_s[ii]
                out_s[0, e, ii * TQ:(ii + 1) * TQ, :] = _tn(dqt[0:HEAD_DIM, :].astype(BF16), eye) * SCALE
                for kk in range(ratio):
                    jj = ii * ratio + kk
                    df_ref[e:e + 1, jj * TK:(jj + 1) * TK] = dqt[67:68, kk * TK:(kk + 1) * TK] - dfk_s[jj][0:1, :]
        for k, ref in enumerate((dq_ref, dk_ref, dv_ref)):
            ref[...] = jnp.concatenate([out_s[k, 0], out_s[k, 1]], axis=1).astype(BF16)

    aug_blk = pl.BlockSpec((2, seq, 128), lambda b, p: (p, b, 0))
    pair_blk = pl.BlockSpec((seq, 128), lambda b, p: (b, p))
    row_blk = pl.BlockSpec((None, 2, seq), lambda b, p: (p, 0, b))
    return _pallas(
        body, name="fox_bwd", grid=(t_tok // seq, N_HEADS // 2),
        in_specs=[aug_blk, aug_blk, pl.BlockSpec((2, nkb, 128, TK), lambda b, p: (p, b, 0, 0)),
                  pl.BlockSpec((seq, 128), lambda b, p: (b, C_BV // 128 + p)), pair_blk, pair_blk, row_blk],
        out_specs=[pair_blk, pair_blk, pair_blk, row_blk],
        out_shape=[jax.ShapeDtypeStruct((t_tok, 512), BF16)] * 3
        + [jax.ShapeDtypeStruct((N_HEADS // 2, 2, t_tok), F32)],
        scratch_shapes=[pltpu.VMEM((nq, 128, TQ), F32), pltpu.VMEM((3, 2, seq, HEAD_DIM), F32),
                        pltpu.VMEM((nq, 2, 8, TQ), F32), pltpu.VMEM((nkb, 8, TK), F32)],
        compiler_params=_cp(("arbitrary", "arbitrary"), VMEM_BIG))(q_aug, k_aug, kt, qkvg, du_b, b_out, lse)


def _fox_fwd(qkvg, f_row, f_col, seq):
    t_tok = qkvg.shape[0]
    nq = seq // TQ

    def body(q_ref, k_ref, v_ref, fr_ref, fc_ref, o_ref, lse_ref, fk_s):
        i = pl.program_id(1)
        for jj in range(nq):
            fk_s[jj] = fr_ref[:, jj * TQ:(jj + 1) * TQ]
        fcol = fc_ref[...]
        tpos = i * TQ + lax.broadcasted_iota(jnp.int32, (TQ, 1), 0)
        lane = lax.broadcasted_iota(jnp.int32, (TQ, 128), 1)
        lse_tile = jnp.zeros((TQ, 128), F32)
        for p in range(N_HEADS // 2):
            q2 = q_ref[:, 128 * p:128 * (p + 1)]
            qs = [q2[:, :HEAD_DIM], q2[:, HEAD_DIM:]]
            fqs = [_col(fcol, 2 * p + e) for e in range(2)]

            def kblock(j, carry):
                row0 = pl.multiple_of(j * TQ, TQ)
                k2 = k_ref[pl.ds(row0, TQ), 128 * p:128 * (p + 1)]
                v2 = v_ref[pl.ds(row0, TQ), 128 * p:128 * (p + 1)]
                fk8 = fk_s[j]
                spos = j * TQ + lax.broadcasted_iota(jnp.int32, (1, TQ), 1)
                keep = spos <= tpos
                new = []
                for e in range(2):
                    m, l, acc = carry[3 * e:3 * e + 3]
                    kh = k2[:, e * HEAD_DIM:(e + 1) * HEAD_DIM]
                    vh = v2[:, e * HEAD_DIM:(e + 1) * HEAD_DIM]
                    s = _nt(qs[e], kh) * SCALE + (fqs[e] - fk8[2 * p + e:2 * p + e + 1, :])
                    s = jnp.where(keep, s, NEG)
                    m_new = jnp.maximum(m, jnp.max(s, axis=1, keepdims=True))
                    alpha = jnp.exp(m - m_new)
                    pe = jnp.exp(s - m_new)
                    l = alpha * l + jnp.sum(pe, axis=1, keepdims=True)
                    acc = alpha * acc + _nn(pe.astype(BF16), vh)
                    new += [m_new, l, acc]
                return tuple(new)

            init = (jnp.full((TQ, 1), NEG, F32), jnp.zeros((TQ, 1), F32), jnp.zeros((TQ, HEAD_DIM), F32)) * 2
            res = lax.fori_loop(0, i + 1, kblock, init)
            outs = []
            for e in range(2):
                m, l, acc = res[3 * e:3 * e + 3]
                outs.append(acc / l)
                lse_tile = jnp.where(lane == 2 * p + e, m + jnp.log(l), lse_tile)
            o_ref[:, 128 * p:128 * (p + 1)] = jnp.concatenate(outs, axis=1).astype(BF16)
        lse_ref[...] = lse_tile

    return _pallas(
        body, name="fox_fwd", grid=(t_tok // seq, nq),
        in_specs=[pl.BlockSpec((TQ, 512), lambda b, i: (b * nq + i, C_BQ // 512)),
                  pl.BlockSpec((seq, 512), lambda b, i: (b, C_BK // 512)),
                  pl.BlockSpec((seq, 512), lambda b, i: (b, C_BV // 512)),
                  pl.BlockSpec((N_HEADS, seq), lambda b, i: (0, b)),
                  pl.BlockSpec((TQ, 128), lambda b, i: (b * nq + i, 0))],
        out_specs=[pl.BlockSpec((TQ, 512), lambda b, i: (b * nq + i, 0)),
                   pl.BlockSpec((TQ, 128), lambda b, i: (b * nq + i, 0))],
        out_shape=[jax.ShapeDtypeStruct((t_tok, 512), BF16), jax.ShapeDtypeStruct((t_tok, 128), F32)],
        scratch_shapes=[pltpu.VMEM((nq, N_HEADS, TQ), F32)],
        compiler_params=_cp(("arbitrary", "arbitrary"), VMEM_MID))(qkvg, qkvg, qkvg, f_row, f_col)


def _fox_bwd(qkvg, du_b, b_out, lse, f_row, f_col, seq):
    t_tok = qkvg.shape[0]
    nq = seq // TQ

    def body(q_ref, k_ref, v_ref, do_ref, o_ref, lse_ref, fr_ref, fc_ref,
             dq_ref, dk_ref, dv_ref, df_ref, dq_s, dk_s, dv_s, col_s, df_s, fk_s):
        p = pl.program_id(1)
        for jj in range(nq):
            fk_s[jj] = fr_ref[:, jj * TQ:(jj + 1) * TQ]
        eye = (lax.broadcasted_iota(jnp.int32, (TQ, TQ), 0) == lax.broadcasted_iota(jnp.int32, (TQ, TQ), 1)).astype(F32)
        for e in range(2):
            h = 2 * p + e
            lo, hi = e * HEAD_DIM, (e + 1) * HEAD_DIM
            for ii in range(nq):
                rows = slice(ii * TQ, (ii + 1) * TQ)
                do = do_ref[rows, :][:, lo:hi].astype(F32)
                ov = o_ref[rows, :][:, lo:hi].astype(F32)
                col_s[0, rows, :] = jnp.sum(do * ov, axis=1, keepdims=True)
                col_s[1, rows, :] = _col(lse_ref[rows, :], h)
                col_s[2, rows, :] = _col(fc_ref[rows, :], h)
                dq_s[rows, :] = jnp.zeros((TQ, HEAD_DIM), F32)
                df_s[ii] = jnp.zeros((8, TQ), F32)
                col_s[3, rows, :] = jnp.zeros((TQ, 1), F32)

            def kblock(j, _):
                krow = pl.multiple_of(j * TQ, TQ)
                kh = k_ref[pl.ds(krow, TQ), :][:, lo:hi]
                vh = v_ref[pl.ds(krow, TQ), :][:, lo:hi]
                fk = _row(fk_s[j], h)
                spos = j * TQ + lax.broadcasted_iota(jnp.int32, (1, TQ), 1)

                def qblock(i, carry):
                    dk_acc, dv_acc, dfk = carry
                    qrow = pl.multiple_of(i * TQ, TQ)
                    qh = q_ref[pl.ds(qrow, TQ), :][:, lo:hi]
                    doh = do_ref[pl.ds(qrow, TQ), :][:, lo:hi]
                    delta = col_s[0, pl.ds(qrow, TQ), :]
                    lse_q = col_s[1, pl.ds(qrow, TQ), :]
                    fq = col_s[2, pl.ds(qrow, TQ), :]
                    tpos = i * TQ + lax.broadcasted_iota(jnp.int32, (TQ, 1), 0)
                    s = _nt(qh, kh) * SCALE + (fq - fk)
                    pr = jnp.where(spos <= tpos, jnp.exp(s - lse_q), 0.0)
                    dp = _nt(doh, vh)
                    ds = pr * (dp - delta)
                    ds_b = ds.astype(BF16)
                    dv_acc = dv_acc + _tn(pr.astype(BF16), doh)
                    dk_acc = dk_acc + _tn(ds_b, qh)
                    dq_s[pl.ds(qrow, TQ), :] += _nn(ds_b, kh)
                    col_s[3, pl.ds(qrow, TQ), :] += jnp.sum(ds, axis=1, keepdims=True)
                    dfk = dfk + jnp.sum(ds, axis=0, keepdims=True)
                    return dk_acc, dv_acc, dfk

                zero = jnp.zeros((TQ, HEAD_DIM), F32)
                dk_acc, dv_acc, dfk = lax.fori_loop(j, nq, qblock, (zero, zero, jnp.zeros((1, TQ), F32)))
                dk_s[e, pl.ds(krow, TQ), :] = dk_acc * SCALE
                dv_s[e, pl.ds(krow, TQ), :] = dv_acc
                df_s[j] -= jnp.broadcast_to(dfk, (8, TQ))
                return 0

            lax.fori_loop(0, nq, kblock, 0)
            dq_s2 = dq_s[...] * SCALE
            dk_s[2 + e] = dq_s2
            for ii in range(nq):
                dfq = jnp.broadcast_to(col_s[3, ii * TQ:(ii + 1) * TQ, :], (TQ, 128))
                df_ref[e:e + 1, ii * TQ:(ii + 1) * TQ] = _tn(dfq, eye, HI)[0:1, :] + df_s[ii][0:1, :]
        dq_ref[...] = jnp.concatenate([dk_s[2], dk_s[3]], axis=1).astype(BF16)
        dk_ref[...] = jnp.concatenate([dk_s[0], dk_s[1]], axis=1).astype(BF16)
        dv_ref[...] = jnp.concatenate([dv_s[0], dv_s[1]], axis=1).astype(BF16)

    blk = lambda off: pl.BlockSpec((seq, 128), lambda b, p: (b, off // 128 + p))
    out_blk = pl.BlockSpec((seq, 128), lambda b, p: (b, p))
    return _pallas(
        body, name="fox_bwd", grid=(t_tok // seq, N_HEADS // 2),
        in_specs=[blk(C_BQ), blk(C_BK), blk(C_BV), out_blk, out_blk,
                  pl.BlockSpec((seq, 128), lambda b, p: (b, 0)),
                  pl.BlockSpec((N_HEADS, seq), lambda b, p: (0, b)),
                  pl.BlockSpec((seq, 128), lambda b, p: (b, 0))],
        out_specs=[out_blk, out_blk, out_blk, pl.BlockSpec((None, 2, seq), lambda b, p: (p, 0, b))],
        out_shape=[jax.ShapeDtypeStruct((t_tok, 512), BF16)] * 3
        + [jax.ShapeDtypeStruct((N_HEADS // 2, 2, t_tok), F32)],
        scratch_shapes=[pltpu.VMEM((seq, HEAD_DIM), F32), pltpu.VMEM((4, seq, HEAD_DIM), F32),
                        pltpu.VMEM((2, seq, HEAD_DIM), F32), pltpu.VMEM((4, seq, 1), F32),
                        pltpu.VMEM((nq, 8, TQ), F32), pltpu.VMEM((nq, N_HEADS, TQ), F32)],
        compiler_params=_cp(("arbitrary", "arbitrary"), VMEM_BIG))(qkvg, qkvg, qkvg, du_b, b_out, lse, f_row, f_col)


def _swa_scores(q_ref, kp, kc, bias_ref, h, mask_p, mask_c):
    q2 = q_ref[:, 128 * (h // 2):128 * (h // 2 + 1)]
    qh = q2[:, (h % 2) * HEAD_DIM:(h % 2 + 1) * HEAD_DIM]
    hk = h // KV_GROUP
    kph = kp[:, hk * HEAD_DIM:(hk + 1) * HEAD_DIM]
    kch = kc[:, hk * HEAD_DIM:(hk + 1) * HEAD_DIM]
    bias = bias_ref[h]
    sp = jnp.where(mask_p, _nt(qh, kph) * SCALE + bias[:, :BLOCK], NEG)
    sc = jnp.where(mask_c, _nt(qh, kch) * SCALE + bias[:, BLOCK:], NEG)
    return qh, kph, kch, sp, sc


def _swa_masks(n):
    ti = lax.broadcasted_iota(jnp.int32, (BLOCK, BLOCK), 0)
    sj = lax.broadcasted_iota(jnp.int32, (BLOCK, BLOCK), 1)
    return jnp.logical_and(sj > ti, n > 0), sj <= ti


def _swa_fwd(qkvg, bias, sinks, seq):
    t_tok = qkvg.shape[0]
    nb = seq // BLOCK

    def body(sink_ref, q_ref, k_ref, v_ref, bias_ref, o_ref, lse_ref):
        n = pl.program_id(1)
        prev = pl.multiple_of(jnp.maximum(n - 1, 0) * BLOCK, BLOCK)
        cur = pl.multiple_of(n * BLOCK, BLOCK)
        kp, kc = k_ref[pl.ds(prev, BLOCK), :], k_ref[pl.ds(cur, BLOCK), :]
        vp, vc = v_ref[pl.ds(prev, BLOCK), :], v_ref[pl.ds(cur, BLOCK), :]
        mask_p, mask_c = _swa_masks(n)
        lane = lax.broadcasted_iota(jnp.int32, (BLOCK, 128), 1)
        lse_tile = jnp.zeros((BLOCK, 128), F32)
        outs = []
        for h in range(N_HEADS):
            hk = h // KV_GROUP
            _, _, _, sp, sc = _swa_scores(q_ref, kp, kc, bias_ref, h, mask_p, mask_c)
            sink = sink_ref[h]
            m = jnp.maximum(jnp.maximum(jnp.max(sp, axis=1, keepdims=True), jnp.max(sc, axis=1, keepdims=True)), sink)
            pp, pc = jnp.exp(sp - m), jnp.exp(sc - m)
            den = jnp.sum(pp, axis=1, keepdims=True) + jnp.sum(pc, axis=1, keepdims=True) + jnp.exp(sink - m)
            acc = (_nn(pp.astype(BF16), vp[:, hk * HEAD_DIM:(hk + 1) * HEAD_DIM])
                   + _nn(pc.astype(BF16), vc[:, hk * HEAD_DIM:(hk + 1) * HEAD_DIM]))
            outs.append(acc / den)
            lse_tile = jnp.where(lane == h, m + jnp.log(den), lse_tile)
            if h % 2 == 1:
                o_ref[:, 128 * (h // 2):128 * (h // 2 + 1)] = jnp.concatenate(outs, axis=1).astype(BF16)
                outs = []
        lse_ref[...] = lse_tile

    return _pallas(
        body, name="swa_fwd", grid=(t_tok // seq, nb),
        in_specs=[pl.BlockSpec(memory_space=pltpu.SMEM),
                  pl.BlockSpec((BLOCK, 512), lambda b, n: (b * nb + n, C_AQ // 512)),
                  pl.BlockSpec((seq, 128), lambda b, n: (b, C_AK // 128)),
                  pl.BlockSpec((seq, 128), lambda b, n: (b, C_AV // 128)),
                  pl.BlockSpec((N_HEADS, BLOCK, 2 * BLOCK), lambda b, n: (0, 0, 0))],
        out_specs=[pl.BlockSpec((BLOCK, 512), lambda b, n: (b * nb + n, 0)),
                   pl.BlockSpec((BLOCK, 128), lambda b, n: (b * nb + n, 0))],
        out_shape=[jax.ShapeDtypeStruct((t_tok, 512), BF16), jax.ShapeDtypeStruct((t_tok, 128), F32)],
        compiler_params=_cp(("arbitrary", "arbitrary"), VMEM_MID))(sinks, qkvg, qkvg, qkvg, bias)


def _swa_bwd(qkvg, du_a, a_out, lse, bias, sinks, seq):
    t_tok = qkvg.shape[0]
    nb = seq // BLOCK

    def body(sink_ref, q_ref, k_ref, v_ref, do_ref, o_ref, lse_ref, bias_ref,
             dq_ref, dkv_ref, dbias_ref, dsink_ref, kv_s):
        b, n = pl.program_id(0), pl.program_id(1)

        @pl.when(jnp.logical_and(b == 0, n == 0))
        def _():
            dbias_ref[...] = jnp.zeros_like(dbias_ref)
            dsink_ref[...] = jnp.zeros_like(dsink_ref)

        @pl.when(n == 0)
        def _():
            kv_s[...] = jnp.zeros_like(kv_s)

        prev = pl.multiple_of(jnp.maximum(n - 1, 0) * BLOCK, BLOCK)
        cur = pl.multiple_of(n * BLOCK, BLOCK)
        kp, kc = k_ref[pl.ds(prev, BLOCK), :], k_ref[pl.ds(cur, BLOCK), :]
        vp, vc = v_ref[pl.ds(prev, BLOCK), :], v_ref[pl.ds(cur, BLOCK), :]
        mask_p, mask_c = _swa_masks(n)
        lse_tile = lse_ref[...]
        dqs = []
        acc = [[jnp.zeros((BLOCK, HEAD_DIM), F32) for _ in range(4)] for _ in range(2)]
        for h in range(N_HEADS):
            hk = h // KV_GROUP
            qh, kph, kch, sp, sc = _swa_scores(q_ref, kp, kc, bias_ref, h, mask_p, mask_c)
            lo, hi = (h % 2) * HEAD_DIM, (h % 2 + 1) * HEAD_DIM
            do2 = do_ref[:, 128 * (h // 2):128 * (h // 2 + 1)]
            o2 = o_ref[:, 128 * (h // 2):128 * (h // 2 + 1)]
            doh = do2[:, lo:hi]
            delta = jnp.sum(doh.astype(F32) * o2[:, lo:hi].astype(F32), axis=1, keepdims=True)
            lse_h = _col(lse_tile, h)
            pp, pc = jnp.exp(sp - lse_h), jnp.exp(sc - lse_h)
            vph = vp[:, hk * HEAD_DIM:(hk + 1) * HEAD_DIM]
            vch = vc[:, hk * HEAD_DIM:(hk + 1) * HEAD_DIM]
            dsp = pp * (_nt(doh, vph) - delta)
            dsc = pc * (_nt(doh, vch) - delta)
            dbias_ref[h, :, :BLOCK] += dsp
            dbias_ref[h, :, BLOCK:] += dsc
            psink = jnp.exp(sink_ref[h] - lse_h)
            dsink_ref[h:h + 1, :] += jnp.broadcast_to(jnp.sum(-psink * delta, axis=0, keepdims=True), (1, 128))
            dsp_b, dsc_b = dsp.astype(BF16), dsc.astype(BF16)
            dqs.append((_nn(dsp_b, kph) + _nn(dsc_b, kch)) * SCALE)
            acc[hk][0] = acc[hk][0] + _tn(dsp_b, qh)
            acc[hk][1] = acc[hk][1] + _tn(dsc_b, qh)
            acc[hk][2] = acc[hk][2] + _tn(pp.astype(BF16), doh)
            acc[hk][3] = acc[hk][3] + _tn(pc.astype(BF16), doh)
            if h % 2 == 1:
                dq_ref[:, 128 * (h // 2):128 * (h // 2 + 1)] = jnp.concatenate(dqs, axis=1).astype(BF16)
                dqs = []
        upd_p = jnp.concatenate([acc[0][0] * SCALE, acc[1][0] * SCALE, acc[0][2], acc[1][2]], axis=1)
        upd_c = jnp.concatenate([acc[0][1] * SCALE, acc[1][1] * SCALE, acc[0][3], acc[1][3]], axis=1)
        kv_s[pl.ds(prev, BLOCK), :] += upd_p
        kv_s[pl.ds(cur, BLOCK), :] += upd_c

        @pl.when(n == nb - 1)
        def _():
            dkv_ref[...] = kv_s[...].astype(BF16)

    return _pallas(
        body, name="swa_bwd", grid=(t_tok // seq, nb),
        in_specs=[pl.BlockSpec(memory_space=pltpu.SMEM),
                  pl.BlockSpec((BLOCK, 512), lambda b, n: (b * nb + n, C_AQ // 512)),
                  pl.BlockSpec((seq, 128), lambda b, n: (b, C_AK // 128)),
                  pl.BlockSpec((seq, 128), lambda b, n: (b, C_AV // 128)),
                  pl.BlockSpec((BLOCK, 512), lambda b, n: (b * nb + n, 0)),
                  pl.BlockSpec((BLOCK, 512), lambda b, n: (b * nb + n, 0)),
                  pl.BlockSpec((BLOCK, 128), lambda b, n: (b * nb + n, 0)),
                  pl.BlockSpec((N_HEADS, BLOCK, 2 * BLOCK), lambda b, n: (0, 0, 0))],
        out_specs=[pl.BlockSpec((BLOCK, 512), lambda b, n: (b * nb + n, 0)),
                   pl.BlockSpec((seq, 256), lambda b, n: (b, 0)),
                   pl.BlockSpec((N_HEADS, BLOCK, 2 * BLOCK), lambda b, n: (0, 0, 0)),
                   pl.BlockSpec((N_HEADS, 128), lambda b, n: (0, 0))],
        out_shape=[jax.ShapeDtypeStruct((t_tok, 512), BF16), jax.ShapeDtypeStruct((t_tok, 256), BF16),
                   jax.ShapeDtypeStruct((N_HEADS, BLOCK, 2 * BLOCK), F32), jax.ShapeDtypeStruct((N_HEADS, 128), F32)],
        scratch_shapes=[pltpu.VMEM((seq, 256), F32)],
        compiler_params=_cp(("arbitrary", "arbitrary"), VMEM_MID))(sinks, qkvg, qkvg, qkvg, du_a, a_out, lse, bias)


def _out_proj(name, u_parts, gate_arr, gate_blk, w_out, x, gmod, seq):
    t_tok = x.shape[0]
    nu = len(u_parts)

    def body(*refs):
        u_refs = refs[:nu]
        g_ref, w_ref, x_ref, gm_ref, yg_ref, y_ref, xn_ref = refs[nu:]
        u = jnp.concatenate([r[...].astype(F32) for r in u_refs], axis=1) if nu > 1 else u_refs[0][...].astype(F32)
        yg = (u * _silu(g_ref[...].astype(F32))).astype(BF16)
        yg_ref[...] = yg
        y = _nn(yg, w_ref[...])
        y_ref[...] = y.astype(BF16)
        xn_ref[...] = x_ref[...] + gm_ref[...] * y

    row = lambda w: pl.BlockSpec((TM, w), lambda i: (i, 0))
    in_specs = [row(u.shape[1]) for u in u_parts]
    in_specs += [pl.BlockSpec((TM, D_MODEL), lambda i: (i, gate_blk)),
                 pl.BlockSpec((D_MODEL, D_MODEL), lambda i: (0, 0)), row(D_MODEL),
                 pl.BlockSpec((None, 1, D_MODEL), lambda i: (i * TM // seq, 0, 0))]
    return _pallas(
        body, name=name, grid=(t_tok // TM,), in_specs=in_specs,
        out_specs=[row(D_MODEL)] * 3,
        out_shape=[jax.ShapeDtypeStruct((t_tok, D_MODEL), BF16)] * 2 + [jax.ShapeDtypeStruct((t_tok, D_MODEL), F32)],
        compiler_params=_cp(("arbitrary",), VMEM_MID))(*u_parts, gate_arr, w_out, x, gmod)


def _out_proj_bwd(name, dxn, gmod, y, w_out, seq, attn=None):
    t_tok = dxn.shape[0]
    tiles_per_seq = seq // TM

    def body(*refs):
        if attn is None:
            dxn_ref, gm_ref, y_ref, w_ref, dy_ref, dgm_ref, dyg_ref = refs
        else:
            dxn_ref, gm_ref, y_ref, w_ref, a_ref, b_ref, g_ref, dy_ref, dgm_ref, dua_ref, dub_ref, dg_ref = refs
        i = pl.program_id(0)
        dxv = dxn_ref[...]
        dy = (dxv * gm_ref[...]).astype(BF16)
        dy_ref[...] = dy

        @pl.when(i % tiles_per_seq == 0)
        def _():
            dgm_ref[...] = jnp.zeros_like(dgm_ref)

        dgm_ref[...] += jnp.sum(dxv * y_ref[...].astype(F32), axis=0, keepdims=True)
        dyg = _nt(dy, w_ref[...])
        if attn is None:
            dyg_ref[...] = dyg
        else:
            gt = g_ref[...].astype(F32)
            du = dyg * _silu(gt)
            dua_ref[...] = du[:, :512].astype(BF16)
            dub_ref[...] = du[:, 512:].astype(BF16)
            u = jnp.concatenate([a_ref[...].astype(F32), b_ref[...].astype(F32)], axis=1)
            dg_ref[...] = (dyg * u * _dsilu(gt)).astype(BF16)

    row = lambda w: pl.BlockSpec((TM, w), lambda i: (i, 0))
    mod_spec = pl.BlockSpec((None, 1, D_MODEL), lambda i: (i * TM // seq, 0, 0))
    in_specs = [row(D_MODEL), mod_spec, row(D_MODEL), pl.BlockSpec((D_MODEL, D_MODEL), lambda i: (0, 0))]
    out_specs = [row(D_MODEL), mod_spec]
    out_shape = [jax.ShapeDtypeStruct((t_tok, D_MODEL), BF16), jax.ShapeDtypeStruct(gmod.shape, F32)]
    args = [dxn, gmod, y, w_out]
    if attn is None:
        out_specs.append(row(D_MODEL))
        out_shape.append(jax.ShapeDtypeStruct((t_tok, D_MODEL), F32))
    else:
        in_specs += [row(512), row(512), pl.BlockSpec((TM, D_MODEL), lambda i: (i, C_GATE // D_MODEL))]
        out_specs += [row(512), row(512), row(D_MODEL)]
        out_shape += [jax.ShapeDtypeStruct((t_tok, 512), BF16)] * 2 + [jax.ShapeDtypeStruct((t_tok, D_MODEL), BF16)]
        args += list(attn)
    return _pallas(body, name=name, grid=(t_tok // TM,), in_specs=in_specs, out_specs=out_specs,
                   out_shape=out_shape, compiler_params=_cp(("arbitrary",), VMEM_MID))(*args)


def _norm_bwd(name, parts, w, x, g, scale, dxn, seq, rows_part=None):
    t_tok = x.shape[0]
    npart = len(parts)
    w3d = w.ndim == 3
    tiles_per_seq = seq // TM
    nrow_in = 0 if rows_part is None else 2

    def body(*refs):
        p_refs = refs[:npart]
        w_ref, x_ref, g_ref, sc_ref, dxn_ref = refs[npart:npart + 5]
        dx_ref, dss_ref, dg_ref = refs[npart + 5 + nrow_in:]
        i = pl.program_id(0)
        dh = jnp.zeros((TM, D_MODEL), F32)
        if rows_part is not None:
            r_ref, wr_ref = refs[npart + 5:npart + 7]
            dh = dh + _tn(r_ref[...].astype(BF16), wr_ref[...])
        for (arr, off), p_ref in zip(parts, p_refs):
            width = arr.shape[1]
            for j in range(width // 256):
                pj = p_ref[:, j * 256:(j + 1) * 256]
                c0 = off + j * 256
                wj = w_ref[c0 // 256] if w3d else w_ref[:, c0:c0 + 256]
                dh = dh + _nt(pj, wj)
        xv = x_ref[...]
        rstd = lax.rsqrt(jnp.mean(xv * xv, axis=-1, keepdims=True) + EPS)
        xhat = xv * rstd
        gv = g_ref[...]
        nrm = xhat * gv

        @pl.when(i % tiles_per_seq == 0)
        def _():
            dss_ref[...] = jnp.zeros_like(dss_ref)

        @pl.when(i == 0)
        def _():
            dg_ref[...] = jnp.zeros_like(dg_ref)

        dss_ref[0:1, :] += jnp.sum(dh, axis=0, keepdims=True)
        dss_ref[1:2, :] += jnp.sum(dh * nrm, axis=0, keepdims=True)
        dn = dh * (1.0 + sc_ref[...])
        dg_ref[0:1, :] += jnp.sum(dn * xhat, axis=0, keepdims=True)
        dxhat = dn * gv
        dx_ref[...] = rstd * (dxhat - xhat * jnp.mean(dxhat * xhat, axis=-1, keepdims=True)) + dxn_ref[...]

    row = lambda wd: pl.BlockSpec((TM, wd), lambda i: (i, 0))
    w_spec = (pl.BlockSpec(w.shape, lambda i: (0, 0, 0)) if w3d else pl.BlockSpec(w.shape, lambda i: (0, 0)))
    in_specs = [row(a.shape[1]) for a, _ in parts]
    in_specs += [w_spec, row(D_MODEL), pl.BlockSpec((1, D_MODEL), lambda i: (0, 0)),
                 pl.BlockSpec((None, 1, D_MODEL), lambda i: (i * TM // seq, 0, 0)), row(D_MODEL)]
    args = [a for a, _ in parts] + [w, x, g, scale, dxn]
    if rows_part is not None:
        in_specs += [pl.BlockSpec((8, TM), lambda i: (0, i)), pl.BlockSpec((8, D_MODEL), lambda i: (0, 0))]
        args += list(rows_part)
    nseq = t_tok // seq
    return _pallas(
        body, name=name, grid=(t_tok // TM,), in_specs=in_specs,
        out_specs=[row(D_MODEL), pl.BlockSpec((None, 8, D_MODEL), lambda i: (i * TM // seq, 0, 0)),
                   pl.BlockSpec((8, D_MODEL), lambda i: (0, 0))],
        out_shape=[jax.ShapeDtypeStruct((t_tok, D_MODEL), F32), jax.ShapeDtypeStruct((nseq, 8, D_MODEL), F32),
                   jax.ShapeDtypeStruct((8, D_MODEL), F32)],
        compiler_params=_cp(("arbitrary",), VMEM_BIG))(*args)


def _dw(name, a, parts, blocked=None):
    t_tok, ka = a.shape
    tt = 512
    npart = len(parts)
    nt = t_tok // tt

    def body(*refs):
        a_ref = refs[0]
        p_refs = refs[1:1 + npart]
        o_refs = refs[1 + npart:1 + 2 * npart]
        acc_refs = refs[1 + 2 * npart:]
        t = pl.program_id(0)
        av = a_ref[...]
        for p_ref, acc in zip(p_refs, acc_refs):
            upd = _tn(av, p_ref[...])

            @pl.when(t == 0)
            def _():
                acc[...] = upd

            @pl.when(t > 0)
            def _():
                acc[...] += upd

        @pl.when(t == nt - 1)
        def _():
            for o_ref, acc in zip(o_refs, acc_refs):
                if blocked is None:
                    o_ref[...] = acc[...].astype(BF16)
                else:
                    for j in range(o_ref.shape[0]):
                        o_ref[j] = acc[:, j * blocked:(j + 1) * blocked].astype(BF16)

    in_specs = [pl.BlockSpec((tt, ka), lambda t: (t, 0))]
    in_specs += [pl.BlockSpec((tt, p.shape[1]), lambda t: (t, 0)) for p in parts]
    if blocked is None:
        out_shape = [jax.ShapeDtypeStruct((ka, p.shape[1]), BF16) for p in parts]
        out_specs = [pl.BlockSpec((ka, p.shape[1]), lambda t: (0, 0)) for p in parts]
    else:
        out_shape = [jax.ShapeDtypeStruct((p.shape[1] // blocked, ka, blocked), BF16) for p in parts]
        out_specs = [pl.BlockSpec((p.shape[1] // blocked, ka, blocked), lambda t: (0, 0, 0)) for p in parts]
    return _pallas(body, name=name, grid=(nt,), in_specs=in_specs, out_specs=out_specs, out_shape=out_shape,
                   scratch_shapes=[pltpu.VMEM((ka, p.shape[1]), F32) for p in parts],
                   compiler_params=_cp(("arbitrary",), VMEM_BIG))(a, *parts)


def _dw_rows(name, rows_t, h):
    t_tok = h.shape[0]
    tt = 512

    def body(r_ref, h_ref, o_ref):
        @pl.when(pl.program_id(0) == 0)
        def _():
            o_ref[...] = jnp.zeros_like(o_ref)

        o_ref[...] += _nn(r_ref[...].astype(BF16), h_ref[...])

    return _pallas(body, name=name, grid=(t_tok // tt,),
                   in_specs=[pl.BlockSpec((8, tt), lambda t: (0, t)), pl.BlockSpec((tt, D_MODEL), lambda t: (t, 0))],
                   out_specs=pl.BlockSpec((8, D_MODEL), lambda t: (0, 0)),
                   out_shape=jax.ShapeDtypeStruct((8, D_MODEL), F32),
                   compiler_params=_cp(("arbitrary",), VMEM_MID))(rows_t, h)


def _lru_gates(xc, blk, wa_ref, wx_ref, ba_ref, bx_ref, sp):
    cols = slice(blk * LRU_BLOCK_W, (blk + 1) * LRU_BLOCK_W)
    xb = xc[:, cols].astype(BF16)
    r = _sigmoid(_nn(xb, wa_ref[blk].astype(BF16)) + ba_ref[:, cols])
    ig = _sigmoid(_nn(xb, wx_ref[blk].astype(BF16)) + bx_ref[:, cols])
    log_a = -LRU_C * r * sp[:, cols]
    a = jnp.exp(log_a)
    mult = jnp.sqrt(_neg_expm1(2.0 * log_a))
    return xb, r, ig, a, mult


def _softplus_neg(lam):
    return jnp.maximum(-lam, 0.0) + jnp.log(1.0 + jnp.exp(-jnp.abs(lam)))


def _conv_taps(xe_ref, cw_ref, cb_ref):
    xc = cb_ref[...] + xe_ref[8:8 + TC, :] * cw_ref[3:4, :]
    for k in range(1, 4):
        xc = xc + xe_ref[8 - k:8 - k + TC, :] * cw_ref[3 - k:4 - k, :]
    return xc


def _lru_fwd(proj, cw, cb, w_a, b_a, w_x, b_x, lam, seq):
    t_tok = proj.shape[0]
    nc = seq // TC

    def body(x_ref, cw_ref, cb_ref, wa_ref, ba_ref, wx_ref, bx_ref, lam_ref, hs_ref, xe_s, a_s, u_s, h_s):
        c = pl.program_id(1)

        @pl.when(c == 0)
        def _():
            xe_s[0:8, :] = jnp.zeros((8, D_MODEL), F32)
            h_s[...] = jnp.zeros_like(h_s)

        xe_s[8:8 + TC, :] = x_ref[...]
        xc = _conv_taps(xe_s, cw_ref, cb_ref)
        sp = _softplus_neg(lam_ref[...])
        for blk in range(LRU_BLOCKS):
            cols = slice(blk * LRU_BLOCK_W, (blk + 1) * LRU_BLOCK_W)
            _, _, ig, a, mult = _lru_gates(xc, blk, wa_ref, wx_ref, ba_ref, bx_ref, sp)
            a_s[:, cols] = a
            u_s[:, cols] = mult * ig * xc[:, cols]

        def step(t, h):
            h = a_s[pl.ds(t, 1), :] * h + u_s[pl.ds(t, 1), :]
            hs_ref[pl.ds(t, 1), :] = h
            return h

        h_s[0:1, :] = lax.fori_loop(0, TC, step, h_s[0:1, :], unroll=8)
        xe_s[0:8, :] = xe_s[TC:TC + 8, :]

    full = lambda shape: pl.BlockSpec(shape, lambda b, c: (0,) * len(shape))
    return _pallas(
        body, name="lru_fwd", grid=(t_tok // seq, nc),
        in_specs=[pl.BlockSpec((TC, D_MODEL), lambda b, c: (b * nc + c, 0)), full((4, D_MODEL)), full((1, D_MODEL)),
                  full((LRU_BLOCKS, LRU_BLOCK_W, LRU_BLOCK_W)), full((1, D_MODEL)),
                  full((LRU_BLOCKS, LRU_BLOCK_W, LRU_BLOCK_W)), full((1, D_MODEL)), full((1, D_MODEL))],
        out_specs=pl.BlockSpec((TC, D_MODEL), lambda b, c: (b * nc + c, 0)),
        out_shape=jax.ShapeDtypeStruct((t_tok, D_MODEL), F32),
        scratch_shapes=[pltpu.VMEM((TC + 8, D_MODEL), F32), pltpu.VMEM((TC, D_MODEL), F32),
                        pltpu.VMEM((TC, D_MODEL), F32), pltpu.VMEM((8, D_MODEL), F32)],
        compiler_params=_cp(("arbitrary", "arbitrary"), VMEM_MID))(proj, cw, cb, w_a, b_a, w_x, b_x, lam)


def _lru_bwd(proj, hs, dyh, cw, cb, w_a, b_a, w_x, b_x, lam, seq):
    t_tok = proj.shape[0]
    nc = seq // TC

    def body(x_ref, xh_ref, g_ref, hs_ref, hh_ref, dy_ref, cw_ref, cb_ref, wa_ref, ba_ref, wx_ref, bx_ref, lam_ref,
             dp_ref, dcw_ref, dvec_ref, dwa_ref, dwx_ref,
             xe_s, he_s, de_s, a_s, r_s, i_s, m_s, dh_s, carry_s):
        b, cr = pl.program_id(0), pl.program_id(1)
        c = nc - 1 - cr

        @pl.when(jnp.logical_and(b == 0, cr == 0))
        def _():
            dcw_ref[...] = jnp.zeros_like(dcw_ref)
            dvec_ref[...] = jnp.zeros_like(dvec_ref)
            dwa_ref[...] = jnp.zeros_like(dwa_ref)
            dwx_ref[...] = jnp.zeros_like(dwx_ref)

        @pl.when(cr == 0)
        def _():
            carry_s[...] = jnp.zeros_like(carry_s)
            de_s[TC:TC + 8, :] = jnp.zeros((8, D_MODEL), F32)

        first = c == 0
        xe_s[0:8, :] = jnp.where(first, 0.0, xh_ref[...])
        xe_s[8:8 + TC, :] = x_ref[...]
        he_s[0:8, :] = jnp.where(first, 0.0, hh_ref[...])
        he_s[8:8 + TC, :] = hs_ref[...]
        xc = _conv_taps(xe_s, cw_ref, cb_ref)
        lam_v = lam_ref[...]
        sp = _softplus_neg(lam_v)
        for blk in range(LRU_BLOCKS):
            cols = slice(blk * LRU_BLOCK_W, (blk + 1) * LRU_BLOCK_W)
            _, r, ig, a, mult = _lru_gates(xc, blk, wa_ref, wx_ref, ba_ref, bx_ref, sp)
            a_s[:, cols], r_s[:, cols], i_s[:, cols], m_s[:, cols] = a, r, ig, mult

        gt = g_ref[...]
        dyh = dy_ref[...]
        dh_s[...] = dyh * _silu(gt)
        dp_ref[:, D_MODEL:] = (dyh * hs_ref[...] * _dsilu(gt)).astype(BF16)

        def step(k, carry):
            t = TC - 1 - k
            dh = dh_s[pl.ds(t, 1), :] + carry
            dh_s[pl.ds(t, 1), :] = dh
            return a_s[pl.ds(t, 1), :] * dh

        carry_s[0:1, :] = lax.fori_loop(0, TC, step, carry_s[0:1, :], unroll=8)

        hprev = he_s[7:7 + TC, :]
        for blk in range(LRU_BLOCKS):
            cols = slice(blk * LRU_BLOCK_W, (blk + 1) * LRU_BLOCK_W)
            xcb = xc[:, cols]
            a, r, ig, mult, dh = a_s[:, cols], r_s[:, cols], i_s[:, cols], m_s[:, cols], dh_s[:, cols]
            spb = sp[:, cols]
            dmult = dh * ig * xcb
            di = dh * mult * xcb
            dxc = dh * mult * ig
            dla = dh * hprev[:, cols] * a - dmult * (a * a) / jnp.maximum(mult, 1e-20)
            dr = dla * (-LRU_C * spb)
            dsp = jnp.sum(dla * (-LRU_C * r), axis=0, keepdims=True)
            dga = dr * r * (1.0 - r)
            dgx = di * ig * (1.0 - ig)
            dga_b, dgx_b = dga.astype(BF16), dgx.astype(BF16)
            xb = xcb.astype(BF16)
            dxc = dxc + _nt(dga_b, wa_ref[blk].astype(BF16)) + _nt(dgx_b, wx_ref[blk].astype(BF16))
            dwa_ref[blk] += _tn(xb, dga_b)
            dwx_ref[blk] += _tn(xb, dgx_b)
            dvec_ref[1:2, cols] += jnp.sum(dga, axis=0, keepdims=True)
            dvec_ref[2:3, cols] += jnp.sum(dgx, axis=0, keepdims=True)
            dvec_ref[3:4, cols] += dsp * (-1.0 / (1.0 + jnp.exp(lam_v[:, cols])))
            de_s[0:TC, cols] = dxc

        dxc = de_s[0:TC, :]
        dvec_ref[0:1, :] += jnp.sum(dxc, axis=0, keepdims=True)
        dxr = dxc * cw_ref[3:4, :]
        dcw_ref[3:4, :] += jnp.sum(dxc * xe_s[8:8 + TC, :], axis=0, keepdims=True)
        for k in range(1, 4):
            dxr = dxr + de_s[k:k + TC, :] * cw_ref[3 - k:4 - k, :]
            dcw_ref[3 - k:4 - k, :] += jnp.sum(dxc * xe_s[8 - k:8 - k + TC, :], axis=0, keepdims=True)
        dp_ref[:, :D_MODEL] = dxr.astype(BF16)
        de_s[TC:TC + 8, :] = de_s[0:8, :]

    chunk = lambda col: pl.BlockSpec((TC, D_MODEL), lambda b, cr: (b * nc + nc - 1 - cr, col))
    halo = lambda col: pl.BlockSpec(
        (8, D_MODEL), lambda b, cr: (jnp.maximum((b * nc + nc - 1 - cr) * (TC // 8) - 1, 0), col))
    full = lambda shape: pl.BlockSpec(shape, lambda b, cr: (0,) * len(shape))
    wblk = (LRU_BLOCKS, LRU_BLOCK_W, LRU_BLOCK_W)
    return _pallas(
        body, name="lru_bwd", grid=(t_tok // seq, nc),
        in_specs=[chunk(0), halo(0), chunk(1), chunk(0), halo(0), chunk(0),
                  full((4, D_MODEL)), full((1, D_MODEL)), full(wblk), full((1, D_MODEL)), full(wblk),
                  full((1, D_MODEL)), full((1, D_MODEL))],
        out_specs=[pl.BlockSpec((TC, 2 * D_MODEL), lambda b, cr: (b * nc + nc - 1 - cr, 0)),
                   full((8, D_MODEL)), full((8, D_MODEL)), full(wblk), full(wblk)],
        out_shape=[jax.ShapeDtypeStruct((t_tok, 2 * D_MODEL), BF16), jax.ShapeDtypeStruct((8, D_MODEL), F32),
                   jax.ShapeDtypeStruct((8, D_MODEL), F32), jax.ShapeDtypeStruct(wblk, F32),
                   jax.ShapeDtypeStruct(wblk, F32)],
        scratch_shapes=[pltpu.VMEM((TC + 8, D_MODEL), F32), pltpu.VMEM((TC + 8, D_MODEL), F32),
                        pltpu.VMEM((TC + 8, D_MODEL), F32)]
        + [pltpu.VMEM((TC, D_MODEL), F32)] * 5 + [pltpu.VMEM((8, D_MODEL), F32)],
        compiler_params=_cp(("arbitrary", "arbitrary"), VMEM_BIG),
    )(proj, proj, proj, hs, hs, dyh, cw, cb, w_a, b_a, w_x, b_x, lam)


def _final_loss(x, g, target):
    t_tok = x.shape[0]

    def body(x_ref, g_ref, t_ref, dx_ref, loss_ref, dg_ref):
        @pl.when(pl.program_id(0) == 0)
        def _():
            loss_ref[...] = jnp.zeros_like(loss_ref)
            dg_ref[...] = jnp.zeros_like(dg_ref)

        xv = x_ref[...]
        gv = g_ref[...]
        rstd = lax.rsqrt(jnp.mean(xv * xv, axis=-1, keepdims=True) + EPS)
        xhat = xv * rstd
        err = xhat * gv - t_ref[...]
        loss_ref[0:1, :] += jnp.sum(err * err, axis=0, keepdims=True) * (0.5 / D_MODEL)
        dout = err * (1.0 / D_MODEL)
        dg_ref[0:1, :] += jnp.sum(dout * xhat, axis=0, keepdims=True)
        dxhat = dout * gv
        dx_ref[...] = rstd * (dxhat - xhat * jnp.mean(dxhat * xhat, axis=-1, keepdims=True))

    row = pl.BlockSpec((TM, D_MODEL), lambda i: (i, 0))
    acc = pl.BlockSpec((8, D_MODEL), lambda i: (0, 0))
    return _pallas(body, name="final_loss", grid=(t_tok // TM,),
                   in_specs=[row, pl.BlockSpec((1, D_MODEL), lambda i: (0, 0)), row],
                   out_specs=[row, acc, acc],
                   out_shape=[jax.ShapeDtypeStruct((t_tok, D_MODEL), F32)] + [jax.ShapeDtypeStruct((8, D_MODEL), F32)] * 2,
                   compiler_params=_cp(("arbitrary",), VMEM_MID))(x, g, target)


def _adam_math(w, g, m, v):
    m_new = ADAM_B1 * m + (1.0 - ADAM_B1) * g
    v_new = ADAM_B2 * v + (1.0 - ADAM_B2) * (g * g)
    m_hat = m_new / (1.0 - ADAM_B1 ** ADAM_STEP)
    v_hat = v_new / (1.0 - ADAM_B2 ** ADAM_STEP)
    delta = -ADAM_LR * (m_hat / (jnp.sqrt(v_hat) + ADAM_EPS) + ADAM_WD * w)
    return delta, m_new, v_new


def _sum_leading(name, x):
    n, rows, cols = x.shape
    tr = PACK_ROWS if rows % PACK_ROWS == 0 else rows

    def body(x_ref, o_ref):
        acc = x_ref[0]
        for d in range(1, n):
            acc = acc + x_ref[d]
        o_ref[...] = acc

    return _pallas(body, name=name, grid=(rows // tr,),
                   in_specs=[pl.BlockSpec((n, tr, cols), lambda i: (0, i, 0))],
                   out_specs=pl.BlockSpec((tr, cols), lambda i: (i, 0)),
                   out_shape=jax.ShapeDtypeStruct((rows, cols), F32),
                   compiler_params=_cp(("arbitrary",), VMEM_MID))(x)


def _adamw(name, w, m, v, g=None, parts=None):
    rows, cols = w.shape
    tr = rows if rows <= 256 else 256

    def body(*refs):
        w_ref, m_ref, v_ref, g_in, g_ref, d_ref, mo_ref, vo_ref = refs
        if parts is None:
            gv = g_in[...]
        else:
            acc = g_in[0].astype(F32)
            for d in range(1, N_DEV):
                acc = acc + g_in[d].astype(F32)
            gv = acc[:, :cols]
        delta, m_new, v_new = _adam_math(w_ref[...], gv, m_ref[...], v_ref[...])
        g_ref[...] = gv
        d_ref[...] = delta
        mo_ref[...] = m_new
        vo_ref[...] = v_new

    row = pl.BlockSpec((tr, cols), lambda i: (i, 0))
    if parts is None:
        g_spec, g_arg = row, g
    else:
        g_spec, g_arg = pl.BlockSpec((N_DEV, tr, parts.shape[2]), lambda i: (0, i, 0)), parts
    return _pallas(body, name=name, grid=(rows // tr,), in_specs=[row, row, row, g_spec], out_specs=[row] * 4,
                   out_shape=[jax.ShapeDtypeStruct((rows, cols), F32)] * 4,
                   compiler_params=_cp(("arbitrary",), VMEM_MID))(w, m, v, g_arg)


def _pack_rows(arrs):
    rows, meta, total = [], [], 0
    for a in arrs:
        flat = a.reshape(-1)
        nrow = -(-flat.shape[0] // 1024) * 8
        rows.append(jnp.pad(flat, (0, nrow * 128 - flat.shape[0])).reshape(nrow, 128))
        meta.append((a.shape, flat.shape[0], nrow))
        total += nrow
    tail = -total % PACK_ROWS
    if tail:
        rows.append(jnp.zeros((tail, 128), F32))
    return jnp.concatenate(rows, axis=0), meta


def _unpack_rows(packed, meta):
    out, r0 = [], 0
    for shape, size, nrow in meta:
        out.append(packed[r0:r0 + nrow].reshape(-1)[:size].reshape(shape))
        r0 += nrow
    return out


WEIGHTS = ["rel_bias", "norm_g", "ada_w", "ada_b", "attn_w_in", "attn_sinks", "attn_b_f", "attn_w_out", "lru_w_in",
           "lru_conv_w", "lru_conv_b", "lru_w_a", "lru_b_a", "lru_w_x", "lru_b_x", "lru_lambda", "lru_w_out", "final_g"]
BIG = ["ada_w", "attn_w_in", "attn_w_out", "lru_w_in", "lru_w_out"]
PACK_ROWS = 256


def kernel(x, c, rel_bias, norm_g, ada_w, ada_b, attn_w_in, attn_sinks, attn_b_f, attn_w_out, lru_w_in, lru_conv_w, lru_conv_b, lru_w_a, lru_b_a, lru_w_x, lru_b_x, lru_lambda, lru_w_out, final_g, loss_target, m_rel_bias, m_norm_g, m_ada_w, m_ada_b, m_attn_w_in, m_attn_sinks, m_attn_b_f, m_attn_w_out, m_lru_w_in, m_lru_conv_w, m_lru_conv_b, m_lru_w_a, m_lru_b_a, m_lru_w_x, m_lru_b_x, m_lru_lambda, m_lru_w_out, m_final_g, v_rel_bias, v_norm_g, v_ada_w, v_ada_b, v_attn_w_in, v_attn_sinks, v_attn_b_f, v_attn_w_out, v_lru_w_in, v_lru_conv_w, v_lru_conv_b, v_lru_w_a, v_lru_b_a, v_lru_w_x, v_lru_b_x, v_lru_lambda, v_lru_w_out, v_final_g):
    nseq, seq, _ = x.shape
    t_tok = nseq * seq
    me = 4 * lax.axis_index("x") + 2 * lax.axis_index("y") + lax.axis_index("c")
    x0 = x.reshape(t_tok, D_MODEL)
    target = loss_target.reshape(t_tok, D_MODEL)

    w_in_pad = jnp.pad(attn_w_in[0].astype(BF16), ((0, 0), (0, SHARD_W_PAD - SHARD_W_IN)))
    vec_shard = jnp.concatenate([lru_conv_w[0], lru_conv_b, lru_b_a, lru_b_x, lru_lambda], axis=0)
    g_w_in, g_w_out0, g_lru_in, g_w_out1, g_vec, g_c = _exchange(
        "gather_weights",
        [w_in_pad, attn_w_out[0].astype(BF16), lru_w_in[0].astype(BF16), lru_w_out[0].astype(BF16), vec_shard, c], [])
    w_full = jnp.transpose(g_w_in[:, :, :SHARD_W_IN], (1, 0, 2)).reshape(D_MODEL, N_DEV * SHARD_W_IN)
    w_aq, w_ak, w_av = w_full[:, 0:512], w_full[:, 512:640], w_full[:, 640:768]
    w_bq, w_bk, w_bv = w_full[:, 768:1280], w_full[:, 1280:1792], w_full[:, 1792:2304]
    w_f, w_gate = w_full[:, 2304:2312], w_full[:, 2312:3336]
    w_main = jnp.concatenate([w_bq, w_bk, w_bv, w_aq, w_gate, w_ak, w_av], axis=1)
    wf_t = jnp.transpose(w_f)
    w_out0 = g_w_out0.reshape(D_MODEL, D_MODEL)
    w_out1 = g_w_out1.reshape(D_MODEL, D_MODEL)
    vec_full = jnp.transpose(g_vec, (1, 0, 2)).reshape(8, D_MODEL)
    conv_w, conv_b, b_a, b_x, lam = vec_full[0:4], vec_full[4:5], vec_full[5:6], vec_full[6:7], vec_full[7:8]
    c_all = g_c.reshape(N_DEV * nseq, D_MODEL)

    ncol = ada_w.shape[2]
    ada_b_slice = lax.dynamic_slice(ada_b.reshape(2, N_DEV, ncol), (0, me, 0), (2, 1, ncol))
    mod_part = _ada_mod(c_all, ada_w, ada_b_slice)
    (g_mod,) = _exchange("gather_mod", [mod_part], [])
    mine = lax.dynamic_slice(g_mod, (0, 0, me * nseq, 0), (N_DEV, 2, nseq, ncol))
    mod = jnp.transpose(mine, (1, 2, 0, 3)).reshape(2, nseq, 3 * D_MODEL)
    shift = [mod[l, :, 0:D_MODEL].reshape(nseq, 1, D_MODEL) for l in range(2)]
    scale = [mod[l, :, D_MODEL:2 * D_MODEL].reshape(nseq, 1, D_MODEL) for l in range(2)]
    gmod = [mod[l, :, 2 * D_MODEL:].reshape(nseq, 1, D_MODEL) for l in range(2)]

    onehot = _bucket_onehot()
    bias = _bias_expand(jnp.transpose(rel_bias), onehot).reshape(N_HEADS, BLOCK, 2 * BLOCK)
    sinks = attn_sinks.reshape(N_HEADS)
    b_f = attn_b_f.reshape(N_HEADS, 1)
    h0, qkvg, fl_t = _norm_proj("norm_proj0", x0, norm_g[0:1], shift[0], scale[0], w_main, seq, BF16, wf_t=wf_t)
    f_row, f_col = _fox_prep(fl_t, b_f, seq)
    a_out, lse_a = _swa_fwd(qkvg, bias, sinks, seq)
    q_aug, k_aug, kt_aug, vt = _fox_aug(qkvg, f_col, seq)
    b_out, lse_b = _fox_fwd_t(q_aug, k_aug, vt, seq)
    yg0, y0, x1 = _out_proj("out_proj0", [a_out, b_out], qkvg, C_GATE // D_MODEL, w_out0, x0, gmod[0], seq)

    h1, proj1 = _norm_proj("norm_proj1", x1, norm_g[1:2], shift[1], scale[1], g_lru_in, seq, F32)
    hs = _lru_fwd(proj1, conv_w, conv_b, lru_w_a[0], b_a, lru_w_x[0], b_x, lam, seq)
    yg1, y1, x2 = _out_proj("out_proj1", [hs], proj1, 1, w_out1, x1, gmod[1], seq)

    dx2, loss_rows, dfinal_rows = _final_loss(x2, final_g.reshape(1, D_MODEL), target)
    loss = lax.psum(jnp.sum(loss_rows[0]), ("x", "y", "c"))

    dy1, dgm1, dyh = _out_proj_bwd("out_proj1_bwd", dx2, gmod[1], y1, w_out1, seq)
    dproj1, dcw, dvec, dw_a, dw_x = _lru_bwd(proj1, hs, dyh, conv_w, conv_b, lru_w_a[0], b_a, lru_w_x[0], b_x, lam, seq)
    dx1, dss1, dg1 = _norm_bwd("norm1_bwd", [(dproj1, 0)], g_lru_in, x1, norm_g[1:2], scale[1], dx2, seq)
    (p_w_out1,) = _dw("dw_out1", yg1, [dy1])
    (p_lru_in,) = _dw("dw_lru_in", h1, [dproj1], blocked=2 * D_MODEL // N_DEV)

    dy0, dgm0, du_a, du_b, dgate = _out_proj_bwd("out_proj0_bwd", dx1, gmod[0], y0, w_out0, seq,
                                                  attn=(a_out, b_out, qkvg))
    dq_a, dkv_a, dbias, dsink = _swa_bwd(qkvg, du_a, a_out, lse_a, bias, sinks, seq)
    dq_b, dk_b, dv_b, df4 = _fox_bwd_t(q_aug, k_aug, kt_aug, qkvg, du_b, b_out, lse_b, seq)
    dfl_t, db_f = _fox_post(df4.reshape(N_HEADS, t_tok), fl_t, b_f, seq)
    parts0 = [(dq_b, C_BQ), (dk_b, C_BK), (dv_b, C_BV), (dq_a, C_AQ), (dgate, C_GATE), (dkv_a, C_AK)]
    dx0, dss0, dg0 = _norm_bwd("norm0_bwd", parts0, w_main, x0, norm_g[0:1], scale[0], dx1, seq,
                               rows_part=(dfl_t, wf_t))
    (p_w_out0,) = _dw("dw_out0", yg0, [dy0])
    pw_bq, pw_bk, pw_bv, pw_aq, pw_gate, pw_akv = _dw("dw_attn_in", h0, [p for p, _ in parts0])
    pw_f = _dw_rows("dw_f", dfl_t, h0)
    dbias_t = _bias_reduce(dbias.reshape(N_HEADS, BLOCK * 2 * BLOCK), onehot)

    p_w_in = jnp.concatenate([pw_aq, pw_akv, pw_bq, pw_bk, pw_bv, jnp.transpose(pw_f).astype(BF16), pw_gate], axis=1)
    p_w_in = jnp.transpose(p_w_in.reshape(D_MODEL, N_DEV, SHARD_W_IN), (1, 0, 2))
    p_w_in = jnp.pad(p_w_in, ((0, 0), (0, 0), (0, SHARD_W_PAD - SHARD_W_IN)))
    rows_out = D_MODEL // N_DEV
    small_partials = [jnp.transpose(dbias_t), jnp.stack([dg0[0], dg1[0]]), dsink[:, 0], db_f[:, 0],
                      dcw[0:4], dvec[0:4], dw_a, dw_x, dfinal_rows[0]]
    gpack, gmeta = _pack_rows(small_partials)
    dmod = jnp.stack([jnp.concatenate([dss[:, 0], dss[:, 1], dgm[:, 0]], axis=1)
                      for dss, dgm in ((dss0, dgm0), (dss1, dgm1))], axis=1)
    g_small, g_dmod, r_w_in, r_w_out0, r_lru_in, r_w_out1 = _exchange(
        "exchange_grads", [gpack, dmod],
        [p_w_in, p_w_out0.reshape(N_DEV, rows_out, D_MODEL), p_lru_in, p_w_out1.reshape(N_DEV, rows_out, D_MODEL)])

    small_sum = _sum_leading("sum_small", g_small)
    d_rel, d_norm_g, d_sinks, d_b_f, d_cw, d_vec, d_wa, d_wx, d_final_g = _unpack_rows(small_sum, gmeta)
    cols = lambda a: lax.dynamic_slice(a, (0, me * LRU_BLOCK_W), (a.shape[0], LRU_BLOCK_W))
    dmod_all = g_dmod.reshape(N_DEV * nseq, 2 * 3 * D_MODEL)
    d_ada_b = _sum_leading("sum_ada_b", dmod_all.reshape(N_DEV * nseq, 2 * 3 * D_MODEL // 128, 128)).reshape(2, 3 * D_MODEL)
    dmod_slice = lax.dynamic_slice(dmod_all.reshape(N_DEV * nseq, 2, N_DEV, ncol), (0, 0, me, 0),
                                   (N_DEV * nseq, 2, 1, ncol)).reshape(N_DEV * nseq, 2, ncol)
    d_ada_w = _ada_w_grad(c_all, jnp.transpose(dmod_slice, (1, 0, 2)))

    given = dict(
        rel_bias=(rel_bias, m_rel_bias, v_rel_bias), norm_g=(norm_g, m_norm_g, v_norm_g),
        ada_w=(ada_w, m_ada_w, v_ada_w), ada_b=(ada_b, m_ada_b, v_ada_b),
        attn_w_in=(attn_w_in, m_attn_w_in, v_attn_w_in), attn_sinks=(attn_sinks, m_attn_sinks, v_attn_sinks),
        attn_b_f=(attn_b_f, m_attn_b_f, v_attn_b_f), attn_w_out=(attn_w_out, m_attn_w_out, v_attn_w_out),
        lru_w_in=(lru_w_in, m_lru_w_in, v_lru_w_in), lru_conv_w=(lru_conv_w, m_lru_conv_w, v_lru_conv_w),
        lru_conv_b=(lru_conv_b, m_lru_conv_b, v_lru_conv_b), lru_w_a=(lru_w_a, m_lru_w_a, v_lru_w_a),
        lru_b_a=(lru_b_a, m_lru_b_a, v_lru_b_a), lru_w_x=(lru_w_x, m_lru_w_x, v_lru_w_x),
        lru_b_x=(lru_b_x, m_lru_b_x, v_lru_b_x), lru_lambda=(lru_lambda, m_lru_lambda, v_lru_lambda),
        lru_w_out=(lru_w_out, m_lru_w_out, v_lru_w_out), final_g=(final_g, m_final_g, v_final_g))
    results = {}

    def big(name, shape2d, g=None, parts=None):
        w, m, v = (a.reshape(shape2d) for a in given[name])
        outs = _adamw("adamw_" + name, w, m, v, g=g, parts=parts)
        results[name] = tuple(o.reshape(given[name][0].shape) for o in outs)

    big("ada_w", (2 * D_MODEL, ncol), g=d_ada_w.reshape(2 * D_MODEL, ncol))
    big("attn_w_in", (D_MODEL, SHARD_W_IN), parts=r_w_in)
    big("attn_w_out", (rows_out, D_MODEL), parts=r_w_out0)
    big("lru_w_in", (D_MODEL, 2 * D_MODEL // N_DEV), parts=r_lru_in)
    big("lru_w_out", (rows_out, D_MODEL), parts=r_w_out1)

    small_grads = dict(
        rel_bias=d_rel, norm_g=d_norm_g, ada_b=d_ada_b, attn_sinks=d_sinks.reshape(1, N_HEADS),
        attn_b_f=d_b_f.reshape(1, N_HEADS), lru_conv_w=cols(d_cw).reshape(1, 4, LRU_BLOCK_W),
        lru_conv_b=cols(d_vec[0:1]), lru_w_a=d_wa.reshape(lru_w_a.shape), lru_b_a=cols(d_vec[1:2]),
        lru_w_x=d_wx.reshape(lru_w_x.shape), lru_b_x=cols(d_vec[2:3]), lru_lambda=cols(d_vec[3:4]),
        final_g=d_final_g)
    small = [n for n in WEIGHTS if n not in BIG]
    wpack, smeta = _pack_rows([given[n][0] for n in small])
    mpack, _ = _pack_rows([given[n][1] for n in small])
    vpack, _ = _pack_rows([given[n][2] for n in small])
    gpack2, _ = _pack_rows([small_grads[n] for n in small])
    packs = _adamw("adamw_small", wpack, mpack, vpack, g=gpack2)
    unpacked = [_unpack_rows(p, smeta) for p in packs]
    for k, n in enumerate(small):
        results[n] = tuple(unpacked[j][k] for j in range(4))

    grad_x = dx0.reshape(x.shape)
    out = [loss, grad_x]
    for j in range(4):
        out += [results[n][j] for n in WEIGHTS]
    return tuple(out)
```

```python
import functools
import math

import jax
import jax.numpy as jnp
from jax import lax
from jax.experimental import pallas as pl
from jax.experimental.pallas import tpu as pltpu

F32 = jnp.float32
BF16 = jnp.bfloat16
HI = lax.Precision.HIGHEST
MESH = pl.DeviceIdType.MESH

N_DEV = 8
D_MODEL = 1024
HEAD_DIM = 64
N_HEADS = 8
KV_GROUP = 4
BLOCK = 128
REL_BUCKETS = 32
REL_MAX_EXACT = 16
REL_MAX_DIST = 128
LRU_BLOCKS = 8
LRU_BLOCK_W = 128
LRU_C = 8.0
EPS = 1e-6
SCALE = HEAD_DIM ** -0.5
NEG = -1e30

ADAM_LR = 0.001
ADAM_B1 = 0.9
ADAM_B2 = 0.999
ADAM_EPS = 1e-08
ADAM_WD = 0.01
ADAM_STEP = 10

C_BQ, C_BK, C_BV, C_AQ, C_GATE, C_AK, C_AV = 0, 512, 1024, 1536, 2048, 3072, 3200
N_MAIN = 3328
SHARD_W_IN = 417
SHARD_W_PAD = 512

TM = 256
TQ = 256
TK = 128
TC = 256
VMEM_BIG = 56 * 1024 * 1024
VMEM_MID = 40 * 1024 * 1024


def _pallas(body, **kw):
    return pl.pallas_call(body, **kw)


def _cp(sem=None, vmem=None):
    kw = {}
    if sem is not None:
        kw["dimension_semantics"] = sem
    if vmem is not None:
        kw["vmem_limit_bytes"] = vmem
    return pltpu.CompilerParams(**kw)


def _nn(a, b, precision=None):
    return jnp.dot(a, b, preferred_element_type=F32, precision=precision)


def _nt(a, b, precision=None):
    return lax.dot_general(a, b, (((1,), (1,)), ((), ())), preferred_element_type=F32, precision=precision)


def _tn(a, b, precision=None):
    return lax.dot_general(a, b, (((0,), (0,)), ((), ())), preferred_element_type=F32, precision=precision)


def _sigmoid(x):
    return 1.0 / (1.0 + jnp.exp(-x))


def _silu(x):
    return x * _sigmoid(x)


def _dsilu(x):
    s = _sigmoid(x)
    return s * (1.0 + x * (1.0 - s))


def _neg_expm1(x):
    poly = x * (1.0 + x * (0.5 + x * (1.0 / 6.0 + x * (1.0 / 24.0))))
    return -jnp.where(jnp.abs(x) < 0.05, poly, jnp.exp(x) - 1.0)


def _col(tile, idx):
    lane = lax.broadcasted_iota(jnp.int32, tile.shape, 1)
    return jnp.sum(jnp.where(lane == idx, tile, 0.0), axis=1, keepdims=True)


def _row(tile, idx):
    sub = lax.broadcasted_iota(jnp.int32, tile.shape, 0)
    return jnp.sum(jnp.where(sub == idx, tile, 0.0), axis=0, keepdims=True)


def _exchange(name, gathers, scatters):
    ng, n = len(gathers), len(gathers) + len(scatters)
    ins = list(gathers) + list(scatters)

    def body(*refs):
        in_refs, out_refs = refs[:n], refs[n:2 * n]
        send_sems, recv_sems, loc_sems = refs[2 * n:]
        x, y, c = lax.axis_index("x"), lax.axis_index("y"), lax.axis_index("c")
        me = 4 * x + 2 * y + c

        def peer(r):
            px = 1 - x if r & 4 else x
            py = 1 - y if r & 2 else y
            pc = 1 - c if r & 1 else c
            return (px, py, pc), 4 * px + 2 * py + pc

        local, sends, recvs = [], [], []
        for k in range(n):
            mine = in_refs[k] if k < ng else in_refs[k].at[me]
            cp = pltpu.make_async_copy(mine, out_refs[k].at[me], loc_sems.at[k])
            cp.start()
            local.append(cp)
            for r in range(1, N_DEV):
                pid, pidx = peer(r)
                src = in_refs[k] if k < ng else in_refs[k].at[pidx]
                snd = pltpu.make_async_remote_copy(
                    src_ref=src, dst_ref=out_refs[k].at[me], send_sem=send_sems.at[r - 1, k],
                    recv_sem=recv_sems.at[r - 1, k], device_id=pid, device_id_type=MESH)
                snd.start()
                sends.append(snd)
                recvs.append(pltpu.make_async_remote_copy(
                    src_ref=src, dst_ref=out_refs[k].at[pidx], send_sem=send_sems.at[r - 1, k],
                    recv_sem=recv_sems.at[r - 1, k], device_id=pid, device_id_type=MESH))
        for rc in recvs:
            rc.wait_recv()
        for snd in sends:
            snd.wait_send()
        for cp in local:
            cp.wait()

    out_shape = [jax.ShapeDtypeStruct((N_DEV,) + a.shape, a.dtype) for a in gathers]
    out_shape += [jax.ShapeDtypeStruct(a.shape, a.dtype) for a in scatters]
    any_spec = pl.BlockSpec(memory_space=pl.ANY)
    return _pallas(
        body, name=name, out_shape=out_shape,
        in_specs=[any_spec] * n, out_specs=[any_spec] * n,
        scratch_shapes=[pltpu.SemaphoreType.DMA((N_DEV - 1, n)), pltpu.SemaphoreType.DMA((N_DEV - 1, n)),
                        pltpu.SemaphoreType.DMA((n,))],
    )(*ins)


def _ada_mod(c_all, ada_w, ada_b_slice):
    def body(c_ref, w_ref, b_ref, o_ref):
        ca = _silu(c_ref[...])
        for l in range(2):
            o_ref[l] = _nn(ca, w_ref[l], HI) + b_ref[l]

    return _pallas(body, name="ada_mod",
                   out_shape=jax.ShapeDtypeStruct((2, c_all.shape[0], ada_w.shape[2]), F32),
                   compiler_params=_cp(vmem=VMEM_MID))(c_all, ada_w, ada_b_slice)


def _ada_w_grad(c_all, dmod_slice):
    def body(c_ref, d_ref, o_ref):
        ca = _silu(c_ref[...])
        for l in range(2):
            o_ref[l] = _tn(ca, d_ref[l], HI)

    return _pallas(body, name="ada_w_grad",
                   out_shape=jax.ShapeDtypeStruct((2, D_MODEL, dmod_slice.shape[2]), F32),
                   compiler_params=_cp(vmem=VMEM_MID))(c_all, dmod_slice)


def _bucket_onehot():
    qi = jnp.arange(BLOCK)[:, None]
    kj = jnp.arange(2 * BLOCK)[None, :]
    rel = qi - kj + BLOCK
    n = jnp.maximum(rel, 0)
    nf = jnp.maximum(n, 1).astype(F32)
    large = REL_MAX_EXACT + (jnp.log(nf / REL_MAX_EXACT) / math.log(REL_MAX_DIST / REL_MAX_EXACT)
                             * (REL_BUCKETS - REL_MAX_EXACT)).astype(jnp.int32)
    large = jnp.minimum(large, REL_BUCKETS - 1)
    bucket = jnp.where(n < REL_MAX_EXACT, n, large).reshape(1, BLOCK * 2 * BLOCK)
    return (jnp.arange(REL_BUCKETS)[:, None] == bucket).astype(F32)


def _bias_expand(rel_bias_t, onehot):
    def body(r_ref, e_ref, o_ref):
        o_ref[...] = _nn(r_ref[...], e_ref[...], HI)

    return _pallas(body, name="bias_expand",
                   out_shape=jax.ShapeDtypeStruct((N_HEADS, onehot.shape[1]), F32),
                   compiler_params=_cp(vmem=VMEM_MID))(rel_bias_t, onehot)


def _bias_reduce(dbias, onehot):
    def body(d_ref, e_ref, o_ref):
        o_ref[...] = _nt(d_ref[...], e_ref[...], HI)

    return _pallas(body, name="bias_reduce",
                   out_shape=jax.ShapeDtypeStruct((N_HEADS, REL_BUCKETS), F32),
                   compiler_params=_cp(vmem=VMEM_MID))(dbias, onehot)


def _norm_proj(name, x, g, shift, scale, w, seq, out_dtype, wf_t=None):
    t_tok = x.shape[0]
    w3d = w.ndim == 3
    n_out = w.shape[0] * w.shape[2] if w3d else w.shape[1]
    cn = w.shape[2] if w3d else 256

    def body(x_ref, g_ref, sh_ref, sc_ref, w_ref, *rest):
        if wf_t is not None:
            wf_ref, h_ref, o_ref, fl_ref = rest
        else:
            h_ref, o_ref = rest
        xv = x_ref[...]
        rstd = lax.rsqrt(jnp.mean(xv * xv, axis=-1, keepdims=True) + EPS)
        h = (xv * rstd) * g_ref[...] * (1.0 + sc_ref[...]) + sh_ref[...]
        hb = h.astype(BF16)
        h_ref[...] = hb
        for j in range(n_out // cn):
            wj = w_ref[j] if w3d else w_ref[:, j * cn:(j + 1) * cn]
            o_ref[:, j * cn:(j + 1) * cn] = _nn(hb, wj).astype(out_dtype)
        if wf_t is not None:
            fl_ref[...] = _nt(wf_ref[...], hb)

    mod_spec = pl.BlockSpec((None, 1, D_MODEL), lambda i: (i * TM // seq, 0, 0))
    w_spec = (pl.BlockSpec(w.shape, lambda i: (0, 0, 0)) if w3d else pl.BlockSpec(w.shape, lambda i: (0, 0)))
    in_specs = [pl.BlockSpec((TM, D_MODEL), lambda i: (i, 0)), pl.BlockSpec((1, D_MODEL), lambda i: (0, 0)),
                mod_spec, mod_spec, w_spec]
    out_shape = [jax.ShapeDtypeStruct((t_tok, D_MODEL), BF16), jax.ShapeDtypeStruct((t_tok, n_out), out_dtype)]
    out_specs = [pl.BlockSpec((TM, D_MODEL), lambda i: (i, 0)), pl.BlockSpec((TM, n_out), lambda i: (i, 0))]
    args = [x, g, shift, scale, w]
    if wf_t is not None:
        in_specs.append(pl.BlockSpec(wf_t.shape, lambda i: (0, 0)))
        out_shape.append(jax.ShapeDtypeStruct((wf_t.shape[0], t_tok), F32))
        out_specs.append(pl.BlockSpec((wf_t.shape[0], TM), lambda i: (0, i)))
        args.append(wf_t)
    return _pallas(body, name=name, grid=(t_tok // TM,), in_specs=in_specs, out_specs=out_specs,
                   out_shape=out_shape, compiler_params=_cp(("arbitrary",), VMEM_BIG))(*args)


def _fox_prep(fl_t, b_f, seq):
    t_tok = fl_t.shape[1]
    ch = 256

    def body(fl_ref, bf_ref, fr_ref, fc_ref):
        z = fl_ref[...] + bf_ref[...]
        logf = jnp.minimum(z, 0.0) - jnp.log(1.0 + jnp.exp(-jnp.abs(z)))
        ri = lax.broadcasted_iota(jnp.int32, (ch, ch), 0)
        ci = lax.broadcasted_iota(jnp.int32, (ch, ch), 1)
        upper = (ri <= ci).astype(F32)
        eye = (ri == ci).astype(F32)
        carry = jnp.zeros((N_HEADS, 1), F32)
        for k in range(seq // ch):
            fk = _nn(logf[:, k * ch:(k + 1) * ch], upper, HI) + carry
            carry = fk[:, ch - 1:ch]
            fr_ref[:, k * ch:(k + 1) * ch] = fk
            padded = jnp.concatenate([fk, jnp.zeros((128 - N_HEADS, ch), F32)], axis=0)
            fc_ref[k * ch:(k + 1) * ch, :] = _nt(eye, padded, HI)

    return _pallas(
        body, name="fox_prep", grid=(t_tok // seq,),
        in_specs=[pl.BlockSpec((N_HEADS, seq), lambda b: (0, b)), pl.BlockSpec((N_HEADS, 1), lambda b: (0, 0))],
        out_specs=[pl.BlockSpec((N_HEADS, seq), lambda b: (0, b)), pl.BlockSpec((seq, 128), lambda b: (b, 0))],
        out_shape=[jax.ShapeDtypeStruct((N_HEADS, t_tok), F32), jax.ShapeDtypeStruct((t_tok, 128), F32)],
        compiler_params=_cp(("arbitrary",), VMEM_MID))(fl_t, b_f)


def _fox_post(df_row, fl_t, b_f, seq):
    t_tok = fl_t.shape[1]
    ch = 256

    def body(d_ref, fl_ref, bf_ref, o_ref, db_ref):
        @pl.when(pl.program_id(0) == 0)
        def _():
            db_ref[...] = jnp.zeros_like(db_ref)

        z = fl_ref[...] + bf_ref[...]
        sig_neg = 1.0 / (1.0 + jnp.exp(z))
        ri = lax.broadcasted_iota(jnp.int32, (ch, ch), 0)
        ci = lax.broadcasted_iota(jnp.int32, (ch, ch), 1)
        lower = (ri >= ci).astype(F32)
        carry = jnp.zeros((N_HEADS, 1), F32)
        tot = jnp.zeros((N_HEADS, 1), F32)
        for k in reversed(range(seq // ch)):
            dk = _nn(d_ref[:, k * ch:(k + 1) * ch], lower, HI) + carry
            carry = dk[:, 0:1]
            dfl = dk * sig_neg[:, k * ch:(k + 1) * ch]
            o_ref[:, k * ch:(k + 1) * ch] = dfl
            tot = tot + jnp.sum(dfl, axis=1, keepdims=True)
        db_ref[...] += jnp.broadcast_to(tot, db_ref.shape)

    return _pallas(
        body, name="fox_post", grid=(t_tok // seq,),
        in_specs=[pl.BlockSpec((N_HEADS, seq), lambda b: (0, b)), pl.BlockSpec((N_HEADS, seq), lambda b: (0, b)),
                  pl.BlockSpec((N_HEADS, 1), lambda b: (0, 0))],
        out_specs=[pl.BlockSpec((N_HEADS, seq), lambda b: (0, b)), pl.BlockSpec((N_HEADS, 128), lambda b: (0, 0))],
        out_shape=[jax.ShapeDtypeStruct((N_HEADS, t_tok), F32), jax.ShapeDtypeStruct((N_HEADS, 128), F32)],
        compiler_params=_cp(("arbitrary",), VMEM_MID))(df_row, fl_t, b_f)


def _eye(n, dtype):
    return (lax.broadcasted_iota(jnp.int32, (n, n), 0) == lax.broadcasted_iota(jnp.int32, (n, n), 1)).astype(dtype)


def _fox_aug(qkvg, f_col, seq):
    t_tok = qkvg.shape[0]
    ta = 256
    nkb = ta // TK

    def body(q_ref, k_ref, v_ref, fc_ref, qa_ref, ka_ref, kt_ref, vt_ref):
        ri = lax.broadcasted_iota(jnp.int32, (128, 128), 0)
        ci = lax.broadcasted_iota(jnp.int32, (128, 128), 1)
        eye = (ri == ci).astype(BF16)
        lane = lax.broadcasted_iota(jnp.int32, (ta, 128), 1)
        ones_q = jnp.where(jnp.logical_and(lane >= 64, lane < 67), 1.0, 0.0)
        ones_k = jnp.where(jnp.logical_and(lane >= 67, lane < 70), 1.0, 0.0)
        fc_tile = fc_ref[...]
        for p in range(N_HEADS // 2):
            q2 = q_ref[:, 128 * p:128 * (p + 1)]
            k2 = k_ref[:, 128 * p:128 * (p + 1)]
            vt = _nt(eye, v_ref[:, 128 * p:128 * (p + 1)]).astype(BF16)
            for kk in range(nkb):
                vt_ref[p, kk] = vt[:, kk * TK:(kk + 1) * TK]
            for e in range(2):
                h = 2 * p + e
                sel = jnp.logical_and(ri == ci + HEAD_DIM * e, ci < HEAD_DIM)
                f = _col(fc_tile, h)
                fh = f.astype(BF16).astype(F32)
                fm = (f - fh).astype(BF16).astype(F32)
                fl = (f - fh - fm).astype(BF16).astype(F32)
                qa = (_nn(q2, jnp.where(sel, SCALE, 0.0).astype(BF16)) + ones_q + jnp.where(lane == 67, fh, 0.0)
                      + jnp.where(lane == 68, fm, 0.0) + jnp.where(lane == 69, fl, 0.0))
                ka = (_nn(k2, jnp.where(sel, 1.0, 0.0).astype(BF16)) + ones_k - jnp.where(lane == 64, fh, 0.0)
                      - jnp.where(lane == 65, fm, 0.0) - jnp.where(lane == 66, fl, 0.0))
                qa_ref[h] = qa.astype(BF16)
                kab = ka.astype(BF16)
                ka_ref[h] = kab
                kt = _nt(eye, kab).astype(BF16)
                for kk in range(nkb):
                    kt_ref[h, kk] = kt[:, kk * TK:(kk + 1) * TK]

    aug = jax.ShapeDtypeStruct((N_HEADS, t_tok, 128), BF16)
    return _pallas(
        body, name="fox_aug", grid=(t_tok // ta,),
        in_specs=[pl.BlockSpec((ta, 512), lambda i: (i, C_BQ // 512)), pl.BlockSpec((ta, 512), lambda i: (i, C_BK // 512)),
                  pl.BlockSpec((ta, 512), lambda i: (i, C_BV // 512)), pl.BlockSpec((ta, 128), lambda i: (i, 0))],
        out_specs=[pl.BlockSpec((N_HEADS, ta, 128), lambda i: (0, i, 0)), pl.BlockSpec((N_HEADS, ta, 128), lambda i: (0, i, 0)),
                   pl.BlockSpec((N_HEADS, nkb, 128, TK), lambda i: (0, i, 0, 0)),
                   pl.BlockSpec((N_HEADS // 2, nkb, 128, TK), lambda i: (0, i, 0, 0))],
        out_shape=[aug, aug, jax.ShapeDtypeStruct((N_HEADS, t_tok // TK, 128, TK), BF16),
                   jax.ShapeDtypeStruct((N_HEADS // 2, t_tok // TK, 128, TK), BF16)],
        compiler_params=_cp(("arbitrary",), VMEM_MID))(qkvg, qkvg, qkvg, f_col)


def _fox_fwd_t(q_aug, k_aug, vt, seq):
    t_tok = q_aug.shape[1]
    nq = seq // TQ
    ratio = TQ // TK

    def body(qa_ref, ka_ref, vt_ref, o_ref, lse_ref, ml_s, acc_s, st_s, p_s, al_s):
        i = pl.program_id(1)
        tpos = i * TQ + lax.broadcasted_iota(jnp.int32, (1, TQ), 1)
        eye = _eye(HEAD_DIM, BF16)
        for h in range(N_HEADS):
            ml_s[0, h] = jnp.full((1, TQ), NEG, F32)
            ml_s[1, h] = jnp.zeros((1, TQ), F32)
            acc_s[h] = jnp.zeros((HEAD_DIM, TQ), F32)
            p_s[1, h] = jnp.zeros((TK, TQ), BF16)
            al_s[1, h] = jnp.ones((1, TQ), F32)

        def scores(j):
            row0 = pl.multiple_of(j * TK, TK)
            for h in range(N_HEADS):
                st_s[j & 1, h] = _nt(ka_ref[h, pl.ds(row0, TK), :], qa_ref[h])

        def softmax(j, masked):
            slot = j & 1
            if masked:
                keep = (j * TK + lax.broadcasted_iota(jnp.int32, (TK, 1), 0)) <= tpos
            for h in range(N_HEADS):
                st = st_s[slot, h]
                if masked:
                    st = jnp.where(keep, st, NEG)
                m = ml_s[0, h]
                m_new = jnp.maximum(m, jnp.max(st, axis=0, keepdims=True))
                alpha = jnp.exp(m - m_new)
                pe = jnp.exp(st - m_new)
                ml_s[0, h] = m_new
                ml_s[1, h] = alpha * ml_s[1, h] + jnp.sum(pe, axis=0, keepdims=True)
                al_s[slot, h] = alpha
                p_s[slot, h] = pe.astype(BF16)

        def values(j):
            slot = j & 1
            jv = jnp.maximum(j, 0)
            for h in range(N_HEADS):
                p, e = divmod(h, 2)
                acc_s[h] = al_s[slot, h] * acc_s[h] + _nn(vt_ref[p, jv, e * HEAD_DIM:(e + 1) * HEAD_DIM, :], p_s[slot, h])

        def step(j, carry):
            values(j - 1)
            softmax(j, False)
            scores(j + 1)
            return carry

        last = ratio * i + ratio - 1
        scores(0)
        lax.fori_loop(0, ratio * i, step, 0)
        for kk in range(ratio):
            j = ratio * i + kk
            values(j - 1)
            softmax(j, True)
            if kk < ratio - 1:
                scores(j + 1)
        values(last)
        for p in range(N_HEADS // 2):
            outs = []
            for e in range(2):
                h = 2 * p + e
                l = ml_s[1, h]
                outs.append(_tn((acc_s[h] / l).astype(BF16), eye))
                lse_ref[p, e:e + 1, :] = ml_s[0, h] + jnp.log(l)
            o_ref[:, 128 * p:128 * (p + 1)] = jnp.concatenate(outs, axis=1).astype(BF16)

    return _pallas(
        body, name="fox_fwd", grid=(t_tok // seq, nq),
        in_specs=[pl.BlockSpec((N_HEADS, TQ, 128), lambda b, i: (0, b * nq + i, 0)),
                  pl.BlockSpec((N_HEADS, seq, 128), lambda b, i: (0, b, 0)),
                  pl.BlockSpec((N_HEADS // 2, seq // TK, 128, TK), lambda b, i: (0, b, 0, 0))],
        out_specs=[pl.BlockSpec((TQ, 512), lambda b, i: (b * nq + i, 0)),
                   pl.BlockSpec((N_HEADS // 2, 2, TQ), lambda b, i: (0, 0, b * nq + i))],
        out_shape=[jax.ShapeDtypeStruct((t_tok, 512), BF16), jax.ShapeDtypeStruct((N_HEADS // 2, 2, t_tok), F32)],
        scratch_shapes=[pltpu.VMEM((2, N_HEADS, 1, TQ), F32), pltpu.VMEM((N_HEADS, HEAD_DIM, TQ), F32),
                        pltpu.VMEM((2, N_HEADS, TK, TQ), F32), pltpu.VMEM((2, N_HEADS, TK, TQ), BF16),
                        pltpu.VMEM((2, N_HEADS, 1, TQ), F32)],
        compiler_params=_cp(("arbitrary", "arbitrary"), VMEM_MID))(q_aug, k_aug, vt)


def _fox_bwd_t(q_aug, k_aug, kt, qkvg, du_b, b_out, lse, seq):
    t_tok = qkvg.shape[0]
    nq = seq // TQ
    nkb = seq // TK
    ratio = TQ // TK
    hg = 4

    def body(qa_ref, ka_ref, kt_ref, v_ref, do_ref, o_ref, lse_ref, dq_ref, dk_ref, dv_ref, df_ref,
             dqt_s, row_s, dfk_s, dk_s, dv_s, dfa_s, st_s, dp_s, pb_s, db_s):
        ones_b = jnp.ones((8, TQ), BF16)
        eye = _eye(HEAD_DIM, BF16)
        lane8 = lax.broadcasted_iota(jnp.int32, (8, 128), 1)
        lane_k = lax.broadcasted_iota(jnp.int32, (TK, 128), 1)
        first = [lane8 < HEAD_DIM, lane8 >= HEAD_DIM]
        for hh in range(hg):
            pp, e = divmod(hh, 2)
            head_lanes = jnp.where(first[e], 1.0, 0.0)
            for ii in range(nq):
                rows = slice(ii * TQ, (ii + 1) * TQ)
                prod = do_ref[rows, 128 * pp:128 * (pp + 1)].astype(F32) * o_ref[rows, 128 * pp:128 * (pp + 1)].astype(F32)
                row_s[hh, ii, 0] = _nt(head_lanes, prod, HI)
                row_s[hh, ii, 1] = jnp.broadcast_to(lse_ref[pp, e:e + 1, ii * TQ:(ii + 1) * TQ], (8, TQ))
                dqt_s[hh, ii] = jnp.zeros((128, TQ), F32)

        def kblock(j, _):
            krow = pl.multiple_of(j * TK, TK)
            spos = j * TK + lax.broadcasted_iota(jnp.int32, (TK, 1), 0)
            for hh in range(hg):
                dk_s[hh] = jnp.zeros((TK, 128), F32)
                dv_s[hh] = jnp.zeros((TK, 128), F32)
                dfa_s[hh] = jnp.zeros((8, TK), F32)

            def scores(i):
                qrow = pl.multiple_of(i * TQ, TQ)
                for hh in range(hg):
                    pp, e = divmod(hh, 2)
                    own = (lane_k < HEAD_DIM) if e == 0 else (lane_k >= HEAD_DIM)
                    v2 = v_ref[pl.ds(krow, TK), 128 * pp:128 * (pp + 1)]
                    vj = jnp.where(own, v2, jnp.zeros_like(v2))
                    st_s[i & 1, hh] = _nt(ka_ref[hh, pl.ds(krow, TK), :], qa_ref[hh, pl.ds(qrow, TQ), :])
                    dp_s[i & 1, hh] = _nt(vj, do_ref[pl.ds(qrow, TQ), 128 * pp:128 * (pp + 1)])

            def elementwise(i, masked):
                slot = i & 1
                if masked:
                    keep = spos <= (i * TQ + lax.broadcasted_iota(jnp.int32, (1, TQ), 1))
                for hh in range(hg):
                    pt = jnp.exp(st_s[slot, hh] - row_s[hh, i, 1][0:1, :])
                    if masked:
                        pt = jnp.where(keep, pt, 0.0)
                    dst = pt * (dp_s[slot, hh] - row_s[hh, i, 0][0:1, :])
                    pb_s[slot, hh] = pt.astype(BF16)
                    db_s[slot, hh] = dst.astype(BF16)

            def grads(i):
                slot = i & 1
                qrow = pl.multiple_of(i * TQ, TQ)
                for hh in range(hg):
                    pp = hh // 2
                    dst_b = db_s[slot, hh]
                    dv_s[hh] += _nn(pb_s[slot, hh], do_ref[pl.ds(qrow, TQ), 128 * pp:128 * (pp + 1)])
                    dk_s[hh] += _nn(dst_b, qa_ref[hh, pl.ds(qrow, TQ), :])
                    dqt_s[hh, i] += _nn(kt_ref[hh, j], dst_b)
                    dfa_s[hh] += _nt(ones_b, dst_b)

            def step(i, carry):
                grads(i - 1)
                elementwise(i, False)
                scores(jnp.minimum(i + 1, nq - 1))
                return carry

            i0 = j // ratio
            scores(i0)
            elementwise(i0, True)
            scores(jnp.minimum(i0 + 1, nq - 1))
            lax.fori_loop(i0 + 1, nq, step, 0)
            grads(nq - 1)
            for pp in range(hg // 2):
                cols = slice(128 * pp, 128 * (pp + 1))
                dk_ref[pl.ds(krow, TK), cols] = jnp.concatenate(
                    [dk_s[2 * pp][:, :HEAD_DIM], dk_s[2 * pp + 1][:, :HEAD_DIM]], axis=1).astype(BF16)
                dv_ref[pl.ds(krow, TK), cols] = jnp.where(lane_k < HEAD_DIM, dv_s[2 * pp], dv_s[2 * pp + 1]).astype(BF16)
            for hh in range(hg):
                dfk_s[hh, j] = dfa_s[hh]
            return 0

        lax.fori_loop(0, nkb, kblock, 0)
        for pp in range(hg // 2):
            for ii in range(nq):
                parts = []
                for e in range(2):
                    dqt = dqt_s[2 * pp + e, ii]
                    parts.append(_tn(dqt[0:HEAD_DIM, :].astype(BF16), eye) * SCALE)
                    for kk in range(ratio):
                        jj = ii * ratio + kk
                        df_ref[pp, e:e + 1, jj * TK:(jj + 1) * TK] = (dqt[67:68, kk * TK:(kk + 1) * TK]
                                                                     - dfk_s[2 * pp + e, jj][0:1, :])
                dq_ref[ii * TQ:(ii + 1) * TQ, 128 * pp:128 * (pp + 1)] = jnp.concatenate(parts, axis=1).astype(BF16)

    aug_blk = pl.BlockSpec((hg, seq, 128), lambda b, g: (g, b, 0))
    pair_blk = pl.BlockSpec((seq, 64 * hg), lambda b, g: (b, g))
    row_blk = pl.BlockSpec((hg // 2, 2, seq), lambda b, g: (g, 0, b))
    return _pallas(
        body, name="fox_bwd", grid=(t_tok // seq, N_HEADS // hg),
        in_specs=[aug_blk, aug_blk, pl.BlockSpec((hg, nkb, 128, TK), lambda b, g: (g, b, 0, 0)),
                  pl.BlockSpec((seq, 64 * hg), lambda b, g: (b, C_BV // (64 * hg) + g)), pair_blk, pair_blk, row_blk],
        out_specs=[pair_blk, pair_blk, pair_blk, row_blk],
        out_shape=[jax.ShapeDtypeStruct((t_tok, 512), BF16)] * 3
        + [jax.ShapeDtypeStruct((N_HEADS // 2, 2, t_tok), F32)],
        scratch_shapes=[pltpu.VMEM((hg, nq, 128, TQ), F32), pltpu.VMEM((hg, nq, 2, 8, TQ), F32),
                        pltpu.VMEM((hg, nkb, 8, TK), F32), pltpu.VMEM((hg, TK, 128), F32),
                        pltpu.VMEM((hg, TK, 128), F32), pltpu.VMEM((hg, 8, TK), F32),
                        pltpu.VMEM((2, hg, TK, TQ), F32), pltpu.VMEM((2, hg, TK, TQ), F32),
                        pltpu.VMEM((2, hg, TK, TQ), BF16), pltpu.VMEM((2, hg, TK, TQ), BF16)],
        compiler_params=_cp(("arbitrary", "arbitrary"), VMEM_BIG))(q_aug, k_aug, kt, qkvg, du_b, b_out, lse)


def _fox_bwd_t_old(q_aug, k_aug, kt, qkvg, du_b, b_out, lse, seq):
    t_tok = qkvg.shape[0]
    nq = seq // TQ
    nkb = seq // TK
    ratio = TQ // TK

    def body(qa_ref, ka_ref, kt_ref, v_ref, do_ref, o_ref, lse_ref, dq_ref, dk_ref, dv_ref, df_ref,
             dqt_s, out_s, row_s, dfk_s):
        ones_b = jnp.ones((8, TQ), BF16)
        ones_f = jnp.ones((8, HEAD_DIM), F32)
        eye = _eye(HEAD_DIM, BF16)
        for e in range(2):
            lo, hi = e * HEAD_DIM, (e + 1) * HEAD_DIM
            for ii in range(nq):
                rows = slice(ii * TQ, (ii + 1) * TQ)
                do = do_ref[rows, :][:, lo:hi].astype(F32)
                ov = o_ref[rows, :][:, lo:hi].astype(F32)
                row_s[ii, 0] = _nt(ones_f, do * ov, HI)
                row_s[ii, 1] = jnp.broadcast_to(lse_ref[e:e + 1, ii * TQ:(ii + 1) * TQ], (8, TQ))
                dqt_s[ii] = jnp.zeros((128, TQ), F32)

            def kblock(j, _):
                krow = pl.multiple_of(j * TK, TK)
                kj = ka_ref[e, pl.ds(krow, TK), :]
                ktj = kt_ref[e, j]
                vj = v_ref[pl.ds(krow, TK), :][:, lo:hi]
                spos = j * TK + lax.broadcasted_iota(jnp.int32, (TK, 1), 0)

                def qblock(i, carry, masked):
                    dk_acc, dv_acc, dfk = carry
                    qrow = pl.multiple_of(i * TQ, TQ)
                    qa = qa_ref[e, pl.ds(qrow, TQ), :]
                    doh = do_ref[pl.ds(qrow, TQ), :][:, lo:hi]
                    pt = jnp.exp(_nt(kj, qa) - row_s[i, 1][0:1, :])
                    if masked:
                        tpos = i * TQ + lax.broadcasted_iota(jnp.int32, (1, TQ), 1)
                        pt = jnp.where(spos <= tpos, pt, 0.0)
                    dst = pt * (_nt(vj, doh) - row_s[i, 0][0:1, :])
                    dst_b = dst.astype(BF16)
                    dv_acc = dv_acc + _nn(pt.astype(BF16), doh)
                    dk_acc = dk_acc + _nn(dst_b, qa)
                    dqt_s[i] += _nn(ktj, dst_b)
                    dfk = dfk + _nt(ones_b, dst_b)
                    return dk_acc, dv_acc, dfk

                i0 = j // ratio
                carry = (jnp.zeros((TK, 128), F32), jnp.zeros((TK, HEAD_DIM), F32), jnp.zeros((8, TK), F32))
                carry = qblock(i0, carry, True)
                dk_acc, dv_acc, dfk = lax.fori_loop(i0 + 1, nq, functools.partial(qblock, masked=False), carry)
                out_s[1, e, pl.ds(krow, TK), :] = dk_acc[:, :HEAD_DIM]
                out_s[2, e, pl.ds(krow, TK), :] = dv_acc
                dfk_s[j] = dfk
                return 0

            lax.fori_loop(0, nkb, kblock, 0)
            for ii in range(nq):
                dqt = dqt_s[ii]
                out_s[0, e, ii * TQ:(ii + 1) * TQ, :] = _tn(dqt[0:HEAD_DIM, :].astype(BF16), eye) * SCALE
                for kk in range(ratio):
                    jj = ii * ratio + kk
                    df_ref[e:e + 1, jj * TK:(jj + 1) * TK] = dqt[67:68, kk * TK:(kk + 1) * TK] - dfk_s[jj][0:1, :]
        for k, ref in enumerate((dq_ref, dk_ref, dv_ref)):
            ref[...] = jnp.concatenate([out_s[k, 0], out_s[k, 1]], axis=1).astype(BF16)

    aug_blk = pl.BlockSpec((2, seq, 128), lambda b, p: (p, b, 0))
    pair_blk = pl.BlockSpec((seq, 128), lambda b, p: (b, p))
    row_blk = pl.BlockSpec((None, 2, seq), lambda b, p: (p, 0, b))
    return _pallas(
        body, name="fox_bwd", grid=(t_tok // seq, N_HEADS // 2),
        in_specs=[aug_blk, aug_blk, pl.BlockSpec((2, nkb, 128, TK), lambda b, p: (p, b, 0, 0)),
                  pl.BlockSpec((seq, 128), lambda b, p: (b, C_BV // 128 + p)), pair_blk, pair_blk, row_blk],
        out_specs=[pair_blk, pair_blk, pair_blk, row_blk],
        out_shape=[jax.ShapeDtypeStruct((t_tok, 512), BF16)] * 3
        + [jax.ShapeDtypeStruct((N_HEADS // 2, 2, t_tok), F32)],
        scratch_shapes=[pltpu.VMEM((nq, 128, TQ), F32), pltpu.VMEM((3, 2, seq, HEAD_DIM), F32),
                        pltpu.VMEM((nq, 2, 8, TQ), F32), pltpu.VMEM((nkb, 8, TK), F32)],
        compiler_params=_cp(("arbitrary", "arbitrary"), VMEM_BIG))(q_aug, k_aug, kt, qkvg, du_b, b_out, lse)


def _fox_fwd(qkvg, f_row, f_col, seq):
    t_tok = qkvg.shape[0]
    nq = seq // TQ

    def body(q_ref, k_ref, v_ref, fr_ref, fc_ref, o_ref, lse_ref, fk_s):
        i = pl.program_id(1)
        for jj in range(nq):
            fk_s[jj] = fr_ref[:, jj * TQ:(jj + 1) * TQ]
        fcol = fc_ref[...]
        tpos = i * TQ + lax.broadcasted_iota(jnp.int32, (TQ, 1), 0)
        lane = lax.broadcasted_iota(jnp.int32, (TQ, 128), 1)
        lse_tile = jnp.zeros((TQ, 128), F32)
        for p in range(N_HEADS // 2):
            q2 = q_ref[:, 128 * p:128 * (p + 1)]
            qs = [q2[:, :HEAD_DIM], q2[:, HEAD_DIM:]]
            fqs = [_col(fcol, 2 * p + e) for e in range(2)]

            def kblock(j, carry):
                row0 = pl.multiple_of(j * TQ, TQ)
                k2 = k_ref[pl.ds(row0, TQ), 128 * p:128 * (p + 1)]
                v2 = v_ref[pl.ds(row0, TQ), 128 * p:128 * (p + 1)]
                fk8 = fk_s[j]
                spos = j * TQ + lax.broadcasted_iota(jnp.int32, (1, TQ), 1)
                keep = spos <= tpos
                new = []
                for e in range(2):
                    m, l, acc = carry[3 * e:3 * e + 3]
                    kh = k2[:, e * HEAD_DIM:(e + 1) * HEAD_DIM]
                    vh = v2[:, e * HEAD_DIM:(e + 1) * HEAD_DIM]
                    s = _nt(qs[e], kh) * SCALE + (fqs[e] - fk8[2 * p + e:2 * p + e + 1, :])
                    s = jnp.where(keep, s, NEG)
                    m_new = jnp.maximum(m, jnp.max(s, axis=1, keepdims=True))
                    alpha = jnp.exp(m - m_new)
                    pe = jnp.exp(s - m_new)
                    l = alpha * l + jnp.sum(pe, axis=1, keepdims=True)
                    acc = alpha * acc + _nn(pe.astype(BF16), vh)
                    new += [m_new, l, acc]
                return tuple(new)

            init = (jnp.full((TQ, 1), NEG, F32), jnp.zeros((TQ, 1), F32), jnp.zeros((TQ, HEAD_DIM), F32)) * 2
            res = lax.fori_loop(0, i + 1, kblock, init)
            outs = []
            for e in range(2):
                m, l, acc = res[3 * e:3 * e + 3]
                outs.append(acc / l)
                lse_tile = jnp.where(lane == 2 * p + e, m + jnp.log(l), lse_tile)
            o_ref[:, 128 * p:128 * (p + 1)] = jnp.concatenate(outs, axis=1).astype(BF16)
        lse_ref[...] = lse_tile

    return _pallas(
        body, name="fox_fwd", grid=(t_tok // seq, nq),
        in_specs=[pl.BlockSpec((TQ, 512), lambda b, i: (b * nq + i, C_BQ // 512)),
                  pl.BlockSpec((seq, 512), lambda b, i: (b, C_BK // 512)),
                  pl.BlockSpec((seq, 512), lambda b, i: (b, C_BV // 512)),
                  pl.BlockSpec((N_HEADS, seq), lambda b, i: (0, b)),
                  pl.BlockSpec((TQ, 128), lambda b, i: (b * nq + i, 0))],
        out_specs=[pl.BlockSpec((TQ, 512), lambda b, i: (b * nq + i, 0)),
                   pl.BlockSpec((TQ, 128), lambda b, i: (b * nq + i, 0))],
        out_shape=[jax.ShapeDtypeStruct((t_tok, 512), BF16), jax.ShapeDtypeStruct((t_tok, 128), F32)],
        scratch_shapes=[pltpu.VMEM((nq, N_HEADS, TQ), F32)],
        compiler_params=_cp(("arbitrary", "arbitrary"), VMEM_MID))(qkvg, qkvg, qkvg, f_row, f_col)


def _fox_bwd(qkvg, du_b, b_out, lse, f_row, f_col, seq):
    t_tok = qkvg.shape[0]
    nq = seq // TQ

    def body(q_ref, k_ref, v_ref, do_ref, o_ref, lse_ref, fr_ref, fc_ref,
             dq_ref, dk_ref, dv_ref, df_ref, dq_s, dk_s, dv_s, col_s, df_s, fk_s):
        p = pl.program_id(1)
        for jj in range(nq):
            fk_s[jj] = fr_ref[:, jj * TQ:(jj + 1) * TQ]
        eye = (lax.broadcasted_iota(jnp.int32, (TQ, TQ), 0) == lax.broadcasted_iota(jnp.int32, (TQ, TQ), 1)).astype(F32)
        for e in range(2):
            h = 2 * p + e
            lo, hi = e * HEAD_DIM, (e + 1) * HEAD_DIM
            for ii in range(nq):
                rows = slice(ii * TQ, (ii + 1) * TQ)
                do = do_ref[rows, :][:, lo:hi].astype(F32)
                ov = o_ref[rows, :][:, lo:hi].astype(F32)
                col_s[0, rows, :] = jnp.sum(do * ov, axis=1, keepdims=True)
                col_s[1, rows, :] = _col(lse_ref[rows, :], h)
                col_s[2, rows, :] = _col(fc_ref[rows, :], h)
                dq_s[rows, :] = jnp.zeros((TQ, HEAD_DIM), F32)
                df_s[ii] = jnp.zeros((8, TQ), F32)
                col_s[3, rows, :] = jnp.zeros((TQ, 1), F32)

            def kblock(j, _):
                krow = pl.multiple_of(j * TQ, TQ)
                kh = k_ref[pl.ds(krow, TQ), :][:, lo:hi]
                vh = v_ref[pl.ds(krow, TQ), :][:, lo:hi]
                fk = _row(fk_s[j], h)
                spos = j * TQ + lax.broadcasted_iota(jnp.int32, (1, TQ), 1)

                def qblock(i, carry):
                    dk_acc, dv_acc, dfk = carry
                    qrow = pl.multiple_of(i * TQ, TQ)
                    qh = q_ref[pl.ds(qrow, TQ), :][:, lo:hi]
                    doh = do_ref[pl.ds(qrow, TQ), :][:, lo:hi]
                    delta = col_s[0, pl.ds(qrow, TQ), :]
                    lse_q = col_s[1, pl.ds(qrow, TQ), :]
                    fq = col_s[2, pl.ds(qrow, TQ), :]
                    tpos = i * TQ + lax.broadcasted_iota(jnp.int32, (TQ, 1), 0)
                    s = _nt(qh, kh) * SCALE + (fq - fk)
                    pr = jnp.where(spos <= tpos, jnp.exp(s - lse_q), 0.0)
                    dp = _nt(doh, vh)
                    ds = pr * (dp - delta)
                    ds_b = ds.astype(BF16)
                    dv_acc = dv_acc + _tn(pr.astype(BF16), doh)
                    dk_acc = dk_acc + _tn(ds_b, qh)
                    dq_s[pl.ds(qrow, TQ), :] += _nn(ds_b, kh)
                    col_s[3, pl.ds(qrow, TQ), :] += jnp.sum(ds, axis=1, keepdims=True)
                    dfk = dfk + jnp.sum(ds, axis=0, keepdims=True)
                    return dk_acc, dv_acc, dfk

                zero = jnp.zeros((TQ, HEAD_DIM), F32)
                dk_acc, dv_acc, dfk = lax.fori_loop(j, nq, qblock, (zero, zero, jnp.zeros((1, TQ), F32)))
                dk_s[e, pl.ds(krow, TQ), :] = dk_acc * SCALE
                dv_s[e, pl.ds(krow, TQ), :] = dv_acc
                df_s[j] -= jnp.broadcast_to(dfk, (8, TQ))
                return 0

            lax.fori_loop(0, nq, kblock, 0)
            dq_s2 = dq_s[...] * SCALE
            dk_s[2 + e] = dq_s2
            for ii in range(nq):
                dfq = jnp.broadcast_to(col_s[3, ii * TQ:(ii + 1) * TQ, :], (TQ, 128))
                df_ref[e:e + 1, ii * TQ:(ii + 1) * TQ] = _tn(dfq, eye, HI)[0:1, :] + df_s[ii][0:1, :]
        dq_ref[...] = jnp.concatenate([dk_s[2], dk_s[3]], axis=1).astype(BF16)
        dk_ref[...] = jnp.concatenate([dk_s[0], dk_s[1]], axis=1).astype(BF16)
        dv_ref[...] = jnp.concatenate([dv_s[0], dv_s[1]], axis=1).astype(BF16)

    blk = lambda off: pl.BlockSpec((seq, 128), lambda b, p: (b, off // 128 + p))
    out_blk = pl.BlockSpec((seq, 128), lambda b, p: (b, p))
    return _pallas(
        body, name="fox_bwd", grid=(t_tok // seq, N_HEADS // 2),
        in_specs=[blk(C_BQ), blk(C_BK), blk(C_BV), out_blk, out_blk,
                  pl.BlockSpec((seq, 128), lambda b, p: (b, 0)),
                  pl.BlockSpec((N_HEADS, seq), lambda b, p: (0, b)),
                  pl.BlockSpec((seq, 128), lambda b, p: (b, 0))],
        out_specs=[out_blk, out_blk, out_blk, pl.BlockSpec((None, 2, seq), lambda b, p: (p, 0, b))],
        out_shape=[jax.ShapeDtypeStruct((t_tok, 512), BF16)] * 3
        + [jax.ShapeDtypeStruct((N_HEADS // 2, 2, t_tok), F32)],
        scratch_shapes=[pltpu.VMEM((seq, HEAD_DIM), F32), pltpu.VMEM((4, seq, HEAD_DIM), F32),
                        pltpu.VMEM((2, seq, HEAD_DIM), F32), pltpu.VMEM((4, seq, 1), F32),
                        pltpu.VMEM((nq, 8, TQ), F32), pltpu.VMEM((nq, N_HEADS, TQ), F32)],
        compiler_params=_cp(("arbitrary", "arbitrary"), VMEM_BIG))(qkvg, qkvg, qkvg, du_b, b_out, lse, f_row, f_col)


def _swa_window(k_ref, v_ref, n):
    prev = pl.multiple_of(jnp.maximum(n - 1, 0) * BLOCK, BLOCK)
    cur = pl.multiple_of(n * BLOCK, BLOCK)
    kwin = jnp.concatenate([k_ref[pl.ds(prev, BLOCK), :], k_ref[pl.ds(cur, BLOCK), :]], axis=0)
    vwin = jnp.concatenate([v_ref[pl.ds(prev, BLOCK), :], v_ref[pl.ds(cur, BLOCK), :]], axis=0)
    ti = lax.broadcasted_iota(jnp.int32, (BLOCK, 2 * BLOCK), 0)
    sj = lax.broadcasted_iota(jnp.int32, (BLOCK, 2 * BLOCK), 1)
    rel = ti - sj + BLOCK
    first_key = jnp.where(n > 0, 0, BLOCK)
    mask = jnp.logical_and(jnp.logical_and(rel >= 0, rel < BLOCK), sj >= first_key)
    return kwin, vwin, mask, prev, cur


def _head_cols(ref, h):
    pair = ref[:, 128 * (h // 2):128 * (h // 2 + 1)]
    return pair[:, (h % 2) * HEAD_DIM:(h % 2 + 1) * HEAD_DIM]


def _swa_logits(q_ref, kwin, bias_ref, h, mask):
    hk = h // KV_GROUP
    s = _nt(_head_cols(q_ref, h), kwin[:, hk * HEAD_DIM:(hk + 1) * HEAD_DIM]) * SCALE + bias_ref[h]
    return jnp.where(mask, s, NEG)


def _swa_fwd(qkvg, bias, sinks, seq):
    t_tok = qkvg.shape[0]
    nb = seq // BLOCK

    def body(sink_ref, q_ref, k_ref, v_ref, bias_ref, o_ref, lse_ref, s_s, p_s, den_s):
        n = pl.program_id(1)
        kwin, vwin, mask, _, _ = _swa_window(k_ref, v_ref, n)
        for h in range(N_HEADS):
            s_s[h] = _swa_logits(q_ref, kwin, bias_ref, h, mask)
        lane = lax.broadcasted_iota(jnp.int32, (BLOCK, 128), 1)
        lse_tile = jnp.zeros((BLOCK, 128), F32)
        for h in range(N_HEADS):
            s = s_s[h]
            sink = sink_ref[h]
            m = jnp.maximum(jnp.max(s, axis=1, keepdims=True), sink)
            pe = jnp.exp(s - m)
            den = jnp.sum(pe, axis=1, keepdims=True) + jnp.exp(sink - m)
            p_s[h] = pe.astype(BF16)
            den_s[h] = den
            lse_tile = jnp.where(lane == h, m + jnp.log(den), lse_tile)
        lse_ref[...] = lse_tile
        for pr in range(N_HEADS // 2):
            outs = []
            for h in (2 * pr, 2 * pr + 1):
                hk = h // KV_GROUP
                outs.append(_nn(p_s[h], vwin[:, hk * HEAD_DIM:(hk + 1) * HEAD_DIM]) / den_s[h])
            o_ref[:, 128 * pr:128 * (pr + 1)] = jnp.concatenate(outs, axis=1).astype(BF16)

    return _pallas(
        body, name="swa_fwd", grid=(t_tok // seq, nb),
        in_specs=[pl.BlockSpec(memory_space=pltpu.SMEM),
                  pl.BlockSpec((BLOCK, 512), lambda b, n: (b * nb + n, C_AQ // 512)),
                  pl.BlockSpec((seq, 128), lambda b, n: (b, C_AK // 128)),
                  pl.BlockSpec((seq, 128), lambda b, n: (b, C_AV // 128)),
                  pl.BlockSpec((N_HEADS, BLOCK, 2 * BLOCK), lambda b, n: (0, 0, 0))],
        out_specs=[pl.BlockSpec((BLOCK, 512), lambda b, n: (b * nb + n, 0)),
                   pl.BlockSpec((BLOCK, 128), lambda b, n: (b * nb + n, 0))],
        out_shape=[jax.ShapeDtypeStruct((t_tok, 512), BF16), jax.ShapeDtypeStruct((t_tok, 128), F32)],
        scratch_shapes=[pltpu.VMEM((N_HEADS, BLOCK, 2 * BLOCK), F32), pltpu.VMEM((N_HEADS, BLOCK, 2 * BLOCK), BF16),
                        pltpu.VMEM((N_HEADS, BLOCK, 1), F32)],
        compiler_params=_cp(("arbitrary", "arbitrary"), VMEM_MID))(sinks, qkvg, qkvg, qkvg, bias)


def _swa_bwd(qkvg, du_a, a_out, lse, bias, sinks, seq):
    t_tok = qkvg.shape[0]
    nb = seq // BLOCK

    def body(sink_ref, q_ref, k_ref, v_ref, do_ref, o_ref, lse_ref, bias_ref,
             dq_ref, dkv_ref, dbias_ref, dsink_ref, kv_s, s_s, dp_s, pb_s, db_s):
        b, n = pl.program_id(0), pl.program_id(1)

        @pl.when(jnp.logical_and(b == 0, n == 0))
        def _():
            dbias_ref[...] = jnp.zeros_like(dbias_ref)
            dsink_ref[...] = jnp.zeros_like(dsink_ref)

        @pl.when(n == 0)
        def _():
            kv_s[...] = jnp.zeros_like(kv_s)

        kwin, vwin, mask, prev, cur = _swa_window(k_ref, v_ref, n)
        for h in range(N_HEADS):
            hk = h // KV_GROUP
            s_s[h] = _swa_logits(q_ref, kwin, bias_ref, h, mask)
            dp_s[h] = _nt(_head_cols(do_ref, h), vwin[:, hk * HEAD_DIM:(hk + 1) * HEAD_DIM])
        lse_tile = lse_ref[...]
        for h in range(N_HEADS):
            delta = jnp.sum(_head_cols(do_ref, h).astype(F32) * _head_cols(o_ref, h).astype(F32), axis=1, keepdims=True)
            lse_h = _col(lse_tile, h)
            pe = jnp.exp(s_s[h] - lse_h)
            ds = pe * (dp_s[h] - delta)
            dbias_ref[h] += ds
            psink = jnp.exp(sink_ref[h] - lse_h)
            dsink_ref[h:h + 1, :] += jnp.broadcast_to(jnp.sum(-psink * delta, axis=0, keepdims=True), (1, 128))
            pb_s[h] = pe.astype(BF16)
            db_s[h] = ds.astype(BF16)
        for pr in range(N_HEADS // 2):
            dqs = []
            for h in (2 * pr, 2 * pr + 1):
                hk = h // KV_GROUP
                dqs.append(_nn(db_s[h], kwin[:, hk * HEAD_DIM:(hk + 1) * HEAD_DIM]) * SCALE)
            dq_ref[:, 128 * pr:128 * (pr + 1)] = jnp.concatenate(dqs, axis=1).astype(BF16)
        dks, dvs = [], []
        for hk in range(N_HEADS // KV_GROUP):
            dk = jnp.zeros((2 * BLOCK, HEAD_DIM), F32)
            dv = jnp.zeros((2 * BLOCK, HEAD_DIM), F32)
            for h in range(hk * KV_GROUP, (hk + 1) * KV_GROUP):
                dk = dk + _tn(db_s[h], _head_cols(q_ref, h))
                dv = dv + _tn(pb_s[h], _head_cols(do_ref, h))
            dks.append(dk * SCALE)
            dvs.append(dv)
        upd = jnp.concatenate(dks + dvs, axis=1)
        kv_s[pl.ds(prev, BLOCK), :] += upd[:BLOCK]
        kv_s[pl.ds(cur, BLOCK), :] += upd[BLOCK:]

        @pl.when(n == nb - 1)
        def _():
            dkv_ref[...] = kv_s[...].astype(BF16)

    return _pallas(
        body, name="swa_bwd", grid=(t_tok // seq, nb),
        in_specs=[pl.BlockSpec(memory_space=pltpu.SMEM),
                  pl.BlockSpec((BLOCK, 512), lambda b, n: (b * nb + n, C_AQ // 512)),
                  pl.BlockSpec((seq, 128), lambda b, n: (b, C_AK // 128)),
                  pl.BlockSpec((seq, 128), lambda b, n: (b, C_AV // 128)),
                  pl.BlockSpec((BLOCK, 512), lambda b, n: (b * nb + n, 0)),
                  pl.BlockSpec((BLOCK, 512), lambda b, n: (b * nb + n, 0)),
                  pl.BlockSpec((BLOCK, 128), lambda b, n: (b * nb + n, 0)),
                  pl.BlockSpec((N_HEADS, BLOCK, 2 * BLOCK), lambda b, n: (0, 0, 0))],
        out_specs=[pl.BlockSpec((BLOCK, 512), lambda b, n: (b * nb + n, 0)),
                   pl.BlockSpec((seq, 256), lambda b, n: (b, 0)),
                   pl.BlockSpec((N_HEADS, BLOCK, 2 * BLOCK), lambda b, n: (0, 0, 0)),
                   pl.BlockSpec((N_HEADS, 128), lambda b, n: (0, 0))],
        out_shape=[jax.ShapeDtypeStruct((t_tok, 512), BF16), jax.ShapeDtypeStruct((t_tok, 256), BF16),
                   jax.ShapeDtypeStruct((N_HEADS, BLOCK, 2 * BLOCK), F32), jax.ShapeDtypeStruct((N_HEADS, 128), F32)],
        scratch_shapes=[pltpu.VMEM((seq, 256), F32),
                        pltpu.VMEM((N_HEADS, BLOCK, 2 * BLOCK), F32), pltpu.VMEM((N_HEADS, BLOCK, 2 * BLOCK), F32),
                        pltpu.VMEM((N_HEADS, BLOCK, 2 * BLOCK), BF16), pltpu.VMEM((N_HEADS, BLOCK, 2 * BLOCK), BF16)],
        compiler_params=_cp(("arbitrary", "arbitrary"), VMEM_MID))(sinks, qkvg, qkvg, qkvg, du_a, a_out, lse, bias)


def _out_proj(name, u_parts, gate_arr, gate_blk, w_out, x, gmod, seq):
    t_tok = x.shape[0]
    nu = len(u_parts)

    def body(*refs):
        u_refs = refs[:nu]
        g_ref, w_ref, x_ref, gm_ref, yg_ref, y_ref, xn_ref = refs[nu:]
        u = jnp.concatenate([r[...].astype(F32) for r in u_refs], axis=1) if nu > 1 else u_refs[0][...].astype(F32)
        yg = (u * _silu(g_ref[...].astype(F32))).astype(BF16)
        yg_ref[...] = yg
        y = _nn(yg, w_ref[...])
        y_ref[...] = y.astype(BF16)
        xn_ref[...] = x_ref[...] + gm_ref[...] * y

    row = lambda w: pl.BlockSpec((TM, w), lambda i: (i, 0))
    in_specs = [row(u.shape[1]) for u in u_parts]
    in_specs += [pl.BlockSpec((TM, D_MODEL), lambda i: (i, gate_blk)),
                 pl.BlockSpec((D_MODEL, D_MODEL), lambda i: (0, 0)), row(D_MODEL),
                 pl.BlockSpec((None, 1, D_MODEL), lambda i: (i * TM // seq, 0, 0))]
    return _pallas(
        body, name=name, grid=(t_tok // TM,), in_specs=in_specs,
        out_specs=[row(D_MODEL)] * 3,
        out_shape=[jax.ShapeDtypeStruct((t_tok, D_MODEL), BF16)] * 2 + [jax.ShapeDtypeStruct((t_tok, D_MODEL), F32)],
        compiler_params=_cp(("arbitrary",), VMEM_MID))(*u_parts, gate_arr, w_out, x, gmod)


def _out_proj_bwd(name, dxn, gmod, y, w_out, seq, attn=None):
    t_tok = dxn.shape[0]
    tiles_per_seq = seq // TM

    def body(*refs):
        if attn is None:
            dxn_ref, gm_ref, y_ref, w_ref, dy_ref, dgm_ref, dyg_ref = refs
        else:
            dxn_ref, gm_ref, y_ref, w_ref, a_ref, b_ref, g_ref, dy_ref, dgm_ref, dua_ref, dub_ref, dg_ref = refs
        i = pl.program_id(0)
        dxv = dxn_ref[...]
        dy = (dxv * gm_ref[...]).astype(BF16)
        dy_ref[...] = dy

        @pl.when(i % tiles_per_seq == 0)
        def _():
            dgm_ref[...] = jnp.zeros_like(dgm_ref)

        dgm_ref[...] += jnp.sum(dxv * y_ref[...].astype(F32), axis=0, keepdims=True)
        dyg = _nt(dy, w_ref[...])
        if attn is None:
            dyg_ref[...] = dyg
        else:
            gt = g_ref[...].astype(F32)
            du = dyg * _silu(gt)
            dua_ref[...] = du[:, :512].astype(BF16)
            dub_ref[...] = du[:, 512:].astype(BF16)
            u = jnp.concatenate([a_ref[...].astype(F32), b_ref[...].astype(F32)], axis=1)
            dg_ref[...] = (dyg * u * _dsilu(gt)).astype(BF16)

    row = lambda w: pl.BlockSpec((TM, w), lambda i: (i, 0))
    mod_spec = pl.BlockSpec((None, 1, D_MODEL), lambda i: (i * TM // seq, 0, 0))
    in_specs = [row(D_MODEL), mod_spec, row(D_MODEL), pl.BlockSpec((D_MODEL, D_MODEL), lambda i: (0, 0))]
    out_specs = [row(D_MODEL), mod_spec]
    out_shape = [jax.ShapeDtypeStruct((t_tok, D_MODEL), BF16), jax.ShapeDtypeStruct(gmod.shape, F32)]
    args = [dxn, gmod, y, w_out]
    if attn is None:
        out_specs.append(row(D_MODEL))
        out_shape.append(jax.ShapeDtypeStruct((t_tok, D_MODEL), F32))
    else:
        in_specs += [row(512), row(512), pl.BlockSpec((TM, D_MODEL), lambda i: (i, C_GATE // D_MODEL))]
        out_specs += [row(512), row(512), row(D_MODEL)]
        out_shape += [jax.ShapeDtypeStruct((t_tok, 512), BF16)] * 2 + [jax.ShapeDtypeStruct((t_tok, D_MODEL), BF16)]
        args += list(attn)
    return _pallas(body, name=name, grid=(t_tok // TM,), in_specs=in_specs, out_specs=out_specs,
                   out_shape=out_shape, compiler_params=_cp(("arbitrary",), VMEM_MID))(*args)


def _norm_bwd(name, parts, w, x, g, scale, dxn, seq, rows_part=None):
    t_tok = x.shape[0]
    npart = len(parts)
    w3d = w.ndim == 3
    tiles_per_seq = seq // TM
    nrow_in = 0 if rows_part is None else 2

    def body(*refs):
        p_refs = refs[:npart]
        w_ref, x_ref, g_ref, sc_ref, dxn_ref = refs[npart:npart + 5]
        dx_ref, dss_ref, dg_ref = refs[npart + 5 + nrow_in:]
        i = pl.program_id(0)
        dh = jnp.zeros((TM, D_MODEL), F32)
        if rows_part is not None:
            r_ref, wr_ref = refs[npart + 5:npart + 7]
            dh = dh + _tn(r_ref[...].astype(BF16), wr_ref[...])
        for (arr, off), p_ref in zip(parts, p_refs):
            width = arr.shape[1]
            for j in range(width // 256):
                pj = p_ref[:, j * 256:(j + 1) * 256]
                c0 = off + j * 256
                wj = w_ref[c0 // 256] if w3d else w_ref[:, c0:c0 + 256]
                dh = dh + _nt(pj, wj)
        xv = x_ref[...]
        rstd = lax.rsqrt(jnp.mean(xv * xv, axis=-1, keepdims=True) + EPS)
        xhat = xv * rstd
        gv = g_ref[...]
        nrm = xhat * gv

        @pl.when(i % tiles_per_seq == 0)
        def _():
            dss_ref[...] = jnp.zeros_like(dss_ref)

        @pl.when(i == 0)
        def _():
            dg_ref[...] = jnp.zeros_like(dg_ref)

        dss_ref[0:1, :] += jnp.sum(dh, axis=0, keepdims=True)
        dss_ref[1:2, :] += jnp.sum(dh * nrm, axis=0, keepdims=True)
        dn = dh * (1.0 + sc_ref[...])
        dg_ref[0:1, :] += jnp.sum(dn * xhat, axis=0, keepdims=True)
        dxhat = dn * gv
        dx_ref[...] = rstd * (dxhat - xhat * jnp.mean(dxhat * xhat, axis=-1, keepdims=True)) + dxn_ref[...]

    row = lambda wd: pl.BlockSpec((TM, wd), lambda i: (i, 0))
    w_spec = (pl.BlockSpec(w.shape, lambda i: (0, 0, 0)) if w3d else pl.BlockSpec(w.shape, lambda i: (0, 0)))
    in_specs = [row(a.shape[1]) for a, _ in parts]
    in_specs += [w_spec, row(D_MODEL), pl.BlockSpec((1, D_MODEL), lambda i: (0, 0)),
                 pl.BlockSpec((None, 1, D_MODEL), lambda i: (i * TM // seq, 0, 0)), row(D_MODEL)]
    args = [a for a, _ in parts] + [w, x, g, scale, dxn]
    if rows_part is not None:
        in_specs += [pl.BlockSpec((8, TM), lambda i: (0, i)), pl.BlockSpec((8, D_MODEL), lambda i: (0, 0))]
        args += list(rows_part)
    nseq = t_tok // seq
    return _pallas(
        body, name=name, grid=(t_tok // TM,), in_specs=in_specs,
        out_specs=[row(D_MODEL), pl.BlockSpec((None, 8, D_MODEL), lambda i: (i * TM // seq, 0, 0)),
                   pl.BlockSpec((8, D_MODEL), lambda i: (0, 0))],
        out_shape=[jax.ShapeDtypeStruct((t_tok, D_MODEL), F32), jax.ShapeDtypeStruct((nseq, 8, D_MODEL), F32),
                   jax.ShapeDtypeStruct((8, D_MODEL), F32)],
        compiler_params=_cp(("arbitrary",), VMEM_BIG))(*args)


def _dw(name, a, parts, blocked=None):
    t_tok, ka = a.shape
    tt = 512
    npart = len(parts)
    nt = t_tok // tt

    def body(*refs):
        a_ref = refs[0]
        p_refs = refs[1:1 + npart]
        o_refs = refs[1 + npart:1 + 2 * npart]
        acc_refs = refs[1 + 2 * npart:]
        t = pl.program_id(0)
        av = a_ref[...]
        for p_ref, acc in zip(p_refs, acc_refs):
            upd = _tn(av, p_ref[...])

            @pl.when(t == 0)
            def _():
                acc[...] = upd

            @pl.when(t > 0)
            def _():
                acc[...] += upd

        @pl.when(t == nt - 1)
        def _():
            for o_ref, acc in zip(o_refs, acc_refs):
                if blocked is None:
                    o_ref[...] = acc[...].astype(BF16)
                else:
                    for j in range(o_ref.shape[0]):
                        o_ref[j] = acc[:, j * blocked:(j + 1) * blocked].astype(BF16)

    in_specs = [pl.BlockSpec((tt, ka), lambda t: (t, 0))]
    in_specs += [pl.BlockSpec((tt, p.shape[1]), lambda t: (t, 0)) for p in parts]
    if blocked is None:
        out_shape = [jax.ShapeDtypeStruct((ka, p.shape[1]), BF16) for p in parts]
        out_specs = [pl.BlockSpec((ka, p.shape[1]), lambda t: (0, 0)) for p in parts]
    else:
        out_shape = [jax.ShapeDtypeStruct((p.shape[1] // blocked, ka, blocked), BF16) for p in parts]
        out_specs = [pl.BlockSpec((p.shape[1] // blocked, ka, blocked), lambda t: (0, 0, 0)) for p in parts]
    return _pallas(body, name=name, grid=(nt,), in_specs=in_specs, out_specs=out_specs, out_shape=out_shape,
                   scratch_shapes=[pltpu.VMEM((ka, p.shape[1]), F32) for p in parts],
                   compiler_params=_cp(("arbitrary",), VMEM_BIG))(a, *parts)


def _dw_rows(name, rows_t, h):
    t_tok = h.shape[0]
    tt = 512

    def body(r_ref, h_ref, o_ref):
        @pl.when(pl.program_id(0) == 0)
        def _():
            o_ref[...] = jnp.zeros_like(o_ref)

        o_ref[...] += _nn(r_ref[...].astype(BF16), h_ref[...])

    return _pallas(body, name=name, grid=(t_tok // tt,),
                   in_specs=[pl.BlockSpec((8, tt), lambda t: (0, t)), pl.BlockSpec((tt, D_MODEL), lambda t: (t, 0))],
                   out_specs=pl.BlockSpec((8, D_MODEL), lambda t: (0, 0)),
                   out_shape=jax.ShapeDtypeStruct((8, D_MODEL), F32),
                   compiler_params=_cp(("arbitrary",), VMEM_MID))(rows_t, h)


def _lru_gates(xc, blk, wa_ref, wx_ref, ba_ref, bx_ref, sp):
    cols = slice(blk * LRU_BLOCK_W, (blk + 1) * LRU_BLOCK_W)
    xb = xc[:, cols].astype(BF16)
    r = _sigmoid(_nn(xb, wa_ref[blk].astype(BF16)) + ba_ref[:, cols])
    ig = _sigmoid(_nn(xb, wx_ref[blk].astype(BF16)) + bx_ref[:, cols])
    log_a = -LRU_C * r * sp[:, cols]
    a = jnp.exp(log_a)
    mult = jnp.sqrt(_neg_expm1(2.0 * log_a))
    return xb, r, ig, a, mult


def _softplus_neg(lam):
    return jnp.maximum(-lam, 0.0) + jnp.log(1.0 + jnp.exp(-jnp.abs(lam)))


def _conv_taps(xe_ref, cw_ref, cb_ref):
    xc = cb_ref[...] + xe_ref[8:8 + TC, :] * cw_ref[3:4, :]
    for k in range(1, 4):
        xc = xc + xe_ref[8 - k:8 - k + TC, :] * cw_ref[3 - k:4 - k, :]
    return xc


def _lru_fwd(proj, cw, cb, w_a, b_a, w_x, b_x, lam, seq):
    t_tok = proj.shape[0]
    nc = seq // TC

    def body(x_ref, cw_ref, cb_ref, wa_ref, ba_ref, wx_ref, bx_ref, lam_ref, hs_ref, xe_s, a_s, u_s, h_s):
        c = pl.program_id(1)

        @pl.when(c == 0)
        def _():
            xe_s[0:8, :] = jnp.zeros((8, D_MODEL), F32)
            h_s[...] = jnp.zeros_like(h_s)

        xe_s[8:8 + TC, :] = x_ref[...]
        xc = _conv_taps(xe_s, cw_ref, cb_ref)
        sp = _softplus_neg(lam_ref[...])
        for blk in range(LRU_BLOCKS):
            cols = slice(blk * LRU_BLOCK_W, (blk + 1) * LRU_BLOCK_W)
            _, _, ig, a, mult = _lru_gates(xc, blk, wa_ref, wx_ref, ba_ref, bx_ref, sp)
            a_s[:, cols] = a
            u_s[:, cols] = mult * ig * xc[:, cols]

        def step(t, h):
            h = a_s[pl.ds(t, 1), :] * h + u_s[pl.ds(t, 1), :]
            hs_ref[pl.ds(t, 1), :] = h
            return h

        h_s[0:1, :] = lax.fori_loop(0, TC, step, h_s[0:1, :], unroll=8)
        xe_s[0:8, :] = xe_s[TC:TC + 8, :]

    full = lambda shape: pl.BlockSpec(shape, lambda b, c: (0,) * len(shape))
    return _pallas(
        body, name="lru_fwd", grid=(t_tok // seq, nc),
        in_specs=[pl.BlockSpec((TC, D_MODEL), lambda b, c: (b * nc + c, 0)), full((4, D_MODEL)), full((1, D_MODEL)),
                  full((LRU_BLOCKS, LRU_BLOCK_W, LRU_BLOCK_W)), full((1, D_MODEL)),
                  full((LRU_BLOCKS, LRU_BLOCK_W, LRU_BLOCK_W)), full((1, D_MODEL)), full((1, D_MODEL))],
        out_specs=pl.BlockSpec((TC, D_MODEL), lambda b, c: (b * nc + c, 0)),
        out_shape=jax.ShapeDtypeStruct((t_tok, D_MODEL), F32),
        scratch_shapes=[pltpu.VMEM((TC + 8, D_MODEL), F32), pltpu.VMEM((TC, D_MODEL), F32),
                        pltpu.VMEM((TC, D_MODEL), F32), pltpu.VMEM((8, D_MODEL), F32)],
        compiler_params=_cp(("arbitrary", "arbitrary"), VMEM_MID))(proj, cw, cb, w_a, b_a, w_x, b_x, lam)


def _lru_bwd(proj, hs, dyh, cw, cb, w_a, b_a, w_x, b_x, lam, seq):
    t_tok = proj.shape[0]
    nc = seq // TC

    def body(x_ref, xh_ref, g_ref, hs_ref, hh_ref, dy_ref, cw_ref, cb_ref, wa_ref, ba_ref, wx_ref, bx_ref, lam_ref,
             dp_ref, dcw_ref, dvec_ref, dwa_ref, dwx_ref,
             xe_s, he_s, de_s, a_s, r_s, i_s, m_s, dh_s, carry_s):
        b, cr = pl.program_id(0), pl.program_id(1)
        c = nc - 1 - cr

        @pl.when(jnp.logical_and(b == 0, cr == 0))
        def _():
            dcw_ref[...] = jnp.zeros_like(dcw_ref)
            dvec_ref[...] = jnp.zeros_like(dvec_ref)
            dwa_ref[...] = jnp.zeros_like(dwa_ref)
            dwx_ref[...] = jnp.zeros_like(dwx_ref)

        @pl.when(cr == 0)
        def _():
            carry_s[...] = jnp.zeros_like(carry_s)
            de_s[TC:TC + 8, :] = jnp.zeros((8, D_MODEL), F32)

        first = c == 0
        xe_s[0:8, :] = jnp.where(first, 0.0, xh_ref[...])
        xe_s[8:8 + TC, :] = x_ref[...]
        he_s[0:8, :] = jnp.where(first, 0.0, hh_ref[...])
        he_s[8:8 + TC, :] = hs_ref[...]
        xc = _conv_taps(xe_s, cw_ref, cb_ref)
        lam_v = lam_ref[...]
        sp = _softplus_neg(lam_v)
        for blk in range(LRU_BLOCKS):
            cols = slice(blk * LRU_BLOCK_W, (blk + 1) * LRU_BLOCK_W)
            _, r, ig, a, mult = _lru_gates(xc, blk, wa_ref, wx_ref, ba_ref, bx_ref, sp)
            a_s[:, cols], r_s[:, cols], i_s[:, cols], m_s[:, cols] = a, r, ig, mult

        gt = g_ref[...]
        dyh = dy_ref[...]
        dh_s[...] = dyh * _silu(gt)
        dp_ref[:, D_MODEL:] = (dyh * hs_ref[...] * _dsilu(gt)).astype(BF16)

        def step(k, carry):
            t = TC - 1 - k
            dh = dh_s[pl.ds(t, 1), :] + carry
            dh_s[pl.ds(t, 1), :] = dh
            return a_s[pl.ds(t, 1), :] * dh

        carry_s[0:1, :] = lax.fori_loop(0, TC, step, carry_s[0:1, :], unroll=8)

        hprev = he_s[7:7 + TC, :]
        for blk in range(LRU_BLOCKS):
            cols = slice(blk * LRU_BLOCK_W, (blk + 1) * LRU_BLOCK_W)
            xcb = xc[:, cols]
            a, r, ig, mult, dh = a_s[:, cols], r_s[:, cols], i_s[:, cols], m_s[:, cols], dh_s[:, cols]
            spb = sp[:, cols]
            dmult = dh * ig * xcb
            di = dh * mult * xcb
            dxc = dh * mult * ig
            dla = dh * hprev[:, cols] * a - dmult * (a * a) / jnp.maximum(mult, 1e-20)
            dr = dla * (-LRU_C * spb)
            dsp = jnp.sum(dla * (-LRU_C * r), axis=0, keepdims=True)
            dga = dr * r * (1.0 - r)
            dgx = di * ig * (1.0 - ig)
            dga_b, dgx_b = dga.astype(BF16), dgx.astype(BF16)
            xb = xcb.astype(BF16)
            dxc = dxc + _nt(dga_b, wa_ref[blk].astype(BF16)) + _nt(dgx_b, wx_ref[blk].astype(BF16))
            dwa_ref[blk] += _tn(xb, dga_b)
            dwx_ref[blk] += _tn(xb, dgx_b)
            dvec_ref[1:2, cols] += jnp.sum(dga, axis=0, keepdims=True)
            dvec_ref[2:3, cols] += jnp.sum(dgx, axis=0, keepdims=True)
            dvec_ref[3:4, cols] += dsp * (-1.0 / (1.0 + jnp.exp(lam_v[:, cols])))
            de_s[0:TC, cols] = dxc

        dxc = de_s[0:TC, :]
        dvec_ref[0:1, :] += jnp.sum(dxc, axis=0, keepdims=True)
        dxr = dxc * cw_ref[3:4, :]
        dcw_ref[3:4, :] += jnp.sum(dxc * xe_s[8:8 + TC, :], axis=0, keepdims=True)
        for k in range(1, 4):
            dxr = dxr + de_s[k:k + TC, :] * cw_ref[3 - k:4 - k, :]
            dcw_ref[3 - k:4 - k, :] += jnp.sum(dxc * xe_s[8 - k:8 - k + TC, :], axis=0, keepdims=True)
        dp_ref[:, :D_MODEL] = dxr.astype(BF16)
        de_s[TC:TC + 8, :] = de_s[0:8, :]

    chunk = lambda col: pl.BlockSpec((TC, D_MODEL), lambda b, cr: (b * nc + nc - 1 - cr, col))
    halo = lambda col: pl.BlockSpec(
        (8, D_MODEL), lambda b, cr: (jnp.maximum((b * nc + nc - 1 - cr) * (TC // 8) - 1, 0), col))
    full = lambda shape: pl.BlockSpec(shape, lambda b, cr: (0,) * len(shape))
    wblk = (LRU_BLOCKS, LRU_BLOCK_W, LRU_BLOCK_W)
    return _pallas(
        body, name="lru_bwd", grid=(t_tok // seq, nc),
        in_specs=[chunk(0), halo(0), chunk(1), chunk(0), halo(0), chunk(0),
                  full((4, D_MODEL)), full((1, D_MODEL)), full(wblk), full((1, D_MODEL)), full(wblk),
                  full((1, D_MODEL)), full((1, D_MODEL))],
        out_specs=[pl.BlockSpec((TC, 2 * D_MODEL), lambda b, cr: (b * nc + nc - 1 - cr, 0)),
                   full((8, D_MODEL)), full((8, D_MODEL)), full(wblk), full(wblk)],
        out_shape=[jax.ShapeDtypeStruct((t_tok, 2 * D_MODEL), BF16), jax.ShapeDtypeStruct((8, D_MODEL), F32),
                   jax.ShapeDtypeStruct((8, D_MODEL), F32), jax.ShapeDtypeStruct(wblk, F32),
                   jax.ShapeDtypeStruct(wblk, F32)],
        scratch_shapes=[pltpu.VMEM((TC + 8, D_MODEL), F32), pltpu.VMEM((TC + 8, D_MODEL), F32),
                        pltpu.VMEM((TC + 8, D_MODEL), F32)]
        + [pltpu.VMEM((TC, D_MODEL), F32)] * 5 + [pltpu.VMEM((8, D_MODEL), F32)],
        compiler_params=_cp(("arbitrary", "arbitrary"), VMEM_BIG),
    )(proj, proj, proj, hs, hs, dyh, cw, cb, w_a, b_a, w_x, b_x, lam)


def _final_loss(x, g, target):
    t_tok = x.shape[0]

    def body(x_ref, g_ref, t_ref, dx_ref, loss_ref, dg_ref):
        @pl.when(pl.program_id(0) == 0)
        def _():
            loss_ref[...] = jnp.zeros_like(loss_ref)
            dg_ref[...] = jnp.zeros_like(dg_ref)

        xv = x_ref[...]
        gv = g_ref[...]
        rstd = lax.rsqrt(jnp.mean(xv * xv, axis=-1, keepdims=True) + EPS)
        xhat = xv * rstd
        err = xhat * gv - t_ref[...]
        loss_ref[0:1, :] += jnp.sum(err * err, axis=0, keepdims=True) * (0.5 / D_MODEL)
        dout = err * (1.0 / D_MODEL)
        dg_ref[0:1, :] += jnp.sum(dout * xhat, axis=0, keepdims=True)
        dxhat = dout * gv
        dx_ref[...] = rstd * (dxhat - xhat * jnp.mean(dxhat * xhat, axis=-1, keepdims=True))

    row = pl.BlockSpec((TM, D_MODEL), lambda i: (i, 0))
    acc = pl.BlockSpec((8, D_MODEL), lambda i: (0, 0))
    return _pallas(body, name="final_loss", grid=(t_tok // TM,),
                   in_specs=[row, pl.BlockSpec((1, D_MODEL), lambda i: (0, 0)), row],
                   out_specs=[row, acc, acc],
                   out_shape=[jax.ShapeDtypeStruct((t_tok, D_MODEL), F32)] + [jax.ShapeDtypeStruct((8, D_MODEL), F32)] * 2,
                   compiler_params=_cp(("arbitrary",), VMEM_MID))(x, g, target)


def _adam_math(w, g, m, v):
    m_new = ADAM_B1 * m + (1.0 - ADAM_B1) * g
    v_new = ADAM_B2 * v + (1.0 - ADAM_B2) * (g * g)
    m_hat = m_new / (1.0 - ADAM_B1 ** ADAM_STEP)
    v_hat = v_new / (1.0 - ADAM_B2 ** ADAM_STEP)
    delta = -ADAM_LR * (m_hat / (jnp.sqrt(v_hat) + ADAM_EPS) + ADAM_WD * w)
    return delta, m_new, v_new


def _sum_leading(name, x):
    n, rows, cols = x.shape
    tr = PACK_ROWS if rows % PACK_ROWS == 0 else rows

    def body(x_ref, o_ref):
        acc = x_ref[0]
        for d in range(1, n):
            acc = acc + x_ref[d]
        o_ref[...] = acc

    return _pallas(body, name=name, grid=(rows // tr,),
                   in_specs=[pl.BlockSpec((n, tr, cols), lambda i: (0, i, 0))],
                   out_specs=pl.BlockSpec((tr, cols), lambda i: (i, 0)),
                   out_shape=jax.ShapeDtypeStruct((rows, cols), F32),
                   compiler_params=_cp(("arbitrary",), VMEM_MID))(x)


def _adamw(name, w, m, v, g=None, parts=None):
    rows, cols = w.shape
    tr = rows if rows <= 256 else 256

    def body(*refs):
        w_ref, m_ref, v_ref, g_in, g_ref, d_ref, mo_ref, vo_ref = refs
        if parts is None:
            gv = g_in[...]
        else:
            acc = g_in[0].astype(F32)
            for d in range(1, N_DEV):
                acc = acc + g_in[d].astype(F32)
            gv = acc[:, :cols]
        delta, m_new, v_new = _adam_math(w_ref[...], gv, m_ref[...], v_ref[...])
        g_ref[...] = gv
        d_ref[...] = delta
        mo_ref[...] = m_new
        vo_ref[...] = v_new

    row = pl.BlockSpec((tr, cols), lambda i: (i, 0))
    if parts is None:
        g_spec, g_arg = row, g
    else:
        g_spec, g_arg = pl.BlockSpec((N_DEV, tr, parts.shape[2]), lambda i: (0, i, 0)), parts
    return _pallas(body, name=name, grid=(rows // tr,), in_specs=[row, row, row, g_spec], out_specs=[row] * 4,
                   out_shape=[jax.ShapeDtypeStruct((rows, cols), F32)] * 4,
                   compiler_params=_cp(("arbitrary",), VMEM_MID))(w, m, v, g_arg)


def _pack_rows(arrs):
    rows, meta, total = [], [], 0
    for a in arrs:
        flat = a.reshape(-1)
        nrow = -(-flat.shape[0] // 1024) * 8
        rows.append(jnp.pad(flat, (0, nrow * 128 - flat.shape[0])).reshape(nrow, 128))
        meta.append((a.shape, flat.shape[0], nrow))
        total += nrow
    tail = -total % PACK_ROWS
    if tail:
        rows.append(jnp.zeros((tail, 128), F32))
    return jnp.concatenate(rows, axis=0), meta


def _unpack_rows(packed, meta):
    out, r0 = [], 0
    for shape, size, nrow in meta:
        out.append(packed[r0:r0 + nrow].reshape(-1)[:size].reshape(shape))
        r0 += nrow
    return out


WEIGHTS = ["rel_bias", "norm_g", "ada_w", "ada_b", "attn_w_in", "attn_sinks", "attn_b_f", "attn_w_out", "lru_w_in",
           "lru_conv_w", "lru_conv_b", "lru_w_a", "lru_b_a", "lru_w_x", "lru_b_x", "lru_lambda", "lru_w_out", "final_g"]
BIG = ["ada_w", "attn_w_in", "attn_w_out", "lru_w_in", "lru_w_out"]
PACK_ROWS = 256


def kernel(x, c, rel_bias, norm_g, ada_w, ada_b, attn_w_in, attn_sinks, attn_b_f, attn_w_out, lru_w_in, lru_conv_w, lru_conv_b, lru_w_a, lru_b_a, lru_w_x, lru_b_x, lru_lambda, lru_w_out, final_g, loss_target, m_rel_bias, m_norm_g, m_ada_w, m_ada_b, m_attn_w_in, m_attn_sinks, m_attn_b_f, m_attn_w_out, m_lru_w_in, m_lru_conv_w, m_lru_conv_b, m_lru_w_a, m_lru_b_a, m_lru_w_x, m_lru_b_x, m_lru_lambda, m_lru_w_out, m_final_g, v_rel_bias, v_norm_g, v_ada_w, v_ada_b, v_attn_w_in, v_attn_sinks, v_attn_b_f, v_attn_w_out, v_lru_w_in, v_lru_conv_w, v_lru_conv_b, v_lru_w_a, v_lru_b_a, v_lru_w_x, v_lru_b_x, v_lru_lambda, v_lru_w_out, v_final_g):
    nseq, seq, _ = x.shape
    t_tok = nseq * seq
    me = 4 * lax.axis_index("x") + 2 * lax.axis_index("y") + lax.axis_index("c")
    x0 = x.reshape(t_tok, D_MODEL)
    target = loss_target.reshape(t_tok, D_MODEL)

    w_in_pad = jnp.pad(attn_w_in[0].astype(BF16), ((0, 0), (0, SHARD_W_PAD - SHARD_W_IN)))
    vec_shard = jnp.concatenate([lru_conv_w[0], lru_conv_b, lru_b_a, lru_b_x, lru_lambda], axis=0)
    g_w_in, g_w_out0, g_lru_in, g_w_out1, g_vec, g_c = _exchange(
        "gather_weights",
        [w_in_pad, attn_w_out[0].astype(BF16), lru_w_in[0].astype(BF16), lru_w_out[0].astype(BF16), vec_shard, c], [])
    w_full = jnp.transpose(g_w_in[:, :, :SHARD_W_IN], (1, 0, 2)).reshape(D_MODEL, N_DEV * SHARD_W_IN)
    w_aq, w_ak, w_av = w_full[:, 0:512], w_full[:, 512:640], w_full[:, 640:768]
    w_bq, w_bk, w_bv = w_full[:, 768:1280], w_full[:, 1280:1792], w_full[:, 1792:2304]
    w_f, w_gate = w_full[:, 2304:2312], w_full[:, 2312:3336]
    w_main = jnp.concatenate([w_bq, w_bk, w_bv, w_aq, w_gate, w_ak, w_av], axis=1)
    wf_t = jnp.transpose(w_f)
    w_out0 = g_w_out0.reshape(D_MODEL, D_MODEL)
    w_out1 = g_w_out1.reshape(D_MODEL, D_MODEL)
    vec_full = jnp.transpose(g_vec, (1, 0, 2)).reshape(8, D_MODEL)
    conv_w, conv_b, b_a, b_x, lam = vec_full[0:4], vec_full[4:5], vec_full[5:6], vec_full[6:7], vec_full[7:8]
    c_all = g_c.reshape(N_DEV * nseq, D_MODEL)

    ncol = ada_w.shape[2]
    ada_b_slice = lax.dynamic_slice(ada_b.reshape(2, N_DEV, ncol), (0, me, 0), (2, 1, ncol))
    mod_part = _ada_mod(c_all, ada_w, ada_b_slice)
    (g_mod,) = _exchange("gather_mod", [mod_part], [])
    mine = lax.dynamic_slice(g_mod, (0, 0, me * nseq, 0), (N_DEV, 2, nseq, ncol))
    mod = jnp.transpose(mine, (1, 2, 0, 3)).reshape(2, nseq, 3 * D_MODEL)
    shift = [mod[l, :, 0:D_MODEL].reshape(nseq, 1, D_MODEL) for l in range(2)]
    scale = [mod[l, :, D_MODEL:2 * D_MODEL].reshape(nseq, 1, D_MODEL) for l in range(2)]
    gmod = [mod[l, :, 2 * D_MODEL:].reshape(nseq, 1, D_MODEL) for l in range(2)]

    onehot = _bucket_onehot()
    bias = _bias_expand(jnp.transpose(rel_bias), onehot).reshape(N_HEADS, BLOCK, 2 * BLOCK)
    sinks = attn_sinks.reshape(N_HEADS)
    b_f = attn_b_f.reshape(N_HEADS, 1)
    h0, qkvg, fl_t = _norm_proj("norm_proj0", x0, norm_g[0:1], shift[0], scale[0], w_main, seq, BF16, wf_t=wf_t)
    f_row, f_col = _fox_prep(fl_t, b_f, seq)
    a_out, lse_a = _swa_fwd(qkvg, bias, sinks, seq)
    q_aug, k_aug, kt_aug, vt = _fox_aug(qkvg, f_col, seq)
    b_out, lse_b = _fox_fwd_t(q_aug, k_aug, vt, seq)
    yg0, y0, x1 = _out_proj("out_proj0", [a_out, b_out], qkvg, C_GATE // D_MODEL, w_out0, x0, gmod[0], seq)

    h1, proj1 = _norm_proj("norm_proj1", x1, norm_g[1:2], shift[1], scale[1], g_lru_in, seq, F32)
    hs = _lru_fwd(proj1, conv_w, conv_b, lru_w_a[0], b_a, lru_w_x[0], b_x, lam, seq)
    yg1, y1, x2 = _out_proj("out_proj1", [hs], proj1, 1, w_out1, x1, gmod[1], seq)

    dx2, loss_rows, dfinal_rows = _final_loss(x2, final_g.reshape(1, D_MODEL), target)
    loss = lax.psum(jnp.sum(loss_rows[0]), ("x", "y", "c"))

    dy1, dgm1, dyh = _out_proj_bwd("out_proj1_bwd", dx2, gmod[1], y1, w_out1, seq)
    dproj1, dcw, dvec, dw_a, dw_x = _lru_bwd(proj1, hs, dyh, conv_w, conv_b, lru_w_a[0], b_a, lru_w_x[0], b_x, lam, seq)
    dx1, dss1, dg1 = _norm_bwd("norm1_bwd", [(dproj1, 0)], g_lru_in, x1, norm_g[1:2], scale[1], dx2, seq)
    (p_w_out1,) = _dw("dw_out1", yg1, [dy1])
    (p_lru_in,) = _dw("dw_lru_in", h1, [dproj1], blocked=2 * D_MODEL // N_DEV)

    dy0, dgm0, du_a, du_b, dgate = _out_proj_bwd("out_proj0_bwd", dx1, gmod[0], y0, w_out0, seq,
                                                  attn=(a_out, b_out, qkvg))
    dq_a, dkv_a, dbias, dsink = _swa_bwd(qkvg, du_a, a_out, lse_a, bias, sinks, seq)
    dq_b, dk_b, dv_b, df4 = _fox_bwd_t(q_aug, k_aug, kt_aug, qkvg, du_b, b_out, lse_b, seq)
    dfl_t, db_f = _fox_post(df4.reshape(N_HEADS, t_tok), fl_t, b_f, seq)
    parts0 = [(dq_b, C_BQ), (dk_b, C_BK), (dv_b, C_BV), (dq_a, C_AQ), (dgate, C_GATE), (dkv_a, C_AK)]
    dx0, dss0, dg0 = _norm_bwd("norm0_bwd", parts0, w_main, x0, norm_g[0:1], scale[0], dx1, seq,
                               rows_part=(dfl_t, wf_t))
    (p_w_out0,) = _dw("dw_out0", yg0, [dy0])
    pw_bq, pw_bk, pw_bv, pw_aq, pw_gate, pw_akv = _dw("dw_attn_in", h0, [p for p, _ in parts0])
    pw_f = _dw_rows("dw_f", dfl_t, h0)
    dbias_t = _bias_reduce(dbias.reshape(N_HEADS, BLOCK * 2 * BLOCK), onehot)

    p_w_in = jnp.concatenate([pw_aq, pw_akv, pw_bq, pw_bk, pw_bv, jnp.transpose(pw_f).astype(BF16), pw_gate], axis=1)
    p_w_in = jnp.transpose(p_w_in.reshape(D_MODEL, N_DEV, SHARD_W_IN), (1, 0, 2))
    p_w_in = jnp.pad(p_w_in, ((0, 0), (0, 0), (0, SHARD_W_PAD - SHARD_W_IN)))
    rows_out = D_MODEL // N_DEV
    small_partials = [jnp.transpose(dbias_t), jnp.stack([dg0[0], dg1[0]]), dsink[:, 0], db_f[:, 0],
                      dcw[0:4], dvec[0:4], dw_a, dw_x, dfinal_rows[0]]
    gpack, gmeta = _pack_rows(small_partials)
    dmod = jnp.stack([jnp.concatenate([dss[:, 0], dss[:, 1], dgm[:, 0]], axis=1)
                      for dss, dgm in ((dss0, dgm0), (dss1, dgm1))], axis=1)
    g_small, g_dmod, r_w_in, r_w_out0, r_lru_in, r_w_out1 = _exchange(
        "exchange_grads", [gpack, dmod],
        [p_w_in, p_w_out0.reshape(N_DEV, rows_out, D_MODEL), p_lru_in, p_w_out1.reshape(N_DEV, rows_out, D_MODEL)])

    small_sum = _sum_leading("sum_small", g_small)
    d_rel, d_norm_g, d_sinks, d_b_f, d_cw, d_vec, d_wa, d_wx, d_final_g = _unpack_rows(small_sum, gmeta)
    cols = lambda a: lax.dynamic_slice(a, (0, me * LRU_BLOCK_W), (a.shape[0], LRU_BLOCK_W))
    dmod_all = g_dmod.reshape(N_DEV * nseq, 2 * 3 * D_MODEL)
    d_ada_b = _sum_leading("sum_ada_b", dmod_all.reshape(N_DEV * nseq, 2 * 3 * D_MODEL // 128, 128)).reshape(2, 3 * D_MODEL)
    dmod_slice = lax.dynamic_slice(dmod_all.reshape(N_DEV * nseq, 2, N_DEV, ncol), (0, 0, me, 0),
                                   (N_DEV * nseq, 2, 1, ncol)).reshape(N_DEV * nseq, 2, ncol)
    d_ada_w = _ada_w_grad(c_all, jnp.transpose(dmod_slice, (1, 0, 2)))

    given = dict(
        rel_bias=(rel_bias, m_rel_bias, v_rel_bias), norm_g=(norm_g, m_norm_g, v_norm_g),
        ada_w=(ada_w, m_ada_w, v_ada_w), ada_b=(ada_b, m_ada_b, v_ada_b),
        attn_w_in=(attn_w_in, m_attn_w_in, v_attn_w_in), attn_sinks=(attn_sinks, m_attn_sinks, v_attn_sinks),
        attn_b_f=(attn_b_f, m_attn_b_f, v_attn_b_f), attn_w_out=(attn_w_out, m_attn_w_out, v_attn_w_out),
        lru_w_in=(lru_w_in, m_lru_w_in, v_lru_w_in), lru_conv_w=(lru_conv_w, m_lru_conv_w, v_lru_conv_w),
        lru_conv_b=(lru_conv_b, m_lru_conv_b, v_lru_conv_b), lru_w_a=(lru_w_a, m_lru_w_a, v_lru_w_a),
        lru_b_a=(lru_b_a, m_lru_b_a, v_lru_b_a), lru_w_x=(lru_w_x, m_lru_w_x, v_lru_w_x),
        lru_b_x=(lru_b_x, m_lru_b_x, v_lru_b_x), lru_lambda=(lru_lambda, m_lru_lambda, v_lru_lambda),
        lru_w_out=(lru_w_out, m_lru_w_out, v_lru_w_out), final_g=(final_g, m_final_g, v_final_g))
    results = {}

    def big(name, shape2d, g=None, parts=None):
        w, m, v = (a.reshape(shape2d) for a in given[name])
        outs = _adamw("adamw_" + name, w, m, v, g=g, parts=parts)
        results[name] = tuple(o.reshape(given[name][0].shape) for o in outs)

    big("ada_w", (2 * D_MODEL, ncol), g=d_ada_w.reshape(2 * D_MODEL, ncol))
    big("attn_w_in", (D_MODEL, SHARD_W_IN), parts=r_w_in)
    big("attn_w_out", (rows_out, D_MODEL), parts=r_w_out0)
    big("lru_w_in", (D_MODEL, 2 * D_MODEL // N_DEV), parts=r_lru_in)
    big("lru_w_out", (rows_out, D_MODEL), parts=r_w_out1)

    small_grads = dict(
        rel_bias=d_rel, norm_g=d_norm_g, ada_b=d_ada_b, attn_sinks=d_sinks.reshape(1, N_HEADS),
        attn_b_f=d_b_f.reshape(1, N_HEADS), lru_conv_w=cols(d_cw).reshape(1, 4, LRU_BLOCK_W),
        lru_conv_b=cols(d_vec[0:1]), lru_w_a=d_wa.reshape(lru_w_a.shape), lru_b_a=cols(d_vec[1:2]),
        lru_w_x=d_wx.reshape(lru_w_x.shape), lru_b_x=cols(d_vec[2:3]), lru_lambda=cols(d_vec[3:4]),
        final_g=d_final_g)
    small = [n for n in WEIGHTS if n not in BIG]
    wpack, smeta = _pack_rows([given[n][0] for n in small])
    mpack, _ = _pack_rows([given[n][1] for n in small])
    vpack, _ = _pack_rows([given[n][2] for n in small])
    gpack2, _ = _pack_rows([small_grads[n] for n in small])
    packs = _adamw("adamw_small", wpack, mpack, vpack, g=gpack2)
    unpacked = [_unpack_rows(p, smeta) for p in packs]
    for k, n in enumerate(small):
        results[n] = tuple(unpacked[j][k] for j in range(4))

    grad_x = dx0.reshape(x.shape)
    out = [loss, grad_x]
    for j in range(4):
        out += [results[n][j] for n in WEIGHTS]
    return tuple(out)
```

```python
import functools
import math

import jax
import jax.numpy as jnp
from jax import lax
from jax.experimental import pallas as pl
from jax.experimental.pallas import tpu as pltpu

F32 = jnp.float32
BF16 = jnp.bfloat16
HI = lax.Precision.HIGHEST
MESH = pl.DeviceIdType.MESH

N_DEV = 8
D_MODEL = 1024
HEAD_DIM = 64
N_HEADS = 8
KV_GROUP = 4
BLOCK = 128
REL_BUCKETS = 32
REL_MAX_EXACT = 16
REL_MAX_DIST = 128
LRU_BLOCKS = 8
LRU_BLOCK_W = 128
LRU_C = 8.0
EPS = 1e-6
SCALE = HEAD_DIM ** -0.5
NEG = -1e30

ADAM_LR = 0.001
ADAM_B1 = 0.9
ADAM_B2 = 0.999
ADAM_EPS = 1e-08
ADAM_WD = 0.01
ADAM_STEP = 10

C_BQ, C_BK, C_BV, C_AQ, C_GATE, C_AK, C_AV = 0, 512, 1024, 1536, 2048, 3072, 3200
N_MAIN = 3328
SHARD_W_IN = 417
SHARD_W_PAD = 512

TM = 256
TQ = 256
TK = 128
TC = 256
VMEM_BIG = 56 * 1024 * 1024
VMEM_MID = 40 * 1024 * 1024


def _pallas(body, **kw):
    return pl.pallas_call(body, **kw)


def _cp(sem=None, vmem=None):
    kw = {}
    if sem is not None:
        kw["dimension_semantics"] = sem
    if vmem is not None:
        kw["vmem_limit_bytes"] = vmem
    return pltpu.CompilerParams(**kw)


def _nn(a, b, precision=None):
    return jnp.dot(a, b, preferred_element_type=F32, precision=precision)


def _nt(a, b, precision=None):
    return lax.dot_general(a, b, (((1,), (1,)), ((), ())), preferred_element_type=F32, precision=precision)


def _tn(a, b, precision=None):
    return lax.dot_general(a, b, (((0,), (0,)), ((), ())), preferred_element_type=F32, precision=precision)


def _sigmoid(x):
    return 1.0 / (1.0 + jnp.exp(-x))


def _silu(x):
    return x * _sigmoid(x)


def _dsilu(x):
    s = _sigmoid(x)
    return s * (1.0 + x * (1.0 - s))


def _neg_expm1(x):
    poly = x * (1.0 + x * (0.5 + x * (1.0 / 6.0 + x * (1.0 / 24.0))))
    return -jnp.where(jnp.abs(x) < 0.05, poly, jnp.exp(x) - 1.0)


def _col(tile, idx):
    lane = lax.broadcasted_iota(jnp.int32, tile.shape, 1)
    return jnp.sum(jnp.where(lane == idx, tile, 0.0), axis=1, keepdims=True)


def _row(tile, idx):
    sub = lax.broadcasted_iota(jnp.int32, tile.shape, 0)
    return jnp.sum(jnp.where(sub == idx, tile, 0.0), axis=0, keepdims=True)


def _exchange(name, gathers, scatters, axes=("x", "y", "c")):
    ng, n = len(gathers), len(gathers) + len(scatters)
    ins = list(gathers) + list(scatters)
    group = 2 ** len(axes)

    def body(*refs):
        in_refs, out_refs = refs[:n], refs[n:2 * n]
        send_sems, recv_sems, loc_sems = refs[2 * n:]
        coord = {a: lax.axis_index(a) for a in ("x", "y", "c")}

        def member(r):
            pc = dict(coord)
            idx = 0
            for k, a in enumerate(axes):
                if r & (1 << (len(axes) - 1 - k)):
                    pc[a] = 1 - coord[a]
                idx = 2 * idx + pc[a]
            return (pc["x"], pc["y"], pc["c"]), idx

        _, me = member(0)

        def peer(r):
            return member(r)

        local, sends, recvs = [], [], []
        for k in range(n):
            mine = in_refs[k] if k < ng else in_refs[k].at[me]
            cp = pltpu.make_async_copy(mine, out_refs[k].at[me], loc_sems.at[k])
            cp.start()
            local.append(cp)
            for r in range(1, group):
                pid, pidx = peer(r)
                src = in_refs[k] if k < ng else in_refs[k].at[pidx]
                snd = pltpu.make_async_remote_copy(
                    src_ref=src, dst_ref=out_refs[k].at[me], send_sem=send_sems.at[r - 1, k],
                    recv_sem=recv_sems.at[r - 1, k], device_id=pid, device_id_type=MESH)
                snd.start()
                sends.append(snd)
                recvs.append(pltpu.make_async_remote_copy(
                    src_ref=src, dst_ref=out_refs[k].at[pidx], send_sem=send_sems.at[r - 1, k],
                    recv_sem=recv_sems.at[r - 1, k], device_id=pid, device_id_type=MESH))
        for rc in recvs:
            rc.wait_recv()
        for snd in sends:
            snd.wait_send()
        for cp in local:
            cp.wait()

    out_shape = [jax.ShapeDtypeStruct((group,) + a.shape, a.dtype) for a in gathers]
    out_shape += [jax.ShapeDtypeStruct(a.shape, a.dtype) for a in scatters]
    any_spec = pl.BlockSpec(memory_space=pl.ANY)
    return _pallas(
        body, name=name, out_shape=out_shape,
        in_specs=[any_spec] * n, out_specs=[any_spec] * n,
        scratch_shapes=[pltpu.SemaphoreType.DMA((group - 1, n)), pltpu.SemaphoreType.DMA((group - 1, n)),
                        pltpu.SemaphoreType.DMA((n,))],
    )(*ins)


def _ada_mod(c_all, ada_w, ada_b_slice):
    def body(c_ref, w_ref, b_ref, o_ref):
        ca = _silu(c_ref[...])
        for l in range(2):
            o_ref[l] = _nn(ca, w_ref[l], HI) + b_ref[l]

    return _pallas(body, name="ada_mod",
                   out_shape=jax.ShapeDtypeStruct((2, c_all.shape[0], ada_w.shape[2]), F32),
                   compiler_params=_cp(vmem=VMEM_MID))(c_all, ada_w, ada_b_slice)


def _ada_w_grad(c_all, dmod_slice):
    def body(c_ref, d_ref, o_ref):
        ca = _silu(c_ref[...])
        for l in range(2):
            o_ref[l] = _tn(ca, d_ref[l], HI)

    return _pallas(body, name="ada_w_grad",
                   out_shape=jax.ShapeDtypeStruct((2, D_MODEL, dmod_slice.shape[2]), F32),
                   compiler_params=_cp(vmem=VMEM_MID))(c_all, dmod_slice)


def _bucket_onehot():
    qi = jnp.arange(BLOCK)[:, None]
    kj = jnp.arange(2 * BLOCK)[None, :]
    rel = qi - kj + BLOCK
    n = jnp.maximum(rel, 0)
    nf = jnp.maximum(n, 1).astype(F32)
    large = REL_MAX_EXACT + (jnp.log(nf / REL_MAX_EXACT) / math.log(REL_MAX_DIST / REL_MAX_EXACT)
                             * (REL_BUCKETS - REL_MAX_EXACT)).astype(jnp.int32)
    large = jnp.minimum(large, REL_BUCKETS - 1)
    bucket = jnp.where(n < REL_MAX_EXACT, n, large).reshape(1, BLOCK * 2 * BLOCK)
    return (jnp.arange(REL_BUCKETS)[:, None] == bucket).astype(F32)


def _bias_expand(rel_bias_t, onehot):
    def body(r_ref, e_ref, o_ref):
        o_ref[...] = _nn(r_ref[...], e_ref[...], HI)

    return _pallas(body, name="bias_expand",
                   out_shape=jax.ShapeDtypeStruct((N_HEADS, onehot.shape[1]), F32),
                   compiler_params=_cp(vmem=VMEM_MID))(rel_bias_t, onehot)


def _bias_reduce(dbias, onehot):
    def body(d_ref, e_ref, o_ref):
        o_ref[...] = _nt(d_ref[...], e_ref[...], HI)

    return _pallas(body, name="bias_reduce",
                   out_shape=jax.ShapeDtypeStruct((N_HEADS, REL_BUCKETS), F32),
                   compiler_params=_cp(vmem=VMEM_MID))(dbias, onehot)


def _norm_proj(name, x, g, shift, scale, w, seq, out_dtype, wf_t=None):
    t_tok = x.shape[0]
    w3d = w.ndim == 3
    n_out = w.shape[0] * w.shape[2] if w3d else w.shape[1]
    cn = w.shape[2] if w3d else 256

    def body(x_ref, g_ref, sh_ref, sc_ref, w_ref, *rest):
        if wf_t is not None:
            wf_ref, h_ref, o_ref, fl_ref = rest
        else:
            h_ref, o_ref = rest
        xv = x_ref[...]
        rstd = lax.rsqrt(jnp.mean(xv * xv, axis=-1, keepdims=True) + EPS)
        h = (xv * rstd) * g_ref[...] * (1.0 + sc_ref[...]) + sh_ref[...]
        hb = h.astype(BF16)
        h_ref[...] = hb
        for j in range(n_out // cn):
            wj = w_ref[j] if w3d else w_ref[:, j * cn:(j + 1) * cn]
            o_ref[:, j * cn:(j + 1) * cn] = _nn(hb, wj).astype(out_dtype)
        if wf_t is not None:
            fl_ref[...] = _nt(wf_ref[...], hb)

    mod_spec = pl.BlockSpec((None, 1, D_MODEL), lambda i: (i * TM // seq, 0, 0))
    w_spec = (pl.BlockSpec(w.shape, lambda i: (0, 0, 0)) if w3d else pl.BlockSpec(w.shape, lambda i: (0, 0)))
    in_specs = [pl.BlockSpec((TM, D_MODEL), lambda i: (i, 0)), pl.BlockSpec((1, D_MODEL), lambda i: (0, 0)),
                mod_spec, mod_spec, w_spec]
    out_shape = [jax.ShapeDtypeStruct((t_tok, D_MODEL), BF16), jax.ShapeDtypeStruct((t_tok, n_out), out_dtype)]
    out_specs = [pl.BlockSpec((TM, D_MODEL), lambda i: (i, 0)), pl.BlockSpec((TM, n_out), lambda i: (i, 0))]
    args = [x, g, shift, scale, w]
    if wf_t is not None:
        in_specs.append(pl.BlockSpec(wf_t.shape, lambda i: (0, 0)))
        out_shape.append(jax.ShapeDtypeStruct((wf_t.shape[0], t_tok), F32))
        out_specs.append(pl.BlockSpec((wf_t.shape[0], TM), lambda i: (0, i)))
        args.append(wf_t)
    return _pallas(body, name=name, grid=(t_tok // TM,), in_specs=in_specs, out_specs=out_specs,
                   out_shape=out_shape, compiler_params=_cp(("arbitrary",), VMEM_BIG))(*args)


def _fox_prep(fl_t, b_f, seq):
    t_tok = fl_t.shape[1]
    ch = 256

    def body(fl_ref, bf_ref, fr_ref, fc_ref):
        z = fl_ref[...] + bf_ref[...]
        logf = jnp.minimum(z, 0.0) - jnp.log(1.0 + jnp.exp(-jnp.abs(z)))
        ri = lax.broadcasted_iota(jnp.int32, (ch, ch), 0)
        ci = lax.broadcasted_iota(jnp.int32, (ch, ch), 1)
        upper = (ri <= ci).astype(F32)
        eye = (ri == ci).astype(F32)
        carry = jnp.zeros((N_HEADS, 1), F32)
        for k in range(seq // ch):
            fk = _nn(logf[:, k * ch:(k + 1) * ch], upper, HI) + carry
            carry = fk[:, ch - 1:ch]
            fr_ref[:, k * ch:(k + 1) * ch] = fk
            padded = jnp.concatenate([fk, jnp.zeros((128 - N_HEADS, ch), F32)], axis=0)
            fc_ref[k * ch:(k + 1) * ch, :] = _nt(eye, padded, HI)

    return _pallas(
        body, name="fox_prep", grid=(t_tok // seq,),
        in_specs=[pl.BlockSpec((N_HEADS, seq), lambda b: (0, b)), pl.BlockSpec((N_HEADS, 1), lambda b: (0, 0))],
        out_specs=[pl.BlockSpec((N_HEADS, seq), lambda b: (0, b)), pl.BlockSpec((seq, 128), lambda b: (b, 0))],
        out_shape=[jax.ShapeDtypeStruct((N_HEADS, t_tok), F32), jax.ShapeDtypeStruct((t_tok, 128), F32)],
        compiler_params=_cp(("arbitrary",), VMEM_MID))(fl_t, b_f)


def _fox_post(df_row, fl_t, b_f, seq):
    t_tok = fl_t.shape[1]
    ch = 256

    def body(d_ref, fl_ref, bf_ref, o_ref, db_ref):
        @pl.when(pl.program_id(0) == 0)
        def _():
            db_ref[...] = jnp.zeros_like(db_ref)

        z = fl_ref[...] + bf_ref[...]
        sig_neg = 1.0 / (1.0 + jnp.exp(z))
        ri = lax.broadcasted_iota(jnp.int32, (ch, ch), 0)
        ci = lax.broadcasted_iota(jnp.int32, (ch, ch), 1)
        lower = (ri >= ci).astype(F32)
        carry = jnp.zeros((N_HEADS, 1), F32)
        tot = jnp.zeros((N_HEADS, 1), F32)
        for k in reversed(range(seq // ch)):
            dk = _nn(d_ref[:, k * ch:(k + 1) * ch], lower, HI) + carry
            carry = dk[:, 0:1]
            dfl = dk * sig_neg[:, k * ch:(k + 1) * ch]
            o_ref[:, k * ch:(k + 1) * ch] = dfl
            tot = tot + jnp.sum(dfl, axis=1, keepdims=True)
        db_ref[...] += jnp.broadcast_to(tot, db_ref.shape)

    return _pallas(
        body, name="fox_post", grid=(t_tok // seq,),
        in_specs=[pl.BlockSpec((N_HEADS, seq), lambda b: (0, b)), pl.BlockSpec((N_HEADS, seq), lambda b: (0, b)),
                  pl.BlockSpec((N_HEADS, 1), lambda b: (0, 0))],
        out_specs=[pl.BlockSpec((N_HEADS, seq), lambda b: (0, b)), pl.BlockSpec((N_HEADS, 128), lambda b: (0, 0))],
        out_shape=[jax.ShapeDtypeStruct((N_HEADS, t_tok), F32), jax.ShapeDtypeStruct((N_HEADS, 128), F32)],
        compiler_params=_cp(("arbitrary",), VMEM_MID))(df_row, fl_t, b_f)


def _eye(n, dtype):
    return (lax.broadcasted_iota(jnp.int32, (n, n), 0) == lax.broadcasted_iota(jnp.int32, (n, n), 1)).astype(dtype)


def _fox_aug(qkvg, f_col, seq):
    t_tok = qkvg.shape[0]
    ta = 256
    nkb = ta // TK

    def body(q_ref, k_ref, v_ref, fc_ref, qa_ref, ka_ref, kt_ref, vt_ref):
        ri = lax.broadcasted_iota(jnp.int32, (128, 128), 0)
        ci = lax.broadcasted_iota(jnp.int32, (128, 128), 1)
        eye = (ri == ci).astype(BF16)
        lane = lax.broadcasted_iota(jnp.int32, (ta, 128), 1)
        ones_q = jnp.where(jnp.logical_and(lane >= 64, lane < 67), 1.0, 0.0)
        ones_k = jnp.where(jnp.logical_and(lane >= 67, lane < 70), 1.0, 0.0)
        fc_tile = fc_ref[...]
        for p in range(N_HEADS // 2):
            q2 = q_ref[:, 128 * p:128 * (p + 1)]
            k2 = k_ref[:, 128 * p:128 * (p + 1)]
            vt = _nt(eye, v_ref[:, 128 * p:128 * (p + 1)]).astype(BF16)
            for kk in range(nkb):
                vt_ref[p, kk] = vt[:, kk * TK:(kk + 1) * TK]
            for e in range(2):
                h = 2 * p + e
                sel = jnp.logical_and(ri == ci + HEAD_DIM * e, ci < HEAD_DIM)
                f = _col(fc_tile, h)
                fh = f.astype(BF16).astype(F32)
                fm = (f - fh).astype(BF16).astype(F32)
                fl = (f - fh - fm).astype(BF16).astype(F32)
                qa = (_nn(q2, jnp.where(sel, SCALE, 0.0).astype(BF16)) + ones_q + jnp.where(lane == 67, fh, 0.0)
                      + jnp.where(lane == 68, fm, 0.0) + jnp.where(lane == 69, fl, 0.0))
                ka = (_nn(k2, jnp.where(sel, 1.0, 0.0).astype(BF16)) + ones_k - jnp.where(lane == 64, fh, 0.0)
                      - jnp.where(lane == 65, fm, 0.0) - jnp.where(lane == 66, fl, 0.0))
                qa_ref[h] = qa.astype(BF16)
                kab = ka.astype(BF16)
                ka_ref[h] = kab
                kt = _nt(eye, kab).astype(BF16)
                for kk in range(nkb):
                    kt_ref[h, kk] = kt[:, kk * TK:(kk + 1) * TK]

    aug = jax.ShapeDtypeStruct((N_HEADS, t_tok, 128), BF16)
    return _pallas(
        body, name="fox_aug", grid=(t_tok // ta,),
        in_specs=[pl.BlockSpec((ta, 512), lambda i: (i, C_BQ // 512)), pl.BlockSpec((ta, 512), lambda i: (i, C_BK // 512)),
                  pl.BlockSpec((ta, 512), lambda i: (i, C_BV // 512)), pl.BlockSpec((ta, 128), lambda i: (i, 0))],
        out_specs=[pl.BlockSpec((N_HEADS, ta, 128), lambda i: (0, i, 0)), pl.BlockSpec((N_HEADS, ta, 128), lambda i: (0, i, 0)),
                   pl.BlockSpec((N_HEADS, nkb, 128, TK), lambda i: (0, i, 0, 0)),
                   pl.BlockSpec((N_HEADS // 2, nkb, 128, TK), lambda i: (0, i, 0, 0))],
        out_shape=[aug, aug, jax.ShapeDtypeStruct((N_HEADS, t_tok // TK, 128, TK), BF16),
                   jax.ShapeDtypeStruct((N_HEADS // 2, t_tok // TK, 128, TK), BF16)],
        compiler_params=_cp(("arbitrary",), VMEM_MID))(qkvg, qkvg, qkvg, f_col)


def _fox_fwd_t(q_aug, k_aug, vt, seq):
    t_tok = q_aug.shape[1]
    nq = seq // TQ
    ratio = TQ // TK

    def body(qa_ref, ka_ref, vt_ref, o_ref, lse_ref, ml_s, acc_s, st_s, p_s, al_s):
        i = pl.program_id(1)
        tpos = i * TQ + lax.broadcasted_iota(jnp.int32, (1, TQ), 1)
        eye = _eye(HEAD_DIM, BF16)
        for h in range(N_HEADS):
            ml_s[0, h] = jnp.full((1, TQ), NEG, F32)
            ml_s[1, h] = jnp.zeros((1, TQ), F32)
            acc_s[h] = jnp.zeros((HEAD_DIM, TQ), F32)
            p_s[1, h] = jnp.zeros((TK, TQ), BF16)
            al_s[1, h] = jnp.ones((1, TQ), F32)

        def scores(j):
            row0 = pl.multiple_of(j * TK, TK)
            for h in range(N_HEADS):
                st_s[j & 1, h] = _nt(ka_ref[h, pl.ds(row0, TK), :], qa_ref[h])

        def softmax(j, masked):
            slot = j & 1
            if masked:
                keep = (j * TK + lax.broadcasted_iota(jnp.int32, (TK, 1), 0)) <= tpos
            for h in range(N_HEADS):
                st = st_s[slot, h]
                if masked:
                    st = jnp.where(keep, st, NEG)
                m = ml_s[0, h]
                m_new = jnp.maximum(m, jnp.max(st, axis=0, keepdims=True))
                alpha = jnp.exp(m - m_new)
                pe = jnp.exp(st - m_new)
                ml_s[0, h] = m_new
                ml_s[1, h] = alpha * ml_s[1, h] + jnp.sum(pe, axis=0, keepdims=True)
                al_s[slot, h] = alpha
                p_s[slot, h] = pe.astype(BF16)

        def values(j):
            slot = j & 1
            jv = jnp.maximum(j, 0)
            for h in range(N_HEADS):
                p, e = divmod(h, 2)
                acc_s[h] = al_s[slot, h] * acc_s[h] + _nn(vt_ref[p, jv, e * HEAD_DIM:(e + 1) * HEAD_DIM, :], p_s[slot, h])

        def step(j, carry):
            values(j - 1)
            softmax(j, False)
            scores(j + 1)
            return carry

        last = ratio * i + ratio - 1
        scores(0)
        lax.fori_loop(0, ratio * i, step, 0)
        for kk in range(ratio):
            j = ratio * i + kk
            values(j - 1)
            softmax(j, True)
            if kk < ratio - 1:
                scores(j + 1)
        values(last)
        for p in range(N_HEADS // 2):
            outs = []
            for e in range(2):
                h = 2 * p + e
                l = ml_s[1, h]
                outs.append(_tn((acc_s[h] / l).astype(BF16), eye))
                lse_ref[p, e:e + 1, :] = ml_s[0, h] + jnp.log(l)
            o_ref[:, 128 * p:128 * (p + 1)] = jnp.concatenate(outs, axis=1).astype(BF16)

    return _pallas(
        body, name="fox_fwd", grid=(t_tok // seq, nq),
        in_specs=[pl.BlockSpec((N_HEADS, TQ, 128), lambda b, i: (0, b * nq + i, 0)),
                  pl.BlockSpec((N_HEADS, seq, 128), lambda b, i: (0, b, 0)),
                  pl.BlockSpec((N_HEADS // 2, seq // TK, 128, TK), lambda b, i: (0, b, 0, 0))],
        out_specs=[pl.BlockSpec((TQ, 512), lambda b, i: (b * nq + i, 0)),
                   pl.BlockSpec((N_HEADS // 2, 2, TQ), lambda b, i: (0, 0, b * nq + i))],
        out_shape=[jax.ShapeDtypeStruct((t_tok, 512), BF16), jax.ShapeDtypeStruct((N_HEADS // 2, 2, t_tok), F32)],
        scratch_shapes=[pltpu.VMEM((2, N_HEADS, 1, TQ), F32), pltpu.VMEM((N_HEADS, HEAD_DIM, TQ), F32),
                        pltpu.VMEM((2, N_HEADS, TK, TQ), F32), pltpu.VMEM((2, N_HEADS, TK, TQ), BF16),
                        pltpu.VMEM((2, N_HEADS, 1, TQ), F32)],
        compiler_params=_cp(("arbitrary", "arbitrary"), VMEM_MID))(q_aug, k_aug, vt)


def _fox_bwd_t(q_aug, k_aug, kt, qkvg, du_b, b_out, lse, seq):
    t_tok = qkvg.shape[0]
    nq = seq // TQ
    nkb = seq // TK
    ratio = TQ // TK
    hg = 4

    def body(qa_ref, ka_ref, kt_ref, v_ref, do_ref, o_ref, lse_ref, dq_ref, dk_ref, dv_ref, df_ref,
             dqt_s, row_s, dfk_s, dk_s, dv_s, dfa_s, st_s, dp_s, pb_s, db_s):
        ones_b = jnp.ones((8, TQ), BF16)
        eye = _eye(HEAD_DIM, BF16)
        lane8 = lax.broadcasted_iota(jnp.int32, (8, 128), 1)
        lane_k = lax.broadcasted_iota(jnp.int32, (TK, 128), 1)
        first = [lane8 < HEAD_DIM, lane8 >= HEAD_DIM]
        for hh in range(hg):
            pp, e = divmod(hh, 2)
            head_lanes = jnp.where(first[e], 1.0, 0.0)
            for ii in range(nq):
                rows = slice(ii * TQ, (ii + 1) * TQ)
                prod = do_ref[rows, 128 * pp:128 * (pp + 1)].astype(F32) * o_ref[rows, 128 * pp:128 * (pp + 1)].astype(F32)
                row_s[hh, ii, 0] = _nt(head_lanes, prod, HI)
                row_s[hh, ii, 1] = jnp.broadcast_to(lse_ref[pp, e:e + 1, ii * TQ:(ii + 1) * TQ], (8, TQ))
                dqt_s[hh, ii] = jnp.zeros((128, TQ), F32)

        def kblock(j, _):
            krow = pl.multiple_of(j * TK, TK)
            spos = j * TK + lax.broadcasted_iota(jnp.int32, (TK, 1), 0)
            for hh in range(hg):
                dk_s[hh] = jnp.zeros((TK, 128), F32)
                dv_s[hh] = jnp.zeros((TK, 128), F32)
                dfa_s[hh] = jnp.zeros((8, TK), F32)

            def scores(i):
                qrow = pl.multiple_of(i * TQ, TQ)
                for hh in range(hg):
                    pp, e = divmod(hh, 2)
                    own = (lane_k < HEAD_DIM) if e == 0 else (lane_k >= HEAD_DIM)
                    v2 = v_ref[pl.ds(krow, TK), 128 * pp:128 * (pp + 1)]
                    vj = jnp.where(own, v2, jnp.zeros_like(v2))
                    st_s[i & 1, hh] = _nt(ka_ref[hh, pl.ds(krow, TK), :], qa_ref[hh, pl.ds(qrow, TQ), :])
                    dp_s[i & 1, hh] = _nt(vj, do_ref[pl.ds(qrow, TQ), 128 * pp:128 * (pp + 1)])

            def elementwise(i, masked):
                slot = i & 1
                if masked:
                    keep = spos <= (i * TQ + lax.broadcasted_iota(jnp.int32, (1, TQ), 1))
                for hh in range(hg):
                    pt = jnp.exp(st_s[slot, hh] - row_s[hh, i, 1][0:1, :])
                    if masked:
                        pt = jnp.where(keep, pt, 0.0)
                    dst = pt * (dp_s[slot, hh] - row_s[hh, i, 0][0:1, :])
                    pb_s[slot, hh] = pt.astype(BF16)
                    db_s[slot, hh] = dst.astype(BF16)

            def grads(i):
                slot = i & 1
                qrow = pl.multiple_of(i * TQ, TQ)
                for hh in range(hg):
                    pp = hh // 2
                    dst_b = db_s[slot, hh]
                    dv_s[hh] += _nn(pb_s[slot, hh], do_ref[pl.ds(qrow, TQ), 128 * pp:128 * (pp + 1)])
                    dk_s[hh] += _nn(dst_b, qa_ref[hh, pl.ds(qrow, TQ), :])
                    dqt_s[hh, i] += _nn(kt_ref[hh, j], dst_b)
                    dfa_s[hh] += _nt(ones_b, dst_b)

            def step(i, carry):
                grads(i - 1)
                elementwise(i, False)
                scores(jnp.minimum(i + 1, nq - 1))
                return carry

            i0 = j // ratio
            scores(i0)
            elementwise(i0, True)
            scores(jnp.minimum(i0 + 1, nq - 1))
            lax.fori_loop(i0 + 1, nq, step, 0)
            grads(nq - 1)
            for pp in range(hg // 2):
                cols = slice(128 * pp, 128 * (pp + 1))
                dk_ref[pl.ds(krow, TK), cols] = jnp.concatenate(
                    [dk_s[2 * pp][:, :HEAD_DIM], dk_s[2 * pp + 1][:, :HEAD_DIM]], axis=1).astype(BF16)
                dv_ref[pl.ds(krow, TK), cols] = jnp.where(lane_k < HEAD_DIM, dv_s[2 * pp], dv_s[2 * pp + 1]).astype(BF16)
            for hh in range(hg):
                dfk_s[hh, j] = dfa_s[hh]
            return 0

        lax.fori_loop(0, nkb, kblock, 0)
        for pp in range(hg // 2):
            for ii in range(nq):
                parts = []
                for e in range(2):
                    dqt = dqt_s[2 * pp + e, ii]
                    parts.append(_tn(dqt[0:HEAD_DIM, :].astype(BF16), eye) * SCALE)
                    for kk in range(ratio):
                        jj = ii * ratio + kk
                        df_ref[pp, e:e + 1, jj * TK:(jj + 1) * TK] = (dqt[67:68, kk * TK:(kk + 1) * TK]
                                                                     - dfk_s[2 * pp + e, jj][0:1, :])
                dq_ref[ii * TQ:(ii + 1) * TQ, 128 * pp:128 * (pp + 1)] = jnp.concatenate(parts, axis=1).astype(BF16)

    aug_blk = pl.BlockSpec((hg, seq, 128), lambda b, g: (g, b, 0))
    pair_blk = pl.BlockSpec((seq, 64 * hg), lambda b, g: (b, g))
    row_blk = pl.BlockSpec((hg // 2, 2, seq), lambda b, g: (g, 0, b))
    return _pallas(
        body, name="fox_bwd", grid=(t_tok // seq, N_HEADS // hg),
        in_specs=[aug_blk, aug_blk, pl.BlockSpec((hg, nkb, 128, TK), lambda b, g: (g, b, 0, 0)),
                  pl.BlockSpec((seq, 64 * hg), lambda b, g: (b, C_BV // (64 * hg) + g)), pair_blk, pair_blk, row_blk],
        out_specs=[pair_blk, pair_blk, pair_blk, row_blk],
        out_shape=[jax.ShapeDtypeStruct((t_tok, 512), BF16)] * 3
        + [jax.ShapeDtypeStruct((N_HEADS // 2, 2, t_tok), F32)],
        scratch_shapes=[pltpu.VMEM((hg, nq, 128, TQ), F32), pltpu.VMEM((hg, nq, 2, 8, TQ), F32),
                        pltpu.VMEM((hg, nkb, 8, TK), F32), pltpu.VMEM((hg, TK, 128), F32),
                        pltpu.VMEM((hg, TK, 128), F32), pltpu.VMEM((hg, 8, TK), F32),
                        pltpu.VMEM((2, hg, TK, TQ), F32), pltpu.VMEM((2, hg, TK, TQ), F32),
                        pltpu.VMEM((2, hg, TK, TQ), BF16), pltpu.VMEM((2, hg, TK, TQ), BF16)],
        compiler_params=_cp(("arbitrary", "arbitrary"), VMEM_BIG))(q_aug, k_aug, kt, qkvg, du_b, b_out, lse)


def _fox_bwd_t_old(q_aug, k_aug, kt, qkvg, du_b, b_out, lse, seq):
    t_tok = qkvg.shape[0]
    nq = seq // TQ
    nkb = seq // TK
    ratio = TQ // TK

    def body(qa_ref, ka_ref, kt_ref, v_ref, do_ref, o_ref, lse_ref, dq_ref, dk_ref, dv_ref, df_ref,
             dqt_s, out_s, row_s, dfk_s):
        ones_b = jnp.ones((8, TQ), BF16)
        ones_f = jnp.ones((8, HEAD_DIM), F32)
        eye = _eye(HEAD_DIM, BF16)
        for e in range(2):
            lo, hi = e * HEAD_DIM, (e + 1) * HEAD_DIM
            for ii in range(nq):
                rows = slice(ii * TQ, (ii + 1) * TQ)
                do = do_ref[rows, :][:, lo:hi].astype(F32)
                ov = o_ref[rows, :][:, lo:hi].astype(F32)
                row_s[ii, 0] = _nt(ones_f, do * ov, HI)
                row_s[ii, 1] = jnp.broadcast_to(lse_ref[e:e + 1, ii * TQ:(ii + 1) * TQ], (8, TQ))
                dqt_s[ii] = jnp.zeros((128, TQ), F32)

            def kblock(j, _):
                krow = pl.multiple_of(j * TK, TK)
                kj = ka_ref[e, pl.ds(krow, TK), :]
                ktj = kt_ref[e, j]
                vj = v_ref[pl.ds(krow, TK), :][:, lo:hi]
                spos = j * TK + lax.broadcasted_iota(jnp.int32, (TK, 1), 0)

                def qblock(i, carry, masked):
                    dk_acc, dv_acc, dfk = carry
                    qrow = pl.multiple_of(i * TQ, TQ)
                    qa = qa_ref[e, pl.ds(qrow, TQ), :]
                    doh = do_ref[pl.ds(qrow, TQ), :][:, lo:hi]
                    pt = jnp.exp(_nt(kj, qa) - row_s[i, 1][0:1, :])
                    if masked:
                        tpos = i * TQ + lax.broadcasted_iota(jnp.int32, (1, TQ), 1)
                        pt = jnp.where(spos <= tpos, pt, 0.0)
                    dst = pt * (_nt(vj, doh) - row_s[i, 0][0:1, :])
                    dst_b = dst.astype(BF16)
                    dv_acc = dv_acc + _nn(pt.astype(BF16), doh)
                    dk_acc = dk_acc + _nn(dst_b, qa)
                    dqt_s[i] += _nn(ktj, dst_b)
                    dfk = dfk + _nt(ones_b, dst_b)
                    return dk_acc, dv_acc, dfk

                i0 = j // ratio
                carry = (jnp.zeros((TK, 128), F32), jnp.zeros((TK, HEAD_DIM), F32), jnp.zeros((8, TK), F32))
                carry = qblock(i0, carry, True)
                dk_acc, dv_acc, dfk = lax.fori_loop(i0 + 1, nq, functools.partial(qblock, masked=False), carry)
                out_s[1, e, pl.ds(krow, TK), :] = dk_acc[:, :HEAD_DIM]
                out_s[2, e, pl.ds(krow, TK), :] = dv_acc
                dfk_s[j] = dfk
                return 0

            lax.fori_loop(0, nkb, kblock, 0)
            for ii in range(nq):
                dqt = dqt_s[ii]
                out_s[0, e, ii * TQ:(ii + 1) * TQ, :] = _tn(dqt[0:HEAD_DIM, :].astype(BF16), eye) * SCALE
                for kk in range(ratio):
                    jj = ii * ratio + kk
                    df_ref[e:e + 1, jj * TK:(jj + 1) * TK] = dqt[67:68, kk * TK:(kk + 1) * TK] - dfk_s[jj][0:1, :]
        for k, ref in enumerate((dq_ref, dk_ref, dv_ref)):
            ref[...] = jnp.concatenate([out_s[k, 0], out_s[k, 1]], axis=1).astype(BF16)

    aug_blk = pl.BlockSpec((2, seq, 128), lambda b, p: (p, b, 0))
    pair_blk = pl.BlockSpec((seq, 128), lambda b, p: (b, p))
    row_blk = pl.BlockSpec((None, 2, seq), lambda b, p: (p, 0, b))
    return _pallas(
        body, name="fox_bwd", grid=(t_tok // seq, N_HEADS // 2),
        in_specs=[aug_blk, aug_blk, pl.BlockSpec((2, nkb, 128, TK), lambda b, p: (p, b, 0, 0)),
                  pl.BlockSpec((seq, 128), lambda b, p: (b, C_BV // 128 + p)), pair_blk, pair_blk, row_blk],
        out_specs=[pair_blk, pair_blk, pair_blk, row_blk],
        out_shape=[jax.ShapeDtypeStruct((t_tok, 512), BF16)] * 3
        + [jax.ShapeDtypeStruct((N_HEADS // 2, 2, t_tok), F32)],
        scratch_shapes=[pltpu.VMEM((nq, 128, TQ), F32), pltpu.VMEM((3, 2, seq, HEAD_DIM), F32),
                        pltpu.VMEM((nq, 2, 8, TQ), F32), pltpu.VMEM((nkb, 8, TK), F32)],
        compiler_params=_cp(("arbitrary", "arbitrary"), VMEM_BIG))(q_aug, k_aug, kt, qkvg, du_b, b_out, lse)


def _fox_fwd(qkvg, f_row, f_col, seq):
    t_tok = qkvg.shape[0]
    nq = seq // TQ

    def body(q_ref, k_ref, v_ref, fr_ref, fc_ref, o_ref, lse_ref, fk_s):
        i = pl.program_id(1)
        for jj in range(nq):
            fk_s[jj] = fr_ref[:, jj * TQ:(jj + 1) * TQ]
        fcol = fc_ref[...]
        tpos = i * TQ + lax.broadcasted_iota(jnp.int32, (TQ, 1), 0)
        lane = lax.broadcasted_iota(jnp.int32, (TQ, 128), 1)
        lse_tile = jnp.zeros((TQ, 128), F32)
        for p in range(N_HEADS // 2):
            q2 = q_ref[:, 128 * p:128 * (p + 1)]
            qs = [q2[:, :HEAD_DIM], q2[:, HEAD_DIM:]]
            fqs = [_col(fcol, 2 * p + e) for e in range(2)]

            def kblock(j, carry):
                row0 = pl.multiple_of(j * TQ, TQ)
                k2 = k_ref[pl.ds(row0, TQ), 128 * p:128 * (p + 1)]
                v2 = v_ref[pl.ds(row0, TQ), 128 * p:128 * (p + 1)]
                fk8 = fk_s[j]
                spos = j * TQ + lax.broadcasted_iota(jnp.int32, (1, TQ), 1)
                keep = spos <= tpos
                new = []
                for e in range(2):
                    m, l, acc = carry[3 * e:3 * e + 3]
                    kh = k2[:, e * HEAD_DIM:(e + 1) * HEAD_DIM]
                    vh = v2[:, e * HEAD_DIM:(e + 1) * HEAD_DIM]
                    s = _nt(qs[e], kh) * SCALE + (fqs[e] - fk8[2 * p + e:2 * p + e + 1, :])
                    s = jnp.where(keep, s, NEG)
                    m_new = jnp.maximum(m, jnp.max(s, axis=1, keepdims=True))
                    alpha = jnp.exp(m - m_new)
                    pe = jnp.exp(s - m_new)
                    l = alpha * l + jnp.sum(pe, axis=1, keepdims=True)
                    acc = alpha * acc + _nn(pe.astype(BF16), vh)
                    new += [m_new, l, acc]
                return tuple(new)

            init = (jnp.full((TQ, 1), NEG, F32), jnp.zeros((TQ, 1), F32), jnp.zeros((TQ, HEAD_DIM), F32)) * 2
            res = lax.fori_loop(0, i + 1, kblock, init)
            outs = []
            for e in range(2):
                m, l, acc = res[3 * e:3 * e + 3]
                outs.append(acc / l)
                lse_tile = jnp.where(lane == 2 * p + e, m + jnp.log(l), lse_tile)
            o_ref[:, 128 * p:128 * (p + 1)] = jnp.concatenate(outs, axis=1).astype(BF16)
        lse_ref[...] = lse_tile

    return _pallas(
        body, name="fox_fwd", grid=(t_tok // seq, nq),
        in_specs=[pl.BlockSpec((TQ, 512), lambda b, i: (b * nq + i, C_BQ // 512)),
                  pl.BlockSpec((seq, 512), lambda b, i: (b, C_BK // 512)),
                  pl.BlockSpec((seq, 512), lambda b, i: (b, C_BV // 512)),
                  pl.BlockSpec((N_HEADS, seq), lambda b, i: (0, b)),
                  pl.BlockSpec((TQ, 128), lambda b, i: (b * nq + i, 0))],
        out_specs=[pl.BlockSpec((TQ, 512), lambda b, i: (b * nq + i, 0)),
                   pl.BlockSpec((TQ, 128), lambda b, i: (b * nq + i, 0))],
        out_shape=[jax.ShapeDtypeStruct((t_tok, 512), BF16), jax.ShapeDtypeStruct((t_tok, 128), F32)],
        scratch_shapes=[pltpu.VMEM((nq, N_HEADS, TQ), F32)],
        compiler_params=_cp(("arbitrary", "arbitrary"), VMEM_MID))(qkvg, qkvg, qkvg, f_row, f_col)


def _fox_bwd(qkvg, du_b, b_out, lse, f_row, f_col, seq):
    t_tok = qkvg.shape[0]
    nq = seq // TQ

    def body(q_ref, k_ref, v_ref, do_ref, o_ref, lse_ref, fr_ref, fc_ref,
             dq_ref, dk_ref, dv_ref, df_ref, dq_s, dk_s, dv_s, col_s, df_s, fk_s):
        p = pl.program_id(1)
        for jj in range(nq):
            fk_s[jj] = fr_ref[:, jj * TQ:(jj + 1) * TQ]
        eye = (lax.broadcasted_iota(jnp.int32, (TQ, TQ), 0) == lax.broadcasted_iota(jnp.int32, (TQ, TQ), 1)).astype(F32)
        for e in range(2):
            h = 2 * p + e
            lo, hi = e * HEAD_DIM, (e + 1) * HEAD_DIM
            for ii in range(nq):
                rows = slice(ii * TQ, (ii + 1) * TQ)
                do = do_ref[rows, :][:, lo:hi].astype(F32)
                ov = o_ref[rows, :][:, lo:hi].astype(F32)
                col_s[0, rows, :] = jnp.sum(do * ov, axis=1, keepdims=True)
                col_s[1, rows, :] = _col(lse_ref[rows, :], h)
                col_s[2, rows, :] = _col(fc_ref[rows, :], h)
                dq_s[rows, :] = jnp.zeros((TQ, HEAD_DIM), F32)
                df_s[ii] = jnp.zeros((8, TQ), F32)
                col_s[3, rows, :] = jnp.zeros((TQ, 1), F32)

            def kblock(j, _):
                krow = pl.multiple_of(j * TQ, TQ)
                kh = k_ref[pl.ds(krow, TQ), :][:, lo:hi]
                vh = v_ref[pl.ds(krow, TQ), :][:, lo:hi]
                fk = _row(fk_s[j], h)
                spos = j * TQ + lax.broadcasted_iota(jnp.int32, (1, TQ), 1)

                def qblock(i, carry):
                    dk_acc, dv_acc, dfk = carry
                    qrow = pl.multiple_of(i * TQ, TQ)
                    qh = q_ref[pl.ds(qrow, TQ), :][:, lo:hi]
                    doh = do_ref[pl.ds(qrow, TQ), :][:, lo:hi]
                    delta = col_s[0, pl.ds(qrow, TQ), :]
                    lse_q = col_s[1, pl.ds(qrow, TQ), :]
                    fq = col_s[2, pl.ds(qrow, TQ), :]
                    tpos = i * TQ + lax.broadcasted_iota(jnp.int32, (TQ, 1), 0)
                    s = _nt(qh, kh) * SCALE + (fq - fk)
                    pr = jnp.where(spos <= tpos, jnp.exp(s - lse_q), 0.0)
                    dp = _nt(doh, vh)
                    ds = pr * (dp - delta)
                    ds_b = ds.astype(BF16)
                    dv_acc = dv_acc + _tn(pr.astype(BF16), doh)
                    dk_acc = dk_acc + _tn(ds_b, qh)
                    dq_s[pl.ds(qrow, TQ), :] += _nn(ds_b, kh)
                    col_s[3, pl.ds(qrow, TQ), :] += jnp.sum(ds, axis=1, keepdims=True)
                    dfk = dfk + jnp.sum(ds, axis=0, keepdims=True)
                    return dk_acc, dv_acc, dfk

                zero = jnp.zeros((TQ, HEAD_DIM), F32)
                dk_acc, dv_acc, dfk = lax.fori_loop(j, nq, qblock, (zero, zero, jnp.zeros((1, TQ), F32)))
                dk_s[e, pl.ds(krow, TQ), :] = dk_acc * SCALE
                dv_s[e, pl.ds(krow, TQ), :] = dv_acc
                df_s[j] -= jnp.broadcast_to(dfk, (8, TQ))
                return 0

            lax.fori_loop(0, nq, kblock, 0)
            dq_s2 = dq_s[...] * SCALE
            dk_s[2 + e] = dq_s2
            for ii in range(nq):
                dfq = jnp.broadcast_to(col_s[3, ii * TQ:(ii + 1) * TQ, :], (TQ, 128))
                df_ref[e:e + 1, ii * TQ:(ii + 1) * TQ] = _tn(dfq, eye, HI)[0:1, :] + df_s[ii][0:1, :]
        dq_ref[...] = jnp.concatenate([dk_s[2], dk_s[3]], axis=1).astype(BF16)
        dk_ref[...] = jnp.concatenate([dk_s[0], dk_s[1]], axis=1).astype(BF16)
        dv_ref[...] = jnp.concatenate([dv_s[0], dv_s[1]], axis=1).astype(BF16)

    blk = lambda off: pl.BlockSpec((seq, 128), lambda b, p: (b, off // 128 + p))
    out_blk = pl.BlockSpec((seq, 128), lambda b, p: (b, p))
    return _pallas(
        body, name="fox_bwd", grid=(t_tok // seq, N_HEADS // 2),
        in_specs=[blk(C_BQ), blk(C_BK), blk(C_BV), out_blk, out_blk,
                  pl.BlockSpec((seq, 128), lambda b, p: (b, 0)),
                  pl.BlockSpec((N_HEADS, seq), lambda b, p: (0, b)),
                  pl.BlockSpec((seq, 128), lambda b, p: (b, 0))],
        out_specs=[out_blk, out_blk, out_blk, pl.BlockSpec((None, 2, seq), lambda b, p: (p, 0, b))],
        out_shape=[jax.ShapeDtypeStruct((t_tok, 512), BF16)] * 3
        + [jax.ShapeDtypeStruct((N_HEADS // 2, 2, t_tok), F32)],
        scratch_shapes=[pltpu.VMEM((seq, HEAD_DIM), F32), pltpu.VMEM((4, seq, HEAD_DIM), F32),
                        pltpu.VMEM((2, seq, HEAD_DIM), F32), pltpu.VMEM((4, seq, 1), F32),
                        pltpu.VMEM((nq, 8, TQ), F32), pltpu.VMEM((nq, N_HEADS, TQ), F32)],
        compiler_params=_cp(("arbitrary", "arbitrary"), VMEM_BIG))(qkvg, qkvg, qkvg, du_b, b_out, lse, f_row, f_col)


def _swa_window(k_ref, v_ref, n):
    prev = pl.multiple_of(jnp.maximum(n - 1, 0) * BLOCK, BLOCK)
    cur = pl.multiple_of(n * BLOCK, BLOCK)
    kwin = jnp.concatenate([k_ref[pl.ds(prev, BLOCK), :], k_ref[pl.ds(cur, BLOCK), :]], axis=0)
    vwin = jnp.concatenate([v_ref[pl.ds(prev, BLOCK), :], v_ref[pl.ds(cur, BLOCK), :]], axis=0)
    ti = lax.broadcasted_iota(jnp.int32, (BLOCK, 2 * BLOCK), 0)
    sj = lax.broadcasted_iota(jnp.int32, (BLOCK, 2 * BLOCK), 1)
    rel = ti - sj + BLOCK
    first_key = jnp.where(n > 0, 0, BLOCK)
    mask = jnp.logical_and(jnp.logical_and(rel >= 0, rel < BLOCK), sj >= first_key)
    return kwin, vwin, mask, prev, cur


def _head_cols(ref, h):
    pair = ref[:, 128 * (h // 2):128 * (h // 2 + 1)]
    return pair[:, (h % 2) * HEAD_DIM:(h % 2 + 1) * HEAD_DIM]


def _swa_logits(q_ref, kwin, bias_ref, h, mask):
    hk = h // KV_GROUP
    s = _nt(_head_cols(q_ref, h), kwin[:, hk * HEAD_DIM:(hk + 1) * HEAD_DIM]) * SCALE + bias_ref[h]
    return jnp.where(mask, s, NEG)


def _swa_fwd(qkvg, bias, sinks, seq):
    t_tok = qkvg.shape[0]
    nb = seq // BLOCK

    def body(sink_ref, q_ref, k_ref, v_ref, bias_ref, o_ref, lse_ref, s_s, p_s, den_s):
        n = pl.program_id(1)
        kwin, vwin, mask, _, _ = _swa_window(k_ref, v_ref, n)
        for h in range(N_HEADS):
            s_s[h] = _swa_logits(q_ref, kwin, bias_ref, h, mask)
        lane = lax.broadcasted_iota(jnp.int32, (BLOCK, 128), 1)
        lse_tile = jnp.zeros((BLOCK, 128), F32)
        for h in range(N_HEADS):
            s = s_s[h]
            sink = sink_ref[h]
            m = jnp.maximum(jnp.max(s, axis=1, keepdims=True), sink)
            pe = jnp.exp(s - m)
            den = jnp.sum(pe, axis=1, keepdims=True) + jnp.exp(sink - m)
            p_s[h] = pe.astype(BF16)
            den_s[h] = den
            lse_tile = jnp.where(lane == h, m + jnp.log(den), lse_tile)
        lse_ref[...] = lse_tile
        for pr in range(N_HEADS // 2):
            outs = []
            for h in (2 * pr, 2 * pr + 1):
                hk = h // KV_GROUP
                outs.append(_nn(p_s[h], vwin[:, hk * HEAD_DIM:(hk + 1) * HEAD_DIM]) / den_s[h])
            o_ref[:, 128 * pr:128 * (pr + 1)] = jnp.concatenate(outs, axis=1).astype(BF16)

    return _pallas(
        body, name="swa_fwd", grid=(t_tok // seq, nb),
        in_specs=[pl.BlockSpec(memory_space=pltpu.SMEM),
                  pl.BlockSpec((BLOCK, 512), lambda b, n: (b * nb + n, C_AQ // 512)),
                  pl.BlockSpec((seq, 128), lambda b, n: (b, C_AK // 128)),
                  pl.BlockSpec((seq, 128), lambda b, n: (b, C_AV // 128)),
                  pl.BlockSpec((N_HEADS, BLOCK, 2 * BLOCK), lambda b, n: (0, 0, 0))],
        out_specs=[pl.BlockSpec((BLOCK, 512), lambda b, n: (b * nb + n, 0)),
                   pl.BlockSpec((BLOCK, 128), lambda b, n: (b * nb + n, 0))],
        out_shape=[jax.ShapeDtypeStruct((t_tok, 512), BF16), jax.ShapeDtypeStruct((t_tok, 128), F32)],
        scratch_shapes=[pltpu.VMEM((N_HEADS, BLOCK, 2 * BLOCK), F32), pltpu.VMEM((N_HEADS, BLOCK, 2 * BLOCK), BF16),
                        pltpu.VMEM((N_HEADS, BLOCK, 1), F32)],
        compiler_params=_cp(("arbitrary", "arbitrary"), VMEM_MID))(sinks, qkvg, qkvg, qkvg, bias)


def _swa_bwd(qkvg, du_a, a_out, lse, bias, sinks, seq):
    t_tok = qkvg.shape[0]
    nb = seq // BLOCK

    def body(sink_ref, q_ref, k_ref, v_ref, do_ref, o_ref, lse_ref, bias_ref,
             dq_ref, dkv_ref, dbias_ref, dsink_ref, kv_s, s_s, dp_s, pb_s, db_s):
        b, n = pl.program_id(0), pl.program_id(1)

        @pl.when(jnp.logical_and(b == 0, n == 0))
        def _():
            dbias_ref[...] = jnp.zeros_like(dbias_ref)
            dsink_ref[...] = jnp.zeros_like(dsink_ref)

        @pl.when(n == 0)
        def _():
            kv_s[...] = jnp.zeros_like(kv_s)

        kwin, vwin, mask, prev, cur = _swa_window(k_ref, v_ref, n)
        for h in range(N_HEADS):
            hk = h // KV_GROUP
            s_s[h] = _swa_logits(q_ref, kwin, bias_ref, h, mask)
            dp_s[h] = _nt(_head_cols(do_ref, h), vwin[:, hk * HEAD_DIM:(hk + 1) * HEAD_DIM])
        lse_tile = lse_ref[...]
        for h in range(N_HEADS):
            delta = jnp.sum(_head_cols(do_ref, h).astype(F32) * _head_cols(o_ref, h).astype(F32), axis=1, keepdims=True)
            lse_h = _col(lse_tile, h)
            pe = jnp.exp(s_s[h] - lse_h)
            ds = pe * (dp_s[h] - delta)
            dbias_ref[h] += ds
            psink = jnp.exp(sink_ref[h] - lse_h)
            dsink_ref[h:h + 1, :] += jnp.broadcast_to(jnp.sum(-psink * delta, axis=0, keepdims=True), (1, 128))
            pb_s[h] = pe.astype(BF16)
            db_s[h] = ds.astype(BF16)
        for pr in range(N_HEADS // 2):
            dqs = []
            for h in (2 * pr, 2 * pr + 1):
                hk = h // KV_GROUP
                dqs.append(_nn(db_s[h], kwin[:, hk * HEAD_DIM:(hk + 1) * HEAD_DIM]) * SCALE)
            dq_ref[:, 128 * pr:128 * (pr + 1)] = jnp.concatenate(dqs, axis=1).astype(BF16)
        dks, dvs = [], []
        for hk in range(N_HEADS // KV_GROUP):
            dk = jnp.zeros((2 * BLOCK, HEAD_DIM), F32)
            dv = jnp.zeros((2 * BLOCK, HEAD_DIM), F32)
            for h in range(hk * KV_GROUP, (hk + 1) * KV_GROUP):
                dk = dk + _tn(db_s[h], _head_cols(q_ref, h))
                dv = dv + _tn(pb_s[h], _head_cols(do_ref, h))
            dks.append(dk * SCALE)
            dvs.append(dv)
        upd = jnp.concatenate(dks + dvs, axis=1)
        kv_s[pl.ds(prev, BLOCK), :] += upd[:BLOCK]
        kv_s[pl.ds(cur, BLOCK), :] += upd[BLOCK:]

        @pl.when(n == nb - 1)
        def _():
            dkv_ref[...] = kv_s[...].astype(BF16)

    return _pallas(
        body, name="swa_bwd", grid=(t_tok // seq, nb),
        in_specs=[pl.BlockSpec(memory_space=pltpu.SMEM),
                  pl.BlockSpec((BLOCK, 512), lambda b, n: (b * nb + n, C_AQ // 512)),
                  pl.BlockSpec((seq, 128), lambda b, n: (b, C_AK // 128)),
                  pl.BlockSpec((seq, 128), lambda b, n: (b, C_AV // 128)),
                  pl.BlockSpec((BLOCK, 512), lambda b, n: (b * nb + n, 0)),
                  pl.BlockSpec((BLOCK, 512), lambda b, n: (b * nb + n, 0)),
                  pl.BlockSpec((BLOCK, 128), lambda b, n: (b * nb + n, 0)),
                  pl.BlockSpec((N_HEADS, BLOCK, 2 * BLOCK), lambda b, n: (0, 0, 0))],
        out_specs=[pl.BlockSpec((BLOCK, 512), lambda b, n: (b * nb + n, 0)),
                   pl.BlockSpec((seq, 256), lambda b, n: (b, 0)),
                   pl.BlockSpec((N_HEADS, BLOCK, 2 * BLOCK), lambda b, n: (0, 0, 0)),
                   pl.BlockSpec((N_HEADS, 128), lambda b, n: (0, 0))],
        out_shape=[jax.ShapeDtypeStruct((t_tok, 512), BF16), jax.ShapeDtypeStruct((t_tok, 256), BF16),
                   jax.ShapeDtypeStruct((N_HEADS, BLOCK, 2 * BLOCK), F32), jax.ShapeDtypeStruct((N_HEADS, 128), F32)],
        scratch_shapes=[pltpu.VMEM((seq, 256), F32),
                        pltpu.VMEM((N_HEADS, BLOCK, 2 * BLOCK), F32), pltpu.VMEM((N_HEADS, BLOCK, 2 * BLOCK), F32),
                        pltpu.VMEM((N_HEADS, BLOCK, 2 * BLOCK), BF16), pltpu.VMEM((N_HEADS, BLOCK, 2 * BLOCK), BF16)],
        compiler_params=_cp(("arbitrary", "arbitrary"), VMEM_MID))(sinks, qkvg, qkvg, qkvg, du_a, a_out, lse, bias)


def _out_proj(name, u_parts, gate_arr, gate_blk, w_out, x, gmod, seq):
    t_tok = x.shape[0]
    nu = len(u_parts)

    def body(*refs):
        u_refs = refs[:nu]
        g_ref, w_ref, x_ref, gm_ref, yg_ref, y_ref, xn_ref = refs[nu:]
        u = jnp.concatenate([r[...].astype(F32) for r in u_refs], axis=1) if nu > 1 else u_refs[0][...].astype(F32)
        yg = (u * _silu(g_ref[...].astype(F32))).astype(BF16)
        yg_ref[...] = yg
        y = _nn(yg, w_ref[...])
        y_ref[...] = y.astype(BF16)
        xn_ref[...] = x_ref[...] + gm_ref[...] * y

    row = lambda w: pl.BlockSpec((TM, w), lambda i: (i, 0))
    in_specs = [row(u.shape[1]) for u in u_parts]
    in_specs += [pl.BlockSpec((TM, D_MODEL), lambda i: (i, gate_blk)),
                 pl.BlockSpec((D_MODEL, D_MODEL), lambda i: (0, 0)), row(D_MODEL),
                 pl.BlockSpec((None, 1, D_MODEL), lambda i: (i * TM // seq, 0, 0))]
    return _pallas(
        body, name=name, grid=(t_tok // TM,), in_specs=in_specs,
        out_specs=[row(D_MODEL)] * 3,
        out_shape=[jax.ShapeDtypeStruct((t_tok, D_MODEL), BF16)] * 2 + [jax.ShapeDtypeStruct((t_tok, D_MODEL), F32)],
        compiler_params=_cp(("arbitrary",), VMEM_MID))(*u_parts, gate_arr, w_out, x, gmod)


def _out_proj_bwd(name, dxn, gmod, y, w_out, seq, attn=None):
    t_tok = dxn.shape[0]
    tiles_per_seq = seq // TM

    def body(*refs):
        if attn is None:
            dxn_ref, gm_ref, y_ref, w_ref, dy_ref, dgm_ref, dyg_ref = refs
        else:
            dxn_ref, gm_ref, y_ref, w_ref, a_ref, b_ref, g_ref, dy_ref, dgm_ref, dua_ref, dub_ref, dg_ref = refs
        i = pl.program_id(0)
        dxv = dxn_ref[...]
        dy = (dxv * gm_ref[...]).astype(BF16)
        dy_ref[...] = dy

        @pl.when(i % tiles_per_seq == 0)
        def _():
            dgm_ref[...] = jnp.zeros_like(dgm_ref)

        dgm_ref[...] += jnp.sum(dxv * y_ref[...].astype(F32), axis=0, keepdims=True)
        dyg = _nt(dy, w_ref[...])
        if attn is None:
            dyg_ref[...] = dyg
        else:
            gt = g_ref[...].astype(F32)
            du = dyg * _silu(gt)
            dua_ref[...] = du[:, :512].astype(BF16)
            dub_ref[...] = du[:, 512:].astype(BF16)
            u = jnp.concatenate([a_ref[...].astype(F32), b_ref[...].astype(F32)], axis=1)
            dg_ref[...] = (dyg * u * _dsilu(gt)).astype(BF16)

    row = lambda w: pl.BlockSpec((TM, w), lambda i: (i, 0))
    mod_spec = pl.BlockSpec((None, 1, D_MODEL), lambda i: (i * TM // seq, 0, 0))
    in_specs = [row(D_MODEL), mod_spec, row(D_MODEL), pl.BlockSpec((D_MODEL, D_MODEL), lambda i: (0, 0))]
    out_specs = [row(D_MODEL), mod_spec]
    out_shape = [jax.ShapeDtypeStruct((t_tok, D_MODEL), BF16), jax.ShapeDtypeStruct(gmod.shape, F32)]
    args = [dxn, gmod, y, w_out]
    if attn is None:
        out_specs.append(row(D_MODEL))
        out_shape.append(jax.ShapeDtypeStruct((t_tok, D_MODEL), F32))
    else:
        in_specs += [row(512), row(512), pl.BlockSpec((TM, D_MODEL), lambda i: (i, C_GATE // D_MODEL))]
        out_specs += [row(512), row(512), row(D_MODEL)]
        out_shape += [jax.ShapeDtypeStruct((t_tok, 512), BF16)] * 2 + [jax.ShapeDtypeStruct((t_tok, D_MODEL), BF16)]
        args += list(attn)
    return _pallas(body, name=name, grid=(t_tok // TM,), in_specs=in_specs, out_specs=out_specs,
                   out_shape=out_shape, compiler_params=_cp(("arbitrary",), VMEM_MID))(*args)


def _norm_bwd(name, parts, w, x, g, scale, dxn, seq, rows_part=None):
    t_tok = x.shape[0]
    npart = len(parts)
    w3d = w.ndim == 3
    tiles_per_seq = seq // TM
    nrow_in = 0 if rows_part is None else 2

    def body(*refs):
        p_refs = refs[:npart]
        w_ref, x_ref, g_ref, sc_ref, dxn_ref = refs[npart:npart + 5]
        dx_ref, dss_ref, dg_ref = refs[npart + 5 + nrow_in:]
        i = pl.program_id(0)
        dh = jnp.zeros((TM, D_MODEL), F32)
        if rows_part is not None:
            r_ref, wr_ref = refs[npart + 5:npart + 7]
            dh = dh + _tn(r_ref[...].astype(BF16), wr_ref[...])
        for (arr, off), p_ref in zip(parts, p_refs):
            width = arr.shape[1]
            for j in range(width // 256):
                pj = p_ref[:, j * 256:(j + 1) * 256]
                c0 = off + j * 256
                wj = w_ref[c0 // 256] if w3d else w_ref[:, c0:c0 + 256]
                dh = dh + _nt(pj, wj)
        xv = x_ref[...]
        rstd = lax.rsqrt(jnp.mean(xv * xv, axis=-1, keepdims=True) + EPS)
        xhat = xv * rstd
        gv = g_ref[...]
        nrm = xhat * gv

        @pl.when(i % tiles_per_seq == 0)
        def _():
            dss_ref[...] = jnp.zeros_like(dss_ref)

        @pl.when(i == 0)
        def _():
            dg_ref[...] = jnp.zeros_like(dg_ref)

        dss_ref[0:1, :] += jnp.sum(dh, axis=0, keepdims=True)
        dss_ref[1:2, :] += jnp.sum(dh * nrm, axis=0, keepdims=True)
        dn = dh * (1.0 + sc_ref[...])
        dg_ref[0:1, :] += jnp.sum(dn * xhat, axis=0, keepdims=True)
        dxhat = dn * gv
        dx_ref[...] = rstd * (dxhat - xhat * jnp.mean(dxhat * xhat, axis=-1, keepdims=True)) + dxn_ref[...]

    row = lambda wd: pl.BlockSpec((TM, wd), lambda i: (i, 0))
    w_spec = (pl.BlockSpec(w.shape, lambda i: (0, 0, 0)) if w3d else pl.BlockSpec(w.shape, lambda i: (0, 0)))
    in_specs = [row(a.shape[1]) for a, _ in parts]
    in_specs += [w_spec, row(D_MODEL), pl.BlockSpec((1, D_MODEL), lambda i: (0, 0)),
                 pl.BlockSpec((None, 1, D_MODEL), lambda i: (i * TM // seq, 0, 0)), row(D_MODEL)]
    args = [a for a, _ in parts] + [w, x, g, scale, dxn]
    if rows_part is not None:
        in_specs += [pl.BlockSpec((8, TM), lambda i: (0, i)), pl.BlockSpec((8, D_MODEL), lambda i: (0, 0))]
        args += list(rows_part)
    nseq = t_tok // seq
    return _pallas(
        body, name=name, grid=(t_tok // TM,), in_specs=in_specs,
        out_specs=[row(D_MODEL), pl.BlockSpec((None, 8, D_MODEL), lambda i: (i * TM // seq, 0, 0)),
                   pl.BlockSpec((8, D_MODEL), lambda i: (0, 0))],
        out_shape=[jax.ShapeDtypeStruct((t_tok, D_MODEL), F32), jax.ShapeDtypeStruct((nseq, 8, D_MODEL), F32),
                   jax.ShapeDtypeStruct((8, D_MODEL), F32)],
        compiler_params=_cp(("arbitrary",), VMEM_BIG))(*args)


def _core_major(d):
    return 4 * (d % 2) + d // 2


def _dw(name, a, parts, blocked=None):
    t_tok, ka = a.shape
    tt = 512
    npart = len(parts)
    nt = t_tok // tt

    def body(*refs):
        a_ref = refs[0]
        p_refs = refs[1:1 + npart]
        o_refs = refs[1 + npart:1 + 2 * npart]
        acc_refs = refs[1 + 2 * npart:]
        t = pl.program_id(0)
        av = a_ref[...]
        for p_ref, acc in zip(p_refs, acc_refs):
            upd = _tn(av, p_ref[...])

            @pl.when(t == 0)
            def _():
                acc[...] = upd

            @pl.when(t > 0)
            def _():
                acc[...] += upd

        @pl.when(t == nt - 1)
        def _():
            for o_ref, acc in zip(o_refs, acc_refs):
                if blocked is None:
                    o_ref[...] = acc[...].astype(BF16)
                else:
                    for j in range(o_ref.shape[0]):
                        o_ref[_core_major(j)] = acc[:, j * blocked:(j + 1) * blocked].astype(BF16)

    in_specs = [pl.BlockSpec((tt, ka), lambda t: (t, 0))]
    in_specs += [pl.BlockSpec((tt, p.shape[1]), lambda t: (t, 0)) for p in parts]
    if blocked is None:
        out_shape = [jax.ShapeDtypeStruct((ka, p.shape[1]), BF16) for p in parts]
        out_specs = [pl.BlockSpec((ka, p.shape[1]), lambda t: (0, 0)) for p in parts]
    else:
        out_shape = [jax.ShapeDtypeStruct((p.shape[1] // blocked, ka, blocked), BF16) for p in parts]
        out_specs = [pl.BlockSpec((p.shape[1] // blocked, ka, blocked), lambda t: (0, 0, 0)) for p in parts]
    return _pallas(body, name=name, grid=(nt,), in_specs=in_specs, out_specs=out_specs, out_shape=out_shape,
                   scratch_shapes=[pltpu.VMEM((ka, p.shape[1]), F32) for p in parts],
                   compiler_params=_cp(("arbitrary",), VMEM_BIG))(a, *parts)


def _dw_rows(name, rows_t, h):
    t_tok = h.shape[0]
    tt = 512

    def body(r_ref, h_ref, o_ref):
        @pl.when(pl.program_id(0) == 0)
        def _():
            o_ref[...] = jnp.zeros_like(o_ref)

        o_ref[...] += _nn(r_ref[...].astype(BF16), h_ref[...])

    return _pallas(body, name=name, grid=(t_tok // tt,),
                   in_specs=[pl.BlockSpec((8, tt), lambda t: (0, t)), pl.BlockSpec((tt, D_MODEL), lambda t: (t, 0))],
                   out_specs=pl.BlockSpec((8, D_MODEL), lambda t: (0, 0)),
                   out_shape=jax.ShapeDtypeStruct((8, D_MODEL), F32),
                   compiler_params=_cp(("arbitrary",), VMEM_MID))(rows_t, h)


def _lru_gates(xc, blk, wa_ref, wx_ref, ba_ref, bx_ref, sp):
    cols = slice(blk * LRU_BLOCK_W, (blk + 1) * LRU_BLOCK_W)
    xb = xc[:, cols].astype(BF16)
    r = _sigmoid(_nn(xb, wa_ref[blk].astype(BF16)) + ba_ref[:, cols])
    ig = _sigmoid(_nn(xb, wx_ref[blk].astype(BF16)) + bx_ref[:, cols])
    log_a = -LRU_C * r * sp[:, cols]
    a = jnp.exp(log_a)
    mult = jnp.sqrt(_neg_expm1(2.0 * log_a))
    return xb, r, ig, a, mult


def _softplus_neg(lam):
    return jnp.maximum(-lam, 0.0) + jnp.log(1.0 + jnp.exp(-jnp.abs(lam)))


def _conv_taps(xe_ref, cw_ref, cb_ref):
    xc = cb_ref[...] + xe_ref[8:8 + TC, :] * cw_ref[3:4, :]
    for k in range(1, 4):
        xc = xc + xe_ref[8 - k:8 - k + TC, :] * cw_ref[3 - k:4 - k, :]
    return xc


def _lru_fwd(proj, cw, cb, w_a, b_a, w_x, b_x, lam, seq):
    t_tok = proj.shape[0]
    nc = seq // TC

    def body(x_ref, cw_ref, cb_ref, wa_ref, ba_ref, wx_ref, bx_ref, lam_ref, hs_ref, xe_s, a_s, u_s, h_s):
        c = pl.program_id(1)

        @pl.when(c == 0)
        def _():
            xe_s[0:8, :] = jnp.zeros((8, D_MODEL), F32)
            h_s[...] = jnp.zeros_like(h_s)

        xe_s[8:8 + TC, :] = x_ref[...]
        xc = _conv_taps(xe_s, cw_ref, cb_ref)
        sp = _softplus_neg(lam_ref[...])
        for blk in range(LRU_BLOCKS):
            cols = slice(blk * LRU_BLOCK_W, (blk + 1) * LRU_BLOCK_W)
            _, _, ig, a, mult = _lru_gates(xc, blk, wa_ref, wx_ref, ba_ref, bx_ref, sp)
            a_s[:, cols] = a
            u_s[:, cols] = mult * ig * xc[:, cols]

        def step(t, h):
            h = a_s[pl.ds(t, 1), :] * h + u_s[pl.ds(t, 1), :]
            hs_ref[pl.ds(t, 1), :] = h
            return h

        h_s[0:1, :] = lax.fori_loop(0, TC, step, h_s[0:1, :], unroll=8)
        xe_s[0:8, :] = xe_s[TC:TC + 8, :]

    full = lambda shape: pl.BlockSpec(shape, lambda b, c: (0,) * len(shape))
    return _pallas(
        body, name="lru_fwd", grid=(t_tok // seq, nc),
        in_specs=[pl.BlockSpec((TC, D_MODEL), lambda b, c: (b * nc + c, 0)), full((4, D_MODEL)), full((1, D_MODEL)),
                  full((LRU_BLOCKS, LRU_BLOCK_W, LRU_BLOCK_W)), full((1, D_MODEL)),
                  full((LRU_BLOCKS, LRU_BLOCK_W, LRU_BLOCK_W)), full((1, D_MODEL)), full((1, D_MODEL))],
        out_specs=pl.BlockSpec((TC, D_MODEL), lambda b, c: (b * nc + c, 0)),
        out_shape=jax.ShapeDtypeStruct((t_tok, D_MODEL), F32),
        scratch_shapes=[pltpu.VMEM((TC + 8, D_MODEL), F32), pltpu.VMEM((TC, D_MODEL), F32),
                        pltpu.VMEM((TC, D_MODEL), F32), pltpu.VMEM((8, D_MODEL), F32)],
        compiler_params=_cp(("arbitrary", "arbitrary"), VMEM_MID))(proj, cw, cb, w_a, b_a, w_x, b_x, lam)


def _lru_bwd(proj, hs, dyh, cw, cb, w_a, b_a, w_x, b_x, lam, seq):
    t_tok = proj.shape[0]
    nc = seq // TC

    def body(x_ref, xh_ref, g_ref, hs_ref, hh_ref, dy_ref, cw_ref, cb_ref, wa_ref, ba_ref, wx_ref, bx_ref, lam_ref,
             dp_ref, dcw_ref, dvec_ref, dwa_ref, dwx_ref,
             xe_s, he_s, de_s, a_s, r_s, i_s, m_s, dh_s, carry_s):
        b, cr = pl.program_id(0), pl.program_id(1)
        c = nc - 1 - cr

        @pl.when(jnp.logical_and(b == 0, cr == 0))
        def _():
            dcw_ref[...] = jnp.zeros_like(dcw_ref)
            dvec_ref[...] = jnp.zeros_like(dvec_ref)
            dwa_ref[...] = jnp.zeros_like(dwa_ref)
            dwx_ref[...] = jnp.zeros_like(dwx_ref)

        @pl.when(cr == 0)
        def _():
            carry_s[...] = jnp.zeros_like(carry_s)
            de_s[TC:TC + 8, :] = jnp.zeros((8, D_MODEL), F32)

        first = c == 0
        xe_s[0:8, :] = jnp.where(first, 0.0, xh_ref[...])
        xe_s[8:8 + TC, :] = x_ref[...]
        he_s[0:8, :] = jnp.where(first, 0.0, hh_ref[...])
        he_s[8:8 + TC, :] = hs_ref[...]
        xc = _conv_taps(xe_s, cw_ref, cb_ref)
        lam_v = lam_ref[...]
        sp = _softplus_neg(lam_v)
        for blk in range(LRU_BLOCKS):
            cols = slice(blk * LRU_BLOCK_W, (blk + 1) * LRU_BLOCK_W)
            _, r, ig, a, mult = _lru_gates(xc, blk, wa_ref, wx_ref, ba_ref, bx_ref, sp)
            a_s[:, cols], r_s[:, cols], i_s[:, cols], m_s[:, cols] = a, r, ig, mult

        gt = g_ref[...]
        dyh = dy_ref[...]
        dh_s[...] = dyh * _silu(gt)
        dp_ref[:, D_MODEL:] = (dyh * hs_ref[...] * _dsilu(gt)).astype(BF16)

        def step(k, carry):
            t = TC - 1 - k
            dh = dh_s[pl.ds(t, 1), :] + carry
            dh_s[pl.ds(t, 1), :] = dh
            return a_s[pl.ds(t, 1), :] * dh

        carry_s[0:1, :] = lax.fori_loop(0, TC, step, carry_s[0:1, :], unroll=8)

        hprev = he_s[7:7 + TC, :]
        for blk in range(LRU_BLOCKS):
            cols = slice(blk * LRU_BLOCK_W, (blk + 1) * LRU_BLOCK_W)
            xcb = xc[:, cols]
            a, r, ig, mult, dh = a_s[:, cols], r_s[:, cols], i_s[:, cols], m_s[:, cols], dh_s[:, cols]
            spb = sp[:, cols]
            dmult = dh * ig * xcb
            di = dh * mult * xcb
            dxc = dh * mult * ig
            dla = dh * hprev[:, cols] * a - dmult * (a * a) / jnp.maximum(mult, 1e-20)
            dr = dla * (-LRU_C * spb)
            dsp = jnp.sum(dla * (-LRU_C * r), axis=0, keepdims=True)
            dga = dr * r * (1.0 - r)
            dgx = di * ig * (1.0 - ig)
            dga_b, dgx_b = dga.astype(BF16), dgx.astype(BF16)
            xb = xcb.astype(BF16)
            dxc = dxc + _nt(dga_b, wa_ref[blk].astype(BF16)) + _nt(dgx_b, wx_ref[blk].astype(BF16))
            dwa_ref[blk] += _tn(xb, dga_b)
            dwx_ref[blk] += _tn(xb, dgx_b)
            dvec_ref[1:2, cols] += jnp.sum(dga, axis=0, keepdims=True)
            dvec_ref[2:3, cols] += jnp.sum(dgx, axis=0, keepdims=True)
            dvec_ref[3:4, cols] += dsp * (-1.0 / (1.0 + jnp.exp(lam_v[:, cols])))
            de_s[0:TC, cols] = dxc

        dxc = de_s[0:TC, :]
        dvec_ref[0:1, :] += jnp.sum(dxc, axis=0, keepdims=True)
        dxr = dxc * cw_ref[3:4, :]
        dcw_ref[3:4, :] += jnp.sum(dxc * xe_s[8:8 + TC, :], axis=0, keepdims=True)
        for k in range(1, 4):
            dxr = dxr + de_s[k:k + TC, :] * cw_ref[3 - k:4 - k, :]
            dcw_ref[3 - k:4 - k, :] += jnp.sum(dxc * xe_s[8 - k:8 - k + TC, :], axis=0, keepdims=True)
        dp_ref[:, :D_MODEL] = dxr.astype(BF16)
        de_s[TC:TC + 8, :] = de_s[0:8, :]

    chunk = lambda col: pl.BlockSpec((TC, D_MODEL), lambda b, cr: (b * nc + nc - 1 - cr, col))
    halo = lambda col: pl.BlockSpec(
        (8, D_MODEL), lambda b, cr: (jnp.maximum((b * nc + nc - 1 - cr) * (TC // 8) - 1, 0), col))
    full = lambda shape: pl.BlockSpec(shape, lambda b, cr: (0,) * len(shape))
    wblk = (LRU_BLOCKS, LRU_BLOCK_W, LRU_BLOCK_W)
    return _pallas(
        body, name="lru_bwd", grid=(t_tok // seq, nc),
        in_specs=[chunk(0), halo(0), chunk(1), chunk(0), halo(0), chunk(0),
                  full((4, D_MODEL)), full((1, D_MODEL)), full(wblk), full((1, D_MODEL)), full(wblk),
                  full((1, D_MODEL)), full((1, D_MODEL))],
        out_specs=[pl.BlockSpec((TC, 2 * D_MODEL), lambda b, cr: (b * nc + nc - 1 - cr, 0)),
                   full((8, D_MODEL)), full((8, D_MODEL)), full(wblk), full(wblk)],
        out_shape=[jax.ShapeDtypeStruct((t_tok, 2 * D_MODEL), BF16), jax.ShapeDtypeStruct((8, D_MODEL), F32),
                   jax.ShapeDtypeStruct((8, D_MODEL), F32), jax.ShapeDtypeStruct(wblk, F32),
                   jax.ShapeDtypeStruct(wblk, F32)],
        scratch_shapes=[pltpu.VMEM((TC + 8, D_MODEL), F32), pltpu.VMEM((TC + 8, D_MODEL), F32),
                        pltpu.VMEM((TC + 8, D_MODEL), F32)]
        + [pltpu.VMEM((TC, D_MODEL), F32)] * 5 + [pltpu.VMEM((8, D_MODEL), F32)],
        compiler_params=_cp(("arbitrary", "arbitrary"), VMEM_BIG),
    )(proj, proj, proj, hs, hs, dyh, cw, cb, w_a, b_a, w_x, b_x, lam)


def _final_loss(x, g, target):
    t_tok = x.shape[0]

    def body(x_ref, g_ref, t_ref, dx_ref, loss_ref, dg_ref):
        @pl.when(pl.program_id(0) == 0)
        def _():
            loss_ref[...] = jnp.zeros_like(loss_ref)
            dg_ref[...] = jnp.zeros_like(dg_ref)

        xv = x_ref[...]
        gv = g_ref[...]
        rstd = lax.rsqrt(jnp.mean(xv * xv, axis=-1, keepdims=True) + EPS)
        xhat = xv * rstd
        err = xhat * gv - t_ref[...]
        loss_ref[0:1, :] += jnp.sum(err * err, axis=0, keepdims=True) * (0.5 / D_MODEL)
        dout = err * (1.0 / D_MODEL)
        dg_ref[0:1, :] += jnp.sum(dout * xhat, axis=0, keepdims=True)
        dxhat = dout * gv
        dx_ref[...] = rstd * (dxhat - xhat * jnp.mean(dxhat * xhat, axis=-1, keepdims=True))

    row = pl.BlockSpec((TM, D_MODEL), lambda i: (i, 0))
    acc = pl.BlockSpec((8, D_MODEL), lambda i: (0, 0))
    return _pallas(body, name="final_loss", grid=(t_tok // TM,),
                   in_specs=[row, pl.BlockSpec((1, D_MODEL), lambda i: (0, 0)), row],
                   out_specs=[row, acc, acc],
                   out_shape=[jax.ShapeDtypeStruct((t_tok, D_MODEL), F32)] + [jax.ShapeDtypeStruct((8, D_MODEL), F32)] * 2,
                   compiler_params=_cp(("arbitrary",), VMEM_MID))(x, g, target)


def _adam_math(w, g, m, v):
    m_new = ADAM_B1 * m + (1.0 - ADAM_B1) * g
    v_new = ADAM_B2 * v + (1.0 - ADAM_B2) * (g * g)
    m_hat = m_new / (1.0 - ADAM_B1 ** ADAM_STEP)
    v_hat = v_new / (1.0 - ADAM_B2 ** ADAM_STEP)
    delta = -ADAM_LR * (m_hat / (jnp.sqrt(v_hat) + ADAM_EPS) + ADAM_WD * w)
    return delta, m_new, v_new


def _sum_leading(name, x, out_dtype=F32):
    n, rows, cols = x.shape
    tr = PACK_ROWS if rows % PACK_ROWS == 0 else rows

    def body(x_ref, o_ref):
        acc = x_ref[0].astype(F32)
        for d in range(1, n):
            acc = acc + x_ref[d].astype(F32)
        o_ref[...] = acc.astype(out_dtype)

    return _pallas(body, name=name, grid=(rows // tr,),
                   in_specs=[pl.BlockSpec((n, tr, cols), lambda i: (0, i, 0))],
                   out_specs=pl.BlockSpec((tr, cols), lambda i: (i, 0)),
                   out_shape=jax.ShapeDtypeStruct((rows, cols), out_dtype),
                   compiler_params=_cp(("arbitrary",), VMEM_MID))(x)


def _adamw(name, w, m, v, g=None, parts=None):
    rows, cols = w.shape
    tr = rows if rows <= 256 else 256

    def body(*refs):
        w_ref, m_ref, v_ref, g_in, g_ref, d_ref, mo_ref, vo_ref = refs
        if parts is None:
            gv = g_in[...]
        else:
            acc = g_in[0].astype(F32)
            for d in range(1, parts.shape[0]):
                acc = acc + g_in[d].astype(F32)
            gv = acc[:, :cols]
        delta, m_new, v_new = _adam_math(w_ref[...], gv, m_ref[...], v_ref[...])
        g_ref[...] = gv
        d_ref[...] = delta
        mo_ref[...] = m_new
        vo_ref[...] = v_new

    row = pl.BlockSpec((tr, cols), lambda i: (i, 0))
    if parts is None:
        g_spec, g_arg = row, g
    else:
        g_spec, g_arg = pl.BlockSpec((parts.shape[0], tr, parts.shape[2]), lambda i: (0, i, 0)), parts
    return _pallas(body, name=name, grid=(rows // tr,), in_specs=[row, row, row, g_spec], out_specs=[row] * 4,
                   out_shape=[jax.ShapeDtypeStruct((rows, cols), F32)] * 4,
                   compiler_params=_cp(("arbitrary",), VMEM_MID))(w, m, v, g_arg)


def _pack_rows(arrs):
    rows, meta, total = [], [], 0
    for a in arrs:
        flat = a.reshape(-1)
        nrow = -(-flat.shape[0] // 1024) * 8
        rows.append(jnp.pad(flat, (0, nrow * 128 - flat.shape[0])).reshape(nrow, 128))
        meta.append((a.shape, flat.shape[0], nrow))
        total += nrow
    tail = -total % PACK_ROWS
    if tail:
        rows.append(jnp.zeros((tail, 128), F32))
    return jnp.concatenate(rows, axis=0), meta


def _unpack_rows(packed, meta):
    out, r0 = [], 0
    for shape, size, nrow in meta:
        out.append(packed[r0:r0 + nrow].reshape(-1)[:size].reshape(shape))
        r0 += nrow
    return out


WEIGHTS = ["rel_bias", "norm_g", "ada_w", "ada_b", "attn_w_in", "attn_sinks", "attn_b_f", "attn_w_out", "lru_w_in",
           "lru_conv_w", "lru_conv_b", "lru_w_a", "lru_b_a", "lru_w_x", "lru_b_x", "lru_lambda", "lru_w_out", "final_g"]
BIG = ["ada_w", "attn_w_in", "attn_w_out", "lru_w_in", "lru_w_out"]
PACK_ROWS = 256


def kernel(x, c, rel_bias, norm_g, ada_w, ada_b, attn_w_in, attn_sinks, attn_b_f, attn_w_out, lru_w_in, lru_conv_w, lru_conv_b, lru_w_a, lru_b_a, lru_w_x, lru_b_x, lru_lambda, lru_w_out, final_g, loss_target, m_rel_bias, m_norm_g, m_ada_w, m_ada_b, m_attn_w_in, m_attn_sinks, m_attn_b_f, m_attn_w_out, m_lru_w_in, m_lru_conv_w, m_lru_conv_b, m_lru_w_a, m_lru_b_a, m_lru_w_x, m_lru_b_x, m_lru_lambda, m_lru_w_out, m_final_g, v_rel_bias, v_norm_g, v_ada_w, v_ada_b, v_attn_w_in, v_attn_sinks, v_attn_b_f, v_attn_w_out, v_lru_w_in, v_lru_conv_w, v_lru_conv_b, v_lru_w_a, v_lru_b_a, v_lru_w_x, v_lru_b_x, v_lru_lambda, v_lru_w_out, v_final_g):
    nseq, seq, _ = x.shape
    t_tok = nseq * seq
    me = 4 * lax.axis_index("x") + 2 * lax.axis_index("y") + lax.axis_index("c")
    me_core_major = 4 * lax.axis_index("c") + 2 * lax.axis_index("x") + lax.axis_index("y")
    x0 = x.reshape(t_tok, D_MODEL)
    target = loss_target.reshape(t_tok, D_MODEL)

    w_in_pad = jnp.pad(attn_w_in[0].astype(BF16), ((0, 0), (0, SHARD_W_PAD - SHARD_W_IN)))
    vec_shard = jnp.concatenate([lru_conv_w[0], lru_conv_b, lru_b_a, lru_b_x, lru_lambda], axis=0)
    per_chip = _exchange(
        "gather_weights_chips",
        [w_in_pad, attn_w_out[0].astype(BF16), lru_w_in[0].astype(BF16), lru_w_out[0].astype(BF16), vec_shard, c],
        [], axes=("x", "y"))
    g_w_in, g_w_out0, g_lru_in, g_w_out1, g_vec, g_c = _exchange("gather_weights_cores", list(per_chip), [], axes=("c",))
    by_device = lambda a: jnp.transpose(a, (1, 0) + tuple(range(2, a.ndim))).reshape((N_DEV,) + a.shape[2:])
    g_w_in, g_w_out0, g_lru_in, g_w_out1, g_vec = (by_device(a) for a in (g_w_in, g_w_out0, g_lru_in, g_w_out1, g_vec))
    w_full = jnp.transpose(g_w_in[:, :, :SHARD_W_IN], (1, 0, 2)).reshape(D_MODEL, N_DEV * SHARD_W_IN)
    w_aq, w_ak, w_av = w_full[:, 0:512], w_full[:, 512:640], w_full[:, 640:768]
    w_bq, w_bk, w_bv = w_full[:, 768:1280], w_full[:, 1280:1792], w_full[:, 1792:2304]
    w_f, w_gate = w_full[:, 2304:2312], w_full[:, 2312:3336]
    w_main = jnp.concatenate([w_bq, w_bk, w_bv, w_aq, w_gate, w_ak, w_av], axis=1)
    wf_t = jnp.transpose(w_f)
    w_out0 = g_w_out0.reshape(D_MODEL, D_MODEL)
    w_out1 = g_w_out1.reshape(D_MODEL, D_MODEL)
    vec_full = jnp.transpose(g_vec, (1, 0, 2)).reshape(8, D_MODEL)
    conv_w, conv_b, b_a, b_x, lam = vec_full[0:4], vec_full[4:5], vec_full[5:6], vec_full[6:7], vec_full[7:8]
    c_all = g_c.reshape(N_DEV * nseq, D_MODEL)

    ncol = ada_w.shape[2]
    ada_b_slice = lax.dynamic_slice(ada_b.reshape(2, N_DEV, ncol), (0, me, 0), (2, 1, ncol))
    mod_part = _ada_mod(c_all, ada_w, ada_b_slice)
    (g_mod,) = _exchange("gather_mod", [mod_part], [])
    mine = lax.dynamic_slice(g_mod, (0, 0, me_core_major * nseq, 0), (N_DEV, 2, nseq, ncol))
    mod = jnp.transpose(mine, (1, 2, 0, 3)).reshape(2, nseq, 3 * D_MODEL)
    shift = [mod[l, :, 0:D_MODEL].reshape(nseq, 1, D_MODEL) for l in range(2)]
    scale = [mod[l, :, D_MODEL:2 * D_MODEL].reshape(nseq, 1, D_MODEL) for l in range(2)]
    gmod = [mod[l, :, 2 * D_MODEL:].reshape(nseq, 1, D_MODEL) for l in range(2)]

    onehot = _bucket_onehot()
    bias = _bias_expand(jnp.transpose(rel_bias), onehot).reshape(N_HEADS, BLOCK, 2 * BLOCK)
    sinks = attn_sinks.reshape(N_HEADS)
    b_f = attn_b_f.reshape(N_HEADS, 1)
    h0, qkvg, fl_t = _norm_proj("norm_proj0", x0, norm_g[0:1], shift[0], scale[0], w_main, seq, BF16, wf_t=wf_t)
    f_row, f_col = _fox_prep(fl_t, b_f, seq)
    a_out, lse_a = _swa_fwd(qkvg, bias, sinks, seq)
    q_aug, k_aug, kt_aug, vt = _fox_aug(qkvg, f_col, seq)
    b_out, lse_b = _fox_fwd_t(q_aug, k_aug, vt, seq)
    yg0, y0, x1 = _out_proj("out_proj0", [a_out, b_out], qkvg, C_GATE // D_MODEL, w_out0, x0, gmod[0], seq)

    h1, proj1 = _norm_proj("norm_proj1", x1, norm_g[1:2], shift[1], scale[1], g_lru_in, seq, F32)
    hs = _lru_fwd(proj1, conv_w, conv_b, lru_w_a[0], b_a, lru_w_x[0], b_x, lam, seq)
    yg1, y1, x2 = _out_proj("out_proj1", [hs], proj1, 1, w_out1, x1, gmod[1], seq)

    dx2, loss_rows, dfinal_rows = _final_loss(x2, final_g.reshape(1, D_MODEL), target)
    loss = lax.psum(jnp.sum(loss_rows[0]), ("x", "y", "c"))

    dy1, dgm1, dyh = _out_proj_bwd("out_proj1_bwd", dx2, gmod[1], y1, w_out1, seq)
    dproj1, dcw, dvec, dw_a, dw_x = _lru_bwd(proj1, hs, dyh, conv_w, conv_b, lru_w_a[0], b_a, lru_w_x[0], b_x, lam, seq)
    dx1, dss1, dg1 = _norm_bwd("norm1_bwd", [(dproj1, 0)], g_lru_in, x1, norm_g[1:2], scale[1], dx2, seq)
    (p_w_out1,) = _dw("dw_out1", yg1, [dy1])
    (p_lru_in,) = _dw("dw_lru_in", h1, [dproj1], blocked=2 * D_MODEL // N_DEV)

    dy0, dgm0, du_a, du_b, dgate = _out_proj_bwd("out_proj0_bwd", dx1, gmod[0], y0, w_out0, seq,
                                                  attn=(a_out, b_out, qkvg))
    dq_a, dkv_a, dbias, dsink = _swa_bwd(qkvg, du_a, a_out, lse_a, bias, sinks, seq)
    dq_b, dk_b, dv_b, df4 = _fox_bwd_t(q_aug, k_aug, kt_aug, qkvg, du_b, b_out, lse_b, seq)
    dfl_t, db_f = _fox_post(df4.reshape(N_HEADS, t_tok), fl_t, b_f, seq)
    parts0 = [(dq_b, C_BQ), (dk_b, C_BK), (dv_b, C_BV), (dq_a, C_AQ), (dgate, C_GATE), (dkv_a, C_AK)]
    dx0, dss0, dg0 = _norm_bwd("norm0_bwd", parts0, w_main, x0, norm_g[0:1], scale[0], dx1, seq,
                               rows_part=(dfl_t, wf_t))
    (p_w_out0,) = _dw("dw_out0", yg0, [dy0])
    pw_bq, pw_bk, pw_bv, pw_aq, pw_gate, pw_akv = _dw("dw_attn_in", h0, [p for p, _ in parts0])
    pw_f = _dw_rows("dw_f", dfl_t, h0)
    dbias_t = _bias_reduce(dbias.reshape(N_HEADS, BLOCK * 2 * BLOCK), onehot)

    p_w_in = jnp.concatenate([pw_aq, pw_akv, pw_bq, pw_bk, pw_bv, jnp.transpose(pw_f).astype(BF16), pw_gate], axis=1)
    p_w_in = jnp.transpose(p_w_in.reshape(D_MODEL, N_DEV // 2, 2, SHARD_W_IN), (2, 1, 0, 3))
    p_w_in = jnp.pad(p_w_in, ((0, 0), (0, 0), (0, 0), (0, SHARD_W_PAD - SHARD_W_IN)))
    rows_out = D_MODEL // N_DEV
    core_major_rows = lambda a: jnp.transpose(a.reshape(N_DEV // 2, 2, rows_out, D_MODEL), (1, 0, 2, 3))
    blocks = [p_w_in, core_major_rows(p_w_out0), p_lru_in.reshape(2, N_DEV // 2, D_MODEL, -1), core_major_rows(p_w_out1)]
    from_sibling = _exchange("grads_cores", [], blocks, axes=("c",))
    chip_sums = [_sum_leading("pair_sum%d" % k, q.reshape(2, -1, q.shape[-1]), BF16).reshape(q.shape[1:])
                 for k, q in enumerate(from_sibling)]
    small_partials = [jnp.transpose(dbias_t), jnp.stack([dg0[0], dg1[0]]), dsink[:, 0], db_f[:, 0],
                      dcw[0:4], dvec[0:4], dfinal_rows[0]]
    gpack, gmeta = _pack_rows(small_partials)
    dwax = jnp.stack([dw_a, dw_x]).astype(BF16)
    dmod = jnp.stack([jnp.concatenate([dss[:, 0], dss[:, 1], dgm[:, 0]], axis=1)
                      for dss, dgm in ((dss0, dgm0), (dss1, dgm1))], axis=1)
    c_small, c_dwax, c_dmod, r_w_in, r_w_out0, r_lru_in, r_w_out1 = _exchange(
        "grads_chips", [gpack, dwax, dmod], chip_sums, axes=("x", "y"))
    g_small, g_dwax, g_dmod = _exchange("small_cores", [c_small, c_dwax, c_dmod], [], axes=("c",))

    small_sum = _sum_leading("sum_small", g_small.reshape((N_DEV,) + gpack.shape))
    d_rel, d_norm_g, d_sinks, d_b_f, d_cw, d_vec, d_final_g = _unpack_rows(small_sum, gmeta)
    d_wax = _sum_leading("sum_dwax", g_dwax.reshape(N_DEV, 2 * LRU_BLOCKS * LRU_BLOCK_W, LRU_BLOCK_W))
    d_wa, d_wx = d_wax[:LRU_BLOCKS * LRU_BLOCK_W], d_wax[LRU_BLOCKS * LRU_BLOCK_W:]
    cols = lambda a: lax.dynamic_slice(a, (0, me * LRU_BLOCK_W), (a.shape[0], LRU_BLOCK_W))
    dmod_all = g_dmod.reshape(N_DEV * nseq, 2 * 3 * D_MODEL)
    d_ada_b = _sum_leading("sum_ada_b", dmod_all.reshape(N_DEV * nseq, 2 * 3 * D_MODEL // 128, 128)).reshape(2, 3 * D_MODEL)
    dmod_slice = lax.dynamic_slice(dmod_all.reshape(N_DEV * nseq, 2, N_DEV, ncol), (0, 0, me, 0),
                                   (N_DEV * nseq, 2, 1, ncol)).reshape(N_DEV * nseq, 2, ncol)
    d_ada_w = _ada_w_grad(c_all, jnp.transpose(dmod_slice, (1, 0, 2)))

    given = dict(
        rel_bias=(rel_bias, m_rel_bias, v_rel_bias), norm_g=(norm_g, m_norm_g, v_norm_g),
        ada_w=(ada_w, m_ada_w, v_ada_w), ada_b=(ada_b, m_ada_b, v_ada_b),
        attn_w_in=(attn_w_in, m_attn_w_in, v_attn_w_in), attn_sinks=(attn_sinks, m_attn_sinks, v_attn_sinks),
        attn_b_f=(attn_b_f, m_attn_b_f, v_attn_b_f), attn_w_out=(attn_w_out, m_attn_w_out, v_attn_w_out),
        lru_w_in=(lru_w_in, m_lru_w_in, v_lru_w_in), lru_conv_w=(lru_conv_w, m_lru_conv_w, v_lru_conv_w),
        lru_conv_b=(lru_conv_b, m_lru_conv_b, v_lru_conv_b), lru_w_a=(lru_w_a, m_lru_w_a, v_lru_w_a),
        lru_b_a=(lru_b_a, m_lru_b_a, v_lru_b_a), lru_w_x=(lru_w_x, m_lru_w_x, v_lru_w_x),
        lru_b_x=(lru_b_x, m_lru_b_x, v_lru_b_x), lru_lambda=(lru_lambda, m_lru_lambda, v_lru_lambda),
        lru_w_out=(lru_w_out, m_lru_w_out, v_lru_w_out), final_g=(final_g, m_final_g, v_final_g))
    results = {}

    def big(name, shape2d, g=None, parts=None):
        w, m, v = (a.reshape(shape2d) for a in given[name])
        outs = _adamw("adamw_" + name, w, m, v, g=g, parts=parts)
        results[name] = tuple(o.reshape(given[name][0].shape) for o in outs)

    big("ada_w", (2 * D_MODEL, ncol), g=d_ada_w.reshape(2 * D_MODEL, ncol))
    big("attn_w_in", (D_MODEL, SHARD_W_IN), parts=r_w_in)
    big("attn_w_out", (rows_out, D_MODEL), parts=r_w_out0)
    big("lru_w_in", (D_MODEL, 2 * D_MODEL // N_DEV), parts=r_lru_in)
    big("lru_w_out", (rows_out, D_MODEL), parts=r_w_out1)

    small_grads = dict(
        rel_bias=d_rel, norm_g=d_norm_g, ada_b=d_ada_b, attn_sinks=d_sinks.reshape(1, N_HEADS),
        attn_b_f=d_b_f.reshape(1, N_HEADS), lru_conv_w=cols(d_cw).reshape(1, 4, LRU_BLOCK_W),
        lru_conv_b=cols(d_vec[0:1]), lru_w_a=d_wa.reshape(lru_w_a.shape), lru_b_a=cols(d_vec[1:2]),
        lru_w_x=d_wx.reshape(lru_w_x.shape), lru_b_x=cols(d_vec[2:3]), lru_lambda=cols(d_vec[3:4]),
        final_g=d_final_g)
    small = [n for n in WEIGHTS if n not in BIG]
    wpack, smeta = _pack_rows([given[n][0] for n in small])
    mpack, _ = _pack_rows([given[n][1] for n in small])
    vpack, _ = _pack_rows([given[n][2] for n in small])
    gpack2, _ = _pack_rows([small_grads[n] for n in small])
    packs = _adamw("adamw_small", wpack, mpack, vpack, g=gpack2)
    unpacked = [_unpack_rows(p, smeta) for p in packs]
    for k, n in enumerate(small):
        results[n] = tuple(unpacked[j][k] for j in range(4))

    grad_x = dx0.reshape(x.shape)
    out = [loss, grad_x]
    for j in range(4):
        out += [results[n][j] for n in WEIGHTS]
    return tuple(out)
```

```python
import functools
import math

import jax
import jax.numpy as jnp
from jax import lax
from jax.experimental import pallas as pl
from jax.experimental.pallas import tpu as pltpu

F32 = jnp.float32
BF16 = jnp.bfloat16
HI = lax.Precision.HIGHEST
MESH = pl.DeviceIdType.MESH

N_DEV = 8
D_MODEL = 1024
HEAD_DIM = 64
N_HEADS = 8
KV_GROUP = 4
BLOCK = 128
REL_BUCKETS = 32
REL_MAX_EXACT = 16
REL_MAX_DIST = 128
LRU_BLOCKS = 8
LRU_BLOCK_W = 128
LRU_C = 8.0
EPS = 1e-6
SCALE = HEAD_DIM ** -0.5
NEG = -1e30

ADAM_LR = 0.001
ADAM_B1 = 0.9
ADAM_B2 = 0.999
ADAM_EPS = 1e-08
ADAM_WD = 0.01
ADAM_STEP = 10

C_BQ, C_BK, C_BV, C_AQ, C_GATE, C_AK, C_AV = 0, 512, 1024, 1536, 2048, 3072, 3200
N_MAIN = 3328
SHARD_W_IN = 417
SHARD_W_PAD = 512

TM = 256
TQ = 256
TK = 128
TC = 256
VMEM_BIG = 56 * 1024 * 1024
VMEM_MID = 40 * 1024 * 1024


def _pallas(body, **kw):
    return pl.pallas_call(body, **kw)


def _cp(sem=None, vmem=None):
    kw = {}
    if sem is not None:
        kw["dimension_semantics"] = sem
    if vmem is not None:
        kw["vmem_limit_bytes"] = vmem
    return pltpu.CompilerParams(**kw)


def _nn(a, b, precision=None):
    return jnp.dot(a, b, preferred_element_type=F32, precision=precision)


def _nt(a, b, precision=None):
    return lax.dot_general(a, b, (((1,), (1,)), ((), ())), preferred_element_type=F32, precision=precision)


def _tn(a, b, precision=None):
    return lax.dot_general(a, b, (((0,), (0,)), ((), ())), preferred_element_type=F32, precision=precision)


def _sigmoid(x):
    return 1.0 / (1.0 + jnp.exp(-x))


def _silu(x):
    return x * _sigmoid(x)


def _dsilu(x):
    s = _sigmoid(x)
    return s * (1.0 + x * (1.0 - s))


def _neg_expm1(x):
    poly = x * (1.0 + x * (0.5 + x * (1.0 / 6.0 + x * (1.0 / 24.0))))
    return -jnp.where(jnp.abs(x) < 0.05, poly, jnp.exp(x) - 1.0)


def _col(tile, idx):
    lane = lax.broadcasted_iota(jnp.int32, tile.shape, 1)
    return jnp.sum(jnp.where(lane == idx, tile, 0.0), axis=1, keepdims=True)


def _row(tile, idx):
    sub = lax.broadcasted_iota(jnp.int32, tile.shape, 0)
    return jnp.sum(jnp.where(sub == idx, tile, 0.0), axis=0, keepdims=True)


def _exchange(name, gathers, scatters, axes=("x", "y", "c"), chunks=1):
    ng, n = len(gathers), len(gathers) + len(scatters)
    ins = list(gathers) + list(scatters)
    group = 2 ** len(axes)

    def body(*refs):
        in_refs, out_refs = refs[:n], refs[n:2 * n]
        send_sems, recv_sems, loc_sems = refs[2 * n:]
        coord = {a: lax.axis_index(a) for a in ("x", "y", "c")}

        def member(r):
            pc = dict(coord)
            idx = 0
            for k, a in enumerate(axes):
                if r & (1 << (len(axes) - 1 - k)):
                    pc[a] = 1 - coord[a]
                idx = 2 * idx + pc[a]
            return (pc["x"], pc["y"], pc["c"]), idx

        _, me = member(0)

        def peer(r):
            return member(r)

        local, sends, recvs = [], [], []
        for k in range(n):
            mine = in_refs[k] if k < ng else in_refs[k].at[me]
            cp = pltpu.make_async_copy(mine, out_refs[k].at[me], loc_sems.at[k])
            cp.start()
            local.append(cp)
            lead = mine.shape[0]
            nchunk = max(q for q in range(1, chunks + 1) if lead % q == 0)
            step = lead // nchunk
            for r in range(1, group):
                pid, pidx = peer(r)
                src = in_refs[k] if k < ng else in_refs[k].at[pidx]
                for q in range(nchunk):
                    rows = pl.ds(q * step, step)
                    sems = dict(send_sem=send_sems.at[r - 1, k, q], recv_sem=recv_sems.at[r - 1, k, q],
                                device_id=pid, device_id_type=MESH)
                    snd = pltpu.make_async_remote_copy(src_ref=src.at[rows], dst_ref=out_refs[k].at[me].at[rows], **sems)
                    snd.start()
                    sends.append(snd)
                    recvs.append(pltpu.make_async_remote_copy(
                        src_ref=src.at[rows], dst_ref=out_refs[k].at[pidx].at[rows], **sems))
        for rc in recvs:
            rc.wait_recv()
        for snd in sends:
            snd.wait_send()
        for cp in local:
            cp.wait()

    out_shape = [jax.ShapeDtypeStruct((group,) + a.shape, a.dtype) for a in gathers]
    out_shape += [jax.ShapeDtypeStruct(a.shape, a.dtype) for a in scatters]
    any_spec = pl.BlockSpec(memory_space=pl.ANY)
    return _pallas(
        body, name=name, out_shape=out_shape,
        in_specs=[any_spec] * n, out_specs=[any_spec] * n,
        scratch_shapes=[pltpu.SemaphoreType.DMA((group - 1, n, chunks)), pltpu.SemaphoreType.DMA((group - 1, n, chunks)),
                        pltpu.SemaphoreType.DMA((n,))],
    )(*ins)


def _peer_of(r):
    x, y, c = lax.axis_index("x"), lax.axis_index("y"), lax.axis_index("c")
    px = 1 - x if r & 4 else x
    py = 1 - y if r & 2 else y
    pc = 1 - c if r & 1 else c
    return (px, py, pc), 4 * px + 2 * py + pc


def _split_copies(in_refs, land_refs, send_sems, recv_sems, ng):
    _, me = _peer_of(0)
    pairs = []
    for k, (src_ref, land) in enumerate(zip(in_refs, land_refs)):
        for r in range(1, N_DEV):
            pid, pidx = _peer_of(r)
            src = src_ref if k < ng else src_ref.at[pidx]
            slot = (N_DEV - 1) * k + r - 1
            sems = dict(send_sem=send_sems.at[slot], recv_sem=recv_sems.at[slot], device_id=pid, device_id_type=MESH)
            pairs.append((pltpu.make_async_remote_copy(src_ref=src, dst_ref=land.at[me], **sems),
                          pltpu.make_async_remote_copy(src_ref=src, dst_ref=land.at[pidx], **sems)))
    return pairs


def _exchange_start(name, gathers, scatters, after):
    ng, n = len(gathers), len(gathers) + len(scatters)
    ins = list(gathers) + list(scatters)
    lands = [jax.ShapeDtypeStruct((N_DEV,) + a.shape, a.dtype) for a in gathers]
    lands += [jax.ShapeDtypeStruct(a.shape, a.dtype) for a in scatters]

    def body(*refs):
        in_refs, land_refs = refs[:n], refs[n:2 * n]
        send_sems, recv_sems = refs[2 * n + 1:2 * n + 3]
        token = refs[-1]
        for send, _ in _split_copies(in_refs, land_refs, send_sems, recv_sems, ng):
            send.start()
        token[...] = jnp.zeros_like(token)

    hbm = pl.BlockSpec(memory_space=pltpu.HBM)
    sem = pl.BlockSpec(memory_space=pltpu.SEMAPHORE)
    sem_shape = pltpu.SemaphoreType.DMA(((N_DEV - 1) * n,))
    out_shape = [sem_shape, sem_shape] + [pltpu.HBM(a.shape, a.dtype) for a in ins]
    out_shape += [pltpu.HBM(l.shape, l.dtype) for l in lands] + [jax.ShapeDtypeStruct((8, 128), F32)]
    args = [pltpu.with_memory_space_constraint(a, pltpu.HBM) for a in ins]
    args += [pltpu.with_memory_space_constraint(lax.empty(l.shape, l.dtype), pltpu.HBM) for l in lands]
    outs = _pallas(
        body, name=name, out_shape=out_shape,
        in_specs=[hbm] * (2 * n) + [pl.BlockSpec(memory_space=pl.ANY)],
        out_specs=[sem, sem] + [hbm] * (2 * n) + [pl.BlockSpec(memory_space=pltpu.VMEM)],
        input_output_aliases={i: 2 + i for i in range(2 * n)},
        compiler_params=pltpu.CompilerParams(has_side_effects=pltpu.SideEffectType.DATAFLOW_SIDE_EFFECTING),
    )(*args, after)
    return (outs[0], outs[1], list(outs[2:2 + n]), list(outs[2 + n:2 + 2 * n]), ng), outs[-1]


def _exchange_wait(name, handle, after):
    send_sems, recv_sems, srcs, lands, ng = handle
    n = len(srcs)

    def body(*refs):
        in_refs, land_refs = refs[:n], refs[n:2 * n]
        send_ref, recv_ref = refs[2 * n:2 * n + 2]
        for send, recv in _split_copies(in_refs, land_refs, send_ref, recv_ref, ng):
            send.wait_send()
            recv.wait_recv()

    hbm = pl.BlockSpec(memory_space=pltpu.HBM)
    sem = pl.BlockSpec(memory_space=pltpu.SEMAPHORE)
    outs = _pallas(
        body, name=name, out_shape=[pltpu.HBM(a.shape, a.dtype) for a in srcs + lands],
        in_specs=[hbm] * (2 * n) + [sem, sem, pl.BlockSpec(memory_space=pl.ANY)],
        out_specs=[hbm] * (2 * n), input_output_aliases={i: i for i in range(2 * n)},
        compiler_params=pltpu.CompilerParams(has_side_effects=pltpu.SideEffectType.DATAFLOW_SIDE_EFFECTING),
    )(*srcs, *lands, send_sems, recv_sems, after)
    return list(outs[n:])


def _with_own(land, own, me):
    return lax.dynamic_update_slice(land, own[None], (me,) + (0,) * own.ndim)


def _ada_mod(c_all, ada_w, ada_b_slice):
    def body(c_ref, w_ref, b_ref, o_ref):
        ca = _silu(c_ref[...])
        for l in range(2):
            o_ref[l] = _nn(ca, w_ref[l], HI) + b_ref[l]

    return _pallas(body, name="ada_mod",
                   out_shape=jax.ShapeDtypeStruct((2, c_all.shape[0], ada_w.shape[2]), F32),
                   compiler_params=_cp(vmem=VMEM_MID))(c_all, ada_w, ada_b_slice)


def _ada_w_grad(c_all, dmod_slice):
    def body(c_ref, d_ref, o_ref):
        ca = _silu(c_ref[...])
        for l in range(2):
            o_ref[l] = _tn(ca, d_ref[l], HI)

    return _pallas(body, name="ada_w_grad",
                   out_shape=jax.ShapeDtypeStruct((2, D_MODEL, dmod_slice.shape[2]), F32),
                   compiler_params=_cp(vmem=VMEM_MID))(c_all, dmod_slice)


def _bucket_onehot():
    qi = jnp.arange(BLOCK)[:, None]
    kj = jnp.arange(2 * BLOCK)[None, :]
    rel = qi - kj + BLOCK
    n = jnp.maximum(rel, 0)
    nf = jnp.maximum(n, 1).astype(F32)
    large = REL_MAX_EXACT + (jnp.log(nf / REL_MAX_EXACT) / math.log(REL_MAX_DIST / REL_MAX_EXACT)
                             * (REL_BUCKETS - REL_MAX_EXACT)).astype(jnp.int32)
    large = jnp.minimum(large, REL_BUCKETS - 1)
    bucket = jnp.where(n < REL_MAX_EXACT, n, large).reshape(1, BLOCK * 2 * BLOCK)
    return (jnp.arange(REL_BUCKETS)[:, None] == bucket).astype(F32)


def _bias_expand(rel_bias_t, onehot):
    def body(r_ref, e_ref, o_ref):
        o_ref[...] = _nn(r_ref[...], e_ref[...], HI)

    return _pallas(body, name="bias_expand",
                   out_shape=jax.ShapeDtypeStruct((N_HEADS, onehot.shape[1]), F32),
                   compiler_params=_cp(vmem=VMEM_MID))(rel_bias_t, onehot)


def _bias_reduce(dbias, onehot):
    def body(d_ref, e_ref, o_ref):
        o_ref[...] = _nt(d_ref[...], e_ref[...], HI)

    return _pallas(body, name="bias_reduce",
                   out_shape=jax.ShapeDtypeStruct((N_HEADS, REL_BUCKETS), F32),
                   compiler_params=_cp(vmem=VMEM_MID))(dbias, onehot)


def _norm_proj(name, x, g, shift, scale, w, seq, out_dtype, wf_t=None):
    t_tok = x.shape[0]
    w3d = w.ndim == 3
    n_out = w.shape[0] * w.shape[2] if w3d else w.shape[1]
    cn = w.shape[2] if w3d else 256

    def body(x_ref, g_ref, sh_ref, sc_ref, w_ref, *rest):
        if wf_t is not None:
            wf_ref, h_ref, o_ref, fl_ref = rest
        else:
            h_ref, o_ref = rest
        xv = x_ref[...]
        rstd = lax.rsqrt(jnp.mean(xv * xv, axis=-1, keepdims=True) + EPS)
        h = (xv * rstd) * g_ref[...] * (1.0 + sc_ref[...]) + sh_ref[...]
        hb = h.astype(BF16)
        h_ref[...] = hb
        for j in range(n_out // cn):
            wj = w_ref[j] if w3d else w_ref[:, j * cn:(j + 1) * cn]
            o_ref[:, j * cn:(j + 1) * cn] = _nn(hb, wj).astype(out_dtype)
        if wf_t is not None:
            fl_ref[...] = _nt(wf_ref[...], hb)

    mod_spec = pl.BlockSpec((None, 1, D_MODEL), lambda i: (i * TM // seq, 0, 0))
    w_spec = (pl.BlockSpec(w.shape, lambda i: (0, 0, 0)) if w3d else pl.BlockSpec(w.shape, lambda i: (0, 0)))
    in_specs = [pl.BlockSpec((TM, D_MODEL), lambda i: (i, 0)), pl.BlockSpec((1, D_MODEL), lambda i: (0, 0)),
                mod_spec, mod_spec, w_spec]
    out_shape = [jax.ShapeDtypeStruct((t_tok, D_MODEL), BF16), jax.ShapeDtypeStruct((t_tok, n_out), out_dtype)]
    out_specs = [pl.BlockSpec((TM, D_MODEL), lambda i: (i, 0)), pl.BlockSpec((TM, n_out), lambda i: (i, 0))]
    args = [x, g, shift, scale, w]
    if wf_t is not None:
        in_specs.append(pl.BlockSpec(wf_t.shape, lambda i: (0, 0)))
        out_shape.append(jax.ShapeDtypeStruct((wf_t.shape[0], t_tok), F32))
        out_specs.append(pl.BlockSpec((wf_t.shape[0], TM), lambda i: (0, i)))
        args.append(wf_t)
    return _pallas(body, name=name, grid=(t_tok // TM,), in_specs=in_specs, out_specs=out_specs,
                   out_shape=out_shape, compiler_params=_cp(("arbitrary",), VMEM_BIG))(*args)


def _fox_prep(fl_t, b_f, seq):
    t_tok = fl_t.shape[1]
    ch = 256

    def body(fl_ref, bf_ref, fr_ref, fc_ref):
        z = fl_ref[...] + bf_ref[...]
        logf = jnp.minimum(z, 0.0) - jnp.log(1.0 + jnp.exp(-jnp.abs(z)))
        ri = lax.broadcasted_iota(jnp.int32, (ch, ch), 0)
        ci = lax.broadcasted_iota(jnp.int32, (ch, ch), 1)
        upper = (ri <= ci).astype(F32)
        eye = (ri == ci).astype(F32)
        carry = jnp.zeros((N_HEADS, 1), F32)
        for k in range(seq // ch):
            fk = _nn(logf[:, k * ch:(k + 1) * ch], upper, HI) + carry
            carry = fk[:, ch - 1:ch]
            fr_ref[:, k * ch:(k + 1) * ch] = fk
            padded = jnp.concatenate([fk, jnp.zeros((128 - N_HEADS, ch), F32)], axis=0)
            fc_ref[k * ch:(k + 1) * ch, :] = _nt(eye, padded, HI)

    return _pallas(
        body, name="fox_prep", grid=(t_tok // seq,),
        in_specs=[pl.BlockSpec((N_HEADS, seq), lambda b: (0, b)), pl.BlockSpec((N_HEADS, 1), lambda b: (0, 0))],
        out_specs=[pl.BlockSpec((N_HEADS, seq), lambda b: (0, b)), pl.BlockSpec((seq, 128), lambda b: (b, 0))],
        out_shape=[jax.ShapeDtypeStruct((N_HEADS, t_tok), F32), jax.ShapeDtypeStruct((t_tok, 128), F32)],
        compiler_params=_cp(("arbitrary",), VMEM_MID))(fl_t, b_f)


def _fox_post(df_row, fl_t, b_f, seq):
    t_tok = fl_t.shape[1]
    ch = 256

    def body(d_ref, fl_ref, bf_ref, o_ref, db_ref):
        @pl.when(pl.program_id(0) == 0)
        def _():
            db_ref[...] = jnp.zeros_like(db_ref)

        z = fl_ref[...] + bf_ref[...]
        sig_neg = 1.0 / (1.0 + jnp.exp(z))
        ri = lax.broadcasted_iota(jnp.int32, (ch, ch), 0)
        ci = lax.broadcasted_iota(jnp.int32, (ch, ch), 1)
        lower = (ri >= ci).astype(F32)
        carry = jnp.zeros((N_HEADS, 1), F32)
        tot = jnp.zeros((N_HEADS, 1), F32)
        for k in reversed(range(seq // ch)):
            dk = _nn(d_ref[:, k * ch:(k + 1) * ch], lower, HI) + carry
            carry = dk[:, 0:1]
            dfl = dk * sig_neg[:, k * ch:(k + 1) * ch]
            o_ref[:, k * ch:(k + 1) * ch] = dfl
            tot = tot + jnp.sum(dfl, axis=1, keepdims=True)
        db_ref[...] += jnp.broadcast_to(tot, db_ref.shape)

    return _pallas(
        body, name="fox_post", grid=(t_tok // seq,),
        in_specs=[pl.BlockSpec((N_HEADS, seq), lambda b: (0, b)), pl.BlockSpec((N_HEADS, seq), lambda b: (0, b)),
                  pl.BlockSpec((N_HEADS, 1), lambda b: (0, 0))],
        out_specs=[pl.BlockSpec((N_HEADS, seq), lambda b: (0, b)), pl.BlockSpec((N_HEADS, 128), lambda b: (0, 0))],
        out_shape=[jax.ShapeDtypeStruct((N_HEADS, t_tok), F32), jax.ShapeDtypeStruct((N_HEADS, 128), F32)],
        compiler_params=_cp(("arbitrary",), VMEM_MID))(df_row, fl_t, b_f)


def _eye(n, dtype):
    return (lax.broadcasted_iota(jnp.int32, (n, n), 0) == lax.broadcasted_iota(jnp.int32, (n, n), 1)).astype(dtype)


def _fox_aug(qkvg, f_col, seq):
    t_tok = qkvg.shape[0]
    ta = 256
    nkb = ta // TK

    def body(q_ref, k_ref, v_ref, fc_ref, qa_ref, ka_ref, kt_ref, vt_ref):
        ri = lax.broadcasted_iota(jnp.int32, (128, 128), 0)
        ci = lax.broadcasted_iota(jnp.int32, (128, 128), 1)
        eye = (ri == ci).astype(BF16)
        lane = lax.broadcasted_iota(jnp.int32, (ta, 128), 1)
        ones_q = jnp.where(jnp.logical_and(lane >= 64, lane < 67), 1.0, 0.0)
        ones_k = jnp.where(jnp.logical_and(lane >= 67, lane < 70), 1.0, 0.0)
        fc_tile = fc_ref[...]
        for p in range(N_HEADS // 2):
            q2 = q_ref[:, 128 * p:128 * (p + 1)]
            k2 = k_ref[:, 128 * p:128 * (p + 1)]
            vt = _nt(eye, v_ref[:, 128 * p:128 * (p + 1)]).astype(BF16)
            for kk in range(nkb):
                vt_ref[p, kk] = vt[:, kk * TK:(kk + 1) * TK]
            for e in range(2):
                h = 2 * p + e
                sel = jnp.logical_and(ri == ci + HEAD_DIM * e, ci < HEAD_DIM)
                f = _col(fc_tile, h)
                fh = f.astype(BF16).astype(F32)
                fm = (f - fh).astype(BF16).astype(F32)
                fl = (f - fh - fm).astype(BF16).astype(F32)
                qa = (_nn(q2, jnp.where(sel, SCALE, 0.0).astype(BF16)) + ones_q + jnp.where(lane == 67, fh, 0.0)
                      + jnp.where(lane == 68, fm, 0.0) + jnp.where(lane == 69, fl, 0.0))
                ka = (_nn(k2, jnp.where(sel, 1.0, 0.0).astype(BF16)) + ones_k - jnp.where(lane == 64, fh, 0.0)
                      - jnp.where(lane == 65, fm, 0.0) - jnp.where(lane == 66, fl, 0.0))
                qa_ref[h] = qa.astype(BF16)
                kab = ka.astype(BF16)
                ka_ref[h] = kab
                kt = _nt(eye, kab).astype(BF16)
                for kk in range(nkb):
                    kt_ref[h, kk] = kt[:, kk * TK:(kk + 1) * TK]

    aug = jax.ShapeDtypeStruct((N_HEADS, t_tok, 128), BF16)
    return _pallas(
        body, name="fox_aug", grid=(t_tok // ta,),
        in_specs=[pl.BlockSpec((ta, 512), lambda i: (i, C_BQ // 512)), pl.BlockSpec((ta, 512), lambda i: (i, C_BK // 512)),
                  pl.BlockSpec((ta, 512), lambda i: (i, C_BV // 512)), pl.BlockSpec((ta, 128), lambda i: (i, 0))],
        out_specs=[pl.BlockSpec((N_HEADS, ta, 128), lambda i: (0, i, 0)), pl.BlockSpec((N_HEADS, ta, 128), lambda i: (0, i, 0)),
                   pl.BlockSpec((N_HEADS, nkb, 128, TK), lambda i: (0, i, 0, 0)),
                   pl.BlockSpec((N_HEADS // 2, nkb, 128, TK), lambda i: (0, i, 0, 0))],
        out_shape=[aug, aug, jax.ShapeDtypeStruct((N_HEADS, t_tok // TK, 128, TK), BF16),
                   jax.ShapeDtypeStruct((N_HEADS // 2, t_tok // TK, 128, TK), BF16)],
        compiler_params=_cp(("arbitrary",), VMEM_MID))(qkvg, qkvg, qkvg, f_col)


def _fox_fwd_t(q_aug, k_aug, vt, seq):
    t_tok = q_aug.shape[1]
    nq = seq // TQ
    ratio = TQ // TK

    def body(qa_ref, ka_ref, vt_ref, o_ref, lse_ref, ml_s, acc_s, st_s, p_s, al_s):
        i = pl.program_id(1)
        tpos = i * TQ + lax.broadcasted_iota(jnp.int32, (1, TQ), 1)
        eye = _eye(HEAD_DIM, BF16)
        for h in range(N_HEADS):
            ml_s[0, h] = jnp.full((1, TQ), NEG, F32)
            ml_s[1, h] = jnp.zeros((1, TQ), F32)
            acc_s[h] = jnp.zeros((HEAD_DIM, TQ), F32)
            p_s[1, h] = jnp.zeros((TK, TQ), BF16)
            al_s[1, h] = jnp.ones((1, TQ), F32)

        def scores(j):
            row0 = pl.multiple_of(j * TK, TK)
            for h in range(N_HEADS):
                st_s[j & 1, h] = _nt(ka_ref[h, pl.ds(row0, TK), :], qa_ref[h])

        def softmax(j, masked):
            slot = j & 1
            if masked:
                keep = (j * TK + lax.broadcasted_iota(jnp.int32, (TK, 1), 0)) <= tpos
            for h in range(N_HEADS):
                st = st_s[slot, h]
                if masked:
                    st = jnp.where(keep, st, NEG)
                m = ml_s[0, h]
                m_new = jnp.maximum(m, jnp.max(st, axis=0, keepdims=True))
                alpha = jnp.exp(m - m_new)
                pe = jnp.exp(st - m_new)
                ml_s[0, h] = m_new
                ml_s[1, h] = alpha * ml_s[1, h] + jnp.sum(pe, axis=0, keepdims=True)
                al_s[slot, h] = alpha
                p_s[slot, h] = pe.astype(BF16)

        def values(j):
            slot = j & 1
            jv = jnp.maximum(j, 0)
            for h in range(N_HEADS):
                p, e = divmod(h, 2)
                acc_s[h] = al_s[slot, h] * acc_s[h] + _nn(vt_ref[p, jv, e * HEAD_DIM:(e + 1) * HEAD_DIM, :], p_s[slot, h])

        def step(j, carry):
            values(j - 1)
            softmax(j, False)
            scores(j + 1)
            return carry

        last = ratio * i + ratio - 1
        scores(0)
        lax.fori_loop(0, ratio * i, step, 0)
        for kk in range(ratio):
            j = ratio * i + kk
            values(j - 1)
            softmax(j, True)
            if kk < ratio - 1:
                scores(j + 1)
        values(last)
        for p in range(N_HEADS // 2):
            outs = []
            for e in range(2):
                h = 2 * p + e
                l = ml_s[1, h]
                outs.append(_tn((acc_s[h] / l).astype(BF16), eye))
                lse_ref[p, e:e + 1, :] = ml_s[0, h] + jnp.log(l)
            o_ref[:, 128 * p:128 * (p + 1)] = jnp.concatenate(outs, axis=1).astype(BF16)

    return _pallas(
        body, name="fox_fwd", grid=(t_tok // seq, nq),
        in_specs=[pl.BlockSpec((N_HEADS, TQ, 128), lambda b, i: (0, b * nq + i, 0)),
                  pl.BlockSpec((N_HEADS, seq, 128), lambda b, i: (0, b, 0)),
                  pl.BlockSpec((N_HEADS // 2, seq // TK, 128, TK), lambda b, i: (0, b, 0, 0))],
        out_specs=[pl.BlockSpec((TQ, 512), lambda b, i: (b * nq + i, 0)),
                   pl.BlockSpec((N_HEADS // 2, 2, TQ), lambda b, i: (0, 0, b * nq + i))],
        out_shape=[jax.ShapeDtypeStruct((t_tok, 512), BF16), jax.ShapeDtypeStruct((N_HEADS // 2, 2, t_tok), F32)],
        scratch_shapes=[pltpu.VMEM((2, N_HEADS, 1, TQ), F32), pltpu.VMEM((N_HEADS, HEAD_DIM, TQ), F32),
                        pltpu.VMEM((2, N_HEADS, TK, TQ), F32), pltpu.VMEM((2, N_HEADS, TK, TQ), BF16),
                        pltpu.VMEM((2, N_HEADS, 1, TQ), F32)],
        compiler_params=_cp(("arbitrary", "arbitrary"), VMEM_MID))(q_aug, k_aug, vt)


def _fox_bwd_t(q_aug, k_aug, kt, qkvg, du_b, b_out, lse, seq):
    t_tok = qkvg.shape[0]
    nq = seq // TQ
    nkb = seq // TK
    ratio = TQ // TK
    hg = 4

    def body(qa_ref, ka_ref, kt_ref, v_ref, do_ref, o_ref, lse_ref, dq_ref, dk_ref, dv_ref, df_ref,
             dqt_s, row_s, dfk_s, dk_s, dv_s, dfa_s, st_s, dp_s, pb_s, db_s):
        ones_b = jnp.ones((8, TQ), BF16)
        eye = _eye(HEAD_DIM, BF16)
        lane8 = lax.broadcasted_iota(jnp.int32, (8, 128), 1)
        lane_k = lax.broadcasted_iota(jnp.int32, (TK, 128), 1)
        first = [lane8 < HEAD_DIM, lane8 >= HEAD_DIM]
        for hh in range(hg):
            pp, e = divmod(hh, 2)
            head_lanes = jnp.where(first[e], 1.0, 0.0)
            for ii in range(nq):
                rows = slice(ii * TQ, (ii + 1) * TQ)
                prod = do_ref[rows, 128 * pp:128 * (pp + 1)].astype(F32) * o_ref[rows, 128 * pp:128 * (pp + 1)].astype(F32)
                row_s[hh, ii, 0] = _nt(head_lanes, prod, HI)
                row_s[hh, ii, 1] = jnp.broadcast_to(lse_ref[pp, e:e + 1, ii * TQ:(ii + 1) * TQ], (8, TQ))
                dqt_s[hh, ii] = jnp.zeros((128, TQ), F32)

        def kblock(j, _):
            krow = pl.multiple_of(j * TK, TK)
            spos = j * TK + lax.broadcasted_iota(jnp.int32, (TK, 1), 0)
            for hh in range(hg):
                dk_s[hh] = jnp.zeros((TK, 128), F32)
                dv_s[hh] = jnp.zeros((TK, 128), F32)
                dfa_s[hh] = jnp.zeros((8, TK), F32)

            def scores(i):
                qrow = pl.multiple_of(i * TQ, TQ)
                for hh in range(hg):
                    pp, e = divmod(hh, 2)
                    own = (lane_k < HEAD_DIM) if e == 0 else (lane_k >= HEAD_DIM)
                    v2 = v_ref[pl.ds(krow, TK), 128 * pp:128 * (pp + 1)]
                    vj = jnp.where(own, v2, jnp.zeros_like(v2))
                    st_s[i & 1, hh] = _nt(ka_ref[hh, pl.ds(krow, TK), :], qa_ref[hh, pl.ds(qrow, TQ), :])
                    dp_s[i & 1, hh] = _nt(vj, do_ref[pl.ds(qrow, TQ), 128 * pp:128 * (pp + 1)])

            def elementwise(i, masked):
                slot = i & 1
                if masked:
                    keep = spos <= (i * TQ + lax.broadcasted_iota(jnp.int32, (1, TQ), 1))
                for hh in range(hg):
                    pt = jnp.exp(st_s[slot, hh] - row_s[hh, i, 1][0:1, :])
                    if masked:
                        pt = jnp.where(keep, pt, 0.0)
                    dst = pt * (dp_s[slot, hh] - row_s[hh, i, 0][0:1, :])
                    pb_s[slot, hh] = pt.astype(BF16)
                    db_s[slot, hh] = dst.astype(BF16)

            def grads(i):
                slot = i & 1
                qrow = pl.multiple_of(i * TQ, TQ)
                for hh in range(hg):
                    pp = hh // 2
                    dst_b = db_s[slot, hh]
                    dv_s[hh] += _nn(pb_s[slot, hh], do_ref[pl.ds(qrow, TQ), 128 * pp:128 * (pp + 1)])
                    dk_s[hh] += _nn(dst_b, qa_ref[hh, pl.ds(qrow, TQ), :])
                    dqt_s[hh, i] += _nn(kt_ref[hh, j], dst_b)
                    dfa_s[hh] += _nt(ones_b, dst_b)

            def step(i, carry):
                grads(i - 1)
                elementwise(i, False)
                scores(jnp.minimum(i + 1, nq - 1))
                return carry

            i0 = j // ratio
            scores(i0)
            elementwise(i0, True)
            scores(jnp.minimum(i0 + 1, nq - 1))
            lax.fori_loop(i0 + 1, nq, step, 0)
            grads(nq - 1)
            for pp in range(hg // 2):
                cols = slice(128 * pp, 128 * (pp + 1))
                dk_ref[pl.ds(krow, TK), cols] = jnp.concatenate(
                    [dk_s[2 * pp][:, :HEAD_DIM], dk_s[2 * pp + 1][:, :HEAD_DIM]], axis=1).astype(BF16)
                dv_ref[pl.ds(krow, TK), cols] = jnp.where(lane_k < HEAD_DIM, dv_s[2 * pp], dv_s[2 * pp + 1]).astype(BF16)
            for hh in range(hg):
                dfk_s[hh, j] = dfa_s[hh]
            return 0

        lax.fori_loop(0, nkb, kblock, 0)
        for pp in range(hg // 2):
            for ii in range(nq):
                parts = []
                for e in range(2):
                    dqt = dqt_s[2 * pp + e, ii]
                    parts.append(_tn(dqt[0:HEAD_DIM, :].astype(BF16), eye) * SCALE)
                    for kk in range(ratio):
                        jj = ii * ratio + kk
                        df_ref[pp, e:e + 1, jj * TK:(jj + 1) * TK] = (dqt[67:68, kk * TK:(kk + 1) * TK]
                                                                     - dfk_s[2 * pp + e, jj][0:1, :])
                dq_ref[ii * TQ:(ii + 1) * TQ, 128 * pp:128 * (pp + 1)] = jnp.concatenate(parts, axis=1).astype(BF16)

    aug_blk = pl.BlockSpec((hg, seq, 128), lambda b, g: (g, b, 0))
    pair_blk = pl.BlockSpec((seq, 64 * hg), lambda b, g: (b, g))
    row_blk = pl.BlockSpec((hg // 2, 2, seq), lambda b, g: (g, 0, b))
    return _pallas(
        body, name="fox_bwd", grid=(t_tok // seq, N_HEADS // hg),
        in_specs=[aug_blk, aug_blk, pl.BlockSpec((hg, nkb, 128, TK), lambda b, g: (g, b, 0, 0)),
                  pl.BlockSpec((seq, 64 * hg), lambda b, g: (b, C_BV // (64 * hg) + g)), pair_blk, pair_blk, row_blk],
        out_specs=[pair_blk, pair_blk, pair_blk, row_blk],
        out_shape=[jax.ShapeDtypeStruct((t_tok, 512), BF16)] * 3
        + [jax.ShapeDtypeStruct((N_HEADS // 2, 2, t_tok), F32)],
        scratch_shapes=[pltpu.VMEM((hg, nq, 128, TQ), F32), pltpu.VMEM((hg, nq, 2, 8, TQ), F32),
                        pltpu.VMEM((hg, nkb, 8, TK), F32), pltpu.VMEM((hg, TK, 128), F32),
                        pltpu.VMEM((hg, TK, 128), F32), pltpu.VMEM((hg, 8, TK), F32),
                        pltpu.VMEM((2, hg, TK, TQ), F32), pltpu.VMEM((2, hg, TK, TQ), F32),
                        pltpu.VMEM((2, hg, TK, TQ), BF16), pltpu.VMEM((2, hg, TK, TQ), BF16)],
        compiler_params=_cp(("arbitrary", "arbitrary"), VMEM_BIG))(q_aug, k_aug, kt, qkvg, du_b, b_out, lse)


def _fox_bwd_t_old(q_aug, k_aug, kt, qkvg, du_b, b_out, lse, seq):
    t_tok = qkvg.shape[0]
    nq = seq // TQ
    nkb = seq // TK
    ratio = TQ // TK

    def body(qa_ref, ka_ref, kt_ref, v_ref, do_ref, o_ref, lse_ref, dq_ref, dk_ref, dv_ref, df_ref,
             dqt_s, out_s, row_s, dfk_s):
        ones_b = jnp.ones((8, TQ), BF16)
        ones_f = jnp.ones((8, HEAD_DIM), F32)
        eye = _eye(HEAD_DIM, BF16)
        for e in range(2):
            lo, hi = e * HEAD_DIM, (e + 1) * HEAD_DIM
            for ii in range(nq):
                rows = slice(ii * TQ, (ii + 1) * TQ)
                do = do_ref[rows, :][:, lo:hi].astype(F32)
                ov = o_ref[rows, :][:, lo:hi].astype(F32)
                row_s[ii, 0] = _nt(ones_f, do * ov, HI)
                row_s[ii, 1] = jnp.broadcast_to(lse_ref[e:e + 1, ii * TQ:(ii + 1) * TQ], (8, TQ))
                dqt_s[ii] = jnp.zeros((128, TQ), F32)

            def kblock(j, _):
                krow = pl.multiple_of(j * TK, TK)
                kj = ka_ref[e, pl.ds(krow, TK), :]
                ktj = kt_ref[e, j]
                vj = v_ref[pl.ds(krow, TK), :][:, lo:hi]
                spos = j * TK + lax.broadcasted_iota(jnp.int32, (TK, 1), 0)

                def qblock(i, carry, masked):
                    dk_acc, dv_acc, dfk = carry
                    qrow = pl.multiple_of(i * TQ, TQ)
                    qa = qa_ref[e, pl.ds(qrow, TQ), :]
                    doh = do_ref[pl.ds(qrow, TQ), :][:, lo:hi]
                    pt = jnp.exp(_nt(kj, qa) - row_s[i, 1][0:1, :])
                    if masked:
                        tpos = i * TQ + lax.broadcasted_iota(jnp.int32, (1, TQ), 1)
                        pt = jnp.where(spos <= tpos, pt, 0.0)
                    dst = pt * (_nt(vj, doh) - row_s[i, 0][0:1, :])
                    dst_b = dst.astype(BF16)
                    dv_acc = dv_acc + _nn(pt.astype(BF16), doh)
                    dk_acc = dk_acc + _nn(dst_b, qa)
                    dqt_s[i] += _nn(ktj, dst_b)
                    dfk = dfk + _nt(ones_b, dst_b)
                    return dk_acc, dv_acc, dfk

                i0 = j // ratio
                carry = (jnp.zeros((TK, 128), F32), jnp.zeros((TK, HEAD_DIM), F32), jnp.zeros((8, TK), F32))
                carry = qblock(i0, carry, True)
                dk_acc, dv_acc, dfk = lax.fori_loop(i0 + 1, nq, functools.partial(qblock, masked=False), carry)
                out_s[1, e, pl.ds(krow, TK), :] = dk_acc[:, :HEAD_DIM]
                out_s[2, e, pl.ds(krow, TK), :] = dv_acc
                dfk_s[j] = dfk
                return 0

            lax.fori_loop(0, nkb, kblock, 0)
            for ii in range(nq):
                dqt = dqt_s[ii]
                out_s[0, e, ii * TQ:(ii + 1) * TQ, :] = _tn(dqt[0:HEAD_DIM, :].astype(BF16), eye) * SCALE
                for kk in range(ratio):
                    jj = ii * ratio + kk
                    df_ref[e:e + 1, jj * TK:(jj + 1) * TK] = dqt[67:68, kk * TK:(kk + 1) * TK] - dfk_s[jj][0:1, :]
        for k, ref in enumerate((dq_ref, dk_ref, dv_ref)):
            ref[...] = jnp.concatenate([out_s[k, 0], out_s[k, 1]], axis=1).astype(BF16)

    aug_blk = pl.BlockSpec((2, seq, 128), lambda b, p: (p, b, 0))
    pair_blk = pl.BlockSpec((seq, 128), lambda b, p: (b, p))
    row_blk = pl.BlockSpec((None, 2, seq), lambda b, p: (p, 0, b))
    return _pallas(
        body, name="fox_bwd", grid=(t_tok // seq, N_HEADS // 2),
        in_specs=[aug_blk, aug_blk, pl.BlockSpec((2, nkb, 128, TK), lambda b, p: (p, b, 0, 0)),
                  pl.BlockSpec((seq, 128), lambda b, p: (b, C_BV // 128 + p)), pair_blk, pair_blk, row_blk],
        out_specs=[pair_blk, pair_blk, pair_blk, row_blk],
        out_shape=[jax.ShapeDtypeStruct((t_tok, 512), BF16)] * 3
        + [jax.ShapeDtypeStruct((N_HEADS // 2, 2, t_tok), F32)],
        scratch_shapes=[pltpu.VMEM((nq, 128, TQ), F32), pltpu.VMEM((3, 2, seq, HEAD_DIM), F32),
                        pltpu.VMEM((nq, 2, 8, TQ), F32), pltpu.VMEM((nkb, 8, TK), F32)],
        compiler_params=_cp(("arbitrary", "arbitrary"), VMEM_BIG))(q_aug, k_aug, kt, qkvg, du_b, b_out, lse)


def _fox_fwd(qkvg, f_row, f_col, seq):
    t_tok = qkvg.shape[0]
    nq = seq // TQ

    def body(q_ref, k_ref, v_ref, fr_ref, fc_ref, o_ref, lse_ref, fk_s):
        i = pl.program_id(1)
        for jj in range(nq):
            fk_s[jj] = fr_ref[:, jj * TQ:(jj + 1) * TQ]
        fcol = fc_ref[...]
        tpos = i * TQ + lax.broadcasted_iota(jnp.int32, (TQ, 1), 0)
        lane = lax.broadcasted_iota(jnp.int32, (TQ, 128), 1)
        lse_tile = jnp.zeros((TQ, 128), F32)
        for p in range(N_HEADS // 2):
            q2 = q_ref[:, 128 * p:128 * (p + 1)]
            qs = [q2[:, :HEAD_DIM], q2[:, HEAD_DIM:]]
            fqs = [_col(fcol, 2 * p + e) for e in range(2)]

            def kblock(j, carry):
                row0 = pl.multiple_of(j * TQ, TQ)
                k2 = k_ref[pl.ds(row0, TQ), 128 * p:128 * (p + 1)]
                v2 = v_ref[pl.ds(row0, TQ), 128 * p:128 * (p + 1)]
                fk8 = fk_s[j]
                spos = j * TQ + lax.broadcasted_iota(jnp.int32, (1, TQ), 1)
                keep = spos <= tpos
                new = []
                for e in range(2):
                    m, l, acc = carry[3 * e:3 * e + 3]
                    kh = k2[:, e * HEAD_DIM:(e + 1) * HEAD_DIM]
                    vh = v2[:, e * HEAD_DIM:(e + 1) * HEAD_DIM]
                    s = _nt(qs[e], kh) * SCALE + (fqs[e] - fk8[2 * p + e:2 * p + e + 1, :])
                    s = jnp.where(keep, s, NEG)
                    m_new = jnp.maximum(m, jnp.max(s, axis=1, keepdims=True))
                    alpha = jnp.exp(m - m_new)
                    pe = jnp.exp(s - m_new)
                    l = alpha * l + jnp.sum(pe, axis=1, keepdims=True)
                    acc = alpha * acc + _nn(pe.astype(BF16), vh)
                    new += [m_new, l, acc]
                return tuple(new)

            init = (jnp.full((TQ, 1), NEG, F32), jnp.zeros((TQ, 1), F32), jnp.zeros((TQ, HEAD_DIM), F32)) * 2
            res = lax.fori_loop(0, i + 1, kblock, init)
            outs = []
            for e in range(2):
                m, l, acc = res[3 * e:3 * e + 3]
                outs.append(acc / l)
                lse_tile = jnp.where(lane == 2 * p + e, m + jnp.log(l), lse_tile)
            o_ref[:, 128 * p:128 * (p + 1)] = jnp.concatenate(outs, axis=1).astype(BF16)
        lse_ref[...] = lse_tile

    return _pallas(
        body, name="fox_fwd", grid=(t_tok // seq, nq),
        in_specs=[pl.BlockSpec((TQ, 512), lambda b, i: (b * nq + i, C_BQ // 512)),
                  pl.BlockSpec((seq, 512), lambda b, i: (b, C_BK // 512)),
                  pl.BlockSpec((seq, 512), lambda b, i: (b, C_BV // 512)),
                  pl.BlockSpec((N_HEADS, seq), lambda b, i: (0, b)),
                  pl.BlockSpec((TQ, 128), lambda b, i: (b * nq + i, 0))],
        out_specs=[pl.BlockSpec((TQ, 512), lambda b, i: (b * nq + i, 0)),
                   pl.BlockSpec((TQ, 128), lambda b, i: (b * nq + i, 0))],
        out_shape=[jax.ShapeDtypeStruct((t_tok, 512), BF16), jax.ShapeDtypeStruct((t_tok, 128), F32)],
        scratch_shapes=[pltpu.VMEM((nq, N_HEADS, TQ), F32)],
        compiler_params=_cp(("arbitrary", "arbitrary"), VMEM_MID))(qkvg, qkvg, qkvg, f_row, f_col)


def _fox_bwd(qkvg, du_b, b_out, lse, f_row, f_col, seq):
    t_tok = qkvg.shape[0]
    nq = seq // TQ

    def body(q_ref, k_ref, v_ref, do_ref, o_ref, lse_ref, fr_ref, fc_ref,
             dq_ref, dk_ref, dv_ref, df_ref, dq_s, dk_s, dv_s, col_s, df_s, fk_s):
        p = pl.program_id(1)
        for jj in range(nq):
            fk_s[jj] = fr_ref[:, jj * TQ:(jj + 1) * TQ]
        eye = (lax.broadcasted_iota(jnp.int32, (TQ, TQ), 0) == lax.broadcasted_iota(jnp.int32, (TQ, TQ), 1)).astype(F32)
        for e in range(2):
            h = 2 * p + e
            lo, hi = e * HEAD_DIM, (e + 1) * HEAD_DIM
            for ii in range(nq):
                rows = slice(ii * TQ, (ii + 1) * TQ)
                do = do_ref[rows, :][:, lo:hi].astype(F32)
                ov = o_ref[rows, :][:, lo:hi].astype(F32)
                col_s[0, rows, :] = jnp.sum(do * ov, axis=1, keepdims=True)
                col_s[1, rows, :] = _col(lse_ref[rows, :], h)
                col_s[2, rows, :] = _col(fc_ref[rows, :], h)
                dq_s[rows, :] = jnp.zeros((TQ, HEAD_DIM), F32)
                df_s[ii] = jnp.zeros((8, TQ), F32)
                col_s[3, rows, :] = jnp.zeros((TQ, 1), F32)

            def kblock(j, _):
                krow = pl.multiple_of(j * TQ, TQ)
                kh = k_ref[pl.ds(krow, TQ), :][:, lo:hi]
                vh = v_ref[pl.ds(krow, TQ), :][:, lo:hi]
                fk = _row(fk_s[j], h)
                spos = j * TQ + lax.broadcasted_iota(jnp.int32, (1, TQ), 1)

                def qblock(i, carry):
                    dk_acc, dv_acc, dfk = carry
                    qrow = pl.multiple_of(i * TQ, TQ)
                    qh = q_ref[pl.ds(qrow, TQ), :][:, lo:hi]
                    doh = do_ref[pl.ds(qrow, TQ), :][:, lo:hi]
                    delta = col_s[0, pl.ds(qrow, TQ), :]
                    lse_q = col_s[1, pl.ds(qrow, TQ), :]
                    fq = col_s[2, pl.ds(qrow, TQ), :]
                    tpos = i * TQ + lax.broadcasted_iota(jnp.int32, (TQ, 1), 0)
                    s = _nt(qh, kh) * SCALE + (fq - fk)
                    pr = jnp.where(spos <= tpos, jnp.exp(s - lse_q), 0.0)
                    dp = _nt(doh, vh)
                    ds = pr * (dp - delta)
                    ds_b = ds.astype(BF16)
                    dv_acc = dv_acc + _tn(pr.astype(BF16), doh)
                    dk_acc = dk_acc + _tn(ds_b, qh)
                    dq_s[pl.ds(qrow, TQ), :] += _nn(ds_b, kh)
                    col_s[3, pl.ds(qrow, TQ), :] += jnp.sum(ds, axis=1, keepdims=True)
                    dfk = dfk + jnp.sum(ds, axis=0, keepdims=True)
                    return dk_acc, dv_acc, dfk

                zero = jnp.zeros((TQ, HEAD_DIM), F32)
                dk_acc, dv_acc, dfk = lax.fori_loop(j, nq, qblock, (zero, zero, jnp.zeros((1, TQ), F32)))
                dk_s[e, pl.ds(krow, TQ), :] = dk_acc * SCALE
                dv_s[e, pl.ds(krow, TQ), :] = dv_acc
                df_s[j] -= jnp.broadcast_to(dfk, (8, TQ))
                return 0

            lax.fori_loop(0, nq, kblock, 0)
            dq_s2 = dq_s[...] * SCALE
            dk_s[2 + e] = dq_s2
            for ii in range(nq):
                dfq = jnp.broadcast_to(col_s[3, ii * TQ:(ii + 1) * TQ, :], (TQ, 128))
                df_ref[e:e + 1, ii * TQ:(ii + 1) * TQ] = _tn(dfq, eye, HI)[0:1, :] + df_s[ii][0:1, :]
        dq_ref[...] = jnp.concatenate([dk_s[2], dk_s[3]], axis=1).astype(BF16)
        dk_ref[...] = jnp.concatenate([dk_s[0], dk_s[1]], axis=1).astype(BF16)
        dv_ref[...] = jnp.concatenate([dv_s[0], dv_s[1]], axis=1).astype(BF16)

    blk = lambda off: pl.BlockSpec((seq, 128), lambda b, p: (b, off // 128 + p))
    out_blk = pl.BlockSpec((seq, 128), lambda b, p: (b, p))
    return _pallas(
        body, name="fox_bwd", grid=(t_tok // seq, N_HEADS // 2),
        in_specs=[blk(C_BQ), blk(C_BK), blk(C_BV), out_blk, out_blk,
                  pl.BlockSpec((seq, 128), lambda b, p: (b, 0)),
                  pl.BlockSpec((N_HEADS, seq), lambda b, p: (0, b)),
                  pl.BlockSpec((seq, 128), lambda b, p: (b, 0))],
        out_specs=[out_blk, out_blk, out_blk, pl.BlockSpec((None, 2, seq), lambda b, p: (p, 0, b))],
        out_shape=[jax.ShapeDtypeStruct((t_tok, 512), BF16)] * 3
        + [jax.ShapeDtypeStruct((N_HEADS // 2, 2, t_tok), F32)],
        scratch_shapes=[pltpu.VMEM((seq, HEAD_DIM), F32), pltpu.VMEM((4, seq, HEAD_DIM), F32),
                        pltpu.VMEM((2, seq, HEAD_DIM), F32), pltpu.VMEM((4, seq, 1), F32),
                        pltpu.VMEM((nq, 8, TQ), F32), pltpu.VMEM((nq, N_HEADS, TQ), F32)],
        compiler_params=_cp(("arbitrary", "arbitrary"), VMEM_BIG))(qkvg, qkvg, qkvg, du_b, b_out, lse, f_row, f_col)


def _swa_window(k_ref, v_ref, n):
    prev = pl.multiple_of(jnp.maximum(n - 1, 0) * BLOCK, BLOCK)
    cur = pl.multiple_of(n * BLOCK, BLOCK)
    kwin = jnp.concatenate([k_ref[pl.ds(prev, BLOCK), :], k_ref[pl.ds(cur, BLOCK), :]], axis=0)
    vwin = jnp.concatenate([v_ref[pl.ds(prev, BLOCK), :], v_ref[pl.ds(cur, BLOCK), :]], axis=0)
    ti = lax.broadcasted_iota(jnp.int32, (BLOCK, 2 * BLOCK), 0)
    sj = lax.broadcasted_iota(jnp.int32, (BLOCK, 2 * BLOCK), 1)
    rel = ti - sj + BLOCK
    first_key = jnp.where(n > 0, 0, BLOCK)
    mask = jnp.logical_and(jnp.logical_and(rel >= 0, rel < BLOCK), sj >= first_key)
    return kwin, vwin, mask, prev, cur


def _head_cols(ref, h):
    pair = ref[:, 128 * (h // 2):128 * (h // 2 + 1)]
    return pair[:, (h % 2) * HEAD_DIM:(h % 2 + 1) * HEAD_DIM]


def _swa_logits(q_ref, kwin, bias_ref, h, mask):
    hk = h // KV_GROUP
    s = _nt(_head_cols(q_ref, h), kwin[:, hk * HEAD_DIM:(hk + 1) * HEAD_DIM]) * SCALE + bias_ref[h]
    return jnp.where(mask, s, NEG)


def _swa_fwd(qkvg, bias, sinks, seq):
    t_tok = qkvg.shape[0]
    nb = seq // BLOCK

    def body(sink_ref, q_ref, k_ref, v_ref, bias_ref, o_ref, lse_ref, s_s, p_s, den_s):
        n = pl.program_id(1)
        kwin, vwin, mask, _, _ = _swa_window(k_ref, v_ref, n)
        for h in range(N_HEADS):
            s_s[h] = _swa_logits(q_ref, kwin, bias_ref, h, mask)
        lane = lax.broadcasted_iota(jnp.int32, (BLOCK, 128), 1)
        lse_tile = jnp.zeros((BLOCK, 128), F32)
        for h in range(N_HEADS):
            s = s_s[h]
            sink = sink_ref[h]
            m = jnp.maximum(jnp.max(s, axis=1, keepdims=True), sink)
            pe = jnp.exp(s - m)
            den = jnp.sum(pe, axis=1, keepdims=True) + jnp.exp(sink - m)
            p_s[h] = pe.astype(BF16)
            den_s[h] = den
            lse_tile = jnp.where(lane == h, m + jnp.log(den), lse_tile)
        lse_ref[...] = lse_tile
        for pr in range(N_HEADS // 2):
            outs = []
            for h in (2 * pr, 2 * pr + 1):
                hk = h // KV_GROUP
                outs.append(_nn(p_s[h], vwin[:, hk * HEAD_DIM:(hk + 1) * HEAD_DIM]) / den_s[h])
            o_ref[:, 128 * pr:128 * (pr + 1)] = jnp.concatenate(outs, axis=1).astype(BF16)

    return _pallas(
        body, name="swa_fwd", grid=(t_tok // seq, nb),
        in_specs=[pl.BlockSpec(memory_space=pltpu.SMEM),
                  pl.BlockSpec((BLOCK, 512), lambda b, n: (b * nb + n, C_AQ // 512)),
                  pl.BlockSpec((seq, 128), lambda b, n: (b, C_AK // 128)),
                  pl.BlockSpec((seq, 128), lambda b, n: (b, C_AV // 128)),
                  pl.BlockSpec((N_HEADS, BLOCK, 2 * BLOCK), lambda b, n: (0, 0, 0))],
        out_specs=[pl.BlockSpec((BLOCK, 512), lambda b, n: (b * nb + n, 0)),
                   pl.BlockSpec((BLOCK, 128), lambda b, n: (b * nb + n, 0))],
        out_shape=[jax.ShapeDtypeStruct((t_tok, 512), BF16), jax.ShapeDtypeStruct((t_tok, 128), F32)],
        scratch_shapes=[pltpu.VMEM((N_HEADS, BLOCK, 2 * BLOCK), F32), pltpu.VMEM((N_HEADS, BLOCK, 2 * BLOCK), BF16),
                        pltpu.VMEM((N_HEADS, BLOCK, 1), F32)],
        compiler_params=_cp(("arbitrary", "arbitrary"), VMEM_MID))(sinks, qkvg, qkvg, qkvg, bias)


def _swa_bwd(qkvg, du_a, a_out, lse, bias, sinks, seq):
    t_tok = qkvg.shape[0]
    nb = seq // BLOCK

    def body(sink_ref, q_ref, k_ref, v_ref, do_ref, o_ref, lse_ref, bias_ref,
             dq_ref, dkv_ref, dbias_ref, dsink_ref, kv_s, s_s, dp_s, pb_s, db_s):
        b, n = pl.program_id(0), pl.program_id(1)

        @pl.when(jnp.logical_and(b == 0, n == 0))
        def _():
            dbias_ref[...] = jnp.zeros_like(dbias_ref)
            dsink_ref[...] = jnp.zeros_like(dsink_ref)

        @pl.when(n == 0)
        def _():
            kv_s[...] = jnp.zeros_like(kv_s)

        kwin, vwin, mask, prev, cur = _swa_window(k_ref, v_ref, n)
        for h in range(N_HEADS):
            hk = h // KV_GROUP
            s_s[h] = _swa_logits(q_ref, kwin, bias_ref, h, mask)
            dp_s[h] = _nt(_head_cols(do_ref, h), vwin[:, hk * HEAD_DIM:(hk + 1) * HEAD_DIM])
        lse_tile = lse_ref[...]
        for h in range(N_HEADS):
            delta = jnp.sum(_head_cols(do_ref, h).astype(F32) * _head_cols(o_ref, h).astype(F32), axis=1, keepdims=True)
            lse_h = _col(lse_tile, h)
            pe = jnp.exp(s_s[h] - lse_h)
            ds = pe * (dp_s[h] - delta)
            dbias_ref[h] += ds
            psink = jnp.exp(sink_ref[h] - lse_h)
            dsink_ref[h:h + 1, :] += jnp.broadcast_to(jnp.sum(-psink * delta, axis=0, keepdims=True), (1, 128))
            pb_s[h] = pe.astype(BF16)
            db_s[h] = ds.astype(BF16)
        for pr in range(N_HEADS // 2):
            dqs = []
            for h in (2 * pr, 2 * pr + 1):
                hk = h // KV_GROUP
                dqs.append(_nn(db_s[h], kwin[:, hk * HEAD_DIM:(hk + 1) * HEAD_DIM]) * SCALE)
            dq_ref[:, 128 * pr:128 * (pr + 1)] = jnp.concatenate(dqs, axis=1).astype(BF16)
        dks, dvs = [], []
        for hk in range(N_HEADS // KV_GROUP):
            dk = jnp.zeros((2 * BLOCK, HEAD_DIM), F32)
            dv = jnp.zeros((2 * BLOCK, HEAD_DIM), F32)
            for h in range(hk * KV_GROUP, (hk + 1) * KV_GROUP):
                dk = dk + _tn(db_s[h], _head_cols(q_ref, h))
                dv = dv + _tn(pb_s[h], _head_cols(do_ref, h))
            dks.append(dk * SCALE)
            dvs.append(dv)
        upd = jnp.concatenate(dks + dvs, axis=1)
        kv_s[pl.ds(prev, BLOCK), :] += upd[:BLOCK]
        kv_s[pl.ds(cur, BLOCK), :] += upd[BLOCK:]

        @pl.when(n == nb - 1)
        def _():
            dkv_ref[...] = kv_s[...].astype(BF16)

    return _pallas(
        body, name="swa_bwd", grid=(t_tok // seq, nb),
        in_specs=[pl.BlockSpec(memory_space=pltpu.SMEM),
                  pl.BlockSpec((BLOCK, 512), lambda b, n: (b * nb + n, C_AQ // 512)),
                  pl.BlockSpec((seq, 128), lambda b, n: (b, C_AK // 128)),
                  pl.BlockSpec((seq, 128), lambda b, n: (b, C_AV // 128)),
                  pl.BlockSpec((BLOCK, 512), lambda b, n: (b * nb + n, 0)),
                  pl.BlockSpec((BLOCK, 512), lambda b, n: (b * nb + n, 0)),
                  pl.BlockSpec((BLOCK, 128), lambda b, n: (b * nb + n, 0)),
                  pl.BlockSpec((N_HEADS, BLOCK, 2 * BLOCK), lambda b, n: (0, 0, 0))],
        out_specs=[pl.BlockSpec((BLOCK, 512), lambda b, n: (b * nb + n, 0)),
                   pl.BlockSpec((seq, 256), lambda b, n: (b, 0)),
                   pl.BlockSpec((N_HEADS, BLOCK, 2 * BLOCK), lambda b, n: (0, 0, 0)),
                   pl.BlockSpec((N_HEADS, 128), lambda b, n: (0, 0))],
        out_shape=[jax.ShapeDtypeStruct((t_tok, 512), BF16), jax.ShapeDtypeStruct((t_tok, 256), BF16),
                   jax.ShapeDtypeStruct((N_HEADS, BLOCK, 2 * BLOCK), F32), jax.ShapeDtypeStruct((N_HEADS, 128), F32)],
        scratch_shapes=[pltpu.VMEM((seq, 256), F32),
                        pltpu.VMEM((N_HEADS, BLOCK, 2 * BLOCK), F32), pltpu.VMEM((N_HEADS, BLOCK, 2 * BLOCK), F32),
                        pltpu.VMEM((N_HEADS, BLOCK, 2 * BLOCK), BF16), pltpu.VMEM((N_HEADS, BLOCK, 2 * BLOCK), BF16)],
        compiler_params=_cp(("arbitrary", "arbitrary"), VMEM_MID))(sinks, qkvg, qkvg, qkvg, du_a, a_out, lse, bias)


def _out_proj(name, u_parts, gate_arr, gate_blk, w_out, x, gmod, seq):
    t_tok = x.shape[0]
    nu = len(u_parts)

    def body(*refs):
        u_refs = refs[:nu]
        g_ref, w_ref, x_ref, gm_ref, yg_ref, y_ref, xn_ref = refs[nu:]
        u = jnp.concatenate([r[...].astype(F32) for r in u_refs], axis=1) if nu > 1 else u_refs[0][...].astype(F32)
        yg = (u * _silu(g_ref[...].astype(F32))).astype(BF16)
        yg_ref[...] = yg
        y = _nn(yg, w_ref[...])
        y_ref[...] = y.astype(BF16)
        xn_ref[...] = x_ref[...] + gm_ref[...] * y

    row = lambda w: pl.BlockSpec((TM, w), lambda i: (i, 0))
    in_specs = [row(u.shape[1]) for u in u_parts]
    in_specs += [pl.BlockSpec((TM, D_MODEL), lambda i: (i, gate_blk)),
                 pl.BlockSpec((D_MODEL, D_MODEL), lambda i: (0, 0)), row(D_MODEL),
                 pl.BlockSpec((None, 1, D_MODEL), lambda i: (i * TM // seq, 0, 0))]
    return _pallas(
        body, name=name, grid=(t_tok // TM,), in_specs=in_specs,
        out_specs=[row(D_MODEL)] * 3,
        out_shape=[jax.ShapeDtypeStruct((t_tok, D_MODEL), BF16)] * 2 + [jax.ShapeDtypeStruct((t_tok, D_MODEL), F32)],
        compiler_params=_cp(("arbitrary",), VMEM_MID))(*u_parts, gate_arr, w_out, x, gmod)


def _out_proj_bwd(name, dxn, gmod, y, w_out, seq, attn=None):
    t_tok = dxn.shape[0]
    tiles_per_seq = seq // TM

    def body(*refs):
        if attn is None:
            dxn_ref, gm_ref, y_ref, w_ref, dy_ref, dgm_ref, dyg_ref = refs
        else:
            dxn_ref, gm_ref, y_ref, w_ref, a_ref, b_ref, g_ref, dy_ref, dgm_ref, dua_ref, dub_ref, dg_ref = refs
        i = pl.program_id(0)
        dxv = dxn_ref[...]
        dy = (dxv * gm_ref[...]).astype(BF16)
        dy_ref[...] = dy

        @pl.when(i % tiles_per_seq == 0)
        def _():
            dgm_ref[...] = jnp.zeros_like(dgm_ref)

        dgm_ref[...] += jnp.sum(dxv * y_ref[...].astype(F32), axis=0, keepdims=True)
        dyg = _nt(dy, w_ref[...])
        if attn is None:
            dyg_ref[...] = dyg
        else:
            gt = g_ref[...].astype(F32)
            du = dyg * _silu(gt)
            dua_ref[...] = du[:, :512].astype(BF16)
            dub_ref[...] = du[:, 512:].astype(BF16)
            u = jnp.concatenate([a_ref[...].astype(F32), b_ref[...].astype(F32)], axis=1)
            dg_ref[...] = (dyg * u * _dsilu(gt)).astype(BF16)

    row = lambda w: pl.BlockSpec((TM, w), lambda i: (i, 0))
    mod_spec = pl.BlockSpec((None, 1, D_MODEL), lambda i: (i * TM // seq, 0, 0))
    in_specs = [row(D_MODEL), mod_spec, row(D_MODEL), pl.BlockSpec((D_MODEL, D_MODEL), lambda i: (0, 0))]
    out_specs = [row(D_MODEL), mod_spec]
    out_shape = [jax.ShapeDtypeStruct((t_tok, D_MODEL), BF16), jax.ShapeDtypeStruct(gmod.shape, F32)]
    args = [dxn, gmod, y, w_out]
    if attn is None:
        out_specs.append(row(D_MODEL))
        out_shape.append(jax.ShapeDtypeStruct((t_tok, D_MODEL), F32))
    else:
        in_specs += [row(512), row(512), pl.BlockSpec((TM, D_MODEL), lambda i: (i, C_GATE // D_MODEL))]
        out_specs += [row(512), row(512), row(D_MODEL)]
        out_shape += [jax.ShapeDtypeStruct((t_tok, 512), BF16)] * 2 + [jax.ShapeDtypeStruct((t_tok, D_MODEL), BF16)]
        args += list(attn)
    return _pallas(body, name=name, grid=(t_tok // TM,), in_specs=in_specs, out_specs=out_specs,
                   out_shape=out_shape, compiler_params=_cp(("arbitrary",), VMEM_MID))(*args)


def _norm_bwd(name, parts, w, x, g, scale, dxn, seq, rows_part=None):
    t_tok = x.shape[0]
    npart = len(parts)
    w3d = w.ndim == 3
    tiles_per_seq = seq // TM
    nrow_in = 0 if rows_part is None else 2

    def body(*refs):
        p_refs = refs[:npart]
        w_ref, x_ref, g_ref, sc_ref, dxn_ref = refs[npart:npart + 5]
        dx_ref, dss_ref, dg_ref = refs[npart + 5 + nrow_in:]
        i = pl.program_id(0)
        dh = jnp.zeros((TM, D_MODEL), F32)
        if rows_part is not None:
            r_ref, wr_ref = refs[npart + 5:npart + 7]
            dh = dh + _tn(r_ref[...].astype(BF16), wr_ref[...])
        for (arr, off), p_ref in zip(parts, p_refs):
            width = arr.shape[1]
            for j in range(width // 256):
                pj = p_ref[:, j * 256:(j + 1) * 256]
                c0 = off + j * 256
                wj = w_ref[c0 // 256] if w3d else w_ref[:, c0:c0 + 256]
                dh = dh + _nt(pj, wj)
        xv = x_ref[...]
        rstd = lax.rsqrt(jnp.mean(xv * xv, axis=-1, keepdims=True) + EPS)
        xhat = xv * rstd
        gv = g_ref[...]
        nrm = xhat * gv

        @pl.when(i % tiles_per_seq == 0)
        def _():
            dss_ref[...] = jnp.zeros_like(dss_ref)

        @pl.when(i == 0)
        def _():
            dg_ref[...] = jnp.zeros_like(dg_ref)

        dss_ref[0:1, :] += jnp.sum(dh, axis=0, keepdims=True)
        dss_ref[1:2, :] += jnp.sum(dh * nrm, axis=0, keepdims=True)
        dn = dh * (1.0 + sc_ref[...])
        dg_ref[0:1, :] += jnp.sum(dn * xhat, axis=0, keepdims=True)
        dxhat = dn * gv
        dx_ref[...] = rstd * (dxhat - xhat * jnp.mean(dxhat * xhat, axis=-1, keepdims=True)) + dxn_ref[...]

    row = lambda wd: pl.BlockSpec((TM, wd), lambda i: (i, 0))
    w_spec = (pl.BlockSpec(w.shape, lambda i: (0, 0, 0)) if w3d else pl.BlockSpec(w.shape, lambda i: (0, 0)))
    in_specs = [row(a.shape[1]) for a, _ in parts]
    in_specs += [w_spec, row(D_MODEL), pl.BlockSpec((1, D_MODEL), lambda i: (0, 0)),
                 pl.BlockSpec((None, 1, D_MODEL), lambda i: (i * TM // seq, 0, 0)), row(D_MODEL)]
    args = [a for a, _ in parts] + [w, x, g, scale, dxn]
    if rows_part is not None:
        in_specs += [pl.BlockSpec((8, TM), lambda i: (0, i)), pl.BlockSpec((8, D_MODEL), lambda i: (0, 0))]
        args += list(rows_part)
    nseq = t_tok // seq
    return _pallas(
        body, name=name, grid=(t_tok // TM,), in_specs=in_specs,
        out_specs=[row(D_MODEL), pl.BlockSpec((None, 8, D_MODEL), lambda i: (i * TM // seq, 0, 0)),
                   pl.BlockSpec((8, D_MODEL), lambda i: (0, 0))],
        out_shape=[jax.ShapeDtypeStruct((t_tok, D_MODEL), F32), jax.ShapeDtypeStruct((nseq, 8, D_MODEL), F32),
                   jax.ShapeDtypeStruct((8, D_MODEL), F32)],
        compiler_params=_cp(("arbitrary",), VMEM_BIG))(*args)


def _dw(name, a, parts, blocked=None):
    t_tok, ka = a.shape
    tt = 512
    npart = len(parts)
    nt = t_tok // tt

    def body(*refs):
        a_ref = refs[0]
        p_refs = refs[1:1 + npart]
        o_refs = refs[1 + npart:1 + 2 * npart]
        acc_refs = refs[1 + 2 * npart:]
        t = pl.program_id(0)
        av = a_ref[...]
        for p_ref, acc in zip(p_refs, acc_refs):
            upd = _tn(av, p_ref[...])

            @pl.when(t == 0)
            def _():
                acc[...] = upd

            @pl.when(t > 0)
            def _():
                acc[...] += upd

        @pl.when(t == nt - 1)
        def _():
            for o_ref, acc in zip(o_refs, acc_refs):
                if blocked is None:
                    o_ref[...] = acc[...].astype(BF16)
                else:
                    for j in range(o_ref.shape[0]):
                        o_ref[j] = acc[:, j * blocked:(j + 1) * blocked].astype(BF16)

    in_specs = [pl.BlockSpec((tt, ka), lambda t: (t, 0))]
    in_specs += [pl.BlockSpec((tt, p.shape[1]), lambda t: (t, 0)) for p in parts]
    if blocked is None:
        out_shape = [jax.ShapeDtypeStruct((ka, p.shape[1]), BF16) for p in parts]
        out_specs = [pl.BlockSpec((ka, p.shape[1]), lambda t: (0, 0)) for p in parts]
    else:
        out_shape = [jax.ShapeDtypeStruct((p.shape[1] // blocked, ka, blocked), BF16) for p in parts]
        out_specs = [pl.BlockSpec((p.shape[1] // blocked, ka, blocked), lambda t: (0, 0, 0)) for p in parts]
    return _pallas(body, name=name, grid=(nt,), in_specs=in_specs, out_specs=out_specs, out_shape=out_shape,
                   scratch_shapes=[pltpu.VMEM((ka, p.shape[1]), F32) for p in parts],
                   compiler_params=_cp(("arbitrary",), VMEM_BIG))(a, *parts)


def _dw_rows(name, rows_t, h):
    t_tok = h.shape[0]
    tt = 512

    def body(r_ref, h_ref, o_ref):
        @pl.when(pl.program_id(0) == 0)
        def _():
            o_ref[...] = jnp.zeros_like(o_ref)

        o_ref[...] += _nn(r_ref[...].astype(BF16), h_ref[...])

    return _pallas(body, name=name, grid=(t_tok // tt,),
                   in_specs=[pl.BlockSpec((8, tt), lambda t: (0, t)), pl.BlockSpec((tt, D_MODEL), lambda t: (t, 0))],
                   out_specs=pl.BlockSpec((8, D_MODEL), lambda t: (0, 0)),
                   out_shape=jax.ShapeDtypeStruct((8, D_MODEL), F32),
                   compiler_params=_cp(("arbitrary",), VMEM_MID))(rows_t, h)


def _lru_gates(xc, blk, wa_ref, wx_ref, ba_ref, bx_ref, sp):
    cols = slice(blk * LRU_BLOCK_W, (blk + 1) * LRU_BLOCK_W)
    xb = xc[:, cols].astype(BF16)
    r = _sigmoid(_nn(xb, wa_ref[blk].astype(BF16)) + ba_ref[:, cols])
    ig = _sigmoid(_nn(xb, wx_ref[blk].astype(BF16)) + bx_ref[:, cols])
    log_a = -LRU_C * r * sp[:, cols]
    a = jnp.exp(log_a)
    mult = jnp.sqrt(_neg_expm1(2.0 * log_a))
    return xb, r, ig, a, mult


def _softplus_neg(lam):
    return jnp.maximum(-lam, 0.0) + jnp.log(1.0 + jnp.exp(-jnp.abs(lam)))


def _conv_taps(xe_ref, cw_ref, cb_ref):
    xc = cb_ref[...] + xe_ref[8:8 + TC, :] * cw_ref[3:4, :]
    for k in range(1, 4):
        xc = xc + xe_ref[8 - k:8 - k + TC, :] * cw_ref[3 - k:4 - k, :]
    return xc


def _lru_fwd(proj, cw, cb, w_a, b_a, w_x, b_x, lam, seq):
    t_tok = proj.shape[0]
    nc = seq // TC

    def body(x_ref, cw_ref, cb_ref, wa_ref, ba_ref, wx_ref, bx_ref, lam_ref, hs_ref, xe_s, a_s, u_s, h_s):
        c = pl.program_id(1)

        @pl.when(c == 0)
        def _():
            xe_s[0:8, :] = jnp.zeros((8, D_MODEL), F32)
            h_s[...] = jnp.zeros_like(h_s)

        xe_s[8:8 + TC, :] = x_ref[...]
        xc = _conv_taps(xe_s, cw_ref, cb_ref)
        sp = _softplus_neg(lam_ref[...])
        for blk in range(LRU_BLOCKS):
            cols = slice(blk * LRU_BLOCK_W, (blk + 1) * LRU_BLOCK_W)
            _, _, ig, a, mult = _lru_gates(xc, blk, wa_ref, wx_ref, ba_ref, bx_ref, sp)
            a_s[:, cols] = a
            u_s[:, cols] = mult * ig * xc[:, cols]

        def step(t, h):
            h = a_s[pl.ds(t, 1), :] * h + u_s[pl.ds(t, 1), :]
            hs_ref[pl.ds(t, 1), :] = h
            return h

        h_s[0:1, :] = lax.fori_loop(0, TC, step, h_s[0:1, :], unroll=8)
        xe_s[0:8, :] = xe_s[TC:TC + 8, :]

    full = lambda shape: pl.BlockSpec(shape, lambda b, c: (0,) * len(shape))
    return _pallas(
        body, name="lru_fwd", grid=(t_tok // seq, nc),
        in_specs=[pl.BlockSpec((TC, D_MODEL), lambda b, c: (b * nc + c, 0)), full((4, D_MODEL)), full((1, D_MODEL)),
                  full((LRU_BLOCKS, LRU_BLOCK_W, LRU_BLOCK_W)), full((1, D_MODEL)),
                  full((LRU_BLOCKS, LRU_BLOCK_W, LRU_BLOCK_W)), full((1, D_MODEL)), full((1, D_MODEL))],
        out_specs=pl.BlockSpec((TC, D_MODEL), lambda b, c: (b * nc + c, 0)),
        out_shape=jax.ShapeDtypeStruct((t_tok, D_MODEL), F32),
        scratch_shapes=[pltpu.VMEM((TC + 8, D_MODEL), F32), pltpu.VMEM((TC, D_MODEL), F32),
                        pltpu.VMEM((TC, D_MODEL), F32), pltpu.VMEM((8, D_MODEL), F32)],
        compiler_params=_cp(("arbitrary", "arbitrary"), VMEM_MID))(proj, cw, cb, w_a, b_a, w_x, b_x, lam)


def _lru_bwd(proj, hs, dyh, cw, cb, w_a, b_a, w_x, b_x, lam, seq):
    t_tok = proj.shape[0]
    nc = seq // TC

    def body(x_ref, xh_ref, g_ref, hs_ref, hh_ref, dy_ref, cw_ref, cb_ref, wa_ref, ba_ref, wx_ref, bx_ref, lam_ref,
             dp_ref, dcw_ref, dvec_ref, dwa_ref, dwx_ref,
             xe_s, he_s, de_s, a_s, r_s, i_s, m_s, dh_s, carry_s):
        b, cr = pl.program_id(0), pl.program_id(1)
        c = nc - 1 - cr

        @pl.when(jnp.logical_and(b == 0, cr == 0))
        def _():
            dcw_ref[...] = jnp.zeros_like(dcw_ref)
            dvec_ref[...] = jnp.zeros_like(dvec_ref)
            dwa_ref[...] = jnp.zeros_like(dwa_ref)
            dwx_ref[...] = jnp.zeros_like(dwx_ref)

        @pl.when(cr == 0)
        def _():
            carry_s[...] = jnp.zeros_like(carry_s)
            de_s[TC:TC + 8, :] = jnp.zeros((8, D_MODEL), F32)

        first = c == 0
        xe_s[0:8, :] = jnp.where(first, 0.0, xh_ref[...])
        xe_s[8:8 + TC, :] = x_ref[...]
        he_s[0:8, :] = jnp.where(first, 0.0, hh_ref[...])
        he_s[8:8 + TC, :] = hs_ref[...]
        xc = _conv_taps(xe_s, cw_ref, cb_ref)
        lam_v = lam_ref[...]
        sp = _softplus_neg(lam_v)
        for blk in range(LRU_BLOCKS):
            cols = slice(blk * LRU_BLOCK_W, (blk + 1) * LRU_BLOCK_W)
            _, r, ig, a, mult = _lru_gates(xc, blk, wa_ref, wx_ref, ba_ref, bx_ref, sp)
            a_s[:, cols], r_s[:, cols], i_s[:, cols], m_s[:, cols] = a, r, ig, mult

        gt = g_ref[...]
        dyh = dy_ref[...]
        dh_s[...] = dyh * _silu(gt)
        dp_ref[:, D_MODEL:] = (dyh * hs_ref[...] * _dsilu(gt)).astype(BF16)

        def step(k, carry):
            t = TC - 1 - k
            dh = dh_s[pl.ds(t, 1), :] + carry
            dh_s[pl.ds(t, 1), :] = dh
            return a_s[pl.ds(t, 1), :] * dh

        carry_s[0:1, :] = lax.fori_loop(0, TC, step, carry_s[0:1, :], unroll=8)

        hprev = he_s[7:7 + TC, :]
        for blk in range(LRU_BLOCKS):
            cols = slice(blk * LRU_BLOCK_W, (blk + 1) * LRU_BLOCK_W)
            xcb = xc[:, cols]
            a, r, ig, mult, dh = a_s[:, cols], r_s[:, cols], i_s[:, cols], m_s[:, cols], dh_s[:, cols]
            spb = sp[:, cols]
            dmult = dh * ig * xcb
            di = dh * mult * xcb
            dxc = dh * mult * ig
            dla = dh * hprev[:, cols] * a - dmult * (a * a) / jnp.maximum(mult, 1e-20)
            dr = dla * (-LRU_C * spb)
            dsp = jnp.sum(dla * (-LRU_C * r), axis=0, keepdims=True)
            dga = dr * r * (1.0 - r)
            dgx = di * ig * (1.0 - ig)
            dga_b, dgx_b = dga.astype(BF16), dgx.astype(BF16)
            xb = xcb.astype(BF16)
            dxc = dxc + _nt(dga_b, wa_ref[blk].astype(BF16)) + _nt(dgx_b, wx_ref[blk].astype(BF16))
            dwa_ref[blk] += _tn(xb, dga_b)
            dwx_ref[blk] += _tn(xb, dgx_b)
            dvec_ref[1:2, cols] += jnp.sum(dga, axis=0, keepdims=True)
            dvec_ref[2:3, cols] += jnp.sum(dgx, axis=0, keepdims=True)
            dvec_ref[3:4, cols] += dsp * (-1.0 / (1.0 + jnp.exp(lam_v[:, cols])))
            de_s[0:TC, cols] = dxc

        dxc = de_s[0:TC, :]
        dvec_ref[0:1, :] += jnp.sum(dxc, axis=0, keepdims=True)
        dxr = dxc * cw_ref[3:4, :]
        dcw_ref[3:4, :] += jnp.sum(dxc * xe_s[8:8 + TC, :], axis=0, keepdims=True)
        for k in range(1, 4):
            dxr = dxr + de_s[k:k + TC, :] * cw_ref[3 - k:4 - k, :]
            dcw_ref[3 - k:4 - k, :] += jnp.sum(dxc * xe_s[8 - k:8 - k + TC, :], axis=0, keepdims=True)
        dp_ref[:, :D_MODEL] = dxr.astype(BF16)
        de_s[TC:TC + 8, :] = de_s[0:8, :]

    chunk = lambda col: pl.BlockSpec((TC, D_MODEL), lambda b, cr: (b * nc + nc - 1 - cr, col))
    halo = lambda col: pl.BlockSpec(
        (8, D_MODEL), lambda b, cr: (jnp.maximum((b * nc + nc - 1 - cr) * (TC // 8) - 1, 0), col))
    full = lambda shape: pl.BlockSpec(shape, lambda b, cr: (0,) * len(shape))
    wblk = (LRU_BLOCKS, LRU_BLOCK_W, LRU_BLOCK_W)
    return _pallas(
        body, name="lru_bwd", grid=(t_tok // seq, nc),
        in_specs=[chunk(0), halo(0), chunk(1), chunk(0), halo(0), chunk(0),
                  full((4, D_MODEL)), full((1, D_MODEL)), full(wblk), full((1, D_MODEL)), full(wblk),
                  full((1, D_MODEL)), full((1, D_MODEL))],
        out_specs=[pl.BlockSpec((TC, 2 * D_MODEL), lambda b, cr: (b * nc + nc - 1 - cr, 0)),
                   full((8, D_MODEL)), full((8, D_MODEL)), full(wblk), full(wblk)],
        out_shape=[jax.ShapeDtypeStruct((t_tok, 2 * D_MODEL), BF16), jax.ShapeDtypeStruct((8, D_MODEL), F32),
                   jax.ShapeDtypeStruct((8, D_MODEL), F32), jax.ShapeDtypeStruct(wblk, F32),
                   jax.ShapeDtypeStruct(wblk, F32)],
        scratch_shapes=[pltpu.VMEM((TC + 8, D_MODEL), F32), pltpu.VMEM((TC + 8, D_MODEL), F32),
                        pltpu.VMEM((TC + 8, D_MODEL), F32)]
        + [pltpu.VMEM((TC, D_MODEL), F32)] * 5 + [pltpu.VMEM((8, D_MODEL), F32)],
        compiler_params=_cp(("arbitrary", "arbitrary"), VMEM_BIG),
    )(proj, proj, proj, hs, hs, dyh, cw, cb, w_a, b_a, w_x, b_x, lam)


def _final_loss(x, g, target):
    t_tok = x.shape[0]

    def body(x_ref, g_ref, t_ref, dx_ref, loss_ref, dg_ref):
        @pl.when(pl.program_id(0) == 0)
        def _():
            loss_ref[...] = jnp.zeros_like(loss_ref)
            dg_ref[...] = jnp.zeros_like(dg_ref)

        xv = x_ref[...]
        gv = g_ref[...]
        rstd = lax.rsqrt(jnp.mean(xv * xv, axis=-1, keepdims=True) + EPS)
        xhat = xv * rstd
        err = xhat * gv - t_ref[...]
        loss_ref[0:1, :] += jnp.sum(err * err, axis=0, keepdims=True) * (0.5 / D_MODEL)
        dout = err * (1.0 / D_MODEL)
        dg_ref[0:1, :] += jnp.sum(dout * xhat, axis=0, keepdims=True)
        dxhat = dout * gv
        dx_ref[...] = rstd * (dxhat - xhat * jnp.mean(dxhat * xhat, axis=-1, keepdims=True))

    row = pl.BlockSpec((TM, D_MODEL), lambda i: (i, 0))
    acc = pl.BlockSpec((8, D_MODEL), lambda i: (0, 0))
    return _pallas(body, name="final_loss", grid=(t_tok // TM,),
                   in_specs=[row, pl.BlockSpec((1, D_MODEL), lambda i: (0, 0)), row],
                   out_specs=[row, acc, acc],
                   out_shape=[jax.ShapeDtypeStruct((t_tok, D_MODEL), F32)] + [jax.ShapeDtypeStruct((8, D_MODEL), F32)] * 2,
                   compiler_params=_cp(("arbitrary",), VMEM_MID))(x, g, target)


def _adam_math(w, g, m, v):
    m_new = ADAM_B1 * m + (1.0 - ADAM_B1) * g
    v_new = ADAM_B2 * v + (1.0 - ADAM_B2) * (g * g)
    m_hat = m_new / (1.0 - ADAM_B1 ** ADAM_STEP)
    v_hat = v_new / (1.0 - ADAM_B2 ** ADAM_STEP)
    delta = -ADAM_LR * (m_hat / (jnp.sqrt(v_hat) + ADAM_EPS) + ADAM_WD * w)
    return delta, m_new, v_new


def _sum_leading(name, x, out_dtype=F32):
    n, rows, cols = x.shape
    tr = PACK_ROWS if rows % PACK_ROWS == 0 else rows

    def body(x_ref, o_ref):
        acc = x_ref[0].astype(F32)
        for d in range(1, n):
            acc = acc + x_ref[d].astype(F32)
        o_ref[...] = acc.astype(out_dtype)

    return _pallas(body, name=name, grid=(rows // tr,),
                   in_specs=[pl.BlockSpec((n, tr, cols), lambda i: (0, i, 0))],
                   out_specs=pl.BlockSpec((tr, cols), lambda i: (i, 0)),
                   out_shape=jax.ShapeDtypeStruct((rows, cols), out_dtype),
                   compiler_params=_cp(("arbitrary",), VMEM_MID))(x)


def _adamw(name, w, m, v, g=None, parts=None):
    rows, cols = w.shape
    tr = rows if rows <= 256 else 256

    def body(*refs):
        w_ref, m_ref, v_ref, g_in, g_ref, d_ref, mo_ref, vo_ref = refs
        if parts is None:
            gv = g_in[...]
        else:
            acc = g_in[0].astype(F32)
            for d in range(1, parts.shape[0]):
                acc = acc + g_in[d].astype(F32)
            gv = acc[:, :cols]
        delta, m_new, v_new = _adam_math(w_ref[...], gv, m_ref[...], v_ref[...])
        g_ref[...] = gv
        d_ref[...] = delta
        mo_ref[...] = m_new
        vo_ref[...] = v_new

    row = pl.BlockSpec((tr, cols), lambda i: (i, 0))
    if parts is None:
        g_spec, g_arg = row, g
    else:
        g_spec, g_arg = pl.BlockSpec((parts.shape[0], tr, parts.shape[2]), lambda i: (0, i, 0)), parts
    return _pallas(body, name=name, grid=(rows // tr,), in_specs=[row, row, row, g_spec], out_specs=[row] * 4,
                   out_shape=[jax.ShapeDtypeStruct((rows, cols), F32)] * 4,
                   compiler_params=_cp(("arbitrary",), VMEM_MID))(w, m, v, g_arg)


def _pack_rows(arrs):
    rows, meta, total = [], [], 0
    for a in arrs:
        flat = a.reshape(-1)
        nrow = -(-flat.shape[0] // 1024) * 8
        rows.append(jnp.pad(flat, (0, nrow * 128 - flat.shape[0])).reshape(nrow, 128))
        meta.append((a.shape, flat.shape[0], nrow))
        total += nrow
    tail = -total % PACK_ROWS
    if tail:
        rows.append(jnp.zeros((tail, 128), F32))
    return jnp.concatenate(rows, axis=0), meta


def _unpack_rows(packed, meta):
    out, r0 = [], 0
    for shape, size, nrow in meta:
        out.append(packed[r0:r0 + nrow].reshape(-1)[:size].reshape(shape))
        r0 += nrow
    return out


WEIGHTS = ["rel_bias", "norm_g", "ada_w", "ada_b", "attn_w_in", "attn_sinks", "attn_b_f", "attn_w_out", "lru_w_in",
           "lru_conv_w", "lru_conv_b", "lru_w_a", "lru_b_a", "lru_w_x", "lru_b_x", "lru_lambda", "lru_w_out", "final_g"]
BIG = ["ada_w", "attn_w_in", "attn_w_out", "lru_w_in", "lru_w_out"]
PACK_ROWS = 256


def kernel(x, c, rel_bias, norm_g, ada_w, ada_b, attn_w_in, attn_sinks, attn_b_f, attn_w_out, lru_w_in, lru_conv_w, lru_conv_b, lru_w_a, lru_b_a, lru_w_x, lru_b_x, lru_lambda, lru_w_out, final_g, loss_target, m_rel_bias, m_norm_g, m_ada_w, m_ada_b, m_attn_w_in, m_attn_sinks, m_attn_b_f, m_attn_w_out, m_lru_w_in, m_lru_conv_w, m_lru_conv_b, m_lru_w_a, m_lru_b_a, m_lru_w_x, m_lru_b_x, m_lru_lambda, m_lru_w_out, m_final_g, v_rel_bias, v_norm_g, v_ada_w, v_ada_b, v_attn_w_in, v_attn_sinks, v_attn_b_f, v_attn_w_out, v_lru_w_in, v_lru_conv_w, v_lru_conv_b, v_lru_w_a, v_lru_b_a, v_lru_w_x, v_lru_b_x, v_lru_lambda, v_lru_w_out, v_final_g):
    nseq, seq, _ = x.shape
    t_tok = nseq * seq
    me = 4 * lax.axis_index("x") + 2 * lax.axis_index("y") + lax.axis_index("c")
    x0 = x.reshape(t_tok, D_MODEL)
    target = loss_target.reshape(t_tok, D_MODEL)

    w_in_pad = jnp.pad(attn_w_in[0].astype(BF16), ((0, 0), (0, SHARD_W_PAD - SHARD_W_IN)))
    vec_shard = jnp.concatenate([lru_conv_w[0], lru_conv_b, lru_b_a, lru_b_x, lru_lambda], axis=0)
    g_w_in, g_vec, g_c = _exchange("gather_first", [w_in_pad, vec_shard, c], [])
    later_w = [attn_w_out[0].astype(BF16), lru_w_in[0].astype(BF16), lru_w_out[0].astype(BF16)]
    later_handle, later_token = _exchange_start("gather_later_start", later_w, [], after=g_vec)
    w_full = jnp.transpose(g_w_in[:, :, :SHARD_W_IN], (1, 0, 2)).reshape(D_MODEL, N_DEV * SHARD_W_IN)
    w_aq, w_ak, w_av = w_full[:, 0:512], w_full[:, 512:640], w_full[:, 640:768]
    w_bq, w_bk, w_bv = w_full[:, 768:1280], w_full[:, 1280:1792], w_full[:, 1792:2304]
    w_f, w_gate = w_full[:, 2304:2312], w_full[:, 2312:3336]
    w_main = jnp.concatenate([w_bq, w_bk, w_bv, w_aq, w_gate, w_ak, w_av], axis=1)
    wf_t = jnp.transpose(w_f)
    vec_full = jnp.transpose(g_vec, (1, 0, 2)).reshape(8, D_MODEL)
    conv_w, conv_b, b_a, b_x, lam = vec_full[0:4], vec_full[4:5], vec_full[5:6], vec_full[6:7], vec_full[7:8]
    c_all = g_c.reshape(N_DEV * nseq, D_MODEL)

    ncol = ada_w.shape[2]
    ada_b_slice = lax.dynamic_slice(ada_b.reshape(2, N_DEV, ncol), (0, me, 0), (2, 1, ncol))
    mod_part = _ada_mod(c_all, ada_w, ada_b_slice)
    (g_mod,) = _exchange("gather_mod", [mod_part], [])
    mine = lax.dynamic_slice(g_mod, (0, 0, me * nseq, 0), (N_DEV, 2, nseq, ncol))
    mod = jnp.transpose(mine, (1, 2, 0, 3)).reshape(2, nseq, 3 * D_MODEL)
    shift = [mod[l, :, 0:D_MODEL].reshape(nseq, 1, D_MODEL) for l in range(2)]
    scale = [mod[l, :, D_MODEL:2 * D_MODEL].reshape(nseq, 1, D_MODEL) for l in range(2)]
    gmod = [mod[l, :, 2 * D_MODEL:].reshape(nseq, 1, D_MODEL) for l in range(2)]

    onehot = _bucket_onehot()
    bias = _bias_expand(jnp.transpose(rel_bias), onehot).reshape(N_HEADS, BLOCK, 2 * BLOCK)
    sinks = attn_sinks.reshape(N_HEADS)
    b_f = attn_b_f.reshape(N_HEADS, 1)
    norm_g0 = norm_g[0:1] + later_token[0:1, 0:1]
    h0, qkvg, fl_t = _norm_proj("norm_proj0", x0, norm_g0, shift[0], scale[0], w_main, seq, BF16, wf_t=wf_t)
    f_row, f_col = _fox_prep(fl_t, b_f, seq)
    a_out, lse_a = _swa_fwd(qkvg, bias, sinks, seq)
    q_aug, k_aug, kt_aug, vt = _fox_aug(qkvg, f_col, seq)
    b_out, lse_b = _fox_fwd_t(q_aug, k_aug, vt, seq)
    g_later = _exchange_wait("gather_later_wait", later_handle, after=lse_b)
    w_out0, g_lru_in, w_out1 = (_with_own(g, w, me) for g, w in zip(g_later, later_w))
    w_out0, w_out1 = w_out0.reshape(D_MODEL, D_MODEL), w_out1.reshape(D_MODEL, D_MODEL)
    yg0, y0, x1 = _out_proj("out_proj0", [a_out, b_out], qkvg, C_GATE // D_MODEL, w_out0, x0, gmod[0], seq)

    h1, proj1 = _norm_proj("norm_proj1", x1, norm_g[1:2], shift[1], scale[1], g_lru_in, seq, F32)
    hs = _lru_fwd(proj1, conv_w, conv_b, lru_w_a[0], b_a, lru_w_x[0], b_x, lam, seq)
    yg1, y1, x2 = _out_proj("out_proj1", [hs], proj1, 1, w_out1, x1, gmod[1], seq)

    dx2, loss_rows, dfinal_rows = _final_loss(x2, final_g.reshape(1, D_MODEL), target)
    loss = lax.psum(jnp.sum(loss_rows[0]), ("x", "y", "c"))

    dy1, dgm1, dyh = _out_proj_bwd("out_proj1_bwd", dx2, gmod[1], y1, w_out1, seq)
    dproj1, dcw, dvec, dw_a, dw_x = _lru_bwd(proj1, hs, dyh, conv_w, conv_b, lru_w_a[0], b_a, lru_w_x[0], b_x, lam, seq)
    dx1, dss1, dg1 = _norm_bwd("norm1_bwd", [(dproj1, 0)], g_lru_in, x1, norm_g[1:2], scale[1], dx2, seq)
    (p_w_out1,) = _dw("dw_out1", yg1, [dy1])
    (p_lru_in,) = _dw("dw_lru_in", h1, [dproj1], blocked=2 * D_MODEL // N_DEV)

    rows_out = D_MODEL // N_DEV
    gpack1, gmeta1 = _pack_rows([dcw[0:4], dvec[0:4], dg1[0], dfinal_rows[0]])
    dwax = jnp.stack([dw_a, dw_x]).astype(BF16)
    own1 = [gpack1, dwax, p_lru_in, p_w_out1.reshape(N_DEV, rows_out, D_MODEL)]
    grads1_handle, grads1_token = _exchange_start("grads1_start", own1[:2], own1[2:], after=dx1)

    gmod0 = gmod[0] + grads1_token[0:1, 0:1]
    dy0, dgm0, du_a, du_b, dgate = _out_proj_bwd("out_proj0_bwd", dx1, gmod0, y0, w_out0, seq,
                                                  attn=(a_out, b_out, qkvg))
    dq_a, dkv_a, dbias, dsink = _swa_bwd(qkvg, du_a, a_out, lse_a, bias, sinks, seq)
    dq_b, dk_b, dv_b, df4 = _fox_bwd_t(q_aug, k_aug, kt_aug, qkvg, du_b, b_out, lse_b, seq)
    dfl_t, db_f = _fox_post(df4.reshape(N_HEADS, t_tok), fl_t, b_f, seq)
    parts0 = [(dq_b, C_BQ), (dk_b, C_BK), (dv_b, C_BV), (dq_a, C_AQ), (dgate, C_GATE), (dkv_a, C_AK)]
    dx0, dss0, dg0 = _norm_bwd("norm0_bwd", parts0, w_main, x0, norm_g[0:1], scale[0], dx1, seq,
                               rows_part=(dfl_t, wf_t))
    (p_w_out0,) = _dw("dw_out0", yg0, [dy0])
    pw_bq, pw_bk, pw_bv, pw_aq, pw_gate, pw_akv = _dw("dw_attn_in", h0, [p for p, _ in parts0])
    pw_f = _dw_rows("dw_f", dfl_t, h0)
    dbias_t = _bias_reduce(dbias.reshape(N_HEADS, BLOCK * 2 * BLOCK), onehot)

    p_w_in = jnp.concatenate([pw_aq, pw_akv, pw_bq, pw_bk, pw_bv, jnp.transpose(pw_f).astype(BF16), pw_gate], axis=1)
    p_w_in = jnp.transpose(p_w_in.reshape(D_MODEL, N_DEV, SHARD_W_IN), (1, 0, 2))
    p_w_in = jnp.pad(p_w_in, ((0, 0), (0, 0), (0, SHARD_W_PAD - SHARD_W_IN)))
    gpack0, gmeta0 = _pack_rows([jnp.transpose(dbias_t), dg0[0], dsink[:, 0], db_f[:, 0]])
    dmod = jnp.stack([jnp.concatenate([dss[:, 0], dss[:, 1], dgm[:, 0]], axis=1)
                      for dss, dgm in ((dss0, dgm0), (dss1, dgm1))], axis=1)
    g_small0, g_dmod, r_w_in, r_w_out0 = _exchange(
        "exchange_grads0", [gpack0, dmod], [p_w_in, p_w_out0.reshape(N_DEV, rows_out, D_MODEL)])
    landed1 = _exchange_wait("grads1_wait", grads1_handle, after=g_small0)
    g_small1, g_dwax = (_with_own(g, a, me) for g, a in zip(landed1[:2], own1[:2]))
    r_lru_in, r_w_out1 = (_with_own(g, lax.dynamic_index_in_dim(a, me, 0, keepdims=False), me)
                          for g, a in zip(landed1[2:], own1[2:]))

    d_rel, d_g0, d_sinks, d_b_f = _unpack_rows(_sum_leading("sum_small0", g_small0), gmeta0)
    d_cw, d_vec, d_g1, d_final_g = _unpack_rows(_sum_leading("sum_small1", g_small1), gmeta1)
    d_norm_g = jnp.stack([d_g0, d_g1])
    d_wax = _sum_leading("sum_dwax", g_dwax.reshape(N_DEV, 2 * LRU_BLOCKS * LRU_BLOCK_W, LRU_BLOCK_W))
    d_wa, d_wx = d_wax[:LRU_BLOCKS * LRU_BLOCK_W], d_wax[LRU_BLOCKS * LRU_BLOCK_W:]
    cols = lambda a: lax.dynamic_slice(a, (0, me * LRU_BLOCK_W), (a.shape[0], LRU_BLOCK_W))
    dmod_all = g_dmod.reshape(N_DEV * nseq, 2 * 3 * D_MODEL)
    d_ada_b = _sum_leading("sum_ada_b", dmod_all.reshape(N_DEV * nseq, 2 * 3 * D_MODEL // 128, 128)).reshape(2, 3 * D_MODEL)
    dmod_slice = lax.dynamic_slice(dmod_all.reshape(N_DEV * nseq, 2, N_DEV, ncol), (0, 0, me, 0),
                                   (N_DEV * nseq, 2, 1, ncol)).reshape(N_DEV * nseq, 2, ncol)
    d_ada_w = _ada_w_grad(c_all, jnp.transpose(dmod_slice, (1, 0, 2)))

    given = dict(
        rel_bias=(rel_bias, m_rel_bias, v_rel_bias), norm_g=(norm_g, m_norm_g, v_norm_g),
        ada_w=(ada_w, m_ada_w, v_ada_w), ada_b=(ada_b, m_ada_b, v_ada_b),
        attn_w_in=(attn_w_in, m_attn_w_in, v_attn_w_in), attn_sinks=(attn_sinks, m_attn_sinks, v_attn_sinks),
        attn_b_f=(attn_b_f, m_attn_b_f, v_attn_b_f), attn_w_out=(attn_w_out, m_attn_w_out, v_attn_w_out),
        lru_w_in=(lru_w_in, m_lru_w_in, v_lru_w_in), lru_conv_w=(lru_conv_w, m_lru_conv_w, v_lru_conv_w),
        lru_conv_b=(lru_conv_b, m_lru_conv_b, v_lru_conv_b), lru_w_a=(lru_w_a, m_lru_w_a, v_lru_w_a),
        lru_b_a=(lru_b_a, m_lru_b_a, v_lru_b_a), lru_w_x=(lru_w_x, m_lru_w_x, v_lru_w_x),
        lru_b_x=(lru_b_x, m_lru_b_x, v_lru_b_x), lru_lambda=(lru_lambda, m_lru_lambda, v_lru_lambda),
        lru_w_out=(lru_w_out, m_lru_w_out, v_lru_w_out), final_g=(final_g, m_final_g, v_final_g))
    results = {}

    def big(name, shape2d, g=None, parts=None):
        w, m, v = (a.reshape(shape2d) for a in given[name])
        outs = _adamw("adamw_" + name, w, m, v, g=g, parts=parts)
        results[name] = tuple(o.reshape(given[name][0].shape) for o in outs)

    big("ada_w", (2 * D_MODEL, ncol), g=d_ada_w.reshape(2 * D_MODEL, ncol))
    big("attn_w_in", (D_MODEL, SHARD_W_IN), parts=r_w_in)
    big("attn_w_out", (rows_out, D_MODEL), parts=r_w_out0)
    big("lru_w_in", (D_MODEL, 2 * D_MODEL // N_DEV), parts=r_lru_in)
    big("lru_w_out", (rows_out, D_MODEL), parts=r_w_out1)

    small_grads = dict(
        rel_bias=d_rel, norm_g=d_norm_g, ada_b=d_ada_b, attn_sinks=d_sinks.reshape(1, N_HEADS),
        attn_b_f=d_b_f.reshape(1, N_HEADS), lru_conv_w=cols(d_cw).reshape(1, 4, LRU_BLOCK_W),
        lru_conv_b=cols(d_vec[0:1]), lru_w_a=d_wa.reshape(lru_w_a.shape), lru_b_a=cols(d_vec[1:2]),
        lru_w_x=d_wx.reshape(lru_w_x.shape), lru_b_x=cols(d_vec[2:3]), lru_lambda=cols(d_vec[3:4]),
        final_g=d_final_g)
    small = [n for n in WEIGHTS if n not in BIG]
    wpack, smeta = _pack_rows([given[n][0] for n in small])
    mpack, _ = _pack_rows([given[n][1] for n in small])
    vpack, _ = _pack_rows([given[n][2] for n in small])
    gpack2, _ = _pack_rows([small_grads[n] for n in small])
    packs = _adamw("adamw_small", wpack, mpack, vpack, g=gpack2)
    unpacked = [_unpack_rows(p, smeta) for p in packs]
    for k, n in enumerate(small):
        results[n] = tuple(unpacked[j][k] for j in range(4))

    grad_x = dx0.reshape(x.shape)
    out = [loss, grad_x]
    for j in range(4):
        out += [results[n][j] for n in WEIGHTS]
    return tuple(out)
```

```python
import functools
import math

import jax
import jax.numpy as jnp
from jax import lax
from jax.experimental import pallas as pl
from jax.experimental.pallas import tpu as pltpu

F32 = jnp.float32
BF16 = jnp.bfloat16
HI = lax.Precision.HIGHEST
MESH = pl.DeviceIdType.MESH

N_DEV = 8
D_MODEL = 1024
HEAD_DIM = 64
N_HEADS = 8
KV_GROUP = 4
BLOCK = 128
REL_BUCKETS = 32
REL_MAX_EXACT = 16
REL_MAX_DIST = 128
LRU_BLOCKS = 8
LRU_BLOCK_W = 128
LRU_C = 8.0
EPS = 1e-6
SCALE = HEAD_DIM ** -0.5
NEG = -1e30

ADAM_LR = 0.001
ADAM_B1 = 0.9
ADAM_B2 = 0.999
ADAM_EPS = 1e-08
ADAM_WD = 0.01
ADAM_STEP = 10

C_BQ, C_BK, C_BV, C_AQ, C_GATE, C_AK, C_AV = 0, 512, 1024, 1536, 2048, 3072, 3200
N_MAIN = 3328
SHARD_W_IN = 417
SHARD_W_PAD = 512

TM = 256
TQ = 256
TK = 128
TC = 256
VMEM_BIG = 56 * 1024 * 1024
VMEM_MID = 40 * 1024 * 1024


def _pallas(body, **kw):
    return pl.pallas_call(body, **kw)


def _cp(sem=None, vmem=None):
    kw = {}
    if sem is not None:
        kw["dimension_semantics"] = sem
    if vmem is not None:
        kw["vmem_limit_bytes"] = vmem
    return pltpu.CompilerParams(**kw)


def _nn(a, b, precision=None):
    return jnp.dot(a, b, preferred_element_type=F32, precision=precision)


def _nt(a, b, precision=None):
    return lax.dot_general(a, b, (((1,), (1,)), ((), ())), preferred_element_type=F32, precision=precision)


def _tn(a, b, precision=None):
    return lax.dot_general(a, b, (((0,), (0,)), ((), ())), preferred_element_type=F32, precision=precision)


def _sigmoid(x):
    return 1.0 / (1.0 + jnp.exp(-x))


def _silu(x):
    return x * _sigmoid(x)


def _dsilu(x):
    s = _sigmoid(x)
    return s * (1.0 + x * (1.0 - s))


def _neg_expm1(x):
    poly = x * (1.0 + x * (0.5 + x * (1.0 / 6.0 + x * (1.0 / 24.0))))
    return -jnp.where(jnp.abs(x) < 0.05, poly, jnp.exp(x) - 1.0)


def _col(tile, idx):
    lane = lax.broadcasted_iota(jnp.int32, tile.shape, 1)
    return jnp.sum(jnp.where(lane == idx, tile, 0.0), axis=1, keepdims=True)


def _row(tile, idx):
    sub = lax.broadcasted_iota(jnp.int32, tile.shape, 0)
    return jnp.sum(jnp.where(sub == idx, tile, 0.0), axis=0, keepdims=True)


def _exchange(name, gathers, scatters, axes=("x", "y", "c"), chunks=1):
    ng, n = len(gathers), len(gathers) + len(scatters)
    ins = list(gathers) + list(scatters)
    group = 2 ** len(axes)

    def body(*refs):
        in_refs, out_refs = refs[:n], refs[n:2 * n]
        send_sems, recv_sems, loc_sems = refs[2 * n:]
        coord = {a: lax.axis_index(a) for a in ("x", "y", "c")}

        def member(r):
            pc = dict(coord)
            idx = 0
            for k, a in enumerate(axes):
                if r & (1 << (len(axes) - 1 - k)):
                    pc[a] = 1 - coord[a]
                idx = 2 * idx + pc[a]
            return (pc["x"], pc["y"], pc["c"]), idx

        _, me = member(0)

        def peer(r):
            return member(r)

        local, sends, recvs = [], [], []
        for k in range(n):
            mine = in_refs[k] if k < ng else in_refs[k].at[me]
            cp = pltpu.make_async_copy(mine, out_refs[k].at[me], loc_sems.at[k])
            cp.start()
            local.append(cp)
            lead = mine.shape[0]
            nchunk = max(q for q in range(1, chunks + 1) if lead % q == 0)
            step = lead // nchunk
            for r in range(1, group):
                pid, pidx = peer(r)
                src = in_refs[k] if k < ng else in_refs[k].at[pidx]
                for q in range(nchunk):
                    rows = pl.ds(q * step, step)
                    sems = dict(send_sem=send_sems.at[r - 1, k, q], recv_sem=recv_sems.at[r - 1, k, q],
                                device_id=pid, device_id_type=MESH)
                    snd = pltpu.make_async_remote_copy(src_ref=src.at[rows], dst_ref=out_refs[k].at[me].at[rows], **sems)
                    snd.start()
                    sends.append(snd)
                    recvs.append(pltpu.make_async_remote_copy(
                        src_ref=src.at[rows], dst_ref=out_refs[k].at[pidx].at[rows], **sems))
        for rc in recvs:
            rc.wait_recv()
        for snd in sends:
            snd.wait_send()
        for cp in local:
            cp.wait()

    out_shape = [jax.ShapeDtypeStruct((group,) + a.shape, a.dtype) for a in gathers]
    out_shape += [jax.ShapeDtypeStruct(a.shape, a.dtype) for a in scatters]
    any_spec = pl.BlockSpec(memory_space=pl.ANY)
    return _pallas(
        body, name=name, out_shape=out_shape,
        in_specs=[any_spec] * n, out_specs=[any_spec] * n,
        scratch_shapes=[pltpu.SemaphoreType.DMA((group - 1, n, chunks)), pltpu.SemaphoreType.DMA((group - 1, n, chunks)),
                        pltpu.SemaphoreType.DMA((n,))],
    )(*ins)


def _peer_of(r):
    x, y, c = lax.axis_index("x"), lax.axis_index("y"), lax.axis_index("c")
    px = 1 - x if r & 4 else x
    py = 1 - y if r & 2 else y
    pc = 1 - c if r & 1 else c
    return (px, py, pc), 4 * px + 2 * py + pc


def _split_copies(in_refs, land_refs, send_sems, recv_sems, ng):
    _, me = _peer_of(0)
    pairs = []
    for k, (src_ref, land) in enumerate(zip(in_refs, land_refs)):
        for r in range(1, N_DEV):
            pid, pidx = _peer_of(r)
            src = src_ref if k < ng else src_ref.at[pidx]
            slot = (N_DEV - 1) * k + r - 1
            sems = dict(send_sem=send_sems.at[slot], recv_sem=recv_sems.at[slot], device_id=pid, device_id_type=MESH)
            pairs.append((pltpu.make_async_remote_copy(src_ref=src, dst_ref=land.at[me], **sems),
                          pltpu.make_async_remote_copy(src_ref=src, dst_ref=land.at[pidx], **sems)))
    return pairs


def _exchange_start(name, gathers, scatters, after):
    ng, n = len(gathers), len(gathers) + len(scatters)
    ins = list(gathers) + list(scatters)
    lands = [jax.ShapeDtypeStruct((N_DEV,) + a.shape, a.dtype) for a in gathers]
    lands += [jax.ShapeDtypeStruct(a.shape, a.dtype) for a in scatters]

    def body(*refs):
        in_refs, land_refs = refs[:n], refs[n:2 * n]
        send_sems, recv_sems = refs[2 * n + 1:2 * n + 3]
        token = refs[-1]
        for send, _ in _split_copies(in_refs, land_refs, send_sems, recv_sems, ng):
            send.start()
        token[...] = jnp.zeros_like(token)

    hbm = pl.BlockSpec(memory_space=pltpu.HBM)
    sem = pl.BlockSpec(memory_space=pltpu.SEMAPHORE)
    sem_shape = pltpu.SemaphoreType.DMA(((N_DEV - 1) * n,))
    out_shape = [sem_shape, sem_shape] + [pltpu.HBM(a.shape, a.dtype) for a in ins]
    out_shape += [pltpu.HBM(l.shape, l.dtype) for l in lands] + [jax.ShapeDtypeStruct((8, 128), F32)]
    args = [pltpu.with_memory_space_constraint(a, pltpu.HBM) for a in ins]
    args += [pltpu.with_memory_space_constraint(lax.empty(l.shape, l.dtype), pltpu.HBM) for l in lands]
    outs = _pallas(
        body, name=name, out_shape=out_shape,
        in_specs=[hbm] * (2 * n) + [pl.BlockSpec(memory_space=pl.ANY)],
        out_specs=[sem, sem] + [hbm] * (2 * n) + [pl.BlockSpec(memory_space=pltpu.VMEM)],
        input_output_aliases={i: 2 + i for i in range(2 * n)},
        compiler_params=pltpu.CompilerParams(has_side_effects=pltpu.SideEffectType.DATAFLOW_SIDE_EFFECTING),
    )(*args, after)
    return (outs[0], outs[1], list(outs[2:2 + n]), list(outs[2 + n:2 + 2 * n]), ng), outs[-1]


def _exchange_wait(name, handle, after):
    send_sems, recv_sems, srcs, lands, ng = handle
    n = len(srcs)

    def body(*refs):
        in_refs, land_refs = refs[:n], refs[n:2 * n]
        send_ref, recv_ref = refs[2 * n:2 * n + 2]
        for send, recv in _split_copies(in_refs, land_refs, send_ref, recv_ref, ng):
            send.wait_send()
            recv.wait_recv()

    hbm = pl.BlockSpec(memory_space=pltpu.HBM)
    sem = pl.BlockSpec(memory_space=pltpu.SEMAPHORE)
    outs = _pallas(
        body, name=name, out_shape=[pltpu.HBM(a.shape, a.dtype) for a in srcs + lands],
        in_specs=[hbm] * (2 * n) + [sem, sem, pl.BlockSpec(memory_space=pl.ANY)],
        out_specs=[hbm] * (2 * n), input_output_aliases={i: i for i in range(2 * n)},
        compiler_params=pltpu.CompilerParams(has_side_effects=pltpu.SideEffectType.DATAFLOW_SIDE_EFFECTING),
    )(*srcs, *lands, send_sems, recv_sems, after)
    return list(outs[n:])


def _with_own(land, own, me):
    return lax.dynamic_update_slice(land, own[None], (me,) + (0,) * own.ndim)


def _ada_mod(c_all, ada_w, ada_b_slice):
    def body(c_ref, w_ref, b_ref, o_ref):
        ca = _silu(c_ref[...])
        for l in range(2):
            o_ref[l] = _nn(ca, w_ref[l], HI) + b_ref[l]

    return _pallas(body, name="ada_mod",
                   out_shape=jax.ShapeDtypeStruct((2, c_all.shape[0], ada_w.shape[2]), F32),
                   compiler_params=_cp(vmem=VMEM_MID))(c_all, ada_w, ada_b_slice)


def _ada_w_grad(c_all, dmod_slice):
    def body(c_ref, d_ref, o_ref):
        ca = _silu(c_ref[...])
        for l in range(2):
            o_ref[l] = _tn(ca, d_ref[l], HI)

    return _pallas(body, name="ada_w_grad",
                   out_shape=jax.ShapeDtypeStruct((2, D_MODEL, dmod_slice.shape[2]), F32),
                   compiler_params=_cp(vmem=VMEM_MID))(c_all, dmod_slice)


def _bucket_onehot():
    qi = jnp.arange(BLOCK)[:, None]
    kj = jnp.arange(2 * BLOCK)[None, :]
    rel = qi - kj + BLOCK
    n = jnp.maximum(rel, 0)
    nf = jnp.maximum(n, 1).astype(F32)
    large = REL_MAX_EXACT + (jnp.log(nf / REL_MAX_EXACT) / math.log(REL_MAX_DIST / REL_MAX_EXACT)
                             * (REL_BUCKETS - REL_MAX_EXACT)).astype(jnp.int32)
    large = jnp.minimum(large, REL_BUCKETS - 1)
    bucket = jnp.where(n < REL_MAX_EXACT, n, large).reshape(1, BLOCK * 2 * BLOCK)
    return (jnp.arange(REL_BUCKETS)[:, None] == bucket).astype(F32)


def _bias_expand(rel_bias_t, onehot):
    def body(r_ref, e_ref, o_ref):
        o_ref[...] = _nn(r_ref[...], e_ref[...], HI)

    return _pallas(body, name="bias_expand",
                   out_shape=jax.ShapeDtypeStruct((N_HEADS, onehot.shape[1]), F32),
                   compiler_params=_cp(vmem=VMEM_MID))(rel_bias_t, onehot)


def _bias_reduce(dbias, onehot):
    def body(d_ref, e_ref, o_ref):
        o_ref[...] = _nt(d_ref[...], e_ref[...], HI)

    return _pallas(body, name="bias_reduce",
                   out_shape=jax.ShapeDtypeStruct((N_HEADS, REL_BUCKETS), F32),
                   compiler_params=_cp(vmem=VMEM_MID))(dbias, onehot)


def _norm_proj(name, x, g, shift, scale, w, seq, out_dtype, wf_t=None):
    t_tok = x.shape[0]
    w3d = w.ndim == 3
    n_out = w.shape[0] * w.shape[2] if w3d else w.shape[1]
    cn = w.shape[2] if w3d else 256

    def body(x_ref, g_ref, sh_ref, sc_ref, w_ref, *rest):
        if wf_t is not None:
            wf_ref, h_ref, o_ref, fl_ref = rest
        else:
            h_ref, o_ref = rest
        xv = x_ref[...]
        rstd = lax.rsqrt(jnp.mean(xv * xv, axis=-1, keepdims=True) + EPS)
        h = (xv * rstd) * g_ref[...] * (1.0 + sc_ref[...]) + sh_ref[...]
        hb = h.astype(BF16)
        h_ref[...] = hb
        for j in range(n_out // cn):
            wj = w_ref[j] if w3d else w_ref[:, j * cn:(j + 1) * cn]
            o_ref[:, j * cn:(j + 1) * cn] = _nn(hb, wj).astype(out_dtype)
        if wf_t is not None:
            fl_ref[...] = _nt(wf_ref[...], hb)

    mod_spec = pl.BlockSpec((None, 1, D_MODEL), lambda i: (i * TM // seq, 0, 0))
    w_spec = (pl.BlockSpec(w.shape, lambda i: (0, 0, 0)) if w3d else pl.BlockSpec(w.shape, lambda i: (0, 0)))
    in_specs = [pl.BlockSpec((TM, D_MODEL), lambda i: (i, 0)), pl.BlockSpec((1, D_MODEL), lambda i: (0, 0)),
                mod_spec, mod_spec, w_spec]
    out_shape = [jax.ShapeDtypeStruct((t_tok, D_MODEL), BF16), jax.ShapeDtypeStruct((t_tok, n_out), out_dtype)]
    out_specs = [pl.BlockSpec((TM, D_MODEL), lambda i: (i, 0)), pl.BlockSpec((TM, n_out), lambda i: (i, 0))]
    args = [x, g, shift, scale, w]
    if wf_t is not None:
        in_specs.append(pl.BlockSpec(wf_t.shape, lambda i: (0, 0)))
        out_shape.append(jax.ShapeDtypeStruct((wf_t.shape[0], t_tok), F32))
        out_specs.append(pl.BlockSpec((wf_t.shape[0], TM), lambda i: (0, i)))
        args.append(wf_t)
    return _pallas(body, name=name, grid=(t_tok // TM,), in_specs=in_specs, out_specs=out_specs,
                   out_shape=out_shape, compiler_params=_cp(("arbitrary",), VMEM_BIG))(*args)


def _fox_prep(fl_t, b_f, seq):
    t_tok = fl_t.shape[1]
    ch = 256

    def body(fl_ref, bf_ref, fr_ref, fc_ref):
        z = fl_ref[...] + bf_ref[...]
        logf = jnp.minimum(z, 0.0) - jnp.log(1.0 + jnp.exp(-jnp.abs(z)))
        ri = lax.broadcasted_iota(jnp.int32, (ch, ch), 0)
        ci = lax.broadcasted_iota(jnp.int32, (ch, ch), 1)
        upper = (ri <= ci).astype(F32)
        eye = (ri == ci).astype(F32)
        carry = jnp.zeros((N_HEADS, 1), F32)
        for k in range(seq // ch):
            fk = _nn(logf[:, k * ch:(k + 1) * ch], upper, HI) + carry
            carry = fk[:, ch - 1:ch]
            fr_ref[:, k * ch:(k + 1) * ch] = fk
            padded = jnp.concatenate([fk, jnp.zeros((128 - N_HEADS, ch), F32)], axis=0)
            fc_ref[k * ch:(k + 1) * ch, :] = _nt(eye, padded, HI)

    return _pallas(
        body, name="fox_prep", grid=(t_tok // seq,),
        in_specs=[pl.BlockSpec((N_HEADS, seq), lambda b: (0, b)), pl.BlockSpec((N_HEADS, 1), lambda b: (0, 0))],
        out_specs=[pl.BlockSpec((N_HEADS, seq), lambda b: (0, b)), pl.BlockSpec((seq, 128), lambda b: (b, 0))],
        out_shape=[jax.ShapeDtypeStruct((N_HEADS, t_tok), F32), jax.ShapeDtypeStruct((t_tok, 128), F32)],
        compiler_params=_cp(("arbitrary",), VMEM_MID))(fl_t, b_f)


def _fox_post(df_row, fl_t, b_f, seq):
    t_tok = fl_t.shape[1]
    ch = 256

    def body(d_ref, fl_ref, bf_ref, o_ref, db_ref):
        @pl.when(pl.program_id(0) == 0)
        def _():
            db_ref[...] = jnp.zeros_like(db_ref)

        z = fl_ref[...] + bf_ref[...]
        sig_neg = 1.0 / (1.0 + jnp.exp(z))
        ri = lax.broadcasted_iota(jnp.int32, (ch, ch), 0)
        ci = lax.broadcasted_iota(jnp.int32, (ch, ch), 1)
        lower = (ri >= ci).astype(F32)
        carry = jnp.zeros((N_HEADS, 1), F32)
        tot = jnp.zeros((N_HEADS, 1), F32)
        for k in reversed(range(seq // ch)):
            dk = _nn(d_ref[:, k * ch:(k + 1) * ch], lower, HI) + carry
            carry = dk[:, 0:1]
            dfl = dk * sig_neg[:, k * ch:(k + 1) * ch]
            o_ref[:, k * ch:(k + 1) * ch] = dfl
            tot = tot + jnp.sum(dfl, axis=1, keepdims=True)
        db_ref[...] += jnp.broadcast_to(tot, db_ref.shape)

    return _pallas(
        body, name="fox_post", grid=(t_tok // seq,),
        in_specs=[pl.BlockSpec((N_HEADS, seq), lambda b: (0, b)), pl.BlockSpec((N_HEADS, seq), lambda b: (0, b)),
                  pl.BlockSpec((N_HEADS, 1), lambda b: (0, 0))],
        out_specs=[pl.BlockSpec((N_HEADS, seq), lambda b: (0, b)), pl.BlockSpec((N_HEADS, 128), lambda b: (0, 0))],
        out_shape=[jax.ShapeDtypeStruct((N_HEADS, t_tok), F32), jax.ShapeDtypeStruct((N_HEADS, 128), F32)],
        compiler_params=_cp(("arbitrary",), VMEM_MID))(df_row, fl_t, b_f)


def _eye(n, dtype):
    return (lax.broadcasted_iota(jnp.int32, (n, n), 0) == lax.broadcasted_iota(jnp.int32, (n, n), 1)).astype(dtype)


def _fox_aug(qkvg, f_col, seq):
    t_tok = qkvg.shape[0]
    ta = 256
    nkb = ta // TK

    def body(q_ref, k_ref, v_ref, fc_ref, qa_ref, ka_ref, kt_ref, vt_ref):
        ri = lax.broadcasted_iota(jnp.int32, (128, 128), 0)
        ci = lax.broadcasted_iota(jnp.int32, (128, 128), 1)
        eye = (ri == ci).astype(BF16)
        lane = lax.broadcasted_iota(jnp.int32, (ta, 128), 1)
        ones_q = jnp.where(jnp.logical_and(lane >= 64, lane < 67), 1.0, 0.0)
        ones_k = jnp.where(jnp.logical_and(lane >= 67, lane < 70), 1.0, 0.0)
        fc_tile = fc_ref[...]
        for p in range(N_HEADS // 2):
            q2 = q_ref[:, 128 * p:128 * (p + 1)]
            k2 = k_ref[:, 128 * p:128 * (p + 1)]
            vt = _nt(eye, v_ref[:, 128 * p:128 * (p + 1)]).astype(BF16)
            for kk in range(nkb):
                vt_ref[p, kk] = vt[:, kk * TK:(kk + 1) * TK]
            for e in range(2):
                h = 2 * p + e
                sel = jnp.logical_and(ri == ci + HEAD_DIM * e, ci < HEAD_DIM)
                f = _col(fc_tile, h)
                fh = f.astype(BF16).astype(F32)
                fm = (f - fh).astype(BF16).astype(F32)
                fl = (f - fh - fm).astype(BF16).astype(F32)
                qa = (_nn(q2, jnp.where(sel, SCALE, 0.0).astype(BF16)) + ones_q + jnp.where(lane == 67, fh, 0.0)
                      + jnp.where(lane == 68, fm, 0.0) + jnp.where(lane == 69, fl, 0.0))
                ka = (_nn(k2, jnp.where(sel, 1.0, 0.0).astype(BF16)) + ones_k - jnp.where(lane == 64, fh, 0.0)
                      - jnp.where(lane == 65, fm, 0.0) - jnp.where(lane == 66, fl, 0.0))
                qa_ref[h] = qa.astype(BF16)
                kab = ka.astype(BF16)
                ka_ref[h] = kab
                kt = _nt(eye, kab).astype(BF16)
                for kk in range(nkb):
                    kt_ref[h, kk] = kt[:, kk * TK:(kk + 1) * TK]

    aug = jax.ShapeDtypeStruct((N_HEADS, t_tok, 128), BF16)
    return _pallas(
        body, name="fox_aug", grid=(t_tok // ta,),
        in_specs=[pl.BlockSpec((ta, 512), lambda i: (i, C_BQ // 512)), pl.BlockSpec((ta, 512), lambda i: (i, C_BK // 512)),
                  pl.BlockSpec((ta, 512), lambda i: (i, C_BV // 512)), pl.BlockSpec((ta, 128), lambda i: (i, 0))],
        out_specs=[pl.BlockSpec((N_HEADS, ta, 128), lambda i: (0, i, 0)), pl.BlockSpec((N_HEADS, ta, 128), lambda i: (0, i, 0)),
                   pl.BlockSpec((N_HEADS, nkb, 128, TK), lambda i: (0, i, 0, 0)),
                   pl.BlockSpec((N_HEADS // 2, nkb, 128, TK), lambda i: (0, i, 0, 0))],
        out_shape=[aug, aug, jax.ShapeDtypeStruct((N_HEADS, t_tok // TK, 128, TK), BF16),
                   jax.ShapeDtypeStruct((N_HEADS // 2, t_tok // TK, 128, TK), BF16)],
        compiler_params=_cp(("arbitrary",), VMEM_MID))(qkvg, qkvg, qkvg, f_col)


def _fox_fwd_t(q_aug, k_aug, vt, seq):
    t_tok = q_aug.shape[1]
    nq = seq // TQ
    ratio = TQ // TK

    def body(qa_ref, ka_ref, vt_ref, o_ref, lse_ref, ml_s, acc_s, st_s, p_s, al_s):
        i = pl.program_id(1)
        tpos = i * TQ + lax.broadcasted_iota(jnp.int32, (1, TQ), 1)
        eye = _eye(HEAD_DIM, BF16)
        for h in range(N_HEADS):
            ml_s[0, h] = jnp.full((1, TQ), NEG, F32)
            ml_s[1, h] = jnp.zeros((1, TQ), F32)
            acc_s[h] = jnp.zeros((HEAD_DIM, TQ), F32)
            p_s[1, h] = jnp.zeros((TK, TQ), BF16)
            al_s[1, h] = jnp.ones((1, TQ), F32)

        def scores(j):
            row0 = pl.multiple_of(j * TK, TK)
            for h in range(N_HEADS):
                st_s[j & 1, h] = _nt(ka_ref[h, pl.ds(row0, TK), :], qa_ref[h])

        def softmax(j, masked):
            slot = j & 1
            if masked:
                keep = (j * TK + lax.broadcasted_iota(jnp.int32, (TK, 1), 0)) <= tpos
            for h in range(N_HEADS):
                st = st_s[slot, h]
                if masked:
                    st = jnp.where(keep, st, NEG)
                m = ml_s[0, h]
                m_new = jnp.maximum(m, jnp.max(st, axis=0, keepdims=True))
                alpha = jnp.exp(m - m_new)
                pe = jnp.exp(st - m_new)
                ml_s[0, h] = m_new
                ml_s[1, h] = alpha * ml_s[1, h] + jnp.sum(pe, axis=0, keepdims=True)
                al_s[slot, h] = alpha
                p_s[slot, h] = pe.astype(BF16)

        def values(j):
            slot = j & 1
            jv = jnp.maximum(j, 0)
            for h in range(N_HEADS):
                p, e = divmod(h, 2)
                acc_s[h] = al_s[slot, h] * acc_s[h] + _nn(vt_ref[p, jv, e * HEAD_DIM:(e + 1) * HEAD_DIM, :], p_s[slot, h])

        def step(j, carry):
            values(j - 1)
            softmax(j, False)
            scores(j + 1)
            return carry

        last = ratio * i + ratio - 1
        scores(0)
        lax.fori_loop(0, ratio * i, step, 0)
        for kk in range(ratio):
            j = ratio * i + kk
            values(j - 1)
            softmax(j, True)
            if kk < ratio - 1:
                scores(j + 1)
        values(last)
        for p in range(N_HEADS // 2):
            outs = []
            for e in range(2):
                h = 2 * p + e
                l = ml_s[1, h]
                outs.append(_tn((acc_s[h] / l).astype(BF16), eye))
                lse_ref[p, e:e + 1, :] = ml_s[0, h] + jnp.log(l)
            o_ref[:, 128 * p:128 * (p + 1)] = jnp.concatenate(outs, axis=1).astype(BF16)

    return _pallas(
        body, name="fox_fwd", grid=(t_tok // seq, nq),
        in_specs=[pl.BlockSpec((N_HEADS, TQ, 128), lambda b, i: (0, b * nq + i, 0)),
                  pl.BlockSpec((N_HEADS, seq, 128), lambda b, i: (0, b, 0)),
                  pl.BlockSpec((N_HEADS // 2, seq // TK, 128, TK), lambda b, i: (0, b, 0, 0))],
        out_specs=[pl.BlockSpec((TQ, 512), lambda b, i: (b * nq + i, 0)),
                   pl.BlockSpec((N_HEADS // 2, 2, TQ), lambda b, i: (0, 0, b * nq + i))],
        out_shape=[jax.ShapeDtypeStruct((t_tok, 512), BF16), jax.ShapeDtypeStruct((N_HEADS // 2, 2, t_tok), F32)],
        scratch_shapes=[pltpu.VMEM((2, N_HEADS, 1, TQ), F32), pltpu.VMEM((N_HEADS, HEAD_DIM, TQ), F32),
                        pltpu.VMEM((2, N_HEADS, TK, TQ), F32), pltpu.VMEM((2, N_HEADS, TK, TQ), BF16),
                        pltpu.VMEM((2, N_HEADS, 1, TQ), F32)],
        compiler_params=_cp(("arbitrary", "arbitrary"), VMEM_MID))(q_aug, k_aug, vt)


def _fox_bwd_t(q_aug, k_aug, kt, qkvg, du_b, b_out, lse, seq):
    t_tok = qkvg.shape[0]
    nq = seq // TQ
    nkb = seq // TK
    ratio = TQ // TK
    hg = 4

    def body(qa_ref, ka_ref, kt_ref, v_ref, do_ref, o_ref, lse_ref, dq_ref, dk_ref, dv_ref, df_ref,
             dqt_s, row_s, dfk_s, dk_s, dv_s, dfa_s, st_s, dp_s, pb_s, db_s):
        ones_b = jnp.ones((8, TQ), BF16)
        eye = _eye(HEAD_DIM, BF16)
        lane8 = lax.broadcasted_iota(jnp.int32, (8, 128), 1)
        lane_k = lax.broadcasted_iota(jnp.int32, (TK, 128), 1)
        first = [lane8 < HEAD_DIM, lane8 >= HEAD_DIM]
        for hh in range(hg):
            pp, e = divmod(hh, 2)
            head_lanes = jnp.where(first[e], 1.0, 0.0)
            for ii in range(nq):
                rows = slice(ii * TQ, (ii + 1) * TQ)
                prod = do_ref[rows, 128 * pp:128 * (pp + 1)].astype(F32) * o_ref[rows, 128 * pp:128 * (pp + 1)].astype(F32)
                row_s[hh, ii, 0] = _nt(head_lanes, prod, HI)
                row_s[hh, ii, 1] = jnp.broadcast_to(lse_ref[pp, e:e + 1, ii * TQ:(ii + 1) * TQ], (8, TQ))
                dqt_s[hh, ii] = jnp.zeros((128, TQ), F32)

        def kblock(j, _):
            krow = pl.multiple_of(j * TK, TK)
            spos = j * TK + lax.broadcasted_iota(jnp.int32, (TK, 1), 0)
            for hh in range(hg):
                dk_s[hh] = jnp.zeros((TK, 128), F32)
                dv_s[hh] = jnp.zeros((TK, 128), F32)
                dfa_s[hh] = jnp.zeros((8, TK), F32)

            def scores(i):
                qrow = pl.multiple_of(i * TQ, TQ)
                for hh in range(hg):
                    pp, e = divmod(hh, 2)
                    own = (lane_k < HEAD_DIM) if e == 0 else (lane_k >= HEAD_DIM)
                    v2 = v_ref[pl.ds(krow, TK), 128 * pp:128 * (pp + 1)]
                    vj = jnp.where(own, v2, jnp.zeros_like(v2))
                    st_s[i & 1, hh] = _nt(ka_ref[hh, pl.ds(krow, TK), :], qa_ref[hh, pl.ds(qrow, TQ), :])
                    dp_s[i & 1, hh] = _nt(vj, do_ref[pl.ds(qrow, TQ), 128 * pp:128 * (pp + 1)])

            def elementwise(i, masked):
                slot = i & 1
                if masked:
                    keep = spos <= (i * TQ + lax.broadcasted_iota(jnp.int32, (1, TQ), 1))
                for hh in range(hg):
                    pt = jnp.exp(st_s[slot, hh] - row_s[hh, i, 1][0:1, :])
                    if masked:
                        pt = jnp.where(keep, pt, 0.0)
                    dst = pt * (dp_s[slot, hh] - row_s[hh, i, 0][0:1, :])
                    pb_s[slot, hh] = pt.astype(BF16)
                    db_s[slot, hh] = dst.astype(BF16)

            def grads(i):
                slot = i & 1
                qrow = pl.multiple_of(i * TQ, TQ)
                for hh in range(hg):
                    pp = hh // 2
                    dst_b = db_s[slot, hh]
                    dv_s[hh] += _nn(pb_s[slot, hh], do_ref[pl.ds(qrow, TQ), 128 * pp:128 * (pp + 1)])
                    dk_s[hh] += _nn(dst_b, qa_ref[hh, pl.ds(qrow, TQ), :])
                    dqt_s[hh, i] += _nn(kt_ref[hh, j], dst_b)
                    dfa_s[hh] += _nt(ones_b, dst_b)

            def step(i, carry):
                grads(i - 1)
                elementwise(i, False)
                scores(jnp.minimum(i + 1, nq - 1))
                return carry

            i0 = j // ratio
            scores(i0)
            elementwise(i0, True)
            scores(jnp.minimum(i0 + 1, nq - 1))
            lax.fori_loop(i0 + 1, nq, step, 0)
            grads(nq - 1)
            for pp in range(hg // 2):
                cols = slice(128 * pp, 128 * (pp + 1))
                dk_ref[pl.ds(krow, TK), cols] = jnp.concatenate(
                    [dk_s[2 * pp][:, :HEAD_DIM], dk_s[2 * pp + 1][:, :HEAD_DIM]], axis=1).astype(BF16)
                dv_ref[pl.ds(krow, TK), cols] = jnp.where(lane_k < HEAD_DIM, dv_s[2 * pp], dv_s[2 * pp + 1]).astype(BF16)
            for hh in range(hg):
                dfk_s[hh, j] = dfa_s[hh]
            return 0

        lax.fori_loop(0, nkb, kblock, 0)
        for pp in range(hg // 2):
            for ii in range(nq):
                parts = []
                for e in range(2):
                    dqt = dqt_s[2 * pp + e, ii]
                    parts.append(_tn(dqt[0:HEAD_DIM, :].astype(BF16), eye) * SCALE)
                    for kk in range(ratio):
                        jj = ii * ratio + kk
                        df_ref[pp, e:e + 1, jj * TK:(jj + 1) * TK] = (dqt[67:68, kk * TK:(kk + 1) * TK]
                                                                     - dfk_s[2 * pp + e, jj][0:1, :])
                dq_ref[ii * TQ:(ii + 1) * TQ, 128 * pp:128 * (pp + 1)] = jnp.concatenate(parts, axis=1).astype(BF16)

    aug_blk = pl.BlockSpec((hg, seq, 128), lambda b, g: (g, b, 0))
    pair_blk = pl.BlockSpec((seq, 64 * hg), lambda b, g: (b, g))
    row_blk = pl.BlockSpec((hg // 2, 2, seq), lambda b, g: (g, 0, b))
    return _pallas(
        body, name="fox_bwd", grid=(t_tok // seq, N_HEADS // hg),
        in_specs=[aug_blk, aug_blk, pl.BlockSpec((hg, nkb, 128, TK), lambda b, g: (g, b, 0, 0)),
                  pl.BlockSpec((seq, 64 * hg), lambda b, g: (b, C_BV // (64 * hg) + g)), pair_blk, pair_blk, row_blk],
        out_specs=[pair_blk, pair_blk, pair_blk, row_blk],
        out_shape=[jax.ShapeDtypeStruct((t_tok, 512), BF16)] * 3
        + [jax.ShapeDtypeStruct((N_HEADS // 2, 2, t_tok), F32)],
        scratch_shapes=[pltpu.VMEM((hg, nq, 128, TQ), F32), pltpu.VMEM((hg, nq, 2, 8, TQ), F32),
                        pltpu.VMEM((hg, nkb, 8, TK), F32), pltpu.VMEM((hg, TK, 128), F32),
                        pltpu.VMEM((hg, TK, 128), F32), pltpu.VMEM((hg, 8, TK), F32),
                        pltpu.VMEM((2, hg, TK, TQ), F32), pltpu.VMEM((2, hg, TK, TQ), F32),
                        pltpu.VMEM((2, hg, TK, TQ), BF16), pltpu.VMEM((2, hg, TK, TQ), BF16)],
        compiler_params=_cp(("arbitrary", "arbitrary"), VMEM_BIG))(q_aug, k_aug, kt, qkvg, du_b, b_out, lse)


def _fox_bwd_t_old(q_aug, k_aug, kt, qkvg, du_b, b_out, lse, seq):
    t_tok = qkvg.shape[0]
    nq = seq // TQ
    nkb = seq // TK
    ratio = TQ // TK

    def body(qa_ref, ka_ref, kt_ref, v_ref, do_ref, o_ref, lse_ref, dq_ref, dk_ref, dv_ref, df_ref,
             dqt_s, out_s, row_s, dfk_s):
        ones_b = jnp.ones((8, TQ), BF16)
        ones_f = jnp.ones((8, HEAD_DIM), F32)
        eye = _eye(HEAD_DIM, BF16)
        for e in range(2):
            lo, hi = e * HEAD_DIM, (e + 1) * HEAD_DIM
            for ii in range(nq):
                rows = slice(ii * TQ, (ii + 1) * TQ)
                do = do_ref[rows, :][:, lo:hi].astype(F32)
                ov = o_ref[rows, :][:, lo:hi].astype(F32)
                row_s[ii, 0] = _nt(ones_f, do * ov, HI)
                row_s[ii, 1] = jnp.broadcast_to(lse_ref[e:e + 1, ii * TQ:(ii + 1) * TQ], (8, TQ))
                dqt_s[ii] = jnp.zeros((128, TQ), F32)

            def kblock(j, _):
                krow = pl.multiple_of(j * TK, TK)
                kj = ka_ref[e, pl.ds(krow, TK), :]
                ktj = kt_ref[e, j]
                vj = v_ref[pl.ds(krow, TK), :][:, lo:hi]
                spos = j * TK + lax.broadcasted_iota(jnp.int32, (TK, 1), 0)

                def qblock(i, carry, masked):
                    dk_acc, dv_acc, dfk = carry
                    qrow = pl.multiple_of(i * TQ, TQ)
                    qa = qa_ref[e, pl.ds(qrow, TQ), :]
                    doh = do_ref[pl.ds(qrow, TQ), :][:, lo:hi]
                    pt = jnp.exp(_nt(kj, qa) - row_s[i, 1][0:1, :])
                    if masked:
                        tpos = i * TQ + lax.broadcasted_iota(jnp.int32, (1, TQ), 1)
                        pt = jnp.where(spos <= tpos, pt, 0.0)
                    dst = pt * (_nt(vj, doh) - row_s[i, 0][0:1, :])
                    dst_b = dst.astype(BF16)
                    dv_acc = dv_acc + _nn(pt.astype(BF16), doh)
                    dk_acc = dk_acc + _nn(dst_b, qa)
                    dqt_s[i] += _nn(ktj, dst_b)
                    dfk = dfk + _nt(ones_b, dst_b)
                    return dk_acc, dv_acc, dfk

                i0 = j // ratio
                carry = (jnp.zeros((TK, 128), F32), jnp.zeros((TK, HEAD_DIM), F32), jnp.zeros((8, TK), F32))
                carry = qblock(i0, carry, True)
                dk_acc, dv_acc, dfk = lax.fori_loop(i0 + 1, nq, functools.partial(qblock, masked=False), carry)
                out_s[1, e, pl.ds(krow, TK), :] = dk_acc[:, :HEAD_DIM]
                out_s[2, e, pl.ds(krow, TK), :] = dv_acc
                dfk_s[j] = dfk
                return 0

            lax.fori_loop(0, nkb, kblock, 0)
            for ii in range(nq):
                dqt = dqt_s[ii]
                out_s[0, e, ii * TQ:(ii + 1) * TQ, :] = _tn(dqt[0:HEAD_DIM, :].astype(BF16), eye) * SCALE
                for kk in range(ratio):
                    jj = ii * ratio + kk
                    df_ref[e:e + 1, jj * TK:(jj + 1) * TK] = dqt[67:68, kk * TK:(kk + 1) * TK] - dfk_s[jj][0:1, :]
        for k, ref in enumerate((dq_ref, dk_ref, dv_ref)):
            ref[...] = jnp.concatenate([out_s[k, 0], out_s[k, 1]], axis=1).astype(BF16)

    aug_blk = pl.BlockSpec((2, seq, 128), lambda b, p: (p, b, 0))
    pair_blk = pl.BlockSpec((seq, 128), lambda b, p: (b, p))
    row_blk = pl.BlockSpec((None, 2, seq), lambda b, p: (p, 0, b))
    return _pallas(
        body, name="fox_bwd", grid=(t_tok // seq, N_HEADS // 2),
        in_specs=[aug_blk, aug_blk, pl.BlockSpec((2, nkb, 128, TK), lambda b, p: (p, b, 0, 0)),
                  pl.BlockSpec((seq, 128), lambda b, p: (b, C_BV // 128 + p)), pair_blk, pair_blk, row_blk],
        out_specs=[pair_blk, pair_blk, pair_blk, row_blk],
        out_shape=[jax.ShapeDtypeStruct((t_tok, 512), BF16)] * 3
        + [jax.ShapeDtypeStruct((N_HEADS // 2, 2, t_tok), F32)],
        scratch_shapes=[pltpu.VMEM((nq, 128, TQ), F32), pltpu.VMEM((3, 2, seq, HEAD_DIM), F32),
                        pltpu.VMEM((nq, 2, 8, TQ), F32), pltpu.VMEM((nkb, 8, TK), F32)],
        compiler_params=_cp(("arbitrary", "arbitrary"), VMEM_BIG))(q_aug, k_aug, kt, qkvg, du_b, b_out, lse)


def _fox_fwd(qkvg, f_row, f_col, seq):
    t_tok = qkvg.shape[0]
    nq = seq // TQ

    def body(q_ref, k_ref, v_ref, fr_ref, fc_ref, o_ref, lse_ref, fk_s):
        i = pl.program_id(1)
        for jj in range(nq):
            fk_s[jj] = fr_ref[:, jj * TQ:(jj + 1) * TQ]
        fcol = fc_ref[...]
        tpos = i * TQ + lax.broadcasted_iota(jnp.int32, (TQ, 1), 0)
        lane = lax.broadcasted_iota(jnp.int32, (TQ, 128), 1)
        lse_tile = jnp.zeros((TQ, 128), F32)
        for p in range(N_HEADS // 2):
            q2 = q_ref[:, 128 * p:128 * (p + 1)]
            qs = [q2[:, :HEAD_DIM], q2[:, HEAD_DIM:]]
            fqs = [_col(fcol, 2 * p + e) for e in range(2)]

            def kblock(j, carry):
                row0 = pl.multiple_of(j * TQ, TQ)
                k2 = k_ref[pl.ds(row0, TQ), 128 * p:128 * (p + 1)]
                v2 = v_ref[pl.ds(row0, TQ), 128 * p:128 * (p + 1)]
                fk8 = fk_s[j]
                spos = j * TQ + lax.broadcasted_iota(jnp.int32, (1, TQ), 1)
                keep = spos <= tpos
                new = []
                for e in range(2):
                    m, l, acc = carry[3 * e:3 * e + 3]
                    kh = k2[:, e * HEAD_DIM:(e + 1) * HEAD_DIM]
                    vh = v2[:, e * HEAD_DIM:(e + 1) * HEAD_DIM]
                    s = _nt(qs[e], kh) * SCALE + (fqs[e] - fk8[2 * p + e:2 * p + e + 1, :])
                    s = jnp.where(keep, s, NEG)
                    m_new = jnp.maximum(m, jnp.max(s, axis=1, keepdims=True))
                    alpha = jnp.exp(m - m_new)
                    pe = jnp.exp(s - m_new)
                    l = alpha * l + jnp.sum(pe, axis=1, keepdims=True)
                    acc = alpha * acc + _nn(pe.astype(BF16), vh)
                    new += [m_new, l, acc]
                return tuple(new)

            init = (jnp.full((TQ, 1), NEG, F32), jnp.zeros((TQ, 1), F32), jnp.zeros((TQ, HEAD_DIM), F32)) * 2
            res = lax.fori_loop(0, i + 1, kblock, init)
            outs = []
            for e in range(2):
                m, l, acc = res[3 * e:3 * e + 3]
                outs.append(acc / l)
                lse_tile = jnp.where(lane == 2 * p + e, m + jnp.log(l), lse_tile)
            o_ref[:, 128 * p:128 * (p + 1)] = jnp.concatenate(outs, axis=1).astype(BF16)
        lse_ref[...] = lse_tile

    return _pallas(
        body, name="fox_fwd", grid=(t_tok // seq, nq),
        in_specs=[pl.BlockSpec((TQ, 512), lambda b, i: (b * nq + i, C_BQ // 512)),
                  pl.BlockSpec((seq, 512), lambda b, i: (b, C_BK // 512)),
                  pl.BlockSpec((seq, 512), lambda b, i: (b, C_BV // 512)),
                  pl.BlockSpec((N_HEADS, seq), lambda b, i: (0, b)),
                  pl.BlockSpec((TQ, 128), lambda b, i: (b * nq + i, 0))],
        out_specs=[pl.BlockSpec((TQ, 512), lambda b, i: (b * nq + i, 0)),
                   pl.BlockSpec((TQ, 128), lambda b, i: (b * nq + i, 0))],
        out_shape=[jax.ShapeDtypeStruct((t_tok, 512), BF16), jax.ShapeDtypeStruct((t_tok, 128), F32)],
        scratch_shapes=[pltpu.VMEM((nq, N_HEADS, TQ), F32)],
        compiler_params=_cp(("arbitrary", "arbitrary"), VMEM_MID))(qkvg, qkvg, qkvg, f_row, f_col)


def _fox_bwd(qkvg, du_b, b_out, lse, f_row, f_col, seq):
    t_tok = qkvg.shape[0]
    nq = seq // TQ

    def body(q_ref, k_ref, v_ref, do_ref, o_ref, lse_ref, fr_ref, fc_ref,
             dq_ref, dk_ref, dv_ref, df_ref, dq_s, dk_s, dv_s, col_s, df_s, fk_s):
        p = pl.program_id(1)
        for jj in range(nq):
            fk_s[jj] = fr_ref[:, jj * TQ:(jj + 1) * TQ]
        eye = (lax.broadcasted_iota(jnp.int32, (TQ, TQ), 0) == lax.broadcasted_iota(jnp.int32, (TQ, TQ), 1)).astype(F32)
        for e in range(2):
            h = 2 * p + e
            lo, hi = e * HEAD_DIM, (e + 1) * HEAD_DIM
            for ii in range(nq):
                rows = slice(ii * TQ, (ii + 1) * TQ)
                do = do_ref[rows, :][:, lo:hi].astype(F32)
                ov = o_ref[rows, :][:, lo:hi].astype(F32)
                col_s[0, rows, :] = jnp.sum(do * ov, axis=1, keepdims=True)
                col_s[1, rows, :] = _col(lse_ref[rows, :], h)
                col_s[2, rows, :] = _col(fc_ref[rows, :], h)
                dq_s[rows, :] = jnp.zeros((TQ, HEAD_DIM), F32)
                df_s[ii] = jnp.zeros((8, TQ), F32)
                col_s[3, rows, :] = jnp.zeros((TQ, 1), F32)

            def kblock(j, _):
                krow = pl.multiple_of(j * TQ, TQ)
                kh = k_ref[pl.ds(krow, TQ), :][:, lo:hi]
                vh = v_ref[pl.ds(krow, TQ), :][:, lo:hi]
                fk = _row(fk_s[j], h)
                spos = j * TQ + lax.broadcasted_iota(jnp.int32, (1, TQ), 1)

                def qblock(i, carry):
                    dk_acc, dv_acc, dfk = carry
                    qrow = pl.multiple_of(i * TQ, TQ)
                    qh = q_ref[pl.ds(qrow, TQ), :][:, lo:hi]
                    doh = do_ref[pl.ds(qrow, TQ), :][:, lo:hi]
                    delta = col_s[0, pl.ds(qrow, TQ), :]
                    lse_q = col_s[1, pl.ds(qrow, TQ), :]
                    fq = col_s[2, pl.ds(qrow, TQ), :]
                    tpos = i * TQ + lax.broadcasted_iota(jnp.int32, (TQ, 1), 0)
                    s = _nt(qh, kh) * SCALE + (fq - fk)
                    pr = jnp.where(spos <= tpos, jnp.exp(s - lse_q), 0.0)
                    dp = _nt(doh, vh)
                    ds = pr * (dp - delta)
                    ds_b = ds.astype(BF16)
                    dv_acc = dv_acc + _tn(pr.astype(BF16), doh)
                    dk_acc = dk_acc + _tn(ds_b, qh)
                    dq_s[pl.ds(qrow, TQ), :] += _nn(ds_b, kh)
                    col_s[3, pl.ds(qrow, TQ), :] += jnp.sum(ds, axis=1, keepdims=True)
                    dfk = dfk + jnp.sum(ds, axis=0, keepdims=True)
                    return dk_acc, dv_acc, dfk

                zero = jnp.zeros((TQ, HEAD_DIM), F32)
                dk_acc, dv_acc, dfk = lax.fori_loop(j, nq, qblock, (zero, zero, jnp.zeros((1, TQ), F32)))
                dk_s[e, pl.ds(krow, TQ), :] = dk_acc * SCALE
                dv_s[e, pl.ds(krow, TQ), :] = dv_acc
                df_s[j] -= jnp.broadcast_to(dfk, (8, TQ))
                return 0

            lax.fori_loop(0, nq, kblock, 0)
            dq_s2 = dq_s[...] * SCALE
            dk_s[2 + e] = dq_s2
            for ii in range(nq):
                dfq = jnp.broadcast_to(col_s[3, ii * TQ:(ii + 1) * TQ, :], (TQ, 128))
                df_ref[e:e + 1, ii * TQ:(ii + 1) * TQ] = _tn(dfq, eye, HI)[0:1, :] + df_s[ii][0:1, :]
        dq_ref[...] = jnp.concatenate([dk_s[2], dk_s[3]], axis=1).astype(BF16)
        dk_ref[...] = jnp.concatenate([dk_s[0], dk_s[1]], axis=1).astype(BF16)
        dv_ref[...] = jnp.concatenate([dv_s[0], dv_s[1]], axis=1).astype(BF16)

    blk = lambda off: pl.BlockSpec((seq, 128), lambda b, p: (b, off // 128 + p))
    out_blk = pl.BlockSpec((seq, 128), lambda b, p: (b, p))
    return _pallas(
        body, name="fox_bwd", grid=(t_tok // seq, N_HEADS // 2),
        in_specs=[blk(C_BQ), blk(C_BK), blk(C_BV), out_blk, out_blk,
                  pl.BlockSpec((seq, 128), lambda b, p: (b, 0)),
                  pl.BlockSpec((N_HEADS, seq), lambda b, p: (0, b)),
                  pl.BlockSpec((seq, 128), lambda b, p: (b, 0))],
        out_specs=[out_blk, out_blk, out_blk, pl.BlockSpec((None, 2, seq), lambda b, p: (p, 0, b))],
        out_shape=[jax.ShapeDtypeStruct((t_tok, 512), BF16)] * 3
        + [jax.ShapeDtypeStruct((N_HEADS // 2, 2, t_tok), F32)],
        scratch_shapes=[pltpu.VMEM((seq, HEAD_DIM), F32), pltpu.VMEM((4, seq, HEAD_DIM), F32),
                        pltpu.VMEM((2, seq, HEAD_DIM), F32), pltpu.VMEM((4, seq, 1), F32),
                        pltpu.VMEM((nq, 8, TQ), F32), pltpu.VMEM((nq, N_HEADS, TQ), F32)],
        compiler_params=_cp(("arbitrary", "arbitrary"), VMEM_BIG))(qkvg, qkvg, qkvg, du_b, b_out, lse, f_row, f_col)


def _swa_window(k_ref, v_ref, n):
    prev = pl.multiple_of(jnp.maximum(n - 1, 0) * BLOCK, BLOCK)
    cur = pl.multiple_of(n * BLOCK, BLOCK)
    kwin = jnp.concatenate([k_ref[pl.ds(prev, BLOCK), :], k_ref[pl.ds(cur, BLOCK), :]], axis=0)
    vwin = jnp.concatenate([v_ref[pl.ds(prev, BLOCK), :], v_ref[pl.ds(cur, BLOCK), :]], axis=0)
    ti = lax.broadcasted_iota(jnp.int32, (BLOCK, 2 * BLOCK), 0)
    sj = lax.broadcasted_iota(jnp.int32, (BLOCK, 2 * BLOCK), 1)
    rel = ti - sj + BLOCK
    first_key = jnp.where(n > 0, 0, BLOCK)
    mask = jnp.logical_and(jnp.logical_and(rel >= 0, rel < BLOCK), sj >= first_key)
    return kwin, vwin, mask, prev, cur


def _head_cols(ref, h):
    pair = ref[:, 128 * (h // 2):128 * (h // 2 + 1)]
    return pair[:, (h % 2) * HEAD_DIM:(h % 2 + 1) * HEAD_DIM]


def _swa_logits(q_ref, kwin, bias_ref, h, mask):
    hk = h // KV_GROUP
    s = _nt(_head_cols(q_ref, h), kwin[:, hk * HEAD_DIM:(hk + 1) * HEAD_DIM]) * SCALE + bias_ref[h]
    return jnp.where(mask, s, NEG)


def _swa_fwd(qkvg, bias, sinks, seq):
    t_tok = qkvg.shape[0]
    nb = seq // BLOCK

    def body(sink_ref, q_ref, k_ref, v_ref, bias_ref, o_ref, lse_ref, s_s, p_s, den_s):
        n = pl.program_id(1)
        kwin, vwin, mask, _, _ = _swa_window(k_ref, v_ref, n)
        for h in range(N_HEADS):
            s_s[h] = _swa_logits(q_ref, kwin, bias_ref, h, mask)
        lane = lax.broadcasted_iota(jnp.int32, (BLOCK, 128), 1)
        lse_tile = jnp.zeros((BLOCK, 128), F32)
        for h in range(N_HEADS):
            s = s_s[h]
            sink = sink_ref[h]
            m = jnp.maximum(jnp.max(s, axis=1, keepdims=True), sink)
            pe = jnp.exp(s - m)
            den = jnp.sum(pe, axis=1, keepdims=True) + jnp.exp(sink - m)
            p_s[h] = pe.astype(BF16)
            den_s[h] = den
            lse_tile = jnp.where(lane == h, m + jnp.log(den), lse_tile)
        lse_ref[...] = lse_tile
        for pr in range(N_HEADS // 2):
            outs = []
            for h in (2 * pr, 2 * pr + 1):
                hk = h // KV_GROUP
                outs.append(_nn(p_s[h], vwin[:, hk * HEAD_DIM:(hk + 1) * HEAD_DIM]) / den_s[h])
            o_ref[:, 128 * pr:128 * (pr + 1)] = jnp.concatenate(outs, axis=1).astype(BF16)

    return _pallas(
        body, name="swa_fwd", grid=(t_tok // seq, nb),
        in_specs=[pl.BlockSpec(memory_space=pltpu.SMEM),
                  pl.BlockSpec((BLOCK, 512), lambda b, n: (b * nb + n, C_AQ // 512)),
                  pl.BlockSpec((seq, 128), lambda b, n: (b, C_AK // 128)),
                  pl.BlockSpec((seq, 128), lambda b, n: (b, C_AV // 128)),
                  pl.BlockSpec((N_HEADS, BLOCK, 2 * BLOCK), lambda b, n: (0, 0, 0))],
        out_specs=[pl.BlockSpec((BLOCK, 512), lambda b, n: (b * nb + n, 0)),
                   pl.BlockSpec((BLOCK, 128), lambda b, n: (b * nb + n, 0))],
        out_shape=[jax.ShapeDtypeStruct((t_tok, 512), BF16), jax.ShapeDtypeStruct((t_tok, 128), F32)],
        scratch_shapes=[pltpu.VMEM((N_HEADS, BLOCK, 2 * BLOCK), F32), pltpu.VMEM((N_HEADS, BLOCK, 2 * BLOCK), BF16),
                        pltpu.VMEM((N_HEADS, BLOCK, 1), F32)],
        compiler_params=_cp(("arbitrary", "arbitrary"), VMEM_MID))(sinks, qkvg, qkvg, qkvg, bias)


def _swa_bwd(qkvg, du_a, a_out, lse, bias, sinks, seq):
    t_tok = qkvg.shape[0]
    nb = seq // BLOCK

    def body(sink_ref, q_ref, k_ref, v_ref, do_ref, o_ref, lse_ref, bias_ref,
             dq_ref, dkv_ref, dbias_ref, dsink_ref, kv_s, s_s, dp_s, pb_s, db_s):
        b, n = pl.program_id(0), pl.program_id(1)

        @pl.when(jnp.logical_and(b == 0, n == 0))
        def _():
            dbias_ref[...] = jnp.zeros_like(dbias_ref)
            dsink_ref[...] = jnp.zeros_like(dsink_ref)

        @pl.when(n == 0)
        def _():
            kv_s[...] = jnp.zeros_like(kv_s)

        kwin, vwin, mask, prev, cur = _swa_window(k_ref, v_ref, n)
        for h in range(N_HEADS):
            hk = h // KV_GROUP
            s_s[h] = _swa_logits(q_ref, kwin, bias_ref, h, mask)
            dp_s[h] = _nt(_head_cols(do_ref, h), vwin[:, hk * HEAD_DIM:(hk + 1) * HEAD_DIM])
        lse_tile = lse_ref[...]
        for h in range(N_HEADS):
            delta = jnp.sum(_head_cols(do_ref, h).astype(F32) * _head_cols(o_ref, h).astype(F32), axis=1, keepdims=True)
            lse_h = _col(lse_tile, h)
            pe = jnp.exp(s_s[h] - lse_h)
            ds = pe * (dp_s[h] - delta)
            dbias_ref[h] += ds
            psink = jnp.exp(sink_ref[h] - lse_h)
            dsink_ref[h:h + 1, :] += jnp.broadcast_to(jnp.sum(-psink * delta, axis=0, keepdims=True), (1, 128))
            pb_s[h] = pe.astype(BF16)
            db_s[h] = ds.astype(BF16)
        for pr in range(N_HEADS // 2):
            dqs = []
            for h in (2 * pr, 2 * pr + 1):
                hk = h // KV_GROUP
                dqs.append(_nn(db_s[h], kwin[:, hk * HEAD_DIM:(hk + 1) * HEAD_DIM]) * SCALE)
            dq_ref[:, 128 * pr:128 * (pr + 1)] = jnp.concatenate(dqs, axis=1).astype(BF16)
        dks, dvs = [], []
        for hk in range(N_HEADS // KV_GROUP):
            dk = jnp.zeros((2 * BLOCK, HEAD_DIM), F32)
            dv = jnp.zeros((2 * BLOCK, HEAD_DIM), F32)
            for h in range(hk * KV_GROUP, (hk + 1) * KV_GROUP):
                dk = dk + _tn(db_s[h], _head_cols(q_ref, h))
                dv = dv + _tn(pb_s[h], _head_cols(do_ref, h))
            dks.append(dk * SCALE)
            dvs.append(dv)
        upd = jnp.concatenate(dks + dvs, axis=1)
        kv_s[pl.ds(prev, BLOCK), :] += upd[:BLOCK]
        kv_s[pl.ds(cur, BLOCK), :] += upd[BLOCK:]

        @pl.when(n == nb - 1)
        def _():
            dkv_ref[...] = kv_s[...].astype(BF16)

    return _pallas(
        body, name="swa_bwd", grid=(t_tok // seq, nb),
        in_specs=[pl.BlockSpec(memory_space=pltpu.SMEM),
                  pl.BlockSpec((BLOCK, 512), lambda b, n: (b * nb + n, C_AQ // 512)),
                  pl.BlockSpec((seq, 128), lambda b, n: (b, C_AK // 128)),
                  pl.BlockSpec((seq, 128), lambda b, n: (b, C_AV // 128)),
                  pl.BlockSpec((BLOCK, 512), lambda b, n: (b * nb + n, 0)),
                  pl.BlockSpec((BLOCK, 512), lambda b, n: (b * nb + n, 0)),
                  pl.BlockSpec((BLOCK, 128), lambda b, n: (b * nb + n, 0)),
                  pl.BlockSpec((N_HEADS, BLOCK, 2 * BLOCK), lambda b, n: (0, 0, 0))],
        out_specs=[pl.BlockSpec((BLOCK, 512), lambda b, n: (b * nb + n, 0)),
                   pl.BlockSpec((seq, 256), lambda b, n: (b, 0)),
                   pl.BlockSpec((N_HEADS, BLOCK, 2 * BLOCK), lambda b, n: (0, 0, 0)),
                   pl.BlockSpec((N_HEADS, 128), lambda b, n: (0, 0))],
        out_shape=[jax.ShapeDtypeStruct((t_tok, 512), BF16), jax.ShapeDtypeStruct((t_tok, 256), BF16),
                   jax.ShapeDtypeStruct((N_HEADS, BLOCK, 2 * BLOCK), F32), jax.ShapeDtypeStruct((N_HEADS, 128), F32)],
        scratch_shapes=[pltpu.VMEM((seq, 256), F32),
                        pltpu.VMEM((N_HEADS, BLOCK, 2 * BLOCK), F32), pltpu.VMEM((N_HEADS, BLOCK, 2 * BLOCK), F32),
                        pltpu.VMEM((N_HEADS, BLOCK, 2 * BLOCK), BF16), pltpu.VMEM((N_HEADS, BLOCK, 2 * BLOCK), BF16)],
        compiler_params=_cp(("arbitrary", "arbitrary"), VMEM_MID))(sinks, qkvg, qkvg, qkvg, du_a, a_out, lse, bias)


def _out_proj(name, u_parts, gate_arr, gate_blk, w_out, x, gmod, seq):
    t_tok = x.shape[0]
    nu = len(u_parts)

    def body(*refs):
        u_refs = refs[:nu]
        g_ref, w_ref, x_ref, gm_ref, yg_ref, y_ref, xn_ref = refs[nu:]
        u = jnp.concatenate([r[...].astype(F32) for r in u_refs], axis=1) if nu > 1 else u_refs[0][...].astype(F32)
        yg = (u * _silu(g_ref[...].astype(F32))).astype(BF16)
        yg_ref[...] = yg
        y = _nn(yg, w_ref[...])
        y_ref[...] = y.astype(BF16)
        xn_ref[...] = x_ref[...] + gm_ref[...] * y

    row = lambda w: pl.BlockSpec((TM, w), lambda i: (i, 0))
    in_specs = [row(u.shape[1]) for u in u_parts]
    in_specs += [pl.BlockSpec((TM, D_MODEL), lambda i: (i, gate_blk)),
                 pl.BlockSpec((D_MODEL, D_MODEL), lambda i: (0, 0)), row(D_MODEL),
                 pl.BlockSpec((None, 1, D_MODEL), lambda i: (i * TM // seq, 0, 0))]
    return _pallas(
        body, name=name, grid=(t_tok // TM,), in_specs=in_specs,
        out_specs=[row(D_MODEL)] * 3,
        out_shape=[jax.ShapeDtypeStruct((t_tok, D_MODEL), BF16)] * 2 + [jax.ShapeDtypeStruct((t_tok, D_MODEL), F32)],
        compiler_params=_cp(("arbitrary",), VMEM_MID))(*u_parts, gate_arr, w_out, x, gmod)


def _out_proj_bwd(name, dxn, gmod, y, w_out, seq, attn=None):
    t_tok = dxn.shape[0]
    tiles_per_seq = seq // TM

    def body(*refs):
        if attn is None:
            dxn_ref, gm_ref, y_ref, w_ref, dy_ref, dgm_ref, dyg_ref = refs
        else:
            dxn_ref, gm_ref, y_ref, w_ref, a_ref, b_ref, g_ref, dy_ref, dgm_ref, dua_ref, dub_ref, dg_ref = refs
        i = pl.program_id(0)
        dxv = dxn_ref[...]
        dy = (dxv * gm_ref[...]).astype(BF16)
        dy_ref[...] = dy

        @pl.when(i % tiles_per_seq == 0)
        def _():
            dgm_ref[...] = jnp.zeros_like(dgm_ref)

        dgm_ref[...] += jnp.sum(dxv * y_ref[...].astype(F32), axis=0, keepdims=True)
        dyg = _nt(dy, w_ref[...])
        if attn is None:
            dyg_ref[...] = dyg
        else:
            gt = g_ref[...].astype(F32)
            du = dyg * _silu(gt)
            dua_ref[...] = du[:, :512].astype(BF16)
            dub_ref[...] = du[:, 512:].astype(BF16)
            u = jnp.concatenate([a_ref[...].astype(F32), b_ref[...].astype(F32)], axis=1)
            dg_ref[...] = (dyg * u * _dsilu(gt)).astype(BF16)

    row = lambda w: pl.BlockSpec((TM, w), lambda i: (i, 0))
    mod_spec = pl.BlockSpec((None, 1, D_MODEL), lambda i: (i * TM // seq, 0, 0))
    in_specs = [row(D_MODEL), mod_spec, row(D_MODEL), pl.BlockSpec((D_MODEL, D_MODEL), lambda i: (0, 0))]
    out_specs = [row(D_MODEL), mod_spec]
    out_shape = [jax.ShapeDtypeStruct((t_tok, D_MODEL), BF16), jax.ShapeDtypeStruct(gmod.shape, F32)]
    args = [dxn, gmod, y, w_out]
    if attn is None:
        out_specs.append(row(D_MODEL))
        out_shape.append(jax.ShapeDtypeStruct((t_tok, D_MODEL), F32))
    else:
        in_specs += [row(512), row(512), pl.BlockSpec((TM, D_MODEL), lambda i: (i, C_GATE // D_MODEL))]
        out_specs += [row(512), row(512), row(D_MODEL)]
        out_shape += [jax.ShapeDtypeStruct((t_tok, 512), BF16)] * 2 + [jax.ShapeDtypeStruct((t_tok, D_MODEL), BF16)]
        args += list(attn)
    return _pallas(body, name=name, grid=(t_tok // TM,), in_specs=in_specs, out_specs=out_specs,
                   out_shape=out_shape, compiler_params=_cp(("arbitrary",), VMEM_MID))(*args)


def _norm_bwd(name, parts, w, x, g, scale, dxn, seq, rows_part=None):
    t_tok = x.shape[0]
    npart = len(parts)
    w3d = w.ndim == 3
    tiles_per_seq = seq // TM
    nrow_in = 0 if rows_part is None else 2

    def body(*refs):
        p_refs = refs[:npart]
        w_ref, x_ref, g_ref, sc_ref, dxn_ref = refs[npart:npart + 5]
        dx_ref, dss_ref, dg_ref = refs[npart + 5 + nrow_in:]
        i = pl.program_id(0)
        dh = jnp.zeros((TM, D_MODEL), F32)
        if rows_part is not None:
            r_ref, wr_ref = refs[npart + 5:npart + 7]
            dh = dh + _tn(r_ref[...].astype(BF16), wr_ref[...])
        for (arr, off), p_ref in zip(parts, p_refs):
            width = arr.shape[1]
            for j in range(width // 256):
                pj = p_ref[:, j * 256:(j + 1) * 256]
                c0 = off + j * 256
                wj = w_ref[c0 // 256] if w3d else w_ref[:, c0:c0 + 256]
                dh = dh + _nt(pj, wj)
        xv = x_ref[...]
        rstd = lax.rsqrt(jnp.mean(xv * xv, axis=-1, keepdims=True) + EPS)
        xhat = xv * rstd
        gv = g_ref[...]
        nrm = xhat * gv

        @pl.when(i % tiles_per_seq == 0)
        def _():
            dss_ref[...] = jnp.zeros_like(dss_ref)

        @pl.when(i == 0)
        def _():
            dg_ref[...] = jnp.zeros_like(dg_ref)

        dss_ref[0:1, :] += jnp.sum(dh, axis=0, keepdims=True)
        dss_ref[1:2, :] += jnp.sum(dh * nrm, axis=0, keepdims=True)
        dn = dh * (1.0 + sc_ref[...])
        dg_ref[0:1, :] += jnp.sum(dn * xhat, axis=0, keepdims=True)
        dxhat = dn * gv
        dx_ref[...] = rstd * (dxhat - xhat * jnp.mean(dxhat * xhat, axis=-1, keepdims=True)) + dxn_ref[...]

    row = lambda wd: pl.BlockSpec((TM, wd), lambda i: (i, 0))
    w_spec = (pl.BlockSpec(w.shape, lambda i: (0, 0, 0)) if w3d else pl.BlockSpec(w.shape, lambda i: (0, 0)))
    in_specs = [row(a.shape[1]) for a, _ in parts]
    in_specs += [w_spec, row(D_MODEL), pl.BlockSpec((1, D_MODEL), lambda i: (0, 0)),
                 pl.BlockSpec((None, 1, D_MODEL), lambda i: (i * TM // seq, 0, 0)), row(D_MODEL)]
    args = [a for a, _ in parts] + [w, x, g, scale, dxn]
    if rows_part is not None:
        in_specs += [pl.BlockSpec((8, TM), lambda i: (0, i)), pl.BlockSpec((8, D_MODEL), lambda i: (0, 0))]
        args += list(rows_part)
    nseq = t_tok // seq
    return _pallas(
        body, name=name, grid=(t_tok // TM,), in_specs=in_specs,
        out_specs=[row(D_MODEL), pl.BlockSpec((None, 8, D_MODEL), lambda i: (i * TM // seq, 0, 0)),
                   pl.BlockSpec((8, D_MODEL), lambda i: (0, 0))],
        out_shape=[jax.ShapeDtypeStruct((t_tok, D_MODEL), F32), jax.ShapeDtypeStruct((nseq, 8, D_MODEL), F32),
                   jax.ShapeDtypeStruct((8, D_MODEL), F32)],
        compiler_params=_cp(("arbitrary",), VMEM_BIG))(*args)


def _dw(name, a, parts, blocked=None):
    t_tok, ka = a.shape
    tt = 512
    npart = len(parts)
    nt = t_tok // tt

    def body(*refs):
        a_ref = refs[0]
        p_refs = refs[1:1 + npart]
        o_refs = refs[1 + npart:1 + 2 * npart]
        acc_refs = refs[1 + 2 * npart:]
        t = pl.program_id(0)
        av = a_ref[...]
        for p_ref, acc in zip(p_refs, acc_refs):
            upd = _tn(av, p_ref[...])

            @pl.when(t == 0)
            def _():
                acc[...] = upd

            @pl.when(t > 0)
            def _():
                acc[...] += upd

        @pl.when(t == nt - 1)
        def _():
            for o_ref, acc in zip(o_refs, acc_refs):
                if blocked is None:
                    o_ref[...] = acc[...].astype(BF16)
                else:
                    for j in range(o_ref.shape[0]):
                        o_ref[j] = acc[:, j * blocked:(j + 1) * blocked].astype(BF16)

    in_specs = [pl.BlockSpec((tt, ka), lambda t: (t, 0))]
    in_specs += [pl.BlockSpec((tt, p.shape[1]), lambda t: (t, 0)) for p in parts]
    if blocked is None:
        out_shape = [jax.ShapeDtypeStruct((ka, p.shape[1]), BF16) for p in parts]
        out_specs = [pl.BlockSpec((ka, p.shape[1]), lambda t: (0, 0)) for p in parts]
    else:
        out_shape = [jax.ShapeDtypeStruct((p.shape[1] // blocked, ka, blocked), BF16) for p in parts]
        out_specs = [pl.BlockSpec((p.shape[1] // blocked, ka, blocked), lambda t: (0, 0, 0)) for p in parts]
    return _pallas(body, name=name, grid=(nt,), in_specs=in_specs, out_specs=out_specs, out_shape=out_shape,
                   scratch_shapes=[pltpu.VMEM((ka, p.shape[1]), F32) for p in parts],
                   compiler_params=_cp(("arbitrary",), VMEM_BIG))(a, *parts)


def _dw_rows(name, rows_t, h):
    t_tok = h.shape[0]
    tt = 512

    def body(r_ref, h_ref, o_ref):
        @pl.when(pl.program_id(0) == 0)
        def _():
            o_ref[...] = jnp.zeros_like(o_ref)

        o_ref[...] += _nn(r_ref[...].astype(BF16), h_ref[...])

    return _pallas(body, name=name, grid=(t_tok // tt,),
                   in_specs=[pl.BlockSpec((8, tt), lambda t: (0, t)), pl.BlockSpec((tt, D_MODEL), lambda t: (t, 0))],
                   out_specs=pl.BlockSpec((8, D_MODEL), lambda t: (0, 0)),
                   out_shape=jax.ShapeDtypeStruct((8, D_MODEL), F32),
                   compiler_params=_cp(("arbitrary",), VMEM_MID))(rows_t, h)


def _lru_gates(xc, blk, wa_ref, wx_ref, ba_ref, bx_ref, sp):
    cols = slice(blk * LRU_BLOCK_W, (blk + 1) * LRU_BLOCK_W)
    xb = xc[:, cols].astype(BF16)
    r = _sigmoid(_nn(xb, wa_ref[blk].astype(BF16)) + ba_ref[:, cols])
    ig = _sigmoid(_nn(xb, wx_ref[blk].astype(BF16)) + bx_ref[:, cols])
    log_a = -LRU_C * r * sp[:, cols]
    a = jnp.exp(log_a)
    mult = jnp.sqrt(_neg_expm1(2.0 * log_a))
    return xb, r, ig, a, mult


def _softplus_neg(lam):
    return jnp.maximum(-lam, 0.0) + jnp.log(1.0 + jnp.exp(-jnp.abs(lam)))


def _conv_taps(xe_ref, cw_ref, cb_ref):
    xc = cb_ref[...] + xe_ref[8:8 + TC, :] * cw_ref[3:4, :]
    for k in range(1, 4):
        xc = xc + xe_ref[8 - k:8 - k + TC, :] * cw_ref[3 - k:4 - k, :]
    return xc


def _lru_fwd(proj, cw, cb, w_a, b_a, w_x, b_x, lam, seq):
    t_tok = proj.shape[0]
    nc = seq // TC

    def body(x_ref, cw_ref, cb_ref, wa_ref, ba_ref, wx_ref, bx_ref, lam_ref, hs_ref, xe_s, a_s, u_s, h_s):
        c = pl.program_id(1)

        @pl.when(c == 0)
        def _():
            xe_s[0:8, :] = jnp.zeros((8, D_MODEL), F32)
            h_s[...] = jnp.zeros_like(h_s)

        xe_s[8:8 + TC, :] = x_ref[...]
        xc = _conv_taps(xe_s, cw_ref, cb_ref)
        sp = _softplus_neg(lam_ref[...])
        for blk in range(LRU_BLOCKS):
            cols = slice(blk * LRU_BLOCK_W, (blk + 1) * LRU_BLOCK_W)
            _, _, ig, a, mult = _lru_gates(xc, blk, wa_ref, wx_ref, ba_ref, bx_ref, sp)
            a_s[:, cols] = a
            u_s[:, cols] = mult * ig * xc[:, cols]

        def step(t, h):
            h = a_s[pl.ds(t, 1), :] * h + u_s[pl.ds(t, 1), :]
            hs_ref[pl.ds(t, 1), :] = h
            return h

        h_s[0:1, :] = lax.fori_loop(0, TC, step, h_s[0:1, :], unroll=8)
        xe_s[0:8, :] = xe_s[TC:TC + 8, :]

    full = lambda shape: pl.BlockSpec(shape, lambda b, c: (0,) * len(shape))
    return _pallas(
        body, name="lru_fwd", grid=(t_tok // seq, nc),
        in_specs=[pl.BlockSpec((TC, D_MODEL), lambda b, c: (b * nc + c, 0)), full((4, D_MODEL)), full((1, D_MODEL)),
                  full((LRU_BLOCKS, LRU_BLOCK_W, LRU_BLOCK_W)), full((1, D_MODEL)),
                  full((LRU_BLOCKS, LRU_BLOCK_W, LRU_BLOCK_W)), full((1, D_MODEL)), full((1, D_MODEL))],
        out_specs=pl.BlockSpec((TC, D_MODEL), lambda b, c: (b * nc + c, 0)),
        out_shape=jax.ShapeDtypeStruct((t_tok, D_MODEL), F32),
        scratch_shapes=[pltpu.VMEM((TC + 8, D_MODEL), F32), pltpu.VMEM((TC, D_MODEL), F32),
                        pltpu.VMEM((TC, D_MODEL), F32), pltpu.VMEM((8, D_MODEL), F32)],
        compiler_params=_cp(("arbitrary", "arbitrary"), VMEM_MID))(proj, cw, cb, w_a, b_a, w_x, b_x, lam)


def _lru_bwd(proj, hs, dyh, cw, cb, w_a, b_a, w_x, b_x, lam, seq):
    t_tok = proj.shape[0]
    nc = seq // TC

    def body(x_ref, xh_ref, g_ref, hs_ref, hh_ref, dy_ref, cw_ref, cb_ref, wa_ref, ba_ref, wx_ref, bx_ref, lam_ref,
             dp_ref, dcw_ref, dvec_ref, dwa_ref, dwx_ref,
             xe_s, he_s, de_s, a_s, r_s, i_s, m_s, dh_s, carry_s):
        b, cr = pl.program_id(0), pl.program_id(1)
        c = nc - 1 - cr

        @pl.when(jnp.logical_and(b == 0, cr == 0))
        def _():
            dcw_ref[...] = jnp.zeros_like(dcw_ref)
            dvec_ref[...] = jnp.zeros_like(dvec_ref)
            dwa_ref[...] = jnp.zeros_like(dwa_ref)
            dwx_ref[...] = jnp.zeros_like(dwx_ref)

        @pl.when(cr == 0)
        def _():
            carry_s[...] = jnp.zeros_like(carry_s)
            de_s[TC:TC + 8, :] = jnp.zeros((8, D_MODEL), F32)

        first = c == 0
        xe_s[0:8, :] = jnp.where(first, 0.0, xh_ref[...])
        xe_s[8:8 + TC, :] = x_ref[...]
        he_s[0:8, :] = jnp.where(first, 0.0, hh_ref[...])
        he_s[8:8 + TC, :] = hs_ref[...]
        xc = _conv_taps(xe_s, cw_ref, cb_ref)
        lam_v = lam_ref[...]
        sp = _softplus_neg(lam_v)
        for blk in range(LRU_BLOCKS):
            cols = slice(blk * LRU_BLOCK_W, (blk + 1) * LRU_BLOCK_W)
            _, r, ig, a, mult = _lru_gates(xc, blk, wa_ref, wx_ref, ba_ref, bx_ref, sp)
            a_s[:, cols], r_s[:, cols], i_s[:, cols], m_s[:, cols] = a, r, ig, mult

        gt = g_ref[...]
        dyh = dy_ref[...]
        dh_s[...] = dyh * _silu(gt)
        dp_ref[:, D_MODEL:] = (dyh * hs_ref[...] * _dsilu(gt)).astype(BF16)

        def step(k, carry):
            t = TC - 1 - k
            dh = dh_s[pl.ds(t, 1), :] + carry
            dh_s[pl.ds(t, 1), :] = dh
            return a_s[pl.ds(t, 1), :] * dh

        carry_s[0:1, :] = lax.fori_loop(0, TC, step, carry_s[0:1, :], unroll=8)

        hprev = he_s[7:7 + TC, :]
        for blk in range(LRU_BLOCKS):
            cols = slice(blk * LRU_BLOCK_W, (blk + 1) * LRU_BLOCK_W)
            xcb = xc[:, cols]
            a, r, ig, mult, dh = a_s[:, cols], r_s[:, cols], i_s[:, cols], m_s[:, cols], dh_s[:, cols]
            spb = sp[:, cols]
            dmult = dh * ig * xcb
            di = dh * mult * xcb
            dxc = dh * mult * ig
            dla = dh * hprev[:, cols] * a - dmult * (a * a) / jnp.maximum(mult, 1e-20)
            dr = dla * (-LRU_C * spb)
            dsp = jnp.sum(dla * (-LRU_C * r), axis=0, keepdims=True)
            dga = dr * r * (1.0 - r)
            dgx = di * ig * (1.0 - ig)
            dga_b, dgx_b = dga.astype(BF16), dgx.astype(BF16)
            xb = xcb.astype(BF16)
            dxc = dxc + _nt(dga_b, wa_ref[blk].astype(BF16)) + _nt(dgx_b, wx_ref[blk].astype(BF16))
            dwa_ref[blk] += _tn(xb, dga_b)
            dwx_ref[blk] += _tn(xb, dgx_b)
            dvec_ref[1:2, cols] += jnp.sum(dga, axis=0, keepdims=True)
            dvec_ref[2:3, cols] += jnp.sum(dgx, axis=0, keepdims=True)
            dvec_ref[3:4, cols] += dsp * (-1.0 / (1.0 + jnp.exp(lam_v[:, cols])))
            de_s[0:TC, cols] = dxc

        dxc = de_s[0:TC, :]
        dvec_ref[0:1, :] += jnp.sum(dxc, axis=0, keepdims=True)
        dxr = dxc * cw_ref[3:4, :]
        dcw_ref[3:4, :] += jnp.sum(dxc * xe_s[8:8 + TC, :], axis=0, keepdims=True)
        for k in range(1, 4):
            dxr = dxr + de_s[k:k + TC, :] * cw_ref[3 - k:4 - k, :]
            dcw_ref[3 - k:4 - k, :] += jnp.sum(dxc * xe_s[8 - k:8 - k + TC, :], axis=0, keepdims=True)
        dp_ref[:, :D_MODEL] = dxr.astype(BF16)
        de_s[TC:TC + 8, :] = de_s[0:8, :]

    chunk = lambda col: pl.BlockSpec((TC, D_MODEL), lambda b, cr: (b * nc + nc - 1 - cr, col))
    halo = lambda col: pl.BlockSpec(
        (8, D_MODEL), lambda b, cr: (jnp.maximum((b * nc + nc - 1 - cr) * (TC // 8) - 1, 0), col))
    full = lambda shape: pl.BlockSpec(shape, lambda b, cr: (0,) * len(shape))
    wblk = (LRU_BLOCKS, LRU_BLOCK_W, LRU_BLOCK_W)
    return _pallas(
        body, name="lru_bwd", grid=(t_tok // seq, nc),
        in_specs=[chunk(0), halo(0), chunk(1), chunk(0), halo(0), chunk(0),
                  full((4, D_MODEL)), full((1, D_MODEL)), full(wblk), full((1, D_MODEL)), full(wblk),
                  full((1, D_MODEL)), full((1, D_MODEL))],
        out_specs=[pl.BlockSpec((TC, 2 * D_MODEL), lambda b, cr: (b * nc + nc - 1 - cr, 0)),
                   full((8, D_MODEL)), full((8, D_MODEL)), full(wblk), full(wblk)],
        out_shape=[jax.ShapeDtypeStruct((t_tok, 2 * D_MODEL), BF16), jax.ShapeDtypeStruct((8, D_MODEL), F32),
                   jax.ShapeDtypeStruct((8, D_MODEL), F32), jax.ShapeDtypeStruct(wblk, F32),
                   jax.ShapeDtypeStruct(wblk, F32)],
        scratch_shapes=[pltpu.VMEM((TC + 8, D_MODEL), F32), pltpu.VMEM((TC + 8, D_MODEL), F32),
                        pltpu.VMEM((TC + 8, D_MODEL), F32)]
        + [pltpu.VMEM((TC, D_MODEL), F32)] * 5 + [pltpu.VMEM((8, D_MODEL), F32)],
        compiler_params=_cp(("arbitrary", "arbitrary"), VMEM_BIG),
    )(proj, proj, proj, hs, hs, dyh, cw, cb, w_a, b_a, w_x, b_x, lam)


def _final_loss(x, g, target):
    t_tok = x.shape[0]

    def body(x_ref, g_ref, t_ref, dx_ref, loss_ref, dg_ref):
        @pl.when(pl.program_id(0) == 0)
        def _():
            loss_ref[...] = jnp.zeros_like(loss_ref)
            dg_ref[...] = jnp.zeros_like(dg_ref)

        xv = x_ref[...]
        gv = g_ref[...]
        rstd = lax.rsqrt(jnp.mean(xv * xv, axis=-1, keepdims=True) + EPS)
        xhat = xv * rstd
        err = xhat * gv - t_ref[...]
        loss_ref[0:1, :] += jnp.sum(err * err, axis=0, keepdims=True) * (0.5 / D_MODEL)
        dout = err * (1.0 / D_MODEL)
        dg_ref[0:1, :] += jnp.sum(dout * xhat, axis=0, keepdims=True)
        dxhat = dout * gv
        dx_ref[...] = rstd * (dxhat - xhat * jnp.mean(dxhat * xhat, axis=-1, keepdims=True))

    row = pl.BlockSpec((TM, D_MODEL), lambda i: (i, 0))
    acc = pl.BlockSpec((8, D_MODEL), lambda i: (0, 0))
    return _pallas(body, name="final_loss", grid=(t_tok // TM,),
                   in_specs=[row, pl.BlockSpec((1, D_MODEL), lambda i: (0, 0)), row],
                   out_specs=[row, acc, acc],
                   out_shape=[jax.ShapeDtypeStruct((t_tok, D_MODEL), F32)] + [jax.ShapeDtypeStruct((8, D_MODEL), F32)] * 2,
                   compiler_params=_cp(("arbitrary",), VMEM_MID))(x, g, target)


def _adam_math(w, g, m, v):
    m_new = ADAM_B1 * m + (1.0 - ADAM_B1) * g
    v_new = ADAM_B2 * v + (1.0 - ADAM_B2) * (g * g)
    m_hat = m_new / (1.0 - ADAM_B1 ** ADAM_STEP)
    v_hat = v_new / (1.0 - ADAM_B2 ** ADAM_STEP)
    delta = -ADAM_LR * (m_hat / (jnp.sqrt(v_hat) + ADAM_EPS) + ADAM_WD * w)
    return delta, m_new, v_new


def _sum_leading(name, x, out_dtype=F32):
    n, rows, cols = x.shape
    tr = PACK_ROWS if rows % PACK_ROWS == 0 else rows

    def body(x_ref, o_ref):
        acc = x_ref[0].astype(F32)
        for d in range(1, n):
            acc = acc + x_ref[d].astype(F32)
        o_ref[...] = acc.astype(out_dtype)

    return _pallas(body, name=name, grid=(rows // tr,),
                   in_specs=[pl.BlockSpec((n, tr, cols), lambda i: (0, i, 0))],
                   out_specs=pl.BlockSpec((tr, cols), lambda i: (i, 0)),
                   out_shape=jax.ShapeDtypeStruct((rows, cols), out_dtype),
                   compiler_params=_cp(("arbitrary",), VMEM_MID))(x)


def _adamw(name, w, m, v, g=None, parts=None):
    rows, cols = w.shape
    tr = rows if rows <= 256 else 256

    def body(*refs):
        w_ref, m_ref, v_ref, g_in, g_ref, d_ref, mo_ref, vo_ref = refs
        if parts is None:
            gv = g_in[...]
        else:
            acc = g_in[0].astype(F32)
            for d in range(1, parts.shape[0]):
                acc = acc + g_in[d].astype(F32)
            gv = acc[:, :cols]
        delta, m_new, v_new = _adam_math(w_ref[...], gv, m_ref[...], v_ref[...])
        g_ref[...] = gv
        d_ref[...] = delta
        mo_ref[...] = m_new
        vo_ref[...] = v_new

    row = pl.BlockSpec((tr, cols), lambda i: (i, 0))
    if parts is None:
        g_spec, g_arg = row, g
    else:
        g_spec, g_arg = pl.BlockSpec((parts.shape[0], tr, parts.shape[2]), lambda i: (0, i, 0)), parts
    return _pallas(body, name=name, grid=(rows // tr,), in_specs=[row, row, row, g_spec], out_specs=[row] * 4,
                   out_shape=[jax.ShapeDtypeStruct((rows, cols), F32)] * 4,
                   compiler_params=_cp(("arbitrary",), VMEM_MID))(w, m, v, g_arg)


def _pack_rows(arrs):
    rows, meta, total = [], [], 0
    for a in arrs:
        flat = a.reshape(-1)
        nrow = -(-flat.shape[0] // 1024) * 8
        rows.append(jnp.pad(flat, (0, nrow * 128 - flat.shape[0])).reshape(nrow, 128))
        meta.append((a.shape, flat.shape[0], nrow))
        total += nrow
    tail = -total % PACK_ROWS
    if tail:
        rows.append(jnp.zeros((tail, 128), F32))
    return jnp.concatenate(rows, axis=0), meta


def _unpack_rows(packed, meta):
    out, r0 = [], 0
    for shape, size, nrow in meta:
        out.append(packed[r0:r0 + nrow].reshape(-1)[:size].reshape(shape))
        r0 += nrow
    return out


WEIGHTS = ["rel_bias", "norm_g", "ada_w", "ada_b", "attn_w_in", "attn_sinks", "attn_b_f", "attn_w_out", "lru_w_in",
           "lru_conv_w", "lru_conv_b", "lru_w_a", "lru_b_a", "lru_w_x", "lru_b_x", "lru_lambda", "lru_w_out", "final_g"]
BIG = ["ada_w", "attn_w_in", "attn_w_out", "lru_w_in", "lru_w_out"]
PACK_ROWS = 256


def kernel(x, c, rel_bias, norm_g, ada_w, ada_b, attn_w_in, attn_sinks, attn_b_f, attn_w_out, lru_w_in, lru_conv_w, lru_conv_b, lru_w_a, lru_b_a, lru_w_x, lru_b_x, lru_lambda, lru_w_out, final_g, loss_target, m_rel_bias, m_norm_g, m_ada_w, m_ada_b, m_attn_w_in, m_attn_sinks, m_attn_b_f, m_attn_w_out, m_lru_w_in, m_lru_conv_w, m_lru_conv_b, m_lru_w_a, m_lru_b_a, m_lru_w_x, m_lru_b_x, m_lru_lambda, m_lru_w_out, m_final_g, v_rel_bias, v_norm_g, v_ada_w, v_ada_b, v_attn_w_in, v_attn_sinks, v_attn_b_f, v_attn_w_out, v_lru_w_in, v_lru_conv_w, v_lru_conv_b, v_lru_w_a, v_lru_b_a, v_lru_w_x, v_lru_b_x, v_lru_lambda, v_lru_w_out, v_final_g):
    nseq, seq, _ = x.shape
    t_tok = nseq * seq
    me = 4 * lax.axis_index("x") + 2 * lax.axis_index("y") + lax.axis_index("c")
    x0 = x.reshape(t_tok, D_MODEL)
    target = loss_target.reshape(t_tok, D_MODEL)

    w_in_pad = jnp.pad(attn_w_in[0].astype(BF16), ((0, 0), (0, SHARD_W_PAD - SHARD_W_IN)))
    vec_shard = jnp.concatenate([lru_conv_w[0], lru_conv_b, lru_b_a, lru_b_x, lru_lambda], axis=0)
    g_w_in, g_vec, g_c = _exchange("gather_first", [w_in_pad, vec_shard, c], [])
    later_w = [attn_w_out[0].astype(BF16), lru_w_in[0].astype(BF16), lru_w_out[0].astype(BF16)]
    later_handle, later_token = _exchange_start("gather_later_start", later_w, [], after=g_vec)
    w_full = jnp.transpose(g_w_in[:, :, :SHARD_W_IN], (1, 0, 2)).reshape(D_MODEL, N_DEV * SHARD_W_IN)
    w_aq, w_ak, w_av = w_full[:, 0:512], w_full[:, 512:640], w_full[:, 640:768]
    w_bq, w_bk, w_bv = w_full[:, 768:1280], w_full[:, 1280:1792], w_full[:, 1792:2304]
    w_f, w_gate = w_full[:, 2304:2312], w_full[:, 2312:3336]
    w_main = jnp.concatenate([w_bq, w_bk, w_bv, w_aq, w_gate, w_ak, w_av], axis=1)
    wf_t = jnp.transpose(w_f)
    vec_full = jnp.transpose(g_vec, (1, 0, 2)).reshape(8, D_MODEL)
    conv_w, conv_b, b_a, b_x, lam = vec_full[0:4], vec_full[4:5], vec_full[5:6], vec_full[6:7], vec_full[7:8]
    c_all = g_c.reshape(N_DEV * nseq, D_MODEL)

    ncol = ada_w.shape[2]
    ada_b_slice = lax.dynamic_slice(ada_b.reshape(2, N_DEV, ncol), (0, me, 0), (2, 1, ncol))
    mod_part = _ada_mod(c_all, ada_w, ada_b_slice)
    (g_mod,) = _exchange("gather_mod", [mod_part], [])
    mine = lax.dynamic_slice(g_mod, (0, 0, me * nseq, 0), (N_DEV, 2, nseq, ncol))
    mod = jnp.transpose(mine, (1, 2, 0, 3)).reshape(2, nseq, 3 * D_MODEL)
    shift = [mod[l, :, 0:D_MODEL].reshape(nseq, 1, D_MODEL) for l in range(2)]
    scale = [mod[l, :, D_MODEL:2 * D_MODEL].reshape(nseq, 1, D_MODEL) for l in range(2)]
    gmod = [mod[l, :, 2 * D_MODEL:].reshape(nseq, 1, D_MODEL) for l in range(2)]

    onehot = _bucket_onehot()
    bias = _bias_expand(jnp.transpose(rel_bias), onehot).reshape(N_HEADS, BLOCK, 2 * BLOCK)
    sinks = attn_sinks.reshape(N_HEADS)
    b_f = attn_b_f.reshape(N_HEADS, 1)
    norm_g0 = norm_g[0:1] + later_token[0:1, 0:1]
    h0, qkvg, fl_t = _norm_proj("norm_proj0", x0, norm_g0, shift[0], scale[0], w_main, seq, BF16, wf_t=wf_t)
    f_row, f_col = _fox_prep(fl_t, b_f, seq)
    a_out, lse_a = _swa_fwd(qkvg, bias, sinks, seq)
    q_aug, k_aug, kt_aug, vt = _fox_aug(qkvg, f_col, seq)
    b_out, lse_b = _fox_fwd_t(q_aug, k_aug, vt, seq)
    g_later = _exchange_wait("gather_later_wait", later_handle, after=lse_b)
    w_out0, g_lru_in, w_out1 = (_with_own(g, w, me) for g, w in zip(g_later, later_w))
    w_out0, w_out1 = w_out0.reshape(D_MODEL, D_MODEL), w_out1.reshape(D_MODEL, D_MODEL)
    yg0, y0, x1 = _out_proj("out_proj0", [a_out, b_out], qkvg, C_GATE // D_MODEL, w_out0, x0, gmod[0], seq)

    h1, proj1 = _norm_proj("norm_proj1", x1, norm_g[1:2], shift[1], scale[1], g_lru_in, seq, F32)
    hs = _lru_fwd(proj1, conv_w, conv_b, lru_w_a[0], b_a, lru_w_x[0], b_x, lam, seq)
    yg1, y1, x2 = _out_proj("out_proj1", [hs], proj1, 1, w_out1, x1, gmod[1], seq)

    dx2, loss_rows, dfinal_rows = _final_loss(x2, final_g.reshape(1, D_MODEL), target)
    loss = lax.psum(jnp.sum(loss_rows[0]), ("x", "y", "c"))

    dy1, dgm1, dyh = _out_proj_bwd("out_proj1_bwd", dx2, gmod[1], y1, w_out1, seq)
    dproj1, dcw, dvec, dw_a, dw_x = _lru_bwd(proj1, hs, dyh, conv_w, conv_b, lru_w_a[0], b_a, lru_w_x[0], b_x, lam, seq)
    dx1, dss1, dg1 = _norm_bwd("norm1_bwd", [(dproj1, 0)], g_lru_in, x1, norm_g[1:2], scale[1], dx2, seq)
    (p_w_out1,) = _dw("dw_out1", yg1, [dy1])
    (p_lru_in,) = _dw("dw_lru_in", h1, [dproj1], blocked=2 * D_MODEL // N_DEV)

    rows_out = D_MODEL // N_DEV
    gpack1, gmeta1 = _pack_rows([dcw[0:4], dvec[0:4], dg1[0], dfinal_rows[0]])
    dwax = jnp.stack([dw_a, dw_x]).astype(BF16)
    own1 = [gpack1, dwax, p_lru_in, p_w_out1.reshape(N_DEV, rows_out, D_MODEL)]
    grads1_handle, grads1_token = _exchange_start("grads1_start", own1[:2], own1[2:], after=dx1)

    gmod0 = gmod[0] + grads1_token[0:1, 0:1]
    dy0, dgm0, du_a, du_b, dgate = _out_proj_bwd("out_proj0_bwd", dx1, gmod0, y0, w_out0, seq,
                                                  attn=(a_out, b_out, qkvg))
    dq_a, dkv_a, dbias, dsink = _swa_bwd(qkvg, du_a, a_out, lse_a, bias, sinks, seq)
    dq_b, dk_b, dv_b, df4 = _fox_bwd_t(q_aug, k_aug, kt_aug, qkvg, du_b, b_out, lse_b, seq)
    dfl_t, db_f = _fox_post(df4.reshape(N_HEADS, t_tok), fl_t, b_f, seq)
    parts0 = [(dq_b, C_BQ), (dk_b, C_BK), (dv_b, C_BV), (dq_a, C_AQ), (dgate, C_GATE), (dkv_a, C_AK)]
    (p_w_out0,) = _dw("dw_out0", yg0, [dy0])
    pw_bq, pw_bk, pw_bv, pw_aq, pw_gate, pw_akv = _dw("dw_attn_in", h0, [p for p, _ in parts0])
    pw_f = _dw_rows("dw_f", dfl_t, h0)

    p_w_in = jnp.concatenate([pw_aq, pw_akv, pw_bq, pw_bk, pw_bv, jnp.transpose(pw_f).astype(BF16), pw_gate], axis=1)
    p_w_in = jnp.transpose(p_w_in.reshape(D_MODEL, N_DEV, SHARD_W_IN), (1, 0, 2))
    p_w_in = jnp.pad(p_w_in, ((0, 0), (0, 0), (0, SHARD_W_PAD - SHARD_W_IN)))
    own0 = [p_w_in, p_w_out0.reshape(N_DEV, rows_out, D_MODEL)]
    landed1 = _exchange_wait("grads1_wait", grads1_handle, after=p_w_in)
    grads0_handle, grads0_token = _exchange_start("grads0_start", [], own0, after=landed1[0])
    scale0 = scale[0] + grads0_token[0:1, 0:1]
    dx0, dss0, dg0 = _norm_bwd("norm0_bwd", parts0, w_main, x0, norm_g[0:1], scale0, dx1, seq,
                               rows_part=(dfl_t, wf_t))
    dbias_t = _bias_reduce(dbias.reshape(N_HEADS, BLOCK * 2 * BLOCK), onehot)

    gpack0, gmeta0 = _pack_rows([jnp.transpose(dbias_t), dg0[0], dsink[:, 0], db_f[:, 0]])
    dmod = jnp.stack([jnp.concatenate([dss[:, 0], dss[:, 1], dgm[:, 0]], axis=1)
                      for dss, dgm in ((dss0, dgm0), (dss1, dgm1))], axis=1)
    g_small0, g_dmod = _exchange("exchange_small", [gpack0, dmod], [])
    landed0 = _exchange_wait("grads0_wait", grads0_handle, after=g_small0)
    r_w_in, r_w_out0 = (_with_own(g, lax.dynamic_index_in_dim(a, me, 0, keepdims=False), me)
                        for g, a in zip(landed0, own0))
    g_small1, g_dwax = (_with_own(g, a, me) for g, a in zip(landed1[:2], own1[:2]))
    r_lru_in, r_w_out1 = (_with_own(g, lax.dynamic_index_in_dim(a, me, 0, keepdims=False), me)
                          for g, a in zip(landed1[2:], own1[2:]))

    d_rel, d_g0, d_sinks, d_b_f = _unpack_rows(_sum_leading("sum_small0", g_small0), gmeta0)
    d_cw, d_vec, d_g1, d_final_g = _unpack_rows(_sum_leading("sum_small1", g_small1), gmeta1)
    d_norm_g = jnp.stack([d_g0, d_g1])
    d_wax = _sum_leading("sum_dwax", g_dwax.reshape(N_DEV, 2 * LRU_BLOCKS * LRU_BLOCK_W, LRU_BLOCK_W))
    d_wa, d_wx = d_wax[:LRU_BLOCKS * LRU_BLOCK_W], d_wax[LRU_BLOCKS * LRU_BLOCK_W:]
    cols = lambda a: lax.dynamic_slice(a, (0, me * LRU_BLOCK_W), (a.shape[0], LRU_BLOCK_W))
    dmod_all = g_dmod.reshape(N_DEV * nseq, 2 * 3 * D_MODEL)
    d_ada_b = _sum_leading("sum_ada_b", dmod_all.reshape(N_DEV * nseq, 2 * 3 * D_MODEL // 128, 128)).reshape(2, 3 * D_MODEL)
    dmod_slice = lax.dynamic_slice(dmod_all.reshape(N_DEV * nseq, 2, N_DEV, ncol), (0, 0, me, 0),
                                   (N_DEV * nseq, 2, 1, ncol)).reshape(N_DEV * nseq, 2, ncol)
    d_ada_w = _ada_w_grad(c_all, jnp.transpose(dmod_slice, (1, 0, 2)))

    given = dict(
        rel_bias=(rel_bias, m_rel_bias, v_rel_bias), norm_g=(norm_g, m_norm_g, v_norm_g),
        ada_w=(ada_w, m_ada_w, v_ada_w), ada_b=(ada_b, m_ada_b, v_ada_b),
        attn_w_in=(attn_w_in, m_attn_w_in, v_attn_w_in), attn_sinks=(attn_sinks, m_attn_sinks, v_attn_sinks),
        attn_b_f=(attn_b_f, m_attn_b_f, v_attn_b_f), attn_w_out=(attn_w_out, m_attn_w_out, v_attn_w_out),
        lru_w_in=(lru_w_in, m_lru_w_in, v_lru_w_in), lru_conv_w=(lru_conv_w, m_lru_conv_w, v_lru_conv_w),
        lru_conv_b=(lru_conv_b, m_lru_conv_b, v_lru_conv_b), lru_w_a=(lru_w_a, m_lru_w_a, v_lru_w_a),
        lru_b_a=(lru_b_a, m_lru_b_a, v_lru_b_a), lru_w_x=(lru_w_x, m_lru_w_x, v_lru_w_x),
        lru_b_x=(lru_b_x, m_lru_b_x, v_lru_b_x), lru_lambda=(lru_lambda, m_lru_lambda, v_lru_lambda),
        lru_w_out=(lru_w_out, m_lru_w_out, v_lru_w_out), final_g=(final_g, m_final_g, v_final_g))
    results = {}

    def big(name, shape2d, g=None, parts=None):
        w, m, v = (a.reshape(shape2d) for a in given[name])
        outs = _adamw("adamw_" + name, w, m, v, g=g, parts=parts)
        results[name] = tuple(o.reshape(given[name][0].shape) for o in outs)

    big("ada_w", (2 * D_MODEL, ncol), g=d_ada_w.reshape(2 * D_MODEL, ncol))
    big("attn_w_in", (D_MODEL, SHARD_W_IN), parts=r_w_in)
    big("attn_w_out", (rows_out, D_MODEL), parts=r_w_out0)
    big("lru_w_in", (D_MODEL, 2 * D_MODEL // N_DEV), parts=r_lru_in)
    big("lru_w_out", (rows_out, D_MODEL), parts=r_w_out1)

    small_grads = dict(
        rel_bias=d_rel, norm_g=d_norm_g, ada_b=d_ada_b, attn_sinks=d_sinks.reshape(1, N_HEADS),
        attn_b_f=d_b_f.reshape(1, N_HEADS), lru_conv_w=cols(d_cw).reshape(1, 4, LRU_BLOCK_W),
        lru_conv_b=cols(d_vec[0:1]), lru_w_a=d_wa.reshape(lru_w_a.shape), lru_b_a=cols(d_vec[1:2]),
        lru_w_x=d_wx.reshape(lru_w_x.shape), lru_b_x=cols(d_vec[2:3]), lru_lambda=cols(d_vec[3:4]),
        final_g=d_final_g)
    small = [n for n in WEIGHTS if n not in BIG]
    wpack, smeta = _pack_rows([given[n][0] for n in small])
    mpack, _ = _pack_rows([given[n][1] for n in small])
    vpack, _ = _pack_rows([given[n][2] for n in small])
    gpack2, _ = _pack_rows([small_grads[n] for n in small])
    packs = _adamw("adamw_small", wpack, mpack, vpack, g=gpack2)
    unpacked = [_unpack_rows(p, smeta) for p in packs]
    for k, n in enumerate(small):
        results[n] = tuple(unpacked[j][k] for j in range(4))

    grad_x = dx0.reshape(x.shape)
    out = [loss, grad_x]
    for j in range(4):
        out += [results[n][j] for n in WEIGHTS]
    return tuple(out)
```

```python
import functools
import math

import jax
import jax.numpy as jnp
from jax import lax
from jax.experimental import pallas as pl
from jax.experimental.pallas import tpu as pltpu

F32 = jnp.float32
BF16 = jnp.bfloat16
HI = lax.Precision.HIGHEST
MESH = pl.DeviceIdType.MESH

N_DEV = 8
D_MODEL = 1024
HEAD_DIM = 64
N_HEADS = 8
KV_GROUP = 4
BLOCK = 128
REL_BUCKETS = 32
REL_MAX_EXACT = 16
REL_MAX_DIST = 128
LRU_BLOCKS = 8
LRU_BLOCK_W = 128
LRU_C = 8.0
EPS = 1e-6
SCALE = HEAD_DIM ** -0.5
NEG = -1e30

ADAM_LR = 0.001
ADAM_B1 = 0.9
ADAM_B2 = 0.999
ADAM_EPS = 1e-08
ADAM_WD = 0.01
ADAM_STEP = 10

C_BQ, C_BK, C_BV, C_AQ, C_GATE, C_AK, C_AV = 0, 512, 1024, 1536, 2048, 3072, 3200
N_MAIN = 3328
SHARD_W_IN = 417
SHARD_W_PAD = 512

TM = 256
TQ = 256
TK = 128
TKB = 256
TC = 256
VMEM_BIG = 56 * 1024 * 1024
VMEM_MID = 40 * 1024 * 1024


def _pallas(body, **kw):
    return pl.pallas_call(body, **kw)


def _cp(sem=None, vmem=None):
    kw = {}
    if sem is not None:
        kw["dimension_semantics"] = sem
    if vmem is not None:
        kw["vmem_limit_bytes"] = vmem
    return pltpu.CompilerParams(**kw)


def _nn(a, b, precision=None):
    return jnp.dot(a, b, preferred_element_type=F32, precision=precision)


def _nt(a, b, precision=None):
    return lax.dot_general(a, b, (((1,), (1,)), ((), ())), preferred_element_type=F32, precision=precision)


def _tn(a, b, precision=None):
    return lax.dot_general(a, b, (((0,), (0,)), ((), ())), preferred_element_type=F32, precision=precision)


def _sigmoid(x):
    return 1.0 / (1.0 + jnp.exp(-x))


def _silu(x):
    return x * _sigmoid(x)


def _dsilu(x):
    s = _sigmoid(x)
    return s * (1.0 + x * (1.0 - s))


def _neg_expm1(x):
    poly = x * (1.0 + x * (0.5 + x * (1.0 / 6.0 + x * (1.0 / 24.0))))
    return -jnp.where(jnp.abs(x) < 0.05, poly, jnp.exp(x) - 1.0)


def _col(tile, idx):
    lane = lax.broadcasted_iota(jnp.int32, tile.shape, 1)
    return jnp.sum(jnp.where(lane == idx, tile, 0.0), axis=1, keepdims=True)


def _row(tile, idx):
    sub = lax.broadcasted_iota(jnp.int32, tile.shape, 0)
    return jnp.sum(jnp.where(sub == idx, tile, 0.0), axis=0, keepdims=True)


def _exchange(name, gathers, scatters, axes=("x", "y", "c"), chunks=1):
    ng, n = len(gathers), len(gathers) + len(scatters)
    ins = list(gathers) + list(scatters)
    group = 2 ** len(axes)

    def body(*refs):
        in_refs, out_refs = refs[:n], refs[n:2 * n]
        send_sems, recv_sems, loc_sems = refs[2 * n:]
        coord = {a: lax.axis_index(a) for a in ("x", "y", "c")}

        def member(r):
            pc = dict(coord)
            idx = 0
            for k, a in enumerate(axes):
                if r & (1 << (len(axes) - 1 - k)):
                    pc[a] = 1 - coord[a]
                idx = 2 * idx + pc[a]
            return (pc["x"], pc["y"], pc["c"]), idx

        _, me = member(0)

        def peer(r):
            return member(r)

        local, sends, recvs = [], [], []
        for k in range(n):
            mine = in_refs[k] if k < ng else in_refs[k].at[me]
            cp = pltpu.make_async_copy(mine, out_refs[k].at[me], loc_sems.at[k])
            cp.start()
            local.append(cp)
            lead = mine.shape[0]
            nchunk = max(q for q in range(1, chunks + 1) if lead % q == 0)
            step = lead // nchunk
            for r in range(1, group):
                pid, pidx = peer(r)
                src = in_refs[k] if k < ng else in_refs[k].at[pidx]
                for q in range(nchunk):
                    rows = pl.ds(q * step, step)
                    sems = dict(send_sem=send_sems.at[r - 1, k, q], recv_sem=recv_sems.at[r - 1, k, q],
                                device_id=pid, device_id_type=MESH)
                    snd = pltpu.make_async_remote_copy(src_ref=src.at[rows], dst_ref=out_refs[k].at[me].at[rows], **sems)
                    snd.start()
                    sends.append(snd)
                    recvs.append(pltpu.make_async_remote_copy(
                        src_ref=src.at[rows], dst_ref=out_refs[k].at[pidx].at[rows], **sems))
        for rc in recvs:
            rc.wait_recv()
        for snd in sends:
            snd.wait_send()
        for cp in local:
            cp.wait()

    out_shape = [jax.ShapeDtypeStruct((group,) + a.shape, a.dtype) for a in gathers]
    out_shape += [jax.ShapeDtypeStruct(a.shape, a.dtype) for a in scatters]
    any_spec = pl.BlockSpec(memory_space=pl.ANY)
    return _pallas(
        body, name=name, out_shape=out_shape,
        in_specs=[any_spec] * n, out_specs=[any_spec] * n,
        scratch_shapes=[pltpu.SemaphoreType.DMA((group - 1, n, chunks)), pltpu.SemaphoreType.DMA((group - 1, n, chunks)),
                        pltpu.SemaphoreType.DMA((n,))],
    )(*ins)


def _peer_of(r):
    x, y, c = lax.axis_index("x"), lax.axis_index("y"), lax.axis_index("c")
    px = 1 - x if r & 4 else x
    py = 1 - y if r & 2 else y
    pc = 1 - c if r & 1 else c
    return (px, py, pc), 4 * px + 2 * py + pc


def _split_copies(in_refs, land_refs, send_sems, recv_sems, ng, with_recv):
    _, me = _peer_of(0)
    pairs = []
    for k, (src_ref, land) in enumerate(zip(in_refs, land_refs)):
        for r in range(1, N_DEV):
            pid, pidx = _peer_of(r)
            src = src_ref if k < ng else src_ref.at[pidx]
            slot = (N_DEV - 1) * k + r - 1
            sems = dict(send_sem=send_sems.at[slot], recv_sem=recv_sems.at[slot], device_id=pid, device_id_type=MESH)
            send = pltpu.make_async_remote_copy(src_ref=src, dst_ref=land.at[me], **sems)
            recv = pltpu.make_async_remote_copy(src_ref=src, dst_ref=land.at[pidx], **sems) if with_recv else None
            pairs.append((send, recv))
    return pairs


def _exchange_start(name, gathers, scatters, after):
    ng, n = len(gathers), len(gathers) + len(scatters)
    ins = list(gathers) + list(scatters)
    lands = [jax.ShapeDtypeStruct((N_DEV,) + a.shape, a.dtype) for a in gathers]
    lands += [jax.ShapeDtypeStruct(a.shape, a.dtype) for a in scatters]

    def body(*refs):
        in_refs, land_refs = refs[:n], refs[n:2 * n]
        send_sems, recv_sems = refs[2 * n + 1:2 * n + 3]
        token = refs[-1]
        for send, _ in _split_copies(in_refs, land_refs, send_sems, recv_sems, ng, False):
            send.start()
        token[...] = jnp.zeros_like(token)

    hbm = pl.BlockSpec(memory_space=pltpu.HBM)
    sem = pl.BlockSpec(memory_space=pltpu.SEMAPHORE)
    sem_shape = pltpu.SemaphoreType.DMA(((N_DEV - 1) * n,))
    out_shape = [sem_shape, sem_shape] + [pltpu.HBM(a.shape, a.dtype) for a in ins]
    out_shape += [pltpu.HBM(l.shape, l.dtype) for l in lands] + [jax.ShapeDtypeStruct((8, 128), F32)]
    args = [pltpu.with_memory_space_constraint(a, pltpu.HBM) for a in ins]
    args += [pltpu.with_memory_space_constraint(lax.empty(l.shape, l.dtype), pltpu.HBM) for l in lands]
    outs = _pallas(
        body, name=name, out_shape=out_shape,
        in_specs=[hbm] * (2 * n) + [pl.BlockSpec(memory_space=pl.ANY)],
        out_specs=[sem, sem] + [hbm] * (2 * n) + [pl.BlockSpec(memory_space=pltpu.VMEM)],
        input_output_aliases={i: 2 + i for i in range(2 * n)},
        compiler_params=pltpu.CompilerParams(has_side_effects=pltpu.SideEffectType.DATAFLOW_SIDE_EFFECTING),
    )(*args, after)
    return (outs[0], outs[1], list(outs[2:2 + n]), list(outs[2 + n:2 + 2 * n]), ng), outs[-1]


def _exchange_wait(name, handle, after):
    send_sems, recv_sems, srcs, lands, ng = handle
    n = len(srcs)

    def body(*refs):
        in_refs, land_refs = refs[:n], refs[n:2 * n]
        send_ref, recv_ref = refs[2 * n:2 * n + 2]
        for send, recv in _split_copies(in_refs, land_refs, send_ref, recv_ref, ng, True):
            send.wait_send()
            recv.wait_recv()

    hbm = pl.BlockSpec(memory_space=pltpu.HBM)
    sem = pl.BlockSpec(memory_space=pltpu.SEMAPHORE)
    outs = _pallas(
        body, name=name, out_shape=[pltpu.HBM(a.shape, a.dtype) for a in srcs + lands],
        in_specs=[hbm] * (2 * n) + [sem, sem, pl.BlockSpec(memory_space=pl.ANY)],
        out_specs=[hbm] * (2 * n), input_output_aliases={i: i for i in range(2 * n)},
        compiler_params=pltpu.CompilerParams(has_side_effects=pltpu.SideEffectType.DATAFLOW_SIDE_EFFECTING),
    )(*srcs, *lands, send_sems, recv_sems, after)
    return list(outs[n:])


def _with_own(land, own, me):
    return lax.dynamic_update_slice(land, own[None], (me,) + (0,) * own.ndim)


def _ada_mod(c_all, ada_w, ada_b_slice):
    def body(c_ref, w_ref, b_ref, o_ref):
        ca = _silu(c_ref[...])
        for l in range(2):
            o_ref[l] = _nn(ca, w_ref[l], HI) + b_ref[l]

    return _pallas(body, name="ada_mod",
                   out_shape=jax.ShapeDtypeStruct((2, c_all.shape[0], ada_w.shape[2]), F32),
                   compiler_params=_cp(vmem=VMEM_MID))(c_all, ada_w, ada_b_slice)


def _ada_w_grad(c_all, dmod_slice):
    def body(c_ref, d_ref, o_ref):
        ca = _silu(c_ref[...])
        for l in range(2):
            o_ref[l] = _tn(ca, d_ref[l], HI)

    return _pallas(body, name="ada_w_grad",
                   out_shape=jax.ShapeDtypeStruct((2, D_MODEL, dmod_slice.shape[2]), F32),
                   compiler_params=_cp(vmem=VMEM_MID))(c_all, dmod_slice)


def _bucket_onehot():
    qi = jnp.arange(BLOCK)[:, None]
    kj = jnp.arange(2 * BLOCK)[None, :]
    rel = qi - kj + BLOCK
    n = jnp.maximum(rel, 0)
    nf = jnp.maximum(n, 1).astype(F32)
    large = REL_MAX_EXACT + (jnp.log(nf / REL_MAX_EXACT) / math.log(REL_MAX_DIST / REL_MAX_EXACT)
                             * (REL_BUCKETS - REL_MAX_EXACT)).astype(jnp.int32)
    large = jnp.minimum(large, REL_BUCKETS - 1)
    bucket = jnp.where(n < REL_MAX_EXACT, n, large).reshape(1, BLOCK * 2 * BLOCK)
    return (jnp.arange(REL_BUCKETS)[:, None] == bucket).astype(F32)


def _bias_expand(rel_bias_t, onehot):
    def body(r_ref, e_ref, o_ref):
        o_ref[...] = _nn(r_ref[...], e_ref[...], HI)

    return _pallas(body, name="bias_expand",
                   out_shape=jax.ShapeDtypeStruct((N_HEADS, onehot.shape[1]), F32),
                   compiler_params=_cp(vmem=VMEM_MID))(rel_bias_t, onehot)


def _bias_reduce(dbias, onehot):
    def body(d_ref, e_ref, o_ref):
        o_ref[...] = _nt(d_ref[...], e_ref[...], HI)

    return _pallas(body, name="bias_reduce",
                   out_shape=jax.ShapeDtypeStruct((N_HEADS, REL_BUCKETS), F32),
                   compiler_params=_cp(vmem=VMEM_MID))(dbias, onehot)


def _norm_proj(name, x, g, shift, scale, w, seq, out_dtype, wf_t=None):
    t_tok = x.shape[0]
    w3d = w.ndim == 3
    n_out = w.shape[0] * w.shape[2] if w3d else w.shape[1]
    cn = w.shape[2] if w3d else 256

    def body(x_ref, g_ref, sh_ref, sc_ref, w_ref, *rest):
        if wf_t is not None:
            wf_ref, h_ref, o_ref, fl_ref = rest
        else:
            h_ref, o_ref = rest
        xv = x_ref[...]
        rstd = lax.rsqrt(jnp.mean(xv * xv, axis=-1, keepdims=True) + EPS)
        h = (xv * rstd) * g_ref[...] * (1.0 + sc_ref[...]) + sh_ref[...]
        hb = h.astype(BF16)
        h_ref[...] = hb
        for j in range(n_out // cn):
            wj = w_ref[j] if w3d else w_ref[:, j * cn:(j + 1) * cn]
            o_ref[:, j * cn:(j + 1) * cn] = _nn(hb, wj).astype(out_dtype)
        if wf_t is not None:
            fl_ref[...] = _nt(wf_ref[...], hb)

    mod_spec = pl.BlockSpec((None, 1, D_MODEL), lambda i: (i * TM // seq, 0, 0))
    w_spec = (pl.BlockSpec(w.shape, lambda i: (0, 0, 0)) if w3d else pl.BlockSpec(w.shape, lambda i: (0, 0)))
    in_specs = [pl.BlockSpec((TM, D_MODEL), lambda i: (i, 0)), pl.BlockSpec((1, D_MODEL), lambda i: (0, 0)),
                mod_spec, mod_spec, w_spec]
    out_shape = [jax.ShapeDtypeStruct((t_tok, D_MODEL), BF16), jax.ShapeDtypeStruct((t_tok, n_out), out_dtype)]
    out_specs = [pl.BlockSpec((TM, D_MODEL), lambda i: (i, 0)), pl.BlockSpec((TM, n_out), lambda i: (i, 0))]
    args = [x, g, shift, scale, w]
    if wf_t is not None:
        in_specs.append(pl.BlockSpec(wf_t.shape, lambda i: (0, 0)))
        out_shape.append(jax.ShapeDtypeStruct((wf_t.shape[0], t_tok), F32))
        out_specs.append(pl.BlockSpec((wf_t.shape[0], TM), lambda i: (0, i)))
        args.append(wf_t)
    return _pallas(body, name=name, grid=(t_tok // TM,), in_specs=in_specs, out_specs=out_specs,
                   out_shape=out_shape, compiler_params=_cp(("arbitrary",), VMEM_BIG))(*args)


def _fox_prep(fl_t, b_f, seq):
    t_tok = fl_t.shape[1]
    ch = 256

    def body(fl_ref, bf_ref, fr_ref, fc_ref):
        z = fl_ref[...] + bf_ref[...]
        logf = jnp.minimum(z, 0.0) - jnp.log(1.0 + jnp.exp(-jnp.abs(z)))
        ri = lax.broadcasted_iota(jnp.int32, (ch, ch), 0)
        ci = lax.broadcasted_iota(jnp.int32, (ch, ch), 1)
        upper = (ri <= ci).astype(F32)
        eye = (ri == ci).astype(F32)
        carry = jnp.zeros((N_HEADS, 1), F32)
        for k in range(seq // ch):
            fk = _nn(logf[:, k * ch:(k + 1) * ch], upper, HI) + carry
            carry = fk[:, ch - 1:ch]
            fr_ref[:, k * ch:(k + 1) * ch] = fk
            padded = jnp.concatenate([fk, jnp.zeros((128 - N_HEADS, ch), F32)], axis=0)
            fc_ref[k * ch:(k + 1) * ch, :] = _nt(eye, padded, HI)

    return _pallas(
        body, name="fox_prep", grid=(t_tok // seq,),
        in_specs=[pl.BlockSpec((N_HEADS, seq), lambda b: (0, b)), pl.BlockSpec((N_HEADS, 1), lambda b: (0, 0))],
        out_specs=[pl.BlockSpec((N_HEADS, seq), lambda b: (0, b)), pl.BlockSpec((seq, 128), lambda b: (b, 0))],
        out_shape=[jax.ShapeDtypeStruct((N_HEADS, t_tok), F32), jax.ShapeDtypeStruct((t_tok, 128), F32)],
        compiler_params=_cp(("arbitrary",), VMEM_MID))(fl_t, b_f)


def _fox_post(df_row, fl_t, b_f, seq):
    t_tok = fl_t.shape[1]
    ch = 256

    def body(d_ref, fl_ref, bf_ref, o_ref, db_ref):
        @pl.when(pl.program_id(0) == 0)
        def _():
            db_ref[...] = jnp.zeros_like(db_ref)

        z = fl_ref[...] + bf_ref[...]
        sig_neg = 1.0 / (1.0 + jnp.exp(z))
        ri = lax.broadcasted_iota(jnp.int32, (ch, ch), 0)
        ci = lax.broadcasted_iota(jnp.int32, (ch, ch), 1)
        lower = (ri >= ci).astype(F32)
        carry = jnp.zeros((N_HEADS, 1), F32)
        tot = jnp.zeros((N_HEADS, 1), F32)
        for k in reversed(range(seq // ch)):
            dk = _nn(d_ref[:, k * ch:(k + 1) * ch], lower, HI) + carry
            carry = dk[:, 0:1]
            dfl = dk * sig_neg[:, k * ch:(k + 1) * ch]
            o_ref[:, k * ch:(k + 1) * ch] = dfl
            tot = tot + jnp.sum(dfl, axis=1, keepdims=True)
        db_ref[...] += jnp.broadcast_to(tot, db_ref.shape)

    return _pallas(
        body, name="fox_post", grid=(t_tok // seq,),
        in_specs=[pl.BlockSpec((N_HEADS, seq), lambda b: (0, b)), pl.BlockSpec((N_HEADS, seq), lambda b: (0, b)),
                  pl.BlockSpec((N_HEADS, 1), lambda b: (0, 0))],
        out_specs=[pl.BlockSpec((N_HEADS, seq), lambda b: (0, b)), pl.BlockSpec((N_HEADS, 128), lambda b: (0, 0))],
        out_shape=[jax.ShapeDtypeStruct((N_HEADS, t_tok), F32), jax.ShapeDtypeStruct((N_HEADS, 128), F32)],
        compiler_params=_cp(("arbitrary",), VMEM_MID))(df_row, fl_t, b_f)


def _eye(n, dtype):
    return (lax.broadcasted_iota(jnp.int32, (n, n), 0) == lax.broadcasted_iota(jnp.int32, (n, n), 1)).astype(dtype)


def _fox_aug(qkvg, f_col, seq):
    t_tok = qkvg.shape[0]
    ta = 256
    nkb = ta // TK

    def body(q_ref, k_ref, v_ref, fc_ref, qa_ref, ka_ref, kt_ref, vt_ref):
        ri = lax.broadcasted_iota(jnp.int32, (128, 128), 0)
        ci = lax.broadcasted_iota(jnp.int32, (128, 128), 1)
        eye = (ri == ci).astype(BF16)
        lane = lax.broadcasted_iota(jnp.int32, (ta, 128), 1)
        ones_q = jnp.where(jnp.logical_and(lane >= 64, lane < 67), 1.0, 0.0)
        ones_k = jnp.where(jnp.logical_and(lane >= 67, lane < 70), 1.0, 0.0)
        fc_tile = fc_ref[...]
        for p in range(N_HEADS // 2):
            q2 = q_ref[:, 128 * p:128 * (p + 1)]
            k2 = k_ref[:, 128 * p:128 * (p + 1)]
            vt = _nt(eye, v_ref[:, 128 * p:128 * (p + 1)]).astype(BF16)
            for kk in range(nkb):
                vt_ref[p, kk] = vt[:, kk * TK:(kk + 1) * TK]
            for e in range(2):
                h = 2 * p + e
                sel = jnp.logical_and(ri == ci + HEAD_DIM * e, ci < HEAD_DIM)
                f = _col(fc_tile, h)
                fh = f.astype(BF16).astype(F32)
                fm = (f - fh).astype(BF16).astype(F32)
                fl = (f - fh - fm).astype(BF16).astype(F32)
                qa = (_nn(q2, jnp.where(sel, SCALE, 0.0).astype(BF16)) + ones_q + jnp.where(lane == 67, fh, 0.0)
                      + jnp.where(lane == 68, fm, 0.0) + jnp.where(lane == 69, fl, 0.0))
                ka = (_nn(k2, jnp.where(sel, 1.0, 0.0).astype(BF16)) + ones_k - jnp.where(lane == 64, fh, 0.0)
                      - jnp.where(lane == 65, fm, 0.0) - jnp.where(lane == 66, fl, 0.0))
                qa_ref[h] = qa.astype(BF16)
                kab = ka.astype(BF16)
                ka_ref[h] = kab
                kt = _nt(eye, kab).astype(BF16)
                for kk in range(ta // TKB):
                    kt_ref[h, kk] = kt[:, kk * TKB:(kk + 1) * TKB]

    aug = jax.ShapeDtypeStruct((N_HEADS, t_tok, 128), BF16)
    return _pallas(
        body, name="fox_aug", grid=(t_tok // ta,),
        in_specs=[pl.BlockSpec((ta, 512), lambda i: (i, C_BQ // 512)), pl.BlockSpec((ta, 512), lambda i: (i, C_BK // 512)),
                  pl.BlockSpec((ta, 512), lambda i: (i, C_BV // 512)), pl.BlockSpec((ta, 128), lambda i: (i, 0))],
        out_specs=[pl.BlockSpec((N_HEADS, ta, 128), lambda i: (0, i, 0)), pl.BlockSpec((N_HEADS, ta, 128), lambda i: (0, i, 0)),
                   pl.BlockSpec((N_HEADS, ta // TKB, 128, TKB), lambda i: (0, i, 0, 0)),
                   pl.BlockSpec((N_HEADS // 2, nkb, 128, TK), lambda i: (0, i, 0, 0))],
        out_shape=[aug, aug, jax.ShapeDtypeStruct((N_HEADS, t_tok // TKB, 128, TKB), BF16),
                   jax.ShapeDtypeStruct((N_HEADS // 2, t_tok // TK, 128, TK), BF16)],
        compiler_params=_cp(("arbitrary",), VMEM_MID))(qkvg, qkvg, qkvg, f_col)


def _fox_fwd_t(q_aug, k_aug, vt, seq):
    t_tok = q_aug.shape[1]
    nq = seq // TQ
    ratio = TQ // TK

    def body(qa_ref, ka_ref, vt_ref, o_ref, lse_ref, ml_s, acc_s, st_s, p_s, al_s):
        i = pl.program_id(1)
        tpos = i * TQ + lax.broadcasted_iota(jnp.int32, (1, TQ), 1)
        eye = _eye(HEAD_DIM, BF16)
        for h in range(N_HEADS):
            ml_s[0, h] = jnp.full((1, TQ), NEG, F32)
            ml_s[1, h] = jnp.zeros((1, TQ), F32)
            acc_s[h] = jnp.zeros((HEAD_DIM, TQ), F32)
            p_s[1, h] = jnp.zeros((TK, TQ), BF16)
            al_s[1, h] = jnp.ones((1, TQ), F32)

        def scores(j):
            row0 = pl.multiple_of(j * TK, TK)
            for h in range(N_HEADS):
                st_s[j & 1, h] = _nt(ka_ref[h, pl.ds(row0, TK), :], qa_ref[h])

        def softmax(j, masked):
            slot = j & 1
            if masked:
                keep = (j * TK + lax.broadcasted_iota(jnp.int32, (TK, 1), 0)) <= tpos
            for h in range(N_HEADS):
                st = st_s[slot, h]
                if masked:
                    st = jnp.where(keep, st, NEG)
                m = ml_s[0, h]
                m_new = jnp.maximum(m, jnp.max(st, axis=0, keepdims=True))
                alpha = jnp.exp(m - m_new)
                pe = jnp.exp(st - m_new)
                ml_s[0, h] = m_new
                ml_s[1, h] = alpha * ml_s[1, h] + jnp.sum(pe, axis=0, keepdims=True)
                al_s[slot, h] = alpha
                p_s[slot, h] = pe.astype(BF16)

        def values(j):
            slot = j & 1
            jv = jnp.maximum(j, 0)
            for h in range(N_HEADS):
                p, e = divmod(h, 2)
                acc_s[h] = al_s[slot, h] * acc_s[h] + _nn(vt_ref[p, jv, e * HEAD_DIM:(e + 1) * HEAD_DIM, :], p_s[slot, h])

        def step(j, carry):
            values(j - 1)
            softmax(j, False)
            scores(j + 1)
            return carry

        last = ratio * i + ratio - 1
        scores(0)
        lax.fori_loop(0, ratio * i, step, 0)
        for kk in range(ratio):
            j = ratio * i + kk
            values(j - 1)
            softmax(j, True)
            if kk < ratio - 1:
                scores(j + 1)
        values(last)
        for p in range(N_HEADS // 2):
            outs = []
            for e in range(2):
                h = 2 * p + e
                l = ml_s[1, h]
                outs.append(_tn((acc_s[h] / l).astype(BF16), eye))
                lse_ref[p, e:e + 1, :] = ml_s[0, h] + jnp.log(l)
            o_ref[:, 128 * p:128 * (p + 1)] = jnp.concatenate(outs, axis=1).astype(BF16)

    return _pallas(
        body, name="fox_fwd", grid=(t_tok // seq, nq),
        in_specs=[pl.BlockSpec((N_HEADS, TQ, 128), lambda b, i: (0, b * nq + i, 0)),
                  pl.BlockSpec((N_HEADS, seq, 128), lambda b, i: (0, b, 0)),
                  pl.BlockSpec((N_HEADS // 2, seq // TK, 128, TK), lambda b, i: (0, b, 0, 0))],
        out_specs=[pl.BlockSpec((TQ, 512), lambda b, i: (b * nq + i, 0)),
                   pl.BlockSpec((N_HEADS // 2, 2, TQ), lambda b, i: (0, 0, b * nq + i))],
        out_shape=[jax.ShapeDtypeStruct((t_tok, 512), BF16), jax.ShapeDtypeStruct((N_HEADS // 2, 2, t_tok), F32)],
        scratch_shapes=[pltpu.VMEM((2, N_HEADS, 1, TQ), F32), pltpu.VMEM((N_HEADS, HEAD_DIM, TQ), F32),
                        pltpu.VMEM((2, N_HEADS, TK, TQ), F32), pltpu.VMEM((2, N_HEADS, TK, TQ), BF16),
                        pltpu.VMEM((2, N_HEADS, 1, TQ), F32)],
        compiler_params=_cp(("arbitrary", "arbitrary"), VMEM_MID))(q_aug, k_aug, vt)


def _fox_bwd_t(q_aug, k_aug, kt, qkvg, du_b, b_out, lse, seq):
    TK = TKB
    t_tok = qkvg.shape[0]
    nq = seq // TQ
    nkb = seq // TK
    ratio = TQ // TK
    hg = 4

    def body(qa_ref, ka_ref, kt_ref, v_ref, do_ref, o_ref, lse_ref, dq_ref, dk_ref, dv_ref, df_ref,
             dqt_s, row_s, dfk_s, dk_s, dv_s, dfa_s, st_s, dp_s, pb_s, db_s):
        ones_b = jnp.ones((8, TQ), BF16)
        eye = _eye(HEAD_DIM, BF16)
        lane8 = lax.broadcasted_iota(jnp.int32, (8, 128), 1)
        lane_k = lax.broadcasted_iota(jnp.int32, (TK, 128), 1)
        first = [lane8 < HEAD_DIM, lane8 >= HEAD_DIM]
        for hh in range(hg):
            pp, e = divmod(hh, 2)
            head_lanes = jnp.where(first[e], 1.0, 0.0)
            for ii in range(nq):
                rows = slice(ii * TQ, (ii + 1) * TQ)
                prod = do_ref[rows, 128 * pp:128 * (pp + 1)].astype(F32) * o_ref[rows, 128 * pp:128 * (pp + 1)].astype(F32)
                row_s[hh, ii, 0] = _nt(head_lanes, prod, HI)
                row_s[hh, ii, 1] = jnp.broadcast_to(lse_ref[pp, e:e + 1, ii * TQ:(ii + 1) * TQ], (8, TQ))
                dqt_s[hh, ii] = jnp.zeros((128, TQ), F32)

        def kblock(j, _):
            krow = pl.multiple_of(j * TK, TK)
            spos = j * TK + lax.broadcasted_iota(jnp.int32, (TK, 1), 0)
            for hh in range(hg):
                dk_s[hh] = jnp.zeros((TK, 128), F32)
                dv_s[hh] = jnp.zeros((TK, 128), F32)
                dfa_s[hh] = jnp.zeros((8, TK), F32)

            def scores(i):
                qrow = pl.multiple_of(i * TQ, TQ)
                for hh in range(hg):
                    pp, e = divmod(hh, 2)
                    own = (lane_k < HEAD_DIM) if e == 0 else (lane_k >= HEAD_DIM)
                    v2 = v_ref[pl.ds(krow, TK), 128 * pp:128 * (pp + 1)]
                    vj = jnp.where(own, v2, jnp.zeros_like(v2))
                    st_s[i & 1, hh] = _nt(ka_ref[hh, pl.ds(krow, TK), :], qa_ref[hh, pl.ds(qrow, TQ), :])
                    dp_s[i & 1, hh] = _nt(vj, do_ref[pl.ds(qrow, TQ), 128 * pp:128 * (pp + 1)])

            def elementwise(i, masked):
                slot = i & 1
                if masked:
                    keep = spos <= (i * TQ + lax.broadcasted_iota(jnp.int32, (1, TQ), 1))
                for hh in range(hg):
                    pt = jnp.exp(st_s[slot, hh] - row_s[hh, i, 1][0:1, :])
                    if masked:
                        pt = jnp.where(keep, pt, 0.0)
                    dst = pt * (dp_s[slot, hh] - row_s[hh, i, 0][0:1, :])
                    pb_s[slot, hh] = pt.astype(BF16)
                    db_s[slot, hh] = dst.astype(BF16)

            def grads(i):
                slot = i & 1
                qrow = pl.multiple_of(i * TQ, TQ)
                for hh in range(hg):
                    pp = hh // 2
                    dst_b = db_s[slot, hh]
                    dv_s[hh] += _nn(pb_s[slot, hh], do_ref[pl.ds(qrow, TQ), 128 * pp:128 * (pp + 1)])
                    dk_s[hh] += _nn(dst_b, qa_ref[hh, pl.ds(qrow, TQ), :])
                    dqt_s[hh, i] += _nn(kt_ref[hh, j], dst_b)
                    dfa_s[hh] += _nt(ones_b, dst_b)

            def step(i, carry):
                grads(i - 1)
                elementwise(i, False)
                scores(jnp.minimum(i + 1, nq - 1))
                return carry

            i0 = j // ratio
            scores(i0)
            elementwise(i0, True)
            scores(jnp.minimum(i0 + 1, nq - 1))
            lax.fori_loop(i0 + 1, nq, step, 0)
            grads(nq - 1)
            for pp in range(hg // 2):
                cols = slice(128 * pp, 128 * (pp + 1))
                dk_ref[pl.ds(krow, TK), cols] = jnp.concatenate(
                    [dk_s[2 * pp][:, :HEAD_DIM], dk_s[2 * pp + 1][:, :HEAD_DIM]], axis=1).astype(BF16)
                dv_ref[pl.ds(krow, TK), cols] = jnp.where(lane_k < HEAD_DIM, dv_s[2 * pp], dv_s[2 * pp + 1]).astype(BF16)
            for hh in range(hg):
                dfk_s[hh, j] = dfa_s[hh]
            return 0

        lax.fori_loop(0, nkb, kblock, 0)
        for pp in range(hg // 2):
            for ii in range(nq):
                parts = []
                for e in range(2):
                    dqt = dqt_s[2 * pp + e, ii]
                    parts.append(_tn(dqt[0:HEAD_DIM, :].astype(BF16), eye) * SCALE)
                    for kk in range(ratio):
                        jj = ii * ratio + kk
                        df_ref[pp, e:e + 1, jj * TK:(jj + 1) * TK] = (dqt[67:68, kk * TK:(kk + 1) * TK]
                                                                     - dfk_s[2 * pp + e, jj][0:1, :])
                dq_ref[ii * TQ:(ii + 1) * TQ, 128 * pp:128 * (pp + 1)] = jnp.concatenate(parts, axis=1).astype(BF16)

    aug_blk = pl.BlockSpec((hg, seq, 128), lambda b, g: (g, b, 0))
    pair_blk = pl.BlockSpec((seq, 64 * hg), lambda b, g: (b, g))
    row_blk = pl.BlockSpec((hg // 2, 2, seq), lambda b, g: (g, 0, b))
    return _pallas(
        body, name="fox_bwd", grid=(t_tok // seq, N_HEADS // hg),
        in_specs=[aug_blk, aug_blk, pl.BlockSpec((hg, nkb, 128, TK), lambda b, g: (g, b, 0, 0)),
                  pl.BlockSpec((seq, 64 * hg), lambda b, g: (b, C_BV // (64 * hg) + g)), pair_blk, pair_blk, row_blk],
        out_specs=[pair_blk, pair_blk, pair_blk, row_blk],
        out_shape=[jax.ShapeDtypeStruct((t_tok, 512), BF16)] * 3
        + [jax.ShapeDtypeStruct((N_HEADS // 2, 2, t_tok), F32)],
        scratch_shapes=[pltpu.VMEM((hg, nq, 128, TQ), F32), pltpu.VMEM((hg, nq, 2, 8, TQ), F32),
                        pltpu.VMEM((hg, nkb, 8, TK), F32), pltpu.VMEM((hg, TK, 128), F32),
                        pltpu.VMEM((hg, TK, 128), F32), pltpu.VMEM((hg, 8, TK), F32),
                        pltpu.VMEM((2, hg, TK, TQ), F32), pltpu.VMEM((2, hg, TK, TQ), F32),
                        pltpu.VMEM((2, hg, TK, TQ), BF16), pltpu.VMEM((2, hg, TK, TQ), BF16)],
        compiler_params=_cp(("arbitrary", "arbitrary"), VMEM_BIG))(q_aug, k_aug, kt, qkvg, du_b, b_out, lse)


def _fox_bwd_t_old(q_aug, k_aug, kt, qkvg, du_b, b_out, lse, seq):
    t_tok = qkvg.shape[0]
    nq = seq // TQ
    nkb = seq // TK
    ratio = TQ // TK

    def body(qa_ref, ka_ref, kt_ref, v_ref, do_ref, o_ref, lse_ref, dq_ref, dk_ref, dv_ref, df_ref,
             dqt_s, out_s, row_s, dfk_s):
        ones_b = jnp.ones((8, TQ), BF16)
        ones_f = jnp.ones((8, HEAD_DIM), F32)
        eye = _eye(HEAD_DIM, BF16)
        for e in range(2):
            lo, hi = e * HEAD_DIM, (e + 1) * HEAD_DIM
            for ii in range(nq):
                rows = slice(ii * TQ, (ii + 1) * TQ)
                do = do_ref[rows, :][:, lo:hi].astype(F32)
                ov = o_ref[rows, :][:, lo:hi].astype(F32)
                row_s[ii, 0] = _nt(ones_f, do * ov, HI)
                row_s[ii, 1] = jnp.broadcast_to(lse_ref[e:e + 1, ii * TQ:(ii + 1) * TQ], (8, TQ))
                dqt_s[ii] = jnp.zeros((128, TQ), F32)

            def kblock(j, _):
                krow = pl.multiple_of(j * TK, TK)
                kj = ka_ref[e, pl.ds(krow, TK), :]
                ktj = kt_ref[e, j]
                vj = v_ref[pl.ds(krow, TK), :][:, lo:hi]
                spos = j * TK + lax.broadcasted_iota(jnp.int32, (TK, 1), 0)

                def qblock(i, carry, masked):
                    dk_acc, dv_acc, dfk = carry
                    qrow = pl.multiple_of(i * TQ, TQ)
                    qa = qa_ref[e, pl.ds(qrow, TQ), :]
                    doh = do_ref[pl.ds(qrow, TQ), :][:, lo:hi]
                    pt = jnp.exp(_nt(kj, qa) - row_s[i, 1][0:1, :])
                    if masked:
                        tpos = i * TQ + lax.broadcasted_iota(jnp.int32, (1, TQ), 1)
                        pt = jnp.where(spos <= tpos, pt, 0.0)
                    dst = pt * (_nt(vj, doh) - row_s[i, 0][0:1, :])
                    dst_b = dst.astype(BF16)
                    dv_acc = dv_acc + _nn(pt.astype(BF16), doh)
                    dk_acc = dk_acc + _nn(dst_b, qa)
                    dqt_s[i] += _nn(ktj, dst_b)
                    dfk = dfk + _nt(ones_b, dst_b)
                    return dk_acc, dv_acc, dfk

                i0 = j // ratio
                carry = (jnp.zeros((TK, 128), F32), jnp.zeros((TK, HEAD_DIM), F32), jnp.zeros((8, TK), F32))
                carry = qblock(i0, carry, True)
                dk_acc, dv_acc, dfk = lax.fori_loop(i0 + 1, nq, functools.partial(qblock, masked=False), carry)
                out_s[1, e, pl.ds(krow, TK), :] = dk_acc[:, :HEAD_DIM]
                out_s[2, e, pl.ds(krow, TK), :] = dv_acc
                dfk_s[j] = dfk
                return 0

            lax.fori_loop(0, nkb, kblock, 0)
            for ii in range(nq):
                dqt = dqt_s[ii]
                out_s[0, e, ii * TQ:(ii + 1) * TQ, :] = _tn(dqt[0:HEAD_DIM, :].astype(BF16), eye) * SCALE
                for kk in range(ratio):
                    jj = ii * ratio + kk
                    df_ref[e:e + 1, jj * TK:(jj + 1) * TK] = dqt[67:68, kk * TK:(kk + 1) * TK] - dfk_s[jj][0:1, :]
        for k, ref in enumerate((dq_ref, dk_ref, dv_ref)):
            ref[...] = jnp.concatenate([out_s[k, 0], out_s[k, 1]], axis=1).astype(BF16)

    aug_blk = pl.BlockSpec((2, seq, 128), lambda b, p: (p, b, 0))
    pair_blk = pl.BlockSpec((seq, 128), lambda b, p: (b, p))
    row_blk = pl.BlockSpec((None, 2, seq), lambda b, p: (p, 0, b))
    return _pallas(
        body, name="fox_bwd", grid=(t_tok // seq, N_HEADS // 2),
        in_specs=[aug_blk, aug_blk, pl.BlockSpec((2, nkb, 128, TK), lambda b, p: (p, b, 0, 0)),
                  pl.BlockSpec((seq, 128), lambda b, p: (b, C_BV // 128 + p)), pair_blk, pair_blk, row_blk],
        out_specs=[pair_blk, pair_blk, pair_blk, row_blk],
        out_shape=[jax.ShapeDtypeStruct((t_tok, 512), BF16)] * 3
        + [jax.ShapeDtypeStruct((N_HEADS // 2, 2, t_tok), F32)],
        scratch_shapes=[pltpu.VMEM((nq, 128, TQ), F32), pltpu.VMEM((3, 2, seq, HEAD_DIM), F32),
                        pltpu.VMEM((nq, 2, 8, TQ), F32), pltpu.VMEM((nkb, 8, TK), F32)],
        compiler_params=_cp(("arbitrary", "arbitrary"), VMEM_BIG))(q_aug, k_aug, kt, qkvg, du_b, b_out, lse)


def _fox_fwd(qkvg, f_row, f_col, seq):
    t_tok = qkvg.shape[0]
    nq = seq // TQ

    def body(q_ref, k_ref, v_ref, fr_ref, fc_ref, o_ref, lse_ref, fk_s):
        i = pl.program_id(1)
        for jj in range(nq):
            fk_s[jj] = fr_ref[:, jj * TQ:(jj + 1) * TQ]
        fcol = fc_ref[...]
        tpos = i * TQ + lax.broadcasted_iota(jnp.int32, (TQ, 1), 0)
        lane = lax.broadcasted_iota(jnp.int32, (TQ, 128), 1)
        lse_tile = jnp.zeros((TQ, 128), F32)
        for p in range(N_HEADS // 2):
            q2 = q_ref[:, 128 * p:128 * (p + 1)]
            qs = [q2[:, :HEAD_DIM], q2[:, HEAD_DIM:]]
            fqs = [_col(fcol, 2 * p + e) for e in range(2)]

            def kblock(j, carry):
                row0 = pl.multiple_of(j * TQ, TQ)
                k2 = k_ref[pl.ds(row0, TQ), 128 * p:128 * (p + 1)]
                v2 = v_ref[pl.ds(row0, TQ), 128 * p:128 * (p + 1)]
                fk8 = fk_s[j]
                spos = j * TQ + lax.broadcasted_iota(jnp.int32, (1, TQ), 1)
                keep = spos <= tpos
                new = []
                for e in range(2):
                    m, l, acc = carry[3 * e:3 * e + 3]
                    kh = k2[:, e * HEAD_DIM:(e + 1) * HEAD_DIM]
                    vh = v2[:, e * HEAD_DIM:(e + 1) * HEAD_DIM]
                    s = _nt(qs[e], kh) * SCALE + (fqs[e] - fk8[2 * p + e:2 * p + e + 1, :])
                    s = jnp.where(keep, s, NEG)
                    m_new = jnp.maximum(m, jnp.max(s, axis=1, keepdims=True))
                    alpha = jnp.exp(m - m_new)
                    pe = jnp.exp(s - m_new)
                    l = alpha * l + jnp.sum(pe, axis=1, keepdims=True)
                    acc = alpha * acc + _nn(pe.astype(BF16), vh)
                    new += [m_new, l, acc]
                return tuple(new)

            init = (jnp.full((TQ, 1), NEG, F32), jnp.zeros((TQ, 1), F32), jnp.zeros((TQ, HEAD_DIM), F32)) * 2
            res = lax.fori_loop(0, i + 1, kblock, init)
            outs = []
            for e in range(2):
                m, l, acc = res[3 * e:3 * e + 3]
                outs.append(acc / l)
                lse_tile = jnp.where(lane == 2 * p + e, m + jnp.log(l), lse_tile)
            o_ref[:, 128 * p:128 * (p + 1)] = jnp.concatenate(outs, axis=1).astype(BF16)
        lse_ref[...] = lse_tile

    return _pallas(
        body, name="fox_fwd", grid=(t_tok // seq, nq),
        in_specs=[pl.BlockSpec((TQ, 512), lambda b, i: (b * nq + i, C_BQ // 512)),
                  pl.BlockSpec((seq, 512), lambda b, i: (b, C_BK // 512)),
                  pl.BlockSpec((seq, 512), lambda b, i: (b, C_BV // 512)),
                  pl.BlockSpec((N_HEADS, seq), lambda b, i: (0, b)),
                  pl.BlockSpec((TQ, 128), lambda b, i: (b * nq + i, 0))],
        out_specs=[pl.BlockSpec((TQ, 512), lambda b, i: (b * nq + i, 0)),
                   pl.BlockSpec((TQ, 128), lambda b, i: (b * nq + i, 0))],
        out_shape=[jax.ShapeDtypeStruct((t_tok, 512), BF16), jax.ShapeDtypeStruct((t_tok, 128), F32)],
        scratch_shapes=[pltpu.VMEM((nq, N_HEADS, TQ), F32)],
        compiler_params=_cp(("arbitrary", "arbitrary"), VMEM_MID))(qkvg, qkvg, qkvg, f_row, f_col)


def _fox_bwd(qkvg, du_b, b_out, lse, f_row, f_col, seq):
    t_tok = qkvg.shape[0]
    nq = seq // TQ

    def body(q_ref, k_ref, v_ref, do_ref, o_ref, lse_ref, fr_ref, fc_ref,
             dq_ref, dk_ref, dv_ref, df_ref, dq_s, dk_s, dv_s, col_s, df_s, fk_s):
        p = pl.program_id(1)
        for jj in range(nq):
            fk_s[jj] = fr_ref[:, jj * TQ:(jj + 1) * TQ]
        eye = (lax.broadcasted_iota(jnp.int32, (TQ, TQ), 0) == lax.broadcasted_iota(jnp.int32, (TQ, TQ), 1)).astype(F32)
        for e in range(2):
            h = 2 * p + e
            lo, hi = e * HEAD_DIM, (e + 1) * HEAD_DIM
            for ii in range(nq):
                rows = slice(ii * TQ, (ii + 1) * TQ)
                do = do_ref[rows, :][:, lo:hi].astype(F32)
                ov = o_ref[rows, :][:, lo:hi].astype(F32)
                col_s[0, rows, :] = jnp.sum(do * ov, axis=1, keepdims=True)
                col_s[1, rows, :] = _col(lse_ref[rows, :], h)
                col_s[2, rows, :] = _col(fc_ref[rows, :], h)
                dq_s[rows, :] = jnp.zeros((TQ, HEAD_DIM), F32)
                df_s[ii] = jnp.zeros((8, TQ), F32)
                col_s[3, rows, :] = jnp.zeros((TQ, 1), F32)

            def kblock(j, _):
                krow = pl.multiple_of(j * TQ, TQ)
                kh = k_ref[pl.ds(krow, TQ), :][:, lo:hi]
                vh = v_ref[pl.ds(krow, TQ), :][:, lo:hi]
                fk = _row(fk_s[j], h)
                spos = j * TQ + lax.broadcasted_iota(jnp.int32, (1, TQ), 1)

                def qblock(i, carry):
                    dk_acc, dv_acc, dfk = carry
                    qrow = pl.multiple_of(i * TQ, TQ)
                    qh = q_ref[pl.ds(qrow, TQ), :][:, lo:hi]
                    doh = do_ref[pl.ds(qrow, TQ), :][:, lo:hi]
                    delta = col_s[0, pl.ds(qrow, TQ), :]
                    lse_q = col_s[1, pl.ds(qrow, TQ), :]
                    fq = col_s[2, pl.ds(qrow, TQ), :]
                    tpos = i * TQ + lax.broadcasted_iota(jnp.int32, (TQ, 1), 0)
                    s = _nt(qh, kh) * SCALE + (fq - fk)
                    pr = jnp.where(spos <= tpos, jnp.exp(s - lse_q), 0.0)
                    dp = _nt(doh, vh)
                    ds = pr * (dp - delta)
                    ds_b = ds.astype(BF16)
                    dv_acc = dv_acc + _tn(pr.astype(BF16), doh)
                    dk_acc = dk_acc + _tn(ds_b, qh)
                    dq_s[pl.ds(qrow, TQ), :] += _nn(ds_b, kh)
                    col_s[3, pl.ds(qrow, TQ), :] += jnp.sum(ds, axis=1, keepdims=True)
                    dfk = dfk + jnp.sum(ds, axis=0, keepdims=True)
                    return dk_acc, dv_acc, dfk

                zero = jnp.zeros((TQ, HEAD_DIM), F32)
                dk_acc, dv_acc, dfk = lax.fori_loop(j, nq, qblock, (zero, zero, jnp.zeros((1, TQ), F32)))
                dk_s[e, pl.ds(krow, TQ), :] = dk_acc * SCALE
                dv_s[e, pl.ds(krow, TQ), :] = dv_acc
                df_s[j] -= jnp.broadcast_to(dfk, (8, TQ))
                return 0

            lax.fori_loop(0, nq, kblock, 0)
            dq_s2 = dq_s[...] * SCALE
            dk_s[2 + e] = dq_s2
            for ii in range(nq):
                dfq = jnp.broadcast_to(col_s[3, ii * TQ:(ii + 1) * TQ, :], (TQ, 128))
                df_ref[e:e + 1, ii * TQ:(ii + 1) * TQ] = _tn(dfq, eye, HI)[0:1, :] + df_s[ii][0:1, :]
        dq_ref[...] = jnp.concatenate([dk_s[2], dk_s[3]], axis=1).astype(BF16)
        dk_ref[...] = jnp.concatenate([dk_s[0], dk_s[1]], axis=1).astype(BF16)
        dv_ref[...] = jnp.concatenate([dv_s[0], dv_s[1]], axis=1).astype(BF16)

    blk = lambda off: pl.BlockSpec((seq, 128), lambda b, p: (b, off // 128 + p))
    out_blk = pl.BlockSpec((seq, 128), lambda b, p: (b, p))
    return _pallas(
        body, name="fox_bwd", grid=(t_tok // seq, N_HEADS // 2),
        in_specs=[blk(C_BQ), blk(C_BK), blk(C_BV), out_blk, out_blk,
                  pl.BlockSpec((seq, 128), lambda b, p: (b, 0)),
                  pl.BlockSpec((N_HEADS, seq), lambda b, p: (0, b)),
                  pl.BlockSpec((seq, 128), lambda b, p: (b, 0))],
        out_specs=[out_blk, out_blk, out_blk, pl.BlockSpec((None, 2, seq), lambda b, p: (p, 0, b))],
        out_shape=[jax.ShapeDtypeStruct((t_tok, 512), BF16)] * 3
        + [jax.ShapeDtypeStruct((N_HEADS // 2, 2, t_tok), F32)],
        scratch_shapes=[pltpu.VMEM((seq, HEAD_DIM), F32), pltpu.VMEM((4, seq, HEAD_DIM), F32),
                        pltpu.VMEM((2, seq, HEAD_DIM), F32), pltpu.VMEM((4, seq, 1), F32),
                        pltpu.VMEM((nq, 8, TQ), F32), pltpu.VMEM((nq, N_HEADS, TQ), F32)],
        compiler_params=_cp(("arbitrary", "arbitrary"), VMEM_BIG))(qkvg, qkvg, qkvg, du_b, b_out, lse, f_row, f_col)


def _swa_window(k_ref, v_ref, n):
    prev = pl.multiple_of(jnp.maximum(n - 1, 0) * BLOCK, BLOCK)
    cur = pl.multiple_of(n * BLOCK, BLOCK)
    kwin = jnp.concatenate([k_ref[pl.ds(prev, BLOCK), :], k_ref[pl.ds(cur, BLOCK), :]], axis=0)
    vwin = jnp.concatenate([v_ref[pl.ds(prev, BLOCK), :], v_ref[pl.ds(cur, BLOCK), :]], axis=0)
    ti = lax.broadcasted_iota(jnp.int32, (BLOCK, 2 * BLOCK), 0)
    sj = lax.broadcasted_iota(jnp.int32, (BLOCK, 2 * BLOCK), 1)
    rel = ti - sj + BLOCK
    first_key = jnp.where(n > 0, 0, BLOCK)
    mask = jnp.logical_and(jnp.logical_and(rel >= 0, rel < BLOCK), sj >= first_key)
    return kwin, vwin, mask, prev, cur


def _head_cols(ref, h):
    pair = ref[:, 128 * (h // 2):128 * (h // 2 + 1)]
    return pair[:, (h % 2) * HEAD_DIM:(h % 2 + 1) * HEAD_DIM]


def _swa_logits(q_ref, kwin, bias_ref, h, mask):
    hk = h // KV_GROUP
    s = _nt(_head_cols(q_ref, h), kwin[:, hk * HEAD_DIM:(hk + 1) * HEAD_DIM]) * SCALE + bias_ref[h]
    return jnp.where(mask, s, NEG)


def _swa_fwd(qkvg, bias, sinks, seq):
    t_tok = qkvg.shape[0]
    nb = seq // BLOCK

    def body(sink_ref, q_ref, k_ref, v_ref, bias_ref, o_ref, lse_ref, s_s, p_s, den_s):
        n = pl.program_id(1)
        kwin, vwin, mask, _, _ = _swa_window(k_ref, v_ref, n)
        for h in range(N_HEADS):
            s_s[h] = _swa_logits(q_ref, kwin, bias_ref, h, mask)
        lane = lax.broadcasted_iota(jnp.int32, (BLOCK, 128), 1)
        lse_tile = jnp.zeros((BLOCK, 128), F32)
        for h in range(N_HEADS):
            s = s_s[h]
            sink = sink_ref[h]
            m = jnp.maximum(jnp.max(s, axis=1, keepdims=True), sink)
            pe = jnp.exp(s - m)
            den = jnp.sum(pe, axis=1, keepdims=True) + jnp.exp(sink - m)
            p_s[h] = pe.astype(BF16)
            den_s[h] = den
            lse_tile = jnp.where(lane == h, m + jnp.log(den), lse_tile)
        lse_ref[...] = lse_tile
        for pr in range(N_HEADS // 2):
            outs = []
            for h in (2 * pr, 2 * pr + 1):
                hk = h // KV_GROUP
                outs.append(_nn(p_s[h], vwin[:, hk * HEAD_DIM:(hk + 1) * HEAD_DIM]) / den_s[h])
            o_ref[:, 128 * pr:128 * (pr + 1)] = jnp.concatenate(outs, axis=1).astype(BF16)

    return _pallas(
        body, name="swa_fwd", grid=(t_tok // seq, nb),
        in_specs=[pl.BlockSpec(memory_space=pltpu.SMEM),
                  pl.BlockSpec((BLOCK, 512), lambda b, n: (b * nb + n, C_AQ // 512)),
                  pl.BlockSpec((seq, 128), lambda b, n: (b, C_AK // 128)),
                  pl.BlockSpec((seq, 128), lambda b, n: (b, C_AV // 128)),
                  pl.BlockSpec((N_HEADS, BLOCK, 2 * BLOCK), lambda b, n: (0, 0, 0))],
        out_specs=[pl.BlockSpec((BLOCK, 512), lambda b, n: (b * nb + n, 0)),
                   pl.BlockSpec((BLOCK, 128), lambda b, n: (b * nb + n, 0))],
        out_shape=[jax.ShapeDtypeStruct((t_tok, 512), BF16), jax.ShapeDtypeStruct((t_tok, 128), F32)],
        scratch_shapes=[pltpu.VMEM((N_HEADS, BLOCK, 2 * BLOCK), F32), pltpu.VMEM((N_HEADS, BLOCK, 2 * BLOCK), BF16),
                        pltpu.VMEM((N_HEADS, BLOCK, 1), F32)],
        compiler_params=_cp(("arbitrary", "arbitrary"), VMEM_MID))(sinks, qkvg, qkvg, qkvg, bias)


def _swa_bwd(qkvg, du_a, a_out, lse, bias, sinks, seq):
    t_tok = qkvg.shape[0]
    nb = seq // BLOCK

    def body(sink_ref, q_ref, k_ref, v_ref, do_ref, o_ref, lse_ref, bias_ref,
             dq_ref, dkv_ref, dbias_ref, dsink_ref, kv_s, s_s, dp_s, pb_s, db_s):
        b, n = pl.program_id(0), pl.program_id(1)

        @pl.when(jnp.logical_and(b == 0, n == 0))
        def _():
            dbias_ref[...] = jnp.zeros_like(dbias_ref)
            dsink_ref[...] = jnp.zeros_like(dsink_ref)

        @pl.when(n == 0)
        def _():
            kv_s[...] = jnp.zeros_like(kv_s)

        kwin, vwin, mask, prev, cur = _swa_window(k_ref, v_ref, n)
        for h in range(N_HEADS):
            hk = h // KV_GROUP
            s_s[h] = _swa_logits(q_ref, kwin, bias_ref, h, mask)
            dp_s[h] = _nt(_head_cols(do_ref, h), vwin[:, hk * HEAD_DIM:(hk + 1) * HEAD_DIM])
        lse_tile = lse_ref[...]
        for h in range(N_HEADS):
            delta = jnp.sum(_head_cols(do_ref, h).astype(F32) * _head_cols(o_ref, h).astype(F32), axis=1, keepdims=True)
            lse_h = _col(lse_tile, h)
            pe = jnp.exp(s_s[h] - lse_h)
            ds = pe * (dp_s[h] - delta)
            dbias_ref[h] += ds
            psink = jnp.exp(sink_ref[h] - lse_h)
            dsink_ref[h:h + 1, :] += jnp.broadcast_to(jnp.sum(-psink * delta, axis=0, keepdims=True), (1, 128))
            pb_s[h] = pe.astype(BF16)
            db_s[h] = ds.astype(BF16)
        for pr in range(N_HEADS // 2):
            dqs = []
            for h in (2 * pr, 2 * pr + 1):
                hk = h // KV_GROUP
                dqs.append(_nn(db_s[h], kwin[:, hk * HEAD_DIM:(hk + 1) * HEAD_DIM]) * SCALE)
            dq_ref[:, 128 * pr:128 * (pr + 1)] = jnp.concatenate(dqs, axis=1).astype(BF16)
        dks, dvs = [], []
        for hk in range(N_HEADS // KV_GROUP):
            dk = jnp.zeros((2 * BLOCK, HEAD_DIM), F32)
            dv = jnp.zeros((2 * BLOCK, HEAD_DIM), F32)
            for h in range(hk * KV_GROUP, (hk + 1) * KV_GROUP):
                dk = dk + _tn(db_s[h], _head_cols(q_ref, h))
                dv = dv + _tn(pb_s[h], _head_cols(do_ref, h))
            dks.append(dk * SCALE)
            dvs.append(dv)
        upd = jnp.concatenate(dks + dvs, axis=1)
        kv_s[pl.ds(prev, BLOCK), :] += upd[:BLOCK]
        kv_s[pl.ds(cur, BLOCK), :] += upd[BLOCK:]

        @pl.when(n == nb - 1)
        def _():
            dkv_ref[...] = kv_s[...].astype(BF16)

    return _pallas(
        body, name="swa_bwd", grid=(t_tok // seq, nb),
        in_specs=[pl.BlockSpec(memory_space=pltpu.SMEM),
                  pl.BlockSpec((BLOCK, 512), lambda b, n: (b * nb + n, C_AQ // 512)),
                  pl.BlockSpec((seq, 128), lambda b, n: (b, C_AK // 128)),
                  pl.BlockSpec((seq, 128), lambda b, n: (b, C_AV // 128)),
                  pl.BlockSpec((BLOCK, 512), lambda b, n: (b * nb + n, 0)),
                  pl.BlockSpec((BLOCK, 512), lambda b, n: (b * nb + n, 0)),
                  pl.BlockSpec((BLOCK, 128), lambda b, n: (b * nb + n, 0)),
                  pl.BlockSpec((N_HEADS, BLOCK, 2 * BLOCK), lambda b, n: (0, 0, 0))],
        out_specs=[pl.BlockSpec((BLOCK, 512), lambda b, n: (b * nb + n, 0)),
                   pl.BlockSpec((seq, 256), lambda b, n: (b, 0)),
                   pl.BlockSpec((N_HEADS, BLOCK, 2 * BLOCK), lambda b, n: (0, 0, 0)),
                   pl.BlockSpec((N_HEADS, 128), lambda b, n: (0, 0))],
        out_shape=[jax.ShapeDtypeStruct((t_tok, 512), BF16), jax.ShapeDtypeStruct((t_tok, 256), BF16),
                   jax.ShapeDtypeStruct((N_HEADS, BLOCK, 2 * BLOCK), F32), jax.ShapeDtypeStruct((N_HEADS, 128), F32)],
        scratch_shapes=[pltpu.VMEM((seq, 256), F32),
                        pltpu.VMEM((N_HEADS, BLOCK, 2 * BLOCK), F32), pltpu.VMEM((N_HEADS, BLOCK, 2 * BLOCK), F32),
                        pltpu.VMEM((N_HEADS, BLOCK, 2 * BLOCK), BF16), pltpu.VMEM((N_HEADS, BLOCK, 2 * BLOCK), BF16)],
        compiler_params=_cp(("arbitrary", "arbitrary"), VMEM_MID))(sinks, qkvg, qkvg, qkvg, du_a, a_out, lse, bias)


def _out_proj(name, u_parts, gate_arr, gate_blk, w_out, x, gmod, seq):
    t_tok = x.shape[0]
    nu = len(u_parts)

    def body(*refs):
        u_refs = refs[:nu]
        g_ref, w_ref, x_ref, gm_ref, yg_ref, y_ref, xn_ref = refs[nu:]
        u = jnp.concatenate([r[...].astype(F32) for r in u_refs], axis=1) if nu > 1 else u_refs[0][...].astype(F32)
        yg = (u * _silu(g_ref[...].astype(F32))).astype(BF16)
        yg_ref[...] = yg
        y = _nn(yg, w_ref[...])
        y_ref[...] = y.astype(BF16)
        xn_ref[...] = x_ref[...] + gm_ref[...] * y

    row = lambda w: pl.BlockSpec((TM, w), lambda i: (i, 0))
    in_specs = [row(u.shape[1]) for u in u_parts]
    in_specs += [pl.BlockSpec((TM, D_MODEL), lambda i: (i, gate_blk)),
                 pl.BlockSpec((D_MODEL, D_MODEL), lambda i: (0, 0)), row(D_MODEL),
                 pl.BlockSpec((None, 1, D_MODEL), lambda i: (i * TM // seq, 0, 0))]
    return _pallas(
        body, name=name, grid=(t_tok // TM,), in_specs=in_specs,
        out_specs=[row(D_MODEL)] * 3,
        out_shape=[jax.ShapeDtypeStruct((t_tok, D_MODEL), BF16)] * 2 + [jax.ShapeDtypeStruct((t_tok, D_MODEL), F32)],
        compiler_params=_cp(("arbitrary",), VMEM_MID))(*u_parts, gate_arr, w_out, x, gmod)


def _out_proj_bwd(name, dxn, gmod, y, w_out, seq, attn=None):
    t_tok = dxn.shape[0]
    tiles_per_seq = seq // TM

    def body(*refs):
        if attn is None:
            dxn_ref, gm_ref, y_ref, w_ref, dy_ref, dgm_ref, dyg_ref = refs
        else:
            dxn_ref, gm_ref, y_ref, w_ref, a_ref, b_ref, g_ref, dy_ref, dgm_ref, dua_ref, dub_ref, dg_ref = refs
        i = pl.program_id(0)
        dxv = dxn_ref[...]
        dy = (dxv * gm_ref[...]).astype(BF16)
        dy_ref[...] = dy

        @pl.when(i % tiles_per_seq == 0)
        def _():
            dgm_ref[...] = jnp.zeros_like(dgm_ref)

        dgm_ref[...] += jnp.sum(dxv * y_ref[...].astype(F32), axis=0, keepdims=True)
        dyg = _nt(dy, w_ref[...])
        if attn is None:
            dyg_ref[...] = dyg
        else:
            gt = g_ref[...].astype(F32)
            du = dyg * _silu(gt)
            dua_ref[...] = du[:, :512].astype(BF16)
            dub_ref[...] = du[:, 512:].astype(BF16)
            u = jnp.concatenate([a_ref[...].astype(F32), b_ref[...].astype(F32)], axis=1)
            dg_ref[...] = (dyg * u * _dsilu(gt)).astype(BF16)

    row = lambda w: pl.BlockSpec((TM, w), lambda i: (i, 0))
    mod_spec = pl.BlockSpec((None, 1, D_MODEL), lambda i: (i * TM // seq, 0, 0))
    in_specs = [row(D_MODEL), mod_spec, row(D_MODEL), pl.BlockSpec((D_MODEL, D_MODEL), lambda i: (0, 0))]
    out_specs = [row(D_MODEL), mod_spec]
    out_shape = [jax.ShapeDtypeStruct((t_tok, D_MODEL), BF16), jax.ShapeDtypeStruct(gmod.shape, F32)]
    args = [dxn, gmod, y, w_out]
    if attn is None:
        out_specs.append(row(D_MODEL))
        out_shape.append(jax.ShapeDtypeStruct((t_tok, D_MODEL), F32))
    else:
        in_specs += [row(512), row(512), pl.BlockSpec((TM, D_MODEL), lambda i: (i, C_GATE // D_MODEL))]
        out_specs += [row(512), row(512), row(D_MODEL)]
        out_shape += [jax.ShapeDtypeStruct((t_tok, 512), BF16)] * 2 + [jax.ShapeDtypeStruct((t_tok, D_MODEL), BF16)]
        args += list(attn)
    return _pallas(body, name=name, grid=(t_tok // TM,), in_specs=in_specs, out_specs=out_specs,
                   out_shape=out_shape, compiler_params=_cp(("arbitrary",), VMEM_MID))(*args)


def _norm_bwd(name, parts, w, x, g, scale, dxn, seq, rows_part=None):
    t_tok = x.shape[0]
    npart = len(parts)
    w3d = w.ndim == 3
    tiles_per_seq = seq // TM
    nrow_in = 0 if rows_part is None else 2

    def body(*refs):
        p_refs = refs[:npart]
        w_ref, x_ref, g_ref, sc_ref, dxn_ref = refs[npart:npart + 5]
        dx_ref, dss_ref, dg_ref = refs[npart + 5 + nrow_in:]
        i = pl.program_id(0)
        dh = jnp.zeros((TM, D_MODEL), F32)
        if rows_part is not None:
            r_ref, wr_ref = refs[npart + 5:npart + 7]
            dh = dh + _tn(r_ref[...].astype(BF16), wr_ref[...])
        for (arr, off), p_ref in zip(parts, p_refs):
            width = arr.shape[1]
            for j in range(width // 256):
                pj = p_ref[:, j * 256:(j + 1) * 256]
                c0 = off + j * 256
                wj = w_ref[c0 // 256] if w3d else w_ref[:, c0:c0 + 256]
                dh = dh + _nt(pj, wj)
        xv = x_ref[...]
        rstd = lax.rsqrt(jnp.mean(xv * xv, axis=-1, keepdims=True) + EPS)
        xhat = xv * rstd
        gv = g_ref[...]
        nrm = xhat * gv

        @pl.when(i % tiles_per_seq == 0)
        def _():
            dss_ref[...] = jnp.zeros_like(dss_ref)

        @pl.when(i == 0)
        def _():
            dg_ref[...] = jnp.zeros_like(dg_ref)

        dss_ref[0:1, :] += jnp.sum(dh, axis=0, keepdims=True)
        dss_ref[1:2, :] += jnp.sum(dh * nrm, axis=0, keepdims=True)
        dn = dh * (1.0 + sc_ref[...])
        dg_ref[0:1, :] += jnp.sum(dn * xhat, axis=0, keepdims=True)
        dxhat = dn * gv
        dx_ref[...] = rstd * (dxhat - xhat * jnp.mean(dxhat * xhat, axis=-1, keepdims=True)) + dxn_ref[...]

    row = lambda wd: pl.BlockSpec((TM, wd), lambda i: (i, 0))
    w_spec = (pl.BlockSpec(w.shape, lambda i: (0, 0, 0)) if w3d else pl.BlockSpec(w.shape, lambda i: (0, 0)))
    in_specs = [row(a.shape[1]) for a, _ in parts]
    in_specs += [w_spec, row(D_MODEL), pl.BlockSpec((1, D_MODEL), lambda i: (0, 0)),
                 pl.BlockSpec((None, 1, D_MODEL), lambda i: (i * TM // seq, 0, 0)), row(D_MODEL)]
    args = [a for a, _ in parts] + [w, x, g, scale, dxn]
    if rows_part is not None:
        in_specs += [pl.BlockSpec((8, TM), lambda i: (0, i)), pl.BlockSpec((8, D_MODEL), lambda i: (0, 0))]
        args += list(rows_part)
    nseq = t_tok // seq
    return _pallas(
        body, name=name, grid=(t_tok // TM,), in_specs=in_specs,
        out_specs=[row(D_MODEL), pl.BlockSpec((None, 8, D_MODEL), lambda i: (i * TM // seq, 0, 0)),
                   pl.BlockSpec((8, D_MODEL), lambda i: (0, 0))],
        out_shape=[jax.ShapeDtypeStruct((t_tok, D_MODEL), F32), jax.ShapeDtypeStruct((nseq, 8, D_MODEL), F32),
                   jax.ShapeDtypeStruct((8, D_MODEL), F32)],
        compiler_params=_cp(("arbitrary",), VMEM_BIG))(*args)


def _dw(name, a, parts, blocked=None):
    t_tok, ka = a.shape
    tt = 512
    npart = len(parts)
    nt = t_tok // tt

    def body(*refs):
        a_ref = refs[0]
        p_refs = refs[1:1 + npart]
        o_refs = refs[1 + npart:1 + 2 * npart]
        acc_refs = refs[1 + 2 * npart:]
        t = pl.program_id(0)
        av = a_ref[...]
        for p_ref, acc in zip(p_refs, acc_refs):
            upd = _tn(av, p_ref[...])

            @pl.when(t == 0)
            def _():
                acc[...] = upd

            @pl.when(t > 0)
            def _():
                acc[...] += upd

        @pl.when(t == nt - 1)
        def _():
            for o_ref, acc in zip(o_refs, acc_refs):
                if blocked is None:
                    o_ref[...] = acc[...].astype(BF16)
                else:
                    for j in range(o_ref.shape[0]):
                        o_ref[j] = acc[:, j * blocked:(j + 1) * blocked].astype(BF16)

    in_specs = [pl.BlockSpec((tt, ka), lambda t: (t, 0))]
    in_specs += [pl.BlockSpec((tt, p.shape[1]), lambda t: (t, 0)) for p in parts]
    if blocked is None:
        out_shape = [jax.ShapeDtypeStruct((ka, p.shape[1]), BF16) for p in parts]
        out_specs = [pl.BlockSpec((ka, p.shape[1]), lambda t: (0, 0)) for p in parts]
    else:
        out_shape = [jax.ShapeDtypeStruct((p.shape[1] // blocked, ka, blocked), BF16) for p in parts]
        out_specs = [pl.BlockSpec((p.shape[1] // blocked, ka, blocked), lambda t: (0, 0, 0)) for p in parts]
    return _pallas(body, name=name, grid=(nt,), in_specs=in_specs, out_specs=out_specs, out_shape=out_shape,
                   scratch_shapes=[pltpu.VMEM((ka, p.shape[1]), F32) for p in parts],
                   compiler_params=_cp(("arbitrary",), VMEM_BIG))(a, *parts)


def _dw_rows(name, rows_t, h):
    t_tok = h.shape[0]
    tt = 512

    def body(r_ref, h_ref, o_ref):
        @pl.when(pl.program_id(0) == 0)
        def _():
            o_ref[...] = jnp.zeros_like(o_ref)

        o_ref[...] += _nn(r_ref[...].astype(BF16), h_ref[...])

    return _pallas(body, name=name, grid=(t_tok // tt,),
                   in_specs=[pl.BlockSpec((8, tt), lambda t: (0, t)), pl.BlockSpec((tt, D_MODEL), lambda t: (t, 0))],
                   out_specs=pl.BlockSpec((8, D_MODEL), lambda t: (0, 0)),
                   out_shape=jax.ShapeDtypeStruct((8, D_MODEL), F32),
                   compiler_params=_cp(("arbitrary",), VMEM_MID))(rows_t, h)


def _lru_gates(xc, blk, wa_ref, wx_ref, ba_ref, bx_ref, sp):
    cols = slice(blk * LRU_BLOCK_W, (blk + 1) * LRU_BLOCK_W)
    xb = xc[:, cols].astype(BF16)
    r = _sigmoid(_nn(xb, wa_ref[blk].astype(BF16)) + ba_ref[:, cols])
    ig = _sigmoid(_nn(xb, wx_ref[blk].astype(BF16)) + bx_ref[:, cols])
    log_a = -LRU_C * r * sp[:, cols]
    a = jnp.exp(log_a)
    x2 = 2.0 * log_a
    series = -x2 * (1.0 + x2 * (0.5 + x2 * (1.0 / 6.0)))
    z = jnp.where(x2 > -0.01, series, 1.0 - a * a)
    mult = z * lax.rsqrt(jnp.maximum(z, 1e-30))
    return xb, r, ig, a, mult


def _softplus_neg(lam):
    return jnp.maximum(-lam, 0.0) + jnp.log(1.0 + jnp.exp(-jnp.abs(lam)))


def _conv_taps(xe_ref, cw_ref, cb_ref):
    xc = cb_ref[...] + xe_ref[8:8 + TC, :] * cw_ref[3:4, :]
    for k in range(1, 4):
        xc = xc + xe_ref[8 - k:8 - k + TC, :] * cw_ref[3 - k:4 - k, :]
    return xc


def _lru_fwd(proj, cw, cb, w_a, b_a, w_x, b_x, lam, seq):
    t_tok = proj.shape[0]
    nc = seq // TC

    def body(x_ref, cw_ref, cb_ref, wa_ref, ba_ref, wx_ref, bx_ref, lam_ref, hs_ref, xe_s, a_s, u_s, h_s):
        c = pl.program_id(1)

        @pl.when(c == 0)
        def _():
            xe_s[0:8, :] = jnp.zeros((8, D_MODEL), F32)
            h_s[...] = jnp.zeros_like(h_s)

        xe_s[8:8 + TC, :] = x_ref[...]
        xc = _conv_taps(xe_s, cw_ref, cb_ref)
        sp = _softplus_neg(lam_ref[...])
        for blk in range(LRU_BLOCKS):
            cols = slice(blk * LRU_BLOCK_W, (blk + 1) * LRU_BLOCK_W)
            _, _, ig, a, mult = _lru_gates(xc, blk, wa_ref, wx_ref, ba_ref, bx_ref, sp)
            a_s[:, cols] = a
            u_s[:, cols] = mult * ig * xc[:, cols]

        def step(t, h):
            h = a_s[pl.ds(t, 1), :] * h + u_s[pl.ds(t, 1), :]
            hs_ref[pl.ds(t, 1), :] = h
            return h

        h_s[0:1, :] = lax.fori_loop(0, TC, step, h_s[0:1, :], unroll=8)
        xe_s[0:8, :] = xe_s[TC:TC + 8, :]

    full = lambda shape: pl.BlockSpec(shape, lambda b, c: (0,) * len(shape))
    return _pallas(
        body, name="lru_fwd", grid=(t_tok // seq, nc),
        in_specs=[pl.BlockSpec((TC, D_MODEL), lambda b, c: (b * nc + c, 0)), full((4, D_MODEL)), full((1, D_MODEL)),
                  full((LRU_BLOCKS, LRU_BLOCK_W, LRU_BLOCK_W)), full((1, D_MODEL)),
                  full((LRU_BLOCKS, LRU_BLOCK_W, LRU_BLOCK_W)), full((1, D_MODEL)), full((1, D_MODEL))],
        out_specs=pl.BlockSpec((TC, D_MODEL), lambda b, c: (b * nc + c, 0)),
        out_shape=jax.ShapeDtypeStruct((t_tok, D_MODEL), F32),
        scratch_shapes=[pltpu.VMEM((TC + 8, D_MODEL), F32), pltpu.VMEM((TC, D_MODEL), F32),
                        pltpu.VMEM((TC, D_MODEL), F32), pltpu.VMEM((8, D_MODEL), F32)],
        compiler_params=_cp(("arbitrary", "arbitrary"), VMEM_MID))(proj, cw, cb, w_a, b_a, w_x, b_x, lam)


def _lru_bwd(proj, hs, dyh, cw, cb, w_a, b_a, w_x, b_x, lam, seq):
    t_tok = proj.shape[0]
    nc = seq // TC

    def body(x_ref, xh_ref, g_ref, hs_ref, hh_ref, dy_ref, cw_ref, cb_ref, wa_ref, ba_ref, wx_ref, bx_ref, lam_ref,
             dp_ref, dcw_ref, dvec_ref, dwa_ref, dwx_ref,
             xe_s, he_s, de_s, a_s, r_s, i_s, m_s, dh_s, carry_s):
        b, cr = pl.program_id(0), pl.program_id(1)
        c = nc - 1 - cr

        @pl.when(jnp.logical_and(b == 0, cr == 0))
        def _():
            dcw_ref[...] = jnp.zeros_like(dcw_ref)
            dvec_ref[...] = jnp.zeros_like(dvec_ref)
            dwa_ref[...] = jnp.zeros_like(dwa_ref)
            dwx_ref[...] = jnp.zeros_like(dwx_ref)

        @pl.when(cr == 0)
        def _():
            carry_s[...] = jnp.zeros_like(carry_s)
            de_s[TC:TC + 8, :] = jnp.zeros((8, D_MODEL), F32)

        first = c == 0
        xe_s[0:8, :] = jnp.where(first, 0.0, xh_ref[...])
        xe_s[8:8 + TC, :] = x_ref[...]
        he_s[0:8, :] = jnp.where(first, 0.0, hh_ref[...])
        he_s[8:8 + TC, :] = hs_ref[...]
        xc = _conv_taps(xe_s, cw_ref, cb_ref)
        lam_v = lam_ref[...]
        sp = _softplus_neg(lam_v)
        for blk in range(LRU_BLOCKS):
            cols = slice(blk * LRU_BLOCK_W, (blk + 1) * LRU_BLOCK_W)
            _, r, ig, a, mult = _lru_gates(xc, blk, wa_ref, wx_ref, ba_ref, bx_ref, sp)
            a_s[:, cols], r_s[:, cols], i_s[:, cols], m_s[:, cols] = a, r, ig, mult

        gt = g_ref[...]
        dyh = dy_ref[...]
        dh_s[...] = dyh * _silu(gt)
        dp_ref[:, D_MODEL:] = (dyh * hs_ref[...] * _dsilu(gt)).astype(BF16)

        def step(k, carry):
            t = TC - 1 - k
            dh = dh_s[pl.ds(t, 1), :] + carry
            dh_s[pl.ds(t, 1), :] = dh
            return a_s[pl.ds(t, 1), :] * dh

        carry_s[0:1, :] = lax.fori_loop(0, TC, step, carry_s[0:1, :], unroll=8)

        hprev = he_s[7:7 + TC, :]
        for blk in range(LRU_BLOCKS):
            cols = slice(blk * LRU_BLOCK_W, (blk + 1) * LRU_BLOCK_W)
            xcb = xc[:, cols]
            a, r, ig, mult, dh = a_s[:, cols], r_s[:, cols], i_s[:, cols], m_s[:, cols], dh_s[:, cols]
            spb = sp[:, cols]
            dmult = dh * ig * xcb
            di = dh * mult * xcb
            dxc = dh * mult * ig
            dla = dh * hprev[:, cols] * a - dmult * (a * a) * lax.rsqrt(jnp.maximum(mult * mult, 1e-30))
            dr = dla * (-LRU_C * spb)
            dsp = jnp.sum(dla * (-LRU_C * r), axis=0, keepdims=True)
            dga = dr * r * (1.0 - r)
            dgx = di * ig * (1.0 - ig)
            dga_b, dgx_b = dga.astype(BF16), dgx.astype(BF16)
            xb = xcb.astype(BF16)
            dxc = dxc + _nt(dga_b, wa_ref[blk].astype(BF16)) + _nt(dgx_b, wx_ref[blk].astype(BF16))
            dwa_ref[blk] += _tn(xb, dga_b)
            dwx_ref[blk] += _tn(xb, dgx_b)
            dvec_ref[1:2, cols] += jnp.sum(dga, axis=0, keepdims=True)
            dvec_ref[2:3, cols] += jnp.sum(dgx, axis=0, keepdims=True)
            dvec_ref[3:4, cols] += dsp * (-1.0 / (1.0 + jnp.exp(lam_v[:, cols])))
            de_s[0:TC, cols] = dxc

        dxc = de_s[0:TC, :]
        dvec_ref[0:1, :] += jnp.sum(dxc, axis=0, keepdims=True)
        dxr = dxc * cw_ref[3:4, :]
        dcw_ref[3:4, :] += jnp.sum(dxc * xe_s[8:8 + TC, :], axis=0, keepdims=True)
        for k in range(1, 4):
            dxr = dxr + de_s[k:k + TC, :] * cw_ref[3 - k:4 - k, :]
            dcw_ref[3 - k:4 - k, :] += jnp.sum(dxc * xe_s[8 - k:8 - k + TC, :], axis=0, keepdims=True)
        dp_ref[:, :D_MODEL] = dxr.astype(BF16)
        de_s[TC:TC + 8, :] = de_s[0:8, :]

    chunk = lambda col: pl.BlockSpec((TC, D_MODEL), lambda b, cr: (b * nc + nc - 1 - cr, col))
    halo = lambda col: pl.BlockSpec(
        (8, D_MODEL), lambda b, cr: (jnp.maximum((b * nc + nc - 1 - cr) * (TC // 8) - 1, 0), col))
    full = lambda shape: pl.BlockSpec(shape, lambda b, cr: (0,) * len(shape))
    wblk = (LRU_BLOCKS, LRU_BLOCK_W, LRU_BLOCK_W)
    return _pallas(
        body, name="lru_bwd", grid=(t_tok // seq, nc),
        in_specs=[chunk(0), halo(0), chunk(1), chunk(0), halo(0), chunk(0),
                  full((4, D_MODEL)), full((1, D_MODEL)), full(wblk), full((1, D_MODEL)), full(wblk),
                  full((1, D_MODEL)), full((1, D_MODEL))],
        out_specs=[pl.BlockSpec((TC, 2 * D_MODEL), lambda b, cr: (b * nc + nc - 1 - cr, 0)),
                   full((8, D_MODEL)), full((8, D_MODEL)), full(wblk), full(wblk)],
        out_shape=[jax.ShapeDtypeStruct((t_tok, 2 * D_MODEL), BF16), jax.ShapeDtypeStruct((8, D_MODEL), F32),
                   jax.ShapeDtypeStruct((8, D_MODEL), F32), jax.ShapeDtypeStruct(wblk, F32),
                   jax.ShapeDtypeStruct(wblk, F32)],
        scratch_shapes=[pltpu.VMEM((TC + 8, D_MODEL), F32), pltpu.VMEM((TC + 8, D_MODEL), F32),
                        pltpu.VMEM((TC + 8, D_MODEL), F32)]
        + [pltpu.VMEM((TC, D_MODEL), F32)] * 5 + [pltpu.VMEM((8, D_MODEL), F32)],
        compiler_params=_cp(("arbitrary", "arbitrary"), VMEM_BIG),
    )(proj, proj, proj, hs, hs, dyh, cw, cb, w_a, b_a, w_x, b_x, lam)


def _last_layer_tail(hs, proj, w_out, x, gmod, final_g, target, seq):
    t_tok = x.shape[0]
    tiles_per_seq = seq // TM

    def body(hs_ref, g_ref, w_ref, x_ref, gm_ref, fg_ref, t_ref,
             yg_ref, dx_ref, dy_ref, dyg_ref, dgm_ref, loss_ref, dfg_ref):
        i = pl.program_id(0)

        @pl.when(i == 0)
        def _():
            loss_ref[...] = jnp.zeros_like(loss_ref)
            dfg_ref[...] = jnp.zeros_like(dfg_ref)

        @pl.when(i % tiles_per_seq == 0)
        def _():
            dgm_ref[...] = jnp.zeros_like(dgm_ref)

        gm = gm_ref[...]
        yg = (hs_ref[...] * _silu(g_ref[...])).astype(BF16)
        yg_ref[...] = yg
        y = _nn(yg, w_ref[...])
        xv = x_ref[...] + gm * y
        gv = fg_ref[...]
        rstd = lax.rsqrt(jnp.mean(xv * xv, axis=-1, keepdims=True) + EPS)
        xhat = xv * rstd
        err = xhat * gv - t_ref[...]
        loss_ref[0:1, :] += jnp.sum(err * err, axis=0, keepdims=True) * (0.5 / D_MODEL)
        dout = err * (1.0 / D_MODEL)
        dfg_ref[0:1, :] += jnp.sum(dout * xhat, axis=0, keepdims=True)
        dxhat = dout * gv
        dxv = rstd * (dxhat - xhat * jnp.mean(dxhat * xhat, axis=-1, keepdims=True))
        dx_ref[...] = dxv
        dgm_ref[...] += jnp.sum(dxv * y, axis=0, keepdims=True)
        dy = (dxv * gm).astype(BF16)
        dy_ref[...] = dy
        dyg_ref[...] = _nt(dy, w_ref[...])

    row = pl.BlockSpec((TM, D_MODEL), lambda i: (i, 0))
    acc = pl.BlockSpec((8, D_MODEL), lambda i: (0, 0))
    mod_spec = pl.BlockSpec((None, 1, D_MODEL), lambda i: (i * TM // seq, 0, 0))
    return _pallas(
        body, name="last_layer_tail", grid=(t_tok // TM,),
        in_specs=[row, pl.BlockSpec((TM, D_MODEL), lambda i: (i, 1)), pl.BlockSpec((D_MODEL, D_MODEL), lambda i: (0, 0)),
                  row, mod_spec, pl.BlockSpec((1, D_MODEL), lambda i: (0, 0)), row],
        out_specs=[row, row, row, row, mod_spec, acc, acc],
        out_shape=[jax.ShapeDtypeStruct((t_tok, D_MODEL), BF16), jax.ShapeDtypeStruct((t_tok, D_MODEL), F32),
                   jax.ShapeDtypeStruct((t_tok, D_MODEL), BF16), jax.ShapeDtypeStruct((t_tok, D_MODEL), F32),
                   jax.ShapeDtypeStruct(gmod.shape, F32), jax.ShapeDtypeStruct((8, D_MODEL), F32),
                   jax.ShapeDtypeStruct((8, D_MODEL), F32)],
        compiler_params=_cp(("arbitrary",), VMEM_BIG))(hs, proj, w_out, x, gmod, final_g, target)


def _final_loss(x, g, target):
    t_tok = x.shape[0]

    def body(x_ref, g_ref, t_ref, dx_ref, loss_ref, dg_ref):
        @pl.when(pl.program_id(0) == 0)
        def _():
            loss_ref[...] = jnp.zeros_like(loss_ref)
            dg_ref[...] = jnp.zeros_like(dg_ref)

        xv = x_ref[...]
        gv = g_ref[...]
        rstd = lax.rsqrt(jnp.mean(xv * xv, axis=-1, keepdims=True) + EPS)
        xhat = xv * rstd
        err = xhat * gv - t_ref[...]
        loss_ref[0:1, :] += jnp.sum(err * err, axis=0, keepdims=True) * (0.5 / D_MODEL)
        dout = err * (1.0 / D_MODEL)
        dg_ref[0:1, :] += jnp.sum(dout * xhat, axis=0, keepdims=True)
        dxhat = dout * gv
        dx_ref[...] = rstd * (dxhat - xhat * jnp.mean(dxhat * xhat, axis=-1, keepdims=True))

    row = pl.BlockSpec((TM, D_MODEL), lambda i: (i, 0))
    acc = pl.BlockSpec((8, D_MODEL), lambda i: (0, 0))
    return _pallas(body, name="final_loss", grid=(t_tok // TM,),
                   in_specs=[row, pl.BlockSpec((1, D_MODEL), lambda i: (0, 0)), row],
                   out_specs=[row, acc, acc],
                   out_shape=[jax.ShapeDtypeStruct((t_tok, D_MODEL), F32)] + [jax.ShapeDtypeStruct((8, D_MODEL), F32)] * 2,
                   compiler_params=_cp(("arbitrary",), VMEM_MID))(x, g, target)


def _adam_math(w, g, m, v):
    m_new = ADAM_B1 * m + (1.0 - ADAM_B1) * g
    v_new = ADAM_B2 * v + (1.0 - ADAM_B2) * (g * g)
    m_hat = m_new / (1.0 - ADAM_B1 ** ADAM_STEP)
    v_hat = v_new / (1.0 - ADAM_B2 ** ADAM_STEP)
    delta = -ADAM_LR * (m_hat / (jnp.sqrt(v_hat) + ADAM_EPS) + ADAM_WD * w)
    return delta, m_new, v_new


def _sum_leading(name, x, out_dtype=F32):
    n, rows, cols = x.shape
    tr = PACK_ROWS if rows % PACK_ROWS == 0 else rows

    def body(x_ref, o_ref):
        acc = x_ref[0].astype(F32)
        for d in range(1, n):
            acc = acc + x_ref[d].astype(F32)
        o_ref[...] = acc.astype(out_dtype)

    return _pallas(body, name=name, grid=(rows // tr,),
                   in_specs=[pl.BlockSpec((n, tr, cols), lambda i: (0, i, 0))],
                   out_specs=pl.BlockSpec((tr, cols), lambda i: (i, 0)),
                   out_shape=jax.ShapeDtypeStruct((rows, cols), out_dtype),
                   compiler_params=_cp(("arbitrary",), VMEM_MID))(x)


def _adamw(name, w, m, v, g=None, parts=None):
    rows, cols = w.shape
    tr = rows if rows <= 256 else 256

    def body(*refs):
        w_ref, m_ref, v_ref, g_in, g_ref, d_ref, mo_ref, vo_ref = refs
        if parts is None:
            gv = g_in[...]
        else:
            acc = g_in[0].astype(F32)
            for d in range(1, parts.shape[0]):
                acc = acc + g_in[d].astype(F32)
            gv = acc[:, :cols]
        delta, m_new, v_new = _adam_math(w_ref[...], gv, m_ref[...], v_ref[...])
        g_ref[...] = gv
        d_ref[...] = delta
        mo_ref[...] = m_new
        vo_ref[...] = v_new

    row = pl.BlockSpec((tr, cols), lambda i: (i, 0))
    if parts is None:
        g_spec, g_arg = row, g
    else:
        g_spec, g_arg = pl.BlockSpec((parts.shape[0], tr, parts.shape[2]), lambda i: (0, i, 0)), parts
    return _pallas(body, name=name, grid=(rows // tr,), in_specs=[row, row, row, g_spec], out_specs=[row] * 4,
                   out_shape=[jax.ShapeDtypeStruct((rows, cols), F32)] * 4,
                   compiler_params=_cp(("arbitrary",), VMEM_MID))(w, m, v, g_arg)


def _pack_rows(arrs):
    rows, meta, total = [], [], 0
    for a in arrs:
        flat = a.reshape(-1)
        nrow = -(-flat.shape[0] // 1024) * 8
        rows.append(jnp.pad(flat, (0, nrow * 128 - flat.shape[0])).reshape(nrow, 128))
        meta.append((a.shape, flat.shape[0], nrow))
        total += nrow
    tail = -total % PACK_ROWS
    if tail:
        rows.append(jnp.zeros((tail, 128), F32))
    return jnp.concatenate(rows, axis=0), meta


def _unpack_rows(packed, meta):
    out, r0 = [], 0
    for shape, size, nrow in meta:
        out.append(packed[r0:r0 + nrow].reshape(-1)[:size].reshape(shape))
        r0 += nrow
    return out


WEIGHTS = ["rel_bias", "norm_g", "ada_w", "ada_b", "attn_w_in", "attn_sinks", "attn_b_f", "attn_w_out", "lru_w_in",
           "lru_conv_w", "lru_conv_b", "lru_w_a", "lru_b_a", "lru_w_x", "lru_b_x", "lru_lambda", "lru_w_out", "final_g"]
BIG = ["ada_w", "attn_w_in", "attn_w_out", "lru_w_in", "lru_w_out"]
PACK_ROWS = 256


def kernel(x, c, rel_bias, norm_g, ada_w, ada_b, attn_w_in, attn_sinks, attn_b_f, attn_w_out, lru_w_in, lru_conv_w, lru_conv_b, lru_w_a, lru_b_a, lru_w_x, lru_b_x, lru_lambda, lru_w_out, final_g, loss_target, m_rel_bias, m_norm_g, m_ada_w, m_ada_b, m_attn_w_in, m_attn_sinks, m_attn_b_f, m_attn_w_out, m_lru_w_in, m_lru_conv_w, m_lru_conv_b, m_lru_w_a, m_lru_b_a, m_lru_w_x, m_lru_b_x, m_lru_lambda, m_lru_w_out, m_final_g, v_rel_bias, v_norm_g, v_ada_w, v_ada_b, v_attn_w_in, v_attn_sinks, v_attn_b_f, v_attn_w_out, v_lru_w_in, v_lru_conv_w, v_lru_conv_b, v_lru_w_a, v_lru_b_a, v_lru_w_x, v_lru_b_x, v_lru_lambda, v_lru_w_out, v_final_g):
    nseq, seq, _ = x.shape
    t_tok = nseq * seq
    me = 4 * lax.axis_index("x") + 2 * lax.axis_index("y") + lax.axis_index("c")
    x0 = x.reshape(t_tok, D_MODEL)
    target = loss_target.reshape(t_tok, D_MODEL)

    w_in_pad = jnp.pad(attn_w_in[0].astype(BF16), ((0, 0), (0, SHARD_W_PAD - SHARD_W_IN)))
    vec_shard = jnp.concatenate([lru_conv_w[0], lru_conv_b, lru_b_a, lru_b_x, lru_lambda], axis=0)
    g_w_in, g_vec, g_c = _exchange("gather_first", [w_in_pad, vec_shard, c], [])
    later_w = [attn_w_out[0].astype(BF16), lru_w_in[0].astype(BF16), lru_w_out[0].astype(BF16)]
    later_handle, later_token = _exchange_start("gather_later_start", later_w, [], after=g_vec)
    w_full = jnp.transpose(g_w_in[:, :, :SHARD_W_IN], (1, 0, 2)).reshape(D_MODEL, N_DEV * SHARD_W_IN)
    w_aq, w_ak, w_av = w_full[:, 0:512], w_full[:, 512:640], w_full[:, 640:768]
    w_bq, w_bk, w_bv = w_full[:, 768:1280], w_full[:, 1280:1792], w_full[:, 1792:2304]
    w_f, w_gate = w_full[:, 2304:2312], w_full[:, 2312:3336]
    w_main = jnp.concatenate([w_bq, w_bk, w_bv, w_aq, w_gate, w_ak, w_av], axis=1)
    wf_t = jnp.transpose(w_f)
    vec_full = jnp.transpose(g_vec, (1, 0, 2)).reshape(8, D_MODEL)
    conv_w, conv_b, b_a, b_x, lam = vec_full[0:4], vec_full[4:5], vec_full[5:6], vec_full[6:7], vec_full[7:8]
    c_all = g_c.reshape(N_DEV * nseq, D_MODEL)

    ncol = ada_w.shape[2]
    ada_b_slice = lax.dynamic_slice(ada_b.reshape(2, N_DEV, ncol), (0, me, 0), (2, 1, ncol))
    mod_part = _ada_mod(c_all, ada_w, ada_b_slice)
    (g_mod,) = _exchange("gather_mod", [mod_part], [])
    mine = lax.dynamic_slice(g_mod, (0, 0, me * nseq, 0), (N_DEV, 2, nseq, ncol))
    mod = jnp.transpose(mine, (1, 2, 0, 3)).reshape(2, nseq, 3 * D_MODEL)
    shift = [mod[l, :, 0:D_MODEL].reshape(nseq, 1, D_MODEL) for l in range(2)]
    scale = [mod[l, :, D_MODEL:2 * D_MODEL].reshape(nseq, 1, D_MODEL) for l in range(2)]
    gmod = [mod[l, :, 2 * D_MODEL:].reshape(nseq, 1, D_MODEL) for l in range(2)]

    onehot = _bucket_onehot()
    bias = _bias_expand(jnp.transpose(rel_bias), onehot).reshape(N_HEADS, BLOCK, 2 * BLOCK)
    sinks = attn_sinks.reshape(N_HEADS)
    b_f = attn_b_f.reshape(N_HEADS, 1)
    norm_g0 = norm_g[0:1] + later_token[0:1, 0:1]
    h0, qkvg, fl_t = _norm_proj("norm_proj0", x0, norm_g0, shift[0], scale[0], w_main, seq, BF16, wf_t=wf_t)
    f_row, f_col = _fox_prep(fl_t, b_f, seq)
    a_out, lse_a = _swa_fwd(qkvg, bias, sinks, seq)
    q_aug, k_aug, kt_aug, vt = _fox_aug(qkvg, f_col, seq)
    b_out, lse_b = _fox_fwd_t(q_aug, k_aug, vt, seq)
    g_later = _exchange_wait("gather_later_wait", later_handle, after=lse_b)
    w_out0, g_lru_in, w_out1 = (_with_own(g, w, me) for g, w in zip(g_later, later_w))
    w_out0, w_out1 = w_out0.reshape(D_MODEL, D_MODEL), w_out1.reshape(D_MODEL, D_MODEL)
    yg0, y0, x1 = _out_proj("out_proj0", [a_out, b_out], qkvg, C_GATE // D_MODEL, w_out0, x0, gmod[0], seq)

    h1, proj1 = _norm_proj("norm_proj1", x1, norm_g[1:2], shift[1], scale[1], g_lru_in, seq, F32)
    hs = _lru_fwd(proj1, conv_w, conv_b, lru_w_a[0], b_a, lru_w_x[0], b_x, lam, seq)

    yg1, dx2, dy1, dyh, dgm1, loss_rows, dfinal_rows = _last_layer_tail(
        hs, proj1, w_out1, x1, gmod[1], final_g.reshape(1, D_MODEL), target, seq)
    loss = lax.psum(jnp.sum(loss_rows[0]), ("x", "y", "c"))

    dproj1, dcw, dvec, dw_a, dw_x = _lru_bwd(proj1, hs, dyh, conv_w, conv_b, lru_w_a[0], b_a, lru_w_x[0], b_x, lam, seq)
    dx1, dss1, dg1 = _norm_bwd("norm1_bwd", [(dproj1, 0)], g_lru_in, x1, norm_g[1:2], scale[1], dx2, seq)
    (p_w_out1,) = _dw("dw_out1", yg1, [dy1])
    (p_lru_in,) = _dw("dw_lru_in", h1, [dproj1], blocked=2 * D_MODEL // N_DEV)

    rows_out = D_MODEL // N_DEV
    gpack1, gmeta1 = _pack_rows([dcw[0:4], dvec[0:4], dg1[0], dfinal_rows[0]])
    dwax = jnp.stack([dw_a, dw_x]).astype(BF16)
    own1 = [gpack1, dwax, p_lru_in, p_w_out1.reshape(N_DEV, rows_out, D_MODEL)]
    grads1_handle, grads1_token = _exchange_start("grads1_start", own1[:2], own1[2:], after=dx1)

    gmod0 = gmod[0] + grads1_token[0:1, 0:1]
    dy0, dgm0, du_a, du_b, dgate = _out_proj_bwd("out_proj0_bwd", dx1, gmod0, y0, w_out0, seq,
                                                  attn=(a_out, b_out, qkvg))
    dq_a, dkv_a, dbias, dsink = _swa_bwd(qkvg, du_a, a_out, lse_a, bias, sinks, seq)
    dq_b, dk_b, dv_b, df4 = _fox_bwd_t(q_aug, k_aug, kt_aug, qkvg, du_b, b_out, lse_b, seq)
    dfl_t, db_f = _fox_post(df4.reshape(N_HEADS, t_tok), fl_t, b_f, seq)
    parts0 = [(dq_b, C_BQ), (dk_b, C_BK), (dv_b, C_BV), (dq_a, C_AQ), (dgate, C_GATE), (dkv_a, C_AK)]
    (p_w_out0,) = _dw("dw_out0", yg0, [dy0])
    pw_bq, pw_bk, pw_bv, pw_aq, pw_gate, pw_akv = _dw("dw_attn_in", h0, [p for p, _ in parts0])
    pw_f = _dw_rows("dw_f", dfl_t, h0)

    p_w_in = jnp.concatenate([pw_aq, pw_akv, pw_bq, pw_bk, pw_bv, jnp.transpose(pw_f).astype(BF16), pw_gate], axis=1)
    p_w_in = jnp.transpose(p_w_in.reshape(D_MODEL, N_DEV, SHARD_W_IN), (1, 0, 2))
    p_w_in = jnp.pad(p_w_in, ((0, 0), (0, 0), (0, SHARD_W_PAD - SHARD_W_IN)))
    own0 = [p_w_in, p_w_out0.reshape(N_DEV, rows_out, D_MODEL)]
    landed1 = _exchange_wait("grads1_wait", grads1_handle, after=p_w_in)
    grads0_handle, grads0_token = _exchange_start("grads0_start", [], own0, after=landed1[0])
    scale0 = scale[0] + grads0_token[0:1, 0:1]
    dx0, dss0, dg0 = _norm_bwd("norm0_bwd", parts0, w_main, x0, norm_g[0:1], scale0, dx1, seq,
                               rows_part=(dfl_t, wf_t))
    dbias_t = _bias_reduce(dbias.reshape(N_HEADS, BLOCK * 2 * BLOCK), onehot)

    gpack0, gmeta0 = _pack_rows([jnp.transpose(dbias_t), dg0[0], dsink[:, 0], db_f[:, 0]])
    dmod = jnp.stack([jnp.concatenate([dss[:, 0], dss[:, 1], dgm[:, 0]], axis=1)
                      for dss, dgm in ((dss0, dgm0), (dss1, dgm1))], axis=1)
    g_small0, g_dmod = _exchange("exchange_small", [gpack0, dmod], [])
    landed0 = _exchange_wait("grads0_wait", grads0_handle, after=g_small0)
    r_w_in, r_w_out0 = (_with_own(g, lax.dynamic_index_in_dim(a, me, 0, keepdims=False), me)
                        for g, a in zip(landed0, own0))
    g_small1, g_dwax = (_with_own(g, a, me) for g, a in zip(landed1[:2], own1[:2]))
    r_lru_in, r_w_out1 = (_with_own(g, lax.dynamic_index_in_dim(a, me, 0, keepdims=False), me)
                          for g, a in zip(landed1[2:], own1[2:]))

    d_rel, d_g0, d_sinks, d_b_f = _unpack_rows(_sum_leading("sum_small0", g_small0), gmeta0)
    d_cw, d_vec, d_g1, d_final_g = _unpack_rows(_sum_leading("sum_small1", g_small1), gmeta1)
    d_norm_g = jnp.stack([d_g0, d_g1])
    d_wax = _sum_leading("sum_dwax", g_dwax.reshape(N_DEV, 2 * LRU_BLOCKS * LRU_BLOCK_W, LRU_BLOCK_W))
    d_wa, d_wx = d_wax[:LRU_BLOCKS * LRU_BLOCK_W], d_wax[LRU_BLOCKS * LRU_BLOCK_W:]
    cols = lambda a: lax.dynamic_slice(a, (0, me * LRU_BLOCK_W), (a.shape[0], LRU_BLOCK_W))
    dmod_all = g_dmod.reshape(N_DEV * nseq, 2 * 3 * D_MODEL)
    d_ada_b = _sum_leading("sum_ada_b", dmod_all.reshape(N_DEV * nseq, 2 * 3 * D_MODEL // 128, 128)).reshape(2, 3 * D_MODEL)
    dmod_slice = lax.dynamic_slice(dmod_all.reshape(N_DEV * nseq, 2, N_DEV, ncol), (0, 0, me, 0),
                                   (N_DEV * nseq, 2, 1, ncol)).reshape(N_DEV * nseq, 2, ncol)
    d_ada_w = _ada_w_grad(c_all, jnp.transpose(dmod_slice, (1, 0, 2)))

    given = dict(
        rel_bias=(rel_bias, m_rel_bias, v_rel_bias), norm_g=(norm_g, m_norm_g, v_norm_g),
        ada_w=(ada_w, m_ada_w, v_ada_w), ada_b=(ada_b, m_ada_b, v_ada_b),
        attn_w_in=(attn_w_in, m_attn_w_in, v_attn_w_in), attn_sinks=(attn_sinks, m_attn_sinks, v_attn_sinks),
        attn_b_f=(attn_b_f, m_attn_b_f, v_attn_b_f), attn_w_out=(attn_w_out, m_attn_w_out, v_attn_w_out),
        lru_w_in=(lru_w_in, m_lru_w_in, v_lru_w_in), lru_conv_w=(lru_conv_w, m_lru_conv_w, v_lru_conv_w),
        lru_conv_b=(lru_conv_b, m_lru_conv_b, v_lru_conv_b), lru_w_a=(lru_w_a, m_lru_w_a, v_lru_w_a),
        lru_b_a=(lru_b_a, m_lru_b_a, v_lru_b_a), lru_w_x=(lru_w_x, m_lru_w_x, v_lru_w_x),
        lru_b_x=(lru_b_x, m_lru_b_x, v_lru_b_x), lru_lambda=(lru_lambda, m_lru_lambda, v_lru_lambda),
        lru_w_out=(lru_w_out, m_lru_w_out, v_lru_w_out), final_g=(final_g, m_final_g, v_final_g))
    results = {}

    def big(name, shape2d, g=None, parts=None):
        w, m, v = (a.reshape(shape2d) for a in given[name])
        outs = _adamw("adamw_" + name, w, m, v, g=g, parts=parts)
        results[name] = tuple(o.reshape(given[name][0].shape) for o in outs)

    big("ada_w", (2 * D_MODEL, ncol), g=d_ada_w.reshape(2 * D_MODEL, ncol))
    big("attn_w_in", (D_MODEL, SHARD_W_IN), parts=r_w_in)
    big("attn_w_out", (rows_out, D_MODEL), parts=r_w_out0)
    big("lru_w_in", (D_MODEL, 2 * D_MODEL // N_DEV), parts=r_lru_in)
    big("lru_w_out", (rows_out, D_MODEL), parts=r_w_out1)

    small_grads = dict(
        rel_bias=d_rel, norm_g=d_norm_g, ada_b=d_ada_b, attn_sinks=d_sinks.reshape(1, N_HEADS),
        attn_b_f=d_b_f.reshape(1, N_HEADS), lru_conv_w=cols(d_cw).reshape(1, 4, LRU_BLOCK_W),
        lru_conv_b=cols(d_vec[0:1]), lru_w_a=d_wa.reshape(lru_w_a.shape), lru_b_a=cols(d_vec[1:2]),
        lru_w_x=d_wx.reshape(lru_w_x.shape), lru_b_x=cols(d_vec[2:3]), lru_lambda=cols(d_vec[3:4]),
        final_g=d_final_g)
    small = [n for n in WEIGHTS if n not in BIG]
    wpack, smeta = _pack_rows([given[n][0] for n in small])
    mpack, _ = _pack_rows([given[n][1] for n in small])
    vpack, _ = _pack_rows([given[n][2] for n in small])
    gpack2, _ = _pack_rows([small_grads[n] for n in small])
    packs = _adamw("adamw_small", wpack, mpack, vpack, g=gpack2)
    unpacked = [_unpack_rows(p, smeta) for p in packs]
    for k, n in enumerate(small):
        results[n] = tuple(unpacked[j][k] for j in range(4))

    grad_x = dx0.reshape(x.shape)
    out = [loss, grad_x]
    for j in range(4):
        out += [results[n][j] for n in WEIGHTS]
    return tuple(out)
```

```python
import functools
import math

import jax
import jax.numpy as jnp
from jax import lax
from jax.experimental import pallas as pl
from jax.experimental.pallas import tpu as pltpu

F32 = jnp.float32
BF16 = jnp.bfloat16
HI = lax.Precision.HIGHEST
MESH = pl.DeviceIdType.MESH

N_DEV = 8
D_MODEL = 1024
HEAD_DIM = 64
N_HEADS = 8
KV_GROUP = 4
BLOCK = 128
REL_BUCKETS = 32
REL_MAX_EXACT = 16
REL_MAX_DIST = 128
LRU_BLOCKS = 8
LRU_BLOCK_W = 128
LRU_C = 8.0
EPS = 1e-6
SCALE = HEAD_DIM ** -0.5
NEG = -1e30

ADAM_LR = 0.001
ADAM_B1 = 0.9
ADAM_B2 = 0.999
ADAM_EPS = 1e-08
ADAM_WD = 0.01
ADAM_STEP = 10

C_BQ, C_BK, C_BV, C_AQ, C_GATE, C_AK, C_AV = 0, 512, 1024, 1536, 2048, 3072, 3200
N_MAIN = 3328
SHARD_W_IN = 417
SHARD_W_PAD = 512

TM = 256
TQ = 256
TK = 128
TKB = 256
TC = 256
SWA_SUB = 2
VMEM_BIG = 56 * 1024 * 1024
VMEM_MID = 40 * 1024 * 1024


def _pallas(body, **kw):
    return pl.pallas_call(body, **kw)


def _cp(sem=None, vmem=None):
    kw = {}
    if sem is not None:
        kw["dimension_semantics"] = sem
    if vmem is not None:
        kw["vmem_limit_bytes"] = vmem
    return pltpu.CompilerParams(**kw)


def _nn(a, b, precision=None):
    return jnp.dot(a, b, preferred_element_type=F32, precision=precision)


def _nt(a, b, precision=None):
    return lax.dot_general(a, b, (((1,), (1,)), ((), ())), preferred_element_type=F32, precision=precision)


def _tn(a, b, precision=None):
    return lax.dot_general(a, b, (((0,), (0,)), ((), ())), preferred_element_type=F32, precision=precision)


def _sigmoid(x):
    return 1.0 / (1.0 + jnp.exp(-x))


def _silu(x):
    return x * _sigmoid(x)


def _dsilu(x):
    s = _sigmoid(x)
    return s * (1.0 + x * (1.0 - s))


def _neg_expm1(x):
    poly = x * (1.0 + x * (0.5 + x * (1.0 / 6.0 + x * (1.0 / 24.0))))
    return -jnp.where(jnp.abs(x) < 0.05, poly, jnp.exp(x) - 1.0)


def _col(tile, idx):
    lane = lax.broadcasted_iota(jnp.int32, tile.shape, 1)
    return jnp.sum(jnp.where(lane == idx, tile, 0.0), axis=1, keepdims=True)


def _row(tile, idx):
    sub = lax.broadcasted_iota(jnp.int32, tile.shape, 0)
    return jnp.sum(jnp.where(sub == idx, tile, 0.0), axis=0, keepdims=True)


def _exchange(name, gathers, scatters, axes=("x", "y", "c"), chunks=1):
    ng, n = len(gathers), len(gathers) + len(scatters)
    ins = list(gathers) + list(scatters)
    group = 2 ** len(axes)

    def body(*refs):
        in_refs, out_refs = refs[:n], refs[n:2 * n]
        send_sems, recv_sems, loc_sems = refs[2 * n:]
        coord = {a: lax.axis_index(a) for a in ("x", "y", "c")}

        def member(r):
            pc = dict(coord)
            idx = 0
            for k, a in enumerate(axes):
                if r & (1 << (len(axes) - 1 - k)):
                    pc[a] = 1 - coord[a]
                idx = 2 * idx + pc[a]
            return (pc["x"], pc["y"], pc["c"]), idx

        _, me = member(0)

        def peer(r):
            return member(r)

        local, sends, recvs = [], [], []
        for k in range(n):
            mine = in_refs[k] if k < ng else in_refs[k].at[me]
            cp = pltpu.make_async_copy(mine, out_refs[k].at[me], loc_sems.at[k])
            cp.start()
            local.append(cp)
            lead = mine.shape[0]
            nchunk = max(q for q in range(1, chunks + 1) if lead % q == 0)
            step = lead // nchunk
            for r in range(1, group):
                pid, pidx = peer(r)
                src = in_refs[k] if k < ng else in_refs[k].at[pidx]
                for q in range(nchunk):
                    rows = pl.ds(q * step, step)
                    sems = dict(send_sem=send_sems.at[r - 1, k, q], recv_sem=recv_sems.at[r - 1, k, q],
                                device_id=pid, device_id_type=MESH)
                    snd = pltpu.make_async_remote_copy(src_ref=src.at[rows], dst_ref=out_refs[k].at[me].at[rows], **sems)
                    snd.start()
                    sends.append(snd)
                    recvs.append(pltpu.make_async_remote_copy(
                        src_ref=src.at[rows], dst_ref=out_refs[k].at[pidx].at[rows], **sems))
        for rc in recvs:
            rc.wait_recv()
        for snd in sends:
            snd.wait_send()
        for cp in local:
            cp.wait()

    out_shape = [jax.ShapeDtypeStruct((group,) + a.shape, a.dtype) for a in gathers]
    out_shape += [jax.ShapeDtypeStruct(a.shape, a.dtype) for a in scatters]
    any_spec = pl.BlockSpec(memory_space=pl.ANY)
    return _pallas(
        body, name=name, out_shape=out_shape,
        in_specs=[any_spec] * n, out_specs=[any_spec] * n,
        scratch_shapes=[pltpu.SemaphoreType.DMA((group - 1, n, chunks)), pltpu.SemaphoreType.DMA((group - 1, n, chunks)),
                        pltpu.SemaphoreType.DMA((n,))],
    )(*ins)


def _peer_of(r):
    x, y, c = lax.axis_index("x"), lax.axis_index("y"), lax.axis_index("c")
    px = 1 - x if r & 4 else x
    py = 1 - y if r & 2 else y
    pc = 1 - c if r & 1 else c
    return (px, py, pc), 4 * px + 2 * py + pc


def _split_copies(in_refs, land_refs, send_sems, recv_sems, ng, with_recv):
    _, me = _peer_of(0)
    pairs = []
    for k, (src_ref, land) in enumerate(zip(in_refs, land_refs)):
        for r in range(1, N_DEV):
            pid, pidx = _peer_of(r)
            src = src_ref if k < ng else src_ref.at[pidx]
            slot = (N_DEV - 1) * k + r - 1
            sems = dict(send_sem=send_sems.at[slot], recv_sem=recv_sems.at[slot], device_id=pid, device_id_type=MESH)
            send = pltpu.make_async_remote_copy(src_ref=src, dst_ref=land.at[me], **sems)
            recv = pltpu.make_async_remote_copy(src_ref=src, dst_ref=land.at[pidx], **sems) if with_recv else None
            pairs.append((send, recv))
    return pairs


def _exchange_start(name, gathers, scatters, after):
    ng, n = len(gathers), len(gathers) + len(scatters)
    ins = list(gathers) + list(scatters)
    lands = [jax.ShapeDtypeStruct((N_DEV,) + a.shape, a.dtype) for a in gathers]
    lands += [jax.ShapeDtypeStruct(a.shape, a.dtype) for a in scatters]

    def body(*refs):
        in_refs, land_refs = refs[:n], refs[n:2 * n]
        send_sems, recv_sems = refs[2 * n + 1:2 * n + 3]
        token = refs[-1]
        for send, _ in _split_copies(in_refs, land_refs, send_sems, recv_sems, ng, False):
            send.start()
        token[...] = jnp.zeros_like(token)

    hbm = pl.BlockSpec(memory_space=pltpu.HBM)
    sem = pl.BlockSpec(memory_space=pltpu.SEMAPHORE)
    sem_shape = pltpu.SemaphoreType.DMA(((N_DEV - 1) * n,))
    out_shape = [sem_shape, sem_shape] + [pltpu.HBM(a.shape, a.dtype) for a in ins]
    out_shape += [pltpu.HBM(l.shape, l.dtype) for l in lands] + [jax.ShapeDtypeStruct((8, 128), F32)]
    args = [pltpu.with_memory_space_constraint(a, pltpu.HBM) for a in ins]
    args += [pltpu.with_memory_space_constraint(lax.empty(l.shape, l.dtype), pltpu.HBM) for l in lands]
    outs = _pallas(
        body, name=name, out_shape=out_shape,
        in_specs=[hbm] * (2 * n) + [pl.BlockSpec(memory_space=pl.ANY)],
        out_specs=[sem, sem] + [hbm] * (2 * n) + [pl.BlockSpec(memory_space=pltpu.VMEM)],
        input_output_aliases={i: 2 + i for i in range(2 * n)},
        compiler_params=pltpu.CompilerParams(has_side_effects=pltpu.SideEffectType.DATAFLOW_SIDE_EFFECTING),
    )(*args, after)
    return (outs[0], outs[1], list(outs[2:2 + n]), list(outs[2 + n:2 + 2 * n]), ng), outs[-1]


def _exchange_wait(name, handle, after):
    send_sems, recv_sems, srcs, lands, ng = handle
    n = len(srcs)

    def body(*refs):
        in_refs, land_refs = refs[:n], refs[n:2 * n]
        send_ref, recv_ref = refs[2 * n:2 * n + 2]
        for send, recv in _split_copies(in_refs, land_refs, send_ref, recv_ref, ng, True):
            send.wait_send()
            recv.wait_recv()

    hbm = pl.BlockSpec(memory_space=pltpu.HBM)
    sem = pl.BlockSpec(memory_space=pltpu.SEMAPHORE)
    outs = _pallas(
        body, name=name, out_shape=[pltpu.HBM(a.shape, a.dtype) for a in srcs + lands],
        in_specs=[hbm] * (2 * n) + [sem, sem, pl.BlockSpec(memory_space=pl.ANY)],
        out_specs=[hbm] * (2 * n), input_output_aliases={i: i for i in range(2 * n)},
        compiler_params=pltpu.CompilerParams(has_side_effects=pltpu.SideEffectType.DATAFLOW_SIDE_EFFECTING),
    )(*srcs, *lands, send_sems, recv_sems, after)
    return list(outs[n:])


def _with_own(land, own, me):
    return lax.dynamic_update_slice(land, own[None], (me,) + (0,) * own.ndim)


def _ada_mod(c_all, ada_w, ada_b_slice):
    def body(c_ref, w_ref, b_ref, o_ref):
        ca = _silu(c_ref[...])
        for l in range(2):
            o_ref[l] = _nn(ca, w_ref[l], HI) + b_ref[l]

    return _pallas(body, name="ada_mod",
                   out_shape=jax.ShapeDtypeStruct((2, c_all.shape[0], ada_w.shape[2]), F32),
                   compiler_params=_cp(vmem=VMEM_MID))(c_all, ada_w, ada_b_slice)


def _ada_w_grad(c_all, dmod_slice):
    def body(c_ref, d_ref, o_ref):
        ca = _silu(c_ref[...])
        for l in range(2):
            o_ref[l] = _tn(ca, d_ref[l], HI)

    return _pallas(body, name="ada_w_grad",
                   out_shape=jax.ShapeDtypeStruct((2, D_MODEL, dmod_slice.shape[2]), F32),
                   compiler_params=_cp(vmem=VMEM_MID))(c_all, dmod_slice)


def _bucket_onehot():
    qi = jnp.arange(BLOCK)[:, None]
    kj = jnp.arange(2 * BLOCK)[None, :]
    rel = qi - kj + BLOCK
    n = jnp.maximum(rel, 0)
    nf = jnp.maximum(n, 1).astype(F32)
    large = REL_MAX_EXACT + (jnp.log(nf / REL_MAX_EXACT) / math.log(REL_MAX_DIST / REL_MAX_EXACT)
                             * (REL_BUCKETS - REL_MAX_EXACT)).astype(jnp.int32)
    large = jnp.minimum(large, REL_BUCKETS - 1)
    bucket = jnp.where(n < REL_MAX_EXACT, n, large).reshape(1, BLOCK * 2 * BLOCK)
    return (jnp.arange(REL_BUCKETS)[:, None] == bucket).astype(F32)


def _bias_expand(rel_bias_t, onehot):
    def body(r_ref, e_ref, o_ref):
        o_ref[...] = _nn(r_ref[...], e_ref[...], HI)

    return _pallas(body, name="bias_expand",
                   out_shape=jax.ShapeDtypeStruct((N_HEADS, onehot.shape[1]), F32),
                   compiler_params=_cp(vmem=VMEM_MID))(rel_bias_t, onehot)


def _bias_reduce(dbias, onehot):
    def body(d_ref, e_ref, o_ref):
        o_ref[...] = _nt(d_ref[...], e_ref[...], HI)

    return _pallas(body, name="bias_reduce",
                   out_shape=jax.ShapeDtypeStruct((N_HEADS, REL_BUCKETS), F32),
                   compiler_params=_cp(vmem=VMEM_MID))(dbias, onehot)


def _norm_proj(name, x, g, shift, scale, w, seq, out_dtype, wf_t=None):
    t_tok = x.shape[0]
    w3d = w.ndim == 3
    n_out = w.shape[0] * w.shape[2] if w3d else w.shape[1]
    cn = w.shape[2] if w3d else 256

    def body(x_ref, g_ref, sh_ref, sc_ref, w_ref, *rest):
        if wf_t is not None:
            wf_ref, h_ref, o_ref, fl_ref = rest
        else:
            h_ref, o_ref = rest
        xv = x_ref[...]
        rstd = lax.rsqrt(jnp.mean(xv * xv, axis=-1, keepdims=True) + EPS)
        h = (xv * rstd) * g_ref[...] * (1.0 + sc_ref[...]) + sh_ref[...]
        hb = h.astype(BF16)
        h_ref[...] = hb
        for j in range(n_out // cn):
            wj = w_ref[j] if w3d else w_ref[:, j * cn:(j + 1) * cn]
            o_ref[:, j * cn:(j + 1) * cn] = _nn(hb, wj).astype(out_dtype)
        if wf_t is not None:
            fl_ref[...] = _nt(wf_ref[...], hb)

    mod_spec = pl.BlockSpec((None, 1, D_MODEL), lambda i: (i * TM // seq, 0, 0))
    w_spec = (pl.BlockSpec(w.shape, lambda i: (0, 0, 0)) if w3d else pl.BlockSpec(w.shape, lambda i: (0, 0)))
    in_specs = [pl.BlockSpec((TM, D_MODEL), lambda i: (i, 0)), pl.BlockSpec((1, D_MODEL), lambda i: (0, 0)),
                mod_spec, mod_spec, w_spec]
    out_shape = [jax.ShapeDtypeStruct((t_tok, D_MODEL), BF16), jax.ShapeDtypeStruct((t_tok, n_out), out_dtype)]
    out_specs = [pl.BlockSpec((TM, D_MODEL), lambda i: (i, 0)), pl.BlockSpec((TM, n_out), lambda i: (i, 0))]
    args = [x, g, shift, scale, w]
    if wf_t is not None:
        in_specs.append(pl.BlockSpec(wf_t.shape, lambda i: (0, 0)))
        out_shape.append(jax.ShapeDtypeStruct((wf_t.shape[0], t_tok), F32))
        out_specs.append(pl.BlockSpec((wf_t.shape[0], TM), lambda i: (0, i)))
        args.append(wf_t)
    return _pallas(body, name=name, grid=(t_tok // TM,), in_specs=in_specs, out_specs=out_specs,
                   out_shape=out_shape, compiler_params=_cp(("arbitrary",), VMEM_BIG))(*args)


def _fox_prep(fl_t, b_f, seq):
    t_tok = fl_t.shape[1]
    ch = 256

    def body(fl_ref, bf_ref, fr_ref, fc_ref):
        z = fl_ref[...] + bf_ref[...]
        logf = jnp.minimum(z, 0.0) - jnp.log(1.0 + jnp.exp(-jnp.abs(z)))
        ri = lax.broadcasted_iota(jnp.int32, (ch, ch), 0)
        ci = lax.broadcasted_iota(jnp.int32, (ch, ch), 1)
        upper = (ri <= ci).astype(F32)
        eye = (ri == ci).astype(F32)
        carry = jnp.zeros((N_HEADS, 1), F32)
        for k in range(seq // ch):
            fk = _nn(logf[:, k * ch:(k + 1) * ch], upper, HI) + carry
            carry = fk[:, ch - 1:ch]
            fr_ref[:, k * ch:(k + 1) * ch] = fk
            padded = jnp.concatenate([fk, jnp.zeros((128 - N_HEADS, ch), F32)], axis=0)
            fc_ref[k * ch:(k + 1) * ch, :] = _nt(eye, padded, HI)

    return _pallas(
        body, name="fox_prep", grid=(t_tok // seq,),
        in_specs=[pl.BlockSpec((N_HEADS, seq), lambda b: (0, b)), pl.BlockSpec((N_HEADS, 1), lambda b: (0, 0))],
        out_specs=[pl.BlockSpec((N_HEADS, seq), lambda b: (0, b)), pl.BlockSpec((seq, 128), lambda b: (b, 0))],
        out_shape=[jax.ShapeDtypeStruct((N_HEADS, t_tok), F32), jax.ShapeDtypeStruct((t_tok, 128), F32)],
        compiler_params=_cp(("arbitrary",), VMEM_MID))(fl_t, b_f)


def _fox_post(df_row, fl_t, b_f, seq):
    t_tok = fl_t.shape[1]
    ch = 256

    def body(d_ref, fl_ref, bf_ref, o_ref, db_ref):
        @pl.when(pl.program_id(0) == 0)
        def _():
            db_ref[...] = jnp.zeros_like(db_ref)

        z = fl_ref[...] + bf_ref[...]
        sig_neg = 1.0 / (1.0 + jnp.exp(z))
        ri = lax.broadcasted_iota(jnp.int32, (ch, ch), 0)
        ci = lax.broadcasted_iota(jnp.int32, (ch, ch), 1)
        lower = (ri >= ci).astype(F32)
        carry = jnp.zeros((N_HEADS, 1), F32)
        tot = jnp.zeros((N_HEADS, 1), F32)
        for k in reversed(range(seq // ch)):
            dk = _nn(d_ref[:, k * ch:(k + 1) * ch], lower, HI) + carry
            carry = dk[:, 0:1]
            dfl = dk * sig_neg[:, k * ch:(k + 1) * ch]
            o_ref[:, k * ch:(k + 1) * ch] = dfl
            tot = tot + jnp.sum(dfl, axis=1, keepdims=True)
        db_ref[...] += jnp.broadcast_to(tot, db_ref.shape)

    return _pallas(
        body, name="fox_post", grid=(t_tok // seq,),
        in_specs=[pl.BlockSpec((N_HEADS, seq), lambda b: (0, b)), pl.BlockSpec((N_HEADS, seq), lambda b: (0, b)),
                  pl.BlockSpec((N_HEADS, 1), lambda b: (0, 0))],
        out_specs=[pl.BlockSpec((N_HEADS, seq), lambda b: (0, b)), pl.BlockSpec((N_HEADS, 128), lambda b: (0, 0))],
        out_shape=[jax.ShapeDtypeStruct((N_HEADS, t_tok), F32), jax.ShapeDtypeStruct((N_HEADS, 128), F32)],
        compiler_params=_cp(("arbitrary",), VMEM_MID))(df_row, fl_t, b_f)


def _eye(n, dtype):
    return (lax.broadcasted_iota(jnp.int32, (n, n), 0) == lax.broadcasted_iota(jnp.int32, (n, n), 1)).astype(dtype)


def _fox_aug(qkvg, f_col, seq):
    t_tok = qkvg.shape[0]
    ta = 256
    nkb = ta // TK

    def body(q_ref, k_ref, v_ref, fc_ref, qa_ref, ka_ref, kt_ref, vt_ref):
        ri = lax.broadcasted_iota(jnp.int32, (128, 128), 0)
        ci = lax.broadcasted_iota(jnp.int32, (128, 128), 1)
        eye = (ri == ci).astype(BF16)
        lane = lax.broadcasted_iota(jnp.int32, (ta, 128), 1)
        ones_q = jnp.where(jnp.logical_and(lane >= 64, lane < 67), 1.0, 0.0)
        ones_k = jnp.where(jnp.logical_and(lane >= 67, lane < 70), 1.0, 0.0)
        fc_tile = fc_ref[...]
        for p in range(N_HEADS // 2):
            q2 = q_ref[:, 128 * p:128 * (p + 1)]
            k2 = k_ref[:, 128 * p:128 * (p + 1)]
            vt = _nt(eye, v_ref[:, 128 * p:128 * (p + 1)]).astype(BF16)
            for kk in range(nkb):
                vt_ref[p, kk] = vt[:, kk * TK:(kk + 1) * TK]
            for e in range(2):
                h = 2 * p + e
                sel = jnp.logical_and(ri == ci + HEAD_DIM * e, ci < HEAD_DIM)
                f = _col(fc_tile, h)
                fh = f.astype(BF16).astype(F32)
                fm = (f - fh).astype(BF16).astype(F32)
                fl = (f - fh - fm).astype(BF16).astype(F32)
                qa = (_nn(q2, jnp.where(sel, SCALE, 0.0).astype(BF16)) + ones_q + jnp.where(lane == 67, fh, 0.0)
                      + jnp.where(lane == 68, fm, 0.0) + jnp.where(lane == 69, fl, 0.0))
                ka = (_nn(k2, jnp.where(sel, 1.0, 0.0).astype(BF16)) + ones_k - jnp.where(lane == 64, fh, 0.0)
                      - jnp.where(lane == 65, fm, 0.0) - jnp.where(lane == 66, fl, 0.0))
                qa_ref[h] = qa.astype(BF16)
                kab = ka.astype(BF16)
                ka_ref[h] = kab
                kt = _nt(eye, kab).astype(BF16)
                for kk in range(ta // TKB):
                    kt_ref[h, kk] = kt[:, kk * TKB:(kk + 1) * TKB]

    aug = jax.ShapeDtypeStruct((N_HEADS, t_tok, 128), BF16)
    return _pallas(
        body, name="fox_aug", grid=(t_tok // ta,),
        in_specs=[pl.BlockSpec((ta, 512), lambda i: (i, C_BQ // 512)), pl.BlockSpec((ta, 512), lambda i: (i, C_BK // 512)),
                  pl.BlockSpec((ta, 512), lambda i: (i, C_BV // 512)), pl.BlockSpec((ta, 128), lambda i: (i, 0))],
        out_specs=[pl.BlockSpec((N_HEADS, ta, 128), lambda i: (0, i, 0)), pl.BlockSpec((N_HEADS, ta, 128), lambda i: (0, i, 0)),
                   pl.BlockSpec((N_HEADS, ta // TKB, 128, TKB), lambda i: (0, i, 0, 0)),
                   pl.BlockSpec((N_HEADS // 2, nkb, 128, TK), lambda i: (0, i, 0, 0))],
        out_shape=[aug, aug, jax.ShapeDtypeStruct((N_HEADS, t_tok // TKB, 128, TKB), BF16),
                   jax.ShapeDtypeStruct((N_HEADS // 2, t_tok // TK, 128, TK), BF16)],
        compiler_params=_cp(("arbitrary",), VMEM_MID))(qkvg, qkvg, qkvg, f_col)


def _fox_fwd_t(q_aug, k_aug, vt, seq):
    t_tok = q_aug.shape[1]
    nq = seq // TQ
    ratio = TQ // TK

    def body(qa_ref, ka_ref, vt_ref, o_ref, lse_ref, ml_s, acc_s, st_s, p_s, al_s):
        i = pl.program_id(1)
        tpos = i * TQ + lax.broadcasted_iota(jnp.int32, (1, TQ), 1)
        eye = _eye(HEAD_DIM, BF16)
        for h in range(N_HEADS):
            ml_s[0, h] = jnp.full((1, TQ), NEG, F32)
            ml_s[1, h] = jnp.zeros((1, TQ), F32)
            acc_s[h] = jnp.zeros((HEAD_DIM, TQ), F32)
            p_s[1, h] = jnp.zeros((TK, TQ), BF16)
            al_s[1, h] = jnp.ones((1, TQ), F32)

        def scores(j):
            row0 = pl.multiple_of(j * TK, TK)
            for h in range(N_HEADS):
                st_s[j & 1, h] = _nt(ka_ref[h, pl.ds(row0, TK), :], qa_ref[h])

        def softmax(j, masked):
            slot = j & 1
            if masked:
                keep = (j * TK + lax.broadcasted_iota(jnp.int32, (TK, 1), 0)) <= tpos
            for h in range(N_HEADS):
                st = st_s[slot, h]
                if masked:
                    st = jnp.where(keep, st, NEG)
                m = ml_s[0, h]
                m_new = jnp.maximum(m, jnp.max(st, axis=0, keepdims=True))
                alpha = jnp.exp(m - m_new)
                pe = jnp.exp(st - m_new)
                ml_s[0, h] = m_new
                ml_s[1, h] = alpha * ml_s[1, h] + jnp.sum(pe, axis=0, keepdims=True)
                al_s[slot, h] = alpha
                p_s[slot, h] = pe.astype(BF16)

        def values(j):
            slot = j & 1
            jv = jnp.maximum(j, 0)
            for h in range(N_HEADS):
                p, e = divmod(h, 2)
                acc_s[h] = al_s[slot, h] * acc_s[h] + _nn(vt_ref[p, jv, e * HEAD_DIM:(e + 1) * HEAD_DIM, :], p_s[slot, h])

        def step(j, carry):
            values(j - 1)
            softmax(j, False)
            scores(j + 1)
            return carry

        last = ratio * i + ratio - 1
        scores(0)
        lax.fori_loop(0, ratio * i, step, 0)
        for kk in range(ratio):
            j = ratio * i + kk
            values(j - 1)
            softmax(j, True)
            if kk < ratio - 1:
                scores(j + 1)
        values(last)
        for p in range(N_HEADS // 2):
            outs = []
            for e in range(2):
                h = 2 * p + e
                l = ml_s[1, h]
                outs.append(_tn((acc_s[h] / l).astype(BF16), eye))
                lse_ref[p, e:e + 1, :] = ml_s[0, h] + jnp.log(l)
            o_ref[:, 128 * p:128 * (p + 1)] = jnp.concatenate(outs, axis=1).astype(BF16)

    return _pallas(
        body, name="fox_fwd", grid=(t_tok // seq, nq),
        in_specs=[pl.BlockSpec((N_HEADS, TQ, 128), lambda b, i: (0, b * nq + i, 0)),
                  pl.BlockSpec((N_HEADS, seq, 128), lambda b, i: (0, b, 0)),
                  pl.BlockSpec((N_HEADS // 2, seq // TK, 128, TK), lambda b, i: (0, b, 0, 0))],
        out_specs=[pl.BlockSpec((TQ, 512), lambda b, i: (b * nq + i, 0)),
                   pl.BlockSpec((N_HEADS // 2, 2, TQ), lambda b, i: (0, 0, b * nq + i))],
        out_shape=[jax.ShapeDtypeStruct((t_tok, 512), BF16), jax.ShapeDtypeStruct((N_HEADS // 2, 2, t_tok), F32)],
        scratch_shapes=[pltpu.VMEM((2, N_HEADS, 1, TQ), F32), pltpu.VMEM((N_HEADS, HEAD_DIM, TQ), F32),
                        pltpu.VMEM((2, N_HEADS, TK, TQ), F32), pltpu.VMEM((2, N_HEADS, TK, TQ), BF16),
                        pltpu.VMEM((2, N_HEADS, 1, TQ), F32)],
        compiler_params=_cp(("arbitrary", "arbitrary"), VMEM_MID))(q_aug, k_aug, vt)


def _fox_bwd_t(q_aug, k_aug, kt, qkvg, du_b, b_out, lse, seq):
    TK = TKB
    t_tok = qkvg.shape[0]
    nq = seq // TQ
    nkb = seq // TK
    ratio = TQ // TK
    hg = 4

    def body(qa_ref, ka_ref, kt_ref, v_ref, do_ref, o_ref, lse_ref, dq_ref, dk_ref, dv_ref, df_ref,
             dqt_s, row_s, dfk_s, dk_s, dv_s, dfa_s, st_s, dp_s, pb_s, db_s):
        ones_b = jnp.ones((8, TQ), BF16)
        eye = _eye(HEAD_DIM, BF16)
        lane8 = lax.broadcasted_iota(jnp.int32, (8, 128), 1)
        lane_k = lax.broadcasted_iota(jnp.int32, (TK, 128), 1)
        first = [lane8 < HEAD_DIM, lane8 >= HEAD_DIM]
        for hh in range(hg):
            pp, e = divmod(hh, 2)
            head_lanes = jnp.where(first[e], 1.0, 0.0)
            for ii in range(nq):
                rows = slice(ii * TQ, (ii + 1) * TQ)
                prod = do_ref[rows, 128 * pp:128 * (pp + 1)].astype(F32) * o_ref[rows, 128 * pp:128 * (pp + 1)].astype(F32)
                row_s[hh, ii, 0] = _nt(head_lanes, prod, HI)
                row_s[hh, ii, 1] = jnp.broadcast_to(lse_ref[pp, e:e + 1, ii * TQ:(ii + 1) * TQ], (8, TQ))
                dqt_s[hh, ii] = jnp.zeros((128, TQ), F32)

        def kblock(j, _):
            krow = pl.multiple_of(j * TK, TK)
            spos = j * TK + lax.broadcasted_iota(jnp.int32, (TK, 1), 0)
            for hh in range(hg):
                dk_s[hh] = jnp.zeros((TK, 128), F32)
                dv_s[hh] = jnp.zeros((TK, 128), F32)
                dfa_s[hh] = jnp.zeros((8, TK), F32)

            def scores(i):
                qrow = pl.multiple_of(i * TQ, TQ)
                for hh in range(hg):
                    pp, e = divmod(hh, 2)
                    own = (lane_k < HEAD_DIM) if e == 0 else (lane_k >= HEAD_DIM)
                    v2 = v_ref[pl.ds(krow, TK), 128 * pp:128 * (pp + 1)]
                    vj = jnp.where(own, v2, jnp.zeros_like(v2))
                    st_s[i & 1, hh] = _nt(ka_ref[hh, pl.ds(krow, TK), :], qa_ref[hh, pl.ds(qrow, TQ), :])
                    dp_s[i & 1, hh] = _nt(vj, do_ref[pl.ds(qrow, TQ), 128 * pp:128 * (pp + 1)])

            def elementwise(i, masked):
                slot = i & 1
                if masked:
                    keep = spos <= (i * TQ + lax.broadcasted_iota(jnp.int32, (1, TQ), 1))
                for hh in range(hg):
                    pt = jnp.exp(st_s[slot, hh] - row_s[hh, i, 1][0:1, :])
                    if masked:
                        pt = jnp.where(keep, pt, 0.0)
                    dst = pt * (dp_s[slot, hh] - row_s[hh, i, 0][0:1, :])
                    pb_s[slot, hh] = pt.astype(BF16)
                    db_s[slot, hh] = dst.astype(BF16)

            def grads(i):
                slot = i & 1
                qrow = pl.multiple_of(i * TQ, TQ)
                for hh in range(hg):
                    pp = hh // 2
                    dst_b = db_s[slot, hh]
                    dv_s[hh] += _nn(pb_s[slot, hh], do_ref[pl.ds(qrow, TQ), 128 * pp:128 * (pp + 1)])
                    dk_s[hh] += _nn(dst_b, qa_ref[hh, pl.ds(qrow, TQ), :])
                    dqt_s[hh, i] += _nn(kt_ref[hh, j], dst_b)
                    dfa_s[hh] += _nt(ones_b, dst_b)

            def step(i, carry):
                grads(i - 1)
                elementwise(i, False)
                scores(jnp.minimum(i + 1, nq - 1))
                return carry

            i0 = j // ratio
            scores(i0)
            elementwise(i0, True)
            scores(jnp.minimum(i0 + 1, nq - 1))
            lax.fori_loop(i0 + 1, nq, step, 0)
            grads(nq - 1)
            for pp in range(hg // 2):
                cols = slice(128 * pp, 128 * (pp + 1))
                dk_ref[pl.ds(krow, TK), cols] = jnp.concatenate(
                    [dk_s[2 * pp][:, :HEAD_DIM], dk_s[2 * pp + 1][:, :HEAD_DIM]], axis=1).astype(BF16)
                dv_ref[pl.ds(krow, TK), cols] = jnp.where(lane_k < HEAD_DIM, dv_s[2 * pp], dv_s[2 * pp + 1]).astype(BF16)
            for hh in range(hg):
                dfk_s[hh, j] = dfa_s[hh]
            return 0

        lax.fori_loop(0, nkb, kblock, 0)
        for pp in range(hg // 2):
            for ii in range(nq):
                parts = []
                for e in range(2):
                    dqt = dqt_s[2 * pp + e, ii]
                    parts.append(_tn(dqt[0:HEAD_DIM, :].astype(BF16), eye) * SCALE)
                    for kk in range(ratio):
                        jj = ii * ratio + kk
                        df_ref[pp, e:e + 1, jj * TK:(jj + 1) * TK] = (dqt[67:68, kk * TK:(kk + 1) * TK]
                                                                     - dfk_s[2 * pp + e, jj][0:1, :])
                dq_ref[ii * TQ:(ii + 1) * TQ, 128 * pp:128 * (pp + 1)] = jnp.concatenate(parts, axis=1).astype(BF16)

    aug_blk = pl.BlockSpec((hg, seq, 128), lambda b, g: (g, b, 0))
    pair_blk = pl.BlockSpec((seq, 64 * hg), lambda b, g: (b, g))
    row_blk = pl.BlockSpec((hg // 2, 2, seq), lambda b, g: (g, 0, b))
    return _pallas(
        body, name="fox_bwd", grid=(t_tok // seq, N_HEADS // hg),
        in_specs=[aug_blk, aug_blk, pl.BlockSpec((hg, nkb, 128, TK), lambda b, g: (g, b, 0, 0)),
                  pl.BlockSpec((seq, 64 * hg), lambda b, g: (b, C_BV // (64 * hg) + g)), pair_blk, pair_blk, row_blk],
        out_specs=[pair_blk, pair_blk, pair_blk, row_blk],
        out_shape=[jax.ShapeDtypeStruct((t_tok, 512), BF16)] * 3
        + [jax.ShapeDtypeStruct((N_HEADS // 2, 2, t_tok), F32)],
        scratch_shapes=[pltpu.VMEM((hg, nq, 128, TQ), F32), pltpu.VMEM((hg, nq, 2, 8, TQ), F32),
                        pltpu.VMEM((hg, nkb, 8, TK), F32), pltpu.VMEM((hg, TK, 128), F32),
                        pltpu.VMEM((hg, TK, 128), F32), pltpu.VMEM((hg, 8, TK), F32),
                        pltpu.VMEM((2, hg, TK, TQ), F32), pltpu.VMEM((2, hg, TK, TQ), F32),
                        pltpu.VMEM((2, hg, TK, TQ), BF16), pltpu.VMEM((2, hg, TK, TQ), BF16)],
        compiler_params=_cp(("arbitrary", "arbitrary"), VMEM_BIG))(q_aug, k_aug, kt, qkvg, du_b, b_out, lse)


def _fox_bwd_t_old(q_aug, k_aug, kt, qkvg, du_b, b_out, lse, seq):
    t_tok = qkvg.shape[0]
    nq = seq // TQ
    nkb = seq // TK
    ratio = TQ // TK

    def body(qa_ref, ka_ref, kt_ref, v_ref, do_ref, o_ref, lse_ref, dq_ref, dk_ref, dv_ref, df_ref,
             dqt_s, out_s, row_s, dfk_s):
        ones_b = jnp.ones((8, TQ), BF16)
        ones_f = jnp.ones((8, HEAD_DIM), F32)
        eye = _eye(HEAD_DIM, BF16)
        for e in range(2):
            lo, hi = e * HEAD_DIM, (e + 1) * HEAD_DIM
            for ii in range(nq):
                rows = slice(ii * TQ, (ii + 1) * TQ)
                do = do_ref[rows, :][:, lo:hi].astype(F32)
                ov = o_ref[rows, :][:, lo:hi].astype(F32)
                row_s[ii, 0] = _nt(ones_f, do * ov, HI)
                row_s[ii, 1] = jnp.broadcast_to(lse_ref[e:e + 1, ii * TQ:(ii + 1) * TQ], (8, TQ))
                dqt_s[ii] = jnp.zeros((128, TQ), F32)

            def kblock(j, _):
                krow = pl.multiple_of(j * TK, TK)
                kj = ka_ref[e, pl.ds(krow, TK), :]
                ktj = kt_ref[e, j]
                vj = v_ref[pl.ds(krow, TK), :][:, lo:hi]
                spos = j * TK + lax.broadcasted_iota(jnp.int32, (TK, 1), 0)

                def qblock(i, carry, masked):
                    dk_acc, dv_acc, dfk = carry
                    qrow = pl.multiple_of(i * TQ, TQ)
                    qa = qa_ref[e, pl.ds(qrow, TQ), :]
                    doh = do_ref[pl.ds(qrow, TQ), :][:, lo:hi]
                    pt = jnp.exp(_nt(kj, qa) - row_s[i, 1][0:1, :])
                    if masked:
                        tpos = i * TQ + lax.broadcasted_iota(jnp.int32, (1, TQ), 1)
                        pt = jnp.where(spos <= tpos, pt, 0.0)
                    dst = pt * (_nt(vj, doh) - row_s[i, 0][0:1, :])
                    dst_b = dst.astype(BF16)
                    dv_acc = dv_acc + _nn(pt.astype(BF16), doh)
                    dk_acc = dk_acc + _nn(dst_b, qa)
                    dqt_s[i] += _nn(ktj, dst_b)
                    dfk = dfk + _nt(ones_b, dst_b)
                    return dk_acc, dv_acc, dfk

                i0 = j // ratio
                carry = (jnp.zeros((TK, 128), F32), jnp.zeros((TK, HEAD_DIM), F32), jnp.zeros((8, TK), F32))
                carry = qblock(i0, carry, True)
                dk_acc, dv_acc, dfk = lax.fori_loop(i0 + 1, nq, functools.partial(qblock, masked=False), carry)
                out_s[1, e, pl.ds(krow, TK), :] = dk_acc[:, :HEAD_DIM]
                out_s[2, e, pl.ds(krow, TK), :] = dv_acc
                dfk_s[j] = dfk
                return 0

            lax.fori_loop(0, nkb, kblock, 0)
            for ii in range(nq):
                dqt = dqt_s[ii]
                out_s[0, e, ii * TQ:(ii + 1) * TQ, :] = _tn(dqt[0:HEAD_DIM, :].astype(BF16), eye) * SCALE
                for kk in range(ratio):
                    jj = ii * ratio + kk
                    df_ref[e:e + 1, jj * TK:(jj + 1) * TK] = dqt[67:68, kk * TK:(kk + 1) * TK] - dfk_s[jj][0:1, :]
        for k, ref in enumerate((dq_ref, dk_ref, dv_ref)):
            ref[...] = jnp.concatenate([out_s[k, 0], out_s[k, 1]], axis=1).astype(BF16)

    aug_blk = pl.BlockSpec((2, seq, 128), lambda b, p: (p, b, 0))
    pair_blk = pl.BlockSpec((seq, 128), lambda b, p: (b, p))
    row_blk = pl.BlockSpec((None, 2, seq), lambda b, p: (p, 0, b))
    return _pallas(
        body, name="fox_bwd", grid=(t_tok // seq, N_HEADS // 2),
        in_specs=[aug_blk, aug_blk, pl.BlockSpec((2, nkb, 128, TK), lambda b, p: (p, b, 0, 0)),
                  pl.BlockSpec((seq, 128), lambda b, p: (b, C_BV // 128 + p)), pair_blk, pair_blk, row_blk],
        out_specs=[pair_blk, pair_blk, pair_blk, row_blk],
        out_shape=[jax.ShapeDtypeStruct((t_tok, 512), BF16)] * 3
        + [jax.ShapeDtypeStruct((N_HEADS // 2, 2, t_tok), F32)],
        scratch_shapes=[pltpu.VMEM((nq, 128, TQ), F32), pltpu.VMEM((3, 2, seq, HEAD_DIM), F32),
                        pltpu.VMEM((nq, 2, 8, TQ), F32), pltpu.VMEM((nkb, 8, TK), F32)],
        compiler_params=_cp(("arbitrary", "arbitrary"), VMEM_BIG))(q_aug, k_aug, kt, qkvg, du_b, b_out, lse)


def _fox_fwd(qkvg, f_row, f_col, seq):
    t_tok = qkvg.shape[0]
    nq = seq // TQ

    def body(q_ref, k_ref, v_ref, fr_ref, fc_ref, o_ref, lse_ref, fk_s):
        i = pl.program_id(1)
        for jj in range(nq):
            fk_s[jj] = fr_ref[:, jj * TQ:(jj + 1) * TQ]
        fcol = fc_ref[...]
        tpos = i * TQ + lax.broadcasted_iota(jnp.int32, (TQ, 1), 0)
        lane = lax.broadcasted_iota(jnp.int32, (TQ, 128), 1)
        lse_tile = jnp.zeros((TQ, 128), F32)
        for p in range(N_HEADS // 2):
            q2 = q_ref[:, 128 * p:128 * (p + 1)]
            qs = [q2[:, :HEAD_DIM], q2[:, HEAD_DIM:]]
            fqs = [_col(fcol, 2 * p + e) for e in range(2)]

            def kblock(j, carry):
                row0 = pl.multiple_of(j * TQ, TQ)
                k2 = k_ref[pl.ds(row0, TQ), 128 * p:128 * (p + 1)]
                v2 = v_ref[pl.ds(row0, TQ), 128 * p:128 * (p + 1)]
                fk8 = fk_s[j]
                spos = j * TQ + lax.broadcasted_iota(jnp.int32, (1, TQ), 1)
                keep = spos <= tpos
                new = []
                for e in range(2):
                    m, l, acc = carry[3 * e:3 * e + 3]
                    kh = k2[:, e * HEAD_DIM:(e + 1) * HEAD_DIM]
                    vh = v2[:, e * HEAD_DIM:(e + 1) * HEAD_DIM]
                    s = _nt(qs[e], kh) * SCALE + (fqs[e] - fk8[2 * p + e:2 * p + e + 1, :])
                    s = jnp.where(keep, s, NEG)
                    m_new = jnp.maximum(m, jnp.max(s, axis=1, keepdims=True))
                    alpha = jnp.exp(m - m_new)
                    pe = jnp.exp(s - m_new)
                    l = alpha * l + jnp.sum(pe, axis=1, keepdims=True)
                    acc = alpha * acc + _nn(pe.astype(BF16), vh)
                    new += [m_new, l, acc]
                return tuple(new)

            init = (jnp.full((TQ, 1), NEG, F32), jnp.zeros((TQ, 1), F32), jnp.zeros((TQ, HEAD_DIM), F32)) * 2
            res = lax.fori_loop(0, i + 1, kblock, init)
            outs = []
            for e in range(2):
                m, l, acc = res[3 * e:3 * e + 3]
                outs.append(acc / l)
                lse_tile = jnp.where(lane == 2 * p + e, m + jnp.log(l), lse_tile)
            o_ref[:, 128 * p:128 * (p + 1)] = jnp.concatenate(outs, axis=1).astype(BF16)
        lse_ref[...] = lse_tile

    return _pallas(
        body, name="fox_fwd", grid=(t_tok // seq, nq),
        in_specs=[pl.BlockSpec((TQ, 512), lambda b, i: (b * nq + i, C_BQ // 512)),
                  pl.BlockSpec((seq, 512), lambda b, i: (b, C_BK // 512)),
                  pl.BlockSpec((seq, 512), lambda b, i: (b, C_BV // 512)),
                  pl.BlockSpec((N_HEADS, seq), lambda b, i: (0, b)),
                  pl.BlockSpec((TQ, 128), lambda b, i: (b * nq + i, 0))],
        out_specs=[pl.BlockSpec((TQ, 512), lambda b, i: (b * nq + i, 0)),
                   pl.BlockSpec((TQ, 128), lambda b, i: (b * nq + i, 0))],
        out_shape=[jax.ShapeDtypeStruct((t_tok, 512), BF16), jax.ShapeDtypeStruct((t_tok, 128), F32)],
        scratch_shapes=[pltpu.VMEM((nq, N_HEADS, TQ), F32)],
        compiler_params=_cp(("arbitrary", "arbitrary"), VMEM_MID))(qkvg, qkvg, qkvg, f_row, f_col)


def _fox_bwd(qkvg, du_b, b_out, lse, f_row, f_col, seq):
    t_tok = qkvg.shape[0]
    nq = seq // TQ

    def body(q_ref, k_ref, v_ref, do_ref, o_ref, lse_ref, fr_ref, fc_ref,
             dq_ref, dk_ref, dv_ref, df_ref, dq_s, dk_s, dv_s, col_s, df_s, fk_s):
        p = pl.program_id(1)
        for jj in range(nq):
            fk_s[jj] = fr_ref[:, jj * TQ:(jj + 1) * TQ]
        eye = (lax.broadcasted_iota(jnp.int32, (TQ, TQ), 0) == lax.broadcasted_iota(jnp.int32, (TQ, TQ), 1)).astype(F32)
        for e in range(2):
            h = 2 * p + e
            lo, hi = e * HEAD_DIM, (e + 1) * HEAD_DIM
            for ii in range(nq):
                rows = slice(ii * TQ, (ii + 1) * TQ)
                do = do_ref[rows, :][:, lo:hi].astype(F32)
                ov = o_ref[rows, :][:, lo:hi].astype(F32)
                col_s[0, rows, :] = jnp.sum(do * ov, axis=1, keepdims=True)
                col_s[1, rows, :] = _col(lse_ref[rows, :], h)
                col_s[2, rows, :] = _col(fc_ref[rows, :], h)
                dq_s[rows, :] = jnp.zeros((TQ, HEAD_DIM), F32)
                df_s[ii] = jnp.zeros((8, TQ), F32)
                col_s[3, rows, :] = jnp.zeros((TQ, 1), F32)

            def kblock(j, _):
                krow = pl.multiple_of(j * TQ, TQ)
                kh = k_ref[pl.ds(krow, TQ), :][:, lo:hi]
                vh = v_ref[pl.ds(krow, TQ), :][:, lo:hi]
                fk = _row(fk_s[j], h)
                spos = j * TQ + lax.broadcasted_iota(jnp.int32, (1, TQ), 1)

                def qblock(i, carry):
                    dk_acc, dv_acc, dfk = carry
                    qrow = pl.multiple_of(i * TQ, TQ)
                    qh = q_ref[pl.ds(qrow, TQ), :][:, lo:hi]
                    doh = do_ref[pl.ds(qrow, TQ), :][:, lo:hi]
                    delta = col_s[0, pl.ds(qrow, TQ), :]
                    lse_q = col_s[1, pl.ds(qrow, TQ), :]
                    fq = col_s[2, pl.ds(qrow, TQ), :]
                    tpos = i * TQ + lax.broadcasted_iota(jnp.int32, (TQ, 1), 0)
                    s = _nt(qh, kh) * SCALE + (fq - fk)
                    pr = jnp.where(spos <= tpos, jnp.exp(s - lse_q), 0.0)
                    dp = _nt(doh, vh)
                    ds = pr * (dp - delta)
                    ds_b = ds.astype(BF16)
                    dv_acc = dv_acc + _tn(pr.astype(BF16), doh)
                    dk_acc = dk_acc + _tn(ds_b, qh)
                    dq_s[pl.ds(qrow, TQ), :] += _nn(ds_b, kh)
                    col_s[3, pl.ds(qrow, TQ), :] += jnp.sum(ds, axis=1, keepdims=True)
                    dfk = dfk + jnp.sum(ds, axis=0, keepdims=True)
                    return dk_acc, dv_acc, dfk

                zero = jnp.zeros((TQ, HEAD_DIM), F32)
                dk_acc, dv_acc, dfk = lax.fori_loop(j, nq, qblock, (zero, zero, jnp.zeros((1, TQ), F32)))
                dk_s[e, pl.ds(krow, TQ), :] = dk_acc * SCALE
                dv_s[e, pl.ds(krow, TQ), :] = dv_acc
                df_s[j] -= jnp.broadcast_to(dfk, (8, TQ))
                return 0

            lax.fori_loop(0, nq, kblock, 0)
            dq_s2 = dq_s[...] * SCALE
            dk_s[2 + e] = dq_s2
            for ii in range(nq):
                dfq = jnp.broadcast_to(col_s[3, ii * TQ:(ii + 1) * TQ, :], (TQ, 128))
                df_ref[e:e + 1, ii * TQ:(ii + 1) * TQ] = _tn(dfq, eye, HI)[0:1, :] + df_s[ii][0:1, :]
        dq_ref[...] = jnp.concatenate([dk_s[2], dk_s[3]], axis=1).astype(BF16)
        dk_ref[...] = jnp.concatenate([dk_s[0], dk_s[1]], axis=1).astype(BF16)
        dv_ref[...] = jnp.concatenate([dv_s[0], dv_s[1]], axis=1).astype(BF16)

    blk = lambda off: pl.BlockSpec((seq, 128), lambda b, p: (b, off // 128 + p))
    out_blk = pl.BlockSpec((seq, 128), lambda b, p: (b, p))
    return _pallas(
        body, name="fox_bwd", grid=(t_tok // seq, N_HEADS // 2),
        in_specs=[blk(C_BQ), blk(C_BK), blk(C_BV), out_blk, out_blk,
                  pl.BlockSpec((seq, 128), lambda b, p: (b, 0)),
                  pl.BlockSpec((N_HEADS, seq), lambda b, p: (0, b)),
                  pl.BlockSpec((seq, 128), lambda b, p: (b, 0))],
        out_specs=[out_blk, out_blk, out_blk, pl.BlockSpec((None, 2, seq), lambda b, p: (p, 0, b))],
        out_shape=[jax.ShapeDtypeStruct((t_tok, 512), BF16)] * 3
        + [jax.ShapeDtypeStruct((N_HEADS // 2, 2, t_tok), F32)],
        scratch_shapes=[pltpu.VMEM((seq, HEAD_DIM), F32), pltpu.VMEM((4, seq, HEAD_DIM), F32),
                        pltpu.VMEM((2, seq, HEAD_DIM), F32), pltpu.VMEM((4, seq, 1), F32),
                        pltpu.VMEM((nq, 8, TQ), F32), pltpu.VMEM((nq, N_HEADS, TQ), F32)],
        compiler_params=_cp(("arbitrary", "arbitrary"), VMEM_BIG))(qkvg, qkvg, qkvg, du_b, b_out, lse, f_row, f_col)


def _swa_window(k_ref, v_ref, n):
    prev = pl.multiple_of(jnp.maximum(n - 1, 0) * BLOCK, BLOCK)
    cur = pl.multiple_of(n * BLOCK, BLOCK)
    kwin = jnp.concatenate([k_ref[pl.ds(prev, BLOCK), :], k_ref[pl.ds(cur, BLOCK), :]], axis=0)
    vwin = jnp.concatenate([v_ref[pl.ds(prev, BLOCK), :], v_ref[pl.ds(cur, BLOCK), :]], axis=0)
    ti = lax.broadcasted_iota(jnp.int32, (BLOCK, 2 * BLOCK), 0)
    sj = lax.broadcasted_iota(jnp.int32, (BLOCK, 2 * BLOCK), 1)
    rel = ti - sj + BLOCK
    first_key = jnp.where(n > 0, 0, BLOCK)
    mask = jnp.logical_and(jnp.logical_and(rel >= 0, rel < BLOCK), sj >= first_key)
    return kwin, vwin, mask, prev, cur


def _head_cols(ref, h):
    pair = ref[:, 128 * (h // 2):128 * (h // 2 + 1)]
    return pair[:, (h % 2) * HEAD_DIM:(h % 2 + 1) * HEAD_DIM]


def _swa_logits(q_ref, kwin, bias_ref, h, mask):
    hk = h // KV_GROUP
    s = _nt(_head_cols(q_ref, h), kwin[:, hk * HEAD_DIM:(hk + 1) * HEAD_DIM]) * SCALE + bias_ref[h]
    return jnp.where(mask, s, NEG)


def _swa_fwd(qkvg, bias, sinks, seq):
    t_tok = qkvg.shape[0]
    nb = seq // BLOCK

    def body(sink_ref, q_ref, k_ref, v_ref, bias_ref, o_ref, lse_ref, s_s, p_s, den_s):
        g = pl.program_id(1)
        subs = [pl.ds(s * BLOCK, BLOCK) for s in range(SWA_SUB)]
        wins = [_swa_window(k_ref, v_ref, SWA_SUB * g + s) for s in range(SWA_SUB)]
        for s in range(SWA_SUB):
            for h in range(N_HEADS):
                s_s[s * N_HEADS + h] = _swa_logits(q_ref.at[subs[s]], wins[s][0], bias_ref, h, wins[s][2])
        lane = lax.broadcasted_iota(jnp.int32, (BLOCK, 128), 1)
        for s in range(SWA_SUB):
            lse_tile = jnp.zeros((BLOCK, 128), F32)
            for h in range(N_HEADS):
                sc = s_s[s * N_HEADS + h]
                sink = sink_ref[h]
                m = jnp.maximum(jnp.max(sc, axis=1, keepdims=True), sink)
                pe = jnp.exp(sc - m)
                den = jnp.sum(pe, axis=1, keepdims=True) + jnp.exp(sink - m)
                p_s[s * N_HEADS + h] = pe.astype(BF16)
                den_s[s * N_HEADS + h] = den
                lse_tile = jnp.where(lane == h, m + jnp.log(den), lse_tile)
            lse_ref[subs[s], :] = lse_tile
        for s in range(SWA_SUB):
            vwin = wins[s][1]
            for pr in range(N_HEADS // 2):
                outs = []
                for h in (2 * pr, 2 * pr + 1):
                    hk = h // KV_GROUP
                    outs.append(_nn(p_s[s * N_HEADS + h], vwin[:, hk * HEAD_DIM:(hk + 1) * HEAD_DIM]) / den_s[s * N_HEADS + h])
                o_ref[subs[s], 128 * pr:128 * (pr + 1)] = jnp.concatenate(outs, axis=1).astype(BF16)

    rows = SWA_SUB * BLOCK
    steps = nb // SWA_SUB
    return _pallas(
        body, name="swa_fwd", grid=(t_tok // seq, steps),
        in_specs=[pl.BlockSpec(memory_space=pltpu.SMEM),
                  pl.BlockSpec((rows, 512), lambda b, n: (b * steps + n, C_AQ // 512)),
                  pl.BlockSpec((seq, 128), lambda b, n: (b, C_AK // 128)),
                  pl.BlockSpec((seq, 128), lambda b, n: (b, C_AV // 128)),
                  pl.BlockSpec((N_HEADS, BLOCK, 2 * BLOCK), lambda b, n: (0, 0, 0))],
        out_specs=[pl.BlockSpec((rows, 512), lambda b, n: (b * steps + n, 0)),
                   pl.BlockSpec((rows, 128), lambda b, n: (b * steps + n, 0))],
        out_shape=[jax.ShapeDtypeStruct((t_tok, 512), BF16), jax.ShapeDtypeStruct((t_tok, 128), F32)],
        scratch_shapes=[pltpu.VMEM((SWA_SUB * N_HEADS, BLOCK, 2 * BLOCK), F32),
                        pltpu.VMEM((SWA_SUB * N_HEADS, BLOCK, 2 * BLOCK), BF16),
                        pltpu.VMEM((SWA_SUB * N_HEADS, BLOCK, 1), F32)],
        compiler_params=_cp(("arbitrary", "arbitrary"), VMEM_MID))(sinks, qkvg, qkvg, qkvg, bias)


def _swa_bwd(qkvg, du_a, a_out, lse, bias, sinks, seq):
    t_tok = qkvg.shape[0]
    nb = seq // BLOCK

    def body(sink_ref, q_ref, k_ref, v_ref, do_ref, o_ref, lse_ref, bias_ref,
             dq_ref, dkv_ref, dbias_ref, dsink_ref, kv_s, s_s, dp_s, pb_s, db_s):
        b, n = pl.program_id(0), pl.program_id(1)

        @pl.when(jnp.logical_and(b == 0, n == 0))
        def _():
            dbias_ref[...] = jnp.zeros_like(dbias_ref)
            dsink_ref[...] = jnp.zeros_like(dsink_ref)

        @pl.when(n == 0)
        def _():
            kv_s[...] = jnp.zeros_like(kv_s)

        subs = [pl.ds(s * BLOCK, BLOCK) for s in range(SWA_SUB)]
        wins = [_swa_window(k_ref, v_ref, SWA_SUB * n + s) for s in range(SWA_SUB)]
        for s in range(SWA_SUB):
            kwin, vwin, mask = wins[s][:3]
            for h in range(N_HEADS):
                hk = h // KV_GROUP
                s_s[s * N_HEADS + h] = _swa_logits(q_ref.at[subs[s]], kwin, bias_ref, h, mask)
                dp_s[s * N_HEADS + h] = _nt(_head_cols(do_ref.at[subs[s]], h), vwin[:, hk * HEAD_DIM:(hk + 1) * HEAD_DIM])
        for s in range(SWA_SUB):
            lse_tile = lse_ref[subs[s], :]
            do_s, o_s = do_ref.at[subs[s]], o_ref.at[subs[s]]
            for h in range(N_HEADS):
                delta = jnp.sum(_head_cols(do_s, h).astype(F32) * _head_cols(o_s, h).astype(F32), axis=1, keepdims=True)
                lse_h = _col(lse_tile, h)
                pe = jnp.exp(s_s[s * N_HEADS + h] - lse_h)
                ds = pe * (dp_s[s * N_HEADS + h] - delta)
                dbias_ref[h] += ds
                psink = jnp.exp(sink_ref[h] - lse_h)
                dsink_ref[h:h + 1, :] += jnp.broadcast_to(jnp.sum(-psink * delta, axis=0, keepdims=True), (1, 128))
                pb_s[s * N_HEADS + h] = pe.astype(BF16)
                db_s[s * N_HEADS + h] = ds.astype(BF16)
        for s in range(SWA_SUB):
            kwin, _, _, prev, cur = wins[s]
            q_s, do_s = q_ref.at[subs[s]], do_ref.at[subs[s]]
            for pr in range(N_HEADS // 2):
                dqs = []
                for h in (2 * pr, 2 * pr + 1):
                    hk = h // KV_GROUP
                    dqs.append(_nn(db_s[s * N_HEADS + h], kwin[:, hk * HEAD_DIM:(hk + 1) * HEAD_DIM]) * SCALE)
                dq_ref[subs[s], 128 * pr:128 * (pr + 1)] = jnp.concatenate(dqs, axis=1).astype(BF16)
            dks, dvs = [], []
            for hk in range(N_HEADS // KV_GROUP):
                dk = jnp.zeros((2 * BLOCK, HEAD_DIM), F32)
                dv = jnp.zeros((2 * BLOCK, HEAD_DIM), F32)
                for h in range(hk * KV_GROUP, (hk + 1) * KV_GROUP):
                    dk = dk + _tn(db_s[s * N_HEADS + h], _head_cols(q_s, h))
                    dv = dv + _tn(pb_s[s * N_HEADS + h], _head_cols(do_s, h))
                dks.append(dk * SCALE)
                dvs.append(dv)
            upd = jnp.concatenate(dks + dvs, axis=1)
            kv_s[pl.ds(prev, BLOCK), :] += upd[:BLOCK]
            kv_s[pl.ds(cur, BLOCK), :] += upd[BLOCK:]

        @pl.when(n == steps - 1)
        def _():
            dkv_ref[...] = kv_s[...].astype(BF16)

    rows = SWA_SUB * BLOCK
    steps = nb // SWA_SUB
    tile = (SWA_SUB * N_HEADS, BLOCK, 2 * BLOCK)
    return _pallas(
        body, name="swa_bwd", grid=(t_tok // seq, steps),
        in_specs=[pl.BlockSpec(memory_space=pltpu.SMEM),
                  pl.BlockSpec((rows, 512), lambda b, n: (b * steps + n, C_AQ // 512)),
                  pl.BlockSpec((seq, 128), lambda b, n: (b, C_AK // 128)),
                  pl.BlockSpec((seq, 128), lambda b, n: (b, C_AV // 128)),
                  pl.BlockSpec((rows, 512), lambda b, n: (b * steps + n, 0)),
                  pl.BlockSpec((rows, 512), lambda b, n: (b * steps + n, 0)),
                  pl.BlockSpec((rows, 128), lambda b, n: (b * steps + n, 0)),
                  pl.BlockSpec((N_HEADS, BLOCK, 2 * BLOCK), lambda b, n: (0, 0, 0))],
        out_specs=[pl.BlockSpec((rows, 512), lambda b, n: (b * steps + n, 0)),
                   pl.BlockSpec((seq, 256), lambda b, n: (b, 0)),
                   pl.BlockSpec((N_HEADS, BLOCK, 2 * BLOCK), lambda b, n: (0, 0, 0)),
                   pl.BlockSpec((N_HEADS, 128), lambda b, n: (0, 0))],
        out_shape=[jax.ShapeDtypeStruct((t_tok, 512), BF16), jax.ShapeDtypeStruct((t_tok, 256), BF16),
                   jax.ShapeDtypeStruct((N_HEADS, BLOCK, 2 * BLOCK), F32), jax.ShapeDtypeStruct((N_HEADS, 128), F32)],
        scratch_shapes=[pltpu.VMEM((seq, 256), F32), pltpu.VMEM(tile, F32), pltpu.VMEM(tile, F32),
                        pltpu.VMEM(tile, BF16), pltpu.VMEM(tile, BF16)],
        compiler_params=_cp(("arbitrary", "arbitrary"), VMEM_MID))(sinks, qkvg, qkvg, qkvg, du_a, a_out, lse, bias)


def _out_proj(name, u_parts, gate_arr, gate_blk, w_out, x, gmod, seq):
    t_tok = x.shape[0]
    nu = len(u_parts)

    def body(*refs):
        u_refs = refs[:nu]
        g_ref, w_ref, x_ref, gm_ref, yg_ref, y_ref, xn_ref = refs[nu:]
        u = jnp.concatenate([r[...].astype(F32) for r in u_refs], axis=1) if nu > 1 else u_refs[0][...].astype(F32)
        yg = (u * _silu(g_ref[...].astype(F32))).astype(BF16)
        yg_ref[...] = yg
        y = _nn(yg, w_ref[...])
        y_ref[...] = y.astype(BF16)
        xn_ref[...] = x_ref[...] + gm_ref[...] * y

    row = lambda w: pl.BlockSpec((TM, w), lambda i: (i, 0))
    in_specs = [row(u.shape[1]) for u in u_parts]
    in_specs += [pl.BlockSpec((TM, D_MODEL), lambda i: (i, gate_blk)),
                 pl.BlockSpec((D_MODEL, D_MODEL), lambda i: (0, 0)), row(D_MODEL),
                 pl.BlockSpec((None, 1, D_MODEL), lambda i: (i * TM // seq, 0, 0))]
    return _pallas(
        body, name=name, grid=(t_tok // TM,), in_specs=in_specs,
        out_specs=[row(D_MODEL)] * 3,
        out_shape=[jax.ShapeDtypeStruct((t_tok, D_MODEL), BF16)] * 2 + [jax.ShapeDtypeStruct((t_tok, D_MODEL), F32)],
        compiler_params=_cp(("arbitrary",), VMEM_MID))(*u_parts, gate_arr, w_out, x, gmod)


def _out_proj_bwd(name, dxn, gmod, y, w_out, seq, attn=None):
    t_tok = dxn.shape[0]
    tiles_per_seq = seq // TM

    def body(*refs):
        if attn is None:
            dxn_ref, gm_ref, y_ref, w_ref, dy_ref, dgm_ref, dyg_ref = refs
        else:
            dxn_ref, gm_ref, y_ref, w_ref, a_ref, b_ref, g_ref, dy_ref, dgm_ref, dua_ref, dub_ref, dg_ref = refs
        i = pl.program_id(0)
        dxv = dxn_ref[...]
        dy = (dxv * gm_ref[...]).astype(BF16)
        dy_ref[...] = dy

        @pl.when(i % tiles_per_seq == 0)
        def _():
            dgm_ref[...] = jnp.zeros_like(dgm_ref)

        dgm_ref[...] += jnp.sum(dxv * y_ref[...].astype(F32), axis=0, keepdims=True)
        dyg = _nt(dy, w_ref[...])
        if attn is None:
            dyg_ref[...] = dyg
        else:
            gt = g_ref[...].astype(F32)
            du = dyg * _silu(gt)
            dua_ref[...] = du[:, :512].astype(BF16)
            dub_ref[...] = du[:, 512:].astype(BF16)
            u = jnp.concatenate([a_ref[...].astype(F32), b_ref[...].astype(F32)], axis=1)
            dg_ref[...] = (dyg * u * _dsilu(gt)).astype(BF16)

    row = lambda w: pl.BlockSpec((TM, w), lambda i: (i, 0))
    mod_spec = pl.BlockSpec((None, 1, D_MODEL), lambda i: (i * TM // seq, 0, 0))
    in_specs = [row(D_MODEL), mod_spec, row(D_MODEL), pl.BlockSpec((D_MODEL, D_MODEL), lambda i: (0, 0))]
    out_specs = [row(D_MODEL), mod_spec]
    out_shape = [jax.ShapeDtypeStruct((t_tok, D_MODEL), BF16), jax.ShapeDtypeStruct(gmod.shape, F32)]
    args = [dxn, gmod, y, w_out]
    if attn is None:
        out_specs.append(row(D_MODEL))
        out_shape.append(jax.ShapeDtypeStruct((t_tok, D_MODEL), F32))
    else:
        in_specs += [row(512), row(512), pl.BlockSpec((TM, D_MODEL), lambda i: (i, C_GATE // D_MODEL))]
        out_specs += [row(512), row(512), row(D_MODEL)]
        out_shape += [jax.ShapeDtypeStruct((t_tok, 512), BF16)] * 2 + [jax.ShapeDtypeStruct((t_tok, D_MODEL), BF16)]
        args += list(attn)
    return _pallas(body, name=name, grid=(t_tok // TM,), in_specs=in_specs, out_specs=out_specs,
                   out_shape=out_shape, compiler_params=_cp(("arbitrary",), VMEM_MID))(*args)


def _norm_bwd(name, parts, w, x, g, scale, dxn, seq, rows_part=None):
    t_tok = x.shape[0]
    npart = len(parts)
    w3d = w.ndim == 3
    tiles_per_seq = seq // TM
    nrow_in = 0 if rows_part is None else 2

    def body(*refs):
        p_refs = refs[:npart]
        w_ref, x_ref, g_ref, sc_ref, dxn_ref = refs[npart:npart + 5]
        dx_ref, dss_ref, dg_ref = refs[npart + 5 + nrow_in:]
        i = pl.program_id(0)
        dh = jnp.zeros((TM, D_MODEL), F32)
        if rows_part is not None:
            r_ref, wr_ref = refs[npart + 5:npart + 7]
            dh = dh + _tn(r_ref[...].astype(BF16), wr_ref[...])
        for (arr, off), p_ref in zip(parts, p_refs):
            width = arr.shape[1]
            for j in range(width // 256):
                pj = p_ref[:, j * 256:(j + 1) * 256]
                c0 = off + j * 256
                wj = w_ref[c0 // 256] if w3d else w_ref[:, c0:c0 + 256]
                dh = dh + _nt(pj, wj)
        xv = x_ref[...]
        rstd = lax.rsqrt(jnp.mean(xv * xv, axis=-1, keepdims=True) + EPS)
        xhat = xv * rstd
        gv = g_ref[...]
        nrm = xhat * gv

        @pl.when(i % tiles_per_seq == 0)
        def _():
            dss_ref[...] = jnp.zeros_like(dss_ref)

        @pl.when(i == 0)
        def _():
            dg_ref[...] = jnp.zeros_like(dg_ref)

        dss_ref[0:1, :] += jnp.sum(dh, axis=0, keepdims=True)
        dss_ref[1:2, :] += jnp.sum(dh * nrm, axis=0, keepdims=True)
        dn = dh * (1.0 + sc_ref[...])
        dg_ref[0:1, :] += jnp.sum(dn * xhat, axis=0, keepdims=True)
        dxhat = dn * gv
        dx_ref[...] = rstd * (dxhat - xhat * jnp.mean(dxhat * xhat, axis=-1, keepdims=True)) + dxn_ref[...]

    row = lambda wd: pl.BlockSpec((TM, wd), lambda i: (i, 0))
    w_spec = (pl.BlockSpec(w.shape, lambda i: (0, 0, 0)) if w3d else pl.BlockSpec(w.shape, lambda i: (0, 0)))
    in_specs = [row(a.shape[1]) for a, _ in parts]
    in_specs += [w_spec, row(D_MODEL), pl.BlockSpec((1, D_MODEL), lambda i: (0, 0)),
                 pl.BlockSpec((None, 1, D_MODEL), lambda i: (i * TM // seq, 0, 0)), row(D_MODEL)]
    args = [a for a, _ in parts] + [w, x, g, scale, dxn]
    if rows_part is not None:
        in_specs += [pl.BlockSpec((8, TM), lambda i: (0, i)), pl.BlockSpec((8, D_MODEL), lambda i: (0, 0))]
        args += list(rows_part)
    nseq = t_tok // seq
    return _pallas(
        body, name=name, grid=(t_tok // TM,), in_specs=in_specs,
        out_specs=[row(D_MODEL), pl.BlockSpec((None, 8, D_MODEL), lambda i: (i * TM // seq, 0, 0)),
                   pl.BlockSpec((8, D_MODEL), lambda i: (0, 0))],
        out_shape=[jax.ShapeDtypeStruct((t_tok, D_MODEL), F32), jax.ShapeDtypeStruct((nseq, 8, D_MODEL), F32),
                   jax.ShapeDtypeStruct((8, D_MODEL), F32)],
        compiler_params=_cp(("arbitrary",), VMEM_BIG))(*args)


def _dw(name, a, parts, blocked=None):
    t_tok, ka = a.shape
    tt = 512
    npart = len(parts)
    nt = t_tok // tt

    def body(*refs):
        a_ref = refs[0]
        p_refs = refs[1:1 + npart]
        o_refs = refs[1 + npart:1 + 2 * npart]
        acc_refs = refs[1 + 2 * npart:]
        t = pl.program_id(0)
        av = a_ref[...]
        for p_ref, acc in zip(p_refs, acc_refs):
            upd = _tn(av, p_ref[...])

            @pl.when(t == 0)
            def _():
                acc[...] = upd

            @pl.when(t > 0)
            def _():
                acc[...] += upd

        @pl.when(t == nt - 1)
        def _():
            for o_ref, acc in zip(o_refs, acc_refs):
                if blocked is None:
                    o_ref[...] = acc[...].astype(BF16)
                else:
                    for j in range(o_ref.shape[0]):
                        o_ref[j] = acc[:, j * blocked:(j + 1) * blocked].astype(BF16)

    in_specs = [pl.BlockSpec((tt, ka), lambda t: (t, 0))]
    in_specs += [pl.BlockSpec((tt, p.shape[1]), lambda t: (t, 0)) for p in parts]
    if blocked is None:
        out_shape = [jax.ShapeDtypeStruct((ka, p.shape[1]), BF16) for p in parts]
        out_specs = [pl.BlockSpec((ka, p.shape[1]), lambda t: (0, 0)) for p in parts]
    else:
        out_shape = [jax.ShapeDtypeStruct((p.shape[1] // blocked, ka, blocked), BF16) for p in parts]
        out_specs = [pl.BlockSpec((p.shape[1] // blocked, ka, blocked), lambda t: (0, 0, 0)) for p in parts]
    return _pallas(body, name=name, grid=(nt,), in_specs=in_specs, out_specs=out_specs, out_shape=out_shape,
                   scratch_shapes=[pltpu.VMEM((ka, p.shape[1]), F32) for p in parts],
                   compiler_params=_cp(("arbitrary",), VMEM_BIG))(a, *parts)


def _dw_rows(name, rows_t, h):
    t_tok = h.shape[0]
    tt = 512

    def body(r_ref, h_ref, o_ref):
        @pl.when(pl.program_id(0) == 0)
        def _():
            o_ref[...] = jnp.zeros_like(o_ref)

        o_ref[...] += _nn(r_ref[...].astype(BF16), h_ref[...])

    return _pallas(body, name=name, grid=(t_tok // tt,),
                   in_specs=[pl.BlockSpec((8, tt), lambda t: (0, t)), pl.BlockSpec((tt, D_MODEL), lambda t: (t, 0))],
                   out_specs=pl.BlockSpec((8, D_MODEL), lambda t: (0, 0)),
                   out_shape=jax.ShapeDtypeStruct((8, D_MODEL), F32),
                   compiler_params=_cp(("arbitrary",), VMEM_MID))(rows_t, h)


def _lru_gates(xc, blk, wa_ref, wx_ref, ba_ref, bx_ref, sp):
    cols = slice(blk * LRU_BLOCK_W, (blk + 1) * LRU_BLOCK_W)
    xb = xc[:, cols].astype(BF16)
    r = _sigmoid(_nn(xb, wa_ref[blk].astype(BF16)) + ba_ref[:, cols])
    ig = _sigmoid(_nn(xb, wx_ref[blk].astype(BF16)) + bx_ref[:, cols])
    log_a = -LRU_C * r * sp[:, cols]
    a = jnp.exp(log_a)
    x2 = 2.0 * log_a
    series = -x2 * (1.0 + x2 * (0.5 + x2 * (1.0 / 6.0)))
    z = jnp.where(x2 > -0.01, series, 1.0 - a * a)
    mult = z * lax.rsqrt(jnp.maximum(z, 1e-30))
    return xb, r, ig, a, mult


def _softplus_neg(lam):
    return jnp.maximum(-lam, 0.0) + jnp.log(1.0 + jnp.exp(-jnp.abs(lam)))


def _conv_taps(xe_ref, cw_ref, cb_ref):
    xc = cb_ref[...] + xe_ref[8:8 + TC, :] * cw_ref[3:4, :]
    for k in range(1, 4):
        xc = xc + xe_ref[8 - k:8 - k + TC, :] * cw_ref[3 - k:4 - k, :]
    return xc


def _lru_fwd(proj, cw, cb, w_a, b_a, w_x, b_x, lam, seq):
    t_tok = proj.shape[0]
    nc = seq // TC

    def body(x_ref, cw_ref, cb_ref, wa_ref, ba_ref, wx_ref, bx_ref, lam_ref, hs_ref, xe_s, a_s, u_s, h_s):
        c = pl.program_id(1)

        @pl.when(c == 0)
        def _():
            xe_s[0:8, :] = jnp.zeros((8, D_MODEL), F32)
            h_s[...] = jnp.zeros_like(h_s)

        xe_s[8:8 + TC, :] = x_ref[...]
        xc = _conv_taps(xe_s, cw_ref, cb_ref)
        sp = _softplus_neg(lam_ref[...])
        for blk in range(LRU_BLOCKS):
            cols = slice(blk * LRU_BLOCK_W, (blk + 1) * LRU_BLOCK_W)
            _, _, ig, a, mult = _lru_gates(xc, blk, wa_ref, wx_ref, ba_ref, bx_ref, sp)
            a_s[:, cols] = a
            u_s[:, cols] = mult * ig * xc[:, cols]

        def step(t, h):
            h = a_s[pl.ds(t, 1), :] * h + u_s[pl.ds(t, 1), :]
            hs_ref[pl.ds(t, 1), :] = h
            return h

        h_s[0:1, :] = lax.fori_loop(0, TC, step, h_s[0:1, :], unroll=8)
        xe_s[0:8, :] = xe_s[TC:TC + 8, :]

    full = lambda shape: pl.BlockSpec(shape, lambda b, c: (0,) * len(shape))
    return _pallas(
        body, name="lru_fwd", grid=(t_tok // seq, nc),
        in_specs=[pl.BlockSpec((TC, D_MODEL), lambda b, c: (b * nc + c, 0)), full((4, D_MODEL)), full((1, D_MODEL)),
                  full((LRU_BLOCKS, LRU_BLOCK_W, LRU_BLOCK_W)), full((1, D_MODEL)),
                  full((LRU_BLOCKS, LRU_BLOCK_W, LRU_BLOCK_W)), full((1, D_MODEL)), full((1, D_MODEL))],
        out_specs=pl.BlockSpec((TC, D_MODEL), lambda b, c: (b * nc + c, 0)),
        out_shape=jax.ShapeDtypeStruct((t_tok, D_MODEL), F32),
        scratch_shapes=[pltpu.VMEM((TC + 8, D_MODEL), F32), pltpu.VMEM((TC, D_MODEL), F32),
                        pltpu.VMEM((TC, D_MODEL), F32), pltpu.VMEM((8, D_MODEL), F32)],
        compiler_params=_cp(("arbitrary", "arbitrary"), VMEM_MID))(proj, cw, cb, w_a, b_a, w_x, b_x, lam)


def _lru_bwd(proj, hs, dyh, cw, cb, w_a, b_a, w_x, b_x, lam, seq):
    t_tok = proj.shape[0]
    nc = seq // TC

    def body(x_ref, xh_ref, g_ref, hs_ref, hh_ref, dy_ref, cw_ref, cb_ref, wa_ref, ba_ref, wx_ref, bx_ref, lam_ref,
             dp_ref, dcw_ref, dvec_ref, dwa_ref, dwx_ref,
             xe_s, he_s, de_s, a_s, r_s, i_s, m_s, dh_s, carry_s):
        b, cr = pl.program_id(0), pl.program_id(1)
        c = nc - 1 - cr

        @pl.when(jnp.logical_and(b == 0, cr == 0))
        def _():
            dcw_ref[...] = jnp.zeros_like(dcw_ref)
            dvec_ref[...] = jnp.zeros_like(dvec_ref)
            dwa_ref[...] = jnp.zeros_like(dwa_ref)
            dwx_ref[...] = jnp.zeros_like(dwx_ref)

        @pl.when(cr == 0)
        def _():
            carry_s[...] = jnp.zeros_like(carry_s)
            de_s[TC:TC + 8, :] = jnp.zeros((8, D_MODEL), F32)

        first = c == 0
        xe_s[0:8, :] = jnp.where(first, 0.0, xh_ref[...])
        xe_s[8:8 + TC, :] = x_ref[...]
        he_s[0:8, :] = jnp.where(first, 0.0, hh_ref[...])
        he_s[8:8 + TC, :] = hs_ref[...]
        xc = _conv_taps(xe_s, cw_ref, cb_ref)
        lam_v = lam_ref[...]
        sp = _softplus_neg(lam_v)
        for blk in range(LRU_BLOCKS):
            cols = slice(blk * LRU_BLOCK_W, (blk + 1) * LRU_BLOCK_W)
            _, r, ig, a, mult = _lru_gates(xc, blk, wa_ref, wx_ref, ba_ref, bx_ref, sp)
            a_s[:, cols], r_s[:, cols], i_s[:, cols], m_s[:, cols] = a, r, ig, mult

        gt = g_ref[...]
        dyh = dy_ref[...]
        dh_s[...] = dyh * _silu(gt)
        dp_ref[:, D_MODEL:] = (dyh * hs_ref[...] * _dsilu(gt)).astype(BF16)

        def step(k, carry):
            t = TC - 1 - k
            dh = dh_s[pl.ds(t, 1), :] + carry
            dh_s[pl.ds(t, 1), :] = dh
            return a_s[pl.ds(t, 1), :] * dh

        carry_s[0:1, :] = lax.fori_loop(0, TC, step, carry_s[0:1, :], unroll=8)

        hprev = he_s[7:7 + TC, :]
        for blk in range(LRU_BLOCKS):
            cols = slice(blk * LRU_BLOCK_W, (blk + 1) * LRU_BLOCK_W)
            xcb = xc[:, cols]
            a, r, ig, mult, dh = a_s[:, cols], r_s[:, cols], i_s[:, cols], m_s[:, cols], dh_s[:, cols]
            spb = sp[:, cols]
            dmult = dh * ig * xcb
            di = dh * mult * xcb
            dxc = dh * mult * ig
            dla = dh * hprev[:, cols] * a - dmult * (a * a) * lax.rsqrt(jnp.maximum(mult * mult, 1e-30))
            dr = dla * (-LRU_C * spb)
            dsp = jnp.sum(dla * (-LRU_C * r), axis=0, keepdims=True)
            dga = dr * r * (1.0 - r)
            dgx = di * ig * (1.0 - ig)
            dga_b, dgx_b = dga.astype(BF16), dgx.astype(BF16)
            xb = xcb.astype(BF16)
            dxc = dxc + _nt(dga_b, wa_ref[blk].astype(BF16)) + _nt(dgx_b, wx_ref[blk].astype(BF16))
            dwa_ref[blk] += _tn(xb, dga_b)
            dwx_ref[blk] += _tn(xb, dgx_b)
            dvec_ref[1:2, cols] += jnp.sum(dga, axis=0, keepdims=True)
            dvec_ref[2:3, cols] += jnp.sum(dgx, axis=0, keepdims=True)
            dvec_ref[3:4, cols] += dsp * (-1.0 / (1.0 + jnp.exp(lam_v[:, cols])))
            de_s[0:TC, cols] = dxc

        dxc = de_s[0:TC, :]
        dvec_ref[0:1, :] += jnp.sum(dxc, axis=0, keepdims=True)
        dxr = dxc * cw_ref[3:4, :]
        dcw_ref[3:4, :] += jnp.sum(dxc * xe_s[8:8 + TC, :], axis=0, keepdims=True)
        for k in range(1, 4):
            dxr = dxr + de_s[k:k + TC, :] * cw_ref[3 - k:4 - k, :]
            dcw_ref[3 - k:4 - k, :] += jnp.sum(dxc * xe_s[8 - k:8 - k + TC, :], axis=0, keepdims=True)
        dp_ref[:, :D_MODEL] = dxr.astype(BF16)
        de_s[TC:TC + 8, :] = de_s[0:8, :]

    chunk = lambda col: pl.BlockSpec((TC, D_MODEL), lambda b, cr: (b * nc + nc - 1 - cr, col))
    halo = lambda col: pl.BlockSpec(
        (8, D_MODEL), lambda b, cr: (jnp.maximum((b * nc + nc - 1 - cr) * (TC // 8) - 1, 0), col))
    full = lambda shape: pl.BlockSpec(shape, lambda b, cr: (0,) * len(shape))
    wblk = (LRU_BLOCKS, LRU_BLOCK_W, LRU_BLOCK_W)
    return _pallas(
        body, name="lru_bwd", grid=(t_tok // seq, nc),
        in_specs=[chunk(0), halo(0), chunk(1), chunk(0), halo(0), chunk(0),
                  full((4, D_MODEL)), full((1, D_MODEL)), full(wblk), full((1, D_MODEL)), full(wblk),
                  full((1, D_MODEL)), full((1, D_MODEL))],
        out_specs=[pl.BlockSpec((TC, 2 * D_MODEL), lambda b, cr: (b * nc + nc - 1 - cr, 0)),
                   full((8, D_MODEL)), full((8, D_MODEL)), full(wblk), full(wblk)],
        out_shape=[jax.ShapeDtypeStruct((t_tok, 2 * D_MODEL), BF16), jax.ShapeDtypeStruct((8, D_MODEL), F32),
                   jax.ShapeDtypeStruct((8, D_MODEL), F32), jax.ShapeDtypeStruct(wblk, F32),
                   jax.ShapeDtypeStruct(wblk, F32)],
        scratch_shapes=[pltpu.VMEM((TC + 8, D_MODEL), F32), pltpu.VMEM((TC + 8, D_MODEL), F32),
                        pltpu.VMEM((TC + 8, D_MODEL), F32)]
        + [pltpu.VMEM((TC, D_MODEL), F32)] * 5 + [pltpu.VMEM((8, D_MODEL), F32)],
        compiler_params=_cp(("arbitrary", "arbitrary"), VMEM_BIG),
    )(proj, proj, proj, hs, hs, dyh, cw, cb, w_a, b_a, w_x, b_x, lam)


def _last_layer_tail(hs, proj, w_out, x, gmod, final_g, target, seq):
    t_tok = x.shape[0]
    tiles_per_seq = seq // TM

    def body(hs_ref, g_ref, w_ref, x_ref, gm_ref, fg_ref, t_ref,
             yg_ref, dx_ref, dy_ref, dyg_ref, dgm_ref, loss_ref, dfg_ref):
        i = pl.program_id(0)

        @pl.when(i == 0)
        def _():
            loss_ref[...] = jnp.zeros_like(loss_ref)
            dfg_ref[...] = jnp.zeros_like(dfg_ref)

        @pl.when(i % tiles_per_seq == 0)
        def _():
            dgm_ref[...] = jnp.zeros_like(dgm_ref)

        gm = gm_ref[...]
        yg = (hs_ref[...] * _silu(g_ref[...])).astype(BF16)
        yg_ref[...] = yg
        y = _nn(yg, w_ref[...])
        xv = x_ref[...] + gm * y
        gv = fg_ref[...]
        rstd = lax.rsqrt(jnp.mean(xv * xv, axis=-1, keepdims=True) + EPS)
        xhat = xv * rstd
        err = xhat * gv - t_ref[...]
        loss_ref[0:1, :] += jnp.sum(err * err, axis=0, keepdims=True) * (0.5 / D_MODEL)
        dout = err * (1.0 / D_MODEL)
        dfg_ref[0:1, :] += jnp.sum(dout * xhat, axis=0, keepdims=True)
        dxhat = dout * gv
        dxv = rstd * (dxhat - xhat * jnp.mean(dxhat * xhat, axis=-1, keepdims=True))
        dx_ref[...] = dxv
        dgm_ref[...] += jnp.sum(dxv * y, axis=0, keepdims=True)
        dy = (dxv * gm).astype(BF16)
        dy_ref[...] = dy
        dyg_ref[...] = _nt(dy, w_ref[...])

    row = pl.BlockSpec((TM, D_MODEL), lambda i: (i, 0))
    acc = pl.BlockSpec((8, D_MODEL), lambda i: (0, 0))
    mod_spec = pl.BlockSpec((None, 1, D_MODEL), lambda i: (i * TM // seq, 0, 0))
    return _pallas(
        body, name="last_layer_tail", grid=(t_tok // TM,),
        in_specs=[row, pl.BlockSpec((TM, D_MODEL), lambda i: (i, 1)), pl.BlockSpec((D_MODEL, D_MODEL), lambda i: (0, 0)),
                  row, mod_spec, pl.BlockSpec((1, D_MODEL), lambda i: (0, 0)), row],
        out_specs=[row, row, row, row, mod_spec, acc, acc],
        out_shape=[jax.ShapeDtypeStruct((t_tok, D_MODEL), BF16), jax.ShapeDtypeStruct((t_tok, D_MODEL), F32),
                   jax.ShapeDtypeStruct((t_tok, D_MODEL), BF16), jax.ShapeDtypeStruct((t_tok, D_MODEL), F32),
                   jax.ShapeDtypeStruct(gmod.shape, F32), jax.ShapeDtypeStruct((8, D_MODEL), F32),
                   jax.ShapeDtypeStruct((8, D_MODEL), F32)],
        compiler_params=_cp(("arbitrary",), VMEM_BIG))(hs, proj, w_out, x, gmod, final_g, target)


def _final_loss(x, g, target):
    t_tok = x.shape[0]

    def body(x_ref, g_ref, t_ref, dx_ref, loss_ref, dg_ref):
        @pl.when(pl.program_id(0) == 0)
        def _():
            loss_ref[...] = jnp.zeros_like(loss_ref)
            dg_ref[...] = jnp.zeros_like(dg_ref)

        xv = x_ref[...]
        gv = g_ref[...]
        rstd = lax.rsqrt(jnp.mean(xv * xv, axis=-1, keepdims=True) + EPS)
        xhat = xv * rstd
        err = xhat * gv - t_ref[...]
        loss_ref[0:1, :] += jnp.sum(err * err, axis=0, keepdims=True) * (0.5 / D_MODEL)
        dout = err * (1.0 / D_MODEL)
        dg_ref[0:1, :] += jnp.sum(dout * xhat, axis=0, keepdims=True)
        dxhat = dout * gv
        dx_ref[...] = rstd * (dxhat - xhat * jnp.mean(dxhat * xhat, axis=-1, keepdims=True))

    row = pl.BlockSpec((TM, D_MODEL), lambda i: (i, 0))
    acc = pl.BlockSpec((8, D_MODEL), lambda i: (0, 0))
    return _pallas(body, name="final_loss", grid=(t_tok // TM,),
                   in_specs=[row, pl.BlockSpec((1, D_MODEL), lambda i: (0, 0)), row],
                   out_specs=[row, acc, acc],
                   out_shape=[jax.ShapeDtypeStruct((t_tok, D_MODEL), F32)] + [jax.ShapeDtypeStruct((8, D_MODEL), F32)] * 2,
                   compiler_params=_cp(("arbitrary",), VMEM_MID))(x, g, target)


def _adam_math(w, g, m, v):
    m_new = ADAM_B1 * m + (1.0 - ADAM_B1) * g
    v_new = ADAM_B2 * v + (1.0 - ADAM_B2) * (g * g)
    m_hat = m_new / (1.0 - ADAM_B1 ** ADAM_STEP)
    v_hat = v_new / (1.0 - ADAM_B2 ** ADAM_STEP)
    delta = -ADAM_LR * (m_hat / (jnp.sqrt(v_hat) + ADAM_EPS) + ADAM_WD * w)
    return delta, m_new, v_new


def _sum_leading(name, x, out_dtype=F32):
    n, rows, cols = x.shape
    tr = PACK_ROWS if rows % PACK_ROWS == 0 else rows

    def body(x_ref, o_ref):
        acc = x_ref[0].astype(F32)
        for d in range(1, n):
            acc = acc + x_ref[d].astype(F32)
        o_ref[...] = acc.astype(out_dtype)

    return _pallas(body, name=name, grid=(rows // tr,),
                   in_specs=[pl.BlockSpec((n, tr, cols), lambda i: (0, i, 0))],
                   out_specs=pl.BlockSpec((tr, cols), lambda i: (i, 0)),
                   out_shape=jax.ShapeDtypeStruct((rows, cols), out_dtype),
                   compiler_params=_cp(("arbitrary",), VMEM_MID))(x)


def _adamw(name, w, m, v, g=None, parts=None):
    rows, cols = w.shape
    tr = rows if rows <= 256 else 256

    def body(*refs):
        w_ref, m_ref, v_ref, g_in, g_ref, d_ref, mo_ref, vo_ref = refs
        if parts is None:
            gv = g_in[...]
        else:
            acc = g_in[0].astype(F32)
            for d in range(1, parts.shape[0]):
                acc = acc + g_in[d].astype(F32)
            gv = acc[:, :cols]
        delta, m_new, v_new = _adam_math(w_ref[...], gv, m_ref[...], v_ref[...])
        g_ref[...] = gv
        d_ref[...] = delta
        mo_ref[...] = m_new
        vo_ref[...] = v_new

    row = pl.BlockSpec((tr, cols), lambda i: (i, 0))
    if parts is None:
        g_spec, g_arg = row, g
    else:
        g_spec, g_arg = pl.BlockSpec((parts.shape[0], tr, parts.shape[2]), lambda i: (0, i, 0)), parts
    return _pallas(body, name=name, grid=(rows // tr,), in_specs=[row, row, row, g_spec], out_specs=[row] * 4,
                   out_shape=[jax.ShapeDtypeStruct((rows, cols), F32)] * 4,
                   compiler_params=_cp(("arbitrary",), VMEM_MID))(w, m, v, g_arg)


def _adamw_many(name, groups):
    ntens = len(groups)

    def body(*refs):
        ins, outs = refs[:4 * ntens], refs[4 * ntens:]
        for k in range(ntens):
            w_ref, m_ref, v_ref, g_ref = ins[4 * k:4 * k + 4]
            gv = g_ref[...]
            delta, m_new, v_new = _adam_math(w_ref[...], gv, m_ref[...], v_ref[...])
            for o_ref, val in zip(outs[4 * k:4 * k + 4], (gv, delta, m_new, v_new)):
                o_ref[...] = val

    flat = [a for grp in groups for a in grp]
    out_shape = [jax.ShapeDtypeStruct(grp[0].shape, F32) for grp in groups for _ in range(4)]
    outs = _pallas(body, name=name, out_shape=out_shape, compiler_params=_cp(vmem=VMEM_MID))(*flat)
    return [tuple(outs[4 * k:4 * k + 4]) for k in range(ntens)]


def _pack_rows(arrs):
    rows, meta, total = [], [], 0
    for a in arrs:
        flat = a.reshape(-1)
        nrow = -(-flat.shape[0] // 1024) * 8
        rows.append(jnp.pad(flat, (0, nrow * 128 - flat.shape[0])).reshape(nrow, 128))
        meta.append((a.shape, flat.shape[0], nrow))
        total += nrow
    tail = -total % PACK_ROWS
    if tail:
        rows.append(jnp.zeros((tail, 128), F32))
    return jnp.concatenate(rows, axis=0), meta


def _unpack_rows(packed, meta):
    out, r0 = [], 0
    for shape, size, nrow in meta:
        out.append(packed[r0:r0 + nrow].reshape(-1)[:size].reshape(shape))
        r0 += nrow
    return out


WEIGHTS = ["rel_bias", "norm_g", "ada_w", "ada_b", "attn_w_in", "attn_sinks", "attn_b_f", "attn_w_out", "lru_w_in",
           "lru_conv_w", "lru_conv_b", "lru_w_a", "lru_b_a", "lru_w_x", "lru_b_x", "lru_lambda", "lru_w_out", "final_g"]
BIG = ["ada_w", "attn_w_in", "attn_w_out", "lru_w_in", "lru_w_out"]
PACK_ROWS = 256


def kernel(x, c, rel_bias, norm_g, ada_w, ada_b, attn_w_in, attn_sinks, attn_b_f, attn_w_out, lru_w_in, lru_conv_w, lru_conv_b, lru_w_a, lru_b_a, lru_w_x, lru_b_x, lru_lambda, lru_w_out, final_g, loss_target, m_rel_bias, m_norm_g, m_ada_w, m_ada_b, m_attn_w_in, m_attn_sinks, m_attn_b_f, m_attn_w_out, m_lru_w_in, m_lru_conv_w, m_lru_conv_b, m_lru_w_a, m_lru_b_a, m_lru_w_x, m_lru_b_x, m_lru_lambda, m_lru_w_out, m_final_g, v_rel_bias, v_norm_g, v_ada_w, v_ada_b, v_attn_w_in, v_attn_sinks, v_attn_b_f, v_attn_w_out, v_lru_w_in, v_lru_conv_w, v_lru_conv_b, v_lru_w_a, v_lru_b_a, v_lru_w_x, v_lru_b_x, v_lru_lambda, v_lru_w_out, v_final_g):
    nseq, seq, _ = x.shape
    t_tok = nseq * seq
    me = 4 * lax.axis_index("x") + 2 * lax.axis_index("y") + lax.axis_index("c")
    x0 = x.reshape(t_tok, D_MODEL)
    target = loss_target.reshape(t_tok, D_MODEL)

    w_in_pad = jnp.pad(attn_w_in[0].astype(BF16), ((0, 0), (0, SHARD_W_PAD - SHARD_W_IN)))
    vec_shard = jnp.concatenate([lru_conv_w[0], lru_conv_b, lru_b_a, lru_b_x, lru_lambda], axis=0)
    g_w_in, g_vec, g_c = _exchange("gather_first", [w_in_pad, vec_shard, c], [])
    later_w = [attn_w_out[0].astype(BF16), lru_w_in[0].astype(BF16), lru_w_out[0].astype(BF16)]
    later_handle, later_token = _exchange_start("gather_later_start", later_w, [], after=g_vec)
    w_full = jnp.transpose(g_w_in[:, :, :SHARD_W_IN], (1, 0, 2)).reshape(D_MODEL, N_DEV * SHARD_W_IN)
    w_aq, w_ak, w_av = w_full[:, 0:512], w_full[:, 512:640], w_full[:, 640:768]
    w_bq, w_bk, w_bv = w_full[:, 768:1280], w_full[:, 1280:1792], w_full[:, 1792:2304]
    w_f, w_gate = w_full[:, 2304:2312], w_full[:, 2312:3336]
    w_main = jnp.concatenate([w_bq, w_bk, w_bv, w_aq, w_gate, w_ak, w_av], axis=1)
    wf_t = jnp.transpose(w_f)
    vec_full = jnp.transpose(g_vec, (1, 0, 2)).reshape(8, D_MODEL)
    conv_w, conv_b, b_a, b_x, lam = vec_full[0:4], vec_full[4:5], vec_full[5:6], vec_full[6:7], vec_full[7:8]
    c_all = g_c.reshape(N_DEV * nseq, D_MODEL)

    ncol = ada_w.shape[2]
    ada_b_slice = lax.dynamic_slice(ada_b.reshape(2, N_DEV, ncol), (0, me, 0), (2, 1, ncol))
    mod_part = _ada_mod(c_all, ada_w, ada_b_slice)
    (g_mod,) = _exchange("gather_mod", [mod_part], [])
    mine = lax.dynamic_slice(g_mod, (0, 0, me * nseq, 0), (N_DEV, 2, nseq, ncol))
    mod = jnp.transpose(mine, (1, 2, 0, 3)).reshape(2, nseq, 3 * D_MODEL)
    shift = [mod[l, :, 0:D_MODEL].reshape(nseq, 1, D_MODEL) for l in range(2)]
    scale = [mod[l, :, D_MODEL:2 * D_MODEL].reshape(nseq, 1, D_MODEL) for l in range(2)]
    gmod = [mod[l, :, 2 * D_MODEL:].reshape(nseq, 1, D_MODEL) for l in range(2)]

    onehot = _bucket_onehot()
    bias = _bias_expand(jnp.transpose(rel_bias), onehot).reshape(N_HEADS, BLOCK, 2 * BLOCK)
    sinks = attn_sinks.reshape(N_HEADS)
    b_f = attn_b_f.reshape(N_HEADS, 1)
    norm_g0 = norm_g[0:1] + later_token[0:1, 0:1]
    h0, qkvg, fl_t = _norm_proj("norm_proj0", x0, norm_g0, shift[0], scale[0], w_main, seq, BF16, wf_t=wf_t)
    f_row, f_col = _fox_prep(fl_t, b_f, seq)
    a_out, lse_a = _swa_fwd(qkvg, bias, sinks, seq)
    q_aug, k_aug, kt_aug, vt = _fox_aug(qkvg, f_col, seq)
    b_out, lse_b = _fox_fwd_t(q_aug, k_aug, vt, seq)
    g_later = _exchange_wait("gather_later_wait", later_handle, after=lse_b)
    w_out0, g_lru_in, w_out1 = (_with_own(g, w, me) for g, w in zip(g_later, later_w))
    w_out0, w_out1 = w_out0.reshape(D_MODEL, D_MODEL), w_out1.reshape(D_MODEL, D_MODEL)
    yg0, y0, x1 = _out_proj("out_proj0", [a_out, b_out], qkvg, C_GATE // D_MODEL, w_out0, x0, gmod[0], seq)

    h1, proj1 = _norm_proj("norm_proj1", x1, norm_g[1:2], shift[1], scale[1], g_lru_in, seq, F32)
    hs = _lru_fwd(proj1, conv_w, conv_b, lru_w_a[0], b_a, lru_w_x[0], b_x, lam, seq)

    yg1, dx2, dy1, dyh, dgm1, loss_rows, dfinal_rows = _last_layer_tail(
        hs, proj1, w_out1, x1, gmod[1], final_g.reshape(1, D_MODEL), target, seq)

    dproj1, dcw, dvec, dw_a, dw_x = _lru_bwd(proj1, hs, dyh, conv_w, conv_b, lru_w_a[0], b_a, lru_w_x[0], b_x, lam, seq)
    dx1, dss1, dg1 = _norm_bwd("norm1_bwd", [(dproj1, 0)], g_lru_in, x1, norm_g[1:2], scale[1], dx2, seq)
    (p_w_out1,) = _dw("dw_out1", yg1, [dy1])
    (p_lru_in,) = _dw("dw_lru_in", h1, [dproj1], blocked=2 * D_MODEL // N_DEV)

    rows_out = D_MODEL // N_DEV
    gpack1, gmeta1 = _pack_rows([dcw[0:4], dvec[0:4], dg1[0], dfinal_rows[0]])
    dwax = jnp.stack([dw_a, dw_x]).astype(BF16)
    own1 = [gpack1, dwax, p_lru_in, p_w_out1.reshape(N_DEV, rows_out, D_MODEL)]
    grads1_handle, grads1_token = _exchange_start("grads1_start", own1[:2], own1[2:], after=dx1)

    gmod0 = gmod[0] + grads1_token[0:1, 0:1]
    dy0, dgm0, du_a, du_b, dgate = _out_proj_bwd("out_proj0_bwd", dx1, gmod0, y0, w_out0, seq,
                                                  attn=(a_out, b_out, qkvg))
    dq_a, dkv_a, dbias, dsink = _swa_bwd(qkvg, du_a, a_out, lse_a, bias, sinks, seq)
    dq_b, dk_b, dv_b, df4 = _fox_bwd_t(q_aug, k_aug, kt_aug, qkvg, du_b, b_out, lse_b, seq)
    dfl_t, db_f = _fox_post(df4.reshape(N_HEADS, t_tok), fl_t, b_f, seq)
    parts0 = [(dq_b, C_BQ), (dk_b, C_BK), (dv_b, C_BV), (dq_a, C_AQ), (dgate, C_GATE), (dkv_a, C_AK)]
    (p_w_out0,) = _dw("dw_out0", yg0, [dy0])
    pw_bq, pw_bk, pw_bv, pw_aq, pw_gate, pw_akv = _dw("dw_attn_in", h0, [p for p, _ in parts0])
    pw_f = _dw_rows("dw_f", dfl_t, h0)

    p_w_in = jnp.concatenate([pw_aq, pw_akv, pw_bq, pw_bk, pw_bv, jnp.transpose(pw_f).astype(BF16), pw_gate], axis=1)
    p_w_in = jnp.transpose(p_w_in.reshape(D_MODEL, N_DEV, SHARD_W_IN), (1, 0, 2))
    p_w_in = jnp.pad(p_w_in, ((0, 0), (0, 0), (0, SHARD_W_PAD - SHARD_W_IN)))
    own0 = [p_w_in, p_w_out0.reshape(N_DEV, rows_out, D_MODEL)]
    landed1 = _exchange_wait("grads1_wait", grads1_handle, after=p_w_in)
    grads0_handle, grads0_token = _exchange_start("grads0_start", [], own0, after=landed1[0])
    scale0 = scale[0] + grads0_token[0:1, 0:1]
    dx0, dss0, dg0 = _norm_bwd("norm0_bwd", parts0, w_main, x0, norm_g[0:1], scale0, dx1, seq,
                               rows_part=(dfl_t, wf_t))
    dbias_t = _bias_reduce(dbias.reshape(N_HEADS, BLOCK * 2 * BLOCK), onehot)

    gpack0, gmeta0 = _pack_rows([jnp.transpose(dbias_t), dg0[0], dsink[:, 0], db_f[:, 0], loss_rows[0]])
    dmod = jnp.stack([jnp.concatenate([dss[:, 0], dss[:, 1], dgm[:, 0]], axis=1)
                      for dss, dgm in ((dss0, dgm0), (dss1, dgm1))], axis=1)
    g_small0, g_dmod = _exchange("exchange_small", [gpack0, dmod], [])
    landed0 = _exchange_wait("grads0_wait", grads0_handle, after=g_small0)
    r_w_in, r_w_out0 = (_with_own(g, lax.dynamic_index_in_dim(a, me, 0, keepdims=False), me)
                        for g, a in zip(landed0, own0))
    g_small1, g_dwax = (_with_own(g, a, me) for g, a in zip(landed1[:2], own1[:2]))
    r_lru_in, r_w_out1 = (_with_own(g, lax.dynamic_index_in_dim(a, me, 0, keepdims=False), me)
                          for g, a in zip(landed1[2:], own1[2:]))

    d_rel, d_g0, d_sinks, d_b_f, loss_cols = _unpack_rows(_sum_leading("sum_small0", g_small0), gmeta0)
    loss = jnp.sum(loss_cols)
    d_cw, d_vec, d_g1, d_final_g = _unpack_rows(_sum_leading("sum_small1", g_small1), gmeta1)
    d_norm_g = jnp.stack([d_g0, d_g1])
    d_wax = _sum_leading("sum_dwax", g_dwax.reshape(N_DEV, 2 * LRU_BLOCKS * LRU_BLOCK_W, LRU_BLOCK_W))
    d_wa, d_wx = d_wax[:LRU_BLOCKS * LRU_BLOCK_W], d_wax[LRU_BLOCKS * LRU_BLOCK_W:]
    cols = lambda a: lax.dynamic_slice(a, (0, me * LRU_BLOCK_W), (a.shape[0], LRU_BLOCK_W))
    dmod_all = g_dmod.reshape(N_DEV * nseq, 2 * 3 * D_MODEL)
    d_ada_b = _sum_leading("sum_ada_b", dmod_all.reshape(N_DEV * nseq, 2 * 3 * D_MODEL // 128, 128)).reshape(2, 3 * D_MODEL)
    dmod_slice = lax.dynamic_slice(dmod_all.reshape(N_DEV * nseq, 2, N_DEV, ncol), (0, 0, me, 0),
                                   (N_DEV * nseq, 2, 1, ncol)).reshape(N_DEV * nseq, 2, ncol)
    d_ada_w = _ada_w_grad(c_all, jnp.transpose(dmod_slice, (1, 0, 2)))

    given = dict(
        rel_bias=(rel_bias, m_rel_bias, v_rel_bias), norm_g=(norm_g, m_norm_g, v_norm_g),
        ada_w=(ada_w, m_ada_w, v_ada_w), ada_b=(ada_b, m_ada_b, v_ada_b),
        attn_w_in=(attn_w_in, m_attn_w_in, v_attn_w_in), attn_sinks=(attn_sinks, m_attn_sinks, v_attn_sinks),
        attn_b_f=(attn_b_f, m_attn_b_f, v_attn_b_f), attn_w_out=(attn_w_out, m_attn_w_out, v_attn_w_out),
        lru_w_in=(lru_w_in, m_lru_w_in, v_lru_w_in), lru_conv_w=(lru_conv_w, m_lru_conv_w, v_lru_conv_w),
        lru_conv_b=(lru_conv_b, m_lru_conv_b, v_lru_conv_b), lru_w_a=(lru_w_a, m_lru_w_a, v_lru_w_a),
        lru_b_a=(lru_b_a, m_lru_b_a, v_lru_b_a), lru_w_x=(lru_w_x, m_lru_w_x, v_lru_w_x),
        lru_b_x=(lru_b_x, m_lru_b_x, v_lru_b_x), lru_lambda=(lru_lambda, m_lru_lambda, v_lru_lambda),
        lru_w_out=(lru_w_out, m_lru_w_out, v_lru_w_out), final_g=(final_g, m_final_g, v_final_g))
    results = {}

    def big(name, shape2d, g=None, parts=None):
        w, m, v = (a.reshape(shape2d) for a in given[name])
        outs = _adamw("adamw_" + name, w, m, v, g=g, parts=parts)
        results[name] = tuple(o.reshape(given[name][0].shape) for o in outs)

    big("ada_w", (2 * D_MODEL, ncol), g=d_ada_w.reshape(2 * D_MODEL, ncol))
    big("attn_w_in", (D_MODEL, SHARD_W_IN), parts=r_w_in)
    big("attn_w_out", (rows_out, D_MODEL), parts=r_w_out0)
    big("lru_w_in", (D_MODEL, 2 * D_MODEL // N_DEV), parts=r_lru_in)
    big("lru_w_out", (rows_out, D_MODEL), parts=r_w_out1)

    small_grads = dict(
        rel_bias=d_rel, norm_g=d_norm_g, ada_b=d_ada_b, attn_sinks=d_sinks.reshape(1, N_HEADS),
        attn_b_f=d_b_f.reshape(1, N_HEADS), lru_conv_w=cols(d_cw).reshape(1, 4, LRU_BLOCK_W),
        lru_conv_b=cols(d_vec[0:1]), lru_w_a=d_wa.reshape(lru_w_a.shape), lru_b_a=cols(d_vec[1:2]),
        lru_w_x=d_wx.reshape(lru_w_x.shape), lru_b_x=cols(d_vec[2:3]), lru_lambda=cols(d_vec[3:4]),
        final_g=d_final_g)
    small = [n for n in WEIGHTS if n not in BIG]
    as2d = lambda a: a.reshape(-1, a.shape[-1])
    outs = _adamw_many("adamw_small", [tuple(as2d(a) for a in given[n]) + (as2d(small_grads[n]),) for n in small])
    for n, group in zip(small, outs):
        results[n] = tuple(o.reshape(given[n][0].shape) for o in group)

    grad_x = dx0.reshape(x.shape)
    out = [loss, grad_x]
    for j in range(4):
        out += [results[n][j] for n in WEIGHTS]
    return tuple(out)
```

```python
import functools
import math

import jax
import jax.numpy as jnp
from jax import lax
from jax.experimental import pallas as pl
from jax.experimental.pallas import tpu as pltpu

F32 = jnp.float32
BF16 = jnp.bfloat16
HI = lax.Precision.HIGHEST
MESH = pl.DeviceIdType.MESH

N_DEV = 8
D_MODEL = 1024
HEAD_DIM = 64
N_HEADS = 8
KV_GROUP = 4
BLOCK = 128
REL_BUCKETS = 32
REL_MAX_EXACT = 16
REL_MAX_DIST = 128
LRU_BLOCKS = 8
LRU_BLOCK_W = 128
LRU_C = 8.0
EPS = 1e-6
SCALE = HEAD_DIM ** -0.5
NEG = -1e30

ADAM_LR = 0.001
ADAM_B1 = 0.9
ADAM_B2 = 0.999
ADAM_EPS = 1e-08
ADAM_WD = 0.01
ADAM_STEP = 10

C_BQ, C_BK, C_BV, C_AQ, C_GATE, C_AK, C_AV = 0, 512, 1024, 1536, 2048, 3072, 3200
N_MAIN = 3328
SHARD_W_IN = 417
SHARD_W_PAD = 512

TM = 256
TQ = 256
TK = 128
TKB = 256
TC = 256
SWA_SUB = 4
VMEM_BIG = 56 * 1024 * 1024
VMEM_MID = 40 * 1024 * 1024


def _pallas(body, **kw):
    return pl.pallas_call(body, **kw)


def _cp(sem=None, vmem=None):
    kw = {}
    if sem is not None:
        kw["dimension_semantics"] = sem
    if vmem is not None:
        kw["vmem_limit_bytes"] = vmem
    return pltpu.CompilerParams(**kw)


def _nn(a, b, precision=None):
    return jnp.dot(a, b, preferred_element_type=F32, precision=precision)


def _nt(a, b, precision=None):
    return lax.dot_general(a, b, (((1,), (1,)), ((), ())), preferred_element_type=F32, precision=precision)


def _tn(a, b, precision=None):
    return lax.dot_general(a, b, (((0,), (0,)), ((), ())), preferred_element_type=F32, precision=precision)


def _sigmoid(x):
    return 1.0 / (1.0 + jnp.exp(-x))


def _silu(x):
    return x * _sigmoid(x)


def _dsilu(x):
    s = _sigmoid(x)
    return s * (1.0 + x * (1.0 - s))


def _neg_expm1(x):
    poly = x * (1.0 + x * (0.5 + x * (1.0 / 6.0 + x * (1.0 / 24.0))))
    return -jnp.where(jnp.abs(x) < 0.05, poly, jnp.exp(x) - 1.0)


def _col(tile, idx):
    lane = lax.broadcasted_iota(jnp.int32, tile.shape, 1)
    return jnp.sum(jnp.where(lane == idx, tile, 0.0), axis=1, keepdims=True)


def _row(tile, idx):
    sub = lax.broadcasted_iota(jnp.int32, tile.shape, 0)
    return jnp.sum(jnp.where(sub == idx, tile, 0.0), axis=0, keepdims=True)


def _exchange(name, gathers, scatters, axes=("x", "y", "c"), chunks=1):
    ng, n = len(gathers), len(gathers) + len(scatters)
    ins = list(gathers) + list(scatters)
    group = 2 ** len(axes)

    def body(*refs):
        in_refs, out_refs = refs[:n], refs[n:2 * n]
        send_sems, recv_sems, loc_sems = refs[2 * n:]
        coord = {a: lax.axis_index(a) for a in ("x", "y", "c")}

        def member(r):
            pc = dict(coord)
            idx = 0
            for k, a in enumerate(axes):
                if r & (1 << (len(axes) - 1 - k)):
                    pc[a] = 1 - coord[a]
                idx = 2 * idx + pc[a]
            return (pc["x"], pc["y"], pc["c"]), idx

        _, me = member(0)

        def peer(r):
            return member(r)

        local, sends, recvs = [], [], []
        for k in range(n):
            mine = in_refs[k] if k < ng else in_refs[k].at[me]
            cp = pltpu.make_async_copy(mine, out_refs[k].at[me], loc_sems.at[k])
            cp.start()
            local.append(cp)
            lead = mine.shape[0]
            nchunk = max(q for q in range(1, chunks + 1) if lead % q == 0)
            step = lead // nchunk
            for r in range(1, group):
                pid, pidx = peer(r)
                src = in_refs[k] if k < ng else in_refs[k].at[pidx]
                for q in range(nchunk):
                    rows = pl.ds(q * step, step)
                    sems = dict(send_sem=send_sems.at[r - 1, k, q], recv_sem=recv_sems.at[r - 1, k, q],
                                device_id=pid, device_id_type=MESH)
                    snd = pltpu.make_async_remote_copy(src_ref=src.at[rows], dst_ref=out_refs[k].at[me].at[rows], **sems)
                    snd.start()
                    sends.append(snd)
                    recvs.append(pltpu.make_async_remote_copy(
                        src_ref=src.at[rows], dst_ref=out_refs[k].at[pidx].at[rows], **sems))
        for rc in recvs:
            rc.wait_recv()
        for snd in sends:
            snd.wait_send()
        for cp in local:
            cp.wait()

    out_shape = [jax.ShapeDtypeStruct((group,) + a.shape, a.dtype) for a in gathers]
    out_shape += [jax.ShapeDtypeStruct(a.shape, a.dtype) for a in scatters]
    any_spec = pl.BlockSpec(memory_space=pl.ANY)
    return _pallas(
        body, name=name, out_shape=out_shape,
        in_specs=[any_spec] * n, out_specs=[any_spec] * n,
        scratch_shapes=[pltpu.SemaphoreType.DMA((group - 1, n, chunks)), pltpu.SemaphoreType.DMA((group - 1, n, chunks)),
                        pltpu.SemaphoreType.DMA((n,))],
    )(*ins)


def _peer_of(r):
    x, y, c = lax.axis_index("x"), lax.axis_index("y"), lax.axis_index("c")
    px = 1 - x if r & 4 else x
    py = 1 - y if r & 2 else y
    pc = 1 - c if r & 1 else c
    return (px, py, pc), 4 * px + 2 * py + pc


def _split_copies(in_refs, land_refs, send_sems, recv_sems, ng, with_recv):
    _, me = _peer_of(0)
    pairs = []
    for k, (src_ref, land) in enumerate(zip(in_refs, land_refs)):
        for r in range(1, N_DEV):
            pid, pidx = _peer_of(r)
            src = src_ref if k < ng else src_ref.at[pidx]
            slot = (N_DEV - 1) * k + r - 1
            sems = dict(send_sem=send_sems.at[slot], recv_sem=recv_sems.at[slot], device_id=pid, device_id_type=MESH)
            send = pltpu.make_async_remote_copy(src_ref=src, dst_ref=land.at[me], **sems)
            recv = pltpu.make_async_remote_copy(src_ref=src, dst_ref=land.at[pidx], **sems) if with_recv else None
            pairs.append((send, recv))
    return pairs


def _exchange_start(name, gathers, scatters, after):
    ng, n = len(gathers), len(gathers) + len(scatters)
    ins = list(gathers) + list(scatters)
    lands = [jax.ShapeDtypeStruct((N_DEV,) + a.shape, a.dtype) for a in gathers]
    lands += [jax.ShapeDtypeStruct(a.shape, a.dtype) for a in scatters]

    def body(*refs):
        in_refs, land_refs = refs[:n], refs[n:2 * n]
        send_sems, recv_sems = refs[2 * n + 1:2 * n + 3]
        token = refs[-1]
        for send, _ in _split_copies(in_refs, land_refs, send_sems, recv_sems, ng, False):
            send.start()
        token[...] = jnp.zeros_like(token)

    hbm = pl.BlockSpec(memory_space=pltpu.HBM)
    sem = pl.BlockSpec(memory_space=pltpu.SEMAPHORE)
    sem_shape = pltpu.SemaphoreType.DMA(((N_DEV - 1) * n,))
    out_shape = [sem_shape, sem_shape] + [pltpu.HBM(a.shape, a.dtype) for a in ins]
    out_shape += [pltpu.HBM(l.shape, l.dtype) for l in lands] + [jax.ShapeDtypeStruct((8, 128), F32)]
    args = [pltpu.with_memory_space_constraint(a, pltpu.HBM) for a in ins]
    args += [pltpu.with_memory_space_constraint(lax.empty(l.shape, l.dtype), pltpu.HBM) for l in lands]
    outs = _pallas(
        body, name=name, out_shape=out_shape,
        in_specs=[hbm] * (2 * n) + [pl.BlockSpec(memory_space=pl.ANY)],
        out_specs=[sem, sem] + [hbm] * (2 * n) + [pl.BlockSpec(memory_space=pltpu.VMEM)],
        input_output_aliases={i: 2 + i for i in range(2 * n)},
        compiler_params=pltpu.CompilerParams(has_side_effects=pltpu.SideEffectType.DATAFLOW_SIDE_EFFECTING),
    )(*args, after)
    return (outs[0], outs[1], list(outs[2:2 + n]), list(outs[2 + n:2 + 2 * n]), ng), outs[-1]


def _exchange_wait(name, handle, after):
    send_sems, recv_sems, srcs, lands, ng = handle
    n = len(srcs)

    def body(*refs):
        in_refs, land_refs = refs[:n], refs[n:2 * n]
        send_ref, recv_ref = refs[2 * n:2 * n + 2]
        for send, recv in _split_copies(in_refs, land_refs, send_ref, recv_ref, ng, True):
            send.wait_send()
            recv.wait_recv()

    hbm = pl.BlockSpec(memory_space=pltpu.HBM)
    sem = pl.BlockSpec(memory_space=pltpu.SEMAPHORE)
    outs = _pallas(
        body, name=name, out_shape=[pltpu.HBM(a.shape, a.dtype) for a in srcs + lands],
        in_specs=[hbm] * (2 * n) + [sem, sem, pl.BlockSpec(memory_space=pl.ANY)],
        out_specs=[hbm] * (2 * n), input_output_aliases={i: i for i in range(2 * n)},
        compiler_params=pltpu.CompilerParams(has_side_effects=pltpu.SideEffectType.DATAFLOW_SIDE_EFFECTING),
    )(*srcs, *lands, send_sems, recv_sems, after)
    return list(outs[n:])


def _with_own(land, own, me):
    return lax.dynamic_update_slice(land, own[None], (me,) + (0,) * own.ndim)


def _ada_mod(c_all, ada_w, ada_b_slice):
    def body(c_ref, w_ref, b_ref, o_ref):
        ca = _silu(c_ref[...])
        for l in range(2):
            o_ref[l] = _nn(ca, w_ref[l], HI) + b_ref[l]

    return _pallas(body, name="ada_mod",
                   out_shape=jax.ShapeDtypeStruct((2, c_all.shape[0], ada_w.shape[2]), F32),
                   compiler_params=_cp(vmem=VMEM_MID))(c_all, ada_w, ada_b_slice)


def _ada_w_grad(c_all, dmod_slice):
    def body(c_ref, d_ref, o_ref):
        ca = _silu(c_ref[...])
        for l in range(2):
            o_ref[l] = _tn(ca, d_ref[l], HI)

    return _pallas(body, name="ada_w_grad",
                   out_shape=jax.ShapeDtypeStruct((2, D_MODEL, dmod_slice.shape[2]), F32),
                   compiler_params=_cp(vmem=VMEM_MID))(c_all, dmod_slice)


def _bucket_onehot():
    qi = jnp.arange(BLOCK)[:, None]
    kj = jnp.arange(2 * BLOCK)[None, :]
    rel = qi - kj + BLOCK
    n = jnp.maximum(rel, 0)
    nf = jnp.maximum(n, 1).astype(F32)
    large = REL_MAX_EXACT + (jnp.log(nf / REL_MAX_EXACT) / math.log(REL_MAX_DIST / REL_MAX_EXACT)
                             * (REL_BUCKETS - REL_MAX_EXACT)).astype(jnp.int32)
    large = jnp.minimum(large, REL_BUCKETS - 1)
    bucket = jnp.where(n < REL_MAX_EXACT, n, large).reshape(1, BLOCK * 2 * BLOCK)
    return (jnp.arange(REL_BUCKETS)[:, None] == bucket).astype(F32)


def _bias_expand(rel_bias_t, onehot):
    def body(r_ref, e_ref, o_ref):
        o_ref[...] = _nn(r_ref[...], e_ref[...], HI)

    return _pallas(body, name="bias_expand",
                   out_shape=jax.ShapeDtypeStruct((N_HEADS, onehot.shape[1]), F32),
                   compiler_params=_cp(vmem=VMEM_MID))(rel_bias_t, onehot)


def _bias_reduce(dbias, onehot):
    def body(d_ref, e_ref, o_ref):
        o_ref[...] = _nt(d_ref[...], e_ref[...], HI)

    return _pallas(body, name="bias_reduce",
                   out_shape=jax.ShapeDtypeStruct((N_HEADS, REL_BUCKETS), F32),
                   compiler_params=_cp(vmem=VMEM_MID))(dbias, onehot)


def _norm_proj(name, x, g, shift, scale, w, seq, out_dtype, wf_t=None):
    t_tok = x.shape[0]
    w3d = w.ndim == 3
    n_out = w.shape[0] * w.shape[2] if w3d else w.shape[1]
    cn = w.shape[2] if w3d else 256

    def body(x_ref, g_ref, sh_ref, sc_ref, w_ref, *rest):
        if wf_t is not None:
            wf_ref, h_ref, o_ref, fl_ref = rest
        else:
            h_ref, o_ref = rest
        xv = x_ref[...]
        rstd = lax.rsqrt(jnp.mean(xv * xv, axis=-1, keepdims=True) + EPS)
        h = (xv * rstd) * g_ref[...] * (1.0 + sc_ref[...]) + sh_ref[...]
        hb = h.astype(BF16)
        h_ref[...] = hb
        for j in range(n_out // cn):
            wj = w_ref[j] if w3d else w_ref[:, j * cn:(j + 1) * cn]
            o_ref[:, j * cn:(j + 1) * cn] = _nn(hb, wj).astype(out_dtype)
        if wf_t is not None:
            fl_ref[...] = _nt(wf_ref[...], hb)

    mod_spec = pl.BlockSpec((None, 1, D_MODEL), lambda i: (i * TM // seq, 0, 0))
    w_spec = (pl.BlockSpec(w.shape, lambda i: (0, 0, 0)) if w3d else pl.BlockSpec(w.shape, lambda i: (0, 0)))
    in_specs = [pl.BlockSpec((TM, D_MODEL), lambda i: (i, 0)), pl.BlockSpec((1, D_MODEL), lambda i: (0, 0)),
                mod_spec, mod_spec, w_spec]
    out_shape = [jax.ShapeDtypeStruct((t_tok, D_MODEL), BF16), jax.ShapeDtypeStruct((t_tok, n_out), out_dtype)]
    out_specs = [pl.BlockSpec((TM, D_MODEL), lambda i: (i, 0)), pl.BlockSpec((TM, n_out), lambda i: (i, 0))]
    args = [x, g, shift, scale, w]
    if wf_t is not None:
        in_specs.append(pl.BlockSpec(wf_t.shape, lambda i: (0, 0)))
        out_shape.append(jax.ShapeDtypeStruct((wf_t.shape[0], t_tok), F32))
        out_specs.append(pl.BlockSpec((wf_t.shape[0], TM), lambda i: (0, i)))
        args.append(wf_t)
    return _pallas(body, name=name, grid=(t_tok // TM,), in_specs=in_specs, out_specs=out_specs,
                   out_shape=out_shape, compiler_params=_cp(("arbitrary",), VMEM_BIG))(*args)


def _fox_prep(fl_t, b_f, seq):
    t_tok = fl_t.shape[1]
    ch = 256

    def body(fl_ref, bf_ref, fr_ref, fc_ref):
        z = fl_ref[...] + bf_ref[...]
        logf = jnp.minimum(z, 0.0) - jnp.log(1.0 + jnp.exp(-jnp.abs(z)))
        ri = lax.broadcasted_iota(jnp.int32, (ch, ch), 0)
        ci = lax.broadcasted_iota(jnp.int32, (ch, ch), 1)
        upper = (ri <= ci).astype(F32)
        eye = (ri == ci).astype(F32)
        carry = jnp.zeros((N_HEADS, 1), F32)
        for k in range(seq // ch):
            fk = _nn(logf[:, k * ch:(k + 1) * ch], upper, HI) + carry
            carry = fk[:, ch - 1:ch]
            fr_ref[:, k * ch:(k + 1) * ch] = fk
            padded = jnp.concatenate([fk, jnp.zeros((128 - N_HEADS, ch), F32)], axis=0)
            fc_ref[k * ch:(k + 1) * ch, :] = _nt(eye, padded, HI)

    return _pallas(
        body, name="fox_prep", grid=(t_tok // seq,),
        in_specs=[pl.BlockSpec((N_HEADS, seq), lambda b: (0, b)), pl.BlockSpec((N_HEADS, 1), lambda b: (0, 0))],
        out_specs=[pl.BlockSpec((N_HEADS, seq), lambda b: (0, b)), pl.BlockSpec((seq, 128), lambda b: (b, 0))],
        out_shape=[jax.ShapeDtypeStruct((N_HEADS, t_tok), F32), jax.ShapeDtypeStruct((t_tok, 128), F32)],
        compiler_params=_cp(("arbitrary",), VMEM_MID))(fl_t, b_f)


def _fox_post(df_row, fl_t, b_f, seq):
    t_tok = fl_t.shape[1]
    ch = 256

    def body(d_ref, fl_ref, bf_ref, o_ref, db_ref):
        @pl.when(pl.program_id(0) == 0)
        def _():
            db_ref[...] = jnp.zeros_like(db_ref)

        z = fl_ref[...] + bf_ref[...]
        sig_neg = 1.0 / (1.0 + jnp.exp(z))
        ri = lax.broadcasted_iota(jnp.int32, (ch, ch), 0)
        ci = lax.broadcasted_iota(jnp.int32, (ch, ch), 1)
        lower = (ri >= ci).astype(F32)
        carry = jnp.zeros((N_HEADS, 1), F32)
        tot = jnp.zeros((N_HEADS, 1), F32)
        for k in reversed(range(seq // ch)):
            dk = _nn(d_ref[:, k * ch:(k + 1) * ch], lower, HI) + carry
            carry = dk[:, 0:1]
            dfl = dk * sig_neg[:, k * ch:(k + 1) * ch]
            o_ref[:, k * ch:(k + 1) * ch] = dfl
            tot = tot + jnp.sum(dfl, axis=1, keepdims=True)
        db_ref[...] += jnp.broadcast_to(tot, db_ref.shape)

    return _pallas(
        body, name="fox_post", grid=(t_tok // seq,),
        in_specs=[pl.BlockSpec((N_HEADS, seq), lambda b: (0, b)), pl.BlockSpec((N_HEADS, seq), lambda b: (0, b)),
                  pl.BlockSpec((N_HEADS, 1), lambda b: (0, 0))],
        out_specs=[pl.BlockSpec((N_HEADS, seq), lambda b: (0, b)), pl.BlockSpec((N_HEADS, 128), lambda b: (0, 0))],
        out_shape=[jax.ShapeDtypeStruct((N_HEADS, t_tok), F32), jax.ShapeDtypeStruct((N_HEADS, 128), F32)],
        compiler_params=_cp(("arbitrary",), VMEM_MID))(df_row, fl_t, b_f)


def _eye(n, dtype):
    return (lax.broadcasted_iota(jnp.int32, (n, n), 0) == lax.broadcasted_iota(jnp.int32, (n, n), 1)).astype(dtype)


def _fox_aug(qkvg, f_col, seq):
    t_tok = qkvg.shape[0]
    ta = 256
    nkb = ta // TK

    def body(q_ref, k_ref, v_ref, fc_ref, qa_ref, ka_ref, kt_ref, vt_ref, qt_ref):
        ri = lax.broadcasted_iota(jnp.int32, (128, 128), 0)
        ci = lax.broadcasted_iota(jnp.int32, (128, 128), 1)
        eye = (ri == ci).astype(BF16)
        lane = lax.broadcasted_iota(jnp.int32, (ta, 128), 1)
        ones_q = jnp.where(jnp.logical_and(lane >= 64, lane < 67), 1.0, 0.0)
        ones_k = jnp.where(jnp.logical_and(lane >= 67, lane < 70), 1.0, 0.0)
        fc_tile = fc_ref[...]
        for p in range(N_HEADS // 2):
            q2 = q_ref[:, 128 * p:128 * (p + 1)]
            k2 = k_ref[:, 128 * p:128 * (p + 1)]
            vt = _nt(eye, v_ref[:, 128 * p:128 * (p + 1)]).astype(BF16)
            for kk in range(nkb):
                vt_ref[p, kk] = vt[:, kk * TK:(kk + 1) * TK]
            for e in range(2):
                h = 2 * p + e
                sel = jnp.logical_and(ri == ci + HEAD_DIM * e, ci < HEAD_DIM)
                f = _col(fc_tile, h)
                fh = f.astype(BF16).astype(F32)
                fm = (f - fh).astype(BF16).astype(F32)
                fl = (f - fh - fm).astype(BF16).astype(F32)
                qa = (_nn(q2, jnp.where(sel, SCALE, 0.0).astype(BF16)) + ones_q + jnp.where(lane == 67, fh, 0.0)
                      + jnp.where(lane == 68, fm, 0.0) + jnp.where(lane == 69, fl, 0.0))
                ka = (_nn(k2, jnp.where(sel, 1.0, 0.0).astype(BF16)) + ones_k - jnp.where(lane == 64, fh, 0.0)
                      - jnp.where(lane == 65, fm, 0.0) - jnp.where(lane == 66, fl, 0.0))
                qab = qa.astype(BF16)
                qa_ref[h] = qab
                qt = _nt(eye, qab).astype(BF16)
                for kk in range(ta // TQ):
                    qt_ref[h, kk] = qt[:, kk * TQ:(kk + 1) * TQ]
                kab = ka.astype(BF16)
                ka_ref[h] = kab
                kt = _nt(eye, kab).astype(BF16)
                for kk in range(ta // TKB):
                    kt_ref[h, kk] = kt[:, kk * TKB:(kk + 1) * TKB]

    aug = jax.ShapeDtypeStruct((N_HEADS, t_tok, 128), BF16)
    return _pallas(
        body, name="fox_aug", grid=(t_tok // ta,),
        in_specs=[pl.BlockSpec((ta, 512), lambda i: (i, C_BQ // 512)), pl.BlockSpec((ta, 512), lambda i: (i, C_BK // 512)),
                  pl.BlockSpec((ta, 512), lambda i: (i, C_BV // 512)), pl.BlockSpec((ta, 128), lambda i: (i, 0))],
        out_specs=[pl.BlockSpec((N_HEADS, ta, 128), lambda i: (0, i, 0)), pl.BlockSpec((N_HEADS, ta, 128), lambda i: (0, i, 0)),
                   pl.BlockSpec((N_HEADS, ta // TKB, 128, TKB), lambda i: (0, i, 0, 0)),
                   pl.BlockSpec((N_HEADS // 2, nkb, 128, TK), lambda i: (0, i, 0, 0)),
                   pl.BlockSpec((N_HEADS, ta // TQ, 128, TQ), lambda i: (0, i, 0, 0))],
        out_shape=[aug, aug, jax.ShapeDtypeStruct((N_HEADS, t_tok // TKB, 128, TKB), BF16),
                   jax.ShapeDtypeStruct((N_HEADS // 2, t_tok // TK, 128, TK), BF16),
                   jax.ShapeDtypeStruct((N_HEADS, t_tok // TQ, 128, TQ), BF16)],
        compiler_params=_cp(("arbitrary",), VMEM_MID))(qkvg, qkvg, qkvg, f_col)


def _fox_fwd_t(q_aug, k_aug, vt, seq):
    t_tok = q_aug.shape[1]
    nq = seq // TQ
    ratio = TQ // TK

    def body(qa_ref, ka_ref, vt_ref, o_ref, lse_ref, ml_s, acc_s, st_s, p_s, al_s):
        i = pl.program_id(1)
        tpos = i * TQ + lax.broadcasted_iota(jnp.int32, (1, TQ), 1)
        eye = _eye(HEAD_DIM, BF16)
        for h in range(N_HEADS):
            ml_s[0, h] = jnp.full((1, TQ), NEG, F32)
            ml_s[1, h] = jnp.zeros((1, TQ), F32)
            acc_s[h] = jnp.zeros((HEAD_DIM, TQ), F32)
            p_s[1, h] = jnp.zeros((TK, TQ), BF16)
            al_s[1, h] = jnp.ones((1, TQ), F32)

        def scores(j):
            row0 = pl.multiple_of(j * TK, TK)
            for h in range(N_HEADS):
                st_s[j & 1, h] = _nt(ka_ref[h, pl.ds(row0, TK), :], qa_ref[h])

        def softmax(j, masked):
            slot = j & 1
            if masked:
                keep = (j * TK + lax.broadcasted_iota(jnp.int32, (TK, 1), 0)) <= tpos
            for h in range(N_HEADS):
                st = st_s[slot, h]
                if masked:
                    st = jnp.where(keep, st, NEG)
                m = ml_s[0, h]
                m_new = jnp.maximum(m, jnp.max(st, axis=0, keepdims=True))
                alpha = jnp.exp(m - m_new)
                pe = jnp.exp(st - m_new)
                ml_s[0, h] = m_new
                ml_s[1, h] = alpha * ml_s[1, h] + jnp.sum(pe, axis=0, keepdims=True)
                al_s[slot, h] = alpha
                p_s[slot, h] = pe.astype(BF16)

        def values(j):
            slot = j & 1
            jv = jnp.maximum(j, 0)
            for h in range(N_HEADS):
                p, e = divmod(h, 2)
                acc_s[h] = al_s[slot, h] * acc_s[h] + _nn(vt_ref[p, jv, e * HEAD_DIM:(e + 1) * HEAD_DIM, :], p_s[slot, h])

        def step(j, carry):
            values(j - 1)
            softmax(j, False)
            scores(j + 1)
            return carry

        last = ratio * i + ratio - 1
        scores(0)
        lax.fori_loop(0, ratio * i, step, 0)
        for kk in range(ratio):
            j = ratio * i + kk
            values(j - 1)
            softmax(j, True)
            if kk < ratio - 1:
                scores(j + 1)
        values(last)
        for p in range(N_HEADS // 2):
            outs = []
            for e in range(2):
                h = 2 * p + e
                l = ml_s[1, h]
                outs.append(_tn((acc_s[h] / l).astype(BF16), eye))
                lse_ref[p, e:e + 1, :] = ml_s[0, h] + jnp.log(l)
            o_ref[:, 128 * p:128 * (p + 1)] = jnp.concatenate(outs, axis=1).astype(BF16)

    return _pallas(
        body, name="fox_fwd", grid=(t_tok // seq, nq),
        in_specs=[pl.BlockSpec((N_HEADS, TQ, 128), lambda b, i: (0, b * nq + i, 0)),
                  pl.BlockSpec((N_HEADS, seq, 128), lambda b, i: (0, b, 0)),
                  pl.BlockSpec((N_HEADS // 2, seq // TK, 128, TK), lambda b, i: (0, b, 0, 0))],
        out_specs=[pl.BlockSpec((TQ, 512), lambda b, i: (b * nq + i, 0)),
                   pl.BlockSpec((N_HEADS // 2, 2, TQ), lambda b, i: (0, 0, b * nq + i))],
        out_shape=[jax.ShapeDtypeStruct((t_tok, 512), BF16), jax.ShapeDtypeStruct((N_HEADS // 2, 2, t_tok), F32)],
        scratch_shapes=[pltpu.VMEM((2, N_HEADS, 1, TQ), F32), pltpu.VMEM((N_HEADS, HEAD_DIM, TQ), F32),
                        pltpu.VMEM((2, N_HEADS, TK, TQ), F32), pltpu.VMEM((2, N_HEADS, TK, TQ), BF16),
                        pltpu.VMEM((2, N_HEADS, 1, TQ), F32)],
        compiler_params=_cp(("arbitrary", "arbitrary"), VMEM_MID))(q_aug, k_aug, vt)


def _fox_bwd_t(q_aug, k_aug, kt, qt, qkvg, du_b, b_out, lse, seq):
    TK = TKB
    t_tok = qkvg.shape[0]
    nq = seq // TQ
    nkb = seq // TK
    ratio = TQ // TK
    hg = 4

    def body(qa_ref, ka_ref, kt_ref, qt_ref, v_ref, do_ref, o_ref, lse_ref, dq_ref, dk_ref, dv_ref, df_ref,
             dqt_s, row_s, dfk_s, dk_s, dv_s, dot_s, st_s, dp_s, pb_s, db_s):
        eye = _eye(HEAD_DIM, BF16)
        eye2 = _eye(128, BF16)
        sub_k = lax.broadcasted_iota(jnp.int32, (128, TK), 0)
        lane8 = lax.broadcasted_iota(jnp.int32, (8, 128), 1)
        lane_k = lax.broadcasted_iota(jnp.int32, (TK, 128), 1)
        first = [lane8 < HEAD_DIM, lane8 >= HEAD_DIM]
        for pp in range(hg // 2):
            for ii in range(nq):
                dot_s[pp, ii] = _nt(eye2, do_ref[ii * TQ:(ii + 1) * TQ, 128 * pp:128 * (pp + 1)]).astype(BF16)
        for hh in range(hg):
            pp, e = divmod(hh, 2)
            head_lanes = jnp.where(first[e], 1.0, 0.0)
            for ii in range(nq):
                rows = slice(ii * TQ, (ii + 1) * TQ)
                prod = do_ref[rows, 128 * pp:128 * (pp + 1)].astype(F32) * o_ref[rows, 128 * pp:128 * (pp + 1)].astype(F32)
                row_s[hh, ii, 0] = _nt(head_lanes, prod, HI)
                row_s[hh, ii, 1] = jnp.broadcast_to(lse_ref[pp, e:e + 1, ii * TQ:(ii + 1) * TQ], (8, TQ))
                dqt_s[hh, ii] = jnp.zeros((128, TQ), F32)

        def kblock(j, _):
            krow = pl.multiple_of(j * TK, TK)
            spos = j * TK + lax.broadcasted_iota(jnp.int32, (TK, 1), 0)
            for hh in range(hg):
                dk_s[hh] = jnp.zeros((128, TK), F32)
                dv_s[hh] = jnp.zeros((128, TK), F32)

            def scores(i):
                qrow = pl.multiple_of(i * TQ, TQ)
                for hh in range(hg):
                    pp, e = divmod(hh, 2)
                    own = (lane_k < HEAD_DIM) if e == 0 else (lane_k >= HEAD_DIM)
                    v2 = v_ref[pl.ds(krow, TK), 128 * pp:128 * (pp + 1)]
                    vj = jnp.where(own, v2, jnp.zeros_like(v2))
                    st_s[i & 1, hh] = _nt(ka_ref[hh, pl.ds(krow, TK), :], qa_ref[hh, pl.ds(qrow, TQ), :])
                    dp_s[i & 1, hh] = _nt(vj, do_ref[pl.ds(qrow, TQ), 128 * pp:128 * (pp + 1)])

            def elementwise(i, masked):
                slot = i & 1
                if masked:
                    keep = spos <= (i * TQ + lax.broadcasted_iota(jnp.int32, (1, TQ), 1))
                for hh in range(hg):
                    pt = jnp.exp(st_s[slot, hh] - row_s[hh, i, 1][0:1, :])
                    if masked:
                        pt = jnp.where(keep, pt, 0.0)
                    dst = pt * (dp_s[slot, hh] - row_s[hh, i, 0][0:1, :])
                    pb_s[slot, hh] = pt.astype(BF16)
                    db_s[slot, hh] = dst.astype(BF16)

            def grads(i):
                slot = i & 1
                for hh in range(hg):
                    dst_b = db_s[slot, hh]
                    dv_s[hh] += _nt(dot_s[hh // 2, i], pb_s[slot, hh])
                    dk_s[hh] += _nt(qt_ref[hh, i], dst_b)
                    dqt_s[hh, i] += _nn(kt_ref[hh, j], dst_b)

            def step(i, carry):
                grads(i - 1)
                elementwise(i, False)
                scores(jnp.minimum(i + 1, nq - 1))
                return carry

            i0 = j // ratio
            scores(i0)
            elementwise(i0, True)
            scores(jnp.minimum(i0 + 1, nq - 1))
            lax.fori_loop(i0 + 1, nq, step, 0)
            grads(nq - 1)
            for pp in range(hg // 2):
                cols = slice(128 * pp, 128 * (pp + 1))
                dk_t = jnp.concatenate([dk_s[2 * pp][0:HEAD_DIM, :], dk_s[2 * pp + 1][0:HEAD_DIM, :]], axis=0)
                dv_t = jnp.where(sub_k < HEAD_DIM, dv_s[2 * pp], dv_s[2 * pp + 1])
                dk_ref[pl.ds(krow, TK), cols] = _tn(dk_t.astype(BF16), eye2).astype(BF16)
                dv_ref[pl.ds(krow, TK), cols] = _tn(dv_t.astype(BF16), eye2).astype(BF16)
            for hh in range(hg):
                dfk_s[hh, j] = dk_s[hh][HEAD_DIM:HEAD_DIM + 8, :]
            return 0

        lax.fori_loop(0, nkb, kblock, 0)
        for pp in range(hg // 2):
            for ii in range(nq):
                parts = []
                for e in range(2):
                    dqt = dqt_s[2 * pp + e, ii]
                    parts.append(_tn(dqt[0:HEAD_DIM, :].astype(BF16), eye) * SCALE)
                    for kk in range(ratio):
                        jj = ii * ratio + kk
                        df_ref[pp, e:e + 1, jj * TK:(jj + 1) * TK] = (dqt[67:68, kk * TK:(kk + 1) * TK]
                                                                     - dfk_s[2 * pp + e, jj][0:1, :])
                dq_ref[ii * TQ:(ii + 1) * TQ, 128 * pp:128 * (pp + 1)] = jnp.concatenate(parts, axis=1).astype(BF16)

    aug_blk = pl.BlockSpec((hg, seq, 128), lambda b, g: (g, b, 0))
    pair_blk = pl.BlockSpec((seq, 64 * hg), lambda b, g: (b, g))
    row_blk = pl.BlockSpec((hg // 2, 2, seq), lambda b, g: (g, 0, b))
    return _pallas(
        body, name="fox_bwd", grid=(t_tok // seq, N_HEADS // hg),
        in_specs=[aug_blk, aug_blk, pl.BlockSpec((hg, nkb, 128, TK), lambda b, g: (g, b, 0, 0)),
                  pl.BlockSpec((hg, nq, 128, TQ), lambda b, g: (g, b, 0, 0)),
                  pl.BlockSpec((seq, 64 * hg), lambda b, g: (b, C_BV // (64 * hg) + g)), pair_blk, pair_blk, row_blk],
        out_specs=[pair_blk, pair_blk, pair_blk, row_blk],
        out_shape=[jax.ShapeDtypeStruct((t_tok, 512), BF16)] * 3
        + [jax.ShapeDtypeStruct((N_HEADS // 2, 2, t_tok), F32)],
        scratch_shapes=[pltpu.VMEM((hg, nq, 128, TQ), F32), pltpu.VMEM((hg, nq, 2, 8, TQ), F32),
                        pltpu.VMEM((hg, nkb, 8, TK), F32), pltpu.VMEM((hg, 128, TK), F32),
                        pltpu.VMEM((hg, 128, TK), F32), pltpu.VMEM((hg // 2, nq, 128, TQ), BF16),
                        pltpu.VMEM((2, hg, TK, TQ), F32), pltpu.VMEM((2, hg, TK, TQ), F32),
                        pltpu.VMEM((2, hg, TK, TQ), BF16), pltpu.VMEM((2, hg, TK, TQ), BF16)],
        compiler_params=_cp(("arbitrary", "arbitrary"), VMEM_BIG))(q_aug, k_aug, kt, qt, qkvg, du_b, b_out, lse)


def _fox_bwd_t_old(q_aug, k_aug, kt, qkvg, du_b, b_out, lse, seq):
    t_tok = qkvg.shape[0]
    nq = seq // TQ
    nkb = seq // TK
    ratio = TQ // TK

    def body(qa_ref, ka_ref, kt_ref, v_ref, do_ref, o_ref, lse_ref, dq_ref, dk_ref, dv_ref, df_ref,
             dqt_s, out_s, row_s, dfk_s):
        ones_b = jnp.ones((8, TQ), BF16)
        ones_f = jnp.ones((8, HEAD_DIM), F32)
        eye = _eye(HEAD_DIM, BF16)
        for e in range(2):
            lo, hi = e * HEAD_DIM, (e + 1) * HEAD_DIM
            for ii in range(nq):
                rows = slice(ii * TQ, (ii + 1) * TQ)
                do = do_ref[rows, :][:, lo:hi].astype(F32)
                ov = o_ref[rows, :][:, lo:hi].astype(F32)
                row_s[ii, 0] = _nt(ones_f, do * ov, HI)
                row_s[ii, 1] = jnp.broadcast_to(lse_ref[e:e + 1, ii * TQ:(ii + 1) * TQ], (8, TQ))
                dqt_s[ii] = jnp.zeros((128, TQ), F32)

            def kblock(j, _):
                krow = pl.multiple_of(j * TK, TK)
                kj = ka_ref[e, pl.ds(krow, TK), :]
                ktj = kt_ref[e, j]
                vj = v_ref[pl.ds(krow, TK), :][:, lo:hi]
                spos = j * TK + lax.broadcasted_iota(jnp.int32, (TK, 1), 0)

                def qblock(i, carry, masked):
                    dk_acc, dv_acc, dfk = carry
                    qrow = pl.multiple_of(i * TQ, TQ)
                    qa = qa_ref[e, pl.ds(qrow, TQ), :]
                    doh = do_ref[pl.ds(qrow, TQ), :][:, lo:hi]
                    pt = jnp.exp(_nt(kj, qa) - row_s[i, 1][0:1, :])
                    if masked:
                        tpos = i * TQ + lax.broadcasted_iota(jnp.int32, (1, TQ), 1)
                        pt = jnp.where(spos <= tpos, pt, 0.0)
                    dst = pt * (_nt(vj, doh) - row_s[i, 0][0:1, :])
                    dst_b = dst.astype(BF16)
                    dv_acc = dv_acc + _nn(pt.astype(BF16), doh)
                    dk_acc = dk_acc + _nn(dst_b, qa)
                    dqt_s[i] += _nn(ktj, dst_b)
                    dfk = dfk + _nt(ones_b, dst_b)
                    return dk_acc, dv_acc, dfk

                i0 = j // ratio
                carry = (jnp.zeros((TK, 128), F32), jnp.zeros((TK, HEAD_DIM), F32), jnp.zeros((8, TK), F32))
                carry = qblock(i0, carry, True)
                dk_acc, dv_acc, dfk = lax.fori_loop(i0 + 1, nq, functools.partial(qblock, masked=False), carry)
                out_s[1, e, pl.ds(krow, TK), :] = dk_acc[:, :HEAD_DIM]
                out_s[2, e, pl.ds(krow, TK), :] = dv_acc
                dfk_s[j] = dfk
                return 0

            lax.fori_loop(0, nkb, kblock, 0)
            for ii in range(nq):
                dqt = dqt_s[ii]
                out_s[0, e, ii * TQ:(ii + 1) * TQ, :] = _tn(dqt[0:HEAD_DIM, :].astype(BF16), eye) * SCALE
                for kk in range(ratio):
                    jj = ii * ratio + kk
                    df_ref[e:e + 1, jj * TK:(jj + 1) * TK] = dqt[67:68, kk * TK:(kk + 1) * TK] - dfk_s[jj][0:1, :]
        for k, ref in enumerate((dq_ref, dk_ref, dv_ref)):
            ref[...] = jnp.concatenate([out_s[k, 0], out_s[k, 1]], axis=1).astype(BF16)

    aug_blk = pl.BlockSpec((2, seq, 128), lambda b, p: (p, b, 0))
    pair_blk = pl.BlockSpec((seq, 128), lambda b, p: (b, p))
    row_blk = pl.BlockSpec((None, 2, seq), lambda b, p: (p, 0, b))
    return _pallas(
        body, name="fox_bwd", grid=(t_tok // seq, N_HEADS // 2),
        in_specs=[aug_blk, aug_blk, pl.BlockSpec((2, nkb, 128, TK), lambda b, p: (p, b, 0, 0)),
                  pl.BlockSpec((seq, 128), lambda b, p: (b, C_BV // 128 + p)), pair_blk, pair_blk, row_blk],
        out_specs=[pair_blk, pair_blk, pair_blk, row_blk],
        out_shape=[jax.ShapeDtypeStruct((t_tok, 512), BF16)] * 3
        + [jax.ShapeDtypeStruct((N_HEADS // 2, 2, t_tok), F32)],
        scratch_shapes=[pltpu.VMEM((nq, 128, TQ), F32), pltpu.VMEM((3, 2, seq, HEAD_DIM), F32),
                        pltpu.VMEM((nq, 2, 8, TQ), F32), pltpu.VMEM((nkb, 8, TK), F32)],
        compiler_params=_cp(("arbitrary", "arbitrary"), VMEM_BIG))(q_aug, k_aug, kt, qkvg, du_b, b_out, lse)


def _fox_fwd(qkvg, f_row, f_col, seq):
    t_tok = qkvg.shape[0]
    nq = seq // TQ

    def body(q_ref, k_ref, v_ref, fr_ref, fc_ref, o_ref, lse_ref, fk_s):
        i = pl.program_id(1)
        for jj in range(nq):
            fk_s[jj] = fr_ref[:, jj * TQ:(jj + 1) * TQ]
        fcol = fc_ref[...]
        tpos = i * TQ + lax.broadcasted_iota(jnp.int32, (TQ, 1), 0)
        lane = lax.broadcasted_iota(jnp.int32, (TQ, 128), 1)
        lse_tile = jnp.zeros((TQ, 128), F32)
        for p in range(N_HEADS // 2):
            q2 = q_ref[:, 128 * p:128 * (p + 1)]
            qs = [q2[:, :HEAD_DIM], q2[:, HEAD_DIM:]]
            fqs = [_col(fcol, 2 * p + e) for e in range(2)]

            def kblock(j, carry):
                row0 = pl.multiple_of(j * TQ, TQ)
                k2 = k_ref[pl.ds(row0, TQ), 128 * p:128 * (p + 1)]
                v2 = v_ref[pl.ds(row0, TQ), 128 * p:128 * (p + 1)]
                fk8 = fk_s[j]
                spos = j * TQ + lax.broadcasted_iota(jnp.int32, (1, TQ), 1)
                keep = spos <= tpos
                new = []
                for e in range(2):
                    m, l, acc = carry[3 * e:3 * e + 3]
                    kh = k2[:, e * HEAD_DIM:(e + 1) * HEAD_DIM]
                    vh = v2[:, e * HEAD_DIM:(e + 1) * HEAD_DIM]
                    s = _nt(qs[e], kh) * SCALE + (fqs[e] - fk8[2 * p + e:2 * p + e + 1, :])
                    s = jnp.where(keep, s, NEG)
                    m_new = jnp.maximum(m, jnp.max(s, axis=1, keepdims=True))
                    alpha = jnp.exp(m - m_new)
                    pe = jnp.exp(s - m_new)
                    l = alpha * l + jnp.sum(pe, axis=1, keepdims=True)
                    acc = alpha * acc + _nn(pe.astype(BF16), vh)
                    new += [m_new, l, acc]
                return tuple(new)

            init = (jnp.full((TQ, 1), NEG, F32), jnp.zeros((TQ, 1), F32), jnp.zeros((TQ, HEAD_DIM), F32)) * 2
            res = lax.fori_loop(0, i + 1, kblock, init)
            outs = []
            for e in range(2):
                m, l, acc = res[3 * e:3 * e + 3]
                outs.append(acc / l)
                lse_tile = jnp.where(lane == 2 * p + e, m + jnp.log(l), lse_tile)
            o_ref[:, 128 * p:128 * (p + 1)] = jnp.concatenate(outs, axis=1).astype(BF16)
        lse_ref[...] = lse_tile

    return _pallas(
        body, name="fox_fwd", grid=(t_tok // seq, nq),
        in_specs=[pl.BlockSpec((TQ, 512), lambda b, i: (b * nq + i, C_BQ // 512)),
                  pl.BlockSpec((seq, 512), lambda b, i: (b, C_BK // 512)),
                  pl.BlockSpec((seq, 512), lambda b, i: (b, C_BV // 512)),
                  pl.BlockSpec((N_HEADS, seq), lambda b, i: (0, b)),
                  pl.BlockSpec((TQ, 128), lambda b, i: (b * nq + i, 0))],
        out_specs=[pl.BlockSpec((TQ, 512), lambda b, i: (b * nq + i, 0)),
                   pl.BlockSpec((TQ, 128), lambda b, i: (b * nq + i, 0))],
        out_shape=[jax.ShapeDtypeStruct((t_tok, 512), BF16), jax.ShapeDtypeStruct((t_tok, 128), F32)],
        scratch_shapes=[pltpu.VMEM((nq, N_HEADS, TQ), F32)],
        compiler_params=_cp(("arbitrary", "arbitrary"), VMEM_MID))(qkvg, qkvg, qkvg, f_row, f_col)


def _fox_bwd(qkvg, du_b, b_out, lse, f_row, f_col, seq):
    t_tok = qkvg.shape[0]
    nq = seq // TQ

    def body(q_ref, k_ref, v_ref, do_ref, o_ref, lse_ref, fr_ref, fc_ref,
             dq_ref, dk_ref, dv_ref, df_ref, dq_s, dk_s, dv_s, col_s, df_s, fk_s):
        p = pl.program_id(1)
        for jj in range(nq):
            fk_s[jj] = fr_ref[:, jj * TQ:(jj + 1) * TQ]
        eye = (lax.broadcasted_iota(jnp.int32, (TQ, TQ), 0) == lax.broadcasted_iota(jnp.int32, (TQ, TQ), 1)).astype(F32)
        for e in range(2):
            h = 2 * p + e
            lo, hi = e * HEAD_DIM, (e + 1) * HEAD_DIM
            for ii in range(nq):
                rows = slice(ii * TQ, (ii + 1) * TQ)
                do = do_ref[rows, :][:, lo:hi].astype(F32)
                ov = o_ref[rows, :][:, lo:hi].astype(F32)
                col_s[0, rows, :] = jnp.sum(do * ov, axis=1, keepdims=True)
                col_s[1, rows, :] = _col(lse_ref[rows, :], h)
                col_s[2, rows, :] = _col(fc_ref[rows, :], h)
                dq_s[rows, :] = jnp.zeros((TQ, HEAD_DIM), F32)
                df_s[ii] = jnp.zeros((8, TQ), F32)
                col_s[3, rows, :] = jnp.zeros((TQ, 1), F32)

            def kblock(j, _):
                krow = pl.multiple_of(j * TQ, TQ)
                kh = k_ref[pl.ds(krow, TQ), :][:, lo:hi]
                vh = v_ref[pl.ds(krow, TQ), :][:, lo:hi]
                fk = _row(fk_s[j], h)
                spos = j * TQ + lax.broadcasted_iota(jnp.int32, (1, TQ), 1)

                def qblock(i, carry):
                    dk_acc, dv_acc, dfk = carry
                    qrow = pl.multiple_of(i * TQ, TQ)
                    qh = q_ref[pl.ds(qrow, TQ), :][:, lo:hi]
                    doh = do_ref[pl.ds(qrow, TQ), :][:, lo:hi]
                    delta = col_s[0, pl.ds(qrow, TQ), :]
                    lse_q = col_s[1, pl.ds(qrow, TQ), :]
                    fq = col_s[2, pl.ds(qrow, TQ), :]
                    tpos = i * TQ + lax.broadcasted_iota(jnp.int32, (TQ, 1), 0)
                    s = _nt(qh, kh) * SCALE + (fq - fk)
                    pr = jnp.where(spos <= tpos, jnp.exp(s - lse_q), 0.0)
                    dp = _nt(doh, vh)
                    ds = pr * (dp - delta)
                    ds_b = ds.astype(BF16)
                    dv_acc = dv_acc + _tn(pr.astype(BF16), doh)
                    dk_acc = dk_acc + _tn(ds_b, qh)
                    dq_s[pl.ds(qrow, TQ), :] += _nn(ds_b, kh)
                    col_s[3, pl.ds(qrow, TQ), :] += jnp.sum(ds, axis=1, keepdims=True)
                    dfk = dfk + jnp.sum(ds, axis=0, keepdims=True)
                    return dk_acc, dv_acc, dfk

                zero = jnp.zeros((TQ, HEAD_DIM), F32)
                dk_acc, dv_acc, dfk = lax.fori_loop(j, nq, qblock, (zero, zero, jnp.zeros((1, TQ), F32)))
                dk_s[e, pl.ds(krow, TQ), :] = dk_acc * SCALE
                dv_s[e, pl.ds(krow, TQ), :] = dv_acc
                df_s[j] -= jnp.broadcast_to(dfk, (8, TQ))
                return 0

            lax.fori_loop(0, nq, kblock, 0)
            dq_s2 = dq_s[...] * SCALE
            dk_s[2 + e] = dq_s2
            for ii in range(nq):
                dfq = jnp.broadcast_to(col_s[3, ii * TQ:(ii + 1) * TQ, :], (TQ, 128))
                df_ref[e:e + 1, ii * TQ:(ii + 1) * TQ] = _tn(dfq, eye, HI)[0:1, :] + df_s[ii][0:1, :]
        dq_ref[...] = jnp.concatenate([dk_s[2], dk_s[3]], axis=1).astype(BF16)
        dk_ref[...] = jnp.concatenate([dk_s[0], dk_s[1]], axis=1).astype(BF16)
        dv_ref[...] = jnp.concatenate([dv_s[0], dv_s[1]], axis=1).astype(BF16)

    blk = lambda off: pl.BlockSpec((seq, 128), lambda b, p: (b, off // 128 + p))
    out_blk = pl.BlockSpec((seq, 128), lambda b, p: (b, p))
    return _pallas(
        body, name="fox_bwd", grid=(t_tok // seq, N_HEADS // 2),
        in_specs=[blk(C_BQ), blk(C_BK), blk(C_BV), out_blk, out_blk,
                  pl.BlockSpec((seq, 128), lambda b, p: (b, 0)),
                  pl.BlockSpec((N_HEADS, seq), lambda b, p: (0, b)),
                  pl.BlockSpec((seq, 128), lambda b, p: (b, 0))],
        out_specs=[out_blk, out_blk, out_blk, pl.BlockSpec((None, 2, seq), lambda b, p: (p, 0, b))],
        out_shape=[jax.ShapeDtypeStruct((t_tok, 512), BF16)] * 3
        + [jax.ShapeDtypeStruct((N_HEADS // 2, 2, t_tok), F32)],
        scratch_shapes=[pltpu.VMEM((seq, HEAD_DIM), F32), pltpu.VMEM((4, seq, HEAD_DIM), F32),
                        pltpu.VMEM((2, seq, HEAD_DIM), F32), pltpu.VMEM((4, seq, 1), F32),
                        pltpu.VMEM((nq, 8, TQ), F32), pltpu.VMEM((nq, N_HEADS, TQ), F32)],
        compiler_params=_cp(("arbitrary", "arbitrary"), VMEM_BIG))(qkvg, qkvg, qkvg, du_b, b_out, lse, f_row, f_col)


def _swa_window(k_ref, v_ref, n):
    prev = pl.multiple_of(jnp.maximum(n - 1, 0) * BLOCK, BLOCK)
    cur = pl.multiple_of(n * BLOCK, BLOCK)
    kwin = jnp.concatenate([k_ref[pl.ds(prev, BLOCK), :], k_ref[pl.ds(cur, BLOCK), :]], axis=0)
    vwin = jnp.concatenate([v_ref[pl.ds(prev, BLOCK), :], v_ref[pl.ds(cur, BLOCK), :]], axis=0)
    ti = lax.broadcasted_iota(jnp.int32, (BLOCK, 2 * BLOCK), 0)
    sj = lax.broadcasted_iota(jnp.int32, (BLOCK, 2 * BLOCK), 1)
    rel = ti - sj + BLOCK
    first_key = jnp.where(n > 0, 0, BLOCK)
    mask = jnp.logical_and(jnp.logical_and(rel >= 0, rel < BLOCK), sj >= first_key)
    return kwin, vwin, mask, prev, cur


def _head_cols(ref, h):
    pair = ref[:, 128 * (h // 2):128 * (h // 2 + 1)]
    return pair[:, (h % 2) * HEAD_DIM:(h % 2 + 1) * HEAD_DIM]


def _swa_logits(q_ref, kwin, bias_ref, h, mask):
    hk = h // KV_GROUP
    s = _nt(_head_cols(q_ref, h), kwin[:, hk * HEAD_DIM:(hk + 1) * HEAD_DIM]) * SCALE + bias_ref[h]
    return jnp.where(mask, s, NEG)


def _swa_fwd(qkvg, bias, sinks, seq):
    t_tok = qkvg.shape[0]
    nb = seq // BLOCK

    def body(sink_ref, q_ref, k_ref, v_ref, bias_ref, o_ref, lse_ref, s_s, p_s, den_s):
        g = pl.program_id(1)
        subs = [pl.ds(s * BLOCK, BLOCK) for s in range(SWA_SUB)]
        wins = [_swa_window(k_ref, v_ref, SWA_SUB * g + s) for s in range(SWA_SUB)]
        for s in range(SWA_SUB):
            for h in range(N_HEADS):
                s_s[s * N_HEADS + h] = _swa_logits(q_ref.at[subs[s]], wins[s][0], bias_ref, h, wins[s][2])
        lane = lax.broadcasted_iota(jnp.int32, (BLOCK, 128), 1)
        for s in range(SWA_SUB):
            lse_tile = jnp.zeros((BLOCK, 128), F32)
            for h in range(N_HEADS):
                sc = s_s[s * N_HEADS + h]
                sink = sink_ref[h]
                m = jnp.maximum(jnp.max(sc, axis=1, keepdims=True), sink)
                pe = jnp.exp(sc - m)
                den = jnp.sum(pe, axis=1, keepdims=True) + jnp.exp(sink - m)
                p_s[s * N_HEADS + h] = pe.astype(BF16)
                den_s[s * N_HEADS + h] = den
                lse_tile = jnp.where(lane == h, m + jnp.log(den), lse_tile)
            lse_ref[subs[s], :] = lse_tile
        for s in range(SWA_SUB):
            vwin = wins[s][1]
            for pr in range(N_HEADS // 2):
                outs = []
                for h in (2 * pr, 2 * pr + 1):
                    hk = h // KV_GROUP
                    outs.append(_nn(p_s[s * N_HEADS + h], vwin[:, hk * HEAD_DIM:(hk + 1) * HEAD_DIM]) / den_s[s * N_HEADS + h])
                o_ref[subs[s], 128 * pr:128 * (pr + 1)] = jnp.concatenate(outs, axis=1).astype(BF16)

    rows = SWA_SUB * BLOCK
    steps = nb // SWA_SUB
    return _pallas(
        body, name="swa_fwd", grid=(t_tok // seq, steps),
        in_specs=[pl.BlockSpec(memory_space=pltpu.SMEM),
                  pl.BlockSpec((rows, 512), lambda b, n: (b * steps + n, C_AQ // 512)),
                  pl.BlockSpec((seq, 128), lambda b, n: (b, C_AK // 128)),
                  pl.BlockSpec((seq, 128), lambda b, n: (b, C_AV // 128)),
                  pl.BlockSpec((N_HEADS, BLOCK, 2 * BLOCK), lambda b, n: (0, 0, 0))],
        out_specs=[pl.BlockSpec((rows, 512), lambda b, n: (b * steps + n, 0)),
                   pl.BlockSpec((rows, 128), lambda b, n: (b * steps + n, 0))],
        out_shape=[jax.ShapeDtypeStruct((t_tok, 512), BF16), jax.ShapeDtypeStruct((t_tok, 128), F32)],
        scratch_shapes=[pltpu.VMEM((SWA_SUB * N_HEADS, BLOCK, 2 * BLOCK), F32),
                        pltpu.VMEM((SWA_SUB * N_HEADS, BLOCK, 2 * BLOCK), BF16),
                        pltpu.VMEM((SWA_SUB * N_HEADS, BLOCK, 1), F32)],
        compiler_params=_cp(("arbitrary", "arbitrary"), VMEM_MID))(sinks, qkvg, qkvg, qkvg, bias)


def _swa_bwd(qkvg, du_a, a_out, lse, bias, sinks, seq):
    t_tok = qkvg.shape[0]
    nb = seq // BLOCK

    def body(sink_ref, q_ref, k_ref, v_ref, do_ref, o_ref, lse_ref, bias_ref,
             dq_ref, dkv_ref, dbias_ref, dsink_ref, kv_s, s_s, dp_s, pb_s, db_s):
        b, n = pl.program_id(0), pl.program_id(1)

        @pl.when(jnp.logical_and(b == 0, n == 0))
        def _():
            dbias_ref[...] = jnp.zeros_like(dbias_ref)
            dsink_ref[...] = jnp.zeros_like(dsink_ref)

        @pl.when(n == 0)
        def _():
            kv_s[...] = jnp.zeros_like(kv_s)

        subs = [pl.ds(s * BLOCK, BLOCK) for s in range(SWA_SUB)]
        wins = [_swa_window(k_ref, v_ref, SWA_SUB * n + s) for s in range(SWA_SUB)]
        for s in range(SWA_SUB):
            kwin, vwin, mask = wins[s][:3]
            for h in range(N_HEADS):
                hk = h // KV_GROUP
                s_s[s * N_HEADS + h] = _swa_logits(q_ref.at[subs[s]], kwin, bias_ref, h, mask)
                dp_s[s * N_HEADS + h] = _nt(_head_cols(do_ref.at[subs[s]], h), vwin[:, hk * HEAD_DIM:(hk + 1) * HEAD_DIM])
        for s in range(SWA_SUB):
            lse_tile = lse_ref[subs[s], :]
            do_s, o_s = do_ref.at[subs[s]], o_ref.at[subs[s]]
            for h in range(N_HEADS):
                delta = jnp.sum(_head_cols(do_s, h).astype(F32) * _head_cols(o_s, h).astype(F32), axis=1, keepdims=True)
                lse_h = _col(lse_tile, h)
                pe = jnp.exp(s_s[s * N_HEADS + h] - lse_h)
                ds = pe * (dp_s[s * N_HEADS + h] - delta)
                dbias_ref[h] += ds
                psink = jnp.exp(sink_ref[h] - lse_h)
                dsink_ref[h:h + 1, :] += jnp.broadcast_to(jnp.sum(-psink * delta, axis=0, keepdims=True), (1, 128))
                pb_s[s * N_HEADS + h] = pe.astype(BF16)
                db_s[s * N_HEADS + h] = ds.astype(BF16)
        for s in range(SWA_SUB):
            kwin, _, _, prev, cur = wins[s]
            q_s, do_s = q_ref.at[subs[s]], do_ref.at[subs[s]]
            for pr in range(N_HEADS // 2):
                dqs = []
                for h in (2 * pr, 2 * pr + 1):
                    hk = h // KV_GROUP
                    dqs.append(_nn(db_s[s * N_HEADS + h], kwin[:, hk * HEAD_DIM:(hk + 1) * HEAD_DIM]) * SCALE)
                dq_ref[subs[s], 128 * pr:128 * (pr + 1)] = jnp.concatenate(dqs, axis=1).astype(BF16)
            dks, dvs = [], []
            for hk in range(N_HEADS // KV_GROUP):
                dk = jnp.zeros((2 * BLOCK, HEAD_DIM), F32)
                dv = jnp.zeros((2 * BLOCK, HEAD_DIM), F32)
                for h in range(hk * KV_GROUP, (hk + 1) * KV_GROUP):
                    dk = dk + _tn(db_s[s * N_HEADS + h], _head_cols(q_s, h))
                    dv = dv + _tn(pb_s[s * N_HEADS + h], _head_cols(do_s, h))
                dks.append(dk * SCALE)
                dvs.append(dv)
            upd = jnp.concatenate(dks + dvs, axis=1)
            kv_s[pl.ds(prev, BLOCK), :] += upd[:BLOCK]
            kv_s[pl.ds(cur, BLOCK), :] += upd[BLOCK:]

        @pl.when(n == steps - 1)
        def _():
            dkv_ref[...] = kv_s[...].astype(BF16)

    rows = SWA_SUB * BLOCK
    steps = nb // SWA_SUB
    tile = (SWA_SUB * N_HEADS, BLOCK, 2 * BLOCK)
    return _pallas(
        body, name="swa_bwd", grid=(t_tok // seq, steps),
        in_specs=[pl.BlockSpec(memory_space=pltpu.SMEM),
                  pl.BlockSpec((rows, 512), lambda b, n: (b * steps + n, C_AQ // 512)),
                  pl.BlockSpec((seq, 128), lambda b, n: (b, C_AK // 128)),
                  pl.BlockSpec((seq, 128), lambda b, n: (b, C_AV // 128)),
                  pl.BlockSpec((rows, 512), lambda b, n: (b * steps + n, 0)),
                  pl.BlockSpec((rows, 512), lambda b, n: (b * steps + n, 0)),
                  pl.BlockSpec((rows, 128), lambda b, n: (b * steps + n, 0)),
                  pl.BlockSpec((N_HEADS, BLOCK, 2 * BLOCK), lambda b, n: (0, 0, 0))],
        out_specs=[pl.BlockSpec((rows, 512), lambda b, n: (b * steps + n, 0)),
                   pl.BlockSpec((seq, 256), lambda b, n: (b, 0)),
                   pl.BlockSpec((N_HEADS, BLOCK, 2 * BLOCK), lambda b, n: (0, 0, 0)),
                   pl.BlockSpec((N_HEADS, 128), lambda b, n: (0, 0))],
        out_shape=[jax.ShapeDtypeStruct((t_tok, 512), BF16), jax.ShapeDtypeStruct((t_tok, 256), BF16),
                   jax.ShapeDtypeStruct((N_HEADS, BLOCK, 2 * BLOCK), F32), jax.ShapeDtypeStruct((N_HEADS, 128), F32)],
        scratch_shapes=[pltpu.VMEM((seq, 256), F32), pltpu.VMEM(tile, F32), pltpu.VMEM(tile, F32),
                        pltpu.VMEM(tile, BF16), pltpu.VMEM(tile, BF16)],
        compiler_params=_cp(("arbitrary", "arbitrary"), VMEM_MID))(sinks, qkvg, qkvg, qkvg, du_a, a_out, lse, bias)


def _out_proj(name, u_parts, gate_arr, gate_blk, w_out, x, gmod, seq):
    t_tok = x.shape[0]
    nu = len(u_parts)

    def body(*refs):
        u_refs = refs[:nu]
        g_ref, w_ref, x_ref, gm_ref, yg_ref, y_ref, xn_ref = refs[nu:]
        u = jnp.concatenate([r[...].astype(F32) for r in u_refs], axis=1) if nu > 1 else u_refs[0][...].astype(F32)
        yg = (u * _silu(g_ref[...].astype(F32))).astype(BF16)
        yg_ref[...] = yg
        y = _nn(yg, w_ref[...])
        y_ref[...] = y.astype(BF16)
        xn_ref[...] = x_ref[...] + gm_ref[...] * y

    row = lambda w: pl.BlockSpec((TM, w), lambda i: (i, 0))
    in_specs = [row(u.shape[1]) for u in u_parts]
    in_specs += [pl.BlockSpec((TM, D_MODEL), lambda i: (i, gate_blk)),
                 pl.BlockSpec((D_MODEL, D_MODEL), lambda i: (0, 0)), row(D_MODEL),
                 pl.BlockSpec((None, 1, D_MODEL), lambda i: (i * TM // seq, 0, 0))]
    return _pallas(
        body, name=name, grid=(t_tok // TM,), in_specs=in_specs,
        out_specs=[row(D_MODEL)] * 3,
        out_shape=[jax.ShapeDtypeStruct((t_tok, D_MODEL), BF16)] * 2 + [jax.ShapeDtypeStruct((t_tok, D_MODEL), F32)],
        compiler_params=_cp(("arbitrary",), VMEM_MID))(*u_parts, gate_arr, w_out, x, gmod)


def _out_proj_bwd(name, dxn, gmod, y, w_out, seq, attn=None):
    t_tok = dxn.shape[0]
    tiles_per_seq = seq // TM

    def body(*refs):
        if attn is None:
            dxn_ref, gm_ref, y_ref, w_ref, dy_ref, dgm_ref, dyg_ref = refs
        else:
            dxn_ref, gm_ref, y_ref, w_ref, a_ref, b_ref, g_ref, dy_ref, dgm_ref, dua_ref, dub_ref, dg_ref = refs
        i = pl.program_id(0)
        dxv = dxn_ref[...]
        dy = (dxv * gm_ref[...]).astype(BF16)
        dy_ref[...] = dy

        @pl.when(i % tiles_per_seq == 0)
        def _():
            dgm_ref[...] = jnp.zeros_like(dgm_ref)

        dgm_ref[...] += jnp.sum(dxv * y_ref[...].astype(F32), axis=0, keepdims=True)
        dyg = _nt(dy, w_ref[...])
        if attn is None:
            dyg_ref[...] = dyg
        else:
            gt = g_ref[...].astype(F32)
            du = dyg * _silu(gt)
            dua_ref[...] = du[:, :512].astype(BF16)
            dub_ref[...] = du[:, 512:].astype(BF16)
            u = jnp.concatenate([a_ref[...].astype(F32), b_ref[...].astype(F32)], axis=1)
            dg_ref[...] = (dyg * u * _dsilu(gt)).astype(BF16)

    row = lambda w: pl.BlockSpec((TM, w), lambda i: (i, 0))
    mod_spec = pl.BlockSpec((None, 1, D_MODEL), lambda i: (i * TM // seq, 0, 0))
    in_specs = [row(D_MODEL), mod_spec, row(D_MODEL), pl.BlockSpec((D_MODEL, D_MODEL), lambda i: (0, 0))]
    out_specs = [row(D_MODEL), mod_spec]
    out_shape = [jax.ShapeDtypeStruct((t_tok, D_MODEL), BF16), jax.ShapeDtypeStruct(gmod.shape, F32)]
    args = [dxn, gmod, y, w_out]
    if attn is None:
        out_specs.append(row(D_MODEL))
        out_shape.append(jax.ShapeDtypeStruct((t_tok, D_MODEL), F32))
    else:
        in_specs += [row(512), row(512), pl.BlockSpec((TM, D_MODEL), lambda i: (i, C_GATE // D_MODEL))]
        out_specs += [row(512), row(512), row(D_MODEL)]
        out_shape += [jax.ShapeDtypeStruct((t_tok, 512), BF16)] * 2 + [jax.ShapeDtypeStruct((t_tok, D_MODEL), BF16)]
        args += list(attn)
    return _pallas(body, name=name, grid=(t_tok // TM,), in_specs=in_specs, out_specs=out_specs,
                   out_shape=out_shape, compiler_params=_cp(("arbitrary",), VMEM_MID))(*args)


def _norm_bwd(name, parts, w, x, g, scale, dxn, seq, rows_part=None):
    t_tok = x.shape[0]
    npart = len(parts)
    w3d = w.ndim == 3
    tiles_per_seq = seq // TM
    nrow_in = 0 if rows_part is None else 2

    def body(*refs):
        p_refs = refs[:npart]
        w_ref, x_ref, g_ref, sc_ref, dxn_ref = refs[npart:npart + 5]
        dx_ref, dss_ref, dg_ref = refs[npart + 5 + nrow_in:]
        i = pl.program_id(0)
        dh = jnp.zeros((TM, D_MODEL), F32)
        if rows_part is not None:
            r_ref, wr_ref = refs[npart + 5:npart + 7]
            dh = dh + _tn(r_ref[...].astype(BF16), wr_ref[...])
        for (arr, off), p_ref in zip(parts, p_refs):
            width = arr.shape[1]
            for j in range(width // 256):
                pj = p_ref[:, j * 256:(j + 1) * 256]
                c0 = off + j * 256
                wj = w_ref[c0 // 256] if w3d else w_ref[:, c0:c0 + 256]
                dh = dh + _nt(pj, wj)
        xv = x_ref[...]
        rstd = lax.rsqrt(jnp.mean(xv * xv, axis=-1, keepdims=True) + EPS)
        xhat = xv * rstd
        gv = g_ref[...]
        nrm = xhat * gv

        @pl.when(i % tiles_per_seq == 0)
        def _():
            dss_ref[...] = jnp.zeros_like(dss_ref)

        @pl.when(i == 0)
        def _():
            dg_ref[...] = jnp.zeros_like(dg_ref)

        dss_ref[0:1, :] += jnp.sum(dh, axis=0, keepdims=True)
        dss_ref[1:2, :] += jnp.sum(dh * nrm, axis=0, keepdims=True)
        dn = dh * (1.0 + sc_ref[...])
        dg_ref[0:1, :] += jnp.sum(dn * xhat, axis=0, keepdims=True)
        dxhat = dn * gv
        dx_ref[...] = rstd * (dxhat - xhat * jnp.mean(dxhat * xhat, axis=-1, keepdims=True)) + dxn_ref[...]

    row = lambda wd: pl.BlockSpec((TM, wd), lambda i: (i, 0))
    w_spec = (pl.BlockSpec(w.shape, lambda i: (0, 0, 0)) if w3d else pl.BlockSpec(w.shape, lambda i: (0, 0)))
    in_specs = [row(a.shape[1]) for a, _ in parts]
    in_specs += [w_spec, row(D_MODEL), pl.BlockSpec((1, D_MODEL), lambda i: (0, 0)),
                 pl.BlockSpec((None, 1, D_MODEL), lambda i: (i * TM // seq, 0, 0)), row(D_MODEL)]
    args = [a for a, _ in parts] + [w, x, g, scale, dxn]
    if rows_part is not None:
        in_specs += [pl.BlockSpec((8, TM), lambda i: (0, i)), pl.BlockSpec((8, D_MODEL), lambda i: (0, 0))]
        args += list(rows_part)
    nseq = t_tok // seq
    return _pallas(
        body, name=name, grid=(t_tok // TM,), in_specs=in_specs,
        out_specs=[row(D_MODEL), pl.BlockSpec((None, 8, D_MODEL), lambda i: (i * TM // seq, 0, 0)),
                   pl.BlockSpec((8, D_MODEL), lambda i: (0, 0))],
        out_shape=[jax.ShapeDtypeStruct((t_tok, D_MODEL), F32), jax.ShapeDtypeStruct((nseq, 8, D_MODEL), F32),
                   jax.ShapeDtypeStruct((8, D_MODEL), F32)],
        compiler_params=_cp(("arbitrary",), VMEM_BIG))(*args)


def _dw(name, a, parts, blocked=None):
    t_tok, ka = a.shape
    tt = 512
    npart = len(parts)
    nt = t_tok // tt

    def body(*refs):
        a_ref = refs[0]
        p_refs = refs[1:1 + npart]
        o_refs = refs[1 + npart:1 + 2 * npart]
        acc_refs = refs[1 + 2 * npart:]
        t = pl.program_id(0)
        av = a_ref[...]
        for p_ref, acc in zip(p_refs, acc_refs):
            upd = _tn(av, p_ref[...])

            @pl.when(t == 0)
            def _():
                acc[...] = upd

            @pl.when(t > 0)
            def _():
                acc[...] += upd

        @pl.when(t == nt - 1)
        def _():
            for o_ref, acc in zip(o_refs, acc_refs):
                if blocked is None:
                    o_ref[...] = acc[...].astype(BF16)
                else:
                    for j in range(o_ref.shape[0]):
                        o_ref[j] = acc[:, j * blocked:(j + 1) * blocked].astype(BF16)

    in_specs = [pl.BlockSpec((tt, ka), lambda t: (t, 0))]
    in_specs += [pl.BlockSpec((tt, p.shape[1]), lambda t: (t, 0)) for p in parts]
    if blocked is None:
        out_shape = [jax.ShapeDtypeStruct((ka, p.shape[1]), BF16) for p in parts]
        out_specs = [pl.BlockSpec((ka, p.shape[1]), lambda t: (0, 0)) for p in parts]
    else:
        out_shape = [jax.ShapeDtypeStruct((p.shape[1] // blocked, ka, blocked), BF16) for p in parts]
        out_specs = [pl.BlockSpec((p.shape[1] // blocked, ka, blocked), lambda t: (0, 0, 0)) for p in parts]
    return _pallas(body, name=name, grid=(nt,), in_specs=in_specs, out_specs=out_specs, out_shape=out_shape,
                   scratch_shapes=[pltpu.VMEM((ka, p.shape[1]), F32) for p in parts],
                   compiler_params=_cp(("arbitrary",), VMEM_BIG))(a, *parts)


def _dw_rows(name, rows_t, h):
    t_tok = h.shape[0]
    tt = 512

    def body(r_ref, h_ref, o_ref):
        @pl.when(pl.program_id(0) == 0)
        def _():
            o_ref[...] = jnp.zeros_like(o_ref)

        o_ref[...] += _nn(r_ref[...].astype(BF16), h_ref[...])

    return _pallas(body, name=name, grid=(t_tok // tt,),
                   in_specs=[pl.BlockSpec((8, tt), lambda t: (0, t)), pl.BlockSpec((tt, D_MODEL), lambda t: (t, 0))],
                   out_specs=pl.BlockSpec((8, D_MODEL), lambda t: (0, 0)),
                   out_shape=jax.ShapeDtypeStruct((8, D_MODEL), F32),
                   compiler_params=_cp(("arbitrary",), VMEM_MID))(rows_t, h)


def _lru_gates(xc, blk, wa_ref, wx_ref, ba_ref, bx_ref, sp):
    cols = slice(blk * LRU_BLOCK_W, (blk + 1) * LRU_BLOCK_W)
    xb = xc[:, cols].astype(BF16)
    r = _sigmoid(_nn(xb, wa_ref[blk].astype(BF16)) + ba_ref[:, cols])
    ig = _sigmoid(_nn(xb, wx_ref[blk].astype(BF16)) + bx_ref[:, cols])
    log_a = -LRU_C * r * sp[:, cols]
    a = jnp.exp(log_a)
    x2 = 2.0 * log_a
    series = -x2 * (1.0 + x2 * (0.5 + x2 * (1.0 / 6.0)))
    z = jnp.where(x2 > -0.01, series, 1.0 - a * a)
    mult = z * lax.rsqrt(jnp.maximum(z, 1e-30))
    return xb, r, ig, a, mult


def _softplus_neg(lam):
    return jnp.maximum(-lam, 0.0) + jnp.log(1.0 + jnp.exp(-jnp.abs(lam)))


def _conv_taps(xe_ref, cw_ref, cb_ref):
    xc = cb_ref[...] + xe_ref[8:8 + TC, :] * cw_ref[3:4, :]
    for k in range(1, 4):
        xc = xc + xe_ref[8 - k:8 - k + TC, :] * cw_ref[3 - k:4 - k, :]
    return xc


def _lru_fwd(proj, cw, cb, w_a, b_a, w_x, b_x, lam, seq):
    t_tok = proj.shape[0]
    nc = seq // TC

    def body(x_ref, cw_ref, cb_ref, wa_ref, ba_ref, wx_ref, bx_ref, lam_ref, hs_ref, xe_s, a_s, u_s, h_s):
        c = pl.program_id(1)

        @pl.when(c == 0)
        def _():
            xe_s[0:8, :] = jnp.zeros((8, D_MODEL), F32)
            h_s[...] = jnp.zeros_like(h_s)

        xe_s[8:8 + TC, :] = x_ref[...]
        xc = _conv_taps(xe_s, cw_ref, cb_ref)
        sp = _softplus_neg(lam_ref[...])
        for blk in range(LRU_BLOCKS):
            cols = slice(blk * LRU_BLOCK_W, (blk + 1) * LRU_BLOCK_W)
            _, _, ig, a, mult = _lru_gates(xc, blk, wa_ref, wx_ref, ba_ref, bx_ref, sp)
            a_s[:, cols] = a
            u_s[:, cols] = mult * ig * xc[:, cols]

        def step(t, h):
            h = a_s[pl.ds(t, 1), :] * h + u_s[pl.ds(t, 1), :]
            hs_ref[pl.ds(t, 1), :] = h
            return h

        h_s[0:1, :] = lax.fori_loop(0, TC, step, h_s[0:1, :], unroll=8)
        xe_s[0:8, :] = xe_s[TC:TC + 8, :]

    full = lambda shape: pl.BlockSpec(shape, lambda b, c: (0,) * len(shape))
    return _pallas(
        body, name="lru_fwd", grid=(t_tok // seq, nc),
        in_specs=[pl.BlockSpec((TC, D_MODEL), lambda b, c: (b * nc + c, 0)), full((4, D_MODEL)), full((1, D_MODEL)),
                  full((LRU_BLOCKS, LRU_BLOCK_W, LRU_BLOCK_W)), full((1, D_MODEL)),
                  full((LRU_BLOCKS, LRU_BLOCK_W, LRU_BLOCK_W)), full((1, D_MODEL)), full((1, D_MODEL))],
        out_specs=pl.BlockSpec((TC, D_MODEL), lambda b, c: (b * nc + c, 0)),
        out_shape=jax.ShapeDtypeStruct((t_tok, D_MODEL), F32),
        scratch_shapes=[pltpu.VMEM((TC + 8, D_MODEL), F32), pltpu.VMEM((TC, D_MODEL), F32),
                        pltpu.VMEM((TC, D_MODEL), F32), pltpu.VMEM((8, D_MODEL), F32)],
        compiler_params=_cp(("arbitrary", "arbitrary"), VMEM_MID))(proj, cw, cb, w_a, b_a, w_x, b_x, lam)


def _lru_bwd(proj, hs, dyh, cw, cb, w_a, b_a, w_x, b_x, lam, seq):
    t_tok = proj.shape[0]
    nc = seq // TC

    def body(x_ref, xh_ref, g_ref, hs_ref, hh_ref, dy_ref, cw_ref, cb_ref, wa_ref, ba_ref, wx_ref, bx_ref, lam_ref,
             dp_ref, dcw_ref, dvec_ref, dwa_ref, dwx_ref,
             xe_s, he_s, de_s, a_s, r_s, i_s, m_s, dh_s, carry_s):
        b, cr = pl.program_id(0), pl.program_id(1)
        c = nc - 1 - cr

        @pl.when(jnp.logical_and(b == 0, cr == 0))
        def _():
            dcw_ref[...] = jnp.zeros_like(dcw_ref)
            dvec_ref[...] = jnp.zeros_like(dvec_ref)
            dwa_ref[...] = jnp.zeros_like(dwa_ref)
            dwx_ref[...] = jnp.zeros_like(dwx_ref)

        @pl.when(cr == 0)
        def _():
            carry_s[...] = jnp.zeros_like(carry_s)
            de_s[TC:TC + 8, :] = jnp.zeros((8, D_MODEL), F32)

        first = c == 0
        xe_s[0:8, :] = jnp.where(first, 0.0, xh_ref[...])
        xe_s[8:8 + TC, :] = x_ref[...]
        he_s[0:8, :] = jnp.where(first, 0.0, hh_ref[...])
        he_s[8:8 + TC, :] = hs_ref[...]
        xc = _conv_taps(xe_s, cw_ref, cb_ref)
        lam_v = lam_ref[...]
        sp = _softplus_neg(lam_v)
        for blk in range(LRU_BLOCKS):
            cols = slice(blk * LRU_BLOCK_W, (blk + 1) * LRU_BLOCK_W)
            _, r, ig, a, mult = _lru_gates(xc, blk, wa_ref, wx_ref, ba_ref, bx_ref, sp)
            a_s[:, cols], r_s[:, cols], i_s[:, cols], m_s[:, cols] = a, r, ig, mult

        gt = g_ref[...]
        dyh = dy_ref[...]
        dh_s[...] = dyh * _silu(gt)
        dp_ref[:, D_MODEL:] = (dyh * hs_ref[...] * _dsilu(gt)).astype(BF16)

        def step(k, carry):
            t = TC - 1 - k
            dh = dh_s[pl.ds(t, 1), :] + carry
            dh_s[pl.ds(t, 1), :] = dh
            return a_s[pl.ds(t, 1), :] * dh

        carry_s[0:1, :] = lax.fori_loop(0, TC, step, carry_s[0:1, :], unroll=8)

        hprev = he_s[7:7 + TC, :]
        for blk in range(LRU_BLOCKS):
            cols = slice(blk * LRU_BLOCK_W, (blk + 1) * LRU_BLOCK_W)
            xcb = xc[:, cols]
            a, r, ig, mult, dh = a_s[:, cols], r_s[:, cols], i_s[:, cols], m_s[:, cols], dh_s[:, cols]
            spb = sp[:, cols]
            dmult = dh * ig * xcb
            di = dh * mult * xcb
            dxc = dh * mult * ig
            dla = dh * hprev[:, cols] * a - dmult * (a * a) * lax.rsqrt(jnp.maximum(mult * mult, 1e-30))
            dr = dla * (-LRU_C * spb)
            dsp = jnp.sum(dla * (-LRU_C * r), axis=0, keepdims=True)
            dga = dr * r * (1.0 - r)
            dgx = di * ig * (1.0 - ig)
            dga_b, dgx_b = dga.astype(BF16), dgx.astype(BF16)
            xb = xcb.astype(BF16)
            dxc = dxc + _nt(dga_b, wa_ref[blk].astype(BF16)) + _nt(dgx_b, wx_ref[blk].astype(BF16))
            dwa_ref[blk] += _tn(xb, dga_b)
            dwx_ref[blk] += _tn(xb, dgx_b)
            dvec_ref[1:2, cols] += jnp.sum(dga, axis=0, keepdims=True)
            dvec_ref[2:3, cols] += jnp.sum(dgx, axis=0, keepdims=True)
            dvec_ref[3:4, cols] += dsp * (-1.0 / (1.0 + jnp.exp(lam_v[:, cols])))
            de_s[0:TC, cols] = dxc

        dxc = de_s[0:TC, :]
        dvec_ref[0:1, :] += jnp.sum(dxc, axis=0, keepdims=True)
        dxr = dxc * cw_ref[3:4, :]
        dcw_ref[3:4, :] += jnp.sum(dxc * xe_s[8:8 + TC, :], axis=0, keepdims=True)
        for k in range(1, 4):
            dxr = dxr + de_s[k:k + TC, :] * cw_ref[3 - k:4 - k, :]
            dcw_ref[3 - k:4 - k, :] += jnp.sum(dxc * xe_s[8 - k:8 - k + TC, :], axis=0, keepdims=True)
        dp_ref[:, :D_MODEL] = dxr.astype(BF16)
        de_s[TC:TC + 8, :] = de_s[0:8, :]

    chunk = lambda col: pl.BlockSpec((TC, D_MODEL), lambda b, cr: (b * nc + nc - 1 - cr, col))
    halo = lambda col: pl.BlockSpec(
        (8, D_MODEL), lambda b, cr: (jnp.maximum((b * nc + nc - 1 - cr) * (TC // 8) - 1, 0), col))
    full = lambda shape: pl.BlockSpec(shape, lambda b, cr: (0,) * len(shape))
    wblk = (LRU_BLOCKS, LRU_BLOCK_W, LRU_BLOCK_W)
    return _pallas(
        body, name="lru_bwd", grid=(t_tok // seq, nc),
        in_specs=[chunk(0), halo(0), chunk(1), chunk(0), halo(0), chunk(0),
                  full((4, D_MODEL)), full((1, D_MODEL)), full(wblk), full((1, D_MODEL)), full(wblk),
                  full((1, D_MODEL)), full((1, D_MODEL))],
        out_specs=[pl.BlockSpec((TC, 2 * D_MODEL), lambda b, cr: (b * nc + nc - 1 - cr, 0)),
                   full((8, D_MODEL)), full((8, D_MODEL)), full(wblk), full(wblk)],
        out_shape=[jax.ShapeDtypeStruct((t_tok, 2 * D_MODEL), BF16), jax.ShapeDtypeStruct((8, D_MODEL), F32),
                   jax.ShapeDtypeStruct((8, D_MODEL), F32), jax.ShapeDtypeStruct(wblk, F32),
                   jax.ShapeDtypeStruct(wblk, F32)],
        scratch_shapes=[pltpu.VMEM((TC + 8, D_MODEL), F32), pltpu.VMEM((TC + 8, D_MODEL), F32),
                        pltpu.VMEM((TC + 8, D_MODEL), F32)]
        + [pltpu.VMEM((TC, D_MODEL), F32)] * 5 + [pltpu.VMEM((8, D_MODEL), F32)],
        compiler_params=_cp(("arbitrary", "arbitrary"), VMEM_BIG),
    )(proj, proj, proj, hs, hs, dyh, cw, cb, w_a, b_a, w_x, b_x, lam)


def _last_layer_tail(hs, proj, w_out, x, gmod, final_g, target, seq):
    t_tok = x.shape[0]
    tiles_per_seq = seq // TM

    def body(hs_ref, g_ref, w_ref, x_ref, gm_ref, fg_ref, t_ref,
             yg_ref, dx_ref, dy_ref, dyg_ref, dgm_ref, loss_ref, dfg_ref):
        i = pl.program_id(0)

        @pl.when(i == 0)
        def _():
            loss_ref[...] = jnp.zeros_like(loss_ref)
            dfg_ref[...] = jnp.zeros_like(dfg_ref)

        @pl.when(i % tiles_per_seq == 0)
        def _():
            dgm_ref[...] = jnp.zeros_like(dgm_ref)

        gm = gm_ref[...]
        yg = (hs_ref[...] * _silu(g_ref[...])).astype(BF16)
        yg_ref[...] = yg
        y = _nn(yg, w_ref[...])
        xv = x_ref[...] + gm * y
        gv = fg_ref[...]
        rstd = lax.rsqrt(jnp.mean(xv * xv, axis=-1, keepdims=True) + EPS)
        xhat = xv * rstd
        err = xhat * gv - t_ref[...]
        loss_ref[0:1, :] += jnp.sum(err * err, axis=0, keepdims=True) * (0.5 / D_MODEL)
        dout = err * (1.0 / D_MODEL)
        dfg_ref[0:1, :] += jnp.sum(dout * xhat, axis=0, keepdims=True)
        dxhat = dout * gv
        dxv = rstd * (dxhat - xhat * jnp.mean(dxhat * xhat, axis=-1, keepdims=True))
        dx_ref[...] = dxv
        dgm_ref[...] += jnp.sum(dxv * y, axis=0, keepdims=True)
        dy = (dxv * gm).astype(BF16)
        dy_ref[...] = dy
        dyg_ref[...] = _nt(dy, w_ref[...])

    row = pl.BlockSpec((TM, D_MODEL), lambda i: (i, 0))
    acc = pl.BlockSpec((8, D_MODEL), lambda i: (0, 0))
    mod_spec = pl.BlockSpec((None, 1, D_MODEL), lambda i: (i * TM // seq, 0, 0))
    return _pallas(
        body, name="last_layer_tail", grid=(t_tok // TM,),
        in_specs=[row, pl.BlockSpec((TM, D_MODEL), lambda i: (i, 1)), pl.BlockSpec((D_MODEL, D_MODEL), lambda i: (0, 0)),
                  row, mod_spec, pl.BlockSpec((1, D_MODEL), lambda i: (0, 0)), row],
        out_specs=[row, row, row, row, mod_spec, acc, acc],
        out_shape=[jax.ShapeDtypeStruct((t_tok, D_MODEL), BF16), jax.ShapeDtypeStruct((t_tok, D_MODEL), F32),
                   jax.ShapeDtypeStruct((t_tok, D_MODEL), BF16), jax.ShapeDtypeStruct((t_tok, D_MODEL), F32),
                   jax.ShapeDtypeStruct(gmod.shape, F32), jax.ShapeDtypeStruct((8, D_MODEL), F32),
                   jax.ShapeDtypeStruct((8, D_MODEL), F32)],
        compiler_params=_cp(("arbitrary",), VMEM_BIG))(hs, proj, w_out, x, gmod, final_g, target)


def _final_loss(x, g, target):
    t_tok = x.shape[0]

    def body(x_ref, g_ref, t_ref, dx_ref, loss_ref, dg_ref):
        @pl.when(pl.program_id(0) == 0)
        def _():
            loss_ref[...] = jnp.zeros_like(loss_ref)
            dg_ref[...] = jnp.zeros_like(dg_ref)

        xv = x_ref[...]
        gv = g_ref[...]
        rstd = lax.rsqrt(jnp.mean(xv * xv, axis=-1, keepdims=True) + EPS)
        xhat = xv * rstd
        err = xhat * gv - t_ref[...]
        loss_ref[0:1, :] += jnp.sum(err * err, axis=0, keepdims=True) * (0.5 / D_MODEL)
        dout = err * (1.0 / D_MODEL)
        dg_ref[0:1, :] += jnp.sum(dout * xhat, axis=0, keepdims=True)
        dxhat = dout * gv
        dx_ref[...] = rstd * (dxhat - xhat * jnp.mean(dxhat * xhat, axis=-1, keepdims=True))

    row = pl.BlockSpec((TM, D_MODEL), lambda i: (i, 0))
    acc = pl.BlockSpec((8, D_MODEL), lambda i: (0, 0))
    return _pallas(body, name="final_loss", grid=(t_tok // TM,),
                   in_specs=[row, pl.BlockSpec((1, D_MODEL), lambda i: (0, 0)), row],
                   out_specs=[row, acc, acc],
                   out_shape=[jax.ShapeDtypeStruct((t_tok, D_MODEL), F32)] + [jax.ShapeDtypeStruct((8, D_MODEL), F32)] * 2,
                   compiler_params=_cp(("arbitrary",), VMEM_MID))(x, g, target)


def _adam_math(w, g, m, v):
    m_new = ADAM_B1 * m + (1.0 - ADAM_B1) * g
    v_new = ADAM_B2 * v + (1.0 - ADAM_B2) * (g * g)
    m_hat = m_new / (1.0 - ADAM_B1 ** ADAM_STEP)
    v_hat = v_new / (1.0 - ADAM_B2 ** ADAM_STEP)
    delta = -ADAM_LR * (m_hat / (jnp.sqrt(v_hat) + ADAM_EPS) + ADAM_WD * w)
    return delta, m_new, v_new


def _sum_leading(name, x, out_dtype=F32):
    n, rows, cols = x.shape
    tr = PACK_ROWS if rows % PACK_ROWS == 0 else rows

    def body(x_ref, o_ref):
        acc = x_ref[0].astype(F32)
        for d in range(1, n):
            acc = acc + x_ref[d].astype(F32)
        o_ref[...] = acc.astype(out_dtype)

    return _pallas(body, name=name, grid=(rows // tr,),
                   in_specs=[pl.BlockSpec((n, tr, cols), lambda i: (0, i, 0))],
                   out_specs=pl.BlockSpec((tr, cols), lambda i: (i, 0)),
                   out_shape=jax.ShapeDtypeStruct((rows, cols), out_dtype),
                   compiler_params=_cp(("arbitrary",), VMEM_MID))(x)


def _adamw(name, w, m, v, g=None, parts=None):
    rows, cols = w.shape
    tr = rows if rows <= 256 else 256

    def body(*refs):
        w_ref, m_ref, v_ref, g_in, g_ref, d_ref, mo_ref, vo_ref = refs
        if parts is None:
            gv = g_in[...]
        else:
            acc = g_in[0].astype(F32)
            for d in range(1, parts.shape[0]):
                acc = acc + g_in[d].astype(F32)
            gv = acc[:, :cols]
        delta, m_new, v_new = _adam_math(w_ref[...], gv, m_ref[...], v_ref[...])
        g_ref[...] = gv
        d_ref[...] = delta
        mo_ref[...] = m_new
        vo_ref[...] = v_new

    row = pl.BlockSpec((tr, cols), lambda i: (i, 0))
    if parts is None:
        g_spec, g_arg = row, g
    else:
        g_spec, g_arg = pl.BlockSpec((parts.shape[0], tr, parts.shape[2]), lambda i: (0, i, 0)), parts
    return _pallas(body, name=name, grid=(rows // tr,), in_specs=[row, row, row, g_spec], out_specs=[row] * 4,
                   out_shape=[jax.ShapeDtypeStruct((rows, cols), F32)] * 4,
                   compiler_params=_cp(("arbitrary",), VMEM_MID))(w, m, v, g_arg)


def _adamw_many(name, groups):
    ntens = len(groups)

    def body(*refs):
        ins, outs = refs[:4 * ntens], refs[4 * ntens:]
        for k in range(ntens):
            w_ref, m_ref, v_ref, g_ref = ins[4 * k:4 * k + 4]
            gv = g_ref[...]
            delta, m_new, v_new = _adam_math(w_ref[...], gv, m_ref[...], v_ref[...])
            for o_ref, val in zip(outs[4 * k:4 * k + 4], (gv, delta, m_new, v_new)):
                o_ref[...] = val

    flat = [a for grp in groups for a in grp]
    out_shape = [jax.ShapeDtypeStruct(grp[0].shape, F32) for grp in groups for _ in range(4)]
    outs = _pallas(body, name=name, out_shape=out_shape, compiler_params=_cp(vmem=VMEM_MID))(*flat)
    return [tuple(outs[4 * k:4 * k + 4]) for k in range(ntens)]


def _pack_rows(arrs):
    rows, meta, total = [], [], 0
    for a in arrs:
        flat = a.reshape(-1)
        nrow = -(-flat.shape[0] // 1024) * 8
        rows.append(jnp.pad(flat, (0, nrow * 128 - flat.shape[0])).reshape(nrow, 128))
        meta.append((a.shape, flat.shape[0], nrow))
        total += nrow
    tail = -total % PACK_ROWS
    if tail:
        rows.append(jnp.zeros((tail, 128), F32))
    return jnp.concatenate(rows, axis=0), meta


def _unpack_rows(packed, meta):
    out, r0 = [], 0
    for shape, size, nrow in meta:
        out.append(packed[r0:r0 + nrow].reshape(-1)[:size].reshape(shape))
        r0 += nrow
    return out


WEIGHTS = ["rel_bias", "norm_g", "ada_w", "ada_b", "attn_w_in", "attn_sinks", "attn_b_f", "attn_w_out", "lru_w_in",
           "lru_conv_w", "lru_conv_b", "lru_w_a", "lru_b_a", "lru_w_x", "lru_b_x", "lru_lambda", "lru_w_out", "final_g"]
BIG = ["ada_w", "attn_w_in", "attn_w_out", "lru_w_in", "lru_w_out"]
PACK_ROWS = 256


def kernel(x, c, rel_bias, norm_g, ada_w, ada_b, attn_w_in, attn_sinks, attn_b_f, attn_w_out, lru_w_in, lru_conv_w, lru_conv_b, lru_w_a, lru_b_a, lru_w_x, lru_b_x, lru_lambda, lru_w_out, final_g, loss_target, m_rel_bias, m_norm_g, m_ada_w, m_ada_b, m_attn_w_in, m_attn_sinks, m_attn_b_f, m_attn_w_out, m_lru_w_in, m_lru_conv_w, m_lru_conv_b, m_lru_w_a, m_lru_b_a, m_lru_w_x, m_lru_b_x, m_lru_lambda, m_lru_w_out, m_final_g, v_rel_bias, v_norm_g, v_ada_w, v_ada_b, v_attn_w_in, v_attn_sinks, v_attn_b_f, v_attn_w_out, v_lru_w_in, v_lru_conv_w, v_lru_conv_b, v_lru_w_a, v_lru_b_a, v_lru_w_x, v_lru_b_x, v_lru_lambda, v_lru_w_out, v_final_g):
    nseq, seq, _ = x.shape
    t_tok = nseq * seq
    me = 4 * lax.axis_index("x") + 2 * lax.axis_index("y") + lax.axis_index("c")
    x0 = x.reshape(t_tok, D_MODEL)
    target = loss_target.reshape(t_tok, D_MODEL)

    w_in_pad = jnp.pad(attn_w_in[0].astype(BF16), ((0, 0), (0, SHARD_W_PAD - SHARD_W_IN)))
    vec_shard = jnp.concatenate([lru_conv_w[0], lru_conv_b, lru_b_a, lru_b_x, lru_lambda], axis=0)
    g_w_in, g_vec, g_c = _exchange("gather_first", [w_in_pad, vec_shard, c], [])
    later_w = [attn_w_out[0].astype(BF16), lru_w_in[0].astype(BF16), lru_w_out[0].astype(BF16)]
    later_handle, later_token = _exchange_start("gather_later_start", later_w, [], after=g_vec)
    w_full = jnp.transpose(g_w_in[:, :, :SHARD_W_IN], (1, 0, 2)).reshape(D_MODEL, N_DEV * SHARD_W_IN)
    w_aq, w_ak, w_av = w_full[:, 0:512], w_full[:, 512:640], w_full[:, 640:768]
    w_bq, w_bk, w_bv = w_full[:, 768:1280], w_full[:, 1280:1792], w_full[:, 1792:2304]
    w_f, w_gate = w_full[:, 2304:2312], w_full[:, 2312:3336]
    w_main = jnp.concatenate([w_bq, w_bk, w_bv, w_aq, w_gate, w_ak, w_av], axis=1)
    wf_t = jnp.transpose(w_f)
    vec_full = jnp.transpose(g_vec, (1, 0, 2)).reshape(8, D_MODEL)
    conv_w, conv_b, b_a, b_x, lam = vec_full[0:4], vec_full[4:5], vec_full[5:6], vec_full[6:7], vec_full[7:8]
    c_all = g_c.reshape(N_DEV * nseq, D_MODEL)

    ncol = ada_w.shape[2]
    ada_b_slice = lax.dynamic_slice(ada_b.reshape(2, N_DEV, ncol), (0, me, 0), (2, 1, ncol))
    mod_part = _ada_mod(c_all, ada_w, ada_b_slice)
    (g_mod,) = _exchange("gather_mod", [mod_part], [])
    mine = lax.dynamic_slice(g_mod, (0, 0, me * nseq, 0), (N_DEV, 2, nseq, ncol))
    mod = jnp.transpose(mine, (1, 2, 0, 3)).reshape(2, nseq, 3 * D_MODEL)
    shift = [mod[l, :, 0:D_MODEL].reshape(nseq, 1, D_MODEL) for l in range(2)]
    scale = [mod[l, :, D_MODEL:2 * D_MODEL].reshape(nseq, 1, D_MODEL) for l in range(2)]
    gmod = [mod[l, :, 2 * D_MODEL:].reshape(nseq, 1, D_MODEL) for l in range(2)]

    onehot = _bucket_onehot()
    bias = _bias_expand(jnp.transpose(rel_bias), onehot).reshape(N_HEADS, BLOCK, 2 * BLOCK)
    sinks = attn_sinks.reshape(N_HEADS)
    b_f = attn_b_f.reshape(N_HEADS, 1)
    norm_g0 = norm_g[0:1] + later_token[0:1, 0:1]
    h0, qkvg, fl_t = _norm_proj("norm_proj0", x0, norm_g0, shift[0], scale[0], w_main, seq, BF16, wf_t=wf_t)
    f_row, f_col = _fox_prep(fl_t, b_f, seq)
    a_out, lse_a = _swa_fwd(qkvg, bias, sinks, seq)
    q_aug, k_aug, kt_aug, vt, qt_aug = _fox_aug(qkvg, f_col, seq)
    b_out, lse_b = _fox_fwd_t(q_aug, k_aug, vt, seq)
    g_later = _exchange_wait("gather_later_wait", later_handle, after=lse_b)
    w_out0, g_lru_in, w_out1 = (_with_own(g, w, me) for g, w in zip(g_later, later_w))
    w_out0, w_out1 = w_out0.reshape(D_MODEL, D_MODEL), w_out1.reshape(D_MODEL, D_MODEL)
    yg0, y0, x1 = _out_proj("out_proj0", [a_out, b_out], qkvg, C_GATE // D_MODEL, w_out0, x0, gmod[0], seq)

    h1, proj1 = _norm_proj("norm_proj1", x1, norm_g[1:2], shift[1], scale[1], g_lru_in, seq, F32)
    hs = _lru_fwd(proj1, conv_w, conv_b, lru_w_a[0], b_a, lru_w_x[0], b_x, lam, seq)

    yg1, dx2, dy1, dyh, dgm1, loss_rows, dfinal_rows = _last_layer_tail(
        hs, proj1, w_out1, x1, gmod[1], final_g.reshape(1, D_MODEL), target, seq)

    dproj1, dcw, dvec, dw_a, dw_x = _lru_bwd(proj1, hs, dyh, conv_w, conv_b, lru_w_a[0], b_a, lru_w_x[0], b_x, lam, seq)
    dx1, dss1, dg1 = _norm_bwd("norm1_bwd", [(dproj1, 0)], g_lru_in, x1, norm_g[1:2], scale[1], dx2, seq)
    (p_w_out1,) = _dw("dw_out1", yg1, [dy1])
    (p_lru_in,) = _dw("dw_lru_in", h1, [dproj1], blocked=2 * D_MODEL // N_DEV)

    rows_out = D_MODEL // N_DEV
    gpack1, gmeta1 = _pack_rows([dcw[0:4], dvec[0:4], dg1[0], dfinal_rows[0]])
    dwax = jnp.stack([dw_a, dw_x]).astype(BF16)
    own1 = [gpack1, dwax, p_lru_in, p_w_out1.reshape(N_DEV, rows_out, D_MODEL)]
    grads1_handle, grads1_token = _exchange_start("grads1_start", own1[:2], own1[2:], after=dx1)

    gmod0 = gmod[0] + grads1_token[0:1, 0:1]
    dy0, dgm0, du_a, du_b, dgate = _out_proj_bwd("out_proj0_bwd", dx1, gmod0, y0, w_out0, seq,
                                                  attn=(a_out, b_out, qkvg))
    dq_a, dkv_a, dbias, dsink = _swa_bwd(qkvg, du_a, a_out, lse_a, bias, sinks, seq)
    dq_b, dk_b, dv_b, df4 = _fox_bwd_t(q_aug, k_aug, kt_aug, qt_aug, qkvg, du_b, b_out, lse_b, seq)
    dfl_t, db_f = _fox_post(df4.reshape(N_HEADS, t_tok), fl_t, b_f, seq)
    parts0 = [(dq_b, C_BQ), (dk_b, C_BK), (dv_b, C_BV), (dq_a, C_AQ), (dgate, C_GATE), (dkv_a, C_AK)]
    (p_w_out0,) = _dw("dw_out0", yg0, [dy0])
    pw_bq, pw_bk, pw_bv, pw_aq, pw_gate, pw_akv = _dw("dw_attn_in", h0, [p for p, _ in parts0])
    pw_f = _dw_rows("dw_f", dfl_t, h0)

    p_w_in = jnp.concatenate([pw_aq, pw_akv, pw_bq, pw_bk, pw_bv, jnp.transpose(pw_f).astype(BF16), pw_gate], axis=1)
    p_w_in = jnp.transpose(p_w_in.reshape(D_MODEL, N_DEV, SHARD_W_IN), (1, 0, 2))
    p_w_in = jnp.pad(p_w_in, ((0, 0), (0, 0), (0, SHARD_W_PAD - SHARD_W_IN)))
    own0 = [p_w_in, p_w_out0.reshape(N_DEV, rows_out, D_MODEL)]
    landed1 = _exchange_wait("grads1_wait", grads1_handle, after=p_w_in)
    grads0_handle, grads0_token = _exchange_start("grads0_start", [], own0, after=landed1[0])
    scale0 = scale[0] + grads0_token[0:1, 0:1]
    dx0, dss0, dg0 = _norm_bwd("norm0_bwd", parts0, w_main, x0, norm_g[0:1], scale0, dx1, seq,
                               rows_part=(dfl_t, wf_t))
    dbias_t = _bias_reduce(dbias.reshape(N_HEADS, BLOCK * 2 * BLOCK), onehot)

    gpack0, gmeta0 = _pack_rows([jnp.transpose(dbias_t), dg0[0], dsink[:, 0], db_f[:, 0], loss_rows[0]])
    dmod = jnp.stack([jnp.concatenate([dss[:, 0], dss[:, 1], dgm[:, 0]], axis=1)
                      for dss, dgm in ((dss0, dgm0), (dss1, dgm1))], axis=1)
    g_small0, g_dmod = _exchange("exchange_small", [gpack0, dmod], [])
    landed0 = _exchange_wait("grads0_wait", grads0_handle, after=g_small0)
    r_w_in, r_w_out0 = (_with_own(g, lax.dynamic_index_in_dim(a, me, 0, keepdims=False), me)
                        for g, a in zip(landed0, own0))
    g_small1, g_dwax = (_with_own(g, a, me) for g, a in zip(landed1[:2], own1[:2]))
    r_lru_in, r_w_out1 = (_with_own(g, lax.dynamic_index_in_dim(a, me, 0, keepdims=False), me)
                          for g, a in zip(landed1[2:], own1[2:]))

    d_rel, d_g0, d_sinks, d_b_f, loss_cols = _unpack_rows(_sum_leading("sum_small0", g_small0), gmeta0)
    loss = jnp.sum(loss_cols)
    d_cw, d_vec, d_g1, d_final_g = _unpack_rows(_sum_leading("sum_small1", g_small1), gmeta1)
    d_norm_g = jnp.stack([d_g0, d_g1])
    d_wax = _sum_leading("sum_dwax", g_dwax.reshape(N_DEV, 2 * LRU_BLOCKS * LRU_BLOCK_W, LRU_BLOCK_W))
    d_wa, d_wx = d_wax[:LRU_BLOCKS * LRU_BLOCK_W], d_wax[LRU_BLOCKS * LRU_BLOCK_W:]
    cols = lambda a: lax.dynamic_slice(a, (0, me * LRU_BLOCK_W), (a.shape[0], LRU_BLOCK_W))
    dmod_all = g_dmod.reshape(N_DEV * nseq, 2 * 3 * D_MODEL)
    d_ada_b = _sum_leading("sum_ada_b", dmod_all.reshape(N_DEV * nseq, 2 * 3 * D_MODEL // 128, 128)).reshape(2, 3 * D_MODEL)
    dmod_slice = lax.dynamic_slice(dmod_all.reshape(N_DEV * nseq, 2, N_DEV, ncol), (0, 0, me, 0),
                                   (N_DEV * nseq, 2, 1, ncol)).reshape(N_DEV * nseq, 2, ncol)
    d_ada_w = _ada_w_grad(c_all, jnp.transpose(dmod_slice, (1, 0, 2)))

    given = dict(
        rel_bias=(rel_bias, m_rel_bias, v_rel_bias), norm_g=(norm_g, m_norm_g, v_norm_g),
        ada_w=(ada_w, m_ada_w, v_ada_w), ada_b=(ada_b, m_ada_b, v_ada_b),
        attn_w_in=(attn_w_in, m_attn_w_in, v_attn_w_in), attn_sinks=(attn_sinks, m_attn_sinks, v_attn_sinks),
        attn_b_f=(attn_b_f, m_attn_b_f, v_attn_b_f), attn_w_out=(attn_w_out, m_attn_w_out, v_attn_w_out),
        lru_w_in=(lru_w_in, m_lru_w_in, v_lru_w_in), lru_conv_w=(lru_conv_w, m_lru_conv_w, v_lru_conv_w),
        lru_conv_b=(lru_conv_b, m_lru_conv_b, v_lru_conv_b), lru_w_a=(lru_w_a, m_lru_w_a, v_lru_w_a),
        lru_b_a=(lru_b_a, m_lru_b_a, v_lru_b_a), lru_w_x=(lru_w_x, m_lru_w_x, v_lru_w_x),
        lru_b_x=(lru_b_x, m_lru_b_x, v_lru_b_x), lru_lambda=(lru_lambda, m_lru_lambda, v_lru_lambda),
        lru_w_out=(lru_w_out, m_lru_w_out, v_lru_w_out), final_g=(final_g, m_final_g, v_final_g))
    results = {}

    def big(name, shape2d, g=None, parts=None):
        w, m, v = (a.reshape(shape2d) for a in given[name])
        outs = _adamw("adamw_" + name, w, m, v, g=g, parts=parts)
        results[name] = tuple(o.reshape(given[name][0].shape) for o in outs)

    big("ada_w", (2 * D_MODEL, ncol), g=d_ada_w.reshape(2 * D_MODEL, ncol))
    big("attn_w_in", (D_MODEL, SHARD_W_IN), parts=r_w_in)
    big("attn_w_out", (rows_out, D_MODEL), parts=r_w_out0)
    big("lru_w_in", (D_MODEL, 2 * D_MODEL // N_DEV), parts=r_lru_in)
    big("lru_w_out", (rows_out, D_MODEL), parts=r_w_out1)

    small_grads = dict(
        rel_bias=d_rel, norm_g=d_norm_g, ada_b=d_ada_b, attn_sinks=d_sinks.reshape(1, N_HEADS),
        attn_b_f=d_b_f.reshape(1, N_HEADS), lru_conv_w=cols(d_cw).reshape(1, 4, LRU_BLOCK_W),
        lru_conv_b=cols(d_vec[0:1]), lru_w_a=d_wa.reshape(lru_w_a.shape), lru_b_a=cols(d_vec[1:2]),
        lru_w_x=d_wx.reshape(lru_w_x.shape), lru_b_x=cols(d_vec[2:3]), lru_lambda=cols(d_vec[3:4]),
        final_g=d_final_g)
    small = [n for n in WEIGHTS if n not in BIG]
    as2d = lambda a: a.reshape(-1, a.shape[-1])
    outs = _adamw_many("adamw_small", [tuple(as2d(a) for a in given[n]) + (as2d(small_grads[n]),) for n in small])
    for n, group in zip(small, outs):
        results[n] = tuple(o.reshape(given[n][0].shape) for o in group)

    grad_x = dx0.reshape(x.shape)
    out = [loss, grad_x]
    for j in range(4):
        out += [results[n][j] for n in WEIGHTS]
    return tuple(out)
```

```python
import functools
import math

import jax
import jax.numpy as jnp
from jax import lax
from jax.experimental import pallas as pl
from jax.experimental.pallas import tpu as pltpu

F32 = jnp.float32
BF16 = jnp.bfloat16
HI = lax.Precision.HIGHEST
MESH = pl.DeviceIdType.MESH

N_DEV = 8
D_MODEL = 1024
HEAD_DIM = 64
N_HEADS = 8
KV_GROUP = 4
BLOCK = 128
REL_BUCKETS = 32
REL_MAX_EXACT = 16
REL_MAX_DIST = 128
LRU_BLOCKS = 8
LRU_BLOCK_W = 128
LRU_C = 8.0
EPS = 1e-6
SCALE = HEAD_DIM ** -0.5
NEG = -1e30

ADAM_LR = 0.001
ADAM_B1 = 0.9
ADAM_B2 = 0.999
ADAM_EPS = 1e-08
ADAM_WD = 0.01
ADAM_STEP = 10

C_BQ, C_BK, C_BV, C_AQ, C_GATE, C_AK, C_AV = 0, 512, 1024, 1536, 2048, 3072, 3200
N_MAIN = 3328
SHARD_W_IN = 417
SHARD_W_PAD = 512

TM = 256
TQ = 256
TK = 128
TKB = 256
TC = 256
SWA_SUB = 2
VMEM_BIG = 56 * 1024 * 1024
VMEM_MID = 40 * 1024 * 1024


def _pallas(body, **kw):
    return pl.pallas_call(body, **kw)


def _cp(sem=None, vmem=None):
    kw = {}
    if sem is not None:
        kw["dimension_semantics"] = sem
    if vmem is not None:
        kw["vmem_limit_bytes"] = vmem
    return pltpu.CompilerParams(**kw)


def _nn(a, b, precision=None):
    return jnp.dot(a, b, preferred_element_type=F32, precision=precision)


def _nt(a, b, precision=None):
    return lax.dot_general(a, b, (((1,), (1,)), ((), ())), preferred_element_type=F32, precision=precision)


def _tn(a, b, precision=None):
    return lax.dot_general(a, b, (((0,), (0,)), ((), ())), preferred_element_type=F32, precision=precision)


def _sigmoid(x):
    return 1.0 / (1.0 + jnp.exp(-x))


def _silu(x):
    return x * _sigmoid(x)


def _dsilu(x):
    s = _sigmoid(x)
    return s * (1.0 + x * (1.0 - s))


def _neg_expm1(x):
    poly = x * (1.0 + x * (0.5 + x * (1.0 / 6.0 + x * (1.0 / 24.0))))
    return -jnp.where(jnp.abs(x) < 0.05, poly, jnp.exp(x) - 1.0)


def _col(tile, idx):
    lane = lax.broadcasted_iota(jnp.int32, tile.shape, 1)
    return jnp.sum(jnp.where(lane == idx, tile, 0.0), axis=1, keepdims=True)


def _row(tile, idx):
    sub = lax.broadcasted_iota(jnp.int32, tile.shape, 0)
    return jnp.sum(jnp.where(sub == idx, tile, 0.0), axis=0, keepdims=True)


def _exchange(name, gathers, scatters, axes=("x", "y", "c"), chunks=1):
    ng, n = len(gathers), len(gathers) + len(scatters)
    ins = list(gathers) + list(scatters)
    group = 2 ** len(axes)

    def body(*refs):
        in_refs, out_refs = refs[:n], refs[n:2 * n]
        send_sems, recv_sems, loc_sems = refs[2 * n:]
        coord = {a: lax.axis_index(a) for a in ("x", "y", "c")}

        def member(r):
            pc = dict(coord)
            idx = 0
            for k, a in enumerate(axes):
                if r & (1 << (len(axes) - 1 - k)):
                    pc[a] = 1 - coord[a]
                idx = 2 * idx + pc[a]
            return (pc["x"], pc["y"], pc["c"]), idx

        _, me = member(0)

        def peer(r):
            return member(r)

        local, sends, recvs = [], [], []
        for k in range(n):
            mine = in_refs[k] if k < ng else in_refs[k].at[me]
            cp = pltpu.make_async_copy(mine, out_refs[k].at[me], loc_sems.at[k])
            cp.start()
            local.append(cp)
            lead = mine.shape[0]
            nchunk = max(q for q in range(1, chunks + 1) if lead % q == 0)
            step = lead // nchunk
            for r in range(1, group):
                pid, pidx = peer(r)
                src = in_refs[k] if k < ng else in_refs[k].at[pidx]
                for q in range(nchunk):
                    rows = pl.ds(q * step, step)
                    sems = dict(send_sem=send_sems.at[r - 1, k, q], recv_sem=recv_sems.at[r - 1, k, q],
                                device_id=pid, device_id_type=MESH)
                    snd = pltpu.make_async_remote_copy(src_ref=src.at[rows], dst_ref=out_refs[k].at[me].at[rows], **sems)
                    snd.start()
                    sends.append(snd)
                    recvs.append(pltpu.make_async_remote_copy(
                        src_ref=src.at[rows], dst_ref=out_refs[k].at[pidx].at[rows], **sems))
        for rc in recvs:
            rc.wait_recv()
        for snd in sends:
            snd.wait_send()
        for cp in local:
            cp.wait()

    out_shape = [jax.ShapeDtypeStruct((group,) + a.shape, a.dtype) for a in gathers]
    out_shape += [jax.ShapeDtypeStruct(a.shape, a.dtype) for a in scatters]
    any_spec = pl.BlockSpec(memory_space=pl.ANY)
    return _pallas(
        body, name=name, out_shape=out_shape,
        in_specs=[any_spec] * n, out_specs=[any_spec] * n,
        scratch_shapes=[pltpu.SemaphoreType.DMA((group - 1, n, chunks)), pltpu.SemaphoreType.DMA((group - 1, n, chunks)),
                        pltpu.SemaphoreType.DMA((n,))],
    )(*ins)


def _peer_of(r):
    x, y, c = lax.axis_index("x"), lax.axis_index("y"), lax.axis_index("c")
    px = 1 - x if r & 4 else x
    py = 1 - y if r & 2 else y
    pc = 1 - c if r & 1 else c
    return (px, py, pc), 4 * px + 2 * py + pc


def _split_copies(in_refs, land_refs, send_sems, recv_sems, ng, with_recv):
    _, me = _peer_of(0)
    pairs = []
    for k, (src_ref, land) in enumerate(zip(in_refs, land_refs)):
        for r in range(1, N_DEV):
            pid, pidx = _peer_of(r)
            src = src_ref if k < ng else src_ref.at[pidx]
            slot = (N_DEV - 1) * k + r - 1
            sems = dict(send_sem=send_sems.at[slot], recv_sem=recv_sems.at[slot], device_id=pid, device_id_type=MESH)
            send = pltpu.make_async_remote_copy(src_ref=src, dst_ref=land.at[me], **sems)
            recv = pltpu.make_async_remote_copy(src_ref=src, dst_ref=land.at[pidx], **sems) if with_recv else None
            pairs.append((send, recv))
    return pairs


def _exchange_start(name, gathers, scatters, after):
    ng, n = len(gathers), len(gathers) + len(scatters)
    ins = list(gathers) + list(scatters)
    lands = [jax.ShapeDtypeStruct((N_DEV,) + a.shape, a.dtype) for a in gathers]
    lands += [jax.ShapeDtypeStruct(a.shape, a.dtype) for a in scatters]

    def body(*refs):
        in_refs, land_refs = refs[:n], refs[n:2 * n]
        send_sems, recv_sems = refs[2 * n + 1:2 * n + 3]
        token = refs[-1]
        for send, _ in _split_copies(in_refs, land_refs, send_sems, recv_sems, ng, False):
            send.start()
        token[...] = jnp.zeros_like(token)

    hbm = pl.BlockSpec(memory_space=pltpu.HBM)
    sem = pl.BlockSpec(memory_space=pltpu.SEMAPHORE)
    sem_shape = pltpu.SemaphoreType.DMA(((N_DEV - 1) * n,))
    out_shape = [sem_shape, sem_shape] + [pltpu.HBM(a.shape, a.dtype) for a in ins]
    out_shape += [pltpu.HBM(l.shape, l.dtype) for l in lands] + [jax.ShapeDtypeStruct((8, 128), F32)]
    args = [pltpu.with_memory_space_constraint(a, pltpu.HBM) for a in ins]
    args += [pltpu.with_memory_space_constraint(lax.empty(l.shape, l.dtype), pltpu.HBM) for l in lands]
    outs = _pallas(
        body, name=name, out_shape=out_shape,
        in_specs=[hbm] * (2 * n) + [pl.BlockSpec(memory_space=pl.ANY)],
        out_specs=[sem, sem] + [hbm] * (2 * n) + [pl.BlockSpec(memory_space=pltpu.VMEM)],
        input_output_aliases={i: 2 + i for i in range(2 * n)},
        compiler_params=pltpu.CompilerParams(has_side_effects=pltpu.SideEffectType.DATAFLOW_SIDE_EFFECTING),
    )(*args, after)
    return (outs[0], outs[1], list(outs[2:2 + n]), list(outs[2 + n:2 + 2 * n]), ng), outs[-1]


def _exchange_wait(name, handle, after):
    send_sems, recv_sems, srcs, lands, ng = handle
    n = len(srcs)

    def body(*refs):
        in_refs, land_refs = refs[:n], refs[n:2 * n]
        send_ref, recv_ref = refs[2 * n:2 * n + 2]
        for send, recv in _split_copies(in_refs, land_refs, send_ref, recv_ref, ng, True):
            send.wait_send()
            recv.wait_recv()

    hbm = pl.BlockSpec(memory_space=pltpu.HBM)
    sem = pl.BlockSpec(memory_space=pltpu.SEMAPHORE)
    outs = _pallas(
        body, name=name, out_shape=[pltpu.HBM(a.shape, a.dtype) for a in srcs + lands],
        in_specs=[hbm] * (2 * n) + [sem, sem, pl.BlockSpec(memory_space=pl.ANY)],
        out_specs=[hbm] * (2 * n), input_output_aliases={i: i for i in range(2 * n)},
        compiler_params=pltpu.CompilerParams(has_side_effects=pltpu.SideEffectType.DATAFLOW_SIDE_EFFECTING),
    )(*srcs, *lands, send_sems, recv_sems, after)
    return list(outs[n:])


def _with_own(land, own, me):
    return lax.dynamic_update_slice(land, own[None], (me,) + (0,) * own.ndim)


def _ada_mod(c_all, ada_w, ada_b_slice):
    def body(c_ref, w_ref, b_ref, o_ref):
        ca = _silu(c_ref[...])
        for l in range(2):
            o_ref[l] = _nn(ca, w_ref[l], HI) + b_ref[l]

    return _pallas(body, name="ada_mod",
                   out_shape=jax.ShapeDtypeStruct((2, c_all.shape[0], ada_w.shape[2]), F32),
                   compiler_params=_cp(vmem=VMEM_MID))(c_all, ada_w, ada_b_slice)


def _ada_w_grad(c_all, dmod_slice):
    def body(c_ref, d_ref, o_ref):
        ca = _silu(c_ref[...])
        for l in range(2):
            o_ref[l] = _tn(ca, d_ref[l], HI)

    return _pallas(body, name="ada_w_grad",
                   out_shape=jax.ShapeDtypeStruct((2, D_MODEL, dmod_slice.shape[2]), F32),
                   compiler_params=_cp(vmem=VMEM_MID))(c_all, dmod_slice)


def _bucket_onehot():
    qi = jnp.arange(BLOCK)[:, None]
    kj = jnp.arange(2 * BLOCK)[None, :]
    rel = qi - kj + BLOCK
    n = jnp.maximum(rel, 0)
    nf = jnp.maximum(n, 1).astype(F32)
    large = REL_MAX_EXACT + (jnp.log(nf / REL_MAX_EXACT) / math.log(REL_MAX_DIST / REL_MAX_EXACT)
                             * (REL_BUCKETS - REL_MAX_EXACT)).astype(jnp.int32)
    large = jnp.minimum(large, REL_BUCKETS - 1)
    bucket = jnp.where(n < REL_MAX_EXACT, n, large).reshape(1, BLOCK * 2 * BLOCK)
    return (jnp.arange(REL_BUCKETS)[:, None] == bucket).astype(F32)


def _bias_expand(rel_bias_t, onehot):
    def body(r_ref, e_ref, o_ref):
        o_ref[...] = _nn(r_ref[...], e_ref[...], HI)

    return _pallas(body, name="bias_expand",
                   out_shape=jax.ShapeDtypeStruct((N_HEADS, onehot.shape[1]), F32),
                   compiler_params=_cp(vmem=VMEM_MID))(rel_bias_t, onehot)


def _bias_reduce(dbias, onehot):
    def body(d_ref, e_ref, o_ref):
        o_ref[...] = _nt(d_ref[...], e_ref[...], HI)

    return _pallas(body, name="bias_reduce",
                   out_shape=jax.ShapeDtypeStruct((N_HEADS, REL_BUCKETS), F32),
                   compiler_params=_cp(vmem=VMEM_MID))(dbias, onehot)


def _norm_proj(name, x, g, shift, scale, w, seq, out_dtype, wf_t=None):
    t_tok = x.shape[0]
    w3d = w.ndim == 3
    n_out = w.shape[0] * w.shape[2] if w3d else w.shape[1]
    cn = w.shape[2] if w3d else 256

    def body(x_ref, g_ref, sh_ref, sc_ref, w_ref, *rest):
        if wf_t is not None:
            wf_ref, h_ref, o_ref, fl_ref = rest
        else:
            h_ref, o_ref = rest
        xv = x_ref[...]
        rstd = lax.rsqrt(jnp.mean(xv * xv, axis=-1, keepdims=True) + EPS)
        h = (xv * rstd) * g_ref[...] * (1.0 + sc_ref[...]) + sh_ref[...]
        hb = h.astype(BF16)
        h_ref[...] = hb
        for j in range(n_out // cn):
            wj = w_ref[j] if w3d else w_ref[:, j * cn:(j + 1) * cn]
            o_ref[:, j * cn:(j + 1) * cn] = _nn(hb, wj).astype(out_dtype)
        if wf_t is not None:
            fl_ref[...] = _nt(wf_ref[...], hb)

    mod_spec = pl.BlockSpec((None, 1, D_MODEL), lambda i: (i * TM // seq, 0, 0))
    w_spec = (pl.BlockSpec(w.shape, lambda i: (0, 0, 0)) if w3d else pl.BlockSpec(w.shape, lambda i: (0, 0)))
    in_specs = [pl.BlockSpec((TM, D_MODEL), lambda i: (i, 0)), pl.BlockSpec((1, D_MODEL), lambda i: (0, 0)),
                mod_spec, mod_spec, w_spec]
    out_shape = [jax.ShapeDtypeStruct((t_tok, D_MODEL), BF16), jax.ShapeDtypeStruct((t_tok, n_out), out_dtype)]
    out_specs = [pl.BlockSpec((TM, D_MODEL), lambda i: (i, 0)), pl.BlockSpec((TM, n_out), lambda i: (i, 0))]
    args = [x, g, shift, scale, w]
    if wf_t is not None:
        in_specs.append(pl.BlockSpec(wf_t.shape, lambda i: (0, 0)))
        out_shape.append(jax.ShapeDtypeStruct((wf_t.shape[0], t_tok), F32))
        out_specs.append(pl.BlockSpec((wf_t.shape[0], TM), lambda i: (0, i)))
        args.append(wf_t)
    return _pallas(body, name=name, grid=(t_tok // TM,), in_specs=in_specs, out_specs=out_specs,
                   out_shape=out_shape, compiler_params=_cp(("arbitrary",), VMEM_BIG))(*args)


def _fox_prep(fl_t, b_f, seq):
    t_tok = fl_t.shape[1]
    ch = 256

    def body(fl_ref, bf_ref, fr_ref, fc_ref):
        z = fl_ref[...] + bf_ref[...]
        logf = jnp.minimum(z, 0.0) - jnp.log(1.0 + jnp.exp(-jnp.abs(z)))
        ri = lax.broadcasted_iota(jnp.int32, (ch, ch), 0)
        ci = lax.broadcasted_iota(jnp.int32, (ch, ch), 1)
        upper = (ri <= ci).astype(F32)
        eye = (ri == ci).astype(F32)
        carry = jnp.zeros((N_HEADS, 1), F32)
        for k in range(seq // ch):
            fk = _nn(logf[:, k * ch:(k + 1) * ch], upper, HI) + carry
            carry = fk[:, ch - 1:ch]
            fr_ref[:, k * ch:(k + 1) * ch] = fk
            padded = jnp.concatenate([fk, jnp.zeros((128 - N_HEADS, ch), F32)], axis=0)
            fc_ref[k * ch:(k + 1) * ch, :] = _nt(eye, padded, HI)

    return _pallas(
        body, name="fox_prep", grid=(t_tok // seq,),
        in_specs=[pl.BlockSpec((N_HEADS, seq), lambda b: (0, b)), pl.BlockSpec((N_HEADS, 1), lambda b: (0, 0))],
        out_specs=[pl.BlockSpec((N_HEADS, seq), lambda b: (0, b)), pl.BlockSpec((seq, 128), lambda b: (b, 0))],
        out_shape=[jax.ShapeDtypeStruct((N_HEADS, t_tok), F32), jax.ShapeDtypeStruct((t_tok, 128), F32)],
        compiler_params=_cp(("arbitrary",), VMEM_MID))(fl_t, b_f)


def _fox_post(df_row, fl_t, b_f, seq):
    t_tok = fl_t.shape[1]
    ch = 256

    def body(d_ref, fl_ref, bf_ref, o_ref, db_ref):
        @pl.when(pl.program_id(0) == 0)
        def _():
            db_ref[...] = jnp.zeros_like(db_ref)

        z = fl_ref[...] + bf_ref[...]
        sig_neg = 1.0 / (1.0 + jnp.exp(z))
        ri = lax.broadcasted_iota(jnp.int32, (ch, ch), 0)
        ci = lax.broadcasted_iota(jnp.int32, (ch, ch), 1)
        lower = (ri >= ci).astype(F32)
        carry = jnp.zeros((N_HEADS, 1), F32)
        tot = jnp.zeros((N_HEADS, 1), F32)
        for k in reversed(range(seq // ch)):
            dk = _nn(d_ref[:, k * ch:(k + 1) * ch], lower, HI) + carry
            carry = dk[:, 0:1]
            dfl = dk * sig_neg[:, k * ch:(k + 1) * ch]
            o_ref[:, k * ch:(k + 1) * ch] = dfl
            tot = tot + jnp.sum(dfl, axis=1, keepdims=True)
        db_ref[...] += jnp.broadcast_to(tot, db_ref.shape)

    return _pallas(
        body, name="fox_post", grid=(t_tok // seq,),
        in_specs=[pl.BlockSpec((N_HEADS, seq), lambda b: (0, b)), pl.BlockSpec((N_HEADS, seq), lambda b: (0, b)),
                  pl.BlockSpec((N_HEADS, 1), lambda b: (0, 0))],
        out_specs=[pl.BlockSpec((N_HEADS, seq), lambda b: (0, b)), pl.BlockSpec((N_HEADS, 128), lambda b: (0, 0))],
        out_shape=[jax.ShapeDtypeStruct((N_HEADS, t_tok), F32), jax.ShapeDtypeStruct((N_HEADS, 128), F32)],
        compiler_params=_cp(("arbitrary",), VMEM_MID))(df_row, fl_t, b_f)


def _eye(n, dtype):
    return (lax.broadcasted_iota(jnp.int32, (n, n), 0) == lax.broadcasted_iota(jnp.int32, (n, n), 1)).astype(dtype)


def _fox_aug(qkvg, f_col, seq):
    t_tok = qkvg.shape[0]
    ta = 256
    nkb = ta // TK

    def body(q_ref, k_ref, v_ref, fc_ref, qa_ref, ka_ref, kt_ref, vt_ref, qt_ref):
        ri = lax.broadcasted_iota(jnp.int32, (128, 128), 0)
        ci = lax.broadcasted_iota(jnp.int32, (128, 128), 1)
        eye = (ri == ci).astype(BF16)
        lane = lax.broadcasted_iota(jnp.int32, (ta, 128), 1)
        ones_q = jnp.where(jnp.logical_and(lane >= 64, lane < 67), 1.0, 0.0)
        ones_k = jnp.where(jnp.logical_and(lane >= 67, lane < 70), 1.0, 0.0)
        fc_tile = fc_ref[...]
        for p in range(N_HEADS // 2):
            q2 = q_ref[:, 128 * p:128 * (p + 1)]
            k2 = k_ref[:, 128 * p:128 * (p + 1)]
            vt = _nt(eye, v_ref[:, 128 * p:128 * (p + 1)]).astype(BF16)
            for kk in range(nkb):
                vt_ref[p, kk] = vt[:, kk * TK:(kk + 1) * TK]
            for e in range(2):
                h = 2 * p + e
                sel = jnp.logical_and(ri == ci + HEAD_DIM * e, ci < HEAD_DIM)
                f = _col(fc_tile, h)
                fh = f.astype(BF16).astype(F32)
                fm = (f - fh).astype(BF16).astype(F32)
                fl = (f - fh - fm).astype(BF16).astype(F32)
                qa = (_nn(q2, jnp.where(sel, SCALE, 0.0).astype(BF16)) + ones_q + jnp.where(lane == 67, fh, 0.0)
                      + jnp.where(lane == 68, fm, 0.0) + jnp.where(lane == 69, fl, 0.0))
                ka = (_nn(k2, jnp.where(sel, 1.0, 0.0).astype(BF16)) + ones_k - jnp.where(lane == 64, fh, 0.0)
                      - jnp.where(lane == 65, fm, 0.0) - jnp.where(lane == 66, fl, 0.0))
                qab = qa.astype(BF16)
                qa_ref[h] = qab
                qt = _nt(eye, qab).astype(BF16)
                for kk in range(ta // TQ):
                    qt_ref[h, kk] = qt[:, kk * TQ:(kk + 1) * TQ]
                kab = ka.astype(BF16)
                ka_ref[h] = kab
                kt = _nt(eye, kab).astype(BF16)
                for kk in range(ta // TKB):
                    kt_ref[h, kk] = kt[:, kk * TKB:(kk + 1) * TKB]

    aug = jax.ShapeDtypeStruct((N_HEADS, t_tok, 128), BF16)
    return _pallas(
        body, name="fox_aug", grid=(t_tok // ta,),
        in_specs=[pl.BlockSpec((ta, 512), lambda i: (i, C_BQ // 512)), pl.BlockSpec((ta, 512), lambda i: (i, C_BK // 512)),
                  pl.BlockSpec((ta, 512), lambda i: (i, C_BV // 512)), pl.BlockSpec((ta, 128), lambda i: (i, 0))],
        out_specs=[pl.BlockSpec((N_HEADS, ta, 128), lambda i: (0, i, 0)), pl.BlockSpec((N_HEADS, ta, 128), lambda i: (0, i, 0)),
                   pl.BlockSpec((N_HEADS, ta // TKB, 128, TKB), lambda i: (0, i, 0, 0)),
                   pl.BlockSpec((N_HEADS // 2, nkb, 128, TK), lambda i: (0, i, 0, 0)),
                   pl.BlockSpec((N_HEADS, ta // TQ, 128, TQ), lambda i: (0, i, 0, 0))],
        out_shape=[aug, aug, jax.ShapeDtypeStruct((N_HEADS, t_tok // TKB, 128, TKB), BF16),
                   jax.ShapeDtypeStruct((N_HEADS // 2, t_tok // TK, 128, TK), BF16),
                   jax.ShapeDtypeStruct((N_HEADS, t_tok // TQ, 128, TQ), BF16)],
        compiler_params=_cp(("arbitrary",), VMEM_MID))(qkvg, qkvg, qkvg, f_col)


def _fox_fwd_t(qt, k_aug, vt, seq):
    t_tok = k_aug.shape[1]
    nq = seq // TQ
    ratio = TQ // TK

    def body(qt_ref, ka_ref, vt_ref, o_ref, lse_ref, ml_s, acc_s, st_s, p_s, al_s):
        i = pl.program_id(1)
        tpos = i * TQ + lax.broadcasted_iota(jnp.int32, (1, TQ), 1)
        eye = _eye(HEAD_DIM, BF16)
        for h in range(N_HEADS):
            ml_s[0, h] = jnp.full((1, TQ), NEG, F32)
            ml_s[1, h] = jnp.zeros((1, TQ), F32)
            acc_s[h] = jnp.zeros((HEAD_DIM, TQ), F32)
            p_s[1, h] = jnp.zeros((TK, TQ), BF16)
            al_s[1, h] = jnp.ones((1, TQ), F32)

        def scores(j):
            row0 = pl.multiple_of(j * TK, TK)
            for h in range(N_HEADS):
                st_s[j & 1, h] = _nn(ka_ref[h, pl.ds(row0, TK), :], qt_ref[h, 0])

        def softmax(j, masked):
            slot = j & 1
            if masked:
                keep = (j * TK + lax.broadcasted_iota(jnp.int32, (TK, 1), 0)) <= tpos
            for h in range(N_HEADS):
                st = st_s[slot, h]
                if masked:
                    st = jnp.where(keep, st, NEG)
                m = ml_s[0, h]
                m_new = jnp.maximum(m, jnp.max(st, axis=0, keepdims=True))
                alpha = jnp.exp(m - m_new)
                pe = jnp.exp(st - m_new)
                ml_s[0, h] = m_new
                ml_s[1, h] = alpha * ml_s[1, h] + jnp.sum(pe, axis=0, keepdims=True)
                al_s[slot, h] = alpha
                p_s[slot, h] = pe.astype(BF16)

        def values(j):
            slot = j & 1
            jv = jnp.maximum(j, 0)
            for h in range(N_HEADS):
                p, e = divmod(h, 2)
                acc_s[h] = al_s[slot, h] * acc_s[h] + _nn(vt_ref[p, jv, e * HEAD_DIM:(e + 1) * HEAD_DIM, :], p_s[slot, h])

        def step(j, carry):
            values(j - 1)
            softmax(j, False)
            scores(j + 1)
            return carry

        last = ratio * i + ratio - 1
        scores(0)
        lax.fori_loop(0, ratio * i, step, 0)
        for kk in range(ratio):
            j = ratio * i + kk
            values(j - 1)
            softmax(j, True)
            if kk < ratio - 1:
                scores(j + 1)
        values(last)
        for p in range(N_HEADS // 2):
            outs = []
            for e in range(2):
                h = 2 * p + e
                l = ml_s[1, h]
                outs.append(_tn((acc_s[h] / l).astype(BF16), eye))
                lse_ref[p, e:e + 1, :] = ml_s[0, h] + jnp.log(l)
            o_ref[:, 128 * p:128 * (p + 1)] = jnp.concatenate(outs, axis=1).astype(BF16)

    return _pallas(
        body, name="fox_fwd", grid=(t_tok // seq, nq),
        in_specs=[pl.BlockSpec((N_HEADS, 1, 128, TQ), lambda b, i: (0, b * nq + i, 0, 0)),
                  pl.BlockSpec((N_HEADS, seq, 128), lambda b, i: (0, b, 0)),
                  pl.BlockSpec((N_HEADS // 2, seq // TK, 128, TK), lambda b, i: (0, b, 0, 0))],
        out_specs=[pl.BlockSpec((TQ, 512), lambda b, i: (b * nq + i, 0)),
                   pl.BlockSpec((N_HEADS // 2, 2, TQ), lambda b, i: (0, 0, b * nq + i))],
        out_shape=[jax.ShapeDtypeStruct((t_tok, 512), BF16), jax.ShapeDtypeStruct((N_HEADS // 2, 2, t_tok), F32)],
        scratch_shapes=[pltpu.VMEM((2, N_HEADS, 1, TQ), F32), pltpu.VMEM((N_HEADS, HEAD_DIM, TQ), F32),
                        pltpu.VMEM((2, N_HEADS, TK, TQ), F32), pltpu.VMEM((2, N_HEADS, TK, TQ), BF16),
                        pltpu.VMEM((2, N_HEADS, 1, TQ), F32)],
        compiler_params=_cp(("arbitrary", "arbitrary"), VMEM_MID))(qt, k_aug, vt)


def _fox_bwd_t(q_aug, k_aug, kt, qt, qkvg, du_b, b_out, lse, seq):
    TK = TKB
    t_tok = qkvg.shape[0]
    nq = seq // TQ
    nkb = seq // TK
    ratio = TQ // TK
    hg = 4

    def body(qa_ref, ka_ref, kt_ref, qt_ref, v_ref, do_ref, o_ref, lse_ref, dq_ref, dk_ref, dv_ref, df_ref,
             dqt_s, row_s, dfk_s, dk_s, dv_s, dot_s, st_s, dp_s, pb_s, db_s):
        eye = _eye(HEAD_DIM, BF16)
        eye2 = _eye(128, BF16)
        eye_k = _eye(TK, F32)
        lane8 = lax.broadcasted_iota(jnp.int32, (8, 128), 1)
        lane_k = lax.broadcasted_iota(jnp.int32, (TK, 128), 1)
        first = [lane8 < HEAD_DIM, lane8 >= HEAD_DIM]
        for pp in range(hg // 2):
            for ii in range(nq):
                dot_s[pp, ii] = _nt(eye2, do_ref[ii * TQ:(ii + 1) * TQ, 128 * pp:128 * (pp + 1)]).astype(BF16)
        for hh in range(hg):
            pp, e = divmod(hh, 2)
            head_lanes = jnp.where(first[e], 1.0, 0.0)
            for ii in range(nq):
                rows = slice(ii * TQ, (ii + 1) * TQ)
                prod = do_ref[rows, 128 * pp:128 * (pp + 1)].astype(F32) * o_ref[rows, 128 * pp:128 * (pp + 1)].astype(F32)
                row_s[hh, ii, 0] = _nt(head_lanes, prod, HI)
                row_s[hh, ii, 1] = jnp.broadcast_to(lse_ref[pp, e:e + 1, ii * TQ:(ii + 1) * TQ], (8, TQ))
                dqt_s[hh, ii] = jnp.zeros((128, TQ), F32)

        def kblock(j, _):
            krow = pl.multiple_of(j * TK, TK)
            spos = j * TK + lax.broadcasted_iota(jnp.int32, (TK, 1), 0)
            for hh in range(hg):
                dk_s[hh] = jnp.zeros((TK, 128), F32)
                dv_s[hh] = jnp.zeros((TK, 128), F32)

            def scores(i):
                for hh in range(hg):
                    pp, e = divmod(hh, 2)
                    own = (lane_k < HEAD_DIM) if e == 0 else (lane_k >= HEAD_DIM)
                    v2 = v_ref[pl.ds(krow, TK), 128 * pp:128 * (pp + 1)]
                    vj = jnp.where(own, v2, jnp.zeros_like(v2))
                    st_s[i & 1, hh] = _nn(ka_ref[hh, pl.ds(krow, TK), :], qt_ref[hh, i])
                    dp_s[i & 1, hh] = _nn(vj, dot_s[pp, i])

            def elementwise(i, masked):
                slot = i & 1
                if masked:
                    keep = spos <= (i * TQ + lax.broadcasted_iota(jnp.int32, (1, TQ), 1))
                for hh in range(hg):
                    pt = jnp.exp(st_s[slot, hh] - row_s[hh, i, 1][0:1, :])
                    if masked:
                        pt = jnp.where(keep, pt, 0.0)
                    dst = pt * (dp_s[slot, hh] - row_s[hh, i, 0][0:1, :])
                    pb_s[slot, hh] = pt.astype(BF16)
                    db_s[slot, hh] = dst.astype(BF16)

            def grads(i):
                slot = i & 1
                qrow = pl.multiple_of(i * TQ, TQ)
                for hh in range(hg):
                    dst_b = db_s[slot, hh]
                    dv_s[hh] += _nn(pb_s[slot, hh], do_ref[pl.ds(qrow, TQ), 128 * (hh // 2):128 * (hh // 2 + 1)])
                    dk_s[hh] += _nn(dst_b, qa_ref[hh, pl.ds(qrow, TQ), :])
                    dqt_s[hh, i] += _nn(kt_ref[hh, j], dst_b)

            def step(i, carry):
                grads(i - 1)
                elementwise(i, False)
                scores(jnp.minimum(i + 1, nq - 1))
                return carry

            i0 = j // ratio
            scores(i0)
            elementwise(i0, True)
            scores(jnp.minimum(i0 + 1, nq - 1))
            lax.fori_loop(i0 + 1, nq, step, 0)
            grads(nq - 1)
            for pp in range(hg // 2):
                cols = slice(128 * pp, 128 * (pp + 1))
                dk_ref[pl.ds(krow, TK), cols] = jnp.concatenate(
                    [dk_s[2 * pp][:, :HEAD_DIM], dk_s[2 * pp + 1][:, :HEAD_DIM]], axis=1).astype(BF16)
                dv_ref[pl.ds(krow, TK), cols] = jnp.where(lane_k < HEAD_DIM, dv_s[2 * pp], dv_s[2 * pp + 1]).astype(BF16)
            for hh in range(hg):
                dfk_s[hh, j] = _tn(dk_s[hh][:, HEAD_DIM:HEAD_DIM + 8], eye_k, HI)
            return 0

        lax.fori_loop(0, nkb, kblock, 0)
        for pp in range(hg // 2):
            for ii in range(nq):
                parts = []
                for e in range(2):
                    dqt = dqt_s[2 * pp + e, ii]
                    parts.append(_tn(dqt[0:HEAD_DIM, :].astype(BF16), eye) * SCALE)
                    for kk in range(ratio):
                        jj = ii * ratio + kk
                        df_ref[pp, e:e + 1, jj * TK:(jj + 1) * TK] = (dqt[67:68, kk * TK:(kk + 1) * TK]
                                                                     - dfk_s[2 * pp + e, jj][0:1, :])
                dq_ref[ii * TQ:(ii + 1) * TQ, 128 * pp:128 * (pp + 1)] = jnp.concatenate(parts, axis=1).astype(BF16)

    aug_blk = pl.BlockSpec((hg, seq, 128), lambda b, g: (g, b, 0))
    pair_blk = pl.BlockSpec((seq, 64 * hg), lambda b, g: (b, g))
    row_blk = pl.BlockSpec((hg // 2, 2, seq), lambda b, g: (g, 0, b))
    return _pallas(
        body, name="fox_bwd", grid=(t_tok // seq, N_HEADS // hg),
        in_specs=[aug_blk, aug_blk, pl.BlockSpec((hg, nkb, 128, TK), lambda b, g: (g, b, 0, 0)),
                  pl.BlockSpec((hg, nq, 128, TQ), lambda b, g: (g, b, 0, 0)),
                  pl.BlockSpec((seq, 64 * hg), lambda b, g: (b, C_BV // (64 * hg) + g)), pair_blk, pair_blk, row_blk],
        out_specs=[pair_blk, pair_blk, pair_blk, row_blk],
        out_shape=[jax.ShapeDtypeStruct((t_tok, 512), BF16)] * 3
        + [jax.ShapeDtypeStruct((N_HEADS // 2, 2, t_tok), F32)],
        scratch_shapes=[pltpu.VMEM((hg, nq, 128, TQ), F32), pltpu.VMEM((hg, nq, 2, 8, TQ), F32),
                        pltpu.VMEM((hg, nkb, 8, TK), F32), pltpu.VMEM((hg, TK, 128), F32),
                        pltpu.VMEM((hg, TK, 128), F32), pltpu.VMEM((hg // 2, nq, 128, TQ), BF16),
                        pltpu.VMEM((2, hg, TK, TQ), F32), pltpu.VMEM((2, hg, TK, TQ), F32),
                        pltpu.VMEM((2, hg, TK, TQ), BF16), pltpu.VMEM((2, hg, TK, TQ), BF16)],
        compiler_params=_cp(("arbitrary", "arbitrary"), VMEM_BIG))(q_aug, k_aug, kt, qt, qkvg, du_b, b_out, lse)


def _fox_bwd_t_old(q_aug, k_aug, kt, qkvg, du_b, b_out, lse, seq):
    t_tok = qkvg.shape[0]
    nq = seq // TQ
    nkb = seq // TK
    ratio = TQ // TK

    def body(qa_ref, ka_ref, kt_ref, v_ref, do_ref, o_ref, lse_ref, dq_ref, dk_ref, dv_ref, df_ref,
             dqt_s, out_s, row_s, dfk_s):
        ones_b = jnp.ones((8, TQ), BF16)
        ones_f = jnp.ones((8, HEAD_DIM), F32)
        eye = _eye(HEAD_DIM, BF16)
        for e in range(2):
            lo, hi = e * HEAD_DIM, (e + 1) * HEAD_DIM
            for ii in range(nq):
                rows = slice(ii * TQ, (ii + 1) * TQ)
                do = do_ref[rows, :][:, lo:hi].astype(F32)
                ov = o_ref[rows, :][:, lo:hi].astype(F32)
                row_s[ii, 0] = _nt(ones_f, do * ov, HI)
                row_s[ii, 1] = jnp.broadcast_to(lse_ref[e:e + 1, ii * TQ:(ii + 1) * TQ], (8, TQ))
                dqt_s[ii] = jnp.zeros((128, TQ), F32)

            def kblock(j, _):
                krow = pl.multiple_of(j * TK, TK)
                kj = ka_ref[e, pl.ds(krow, TK), :]
                ktj = kt_ref[e, j]
                vj = v_ref[pl.ds(krow, TK), :][:, lo:hi]
                spos = j * TK + lax.broadcasted_iota(jnp.int32, (TK, 1), 0)

                def qblock(i, carry, masked):
                    dk_acc, dv_acc, dfk = carry
                    qrow = pl.multiple_of(i * TQ, TQ)
                    qa = qa_ref[e, pl.ds(qrow, TQ), :]
                    doh = do_ref[pl.ds(qrow, TQ), :][:, lo:hi]
                    pt = jnp.exp(_nt(kj, qa) - row_s[i, 1][0:1, :])
                    if masked:
                        tpos = i * TQ + lax.broadcasted_iota(jnp.int32, (1, TQ), 1)
                        pt = jnp.where(spos <= tpos, pt, 0.0)
                    dst = pt * (_nt(vj, doh) - row_s[i, 0][0:1, :])
                    dst_b = dst.astype(BF16)
                    dv_acc = dv_acc + _nn(pt.astype(BF16), doh)
                    dk_acc = dk_acc + _nn(dst_b, qa)
                    dqt_s[i] += _nn(ktj, dst_b)
                    dfk = dfk + _nt(ones_b, dst_b)
                    return dk_acc, dv_acc, dfk

                i0 = j // ratio
                carry = (jnp.zeros((TK, 128), F32), jnp.zeros((TK, HEAD_DIM), F32), jnp.zeros((8, TK), F32))
                carry = qblock(i0, carry, True)
                dk_acc, dv_acc, dfk = lax.fori_loop(i0 + 1, nq, functools.partial(qblock, masked=False), carry)
                out_s[1, e, pl.ds(krow, TK), :] = dk_acc[:, :HEAD_DIM]
                out_s[2, e, pl.ds(krow, TK), :] = dv_acc
                dfk_s[j] = dfk
                return 0

            lax.fori_loop(0, nkb, kblock, 0)
            for ii in range(nq):
                dqt = dqt_s[ii]
                out_s[0, e, ii * TQ:(ii + 1) * TQ, :] = _tn(dqt[0:HEAD_DIM, :].astype(BF16), eye) * SCALE
                for kk in range(ratio):
                    jj = ii * ratio + kk
                    df_ref[e:e + 1, jj * TK:(jj + 1) * TK] = dqt[67:68, kk * TK:(kk + 1) * TK] - dfk_s[jj][0:1, :]
        for k, ref in enumerate((dq_ref, dk_ref, dv_ref)):
            ref[...] = jnp.concatenate([out_s[k, 0], out_s[k, 1]], axis=1).astype(BF16)

    aug_blk = pl.BlockSpec((2, seq, 128), lambda b, p: (p, b, 0))
    pair_blk = pl.BlockSpec((seq, 128), lambda b, p: (b, p))
    row_blk = pl.BlockSpec((None, 2, seq), lambda b, p: (p, 0, b))
    return _pallas(
        body, name="fox_bwd", grid=(t_tok // seq, N_HEADS // 2),
        in_specs=[aug_blk, aug_blk, pl.BlockSpec((2, nkb, 128, TK), lambda b, p: (p, b, 0, 0)),
                  pl.BlockSpec((seq, 128), lambda b, p: (b, C_BV // 128 + p)), pair_blk, pair_blk, row_blk],
        out_specs=[pair_blk, pair_blk, pair_blk, row_blk],
        out_shape=[jax.ShapeDtypeStruct((t_tok, 512), BF16)] * 3
        + [jax.ShapeDtypeStruct((N_HEADS // 2, 2, t_tok), F32)],
        scratch_shapes=[pltpu.VMEM((nq, 128, TQ), F32), pltpu.VMEM((3, 2, seq, HEAD_DIM), F32),
                        pltpu.VMEM((nq, 2, 8, TQ), F32), pltpu.VMEM((nkb, 8, TK), F32)],
        compiler_params=_cp(("arbitrary", "arbitrary"), VMEM_BIG))(q_aug, k_aug, kt, qkvg, du_b, b_out, lse)


def _fox_fwd(qkvg, f_row, f_col, seq):
    t_tok = qkvg.shape[0]
    nq = seq // TQ

    def body(q_ref, k_ref, v_ref, fr_ref, fc_ref, o_ref, lse_ref, fk_s):
        i = pl.program_id(1)
        for jj in range(nq):
            fk_s[jj] = fr_ref[:, jj * TQ:(jj + 1) * TQ]
        fcol = fc_ref[...]
        tpos = i * TQ + lax.broadcasted_iota(jnp.int32, (TQ, 1), 0)
        lane = lax.broadcasted_iota(jnp.int32, (TQ, 128), 1)
        lse_tile = jnp.zeros((TQ, 128), F32)
        for p in range(N_HEADS // 2):
            q2 = q_ref[:, 128 * p:128 * (p + 1)]
            qs = [q2[:, :HEAD_DIM], q2[:, HEAD_DIM:]]
            fqs = [_col(fcol, 2 * p + e) for e in range(2)]

            def kblock(j, carry):
                row0 = pl.multiple_of(j * TQ, TQ)
                k2 = k_ref[pl.ds(row0, TQ), 128 * p:128 * (p + 1)]
                v2 = v_ref[pl.ds(row0, TQ), 128 * p:128 * (p + 1)]
                fk8 = fk_s[j]
                spos = j * TQ + lax.broadcasted_iota(jnp.int32, (1, TQ), 1)
                keep = spos <= tpos
                new = []
                for e in range(2):
                    m, l, acc = carry[3 * e:3 * e + 3]
                    kh = k2[:, e * HEAD_DIM:(e + 1) * HEAD_DIM]
                    vh = v2[:, e * HEAD_DIM:(e + 1) * HEAD_DIM]
                    s = _nt(qs[e], kh) * SCALE + (fqs[e] - fk8[2 * p + e:2 * p + e + 1, :])
                    s = jnp.where(keep, s, NEG)
                    m_new = jnp.maximum(m, jnp.max(s, axis=1, keepdims=True))
                    alpha = jnp.exp(m - m_new)
                    pe = jnp.exp(s - m_new)
                    l = alpha * l + jnp.sum(pe, axis=1, keepdims=True)
                    acc = alpha * acc + _nn(pe.astype(BF16), vh)
                    new += [m_new, l, acc]
                return tuple(new)

            init = (jnp.full((TQ, 1), NEG, F32), jnp.zeros((TQ, 1), F32), jnp.zeros((TQ, HEAD_DIM), F32)) * 2
            res = lax.fori_loop(0, i + 1, kblock, init)
            outs = []
            for e in range(2):
                m, l, acc = res[3 * e:3 * e + 3]
                outs.append(acc / l)
                lse_tile = jnp.where(lane == 2 * p + e, m + jnp.log(l), lse_tile)
            o_ref[:, 128 * p:128 * (p + 1)] = jnp.concatenate(outs, axis=1).astype(BF16)
        lse_ref[...] = lse_tile

    return _pallas(
        body, name="fox_fwd", grid=(t_tok // seq, nq),
        in_specs=[pl.BlockSpec((TQ, 512), lambda b, i: (b * nq + i, C_BQ // 512)),
                  pl.BlockSpec((seq, 512), lambda b, i: (b, C_BK // 512)),
                  pl.BlockSpec((seq, 512), lambda b, i: (b, C_BV // 512)),
                  pl.BlockSpec((N_HEADS, seq), lambda b, i: (0, b)),
                  pl.BlockSpec((TQ, 128), lambda b, i: (b * nq + i, 0))],
        out_specs=[pl.BlockSpec((TQ, 512), lambda b, i: (b * nq + i, 0)),
                   pl.BlockSpec((TQ, 128), lambda b, i: (b * nq + i, 0))],
        out_shape=[jax.ShapeDtypeStruct((t_tok, 512), BF16), jax.ShapeDtypeStruct((t_tok, 128), F32)],
        scratch_shapes=[pltpu.VMEM((nq, N_HEADS, TQ), F32)],
        compiler_params=_cp(("arbitrary", "arbitrary"), VMEM_MID))(qkvg, qkvg, qkvg, f_row, f_col)


def _fox_bwd(qkvg, du_b, b_out, lse, f_row, f_col, seq):
    t_tok = qkvg.shape[0]
    nq = seq // TQ

    def body(q_ref, k_ref, v_ref, do_ref, o_ref, lse_ref, fr_ref, fc_ref,
             dq_ref, dk_ref, dv_ref, df_ref, dq_s, dk_s, dv_s, col_s, df_s, fk_s):
        p = pl.program_id(1)
        for jj in range(nq):
            fk_s[jj] = fr_ref[:, jj * TQ:(jj + 1) * TQ]
        eye = (lax.broadcasted_iota(jnp.int32, (TQ, TQ), 0) == lax.broadcasted_iota(jnp.int32, (TQ, TQ), 1)).astype(F32)
        for e in range(2):
            h = 2 * p + e
            lo, hi = e * HEAD_DIM, (e + 1) * HEAD_DIM
            for ii in range(nq):
                rows = slice(ii * TQ, (ii + 1) * TQ)
                do = do_ref[rows, :][:, lo:hi].astype(F32)
                ov = o_ref[rows, :][:, lo:hi].astype(F32)
                col_s[0, rows, :] = jnp.sum(do * ov, axis=1, keepdims=True)
                col_s[1, rows, :] = _col(lse_ref[rows, :], h)
                col_s[2, rows, :] = _col(fc_ref[rows, :], h)
                dq_s[rows, :] = jnp.zeros((TQ, HEAD_DIM), F32)
                df_s[ii] = jnp.zeros((8, TQ), F32)
                col_s[3, rows, :] = jnp.zeros((TQ, 1), F32)

            def kblock(j, _):
                krow = pl.multiple_of(j * TQ, TQ)
                kh = k_ref[pl.ds(krow, TQ), :][:, lo:hi]
                vh = v_ref[pl.ds(krow, TQ), :][:, lo:hi]
                fk = _row(fk_s[j], h)
                spos = j * TQ + lax.broadcasted_iota(jnp.int32, (1, TQ), 1)

                def qblock(i, carry):
                    dk_acc, dv_acc, dfk = carry
                    qrow = pl.multiple_of(i * TQ, TQ)
                    qh = q_ref[pl.ds(qrow, TQ), :][:, lo:hi]
                    doh = do_ref[pl.ds(qrow, TQ), :][:, lo:hi]
                    delta = col_s[0, pl.ds(qrow, TQ), :]
                    lse_q = col_s[1, pl.ds(qrow, TQ), :]
                    fq = col_s[2, pl.ds(qrow, TQ), :]
                    tpos = i * TQ + lax.broadcasted_iota(jnp.int32, (TQ, 1), 0)
                    s = _nt(qh, kh) * SCALE + (fq - fk)
                    pr = jnp.where(spos <= tpos, jnp.exp(s - lse_q), 0.0)
                    dp = _nt(doh, vh)
                    ds = pr * (dp - delta)
                    ds_b = ds.astype(BF16)
                    dv_acc = dv_acc + _tn(pr.astype(BF16), doh)
                    dk_acc = dk_acc + _tn(ds_b, qh)
                    dq_s[pl.ds(qrow, TQ), :] += _nn(ds_b, kh)
                    col_s[3, pl.ds(qrow, TQ), :] += jnp.sum(ds, axis=1, keepdims=True)
                    dfk = dfk + jnp.sum(ds, axis=0, keepdims=True)
                    return dk_acc, dv_acc, dfk

                zero = jnp.zeros((TQ, HEAD_DIM), F32)
                dk_acc, dv_acc, dfk = lax.fori_loop(j, nq, qblock, (zero, zero, jnp.zeros((1, TQ), F32)))
                dk_s[e, pl.ds(krow, TQ), :] = dk_acc * SCALE
                dv_s[e, pl.ds(krow, TQ), :] = dv_acc
                df_s[j] -= jnp.broadcast_to(dfk, (8, TQ))
                return 0

            lax.fori_loop(0, nq, kblock, 0)
            dq_s2 = dq_s[...] * SCALE
            dk_s[2 + e] = dq_s2
            for ii in range(nq):
                dfq = jnp.broadcast_to(col_s[3, ii * TQ:(ii + 1) * TQ, :], (TQ, 128))
                df_ref[e:e + 1, ii * TQ:(ii + 1) * TQ] = _tn(dfq, eye, HI)[0:1, :] + df_s[ii][0:1, :]
        dq_ref[...] = jnp.concatenate([dk_s[2], dk_s[3]], axis=1).astype(BF16)
        dk_ref[...] = jnp.concatenate([dk_s[0], dk_s[1]], axis=1).astype(BF16)
        dv_ref[...] = jnp.concatenate([dv_s[0], dv_s[1]], axis=1).astype(BF16)

    blk = lambda off: pl.BlockSpec((seq, 128), lambda b, p: (b, off // 128 + p))
    out_blk = pl.BlockSpec((seq, 128), lambda b, p: (b, p))
    return _pallas(
        body, name="fox_bwd", grid=(t_tok // seq, N_HEADS // 2),
        in_specs=[blk(C_BQ), blk(C_BK), blk(C_BV), out_blk, out_blk,
                  pl.BlockSpec((seq, 128), lambda b, p: (b, 0)),
                  pl.BlockSpec((N_HEADS, seq), lambda b, p: (0, b)),
                  pl.BlockSpec((seq, 128), lambda b, p: (b, 0))],
        out_specs=[out_blk, out_blk, out_blk, pl.BlockSpec((None, 2, seq), lambda b, p: (p, 0, b))],
        out_shape=[jax.ShapeDtypeStruct((t_tok, 512), BF16)] * 3
        + [jax.ShapeDtypeStruct((N_HEADS // 2, 2, t_tok), F32)],
        scratch_shapes=[pltpu.VMEM((seq, HEAD_DIM), F32), pltpu.VMEM((4, seq, HEAD_DIM), F32),
                        pltpu.VMEM((2, seq, HEAD_DIM), F32), pltpu.VMEM((4, seq, 1), F32),
                        pltpu.VMEM((nq, 8, TQ), F32), pltpu.VMEM((nq, N_HEADS, TQ), F32)],
        compiler_params=_cp(("arbitrary", "arbitrary"), VMEM_BIG))(qkvg, qkvg, qkvg, du_b, b_out, lse, f_row, f_col)


def _swa_window(k_ref, v_ref, n):
    prev = pl.multiple_of(jnp.maximum(n - 1, 0) * BLOCK, BLOCK)
    cur = pl.multiple_of(n * BLOCK, BLOCK)
    kwin = jnp.concatenate([k_ref[pl.ds(prev, BLOCK), :], k_ref[pl.ds(cur, BLOCK), :]], axis=0)
    vwin = jnp.concatenate([v_ref[pl.ds(prev, BLOCK), :], v_ref[pl.ds(cur, BLOCK), :]], axis=0)
    ti = lax.broadcasted_iota(jnp.int32, (BLOCK, 2 * BLOCK), 0)
    sj = lax.broadcasted_iota(jnp.int32, (BLOCK, 2 * BLOCK), 1)
    rel = ti - sj + BLOCK
    first_key = jnp.where(n > 0, 0, BLOCK)
    mask = jnp.logical_and(jnp.logical_and(rel >= 0, rel < BLOCK), sj >= first_key)
    return kwin, vwin, mask, prev, cur


def _head_cols(ref, h):
    pair = ref[:, 128 * (h // 2):128 * (h // 2 + 1)]
    return pair[:, (h % 2) * HEAD_DIM:(h % 2 + 1) * HEAD_DIM]


def _swa_logits(q_ref, kwin, bias_ref, h, mask):
    hk = h // KV_GROUP
    s = _nt(_head_cols(q_ref, h), kwin[:, hk * HEAD_DIM:(hk + 1) * HEAD_DIM]) * SCALE + bias_ref[h]
    return jnp.where(mask, s, NEG)


def _swa_fwd(qkvg, bias, sinks, seq):
    t_tok = qkvg.shape[0]
    nb = seq // BLOCK

    def body(sink_ref, q_ref, k_ref, v_ref, bias_ref, o_ref, lse_ref, s_s, p_s, den_s):
        g = pl.program_id(1)
        subs = [pl.ds(s * BLOCK, BLOCK) for s in range(SWA_SUB)]
        wins = [_swa_window(k_ref, v_ref, SWA_SUB * g + s) for s in range(SWA_SUB)]
        for s in range(SWA_SUB):
            for h in range(N_HEADS):
                s_s[s * N_HEADS + h] = _swa_logits(q_ref.at[subs[s]], wins[s][0], bias_ref, h, wins[s][2])
        lane = lax.broadcasted_iota(jnp.int32, (BLOCK, 128), 1)
        for s in range(SWA_SUB):
            lse_tile = jnp.zeros((BLOCK, 128), F32)
            for h in range(N_HEADS):
                sc = s_s[s * N_HEADS + h]
                sink = sink_ref[h]
                m = jnp.maximum(jnp.max(sc, axis=1, keepdims=True), sink)
                pe = jnp.exp(sc - m)
                den = jnp.sum(pe, axis=1, keepdims=True) + jnp.exp(sink - m)
                p_s[s * N_HEADS + h] = pe.astype(BF16)
                den_s[s * N_HEADS + h] = den
                lse_tile = jnp.where(lane == h, m + jnp.log(den), lse_tile)
            lse_ref[subs[s], :] = lse_tile
        for s in range(SWA_SUB):
            vwin = wins[s][1]
            for pr in range(N_HEADS // 2):
                outs = []
                for h in (2 * pr, 2 * pr + 1):
                    hk = h // KV_GROUP
                    outs.append(_nn(p_s[s * N_HEADS + h], vwin[:, hk * HEAD_DIM:(hk + 1) * HEAD_DIM]) / den_s[s * N_HEADS + h])
                o_ref[subs[s], 128 * pr:128 * (pr + 1)] = jnp.concatenate(outs, axis=1).astype(BF16)

    rows = SWA_SUB * BLOCK
    steps = nb // SWA_SUB
    return _pallas(
        body, name="swa_fwd", grid=(t_tok // seq, steps),
        in_specs=[pl.BlockSpec(memory_space=pltpu.SMEM),
                  pl.BlockSpec((rows, 512), lambda b, n: (b * steps + n, C_AQ // 512)),
                  pl.BlockSpec((seq, 128), lambda b, n: (b, C_AK // 128)),
                  pl.BlockSpec((seq, 128), lambda b, n: (b, C_AV // 128)),
                  pl.BlockSpec((N_HEADS, BLOCK, 2 * BLOCK), lambda b, n: (0, 0, 0))],
        out_specs=[pl.BlockSpec((rows, 512), lambda b, n: (b * steps + n, 0)),
                   pl.BlockSpec((rows, 128), lambda b, n: (b * steps + n, 0))],
        out_shape=[jax.ShapeDtypeStruct((t_tok, 512), BF16), jax.ShapeDtypeStruct((t_tok, 128), F32)],
        scratch_shapes=[pltpu.VMEM((SWA_SUB * N_HEADS, BLOCK, 2 * BLOCK), F32),
                        pltpu.VMEM((SWA_SUB * N_HEADS, BLOCK, 2 * BLOCK), BF16),
                        pltpu.VMEM((SWA_SUB * N_HEADS, BLOCK, 1), F32)],
        compiler_params=_cp(("arbitrary", "arbitrary"), VMEM_MID))(sinks, qkvg, qkvg, qkvg, bias)


def _swa_bwd(qkvg, du_a, a_out, lse, bias, sinks, seq):
    t_tok = qkvg.shape[0]
    nb = seq // BLOCK

    def body(sink_ref, q_ref, k_ref, v_ref, do_ref, o_ref, lse_ref, bias_ref,
             dq_ref, dkv_ref, dbias_ref, dsink_ref, kv_s, s_s, dp_s, pb_s, db_s):
        b, n = pl.program_id(0), pl.program_id(1)

        @pl.when(jnp.logical_and(b == 0, n == 0))
        def _():
            dbias_ref[...] = jnp.zeros_like(dbias_ref)
            dsink_ref[...] = jnp.zeros_like(dsink_ref)

        @pl.when(n == 0)
        def _():
            kv_s[...] = jnp.zeros_like(kv_s)

        subs = [pl.ds(s * BLOCK, BLOCK) for s in range(SWA_SUB)]
        wins = [_swa_window(k_ref, v_ref, SWA_SUB * n + s) for s in range(SWA_SUB)]
        for s in range(SWA_SUB):
            kwin, vwin, mask = wins[s][:3]
            for h in range(N_HEADS):
                hk = h // KV_GROUP
                s_s[s * N_HEADS + h] = _swa_logits(q_ref.at[subs[s]], kwin, bias_ref, h, mask)
                dp_s[s * N_HEADS + h] = _nt(_head_cols(do_ref.at[subs[s]], h), vwin[:, hk * HEAD_DIM:(hk + 1) * HEAD_DIM])
        for s in range(SWA_SUB):
            lse_tile = lse_ref[subs[s], :]
            do_s, o_s = do_ref.at[subs[s]], o_ref.at[subs[s]]
            for h in range(N_HEADS):
                delta = jnp.sum(_head_cols(do_s, h).astype(F32) * _head_cols(o_s, h).astype(F32), axis=1, keepdims=True)
                lse_h = _col(lse_tile, h)
                pe = jnp.exp(s_s[s * N_HEADS + h] - lse_h)
                ds = pe * (dp_s[s * N_HEADS + h] - delta)
                dbias_ref[h] += ds
                psink = jnp.exp(sink_ref[h] - lse_h)
                dsink_ref[h:h + 1, :] += jnp.broadcast_to(jnp.sum(-psink * delta, axis=0, keepdims=True), (1, 128))
                pb_s[s * N_HEADS + h] = pe.astype(BF16)
                db_s[s * N_HEADS + h] = ds.astype(BF16)
        for s in range(SWA_SUB):
            kwin, _, _, prev, cur = wins[s]
            q_s, do_s = q_ref.at[subs[s]], do_ref.at[subs[s]]
            for pr in range(N_HEADS // 2):
                dqs = []
                for h in (2 * pr, 2 * pr + 1):
                    hk = h // KV_GROUP
                    dqs.append(_nn(db_s[s * N_HEADS + h], kwin[:, hk * HEAD_DIM:(hk + 1) * HEAD_DIM]) * SCALE)
                dq_ref[subs[s], 128 * pr:128 * (pr + 1)] = jnp.concatenate(dqs, axis=1).astype(BF16)
            dks, dvs = [], []
            for hk in range(N_HEADS // KV_GROUP):
                dk = jnp.zeros((2 * BLOCK, HEAD_DIM), F32)
                dv = jnp.zeros((2 * BLOCK, HEAD_DIM), F32)
                for h in range(hk * KV_GROUP, (hk + 1) * KV_GROUP):
                    dk = dk + _tn(db_s[s * N_HEADS + h], _head_cols(q_s, h))
                    dv = dv + _tn(pb_s[s * N_HEADS + h], _head_cols(do_s, h))
                dks.append(dk * SCALE)
                dvs.append(dv)
            upd = jnp.concatenate(dks + dvs, axis=1)
            kv_s[pl.ds(prev, BLOCK), :] += upd[:BLOCK]
            kv_s[pl.ds(cur, BLOCK), :] += upd[BLOCK:]

        @pl.when(n == steps - 1)
        def _():
            dkv_ref[...] = kv_s[...].astype(BF16)

    rows = SWA_SUB * BLOCK
    steps = nb // SWA_SUB
    tile = (SWA_SUB * N_HEADS, BLOCK, 2 * BLOCK)
    return _pallas(
        body, name="swa_bwd", grid=(t_tok // seq, steps),
        in_specs=[pl.BlockSpec(memory_space=pltpu.SMEM),
                  pl.BlockSpec((rows, 512), lambda b, n: (b * steps + n, C_AQ // 512)),
                  pl.BlockSpec((seq, 128), lambda b, n: (b, C_AK // 128)),
                  pl.BlockSpec((seq, 128), lambda b, n: (b, C_AV // 128)),
                  pl.BlockSpec((rows, 512), lambda b, n: (b * steps + n, 0)),
                  pl.BlockSpec((rows, 512), lambda b, n: (b * steps + n, 0)),
                  pl.BlockSpec((rows, 128), lambda b, n: (b * steps + n, 0)),
                  pl.BlockSpec((N_HEADS, BLOCK, 2 * BLOCK), lambda b, n: (0, 0, 0))],
        out_specs=[pl.BlockSpec((rows, 512), lambda b, n: (b * steps + n, 0)),
                   pl.BlockSpec((seq, 256), lambda b, n: (b, 0)),
                   pl.BlockSpec((N_HEADS, BLOCK, 2 * BLOCK), lambda b, n: (0, 0, 0)),
                   pl.BlockSpec((N_HEADS, 128), lambda b, n: (0, 0))],
        out_shape=[jax.ShapeDtypeStruct((t_tok, 512), BF16), jax.ShapeDtypeStruct((t_tok, 256), BF16),
                   jax.ShapeDtypeStruct((N_HEADS, BLOCK, 2 * BLOCK), F32), jax.ShapeDtypeStruct((N_HEADS, 128), F32)],
        scratch_shapes=[pltpu.VMEM((seq, 256), F32), pltpu.VMEM(tile, F32), pltpu.VMEM(tile, F32),
                        pltpu.VMEM(tile, BF16), pltpu.VMEM(tile, BF16)],
        compiler_params=_cp(("arbitrary", "arbitrary"), VMEM_MID))(sinks, qkvg, qkvg, qkvg, du_a, a_out, lse, bias)


def _out_proj(name, u_parts, gate_arr, gate_blk, w_out, x, gmod, seq):
    t_tok = x.shape[0]
    nu = len(u_parts)

    def body(*refs):
        u_refs = refs[:nu]
        g_ref, w_ref, x_ref, gm_ref, yg_ref, y_ref, xn_ref = refs[nu:]
        u = jnp.concatenate([r[...].astype(F32) for r in u_refs], axis=1) if nu > 1 else u_refs[0][...].astype(F32)
        yg = (u * _silu(g_ref[...].astype(F32))).astype(BF16)
        yg_ref[...] = yg
        y = _nn(yg, w_ref[...])
        y_ref[...] = y.astype(BF16)
        xn_ref[...] = x_ref[...] + gm_ref[...] * y

    row = lambda w: pl.BlockSpec((TM, w), lambda i: (i, 0))
    in_specs = [row(u.shape[1]) for u in u_parts]
    in_specs += [pl.BlockSpec((TM, D_MODEL), lambda i: (i, gate_blk)),
                 pl.BlockSpec((D_MODEL, D_MODEL), lambda i: (0, 0)), row(D_MODEL),
                 pl.BlockSpec((None, 1, D_MODEL), lambda i: (i * TM // seq, 0, 0))]
    return _pallas(
        body, name=name, grid=(t_tok // TM,), in_specs=in_specs,
        out_specs=[row(D_MODEL)] * 3,
        out_shape=[jax.ShapeDtypeStruct((t_tok, D_MODEL), BF16)] * 2 + [jax.ShapeDtypeStruct((t_tok, D_MODEL), F32)],
        compiler_params=_cp(("arbitrary",), VMEM_MID))(*u_parts, gate_arr, w_out, x, gmod)


def _out_proj_bwd(name, dxn, gmod, y, w_out, seq, attn=None):
    t_tok = dxn.shape[0]
    tiles_per_seq = seq // TM

    def body(*refs):
        if attn is None:
            dxn_ref, gm_ref, y_ref, w_ref, dy_ref, dgm_ref, dyg_ref = refs
        else:
            dxn_ref, gm_ref, y_ref, w_ref, a_ref, b_ref, g_ref, dy_ref, dgm_ref, dua_ref, dub_ref, dg_ref = refs
        i = pl.program_id(0)
        dxv = dxn_ref[...]
        dy = (dxv * gm_ref[...]).astype(BF16)
        dy_ref[...] = dy

        @pl.when(i % tiles_per_seq == 0)
        def _():
            dgm_ref[...] = jnp.zeros_like(dgm_ref)

        dgm_ref[...] += jnp.sum(dxv * y_ref[...].astype(F32), axis=0, keepdims=True)
        dyg = _nn(dy, w_ref[...])
        if attn is None:
            dyg_ref[...] = dyg
        else:
            gt = g_ref[...].astype(F32)
            du = dyg * _silu(gt)
            dua_ref[...] = du[:, :512].astype(BF16)
            dub_ref[...] = du[:, 512:].astype(BF16)
            u = jnp.concatenate([a_ref[...].astype(F32), b_ref[...].astype(F32)], axis=1)
            dg_ref[...] = (dyg * u * _dsilu(gt)).astype(BF16)

    row = lambda w: pl.BlockSpec((TM, w), lambda i: (i, 0))
    mod_spec = pl.BlockSpec((None, 1, D_MODEL), lambda i: (i * TM // seq, 0, 0))
    in_specs = [row(D_MODEL), mod_spec, row(D_MODEL), pl.BlockSpec((D_MODEL, D_MODEL), lambda i: (0, 0))]
    out_specs = [row(D_MODEL), mod_spec]
    out_shape = [jax.ShapeDtypeStruct((t_tok, D_MODEL), BF16), jax.ShapeDtypeStruct(gmod.shape, F32)]
    args = [dxn, gmod, y, w_out]
    if attn is None:
        out_specs.append(row(D_MODEL))
        out_shape.append(jax.ShapeDtypeStruct((t_tok, D_MODEL), F32))
    else:
        in_specs += [row(512), row(512), pl.BlockSpec((TM, D_MODEL), lambda i: (i, C_GATE // D_MODEL))]
        out_specs += [row(512), row(512), row(D_MODEL)]
        out_shape += [jax.ShapeDtypeStruct((t_tok, 512), BF16)] * 2 + [jax.ShapeDtypeStruct((t_tok, D_MODEL), BF16)]
        args += list(attn)
    return _pallas(body, name=name, grid=(t_tok // TM,), in_specs=in_specs, out_specs=out_specs,
                   out_shape=out_shape, compiler_params=_cp(("arbitrary",), VMEM_MID))(*args)


def _norm_bwd(name, parts, w, x, g, scale, dxn, seq, rows_part=None):
    t_tok = x.shape[0]
    npart = len(parts)
    tiles_per_seq = seq // TM
    nrow_in = 0 if rows_part is None else 2

    def body(*refs):
        p_refs = refs[:npart]
        w_ref, x_ref, g_ref, sc_ref, dxn_ref = refs[npart:npart + 5]
        dx_ref, dss_ref, dg_ref = refs[npart + 5 + nrow_in:]
        i = pl.program_id(0)
        dh = jnp.zeros((TM, D_MODEL), F32)
        if rows_part is not None:
            r_ref, wr_ref = refs[npart + 5:npart + 7]
            dh = dh + _tn(r_ref[...].astype(BF16), wr_ref[...])
        for (arr, off), p_ref in zip(parts, p_refs):
            width = arr.shape[1]
            for j in range(width // 256):
                pj = p_ref[:, j * 256:(j + 1) * 256]
                c0 = off + j * 256
                dh = dh + _nn(pj, w_ref[c0:c0 + 256, :])
        xv = x_ref[...]
        rstd = lax.rsqrt(jnp.mean(xv * xv, axis=-1, keepdims=True) + EPS)
        xhat = xv * rstd
        gv = g_ref[...]
        nrm = xhat * gv

        @pl.when(i % tiles_per_seq == 0)
        def _():
            dss_ref[...] = jnp.zeros_like(dss_ref)

        @pl.when(i == 0)
        def _():
            dg_ref[...] = jnp.zeros_like(dg_ref)

        dss_ref[0:1, :] += jnp.sum(dh, axis=0, keepdims=True)
        dss_ref[1:2, :] += jnp.sum(dh * nrm, axis=0, keepdims=True)
        dn = dh * (1.0 + sc_ref[...])
        dg_ref[0:1, :] += jnp.sum(dn * xhat, axis=0, keepdims=True)
        dxhat = dn * gv
        dx_ref[...] = rstd * (dxhat - xhat * jnp.mean(dxhat * xhat, axis=-1, keepdims=True)) + dxn_ref[...]

    row = lambda wd: pl.BlockSpec((TM, wd), lambda i: (i, 0))
    w_spec = pl.BlockSpec(w.shape, lambda i: (0, 0))
    in_specs = [row(a.shape[1]) for a, _ in parts]
    in_specs += [w_spec, row(D_MODEL), pl.BlockSpec((1, D_MODEL), lambda i: (0, 0)),
                 pl.BlockSpec((None, 1, D_MODEL), lambda i: (i * TM // seq, 0, 0)), row(D_MODEL)]
    args = [a for a, _ in parts] + [w, x, g, scale, dxn]
    if rows_part is not None:
        in_specs += [pl.BlockSpec((8, TM), lambda i: (0, i)), pl.BlockSpec((8, D_MODEL), lambda i: (0, 0))]
        args += list(rows_part)
    nseq = t_tok // seq
    return _pallas(
        body, name=name, grid=(t_tok // TM,), in_specs=in_specs,
        out_specs=[row(D_MODEL), pl.BlockSpec((None, 8, D_MODEL), lambda i: (i * TM // seq, 0, 0)),
                   pl.BlockSpec((8, D_MODEL), lambda i: (0, 0))],
        out_shape=[jax.ShapeDtypeStruct((t_tok, D_MODEL), F32), jax.ShapeDtypeStruct((nseq, 8, D_MODEL), F32),
                   jax.ShapeDtypeStruct((8, D_MODEL), F32)],
        compiler_params=_cp(("arbitrary",), VMEM_BIG))(*args)


def _dw(name, a, parts, blocked=None):
    t_tok, ka = a.shape
    tt = 512
    npart = len(parts)
    nt = t_tok // tt

    def body(*refs):
        a_ref = refs[0]
        p_refs = refs[1:1 + npart]
        o_refs = refs[1 + npart:1 + 2 * npart]
        acc_refs = refs[1 + 2 * npart:]
        t = pl.program_id(0)
        av = a_ref[...]
        for p_ref, acc in zip(p_refs, acc_refs):
            upd = _tn(av, p_ref[...])

            @pl.when(t == 0)
            def _():
                acc[...] = upd

            @pl.when(t > 0)
            def _():
                acc[...] += upd

        @pl.when(t == nt - 1)
        def _():
            for o_ref, acc in zip(o_refs, acc_refs):
                if blocked is None:
                    o_ref[...] = acc[...].astype(BF16)
                else:
                    for j in range(o_ref.shape[0]):
                        o_ref[j] = acc[:, j * blocked:(j + 1) * blocked].astype(BF16)

    in_specs = [pl.BlockSpec((tt, ka), lambda t: (t, 0))]
    in_specs += [pl.BlockSpec((tt, p.shape[1]), lambda t: (t, 0)) for p in parts]
    if blocked is None:
        out_shape = [jax.ShapeDtypeStruct((ka, p.shape[1]), BF16) for p in parts]
        out_specs = [pl.BlockSpec((ka, p.shape[1]), lambda t: (0, 0)) for p in parts]
    else:
        out_shape = [jax.ShapeDtypeStruct((p.shape[1] // blocked, ka, blocked), BF16) for p in parts]
        out_specs = [pl.BlockSpec((p.shape[1] // blocked, ka, blocked), lambda t: (0, 0, 0)) for p in parts]
    return _pallas(body, name=name, grid=(nt,), in_specs=in_specs, out_specs=out_specs, out_shape=out_shape,
                   scratch_shapes=[pltpu.VMEM((ka, p.shape[1]), F32) for p in parts],
                   compiler_params=_cp(("arbitrary",), VMEM_BIG))(a, *parts)


def _dw_rows(name, rows_t, h):
    t_tok = h.shape[0]
    tt = 512

    def body(r_ref, h_ref, o_ref):
        @pl.when(pl.program_id(0) == 0)
        def _():
            o_ref[...] = jnp.zeros_like(o_ref)

        o_ref[...] += _nn(r_ref[...].astype(BF16), h_ref[...])

    return _pallas(body, name=name, grid=(t_tok // tt,),
                   in_specs=[pl.BlockSpec((8, tt), lambda t: (0, t)), pl.BlockSpec((tt, D_MODEL), lambda t: (t, 0))],
                   out_specs=pl.BlockSpec((8, D_MODEL), lambda t: (0, 0)),
                   out_shape=jax.ShapeDtypeStruct((8, D_MODEL), F32),
                   compiler_params=_cp(("arbitrary",), VMEM_MID))(rows_t, h)


def _lru_gates(xc, blk, wa_ref, wx_ref, ba_ref, bx_ref, sp):
    cols = slice(blk * LRU_BLOCK_W, (blk + 1) * LRU_BLOCK_W)
    xb = xc[:, cols].astype(BF16)
    r = _sigmoid(_nn(xb, wa_ref[blk].astype(BF16)) + ba_ref[:, cols])
    ig = _sigmoid(_nn(xb, wx_ref[blk].astype(BF16)) + bx_ref[:, cols])
    log_a = -LRU_C * r * sp[:, cols]
    a = jnp.exp(log_a)
    x2 = 2.0 * log_a
    series = -x2 * (1.0 + x2 * (0.5 + x2 * (1.0 / 6.0)))
    z = jnp.where(x2 > -0.01, series, 1.0 - a * a)
    mult = z * lax.rsqrt(jnp.maximum(z, 1e-30))
    return xb, r, ig, a, mult


def _softplus_neg(lam):
    return jnp.maximum(-lam, 0.0) + jnp.log(1.0 + jnp.exp(-jnp.abs(lam)))


def _conv_taps(xe_ref, cw_ref, cb_ref):
    xc = cb_ref[...] + xe_ref[8:8 + TC, :] * cw_ref[3:4, :]
    for k in range(1, 4):
        xc = xc + xe_ref[8 - k:8 - k + TC, :] * cw_ref[3 - k:4 - k, :]
    return xc


def _lru_fwd(proj, cw, cb, w_a, b_a, w_x, b_x, lam, seq):
    t_tok = proj.shape[0]
    nc = seq // TC

    def body(x_ref, cw_ref, cb_ref, wa_ref, ba_ref, wx_ref, bx_ref, lam_ref, hs_ref, xe_s, a_s, u_s, h_s):
        c = pl.program_id(1)

        @pl.when(c == 0)
        def _():
            xe_s[0:8, :] = jnp.zeros((8, D_MODEL), F32)
            h_s[...] = jnp.zeros_like(h_s)

        xe_s[8:8 + TC, :] = x_ref[...]
        xc = _conv_taps(xe_s, cw_ref, cb_ref)
        sp = _softplus_neg(lam_ref[...])
        for blk in range(LRU_BLOCKS):
            cols = slice(blk * LRU_BLOCK_W, (blk + 1) * LRU_BLOCK_W)
            _, _, ig, a, mult = _lru_gates(xc, blk, wa_ref, wx_ref, ba_ref, bx_ref, sp)
            a_s[:, cols] = a
            u_s[:, cols] = mult * ig * xc[:, cols]

        def step(t, h):
            h = a_s[pl.ds(t, 1), :] * h + u_s[pl.ds(t, 1), :]
            hs_ref[pl.ds(t, 1), :] = h
            return h

        h_s[0:1, :] = lax.fori_loop(0, TC, step, h_s[0:1, :], unroll=8)
        xe_s[0:8, :] = xe_s[TC:TC + 8, :]

    full = lambda shape: pl.BlockSpec(shape, lambda b, c: (0,) * len(shape))
    return _pallas(
        body, name="lru_fwd", grid=(t_tok // seq, nc),
        in_specs=[pl.BlockSpec((TC, D_MODEL), lambda b, c: (b * nc + c, 0)), full((4, D_MODEL)), full((1, D_MODEL)),
                  full((LRU_BLOCKS, LRU_BLOCK_W, LRU_BLOCK_W)), full((1, D_MODEL)),
                  full((LRU_BLOCKS, LRU_BLOCK_W, LRU_BLOCK_W)), full((1, D_MODEL)), full((1, D_MODEL))],
        out_specs=pl.BlockSpec((TC, D_MODEL), lambda b, c: (b * nc + c, 0)),
        out_shape=jax.ShapeDtypeStruct((t_tok, D_MODEL), F32),
        scratch_shapes=[pltpu.VMEM((TC + 8, D_MODEL), F32), pltpu.VMEM((TC, D_MODEL), F32),
                        pltpu.VMEM((TC, D_MODEL), F32), pltpu.VMEM((8, D_MODEL), F32)],
        compiler_params=_cp(("arbitrary", "arbitrary"), VMEM_MID))(proj, cw, cb, w_a, b_a, w_x, b_x, lam)


def _lru_bwd(proj, hs, dyh, cw, cb, w_a, b_a, w_x, b_x, lam, seq):
    t_tok = proj.shape[0]
    nc = seq // TC

    def body(x_ref, xh_ref, g_ref, hs_ref, hh_ref, dy_ref, cw_ref, cb_ref, wa_ref, ba_ref, wx_ref, bx_ref, lam_ref,
             dp_ref, dcw_ref, dvec_ref, dwa_ref, dwx_ref,
             xe_s, he_s, de_s, a_s, r_s, i_s, m_s, dh_s, carry_s):
        b, cr = pl.program_id(0), pl.program_id(1)
        c = nc - 1 - cr

        @pl.when(jnp.logical_and(b == 0, cr == 0))
        def _():
            dcw_ref[...] = jnp.zeros_like(dcw_ref)
            dvec_ref[...] = jnp.zeros_like(dvec_ref)
            dwa_ref[...] = jnp.zeros_like(dwa_ref)
            dwx_ref[...] = jnp.zeros_like(dwx_ref)

        @pl.when(cr == 0)
        def _():
            carry_s[...] = jnp.zeros_like(carry_s)
            de_s[TC:TC + 8, :] = jnp.zeros((8, D_MODEL), F32)

        first = c == 0
        xe_s[0:8, :] = jnp.where(first, 0.0, xh_ref[...])
        xe_s[8:8 + TC, :] = x_ref[...]
        he_s[0:8, :] = jnp.where(first, 0.0, hh_ref[...])
        he_s[8:8 + TC, :] = hs_ref[...]
        xc = _conv_taps(xe_s, cw_ref, cb_ref)
        lam_v = lam_ref[...]
        sp = _softplus_neg(lam_v)
        for blk in range(LRU_BLOCKS):
            cols = slice(blk * LRU_BLOCK_W, (blk + 1) * LRU_BLOCK_W)
            _, r, ig, a, mult = _lru_gates(xc, blk, wa_ref, wx_ref, ba_ref, bx_ref, sp)
            a_s[:, cols], r_s[:, cols], i_s[:, cols], m_s[:, cols] = a, r, ig, mult

        gt = g_ref[...]
        dyh = dy_ref[...]
        dh_s[...] = dyh * _silu(gt)
        dp_ref[:, D_MODEL:] = (dyh * hs_ref[...] * _dsilu(gt)).astype(BF16)

        def step(k, carry):
            t = TC - 1 - k
            dh = dh_s[pl.ds(t, 1), :] + carry
            dh_s[pl.ds(t, 1), :] = dh
            return a_s[pl.ds(t, 1), :] * dh

        carry_s[0:1, :] = lax.fori_loop(0, TC, step, carry_s[0:1, :], unroll=8)

        hprev = he_s[7:7 + TC, :]
        for blk in range(LRU_BLOCKS):
            cols = slice(blk * LRU_BLOCK_W, (blk + 1) * LRU_BLOCK_W)
            xcb = xc[:, cols]
            a, r, ig, mult, dh = a_s[:, cols], r_s[:, cols], i_s[:, cols], m_s[:, cols], dh_s[:, cols]
            spb = sp[:, cols]
            dmult = dh * ig * xcb
            di = dh * mult * xcb
            dxc = dh * mult * ig
            dla = dh * hprev[:, cols] * a - dmult * (a * a) * lax.rsqrt(jnp.maximum(mult * mult, 1e-30))
            dr = dla * (-LRU_C * spb)
            dsp = jnp.sum(dla * (-LRU_C * r), axis=0, keepdims=True)
            dga = dr * r * (1.0 - r)
            dgx = di * ig * (1.0 - ig)
            dga_b, dgx_b = dga.astype(BF16), dgx.astype(BF16)
            xb = xcb.astype(BF16)
            dxc = dxc + _nt(dga_b, wa_ref[blk].astype(BF16)) + _nt(dgx_b, wx_ref[blk].astype(BF16))
            dwa_ref[blk] += _tn(xb, dga_b)
            dwx_ref[blk] += _tn(xb, dgx_b)
            dvec_ref[1:2, cols] += jnp.sum(dga, axis=0, keepdims=True)
            dvec_ref[2:3, cols] += jnp.sum(dgx, axis=0, keepdims=True)
            dvec_ref[3:4, cols] += dsp * (-1.0 / (1.0 + jnp.exp(lam_v[:, cols])))
            de_s[0:TC, cols] = dxc

        dxc = de_s[0:TC, :]
        dvec_ref[0:1, :] += jnp.sum(dxc, axis=0, keepdims=True)
        dxr = dxc * cw_ref[3:4, :]
        dcw_ref[3:4, :] += jnp.sum(dxc * xe_s[8:8 + TC, :], axis=0, keepdims=True)
        for k in range(1, 4):
            dxr = dxr + de_s[k:k + TC, :] * cw_ref[3 - k:4 - k, :]
            dcw_ref[3 - k:4 - k, :] += jnp.sum(dxc * xe_s[8 - k:8 - k + TC, :], axis=0, keepdims=True)
        dp_ref[:, :D_MODEL] = dxr.astype(BF16)
        de_s[TC:TC + 8, :] = de_s[0:8, :]

    chunk = lambda col: pl.BlockSpec((TC, D_MODEL), lambda b, cr: (b * nc + nc - 1 - cr, col))
    halo = lambda col: pl.BlockSpec(
        (8, D_MODEL), lambda b, cr: (jnp.maximum((b * nc + nc - 1 - cr) * (TC // 8) - 1, 0), col))
    full = lambda shape: pl.BlockSpec(shape, lambda b, cr: (0,) * len(shape))
    wblk = (LRU_BLOCKS, LRU_BLOCK_W, LRU_BLOCK_W)
    return _pallas(
        body, name="lru_bwd", grid=(t_tok // seq, nc),
        in_specs=[chunk(0), halo(0), chunk(1), chunk(0), halo(0), chunk(0),
                  full((4, D_MODEL)), full((1, D_MODEL)), full(wblk), full((1, D_MODEL)), full(wblk),
                  full((1, D_MODEL)), full((1, D_MODEL))],
        out_specs=[pl.BlockSpec((TC, 2 * D_MODEL), lambda b, cr: (b * nc + nc - 1 - cr, 0)),
                   full((8, D_MODEL)), full((8, D_MODEL)), full(wblk), full(wblk)],
        out_shape=[jax.ShapeDtypeStruct((t_tok, 2 * D_MODEL), BF16), jax.ShapeDtypeStruct((8, D_MODEL), F32),
                   jax.ShapeDtypeStruct((8, D_MODEL), F32), jax.ShapeDtypeStruct(wblk, F32),
                   jax.ShapeDtypeStruct(wblk, F32)],
        scratch_shapes=[pltpu.VMEM((TC + 8, D_MODEL), F32), pltpu.VMEM((TC + 8, D_MODEL), F32),
                        pltpu.VMEM((TC + 8, D_MODEL), F32)]
        + [pltpu.VMEM((TC, D_MODEL), F32)] * 5 + [pltpu.VMEM((8, D_MODEL), F32)],
        compiler_params=_cp(("arbitrary", "arbitrary"), VMEM_BIG),
    )(proj, proj, proj, hs, hs, dyh, cw, cb, w_a, b_a, w_x, b_x, lam)


def _last_layer_tail(hs, proj, w_out, w_out_t, x, gmod, final_g, target, seq):
    t_tok = x.shape[0]
    tiles_per_seq = seq // TM

    def body(hs_ref, g_ref, w_ref, wt_ref, x_ref, gm_ref, fg_ref, t_ref,
             yg_ref, dx_ref, dy_ref, dyg_ref, dgm_ref, loss_ref, dfg_ref):
        i = pl.program_id(0)

        @pl.when(i == 0)
        def _():
            loss_ref[...] = jnp.zeros_like(loss_ref)
            dfg_ref[...] = jnp.zeros_like(dfg_ref)

        @pl.when(i % tiles_per_seq == 0)
        def _():
            dgm_ref[...] = jnp.zeros_like(dgm_ref)

        gm = gm_ref[...]
        yg = (hs_ref[...] * _silu(g_ref[...])).astype(BF16)
        yg_ref[...] = yg
        y = _nn(yg, w_ref[...])
        xv = x_ref[...] + gm * y
        gv = fg_ref[...]
        rstd = lax.rsqrt(jnp.mean(xv * xv, axis=-1, keepdims=True) + EPS)
        xhat = xv * rstd
        err = xhat * gv - t_ref[...]
        loss_ref[0:1, :] += jnp.sum(err * err, axis=0, keepdims=True) * (0.5 / D_MODEL)
        dout = err * (1.0 / D_MODEL)
        dfg_ref[0:1, :] += jnp.sum(dout * xhat, axis=0, keepdims=True)
        dxhat = dout * gv
        dxv = rstd * (dxhat - xhat * jnp.mean(dxhat * xhat, axis=-1, keepdims=True))
        dx_ref[...] = dxv
        dgm_ref[...] += jnp.sum(dxv * y, axis=0, keepdims=True)
        dy = (dxv * gm).astype(BF16)
        dy_ref[...] = dy
        dyg_ref[...] = _nn(dy, wt_ref[...])

    row = pl.BlockSpec((TM, D_MODEL), lambda i: (i, 0))
    acc = pl.BlockSpec((8, D_MODEL), lambda i: (0, 0))
    mod_spec = pl.BlockSpec((None, 1, D_MODEL), lambda i: (i * TM // seq, 0, 0))
    return _pallas(
        body, name="last_layer_tail", grid=(t_tok // TM,),
        in_specs=[row, pl.BlockSpec((TM, D_MODEL), lambda i: (i, 1)), pl.BlockSpec((D_MODEL, D_MODEL), lambda i: (0, 0)),
                  pl.BlockSpec((D_MODEL, D_MODEL), lambda i: (0, 0)),
                  row, mod_spec, pl.BlockSpec((1, D_MODEL), lambda i: (0, 0)), row],
        out_specs=[row, row, row, row, mod_spec, acc, acc],
        out_shape=[jax.ShapeDtypeStruct((t_tok, D_MODEL), BF16), jax.ShapeDtypeStruct((t_tok, D_MODEL), F32),
                   jax.ShapeDtypeStruct((t_tok, D_MODEL), BF16), jax.ShapeDtypeStruct((t_tok, D_MODEL), F32),
                   jax.ShapeDtypeStruct(gmod.shape, F32), jax.ShapeDtypeStruct((8, D_MODEL), F32),
                   jax.ShapeDtypeStruct((8, D_MODEL), F32)],
        compiler_params=_cp(("arbitrary",), VMEM_BIG))(hs, proj, w_out, w_out_t, x, gmod, final_g, target)


def _final_loss(x, g, target):
    t_tok = x.shape[0]

    def body(x_ref, g_ref, t_ref, dx_ref, loss_ref, dg_ref):
        @pl.when(pl.program_id(0) == 0)
        def _():
            loss_ref[...] = jnp.zeros_like(loss_ref)
            dg_ref[...] = jnp.zeros_like(dg_ref)

        xv = x_ref[...]
        gv = g_ref[...]
        rstd = lax.rsqrt(jnp.mean(xv * xv, axis=-1, keepdims=True) + EPS)
        xhat = xv * rstd
        err = xhat * gv - t_ref[...]
        loss_ref[0:1, :] += jnp.sum(err * err, axis=0, keepdims=True) * (0.5 / D_MODEL)
        dout = err * (1.0 / D_MODEL)
        dg_ref[0:1, :] += jnp.sum(dout * xhat, axis=0, keepdims=True)
        dxhat = dout * gv
        dx_ref[...] = rstd * (dxhat - xhat * jnp.mean(dxhat * xhat, axis=-1, keepdims=True))

    row = pl.BlockSpec((TM, D_MODEL), lambda i: (i, 0))
    acc = pl.BlockSpec((8, D_MODEL), lambda i: (0, 0))
    return _pallas(body, name="final_loss", grid=(t_tok // TM,),
                   in_specs=[row, pl.BlockSpec((1, D_MODEL), lambda i: (0, 0)), row],
                   out_specs=[row, acc, acc],
                   out_shape=[jax.ShapeDtypeStruct((t_tok, D_MODEL), F32)] + [jax.ShapeDtypeStruct((8, D_MODEL), F32)] * 2,
                   compiler_params=_cp(("arbitrary",), VMEM_MID))(x, g, target)


def _adam_math(w, g, m, v):
    m_new = ADAM_B1 * m + (1.0 - ADAM_B1) * g
    v_new = ADAM_B2 * v + (1.0 - ADAM_B2) * (g * g)
    m_hat = m_new / (1.0 - ADAM_B1 ** ADAM_STEP)
    v_hat = v_new / (1.0 - ADAM_B2 ** ADAM_STEP)
    delta = -ADAM_LR * (m_hat / (jnp.sqrt(v_hat) + ADAM_EPS) + ADAM_WD * w)
    return delta, m_new, v_new


def _sum_leading(name, x, out_dtype=F32):
    n, rows, cols = x.shape
    tr = PACK_ROWS if rows % PACK_ROWS == 0 else rows

    def body(x_ref, o_ref):
        acc = x_ref[0].astype(F32)
        for d in range(1, n):
            acc = acc + x_ref[d].astype(F32)
        o_ref[...] = acc.astype(out_dtype)

    return _pallas(body, name=name, grid=(rows // tr,),
                   in_specs=[pl.BlockSpec((n, tr, cols), lambda i: (0, i, 0))],
                   out_specs=pl.BlockSpec((tr, cols), lambda i: (i, 0)),
                   out_shape=jax.ShapeDtypeStruct((rows, cols), out_dtype),
                   compiler_params=_cp(("arbitrary",), VMEM_MID))(x)


def _adamw(name, w, m, v, g=None, parts=None):
    rows, cols = w.shape
    tr = rows if rows <= 256 else 256

    def body(*refs):
        w_ref, m_ref, v_ref, g_in, g_ref, d_ref, mo_ref, vo_ref = refs
        if parts is None:
            gv = g_in[...]
        else:
            acc = g_in[0].astype(F32)
            for d in range(1, parts.shape[0]):
                acc = acc + g_in[d].astype(F32)
            gv = acc[:, :cols]
        delta, m_new, v_new = _adam_math(w_ref[...], gv, m_ref[...], v_ref[...])
        g_ref[...] = gv
        d_ref[...] = delta
        mo_ref[...] = m_new
        vo_ref[...] = v_new

    row = pl.BlockSpec((tr, cols), lambda i: (i, 0))
    if parts is None:
        g_spec, g_arg = row, g
    else:
        g_spec, g_arg = pl.BlockSpec((parts.shape[0], tr, parts.shape[2]), lambda i: (0, i, 0)), parts
    return _pallas(body, name=name, grid=(rows // tr,), in_specs=[row, row, row, g_spec], out_specs=[row] * 4,
                   out_shape=[jax.ShapeDtypeStruct((rows, cols), F32)] * 4,
                   compiler_params=_cp(("arbitrary",), VMEM_MID))(w, m, v, g_arg)


def _adamw_many(name, groups):
    ntens = len(groups)

    def body(*refs):
        ins, outs = refs[:4 * ntens], refs[4 * ntens:]
        for k in range(ntens):
            w_ref, m_ref, v_ref, g_ref = ins[4 * k:4 * k + 4]
            gv = g_ref[...]
            delta, m_new, v_new = _adam_math(w_ref[...], gv, m_ref[...], v_ref[...])
            for o_ref, val in zip(outs[4 * k:4 * k + 4], (gv, delta, m_new, v_new)):
                o_ref[...] = val

    flat = [a for grp in groups for a in grp]
    out_shape = [jax.ShapeDtypeStruct(grp[0].shape, F32) for grp in groups for _ in range(4)]
    outs = _pallas(body, name=name, out_shape=out_shape, compiler_params=_cp(vmem=VMEM_MID))(*flat)
    return [tuple(outs[4 * k:4 * k + 4]) for k in range(ntens)]


def _pack_rows(arrs):
    rows, meta, total = [], [], 0
    for a in arrs:
        flat = a.reshape(-1)
        nrow = -(-flat.shape[0] // 1024) * 8
        rows.append(jnp.pad(flat, (0, nrow * 128 - flat.shape[0])).reshape(nrow, 128))
        meta.append((a.shape, flat.shape[0], nrow))
        total += nrow
    tail = -total % PACK_ROWS
    if tail:
        rows.append(jnp.zeros((tail, 128), F32))
    return jnp.concatenate(rows, axis=0), meta


def _unpack_rows(packed, meta):
    out, r0 = [], 0
    for shape, size, nrow in meta:
        out.append(packed[r0:r0 + nrow].reshape(-1)[:size].reshape(shape))
        r0 += nrow
    return out


WEIGHTS = ["rel_bias", "norm_g", "ada_w", "ada_b", "attn_w_in", "attn_sinks", "attn_b_f", "attn_w_out", "lru_w_in",
           "lru_conv_w", "lru_conv_b", "lru_w_a", "lru_b_a", "lru_w_x", "lru_b_x", "lru_lambda", "lru_w_out", "final_g"]
BIG = ["ada_w", "attn_w_in", "attn_w_out", "lru_w_in", "lru_w_out"]
PACK_ROWS = 256


def kernel(x, c, rel_bias, norm_g, ada_w, ada_b, attn_w_in, attn_sinks, attn_b_f, attn_w_out, lru_w_in, lru_conv_w, lru_conv_b, lru_w_a, lru_b_a, lru_w_x, lru_b_x, lru_lambda, lru_w_out, final_g, loss_target, m_rel_bias, m_norm_g, m_ada_w, m_ada_b, m_attn_w_in, m_attn_sinks, m_attn_b_f, m_attn_w_out, m_lru_w_in, m_lru_conv_w, m_lru_conv_b, m_lru_w_a, m_lru_b_a, m_lru_w_x, m_lru_b_x, m_lru_lambda, m_lru_w_out, m_final_g, v_rel_bias, v_norm_g, v_ada_w, v_ada_b, v_attn_w_in, v_attn_sinks, v_attn_b_f, v_attn_w_out, v_lru_w_in, v_lru_conv_w, v_lru_conv_b, v_lru_w_a, v_lru_b_a, v_lru_w_x, v_lru_b_x, v_lru_lambda, v_lru_w_out, v_final_g):
    nseq, seq, _ = x.shape
    t_tok = nseq * seq
    me = 4 * lax.axis_index("x") + 2 * lax.axis_index("y") + lax.axis_index("c")
    x0 = x.reshape(t_tok, D_MODEL)
    target = loss_target.reshape(t_tok, D_MODEL)

    w_in_pad = jnp.pad(attn_w_in[0].astype(BF16), ((0, 0), (0, SHARD_W_PAD - SHARD_W_IN)))
    vec_shard = jnp.concatenate([lru_conv_w[0], lru_conv_b, lru_b_a, lru_b_x, lru_lambda], axis=0)
    g_w_in, g_vec, g_c = _exchange("gather_first", [w_in_pad, vec_shard, c], [])
    later_w = [attn_w_out[0].astype(BF16), lru_w_in[0].astype(BF16), lru_w_out[0].astype(BF16)]
    later_handle, later_token = _exchange_start("gather_later_start", later_w, [], after=g_vec)
    w_full = jnp.transpose(g_w_in[:, :, :SHARD_W_IN], (1, 0, 2)).reshape(D_MODEL, N_DEV * SHARD_W_IN)
    w_aq, w_ak, w_av = w_full[:, 0:512], w_full[:, 512:640], w_full[:, 640:768]
    w_bq, w_bk, w_bv = w_full[:, 768:1280], w_full[:, 1280:1792], w_full[:, 1792:2304]
    w_f, w_gate = w_full[:, 2304:2312], w_full[:, 2312:3336]
    w_main = jnp.concatenate([w_bq, w_bk, w_bv, w_aq, w_gate, w_ak, w_av], axis=1)
    wf_t = jnp.transpose(w_f)
    vec_full = jnp.transpose(g_vec, (1, 0, 2)).reshape(8, D_MODEL)
    conv_w, conv_b, b_a, b_x, lam = vec_full[0:4], vec_full[4:5], vec_full[5:6], vec_full[6:7], vec_full[7:8]
    c_all = g_c.reshape(N_DEV * nseq, D_MODEL)

    ncol = ada_w.shape[2]
    ada_b_slice = lax.dynamic_slice(ada_b.reshape(2, N_DEV, ncol), (0, me, 0), (2, 1, ncol))
    mod_part = _ada_mod(c_all, ada_w, ada_b_slice)
    (g_mod,) = _exchange("gather_mod", [mod_part], [])
    mine = lax.dynamic_slice(g_mod, (0, 0, me * nseq, 0), (N_DEV, 2, nseq, ncol))
    mod = jnp.transpose(mine, (1, 2, 0, 3)).reshape(2, nseq, 3 * D_MODEL)
    shift = [mod[l, :, 0:D_MODEL].reshape(nseq, 1, D_MODEL) for l in range(2)]
    scale = [mod[l, :, D_MODEL:2 * D_MODEL].reshape(nseq, 1, D_MODEL) for l in range(2)]
    gmod = [mod[l, :, 2 * D_MODEL:].reshape(nseq, 1, D_MODEL) for l in range(2)]

    onehot = _bucket_onehot()
    bias = _bias_expand(jnp.transpose(rel_bias), onehot).reshape(N_HEADS, BLOCK, 2 * BLOCK)
    sinks = attn_sinks.reshape(N_HEADS)
    b_f = attn_b_f.reshape(N_HEADS, 1)
    norm_g0 = norm_g[0:1] + later_token[0:1, 0:1]
    h0, qkvg, fl_t = _norm_proj("norm_proj0", x0, norm_g0, shift[0], scale[0], w_main, seq, BF16, wf_t=wf_t)
    f_row, f_col = _fox_prep(fl_t, b_f, seq)
    a_out, lse_a = _swa_fwd(qkvg, bias, sinks, seq)
    q_aug, k_aug, kt_aug, vt, qt_aug = _fox_aug(qkvg, f_col, seq)
    b_out, lse_b = _fox_fwd_t(qt_aug, k_aug, vt, seq)
    g_later = _exchange_wait("gather_later_wait", later_handle, after=lse_b)
    w_out0, g_lru_in, w_out1 = (_with_own(g, w, me) for g, w in zip(g_later, later_w))
    w_out0, w_out1 = w_out0.reshape(D_MODEL, D_MODEL), w_out1.reshape(D_MODEL, D_MODEL)
    w_out0_t, w_out1_t, w_main_t = jnp.transpose(w_out0), jnp.transpose(w_out1), jnp.transpose(w_main)
    lru_in_t = jnp.transpose(g_lru_in, (0, 2, 1)).reshape(2 * D_MODEL, D_MODEL)
    yg0, y0, x1 = _out_proj("out_proj0", [a_out, b_out], qkvg, C_GATE // D_MODEL, w_out0, x0, gmod[0], seq)

    h1, proj1 = _norm_proj("norm_proj1", x1, norm_g[1:2], shift[1], scale[1], g_lru_in, seq, F32)
    hs = _lru_fwd(proj1, conv_w, conv_b, lru_w_a[0], b_a, lru_w_x[0], b_x, lam, seq)

    yg1, dx2, dy1, dyh, dgm1, loss_rows, dfinal_rows = _last_layer_tail(
        hs, proj1, w_out1, w_out1_t, x1, gmod[1], final_g.reshape(1, D_MODEL), target, seq)

    dproj1, dcw, dvec, dw_a, dw_x = _lru_bwd(proj1, hs, dyh, conv_w, conv_b, lru_w_a[0], b_a, lru_w_x[0], b_x, lam, seq)
    dx1, dss1, dg1 = _norm_bwd("norm1_bwd", [(dproj1, 0)], lru_in_t, x1, norm_g[1:2], scale[1], dx2, seq)
    (p_w_out1,) = _dw("dw_out1", yg1, [dy1])
    (p_lru_in,) = _dw("dw_lru_in", h1, [dproj1], blocked=2 * D_MODEL // N_DEV)

    rows_out = D_MODEL // N_DEV
    gpack1, gmeta1 = _pack_rows([dcw[0:4], dvec[0:4], dg1[0], dfinal_rows[0]])
    dwax = jnp.stack([dw_a, dw_x]).astype(BF16)
    own1 = [gpack1, dwax, p_lru_in, p_w_out1.reshape(N_DEV, rows_out, D_MODEL)]
    grads1_handle, grads1_token = _exchange_start("grads1_start", own1[:2], own1[2:], after=dx1)

    gmod0 = gmod[0] + grads1_token[0:1, 0:1]
    dy0, dgm0, du_a, du_b, dgate = _out_proj_bwd("out_proj0_bwd", dx1, gmod0, y0, w_out0_t, seq,
                                                  attn=(a_out, b_out, qkvg))
    dq_a, dkv_a, dbias, dsink = _swa_bwd(qkvg, du_a, a_out, lse_a, bias, sinks, seq)
    dq_b, dk_b, dv_b, df4 = _fox_bwd_t(q_aug, k_aug, kt_aug, qt_aug, qkvg, du_b, b_out, lse_b, seq)
    dfl_t, db_f = _fox_post(df4.reshape(N_HEADS, t_tok), fl_t, b_f, seq)
    parts0 = [(dq_b, C_BQ), (dk_b, C_BK), (dv_b, C_BV), (dq_a, C_AQ), (dgate, C_GATE), (dkv_a, C_AK)]
    (p_w_out0,) = _dw("dw_out0", yg0, [dy0])
    pw_bq, pw_bk, pw_bv, pw_aq, pw_gate, pw_akv = _dw("dw_attn_in", h0, [p for p, _ in parts0])
    pw_f = _dw_rows("dw_f", dfl_t, h0)

    p_w_in = jnp.concatenate([pw_aq, pw_akv, pw_bq, pw_bk, pw_bv, jnp.transpose(pw_f).astype(BF16), pw_gate], axis=1)
    p_w_in = jnp.transpose(p_w_in.reshape(D_MODEL, N_DEV, SHARD_W_IN), (1, 0, 2))
    p_w_in = jnp.pad(p_w_in, ((0, 0), (0, 0), (0, SHARD_W_PAD - SHARD_W_IN)))
    own0 = [p_w_in, p_w_out0.reshape(N_DEV, rows_out, D_MODEL)]
    landed1 = _exchange_wait("grads1_wait", grads1_handle, after=p_w_in)
    grads0_handle, grads0_token = _exchange_start("grads0_start", [], own0, after=landed1[0])
    scale0 = scale[0] + grads0_token[0:1, 0:1]
    dx0, dss0, dg0 = _norm_bwd("norm0_bwd", parts0, w_main_t, x0, norm_g[0:1], scale0, dx1, seq,
                               rows_part=(dfl_t, wf_t))
    dbias_t = _bias_reduce(dbias.reshape(N_HEADS, BLOCK * 2 * BLOCK), onehot)

    gpack0, gmeta0 = _pack_rows([jnp.transpose(dbias_t), dg0[0], dsink[:, 0], db_f[:, 0], loss_rows[0]])
    dmod = jnp.stack([jnp.concatenate([dss[:, 0], dss[:, 1], dgm[:, 0]], axis=1)
                      for dss, dgm in ((dss0, dgm0), (dss1, dgm1))], axis=1)
    g_small0, g_dmod = _exchange("exchange_small", [gpack0, dmod], [])
    landed0 = _exchange_wait("grads0_wait", grads0_handle, after=g_small0)
    r_w_in, r_w_out0 = (_with_own(g, lax.dynamic_index_in_dim(a, me, 0, keepdims=False), me)
                        for g, a in zip(landed0, own0))
    g_small1, g_dwax = (_with_own(g, a, me) for g, a in zip(landed1[:2], own1[:2]))
    r_lru_in, r_w_out1 = (_with_own(g, lax.dynamic_index_in_dim(a, me, 0, keepdims=False), me)
                          for g, a in zip(landed1[2:], own1[2:]))

    d_rel, d_g0, d_sinks, d_b_f, loss_cols = _unpack_rows(_sum_leading("sum_small0", g_small0), gmeta0)
    loss = jnp.sum(loss_cols)
    d_cw, d_vec, d_g1, d_final_g = _unpack_rows(_sum_leading("sum_small1", g_small1), gmeta1)
    d_norm_g = jnp.stack([d_g0, d_g1])
    d_wax = _sum_leading("sum_dwax", g_dwax.reshape(N_DEV, 2 * LRU_BLOCKS * LRU_BLOCK_W, LRU_BLOCK_W))
    d_wa, d_wx = d_wax[:LRU_BLOCKS * LRU_BLOCK_W], d_wax[LRU_BLOCKS * LRU_BLOCK_W:]
    cols = lambda a: lax.dynamic_slice(a, (0, me * LRU_BLOCK_W), (a.shape[0], LRU_BLOCK_W))
    dmod_all = g_dmod.reshape(N_DEV * nseq, 2 * 3 * D_MODEL)
    d_ada_b = _sum_leading("sum_ada_b", dmod_all.reshape(N_DEV * nseq, 2 * 3 * D_MODEL // 128, 128)).reshape(2, 3 * D_MODEL)
    dmod_slice = lax.dynamic_slice(dmod_all.reshape(N_DEV * nseq, 2, N_DEV, ncol), (0, 0, me, 0),
                                   (N_DEV * nseq, 2, 1, ncol)).reshape(N_DEV * nseq, 2, ncol)
    d_ada_w = _ada_w_grad(c_all, jnp.transpose(dmod_slice, (1, 0, 2)))

    given = dict(
        rel_bias=(rel_bias, m_rel_bias, v_rel_bias), norm_g=(norm_g, m_norm_g, v_norm_g),
        ada_w=(ada_w, m_ada_w, v_ada_w), ada_b=(ada_b, m_ada_b, v_ada_b),
        attn_w_in=(attn_w_in, m_attn_w_in, v_attn_w_in), attn_sinks=(attn_sinks, m_attn_sinks, v_attn_sinks),
        attn_b_f=(attn_b_f, m_attn_b_f, v_attn_b_f), attn_w_out=(attn_w_out, m_attn_w_out, v_attn_w_out),
        lru_w_in=(lru_w_in, m_lru_w_in, v_lru_w_in), lru_conv_w=(lru_conv_w, m_lru_conv_w, v_lru_conv_w),
        lru_conv_b=(lru_conv_b, m_lru_conv_b, v_lru_conv_b), lru_w_a=(lru_w_a, m_lru_w_a, v_lru_w_a),
        lru_b_a=(lru_b_a, m_lru_b_a, v_lru_b_a), lru_w_x=(lru_w_x, m_lru_w_x, v_lru_w_x),
        lru_b_x=(lru_b_x, m_lru_b_x, v_lru_b_x), lru_lambda=(lru_lambda, m_lru_lambda, v_lru_lambda),
        lru_w_out=(lru_w_out, m_lru_w_out, v_lru_w_out), final_g=(final_g, m_final_g, v_final_g))
    results = {}

    def big(name, shape2d, g=None, parts=None):
        w, m, v = (a.reshape(shape2d) for a in given[name])
        outs = _adamw("adamw_" + name, w, m, v, g=g, parts=parts)
        results[name] = tuple(o.reshape(given[name][0].shape) for o in outs)

    big("ada_w", (2 * D_MODEL, ncol), g=d_ada_w.reshape(2 * D_MODEL, ncol))
    big("attn_w_in", (D_MODEL, SHARD_W_IN), parts=r_w_in)
    big("attn_w_out", (rows_out, D_MODEL), parts=r_w_out0)
    big("lru_w_in", (D_MODEL, 2 * D_MODEL // N_DEV), parts=r_lru_in)
    big("lru_w_out", (rows_out, D_MODEL), parts=r_w_out1)

    small_grads = dict(
        rel_bias=d_rel, norm_g=d_norm_g, ada_b=d_ada_b, attn_sinks=d_sinks.reshape(1, N_HEADS),
        attn_b_f=d_b_f.reshape(1, N_HEADS), lru_conv_w=cols(d_cw).reshape(1, 4, LRU_BLOCK_W),
        lru_conv_b=cols(d_vec[0:1]), lru_w_a=d_wa.reshape(lru_w_a.shape), lru_b_a=cols(d_vec[1:2]),
        lru_w_x=d_wx.reshape(lru_w_x.shape), lru_b_x=cols(d_vec[2:3]), lru_lambda=cols(d_vec[3:4]),
        final_g=d_final_g)
    small = [n for n in WEIGHTS if n not in BIG]
    as2d = lambda a: a.reshape(-1, a.shape[-1])
    outs = _adamw_many("adamw_small", [tuple(as2d(a) for a in given[n]) + (as2d(small_grads[n]),) for n in small])
    for n, group in zip(small, outs):
        results[n] = tuple(o.reshape(given[n][0].shape) for o in group)

    grad_x = dx0.reshape(x.shape)
    out = [loss, grad_x]
    for j in range(4):
        out += [results[n][j] for n in WEIGHTS]
    return tuple(out)
```

```python
import functools
import math

import jax
import jax.numpy as jnp
from jax import lax
from jax.experimental import pallas as pl
from jax.experimental.pallas import tpu as pltpu

F32 = jnp.float32
BF16 = jnp.bfloat16
HI = lax.Precision.HIGHEST
MESH = pl.DeviceIdType.MESH

N_DEV = 8
D_MODEL = 1024
HEAD_DIM = 64
N_HEADS = 8
KV_GROUP = 4
BLOCK = 128
REL_BUCKETS = 32
REL_MAX_EXACT = 16
REL_MAX_DIST = 128
LRU_BLOCKS = 8
LRU_BLOCK_W = 128
LRU_C = 8.0
EPS = 1e-6
SCALE = HEAD_DIM ** -0.5
NEG = -1e30

ADAM_LR = 0.001
ADAM_B1 = 0.9
ADAM_B2 = 0.999
ADAM_EPS = 1e-08
ADAM_WD = 0.01
ADAM_STEP = 10

C_BQ, C_BK, C_BV, C_AQ, C_GATE, C_AK, C_AV = 0, 512, 1024, 1536, 2048, 3072, 3200
N_MAIN = 3328
SHARD_W_IN = 417
SHARD_W_PAD = 512

TM = 256
TQ = 256
TK = 128
TKB = 256
TC = 256
SWA_SUB = 2
VMEM_BIG = 56 * 1024 * 1024
VMEM_MID = 40 * 1024 * 1024


def _pallas(body, **kw):
    return pl.pallas_call(body, **kw)


def _cp(sem=None, vmem=None):
    kw = {}
    if sem is not None:
        kw["dimension_semantics"] = sem
    if vmem is not None:
        kw["vmem_limit_bytes"] = vmem
    return pltpu.CompilerParams(**kw)


def _nn(a, b, precision=None):
    return jnp.dot(a, b, preferred_element_type=F32, precision=precision)


def _nt(a, b, precision=None):
    return lax.dot_general(a, b, (((1,), (1,)), ((), ())), preferred_element_type=F32, precision=precision)


def _tn(a, b, precision=None):
    return lax.dot_general(a, b, (((0,), (0,)), ((), ())), preferred_element_type=F32, precision=precision)


def _sigmoid(x):
    return 1.0 / (1.0 + jnp.exp(-x))


def _silu(x):
    return x * _sigmoid(x)


def _dsilu(x):
    s = _sigmoid(x)
    return s * (1.0 + x * (1.0 - s))


def _neg_expm1(x):
    poly = x * (1.0 + x * (0.5 + x * (1.0 / 6.0 + x * (1.0 / 24.0))))
    return -jnp.where(jnp.abs(x) < 0.05, poly, jnp.exp(x) - 1.0)


def _col(tile, idx):
    lane = lax.broadcasted_iota(jnp.int32, tile.shape, 1)
    return jnp.sum(jnp.where(lane == idx, tile, 0.0), axis=1, keepdims=True)


def _row(tile, idx):
    sub = lax.broadcasted_iota(jnp.int32, tile.shape, 0)
    return jnp.sum(jnp.where(sub == idx, tile, 0.0), axis=0, keepdims=True)


def _exchange(name, gathers, scatters, axes=("x", "y", "c"), chunks=1):
    ng, n = len(gathers), len(gathers) + len(scatters)
    ins = list(gathers) + list(scatters)
    group = 2 ** len(axes)

    def body(*refs):
        in_refs, out_refs = refs[:n], refs[n:2 * n]
        send_sems, recv_sems, loc_sems = refs[2 * n:]
        coord = {a: lax.axis_index(a) for a in ("x", "y", "c")}

        def member(r):
            pc = dict(coord)
            idx = 0
            for k, a in enumerate(axes):
                if r & (1 << (len(axes) - 1 - k)):
                    pc[a] = 1 - coord[a]
                idx = 2 * idx + pc[a]
            return (pc["x"], pc["y"], pc["c"]), idx

        _, me = member(0)

        def peer(r):
            return member(r)

        local, sends, recvs = [], [], []
        for k in range(n):
            mine = in_refs[k] if k < ng else in_refs[k].at[me]
            cp = pltpu.make_async_copy(mine, out_refs[k].at[me], loc_sems.at[k])
            cp.start()
            local.append(cp)
            lead = mine.shape[0]
            nchunk = max(q for q in range(1, chunks + 1) if lead % q == 0)
            step = lead // nchunk
            for r in range(1, group):
                pid, pidx = peer(r)
                src = in_refs[k] if k < ng else in_refs[k].at[pidx]
                for q in range(nchunk):
                    rows = pl.ds(q * step, step)
                    sems = dict(send_sem=send_sems.at[r - 1, k, q], recv_sem=recv_sems.at[r - 1, k, q],
                                device_id=pid, device_id_type=MESH)
                    snd = pltpu.make_async_remote_copy(src_ref=src.at[rows], dst_ref=out_refs[k].at[me].at[rows], **sems)
                    snd.start()
                    sends.append(snd)
                    recvs.append(pltpu.make_async_remote_copy(
                        src_ref=src.at[rows], dst_ref=out_refs[k].at[pidx].at[rows], **sems))
        for rc in recvs:
            rc.wait_recv()
        for snd in sends:
            snd.wait_send()
        for cp in local:
            cp.wait()

    out_shape = [jax.ShapeDtypeStruct((group,) + a.shape, a.dtype) for a in gathers]
    out_shape += [jax.ShapeDtypeStruct(a.shape, a.dtype) for a in scatters]
    any_spec = pl.BlockSpec(memory_space=pl.ANY)
    return _pallas(
        body, name=name, out_shape=out_shape,
        in_specs=[any_spec] * n, out_specs=[any_spec] * n,
        scratch_shapes=[pltpu.SemaphoreType.DMA((group - 1, n, chunks)), pltpu.SemaphoreType.DMA((group - 1, n, chunks)),
                        pltpu.SemaphoreType.DMA((n,))],
    )(*ins)


def _peer_of(r):
    x, y, c = lax.axis_index("x"), lax.axis_index("y"), lax.axis_index("c")
    px = 1 - x if r & 4 else x
    py = 1 - y if r & 2 else y
    pc = 1 - c if r & 1 else c
    return (px, py, pc), 4 * px + 2 * py + pc


def _split_copies(in_refs, land_refs, send_sems, recv_sems, ng, with_recv):
    _, me = _peer_of(0)
    pairs = []
    for k, (src_ref, land) in enumerate(zip(in_refs, land_refs)):
        for r in range(1, N_DEV):
            pid, pidx = _peer_of(r)
            src = src_ref if k < ng else src_ref.at[pidx]
            slot = (N_DEV - 1) * k + r - 1
            sems = dict(send_sem=send_sems.at[slot], recv_sem=recv_sems.at[slot], device_id=pid, device_id_type=MESH)
            send = pltpu.make_async_remote_copy(src_ref=src, dst_ref=land.at[me], **sems)
            recv = pltpu.make_async_remote_copy(src_ref=src, dst_ref=land.at[pidx], **sems) if with_recv else None
            pairs.append((send, recv))
    return pairs


def _exchange_start(name, gathers, scatters, after):
    ng, n = len(gathers), len(gathers) + len(scatters)
    ins = list(gathers) + list(scatters)
    lands = [jax.ShapeDtypeStruct((N_DEV,) + a.shape, a.dtype) for a in gathers]
    lands += [jax.ShapeDtypeStruct(a.shape, a.dtype) for a in scatters]

    def body(*refs):
        in_refs, land_refs = refs[:n], refs[n:2 * n]
        send_sems, recv_sems = refs[2 * n + 1:2 * n + 3]
        token = refs[-1]
        for send, _ in _split_copies(in_refs, land_refs, send_sems, recv_sems, ng, False):
            send.start()
        token[...] = jnp.zeros_like(token)

    hbm = pl.BlockSpec(memory_space=pltpu.HBM)
    sem = pl.BlockSpec(memory_space=pltpu.SEMAPHORE)
    sem_shape = pltpu.SemaphoreType.DMA(((N_DEV - 1) * n,))
    out_shape = [sem_shape, sem_shape] + [pltpu.HBM(a.shape, a.dtype) for a in ins]
    out_shape += [pltpu.HBM(l.shape, l.dtype) for l in lands] + [jax.ShapeDtypeStruct((8, 128), F32)]
    args = [pltpu.with_memory_space_constraint(a, pltpu.HBM) for a in ins]
    args += [pltpu.with_memory_space_constraint(lax.empty(l.shape, l.dtype), pltpu.HBM) for l in lands]
    outs = _pallas(
        body, name=name, out_shape=out_shape,
        in_specs=[hbm] * (2 * n) + [pl.BlockSpec(memory_space=pl.ANY)],
        out_specs=[sem, sem] + [hbm] * (2 * n) + [pl.BlockSpec(memory_space=pltpu.VMEM)],
        input_output_aliases={i: 2 + i for i in range(2 * n)},
        compiler_params=pltpu.CompilerParams(has_side_effects=pltpu.SideEffectType.DATAFLOW_SIDE_EFFECTING),
    )(*args, after)
    return (outs[0], outs[1], list(outs[2:2 + n]), list(outs[2 + n:2 + 2 * n]), ng), outs[-1]


def _exchange_wait(name, handle, after):
    send_sems, recv_sems, srcs, lands, ng = handle
    n = len(srcs)

    def body(*refs):
        in_refs, land_refs = refs[:n], refs[n:2 * n]
        send_ref, recv_ref = refs[2 * n:2 * n + 2]
        for send, recv in _split_copies(in_refs, land_refs, send_ref, recv_ref, ng, True):
            send.wait_send()
            recv.wait_recv()

    hbm = pl.BlockSpec(memory_space=pltpu.HBM)
    sem = pl.BlockSpec(memory_space=pltpu.SEMAPHORE)
    outs = _pallas(
        body, name=name, out_shape=[pltpu.HBM(a.shape, a.dtype) for a in srcs + lands],
        in_specs=[hbm] * (2 * n) + [sem, sem, pl.BlockSpec(memory_space=pl.ANY)],
        out_specs=[hbm] * (2 * n), input_output_aliases={i: i for i in range(2 * n)},
        compiler_params=pltpu.CompilerParams(has_side_effects=pltpu.SideEffectType.DATAFLOW_SIDE_EFFECTING),
    )(*srcs, *lands, send_sems, recv_sems, after)
    return list(outs[n:])


def _with_own(land, own, me):
    return lax.dynamic_update_slice(land, own[None], (me,) + (0,) * own.ndim)


def _ada_mod(c_all, ada_w, ada_b_slice):
    def body(c_ref, w_ref, b_ref, o_ref):
        ca = _silu(c_ref[...])
        for l in range(2):
            o_ref[l] = _nn(ca, w_ref[l], HI) + b_ref[l]

    return _pallas(body, name="ada_mod",
                   out_shape=jax.ShapeDtypeStruct((2, c_all.shape[0], ada_w.shape[2]), F32),
                   compiler_params=_cp(vmem=VMEM_MID))(c_all, ada_w, ada_b_slice)


def _ada_w_grad(c_all, dmod_slice):
    def body(c_ref, d_ref, o_ref):
        ca = _silu(c_ref[...])
        for l in range(2):
            o_ref[l] = _tn(ca, d_ref[l], HI)

    return _pallas(body, name="ada_w_grad",
                   out_shape=jax.ShapeDtypeStruct((2, D_MODEL, dmod_slice.shape[2]), F32),
                   compiler_params=_cp(vmem=VMEM_MID))(c_all, dmod_slice)


def _bucket_onehot():
    qi = jnp.arange(BLOCK)[:, None]
    kj = jnp.arange(2 * BLOCK)[None, :]
    rel = qi - kj + BLOCK
    n = jnp.maximum(rel, 0)
    nf = jnp.maximum(n, 1).astype(F32)
    large = REL_MAX_EXACT + (jnp.log(nf / REL_MAX_EXACT) / math.log(REL_MAX_DIST / REL_MAX_EXACT)
                             * (REL_BUCKETS - REL_MAX_EXACT)).astype(jnp.int32)
    large = jnp.minimum(large, REL_BUCKETS - 1)
    bucket = jnp.where(n < REL_MAX_EXACT, n, large).reshape(1, BLOCK * 2 * BLOCK)
    return (jnp.arange(REL_BUCKETS)[:, None] == bucket).astype(F32)


def _bias_expand(rel_bias_t, onehot):
    def body(r_ref, e_ref, o_ref):
        o_ref[...] = _nn(r_ref[...], e_ref[...], HI)

    return _pallas(body, name="bias_expand",
                   out_shape=jax.ShapeDtypeStruct((N_HEADS, onehot.shape[1]), F32),
                   compiler_params=_cp(vmem=VMEM_MID))(rel_bias_t, onehot)


def _bias_reduce(dbias, onehot):
    def body(d_ref, e_ref, o_ref):
        o_ref[...] = _nt(d_ref[...], e_ref[...], HI)

    return _pallas(body, name="bias_reduce",
                   out_shape=jax.ShapeDtypeStruct((N_HEADS, REL_BUCKETS), F32),
                   compiler_params=_cp(vmem=VMEM_MID))(dbias, onehot)


def _norm_proj(name, x, g, shift, scale, w, seq, out_dtype, wf_t=None):
    t_tok = x.shape[0]
    w3d = w.ndim == 3
    n_out = w.shape[0] * w.shape[2] if w3d else w.shape[1]
    cn = w.shape[2] if w3d else 256

    def body(x_ref, g_ref, sh_ref, sc_ref, w_ref, *rest):
        if wf_t is not None:
            wf_ref, h_ref, o_ref, fl_ref = rest
        else:
            h_ref, o_ref = rest
        xv = x_ref[...]
        rstd = lax.rsqrt(jnp.mean(xv * xv, axis=-1, keepdims=True) + EPS)
        h = (xv * rstd) * g_ref[...] * (1.0 + sc_ref[...]) + sh_ref[...]
        hb = h.astype(BF16)
        h_ref[...] = hb
        for j in range(n_out // cn):
            wj = w_ref[j] if w3d else w_ref[:, j * cn:(j + 1) * cn]
            o_ref[:, j * cn:(j + 1) * cn] = _nn(hb, wj).astype(out_dtype)
        if wf_t is not None:
            fl_ref[...] = _nt(wf_ref[...], hb)

    mod_spec = pl.BlockSpec((None, 1, D_MODEL), lambda i: (i * TM // seq, 0, 0))
    w_spec = (pl.BlockSpec(w.shape, lambda i: (0, 0, 0)) if w3d else pl.BlockSpec(w.shape, lambda i: (0, 0)))
    in_specs = [pl.BlockSpec((TM, D_MODEL), lambda i: (i, 0)), pl.BlockSpec((1, D_MODEL), lambda i: (0, 0)),
                mod_spec, mod_spec, w_spec]
    out_shape = [jax.ShapeDtypeStruct((t_tok, D_MODEL), BF16), jax.ShapeDtypeStruct((t_tok, n_out), out_dtype)]
    out_specs = [pl.BlockSpec((TM, D_MODEL), lambda i: (i, 0)), pl.BlockSpec((TM, n_out), lambda i: (i, 0))]
    args = [x, g, shift, scale, w]
    if wf_t is not None:
        in_specs.append(pl.BlockSpec(wf_t.shape, lambda i: (0, 0)))
        out_shape.append(jax.ShapeDtypeStruct((wf_t.shape[0], t_tok), F32))
        out_specs.append(pl.BlockSpec((wf_t.shape[0], TM), lambda i: (0, i)))
        args.append(wf_t)
    return _pallas(body, name=name, grid=(t_tok // TM,), in_specs=in_specs, out_specs=out_specs,
                   out_shape=out_shape, compiler_params=_cp(("arbitrary",), VMEM_BIG))(*args)


def _fox_prep(fl_t, b_f, seq):
    t_tok = fl_t.shape[1]
    ch = 256

    def body(fl_ref, bf_ref, fr_ref, fc_ref):
        z = fl_ref[...] + bf_ref[...]
        logf = jnp.minimum(z, 0.0) - jnp.log(1.0 + jnp.exp(-jnp.abs(z)))
        ri = lax.broadcasted_iota(jnp.int32, (ch, ch), 0)
        ci = lax.broadcasted_iota(jnp.int32, (ch, ch), 1)
        upper = (ri <= ci).astype(F32)
        eye = (ri == ci).astype(F32)
        carry = jnp.zeros((N_HEADS, 1), F32)
        for k in range(seq // ch):
            fk = _nn(logf[:, k * ch:(k + 1) * ch], upper, HI) + carry
            carry = fk[:, ch - 1:ch]
            fr_ref[:, k * ch:(k + 1) * ch] = fk
            padded = jnp.concatenate([fk, jnp.zeros((128 - N_HEADS, ch), F32)], axis=0)
            fc_ref[k * ch:(k + 1) * ch, :] = _nt(eye, padded, HI)

    return _pallas(
        body, name="fox_prep", grid=(t_tok // seq,),
        in_specs=[pl.BlockSpec((N_HEADS, seq), lambda b: (0, b)), pl.BlockSpec((N_HEADS, 1), lambda b: (0, 0))],
        out_specs=[pl.BlockSpec((N_HEADS, seq), lambda b: (0, b)), pl.BlockSpec((seq, 128), lambda b: (b, 0))],
        out_shape=[jax.ShapeDtypeStruct((N_HEADS, t_tok), F32), jax.ShapeDtypeStruct((t_tok, 128), F32)],
        compiler_params=_cp(("arbitrary",), VMEM_MID))(fl_t, b_f)


def _fox_post(df_row, fl_t, b_f, seq):
    t_tok = fl_t.shape[1]
    ch = 256

    def body(d_ref, fl_ref, bf_ref, o_ref, db_ref):
        @pl.when(pl.program_id(0) == 0)
        def _():
            db_ref[...] = jnp.zeros_like(db_ref)

        z = fl_ref[...] + bf_ref[...]
        sig_neg = 1.0 / (1.0 + jnp.exp(z))
        ri = lax.broadcasted_iota(jnp.int32, (ch, ch), 0)
        ci = lax.broadcasted_iota(jnp.int32, (ch, ch), 1)
        lower = (ri >= ci).astype(F32)
        carry = jnp.zeros((N_HEADS, 1), F32)
        tot = jnp.zeros((N_HEADS, 1), F32)
        for k in reversed(range(seq // ch)):
            dk = _nn(d_ref[:, k * ch:(k + 1) * ch], lower, HI) + carry
            carry = dk[:, 0:1]
            dfl = dk * sig_neg[:, k * ch:(k + 1) * ch]
            o_ref[:, k * ch:(k + 1) * ch] = dfl
            tot = tot + jnp.sum(dfl, axis=1, keepdims=True)
        db_ref[...] += jnp.broadcast_to(tot, db_ref.shape)

    return _pallas(
        body, name="fox_post", grid=(t_tok // seq,),
        in_specs=[pl.BlockSpec((N_HEADS, seq), lambda b: (0, b)), pl.BlockSpec((N_HEADS, seq), lambda b: (0, b)),
                  pl.BlockSpec((N_HEADS, 1), lambda b: (0, 0))],
        out_specs=[pl.BlockSpec((N_HEADS, seq), lambda b: (0, b)), pl.BlockSpec((N_HEADS, 128), lambda b: (0, 0))],
        out_shape=[jax.ShapeDtypeStruct((N_HEADS, t_tok), F32), jax.ShapeDtypeStruct((N_HEADS, 128), F32)],
        compiler_params=_cp(("arbitrary",), VMEM_MID))(df_row, fl_t, b_f)


def _eye(n, dtype):
    return (lax.broadcasted_iota(jnp.int32, (n, n), 0) == lax.broadcasted_iota(jnp.int32, (n, n), 1)).astype(dtype)


def _fox_aug(qkvg, f_col, seq):
    t_tok = qkvg.shape[0]
    ta = 256
    nkb = ta // TK

    def body(q_ref, k_ref, v_ref, fc_ref, qa_ref, ka_ref, kt_ref, vt_ref):
        ri = lax.broadcasted_iota(jnp.int32, (128, 128), 0)
        ci = lax.broadcasted_iota(jnp.int32, (128, 128), 1)
        eye = (ri == ci).astype(BF16)
        lane = lax.broadcasted_iota(jnp.int32, (ta, 128), 1)
        ones_q = jnp.where(jnp.logical_and(lane >= 64, lane < 67), 1.0, 0.0)
        ones_k = jnp.where(jnp.logical_and(lane >= 67, lane < 70), 1.0, 0.0)
        fc_tile = fc_ref[...]
        for p in range(N_HEADS // 2):
            q2 = q_ref[:, 128 * p:128 * (p + 1)]
            k2 = k_ref[:, 128 * p:128 * (p + 1)]
            vt = _nt(eye, v_ref[:, 128 * p:128 * (p + 1)]).astype(BF16)
            for kk in range(nkb):
                vt_ref[p, kk] = vt[:, kk * TK:(kk + 1) * TK]
            for e in range(2):
                h = 2 * p + e
                sel = jnp.logical_and(ri == ci + HEAD_DIM * e, ci < HEAD_DIM)
                f = _col(fc_tile, h)
                fh = f.astype(BF16).astype(F32)
                fm = (f - fh).astype(BF16).astype(F32)
                fl = (f - fh - fm).astype(BF16).astype(F32)
                qa = (_nn(q2, jnp.where(sel, SCALE, 0.0).astype(BF16)) + ones_q + jnp.where(lane == 67, fh, 0.0)
                      + jnp.where(lane == 68, fm, 0.0) + jnp.where(lane == 69, fl, 0.0))
                ka = (_nn(k2, jnp.where(sel, 1.0, 0.0).astype(BF16)) + ones_k - jnp.where(lane == 64, fh, 0.0)
                      - jnp.where(lane == 65, fm, 0.0) - jnp.where(lane == 66, fl, 0.0))
                qa_ref[h] = qa.astype(BF16)
                kab = ka.astype(BF16)
                ka_ref[h] = kab
                kt = _nt(eye, kab).astype(BF16)
                for kk in range(ta // TKB):
                    kt_ref[h, kk] = kt[:, kk * TKB:(kk + 1) * TKB]

    aug = jax.ShapeDtypeStruct((N_HEADS, t_tok, 128), BF16)
    return _pallas(
        body, name="fox_aug", grid=(t_tok // ta,),
        in_specs=[pl.BlockSpec((ta, 512), lambda i: (i, C_BQ // 512)), pl.BlockSpec((ta, 512), lambda i: (i, C_BK // 512)),
                  pl.BlockSpec((ta, 512), lambda i: (i, C_BV // 512)), pl.BlockSpec((ta, 128), lambda i: (i, 0))],
        out_specs=[pl.BlockSpec((N_HEADS, ta, 128), lambda i: (0, i, 0)), pl.BlockSpec((N_HEADS, ta, 128), lambda i: (0, i, 0)),
                   pl.BlockSpec((N_HEADS, ta // TKB, 128, TKB), lambda i: (0, i, 0, 0)),
                   pl.BlockSpec((N_HEADS // 2, nkb, 128, TK), lambda i: (0, i, 0, 0))],
        out_shape=[aug, aug, jax.ShapeDtypeStruct((N_HEADS, t_tok // TKB, 128, TKB), BF16),
                   jax.ShapeDtypeStruct((N_HEADS // 2, t_tok // TK, 128, TK), BF16)],
        compiler_params=_cp(("arbitrary",), VMEM_MID))(qkvg, qkvg, qkvg, f_col)


def _fox_fwd_t(q_aug, k_aug, vt, seq):
    t_tok = k_aug.shape[1]
    nq = seq // TQ
    ratio = TQ // TK

    def body(qa_ref, ka_ref, vt_ref, o_ref, lse_ref, ml_s, acc_s, st_s, p_s, al_s, qt_s):
        i = pl.program_id(1)
        tpos = i * TQ + lax.broadcasted_iota(jnp.int32, (1, TQ), 1)
        eye = _eye(HEAD_DIM, BF16)
        eye2 = _eye(128, BF16)
        for h in range(N_HEADS):
            qt_s[h] = _nt(eye2, qa_ref[h]).astype(BF16)
            ml_s[0, h] = jnp.full((1, TQ), NEG, F32)
            ml_s[1, h] = jnp.zeros((1, TQ), F32)
            acc_s[h] = jnp.zeros((HEAD_DIM, TQ), F32)
            p_s[1, h] = jnp.zeros((TK, TQ), BF16)
            al_s[1, h] = jnp.ones((1, TQ), F32)

        def scores(j):
            row0 = pl.multiple_of(j * TK, TK)
            for h in range(N_HEADS):
                st_s[j & 1, h] = _nn(ka_ref[h, pl.ds(row0, TK), :], qt_s[h])

        def softmax(j, masked):
            slot = j & 1
            if masked:
                keep = (j * TK + lax.broadcasted_iota(jnp.int32, (TK, 1), 0)) <= tpos
            for h in range(N_HEADS):
                st = st_s[slot, h]
                if masked:
                    st = jnp.where(keep, st, NEG)
                m = ml_s[0, h]
                m_new = jnp.maximum(m, jnp.max(st, axis=0, keepdims=True))
                alpha = jnp.exp(m - m_new)
                pe = jnp.exp(st - m_new)
                ml_s[0, h] = m_new
                ml_s[1, h] = alpha * ml_s[1, h] + jnp.sum(pe, axis=0, keepdims=True)
                al_s[slot, h] = alpha
                p_s[slot, h] = pe.astype(BF16)

        def values(j):
            slot = j & 1
            jv = jnp.maximum(j, 0)
            for h in range(N_HEADS):
                p, e = divmod(h, 2)
                acc_s[h] = al_s[slot, h] * acc_s[h] + _nn(vt_ref[p, jv, e * HEAD_DIM:(e + 1) * HEAD_DIM, :], p_s[slot, h])

        def step(j, carry):
            values(j - 1)
            softmax(j, False)
            scores(j + 1)
            return carry

        last = ratio * i + ratio - 1
        scores(0)
        lax.fori_loop(0, ratio * i, step, 0)
        for kk in range(ratio):
            j = ratio * i + kk
            values(j - 1)
            softmax(j, True)
            if kk < ratio - 1:
                scores(j + 1)
        values(last)
        for p in range(N_HEADS // 2):
            outs = []
            for e in range(2):
                h = 2 * p + e
                l = ml_s[1, h]
                outs.append(_tn((acc_s[h] / l).astype(BF16), eye))
                lse_ref[p, e:e + 1, :] = ml_s[0, h] + jnp.log(l)
            o_ref[:, 128 * p:128 * (p + 1)] = jnp.concatenate(outs, axis=1).astype(BF16)

    return _pallas(
        body, name="fox_fwd", grid=(t_tok // seq, nq),
        in_specs=[pl.BlockSpec((N_HEADS, TQ, 128), lambda b, i: (0, b * nq + i, 0)),
                  pl.BlockSpec((N_HEADS, seq, 128), lambda b, i: (0, b, 0)),
                  pl.BlockSpec((N_HEADS // 2, seq // TK, 128, TK), lambda b, i: (0, b, 0, 0))],
        out_specs=[pl.BlockSpec((TQ, 512), lambda b, i: (b * nq + i, 0)),
                   pl.BlockSpec((N_HEADS // 2, 2, TQ), lambda b, i: (0, 0, b * nq + i))],
        out_shape=[jax.ShapeDtypeStruct((t_tok, 512), BF16), jax.ShapeDtypeStruct((N_HEADS // 2, 2, t_tok), F32)],
        scratch_shapes=[pltpu.VMEM((2, N_HEADS, 1, TQ), F32), pltpu.VMEM((N_HEADS, HEAD_DIM, TQ), F32),
                        pltpu.VMEM((2, N_HEADS, TK, TQ), F32), pltpu.VMEM((2, N_HEADS, TK, TQ), BF16),
                        pltpu.VMEM((2, N_HEADS, 1, TQ), F32), pltpu.VMEM((N_HEADS, 128, TQ), BF16)],
        compiler_params=_cp(("arbitrary", "arbitrary"), VMEM_MID))(q_aug, k_aug, vt)


def _fox_bwd_t(q_aug, k_aug, kt, qkvg, du_b, b_out, lse, seq):
    TK = TKB
    t_tok = qkvg.shape[0]
    nq = seq // TQ
    nkb = seq // TK
    ratio = TQ // TK
    hg = 4

    def body(qa_ref, ka_ref, kt_ref, v_ref, do_ref, o_ref, lse_ref, dq_ref, dk_ref, dv_ref, df_ref,
             dqt_s, row_s, dfk_s, dk_s, dv_s, dot_s, st_s, dp_s, pb_s, db_s, qt_s):
        eye = _eye(HEAD_DIM, BF16)
        eye2 = _eye(128, BF16)
        eye_k = _eye(TK, F32)
        lane8 = lax.broadcasted_iota(jnp.int32, (8, 128), 1)
        lane_k = lax.broadcasted_iota(jnp.int32, (TK, 128), 1)
        first = [lane8 < HEAD_DIM, lane8 >= HEAD_DIM]
        for pp in range(hg // 2):
            for ii in range(nq):
                dot_s[pp, ii] = _nt(eye2, do_ref[ii * TQ:(ii + 1) * TQ, 128 * pp:128 * (pp + 1)]).astype(BF16)
        for hh in range(hg):
            for ii in range(nq):
                qt_s[hh, ii] = _nt(eye2, qa_ref[hh, ii * TQ:(ii + 1) * TQ, :]).astype(BF16)
        for hh in range(hg):
            pp, e = divmod(hh, 2)
            head_lanes = jnp.where(first[e], 1.0, 0.0)
            for ii in range(nq):
                rows = slice(ii * TQ, (ii + 1) * TQ)
                prod = do_ref[rows, 128 * pp:128 * (pp + 1)].astype(F32) * o_ref[rows, 128 * pp:128 * (pp + 1)].astype(F32)
                row_s[hh, ii, 0] = _nt(head_lanes, prod, HI)
                row_s[hh, ii, 1] = jnp.broadcast_to(lse_ref[pp, e:e + 1, ii * TQ:(ii + 1) * TQ], (8, TQ))
                dqt_s[hh, ii] = jnp.zeros((128, TQ), F32)

        def kblock(j, _):
            krow = pl.multiple_of(j * TK, TK)
            spos = j * TK + lax.broadcasted_iota(jnp.int32, (TK, 1), 0)
            for hh in range(hg):
                dk_s[hh] = jnp.zeros((TK, 128), F32)
                dv_s[hh] = jnp.zeros((TK, 128), F32)

            def scores(i):
                for hh in range(hg):
                    pp, e = divmod(hh, 2)
                    own = (lane_k < HEAD_DIM) if e == 0 else (lane_k >= HEAD_DIM)
                    v2 = v_ref[pl.ds(krow, TK), 128 * pp:128 * (pp + 1)]
                    vj = jnp.where(own, v2, jnp.zeros_like(v2))
                    st_s[i & 1, hh] = _nn(ka_ref[hh, pl.ds(krow, TK), :], qt_s[hh, i])
                    dp_s[i & 1, hh] = _nn(vj, dot_s[pp, i])

            def elementwise(i, masked):
                slot = i & 1
                if masked:
                    keep = spos <= (i * TQ + lax.broadcasted_iota(jnp.int32, (1, TQ), 1))
                for hh in range(hg):
                    pt = jnp.exp(st_s[slot, hh] - row_s[hh, i, 1][0:1, :])
                    if masked:
                        pt = jnp.where(keep, pt, 0.0)
                    dst = pt * (dp_s[slot, hh] - row_s[hh, i, 0][0:1, :])
                    pb_s[slot, hh] = pt.astype(BF16)
                    db_s[slot, hh] = dst.astype(BF16)

            def grads(i):
                slot = i & 1
                qrow = pl.multiple_of(i * TQ, TQ)
                for hh in range(hg):
                    dst_b = db_s[slot, hh]
                    dv_s[hh] += _nn(pb_s[slot, hh], do_ref[pl.ds(qrow, TQ), 128 * (hh // 2):128 * (hh // 2 + 1)])
                    dk_s[hh] += _nn(dst_b, qa_ref[hh, pl.ds(qrow, TQ), :])
                    dqt_s[hh, i] += _nn(kt_ref[hh, j], dst_b)

            def step(i, carry):
                grads(i - 1)
                elementwise(i, False)
                scores(jnp.minimum(i + 1, nq - 1))
                return carry

            i0 = j // ratio
            scores(i0)
            elementwise(i0, True)
            scores(jnp.minimum(i0 + 1, nq - 1))
            lax.fori_loop(i0 + 1, nq, step, 0)
            grads(nq - 1)
            for pp in range(hg // 2):
                cols = slice(128 * pp, 128 * (pp + 1))
                dk_ref[pl.ds(krow, TK), cols] = jnp.concatenate(
                    [dk_s[2 * pp][:, :HEAD_DIM], dk_s[2 * pp + 1][:, :HEAD_DIM]], axis=1).astype(BF16)
                dv_ref[pl.ds(krow, TK), cols] = jnp.where(lane_k < HEAD_DIM, dv_s[2 * pp], dv_s[2 * pp + 1]).astype(BF16)
            for hh in range(hg):
                dfk_s[hh, j] = _tn(dk_s[hh][:, HEAD_DIM:HEAD_DIM + 8], eye_k, HI)
            return 0

        lax.fori_loop(0, nkb, kblock, 0)
        for pp in range(hg // 2):
            for ii in range(nq):
                parts = []
                for e in range(2):
                    dqt = dqt_s[2 * pp + e, ii]
                    parts.append(_tn(dqt[0:HEAD_DIM, :].astype(BF16), eye) * SCALE)
                    for kk in range(ratio):
                        jj = ii * ratio + kk
                        df_ref[pp, e:e + 1, jj * TK:(jj + 1) * TK] = (dqt[67:68, kk * TK:(kk + 1) * TK]
                                                                     - dfk_s[2 * pp + e, jj][0:1, :])
                dq_ref[ii * TQ:(ii + 1) * TQ, 128 * pp:128 * (pp + 1)] = jnp.concatenate(parts, axis=1).astype(BF16)

    aug_blk = pl.BlockSpec((hg, seq, 128), lambda b, g: (g, b, 0))
    pair_blk = pl.BlockSpec((seq, 64 * hg), lambda b, g: (b, g))
    row_blk = pl.BlockSpec((hg // 2, 2, seq), lambda b, g: (g, 0, b))
    return _pallas(
        body, name="fox_bwd", grid=(t_tok // seq, N_HEADS // hg),
        in_specs=[aug_blk, aug_blk, pl.BlockSpec((hg, nkb, 128, TK), lambda b, g: (g, b, 0, 0)),
                  pl.BlockSpec((seq, 64 * hg), lambda b, g: (b, C_BV // (64 * hg) + g)), pair_blk, pair_blk, row_blk],
        out_specs=[pair_blk, pair_blk, pair_blk, row_blk],
        out_shape=[jax.ShapeDtypeStruct((t_tok, 512), BF16)] * 3
        + [jax.ShapeDtypeStruct((N_HEADS // 2, 2, t_tok), F32)],
        scratch_shapes=[pltpu.VMEM((hg, nq, 128, TQ), F32), pltpu.VMEM((hg, nq, 2, 8, TQ), F32),
                        pltpu.VMEM((hg, nkb, 8, TK), F32), pltpu.VMEM((hg, TK, 128), F32),
                        pltpu.VMEM((hg, TK, 128), F32), pltpu.VMEM((hg // 2, nq, 128, TQ), BF16),
                        pltpu.VMEM((2, hg, TK, TQ), F32), pltpu.VMEM((2, hg, TK, TQ), F32),
                        pltpu.VMEM((2, hg, TK, TQ), BF16), pltpu.VMEM((2, hg, TK, TQ), BF16),
                        pltpu.VMEM((hg, nq, 128, TQ), BF16)],
        compiler_params=_cp(("arbitrary", "arbitrary"), VMEM_BIG))(q_aug, k_aug, kt, qkvg, du_b, b_out, lse)


def _fox_bwd_t_old(q_aug, k_aug, kt, qkvg, du_b, b_out, lse, seq):
    t_tok = qkvg.shape[0]
    nq = seq // TQ
    nkb = seq // TK
    ratio = TQ // TK

    def body(qa_ref, ka_ref, kt_ref, v_ref, do_ref, o_ref, lse_ref, dq_ref, dk_ref, dv_ref, df_ref,
             dqt_s, out_s, row_s, dfk_s):
        ones_b = jnp.ones((8, TQ), BF16)
        ones_f = jnp.ones((8, HEAD_DIM), F32)
        eye = _eye(HEAD_DIM, BF16)
        for e in range(2):
            lo, hi = e * HEAD_DIM, (e + 1) * HEAD_DIM
            for ii in range(nq):
                rows = slice(ii * TQ, (ii + 1) * TQ)
                do = do_ref[rows, :][:, lo:hi].astype(F32)
                ov = o_ref[rows, :][:, lo:hi].astype(F32)
                row_s[ii, 0] = _nt(ones_f, do * ov, HI)
                row_s[ii, 1] = jnp.broadcast_to(lse_ref[e:e + 1, ii * TQ:(ii + 1) * TQ], (8, TQ))
                dqt_s[ii] = jnp.zeros((128, TQ), F32)

            def kblock(j, _):
                krow = pl.multiple_of(j * TK, TK)
                kj = ka_ref[e, pl.ds(krow, TK), :]
                ktj = kt_ref[e, j]
                vj = v_ref[pl.ds(krow, TK), :][:, lo:hi]
                spos = j * TK + lax.broadcasted_iota(jnp.int32, (TK, 1), 0)

                def qblock(i, carry, masked):
                    dk_acc, dv_acc, dfk = carry
                    qrow = pl.multiple_of(i * TQ, TQ)
                    qa = qa_ref[e, pl.ds(qrow, TQ), :]
                    doh = do_ref[pl.ds(qrow, TQ), :][:, lo:hi]
                    pt = jnp.exp(_nt(kj, qa) - row_s[i, 1][0:1, :])
                    if masked:
                        tpos = i * TQ + lax.broadcasted_iota(jnp.int32, (1, TQ), 1)
                        pt = jnp.where(spos <= tpos, pt, 0.0)
                    dst = pt * (_nt(vj, doh) - row_s[i, 0][0:1, :])
                    dst_b = dst.astype(BF16)
                    dv_acc = dv_acc + _nn(pt.astype(BF16), doh)
                    dk_acc = dk_acc + _nn(dst_b, qa)
                    dqt_s[i] += _nn(ktj, dst_b)
                    dfk = dfk + _nt(ones_b, dst_b)
                    return dk_acc, dv_acc, dfk

                i0 = j // ratio
                carry = (jnp.zeros((TK, 128), F32), jnp.zeros((TK, HEAD_DIM), F32), jnp.zeros((8, TK), F32))
                carry = qblock(i0, carry, True)
                dk_acc, dv_acc, dfk = lax.fori_loop(i0 + 1, nq, functools.partial(qblock, masked=False), carry)
                out_s[1, e, pl.ds(krow, TK), :] = dk_acc[:, :HEAD_DIM]
                out_s[2, e, pl.ds(krow, TK), :] = dv_acc
                dfk_s[j] = dfk
                return 0

            lax.fori_loop(0, nkb, kblock, 0)
            for ii in range(nq):
                dqt = dqt_s[ii]
                out_s[0, e, ii * TQ:(ii + 1) * TQ, :] = _tn(dqt[0:HEAD_DIM, :].astype(BF16), eye) * SCALE
                for kk in range(ratio):
                    jj = ii * ratio + kk
                    df_ref[e:e + 1, jj * TK:(jj + 1) * TK] = dqt[67:68, kk * TK:(kk + 1) * TK] - dfk_s[jj][0:1, :]
        for k, ref in enumerate((dq_ref, dk_ref, dv_ref)):
            ref[...] = jnp.concatenate([out_s[k, 0], out_s[k, 1]], axis=1).astype(BF16)

    aug_blk = pl.BlockSpec((2, seq, 128), lambda b, p: (p, b, 0))
    pair_blk = pl.BlockSpec((seq, 128), lambda b, p: (b, p))
    row_blk = pl.BlockSpec((None, 2, seq), lambda b, p: (p, 0, b))
    return _pallas(
        body, name="fox_bwd", grid=(t_tok // seq, N_HEADS // 2),
        in_specs=[aug_blk, aug_blk, pl.BlockSpec((2, nkb, 128, TK), lambda b, p: (p, b, 0, 0)),
                  pl.BlockSpec((seq, 128), lambda b, p: (b, C_BV // 128 + p)), pair_blk, pair_blk, row_blk],
        out_specs=[pair_blk, pair_blk, pair_blk, row_blk],
        out_shape=[jax.ShapeDtypeStruct((t_tok, 512), BF16)] * 3
        + [jax.ShapeDtypeStruct((N_HEADS // 2, 2, t_tok), F32)],
        scratch_shapes=[pltpu.VMEM((nq, 128, TQ), F32), pltpu.VMEM((3, 2, seq, HEAD_DIM), F32),
                        pltpu.VMEM((nq, 2, 8, TQ), F32), pltpu.VMEM((nkb, 8, TK), F32)],
        compiler_params=_cp(("arbitrary", "arbitrary"), VMEM_BIG))(q_aug, k_aug, kt, qkvg, du_b, b_out, lse)


def _fox_fwd(qkvg, f_row, f_col, seq):
    t_tok = qkvg.shape[0]
    nq = seq // TQ

    def body(q_ref, k_ref, v_ref, fr_ref, fc_ref, o_ref, lse_ref, fk_s):
        i = pl.program_id(1)
        for jj in range(nq):
            fk_s[jj] = fr_ref[:, jj * TQ:(jj + 1) * TQ]
        fcol = fc_ref[...]
        tpos = i * TQ + lax.broadcasted_iota(jnp.int32, (TQ, 1), 0)
        lane = lax.broadcasted_iota(jnp.int32, (TQ, 128), 1)
        lse_tile = jnp.zeros((TQ, 128), F32)
        for p in range(N_HEADS // 2):
            q2 = q_ref[:, 128 * p:128 * (p + 1)]
            qs = [q2[:, :HEAD_DIM], q2[:, HEAD_DIM:]]
            fqs = [_col(fcol, 2 * p + e) for e in range(2)]

            def kblock(j, carry):
                row0 = pl.multiple_of(j * TQ, TQ)
                k2 = k_ref[pl.ds(row0, TQ), 128 * p:128 * (p + 1)]
                v2 = v_ref[pl.ds(row0, TQ), 128 * p:128 * (p + 1)]
                fk8 = fk_s[j]
                spos = j * TQ + lax.broadcasted_iota(jnp.int32, (1, TQ), 1)
                keep = spos <= tpos
                new = []
                for e in range(2):
                    m, l, acc = carry[3 * e:3 * e + 3]
                    kh = k2[:, e * HEAD_DIM:(e + 1) * HEAD_DIM]
                    vh = v2[:, e * HEAD_DIM:(e + 1) * HEAD_DIM]
                    s = _nt(qs[e], kh) * SCALE + (fqs[e] - fk8[2 * p + e:2 * p + e + 1, :])
                    s = jnp.where(keep, s, NEG)
                    m_new = jnp.maximum(m, jnp.max(s, axis=1, keepdims=True))
                    alpha = jnp.exp(m - m_new)
                    pe = jnp.exp(s - m_new)
                    l = alpha * l + jnp.sum(pe, axis=1, keepdims=True)
                    acc = alpha * acc + _nn(pe.astype(BF16), vh)
                    new += [m_new, l, acc]
                return tuple(new)

            init = (jnp.full((TQ, 1), NEG, F32), jnp.zeros((TQ, 1), F32), jnp.zeros((TQ, HEAD_DIM), F32)) * 2
            res = lax.fori_loop(0, i + 1, kblock, init)
            outs = []
            for e in range(2):
                m, l, acc = res[3 * e:3 * e + 3]
                outs.append(acc / l)
                lse_tile = jnp.where(lane == 2 * p + e, m + jnp.log(l), lse_tile)
            o_ref[:, 128 * p:128 * (p + 1)] = jnp.concatenate(outs, axis=1).astype(BF16)
        lse_ref[...] = lse_tile

    return _pallas(
        body, name="fox_fwd", grid=(t_tok // seq, nq),
        in_specs=[pl.BlockSpec((TQ, 512), lambda b, i: (b * nq + i, C_BQ // 512)),
                  pl.BlockSpec((seq, 512), lambda b, i: (b, C_BK // 512)),
                  pl.BlockSpec((seq, 512), lambda b, i: (b, C_BV // 512)),
                  pl.BlockSpec((N_HEADS, seq), lambda b, i: (0, b)),
                  pl.BlockSpec((TQ, 128), lambda b, i: (b * nq + i, 0))],
        out_specs=[pl.BlockSpec((TQ, 512), lambda b, i: (b * nq + i, 0)),
                   pl.BlockSpec((TQ, 128), lambda b, i: (b * nq + i, 0))],
        out_shape=[jax.ShapeDtypeStruct((t_tok, 512), BF16), jax.ShapeDtypeStruct((t_tok, 128), F32)],
        scratch_shapes=[pltpu.VMEM((nq, N_HEADS, TQ), F32)],
        compiler_params=_cp(("arbitrary", "arbitrary"), VMEM_MID))(qkvg, qkvg, qkvg, f_row, f_col)


def _fox_bwd(qkvg, du_b, b_out, lse, f_row, f_col, seq):
    t_tok = qkvg.shape[0]
    nq = seq // TQ

    def body(q_ref, k_ref, v_ref, do_ref, o_ref, lse_ref, fr_ref, fc_ref,
             dq_ref, dk_ref, dv_ref, df_ref, dq_s, dk_s, dv_s, col_s, df_s, fk_s):
        p = pl.program_id(1)
        for jj in range(nq):
            fk_s[jj] = fr_ref[:, jj * TQ:(jj + 1) * TQ]
        eye = (lax.broadcasted_iota(jnp.int32, (TQ, TQ), 0) == lax.broadcasted_iota(jnp.int32, (TQ, TQ), 1)).astype(F32)
        for e in range(2):
            h = 2 * p + e
            lo, hi = e * HEAD_DIM, (e + 1) * HEAD_DIM
            for ii in range(nq):
                rows = slice(ii * TQ, (ii + 1) * TQ)
                do = do_ref[rows, :][:, lo:hi].astype(F32)
                ov = o_ref[rows, :][:, lo:hi].astype(F32)
                col_s[0, rows, :] = jnp.sum(do * ov, axis=1, keepdims=True)
                col_s[1, rows, :] = _col(lse_ref[rows, :], h)
                col_s[2, rows, :] = _col(fc_ref[rows, :], h)
                dq_s[rows, :] = jnp.zeros((TQ, HEAD_DIM), F32)
                df_s[ii] = jnp.zeros((8, TQ), F32)
                col_s[3, rows, :] = jnp.zeros((TQ, 1), F32)

            def kblock(j, _):
                krow = pl.multiple_of(j * TQ, TQ)
                kh = k_ref[pl.ds(krow, TQ), :][:, lo:hi]
                vh = v_ref[pl.ds(krow, TQ), :][:, lo:hi]
                fk = _row(fk_s[j], h)
                spos = j * TQ + lax.broadcasted_iota(jnp.int32, (1, TQ), 1)

                def qblock(i, carry):
                    dk_acc, dv_acc, dfk = carry
                    qrow = pl.multiple_of(i * TQ, TQ)
                    qh = q_ref[pl.ds(qrow, TQ), :][:, lo:hi]
                    doh = do_ref[pl.ds(qrow, TQ), :][:, lo:hi]
                    delta = col_s[0, pl.ds(qrow, TQ), :]
                    lse_q = col_s[1, pl.ds(qrow, TQ), :]
                    fq = col_s[2, pl.ds(qrow, TQ), :]
                    tpos = i * TQ + lax.broadcasted_iota(jnp.int32, (TQ, 1), 0)
                    s = _nt(qh, kh) * SCALE + (fq - fk)
                    pr = jnp.where(spos <= tpos, jnp.exp(s - lse_q), 0.0)
                    dp = _nt(doh, vh)
                    ds = pr * (dp - delta)
                    ds_b = ds.astype(BF16)
                    dv_acc = dv_acc + _tn(pr.astype(BF16), doh)
                    dk_acc = dk_acc + _tn(ds_b, qh)
                    dq_s[pl.ds(qrow, TQ), :] += _nn(ds_b, kh)
                    col_s[3, pl.ds(qrow, TQ), :] += jnp.sum(ds, axis=1, keepdims=True)
                    dfk = dfk + jnp.sum(ds, axis=0, keepdims=True)
                    return dk_acc, dv_acc, dfk

                zero = jnp.zeros((TQ, HEAD_DIM), F32)
                dk_acc, dv_acc, dfk = lax.fori_loop(j, nq, qblock, (zero, zero, jnp.zeros((1, TQ), F32)))
                dk_s[e, pl.ds(krow, TQ), :] = dk_acc * SCALE
                dv_s[e, pl.ds(krow, TQ), :] = dv_acc
                df_s[j] -= jnp.broadcast_to(dfk, (8, TQ))
                return 0

            lax.fori_loop(0, nq, kblock, 0)
            dq_s2 = dq_s[...] * SCALE
            dk_s[2 + e] = dq_s2
            for ii in range(nq):
                dfq = jnp.broadcast_to(col_s[3, ii * TQ:(ii + 1) * TQ, :], (TQ, 128))
                df_ref[e:e + 1, ii * TQ:(ii + 1) * TQ] = _tn(dfq, eye, HI)[0:1, :] + df_s[ii][0:1, :]
        dq_ref[...] = jnp.concatenate([dk_s[2], dk_s[3]], axis=1).astype(BF16)
        dk_ref[...] = jnp.concatenate([dk_s[0], dk_s[1]], axis=1).astype(BF16)
        dv_ref[...] = jnp.concatenate([dv_s[0], dv_s[1]], axis=1).astype(BF16)

    blk = lambda off: pl.BlockSpec((seq, 128), lambda b, p: (b, off // 128 + p))
    out_blk = pl.BlockSpec((seq, 128), lambda b, p: (b, p))
    return _pallas(
        body, name="fox_bwd", grid=(t_tok // seq, N_HEADS // 2),
        in_specs=[blk(C_BQ), blk(C_BK), blk(C_BV), out_blk, out_blk,
                  pl.BlockSpec((seq, 128), lambda b, p: (b, 0)),
                  pl.BlockSpec((N_HEADS, seq), lambda b, p: (0, b)),
                  pl.BlockSpec((seq, 128), lambda b, p: (b, 0))],
        out_specs=[out_blk, out_blk, out_blk, pl.BlockSpec((None, 2, seq), lambda b, p: (p, 0, b))],
        out_shape=[jax.ShapeDtypeStruct((t_tok, 512), BF16)] * 3
        + [jax.ShapeDtypeStruct((N_HEADS // 2, 2, t_tok), F32)],
        scratch_shapes=[pltpu.VMEM((seq, HEAD_DIM), F32), pltpu.VMEM((4, seq, HEAD_DIM), F32),
                        pltpu.VMEM((2, seq, HEAD_DIM), F32), pltpu.VMEM((4, seq, 1), F32),
                        pltpu.VMEM((nq, 8, TQ), F32), pltpu.VMEM((nq, N_HEADS, TQ), F32)],
        compiler_params=_cp(("arbitrary", "arbitrary"), VMEM_BIG))(qkvg, qkvg, qkvg, du_b, b_out, lse, f_row, f_col)


def _swa_window(k_ref, v_ref, n):
    prev = pl.multiple_of(jnp.maximum(n - 1, 0) * BLOCK, BLOCK)
    cur = pl.multiple_of(n * BLOCK, BLOCK)
    kwin = jnp.concatenate([k_ref[pl.ds(prev, BLOCK), :], k_ref[pl.ds(cur, BLOCK), :]], axis=0)
    vwin = jnp.concatenate([v_ref[pl.ds(prev, BLOCK), :], v_ref[pl.ds(cur, BLOCK), :]], axis=0)
    ti = lax.broadcasted_iota(jnp.int32, (BLOCK, 2 * BLOCK), 0)
    sj = lax.broadcasted_iota(jnp.int32, (BLOCK, 2 * BLOCK), 1)
    rel = ti - sj + BLOCK
    first_key = jnp.where(n > 0, 0, BLOCK)
    mask = jnp.logical_and(jnp.logical_and(rel >= 0, rel < BLOCK), sj >= first_key)
    return kwin, vwin, mask, prev, cur


def _head_cols(ref, h):
    pair = ref[:, 128 * (h // 2):128 * (h // 2 + 1)]
    return pair[:, (h % 2) * HEAD_DIM:(h % 2 + 1) * HEAD_DIM]


def _swa_logits(q_ref, kwin, bias_ref, h, mask):
    hk = h // KV_GROUP
    s = _nt(_head_cols(q_ref, h), kwin[:, hk * HEAD_DIM:(hk + 1) * HEAD_DIM]) * SCALE + bias_ref[h]
    return jnp.where(mask, s, NEG)


def _swa_fwd(qkvg, bias, sinks, seq):
    t_tok = qkvg.shape[0]
    nb = seq // BLOCK

    def body(sink_ref, q_ref, k_ref, v_ref, bias_ref, o_ref, lse_ref, s_s, p_s, den_s):
        g = pl.program_id(1)
        subs = [pl.ds(s * BLOCK, BLOCK) for s in range(SWA_SUB)]
        wins = [_swa_window(k_ref, v_ref, SWA_SUB * g + s) for s in range(SWA_SUB)]
        for s in range(SWA_SUB):
            for h in range(N_HEADS):
                s_s[s * N_HEADS + h] = _swa_logits(q_ref.at[subs[s]], wins[s][0], bias_ref, h, wins[s][2])
        lane = lax.broadcasted_iota(jnp.int32, (BLOCK, 128), 1)
        for s in range(SWA_SUB):
            lse_tile = jnp.zeros((BLOCK, 128), F32)
            for h in range(N_HEADS):
                sc = s_s[s * N_HEADS + h]
                sink = sink_ref[h]
                m = jnp.maximum(jnp.max(sc, axis=1, keepdims=True), sink)
                pe = jnp.exp(sc - m)
                den = jnp.sum(pe, axis=1, keepdims=True) + jnp.exp(sink - m)
                p_s[s * N_HEADS + h] = pe.astype(BF16)
                den_s[s * N_HEADS + h] = den
                lse_tile = jnp.where(lane == h, m + jnp.log(den), lse_tile)
            lse_ref[subs[s], :] = lse_tile
        for s in range(SWA_SUB):
            vwin = wins[s][1]
            for pr in range(N_HEADS // 2):
                outs = []
                for h in (2 * pr, 2 * pr + 1):
                    hk = h // KV_GROUP
                    outs.append(_nn(p_s[s * N_HEADS + h], vwin[:, hk * HEAD_DIM:(hk + 1) * HEAD_DIM]) / den_s[s * N_HEADS + h])
                o_ref[subs[s], 128 * pr:128 * (pr + 1)] = jnp.concatenate(outs, axis=1).astype(BF16)

    rows = SWA_SUB * BLOCK
    steps = nb // SWA_SUB
    return _pallas(
        body, name="swa_fwd", grid=(t_tok // seq, steps),
        in_specs=[pl.BlockSpec(memory_space=pltpu.SMEM),
                  pl.BlockSpec((rows, 512), lambda b, n: (b * steps + n, C_AQ // 512)),
                  pl.BlockSpec((seq, 128), lambda b, n: (b, C_AK // 128)),
                  pl.BlockSpec((seq, 128), lambda b, n: (b, C_AV // 128)),
                  pl.BlockSpec((N_HEADS, BLOCK, 2 * BLOCK), lambda b, n: (0, 0, 0))],
        out_specs=[pl.BlockSpec((rows, 512), lambda b, n: (b * steps + n, 0)),
                   pl.BlockSpec((rows, 128), lambda b, n: (b * steps + n, 0))],
        out_shape=[jax.ShapeDtypeStruct((t_tok, 512), BF16), jax.ShapeDtypeStruct((t_tok, 128), F32)],
        scratch_shapes=[pltpu.VMEM((SWA_SUB * N_HEADS, BLOCK, 2 * BLOCK), F32),
                        pltpu.VMEM((SWA_SUB * N_HEADS, BLOCK, 2 * BLOCK), BF16),
                        pltpu.VMEM((SWA_SUB * N_HEADS, BLOCK, 1), F32)],
        compiler_params=_cp(("arbitrary", "arbitrary"), VMEM_MID))(sinks, qkvg, qkvg, qkvg, bias)


def _swa_bwd(qkvg, du_a, a_out, lse, bias, sinks, seq):
    t_tok = qkvg.shape[0]
    nb = seq // BLOCK

    def body(sink_ref, q_ref, k_ref, v_ref, do_ref, o_ref, lse_ref, bias_ref,
             dq_ref, dkv_ref, dbias_ref, dsink_ref, kv_s, s_s, dp_s, pb_s, db_s):
        b, n = pl.program_id(0), pl.program_id(1)

        @pl.when(jnp.logical_and(b == 0, n == 0))
        def _():
            dbias_ref[...] = jnp.zeros_like(dbias_ref)
            dsink_ref[...] = jnp.zeros_like(dsink_ref)

        @pl.when(n == 0)
        def _():
            kv_s[...] = jnp.zeros_like(kv_s)

        subs = [pl.ds(s * BLOCK, BLOCK) for s in range(SWA_SUB)]
        wins = [_swa_window(k_ref, v_ref, SWA_SUB * n + s) for s in range(SWA_SUB)]
        for s in range(SWA_SUB):
            kwin, vwin, mask = wins[s][:3]
            for h in range(N_HEADS):
                hk = h // KV_GROUP
                s_s[s * N_HEADS + h] = _swa_logits(q_ref.at[subs[s]], kwin, bias_ref, h, mask)
                dp_s[s * N_HEADS + h] = _nt(_head_cols(do_ref.at[subs[s]], h), vwin[:, hk * HEAD_DIM:(hk + 1) * HEAD_DIM])
        for s in range(SWA_SUB):
            lse_tile = lse_ref[subs[s], :]
            do_s, o_s = do_ref.at[subs[s]], o_ref.at[subs[s]]
            for h in range(N_HEADS):
                delta = jnp.sum(_head_cols(do_s, h).astype(F32) * _head_cols(o_s, h).astype(F32), axis=1, keepdims=True)
                lse_h = _col(lse_tile, h)
                pe = jnp.exp(s_s[s * N_HEADS + h] - lse_h)
                ds = pe * (dp_s[s * N_HEADS + h] - delta)
                dbias_ref[h] += ds
                psink = jnp.exp(sink_ref[h] - lse_h)
                dsink_ref[h:h + 1, :] += jnp.broadcast_to(jnp.sum(-psink * delta, axis=0, keepdims=True), (1, 128))
                pb_s[s * N_HEADS + h] = pe.astype(BF16)
                db_s[s * N_HEADS + h] = ds.astype(BF16)
        for s in range(SWA_SUB):
            kwin, _, _, prev, cur = wins[s]
            q_s, do_s = q_ref.at[subs[s]], do_ref.at[subs[s]]
            for pr in range(N_HEADS // 2):
                dqs = []
                for h in (2 * pr, 2 * pr + 1):
                    hk = h // KV_GROUP
                    dqs.append(_nn(db_s[s * N_HEADS + h], kwin[:, hk * HEAD_DIM:(hk + 1) * HEAD_DIM]) * SCALE)
                dq_ref[subs[s], 128 * pr:128 * (pr + 1)] = jnp.concatenate(dqs, axis=1).astype(BF16)
            dks, dvs = [], []
            for hk in range(N_HEADS // KV_GROUP):
                dk = jnp.zeros((2 * BLOCK, HEAD_DIM), F32)
                dv = jnp.zeros((2 * BLOCK, HEAD_DIM), F32)
                for h in range(hk * KV_GROUP, (hk + 1) * KV_GROUP):
                    dk = dk + _tn(db_s[s * N_HEADS + h], _head_cols(q_s, h))
                    dv = dv + _tn(pb_s[s * N_HEADS + h], _head_cols(do_s, h))
                dks.append(dk * SCALE)
                dvs.append(dv)
            upd = jnp.concatenate(dks + dvs, axis=1)
            kv_s[pl.ds(prev, BLOCK), :] += upd[:BLOCK]
            kv_s[pl.ds(cur, BLOCK), :] += upd[BLOCK:]

        @pl.when(n == steps - 1)
        def _():
            dkv_ref[...] = kv_s[...].astype(BF16)

    rows = SWA_SUB * BLOCK
    steps = nb // SWA_SUB
    tile = (SWA_SUB * N_HEADS, BLOCK, 2 * BLOCK)
    return _pallas(
        body, name="swa_bwd", grid=(t_tok // seq, steps),
        in_specs=[pl.BlockSpec(memory_space=pltpu.SMEM),
                  pl.BlockSpec((rows, 512), lambda b, n: (b * steps + n, C_AQ // 512)),
                  pl.BlockSpec((seq, 128), lambda b, n: (b, C_AK // 128)),
                  pl.BlockSpec((seq, 128), lambda b, n: (b, C_AV // 128)),
                  pl.BlockSpec((rows, 512), lambda b, n: (b * steps + n, 0)),
                  pl.BlockSpec((rows, 512), lambda b, n: (b * steps + n, 0)),
                  pl.BlockSpec((rows, 128), lambda b, n: (b * steps + n, 0)),
                  pl.BlockSpec((N_HEADS, BLOCK, 2 * BLOCK), lambda b, n: (0, 0, 0))],
        out_specs=[pl.BlockSpec((rows, 512), lambda b, n: (b * steps + n, 0)),
                   pl.BlockSpec((seq, 256), lambda b, n: (b, 0)),
                   pl.BlockSpec((N_HEADS, BLOCK, 2 * BLOCK), lambda b, n: (0, 0, 0)),
                   pl.BlockSpec((N_HEADS, 128), lambda b, n: (0, 0))],
        out_shape=[jax.ShapeDtypeStruct((t_tok, 512), BF16), jax.ShapeDtypeStruct((t_tok, 256), BF16),
                   jax.ShapeDtypeStruct((N_HEADS, BLOCK, 2 * BLOCK), F32), jax.ShapeDtypeStruct((N_HEADS, 128), F32)],
        scratch_shapes=[pltpu.VMEM((seq, 256), F32), pltpu.VMEM(tile, F32), pltpu.VMEM(tile, F32),
                        pltpu.VMEM(tile, BF16), pltpu.VMEM(tile, BF16)],
        compiler_params=_cp(("arbitrary", "arbitrary"), VMEM_MID))(sinks, qkvg, qkvg, qkvg, du_a, a_out, lse, bias)


def _out_proj(name, u_parts, gate_arr, gate_blk, w_out, x, gmod, seq):
    t_tok = x.shape[0]
    nu = len(u_parts)

    def body(*refs):
        u_refs = refs[:nu]
        g_ref, w_ref, x_ref, gm_ref, yg_ref, y_ref, xn_ref = refs[nu:]
        u = jnp.concatenate([r[...].astype(F32) for r in u_refs], axis=1) if nu > 1 else u_refs[0][...].astype(F32)
        yg = (u * _silu(g_ref[...].astype(F32))).astype(BF16)
        yg_ref[...] = yg
        y = _nn(yg, w_ref[...])
        y_ref[...] = y.astype(BF16)
        xn_ref[...] = x_ref[...] + gm_ref[...] * y

    row = lambda w: pl.BlockSpec((TM, w), lambda i: (i, 0))
    in_specs = [row(u.shape[1]) for u in u_parts]
    in_specs += [pl.BlockSpec((TM, D_MODEL), lambda i: (i, gate_blk)),
                 pl.BlockSpec((D_MODEL, D_MODEL), lambda i: (0, 0)), row(D_MODEL),
                 pl.BlockSpec((None, 1, D_MODEL), lambda i: (i * TM // seq, 0, 0))]
    return _pallas(
        body, name=name, grid=(t_tok // TM,), in_specs=in_specs,
        out_specs=[row(D_MODEL)] * 3,
        out_shape=[jax.ShapeDtypeStruct((t_tok, D_MODEL), BF16)] * 2 + [jax.ShapeDtypeStruct((t_tok, D_MODEL), F32)],
        compiler_params=_cp(("arbitrary",), VMEM_MID))(*u_parts, gate_arr, w_out, x, gmod)


def _out_proj_bwd(name, dxn, gmod, y, w_out, seq, attn=None):
    t_tok = dxn.shape[0]
    tiles_per_seq = seq // TM

    def body(*refs):
        if attn is None:
            dxn_ref, gm_ref, y_ref, w_ref, dy_ref, dgm_ref, dyg_ref = refs
        else:
            dxn_ref, gm_ref, y_ref, w_ref, a_ref, b_ref, g_ref, dy_ref, dgm_ref, dua_ref, dub_ref, dg_ref = refs
        i = pl.program_id(0)
        dxv = dxn_ref[...]
        dy = (dxv * gm_ref[...]).astype(BF16)
        dy_ref[...] = dy

        @pl.when(i % tiles_per_seq == 0)
        def _():
            dgm_ref[...] = jnp.zeros_like(dgm_ref)

        dgm_ref[...] += jnp.sum(dxv * y_ref[...].astype(F32), axis=0, keepdims=True)
        dyg = _nn(dy, w_ref[...])
        if attn is None:
            dyg_ref[...] = dyg
        else:
            gt = g_ref[...].astype(F32)
            du = dyg * _silu(gt)
            dua_ref[...] = du[:, :512].astype(BF16)
            dub_ref[...] = du[:, 512:].astype(BF16)
            u = jnp.concatenate([a_ref[...].astype(F32), b_ref[...].astype(F32)], axis=1)
            dg_ref[...] = (dyg * u * _dsilu(gt)).astype(BF16)

    row = lambda w: pl.BlockSpec((TM, w), lambda i: (i, 0))
    mod_spec = pl.BlockSpec((None, 1, D_MODEL), lambda i: (i * TM // seq, 0, 0))
    in_specs = [row(D_MODEL), mod_spec, row(D_MODEL), pl.BlockSpec((D_MODEL, D_MODEL), lambda i: (0, 0))]
    out_specs = [row(D_MODEL), mod_spec]
    out_shape = [jax.ShapeDtypeStruct((t_tok, D_MODEL), BF16), jax.ShapeDtypeStruct(gmod.shape, F32)]
    args = [dxn, gmod, y, w_out]
    if attn is None:
        out_specs.append(row(D_MODEL))
        out_shape.append(jax.ShapeDtypeStruct((t_tok, D_MODEL), F32))
    else:
        in_specs += [row(512), row(512), pl.BlockSpec((TM, D_MODEL), lambda i: (i, C_GATE // D_MODEL))]
        out_specs += [row(512), row(512), row(D_MODEL)]
        out_shape += [jax.ShapeDtypeStruct((t_tok, 512), BF16)] * 2 + [jax.ShapeDtypeStruct((t_tok, D_MODEL), BF16)]
        args += list(attn)
    return _pallas(body, name=name, grid=(t_tok // TM,), in_specs=in_specs, out_specs=out_specs,
                   out_shape=out_shape, compiler_params=_cp(("arbitrary",), VMEM_MID))(*args)


def _norm_bwd(name, parts, w, x, g, scale, dxn, seq, rows_part=None):
    t_tok = x.shape[0]
    npart = len(parts)
    tiles_per_seq = seq // TM
    nrow_in = 0 if rows_part is None else 2

    def body(*refs):
        p_refs = refs[:npart]
        w_ref, x_ref, g_ref, sc_ref, dxn_ref = refs[npart:npart + 5]
        dx_ref, dss_ref, dg_ref = refs[npart + 5 + nrow_in:]
        i = pl.program_id(0)
        dh = jnp.zeros((TM, D_MODEL), F32)
        if rows_part is not None:
            r_ref, wr_ref = refs[npart + 5:npart + 7]
            dh = dh + _tn(r_ref[...].astype(BF16), wr_ref[...])
        for (arr, off), p_ref in zip(parts, p_refs):
            width = arr.shape[1]
            for j in range(width // 256):
                pj = p_ref[:, j * 256:(j + 1) * 256]
                c0 = off + j * 256
                dh = dh + _nn(pj, w_ref[c0:c0 + 256, :])
        xv = x_ref[...]
        rstd = lax.rsqrt(jnp.mean(xv * xv, axis=-1, keepdims=True) + EPS)
        xhat = xv * rstd
        gv = g_ref[...]
        nrm = xhat * gv

        @pl.when(i % tiles_per_seq == 0)
        def _():
            dss_ref[...] = jnp.zeros_like(dss_ref)

        @pl.when(i == 0)
        def _():
            dg_ref[...] = jnp.zeros_like(dg_ref)

        dss_ref[0:1, :] += jnp.sum(dh, axis=0, keepdims=True)
        dss_ref[1:2, :] += jnp.sum(dh * nrm, axis=0, keepdims=True)
        dn = dh * (1.0 + sc_ref[...])
        dg_ref[0:1, :] += jnp.sum(dn * xhat, axis=0, keepdims=True)
        dxhat = dn * gv
        dx_ref[...] = rstd * (dxhat - xhat * jnp.mean(dxhat * xhat, axis=-1, keepdims=True)) + dxn_ref[...]

    row = lambda wd: pl.BlockSpec((TM, wd), lambda i: (i, 0))
    w_spec = pl.BlockSpec(w.shape, lambda i: (0, 0))
    in_specs = [row(a.shape[1]) for a, _ in parts]
    in_specs += [w_spec, row(D_MODEL), pl.BlockSpec((1, D_MODEL), lambda i: (0, 0)),
                 pl.BlockSpec((None, 1, D_MODEL), lambda i: (i * TM // seq, 0, 0)), row(D_MODEL)]
    args = [a for a, _ in parts] + [w, x, g, scale, dxn]
    if rows_part is not None:
        in_specs += [pl.BlockSpec((8, TM), lambda i: (0, i)), pl.BlockSpec((8, D_MODEL), lambda i: (0, 0))]
        args += list(rows_part)
    nseq = t_tok // seq
    return _pallas(
        body, name=name, grid=(t_tok // TM,), in_specs=in_specs,
        out_specs=[row(D_MODEL), pl.BlockSpec((None, 8, D_MODEL), lambda i: (i * TM // seq, 0, 0)),
                   pl.BlockSpec((8, D_MODEL), lambda i: (0, 0))],
        out_shape=[jax.ShapeDtypeStruct((t_tok, D_MODEL), F32), jax.ShapeDtypeStruct((nseq, 8, D_MODEL), F32),
                   jax.ShapeDtypeStruct((8, D_MODEL), F32)],
        compiler_params=_cp(("arbitrary",), VMEM_BIG))(*args)


def _dw(name, a, parts, blocked=None):
    t_tok, ka = a.shape
    tt = 512
    npart = len(parts)
    nt = t_tok // tt

    def body(*refs):
        a_ref = refs[0]
        p_refs = refs[1:1 + npart]
        o_refs = refs[1 + npart:1 + 2 * npart]
        acc_refs = refs[1 + 2 * npart:]
        t = pl.program_id(0)
        av = a_ref[...]
        for p_ref, acc in zip(p_refs, acc_refs):
            upd = _tn(av, p_ref[...])

            @pl.when(t == 0)
            def _():
                acc[...] = upd

            @pl.when(t > 0)
            def _():
                acc[...] += upd

        @pl.when(t == nt - 1)
        def _():
            for o_ref, acc in zip(o_refs, acc_refs):
                if blocked is None:
                    o_ref[...] = acc[...].astype(BF16)
                else:
                    for j in range(o_ref.shape[0]):
                        o_ref[j] = acc[:, j * blocked:(j + 1) * blocked].astype(BF16)

    in_specs = [pl.BlockSpec((tt, ka), lambda t: (t, 0))]
    in_specs += [pl.BlockSpec((tt, p.shape[1]), lambda t: (t, 0)) for p in parts]
    if blocked is None:
        out_shape = [jax.ShapeDtypeStruct((ka, p.shape[1]), BF16) for p in parts]
        out_specs = [pl.BlockSpec((ka, p.shape[1]), lambda t: (0, 0)) for p in parts]
    else:
        out_shape = [jax.ShapeDtypeStruct((p.shape[1] // blocked, ka, blocked), BF16) for p in parts]
        out_specs = [pl.BlockSpec((p.shape[1] // blocked, ka, blocked), lambda t: (0, 0, 0)) for p in parts]
    return _pallas(body, name=name, grid=(nt,), in_specs=in_specs, out_specs=out_specs, out_shape=out_shape,
                   scratch_shapes=[pltpu.VMEM((ka, p.shape[1]), F32) for p in parts],
                   compiler_params=_cp(("arbitrary",), VMEM_BIG))(a, *parts)


def _dw_rows(name, rows_t, h):
    t_tok = h.shape[0]
    tt = 512

    def body(r_ref, h_ref, o_ref):
        @pl.when(pl.program_id(0) == 0)
        def _():
            o_ref[...] = jnp.zeros_like(o_ref)

        o_ref[...] += _nn(r_ref[...].astype(BF16), h_ref[...])

    return _pallas(body, name=name, grid=(t_tok // tt,),
                   in_specs=[pl.BlockSpec((8, tt), lambda t: (0, t)), pl.BlockSpec((tt, D_MODEL), lambda t: (t, 0))],
                   out_specs=pl.BlockSpec((8, D_MODEL), lambda t: (0, 0)),
                   out_shape=jax.ShapeDtypeStruct((8, D_MODEL), F32),
                   compiler_params=_cp(("arbitrary",), VMEM_MID))(rows_t, h)


def _lru_gates(xc, blk, wa_ref, wx_ref, ba_ref, bx_ref, sp):
    cols = slice(blk * LRU_BLOCK_W, (blk + 1) * LRU_BLOCK_W)
    xb = xc[:, cols].astype(BF16)
    r = _sigmoid(_nn(xb, wa_ref[blk].astype(BF16)) + ba_ref[:, cols])
    ig = _sigmoid(_nn(xb, wx_ref[blk].astype(BF16)) + bx_ref[:, cols])
    log_a = -LRU_C * r * sp[:, cols]
    a = jnp.exp(log_a)
    x2 = 2.0 * log_a
    series = -x2 * (1.0 + x2 * (0.5 + x2 * (1.0 / 6.0)))
    z = jnp.where(x2 > -0.01, series, 1.0 - a * a)
    mult = z * lax.rsqrt(jnp.maximum(z, 1e-30))
    return xb, r, ig, a, mult


def _softplus_neg(lam):
    return jnp.maximum(-lam, 0.0) + jnp.log(1.0 + jnp.exp(-jnp.abs(lam)))


def _conv_taps(xe_ref, cw_ref, cb_ref):
    xc = cb_ref[...] + xe_ref[8:8 + TC, :] * cw_ref[3:4, :]
    for k in range(1, 4):
        xc = xc + xe_ref[8 - k:8 - k + TC, :] * cw_ref[3 - k:4 - k, :]
    return xc


def _lru_fwd(proj, cw, cb, w_a, b_a, w_x, b_x, lam, seq):
    t_tok = proj.shape[0]
    nc = seq // TC

    def body(x_ref, cw_ref, cb_ref, wa_ref, ba_ref, wx_ref, bx_ref, lam_ref, hs_ref, xe_s, a_s, u_s, h_s):
        c = pl.program_id(1)

        @pl.when(c == 0)
        def _():
            xe_s[0:8, :] = jnp.zeros((8, D_MODEL), F32)
            h_s[...] = jnp.zeros_like(h_s)

        xe_s[8:8 + TC, :] = x_ref[...]
        xc = _conv_taps(xe_s, cw_ref, cb_ref)
        sp = _softplus_neg(lam_ref[...])
        for blk in range(LRU_BLOCKS):
            cols = slice(blk * LRU_BLOCK_W, (blk + 1) * LRU_BLOCK_W)
            _, _, ig, a, mult = _lru_gates(xc, blk, wa_ref, wx_ref, ba_ref, bx_ref, sp)
            a_s[:, cols] = a
            u_s[:, cols] = mult * ig * xc[:, cols]

        def step(t, h):
            h = a_s[pl.ds(t, 1), :] * h + u_s[pl.ds(t, 1), :]
            hs_ref[pl.ds(t, 1), :] = h
            return h

        h_s[0:1, :] = lax.fori_loop(0, TC, step, h_s[0:1, :], unroll=8)
        xe_s[0:8, :] = xe_s[TC:TC + 8, :]

    full = lambda shape: pl.BlockSpec(shape, lambda b, c: (0,) * len(shape))
    return _pallas(
        body, name="lru_fwd", grid=(t_tok // seq, nc),
        in_specs=[pl.BlockSpec((TC, D_MODEL), lambda b, c: (b * nc + c, 0)), full((4, D_MODEL)), full((1, D_MODEL)),
                  full((LRU_BLOCKS, LRU_BLOCK_W, LRU_BLOCK_W)), full((1, D_MODEL)),
                  full((LRU_BLOCKS, LRU_BLOCK_W, LRU_BLOCK_W)), full((1, D_MODEL)), full((1, D_MODEL))],
        out_specs=pl.BlockSpec((TC, D_MODEL), lambda b, c: (b * nc + c, 0)),
        out_shape=jax.ShapeDtypeStruct((t_tok, D_MODEL), F32),
        scratch_shapes=[pltpu.VMEM((TC + 8, D_MODEL), F32), pltpu.VMEM((TC, D_MODEL), F32),
                        pltpu.VMEM((TC, D_MODEL), F32), pltpu.VMEM((8, D_MODEL), F32)],
        compiler_params=_cp(("arbitrary", "arbitrary"), VMEM_MID))(proj, cw, cb, w_a, b_a, w_x, b_x, lam)


def _lru_bwd(proj, hs, dyh, cw, cb, w_a, b_a, w_x, b_x, lam, seq):
    t_tok = proj.shape[0]
    nc = seq // TC

    def body(x_ref, xh_ref, g_ref, hs_ref, hh_ref, dy_ref, cw_ref, cb_ref, wa_ref, ba_ref, wx_ref, bx_ref, lam_ref,
             dp_ref, dcw_ref, dvec_ref, dwa_ref, dwx_ref,
             xe_s, he_s, de_s, a_s, r_s, i_s, m_s, dh_s, carry_s):
        b, cr = pl.program_id(0), pl.program_id(1)
        c = nc - 1 - cr

        @pl.when(jnp.logical_and(b == 0, cr == 0))
        def _():
            dcw_ref[...] = jnp.zeros_like(dcw_ref)
            dvec_ref[...] = jnp.zeros_like(dvec_ref)
            dwa_ref[...] = jnp.zeros_like(dwa_ref)
            dwx_ref[...] = jnp.zeros_like(dwx_ref)

        @pl.when(cr == 0)
        def _():
            carry_s[...] = jnp.zeros_like(carry_s)
            de_s[TC:TC + 8, :] = jnp.zeros((8, D_MODEL), F32)

        first = c == 0
        xe_s[0:8, :] = jnp.where(first, 0.0, xh_ref[...])
        xe_s[8:8 + TC, :] = x_ref[...]
        he_s[0:8, :] = jnp.where(first, 0.0, hh_ref[...])
        he_s[8:8 + TC, :] = hs_ref[...]
        xc = _conv_taps(xe_s, cw_ref, cb_ref)
        lam_v = lam_ref[...]
        sp = _softplus_neg(lam_v)
        for blk in range(LRU_BLOCKS):
            cols = slice(blk * LRU_BLOCK_W, (blk + 1) * LRU_BLOCK_W)
            _, r, ig, a, mult = _lru_gates(xc, blk, wa_ref, wx_ref, ba_ref, bx_ref, sp)
            a_s[:, cols], r_s[:, cols], i_s[:, cols], m_s[:, cols] = a, r, ig, mult

        gt = g_ref[...]
        dyh = dy_ref[...]
        dh_s[...] = dyh * _silu(gt)
        dp_ref[:, D_MODEL:] = (dyh * hs_ref[...] * _dsilu(gt)).astype(BF16)

        def step(k, carry):
            t = TC - 1 - k
            dh = dh_s[pl.ds(t, 1), :] + carry
            dh_s[pl.ds(t, 1), :] = dh
            return a_s[pl.ds(t, 1), :] * dh

        carry_s[0:1, :] = lax.fori_loop(0, TC, step, carry_s[0:1, :], unroll=8)

        hprev = he_s[7:7 + TC, :]
        for blk in range(LRU_BLOCKS):
            cols = slice(blk * LRU_BLOCK_W, (blk + 1) * LRU_BLOCK_W)
            xcb = xc[:, cols]
            a, r, ig, mult, dh = a_s[:, cols], r_s[:, cols], i_s[:, cols], m_s[:, cols], dh_s[:, cols]
            spb = sp[:, cols]
            dmult = dh * ig * xcb
            di = dh * mult * xcb
            dxc = dh * mult * ig
            dla = dh * hprev[:, cols] * a - dmult * (a * a) * lax.rsqrt(jnp.maximum(mult * mult, 1e-30))
            dr = dla * (-LRU_C * spb)
            dsp = jnp.sum(dla * (-LRU_C * r), axis=0, keepdims=True)
            dga = dr * r * (1.0 - r)
            dgx = di * ig * (1.0 - ig)
            dga_b, dgx_b = dga.astype(BF16), dgx.astype(BF16)
            xb = xcb.astype(BF16)
            dxc = dxc + _nt(dga_b, wa_ref[blk].astype(BF16)) + _nt(dgx_b, wx_ref[blk].astype(BF16))
            dwa_ref[blk] += _tn(xb, dga_b)
            dwx_ref[blk] += _tn(xb, dgx_b)
            dvec_ref[1:2, cols] += jnp.sum(dga, axis=0, keepdims=True)
            dvec_ref[2:3, cols] += jnp.sum(dgx, axis=0, keepdims=True)
            dvec_ref[3:4, cols] += dsp * (-1.0 / (1.0 + jnp.exp(lam_v[:, cols])))
            de_s[0:TC, cols] = dxc

        dxc = de_s[0:TC, :]
        dvec_ref[0:1, :] += jnp.sum(dxc, axis=0, keepdims=True)
        dxr = dxc * cw_ref[3:4, :]
        dcw_ref[3:4, :] += jnp.sum(dxc * xe_s[8:8 + TC, :], axis=0, keepdims=True)
        for k in range(1, 4):
            dxr = dxr + de_s[k:k + TC, :] * cw_ref[3 - k:4 - k, :]
            dcw_ref[3 - k:4 - k, :] += jnp.sum(dxc * xe_s[8 - k:8 - k + TC, :], axis=0, keepdims=True)
        dp_ref[:, :D_MODEL] = dxr.astype(BF16)
        de_s[TC:TC + 8, :] = de_s[0:8, :]

    chunk = lambda col: pl.BlockSpec((TC, D_MODEL), lambda b, cr: (b * nc + nc - 1 - cr, col))
    halo = lambda col: pl.BlockSpec(
        (8, D_MODEL), lambda b, cr: (jnp.maximum((b * nc + nc - 1 - cr) * (TC // 8) - 1, 0), col))
    full = lambda shape: pl.BlockSpec(shape, lambda b, cr: (0,) * len(shape))
    wblk = (LRU_BLOCKS, LRU_BLOCK_W, LRU_BLOCK_W)
    return _pallas(
        body, name="lru_bwd", grid=(t_tok // seq, nc),
        in_specs=[chunk(0), halo(0), chunk(1), chunk(0), halo(0), chunk(0),
                  full((4, D_MODEL)), full((1, D_MODEL)), full(wblk), full((1, D_MODEL)), full(wblk),
                  full((1, D_MODEL)), full((1, D_MODEL))],
        out_specs=[pl.BlockSpec((TC, 2 * D_MODEL), lambda b, cr: (b * nc + nc - 1 - cr, 0)),
                   full((8, D_MODEL)), full((8, D_MODEL)), full(wblk), full(wblk)],
        out_shape=[jax.ShapeDtypeStruct((t_tok, 2 * D_MODEL), BF16), jax.ShapeDtypeStruct((8, D_MODEL), F32),
                   jax.ShapeDtypeStruct((8, D_MODEL), F32), jax.ShapeDtypeStruct(wblk, F32),
                   jax.ShapeDtypeStruct(wblk, F32)],
        scratch_shapes=[pltpu.VMEM((TC + 8, D_MODEL), F32), pltpu.VMEM((TC + 8, D_MODEL), F32),
                        pltpu.VMEM((TC + 8, D_MODEL), F32)]
        + [pltpu.VMEM((TC, D_MODEL), F32)] * 5 + [pltpu.VMEM((8, D_MODEL), F32)],
        compiler_params=_cp(("arbitrary", "arbitrary"), VMEM_BIG),
    )(proj, proj, proj, hs, hs, dyh, cw, cb, w_a, b_a, w_x, b_x, lam)


def _last_layer_tail(hs, proj, w_out, w_out_t, x, gmod, final_g, target, seq):
    t_tok = x.shape[0]
    tiles_per_seq = seq // TM

    def body(hs_ref, g_ref, w_ref, wt_ref, x_ref, gm_ref, fg_ref, t_ref,
             yg_ref, dx_ref, dy_ref, dyg_ref, dgm_ref, loss_ref, dfg_ref):
        i = pl.program_id(0)

        @pl.when(i == 0)
        def _():
            loss_ref[...] = jnp.zeros_like(loss_ref)
            dfg_ref[...] = jnp.zeros_like(dfg_ref)

        @pl.when(i % tiles_per_seq == 0)
        def _():
            dgm_ref[...] = jnp.zeros_like(dgm_ref)

        gm = gm_ref[...]
        yg = (hs_ref[...] * _silu(g_ref[...])).astype(BF16)
        yg_ref[...] = yg
        y = _nn(yg, w_ref[...])
        xv = x_ref[...] + gm * y
        gv = fg_ref[...]
        rstd = lax.rsqrt(jnp.mean(xv * xv, axis=-1, keepdims=True) + EPS)
        xhat = xv * rstd
        err = xhat * gv - t_ref[...]
        loss_ref[0:1, :] += jnp.sum(err * err, axis=0, keepdims=True) * (0.5 / D_MODEL)
        dout = err * (1.0 / D_MODEL)
        dfg_ref[0:1, :] += jnp.sum(dout * xhat, axis=0, keepdims=True)
        dxhat = dout * gv
        dxv = rstd * (dxhat - xhat * jnp.mean(dxhat * xhat, axis=-1, keepdims=True))
        dx_ref[...] = dxv
        dgm_ref[...] += jnp.sum(dxv * y, axis=0, keepdims=True)
        dy = (dxv * gm).astype(BF16)
        dy_ref[...] = dy
        dyg_ref[...] = _nn(dy, wt_ref[...])

    row = pl.BlockSpec((TM, D_MODEL), lambda i: (i, 0))
    acc = pl.BlockSpec((8, D_MODEL), lambda i: (0, 0))
    mod_spec = pl.BlockSpec((None, 1, D_MODEL), lambda i: (i * TM // seq, 0, 0))
    return _pallas(
        body, name="last_layer_tail", grid=(t_tok // TM,),
        in_specs=[row, pl.BlockSpec((TM, D_MODEL), lambda i: (i, 1)), pl.BlockSpec((D_MODEL, D_MODEL), lambda i: (0, 0)),
                  pl.BlockSpec((D_MODEL, D_MODEL), lambda i: (0, 0)),
                  row, mod_spec, pl.BlockSpec((1, D_MODEL), lambda i: (0, 0)), row],
        out_specs=[row, row, row, row, mod_spec, acc, acc],
        out_shape=[jax.ShapeDtypeStruct((t_tok, D_MODEL), BF16), jax.ShapeDtypeStruct((t_tok, D_MODEL), F32),
                   jax.ShapeDtypeStruct((t_tok, D_MODEL), BF16), jax.ShapeDtypeStruct((t_tok, D_MODEL), F32),
                   jax.ShapeDtypeStruct(gmod.shape, F32), jax.ShapeDtypeStruct((8, D_MODEL), F32),
                   jax.ShapeDtypeStruct((8, D_MODEL), F32)],
        compiler_params=_cp(("arbitrary",), VMEM_BIG))(hs, proj, w_out, w_out_t, x, gmod, final_g, target)


def _final_loss(x, g, target):
    t_tok = x.shape[0]

    def body(x_ref, g_ref, t_ref, dx_ref, loss_ref, dg_ref):
        @pl.when(pl.program_id(0) == 0)
        def _():
            loss_ref[...] = jnp.zeros_like(loss_ref)
            dg_ref[...] = jnp.zeros_like(dg_ref)

        xv = x_ref[...]
        gv = g_ref[...]
        rstd = lax.rsqrt(jnp.mean(xv * xv, axis=-1, keepdims=True) + EPS)
        xhat = xv * rstd
        err = xhat * gv - t_ref[...]
        loss_ref[0:1, :] += jnp.sum(err * err, axis=0, keepdims=True) * (0.5 / D_MODEL)
        dout = err * (1.0 / D_MODEL)
        dg_ref[0:1, :] += jnp.sum(dout * xhat, axis=0, keepdims=True)
        dxhat = dout * gv
        dx_ref[...] = rstd * (dxhat - xhat * jnp.mean(dxhat * xhat, axis=-1, keepdims=True))

    row = pl.BlockSpec((TM, D_MODEL), lambda i: (i, 0))
    acc = pl.BlockSpec((8, D_MODEL), lambda i: (0, 0))
    return _pallas(body, name="final_loss", grid=(t_tok // TM,),
                   in_specs=[row, pl.BlockSpec((1, D_MODEL), lambda i: (0, 0)), row],
                   out_specs=[row, acc, acc],
                   out_shape=[jax.ShapeDtypeStruct((t_tok, D_MODEL), F32)] + [jax.ShapeDtypeStruct((8, D_MODEL), F32)] * 2,
                   compiler_params=_cp(("arbitrary",), VMEM_MID))(x, g, target)


def _adam_math(w, g, m, v):
    m_new = ADAM_B1 * m + (1.0 - ADAM_B1) * g
    v_new = ADAM_B2 * v + (1.0 - ADAM_B2) * (g * g)
    m_hat = m_new / (1.0 - ADAM_B1 ** ADAM_STEP)
    v_hat = v_new / (1.0 - ADAM_B2 ** ADAM_STEP)
    delta = -ADAM_LR * (m_hat / (jnp.sqrt(v_hat) + ADAM_EPS) + ADAM_WD * w)
    return delta, m_new, v_new


def _sum_leading(name, x, out_dtype=F32):
    n, rows, cols = x.shape
    tr = PACK_ROWS if rows % PACK_ROWS == 0 else rows

    def body(x_ref, o_ref):
        acc = x_ref[0].astype(F32)
        for d in range(1, n):
            acc = acc + x_ref[d].astype(F32)
        o_ref[...] = acc.astype(out_dtype)

    return _pallas(body, name=name, grid=(rows // tr,),
                   in_specs=[pl.BlockSpec((n, tr, cols), lambda i: (0, i, 0))],
                   out_specs=pl.BlockSpec((tr, cols), lambda i: (i, 0)),
                   out_shape=jax.ShapeDtypeStruct((rows, cols), out_dtype),
                   compiler_params=_cp(("arbitrary",), VMEM_MID))(x)


def _adamw(name, w, m, v, g=None, parts=None):
    rows, cols = w.shape
    tr = rows if rows <= 256 else 256

    def body(*refs):
        w_ref, m_ref, v_ref, g_in, g_ref, d_ref, mo_ref, vo_ref = refs
        if parts is None:
            gv = g_in[...]
        else:
            acc = g_in[0].astype(F32)
            for d in range(1, parts.shape[0]):
                acc = acc + g_in[d].astype(F32)
            gv = acc[:, :cols]
        delta, m_new, v_new = _adam_math(w_ref[...], gv, m_ref[...], v_ref[...])
        g_ref[...] = gv
        d_ref[...] = delta
        mo_ref[...] = m_new
        vo_ref[...] = v_new

    row = pl.BlockSpec((tr, cols), lambda i: (i, 0))
    if parts is None:
        g_spec, g_arg = row, g
    else:
        g_spec, g_arg = pl.BlockSpec((parts.shape[0], tr, parts.shape[2]), lambda i: (0, i, 0)), parts
    return _pallas(body, name=name, grid=(rows // tr,), in_specs=[row, row, row, g_spec], out_specs=[row] * 4,
                   out_shape=[jax.ShapeDtypeStruct((rows, cols), F32)] * 4,
                   compiler_params=_cp(("arbitrary",), VMEM_MID))(w, m, v, g_arg)


def _adamw_many(name, groups):
    ntens = len(groups)

    def body(*refs):
        ins, outs = refs[:4 * ntens], refs[4 * ntens:]
        for k in range(ntens):
            w_ref, m_ref, v_ref, g_ref = ins[4 * k:4 * k + 4]
            gv = g_ref[...]
            delta, m_new, v_new = _adam_math(w_ref[...], gv, m_ref[...], v_ref[...])
            for o_ref, val in zip(outs[4 * k:4 * k + 4], (gv, delta, m_new, v_new)):
                o_ref[...] = val

    flat = [a for grp in groups for a in grp]
    out_shape = [jax.ShapeDtypeStruct(grp[0].shape, F32) for grp in groups for _ in range(4)]
    outs = _pallas(body, name=name, out_shape=out_shape, compiler_params=_cp(vmem=VMEM_MID))(*flat)
    return [tuple(outs[4 * k:4 * k + 4]) for k in range(ntens)]


def _pack_rows(arrs):
    rows, meta, total = [], [], 0
    for a in arrs:
        flat = a.reshape(-1)
        nrow = -(-flat.shape[0] // 1024) * 8
        rows.append(jnp.pad(flat, (0, nrow * 128 - flat.shape[0])).reshape(nrow, 128))
        meta.append((a.shape, flat.shape[0], nrow))
        total += nrow
    tail = -total % PACK_ROWS
    if tail:
        rows.append(jnp.zeros((tail, 128), F32))
    return jnp.concatenate(rows, axis=0), meta


def _unpack_rows(packed, meta):
    out, r0 = [], 0
    for shape, size, nrow in meta:
        out.append(packed[r0:r0 + nrow].reshape(-1)[:size].reshape(shape))
        r0 += nrow
    return out


WEIGHTS = ["rel_bias", "norm_g", "ada_w", "ada_b", "attn_w_in", "attn_sinks", "attn_b_f", "attn_w_out", "lru_w_in",
           "lru_conv_w", "lru_conv_b", "lru_w_a", "lru_b_a", "lru_w_x", "lru_b_x", "lru_lambda", "lru_w_out", "final_g"]
BIG = ["ada_w", "attn_w_in", "attn_w_out", "lru_w_in", "lru_w_out"]
PACK_ROWS = 256


def kernel(x, c, rel_bias, norm_g, ada_w, ada_b, attn_w_in, attn_sinks, attn_b_f, attn_w_out, lru_w_in, lru_conv_w, lru_conv_b, lru_w_a, lru_b_a, lru_w_x, lru_b_x, lru_lambda, lru_w_out, final_g, loss_target, m_rel_bias, m_norm_g, m_ada_w, m_ada_b, m_attn_w_in, m_attn_sinks, m_attn_b_f, m_attn_w_out, m_lru_w_in, m_lru_conv_w, m_lru_conv_b, m_lru_w_a, m_lru_b_a, m_lru_w_x, m_lru_b_x, m_lru_lambda, m_lru_w_out, m_final_g, v_rel_bias, v_norm_g, v_ada_w, v_ada_b, v_attn_w_in, v_attn_sinks, v_attn_b_f, v_attn_w_out, v_lru_w_in, v_lru_conv_w, v_lru_conv_b, v_lru_w_a, v_lru_b_a, v_lru_w_x, v_lru_b_x, v_lru_lambda, v_lru_w_out, v_final_g):
    nseq, seq, _ = x.shape
    t_tok = nseq * seq
    me = 4 * lax.axis_index("x") + 2 * lax.axis_index("y") + lax.axis_index("c")
    x0 = x.reshape(t_tok, D_MODEL)
    target = loss_target.reshape(t_tok, D_MODEL)

    w_in_pad = jnp.pad(attn_w_in[0].astype(BF16), ((0, 0), (0, SHARD_W_PAD - SHARD_W_IN)))
    vec_shard = jnp.concatenate([lru_conv_w[0], lru_conv_b, lru_b_a, lru_b_x, lru_lambda], axis=0)
    g_w_in, g_vec, g_c = _exchange("gather_first", [w_in_pad, vec_shard, c], [])
    later_w = [attn_w_out[0].astype(BF16), lru_w_in[0].astype(BF16), lru_w_out[0].astype(BF16)]
    later_handle, later_token = _exchange_start("gather_later_start", later_w, [], after=g_vec)
    w_full = jnp.transpose(g_w_in[:, :, :SHARD_W_IN], (1, 0, 2)).reshape(D_MODEL, N_DEV * SHARD_W_IN)
    w_aq, w_ak, w_av = w_full[:, 0:512], w_full[:, 512:640], w_full[:, 640:768]
    w_bq, w_bk, w_bv = w_full[:, 768:1280], w_full[:, 1280:1792], w_full[:, 1792:2304]
    w_f, w_gate = w_full[:, 2304:2312], w_full[:, 2312:3336]
    w_main = jnp.concatenate([w_bq, w_bk, w_bv, w_aq, w_gate, w_ak, w_av], axis=1)
    wf_t = jnp.transpose(w_f)
    vec_full = jnp.transpose(g_vec, (1, 0, 2)).reshape(8, D_MODEL)
    conv_w, conv_b, b_a, b_x, lam = vec_full[0:4], vec_full[4:5], vec_full[5:6], vec_full[6:7], vec_full[7:8]
    c_all = g_c.reshape(N_DEV * nseq, D_MODEL)

    ncol = ada_w.shape[2]
    ada_b_slice = lax.dynamic_slice(ada_b.reshape(2, N_DEV, ncol), (0, me, 0), (2, 1, ncol))
    mod_part = _ada_mod(c_all, ada_w, ada_b_slice)
    (g_mod,) = _exchange("gather_mod", [mod_part], [])
    mine = lax.dynamic_slice(g_mod, (0, 0, me * nseq, 0), (N_DEV, 2, nseq, ncol))
    mod = jnp.transpose(mine, (1, 2, 0, 3)).reshape(2, nseq, 3 * D_MODEL)
    shift = [mod[l, :, 0:D_MODEL].reshape(nseq, 1, D_MODEL) for l in range(2)]
    scale = [mod[l, :, D_MODEL:2 * D_MODEL].reshape(nseq, 1, D_MODEL) for l in range(2)]
    gmod = [mod[l, :, 2 * D_MODEL:].reshape(nseq, 1, D_MODEL) for l in range(2)]

    onehot = _bucket_onehot()
    bias = _bias_expand(jnp.transpose(rel_bias), onehot).reshape(N_HEADS, BLOCK, 2 * BLOCK)
    sinks = attn_sinks.reshape(N_HEADS)
    b_f = attn_b_f.reshape(N_HEADS, 1)
    norm_g0 = norm_g[0:1] + later_token[0:1, 0:1]
    h0, qkvg, fl_t = _norm_proj("norm_proj0", x0, norm_g0, shift[0], scale[0], w_main, seq, BF16, wf_t=wf_t)
    f_row, f_col = _fox_prep(fl_t, b_f, seq)
    a_out, lse_a = _swa_fwd(qkvg, bias, sinks, seq)
    q_aug, k_aug, kt_aug, vt = _fox_aug(qkvg, f_col, seq)
    b_out, lse_b = _fox_fwd_t(q_aug, k_aug, vt, seq)
    g_later = _exchange_wait("gather_later_wait", later_handle, after=lse_b)
    w_out0, g_lru_in, w_out1 = (_with_own(g, w, me) for g, w in zip(g_later, later_w))
    w_out0, w_out1 = w_out0.reshape(D_MODEL, D_MODEL), w_out1.reshape(D_MODEL, D_MODEL)
    w_out0_t, w_out1_t, w_main_t = jnp.transpose(w_out0), jnp.transpose(w_out1), jnp.transpose(w_main)
    lru_in_t = jnp.transpose(g_lru_in, (0, 2, 1)).reshape(2 * D_MODEL, D_MODEL)
    yg0, y0, x1 = _out_proj("out_proj0", [a_out, b_out], qkvg, C_GATE // D_MODEL, w_out0, x0, gmod[0], seq)

    h1, proj1 = _norm_proj("norm_proj1", x1, norm_g[1:2], shift[1], scale[1], g_lru_in, seq, F32)
    hs = _lru_fwd(proj1, conv_w, conv_b, lru_w_a[0], b_a, lru_w_x[0], b_x, lam, seq)

    yg1, dx2, dy1, dyh, dgm1, loss_rows, dfinal_rows = _last_layer_tail(
        hs, proj1, w_out1, w_out1_t, x1, gmod[1], final_g.reshape(1, D_MODEL), target, seq)

    dproj1, dcw, dvec, dw_a, dw_x = _lru_bwd(proj1, hs, dyh, conv_w, conv_b, lru_w_a[0], b_a, lru_w_x[0], b_x, lam, seq)
    dx1, dss1, dg1 = _norm_bwd("norm1_bwd", [(dproj1, 0)], lru_in_t, x1, norm_g[1:2], scale[1], dx2, seq)
    (p_w_out1,) = _dw("dw_out1", yg1, [dy1])
    (p_lru_in,) = _dw("dw_lru_in", h1, [dproj1], blocked=2 * D_MODEL // N_DEV)

    rows_out = D_MODEL // N_DEV
    gpack1, gmeta1 = _pack_rows([dcw[0:4], dvec[0:4], dg1[0], dfinal_rows[0]])
    dwax = jnp.stack([dw_a, dw_x]).astype(BF16)
    own1 = [gpack1, dwax, p_lru_in, p_w_out1.reshape(N_DEV, rows_out, D_MODEL)]
    grads1_handle, grads1_token = _exchange_start("grads1_start", own1[:2], own1[2:], after=dx1)

    gmod0 = gmod[0] + grads1_token[0:1, 0:1]
    dy0, dgm0, du_a, du_b, dgate = _out_proj_bwd("out_proj0_bwd", dx1, gmod0, y0, w_out0_t, seq,
                                                  attn=(a_out, b_out, qkvg))
    dq_a, dkv_a, dbias, dsink = _swa_bwd(qkvg, du_a, a_out, lse_a, bias, sinks, seq)
    dq_b, dk_b, dv_b, df4 = _fox_bwd_t(q_aug, k_aug, kt_aug, qkvg, du_b, b_out, lse_b, seq)
    dfl_t, db_f = _fox_post(df4.reshape(N_HEADS, t_tok), fl_t, b_f, seq)
    parts0 = [(dq_b, C_BQ), (dk_b, C_BK), (dv_b, C_BV), (dq_a, C_AQ), (dgate, C_GATE), (dkv_a, C_AK)]
    (p_w_out0,) = _dw("dw_out0", yg0, [dy0])
    pw_bq, pw_bk, pw_bv, pw_aq, pw_gate, pw_akv = _dw("dw_attn_in", h0, [p for p, _ in parts0])
    pw_f = _dw_rows("dw_f", dfl_t, h0)

    p_w_in = jnp.concatenate([pw_aq, pw_akv, pw_bq, pw_bk, pw_bv, jnp.transpose(pw_f).astype(BF16), pw_gate], axis=1)
    p_w_in = jnp.transpose(p_w_in.reshape(D_MODEL, N_DEV, SHARD_W_IN), (1, 0, 2))
    p_w_in = jnp.pad(p_w_in, ((0, 0), (0, 0), (0, SHARD_W_PAD - SHARD_W_IN)))
    own0 = [p_w_in, p_w_out0.reshape(N_DEV, rows_out, D_MODEL)]
    landed1 = _exchange_wait("grads1_wait", grads1_handle, after=p_w_in)
    grads0_handle, grads0_token = _exchange_start("grads0_start", [], own0, after=landed1[0])
    scale0 = scale[0] + grads0_token[0:1, 0:1]
    dx0, dss0, dg0 = _norm_bwd("norm0_bwd", parts0, w_main_t, x0, norm_g[0:1], scale0, dx1, seq,
                               rows_part=(dfl_t, wf_t))
    dbias_t = _bias_reduce(dbias.reshape(N_HEADS, BLOCK * 2 * BLOCK), onehot)

    gpack0, gmeta0 = _pack_rows([jnp.transpose(dbias_t), dg0[0], dsink[:, 0], db_f[:, 0], loss_rows[0]])
    dmod = jnp.stack([jnp.concatenate([dss[:, 0], dss[:, 1], dgm[:, 0]], axis=1)
                      for dss, dgm in ((dss0, dgm0), (dss1, dgm1))], axis=1)
    g_small0, g_dmod = _exchange("exchange_small", [gpack0, dmod], [])
    landed0 = _exchange_wait("grads0_wait", grads0_handle, after=g_small0)
    r_w_in, r_w_out0 = (_with_own(g, lax.dynamic_index_in_dim(a, me, 0, keepdims=False), me)
                        for g, a in zip(landed0, own0))
    g_small1, g_dwax = (_with_own(g, a, me) for g, a in zip(landed1[:2], own1[:2]))
    r_lru_in, r_w_out1 = (_with_own(g, lax.dynamic_index_in_dim(a, me, 0, keepdims=False), me)
                          for g, a in zip(landed1[2:], own1[2:]))

    d_rel, d_g0, d_sinks, d_b_f, loss_cols = _unpack_rows(_sum_leading("sum_small0", g_small0), gmeta0)
    loss = jnp.sum(loss_cols)
    d_cw, d_vec, d_g1, d_final_g = _unpack_rows(_sum_leading("sum_small1", g_small1), gmeta1)
    d_norm_g = jnp.stack([d_g0, d_g1])
    d_wax = _sum_leading("sum_dwax", g_dwax.reshape(N_DEV, 2 * LRU_BLOCKS * LRU_BLOCK_W, LRU_BLOCK_W))
    d_wa, d_wx = d_wax[:LRU_BLOCKS * LRU_BLOCK_W], d_wax[LRU_BLOCKS * LRU_BLOCK_W:]
    cols = lambda a: lax.dynamic_slice(a, (0, me * LRU_BLOCK_W), (a.shape[0], LRU_BLOCK_W))
    dmod_all = g_dmod.reshape(N_DEV * nseq, 2 * 3 * D_MODEL)
    d_ada_b = _sum_leading("sum_ada_b", dmod_all.reshape(N_DEV * nseq, 2 * 3 * D_MODEL // 128, 128)).reshape(2, 3 * D_MODEL)
    dmod_slice = lax.dynamic_slice(dmod_all.reshape(N_DEV * nseq, 2, N_DEV, ncol), (0, 0, me, 0),
                                   (N_DEV * nseq, 2, 1, ncol)).reshape(N_DEV * nseq, 2, ncol)
    d_ada_w = _ada_w_grad(c_all, jnp.transpose(dmod_slice, (1, 0, 2)))

    given = dict(
        rel_bias=(rel_bias, m_rel_bias, v_rel_bias), norm_g=(norm_g, m_norm_g, v_norm_g),
        ada_w=(ada_w, m_ada_w, v_ada_w), ada_b=(ada_b, m_ada_b, v_ada_b),
        attn_w_in=(attn_w_in, m_attn_w_in, v_attn_w_in), attn_sinks=(attn_sinks, m_attn_sinks, v_attn_sinks),
        attn_b_f=(attn_b_f, m_attn_b_f, v_attn_b_f), attn_w_out=(attn_w_out, m_attn_w_out, v_attn_w_out),
        lru_w_in=(lru_w_in, m_lru_w_in, v_lru_w_in), lru_conv_w=(lru_conv_w, m_lru_conv_w, v_lru_conv_w),
        lru_conv_b=(lru_conv_b, m_lru_conv_b, v_lru_conv_b), lru_w_a=(lru_w_a, m_lru_w_a, v_lru_w_a),
        lru_b_a=(lru_b_a, m_lru_b_a, v_lru_b_a), lru_w_x=(lru_w_x, m_lru_w_x, v_lru_w_x),
        lru_b_x=(lru_b_x, m_lru_b_x, v_lru_b_x), lru_lambda=(lru_lambda, m_lru_lambda, v_lru_lambda),
        lru_w_out=(lru_w_out, m_lru_w_out, v_lru_w_out), final_g=(final_g, m_final_g, v_final_g))
    results = {}

    def big(name, shape2d, g=None, parts=None):
        w, m, v = (a.reshape(shape2d) for a in given[name])
        outs = _adamw("adamw_" + name, w, m, v, g=g, parts=parts)
        results[name] = tuple(o.reshape(given[name][0].shape) for o in outs)

    big("ada_w", (2 * D_MODEL, ncol), g=d_ada_w.reshape(2 * D_MODEL, ncol))
    big("attn_w_in", (D_MODEL, SHARD_W_IN), parts=r_w_in)
    big("attn_w_out", (rows_out, D_MODEL), parts=r_w_out0)
    big("lru_w_in", (D_MODEL, 2 * D_MODEL // N_DEV), parts=r_lru_in)
    big("lru_w_out", (rows_out, D_MODEL), parts=r_w_out1)

    small_grads = dict(
        rel_bias=d_rel, norm_g=d_norm_g, ada_b=d_ada_b, attn_sinks=d_sinks.reshape(1, N_HEADS),
        attn_b_f=d_b_f.reshape(1, N_HEADS), lru_conv_w=cols(d_cw).reshape(1, 4, LRU_BLOCK_W),
        lru_conv_b=cols(d_vec[0:1]), lru_w_a=d_wa.reshape(lru_w_a.shape), lru_b_a=cols(d_vec[1:2]),
        lru_w_x=d_wx.reshape(lru_w_x.shape), lru_b_x=cols(d_vec[2:3]), lru_lambda=cols(d_vec[3:4]),
        final_g=d_final_g)
    small = [n for n in WEIGHTS if n not in BIG]
    as2d = lambda a: a.reshape(-1, a.shape[-1])
    outs = _adamw_many("adamw_small", [tuple(as2d(a) for a in given[n]) + (as2d(small_grads[n]),) for n in small])
    for n, group in zip(small, outs):
        results[n] = tuple(o.reshape(given[n][0].shape) for o in group)

    grad_x = dx0.reshape(x.shape)
    out = [loss, grad_x]
    for j in range(4):
        out += [results[n][j] for n in WEIGHTS]
    return tuple(out)
```

```python
import functools
import math

import jax
import jax.numpy as jnp
from jax import lax
from jax.experimental import pallas as pl
from jax.experimental.pallas import tpu as pltpu

F32 = jnp.float32
BF16 = jnp.bfloat16
HI = lax.Precision.HIGHEST
MESH = pl.DeviceIdType.MESH

N_DEV = 8
D_MODEL = 1024
HEAD_DIM = 64
N_HEADS = 8
KV_GROUP = 4
BLOCK = 128
REL_BUCKETS = 32
REL_MAX_EXACT = 16
REL_MAX_DIST = 128
LRU_BLOCKS = 8
LRU_BLOCK_W = 128
LRU_C = 8.0
EPS = 1e-6
SCALE = HEAD_DIM ** -0.5
NEG = -1e30

ADAM_LR = 0.001
ADAM_B1 = 0.9
ADAM_B2 = 0.999
ADAM_EPS = 1e-08
ADAM_WD = 0.01
ADAM_STEP = 10

C_BQ, C_BK, C_BV, C_AQ, C_GATE, C_AK, C_AV = 0, 512, 1024, 1536, 2048, 3072, 3200
N_MAIN = 3328
SHARD_W_IN = 417
SHARD_W_PAD = 512

TM = 512
TQ = 256
TK = 128
TKB = 256
TC = 256
SWA_SUB = 2
VMEM_BIG = 56 * 1024 * 1024
VMEM_MID = 40 * 1024 * 1024


def _pallas(body, **kw):
    return pl.pallas_call(body, **kw)


def _cp(sem=None, vmem=None):
    kw = {}
    if sem is not None:
        kw["dimension_semantics"] = sem
    if vmem is not None:
        kw["vmem_limit_bytes"] = vmem
    return pltpu.CompilerParams(**kw)


def _nn(a, b, precision=None):
    return jnp.dot(a, b, preferred_element_type=F32, precision=precision)


def _nt(a, b, precision=None):
    return lax.dot_general(a, b, (((1,), (1,)), ((), ())), preferred_element_type=F32, precision=precision)


def _tn(a, b, precision=None):
    return lax.dot_general(a, b, (((0,), (0,)), ((), ())), preferred_element_type=F32, precision=precision)


def _sigmoid(x):
    return 1.0 / (1.0 + jnp.exp(-x))


def _silu(x):
    return x * _sigmoid(x)


def _dsilu(x):
    s = _sigmoid(x)
    return s * (1.0 + x * (1.0 - s))


def _neg_expm1(x):
    poly = x * (1.0 + x * (0.5 + x * (1.0 / 6.0 + x * (1.0 / 24.0))))
    return -jnp.where(jnp.abs(x) < 0.05, poly, jnp.exp(x) - 1.0)


def _col(tile, idx):
    lane = lax.broadcasted_iota(jnp.int32, tile.shape, 1)
    return jnp.sum(jnp.where(lane == idx, tile, 0.0), axis=1, keepdims=True)


def _row(tile, idx):
    sub = lax.broadcasted_iota(jnp.int32, tile.shape, 0)
    return jnp.sum(jnp.where(sub == idx, tile, 0.0), axis=0, keepdims=True)


def _exchange(name, gathers, scatters, axes=("x", "y", "c"), chunks=1):
    ng, n = len(gathers), len(gathers) + len(scatters)
    ins = list(gathers) + list(scatters)
    group = 2 ** len(axes)

    def body(*refs):
        in_refs, out_refs = refs[:n], refs[n:2 * n]
        send_sems, recv_sems, loc_sems = refs[2 * n:]
        coord = {a: lax.axis_index(a) for a in ("x", "y", "c")}

        def member(r):
            pc = dict(coord)
            idx = 0
            for k, a in enumerate(axes):
                if r & (1 << (len(axes) - 1 - k)):
                    pc[a] = 1 - coord[a]
                idx = 2 * idx + pc[a]
            return (pc["x"], pc["y"], pc["c"]), idx

        _, me = member(0)

        def peer(r):
            return member(r)

        local, sends, recvs = [], [], []
        for k in range(n):
            mine = in_refs[k] if k < ng else in_refs[k].at[me]
            cp = pltpu.make_async_copy(mine, out_refs[k].at[me], loc_sems.at[k])
            cp.start()
            local.append(cp)
            lead = mine.shape[0]
            nchunk = max(q for q in range(1, chunks + 1) if lead % q == 0)
            step = lead // nchunk
            for r in range(1, group):
                pid, pidx = peer(r)
                src = in_refs[k] if k < ng else in_refs[k].at[pidx]
                for q in range(nchunk):
                    rows = pl.ds(q * step, step)
                    sems = dict(send_sem=send_sems.at[r - 1, k, q], recv_sem=recv_sems.at[r - 1, k, q],
                                device_id=pid, device_id_type=MESH)
                    snd = pltpu.make_async_remote_copy(src_ref=src.at[rows], dst_ref=out_refs[k].at[me].at[rows], **sems)
                    snd.start()
                    sends.append(snd)
                    recvs.append(pltpu.make_async_remote_copy(
                        src_ref=src.at[rows], dst_ref=out_refs[k].at[pidx].at[rows], **sems))
        for rc in recvs:
            rc.wait_recv()
        for snd in sends:
            snd.wait_send()
        for cp in local:
            cp.wait()

    out_shape = [jax.ShapeDtypeStruct((group,) + a.shape, a.dtype) for a in gathers]
    out_shape += [jax.ShapeDtypeStruct(a.shape, a.dtype) for a in scatters]
    any_spec = pl.BlockSpec(memory_space=pl.ANY)
    return _pallas(
        body, name=name, out_shape=out_shape,
        in_specs=[any_spec] * n, out_specs=[any_spec] * n,
        scratch_shapes=[pltpu.SemaphoreType.DMA((group - 1, n, chunks)), pltpu.SemaphoreType.DMA((group - 1, n, chunks)),
                        pltpu.SemaphoreType.DMA((n,))],
    )(*ins)


def _peer_of(r):
    x, y, c = lax.axis_index("x"), lax.axis_index("y"), lax.axis_index("c")
    px = 1 - x if r & 4 else x
    py = 1 - y if r & 2 else y
    pc = 1 - c if r & 1 else c
    return (px, py, pc), 4 * px + 2 * py + pc


def _split_copies(in_refs, land_refs, send_sems, recv_sems, ng, with_recv):
    _, me = _peer_of(0)
    pairs = []
    for k, (src_ref, land) in enumerate(zip(in_refs, land_refs)):
        for r in range(1, N_DEV):
            pid, pidx = _peer_of(r)
            src = src_ref if k < ng else src_ref.at[pidx]
            slot = (N_DEV - 1) * k + r - 1
            sems = dict(send_sem=send_sems.at[slot], recv_sem=recv_sems.at[slot], device_id=pid, device_id_type=MESH)
            send = pltpu.make_async_remote_copy(src_ref=src, dst_ref=land.at[me], **sems)
            recv = pltpu.make_async_remote_copy(src_ref=src, dst_ref=land.at[pidx], **sems) if with_recv else None
            pairs.append((send, recv))
    return pairs


def _exchange_start(name, gathers, scatters, after):
    ng, n = len(gathers), len(gathers) + len(scatters)
    ins = list(gathers) + list(scatters)
    lands = [jax.ShapeDtypeStruct((N_DEV,) + a.shape, a.dtype) for a in gathers]
    lands += [jax.ShapeDtypeStruct(a.shape, a.dtype) for a in scatters]

    def body(*refs):
        in_refs, land_refs = refs[:n], refs[n:2 * n]
        send_sems, recv_sems = refs[2 * n + 1:2 * n + 3]
        token = refs[-1]
        for send, _ in _split_copies(in_refs, land_refs, send_sems, recv_sems, ng, False):
            send.start()
        token[...] = jnp.zeros_like(token)

    hbm = pl.BlockSpec(memory_space=pltpu.HBM)
    sem = pl.BlockSpec(memory_space=pltpu.SEMAPHORE)
    sem_shape = pltpu.SemaphoreType.DMA(((N_DEV - 1) * n,))
    out_shape = [sem_shape, sem_shape] + [pltpu.HBM(a.shape, a.dtype) for a in ins]
    out_shape += [pltpu.HBM(l.shape, l.dtype) for l in lands] + [jax.ShapeDtypeStruct((8, 128), F32)]
    args = [pltpu.with_memory_space_constraint(a, pltpu.HBM) for a in ins]
    args += [pltpu.with_memory_space_constraint(lax.empty(l.shape, l.dtype), pltpu.HBM) for l in lands]
    outs = _pallas(
        body, name=name, out_shape=out_shape,
        in_specs=[hbm] * (2 * n) + [pl.BlockSpec(memory_space=pl.ANY)],
        out_specs=[sem, sem] + [hbm] * (2 * n) + [pl.BlockSpec(memory_space=pltpu.VMEM)],
        input_output_aliases={i: 2 + i for i in range(2 * n)},
        compiler_params=pltpu.CompilerParams(has_side_effects=pltpu.SideEffectType.DATAFLOW_SIDE_EFFECTING),
    )(*args, after)
    return (outs[0], outs[1], list(outs[2:2 + n]), list(outs[2 + n:2 + 2 * n]), ng), outs[-1]


def _exchange_wait(name, handle, after):
    send_sems, recv_sems, srcs, lands, ng = handle
    n = len(srcs)

    def body(*refs):
        in_refs, land_refs = refs[:n], refs[n:2 * n]
        send_ref, recv_ref = refs[2 * n:2 * n + 2]
        for send, recv in _split_copies(in_refs, land_refs, send_ref, recv_ref, ng, True):
            send.wait_send()
            recv.wait_recv()

    hbm = pl.BlockSpec(memory_space=pltpu.HBM)
    sem = pl.BlockSpec(memory_space=pltpu.SEMAPHORE)
    outs = _pallas(
        body, name=name, out_shape=[pltpu.HBM(a.shape, a.dtype) for a in srcs + lands],
        in_specs=[hbm] * (2 * n) + [sem, sem, pl.BlockSpec(memory_space=pl.ANY)],
        out_specs=[hbm] * (2 * n), input_output_aliases={i: i for i in range(2 * n)},
        compiler_params=pltpu.CompilerParams(has_side_effects=pltpu.SideEffectType.DATAFLOW_SIDE_EFFECTING),
    )(*srcs, *lands, send_sems, recv_sems, after)
    return list(outs[n:])


def _with_own(land, own, me):
    return lax.dynamic_update_slice(land, own[None], (me,) + (0,) * own.ndim)


def _ada_mod(c_all, ada_w, ada_b_slice):
    def body(c_ref, w_ref, b_ref, o_ref):
        ca = _silu(c_ref[...])
        for l in range(2):
            o_ref[l] = _nn(ca, w_ref[l], HI) + b_ref[l]

    return _pallas(body, name="ada_mod",
                   out_shape=jax.ShapeDtypeStruct((2, c_all.shape[0], ada_w.shape[2]), F32),
                   compiler_params=_cp(vmem=VMEM_MID))(c_all, ada_w, ada_b_slice)


def _ada_w_grad(c_all, dmod_slice):
    def body(c_ref, d_ref, o_ref):
        ca = _silu(c_ref[...])
        for l in range(2):
            o_ref[l] = _tn(ca, d_ref[l], HI)

    return _pallas(body, name="ada_w_grad",
                   out_shape=jax.ShapeDtypeStruct((2, D_MODEL, dmod_slice.shape[2]), F32),
                   compiler_params=_cp(vmem=VMEM_MID))(c_all, dmod_slice)


def _bucket_onehot():
    qi = jnp.arange(BLOCK)[:, None]
    kj = jnp.arange(2 * BLOCK)[None, :]
    rel = qi - kj + BLOCK
    n = jnp.maximum(rel, 0)
    nf = jnp.maximum(n, 1).astype(F32)
    large = REL_MAX_EXACT + (jnp.log(nf / REL_MAX_EXACT) / math.log(REL_MAX_DIST / REL_MAX_EXACT)
                             * (REL_BUCKETS - REL_MAX_EXACT)).astype(jnp.int32)
    large = jnp.minimum(large, REL_BUCKETS - 1)
    bucket = jnp.where(n < REL_MAX_EXACT, n, large).reshape(1, BLOCK * 2 * BLOCK)
    return (jnp.arange(REL_BUCKETS)[:, None] == bucket).astype(F32)


def _bias_expand(rel_bias_t, onehot):
    def body(r_ref, e_ref, o_ref):
        o_ref[...] = _nn(r_ref[...], e_ref[...], HI)

    return _pallas(body, name="bias_expand",
                   out_shape=jax.ShapeDtypeStruct((N_HEADS, onehot.shape[1]), F32),
                   compiler_params=_cp(vmem=VMEM_MID))(rel_bias_t, onehot)


def _bias_reduce(dbias, onehot):
    def body(d_ref, e_ref, o_ref):
        o_ref[...] = _nt(d_ref[...], e_ref[...], HI)

    return _pallas(body, name="bias_reduce",
                   out_shape=jax.ShapeDtypeStruct((N_HEADS, REL_BUCKETS), F32),
                   compiler_params=_cp(vmem=VMEM_MID))(dbias, onehot)


def _norm_proj(name, x, g, shift, scale, w, seq, out_dtype, wf_t=None):
    t_tok = x.shape[0]
    w3d = w.ndim == 3
    n_out = w.shape[0] * w.shape[2] if w3d else w.shape[1]
    cn = w.shape[2] if w3d else 256

    def body(x_ref, g_ref, sh_ref, sc_ref, w_ref, *rest):
        if wf_t is not None:
            wf_ref, h_ref, o_ref, fl_ref = rest
        else:
            h_ref, o_ref = rest
        xv = x_ref[...]
        rstd = lax.rsqrt(jnp.mean(xv * xv, axis=-1, keepdims=True) + EPS)
        h = (xv * rstd) * g_ref[...] * (1.0 + sc_ref[...]) + sh_ref[...]
        hb = h.astype(BF16)
        h_ref[...] = hb
        for j in range(n_out // cn):
            wj = w_ref[j] if w3d else w_ref[:, j * cn:(j + 1) * cn]
            o_ref[:, j * cn:(j + 1) * cn] = _nn(hb, wj).astype(out_dtype)
        if wf_t is not None:
            fl_ref[...] = _nt(wf_ref[...], hb)

    mod_spec = pl.BlockSpec((None, 1, D_MODEL), lambda i: (i * TM // seq, 0, 0))
    w_spec = (pl.BlockSpec(w.shape, lambda i: (0, 0, 0)) if w3d else pl.BlockSpec(w.shape, lambda i: (0, 0)))
    in_specs = [pl.BlockSpec((TM, D_MODEL), lambda i: (i, 0)), pl.BlockSpec((1, D_MODEL), lambda i: (0, 0)),
                mod_spec, mod_spec, w_spec]
    out_shape = [jax.ShapeDtypeStruct((t_tok, D_MODEL), BF16), jax.ShapeDtypeStruct((t_tok, n_out), out_dtype)]
    out_specs = [pl.BlockSpec((TM, D_MODEL), lambda i: (i, 0)), pl.BlockSpec((TM, n_out), lambda i: (i, 0))]
    args = [x, g, shift, scale, w]
    if wf_t is not None:
        in_specs.append(pl.BlockSpec(wf_t.shape, lambda i: (0, 0)))
        out_shape.append(jax.ShapeDtypeStruct((wf_t.shape[0], t_tok), F32))
        out_specs.append(pl.BlockSpec((wf_t.shape[0], TM), lambda i: (0, i)))
        args.append(wf_t)
    return _pallas(body, name=name, grid=(t_tok // TM,), in_specs=in_specs, out_specs=out_specs,
                   out_shape=out_shape, compiler_params=_cp(("arbitrary",), VMEM_BIG))(*args)


def _fox_prep(fl_t, b_f, seq):
    t_tok = fl_t.shape[1]
    ch = 256

    def body(fl_ref, bf_ref, fr_ref, fc_ref):
        z = fl_ref[...] + bf_ref[...]
        logf = jnp.minimum(z, 0.0) - jnp.log(1.0 + jnp.exp(-jnp.abs(z)))
        ri = lax.broadcasted_iota(jnp.int32, (ch, ch), 0)
        ci = lax.broadcasted_iota(jnp.int32, (ch, ch), 1)
        upper = (ri <= ci).astype(F32)
        eye = (ri == ci).astype(F32)
        carry = jnp.zeros((N_HEADS, 1), F32)
        for k in range(seq // ch):
            fk = _nn(logf[:, k * ch:(k + 1) * ch], upper, HI) + carry
            carry = fk[:, ch - 1:ch]
            fr_ref[:, k * ch:(k + 1) * ch] = fk
            padded = jnp.concatenate([fk, jnp.zeros((128 - N_HEADS, ch), F32)], axis=0)
            fc_ref[k * ch:(k + 1) * ch, :] = _nt(eye, padded, HI)

    return _pallas(
        body, name="fox_prep", grid=(t_tok // seq,),
        in_specs=[pl.BlockSpec((N_HEADS, seq), lambda b: (0, b)), pl.BlockSpec((N_HEADS, 1), lambda b: (0, 0))],
        out_specs=[pl.BlockSpec((N_HEADS, seq), lambda b: (0, b)), pl.BlockSpec((seq, 128), lambda b: (b, 0))],
        out_shape=[jax.ShapeDtypeStruct((N_HEADS, t_tok), F32), jax.ShapeDtypeStruct((t_tok, 128), F32)],
        compiler_params=_cp(("arbitrary",), VMEM_MID))(fl_t, b_f)


def _fox_post(df_row, fl_t, b_f, seq):
    t_tok = fl_t.shape[1]
    ch = 256

    def body(d_ref, fl_ref, bf_ref, o_ref, db_ref):
        @pl.when(pl.program_id(0) == 0)
        def _():
            db_ref[...] = jnp.zeros_like(db_ref)

        z = fl_ref[...] + bf_ref[...]
        sig_neg = 1.0 / (1.0 + jnp.exp(z))
        ri = lax.broadcasted_iota(jnp.int32, (ch, ch), 0)
        ci = lax.broadcasted_iota(jnp.int32, (ch, ch), 1)
        lower = (ri >= ci).astype(F32)
        carry = jnp.zeros((N_HEADS, 1), F32)
        tot = jnp.zeros((N_HEADS, 1), F32)
        for k in reversed(range(seq // ch)):
            dk = _nn(d_ref[:, k * ch:(k + 1) * ch], lower, HI) + carry
            carry = dk[:, 0:1]
            dfl = dk * sig_neg[:, k * ch:(k + 1) * ch]
            o_ref[:, k * ch:(k + 1) * ch] = dfl
            tot = tot + jnp.sum(dfl, axis=1, keepdims=True)
        db_ref[...] += jnp.broadcast_to(tot, db_ref.shape)

    return _pallas(
        body, name="fox_post", grid=(t_tok // seq,),
        in_specs=[pl.BlockSpec((N_HEADS, seq), lambda b: (0, b)), pl.BlockSpec((N_HEADS, seq), lambda b: (0, b)),
                  pl.BlockSpec((N_HEADS, 1), lambda b: (0, 0))],
        out_specs=[pl.BlockSpec((N_HEADS, seq), lambda b: (0, b)), pl.BlockSpec((N_HEADS, 128), lambda b: (0, 0))],
        out_shape=[jax.ShapeDtypeStruct((N_HEADS, t_tok), F32), jax.ShapeDtypeStruct((N_HEADS, 128), F32)],
        compiler_params=_cp(("arbitrary",), VMEM_MID))(df_row, fl_t, b_f)


def _eye(n, dtype):
    return (lax.broadcasted_iota(jnp.int32, (n, n), 0) == lax.broadcasted_iota(jnp.int32, (n, n), 1)).astype(dtype)


def _fox_aug(qkvg, f_col, seq):
    t_tok = qkvg.shape[0]
    ta = 256
    nkb = ta // TK

    def body(q_ref, k_ref, v_ref, fc_ref, qa_ref, ka_ref, kt_ref, vt_ref):
        ri = lax.broadcasted_iota(jnp.int32, (128, 128), 0)
        ci = lax.broadcasted_iota(jnp.int32, (128, 128), 1)
        eye = (ri == ci).astype(BF16)
        lane = lax.broadcasted_iota(jnp.int32, (ta, 128), 1)
        ones_q = jnp.where(jnp.logical_and(lane >= 64, lane < 67), 1.0, 0.0)
        ones_k = jnp.where(jnp.logical_and(lane >= 67, lane < 70), 1.0, 0.0)
        fc_tile = fc_ref[...]
        for p in range(N_HEADS // 2):
            q2 = q_ref[:, 128 * p:128 * (p + 1)]
            k2 = k_ref[:, 128 * p:128 * (p + 1)]
            vt = _nt(eye, v_ref[:, 128 * p:128 * (p + 1)]).astype(BF16)
            for kk in range(nkb):
                vt_ref[p, kk] = vt[:, kk * TK:(kk + 1) * TK]
            for e in range(2):
                h = 2 * p + e
                sel = jnp.logical_and(ri == ci + HEAD_DIM * e, ci < HEAD_DIM)
                f = _col(fc_tile, h)
                fh = f.astype(BF16).astype(F32)
                fm = (f - fh).astype(BF16).astype(F32)
                fl = (f - fh - fm).astype(BF16).astype(F32)
                qa = (_nn(q2, jnp.where(sel, SCALE, 0.0).astype(BF16)) + ones_q + jnp.where(lane == 67, fh, 0.0)
                      + jnp.where(lane == 68, fm, 0.0) + jnp.where(lane == 69, fl, 0.0))
                ka = (_nn(k2, jnp.where(sel, 1.0, 0.0).astype(BF16)) + ones_k - jnp.where(lane == 64, fh, 0.0)
                      - jnp.where(lane == 65, fm, 0.0) - jnp.where(lane == 66, fl, 0.0))
                qa_ref[h] = qa.astype(BF16)
                kab = ka.astype(BF16)
                ka_ref[h] = kab
                kt = _nt(eye, kab).astype(BF16)
                for kk in range(ta // TKB):
                    kt_ref[h, kk] = kt[:, kk * TKB:(kk + 1) * TKB]

    aug = jax.ShapeDtypeStruct((N_HEADS, t_tok, 128), BF16)
    return _pallas(
        body, name="fox_aug", grid=(t_tok // ta,),
        in_specs=[pl.BlockSpec((ta, 512), lambda i: (i, C_BQ // 512)), pl.BlockSpec((ta, 512), lambda i: (i, C_BK // 512)),
                  pl.BlockSpec((ta, 512), lambda i: (i, C_BV // 512)), pl.BlockSpec((ta, 128), lambda i: (i, 0))],
        out_specs=[pl.BlockSpec((N_HEADS, ta, 128), lambda i: (0, i, 0)), pl.BlockSpec((N_HEADS, ta, 128), lambda i: (0, i, 0)),
                   pl.BlockSpec((N_HEADS, ta // TKB, 128, TKB), lambda i: (0, i, 0, 0)),
                   pl.BlockSpec((N_HEADS // 2, nkb, 128, TK), lambda i: (0, i, 0, 0))],
        out_shape=[aug, aug, jax.ShapeDtypeStruct((N_HEADS, t_tok // TKB, 128, TKB), BF16),
                   jax.ShapeDtypeStruct((N_HEADS // 2, t_tok // TK, 128, TK), BF16)],
        compiler_params=_cp(("arbitrary",), VMEM_MID))(qkvg, qkvg, qkvg, f_col)


def _fox_fwd_t(q_aug, k_aug, vt, seq):
    t_tok = k_aug.shape[1]
    nq = seq // TQ
    ratio = TQ // TK

    def body(qa_ref, ka_ref, vt_ref, o_ref, lse_ref, ml_s, acc_s, st_s, p_s, al_s, qt_s):
        i = pl.program_id(1)
        tpos = i * TQ + lax.broadcasted_iota(jnp.int32, (1, TQ), 1)
        eye = _eye(HEAD_DIM, BF16)
        eye2 = _eye(128, BF16)
        for h in range(N_HEADS):
            qt_s[h] = _nt(eye2, qa_ref[h]).astype(BF16)
            ml_s[0, h] = jnp.full((1, TQ), NEG, F32)
            ml_s[1, h] = jnp.zeros((1, TQ), F32)
            acc_s[h] = jnp.zeros((HEAD_DIM, TQ), F32)
            p_s[1, h] = jnp.zeros((TK, TQ), BF16)
            al_s[1, h] = jnp.ones((1, TQ), F32)

        def scores(j):
            row0 = pl.multiple_of(j * TK, TK)
            for h in range(N_HEADS):
                st_s[j & 1, h] = _nn(ka_ref[h, pl.ds(row0, TK), :], qt_s[h])

        def softmax(j, masked):
            slot = j & 1
            if masked:
                keep = (j * TK + lax.broadcasted_iota(jnp.int32, (TK, 1), 0)) <= tpos
            for h in range(N_HEADS):
                st = st_s[slot, h]
                if masked:
                    st = jnp.where(keep, st, NEG)
                m = ml_s[0, h]
                m_new = jnp.maximum(m, jnp.max(st, axis=0, keepdims=True))
                alpha = jnp.exp(m - m_new)
                pe = jnp.exp(st - m_new)
                ml_s[0, h] = m_new
                ml_s[1, h] = alpha * ml_s[1, h] + jnp.sum(pe, axis=0, keepdims=True)
                al_s[slot, h] = alpha
                p_s[slot, h] = pe.astype(BF16)

        def values(j):
            slot = j & 1
            jv = jnp.maximum(j, 0)
            for h in range(N_HEADS):
                p, e = divmod(h, 2)
                acc_s[h] = al_s[slot, h] * acc_s[h] + _nn(vt_ref[p, jv, e * HEAD_DIM:(e + 1) * HEAD_DIM, :], p_s[slot, h])

        def step(j, carry):
            values(j - 1)
            softmax(j, False)
            scores(j + 1)
            return carry

        last = ratio * i + ratio - 1
        scores(0)
        lax.fori_loop(0, ratio * i, step, 0)
        for kk in range(ratio):
            j = ratio * i + kk
            values(j - 1)
            softmax(j, True)
            if kk < ratio - 1:
                scores(j + 1)
        values(last)
        for p in range(N_HEADS // 2):
            outs = []
            for e in range(2):
                h = 2 * p + e
                l = ml_s[1, h]
                outs.append(_tn((acc_s[h] / l).astype(BF16), eye))
                lse_ref[p, e:e + 1, :] = ml_s[0, h] + jnp.log(l)
            o_ref[:, 128 * p:128 * (p + 1)] = jnp.concatenate(outs, axis=1).astype(BF16)

    return _pallas(
        body, name="fox_fwd", grid=(t_tok // seq, nq),
        in_specs=[pl.BlockSpec((N_HEADS, TQ, 128), lambda b, i: (0, b * nq + i, 0)),
                  pl.BlockSpec((N_HEADS, seq, 128), lambda b, i: (0, b, 0)),
                  pl.BlockSpec((N_HEADS // 2, seq // TK, 128, TK), lambda b, i: (0, b, 0, 0))],
        out_specs=[pl.BlockSpec((TQ, 512), lambda b, i: (b * nq + i, 0)),
                   pl.BlockSpec((N_HEADS // 2, 2, TQ), lambda b, i: (0, 0, b * nq + i))],
        out_shape=[jax.ShapeDtypeStruct((t_tok, 512), BF16), jax.ShapeDtypeStruct((N_HEADS // 2, 2, t_tok), F32)],
        scratch_shapes=[pltpu.VMEM((2, N_HEADS, 1, TQ), F32), pltpu.VMEM((N_HEADS, HEAD_DIM, TQ), F32),
                        pltpu.VMEM((2, N_HEADS, TK, TQ), F32), pltpu.VMEM((2, N_HEADS, TK, TQ), BF16),
                        pltpu.VMEM((2, N_HEADS, 1, TQ), F32), pltpu.VMEM((N_HEADS, 128, TQ), BF16)],
        compiler_params=_cp(("arbitrary", "arbitrary"), VMEM_MID))(q_aug, k_aug, vt)


def _fox_bwd_t(q_aug, k_aug, kt, qkvg, du_b, b_out, lse, seq):
    TK = TKB
    t_tok = qkvg.shape[0]
    nq = seq // TQ
    nkb = seq // TK
    ratio = TQ // TK
    hg = 4

    def body(qa_ref, ka_ref, kt_ref, v_ref, do_ref, o_ref, lse_ref, dq_ref, dk_ref, dv_ref, df_ref,
             dqt_s, row_s, dfk_s, dk_s, dv_s, dot_s, st_s, dp_s, pb_s, db_s, qt_s):
        eye = _eye(HEAD_DIM, BF16)
        eye2 = _eye(128, BF16)
        eye_k = _eye(TK, F32)
        lane8 = lax.broadcasted_iota(jnp.int32, (8, 128), 1)
        lane_k = lax.broadcasted_iota(jnp.int32, (TK, 128), 1)
        first = [lane8 < HEAD_DIM, lane8 >= HEAD_DIM]
        for pp in range(hg // 2):
            for ii in range(nq):
                dot_s[pp, ii] = _nt(eye2, do_ref[ii * TQ:(ii + 1) * TQ, 128 * pp:128 * (pp + 1)]).astype(BF16)
        for hh in range(hg):
            for ii in range(nq):
                qt_s[hh, ii] = _nt(eye2, qa_ref[hh, ii * TQ:(ii + 1) * TQ, :]).astype(BF16)
        for hh in range(hg):
            pp, e = divmod(hh, 2)
            head_lanes = jnp.where(first[e], 1.0, 0.0)
            for ii in range(nq):
                rows = slice(ii * TQ, (ii + 1) * TQ)
                prod = do_ref[rows, 128 * pp:128 * (pp + 1)].astype(F32) * o_ref[rows, 128 * pp:128 * (pp + 1)].astype(F32)
                row_s[hh, ii, 0] = _nt(head_lanes, prod, HI)
                row_s[hh, ii, 1] = jnp.broadcast_to(lse_ref[pp, e:e + 1, ii * TQ:(ii + 1) * TQ], (8, TQ))
                dqt_s[hh, ii] = jnp.zeros((128, TQ), F32)

        def kblock(j, _):
            krow = pl.multiple_of(j * TK, TK)
            spos = j * TK + lax.broadcasted_iota(jnp.int32, (TK, 1), 0)
            for hh in range(hg):
                dk_s[hh] = jnp.zeros((TK, 128), F32)
                dv_s[hh] = jnp.zeros((TK, 128), F32)

            def scores(i):
                for hh in range(hg):
                    pp, e = divmod(hh, 2)
                    own = (lane_k < HEAD_DIM) if e == 0 else (lane_k >= HEAD_DIM)
                    v2 = v_ref[pl.ds(krow, TK), 128 * pp:128 * (pp + 1)]
                    vj = jnp.where(own, v2, jnp.zeros_like(v2))
                    st_s[i & 1, hh] = _nn(ka_ref[hh, pl.ds(krow, TK), :], qt_s[hh, i])
                    dp_s[i & 1, hh] = _nn(vj, dot_s[pp, i])

            def elementwise(i, masked):
                slot = i & 1
                if masked:
                    keep = spos <= (i * TQ + lax.broadcasted_iota(jnp.int32, (1, TQ), 1))
                for hh in range(hg):
                    pt = jnp.exp(st_s[slot, hh] - row_s[hh, i, 1][0:1, :])
                    if masked:
                        pt = jnp.where(keep, pt, 0.0)
                    dst = pt * (dp_s[slot, hh] - row_s[hh, i, 0][0:1, :])
                    pb_s[slot, hh] = pt.astype(BF16)
                    db_s[slot, hh] = dst.astype(BF16)

            def grads(i):
                slot = i & 1
                qrow = pl.multiple_of(i * TQ, TQ)
                for hh in range(hg):
                    dst_b = db_s[slot, hh]
                    dv_s[hh] += _nn(pb_s[slot, hh], do_ref[pl.ds(qrow, TQ), 128 * (hh // 2):128 * (hh // 2 + 1)])
                    dk_s[hh] += _nn(dst_b, qa_ref[hh, pl.ds(qrow, TQ), :])
                    dqt_s[hh, i] += _nn(kt_ref[hh, j], dst_b)

            def step(i, carry):
                grads(i - 1)
                elementwise(i, False)
                scores(jnp.minimum(i + 1, nq - 1))
                return carry

            i0 = j // ratio
            scores(i0)
            elementwise(i0, True)
            scores(jnp.minimum(i0 + 1, nq - 1))
            lax.fori_loop(i0 + 1, nq, step, 0)
            grads(nq - 1)
            for pp in range(hg // 2):
                cols = slice(128 * pp, 128 * (pp + 1))
                dk_ref[pl.ds(krow, TK), cols] = jnp.concatenate(
                    [dk_s[2 * pp][:, :HEAD_DIM], dk_s[2 * pp + 1][:, :HEAD_DIM]], axis=1).astype(BF16)
                dv_ref[pl.ds(krow, TK), cols] = jnp.where(lane_k < HEAD_DIM, dv_s[2 * pp], dv_s[2 * pp + 1]).astype(BF16)
            for hh in range(hg):
                dfk_s[hh, j] = _tn(dk_s[hh][:, HEAD_DIM:HEAD_DIM + 8], eye_k, HI)
            return 0

        lax.fori_loop(0, nkb, kblock, 0)
        for pp in range(hg // 2):
            for ii in range(nq):
                parts = []
                for e in range(2):
                    dqt = dqt_s[2 * pp + e, ii]
                    parts.append(_tn(dqt[0:HEAD_DIM, :].astype(BF16), eye) * SCALE)
                    for kk in range(ratio):
                        jj = ii * ratio + kk
                        df_ref[pp, e:e + 1, jj * TK:(jj + 1) * TK] = (dqt[67:68, kk * TK:(kk + 1) * TK]
                                                                     - dfk_s[2 * pp + e, jj][0:1, :])
                dq_ref[ii * TQ:(ii + 1) * TQ, 128 * pp:128 * (pp + 1)] = jnp.concatenate(parts, axis=1).astype(BF16)

    aug_blk = pl.BlockSpec((hg, seq, 128), lambda b, g: (g, b, 0))
    pair_blk = pl.BlockSpec((seq, 64 * hg), lambda b, g: (b, g))
    row_blk = pl.BlockSpec((hg // 2, 2, seq), lambda b, g: (g, 0, b))
    return _pallas(
        body, name="fox_bwd", grid=(t_tok // seq, N_HEADS // hg),
        in_specs=[aug_blk, aug_blk, pl.BlockSpec((hg, nkb, 128, TK), lambda b, g: (g, b, 0, 0)),
                  pl.BlockSpec((seq, 64 * hg), lambda b, g: (b, C_BV // (64 * hg) + g)), pair_blk, pair_blk, row_blk],
        out_specs=[pair_blk, pair_blk, pair_blk, row_blk],
        out_shape=[jax.ShapeDtypeStruct((t_tok, 512), BF16)] * 3
        + [jax.ShapeDtypeStruct((N_HEADS // 2, 2, t_tok), F32)],
        scratch_shapes=[pltpu.VMEM((hg, nq, 128, TQ), F32), pltpu.VMEM((hg, nq, 2, 8, TQ), F32),
                        pltpu.VMEM((hg, nkb, 8, TK), F32), pltpu.VMEM((hg, TK, 128), F32),
                        pltpu.VMEM((hg, TK, 128), F32), pltpu.VMEM((hg // 2, nq, 128, TQ), BF16),
                        pltpu.VMEM((2, hg, TK, TQ), F32), pltpu.VMEM((2, hg, TK, TQ), F32),
                        pltpu.VMEM((2, hg, TK, TQ), BF16), pltpu.VMEM((2, hg, TK, TQ), BF16),
                        pltpu.VMEM((hg, nq, 128, TQ), BF16)],
        compiler_params=_cp(("arbitrary", "arbitrary"), VMEM_BIG))(q_aug, k_aug, kt, qkvg, du_b, b_out, lse)


def _fox_bwd_t_old(q_aug, k_aug, kt, qkvg, du_b, b_out, lse, seq):
    t_tok = qkvg.shape[0]
    nq = seq // TQ
    nkb = seq // TK
    ratio = TQ // TK

    def body(qa_ref, ka_ref, kt_ref, v_ref, do_ref, o_ref, lse_ref, dq_ref, dk_ref, dv_ref, df_ref,
             dqt_s, out_s, row_s, dfk_s):
        ones_b = jnp.ones((8, TQ), BF16)
        ones_f = jnp.ones((8, HEAD_DIM), F32)
        eye = _eye(HEAD_DIM, BF16)
        for e in range(2):
            lo, hi = e * HEAD_DIM, (e + 1) * HEAD_DIM
            for ii in range(nq):
                rows = slice(ii * TQ, (ii + 1) * TQ)
                do = do_ref[rows, :][:, lo:hi].astype(F32)
                ov = o_ref[rows, :][:, lo:hi].astype(F32)
                row_s[ii, 0] = _nt(ones_f, do * ov, HI)
                row_s[ii, 1] = jnp.broadcast_to(lse_ref[e:e + 1, ii * TQ:(ii + 1) * TQ], (8, TQ))
                dqt_s[ii] = jnp.zeros((128, TQ), F32)

            def kblock(j, _):
                krow = pl.multiple_of(j * TK, TK)
                kj = ka_ref[e, pl.ds(krow, TK), :]
                ktj = kt_ref[e, j]
                vj = v_ref[pl.ds(krow, TK), :][:, lo:hi]
                spos = j * TK + lax.broadcasted_iota(jnp.int32, (TK, 1), 0)

                def qblock(i, carry, masked):
                    dk_acc, dv_acc, dfk = carry
                    qrow = pl.multiple_of(i * TQ, TQ)
                    qa = qa_ref[e, pl.ds(qrow, TQ), :]
                    doh = do_ref[pl.ds(qrow, TQ), :][:, lo:hi]
                    pt = jnp.exp(_nt(kj, qa) - row_s[i, 1][0:1, :])
                    if masked:
                        tpos = i * TQ + lax.broadcasted_iota(jnp.int32, (1, TQ), 1)
                        pt = jnp.where(spos <= tpos, pt, 0.0)
                    dst = pt * (_nt(vj, doh) - row_s[i, 0][0:1, :])
                    dst_b = dst.astype(BF16)
                    dv_acc = dv_acc + _nn(pt.astype(BF16), doh)
                    dk_acc = dk_acc + _nn(dst_b, qa)
                    dqt_s[i] += _nn(ktj, dst_b)
                    dfk = dfk + _nt(ones_b, dst_b)
                    return dk_acc, dv_acc, dfk

                i0 = j // ratio
                carry = (jnp.zeros((TK, 128), F32), jnp.zeros((TK, HEAD_DIM), F32), jnp.zeros((8, TK), F32))
                carry = qblock(i0, carry, True)
                dk_acc, dv_acc, dfk = lax.fori_loop(i0 + 1, nq, functools.partial(qblock, masked=False), carry)
                out_s[1, e, pl.ds(krow, TK), :] = dk_acc[:, :HEAD_DIM]
                out_s[2, e, pl.ds(krow, TK), :] = dv_acc
                dfk_s[j] = dfk
                return 0

            lax.fori_loop(0, nkb, kblock, 0)
            for ii in range(nq):
                dqt = dqt_s[ii]
                out_s[0, e, ii * TQ:(ii + 1) * TQ, :] = _tn(dqt[0:HEAD_DIM, :].astype(BF16), eye) * SCALE
                for kk in range(ratio):
                    jj = ii * ratio + kk
                    df_ref[e:e + 1, jj * TK:(jj + 1) * TK] = dqt[67:68, kk * TK:(kk + 1) * TK] - dfk_s[jj][0:1, :]
        for k, ref in enumerate((dq_ref, dk_ref, dv_ref)):
            ref[...] = jnp.concatenate([out_s[k, 0], out_s[k, 1]], axis=1).astype(BF16)

    aug_blk = pl.BlockSpec((2, seq, 128), lambda b, p: (p, b, 0))
    pair_blk = pl.BlockSpec((seq, 128), lambda b, p: (b, p))
    row_blk = pl.BlockSpec((None, 2, seq), lambda b, p: (p, 0, b))
    return _pallas(
        body, name="fox_bwd", grid=(t_tok // seq, N_HEADS // 2),
        in_specs=[aug_blk, aug_blk, pl.BlockSpec((2, nkb, 128, TK), lambda b, p: (p, b, 0, 0)),
                  pl.BlockSpec((seq, 128), lambda b, p: (b, C_BV // 128 + p)), pair_blk, pair_blk, row_blk],
        out_specs=[pair_blk, pair_blk, pair_blk, row_blk],
        out_shape=[jax.ShapeDtypeStruct((t_tok, 512), BF16)] * 3
        + [jax.ShapeDtypeStruct((N_HEADS // 2, 2, t_tok), F32)],
        scratch_shapes=[pltpu.VMEM((nq, 128, TQ), F32), pltpu.VMEM((3, 2, seq, HEAD_DIM), F32),
                        pltpu.VMEM((nq, 2, 8, TQ), F32), pltpu.VMEM((nkb, 8, TK), F32)],
        compiler_params=_cp(("arbitrary", "arbitrary"), VMEM_BIG))(q_aug, k_aug, kt, qkvg, du_b, b_out, lse)


def _fox_fwd(qkvg, f_row, f_col, seq):
    t_tok = qkvg.shape[0]
    nq = seq // TQ

    def body(q_ref, k_ref, v_ref, fr_ref, fc_ref, o_ref, lse_ref, fk_s):
        i = pl.program_id(1)
        for jj in range(nq):
            fk_s[jj] = fr_ref[:, jj * TQ:(jj + 1) * TQ]
        fcol = fc_ref[...]
        tpos = i * TQ + lax.broadcasted_iota(jnp.int32, (TQ, 1), 0)
        lane = lax.broadcasted_iota(jnp.int32, (TQ, 128), 1)
        lse_tile = jnp.zeros((TQ, 128), F32)
        for p in range(N_HEADS // 2):
            q2 = q_ref[:, 128 * p:128 * (p + 1)]
            qs = [q2[:, :HEAD_DIM], q2[:, HEAD_DIM:]]
            fqs = [_col(fcol, 2 * p + e) for e in range(2)]

            def kblock(j, carry):
                row0 = pl.multiple_of(j * TQ, TQ)
                k2 = k_ref[pl.ds(row0, TQ), 128 * p:128 * (p + 1)]
                v2 = v_ref[pl.ds(row0, TQ), 128 * p:128 * (p + 1)]
                fk8 = fk_s[j]
                spos = j * TQ + lax.broadcasted_iota(jnp.int32, (1, TQ), 1)
                keep = spos <= tpos
                new = []
                for e in range(2):
                    m, l, acc = carry[3 * e:3 * e + 3]
                    kh = k2[:, e * HEAD_DIM:(e + 1) * HEAD_DIM]
                    vh = v2[:, e * HEAD_DIM:(e + 1) * HEAD_DIM]
                    s = _nt(qs[e], kh) * SCALE + (fqs[e] - fk8[2 * p + e:2 * p + e + 1, :])
                    s = jnp.where(keep, s, NEG)
                    m_new = jnp.maximum(m, jnp.max(s, axis=1, keepdims=True))
                    alpha = jnp.exp(m - m_new)
                    pe = jnp.exp(s - m_new)
                    l = alpha * l + jnp.sum(pe, axis=1, keepdims=True)
                    acc = alpha * acc + _nn(pe.astype(BF16), vh)
                    new += [m_new, l, acc]
                return tuple(new)

            init = (jnp.full((TQ, 1), NEG, F32), jnp.zeros((TQ, 1), F32), jnp.zeros((TQ, HEAD_DIM), F32)) * 2
            res = lax.fori_loop(0, i + 1, kblock, init)
            outs = []
            for e in range(2):
                m, l, acc = res[3 * e:3 * e + 3]
                outs.append(acc / l)
                lse_tile = jnp.where(lane == 2 * p + e, m + jnp.log(l), lse_tile)
            o_ref[:, 128 * p:128 * (p + 1)] = jnp.concatenate(outs, axis=1).astype(BF16)
        lse_ref[...] = lse_tile

    return _pallas(
        body, name="fox_fwd", grid=(t_tok // seq, nq),
        in_specs=[pl.BlockSpec((TQ, 512), lambda b, i: (b * nq + i, C_BQ // 512)),
                  pl.BlockSpec((seq, 512), lambda b, i: (b, C_BK // 512)),
                  pl.BlockSpec((seq, 512), lambda b, i: (b, C_BV // 512)),
                  pl.BlockSpec((N_HEADS, seq), lambda b, i: (0, b)),
                  pl.BlockSpec((TQ, 128), lambda b, i: (b * nq + i, 0))],
        out_specs=[pl.BlockSpec((TQ, 512), lambda b, i: (b * nq + i, 0)),
                   pl.BlockSpec((TQ, 128), lambda b, i: (b * nq + i, 0))],
        out_shape=[jax.ShapeDtypeStruct((t_tok, 512), BF16), jax.ShapeDtypeStruct((t_tok, 128), F32)],
        scratch_shapes=[pltpu.VMEM((nq, N_HEADS, TQ), F32)],
        compiler_params=_cp(("arbitrary", "arbitrary"), VMEM_MID))(qkvg, qkvg, qkvg, f_row, f_col)


def _fox_bwd(qkvg, du_b, b_out, lse, f_row, f_col, seq):
    t_tok = qkvg.shape[0]
    nq = seq // TQ

    def body(q_ref, k_ref, v_ref, do_ref, o_ref, lse_ref, fr_ref, fc_ref,
             dq_ref, dk_ref, dv_ref, df_ref, dq_s, dk_s, dv_s, col_s, df_s, fk_s):
        p = pl.program_id(1)
        for jj in range(nq):
            fk_s[jj] = fr_ref[:, jj * TQ:(jj + 1) * TQ]
        eye = (lax.broadcasted_iota(jnp.int32, (TQ, TQ), 0) == lax.broadcasted_iota(jnp.int32, (TQ, TQ), 1)).astype(F32)
        for e in range(2):
            h = 2 * p + e
            lo, hi = e * HEAD_DIM, (e + 1) * HEAD_DIM
            for ii in range(nq):
                rows = slice(ii * TQ, (ii + 1) * TQ)
                do = do_ref[rows, :][:, lo:hi].astype(F32)
                ov = o_ref[rows, :][:, lo:hi].astype(F32)
                col_s[0, rows, :] = jnp.sum(do * ov, axis=1, keepdims=True)
                col_s[1, rows, :] = _col(lse_ref[rows, :], h)
                col_s[2, rows, :] = _col(fc_ref[rows, :], h)
                dq_s[rows, :] = jnp.zeros((TQ, HEAD_DIM), F32)
                df_s[ii] = jnp.zeros((8, TQ), F32)
                col_s[3, rows, :] = jnp.zeros((TQ, 1), F32)

            def kblock(j, _):
                krow = pl.multiple_of(j * TQ, TQ)
                kh = k_ref[pl.ds(krow, TQ), :][:, lo:hi]
                vh = v_ref[pl.ds(krow, TQ), :][:, lo:hi]
                fk = _row(fk_s[j], h)
                spos = j * TQ + lax.broadcasted_iota(jnp.int32, (1, TQ), 1)

                def qblock(i, carry):
                    dk_acc, dv_acc, dfk = carry
                    qrow = pl.multiple_of(i * TQ, TQ)
                    qh = q_ref[pl.ds(qrow, TQ), :][:, lo:hi]
                    doh = do_ref[pl.ds(qrow, TQ), :][:, lo:hi]
                    delta = col_s[0, pl.ds(qrow, TQ), :]
                    lse_q = col_s[1, pl.ds(qrow, TQ), :]
                    fq = col_s[2, pl.ds(qrow, TQ), :]
                    tpos = i * TQ + lax.broadcasted_iota(jnp.int32, (TQ, 1), 0)
                    s = _nt(qh, kh) * SCALE + (fq - fk)
                    pr = jnp.where(spos <= tpos, jnp.exp(s - lse_q), 0.0)
                    dp = _nt(doh, vh)
                    ds = pr * (dp - delta)
                    ds_b = ds.astype(BF16)
                    dv_acc = dv_acc + _tn(pr.astype(BF16), doh)
                    dk_acc = dk_acc + _tn(ds_b, qh)
                    dq_s[pl.ds(qrow, TQ), :] += _nn(ds_b, kh)
                    col_s[3, pl.ds(qrow, TQ), :] += jnp.sum(ds, axis=1, keepdims=True)
                    dfk = dfk + jnp.sum(ds, axis=0, keepdims=True)
                    return dk_acc, dv_acc, dfk

                zero = jnp.zeros((TQ, HEAD_DIM), F32)
                dk_acc, dv_acc, dfk = lax.fori_loop(j, nq, qblock, (zero, zero, jnp.zeros((1, TQ), F32)))
                dk_s[e, pl.ds(krow, TQ), :] = dk_acc * SCALE
                dv_s[e, pl.ds(krow, TQ), :] = dv_acc
                df_s[j] -= jnp.broadcast_to(dfk, (8, TQ))
                return 0

            lax.fori_loop(0, nq, kblock, 0)
            dq_s2 = dq_s[...] * SCALE
            dk_s[2 + e] = dq_s2
            for ii in range(nq):
                dfq = jnp.broadcast_to(col_s[3, ii * TQ:(ii + 1) * TQ, :], (TQ, 128))
                df_ref[e:e + 1, ii * TQ:(ii + 1) * TQ] = _tn(dfq, eye, HI)[0:1, :] + df_s[ii][0:1, :]
        dq_ref[...] = jnp.concatenate([dk_s[2], dk_s[3]], axis=1).astype(BF16)
        dk_ref[...] = jnp.concatenate([dk_s[0], dk_s[1]], axis=1).astype(BF16)
        dv_ref[...] = jnp.concatenate([dv_s[0], dv_s[1]], axis=1).astype(BF16)

    blk = lambda off: pl.BlockSpec((seq, 128), lambda b, p: (b, off // 128 + p))
    out_blk = pl.BlockSpec((seq, 128), lambda b, p: (b, p))
    return _pallas(
        body, name="fox_bwd", grid=(t_tok // seq, N_HEADS // 2),
        in_specs=[blk(C_BQ), blk(C_BK), blk(C_BV), out_blk, out_blk,
                  pl.BlockSpec((seq, 128), lambda b, p: (b, 0)),
                  pl.BlockSpec((N_HEADS, seq), lambda b, p: (0, b)),
                  pl.BlockSpec((seq, 128), lambda b, p: (b, 0))],
        out_specs=[out_blk, out_blk, out_blk, pl.BlockSpec((None, 2, seq), lambda b, p: (p, 0, b))],
        out_shape=[jax.ShapeDtypeStruct((t_tok, 512), BF16)] * 3
        + [jax.ShapeDtypeStruct((N_HEADS // 2, 2, t_tok), F32)],
        scratch_shapes=[pltpu.VMEM((seq, HEAD_DIM), F32), pltpu.VMEM((4, seq, HEAD_DIM), F32),
                        pltpu.VMEM((2, seq, HEAD_DIM), F32), pltpu.VMEM((4, seq, 1), F32),
                        pltpu.VMEM((nq, 8, TQ), F32), pltpu.VMEM((nq, N_HEADS, TQ), F32)],
        compiler_params=_cp(("arbitrary", "arbitrary"), VMEM_BIG))(qkvg, qkvg, qkvg, du_b, b_out, lse, f_row, f_col)


def _swa_window(k_ref, v_ref, n):
    prev = pl.multiple_of(jnp.maximum(n - 1, 0) * BLOCK, BLOCK)
    cur = pl.multiple_of(n * BLOCK, BLOCK)
    kwin = jnp.concatenate([k_ref[pl.ds(prev, BLOCK), :], k_ref[pl.ds(cur, BLOCK), :]], axis=0)
    vwin = jnp.concatenate([v_ref[pl.ds(prev, BLOCK), :], v_ref[pl.ds(cur, BLOCK), :]], axis=0)
    ti = lax.broadcasted_iota(jnp.int32, (BLOCK, 2 * BLOCK), 0)
    sj = lax.broadcasted_iota(jnp.int32, (BLOCK, 2 * BLOCK), 1)
    rel = ti - sj + BLOCK
    first_key = jnp.where(n > 0, 0, BLOCK)
    mask = jnp.logical_and(jnp.logical_and(rel >= 0, rel < BLOCK), sj >= first_key)
    return kwin, vwin, mask, prev, cur


def _head_cols(ref, h):
    pair = ref[:, 128 * (h // 2):128 * (h // 2 + 1)]
    return pair[:, (h % 2) * HEAD_DIM:(h % 2 + 1) * HEAD_DIM]


def _swa_logits(q_ref, kwin, bias_ref, h, mask):
    hk = h // KV_GROUP
    s = _nt(_head_cols(q_ref, h), kwin[:, hk * HEAD_DIM:(hk + 1) * HEAD_DIM]) * SCALE + bias_ref[h]
    return jnp.where(mask, s, NEG)


def _swa_fwd(qkvg, bias, sinks, seq):
    t_tok = qkvg.shape[0]
    nb = seq // BLOCK

    def body(sink_ref, q_ref, k_ref, v_ref, bias_ref, o_ref, lse_ref, s_s, p_s, den_s):
        g = pl.program_id(1)
        subs = [pl.ds(s * BLOCK, BLOCK) for s in range(SWA_SUB)]
        wins = [_swa_window(k_ref, v_ref, SWA_SUB * g + s) for s in range(SWA_SUB)]
        for s in range(SWA_SUB):
            for h in range(N_HEADS):
                s_s[s * N_HEADS + h] = _swa_logits(q_ref.at[subs[s]], wins[s][0], bias_ref, h, wins[s][2])
        lane = lax.broadcasted_iota(jnp.int32, (BLOCK, 128), 1)
        for s in range(SWA_SUB):
            lse_tile = jnp.zeros((BLOCK, 128), F32)
            for h in range(N_HEADS):
                sc = s_s[s * N_HEADS + h]
                sink = sink_ref[h]
                m = jnp.maximum(jnp.max(sc, axis=1, keepdims=True), sink)
                pe = jnp.exp(sc - m)
                den = jnp.sum(pe, axis=1, keepdims=True) + jnp.exp(sink - m)
                p_s[s * N_HEADS + h] = pe.astype(BF16)
                den_s[s * N_HEADS + h] = den
                lse_tile = jnp.where(lane == h, m + jnp.log(den), lse_tile)
            lse_ref[subs[s], :] = lse_tile
        for s in range(SWA_SUB):
            vwin = wins[s][1]
            for pr in range(N_HEADS // 2):
                outs = []
                for h in (2 * pr, 2 * pr + 1):
                    hk = h // KV_GROUP
                    outs.append(_nn(p_s[s * N_HEADS + h], vwin[:, hk * HEAD_DIM:(hk + 1) * HEAD_DIM]) / den_s[s * N_HEADS + h])
                o_ref[subs[s], 128 * pr:128 * (pr + 1)] = jnp.concatenate(outs, axis=1).astype(BF16)

    rows = SWA_SUB * BLOCK
    steps = nb // SWA_SUB
    return _pallas(
        body, name="swa_fwd", grid=(t_tok // seq, steps),
        in_specs=[pl.BlockSpec(memory_space=pltpu.SMEM),
                  pl.BlockSpec((rows, 512), lambda b, n: (b * steps + n, C_AQ // 512)),
                  pl.BlockSpec((seq, 128), lambda b, n: (b, C_AK // 128)),
                  pl.BlockSpec((seq, 128), lambda b, n: (b, C_AV // 128)),
                  pl.BlockSpec((N_HEADS, BLOCK, 2 * BLOCK), lambda b, n: (0, 0, 0))],
        out_specs=[pl.BlockSpec((rows, 512), lambda b, n: (b * steps + n, 0)),
                   pl.BlockSpec((rows, 128), lambda b, n: (b * steps + n, 0))],
        out_shape=[jax.ShapeDtypeStruct((t_tok, 512), BF16), jax.ShapeDtypeStruct((t_tok, 128), F32)],
        scratch_shapes=[pltpu.VMEM((SWA_SUB * N_HEADS, BLOCK, 2 * BLOCK), F32),
                        pltpu.VMEM((SWA_SUB * N_HEADS, BLOCK, 2 * BLOCK), BF16),
                        pltpu.VMEM((SWA_SUB * N_HEADS, BLOCK, 1), F32)],
        compiler_params=_cp(("arbitrary", "arbitrary"), VMEM_MID))(sinks, qkvg, qkvg, qkvg, bias)


def _swa_bwd(qkvg, du_a, a_out, lse, bias, sinks, seq):
    t_tok = qkvg.shape[0]
    nb = seq // BLOCK

    def body(sink_ref, q_ref, k_ref, v_ref, do_ref, o_ref, lse_ref, bias_ref,
             dq_ref, dkv_ref, dbias_ref, dsink_ref, kv_s, s_s, dp_s, pb_s, db_s):
        b, n = pl.program_id(0), pl.program_id(1)

        @pl.when(jnp.logical_and(b == 0, n == 0))
        def _():
            dbias_ref[...] = jnp.zeros_like(dbias_ref)
            dsink_ref[...] = jnp.zeros_like(dsink_ref)

        @pl.when(n == 0)
        def _():
            kv_s[...] = jnp.zeros_like(kv_s)

        subs = [pl.ds(s * BLOCK, BLOCK) for s in range(SWA_SUB)]
        wins = [_swa_window(k_ref, v_ref, SWA_SUB * n + s) for s in range(SWA_SUB)]
        for s in range(SWA_SUB):
            kwin, vwin, mask = wins[s][:3]
            for h in range(N_HEADS):
                hk = h // KV_GROUP
                s_s[s * N_HEADS + h] = _swa_logits(q_ref.at[subs[s]], kwin, bias_ref, h, mask)
                dp_s[s * N_HEADS + h] = _nt(_head_cols(do_ref.at[subs[s]], h), vwin[:, hk * HEAD_DIM:(hk + 1) * HEAD_DIM])
        for s in range(SWA_SUB):
            lse_tile = lse_ref[subs[s], :]
            do_s, o_s = do_ref.at[subs[s]], o_ref.at[subs[s]]
            for h in range(N_HEADS):
                delta = jnp.sum(_head_cols(do_s, h).astype(F32) * _head_cols(o_s, h).astype(F32), axis=1, keepdims=True)
                lse_h = _col(lse_tile, h)
                pe = jnp.exp(s_s[s * N_HEADS + h] - lse_h)
                ds = pe * (dp_s[s * N_HEADS + h] - delta)
                dbias_ref[h] += ds
                psink = jnp.exp(sink_ref[h] - lse_h)
                dsink_ref[h:h + 1, :] += jnp.broadcast_to(jnp.sum(-psink * delta, axis=0, keepdims=True), (1, 128))
                pb_s[s * N_HEADS + h] = pe.astype(BF16)
                db_s[s * N_HEADS + h] = ds.astype(BF16)
        for s in range(SWA_SUB):
            kwin, _, _, prev, cur = wins[s]
            q_s, do_s = q_ref.at[subs[s]], do_ref.at[subs[s]]
            for pr in range(N_HEADS // 2):
                dqs = []
                for h in (2 * pr, 2 * pr + 1):
                    hk = h // KV_GROUP
                    dqs.append(_nn(db_s[s * N_HEADS + h], kwin[:, hk * HEAD_DIM:(hk + 1) * HEAD_DIM]) * SCALE)
                dq_ref[subs[s], 128 * pr:128 * (pr + 1)] = jnp.concatenate(dqs, axis=1).astype(BF16)
            dks, dvs = [], []
            for hk in range(N_HEADS // KV_GROUP):
                dk = jnp.zeros((2 * BLOCK, HEAD_DIM), F32)
                dv = jnp.zeros((2 * BLOCK, HEAD_DIM), F32)
                for h in range(hk * KV_GROUP, (hk + 1) * KV_GROUP):
                    dk = dk + _tn(db_s[s * N_HEADS + h], _head_cols(q_s, h))
                    dv = dv + _tn(pb_s[s * N_HEADS + h], _head_cols(do_s, h))
                dks.append(dk * SCALE)
                dvs.append(dv)
            upd = jnp.concatenate(dks + dvs, axis=1)
            kv_s[pl.ds(prev, BLOCK), :] += upd[:BLOCK]
            kv_s[pl.ds(cur, BLOCK), :] += upd[BLOCK:]

        @pl.when(n == steps - 1)
        def _():
            dkv_ref[...] = kv_s[...].astype(BF16)

    rows = SWA_SUB * BLOCK
    steps = nb // SWA_SUB
    tile = (SWA_SUB * N_HEADS, BLOCK, 2 * BLOCK)
    return _pallas(
        body, name="swa_bwd", grid=(t_tok // seq, steps),
        in_specs=[pl.BlockSpec(memory_space=pltpu.SMEM),
                  pl.BlockSpec((rows, 512), lambda b, n: (b * steps + n, C_AQ // 512)),
                  pl.BlockSpec((seq, 128), lambda b, n: (b, C_AK // 128)),
                  pl.BlockSpec((seq, 128), lambda b, n: (b, C_AV // 128)),
                  pl.BlockSpec((rows, 512), lambda b, n: (b * steps + n, 0)),
                  pl.BlockSpec((rows, 512), lambda b, n: (b * steps + n, 0)),
                  pl.BlockSpec((rows, 128), lambda b, n: (b * steps + n, 0)),
                  pl.BlockSpec((N_HEADS, BLOCK, 2 * BLOCK), lambda b, n: (0, 0, 0))],
        out_specs=[pl.BlockSpec((rows, 512), lambda b, n: (b * steps + n, 0)),
                   pl.BlockSpec((seq, 256), lambda b, n: (b, 0)),
                   pl.BlockSpec((N_HEADS, BLOCK, 2 * BLOCK), lambda b, n: (0, 0, 0)),
                   pl.BlockSpec((N_HEADS, 128), lambda b, n: (0, 0))],
        out_shape=[jax.ShapeDtypeStruct((t_tok, 512), BF16), jax.ShapeDtypeStruct((t_tok, 256), BF16),
                   jax.ShapeDtypeStruct((N_HEADS, BLOCK, 2 * BLOCK), F32), jax.ShapeDtypeStruct((N_HEADS, 128), F32)],
        scratch_shapes=[pltpu.VMEM((seq, 256), F32), pltpu.VMEM(tile, F32), pltpu.VMEM(tile, F32),
                        pltpu.VMEM(tile, BF16), pltpu.VMEM(tile, BF16)],
        compiler_params=_cp(("arbitrary", "arbitrary"), VMEM_MID))(sinks, qkvg, qkvg, qkvg, du_a, a_out, lse, bias)


def _out_proj(name, u_parts, gate_arr, gate_blk, w_out, x, gmod, seq):
    t_tok = x.shape[0]
    nu = len(u_parts)

    def body(*refs):
        u_refs = refs[:nu]
        g_ref, w_ref, x_ref, gm_ref, yg_ref, y_ref, xn_ref = refs[nu:]
        u = jnp.concatenate([r[...].astype(F32) for r in u_refs], axis=1) if nu > 1 else u_refs[0][...].astype(F32)
        yg = (u * _silu(g_ref[...].astype(F32))).astype(BF16)
        yg_ref[...] = yg
        y = _nn(yg, w_ref[...])
        y_ref[...] = y.astype(BF16)
        xn_ref[...] = x_ref[...] + gm_ref[...] * y

    row = lambda w: pl.BlockSpec((TM, w), lambda i: (i, 0))
    in_specs = [row(u.shape[1]) for u in u_parts]
    in_specs += [pl.BlockSpec((TM, D_MODEL), lambda i: (i, gate_blk)),
                 pl.BlockSpec((D_MODEL, D_MODEL), lambda i: (0, 0)), row(D_MODEL),
                 pl.BlockSpec((None, 1, D_MODEL), lambda i: (i * TM // seq, 0, 0))]
    return _pallas(
        body, name=name, grid=(t_tok // TM,), in_specs=in_specs,
        out_specs=[row(D_MODEL)] * 3,
        out_shape=[jax.ShapeDtypeStruct((t_tok, D_MODEL), BF16)] * 2 + [jax.ShapeDtypeStruct((t_tok, D_MODEL), F32)],
        compiler_params=_cp(("arbitrary",), VMEM_MID))(*u_parts, gate_arr, w_out, x, gmod)


def _out_proj_bwd(name, dxn, gmod, y, w_out, seq, attn=None):
    t_tok = dxn.shape[0]
    tiles_per_seq = seq // TM

    def body(*refs):
        if attn is None:
            dxn_ref, gm_ref, y_ref, w_ref, dy_ref, dgm_ref, dyg_ref = refs
        else:
            dxn_ref, gm_ref, y_ref, w_ref, a_ref, b_ref, g_ref, dy_ref, dgm_ref, dua_ref, dub_ref, dg_ref = refs
        i = pl.program_id(0)
        dxv = dxn_ref[...]
        dy = (dxv * gm_ref[...]).astype(BF16)
        dy_ref[...] = dy

        @pl.when(i % tiles_per_seq == 0)
        def _():
            dgm_ref[...] = jnp.zeros_like(dgm_ref)

        dgm_ref[...] += jnp.sum(dxv * y_ref[...].astype(F32), axis=0, keepdims=True)
        dyg = _nn(dy, w_ref[...])
        if attn is None:
            dyg_ref[...] = dyg
        else:
            gt = g_ref[...].astype(F32)
            du = dyg * _silu(gt)
            dua_ref[...] = du[:, :512].astype(BF16)
            dub_ref[...] = du[:, 512:].astype(BF16)
            u = jnp.concatenate([a_ref[...].astype(F32), b_ref[...].astype(F32)], axis=1)
            dg_ref[...] = (dyg * u * _dsilu(gt)).astype(BF16)

    row = lambda w: pl.BlockSpec((TM, w), lambda i: (i, 0))
    mod_spec = pl.BlockSpec((None, 1, D_MODEL), lambda i: (i * TM // seq, 0, 0))
    in_specs = [row(D_MODEL), mod_spec, row(D_MODEL), pl.BlockSpec((D_MODEL, D_MODEL), lambda i: (0, 0))]
    out_specs = [row(D_MODEL), mod_spec]
    out_shape = [jax.ShapeDtypeStruct((t_tok, D_MODEL), BF16), jax.ShapeDtypeStruct(gmod.shape, F32)]
    args = [dxn, gmod, y, w_out]
    if attn is None:
        out_specs.append(row(D_MODEL))
        out_shape.append(jax.ShapeDtypeStruct((t_tok, D_MODEL), F32))
    else:
        in_specs += [row(512), row(512), pl.BlockSpec((TM, D_MODEL), lambda i: (i, C_GATE // D_MODEL))]
        out_specs += [row(512), row(512), row(D_MODEL)]
        out_shape += [jax.ShapeDtypeStruct((t_tok, 512), BF16)] * 2 + [jax.ShapeDtypeStruct((t_tok, D_MODEL), BF16)]
        args += list(attn)
    return _pallas(body, name=name, grid=(t_tok // TM,), in_specs=in_specs, out_specs=out_specs,
                   out_shape=out_shape, compiler_params=_cp(("arbitrary",), VMEM_MID))(*args)


def _norm_bwd(name, parts, w, x, g, scale, dxn, seq, rows_part=None):
    t_tok = x.shape[0]
    npart = len(parts)
    tiles_per_seq = seq // TM
    nrow_in = 0 if rows_part is None else 2

    def body(*refs):
        p_refs = refs[:npart]
        w_ref, x_ref, g_ref, sc_ref, dxn_ref = refs[npart:npart + 5]
        dx_ref, dss_ref, dg_ref = refs[npart + 5 + nrow_in:]
        i = pl.program_id(0)
        dh = jnp.zeros((TM, D_MODEL), F32)
        if rows_part is not None:
            r_ref, wr_ref = refs[npart + 5:npart + 7]
            dh = dh + _tn(r_ref[...].astype(BF16), wr_ref[...])
        for (arr, off), p_ref in zip(parts, p_refs):
            width = arr.shape[1]
            for j in range(width // 256):
                pj = p_ref[:, j * 256:(j + 1) * 256]
                c0 = off + j * 256
                dh = dh + _nn(pj, w_ref[c0:c0 + 256, :])
        xv = x_ref[...]
        rstd = lax.rsqrt(jnp.mean(xv * xv, axis=-1, keepdims=True) + EPS)
        xhat = xv * rstd
        gv = g_ref[...]
        nrm = xhat * gv

        @pl.when(i % tiles_per_seq == 0)
        def _():
            dss_ref[...] = jnp.zeros_like(dss_ref)

        @pl.when(i == 0)
        def _():
            dg_ref[...] = jnp.zeros_like(dg_ref)

        dss_ref[0:1, :] += jnp.sum(dh, axis=0, keepdims=True)
        dss_ref[1:2, :] += jnp.sum(dh * nrm, axis=0, keepdims=True)
        dn = dh * (1.0 + sc_ref[...])
        dg_ref[0:1, :] += jnp.sum(dn * xhat, axis=0, keepdims=True)
        dxhat = dn * gv
        dx_ref[...] = rstd * (dxhat - xhat * jnp.mean(dxhat * xhat, axis=-1, keepdims=True)) + dxn_ref[...]

    row = lambda wd: pl.BlockSpec((TM, wd), lambda i: (i, 0))
    w_spec = pl.BlockSpec(w.shape, lambda i: (0, 0))
    in_specs = [row(a.shape[1]) for a, _ in parts]
    in_specs += [w_spec, row(D_MODEL), pl.BlockSpec((1, D_MODEL), lambda i: (0, 0)),
                 pl.BlockSpec((None, 1, D_MODEL), lambda i: (i * TM // seq, 0, 0)), row(D_MODEL)]
    args = [a for a, _ in parts] + [w, x, g, scale, dxn]
    if rows_part is not None:
        in_specs += [pl.BlockSpec((8, TM), lambda i: (0, i)), pl.BlockSpec((8, D_MODEL), lambda i: (0, 0))]
        args += list(rows_part)
    nseq = t_tok // seq
    return _pallas(
        body, name=name, grid=(t_tok // TM,), in_specs=in_specs,
        out_specs=[row(D_MODEL), pl.BlockSpec((None, 8, D_MODEL), lambda i: (i * TM // seq, 0, 0)),
                   pl.BlockSpec((8, D_MODEL), lambda i: (0, 0))],
        out_shape=[jax.ShapeDtypeStruct((t_tok, D_MODEL), F32), jax.ShapeDtypeStruct((nseq, 8, D_MODEL), F32),
                   jax.ShapeDtypeStruct((8, D_MODEL), F32)],
        compiler_params=_cp(("arbitrary",), VMEM_BIG))(*args)


def _dw(name, a, parts, blocked=None):
    t_tok, ka = a.shape
    tt = min(1024, t_tok)
    npart = len(parts)
    nt = t_tok // tt

    def body(*refs):
        a_ref = refs[0]
        p_refs = refs[1:1 + npart]
        o_refs = refs[1 + npart:1 + 2 * npart]
        acc_refs = refs[1 + 2 * npart:]
        t = pl.program_id(0)
        av = a_ref[...]
        for p_ref, acc in zip(p_refs, acc_refs):
            upd = _tn(av, p_ref[...])

            @pl.when(t == 0)
            def _():
                acc[...] = upd

            @pl.when(t > 0)
            def _():
                acc[...] += upd

        @pl.when(t == nt - 1)
        def _():
            for o_ref, acc in zip(o_refs, acc_refs):
                if blocked is None:
                    o_ref[...] = acc[...].astype(BF16)
                else:
                    for j in range(o_ref.shape[0]):
                        o_ref[j] = acc[:, j * blocked:(j + 1) * blocked].astype(BF16)

    in_specs = [pl.BlockSpec((tt, ka), lambda t: (t, 0))]
    in_specs += [pl.BlockSpec((tt, p.shape[1]), lambda t: (t, 0)) for p in parts]
    if blocked is None:
        out_shape = [jax.ShapeDtypeStruct((ka, p.shape[1]), BF16) for p in parts]
        out_specs = [pl.BlockSpec((ka, p.shape[1]), lambda t: (0, 0)) for p in parts]
    else:
        out_shape = [jax.ShapeDtypeStruct((p.shape[1] // blocked, ka, blocked), BF16) for p in parts]
        out_specs = [pl.BlockSpec((p.shape[1] // blocked, ka, blocked), lambda t: (0, 0, 0)) for p in parts]
    return _pallas(body, name=name, grid=(nt,), in_specs=in_specs, out_specs=out_specs, out_shape=out_shape,
                   scratch_shapes=[pltpu.VMEM((ka, p.shape[1]), F32) for p in parts],
                   compiler_params=_cp(("arbitrary",), VMEM_BIG))(a, *parts)


def _dw_rows(name, rows_t, h):
    t_tok = h.shape[0]
    tt = 512

    def body(r_ref, h_ref, o_ref):
        @pl.when(pl.program_id(0) == 0)
        def _():
            o_ref[...] = jnp.zeros_like(o_ref)

        o_ref[...] += _nn(r_ref[...].astype(BF16), h_ref[...])

    return _pallas(body, name=name, grid=(t_tok // tt,),
                   in_specs=[pl.BlockSpec((8, tt), lambda t: (0, t)), pl.BlockSpec((tt, D_MODEL), lambda t: (t, 0))],
                   out_specs=pl.BlockSpec((8, D_MODEL), lambda t: (0, 0)),
                   out_shape=jax.ShapeDtypeStruct((8, D_MODEL), F32),
                   compiler_params=_cp(("arbitrary",), VMEM_MID))(rows_t, h)


def _lru_gates(xc, blk, wa_ref, wx_ref, ba_ref, bx_ref, sp):
    cols = slice(blk * LRU_BLOCK_W, (blk + 1) * LRU_BLOCK_W)
    xb = xc[:, cols].astype(BF16)
    r = _sigmoid(_nn(xb, wa_ref[blk].astype(BF16)) + ba_ref[:, cols])
    ig = _sigmoid(_nn(xb, wx_ref[blk].astype(BF16)) + bx_ref[:, cols])
    log_a = -LRU_C * r * sp[:, cols]
    a = jnp.exp(log_a)
    x2 = 2.0 * log_a
    series = -x2 * (1.0 + x2 * (0.5 + x2 * (1.0 / 6.0)))
    z = jnp.where(x2 > -0.01, series, 1.0 - a * a)
    mult = z * lax.rsqrt(jnp.maximum(z, 1e-30))
    return xb, r, ig, a, mult


def _softplus_neg(lam):
    return jnp.maximum(-lam, 0.0) + jnp.log(1.0 + jnp.exp(-jnp.abs(lam)))


def _conv_taps(xe_ref, cw_ref, cb_ref):
    xc = cb_ref[...] + xe_ref[8:8 + TC, :] * cw_ref[3:4, :]
    for k in range(1, 4):
        xc = xc + xe_ref[8 - k:8 - k + TC, :] * cw_ref[3 - k:4 - k, :]
    return xc


def _lru_fwd(proj, cw, cb, w_a, b_a, w_x, b_x, lam, seq):
    t_tok = proj.shape[0]
    nc = seq // TC

    def body(x_ref, cw_ref, cb_ref, wa_ref, ba_ref, wx_ref, bx_ref, lam_ref, hs_ref, xe_s, a_s, u_s, h_s):
        c = pl.program_id(1)

        @pl.when(c == 0)
        def _():
            xe_s[0:8, :] = jnp.zeros((8, D_MODEL), F32)
            h_s[...] = jnp.zeros_like(h_s)

        xe_s[8:8 + TC, :] = x_ref[...]
        xc = _conv_taps(xe_s, cw_ref, cb_ref)
        sp = _softplus_neg(lam_ref[...])
        for blk in range(LRU_BLOCKS):
            cols = slice(blk * LRU_BLOCK_W, (blk + 1) * LRU_BLOCK_W)
            _, _, ig, a, mult = _lru_gates(xc, blk, wa_ref, wx_ref, ba_ref, bx_ref, sp)
            a_s[:, cols] = a
            u_s[:, cols] = mult * ig * xc[:, cols]

        def step(t, h):
            h = a_s[pl.ds(t, 1), :] * h + u_s[pl.ds(t, 1), :]
            hs_ref[pl.ds(t, 1), :] = h
            return h

        h_s[0:1, :] = lax.fori_loop(0, TC, step, h_s[0:1, :], unroll=8)
        xe_s[0:8, :] = xe_s[TC:TC + 8, :]

    full = lambda shape: pl.BlockSpec(shape, lambda b, c: (0,) * len(shape))
    return _pallas(
        body, name="lru_fwd", grid=(t_tok // seq, nc),
        in_specs=[pl.BlockSpec((TC, D_MODEL), lambda b, c: (b * nc + c, 0)), full((4, D_MODEL)), full((1, D_MODEL)),
                  full((LRU_BLOCKS, LRU_BLOCK_W, LRU_BLOCK_W)), full((1, D_MODEL)),
                  full((LRU_BLOCKS, LRU_BLOCK_W, LRU_BLOCK_W)), full((1, D_MODEL)), full((1, D_MODEL))],
        out_specs=pl.BlockSpec((TC, D_MODEL), lambda b, c: (b * nc + c, 0)),
        out_shape=jax.ShapeDtypeStruct((t_tok, D_MODEL), F32),
        scratch_shapes=[pltpu.VMEM((TC + 8, D_MODEL), F32), pltpu.VMEM((TC, D_MODEL), F32),
                        pltpu.VMEM((TC, D_MODEL), F32), pltpu.VMEM((8, D_MODEL), F32)],
        compiler_params=_cp(("arbitrary", "arbitrary"), VMEM_MID))(proj, cw, cb, w_a, b_a, w_x, b_x, lam)


def _lru_bwd(proj, hs, dyh, cw, cb, w_a, b_a, w_x, b_x, lam, seq):
    t_tok = proj.shape[0]
    nc = seq // TC

    def body(x_ref, xh_ref, g_ref, hs_ref, hh_ref, dy_ref, cw_ref, cb_ref, wa_ref, ba_ref, wx_ref, bx_ref, lam_ref,
             dp_ref, dcw_ref, dvec_ref, dwa_ref, dwx_ref,
             xe_s, he_s, de_s, a_s, r_s, i_s, m_s, dh_s, carry_s):
        b, cr = pl.program_id(0), pl.program_id(1)
        c = nc - 1 - cr

        @pl.when(jnp.logical_and(b == 0, cr == 0))
        def _():
            dcw_ref[...] = jnp.zeros_like(dcw_ref)
            dvec_ref[...] = jnp.zeros_like(dvec_ref)
            dwa_ref[...] = jnp.zeros_like(dwa_ref)
            dwx_ref[...] = jnp.zeros_like(dwx_ref)

        @pl.when(cr == 0)
        def _():
            carry_s[...] = jnp.zeros_like(carry_s)
            de_s[TC:TC + 8, :] = jnp.zeros((8, D_MODEL), F32)

        first = c == 0
        xe_s[0:8, :] = jnp.where(first, 0.0, xh_ref[...])
        xe_s[8:8 + TC, :] = x_ref[...]
        he_s[0:8, :] = jnp.where(first, 0.0, hh_ref[...])
        he_s[8:8 + TC, :] = hs_ref[...]
        xc = _conv_taps(xe_s, cw_ref, cb_ref)
        lam_v = lam_ref[...]
        sp = _softplus_neg(lam_v)
        for blk in range(LRU_BLOCKS):
            cols = slice(blk * LRU_BLOCK_W, (blk + 1) * LRU_BLOCK_W)
            _, r, ig, a, mult = _lru_gates(xc, blk, wa_ref, wx_ref, ba_ref, bx_ref, sp)
            a_s[:, cols], r_s[:, cols], i_s[:, cols], m_s[:, cols] = a, r, ig, mult

        gt = g_ref[...]
        dyh = dy_ref[...]
        dh_s[...] = dyh * _silu(gt)
        dp_ref[:, D_MODEL:] = (dyh * hs_ref[...] * _dsilu(gt)).astype(BF16)

        def step(k, carry):
            t = TC - 1 - k
            dh = dh_s[pl.ds(t, 1), :] + carry
            dh_s[pl.ds(t, 1), :] = dh
            return a_s[pl.ds(t, 1), :] * dh

        carry_s[0:1, :] = lax.fori_loop(0, TC, step, carry_s[0:1, :], unroll=8)

        hprev = he_s[7:7 + TC, :]
        for blk in range(LRU_BLOCKS):
            cols = slice(blk * LRU_BLOCK_W, (blk + 1) * LRU_BLOCK_W)
            xcb = xc[:, cols]
            a, r, ig, mult, dh = a_s[:, cols], r_s[:, cols], i_s[:, cols], m_s[:, cols], dh_s[:, cols]
            spb = sp[:, cols]
            dmult = dh * ig * xcb
            di = dh * mult * xcb
            dxc = dh * mult * ig
            dla = dh * hprev[:, cols] * a - dmult * (a * a) * lax.rsqrt(jnp.maximum(mult * mult, 1e-30))
            dr = dla * (-LRU_C * spb)
            dsp = jnp.sum(dla * (-LRU_C * r), axis=0, keepdims=True)
            dga = dr * r * (1.0 - r)
            dgx = di * ig * (1.0 - ig)
            dga_b, dgx_b = dga.astype(BF16), dgx.astype(BF16)
            xb = xcb.astype(BF16)
            dxc = dxc + _nt(dga_b, wa_ref[blk].astype(BF16)) + _nt(dgx_b, wx_ref[blk].astype(BF16))
            dwa_ref[blk] += _tn(xb, dga_b)
            dwx_ref[blk] += _tn(xb, dgx_b)
            dvec_ref[1:2, cols] += jnp.sum(dga, axis=0, keepdims=True)
            dvec_ref[2:3, cols] += jnp.sum(dgx, axis=0, keepdims=True)
            dvec_ref[3:4, cols] += dsp * (-1.0 / (1.0 + jnp.exp(lam_v[:, cols])))
            de_s[0:TC, cols] = dxc

        dxc = de_s[0:TC, :]
        dvec_ref[0:1, :] += jnp.sum(dxc, axis=0, keepdims=True)
        dxr = dxc * cw_ref[3:4, :]
        dcw_ref[3:4, :] += jnp.sum(dxc * xe_s[8:8 + TC, :], axis=0, keepdims=True)
        for k in range(1, 4):
            dxr = dxr + de_s[k:k + TC, :] * cw_ref[3 - k:4 - k, :]
            dcw_ref[3 - k:4 - k, :] += jnp.sum(dxc * xe_s[8 - k:8 - k + TC, :], axis=0, keepdims=True)
        dp_ref[:, :D_MODEL] = dxr.astype(BF16)
        de_s[TC:TC + 8, :] = de_s[0:8, :]

    chunk = lambda col: pl.BlockSpec((TC, D_MODEL), lambda b, cr: (b * nc + nc - 1 - cr, col))
    halo = lambda col: pl.BlockSpec(
        (8, D_MODEL), lambda b, cr: (jnp.maximum((b * nc + nc - 1 - cr) * (TC // 8) - 1, 0), col))
    full = lambda shape: pl.BlockSpec(shape, lambda b, cr: (0,) * len(shape))
    wblk = (LRU_BLOCKS, LRU_BLOCK_W, LRU_BLOCK_W)
    return _pallas(
        body, name="lru_bwd", grid=(t_tok // seq, nc),
        in_specs=[chunk(0), halo(0), chunk(1), chunk(0), halo(0), chunk(0),
                  full((4, D_MODEL)), full((1, D_MODEL)), full(wblk), full((1, D_MODEL)), full(wblk),
                  full((1, D_MODEL)), full((1, D_MODEL))],
        out_specs=[pl.BlockSpec((TC, 2 * D_MODEL), lambda b, cr: (b * nc + nc - 1 - cr, 0)),
                   full((8, D_MODEL)), full((8, D_MODEL)), full(wblk), full(wblk)],
        out_shape=[jax.ShapeDtypeStruct((t_tok, 2 * D_MODEL), BF16), jax.ShapeDtypeStruct((8, D_MODEL), F32),
                   jax.ShapeDtypeStruct((8, D_MODEL), F32), jax.ShapeDtypeStruct(wblk, F32),
                   jax.ShapeDtypeStruct(wblk, F32)],
        scratch_shapes=[pltpu.VMEM((TC + 8, D_MODEL), F32), pltpu.VMEM((TC + 8, D_MODEL), F32),
                        pltpu.VMEM((TC + 8, D_MODEL), F32)]
        + [pltpu.VMEM((TC, D_MODEL), F32)] * 5 + [pltpu.VMEM((8, D_MODEL), F32)],
        compiler_params=_cp(("arbitrary", "arbitrary"), VMEM_BIG),
    )(proj, proj, proj, hs, hs, dyh, cw, cb, w_a, b_a, w_x, b_x, lam)


def _last_layer_tail(hs, proj, w_out, w_out_t, x, gmod, final_g, target, seq):
    t_tok = x.shape[0]
    tiles_per_seq = seq // TM

    def body(hs_ref, g_ref, w_ref, wt_ref, x_ref, gm_ref, fg_ref, t_ref,
             yg_ref, dx_ref, dy_ref, dyg_ref, dgm_ref, loss_ref, dfg_ref):
        i = pl.program_id(0)

        @pl.when(i == 0)
        def _():
            loss_ref[...] = jnp.zeros_like(loss_ref)
            dfg_ref[...] = jnp.zeros_like(dfg_ref)

        @pl.when(i % tiles_per_seq == 0)
        def _():
            dgm_ref[...] = jnp.zeros_like(dgm_ref)

        gm = gm_ref[...]
        yg = (hs_ref[...] * _silu(g_ref[...])).astype(BF16)
        yg_ref[...] = yg
        y = _nn(yg, w_ref[...])
        xv = x_ref[...] + gm * y
        gv = fg_ref[...]
        rstd = lax.rsqrt(jnp.mean(xv * xv, axis=-1, keepdims=True) + EPS)
        xhat = xv * rstd
        err = xhat * gv - t_ref[...]
        loss_ref[0:1, :] += jnp.sum(err * err, axis=0, keepdims=True) * (0.5 / D_MODEL)
        dout = err * (1.0 / D_MODEL)
        dfg_ref[0:1, :] += jnp.sum(dout * xhat, axis=0, keepdims=True)
        dxhat = dout * gv
        dxv = rstd * (dxhat - xhat * jnp.mean(dxhat * xhat, axis=-1, keepdims=True))
        dx_ref[...] = dxv
        dgm_ref[...] += jnp.sum(dxv * y, axis=0, keepdims=True)
        dy = (dxv * gm).astype(BF16)
        dy_ref[...] = dy
        dyg_ref[...] = _nn(dy, wt_ref[...])

    row = pl.BlockSpec((TM, D_MODEL), lambda i: (i, 0))
    acc = pl.BlockSpec((8, D_MODEL), lambda i: (0, 0))
    mod_spec = pl.BlockSpec((None, 1, D_MODEL), lambda i: (i * TM // seq, 0, 0))
    return _pallas(
        body, name="last_layer_tail", grid=(t_tok // TM,),
        in_specs=[row, pl.BlockSpec((TM, D_MODEL), lambda i: (i, 1)), pl.BlockSpec((D_MODEL, D_MODEL), lambda i: (0, 0)),
                  pl.BlockSpec((D_MODEL, D_MODEL), lambda i: (0, 0)),
                  row, mod_spec, pl.BlockSpec((1, D_MODEL), lambda i: (0, 0)), row],
        out_specs=[row, row, row, row, mod_spec, acc, acc],
        out_shape=[jax.ShapeDtypeStruct((t_tok, D_MODEL), BF16), jax.ShapeDtypeStruct((t_tok, D_MODEL), F32),
                   jax.ShapeDtypeStruct((t_tok, D_MODEL), BF16), jax.ShapeDtypeStruct((t_tok, D_MODEL), F32),
                   jax.ShapeDtypeStruct(gmod.shape, F32), jax.ShapeDtypeStruct((8, D_MODEL), F32),
                   jax.ShapeDtypeStruct((8, D_MODEL), F32)],
        compiler_params=_cp(("arbitrary",), VMEM_BIG))(hs, proj, w_out, w_out_t, x, gmod, final_g, target)


def _final_loss(x, g, target):
    t_tok = x.shape[0]

    def body(x_ref, g_ref, t_ref, dx_ref, loss_ref, dg_ref):
        @pl.when(pl.program_id(0) == 0)
        def _():
            loss_ref[...] = jnp.zeros_like(loss_ref)
            dg_ref[...] = jnp.zeros_like(dg_ref)

        xv = x_ref[...]
        gv = g_ref[...]
        rstd = lax.rsqrt(jnp.mean(xv * xv, axis=-1, keepdims=True) + EPS)
        xhat = xv * rstd
        err = xhat * gv - t_ref[...]
        loss_ref[0:1, :] += jnp.sum(err * err, axis=0, keepdims=True) * (0.5 / D_MODEL)
        dout = err * (1.0 / D_MODEL)
        dg_ref[0:1, :] += jnp.sum(dout * xhat, axis=0, keepdims=True)
        dxhat = dout * gv
        dx_ref[...] = rstd * (dxhat - xhat * jnp.mean(dxhat * xhat, axis=-1, keepdims=True))

    row = pl.BlockSpec((TM, D_MODEL), lambda i: (i, 0))
    acc = pl.BlockSpec((8, D_MODEL), lambda i: (0, 0))
    return _pallas(body, name="final_loss", grid=(t_tok // TM,),
                   in_specs=[row, pl.BlockSpec((1, D_MODEL), lambda i: (0, 0)), row],
                   out_specs=[row, acc, acc],
                   out_shape=[jax.ShapeDtypeStruct((t_tok, D_MODEL), F32)] + [jax.ShapeDtypeStruct((8, D_MODEL), F32)] * 2,
                   compiler_params=_cp(("arbitrary",), VMEM_MID))(x, g, target)


def _adam_math(w, g, m, v):
    m_new = ADAM_B1 * m + (1.0 - ADAM_B1) * g
    v_new = ADAM_B2 * v + (1.0 - ADAM_B2) * (g * g)
    m_hat = m_new / (1.0 - ADAM_B1 ** ADAM_STEP)
    v_hat = v_new / (1.0 - ADAM_B2 ** ADAM_STEP)
    delta = -ADAM_LR * (m_hat / (jnp.sqrt(v_hat) + ADAM_EPS) + ADAM_WD * w)
    return delta, m_new, v_new


def _sum_leading(name, x, out_dtype=F32):
    n, rows, cols = x.shape
    tr = PACK_ROWS if rows % PACK_ROWS == 0 else rows

    def body(x_ref, o_ref):
        acc = x_ref[0].astype(F32)
        for d in range(1, n):
            acc = acc + x_ref[d].astype(F32)
        o_ref[...] = acc.astype(out_dtype)

    return _pallas(body, name=name, grid=(rows // tr,),
                   in_specs=[pl.BlockSpec((n, tr, cols), lambda i: (0, i, 0))],
                   out_specs=pl.BlockSpec((tr, cols), lambda i: (i, 0)),
                   out_shape=jax.ShapeDtypeStruct((rows, cols), out_dtype),
                   compiler_params=_cp(("arbitrary",), VMEM_MID))(x)


def _adamw(name, w, m, v, g=None, parts=None):
    rows, cols = w.shape
    tr = rows if rows <= 256 else 256

    def body(*refs):
        w_ref, m_ref, v_ref, g_in, g_ref, d_ref, mo_ref, vo_ref = refs
        if parts is None:
            gv = g_in[...]
        else:
            acc = g_in[0].astype(F32)
            for d in range(1, parts.shape[0]):
                acc = acc + g_in[d].astype(F32)
            gv = acc[:, :cols]
        delta, m_new, v_new = _adam_math(w_ref[...], gv, m_ref[...], v_ref[...])
        g_ref[...] = gv
        d_ref[...] = delta
        mo_ref[...] = m_new
        vo_ref[...] = v_new

    row = pl.BlockSpec((tr, cols), lambda i: (i, 0))
    if parts is None:
        g_spec, g_arg = row, g
    else:
        g_spec, g_arg = pl.BlockSpec((parts.shape[0], tr, parts.shape[2]), lambda i: (0, i, 0)), parts
    return _pallas(body, name=name, grid=(rows // tr,), in_specs=[row, row, row, g_spec], out_specs=[row] * 4,
                   out_shape=[jax.ShapeDtypeStruct((rows, cols), F32)] * 4,
                   compiler_params=_cp(("arbitrary",), VMEM_MID))(w, m, v, g_arg)


def _adamw_many(name, groups):
    ntens = len(groups)

    def body(*refs):
        ins, outs = refs[:4 * ntens], refs[4 * ntens:]
        for k in range(ntens):
            w_ref, m_ref, v_ref, g_ref = ins[4 * k:4 * k + 4]
            gv = g_ref[...]
            delta, m_new, v_new = _adam_math(w_ref[...], gv, m_ref[...], v_ref[...])
            for o_ref, val in zip(outs[4 * k:4 * k + 4], (gv, delta, m_new, v_new)):
                o_ref[...] = val

    flat = [a for grp in groups for a in grp]
    out_shape = [jax.ShapeDtypeStruct(grp[0].shape, F32) for grp in groups for _ in range(4)]
    outs = _pallas(body, name=name, out_shape=out_shape, compiler_params=_cp(vmem=VMEM_MID))(*flat)
    return [tuple(outs[4 * k:4 * k + 4]) for k in range(ntens)]


def _pack_rows(arrs):
    rows, meta, total = [], [], 0
    for a in arrs:
        flat = a.reshape(-1)
        nrow = -(-flat.shape[0] // 1024) * 8
        rows.append(jnp.pad(flat, (0, nrow * 128 - flat.shape[0])).reshape(nrow, 128))
        meta.append((a.shape, flat.shape[0], nrow))
        total += nrow
    tail = -total % PACK_ROWS
    if tail:
        rows.append(jnp.zeros((tail, 128), F32))
    return jnp.concatenate(rows, axis=0), meta


def _unpack_rows(packed, meta):
    out, r0 = [], 0
    for shape, size, nrow in meta:
        out.append(packed[r0:r0 + nrow].reshape(-1)[:size].reshape(shape))
        r0 += nrow
    return out


WEIGHTS = ["rel_bias", "norm_g", "ada_w", "ada_b", "attn_w_in", "attn_sinks", "attn_b_f", "attn_w_out", "lru_w_in",
           "lru_conv_w", "lru_conv_b", "lru_w_a", "lru_b_a", "lru_w_x", "lru_b_x", "lru_lambda", "lru_w_out", "final_g"]
BIG = ["ada_w", "attn_w_in", "attn_w_out", "lru_w_in", "lru_w_out"]
PACK_ROWS = 256


def kernel(x, c, rel_bias, norm_g, ada_w, ada_b, attn_w_in, attn_sinks, attn_b_f, attn_w_out, lru_w_in, lru_conv_w, lru_conv_b, lru_w_a, lru_b_a, lru_w_x, lru_b_x, lru_lambda, lru_w_out, final_g, loss_target, m_rel_bias, m_norm_g, m_ada_w, m_ada_b, m_attn_w_in, m_attn_sinks, m_attn_b_f, m_attn_w_out, m_lru_w_in, m_lru_conv_w, m_lru_conv_b, m_lru_w_a, m_lru_b_a, m_lru_w_x, m_lru_b_x, m_lru_lambda, m_lru_w_out, m_final_g, v_rel_bias, v_norm_g, v_ada_w, v_ada_b, v_attn_w_in, v_attn_sinks, v_attn_b_f, v_attn_w_out, v_lru_w_in, v_lru_conv_w, v_lru_conv_b, v_lru_w_a, v_lru_b_a, v_lru_w_x, v_lru_b_x, v_lru_lambda, v_lru_w_out, v_final_g):
    nseq, seq, _ = x.shape
    t_tok = nseq * seq
    me = 4 * lax.axis_index("x") + 2 * lax.axis_index("y") + lax.axis_index("c")
    x0 = x.reshape(t_tok, D_MODEL)
    target = loss_target.reshape(t_tok, D_MODEL)

    w_in_pad = jnp.pad(attn_w_in[0].astype(BF16), ((0, 0), (0, SHARD_W_PAD - SHARD_W_IN)))
    vec_shard = jnp.concatenate([lru_conv_w[0], lru_conv_b, lru_b_a, lru_b_x, lru_lambda], axis=0)
    g_w_in, g_vec, g_c = _exchange("gather_first", [w_in_pad, vec_shard, c], [])
    later_w = [attn_w_out[0].astype(BF16), lru_w_in[0].astype(BF16), lru_w_out[0].astype(BF16)]
    later_handle, later_token = _exchange_start("gather_later_start", later_w, [], after=g_vec)
    w_full = jnp.transpose(g_w_in[:, :, :SHARD_W_IN], (1, 0, 2)).reshape(D_MODEL, N_DEV * SHARD_W_IN)
    w_aq, w_ak, w_av = w_full[:, 0:512], w_full[:, 512:640], w_full[:, 640:768]
    w_bq, w_bk, w_bv = w_full[:, 768:1280], w_full[:, 1280:1792], w_full[:, 1792:2304]
    w_f, w_gate = w_full[:, 2304:2312], w_full[:, 2312:3336]
    w_main = jnp.concatenate([w_bq, w_bk, w_bv, w_aq, w_gate, w_ak, w_av], axis=1)
    wf_t = jnp.transpose(w_f)
    vec_full = jnp.transpose(g_vec, (1, 0, 2)).reshape(8, D_MODEL)
    conv_w, conv_b, b_a, b_x, lam = vec_full[0:4], vec_full[4:5], vec_full[5:6], vec_full[6:7], vec_full[7:8]
    c_all = g_c.reshape(N_DEV * nseq, D_MODEL)

    ncol = ada_w.shape[2]
    ada_b_slice = lax.dynamic_slice(ada_b.reshape(2, N_DEV, ncol), (0, me, 0), (2, 1, ncol))
    mod_part = _ada_mod(c_all, ada_w, ada_b_slice)
    (g_mod,) = _exchange("gather_mod", [mod_part], [])
    mine = lax.dynamic_slice(g_mod, (0, 0, me * nseq, 0), (N_DEV, 2, nseq, ncol))
    mod = jnp.transpose(mine, (1, 2, 0, 3)).reshape(2, nseq, 3 * D_MODEL)
    shift = [mod[l, :, 0:D_MODEL].reshape(nseq, 1, D_MODEL) for l in range(2)]
    scale = [mod[l, :, D_MODEL:2 * D_MODEL].reshape(nseq, 1, D_MODEL) for l in range(2)]
    gmod = [mod[l, :, 2 * D_MODEL:].reshape(nseq, 1, D_MODEL) for l in range(2)]

    onehot = _bucket_onehot()
    bias = _bias_expand(jnp.transpose(rel_bias), onehot).reshape(N_HEADS, BLOCK, 2 * BLOCK)
    sinks = attn_sinks.reshape(N_HEADS)
    b_f = attn_b_f.reshape(N_HEADS, 1)
    norm_g0 = norm_g[0:1] + later_token[0:1, 0:1]
    h0, qkvg, fl_t = _norm_proj("norm_proj0", x0, norm_g0, shift[0], scale[0], w_main, seq, BF16, wf_t=wf_t)
    f_row, f_col = _fox_prep(fl_t, b_f, seq)
    a_out, lse_a = _swa_fwd(qkvg, bias, sinks, seq)
    q_aug, k_aug, kt_aug, vt = _fox_aug(qkvg, f_col, seq)
    b_out, lse_b = _fox_fwd_t(q_aug, k_aug, vt, seq)
    g_later = _exchange_wait("gather_later_wait", later_handle, after=lse_b)
    w_out0, g_lru_in, w_out1 = (_with_own(g, w, me) for g, w in zip(g_later, later_w))
    w_out0, w_out1 = w_out0.reshape(D_MODEL, D_MODEL), w_out1.reshape(D_MODEL, D_MODEL)
    w_out0_t, w_out1_t, w_main_t = jnp.transpose(w_out0), jnp.transpose(w_out1), jnp.transpose(w_main)
    lru_in_t = jnp.transpose(g_lru_in, (0, 2, 1)).reshape(2 * D_MODEL, D_MODEL)
    yg0, y0, x1 = _out_proj("out_proj0", [a_out, b_out], qkvg, C_GATE // D_MODEL, w_out0, x0, gmod[0], seq)

    h1, proj1 = _norm_proj("norm_proj1", x1, norm_g[1:2], shift[1], scale[1], g_lru_in, seq, F32)
    hs = _lru_fwd(proj1, conv_w, conv_b, lru_w_a[0], b_a, lru_w_x[0], b_x, lam, seq)

    yg1, dx2, dy1, dyh, dgm1, loss_rows, dfinal_rows = _last_layer_tail(
        hs, proj1, w_out1, w_out1_t, x1, gmod[1], final_g.reshape(1, D_MODEL), target, seq)

    dproj1, dcw, dvec, dw_a, dw_x = _lru_bwd(proj1, hs, dyh, conv_w, conv_b, lru_w_a[0], b_a, lru_w_x[0], b_x, lam, seq)
    dx1, dss1, dg1 = _norm_bwd("norm1_bwd", [(dproj1, 0)], lru_in_t, x1, norm_g[1:2], scale[1], dx2, seq)
    (p_w_out1,) = _dw("dw_out1", yg1, [dy1])
    (p_lru_in,) = _dw("dw_lru_in", h1, [dproj1], blocked=2 * D_MODEL // N_DEV)

    rows_out = D_MODEL // N_DEV
    gpack1, gmeta1 = _pack_rows([dcw[0:4], dvec[0:4], dg1[0], dfinal_rows[0]])
    dwax = jnp.stack([dw_a, dw_x]).astype(BF16)
    own1 = [gpack1, dwax, p_lru_in, p_w_out1.reshape(N_DEV, rows_out, D_MODEL)]
    grads1_handle, grads1_token = _exchange_start("grads1_start", own1[:2], own1[2:], after=dx1)

    gmod0 = gmod[0] + grads1_token[0:1, 0:1]
    dy0, dgm0, du_a, du_b, dgate = _out_proj_bwd("out_proj0_bwd", dx1, gmod0, y0, w_out0_t, seq,
                                                  attn=(a_out, b_out, qkvg))
    dq_a, dkv_a, dbias, dsink = _swa_bwd(qkvg, du_a, a_out, lse_a, bias, sinks, seq)
    dq_b, dk_b, dv_b, df4 = _fox_bwd_t(q_aug, k_aug, kt_aug, qkvg, du_b, b_out, lse_b, seq)
    dfl_t, db_f = _fox_post(df4.reshape(N_HEADS, t_tok), fl_t, b_f, seq)
    parts0 = [(dq_b, C_BQ), (dk_b, C_BK), (dv_b, C_BV), (dq_a, C_AQ), (dgate, C_GATE), (dkv_a, C_AK)]
    (p_w_out0,) = _dw("dw_out0", yg0, [dy0])
    pw_bq, pw_bk, pw_bv, pw_aq, pw_gate, pw_akv = _dw("dw_attn_in", h0, [p for p, _ in parts0])
    pw_f = _dw_rows("dw_f", dfl_t, h0)

    p_w_in = jnp.concatenate([pw_aq, pw_akv, pw_bq, pw_bk, pw_bv, jnp.transpose(pw_f).astype(BF16), pw_gate], axis=1)
    p_w_in = jnp.transpose(p_w_in.reshape(D_MODEL, N_DEV, SHARD_W_IN), (1, 0, 2))
    p_w_in = jnp.pad(p_w_in, ((0, 0), (0, 0), (0, SHARD_W_PAD - SHARD_W_IN)))
    own0 = [p_w_in, p_w_out0.reshape(N_DEV, rows_out, D_MODEL)]
    landed1 = _exchange_wait("grads1_wait", grads1_handle, after=p_w_in)
    grads0_handle, grads0_token = _exchange_start("grads0_start", [], own0, after=landed1[0])
    scale0 = scale[0] + grads0_token[0:1, 0:1]
    dx0, dss0, dg0 = _norm_bwd("norm0_bwd", parts0, w_main_t, x0, norm_g[0:1], scale0, dx1, seq,
                               rows_part=(dfl_t, wf_t))
    dbias_t = _bias_reduce(dbias.reshape(N_HEADS, BLOCK * 2 * BLOCK), onehot)

    gpack0, gmeta0 = _pack_rows([jnp.transpose(dbias_t), dg0[0], dsink[:, 0], db_f[:, 0], loss_rows[0]])
    dmod = jnp.stack([jnp.concatenate([dss[:, 0], dss[:, 1], dgm[:, 0]], axis=1)
                      for dss, dgm in ((dss0, dgm0), (dss1, dgm1))], axis=1)
    g_small0, g_dmod = _exchange("exchange_small", [gpack0, dmod], [])
    landed0 = _exchange_wait("grads0_wait", grads0_handle, after=g_small0)
    r_w_in, r_w_out0 = (_with_own(g, lax.dynamic_index_in_dim(a, me, 0, keepdims=False), me)
                        for g, a in zip(landed0, own0))
    g_small1, g_dwax = (_with_own(g, a, me) for g, a in zip(landed1[:2], own1[:2]))
    r_lru_in, r_w_out1 = (_with_own(g, lax.dynamic_index_in_dim(a, me, 0, keepdims=False), me)
                          for g, a in zip(landed1[2:], own1[2:]))

    d_rel, d_g0, d_sinks, d_b_f, loss_cols = _unpack_rows(_sum_leading("sum_small0", g_small0), gmeta0)
    loss = jnp.sum(loss_cols)
    d_cw, d_vec, d_g1, d_final_g = _unpack_rows(_sum_leading("sum_small1", g_small1), gmeta1)
    d_norm_g = jnp.stack([d_g0, d_g1])
    d_wax = _sum_leading("sum_dwax", g_dwax.reshape(N_DEV, 2 * LRU_BLOCKS * LRU_BLOCK_W, LRU_BLOCK_W))
    d_wa, d_wx = d_wax[:LRU_BLOCKS * LRU_BLOCK_W], d_wax[LRU_BLOCKS * LRU_BLOCK_W:]
    cols = lambda a: lax.dynamic_slice(a, (0, me * LRU_BLOCK_W), (a.shape[0], LRU_BLOCK_W))
    dmod_all = g_dmod.reshape(N_DEV * nseq, 2 * 3 * D_MODEL)
    d_ada_b = _sum_leading("sum_ada_b", dmod_all.reshape(N_DEV * nseq, 2 * 3 * D_MODEL // 128, 128)).reshape(2, 3 * D_MODEL)
    dmod_slice = lax.dynamic_slice(dmod_all.reshape(N_DEV * nseq, 2, N_DEV, ncol), (0, 0, me, 0),
                                   (N_DEV * nseq, 2, 1, ncol)).reshape(N_DEV * nseq, 2, ncol)
    d_ada_w = _ada_w_grad(c_all, jnp.transpose(dmod_slice, (1, 0, 2)))

    given = dict(
        rel_bias=(rel_bias, m_rel_bias, v_rel_bias), norm_g=(norm_g, m_norm_g, v_norm_g),
        ada_w=(ada_w, m_ada_w, v_ada_w), ada_b=(ada_b, m_ada_b, v_ada_b),
        attn_w_in=(attn_w_in, m_attn_w_in, v_attn_w_in), attn_sinks=(attn_sinks, m_attn_sinks, v_attn_sinks),
        attn_b_f=(attn_b_f, m_attn_b_f, v_attn_b_f), attn_w_out=(attn_w_out, m_attn_w_out, v_attn_w_out),
        lru_w_in=(lru_w_in, m_lru_w_in, v_lru_w_in), lru_conv_w=(lru_conv_w, m_lru_conv_w, v_lru_conv_w),
        lru_conv_b=(lru_conv_b, m_lru_conv_b, v_lru_conv_b), lru_w_a=(lru_w_a, m_lru_w_a, v_lru_w_a),
        lru_b_a=(lru_b_a, m_lru_b_a, v_lru_b_a), lru_w_x=(lru_w_x, m_lru_w_x, v_lru_w_x),
        lru_b_x=(lru_b_x, m_lru_b_x, v_lru_b_x), lru_lambda=(lru_lambda, m_lru_lambda, v_lru_lambda),
        lru_w_out=(lru_w_out, m_lru_w_out, v_lru_w_out), final_g=(final_g, m_final_g, v_final_g))
    results = {}

    def big(name, shape2d, g=None, parts=None):
        w, m, v = (a.reshape(shape2d) for a in given[name])
        outs = _adamw("adamw_" + name, w, m, v, g=g, parts=parts)
        results[name] = tuple(o.reshape(given[name][0].shape) for o in outs)

    big("ada_w", (2 * D_MODEL, ncol), g=d_ada_w.reshape(2 * D_MODEL, ncol))
    big("attn_w_in", (D_MODEL, SHARD_W_IN), parts=r_w_in)
    big("attn_w_out", (rows_out, D_MODEL), parts=r_w_out0)
    big("lru_w_in", (D_MODEL, 2 * D_MODEL // N_DEV), parts=r_lru_in)
    big("lru_w_out", (rows_out, D_MODEL), parts=r_w_out1)

    small_grads = dict(
        rel_bias=d_rel, norm_g=d_norm_g, ada_b=d_ada_b, attn_sinks=d_sinks.reshape(1, N_HEADS),
        attn_b_f=d_b_f.reshape(1, N_HEADS), lru_conv_w=cols(d_cw).reshape(1, 4, LRU_BLOCK_W),
        lru_conv_b=cols(d_vec[0:1]), lru_w_a=d_wa.reshape(lru_w_a.shape), lru_b_a=cols(d_vec[1:2]),
        lru_w_x=d_wx.reshape(lru_w_x.shape), lru_b_x=cols(d_vec[2:3]), lru_lambda=cols(d_vec[3:4]),
        final_g=d_final_g)
    small = [n for n in WEIGHTS if n not in BIG]
    as2d = lambda a: a.reshape(-1, a.shape[-1])
    outs = _adamw_many("adamw_small", [tuple(as2d(a) for a in given[n]) + (as2d(small_grads[n]),) for n in small])
    for n, group in zip(small, outs):
        results[n] = tuple(o.reshape(given[n][0].shape) for o in group)

    grad_x = dx0.reshape(x.shape)
    out = [loss, grad_x]
    for j in range(4):
        out += [results[n][j] for n in WEIGHTS]
    return tuple(out)
```

```python
import functools
import math

import jax
import jax.numpy as jnp
from jax import lax
from jax.experimental import pallas as pl
from jax.experimental.pallas import tpu as pltpu

F32 = jnp.float32
BF16 = jnp.bfloat16
HI = lax.Precision.HIGHEST
MESH = pl.DeviceIdType.MESH

N_DEV = 8
D_MODEL = 1024
HEAD_DIM = 64
N_HEADS = 8
KV_GROUP = 4
BLOCK = 128
REL_BUCKETS = 32
REL_MAX_EXACT = 16
REL_MAX_DIST = 128
LRU_BLOCKS = 8
LRU_BLOCK_W = 128
LRU_C = 8.0
EPS = 1e-6
SCALE = HEAD_DIM ** -0.5
NEG = -1e30

ADAM_LR = 0.001
ADAM_B1 = 0.9
ADAM_B2 = 0.999
ADAM_EPS = 1e-08
ADAM_WD = 0.01
ADAM_STEP = 10

C_BQ, C_BK, C_BV, C_AQ, C_GATE, C_AK, C_AV = 0, 512, 1024, 1536, 2048, 3072, 3200
N_MAIN = 3328
SHARD_W_IN = 417
SHARD_W_PAD = 512

TM = 512
TQ = 256
TK = 128
TKB = 256
TC = 512
SWA_SUB = 2
VMEM_BIG = 56 * 1024 * 1024
VMEM_MID = 40 * 1024 * 1024


def _pallas(body, **kw):
    return pl.pallas_call(body, **kw)


def _cp(sem=None, vmem=None):
    kw = {}
    if sem is not None:
        kw["dimension_semantics"] = sem
    if vmem is not None:
        kw["vmem_limit_bytes"] = vmem
    return pltpu.CompilerParams(**kw)


def _nn(a, b, precision=None):
    return jnp.dot(a, b, preferred_element_type=F32, precision=precision)


def _nt(a, b, precision=None):
    return lax.dot_general(a, b, (((1,), (1,)), ((), ())), preferred_element_type=F32, precision=precision)


def _tn(a, b, precision=None):
    return lax.dot_general(a, b, (((0,), (0,)), ((), ())), preferred_element_type=F32, precision=precision)


def _sigmoid(x):
    return 1.0 / (1.0 + jnp.exp(-x))


def _silu(x):
    return x * _sigmoid(x)


def _dsilu(x):
    s = _sigmoid(x)
    return s * (1.0 + x * (1.0 - s))


def _neg_expm1(x):
    poly = x * (1.0 + x * (0.5 + x * (1.0 / 6.0 + x * (1.0 / 24.0))))
    return -jnp.where(jnp.abs(x) < 0.05, poly, jnp.exp(x) - 1.0)


def _col(tile, idx):
    lane = lax.broadcasted_iota(jnp.int32, tile.shape, 1)
    return jnp.sum(jnp.where(lane == idx, tile, 0.0), axis=1, keepdims=True)


def _row(tile, idx):
    sub = lax.broadcasted_iota(jnp.int32, tile.shape, 0)
    return jnp.sum(jnp.where(sub == idx, tile, 0.0), axis=0, keepdims=True)


def _exchange(name, gathers, scatters, axes=("x", "y", "c"), chunks=1):
    ng, n = len(gathers), len(gathers) + len(scatters)
    ins = list(gathers) + list(scatters)
    group = 2 ** len(axes)

    def body(*refs):
        in_refs, out_refs = refs[:n], refs[n:2 * n]
        send_sems, recv_sems, loc_sems = refs[2 * n:]
        coord = {a: lax.axis_index(a) for a in ("x", "y", "c")}

        def member(r):
            pc = dict(coord)
            idx = 0
            for k, a in enumerate(axes):
                if r & (1 << (len(axes) - 1 - k)):
                    pc[a] = 1 - coord[a]
                idx = 2 * idx + pc[a]
            return (pc["x"], pc["y"], pc["c"]), idx

        _, me = member(0)

        def peer(r):
            return member(r)

        local, sends, recvs = [], [], []
        for k in range(n):
            mine = in_refs[k] if k < ng else in_refs[k].at[me]
            cp = pltpu.make_async_copy(mine, out_refs[k].at[me], loc_sems.at[k])
            cp.start()
            local.append(cp)
            lead = mine.shape[0]
            nchunk = max(q for q in range(1, chunks + 1) if lead % q == 0)
            step = lead // nchunk
            for r in range(1, group):
                pid, pidx = peer(r)
                src = in_refs[k] if k < ng else in_refs[k].at[pidx]
                for q in range(nchunk):
                    rows = pl.ds(q * step, step)
                    sems = dict(send_sem=send_sems.at[r - 1, k, q], recv_sem=recv_sems.at[r - 1, k, q],
                                device_id=pid, device_id_type=MESH)
                    snd = pltpu.make_async_remote_copy(src_ref=src.at[rows], dst_ref=out_refs[k].at[me].at[rows], **sems)
                    snd.start()
                    sends.append(snd)
                    recvs.append(pltpu.make_async_remote_copy(
                        src_ref=src.at[rows], dst_ref=out_refs[k].at[pidx].at[rows], **sems))
        for rc in recvs:
            rc.wait_recv()
        for snd in sends:
            snd.wait_send()
        for cp in local:
            cp.wait()

    out_shape = [jax.ShapeDtypeStruct((group,) + a.shape, a.dtype) for a in gathers]
    out_shape += [jax.ShapeDtypeStruct(a.shape, a.dtype) for a in scatters]
    any_spec = pl.BlockSpec(memory_space=pl.ANY)
    return _pallas(
        body, name=name, out_shape=out_shape,
        in_specs=[any_spec] * n, out_specs=[any_spec] * n,
        scratch_shapes=[pltpu.SemaphoreType.DMA((group - 1, n, chunks)), pltpu.SemaphoreType.DMA((group - 1, n, chunks)),
                        pltpu.SemaphoreType.DMA((n,))],
    )(*ins)


def _peer_of(r):
    x, y, c = lax.axis_index("x"), lax.axis_index("y"), lax.axis_index("c")
    px = 1 - x if r & 4 else x
    py = 1 - y if r & 2 else y
    pc = 1 - c if r & 1 else c
    return (px, py, pc), 4 * px + 2 * py + pc


def _split_copies(in_refs, land_refs, send_sems, recv_sems, ng, with_recv):
    _, me = _peer_of(0)
    pairs = []
    for k, (src_ref, land) in enumerate(zip(in_refs, land_refs)):
        for r in range(1, N_DEV):
            pid, pidx = _peer_of(r)
            src = src_ref if k < ng else src_ref.at[pidx]
            slot = (N_DEV - 1) * k + r - 1
            sems = dict(send_sem=send_sems.at[slot], recv_sem=recv_sems.at[slot], device_id=pid, device_id_type=MESH)
            send = pltpu.make_async_remote_copy(src_ref=src, dst_ref=land.at[me], **sems)
            recv = pltpu.make_async_remote_copy(src_ref=src, dst_ref=land.at[pidx], **sems) if with_recv else None
            pairs.append((send, recv))
    return pairs


def _exchange_start(name, gathers, scatters, after):
    ng, n = len(gathers), len(gathers) + len(scatters)
    ins = list(gathers) + list(scatters)
    lands = [jax.ShapeDtypeStruct((N_DEV,) + a.shape, a.dtype) for a in gathers]
    lands += [jax.ShapeDtypeStruct(a.shape, a.dtype) for a in scatters]

    def body(*refs):
        in_refs, land_refs = refs[:n], refs[n:2 * n]
        send_sems, recv_sems = refs[2 * n + 1:2 * n + 3]
        token = refs[-1]
        for send, _ in _split_copies(in_refs, land_refs, send_sems, recv_sems, ng, False):
            send.start()
        token[...] = jnp.zeros_like(token)

    hbm = pl.BlockSpec(memory_space=pltpu.HBM)
    sem = pl.BlockSpec(memory_space=pltpu.SEMAPHORE)
    sem_shape = pltpu.SemaphoreType.DMA(((N_DEV - 1) * n,))
    out_shape = [sem_shape, sem_shape] + [pltpu.HBM(a.shape, a.dtype) for a in ins]
    out_shape += [pltpu.HBM(l.shape, l.dtype) for l in lands] + [jax.ShapeDtypeStruct((8, 128), F32)]
    args = [pltpu.with_memory_space_constraint(a, pltpu.HBM) for a in ins]
    args += [pltpu.with_memory_space_constraint(lax.empty(l.shape, l.dtype), pltpu.HBM) for l in lands]
    outs = _pallas(
        body, name=name, out_shape=out_shape,
        in_specs=[hbm] * (2 * n) + [pl.BlockSpec(memory_space=pl.ANY)],
        out_specs=[sem, sem] + [hbm] * (2 * n) + [pl.BlockSpec(memory_space=pltpu.VMEM)],
        input_output_aliases={i: 2 + i for i in range(2 * n)},
        compiler_params=pltpu.CompilerParams(has_side_effects=pltpu.SideEffectType.DATAFLOW_SIDE_EFFECTING),
    )(*args, after)
    return (outs[0], outs[1], list(outs[2:2 + n]), list(outs[2 + n:2 + 2 * n]), ng), outs[-1]


def _exchange_wait(name, handle, after):
    send_sems, recv_sems, srcs, lands, ng = handle
    n = len(srcs)

    def body(*refs):
        in_refs, land_refs = refs[:n], refs[n:2 * n]
        send_ref, recv_ref = refs[2 * n:2 * n + 2]
        for send, recv in _split_copies(in_refs, land_refs, send_ref, recv_ref, ng, True):
            send.wait_send()
            recv.wait_recv()

    hbm = pl.BlockSpec(memory_space=pltpu.HBM)
    sem = pl.BlockSpec(memory_space=pltpu.SEMAPHORE)
    outs = _pallas(
        body, name=name, out_shape=[pltpu.HBM(a.shape, a.dtype) for a in srcs + lands],
        in_specs=[hbm] * (2 * n) + [sem, sem, pl.BlockSpec(memory_space=pl.ANY)],
        out_specs=[hbm] * (2 * n), input_output_aliases={i: i for i in range(2 * n)},
        compiler_params=pltpu.CompilerParams(has_side_effects=pltpu.SideEffectType.DATAFLOW_SIDE_EFFECTING),
    )(*srcs, *lands, send_sems, recv_sems, after)
    return list(outs[n:])


def _with_own(land, own, me):
    return lax.dynamic_update_slice(land, own[None], (me,) + (0,) * own.ndim)


def _ada_mod(c_all, ada_w, ada_b_slice):
    def body(c_ref, w_ref, b_ref, o_ref):
        ca = _silu(c_ref[...])
        for l in range(2):
            o_ref[l] = _nn(ca, w_ref[l], HI) + b_ref[l]

    return _pallas(body, name="ada_mod",
                   out_shape=jax.ShapeDtypeStruct((2, c_all.shape[0], ada_w.shape[2]), F32),
                   compiler_params=_cp(vmem=VMEM_MID))(c_all, ada_w, ada_b_slice)


def _ada_w_grad(c_all, dmod_slice):
    def body(c_ref, d_ref, o_ref):
        ca = _silu(c_ref[...])
        for l in range(2):
            o_ref[l] = _tn(ca, d_ref[l], HI)

    return _pallas(body, name="ada_w_grad",
                   out_shape=jax.ShapeDtypeStruct((2, D_MODEL, dmod_slice.shape[2]), F32),
                   compiler_params=_cp(vmem=VMEM_MID))(c_all, dmod_slice)


def _bucket_onehot():
    qi = jnp.arange(BLOCK)[:, None]
    kj = jnp.arange(2 * BLOCK)[None, :]
    rel = qi - kj + BLOCK
    n = jnp.maximum(rel, 0)
    nf = jnp.maximum(n, 1).astype(F32)
    large = REL_MAX_EXACT + (jnp.log(nf / REL_MAX_EXACT) / math.log(REL_MAX_DIST / REL_MAX_EXACT)
                             * (REL_BUCKETS - REL_MAX_EXACT)).astype(jnp.int32)
    large = jnp.minimum(large, REL_BUCKETS - 1)
    bucket = jnp.where(n < REL_MAX_EXACT, n, large).reshape(1, BLOCK * 2 * BLOCK)
    return (jnp.arange(REL_BUCKETS)[:, None] == bucket).astype(F32)


def _bias_expand(rel_bias_t, onehot):
    def body(r_ref, e_ref, o_ref):
        o_ref[...] = _nn(r_ref[...], e_ref[...], HI)

    return _pallas(body, name="bias_expand",
                   out_shape=jax.ShapeDtypeStruct((N_HEADS, onehot.shape[1]), F32),
                   compiler_params=_cp(vmem=VMEM_MID))(rel_bias_t, onehot)


def _bias_reduce(dbias, onehot):
    def body(d_ref, e_ref, o_ref):
        o_ref[...] = _nt(d_ref[...], e_ref[...], HI)

    return _pallas(body, name="bias_reduce",
                   out_shape=jax.ShapeDtypeStruct((N_HEADS, REL_BUCKETS), F32),
                   compiler_params=_cp(vmem=VMEM_MID))(dbias, onehot)


def _norm_proj(name, x, g, shift, scale, w, seq, out_dtype, wf_t=None):
    t_tok = x.shape[0]
    w3d = w.ndim == 3
    n_out = w.shape[0] * w.shape[2] if w3d else w.shape[1]
    cn = w.shape[2] if w3d else 256

    def body(x_ref, g_ref, sh_ref, sc_ref, w_ref, *rest):
        if wf_t is not None:
            wf_ref, h_ref, o_ref, fl_ref = rest
        else:
            h_ref, o_ref = rest
        xv = x_ref[...]
        rstd = lax.rsqrt(jnp.mean(xv * xv, axis=-1, keepdims=True) + EPS)
        h = (xv * rstd) * g_ref[...] * (1.0 + sc_ref[...]) + sh_ref[...]
        hb = h.astype(BF16)
        h_ref[...] = hb
        for j in range(n_out // cn):
            wj = w_ref[j] if w3d else w_ref[:, j * cn:(j + 1) * cn]
            o_ref[:, j * cn:(j + 1) * cn] = _nn(hb, wj).astype(out_dtype)
        if wf_t is not None:
            fl_ref[...] = _nt(wf_ref[...], hb)

    mod_spec = pl.BlockSpec((None, 1, D_MODEL), lambda i: (i * TM // seq, 0, 0))
    w_spec = (pl.BlockSpec(w.shape, lambda i: (0, 0, 0)) if w3d else pl.BlockSpec(w.shape, lambda i: (0, 0)))
    in_specs = [pl.BlockSpec((TM, D_MODEL), lambda i: (i, 0)), pl.BlockSpec((1, D_MODEL), lambda i: (0, 0)),
                mod_spec, mod_spec, w_spec]
    out_shape = [jax.ShapeDtypeStruct((t_tok, D_MODEL), BF16), jax.ShapeDtypeStruct((t_tok, n_out), out_dtype)]
    out_specs = [pl.BlockSpec((TM, D_MODEL), lambda i: (i, 0)), pl.BlockSpec((TM, n_out), lambda i: (i, 0))]
    args = [x, g, shift, scale, w]
    if wf_t is not None:
        in_specs.append(pl.BlockSpec(wf_t.shape, lambda i: (0, 0)))
        out_shape.append(jax.ShapeDtypeStruct((wf_t.shape[0], t_tok), F32))
        out_specs.append(pl.BlockSpec((wf_t.shape[0], TM), lambda i: (0, i)))
        args.append(wf_t)
    return _pallas(body, name=name, grid=(t_tok // TM,), in_specs=in_specs, out_specs=out_specs,
                   out_shape=out_shape, compiler_params=_cp(("arbitrary",), VMEM_BIG))(*args)


def _fox_prep(fl_t, b_f, seq):
    t_tok = fl_t.shape[1]
    ch = 256

    def body(fl_ref, bf_ref, fr_ref, fc_ref):
        z = fl_ref[...] + bf_ref[...]
        logf = jnp.minimum(z, 0.0) - jnp.log(1.0 + jnp.exp(-jnp.abs(z)))
        ri = lax.broadcasted_iota(jnp.int32, (ch, ch), 0)
        ci = lax.broadcasted_iota(jnp.int32, (ch, ch), 1)
        upper = (ri <= ci).astype(F32)
        eye = (ri == ci).astype(F32)
        carry = jnp.zeros((N_HEADS, 1), F32)
        for k in range(seq // ch):
            fk = _nn(logf[:, k * ch:(k + 1) * ch], upper, HI) + carry
            carry = fk[:, ch - 1:ch]
            fr_ref[:, k * ch:(k + 1) * ch] = fk
            padded = jnp.concatenate([fk, jnp.zeros((128 - N_HEADS, ch), F32)], axis=0)
            fc_ref[k * ch:(k + 1) * ch, :] = _nt(eye, padded, HI)

    return _pallas(
        body, name="fox_prep", grid=(t_tok // seq,),
        in_specs=[pl.BlockSpec((N_HEADS, seq), lambda b: (0, b)), pl.BlockSpec((N_HEADS, 1), lambda b: (0, 0))],
        out_specs=[pl.BlockSpec((N_HEADS, seq), lambda b: (0, b)), pl.BlockSpec((seq, 128), lambda b: (b, 0))],
        out_shape=[jax.ShapeDtypeStruct((N_HEADS, t_tok), F32), jax.ShapeDtypeStruct((t_tok, 128), F32)],
        compiler_params=_cp(("arbitrary",), VMEM_MID))(fl_t, b_f)


def _fox_post(df_row, fl_t, b_f, seq):
    t_tok = fl_t.shape[1]
    ch = 256

    def body(d_ref, fl_ref, bf_ref, o_ref, db_ref):
        @pl.when(pl.program_id(0) == 0)
        def _():
            db_ref[...] = jnp.zeros_like(db_ref)

        z = fl_ref[...] + bf_ref[...]
        sig_neg = 1.0 / (1.0 + jnp.exp(z))
        ri = lax.broadcasted_iota(jnp.int32, (ch, ch), 0)
        ci = lax.broadcasted_iota(jnp.int32, (ch, ch), 1)
        lower = (ri >= ci).astype(F32)
        carry = jnp.zeros((N_HEADS, 1), F32)
        tot = jnp.zeros((N_HEADS, 1), F32)
        for k in reversed(range(seq // ch)):
            dk = _nn(d_ref[:, k * ch:(k + 1) * ch], lower, HI) + carry
            carry = dk[:, 0:1]
            dfl = dk * sig_neg[:, k * ch:(k + 1) * ch]
            o_ref[:, k * ch:(k + 1) * ch] = dfl
            tot = tot + jnp.sum(dfl, axis=1, keepdims=True)
        db_ref[...] += jnp.broadcast_to(tot, db_ref.shape)

    return _pallas(
        body, name="fox_post", grid=(t_tok // seq,),
        in_specs=[pl.BlockSpec((N_HEADS, seq), lambda b: (0, b)), pl.BlockSpec((N_HEADS, seq), lambda b: (0, b)),
                  pl.BlockSpec((N_HEADS, 1), lambda b: (0, 0))],
        out_specs=[pl.BlockSpec((N_HEADS, seq), lambda b: (0, b)), pl.BlockSpec((N_HEADS, 128), lambda b: (0, 0))],
        out_shape=[jax.ShapeDtypeStruct((N_HEADS, t_tok), F32), jax.ShapeDtypeStruct((N_HEADS, 128), F32)],
        compiler_params=_cp(("arbitrary",), VMEM_MID))(df_row, fl_t, b_f)


def _eye(n, dtype):
    return (lax.broadcasted_iota(jnp.int32, (n, n), 0) == lax.broadcasted_iota(jnp.int32, (n, n), 1)).astype(dtype)


def _fox_aug(qkvg, f_col, seq):
    t_tok = qkvg.shape[0]
    ta = 256
    nkb = ta // TK

    def body(q_ref, k_ref, v_ref, fc_ref, qa_ref, ka_ref, kt_ref, vt_ref):
        ri = lax.broadcasted_iota(jnp.int32, (128, 128), 0)
        ci = lax.broadcasted_iota(jnp.int32, (128, 128), 1)
        eye = (ri == ci).astype(BF16)
        lane = lax.broadcasted_iota(jnp.int32, (ta, 128), 1)
        ones_q = jnp.where(jnp.logical_and(lane >= 64, lane < 67), 1.0, 0.0)
        ones_k = jnp.where(jnp.logical_and(lane >= 67, lane < 70), 1.0, 0.0)
        fc_tile = fc_ref[...]
        for p in range(N_HEADS // 2):
            q2 = q_ref[:, 128 * p:128 * (p + 1)]
            k2 = k_ref[:, 128 * p:128 * (p + 1)]
            vt = _nt(eye, v_ref[:, 128 * p:128 * (p + 1)]).astype(BF16)
            for kk in range(nkb):
                vt_ref[p, kk] = vt[:, kk * TK:(kk + 1) * TK]
            for e in range(2):
                h = 2 * p + e
                sel = jnp.logical_and(ri == ci + HEAD_DIM * e, ci < HEAD_DIM)
                f = _col(fc_tile, h)
                fh = f.astype(BF16).astype(F32)
                fm = (f - fh).astype(BF16).astype(F32)
                fl = (f - fh - fm).astype(BF16).astype(F32)
                qa = (_nn(q2, jnp.where(sel, SCALE, 0.0).astype(BF16)) + ones_q + jnp.where(lane == 67, fh, 0.0)
                      + jnp.where(lane == 68, fm, 0.0) + jnp.where(lane == 69, fl, 0.0))
                ka = (_nn(k2, jnp.where(sel, 1.0, 0.0).astype(BF16)) + ones_k - jnp.where(lane == 64, fh, 0.0)
                      - jnp.where(lane == 65, fm, 0.0) - jnp.where(lane == 66, fl, 0.0))
                qa_ref[h] = qa.astype(BF16)
                kab = ka.astype(BF16)
                ka_ref[h] = kab
                kt = _nt(eye, kab).astype(BF16)
                for kk in range(ta // TKB):
                    kt_ref[h, kk] = kt[:, kk * TKB:(kk + 1) * TKB]

    aug = jax.ShapeDtypeStruct((N_HEADS, t_tok, 128), BF16)
    return _pallas(
        body, name="fox_aug", grid=(t_tok // ta,),
        in_specs=[pl.BlockSpec((ta, 512), lambda i: (i, C_BQ // 512)), pl.BlockSpec((ta, 512), lambda i: (i, C_BK // 512)),
                  pl.BlockSpec((ta, 512), lambda i: (i, C_BV // 512)), pl.BlockSpec((ta, 128), lambda i: (i, 0))],
        out_specs=[pl.BlockSpec((N_HEADS, ta, 128), lambda i: (0, i, 0)), pl.BlockSpec((N_HEADS, ta, 128), lambda i: (0, i, 0)),
                   pl.BlockSpec((N_HEADS, ta // TKB, 128, TKB), lambda i: (0, i, 0, 0)),
                   pl.BlockSpec((N_HEADS // 2, nkb, 128, TK), lambda i: (0, i, 0, 0))],
        out_shape=[aug, aug, jax.ShapeDtypeStruct((N_HEADS, t_tok // TKB, 128, TKB), BF16),
                   jax.ShapeDtypeStruct((N_HEADS // 2, t_tok // TK, 128, TK), BF16)],
        compiler_params=_cp(("arbitrary",), VMEM_MID))(qkvg, qkvg, qkvg, f_col)


def _fox_fwd_t(q_aug, k_aug, vt, seq):
    t_tok = k_aug.shape[1]
    nq = seq // TQ
    ratio = TQ // TK

    def body(qa_ref, ka_ref, vt_ref, o_ref, lse_ref, ml_s, acc_s, st_s, p_s, al_s, qt_s):
        i = pl.program_id(1)
        tpos = i * TQ + lax.broadcasted_iota(jnp.int32, (1, TQ), 1)
        eye = _eye(HEAD_DIM, BF16)
        eye2 = _eye(128, BF16)
        for h in range(N_HEADS):
            qt_s[h] = _nt(eye2, qa_ref[h]).astype(BF16)
            ml_s[0, h] = jnp.full((1, TQ), NEG, F32)
            ml_s[1, h] = jnp.zeros((1, TQ), F32)
            acc_s[h] = jnp.zeros((HEAD_DIM, TQ), F32)
            p_s[1, h] = jnp.zeros((TK, TQ), BF16)
            al_s[1, h] = jnp.ones((1, TQ), F32)

        def scores(j):
            row0 = pl.multiple_of(j * TK, TK)
            for h in range(N_HEADS):
                st_s[j & 1, h] = _nn(ka_ref[h, pl.ds(row0, TK), :], qt_s[h])

        def softmax(j, masked):
            slot = j & 1
            if masked:
                keep = (j * TK + lax.broadcasted_iota(jnp.int32, (TK, 1), 0)) <= tpos
            for h in range(N_HEADS):
                st = st_s[slot, h]
                if masked:
                    st = jnp.where(keep, st, NEG)
                m = ml_s[0, h]
                m_new = jnp.maximum(m, jnp.max(st, axis=0, keepdims=True))
                alpha = jnp.exp(m - m_new)
                pe = jnp.exp(st - m_new)
                ml_s[0, h] = m_new
                ml_s[1, h] = alpha * ml_s[1, h] + jnp.sum(pe, axis=0, keepdims=True)
                al_s[slot, h] = alpha
                p_s[slot, h] = pe.astype(BF16)

        def values(j):
            slot = j & 1
            jv = jnp.maximum(j, 0)
            for h in range(N_HEADS):
                p, e = divmod(h, 2)
                acc_s[h] = al_s[slot, h] * acc_s[h] + _nn(vt_ref[p, jv, e * HEAD_DIM:(e + 1) * HEAD_DIM, :], p_s[slot, h])

        def step(j, carry):
            values(j - 1)
            softmax(j, False)
            scores(j + 1)
            return carry

        last = ratio * i + ratio - 1
        scores(0)
        lax.fori_loop(0, ratio * i, step, 0)
        for kk in range(ratio):
            j = ratio * i + kk
            values(j - 1)
            softmax(j, True)
            if kk < ratio - 1:
                scores(j + 1)
        values(last)
        for p in range(N_HEADS // 2):
            outs = []
            for e in range(2):
                h = 2 * p + e
                l = ml_s[1, h]
                outs.append(_tn((acc_s[h] / l).astype(BF16), eye))
                lse_ref[p, e:e + 1, :] = ml_s[0, h] + jnp.log(l)
            o_ref[:, 128 * p:128 * (p + 1)] = jnp.concatenate(outs, axis=1).astype(BF16)

    return _pallas(
        body, name="fox_fwd", grid=(t_tok // seq, nq),
        in_specs=[pl.BlockSpec((N_HEADS, TQ, 128), lambda b, i: (0, b * nq + i, 0)),
                  pl.BlockSpec((N_HEADS, seq, 128), lambda b, i: (0, b, 0)),
                  pl.BlockSpec((N_HEADS // 2, seq // TK, 128, TK), lambda b, i: (0, b, 0, 0))],
        out_specs=[pl.BlockSpec((TQ, 512), lambda b, i: (b * nq + i, 0)),
                   pl.BlockSpec((N_HEADS // 2, 2, TQ), lambda b, i: (0, 0, b * nq + i))],
        out_shape=[jax.ShapeDtypeStruct((t_tok, 512), BF16), jax.ShapeDtypeStruct((N_HEADS // 2, 2, t_tok), F32)],
        scratch_shapes=[pltpu.VMEM((2, N_HEADS, 1, TQ), F32), pltpu.VMEM((N_HEADS, HEAD_DIM, TQ), F32),
                        pltpu.VMEM((2, N_HEADS, TK, TQ), F32), pltpu.VMEM((2, N_HEADS, TK, TQ), BF16),
                        pltpu.VMEM((2, N_HEADS, 1, TQ), F32), pltpu.VMEM((N_HEADS, 128, TQ), BF16)],
        compiler_params=_cp(("arbitrary", "arbitrary"), VMEM_MID))(q_aug, k_aug, vt)


def _fox_bwd_t(q_aug, k_aug, kt, qkvg, du_b, b_out, lse, seq):
    TK = TKB
    t_tok = qkvg.shape[0]
    nq = seq // TQ
    nkb = seq // TK
    ratio = TQ // TK
    hg = 4

    def body(qa_ref, ka_ref, kt_ref, v_ref, do_ref, o_ref, lse_ref, dq_ref, dk_ref, dv_ref, df_ref,
             dqt_s, row_s, dfk_s, dk_s, dv_s, dot_s, st_s, dp_s, pb_s, db_s, qt_s):
        eye = _eye(HEAD_DIM, BF16)
        eye2 = _eye(128, BF16)
        eye_k = _eye(TK, F32)
        lane8 = lax.broadcasted_iota(jnp.int32, (8, 128), 1)
        lane_k = lax.broadcasted_iota(jnp.int32, (TK, 128), 1)
        first = [lane8 < HEAD_DIM, lane8 >= HEAD_DIM]
        for pp in range(hg // 2):
            for ii in range(nq):
                dot_s[pp, ii] = _nt(eye2, do_ref[ii * TQ:(ii + 1) * TQ, 128 * pp:128 * (pp + 1)]).astype(BF16)
        for hh in range(hg):
            for ii in range(nq):
                qt_s[hh, ii] = _nt(eye2, qa_ref[hh, ii * TQ:(ii + 1) * TQ, :]).astype(BF16)
        for hh in range(hg):
            pp, e = divmod(hh, 2)
            head_lanes = jnp.where(first[e], 1.0, 0.0)
            for ii in range(nq):
                rows = slice(ii * TQ, (ii + 1) * TQ)
                prod = do_ref[rows, 128 * pp:128 * (pp + 1)].astype(F32) * o_ref[rows, 128 * pp:128 * (pp + 1)].astype(F32)
                row_s[hh, ii, 0] = _nt(head_lanes, prod, HI)
                row_s[hh, ii, 1] = jnp.broadcast_to(lse_ref[pp, e:e + 1, ii * TQ:(ii + 1) * TQ], (8, TQ))
                dqt_s[hh, ii] = jnp.zeros((128, TQ), F32)

        def kblock(j, _):
            krow = pl.multiple_of(j * TK, TK)
            spos = j * TK + lax.broadcasted_iota(jnp.int32, (TK, 1), 0)
            for hh in range(hg):
                dk_s[hh] = jnp.zeros((TK, 128), F32)
                dv_s[hh] = jnp.zeros((TK, 128), F32)

            def scores(i):
                for hh in range(hg):
                    pp, e = divmod(hh, 2)
                    own = (lane_k < HEAD_DIM) if e == 0 else (lane_k >= HEAD_DIM)
                    v2 = v_ref[pl.ds(krow, TK), 128 * pp:128 * (pp + 1)]
                    vj = jnp.where(own, v2, jnp.zeros_like(v2))
                    st_s[i & 1, hh] = _nn(ka_ref[hh, pl.ds(krow, TK), :], qt_s[hh, i])
                    dp_s[i & 1, hh] = _nn(vj, dot_s[pp, i])

            def elementwise(i, masked):
                slot = i & 1
                if masked:
                    keep = spos <= (i * TQ + lax.broadcasted_iota(jnp.int32, (1, TQ), 1))
                for hh in range(hg):
                    pt = jnp.exp(st_s[slot, hh] - row_s[hh, i, 1][0:1, :])
                    if masked:
                        pt = jnp.where(keep, pt, 0.0)
                    dst = pt * (dp_s[slot, hh] - row_s[hh, i, 0][0:1, :])
                    pb_s[slot, hh] = pt.astype(BF16)
                    db_s[slot, hh] = dst.astype(BF16)

            def grads(i):
                slot = i & 1
                qrow = pl.multiple_of(i * TQ, TQ)
                for hh in range(hg):
                    dst_b = db_s[slot, hh]
                    dv_s[hh] += _nn(pb_s[slot, hh], do_ref[pl.ds(qrow, TQ), 128 * (hh // 2):128 * (hh // 2 + 1)])
                    dk_s[hh] += _nn(dst_b, qa_ref[hh, pl.ds(qrow, TQ), :])
                    dqt_s[hh, i] += _nn(kt_ref[hh, j], dst_b)

            def step(i, carry):
                grads(i - 1)
                elementwise(i, False)
                scores(jnp.minimum(i + 1, nq - 1))
                return carry

            i0 = j // ratio
            scores(i0)
            elementwise(i0, True)
            scores(jnp.minimum(i0 + 1, nq - 1))
            lax.fori_loop(i0 + 1, nq, step, 0)
            grads(nq - 1)
            for pp in range(hg // 2):
                cols = slice(128 * pp, 128 * (pp + 1))
                dk_ref[pl.ds(krow, TK), cols] = jnp.concatenate(
                    [dk_s[2 * pp][:, :HEAD_DIM], dk_s[2 * pp + 1][:, :HEAD_DIM]], axis=1).astype(BF16)
                dv_ref[pl.ds(krow, TK), cols] = jnp.where(lane_k < HEAD_DIM, dv_s[2 * pp], dv_s[2 * pp + 1]).astype(BF16)
            for hh in range(hg):
                dfk_s[hh, j] = _tn(dk_s[hh][:, HEAD_DIM:HEAD_DIM + 8], eye_k, HI)
            return 0

        lax.fori_loop(0, nkb, kblock, 0)
        for pp in range(hg // 2):
            for ii in range(nq):
                parts = []
                for e in range(2):
                    dqt = dqt_s[2 * pp + e, ii]
                    parts.append(_tn(dqt[0:HEAD_DIM, :].astype(BF16), eye) * SCALE)
                    for kk in range(ratio):
                        jj = ii * ratio + kk
                        df_ref[pp, e:e + 1, jj * TK:(jj + 1) * TK] = (dqt[67:68, kk * TK:(kk + 1) * TK]
                                                                     - dfk_s[2 * pp + e, jj][0:1, :])
                dq_ref[ii * TQ:(ii + 1) * TQ, 128 * pp:128 * (pp + 1)] = jnp.concatenate(parts, axis=1).astype(BF16)

    aug_blk = pl.BlockSpec((hg, seq, 128), lambda b, g: (g, b, 0))
    pair_blk = pl.BlockSpec((seq, 64 * hg), lambda b, g: (b, g))
    row_blk = pl.BlockSpec((hg // 2, 2, seq), lambda b, g: (g, 0, b))
    return _pallas(
        body, name="fox_bwd", grid=(t_tok // seq, N_HEADS // hg),
        in_specs=[aug_blk, aug_blk, pl.BlockSpec((hg, nkb, 128, TK), lambda b, g: (g, b, 0, 0)),
                  pl.BlockSpec((seq, 64 * hg), lambda b, g: (b, C_BV // (64 * hg) + g)), pair_blk, pair_blk, row_blk],
        out_specs=[pair_blk, pair_blk, pair_blk, row_blk],
        out_shape=[jax.ShapeDtypeStruct((t_tok, 512), BF16)] * 3
        + [jax.ShapeDtypeStruct((N_HEADS // 2, 2, t_tok), F32)],
        scratch_shapes=[pltpu.VMEM((hg, nq, 128, TQ), F32), pltpu.VMEM((hg, nq, 2, 8, TQ), F32),
                        pltpu.VMEM((hg, nkb, 8, TK), F32), pltpu.VMEM((hg, TK, 128), F32),
                        pltpu.VMEM((hg, TK, 128), F32), pltpu.VMEM((hg // 2, nq, 128, TQ), BF16),
                        pltpu.VMEM((2, hg, TK, TQ), F32), pltpu.VMEM((2, hg, TK, TQ), F32),
                        pltpu.VMEM((2, hg, TK, TQ), BF16), pltpu.VMEM((2, hg, TK, TQ), BF16),
                        pltpu.VMEM((hg, nq, 128, TQ), BF16)],
        compiler_params=_cp(("arbitrary", "arbitrary"), VMEM_BIG))(q_aug, k_aug, kt, qkvg, du_b, b_out, lse)


def _fox_bwd_t_old(q_aug, k_aug, kt, qkvg, du_b, b_out, lse, seq):
    t_tok = qkvg.shape[0]
    nq = seq // TQ
    nkb = seq // TK
    ratio = TQ // TK

    def body(qa_ref, ka_ref, kt_ref, v_ref, do_ref, o_ref, lse_ref, dq_ref, dk_ref, dv_ref, df_ref,
             dqt_s, out_s, row_s, dfk_s):
        ones_b = jnp.ones((8, TQ), BF16)
        ones_f = jnp.ones((8, HEAD_DIM), F32)
        eye = _eye(HEAD_DIM, BF16)
        for e in range(2):
            lo, hi = e * HEAD_DIM, (e + 1) * HEAD_DIM
            for ii in range(nq):
                rows = slice(ii * TQ, (ii + 1) * TQ)
                do = do_ref[rows, :][:, lo:hi].astype(F32)
                ov = o_ref[rows, :][:, lo:hi].astype(F32)
                row_s[ii, 0] = _nt(ones_f, do * ov, HI)
                row_s[ii, 1] = jnp.broadcast_to(lse_ref[e:e + 1, ii * TQ:(ii + 1) * TQ], (8, TQ))
                dqt_s[ii] = jnp.zeros((128, TQ), F32)

            def kblock(j, _):
                krow = pl.multiple_of(j * TK, TK)
                kj = ka_ref[e, pl.ds(krow, TK), :]
                ktj = kt_ref[e, j]
                vj = v_ref[pl.ds(krow, TK), :][:, lo:hi]
                spos = j * TK + lax.broadcasted_iota(jnp.int32, (TK, 1), 0)

                def qblock(i, carry, masked):
                    dk_acc, dv_acc, dfk = carry
                    qrow = pl.multiple_of(i * TQ, TQ)
                    qa = qa_ref[e, pl.ds(qrow, TQ), :]
                    doh = do_ref[pl.ds(qrow, TQ), :][:, lo:hi]
                    pt = jnp.exp(_nt(kj, qa) - row_s[i, 1][0:1, :])
                    if masked:
                        tpos = i * TQ + lax.broadcasted_iota(jnp.int32, (1, TQ), 1)
                        pt = jnp.where(spos <= tpos, pt, 0.0)
                    dst = pt * (_nt(vj, doh) - row_s[i, 0][0:1, :])
                    dst_b = dst.astype(BF16)
                    dv_acc = dv_acc + _nn(pt.astype(BF16), doh)
                    dk_acc = dk_acc + _nn(dst_b, qa)
                    dqt_s[i] += _nn(ktj, dst_b)
                    dfk = dfk + _nt(ones_b, dst_b)
                    return dk_acc, dv_acc, dfk

                i0 = j // ratio
                carry = (jnp.zeros((TK, 128), F32), jnp.zeros((TK, HEAD_DIM), F32), jnp.zeros((8, TK), F32))
                carry = qblock(i0, carry, True)
                dk_acc, dv_acc, dfk = lax.fori_loop(i0 + 1, nq, functools.partial(qblock, masked=False), carry)
                out_s[1, e, pl.ds(krow, TK), :] = dk_acc[:, :HEAD_DIM]
                out_s[2, e, pl.ds(krow, TK), :] = dv_acc
                dfk_s[j] = dfk
                return 0

            lax.fori_loop(0, nkb, kblock, 0)
            for ii in range(nq):
                dqt = dqt_s[ii]
                out_s[0, e, ii * TQ:(ii + 1) * TQ, :] = _tn(dqt[0:HEAD_DIM, :].astype(BF16), eye) * SCALE
                for kk in range(ratio):
                    jj = ii * ratio + kk
                    df_ref[e:e + 1, jj * TK:(jj + 1) * TK] = dqt[67:68, kk * TK:(kk + 1) * TK] - dfk_s[jj][0:1, :]
        for k, ref in enumerate((dq_ref, dk_ref, dv_ref)):
            ref[...] = jnp.concatenate([out_s[k, 0], out_s[k, 1]], axis=1).astype(BF16)

    aug_blk = pl.BlockSpec((2, seq, 128), lambda b, p: (p, b, 0))
    pair_blk = pl.BlockSpec((seq, 128), lambda b, p: (b, p))
    row_blk = pl.BlockSpec((None, 2, seq), lambda b, p: (p, 0, b))
    return _pallas(
        body, name="fox_bwd", grid=(t_tok // seq, N_HEADS // 2),
        in_specs=[aug_blk, aug_blk, pl.BlockSpec((2, nkb, 128, TK), lambda b, p: (p, b, 0, 0)),
                  pl.BlockSpec((seq, 128), lambda b, p: (b, C_BV // 128 + p)), pair_blk, pair_blk, row_blk],
        out_specs=[pair_blk, pair_blk, pair_blk, row_blk],
        out_shape=[jax.ShapeDtypeStruct((t_tok, 512), BF16)] * 3
        + [jax.ShapeDtypeStruct((N_HEADS // 2, 2, t_tok), F32)],
        scratch_shapes=[pltpu.VMEM((nq, 128, TQ), F32), pltpu.VMEM((3, 2, seq, HEAD_DIM), F32),
                        pltpu.VMEM((nq, 2, 8, TQ), F32), pltpu.VMEM((nkb, 8, TK), F32)],
        compiler_params=_cp(("arbitrary", "arbitrary"), VMEM_BIG))(q_aug, k_aug, kt, qkvg, du_b, b_out, lse)


def _fox_fwd(qkvg, f_row, f_col, seq):
    t_tok = qkvg.shape[0]
    nq = seq // TQ

    def body(q_ref, k_ref, v_ref, fr_ref, fc_ref, o_ref, lse_ref, fk_s):
        i = pl.program_id(1)
        for jj in range(nq):
            fk_s[jj] = fr_ref[:, jj * TQ:(jj + 1) * TQ]
        fcol = fc_ref[...]
        tpos = i * TQ + lax.broadcasted_iota(jnp.int32, (TQ, 1), 0)
        lane = lax.broadcasted_iota(jnp.int32, (TQ, 128), 1)
        lse_tile = jnp.zeros((TQ, 128), F32)
        for p in range(N_HEADS // 2):
            q2 = q_ref[:, 128 * p:128 * (p + 1)]
            qs = [q2[:, :HEAD_DIM], q2[:, HEAD_DIM:]]
            fqs = [_col(fcol, 2 * p + e) for e in range(2)]

            def kblock(j, carry):
                row0 = pl.multiple_of(j * TQ, TQ)
                k2 = k_ref[pl.ds(row0, TQ), 128 * p:128 * (p + 1)]
                v2 = v_ref[pl.ds(row0, TQ), 128 * p:128 * (p + 1)]
                fk8 = fk_s[j]
                spos = j * TQ + lax.broadcasted_iota(jnp.int32, (1, TQ), 1)
                keep = spos <= tpos
                new = []
                for e in range(2):
                    m, l, acc = carry[3 * e:3 * e + 3]
                    kh = k2[:, e * HEAD_DIM:(e + 1) * HEAD_DIM]
                    vh = v2[:, e * HEAD_DIM:(e + 1) * HEAD_DIM]
                    s = _nt(qs[e], kh) * SCALE + (fqs[e] - fk8[2 * p + e:2 * p + e + 1, :])
                    s = jnp.where(keep, s, NEG)
                    m_new = jnp.maximum(m, jnp.max(s, axis=1, keepdims=True))
                    alpha = jnp.exp(m - m_new)
                    pe = jnp.exp(s - m_new)
                    l = alpha * l + jnp.sum(pe, axis=1, keepdims=True)
                    acc = alpha * acc + _nn(pe.astype(BF16), vh)
                    new += [m_new, l, acc]
                return tuple(new)

            init = (jnp.full((TQ, 1), NEG, F32), jnp.zeros((TQ, 1), F32), jnp.zeros((TQ, HEAD_DIM), F32)) * 2
            res = lax.fori_loop(0, i + 1, kblock, init)
            outs = []
            for e in range(2):
                m, l, acc = res[3 * e:3 * e + 3]
                outs.append(acc / l)
                lse_tile = jnp.where(lane == 2 * p + e, m + jnp.log(l), lse_tile)
            o_ref[:, 128 * p:128 * (p + 1)] = jnp.concatenate(outs, axis=1).astype(BF16)
        lse_ref[...] = lse_tile

    return _pallas(
        body, name="fox_fwd", grid=(t_tok // seq, nq),
        in_specs=[pl.BlockSpec((TQ, 512), lambda b, i: (b * nq + i, C_BQ // 512)),
                  pl.BlockSpec((seq, 512), lambda b, i: (b, C_BK // 512)),
                  pl.BlockSpec((seq, 512), lambda b, i: (b, C_BV // 512)),
                  pl.BlockSpec((N_HEADS, seq), lambda b, i: (0, b)),
                  pl.BlockSpec((TQ, 128), lambda b, i: (b * nq + i, 0))],
        out_specs=[pl.BlockSpec((TQ, 512), lambda b, i: (b * nq + i, 0)),
                   pl.BlockSpec((TQ, 128), lambda b, i: (b * nq + i, 0))],
        out_shape=[jax.ShapeDtypeStruct((t_tok, 512), BF16), jax.ShapeDtypeStruct((t_tok, 128), F32)],
        scratch_shapes=[pltpu.VMEM((nq, N_HEADS, TQ), F32)],
        compiler_params=_cp(("arbitrary", "arbitrary"), VMEM_MID))(qkvg, qkvg, qkvg, f_row, f_col)


def _fox_bwd(qkvg, du_b, b_out, lse, f_row, f_col, seq):
    t_tok = qkvg.shape[0]
    nq = seq // TQ

    def body(q_ref, k_ref, v_ref, do_ref, o_ref, lse_ref, fr_ref, fc_ref,
             dq_ref, dk_ref, dv_ref, df_ref, dq_s, dk_s, dv_s, col_s, df_s, fk_s):
        p = pl.program_id(1)
        for jj in range(nq):
            fk_s[jj] = fr_ref[:, jj * TQ:(jj + 1) * TQ]
        eye = (lax.broadcasted_iota(jnp.int32, (TQ, TQ), 0) == lax.broadcasted_iota(jnp.int32, (TQ, TQ), 1)).astype(F32)
        for e in range(2):
            h = 2 * p + e
            lo, hi = e * HEAD_DIM, (e + 1) * HEAD_DIM
            for ii in range(nq):
                rows = slice(ii * TQ, (ii + 1) * TQ)
                do = do_ref[rows, :][:, lo:hi].astype(F32)
                ov = o_ref[rows, :][:, lo:hi].astype(F32)
                col_s[0, rows, :] = jnp.sum(do * ov, axis=1, keepdims=True)
                col_s[1, rows, :] = _col(lse_ref[rows, :], h)
                col_s[2, rows, :] = _col(fc_ref[rows, :], h)
                dq_s[rows, :] = jnp.zeros((TQ, HEAD_DIM), F32)
                df_s[ii] = jnp.zeros((8, TQ), F32)
                col_s[3, rows, :] = jnp.zeros((TQ, 1), F32)

            def kblock(j, _):
                krow = pl.multiple_of(j * TQ, TQ)
                kh = k_ref[pl.ds(krow, TQ), :][:, lo:hi]
                vh = v_ref[pl.ds(krow, TQ), :][:, lo:hi]
                fk = _row(fk_s[j], h)
                spos = j * TQ + lax.broadcasted_iota(jnp.int32, (1, TQ), 1)

                def qblock(i, carry):
                    dk_acc, dv_acc, dfk = carry
                    qrow = pl.multiple_of(i * TQ, TQ)
                    qh = q_ref[pl.ds(qrow, TQ), :][:, lo:hi]
                    doh = do_ref[pl.ds(qrow, TQ), :][:, lo:hi]
                    delta = col_s[0, pl.ds(qrow, TQ), :]
                    lse_q = col_s[1, pl.ds(qrow, TQ), :]
                    fq = col_s[2, pl.ds(qrow, TQ), :]
                    tpos = i * TQ + lax.broadcasted_iota(jnp.int32, (TQ, 1), 0)
                    s = _nt(qh, kh) * SCALE + (fq - fk)
                    pr = jnp.where(spos <= tpos, jnp.exp(s - lse_q), 0.0)
                    dp = _nt(doh, vh)
                    ds = pr * (dp - delta)
                    ds_b = ds.astype(BF16)
                    dv_acc = dv_acc + _tn(pr.astype(BF16), doh)
                    dk_acc = dk_acc + _tn(ds_b, qh)
                    dq_s[pl.ds(qrow, TQ), :] += _nn(ds_b, kh)
                    col_s[3, pl.ds(qrow, TQ), :] += jnp.sum(ds, axis=1, keepdims=True)
                    dfk = dfk + jnp.sum(ds, axis=0, keepdims=True)
                    return dk_acc, dv_acc, dfk

                zero = jnp.zeros((TQ, HEAD_DIM), F32)
                dk_acc, dv_acc, dfk = lax.fori_loop(j, nq, qblock, (zero, zero, jnp.zeros((1, TQ), F32)))
                dk_s[e, pl.ds(krow, TQ), :] = dk_acc * SCALE
                dv_s[e, pl.ds(krow, TQ), :] = dv_acc
                df_s[j] -= jnp.broadcast_to(dfk, (8, TQ))
                return 0

            lax.fori_loop(0, nq, kblock, 0)
            dq_s2 = dq_s[...] * SCALE
            dk_s[2 + e] = dq_s2
            for ii in range(nq):
                dfq = jnp.broadcast_to(col_s[3, ii * TQ:(ii + 1) * TQ, :], (TQ, 128))
                df_ref[e:e + 1, ii * TQ:(ii + 1) * TQ] = _tn(dfq, eye, HI)[0:1, :] + df_s[ii][0:1, :]
        dq_ref[...] = jnp.concatenate([dk_s[2], dk_s[3]], axis=1).astype(BF16)
        dk_ref[...] = jnp.concatenate([dk_s[0], dk_s[1]], axis=1).astype(BF16)
        dv_ref[...] = jnp.concatenate([dv_s[0], dv_s[1]], axis=1).astype(BF16)

    blk = lambda off: pl.BlockSpec((seq, 128), lambda b, p: (b, off // 128 + p))
    out_blk = pl.BlockSpec((seq, 128), lambda b, p: (b, p))
    return _pallas(
        body, name="fox_bwd", grid=(t_tok // seq, N_HEADS // 2),
        in_specs=[blk(C_BQ), blk(C_BK), blk(C_BV), out_blk, out_blk,
                  pl.BlockSpec((seq, 128), lambda b, p: (b, 0)),
                  pl.BlockSpec((N_HEADS, seq), lambda b, p: (0, b)),
                  pl.BlockSpec((seq, 128), lambda b, p: (b, 0))],
        out_specs=[out_blk, out_blk, out_blk, pl.BlockSpec((None, 2, seq), lambda b, p: (p, 0, b))],
        out_shape=[jax.ShapeDtypeStruct((t_tok, 512), BF16)] * 3
        + [jax.ShapeDtypeStruct((N_HEADS // 2, 2, t_tok), F32)],
        scratch_shapes=[pltpu.VMEM((seq, HEAD_DIM), F32), pltpu.VMEM((4, seq, HEAD_DIM), F32),
                        pltpu.VMEM((2, seq, HEAD_DIM), F32), pltpu.VMEM((4, seq, 1), F32),
                        pltpu.VMEM((nq, 8, TQ), F32), pltpu.VMEM((nq, N_HEADS, TQ), F32)],
        compiler_params=_cp(("arbitrary", "arbitrary"), VMEM_BIG))(qkvg, qkvg, qkvg, du_b, b_out, lse, f_row, f_col)


def _swa_window(k_ref, v_ref, n):
    prev = pl.multiple_of(jnp.maximum(n - 1, 0) * BLOCK, BLOCK)
    cur = pl.multiple_of(n * BLOCK, BLOCK)
    kwin = jnp.concatenate([k_ref[pl.ds(prev, BLOCK), :], k_ref[pl.ds(cur, BLOCK), :]], axis=0)
    vwin = jnp.concatenate([v_ref[pl.ds(prev, BLOCK), :], v_ref[pl.ds(cur, BLOCK), :]], axis=0)
    ti = lax.broadcasted_iota(jnp.int32, (BLOCK, 2 * BLOCK), 0)
    sj = lax.broadcasted_iota(jnp.int32, (BLOCK, 2 * BLOCK), 1)
    rel = ti - sj + BLOCK
    first_key = jnp.where(n > 0, 0, BLOCK)
    mask = jnp.logical_and(jnp.logical_and(rel >= 0, rel < BLOCK), sj >= first_key)
    return kwin, vwin, mask, prev, cur


def _head_cols(ref, h):
    pair = ref[:, 128 * (h // 2):128 * (h // 2 + 1)]
    return pair[:, (h % 2) * HEAD_DIM:(h % 2 + 1) * HEAD_DIM]


def _swa_logits(q_ref, kwin, bias_ref, h, mask):
    hk = h // KV_GROUP
    s = _nt(_head_cols(q_ref, h), kwin[:, hk * HEAD_DIM:(hk + 1) * HEAD_DIM]) * SCALE + bias_ref[h]
    return jnp.where(mask, s, NEG)


def _swa_fwd(qkvg, bias, sinks, seq):
    t_tok = qkvg.shape[0]
    nb = seq // BLOCK

    def body(sink_ref, q_ref, k_ref, v_ref, bias_ref, o_ref, lse_ref, s_s, p_s, den_s):
        g = pl.program_id(1)
        subs = [pl.ds(s * BLOCK, BLOCK) for s in range(SWA_SUB)]
        wins = [_swa_window(k_ref, v_ref, SWA_SUB * g + s) for s in range(SWA_SUB)]
        for s in range(SWA_SUB):
            for h in range(N_HEADS):
                s_s[s * N_HEADS + h] = _swa_logits(q_ref.at[subs[s]], wins[s][0], bias_ref, h, wins[s][2])
        lane = lax.broadcasted_iota(jnp.int32, (BLOCK, 128), 1)
        for s in range(SWA_SUB):
            lse_tile = jnp.zeros((BLOCK, 128), F32)
            for h in range(N_HEADS):
                sc = s_s[s * N_HEADS + h]
                sink = sink_ref[h]
                m = jnp.maximum(jnp.max(sc, axis=1, keepdims=True), sink)
                pe = jnp.exp(sc - m)
                den = jnp.sum(pe, axis=1, keepdims=True) + jnp.exp(sink - m)
                p_s[s * N_HEADS + h] = pe.astype(BF16)
                den_s[s * N_HEADS + h] = den
                lse_tile = jnp.where(lane == h, m + jnp.log(den), lse_tile)
            lse_ref[subs[s], :] = lse_tile
        for s in range(SWA_SUB):
            vwin = wins[s][1]
            for pr in range(N_HEADS // 2):
                outs = []
                for h in (2 * pr, 2 * pr + 1):
                    hk = h // KV_GROUP
                    outs.append(_nn(p_s[s * N_HEADS + h], vwin[:, hk * HEAD_DIM:(hk + 1) * HEAD_DIM]) / den_s[s * N_HEADS + h])
                o_ref[subs[s], 128 * pr:128 * (pr + 1)] = jnp.concatenate(outs, axis=1).astype(BF16)

    rows = SWA_SUB * BLOCK
    steps = nb // SWA_SUB
    return _pallas(
        body, name="swa_fwd", grid=(t_tok // seq, steps),
        in_specs=[pl.BlockSpec(memory_space=pltpu.SMEM),
                  pl.BlockSpec((rows, 512), lambda b, n: (b * steps + n, C_AQ // 512)),
                  pl.BlockSpec((seq, 128), lambda b, n: (b, C_AK // 128)),
                  pl.BlockSpec((seq, 128), lambda b, n: (b, C_AV // 128)),
                  pl.BlockSpec((N_HEADS, BLOCK, 2 * BLOCK), lambda b, n: (0, 0, 0))],
        out_specs=[pl.BlockSpec((rows, 512), lambda b, n: (b * steps + n, 0)),
                   pl.BlockSpec((rows, 128), lambda b, n: (b * steps + n, 0))],
        out_shape=[jax.ShapeDtypeStruct((t_tok, 512), BF16), jax.ShapeDtypeStruct((t_tok, 128), F32)],
        scratch_shapes=[pltpu.VMEM((SWA_SUB * N_HEADS, BLOCK, 2 * BLOCK), F32),
                        pltpu.VMEM((SWA_SUB * N_HEADS, BLOCK, 2 * BLOCK), BF16),
                        pltpu.VMEM((SWA_SUB * N_HEADS, BLOCK, 1), F32)],
        compiler_params=_cp(("arbitrary", "arbitrary"), VMEM_MID))(sinks, qkvg, qkvg, qkvg, bias)


def _swa_bwd(qkvg, du_a, a_out, lse, bias, sinks, seq):
    t_tok = qkvg.shape[0]
    nb = seq // BLOCK

    def body(sink_ref, q_ref, k_ref, v_ref, do_ref, o_ref, lse_ref, bias_ref,
             dq_ref, dkv_ref, dbias_ref, dsink_ref, kv_s, s_s, dp_s, pb_s, db_s):
        b, n = pl.program_id(0), pl.program_id(1)

        @pl.when(jnp.logical_and(b == 0, n == 0))
        def _():
            dbias_ref[...] = jnp.zeros_like(dbias_ref)
            dsink_ref[...] = jnp.zeros_like(dsink_ref)

        @pl.when(n == 0)
        def _():
            kv_s[...] = jnp.zeros_like(kv_s)

        subs = [pl.ds(s * BLOCK, BLOCK) for s in range(SWA_SUB)]
        wins = [_swa_window(k_ref, v_ref, SWA_SUB * n + s) for s in range(SWA_SUB)]
        for s in range(SWA_SUB):
            kwin, vwin, mask = wins[s][:3]
            for h in range(N_HEADS):
                hk = h // KV_GROUP
                s_s[s * N_HEADS + h] = _swa_logits(q_ref.at[subs[s]], kwin, bias_ref, h, mask)
                dp_s[s * N_HEADS + h] = _nt(_head_cols(do_ref.at[subs[s]], h), vwin[:, hk * HEAD_DIM:(hk + 1) * HEAD_DIM])
        for s in range(SWA_SUB):
            lse_tile = lse_ref[subs[s], :]
            do_s, o_s = do_ref.at[subs[s]], o_ref.at[subs[s]]
            for h in range(N_HEADS):
                delta = jnp.sum(_head_cols(do_s, h).astype(F32) * _head_cols(o_s, h).astype(F32), axis=1, keepdims=True)
                lse_h = _col(lse_tile, h)
                pe = jnp.exp(s_s[s * N_HEADS + h] - lse_h)
                ds = pe * (dp_s[s * N_HEADS + h] - delta)
                dbias_ref[h] += ds
                psink = jnp.exp(sink_ref[h] - lse_h)
                dsink_ref[h:h + 1, :] += jnp.broadcast_to(jnp.sum(-psink * delta, axis=0, keepdims=True), (1, 128))
                pb_s[s * N_HEADS + h] = pe.astype(BF16)
                db_s[s * N_HEADS + h] = ds.astype(BF16)
        for s in range(SWA_SUB):
            kwin, _, _, prev, cur = wins[s]
            q_s, do_s = q_ref.at[subs[s]], do_ref.at[subs[s]]
            for pr in range(N_HEADS // 2):
                dqs = []
                for h in (2 * pr, 2 * pr + 1):
                    hk = h // KV_GROUP
                    dqs.append(_nn(db_s[s * N_HEADS + h], kwin[:, hk * HEAD_DIM:(hk + 1) * HEAD_DIM]) * SCALE)
                dq_ref[subs[s], 128 * pr:128 * (pr + 1)] = jnp.concatenate(dqs, axis=1).astype(BF16)
            dks, dvs = [], []
            for hk in range(N_HEADS // KV_GROUP):
                dk = jnp.zeros((2 * BLOCK, HEAD_DIM), F32)
                dv = jnp.zeros((2 * BLOCK, HEAD_DIM), F32)
                for h in range(hk * KV_GROUP, (hk + 1) * KV_GROUP):
                    dk = dk + _tn(db_s[s * N_HEADS + h], _head_cols(q_s, h))
                    dv = dv + _tn(pb_s[s * N_HEADS + h], _head_cols(do_s, h))
                dks.append(dk * SCALE)
                dvs.append(dv)
            upd = jnp.concatenate(dks + dvs, axis=1)
            kv_s[pl.ds(prev, BLOCK), :] += upd[:BLOCK]
            kv_s[pl.ds(cur, BLOCK), :] += upd[BLOCK:]

        @pl.when(n == steps - 1)
        def _():
            dkv_ref[...] = kv_s[...].astype(BF16)

    rows = SWA_SUB * BLOCK
    steps = nb // SWA_SUB
    tile = (SWA_SUB * N_HEADS, BLOCK, 2 * BLOCK)
    return _pallas(
        body, name="swa_bwd", grid=(t_tok // seq, steps),
        in_specs=[pl.BlockSpec(memory_space=pltpu.SMEM),
                  pl.BlockSpec((rows, 512), lambda b, n: (b * steps + n, C_AQ // 512)),
                  pl.BlockSpec((seq, 128), lambda b, n: (b, C_AK // 128)),
                  pl.BlockSpec((seq, 128), lambda b, n: (b, C_AV // 128)),
                  pl.BlockSpec((rows, 512), lambda b, n: (b * steps + n, 0)),
                  pl.BlockSpec((rows, 512), lambda b, n: (b * steps + n, 0)),
                  pl.BlockSpec((rows, 128), lambda b, n: (b * steps + n, 0)),
                  pl.BlockSpec((N_HEADS, BLOCK, 2 * BLOCK), lambda b, n: (0, 0, 0))],
        out_specs=[pl.BlockSpec((rows, 512), lambda b, n: (b * steps + n, 0)),
                   pl.BlockSpec((seq, 256), lambda b, n: (b, 0)),
                   pl.BlockSpec((N_HEADS, BLOCK, 2 * BLOCK), lambda b, n: (0, 0, 0)),
                   pl.BlockSpec((N_HEADS, 128), lambda b, n: (0, 0))],
        out_shape=[jax.ShapeDtypeStruct((t_tok, 512), BF16), jax.ShapeDtypeStruct((t_tok, 256), BF16),
                   jax.ShapeDtypeStruct((N_HEADS, BLOCK, 2 * BLOCK), F32), jax.ShapeDtypeStruct((N_HEADS, 128), F32)],
        scratch_shapes=[pltpu.VMEM((seq, 256), F32), pltpu.VMEM(tile, F32), pltpu.VMEM(tile, F32),
                        pltpu.VMEM(tile, BF16), pltpu.VMEM(tile, BF16)],
        compiler_params=_cp(("arbitrary", "arbitrary"), VMEM_MID))(sinks, qkvg, qkvg, qkvg, du_a, a_out, lse, bias)


def _out_proj(name, u_parts, gate_arr, gate_blk, w_out, x, gmod, seq):
    t_tok = x.shape[0]
    nu = len(u_parts)

    def body(*refs):
        u_refs = refs[:nu]
        g_ref, w_ref, x_ref, gm_ref, yg_ref, y_ref, xn_ref = refs[nu:]
        u = jnp.concatenate([r[...].astype(F32) for r in u_refs], axis=1) if nu > 1 else u_refs[0][...].astype(F32)
        yg = (u * _silu(g_ref[...].astype(F32))).astype(BF16)
        yg_ref[...] = yg
        y = _nn(yg, w_ref[...])
        y_ref[...] = y.astype(BF16)
        xn_ref[...] = x_ref[...] + gm_ref[...] * y

    row = lambda w: pl.BlockSpec((TM, w), lambda i: (i, 0))
    in_specs = [row(u.shape[1]) for u in u_parts]
    in_specs += [pl.BlockSpec((TM, D_MODEL), lambda i: (i, gate_blk)),
                 pl.BlockSpec((D_MODEL, D_MODEL), lambda i: (0, 0)), row(D_MODEL),
                 pl.BlockSpec((None, 1, D_MODEL), lambda i: (i * TM // seq, 0, 0))]
    return _pallas(
        body, name=name, grid=(t_tok // TM,), in_specs=in_specs,
        out_specs=[row(D_MODEL)] * 3,
        out_shape=[jax.ShapeDtypeStruct((t_tok, D_MODEL), BF16)] * 2 + [jax.ShapeDtypeStruct((t_tok, D_MODEL), F32)],
        compiler_params=_cp(("arbitrary",), VMEM_MID))(*u_parts, gate_arr, w_out, x, gmod)


def _out_proj_bwd(name, dxn, gmod, y, w_out, seq, attn=None):
    t_tok = dxn.shape[0]
    tiles_per_seq = seq // TM

    def body(*refs):
        if attn is None:
            dxn_ref, gm_ref, y_ref, w_ref, dy_ref, dgm_ref, dyg_ref = refs
        else:
            dxn_ref, gm_ref, y_ref, w_ref, a_ref, b_ref, g_ref, dy_ref, dgm_ref, dua_ref, dub_ref, dg_ref = refs
        i = pl.program_id(0)
        dxv = dxn_ref[...]
        dy = (dxv * gm_ref[...]).astype(BF16)
        dy_ref[...] = dy

        @pl.when(i % tiles_per_seq == 0)
        def _():
            dgm_ref[...] = jnp.zeros_like(dgm_ref)

        dgm_ref[...] += jnp.sum(dxv * y_ref[...].astype(F32), axis=0, keepdims=True)
        dyg = _nn(dy, w_ref[...])
        if attn is None:
            dyg_ref[...] = dyg
        else:
            gt = g_ref[...].astype(F32)
            du = dyg * _silu(gt)
            dua_ref[...] = du[:, :512].astype(BF16)
            dub_ref[...] = du[:, 512:].astype(BF16)
            u = jnp.concatenate([a_ref[...].astype(F32), b_ref[...].astype(F32)], axis=1)
            dg_ref[...] = (dyg * u * _dsilu(gt)).astype(BF16)

    row = lambda w: pl.BlockSpec((TM, w), lambda i: (i, 0))
    mod_spec = pl.BlockSpec((None, 1, D_MODEL), lambda i: (i * TM // seq, 0, 0))
    in_specs = [row(D_MODEL), mod_spec, row(D_MODEL), pl.BlockSpec((D_MODEL, D_MODEL), lambda i: (0, 0))]
    out_specs = [row(D_MODEL), mod_spec]
    out_shape = [jax.ShapeDtypeStruct((t_tok, D_MODEL), BF16), jax.ShapeDtypeStruct(gmod.shape, F32)]
    args = [dxn, gmod, y, w_out]
    if attn is None:
        out_specs.append(row(D_MODEL))
        out_shape.append(jax.ShapeDtypeStruct((t_tok, D_MODEL), F32))
    else:
        in_specs += [row(512), row(512), pl.BlockSpec((TM, D_MODEL), lambda i: (i, C_GATE // D_MODEL))]
        out_specs += [row(512), row(512), row(D_MODEL)]
        out_shape += [jax.ShapeDtypeStruct((t_tok, 512), BF16)] * 2 + [jax.ShapeDtypeStruct((t_tok, D_MODEL), BF16)]
        args += list(attn)
    return _pallas(body, name=name, grid=(t_tok // TM,), in_specs=in_specs, out_specs=out_specs,
                   out_shape=out_shape, compiler_params=_cp(("arbitrary",), VMEM_MID))(*args)


def _norm_bwd(name, parts, w, x, g, scale, dxn, seq, rows_part=None):
    t_tok = x.shape[0]
    npart = len(parts)
    tiles_per_seq = seq // TM
    nrow_in = 0 if rows_part is None else 2

    def body(*refs):
        p_refs = refs[:npart]
        w_ref, x_ref, g_ref, sc_ref, dxn_ref = refs[npart:npart + 5]
        dx_ref, dss_ref, dg_ref = refs[npart + 5 + nrow_in:]
        i = pl.program_id(0)
        dh = jnp.zeros((TM, D_MODEL), F32)
        if rows_part is not None:
            r_ref, wr_ref = refs[npart + 5:npart + 7]
            dh = dh + _tn(r_ref[...].astype(BF16), wr_ref[...])
        for (arr, off), p_ref in zip(parts, p_refs):
            dh = dh + _nn(p_ref[...], w_ref[off:off + arr.shape[1], :])
        xv = x_ref[...]
        rstd = lax.rsqrt(jnp.mean(xv * xv, axis=-1, keepdims=True) + EPS)
        xhat = xv * rstd
        gv = g_ref[...]
        nrm = xhat * gv

        @pl.when(i % tiles_per_seq == 0)
        def _():
            dss_ref[...] = jnp.zeros_like(dss_ref)

        @pl.when(i == 0)
        def _():
            dg_ref[...] = jnp.zeros_like(dg_ref)

        dss_ref[0:1, :] += jnp.sum(dh, axis=0, keepdims=True)
        dss_ref[1:2, :] += jnp.sum(dh * nrm, axis=0, keepdims=True)
        dn = dh * (1.0 + sc_ref[...])
        dg_ref[0:1, :] += jnp.sum(dn * xhat, axis=0, keepdims=True)
        dxhat = dn * gv
        dx_ref[...] = rstd * (dxhat - xhat * jnp.mean(dxhat * xhat, axis=-1, keepdims=True)) + dxn_ref[...]

    row = lambda wd: pl.BlockSpec((TM, wd), lambda i: (i, 0))
    w_spec = pl.BlockSpec(w.shape, lambda i: (0, 0))
    in_specs = [row(a.shape[1]) for a, _ in parts]
    in_specs += [w_spec, row(D_MODEL), pl.BlockSpec((1, D_MODEL), lambda i: (0, 0)),
                 pl.BlockSpec((None, 1, D_MODEL), lambda i: (i * TM // seq, 0, 0)), row(D_MODEL)]
    args = [a for a, _ in parts] + [w, x, g, scale, dxn]
    if rows_part is not None:
        in_specs += [pl.BlockSpec((8, TM), lambda i: (0, i)), pl.BlockSpec((8, D_MODEL), lambda i: (0, 0))]
        args += list(rows_part)
    nseq = t_tok // seq
    return _pallas(
        body, name=name, grid=(t_tok // TM,), in_specs=in_specs,
        out_specs=[row(D_MODEL), pl.BlockSpec((None, 8, D_MODEL), lambda i: (i * TM // seq, 0, 0)),
                   pl.BlockSpec((8, D_MODEL), lambda i: (0, 0))],
        out_shape=[jax.ShapeDtypeStruct((t_tok, D_MODEL), F32), jax.ShapeDtypeStruct((nseq, 8, D_MODEL), F32),
                   jax.ShapeDtypeStruct((8, D_MODEL), F32)],
        compiler_params=_cp(("arbitrary",), VMEM_BIG))(*args)


def _dw(name, a, parts, blocked=None):
    t_tok, ka = a.shape
    tt = min(1024, t_tok)
    npart = len(parts)
    nt = t_tok // tt

    def body(*refs):
        a_ref = refs[0]
        p_refs = refs[1:1 + npart]
        o_refs = refs[1 + npart:1 + 2 * npart]
        acc_refs = refs[1 + 2 * npart:]
        t = pl.program_id(0)
        av = a_ref[...]
        for p_ref, acc in zip(p_refs, acc_refs):
            upd = _tn(av, p_ref[...])

            @pl.when(t == 0)
            def _():
                acc[...] = upd

            @pl.when(t > 0)
            def _():
                acc[...] += upd

        @pl.when(t == nt - 1)
        def _():
            for o_ref, acc in zip(o_refs, acc_refs):
                if blocked is None:
                    o_ref[...] = acc[...].astype(BF16)
                else:
                    for j in range(o_ref.shape[0]):
                        o_ref[j] = acc[:, j * blocked:(j + 1) * blocked].astype(BF16)

    in_specs = [pl.BlockSpec((tt, ka), lambda t: (t, 0))]
    in_specs += [pl.BlockSpec((tt, p.shape[1]), lambda t: (t, 0)) for p in parts]
    if blocked is None:
        out_shape = [jax.ShapeDtypeStruct((ka, p.shape[1]), BF16) for p in parts]
        out_specs = [pl.BlockSpec((ka, p.shape[1]), lambda t: (0, 0)) for p in parts]
    else:
        out_shape = [jax.ShapeDtypeStruct((p.shape[1] // blocked, ka, blocked), BF16) for p in parts]
        out_specs = [pl.BlockSpec((p.shape[1] // blocked, ka, blocked), lambda t: (0, 0, 0)) for p in parts]
    return _pallas(body, name=name, grid=(nt,), in_specs=in_specs, out_specs=out_specs, out_shape=out_shape,
                   scratch_shapes=[pltpu.VMEM((ka, p.shape[1]), F32) for p in parts],
                   compiler_params=_cp(("arbitrary",), VMEM_BIG))(a, *parts)


def _dw_rows(name, rows_t, h):
    t_tok = h.shape[0]
    tt = 512

    def body(r_ref, h_ref, o_ref):
        @pl.when(pl.program_id(0) == 0)
        def _():
            o_ref[...] = jnp.zeros_like(o_ref)

        o_ref[...] += _nn(r_ref[...].astype(BF16), h_ref[...])

    return _pallas(body, name=name, grid=(t_tok // tt,),
                   in_specs=[pl.BlockSpec((8, tt), lambda t: (0, t)), pl.BlockSpec((tt, D_MODEL), lambda t: (t, 0))],
                   out_specs=pl.BlockSpec((8, D_MODEL), lambda t: (0, 0)),
                   out_shape=jax.ShapeDtypeStruct((8, D_MODEL), F32),
                   compiler_params=_cp(("arbitrary",), VMEM_MID))(rows_t, h)


def _lru_gates(xc, blk, wa_ref, wx_ref, ba_ref, bx_ref, sp):
    cols = slice(blk * LRU_BLOCK_W, (blk + 1) * LRU_BLOCK_W)
    xb = xc[:, cols].astype(BF16)
    r = _sigmoid(_nn(xb, wa_ref[blk].astype(BF16)) + ba_ref[:, cols])
    ig = _sigmoid(_nn(xb, wx_ref[blk].astype(BF16)) + bx_ref[:, cols])
    log_a = -LRU_C * r * sp[:, cols]
    a = jnp.exp(log_a)
    x2 = 2.0 * log_a
    series = -x2 * (1.0 + x2 * (0.5 + x2 * (1.0 / 6.0)))
    z = jnp.where(x2 > -0.01, series, 1.0 - a * a)
    mult = z * lax.rsqrt(jnp.maximum(z, 1e-30))
    return xb, r, ig, a, mult


def _softplus_neg(lam):
    return jnp.maximum(-lam, 0.0) + jnp.log(1.0 + jnp.exp(-jnp.abs(lam)))


def _conv_taps(xe_ref, cw_ref, cb_ref):
    xc = cb_ref[...] + xe_ref[8:8 + TC, :] * cw_ref[3:4, :]
    for k in range(1, 4):
        xc = xc + xe_ref[8 - k:8 - k + TC, :] * cw_ref[3 - k:4 - k, :]
    return xc


def _lru_fwd(proj, cw, cb, w_a, b_a, w_x, b_x, lam, seq):
    t_tok = proj.shape[0]
    nc = seq // TC

    def body(x_ref, cw_ref, cb_ref, wa_ref, ba_ref, wx_ref, bx_ref, lam_ref, hs_ref, xe_s, a_s, u_s, h_s):
        c = pl.program_id(1)

        @pl.when(c == 0)
        def _():
            xe_s[0:8, :] = jnp.zeros((8, D_MODEL), F32)
            h_s[...] = jnp.zeros_like(h_s)

        xe_s[8:8 + TC, :] = x_ref[...]
        xc = _conv_taps(xe_s, cw_ref, cb_ref)
        sp = _softplus_neg(lam_ref[...])
        for blk in range(LRU_BLOCKS):
            cols = slice(blk * LRU_BLOCK_W, (blk + 1) * LRU_BLOCK_W)
            _, _, ig, a, mult = _lru_gates(xc, blk, wa_ref, wx_ref, ba_ref, bx_ref, sp)
            a_s[:, cols] = a
            u_s[:, cols] = mult * ig * xc[:, cols]

        def step(t, h):
            h = a_s[pl.ds(t, 1), :] * h + u_s[pl.ds(t, 1), :]
            hs_ref[pl.ds(t, 1), :] = h
            return h

        h_s[0:1, :] = lax.fori_loop(0, TC, step, h_s[0:1, :], unroll=8)
        xe_s[0:8, :] = xe_s[TC:TC + 8, :]

    full = lambda shape: pl.BlockSpec(shape, lambda b, c: (0,) * len(shape))
    return _pallas(
        body, name="lru_fwd", grid=(t_tok // seq, nc),
        in_specs=[pl.BlockSpec((TC, D_MODEL), lambda b, c: (b * nc + c, 0)), full((4, D_MODEL)), full((1, D_MODEL)),
                  full((LRU_BLOCKS, LRU_BLOCK_W, LRU_BLOCK_W)), full((1, D_MODEL)),
                  full((LRU_BLOCKS, LRU_BLOCK_W, LRU_BLOCK_W)), full((1, D_MODEL)), full((1, D_MODEL))],
        out_specs=pl.BlockSpec((TC, D_MODEL), lambda b, c: (b * nc + c, 0)),
        out_shape=jax.ShapeDtypeStruct((t_tok, D_MODEL), F32),
        scratch_shapes=[pltpu.VMEM((TC + 8, D_MODEL), F32), pltpu.VMEM((TC, D_MODEL), F32),
                        pltpu.VMEM((TC, D_MODEL), F32), pltpu.VMEM((8, D_MODEL), F32)],
        compiler_params=_cp(("arbitrary", "arbitrary"), VMEM_BIG))(proj, cw, cb, w_a, b_a, w_x, b_x, lam)


def _lru_bwd(proj, hs, dyh, cw, cb, w_a, b_a, w_x, b_x, lam, seq):
    t_tok = proj.shape[0]
    nc = seq // TC

    def body(x_ref, xh_ref, g_ref, hs_ref, hh_ref, dy_ref, cw_ref, cb_ref, wa_ref, ba_ref, wx_ref, bx_ref, lam_ref,
             dp_ref, dcw_ref, dvec_ref, dwa_ref, dwx_ref,
             xe_s, he_s, de_s, a_s, r_s, i_s, m_s, dh_s, carry_s):
        b, cr = pl.program_id(0), pl.program_id(1)
        c = nc - 1 - cr

        @pl.when(jnp.logical_and(b == 0, cr == 0))
        def _():
            dcw_ref[...] = jnp.zeros_like(dcw_ref)
            dvec_ref[...] = jnp.zeros_like(dvec_ref)
            dwa_ref[...] = jnp.zeros_like(dwa_ref)
            dwx_ref[...] = jnp.zeros_like(dwx_ref)

        @pl.when(cr == 0)
        def _():
            carry_s[...] = jnp.zeros_like(carry_s)
            de_s[TC:TC + 8, :] = jnp.zeros((8, D_MODEL), F32)

        first = c == 0
        xe_s[0:8, :] = jnp.where(first, 0.0, xh_ref[...])
        xe_s[8:8 + TC, :] = x_ref[...]
        he_s[0:8, :] = jnp.where(first, 0.0, hh_ref[...])
        he_s[8:8 + TC, :] = hs_ref[...]
        xc = _conv_taps(xe_s, cw_ref, cb_ref)
        lam_v = lam_ref[...]
        sp = _softplus_neg(lam_v)
        for blk in range(LRU_BLOCKS):
            cols = slice(blk * LRU_BLOCK_W, (blk + 1) * LRU_BLOCK_W)
            _, r, ig, a, mult = _lru_gates(xc, blk, wa_ref, wx_ref, ba_ref, bx_ref, sp)
            a_s[:, cols], r_s[:, cols], i_s[:, cols], m_s[:, cols] = a, r, ig, mult

        gt = g_ref[...]
        dyh = dy_ref[...]
        dh_s[...] = dyh * _silu(gt)
        dp_ref[:, D_MODEL:] = (dyh * hs_ref[...] * _dsilu(gt)).astype(BF16)

        def step(k, carry):
            t = TC - 1 - k
            dh = dh_s[pl.ds(t, 1), :] + carry
            dh_s[pl.ds(t, 1), :] = dh
            return a_s[pl.ds(t, 1), :] * dh

        carry_s[0:1, :] = lax.fori_loop(0, TC, step, carry_s[0:1, :], unroll=8)

        hprev = he_s[7:7 + TC, :]
        for blk in range(LRU_BLOCKS):
            cols = slice(blk * LRU_BLOCK_W, (blk + 1) * LRU_BLOCK_W)
            xcb = xc[:, cols]
            a, r, ig, mult, dh = a_s[:, cols], r_s[:, cols], i_s[:, cols], m_s[:, cols], dh_s[:, cols]
            spb = sp[:, cols]
            dmult = dh * ig * xcb
            di = dh * mult * xcb
            dxc = dh * mult * ig
            dla = dh * hprev[:, cols] * a - dmult * (a * a) * lax.rsqrt(jnp.maximum(mult * mult, 1e-30))
            dr = dla * (-LRU_C * spb)
            dsp = jnp.sum(dla * (-LRU_C * r), axis=0, keepdims=True)
            dga = dr * r * (1.0 - r)
            dgx = di * ig * (1.0 - ig)
            dga_b, dgx_b = dga.astype(BF16), dgx.astype(BF16)
            xb = xcb.astype(BF16)
            dxc = dxc + _nt(dga_b, wa_ref[blk].astype(BF16)) + _nt(dgx_b, wx_ref[blk].astype(BF16))
            dwa_ref[blk] += _tn(xb, dga_b)
            dwx_ref[blk] += _tn(xb, dgx_b)
            dvec_ref[1:2, cols] += jnp.sum(dga, axis=0, keepdims=True)
            dvec_ref[2:3, cols] += jnp.sum(dgx, axis=0, keepdims=True)
            dvec_ref[3:4, cols] += dsp * (-1.0 / (1.0 + jnp.exp(lam_v[:, cols])))
            de_s[0:TC, cols] = dxc

        dxc = de_s[0:TC, :]
        dvec_ref[0:1, :] += jnp.sum(dxc, axis=0, keepdims=True)
        dxr = dxc * cw_ref[3:4, :]
        dcw_ref[3:4, :] += jnp.sum(dxc * xe_s[8:8 + TC, :], axis=0, keepdims=True)
        for k in range(1, 4):
            dxr = dxr + de_s[k:k + TC, :] * cw_ref[3 - k:4 - k, :]
            dcw_ref[3 - k:4 - k, :] += jnp.sum(dxc * xe_s[8 - k:8 - k + TC, :], axis=0, keepdims=True)
        dp_ref[:, :D_MODEL] = dxr.astype(BF16)
        de_s[TC:TC + 8, :] = de_s[0:8, :]

    chunk = lambda col: pl.BlockSpec((TC, D_MODEL), lambda b, cr: (b * nc + nc - 1 - cr, col))
    halo = lambda col: pl.BlockSpec(
        (8, D_MODEL), lambda b, cr: (jnp.maximum((b * nc + nc - 1 - cr) * (TC // 8) - 1, 0), col))
    full = lambda shape: pl.BlockSpec(shape, lambda b, cr: (0,) * len(shape))
    wblk = (LRU_BLOCKS, LRU_BLOCK_W, LRU_BLOCK_W)
    return _pallas(
        body, name="lru_bwd", grid=(t_tok // seq, nc),
        in_specs=[chunk(0), halo(0), chunk(1), chunk(0), halo(0), chunk(0),
                  full((4, D_MODEL)), full((1, D_MODEL)), full(wblk), full((1, D_MODEL)), full(wblk),
                  full((1, D_MODEL)), full((1, D_MODEL))],
        out_specs=[pl.BlockSpec((TC, 2 * D_MODEL), lambda b, cr: (b * nc + nc - 1 - cr, 0)),
                   full((8, D_MODEL)), full((8, D_MODEL)), full(wblk), full(wblk)],
        out_shape=[jax.ShapeDtypeStruct((t_tok, 2 * D_MODEL), BF16), jax.ShapeDtypeStruct((8, D_MODEL), F32),
                   jax.ShapeDtypeStruct((8, D_MODEL), F32), jax.ShapeDtypeStruct(wblk, F32),
                   jax.ShapeDtypeStruct(wblk, F32)],
        scratch_shapes=[pltpu.VMEM((TC + 8, D_MODEL), F32), pltpu.VMEM((TC + 8, D_MODEL), F32),
                        pltpu.VMEM((TC + 8, D_MODEL), F32)]
        + [pltpu.VMEM((TC, D_MODEL), F32)] * 5 + [pltpu.VMEM((8, D_MODEL), F32)],
        compiler_params=_cp(("arbitrary", "arbitrary"), VMEM_BIG),
    )(proj, proj, proj, hs, hs, dyh, cw, cb, w_a, b_a, w_x, b_x, lam)


def _last_layer_tail(hs, proj, w_out, w_out_t, x, gmod, final_g, target, seq):
    t_tok = x.shape[0]
    tiles_per_seq = seq // TM

    def body(hs_ref, g_ref, w_ref, wt_ref, x_ref, gm_ref, fg_ref, t_ref,
             yg_ref, dx_ref, dy_ref, dyg_ref, dgm_ref, loss_ref, dfg_ref):
        i = pl.program_id(0)

        @pl.when(i == 0)
        def _():
            loss_ref[...] = jnp.zeros_like(loss_ref)
            dfg_ref[...] = jnp.zeros_like(dfg_ref)

        @pl.when(i % tiles_per_seq == 0)
        def _():
            dgm_ref[...] = jnp.zeros_like(dgm_ref)

        gm = gm_ref[...]
        yg = (hs_ref[...] * _silu(g_ref[...])).astype(BF16)
        yg_ref[...] = yg
        y = _nn(yg, w_ref[...])
        xv = x_ref[...] + gm * y
        gv = fg_ref[...]
        rstd = lax.rsqrt(jnp.mean(xv * xv, axis=-1, keepdims=True) + EPS)
        xhat = xv * rstd
        err = xhat * gv - t_ref[...]
        loss_ref[0:1, :] += jnp.sum(err * err, axis=0, keepdims=True) * (0.5 / D_MODEL)
        dout = err * (1.0 / D_MODEL)
        dfg_ref[0:1, :] += jnp.sum(dout * xhat, axis=0, keepdims=True)
        dxhat = dout * gv
        dxv = rstd * (dxhat - xhat * jnp.mean(dxhat * xhat, axis=-1, keepdims=True))
        dx_ref[...] = dxv
        dgm_ref[...] += jnp.sum(dxv * y, axis=0, keepdims=True)
        dy = (dxv * gm).astype(BF16)
        dy_ref[...] = dy
        dyg_ref[...] = _nn(dy, wt_ref[...])

    row = pl.BlockSpec((TM, D_MODEL), lambda i: (i, 0))
    acc = pl.BlockSpec((8, D_MODEL), lambda i: (0, 0))
    mod_spec = pl.BlockSpec((None, 1, D_MODEL), lambda i: (i * TM // seq, 0, 0))
    return _pallas(
        body, name="last_layer_tail", grid=(t_tok // TM,),
        in_specs=[row, pl.BlockSpec((TM, D_MODEL), lambda i: (i, 1)), pl.BlockSpec((D_MODEL, D_MODEL), lambda i: (0, 0)),
                  pl.BlockSpec((D_MODEL, D_MODEL), lambda i: (0, 0)),
                  row, mod_spec, pl.BlockSpec((1, D_MODEL), lambda i: (0, 0)), row],
        out_specs=[row, row, row, row, mod_spec, acc, acc],
        out_shape=[jax.ShapeDtypeStruct((t_tok, D_MODEL), BF16), jax.ShapeDtypeStruct((t_tok, D_MODEL), F32),
                   jax.ShapeDtypeStruct((t_tok, D_MODEL), BF16), jax.ShapeDtypeStruct((t_tok, D_MODEL), F32),
                   jax.ShapeDtypeStruct(gmod.shape, F32), jax.ShapeDtypeStruct((8, D_MODEL), F32),
                   jax.ShapeDtypeStruct((8, D_MODEL), F32)],
        compiler_params=_cp(("arbitrary",), VMEM_BIG))(hs, proj, w_out, w_out_t, x, gmod, final_g, target)


def _final_loss(x, g, target):
    t_tok = x.shape[0]

    def body(x_ref, g_ref, t_ref, dx_ref, loss_ref, dg_ref):
        @pl.when(pl.program_id(0) == 0)
        def _():
            loss_ref[...] = jnp.zeros_like(loss_ref)
            dg_ref[...] = jnp.zeros_like(dg_ref)

        xv = x_ref[...]
        gv = g_ref[...]
        rstd = lax.rsqrt(jnp.mean(xv * xv, axis=-1, keepdims=True) + EPS)
        xhat = xv * rstd
        err = xhat * gv - t_ref[...]
        loss_ref[0:1, :] += jnp.sum(err * err, axis=0, keepdims=True) * (0.5 / D_MODEL)
        dout = err * (1.0 / D_MODEL)
        dg_ref[0:1, :] += jnp.sum(dout * xhat, axis=0, keepdims=True)
        dxhat = dout * gv
        dx_ref[...] = rstd * (dxhat - xhat * jnp.mean(dxhat * xhat, axis=-1, keepdims=True))

    row = pl.BlockSpec((TM, D_MODEL), lambda i: (i, 0))
    acc = pl.BlockSpec((8, D_MODEL), lambda i: (0, 0))
    return _pallas(body, name="final_loss", grid=(t_tok // TM,),
                   in_specs=[row, pl.BlockSpec((1, D_MODEL), lambda i: (0, 0)), row],
                   out_specs=[row, acc, acc],
                   out_shape=[jax.ShapeDtypeStruct((t_tok, D_MODEL), F32)] + [jax.ShapeDtypeStruct((8, D_MODEL), F32)] * 2,
                   compiler_params=_cp(("arbitrary",), VMEM_MID))(x, g, target)


def _adam_math(w, g, m, v):
    m_new = ADAM_B1 * m + (1.0 - ADAM_B1) * g
    v_new = ADAM_B2 * v + (1.0 - ADAM_B2) * (g * g)
    m_hat = m_new / (1.0 - ADAM_B1 ** ADAM_STEP)
    v_hat = v_new / (1.0 - ADAM_B2 ** ADAM_STEP)
    delta = -ADAM_LR * (m_hat / (jnp.sqrt(v_hat) + ADAM_EPS) + ADAM_WD * w)
    return delta, m_new, v_new


def _sum_leading(name, x, out_dtype=F32):
    n, rows, cols = x.shape
    tr = PACK_ROWS if rows % PACK_ROWS == 0 else rows

    def body(x_ref, o_ref):
        acc = x_ref[0].astype(F32)
        for d in range(1, n):
            acc = acc + x_ref[d].astype(F32)
        o_ref[...] = acc.astype(out_dtype)

    return _pallas(body, name=name, grid=(rows // tr,),
                   in_specs=[pl.BlockSpec((n, tr, cols), lambda i: (0, i, 0))],
                   out_specs=pl.BlockSpec((tr, cols), lambda i: (i, 0)),
                   out_shape=jax.ShapeDtypeStruct((rows, cols), out_dtype),
                   compiler_params=_cp(("arbitrary",), VMEM_MID))(x)


def _adamw(name, w, m, v, g=None, parts=None):
    rows, cols = w.shape
    tr = rows if rows <= 256 else 256

    def body(*refs):
        w_ref, m_ref, v_ref, g_in, g_ref, d_ref, mo_ref, vo_ref = refs
        if parts is None:
            gv = g_in[...]
        else:
            acc = g_in[0].astype(F32)
            for d in range(1, parts.shape[0]):
                acc = acc + g_in[d].astype(F32)
            gv = acc[:, :cols]
        delta, m_new, v_new = _adam_math(w_ref[...], gv, m_ref[...], v_ref[...])
        g_ref[...] = gv
        d_ref[...] = delta
        mo_ref[...] = m_new
        vo_ref[...] = v_new

    row = pl.BlockSpec((tr, cols), lambda i: (i, 0))
    if parts is None:
        g_spec, g_arg = row, g
    else:
        g_spec, g_arg = pl.BlockSpec((parts.shape[0], tr, parts.shape[2]), lambda i: (0, i, 0)), parts
    return _pallas(body, name=name, grid=(rows // tr,), in_specs=[row, row, row, g_spec], out_specs=[row] * 4,
                   out_shape=[jax.ShapeDtypeStruct((rows, cols), F32)] * 4,
                   compiler_params=_cp(("arbitrary",), VMEM_MID))(w, m, v, g_arg)


def _adamw_many(name, groups):
    ntens = len(groups)

    def body(*refs):
        ins, outs = refs[:4 * ntens], refs[4 * ntens:]
        for k in range(ntens):
            w_ref, m_ref, v_ref, g_ref = ins[4 * k:4 * k + 4]
            gv = g_ref[...]
            delta, m_new, v_new = _adam_math(w_ref[...], gv, m_ref[...], v_ref[...])
            for o_ref, val in zip(outs[4 * k:4 * k + 4], (gv, delta, m_new, v_new)):
                o_ref[...] = val

    flat = [a for grp in groups for a in grp]
    out_shape = [jax.ShapeDtypeStruct(grp[0].shape, F32) for grp in groups for _ in range(4)]
    outs = _pallas(body, name=name, out_shape=out_shape, compiler_params=_cp(vmem=VMEM_MID))(*flat)
    return [tuple(outs[4 * k:4 * k + 4]) for k in range(ntens)]


def _pack_rows(arrs):
    rows, meta, total = [], [], 0
    for a in arrs:
        flat = a.reshape(-1)
        nrow = -(-flat.shape[0] // 1024) * 8
        rows.append(jnp.pad(flat, (0, nrow * 128 - flat.shape[0])).reshape(nrow, 128))
        meta.append((a.shape, flat.shape[0], nrow))
        total += nrow
    tail = -total % PACK_ROWS
    if tail:
        rows.append(jnp.zeros((tail, 128), F32))
    return jnp.concatenate(rows, axis=0), meta


def _unpack_rows(packed, meta):
    out, r0 = [], 0
    for shape, size, nrow in meta:
        out.append(packed[r0:r0 + nrow].reshape(-1)[:size].reshape(shape))
        r0 += nrow
    return out


WEIGHTS = ["rel_bias", "norm_g", "ada_w", "ada_b", "attn_w_in", "attn_sinks", "attn_b_f", "attn_w_out", "lru_w_in",
           "lru_conv_w", "lru_conv_b", "lru_w_a", "lru_b_a", "lru_w_x", "lru_b_x", "lru_lambda", "lru_w_out", "final_g"]
BIG = ["ada_w", "attn_w_in", "attn_w_out", "lru_w_in", "lru_w_out"]
PACK_ROWS = 256


def kernel(x, c, rel_bias, norm_g, ada_w, ada_b, attn_w_in, attn_sinks, attn_b_f, attn_w_out, lru_w_in, lru_conv_w, lru_conv_b, lru_w_a, lru_b_a, lru_w_x, lru_b_x, lru_lambda, lru_w_out, final_g, loss_target, m_rel_bias, m_norm_g, m_ada_w, m_ada_b, m_attn_w_in, m_attn_sinks, m_attn_b_f, m_attn_w_out, m_lru_w_in, m_lru_conv_w, m_lru_conv_b, m_lru_w_a, m_lru_b_a, m_lru_w_x, m_lru_b_x, m_lru_lambda, m_lru_w_out, m_final_g, v_rel_bias, v_norm_g, v_ada_w, v_ada_b, v_attn_w_in, v_attn_sinks, v_attn_b_f, v_attn_w_out, v_lru_w_in, v_lru_conv_w, v_lru_conv_b, v_lru_w_a, v_lru_b_a, v_lru_w_x, v_lru_b_x, v_lru_lambda, v_lru_w_out, v_final_g):
    nseq, seq, _ = x.shape
    t_tok = nseq * seq
    me = 4 * lax.axis_index("x") + 2 * lax.axis_index("y") + lax.axis_index("c")
    x0 = x.reshape(t_tok, D_MODEL)
    target = loss_target.reshape(t_tok, D_MODEL)

    w_in_pad = jnp.pad(attn_w_in[0].astype(BF16), ((0, 0), (0, SHARD_W_PAD - SHARD_W_IN)))
    vec_shard = jnp.concatenate([lru_conv_w[0], lru_conv_b, lru_b_a, lru_b_x, lru_lambda], axis=0)
    g_w_in, g_vec, g_c = _exchange("gather_first", [w_in_pad, vec_shard, c], [])
    later_w = [attn_w_out[0].astype(BF16), lru_w_in[0].astype(BF16), lru_w_out[0].astype(BF16)]
    later_handle, later_token = _exchange_start("gather_later_start", later_w, [], after=g_vec)
    w_full = jnp.transpose(g_w_in[:, :, :SHARD_W_IN], (1, 0, 2)).reshape(D_MODEL, N_DEV * SHARD_W_IN)
    w_aq, w_ak, w_av = w_full[:, 0:512], w_full[:, 512:640], w_full[:, 640:768]
    w_bq, w_bk, w_bv = w_full[:, 768:1280], w_full[:, 1280:1792], w_full[:, 1792:2304]
    w_f, w_gate = w_full[:, 2304:2312], w_full[:, 2312:3336]
    w_main = jnp.concatenate([w_bq, w_bk, w_bv, w_aq, w_gate, w_ak, w_av], axis=1)
    wf_t = jnp.transpose(w_f)
    vec_full = jnp.transpose(g_vec, (1, 0, 2)).reshape(8, D_MODEL)
    conv_w, conv_b, b_a, b_x, lam = vec_full[0:4], vec_full[4:5], vec_full[5:6], vec_full[6:7], vec_full[7:8]
    c_all = g_c.reshape(N_DEV * nseq, D_MODEL)

    ncol = ada_w.shape[2]
    ada_b_slice = lax.dynamic_slice(ada_b.reshape(2, N_DEV, ncol), (0, me, 0), (2, 1, ncol))
    mod_part = _ada_mod(c_all, ada_w, ada_b_slice)
    (g_mod,) = _exchange("gather_mod", [mod_part], [])
    mine = lax.dynamic_slice(g_mod, (0, 0, me * nseq, 0), (N_DEV, 2, nseq, ncol))
    mod = jnp.transpose(mine, (1, 2, 0, 3)).reshape(2, nseq, 3 * D_MODEL)
    shift = [mod[l, :, 0:D_MODEL].reshape(nseq, 1, D_MODEL) for l in range(2)]
    scale = [mod[l, :, D_MODEL:2 * D_MODEL].reshape(nseq, 1, D_MODEL) for l in range(2)]
    gmod = [mod[l, :, 2 * D_MODEL:].reshape(nseq, 1, D_MODEL) for l in range(2)]

    onehot = _bucket_onehot()
    bias = _bias_expand(jnp.transpose(rel_bias), onehot).reshape(N_HEADS, BLOCK, 2 * BLOCK)
    sinks = attn_sinks.reshape(N_HEADS)
    b_f = attn_b_f.reshape(N_HEADS, 1)
    norm_g0 = norm_g[0:1] + later_token[0:1, 0:1]
    h0, qkvg, fl_t = _norm_proj("norm_proj0", x0, norm_g0, shift[0], scale[0], w_main, seq, BF16, wf_t=wf_t)
    f_row, f_col = _fox_prep(fl_t, b_f, seq)
    a_out, lse_a = _swa_fwd(qkvg, bias, sinks, seq)
    q_aug, k_aug, kt_aug, vt = _fox_aug(qkvg, f_col, seq)
    b_out, lse_b = _fox_fwd_t(q_aug, k_aug, vt, seq)
    g_later = _exchange_wait("gather_later_wait", later_handle, after=lse_b)
    w_out0, g_lru_in, w_out1 = (_with_own(g, w, me) for g, w in zip(g_later, later_w))
    w_out0, w_out1 = w_out0.reshape(D_MODEL, D_MODEL), w_out1.reshape(D_MODEL, D_MODEL)
    w_out0_t, w_out1_t, w_main_t = jnp.transpose(w_out0), jnp.transpose(w_out1), jnp.transpose(w_main)
    lru_in_t = jnp.transpose(g_lru_in, (0, 2, 1)).reshape(2 * D_MODEL, D_MODEL)
    yg0, y0, x1 = _out_proj("out_proj0", [a_out, b_out], qkvg, C_GATE // D_MODEL, w_out0, x0, gmod[0], seq)

    h1, proj1 = _norm_proj("norm_proj1", x1, norm_g[1:2], shift[1], scale[1], g_lru_in, seq, F32)
    hs = _lru_fwd(proj1, conv_w, conv_b, lru_w_a[0], b_a, lru_w_x[0], b_x, lam, seq)

    yg1, dx2, dy1, dyh, dgm1, loss_rows, dfinal_rows = _last_layer_tail(
        hs, proj1, w_out1, w_out1_t, x1, gmod[1], final_g.reshape(1, D_MODEL), target, seq)

    dproj1, dcw, dvec, dw_a, dw_x = _lru_bwd(proj1, hs, dyh, conv_w, conv_b, lru_w_a[0], b_a, lru_w_x[0], b_x, lam, seq)
    dx1, dss1, dg1 = _norm_bwd("norm1_bwd", [(dproj1, 0)], lru_in_t, x1, norm_g[1:2], scale[1], dx2, seq)
    (p_w_out1,) = _dw("dw_out1", yg1, [dy1])
    (p_lru_in,) = _dw("dw_lru_in", h1, [dproj1], blocked=2 * D_MODEL // N_DEV)

    rows_out = D_MODEL // N_DEV
    gpack1, gmeta1 = _pack_rows([dcw[0:4], dvec[0:4], dg1[0], dfinal_rows[0]])
    dwax = jnp.stack([dw_a, dw_x]).astype(BF16)
    own1 = [gpack1, dwax, p_lru_in, p_w_out1.reshape(N_DEV, rows_out, D_MODEL)]
    grads1_handle, grads1_token = _exchange_start("grads1_start", own1[:2], own1[2:], after=dx1)

    gmod0 = gmod[0] + grads1_token[0:1, 0:1]
    dy0, dgm0, du_a, du_b, dgate = _out_proj_bwd("out_proj0_bwd", dx1, gmod0, y0, w_out0_t, seq,
                                                  attn=(a_out, b_out, qkvg))
    dq_a, dkv_a, dbias, dsink = _swa_bwd(qkvg, du_a, a_out, lse_a, bias, sinks, seq)
    dq_b, dk_b, dv_b, df4 = _fox_bwd_t(q_aug, k_aug, kt_aug, qkvg, du_b, b_out, lse_b, seq)
    dfl_t, db_f = _fox_post(df4.reshape(N_HEADS, t_tok), fl_t, b_f, seq)
    parts0 = [(dq_b, C_BQ), (dk_b, C_BK), (dv_b, C_BV), (dq_a, C_AQ), (dgate, C_GATE), (dkv_a, C_AK)]
    (p_w_out0,) = _dw("dw_out0", yg0, [dy0])
    pw_bq, pw_bk, pw_bv, pw_aq, pw_gate, pw_akv = _dw("dw_attn_in", h0, [p for p, _ in parts0])
    pw_f = _dw_rows("dw_f", dfl_t, h0)

    p_w_in = jnp.concatenate([pw_aq, pw_akv, pw_bq, pw_bk, pw_bv, jnp.transpose(pw_f).astype(BF16), pw_gate], axis=1)
    p_w_in = jnp.transpose(p_w_in.reshape(D_MODEL, N_DEV, SHARD_W_IN), (1, 0, 2))
    p_w_in = jnp.pad(p_w_in, ((0, 0), (0, 0), (0, SHARD_W_PAD - SHARD_W_IN)))
    own0 = [p_w_in, p_w_out0.reshape(N_DEV, rows_out, D_MODEL)]
    landed1 = _exchange_wait("grads1_wait", grads1_handle, after=p_w_in)
    grads0_handle, grads0_token = _exchange_start("grads0_start", [], own0, after=landed1[0])
    scale0 = scale[0] + grads0_token[0:1, 0:1]
    dx0, dss0, dg0 = _norm_bwd("norm0_bwd", parts0, w_main_t, x0, norm_g[0:1], scale0, dx1, seq,
                               rows_part=(dfl_t, wf_t))
    dbias_t = _bias_reduce(dbias.reshape(N_HEADS, BLOCK * 2 * BLOCK), onehot)

    gpack0, gmeta0 = _pack_rows([jnp.transpose(dbias_t), dg0[0], dsink[:, 0], db_f[:, 0], loss_rows[0]])
    dmod = jnp.stack([jnp.concatenate([dss[:, 0], dss[:, 1], dgm[:, 0]], axis=1)
                      for dss, dgm in ((dss0, dgm0), (dss1, dgm1))], axis=1)
    g_small0, g_dmod = _exchange("exchange_small", [gpack0, dmod], [])
    landed0 = _exchange_wait("grads0_wait", grads0_handle, after=g_small0)
    r_w_in, r_w_out0 = (_with_own(g, lax.dynamic_index_in_dim(a, me, 0, keepdims=False), me)
                        for g, a in zip(landed0, own0))
    g_small1, g_dwax = (_with_own(g, a, me) for g, a in zip(landed1[:2], own1[:2]))
    r_lru_in, r_w_out1 = (_with_own(g, lax.dynamic_index_in_dim(a, me, 0, keepdims=False), me)
                          for g, a in zip(landed1[2:], own1[2:]))

    d_rel, d_g0, d_sinks, d_b_f, loss_cols = _unpack_rows(_sum_leading("sum_small0", g_small0), gmeta0)
    loss = jnp.sum(loss_cols)
    d_cw, d_vec, d_g1, d_final_g = _unpack_rows(_sum_leading("sum_small1", g_small1), gmeta1)
    d_norm_g = jnp.stack([d_g0, d_g1])
    d_wax = _sum_leading("sum_dwax", g_dwax.reshape(N_DEV, 2 * LRU_BLOCKS * LRU_BLOCK_W, LRU_BLOCK_W))
    d_wa, d_wx = d_wax[:LRU_BLOCKS * LRU_BLOCK_W], d_wax[LRU_BLOCKS * LRU_BLOCK_W:]
    cols = lambda a: lax.dynamic_slice(a, (0, me * LRU_BLOCK_W), (a.shape[0], LRU_BLOCK_W))
    dmod_all = g_dmod.reshape(N_DEV * nseq, 2 * 3 * D_MODEL)
    d_ada_b = _sum_leading("sum_ada_b", dmod_all.reshape(N_DEV * nseq, 2 * 3 * D_MODEL // 128, 128)).reshape(2, 3 * D_MODEL)
    dmod_slice = lax.dynamic_slice(dmod_all.reshape(N_DEV * nseq, 2, N_DEV, ncol), (0, 0, me, 0),
                                   (N_DEV * nseq, 2, 1, ncol)).reshape(N_DEV * nseq, 2, ncol)
    d_ada_w = _ada_w_grad(c_all, jnp.transpose(dmod_slice, (1, 0, 2)))

    given = dict(
        rel_bias=(rel_bias, m_rel_bias, v_rel_bias), norm_g=(norm_g, m_norm_g, v_norm_g),
        ada_w=(ada_w, m_ada_w, v_ada_w), ada_b=(ada_b, m_ada_b, v_ada_b),
        attn_w_in=(attn_w_in, m_attn_w_in, v_attn_w_in), attn_sinks=(attn_sinks, m_attn_sinks, v_attn_sinks),
        attn_b_f=(attn_b_f, m_attn_b_f, v_attn_b_f), attn_w_out=(attn_w_out, m_attn_w_out, v_attn_w_out),
        lru_w_in=(lru_w_in, m_lru_w_in, v_lru_w_in), lru_conv_w=(lru_conv_w, m_lru_conv_w, v_lru_conv_w),
        lru_conv_b=(lru_conv_b, m_lru_conv_b, v_lru_conv_b), lru_w_a=(lru_w_a, m_lru_w_a, v_lru_w_a),
        lru_b_a=(lru_b_a, m_lru_b_a, v_lru_b_a), lru_w_x=(lru_w_x, m_lru_w_x, v_lru_w_x),
        lru_b_x=(lru_b_x, m_lru_b_x, v_lru_b_x), lru_lambda=(lru_lambda, m_lru_lambda, v_lru_lambda),
        lru_w_out=(lru_w_out, m_lru_w_out, v_lru_w_out), final_g=(final_g, m_final_g, v_final_g))
    results = {}

    def big(name, shape2d, g=None, parts=None):
        w, m, v = (a.reshape(shape2d) for a in given[name])
        outs = _adamw("adamw_" + name, w, m, v, g=g, parts=parts)
        results[name] = tuple(o.reshape(given[name][0].shape) for o in outs)

    big("ada_w", (2 * D_MODEL, ncol), g=d_ada_w.reshape(2 * D_MODEL, ncol))
    big("attn_w_in", (D_MODEL, SHARD_W_IN), parts=r_w_in)
    big("attn_w_out", (rows_out, D_MODEL), parts=r_w_out0)
    big("lru_w_in", (D_MODEL, 2 * D_MODEL // N_DEV), parts=r_lru_in)
    big("lru_w_out", (rows_out, D_MODEL), parts=r_w_out1)

    small_grads = dict(
        rel_bias=d_rel, norm_g=d_norm_g, ada_b=d_ada_b, attn_sinks=d_sinks.reshape(1, N_HEADS),
        attn_b_f=d_b_f.reshape(1, N_HEADS), lru_conv_w=cols(d_cw).reshape(1, 4, LRU_BLOCK_W),
        lru_conv_b=cols(d_vec[0:1]), lru_w_a=d_wa.reshape(lru_w_a.shape), lru_b_a=cols(d_vec[1:2]),
        lru_w_x=d_wx.reshape(lru_w_x.shape), lru_b_x=cols(d_vec[2:3]), lru_lambda=cols(d_vec[3:4]),
        final_g=d_final_g)
    small = [n for n in WEIGHTS if n not in BIG]
    as2d = lambda a: a.reshape(-1, a.shape[-1])
    outs = _adamw_many("adamw_small", [tuple(as2d(a) for a in given[n]) + (as2d(small_grads[n]),) for n in small])
    for n, group in zip(small, outs):
        results[n] = tuple(o.reshape(given[n][0].shape) for o in group)

    grad_x = dx0.reshape(x.shape)
    out = [loss, grad_x]
    for j in range(4):
        out += [results[n][j] for n in WEIGHTS]
    return tuple(out)
```

```python
import functools
import math

import jax
import jax.numpy as jnp
from jax import lax
from jax.experimental import pallas as pl
from jax.experimental.pallas import tpu as pltpu

F32 = jnp.float32
BF16 = jnp.bfloat16
HI = lax.Precision.HIGHEST
MESH = pl.DeviceIdType.MESH

N_DEV = 8
D_MODEL = 1024
HEAD_DIM = 64
N_HEADS = 8
KV_GROUP = 4
BLOCK = 128
REL_BUCKETS = 32
REL_MAX_EXACT = 16
REL_MAX_DIST = 128
LRU_BLOCKS = 8
LRU_BLOCK_W = 128
LRU_C = 8.0
EPS = 1e-6
SCALE = HEAD_DIM ** -0.5
NEG = -1e30

ADAM_LR = 0.001
ADAM_B1 = 0.9
ADAM_B2 = 0.999
ADAM_EPS = 1e-08
ADAM_WD = 0.01
ADAM_STEP = 10

C_BQ, C_BK, C_BV, C_AQ, C_GATE, C_AK, C_AV = 0, 512, 1024, 1536, 2048, 3072, 3200
N_MAIN = 3328
SHARD_W_IN = 417
SHARD_W_PAD = 512

TM = 512
TQ = 256
TK = 128
TKB = 256
TC = 512
SWA_SUB = 2
VMEM_BIG = 56 * 1024 * 1024
VMEM_MID = 40 * 1024 * 1024


def _pallas(body, **kw):
    return pl.pallas_call(body, **kw)


def _cp(sem=None, vmem=None):
    kw = {}
    if sem is not None:
        kw["dimension_semantics"] = sem
    if vmem is not None:
        kw["vmem_limit_bytes"] = vmem
    return pltpu.CompilerParams(**kw)


def _nn(a, b, precision=None):
    return jnp.dot(a, b, preferred_element_type=F32, precision=precision)


def _nt(a, b, precision=None):
    return lax.dot_general(a, b, (((1,), (1,)), ((), ())), preferred_element_type=F32, precision=precision)


def _tn(a, b, precision=None):
    return lax.dot_general(a, b, (((0,), (0,)), ((), ())), preferred_element_type=F32, precision=precision)


def _sigmoid(x):
    return 1.0 / (1.0 + jnp.exp(-x))


def _silu(x):
    return x * _sigmoid(x)


def _dsilu(x):
    s = _sigmoid(x)
    return s * (1.0 + x * (1.0 - s))


def _neg_expm1(x):
    poly = x * (1.0 + x * (0.5 + x * (1.0 / 6.0 + x * (1.0 / 24.0))))
    return -jnp.where(jnp.abs(x) < 0.05, poly, jnp.exp(x) - 1.0)


def _col(tile, idx):
    lane = lax.broadcasted_iota(jnp.int32, tile.shape, 1)
    return jnp.sum(jnp.where(lane == idx, tile, 0.0), axis=1, keepdims=True)


def _row(tile, idx):
    sub = lax.broadcasted_iota(jnp.int32, tile.shape, 0)
    return jnp.sum(jnp.where(sub == idx, tile, 0.0), axis=0, keepdims=True)


def _exchange(name, gathers, scatters, axes=("x", "y", "c"), chunks=1):
    ng, n = len(gathers), len(gathers) + len(scatters)
    ins = list(gathers) + list(scatters)
    group = 2 ** len(axes)

    def body(*refs):
        in_refs, out_refs = refs[:n], refs[n:2 * n]
        send_sems, recv_sems, loc_sems = refs[2 * n:]
        coord = {a: lax.axis_index(a) for a in ("x", "y", "c")}

        def member(r):
            pc = dict(coord)
            idx = 0
            for k, a in enumerate(axes):
                if r & (1 << (len(axes) - 1 - k)):
                    pc[a] = 1 - coord[a]
                idx = 2 * idx + pc[a]
            return (pc["x"], pc["y"], pc["c"]), idx

        _, me = member(0)

        def peer(r):
            return member(r)

        local, sends, recvs = [], [], []
        for k in range(n):
            mine = in_refs[k] if k < ng else in_refs[k].at[me]
            cp = pltpu.make_async_copy(mine, out_refs[k].at[me], loc_sems.at[k])
            cp.start()
            local.append(cp)
            lead = mine.shape[0]
            nchunk = max(q for q in range(1, chunks + 1) if lead % q == 0)
            step = lead // nchunk
            for r in range(1, group):
                pid, pidx = peer(r)
                src = in_refs[k] if k < ng else in_refs[k].at[pidx]
                for q in range(nchunk):
                    rows = pl.ds(q * step, step)
                    sems = dict(send_sem=send_sems.at[r - 1, k, q], recv_sem=recv_sems.at[r - 1, k, q],
                                device_id=pid, device_id_type=MESH)
                    snd = pltpu.make_async_remote_copy(src_ref=src.at[rows], dst_ref=out_refs[k].at[me].at[rows], **sems)
                    snd.start()
                    sends.append(snd)
                    recvs.append(pltpu.make_async_remote_copy(
                        src_ref=src.at[rows], dst_ref=out_refs[k].at[pidx].at[rows], **sems))
        for rc in recvs:
            rc.wait_recv()
        for snd in sends:
            snd.wait_send()
        for cp in local:
            cp.wait()

    out_shape = [jax.ShapeDtypeStruct((group,) + a.shape, a.dtype) for a in gathers]
    out_shape += [jax.ShapeDtypeStruct(a.shape, a.dtype) for a in scatters]
    any_spec = pl.BlockSpec(memory_space=pl.ANY)
    return _pallas(
        body, name=name, out_shape=out_shape,
        in_specs=[any_spec] * n, out_specs=[any_spec] * n,
        scratch_shapes=[pltpu.SemaphoreType.DMA((group - 1, n, chunks)), pltpu.SemaphoreType.DMA((group - 1, n, chunks)),
                        pltpu.SemaphoreType.DMA((n,))],
    )(*ins)


def _peer_of(r):
    x, y, c = lax.axis_index("x"), lax.axis_index("y"), lax.axis_index("c")
    px = 1 - x if r & 4 else x
    py = 1 - y if r & 2 else y
    pc = 1 - c if r & 1 else c
    return (px, py, pc), 4 * px + 2 * py + pc


def _split_copies(in_refs, land_refs, send_sems, recv_sems, ng, with_recv):
    _, me = _peer_of(0)
    pairs = []
    for k, (src_ref, land) in enumerate(zip(in_refs, land_refs)):
        for r in range(1, N_DEV):
            pid, pidx = _peer_of(r)
            src = src_ref if k < ng else src_ref.at[pidx]
            slot = (N_DEV - 1) * k + r - 1
            sems = dict(send_sem=send_sems.at[slot], recv_sem=recv_sems.at[slot], device_id=pid, device_id_type=MESH)
            send = pltpu.make_async_remote_copy(src_ref=src, dst_ref=land.at[me], **sems)
            recv = pltpu.make_async_remote_copy(src_ref=src, dst_ref=land.at[pidx], **sems) if with_recv else None
            pairs.append((send, recv))
    return pairs


def _exchange_start(name, gathers, scatters, after):
    ng, n = len(gathers), len(gathers) + len(scatters)
    ins = list(gathers) + list(scatters)
    lands = [jax.ShapeDtypeStruct((N_DEV,) + a.shape, a.dtype) for a in gathers]
    lands += [jax.ShapeDtypeStruct(a.shape, a.dtype) for a in scatters]

    def body(*refs):
        in_refs, land_refs = refs[:n], refs[n:2 * n]
        send_sems, recv_sems = refs[2 * n + 1:2 * n + 3]
        token = refs[-1]
        for send, _ in _split_copies(in_refs, land_refs, send_sems, recv_sems, ng, False):
            send.start()
        token[...] = jnp.zeros_like(token)

    hbm = pl.BlockSpec(memory_space=pltpu.HBM)
    sem = pl.BlockSpec(memory_space=pltpu.SEMAPHORE)
    sem_shape = pltpu.SemaphoreType.DMA(((N_DEV - 1) * n,))
    out_shape = [sem_shape, sem_shape] + [pltpu.HBM(a.shape, a.dtype) for a in ins]
    out_shape += [pltpu.HBM(l.shape, l.dtype) for l in lands] + [jax.ShapeDtypeStruct((8, 128), F32)]
    args = [pltpu.with_memory_space_constraint(a, pltpu.HBM) for a in ins]
    args += [pltpu.with_memory_space_constraint(lax.empty(l.shape, l.dtype), pltpu.HBM) for l in lands]
    outs = _pallas(
        body, name=name, out_shape=out_shape,
        in_specs=[hbm] * (2 * n) + [pl.BlockSpec(memory_space=pl.ANY)],
        out_specs=[sem, sem] + [hbm] * (2 * n) + [pl.BlockSpec(memory_space=pltpu.VMEM)],
        input_output_aliases={i: 2 + i for i in range(2 * n)},
        compiler_params=pltpu.CompilerParams(has_side_effects=pltpu.SideEffectType.DATAFLOW_SIDE_EFFECTING),
    )(*args, after)
    return (outs[0], outs[1], list(outs[2:2 + n]), list(outs[2 + n:2 + 2 * n]), ng), outs[-1]


def _exchange_wait(name, handle, after):
    send_sems, recv_sems, srcs, lands, ng = handle
    n = len(srcs)

    def body(*refs):
        in_refs, land_refs = refs[:n], refs[n:2 * n]
        send_ref, recv_ref = refs[2 * n:2 * n + 2]
        for send, recv in _split_copies(in_refs, land_refs, send_ref, recv_ref, ng, True):
            send.wait_send()
            recv.wait_recv()

    hbm = pl.BlockSpec(memory_space=pltpu.HBM)
    sem = pl.BlockSpec(memory_space=pltpu.SEMAPHORE)
    outs = _pallas(
        body, name=name, out_shape=[pltpu.HBM(a.shape, a.dtype) for a in srcs + lands],
        in_specs=[hbm] * (2 * n) + [sem, sem, pl.BlockSpec(memory_space=pl.ANY)],
        out_specs=[hbm] * (2 * n), input_output_aliases={i: i for i in range(2 * n)},
        compiler_params=pltpu.CompilerParams(has_side_effects=pltpu.SideEffectType.DATAFLOW_SIDE_EFFECTING),
    )(*srcs, *lands, send_sems, recv_sems, after)
    return list(outs[n:])


def _with_own(land, own, me):
    return lax.dynamic_update_slice(land, own[None], (me,) + (0,) * own.ndim)


def _ada_mod(c_all, ada_w, ada_b_slice):
    def body(c_ref, w_ref, b_ref, o_ref):
        ca = _silu(c_ref[...])
        for l in range(2):
            o_ref[l] = _nn(ca, w_ref[l], HI) + b_ref[l]

    return _pallas(body, name="ada_mod",
                   out_shape=jax.ShapeDtypeStruct((2, c_all.shape[0], ada_w.shape[2]), F32),
                   compiler_params=_cp(vmem=VMEM_MID))(c_all, ada_w, ada_b_slice)


def _ada_w_grad(c_all, dmod_slice):
    def body(c_ref, d_ref, o_ref):
        ca = _silu(c_ref[...])
        for l in range(2):
            o_ref[l] = _tn(ca, d_ref[l], HI)

    return _pallas(body, name="ada_w_grad",
                   out_shape=jax.ShapeDtypeStruct((2, D_MODEL, dmod_slice.shape[2]), F32),
                   compiler_params=_cp(vmem=VMEM_MID))(c_all, dmod_slice)


def _bucket_onehot():
    qi = jnp.arange(BLOCK)[:, None]
    kj = jnp.arange(2 * BLOCK)[None, :]
    rel = qi - kj + BLOCK
    n = jnp.maximum(rel, 0)
    nf = jnp.maximum(n, 1).astype(F32)
    large = REL_MAX_EXACT + (jnp.log(nf / REL_MAX_EXACT) / math.log(REL_MAX_DIST / REL_MAX_EXACT)
                             * (REL_BUCKETS - REL_MAX_EXACT)).astype(jnp.int32)
    large = jnp.minimum(large, REL_BUCKETS - 1)
    bucket = jnp.where(n < REL_MAX_EXACT, n, large).reshape(1, BLOCK * 2 * BLOCK)
    return (jnp.arange(REL_BUCKETS)[:, None] == bucket).astype(F32)


def _bias_expand(rel_bias_t, onehot):
    def body(r_ref, e_ref, o_ref):
        o_ref[...] = _nn(r_ref[...], e_ref[...], HI)

    return _pallas(body, name="bias_expand",
                   out_shape=jax.ShapeDtypeStruct((N_HEADS, onehot.shape[1]), F32),
                   compiler_params=_cp(vmem=VMEM_MID))(rel_bias_t, onehot)


def _bias_reduce(dbias, onehot):
    def body(d_ref, e_ref, o_ref):
        o_ref[...] = _nt(d_ref[...], e_ref[...], HI)

    return _pallas(body, name="bias_reduce",
                   out_shape=jax.ShapeDtypeStruct((N_HEADS, REL_BUCKETS), F32),
                   compiler_params=_cp(vmem=VMEM_MID))(dbias, onehot)


def _norm_proj(name, x, g, shift, scale, w, seq, out_dtype, wf_t=None):
    t_tok = x.shape[0]
    w3d = w.ndim == 3
    n_out = w.shape[0] * w.shape[2] if w3d else w.shape[1]
    cn = w.shape[2] if w3d else 256

    def body(x_ref, g_ref, sh_ref, sc_ref, w_ref, *rest):
        if wf_t is not None:
            wf_ref, h_ref, o_ref, fl_ref = rest
        else:
            h_ref, o_ref = rest
        xv = x_ref[...]
        rstd = lax.rsqrt(jnp.mean(xv * xv, axis=-1, keepdims=True) + EPS)
        h = (xv * rstd) * g_ref[...] * (1.0 + sc_ref[...]) + sh_ref[...]
        hb = h.astype(BF16)
        h_ref[...] = hb
        for j in range(n_out // cn):
            wj = w_ref[j] if w3d else w_ref[:, j * cn:(j + 1) * cn]
            o_ref[:, j * cn:(j + 1) * cn] = _nn(hb, wj).astype(out_dtype)
        if wf_t is not None:
            fl_ref[...] = _nt(wf_ref[...], hb)

    mod_spec = pl.BlockSpec((None, 1, D_MODEL), lambda i: (i * TM // seq, 0, 0))
    w_spec = (pl.BlockSpec(w.shape, lambda i: (0, 0, 0)) if w3d else pl.BlockSpec(w.shape, lambda i: (0, 0)))
    in_specs = [pl.BlockSpec((TM, D_MODEL), lambda i: (i, 0)), pl.BlockSpec((1, D_MODEL), lambda i: (0, 0)),
                mod_spec, mod_spec, w_spec]
    out_shape = [jax.ShapeDtypeStruct((t_tok, D_MODEL), BF16), jax.ShapeDtypeStruct((t_tok, n_out), out_dtype)]
    out_specs = [pl.BlockSpec((TM, D_MODEL), lambda i: (i, 0)), pl.BlockSpec((TM, n_out), lambda i: (i, 0))]
    args = [x, g, shift, scale, w]
    if wf_t is not None:
        in_specs.append(pl.BlockSpec(wf_t.shape, lambda i: (0, 0)))
        out_shape.append(jax.ShapeDtypeStruct((wf_t.shape[0], t_tok), F32))
        out_specs.append(pl.BlockSpec((wf_t.shape[0], TM), lambda i: (0, i)))
        args.append(wf_t)
    return _pallas(body, name=name, grid=(t_tok // TM,), in_specs=in_specs, out_specs=out_specs,
                   out_shape=out_shape, compiler_params=_cp(("arbitrary",), VMEM_BIG))(*args)


def _norm_mod(name, x, g, shift, scale, seq):
    t_tok = x.shape[0]

    def body(x_ref, g_ref, sh_ref, sc_ref, h_ref):
        xv = x_ref[...]
        rstd = lax.rsqrt(jnp.mean(xv * xv, axis=-1, keepdims=True) + EPS)
        h_ref[...] = ((xv * rstd) * g_ref[...] * (1.0 + sc_ref[...]) + sh_ref[...]).astype(BF16)

    row = pl.BlockSpec((TM, D_MODEL), lambda i: (i, 0))
    mod_spec = pl.BlockSpec((None, 1, D_MODEL), lambda i: (i * TM // seq, 0, 0))
    return _pallas(body, name=name, grid=(t_tok // TM,),
                   in_specs=[row, pl.BlockSpec((1, D_MODEL), lambda i: (0, 0)), mod_spec, mod_spec], out_specs=row,
                   out_shape=jax.ShapeDtypeStruct((t_tok, D_MODEL), BF16),
                   compiler_params=_cp(("arbitrary",), VMEM_MID))(x, g, shift, scale)


def _proj(name, h, w, wf_t, out_dtype):
    t_tok = h.shape[0]
    n_out = w.shape[1]

    def body(h_ref, w_ref, wf_ref, o_ref, fl_ref):
        hb = h_ref[...]
        for j in range(n_out // 256):
            o_ref[:, j * 256:(j + 1) * 256] = _nn(hb, w_ref[:, j * 256:(j + 1) * 256]).astype(out_dtype)
        fl_ref[...] = _nt(wf_ref[...], hb)

    return _pallas(
        body, name=name, grid=(t_tok // TM,),
        in_specs=[pl.BlockSpec((TM, D_MODEL), lambda i: (i, 0)), pl.BlockSpec(w.shape, lambda i: (0, 0)),
                  pl.BlockSpec(wf_t.shape, lambda i: (0, 0))],
        out_specs=[pl.BlockSpec((TM, n_out), lambda i: (i, 0)), pl.BlockSpec((wf_t.shape[0], TM), lambda i: (0, i))],
        out_shape=[jax.ShapeDtypeStruct((t_tok, n_out), out_dtype), jax.ShapeDtypeStruct((wf_t.shape[0], t_tok), F32)],
        compiler_params=_cp(("arbitrary",), VMEM_BIG))(h, w, wf_t)


def _fox_prep(fl_t, b_f, seq):
    t_tok = fl_t.shape[1]
    ch = 256

    def body(fl_ref, bf_ref, fr_ref, fc_ref):
        z = fl_ref[...] + bf_ref[...]
        logf = jnp.minimum(z, 0.0) - jnp.log(1.0 + jnp.exp(-jnp.abs(z)))
        ri = lax.broadcasted_iota(jnp.int32, (ch, ch), 0)
        ci = lax.broadcasted_iota(jnp.int32, (ch, ch), 1)
        upper = (ri <= ci).astype(F32)
        eye = (ri == ci).astype(F32)
        carry = jnp.zeros((N_HEADS, 1), F32)
        for k in range(seq // ch):
            fk = _nn(logf[:, k * ch:(k + 1) * ch], upper, HI) + carry
            carry = fk[:, ch - 1:ch]
            fr_ref[:, k * ch:(k + 1) * ch] = fk
            padded = jnp.concatenate([fk, jnp.zeros((128 - N_HEADS, ch), F32)], axis=0)
            fc_ref[k * ch:(k + 1) * ch, :] = _nt(eye, padded, HI)

    return _pallas(
        body, name="fox_prep", grid=(t_tok // seq,),
        in_specs=[pl.BlockSpec((N_HEADS, seq), lambda b: (0, b)), pl.BlockSpec((N_HEADS, 1), lambda b: (0, 0))],
        out_specs=[pl.BlockSpec((N_HEADS, seq), lambda b: (0, b)), pl.BlockSpec((seq, 128), lambda b: (b, 0))],
        out_shape=[jax.ShapeDtypeStruct((N_HEADS, t_tok), F32), jax.ShapeDtypeStruct((t_tok, 128), F32)],
        compiler_params=_cp(("arbitrary",), VMEM_MID))(fl_t, b_f)


def _fox_post(df_row, fl_t, b_f, seq):
    t_tok = fl_t.shape[1]
    ch = 256

    def body(d_ref, fl_ref, bf_ref, o_ref, db_ref):
        @pl.when(pl.program_id(0) == 0)
        def _():
            db_ref[...] = jnp.zeros_like(db_ref)

        z = fl_ref[...] + bf_ref[...]
        sig_neg = 1.0 / (1.0 + jnp.exp(z))
        ri = lax.broadcasted_iota(jnp.int32, (ch, ch), 0)
        ci = lax.broadcasted_iota(jnp.int32, (ch, ch), 1)
        lower = (ri >= ci).astype(F32)
        carry = jnp.zeros((N_HEADS, 1), F32)
        tot = jnp.zeros((N_HEADS, 1), F32)
        for k in reversed(range(seq // ch)):
            dk = _nn(d_ref[:, k * ch:(k + 1) * ch], lower, HI) + carry
            carry = dk[:, 0:1]
            dfl = dk * sig_neg[:, k * ch:(k + 1) * ch]
            o_ref[:, k * ch:(k + 1) * ch] = dfl
            tot = tot + jnp.sum(dfl, axis=1, keepdims=True)
        db_ref[...] += jnp.broadcast_to(tot, db_ref.shape)

    return _pallas(
        body, name="fox_post", grid=(t_tok // seq,),
        in_specs=[pl.BlockSpec((N_HEADS, seq), lambda b: (0, b)), pl.BlockSpec((N_HEADS, seq), lambda b: (0, b)),
                  pl.BlockSpec((N_HEADS, 1), lambda b: (0, 0))],
        out_specs=[pl.BlockSpec((N_HEADS, seq), lambda b: (0, b)), pl.BlockSpec((N_HEADS, 128), lambda b: (0, 0))],
        out_shape=[jax.ShapeDtypeStruct((N_HEADS, t_tok), F32), jax.ShapeDtypeStruct((N_HEADS, 128), F32)],
        compiler_params=_cp(("arbitrary",), VMEM_MID))(df_row, fl_t, b_f)


def _eye(n, dtype):
    return (lax.broadcasted_iota(jnp.int32, (n, n), 0) == lax.broadcasted_iota(jnp.int32, (n, n), 1)).astype(dtype)


def _fox_aug(qkvg, f_col, seq):
    t_tok = qkvg.shape[0]
    ta = 256
    nkb = ta // TK

    def body(q_ref, k_ref, v_ref, fc_ref, qa_ref, ka_ref, kt_ref, vt_ref):
        ri = lax.broadcasted_iota(jnp.int32, (128, 128), 0)
        ci = lax.broadcasted_iota(jnp.int32, (128, 128), 1)
        eye = (ri == ci).astype(BF16)
        lane = lax.broadcasted_iota(jnp.int32, (ta, 128), 1)
        ones_q = jnp.where(jnp.logical_and(lane >= 64, lane < 67), 1.0, 0.0)
        ones_k = jnp.where(jnp.logical_and(lane >= 67, lane < 70), 1.0, 0.0)
        fc_tile = fc_ref[...]
        for p in range(N_HEADS // 2):
            q2 = q_ref[:, 128 * p:128 * (p + 1)]
            k2 = k_ref[:, 128 * p:128 * (p + 1)]
            vt = _nt(eye, v_ref[:, 128 * p:128 * (p + 1)]).astype(BF16)
            for kk in range(nkb):
                vt_ref[p, kk] = vt[:, kk * TK:(kk + 1) * TK]
            for e in range(2):
                h = 2 * p + e
                sel = jnp.logical_and(ri == ci + HEAD_DIM * e, ci < HEAD_DIM)
                f = _col(fc_tile, h)
                fh = f.astype(BF16).astype(F32)
                fm = (f - fh).astype(BF16).astype(F32)
                fl = (f - fh - fm).astype(BF16).astype(F32)
                qa = (_nn(q2, jnp.where(sel, SCALE, 0.0).astype(BF16)) + ones_q + jnp.where(lane == 67, fh, 0.0)
                      + jnp.where(lane == 68, fm, 0.0) + jnp.where(lane == 69, fl, 0.0))
                ka = (_nn(k2, jnp.where(sel, 1.0, 0.0).astype(BF16)) + ones_k - jnp.where(lane == 64, fh, 0.0)
                      - jnp.where(lane == 65, fm, 0.0) - jnp.where(lane == 66, fl, 0.0))
                qa_ref[h] = qa.astype(BF16)
                kab = ka.astype(BF16)
                ka_ref[h] = kab
                kt = _nt(eye, kab).astype(BF16)
                for kk in range(ta // TKB):
                    kt_ref[h, kk] = kt[:, kk * TKB:(kk + 1) * TKB]

    aug = jax.ShapeDtypeStruct((N_HEADS, t_tok, 128), BF16)
    return _pallas(
        body, name="fox_aug", grid=(t_tok // ta,),
        in_specs=[pl.BlockSpec((ta, 512), lambda i: (i, C_BQ // 512)), pl.BlockSpec((ta, 512), lambda i: (i, C_BK // 512)),
                  pl.BlockSpec((ta, 512), lambda i: (i, C_BV // 512)), pl.BlockSpec((ta, 128), lambda i: (i, 0))],
        out_specs=[pl.BlockSpec((N_HEADS, ta, 128), lambda i: (0, i, 0)), pl.BlockSpec((N_HEADS, ta, 128), lambda i: (0, i, 0)),
                   pl.BlockSpec((N_HEADS, ta // TKB, 128, TKB), lambda i: (0, i, 0, 0)),
                   pl.BlockSpec((N_HEADS // 2, nkb, 128, TK), lambda i: (0, i, 0, 0))],
        out_shape=[aug, aug, jax.ShapeDtypeStruct((N_HEADS, t_tok // TKB, 128, TKB), BF16),
                   jax.ShapeDtypeStruct((N_HEADS // 2, t_tok // TK, 128, TK), BF16)],
        compiler_params=_cp(("arbitrary",), VMEM_MID))(qkvg, qkvg, qkvg, f_col)


def _fox_fwd_t(q_aug, k_aug, vt, seq):
    t_tok = k_aug.shape[1]
    nq = seq // TQ
    ratio = TQ // TK

    def body(qa_ref, ka_ref, vt_ref, o_ref, lse_ref, ml_s, acc_s, st_s, p_s, al_s, qt_s):
        i = pl.program_id(1)
        tpos = i * TQ + lax.broadcasted_iota(jnp.int32, (1, TQ), 1)
        eye = _eye(HEAD_DIM, BF16)
        eye2 = _eye(128, BF16)
        for h in range(N_HEADS):
            qt_s[h] = _nt(eye2, qa_ref[h]).astype(BF16)
            ml_s[0, h] = jnp.full((1, TQ), NEG, F32)
            ml_s[1, h] = jnp.zeros((1, TQ), F32)
            acc_s[h] = jnp.zeros((HEAD_DIM, TQ), F32)
            p_s[1, h] = jnp.zeros((TK, TQ), BF16)
            al_s[1, h] = jnp.ones((1, TQ), F32)

        def scores(j):
            row0 = pl.multiple_of(j * TK, TK)
            for h in range(N_HEADS):
                st_s[j & 1, h] = _nn(ka_ref[h, pl.ds(row0, TK), :], qt_s[h])

        def softmax(j, masked):
            slot = j & 1
            if masked:
                keep = (j * TK + lax.broadcasted_iota(jnp.int32, (TK, 1), 0)) <= tpos
            for h in range(N_HEADS):
                st = st_s[slot, h]
                if masked:
                    st = jnp.where(keep, st, NEG)
                m = ml_s[0, h]
                m_new = jnp.maximum(m, jnp.max(st, axis=0, keepdims=True))
                alpha = jnp.exp(m - m_new)
                pe = jnp.exp(st - m_new)
                ml_s[0, h] = m_new
                ml_s[1, h] = alpha * ml_s[1, h] + jnp.sum(pe, axis=0, keepdims=True)
                al_s[slot, h] = alpha
                p_s[slot, h] = pe.astype(BF16)

        def values(j):
            slot = j & 1
            jv = jnp.maximum(j, 0)
            for h in range(N_HEADS):
                p, e = divmod(h, 2)
                acc_s[h] = al_s[slot, h] * acc_s[h] + _nn(vt_ref[p, jv, e * HEAD_DIM:(e + 1) * HEAD_DIM, :], p_s[slot, h])

        def step(j, carry):
            values(j - 1)
            softmax(j, False)
            scores(j + 1)
            return carry

        last = ratio * i + ratio - 1
        scores(0)
        lax.fori_loop(0, ratio * i, step, 0)
        for kk in range(ratio):
            j = ratio * i + kk
            values(j - 1)
            softmax(j, True)
            if kk < ratio - 1:
                scores(j + 1)
        values(last)
        for p in range(N_HEADS // 2):
            outs = []
            for e in range(2):
                h = 2 * p + e
                l = ml_s[1, h]
                outs.append(_tn((acc_s[h] / l).astype(BF16), eye))
                lse_ref[p, e:e + 1, :] = ml_s[0, h] + jnp.log(l)
            o_ref[:, 128 * p:128 * (p + 1)] = jnp.concatenate(outs, axis=1).astype(BF16)

    return _pallas(
        body, name="fox_fwd", grid=(t_tok // seq, nq),
        in_specs=[pl.BlockSpec((N_HEADS, TQ, 128), lambda b, i: (0, b * nq + i, 0)),
                  pl.BlockSpec((N_HEADS, seq, 128), lambda b, i: (0, b, 0)),
                  pl.BlockSpec((N_HEADS // 2, seq // TK, 128, TK), lambda b, i: (0, b, 0, 0))],
        out_specs=[pl.BlockSpec((TQ, 512), lambda b, i: (b * nq + i, 0)),
                   pl.BlockSpec((N_HEADS // 2, 2, TQ), lambda b, i: (0, 0, b * nq + i))],
        out_shape=[jax.ShapeDtypeStruct((t_tok, 512), BF16), jax.ShapeDtypeStruct((N_HEADS // 2, 2, t_tok), F32)],
        scratch_shapes=[pltpu.VMEM((2, N_HEADS, 1, TQ), F32), pltpu.VMEM((N_HEADS, HEAD_DIM, TQ), F32),
                        pltpu.VMEM((2, N_HEADS, TK, TQ), F32), pltpu.VMEM((2, N_HEADS, TK, TQ), BF16),
                        pltpu.VMEM((2, N_HEADS, 1, TQ), F32), pltpu.VMEM((N_HEADS, 128, TQ), BF16)],
        compiler_params=_cp(("arbitrary", "arbitrary"), VMEM_MID))(q_aug, k_aug, vt)


def _fox_bwd_t(q_aug, k_aug, kt, qkvg, du_b, b_out, lse, seq):
    TK = TKB
    t_tok = qkvg.shape[0]
    nq = seq // TQ
    nkb = seq // TK
    ratio = TQ // TK
    hg = 4

    def body(qa_ref, ka_ref, kt_ref, v_ref, do_ref, o_ref, lse_ref, dq_ref, dk_ref, dv_ref, df_ref,
             dqt_s, row_s, dfk_s, dk_s, dv_s, dot_s, st_s, dp_s, pb_s, db_s, qt_s):
        eye = _eye(HEAD_DIM, BF16)
        eye2 = _eye(128, BF16)
        eye_k = _eye(TK, F32)
        lane8 = lax.broadcasted_iota(jnp.int32, (8, 128), 1)
        lane_k = lax.broadcasted_iota(jnp.int32, (TK, 128), 1)
        first = [lane8 < HEAD_DIM, lane8 >= HEAD_DIM]
        for pp in range(hg // 2):
            for ii in range(nq):
                dot_s[pp, ii] = _nt(eye2, do_ref[ii * TQ:(ii + 1) * TQ, 128 * pp:128 * (pp + 1)]).astype(BF16)
        for hh in range(hg):
            for ii in range(nq):
                qt_s[hh, ii] = _nt(eye2, qa_ref[hh, ii * TQ:(ii + 1) * TQ, :]).astype(BF16)
        for hh in range(hg):
            pp, e = divmod(hh, 2)
            head_lanes = jnp.where(first[e], 1.0, 0.0)
            for ii in range(nq):
                rows = slice(ii * TQ, (ii + 1) * TQ)
                prod = do_ref[rows, 128 * pp:128 * (pp + 1)].astype(F32) * o_ref[rows, 128 * pp:128 * (pp + 1)].astype(F32)
                row_s[hh, ii, 0] = _nt(head_lanes, prod, HI)
                row_s[hh, ii, 1] = jnp.broadcast_to(lse_ref[pp, e:e + 1, ii * TQ:(ii + 1) * TQ], (8, TQ))
                dqt_s[hh, ii] = jnp.zeros((128, TQ), F32)

        def kblock(j, _):
            krow = pl.multiple_of(j * TK, TK)
            spos = j * TK + lax.broadcasted_iota(jnp.int32, (TK, 1), 0)
            for hh in range(hg):
                dk_s[hh] = jnp.zeros((TK, 128), F32)
                dv_s[hh] = jnp.zeros((TK, 128), F32)

            def scores(i):
                for hh in range(hg):
                    pp, e = divmod(hh, 2)
                    own = (lane_k < HEAD_DIM) if e == 0 else (lane_k >= HEAD_DIM)
                    v2 = v_ref[pl.ds(krow, TK), 128 * pp:128 * (pp + 1)]
                    vj = jnp.where(own, v2, jnp.zeros_like(v2))
                    st_s[i & 1, hh] = _nn(ka_ref[hh, pl.ds(krow, TK), :], qt_s[hh, i])
                    dp_s[i & 1, hh] = _nn(vj, dot_s[pp, i])

            def elementwise(i, masked):
                slot = i & 1
                if masked:
                    keep = spos <= (i * TQ + lax.broadcasted_iota(jnp.int32, (1, TQ), 1))
                for hh in range(hg):
                    pt = jnp.exp(st_s[slot, hh] - row_s[hh, i, 1][0:1, :])
                    if masked:
                        pt = jnp.where(keep, pt, 0.0)
                    dst = pt * (dp_s[slot, hh] - row_s[hh, i, 0][0:1, :])
                    pb_s[slot, hh] = pt.astype(BF16)
                    db_s[slot, hh] = dst.astype(BF16)

            def grads(i):
                slot = i & 1
                qrow = pl.multiple_of(i * TQ, TQ)
                for hh in range(hg):
                    dst_b = db_s[slot, hh]
                    dv_s[hh] += _nn(pb_s[slot, hh], do_ref[pl.ds(qrow, TQ), 128 * (hh // 2):128 * (hh // 2 + 1)])
                    dk_s[hh] += _nn(dst_b, qa_ref[hh, pl.ds(qrow, TQ), :])
                    dqt_s[hh, i] += _nn(kt_ref[hh, j], dst_b)

            def step(i, carry):
                grads(i - 1)
                elementwise(i, False)
                scores(jnp.minimum(i + 1, nq - 1))
                return carry

            i0 = j // ratio
            scores(i0)
            elementwise(i0, True)
            scores(jnp.minimum(i0 + 1, nq - 1))
            lax.fori_loop(i0 + 1, nq, step, 0)
            grads(nq - 1)
            for pp in range(hg // 2):
                cols = slice(128 * pp, 128 * (pp + 1))
                dk_ref[pl.ds(krow, TK), cols] = jnp.concatenate(
                    [dk_s[2 * pp][:, :HEAD_DIM], dk_s[2 * pp + 1][:, :HEAD_DIM]], axis=1).astype(BF16)
                dv_ref[pl.ds(krow, TK), cols] = jnp.where(lane_k < HEAD_DIM, dv_s[2 * pp], dv_s[2 * pp + 1]).astype(BF16)
            for hh in range(hg):
                dfk_s[hh, j] = _tn(dk_s[hh][:, HEAD_DIM:HEAD_DIM + 8], eye_k, HI)
            return 0

        lax.fori_loop(0, nkb, kblock, 0)
        for pp in range(hg // 2):
            for ii in range(nq):
                parts = []
                for e in range(2):
                    dqt = dqt_s[2 * pp + e, ii]
                    parts.append(_tn(dqt[0:HEAD_DIM, :].astype(BF16), eye) * SCALE)
                    for kk in range(ratio):
                        jj = ii * ratio + kk
                        df_ref[pp, e:e + 1, jj * TK:(jj + 1) * TK] = (dqt[67:68, kk * TK:(kk + 1) * TK]
                                                                     - dfk_s[2 * pp + e, jj][0:1, :])
                dq_ref[ii * TQ:(ii + 1) * TQ, 128 * pp:128 * (pp + 1)] = jnp.concatenate(parts, axis=1).astype(BF16)

    aug_blk = pl.BlockSpec((hg, seq, 128), lambda b, g: (g, b, 0))
    pair_blk = pl.BlockSpec((seq, 64 * hg), lambda b, g: (b, g))
    row_blk = pl.BlockSpec((hg // 2, 2, seq), lambda b, g: (g, 0, b))
    return _pallas(
        body, name="fox_bwd", grid=(t_tok // seq, N_HEADS // hg),
        in_specs=[aug_blk, aug_blk, pl.BlockSpec((hg, nkb, 128, TK), lambda b, g: (g, b, 0, 0)),
                  pl.BlockSpec((seq, 64 * hg), lambda b, g: (b, C_BV // (64 * hg) + g)), pair_blk, pair_blk, row_blk],
        out_specs=[pair_blk, pair_blk, pair_blk, row_blk],
        out_shape=[jax.ShapeDtypeStruct((t_tok, 512), BF16)] * 3
        + [jax.ShapeDtypeStruct((N_HEADS // 2, 2, t_tok), F32)],
        scratch_shapes=[pltpu.VMEM((hg, nq, 128, TQ), F32), pltpu.VMEM((hg, nq, 2, 8, TQ), F32),
                        pltpu.VMEM((hg, nkb, 8, TK), F32), pltpu.VMEM((hg, TK, 128), F32),
                        pltpu.VMEM((hg, TK, 128), F32), pltpu.VMEM((hg // 2, nq, 128, TQ), BF16),
                        pltpu.VMEM((2, hg, TK, TQ), F32), pltpu.VMEM((2, hg, TK, TQ), F32),
                        pltpu.VMEM((2, hg, TK, TQ), BF16), pltpu.VMEM((2, hg, TK, TQ), BF16),
                        pltpu.VMEM((hg, nq, 128, TQ), BF16)],
        compiler_params=_cp(("arbitrary", "arbitrary"), VMEM_BIG))(q_aug, k_aug, kt, qkvg, du_b, b_out, lse)


def _fox_bwd_t_old(q_aug, k_aug, kt, qkvg, du_b, b_out, lse, seq):
    t_tok = qkvg.shape[0]
    nq = seq // TQ
    nkb = seq // TK
    ratio = TQ // TK

    def body(qa_ref, ka_ref, kt_ref, v_ref, do_ref, o_ref, lse_ref, dq_ref, dk_ref, dv_ref, df_ref,
             dqt_s, out_s, row_s, dfk_s):
        ones_b = jnp.ones((8, TQ), BF16)
        ones_f = jnp.ones((8, HEAD_DIM), F32)
        eye = _eye(HEAD_DIM, BF16)
        for e in range(2):
            lo, hi = e * HEAD_DIM, (e + 1) * HEAD_DIM
            for ii in range(nq):
                rows = slice(ii * TQ, (ii + 1) * TQ)
                do = do_ref[rows, :][:, lo:hi].astype(F32)
                ov = o_ref[rows, :][:, lo:hi].astype(F32)
                row_s[ii, 0] = _nt(ones_f, do * ov, HI)
                row_s[ii, 1] = jnp.broadcast_to(lse_ref[e:e + 1, ii * TQ:(ii + 1) * TQ], (8, TQ))
                dqt_s[ii] = jnp.zeros((128, TQ), F32)

            def kblock(j, _):
                krow = pl.multiple_of(j * TK, TK)
                kj = ka_ref[e, pl.ds(krow, TK), :]
                ktj = kt_ref[e, j]
                vj = v_ref[pl.ds(krow, TK), :][:, lo:hi]
                spos = j * TK + lax.broadcasted_iota(jnp.int32, (TK, 1), 0)

                def qblock(i, carry, masked):
                    dk_acc, dv_acc, dfk = carry
                    qrow = pl.multiple_of(i * TQ, TQ)
                    qa = qa_ref[e, pl.ds(qrow, TQ), :]
                    doh = do_ref[pl.ds(qrow, TQ), :][:, lo:hi]
                    pt = jnp.exp(_nt(kj, qa) - row_s[i, 1][0:1, :])
                    if masked:
                        tpos = i * TQ + lax.broadcasted_iota(jnp.int32, (1, TQ), 1)
                        pt = jnp.where(spos <= tpos, pt, 0.0)
                    dst = pt * (_nt(vj, doh) - row_s[i, 0][0:1, :])
                    dst_b = dst.astype(BF16)
                    dv_acc = dv_acc + _nn(pt.astype(BF16), doh)
                    dk_acc = dk_acc + _nn(dst_b, qa)
                    dqt_s[i] += _nn(ktj, dst_b)
                    dfk = dfk + _nt(ones_b, dst_b)
                    return dk_acc, dv_acc, dfk

                i0 = j // ratio
                carry = (jnp.zeros((TK, 128), F32), jnp.zeros((TK, HEAD_DIM), F32), jnp.zeros((8, TK), F32))
                carry = qblock(i0, carry, True)
                dk_acc, dv_acc, dfk = lax.fori_loop(i0 + 1, nq, functools.partial(qblock, masked=False), carry)
                out_s[1, e, pl.ds(krow, TK), :] = dk_acc[:, :HEAD_DIM]
                out_s[2, e, pl.ds(krow, TK), :] = dv_acc
                dfk_s[j] = dfk
                return 0

            lax.fori_loop(0, nkb, kblock, 0)
            for ii in range(nq):
                dqt = dqt_s[ii]
                out_s[0, e, ii * TQ:(ii + 1) * TQ, :] = _tn(dqt[0:HEAD_DIM, :].astype(BF16), eye) * SCALE
                for kk in range(ratio):
                    jj = ii * ratio + kk
                    df_ref[e:e + 1, jj * TK:(jj + 1) * TK] = dqt[67:68, kk * TK:(kk + 1) * TK] - dfk_s[jj][0:1, :]
        for k, ref in enumerate((dq_ref, dk_ref, dv_ref)):
            ref[...] = jnp.concatenate([out_s[k, 0], out_s[k, 1]], axis=1).astype(BF16)

    aug_blk = pl.BlockSpec((2, seq, 128), lambda b, p: (p, b, 0))
    pair_blk = pl.BlockSpec((seq, 128), lambda b, p: (b, p))
    row_blk = pl.BlockSpec((None, 2, seq), lambda b, p: (p, 0, b))
    return _pallas(
        body, name="fox_bwd", grid=(t_tok // seq, N_HEADS // 2),
        in_specs=[aug_blk, aug_blk, pl.BlockSpec((2, nkb, 128, TK), lambda b, p: (p, b, 0, 0)),
                  pl.BlockSpec((seq, 128), lambda b, p: (b, C_BV // 128 + p)), pair_blk, pair_blk, row_blk],
        out_specs=[pair_blk, pair_blk, pair_blk, row_blk],
        out_shape=[jax.ShapeDtypeStruct((t_tok, 512), BF16)] * 3
        + [jax.ShapeDtypeStruct((N_HEADS // 2, 2, t_tok), F32)],
        scratch_shapes=[pltpu.VMEM((nq, 128, TQ), F32), pltpu.VMEM((3, 2, seq, HEAD_DIM), F32),
                        pltpu.VMEM((nq, 2, 8, TQ), F32), pltpu.VMEM((nkb, 8, TK), F32)],
        compiler_params=_cp(("arbitrary", "arbitrary"), VMEM_BIG))(q_aug, k_aug, kt, qkvg, du_b, b_out, lse)


def _fox_fwd(qkvg, f_row, f_col, seq):
    t_tok = qkvg.shape[0]
    nq = seq // TQ

    def body(q_ref, k_ref, v_ref, fr_ref, fc_ref, o_ref, lse_ref, fk_s):
        i = pl.program_id(1)
        for jj in range(nq):
            fk_s[jj] = fr_ref[:, jj * TQ:(jj + 1) * TQ]
        fcol = fc_ref[...]
        tpos = i * TQ + lax.broadcasted_iota(jnp.int32, (TQ, 1), 0)
        lane = lax.broadcasted_iota(jnp.int32, (TQ, 128), 1)
        lse_tile = jnp.zeros((TQ, 128), F32)
        for p in range(N_HEADS // 2):
            q2 = q_ref[:, 128 * p:128 * (p + 1)]
            qs = [q2[:, :HEAD_DIM], q2[:, HEAD_DIM:]]
            fqs = [_col(fcol, 2 * p + e) for e in range(2)]

            def kblock(j, carry):
                row0 = pl.multiple_of(j * TQ, TQ)
                k2 = k_ref[pl.ds(row0, TQ), 128 * p:128 * (p + 1)]
                v2 = v_ref[pl.ds(row0, TQ), 128 * p:128 * (p + 1)]
                fk8 = fk_s[j]
                spos = j * TQ + lax.broadcasted_iota(jnp.int32, (1, TQ), 1)
                keep = spos <= tpos
                new = []
                for e in range(2):
                    m, l, acc = carry[3 * e:3 * e + 3]
                    kh = k2[:, e * HEAD_DIM:(e + 1) * HEAD_DIM]
                    vh = v2[:, e * HEAD_DIM:(e + 1) * HEAD_DIM]
                    s = _nt(qs[e], kh) * SCALE + (fqs[e] - fk8[2 * p + e:2 * p + e + 1, :])
                    s = jnp.where(keep, s, NEG)
                    m_new = jnp.maximum(m, jnp.max(s, axis=1, keepdims=True))
                    alpha = jnp.exp(m - m_new)
                    pe = jnp.exp(s - m_new)
                    l = alpha * l + jnp.sum(pe, axis=1, keepdims=True)
                    acc = alpha * acc + _nn(pe.astype(BF16), vh)
                    new += [m_new, l, acc]
                return tuple(new)

            init = (jnp.full((TQ, 1), NEG, F32), jnp.zeros((TQ, 1), F32), jnp.zeros((TQ, HEAD_DIM), F32)) * 2
            res = lax.fori_loop(0, i + 1, kblock, init)
            outs = []
            for e in range(2):
                m, l, acc = res[3 * e:3 * e + 3]
                outs.append(acc / l)
                lse_tile = jnp.where(lane == 2 * p + e, m + jnp.log(l), lse_tile)
            o_ref[:, 128 * p:128 * (p + 1)] = jnp.concatenate(outs, axis=1).astype(BF16)
        lse_ref[...] = lse_tile

    return _pallas(
        body, name="fox_fwd", grid=(t_tok // seq, nq),
        in_specs=[pl.BlockSpec((TQ, 512), lambda b, i: (b * nq + i, C_BQ // 512)),
                  pl.BlockSpec((seq, 512), lambda b, i: (b, C_BK // 512)),
                  pl.BlockSpec((seq, 512), lambda b, i: (b, C_BV // 512)),
                  pl.BlockSpec((N_HEADS, seq), lambda b, i: (0, b)),
                  pl.BlockSpec((TQ, 128), lambda b, i: (b * nq + i, 0))],
        out_specs=[pl.BlockSpec((TQ, 512), lambda b, i: (b * nq + i, 0)),
                   pl.BlockSpec((TQ, 128), lambda b, i: (b * nq + i, 0))],
        out_shape=[jax.ShapeDtypeStruct((t_tok, 512), BF16), jax.ShapeDtypeStruct((t_tok, 128), F32)],
        scratch_shapes=[pltpu.VMEM((nq, N_HEADS, TQ), F32)],
        compiler_params=_cp(("arbitrary", "arbitrary"), VMEM_MID))(qkvg, qkvg, qkvg, f_row, f_col)


def _fox_bwd(qkvg, du_b, b_out, lse, f_row, f_col, seq):
    t_tok = qkvg.shape[0]
    nq = seq // TQ

    def body(q_ref, k_ref, v_ref, do_ref, o_ref, lse_ref, fr_ref, fc_ref,
             dq_ref, dk_ref, dv_ref, df_ref, dq_s, dk_s, dv_s, col_s, df_s, fk_s):
        p = pl.program_id(1)
        for jj in range(nq):
            fk_s[jj] = fr_ref[:, jj * TQ:(jj + 1) * TQ]
        eye = (lax.broadcasted_iota(jnp.int32, (TQ, TQ), 0) == lax.broadcasted_iota(jnp.int32, (TQ, TQ), 1)).astype(F32)
        for e in range(2):
            h = 2 * p + e
            lo, hi = e * HEAD_DIM, (e + 1) * HEAD_DIM
            for ii in range(nq):
                rows = slice(ii * TQ, (ii + 1) * TQ)
                do = do_ref[rows, :][:, lo:hi].astype(F32)
                ov = o_ref[rows, :][:, lo:hi].astype(F32)
                col_s[0, rows, :] = jnp.sum(do * ov, axis=1, keepdims=True)
                col_s[1, rows, :] = _col(lse_ref[rows, :], h)
                col_s[2, rows, :] = _col(fc_ref[rows, :], h)
                dq_s[rows, :] = jnp.zeros((TQ, HEAD_DIM), F32)
                df_s[ii] = jnp.zeros((8, TQ), F32)
                col_s[3, rows, :] = jnp.zeros((TQ, 1), F32)

            def kblock(j, _):
                krow = pl.multiple_of(j * TQ, TQ)
                kh = k_ref[pl.ds(krow, TQ), :][:, lo:hi]
                vh = v_ref[pl.ds(krow, TQ), :][:, lo:hi]
                fk = _row(fk_s[j], h)
                spos = j * TQ + lax.broadcasted_iota(jnp.int32, (1, TQ), 1)

                def qblock(i, carry):
                    dk_acc, dv_acc, dfk = carry
                    qrow = pl.multiple_of(i * TQ, TQ)
                    qh = q_ref[pl.ds(qrow, TQ), :][:, lo:hi]
                    doh = do_ref[pl.ds(qrow, TQ), :][:, lo:hi]
                    delta = col_s[0, pl.ds(qrow, TQ), :]
                    lse_q = col_s[1, pl.ds(qrow, TQ), :]
                    fq = col_s[2, pl.ds(qrow, TQ), :]
                    tpos = i * TQ + lax.broadcasted_iota(jnp.int32, (TQ, 1), 0)
                    s = _nt(qh, kh) * SCALE + (fq - fk)
                    pr = jnp.where(spos <= tpos, jnp.exp(s - lse_q), 0.0)
                    dp = _nt(doh, vh)
                    ds = pr * (dp - delta)
                    ds_b = ds.astype(BF16)
                    dv_acc = dv_acc + _tn(pr.astype(BF16), doh)
                    dk_acc = dk_acc + _tn(ds_b, qh)
                    dq_s[pl.ds(qrow, TQ), :] += _nn(ds_b, kh)
                    col_s[3, pl.ds(qrow, TQ), :] += jnp.sum(ds, axis=1, keepdims=True)
                    dfk = dfk + jnp.sum(ds, axis=0, keepdims=True)
                    return dk_acc, dv_acc, dfk

                zero = jnp.zeros((TQ, HEAD_DIM), F32)
                dk_acc, dv_acc, dfk = lax.fori_loop(j, nq, qblock, (zero, zero, jnp.zeros((1, TQ), F32)))
                dk_s[e, pl.ds(krow, TQ), :] = dk_acc * SCALE
                dv_s[e, pl.ds(krow, TQ), :] = dv_acc
                df_s[j] -= jnp.broadcast_to(dfk, (8, TQ))
                return 0

            lax.fori_loop(0, nq, kblock, 0)
            dq_s2 = dq_s[...] * SCALE
            dk_s[2 + e] = dq_s2
            for ii in range(nq):
                dfq = jnp.broadcast_to(col_s[3, ii * TQ:(ii + 1) * TQ, :], (TQ, 128))
                df_ref[e:e + 1, ii * TQ:(ii + 1) * TQ] = _tn(dfq, eye, HI)[0:1, :] + df_s[ii][0:1, :]
        dq_ref[...] = jnp.concatenate([dk_s[2], dk_s[3]], axis=1).astype(BF16)
        dk_ref[...] = jnp.concatenate([dk_s[0], dk_s[1]], axis=1).astype(BF16)
        dv_ref[...] = jnp.concatenate([dv_s[0], dv_s[1]], axis=1).astype(BF16)

    blk = lambda off: pl.BlockSpec((seq, 128), lambda b, p: (b, off // 128 + p))
    out_blk = pl.BlockSpec((seq, 128), lambda b, p: (b, p))
    return _pallas(
        body, name="fox_bwd", grid=(t_tok // seq, N_HEADS // 2),
        in_specs=[blk(C_BQ), blk(C_BK), blk(C_BV), out_blk, out_blk,
                  pl.BlockSpec((seq, 128), lambda b, p: (b, 0)),
                  pl.BlockSpec((N_HEADS, seq), lambda b, p: (0, b)),
                  pl.BlockSpec((seq, 128), lambda b, p: (b, 0))],
        out_specs=[out_blk, out_blk, out_blk, pl.BlockSpec((None, 2, seq), lambda b, p: (p, 0, b))],
        out_shape=[jax.ShapeDtypeStruct((t_tok, 512), BF16)] * 3
        + [jax.ShapeDtypeStruct((N_HEADS // 2, 2, t_tok), F32)],
        scratch_shapes=[pltpu.VMEM((seq, HEAD_DIM), F32), pltpu.VMEM((4, seq, HEAD_DIM), F32),
                        pltpu.VMEM((2, seq, HEAD_DIM), F32), pltpu.VMEM((4, seq, 1), F32),
                        pltpu.VMEM((nq, 8, TQ), F32), pltpu.VMEM((nq, N_HEADS, TQ), F32)],
        compiler_params=_cp(("arbitrary", "arbitrary"), VMEM_BIG))(qkvg, qkvg, qkvg, du_b, b_out, lse, f_row, f_col)


def _swa_window(k_ref, v_ref, n):
    prev = pl.multiple_of(jnp.maximum(n - 1, 0) * BLOCK, BLOCK)
    cur = pl.multiple_of(n * BLOCK, BLOCK)
    kwin = jnp.concatenate([k_ref[pl.ds(prev, BLOCK), :], k_ref[pl.ds(cur, BLOCK), :]], axis=0)
    vwin = jnp.concatenate([v_ref[pl.ds(prev, BLOCK), :], v_ref[pl.ds(cur, BLOCK), :]], axis=0)
    ti = lax.broadcasted_iota(jnp.int32, (BLOCK, 2 * BLOCK), 0)
    sj = lax.broadcasted_iota(jnp.int32, (BLOCK, 2 * BLOCK), 1)
    rel = ti - sj + BLOCK
    first_key = jnp.where(n > 0, 0, BLOCK)
    mask = jnp.logical_and(jnp.logical_and(rel >= 0, rel < BLOCK), sj >= first_key)
    return kwin, vwin, mask, prev, cur


def _head_cols(ref, h):
    pair = ref[:, 128 * (h // 2):128 * (h // 2 + 1)]
    return pair[:, (h % 2) * HEAD_DIM:(h % 2 + 1) * HEAD_DIM]


def _swa_logits(q_ref, kwin, bias_ref, h, mask):
    hk = h // KV_GROUP
    s = _nt(_head_cols(q_ref, h), kwin[:, hk * HEAD_DIM:(hk + 1) * HEAD_DIM]) * SCALE + bias_ref[h]
    return jnp.where(mask, s, NEG)


def _swa_fwd(qkvg, bias, sinks, seq):
    t_tok = qkvg.shape[0]
    nb = seq // BLOCK

    def body(sink_ref, q_ref, k_ref, v_ref, bias_ref, o_ref, lse_ref, s_s, p_s, den_s):
        g = pl.program_id(1)
        subs = [pl.ds(s * BLOCK, BLOCK) for s in range(SWA_SUB)]
        wins = [_swa_window(k_ref, v_ref, SWA_SUB * g + s) for s in range(SWA_SUB)]
        for s in range(SWA_SUB):
            for h in range(N_HEADS):
                s_s[s * N_HEADS + h] = _swa_logits(q_ref.at[subs[s]], wins[s][0], bias_ref, h, wins[s][2])
        lane = lax.broadcasted_iota(jnp.int32, (BLOCK, 128), 1)
        for s in range(SWA_SUB):
            lse_tile = jnp.zeros((BLOCK, 128), F32)
            for h in range(N_HEADS):
                sc = s_s[s * N_HEADS + h]
                sink = sink_ref[h]
                m = jnp.maximum(jnp.max(sc, axis=1, keepdims=True), sink)
                pe = jnp.exp(sc - m)
                den = jnp.sum(pe, axis=1, keepdims=True) + jnp.exp(sink - m)
                p_s[s * N_HEADS + h] = pe.astype(BF16)
                den_s[s * N_HEADS + h] = den
                lse_tile = jnp.where(lane == h, m + jnp.log(den), lse_tile)
            lse_ref[subs[s], :] = lse_tile
        for s in range(SWA_SUB):
            vwin = wins[s][1]
            for pr in range(N_HEADS // 2):
                outs = []
                for h in (2 * pr, 2 * pr + 1):
                    hk = h // KV_GROUP
                    outs.append(_nn(p_s[s * N_HEADS + h], vwin[:, hk * HEAD_DIM:(hk + 1) * HEAD_DIM]) / den_s[s * N_HEADS + h])
                o_ref[subs[s], 128 * pr:128 * (pr + 1)] = jnp.concatenate(outs, axis=1).astype(BF16)

    rows = SWA_SUB * BLOCK
    steps = nb // SWA_SUB
    return _pallas(
        body, name="swa_fwd", grid=(t_tok // seq, steps),
        in_specs=[pl.BlockSpec(memory_space=pltpu.SMEM),
                  pl.BlockSpec((rows, 512), lambda b, n: (b * steps + n, C_AQ // 512)),
                  pl.BlockSpec((seq, 128), lambda b, n: (b, C_AK // 128)),
                  pl.BlockSpec((seq, 128), lambda b, n: (b, C_AV // 128)),
                  pl.BlockSpec((N_HEADS, BLOCK, 2 * BLOCK), lambda b, n: (0, 0, 0))],
        out_specs=[pl.BlockSpec((rows, 512), lambda b, n: (b * steps + n, 0)),
                   pl.BlockSpec((rows, 128), lambda b, n: (b * steps + n, 0))],
        out_shape=[jax.ShapeDtypeStruct((t_tok, 512), BF16), jax.ShapeDtypeStruct((t_tok, 128), F32)],
        scratch_shapes=[pltpu.VMEM((SWA_SUB * N_HEADS, BLOCK, 2 * BLOCK), F32),
                        pltpu.VMEM((SWA_SUB * N_HEADS, BLOCK, 2 * BLOCK), BF16),
                        pltpu.VMEM((SWA_SUB * N_HEADS, BLOCK, 1), F32)],
        compiler_params=_cp(("arbitrary", "arbitrary"), VMEM_MID))(sinks, qkvg, qkvg, qkvg, bias)


def _swa_bwd(qkvg, du_a, a_out, lse, bias, sinks, seq):
    t_tok = qkvg.shape[0]
    nb = seq // BLOCK

    def body(sink_ref, q_ref, k_ref, v_ref, do_ref, o_ref, lse_ref, bias_ref,
             dq_ref, dkv_ref, dbias_ref, dsink_ref, kv_s, s_s, dp_s, pb_s, db_s):
        b, n = pl.program_id(0), pl.program_id(1)

        @pl.when(jnp.logical_and(b == 0, n == 0))
        def _():
            dbias_ref[...] = jnp.zeros_like(dbias_ref)
            dsink_ref[...] = jnp.zeros_like(dsink_ref)

        @pl.when(n == 0)
        def _():
            kv_s[...] = jnp.zeros_like(kv_s)

        subs = [pl.ds(s * BLOCK, BLOCK) for s in range(SWA_SUB)]
        wins = [_swa_window(k_ref, v_ref, SWA_SUB * n + s) for s in range(SWA_SUB)]
        for s in range(SWA_SUB):
            kwin, vwin, mask = wins[s][:3]
            for h in range(N_HEADS):
                hk = h // KV_GROUP
                s_s[s * N_HEADS + h] = _swa_logits(q_ref.at[subs[s]], kwin, bias_ref, h, mask)
                dp_s[s * N_HEADS + h] = _nt(_head_cols(do_ref.at[subs[s]], h), vwin[:, hk * HEAD_DIM:(hk + 1) * HEAD_DIM])
        for s in range(SWA_SUB):
            lse_tile = lse_ref[subs[s], :]
            do_s, o_s = do_ref.at[subs[s]], o_ref.at[subs[s]]
            for h in range(N_HEADS):
                delta = jnp.sum(_head_cols(do_s, h).astype(F32) * _head_cols(o_s, h).astype(F32), axis=1, keepdims=True)
                lse_h = _col(lse_tile, h)
                pe = jnp.exp(s_s[s * N_HEADS + h] - lse_h)
                ds = pe * (dp_s[s * N_HEADS + h] - delta)
                dbias_ref[h] += ds
                psink = jnp.exp(sink_ref[h] - lse_h)
                dsink_ref[h:h + 1, :] += jnp.broadcast_to(jnp.sum(-psink * delta, axis=0, keepdims=True), (1, 128))
                pb_s[s * N_HEADS + h] = pe.astype(BF16)
                db_s[s * N_HEADS + h] = ds.astype(BF16)
        for s in range(SWA_SUB):
            kwin, _, _, prev, cur = wins[s]
            q_s, do_s = q_ref.at[subs[s]], do_ref.at[subs[s]]
            for pr in range(N_HEADS // 2):
                dqs = []
                for h in (2 * pr, 2 * pr + 1):
                    hk = h // KV_GROUP
                    dqs.append(_nn(db_s[s * N_HEADS + h], kwin[:, hk * HEAD_DIM:(hk + 1) * HEAD_DIM]) * SCALE)
                dq_ref[subs[s], 128 * pr:128 * (pr + 1)] = jnp.concatenate(dqs, axis=1).astype(BF16)
            dks, dvs = [], []
            for hk in range(N_HEADS // KV_GROUP):
                dk = jnp.zeros((2 * BLOCK, HEAD_DIM), F32)
                dv = jnp.zeros((2 * BLOCK, HEAD_DIM), F32)
                for h in range(hk * KV_GROUP, (hk + 1) * KV_GROUP):
                    dk = dk + _tn(db_s[s * N_HEADS + h], _head_cols(q_s, h))
                    dv = dv + _tn(pb_s[s * N_HEADS + h], _head_cols(do_s, h))
                dks.append(dk * SCALE)
                dvs.append(dv)
            upd = jnp.concatenate(dks + dvs, axis=1)
            kv_s[pl.ds(prev, BLOCK), :] += upd[:BLOCK]
            kv_s[pl.ds(cur, BLOCK), :] += upd[BLOCK:]

        @pl.when(n == steps - 1)
        def _():
            dkv_ref[...] = kv_s[...].astype(BF16)

    rows = SWA_SUB * BLOCK
    steps = nb // SWA_SUB
    tile = (SWA_SUB * N_HEADS, BLOCK, 2 * BLOCK)
    return _pallas(
        body, name="swa_bwd", grid=(t_tok // seq, steps),
        in_specs=[pl.BlockSpec(memory_space=pltpu.SMEM),
                  pl.BlockSpec((rows, 512), lambda b, n: (b * steps + n, C_AQ // 512)),
                  pl.BlockSpec((seq, 128), lambda b, n: (b, C_AK // 128)),
                  pl.BlockSpec((seq, 128), lambda b, n: (b, C_AV // 128)),
                  pl.BlockSpec((rows, 512), lambda b, n: (b * steps + n, 0)),
                  pl.BlockSpec((rows, 512), lambda b, n: (b * steps + n, 0)),
                  pl.BlockSpec((rows, 128), lambda b, n: (b * steps + n, 0)),
                  pl.BlockSpec((N_HEADS, BLOCK, 2 * BLOCK), lambda b, n: (0, 0, 0))],
        out_specs=[pl.BlockSpec((rows, 512), lambda b, n: (b * steps + n, 0)),
                   pl.BlockSpec((seq, 256), lambda b, n: (b, 0)),
                   pl.BlockSpec((N_HEADS, BLOCK, 2 * BLOCK), lambda b, n: (0, 0, 0)),
                   pl.BlockSpec((N_HEADS, 128), lambda b, n: (0, 0))],
        out_shape=[jax.ShapeDtypeStruct((t_tok, 512), BF16), jax.ShapeDtypeStruct((t_tok, 256), BF16),
                   jax.ShapeDtypeStruct((N_HEADS, BLOCK, 2 * BLOCK), F32), jax.ShapeDtypeStruct((N_HEADS, 128), F32)],
        scratch_shapes=[pltpu.VMEM((seq, 256), F32), pltpu.VMEM(tile, F32), pltpu.VMEM(tile, F32),
                        pltpu.VMEM(tile, BF16), pltpu.VMEM(tile, BF16)],
        compiler_params=_cp(("arbitrary", "arbitrary"), VMEM_MID))(sinks, qkvg, qkvg, qkvg, du_a, a_out, lse, bias)


def _out_proj(name, u_parts, gate_arr, gate_blk, w_out, x, gmod, seq):
    t_tok = x.shape[0]
    nu = len(u_parts)

    def body(*refs):
        u_refs = refs[:nu]
        g_ref, w_ref, x_ref, gm_ref, yg_ref, y_ref, xn_ref = refs[nu:]
        u = jnp.concatenate([r[...].astype(F32) for r in u_refs], axis=1) if nu > 1 else u_refs[0][...].astype(F32)
        yg = (u * _silu(g_ref[...].astype(F32))).astype(BF16)
        yg_ref[...] = yg
        y = _nn(yg, w_ref[...])
        y_ref[...] = y.astype(BF16)
        xn_ref[...] = x_ref[...] + gm_ref[...] * y

    row = lambda w: pl.BlockSpec((TM, w), lambda i: (i, 0))
    in_specs = [row(u.shape[1]) for u in u_parts]
    in_specs += [pl.BlockSpec((TM, D_MODEL), lambda i: (i, gate_blk)),
                 pl.BlockSpec((D_MODEL, D_MODEL), lambda i: (0, 0)), row(D_MODEL),
                 pl.BlockSpec((None, 1, D_MODEL), lambda i: (i * TM // seq, 0, 0))]
    return _pallas(
        body, name=name, grid=(t_tok // TM,), in_specs=in_specs,
        out_specs=[row(D_MODEL)] * 3,
        out_shape=[jax.ShapeDtypeStruct((t_tok, D_MODEL), BF16)] * 2 + [jax.ShapeDtypeStruct((t_tok, D_MODEL), F32)],
        compiler_params=_cp(("arbitrary",), VMEM_MID))(*u_parts, gate_arr, w_out, x, gmod)


def _out_proj_bwd(name, dxn, gmod, y, w_out, seq, attn=None):
    t_tok = dxn.shape[0]
    tiles_per_seq = seq // TM

    def body(*refs):
        if attn is None:
            dxn_ref, gm_ref, y_ref, w_ref, dy_ref, dgm_ref, dyg_ref = refs
        else:
            dxn_ref, gm_ref, y_ref, w_ref, a_ref, b_ref, g_ref, dy_ref, dgm_ref, dua_ref, dub_ref, dg_ref = refs
        i = pl.program_id(0)
        dxv = dxn_ref[...]
        dy = (dxv * gm_ref[...]).astype(BF16)
        dy_ref[...] = dy

        @pl.when(i % tiles_per_seq == 0)
        def _():
            dgm_ref[...] = jnp.zeros_like(dgm_ref)

        dgm_ref[...] += jnp.sum(dxv * y_ref[...].astype(F32), axis=0, keepdims=True)
        dyg = _nn(dy, w_ref[...])
        if attn is None:
            dyg_ref[...] = dyg
        else:
            gt = g_ref[...].astype(F32)
            du = dyg * _silu(gt)
            dua_ref[...] = du[:, :512].astype(BF16)
            dub_ref[...] = du[:, 512:].astype(BF16)
            u = jnp.concatenate([a_ref[...].astype(F32), b_ref[...].astype(F32)], axis=1)
            dg_ref[...] = (dyg * u * _dsilu(gt)).astype(BF16)

    row = lambda w: pl.BlockSpec((TM, w), lambda i: (i, 0))
    mod_spec = pl.BlockSpec((None, 1, D_MODEL), lambda i: (i * TM // seq, 0, 0))
    in_specs = [row(D_MODEL), mod_spec, row(D_MODEL), pl.BlockSpec((D_MODEL, D_MODEL), lambda i: (0, 0))]
    out_specs = [row(D_MODEL), mod_spec]
    out_shape = [jax.ShapeDtypeStruct((t_tok, D_MODEL), BF16), jax.ShapeDtypeStruct(gmod.shape, F32)]
    args = [dxn, gmod, y, w_out]
    if attn is None:
        out_specs.append(row(D_MODEL))
        out_shape.append(jax.ShapeDtypeStruct((t_tok, D_MODEL), F32))
    else:
        in_specs += [row(512), row(512), pl.BlockSpec((TM, D_MODEL), lambda i: (i, C_GATE // D_MODEL))]
        out_specs += [row(512), row(512), row(D_MODEL)]
        out_shape += [jax.ShapeDtypeStruct((t_tok, 512), BF16)] * 2 + [jax.ShapeDtypeStruct((t_tok, D_MODEL), BF16)]
        args += list(attn)
    return _pallas(body, name=name, grid=(t_tok // TM,), in_specs=in_specs, out_specs=out_specs,
                   out_shape=out_shape, compiler_params=_cp(("arbitrary",), VMEM_MID))(*args)


def _norm_bwd(name, parts, w, x, g, scale, dxn, seq, rows_part=None):
    t_tok = x.shape[0]
    npart = len(parts)
    tiles_per_seq = seq // TM
    nrow_in = 0 if rows_part is None else 2

    def body(*refs):
        p_refs = refs[:npart]
        w_ref, x_ref, g_ref, sc_ref, dxn_ref = refs[npart:npart + 5]
        dx_ref, dss_ref, dg_ref = refs[npart + 5 + nrow_in:]
        i = pl.program_id(0)
        dh = jnp.zeros((TM, D_MODEL), F32)
        if rows_part is not None:
            r_ref, wr_ref = refs[npart + 5:npart + 7]
            dh = dh + _tn(r_ref[...].astype(BF16), wr_ref[...])
        for (arr, off), p_ref in zip(parts, p_refs):
            dh = dh + _nn(p_ref[...], w_ref[off:off + arr.shape[1], :])
        xv = x_ref[...]
        rstd = lax.rsqrt(jnp.mean(xv * xv, axis=-1, keepdims=True) + EPS)
        xhat = xv * rstd
        gv = g_ref[...]
        nrm = xhat * gv

        @pl.when(i % tiles_per_seq == 0)
        def _():
            dss_ref[...] = jnp.zeros_like(dss_ref)

        @pl.when(i == 0)
        def _():
            dg_ref[...] = jnp.zeros_like(dg_ref)

        dss_ref[0:1, :] += jnp.sum(dh, axis=0, keepdims=True)
        dss_ref[1:2, :] += jnp.sum(dh * nrm, axis=0, keepdims=True)
        dn = dh * (1.0 + sc_ref[...])
        dg_ref[0:1, :] += jnp.sum(dn * xhat, axis=0, keepdims=True)
        dxhat = dn * gv
        dx_ref[...] = rstd * (dxhat - xhat * jnp.mean(dxhat * xhat, axis=-1, keepdims=True)) + dxn_ref[...]

    row = lambda wd: pl.BlockSpec((TM, wd), lambda i: (i, 0))
    w_spec = pl.BlockSpec(w.shape, lambda i: (0, 0))
    in_specs = [row(a.shape[1]) for a, _ in parts]
    in_specs += [w_spec, row(D_MODEL), pl.BlockSpec((1, D_MODEL), lambda i: (0, 0)),
                 pl.BlockSpec((None, 1, D_MODEL), lambda i: (i * TM // seq, 0, 0)), row(D_MODEL)]
    args = [a for a, _ in parts] + [w, x, g, scale, dxn]
    if rows_part is not None:
        in_specs += [pl.BlockSpec((8, TM), lambda i: (0, i)), pl.BlockSpec((8, D_MODEL), lambda i: (0, 0))]
        args += list(rows_part)
    nseq = t_tok // seq
    return _pallas(
        body, name=name, grid=(t_tok // TM,), in_specs=in_specs,
        out_specs=[row(D_MODEL), pl.BlockSpec((None, 8, D_MODEL), lambda i: (i * TM // seq, 0, 0)),
                   pl.BlockSpec((8, D_MODEL), lambda i: (0, 0))],
        out_shape=[jax.ShapeDtypeStruct((t_tok, D_MODEL), F32), jax.ShapeDtypeStruct((nseq, 8, D_MODEL), F32),
                   jax.ShapeDtypeStruct((8, D_MODEL), F32)],
        compiler_params=_cp(("arbitrary",), VMEM_BIG))(*args)


def _dw(name, a, parts, blocked=None):
    t_tok, ka = a.shape
    tt = min(1024, t_tok)
    npart = len(parts)
    nt = t_tok // tt

    def body(*refs):
        a_ref = refs[0]
        p_refs = refs[1:1 + npart]
        o_refs = refs[1 + npart:1 + 2 * npart]
        acc_refs = refs[1 + 2 * npart:]
        t = pl.program_id(0)
        av = a_ref[...]
        for p_ref, acc in zip(p_refs, acc_refs):
            upd = _tn(av, p_ref[...])

            @pl.when(t == 0)
            def _():
                acc[...] = upd

            @pl.when(t > 0)
            def _():
                acc[...] += upd

        @pl.when(t == nt - 1)
        def _():
            for o_ref, acc in zip(o_refs, acc_refs):
                if blocked is None:
                    o_ref[...] = acc[...].astype(BF16)
                else:
                    for j in range(o_ref.shape[0]):
                        o_ref[j] = acc[:, j * blocked:(j + 1) * blocked].astype(BF16)

    in_specs = [pl.BlockSpec((tt, ka), lambda t: (t, 0))]
    in_specs += [pl.BlockSpec((tt, p.shape[1]), lambda t: (t, 0)) for p in parts]
    if blocked is None:
        out_shape = [jax.ShapeDtypeStruct((ka, p.shape[1]), BF16) for p in parts]
        out_specs = [pl.BlockSpec((ka, p.shape[1]), lambda t: (0, 0)) for p in parts]
    else:
        out_shape = [jax.ShapeDtypeStruct((p.shape[1] // blocked, ka, blocked), BF16) for p in parts]
        out_specs = [pl.BlockSpec((p.shape[1] // blocked, ka, blocked), lambda t: (0, 0, 0)) for p in parts]
    return _pallas(body, name=name, grid=(nt,), in_specs=in_specs, out_specs=out_specs, out_shape=out_shape,
                   scratch_shapes=[pltpu.VMEM((ka, p.shape[1]), F32) for p in parts],
                   compiler_params=_cp(("arbitrary",), VMEM_BIG))(a, *parts)


def _dw_rows(name, rows_t, h):
    t_tok = h.shape[0]
    tt = 512

    def body(r_ref, h_ref, o_ref):
        @pl.when(pl.program_id(0) == 0)
        def _():
            o_ref[...] = jnp.zeros_like(o_ref)

        o_ref[...] += _nn(r_ref[...].astype(BF16), h_ref[...])

    return _pallas(body, name=name, grid=(t_tok // tt,),
                   in_specs=[pl.BlockSpec((8, tt), lambda t: (0, t)), pl.BlockSpec((tt, D_MODEL), lambda t: (t, 0))],
                   out_specs=pl.BlockSpec((8, D_MODEL), lambda t: (0, 0)),
                   out_shape=jax.ShapeDtypeStruct((8, D_MODEL), F32),
                   compiler_params=_cp(("arbitrary",), VMEM_MID))(rows_t, h)


def _lru_gates(xc, blk, wa_ref, wx_ref, ba_ref, bx_ref, sp):
    cols = slice(blk * LRU_BLOCK_W, (blk + 1) * LRU_BLOCK_W)
    xb = xc[:, cols].astype(BF16)
    r = _sigmoid(_nn(xb, wa_ref[blk].astype(BF16)) + ba_ref[:, cols])
    ig = _sigmoid(_nn(xb, wx_ref[blk].astype(BF16)) + bx_ref[:, cols])
    log_a = -LRU_C * r * sp[:, cols]
    a = jnp.exp(log_a)
    x2 = 2.0 * log_a
    series = -x2 * (1.0 + x2 * (0.5 + x2 * (1.0 / 6.0)))
    z = jnp.where(x2 > -0.01, series, 1.0 - a * a)
    mult = z * lax.rsqrt(jnp.maximum(z, 1e-30))
    return xb, r, ig, a, mult


def _softplus_neg(lam):
    return jnp.maximum(-lam, 0.0) + jnp.log(1.0 + jnp.exp(-jnp.abs(lam)))


def _conv_taps(xe_ref, cw_ref, cb_ref):
    xc = cb_ref[...] + xe_ref[8:8 + TC, :] * cw_ref[3:4, :]
    for k in range(1, 4):
        xc = xc + xe_ref[8 - k:8 - k + TC, :] * cw_ref[3 - k:4 - k, :]
    return xc


def _lru_fwd(proj, cw, cb, w_a, b_a, w_x, b_x, lam, seq):
    t_tok = proj.shape[0]
    nc = seq // TC

    def body(x_ref, cw_ref, cb_ref, wa_ref, ba_ref, wx_ref, bx_ref, lam_ref, hs_ref, xe_s, a_s, u_s, h_s):
        c = pl.program_id(1)

        @pl.when(c == 0)
        def _():
            xe_s[0:8, :] = jnp.zeros((8, D_MODEL), F32)
            h_s[...] = jnp.zeros_like(h_s)

        xe_s[8:8 + TC, :] = x_ref[...]
        xc = _conv_taps(xe_s, cw_ref, cb_ref)
        sp = _softplus_neg(lam_ref[...])
        for blk in range(LRU_BLOCKS):
            cols = slice(blk * LRU_BLOCK_W, (blk + 1) * LRU_BLOCK_W)
            _, _, ig, a, mult = _lru_gates(xc, blk, wa_ref, wx_ref, ba_ref, bx_ref, sp)
            a_s[:, cols] = a
            u_s[:, cols] = mult * ig * xc[:, cols]

        def step(t, h):
            h = a_s[pl.ds(t, 1), :] * h + u_s[pl.ds(t, 1), :]
            hs_ref[pl.ds(t, 1), :] = h
            return h

        h_s[0:1, :] = lax.fori_loop(0, TC, step, h_s[0:1, :], unroll=8)
        xe_s[0:8, :] = xe_s[TC:TC + 8, :]

    full = lambda shape: pl.BlockSpec(shape, lambda b, c: (0,) * len(shape))
    return _pallas(
        body, name="lru_fwd", grid=(t_tok // seq, nc),
        in_specs=[pl.BlockSpec((TC, D_MODEL), lambda b, c: (b * nc + c, 0)), full((4, D_MODEL)), full((1, D_MODEL)),
                  full((LRU_BLOCKS, LRU_BLOCK_W, LRU_BLOCK_W)), full((1, D_MODEL)),
                  full((LRU_BLOCKS, LRU_BLOCK_W, LRU_BLOCK_W)), full((1, D_MODEL)), full((1, D_MODEL))],
        out_specs=pl.BlockSpec((TC, D_MODEL), lambda b, c: (b * nc + c, 0)),
        out_shape=jax.ShapeDtypeStruct((t_tok, D_MODEL), F32),
        scratch_shapes=[pltpu.VMEM((TC + 8, D_MODEL), F32), pltpu.VMEM((TC, D_MODEL), F32),
                        pltpu.VMEM((TC, D_MODEL), F32), pltpu.VMEM((8, D_MODEL), F32)],
        compiler_params=_cp(("arbitrary", "arbitrary"), VMEM_BIG))(proj, cw, cb, w_a, b_a, w_x, b_x, lam)


def _lru_bwd(proj, hs, dyh, cw, cb, w_a, b_a, w_x, b_x, lam, seq):
    t_tok = proj.shape[0]
    nc = seq // TC

    def body(x_ref, xh_ref, g_ref, hs_ref, hh_ref, dy_ref, cw_ref, cb_ref, wa_ref, ba_ref, wx_ref, bx_ref, lam_ref,
             dp_ref, dcw_ref, dvec_ref, dwa_ref, dwx_ref,
             xe_s, he_s, de_s, a_s, r_s, i_s, m_s, dh_s, carry_s):
        b, cr = pl.program_id(0), pl.program_id(1)
        c = nc - 1 - cr

        @pl.when(jnp.logical_and(b == 0, cr == 0))
        def _():
            dcw_ref[...] = jnp.zeros_like(dcw_ref)
            dvec_ref[...] = jnp.zeros_like(dvec_ref)
            dwa_ref[...] = jnp.zeros_like(dwa_ref)
            dwx_ref[...] = jnp.zeros_like(dwx_ref)

        @pl.when(cr == 0)
        def _():
            carry_s[...] = jnp.zeros_like(carry_s)
            de_s[TC:TC + 8, :] = jnp.zeros((8, D_MODEL), F32)

        first = c == 0
        xe_s[0:8, :] = jnp.where(first, 0.0, xh_ref[...])
        xe_s[8:8 + TC, :] = x_ref[...]
        he_s[0:8, :] = jnp.where(first, 0.0, hh_ref[...])
        he_s[8:8 + TC, :] = hs_ref[...]
        xc = _conv_taps(xe_s, cw_ref, cb_ref)
        lam_v = lam_ref[...]
        sp = _softplus_neg(lam_v)
        for blk in range(LRU_BLOCKS):
            cols = slice(blk * LRU_BLOCK_W, (blk + 1) * LRU_BLOCK_W)
            _, r, ig, a, mult = _lru_gates(xc, blk, wa_ref, wx_ref, ba_ref, bx_ref, sp)
            a_s[:, cols], r_s[:, cols], i_s[:, cols], m_s[:, cols] = a, r, ig, mult

        gt = g_ref[...]
        dyh = dy_ref[...]
        dh_s[...] = dyh * _silu(gt)
        dp_ref[:, D_MODEL:] = (dyh * hs_ref[...] * _dsilu(gt)).astype(BF16)

        def step(k, carry):
            t = TC - 1 - k
            dh = dh_s[pl.ds(t, 1), :] + carry
            dh_s[pl.ds(t, 1), :] = dh
            return a_s[pl.ds(t, 1), :] * dh

        carry_s[0:1, :] = lax.fori_loop(0, TC, step, carry_s[0:1, :], unroll=8)

        hprev = he_s[7:7 + TC, :]
        for blk in range(LRU_BLOCKS):
            cols = slice(blk * LRU_BLOCK_W, (blk + 1) * LRU_BLOCK_W)
            xcb = xc[:, cols]
            a, r, ig, mult, dh = a_s[:, cols], r_s[:, cols], i_s[:, cols], m_s[:, cols], dh_s[:, cols]
            spb = sp[:, cols]
            dmult = dh * ig * xcb
            di = dh * mult * xcb
            dxc = dh * mult * ig
            dla = dh * hprev[:, cols] * a - dmult * (a * a) * lax.rsqrt(jnp.maximum(mult * mult, 1e-30))
            dr = dla * (-LRU_C * spb)
            dsp = jnp.sum(dla * (-LRU_C * r), axis=0, keepdims=True)
            dga = dr * r * (1.0 - r)
            dgx = di * ig * (1.0 - ig)
            dga_b, dgx_b = dga.astype(BF16), dgx.astype(BF16)
            xb = xcb.astype(BF16)
            dxc = dxc + _nt(dga_b, wa_ref[blk].astype(BF16)) + _nt(dgx_b, wx_ref[blk].astype(BF16))
            dwa_ref[blk] += _tn(xb, dga_b)
            dwx_ref[blk] += _tn(xb, dgx_b)
            dvec_ref[1:2, cols] += jnp.sum(dga, axis=0, keepdims=True)
            dvec_ref[2:3, cols] += jnp.sum(dgx, axis=0, keepdims=True)
            dvec_ref[3:4, cols] += dsp * (-1.0 / (1.0 + jnp.exp(lam_v[:, cols])))
            de_s[0:TC, cols] = dxc

        dxc = de_s[0:TC, :]
        dvec_ref[0:1, :] += jnp.sum(dxc, axis=0, keepdims=True)
        dxr = dxc * cw_ref[3:4, :]
        dcw_ref[3:4, :] += jnp.sum(dxc * xe_s[8:8 + TC, :], axis=0, keepdims=True)
        for k in range(1, 4):
            dxr = dxr + de_s[k:k + TC, :] * cw_ref[3 - k:4 - k, :]
            dcw_ref[3 - k:4 - k, :] += jnp.sum(dxc * xe_s[8 - k:8 - k + TC, :], axis=0, keepdims=True)
        dp_ref[:, :D_MODEL] = dxr.astype(BF16)
        de_s[TC:TC + 8, :] = de_s[0:8, :]

    chunk = lambda col: pl.BlockSpec((TC, D_MODEL), lambda b, cr: (b * nc + nc - 1 - cr, col))
    halo = lambda col: pl.BlockSpec(
        (8, D_MODEL), lambda b, cr: (jnp.maximum((b * nc + nc - 1 - cr) * (TC // 8) - 1, 0), col))
    full = lambda shape: pl.BlockSpec(shape, lambda b, cr: (0,) * len(shape))
    wblk = (LRU_BLOCKS, LRU_BLOCK_W, LRU_BLOCK_W)
    return _pallas(
        body, name="lru_bwd", grid=(t_tok // seq, nc),
        in_specs=[chunk(0), halo(0), chunk(1), chunk(0), halo(0), chunk(0),
                  full((4, D_MODEL)), full((1, D_MODEL)), full(wblk), full((1, D_MODEL)), full(wblk),
                  full((1, D_MODEL)), full((1, D_MODEL))],
        out_specs=[pl.BlockSpec((TC, 2 * D_MODEL), lambda b, cr: (b * nc + nc - 1 - cr, 0)),
                   full((8, D_MODEL)), full((8, D_MODEL)), full(wblk), full(wblk)],
        out_shape=[jax.ShapeDtypeStruct((t_tok, 2 * D_MODEL), BF16), jax.ShapeDtypeStruct((8, D_MODEL), F32),
                   jax.ShapeDtypeStruct((8, D_MODEL), F32), jax.ShapeDtypeStruct(wblk, F32),
                   jax.ShapeDtypeStruct(wblk, F32)],
        scratch_shapes=[pltpu.VMEM((TC + 8, D_MODEL), F32), pltpu.VMEM((TC + 8, D_MODEL), F32),
                        pltpu.VMEM((TC + 8, D_MODEL), F32)]
        + [pltpu.VMEM((TC, D_MODEL), F32)] * 5 + [pltpu.VMEM((8, D_MODEL), F32)],
        compiler_params=_cp(("arbitrary", "arbitrary"), VMEM_BIG),
    )(proj, proj, proj, hs, hs, dyh, cw, cb, w_a, b_a, w_x, b_x, lam)


def _last_layer_tail(hs, proj, w_out, w_out_t, x, gmod, final_g, target, seq):
    t_tok = x.shape[0]
    tiles_per_seq = seq // TM

    def body(hs_ref, g_ref, w_ref, wt_ref, x_ref, gm_ref, fg_ref, t_ref,
             yg_ref, dx_ref, dy_ref, dyg_ref, dgm_ref, loss_ref, dfg_ref):
        i = pl.program_id(0)

        @pl.when(i == 0)
        def _():
            loss_ref[...] = jnp.zeros_like(loss_ref)
            dfg_ref[...] = jnp.zeros_like(dfg_ref)

        @pl.when(i % tiles_per_seq == 0)
        def _():
            dgm_ref[...] = jnp.zeros_like(dgm_ref)

        gm = gm_ref[...]
        yg = (hs_ref[...] * _silu(g_ref[...])).astype(BF16)
        yg_ref[...] = yg
        y = _nn(yg, w_ref[...])
        xv = x_ref[...] + gm * y
        gv = fg_ref[...]
        rstd = lax.rsqrt(jnp.mean(xv * xv, axis=-1, keepdims=True) + EPS)
        xhat = xv * rstd
        err = xhat * gv - t_ref[...]
        loss_ref[0:1, :] += jnp.sum(err * err, axis=0, keepdims=True) * (0.5 / D_MODEL)
        dout = err * (1.0 / D_MODEL)
        dfg_ref[0:1, :] += jnp.sum(dout * xhat, axis=0, keepdims=True)
        dxhat = dout * gv
        dxv = rstd * (dxhat - xhat * jnp.mean(dxhat * xhat, axis=-1, keepdims=True))
        dx_ref[...] = dxv
        dgm_ref[...] += jnp.sum(dxv * y, axis=0, keepdims=True)
        dy = (dxv * gm).astype(BF16)
        dy_ref[...] = dy
        dyg_ref[...] = _nn(dy, wt_ref[...])

    row = pl.BlockSpec((TM, D_MODEL), lambda i: (i, 0))
    acc = pl.BlockSpec((8, D_MODEL), lambda i: (0, 0))
    mod_spec = pl.BlockSpec((None, 1, D_MODEL), lambda i: (i * TM // seq, 0, 0))
    return _pallas(
        body, name="last_layer_tail", grid=(t_tok // TM,),
        in_specs=[row, pl.BlockSpec((TM, D_MODEL), lambda i: (i, 1)), pl.BlockSpec((D_MODEL, D_MODEL), lambda i: (0, 0)),
                  pl.BlockSpec((D_MODEL, D_MODEL), lambda i: (0, 0)),
                  row, mod_spec, pl.BlockSpec((1, D_MODEL), lambda i: (0, 0)), row],
        out_specs=[row, row, row, row, mod_spec, acc, acc],
        out_shape=[jax.ShapeDtypeStruct((t_tok, D_MODEL), BF16), jax.ShapeDtypeStruct((t_tok, D_MODEL), F32),
                   jax.ShapeDtypeStruct((t_tok, D_MODEL), BF16), jax.ShapeDtypeStruct((t_tok, D_MODEL), F32),
                   jax.ShapeDtypeStruct(gmod.shape, F32), jax.ShapeDtypeStruct((8, D_MODEL), F32),
                   jax.ShapeDtypeStruct((8, D_MODEL), F32)],
        compiler_params=_cp(("arbitrary",), VMEM_BIG))(hs, proj, w_out, w_out_t, x, gmod, final_g, target)


def _final_loss(x, g, target):
    t_tok = x.shape[0]

    def body(x_ref, g_ref, t_ref, dx_ref, loss_ref, dg_ref):
        @pl.when(pl.program_id(0) == 0)
        def _():
            loss_ref[...] = jnp.zeros_like(loss_ref)
            dg_ref[...] = jnp.zeros_like(dg_ref)

        xv = x_ref[...]
        gv = g_ref[...]
        rstd = lax.rsqrt(jnp.mean(xv * xv, axis=-1, keepdims=True) + EPS)
        xhat = xv * rstd
        err = xhat * gv - t_ref[...]
        loss_ref[0:1, :] += jnp.sum(err * err, axis=0, keepdims=True) * (0.5 / D_MODEL)
        dout = err * (1.0 / D_MODEL)
        dg_ref[0:1, :] += jnp.sum(dout * xhat, axis=0, keepdims=True)
        dxhat = dout * gv
        dx_ref[...] = rstd * (dxhat - xhat * jnp.mean(dxhat * xhat, axis=-1, keepdims=True))

    row = pl.BlockSpec((TM, D_MODEL), lambda i: (i, 0))
    acc = pl.BlockSpec((8, D_MODEL), lambda i: (0, 0))
    return _pallas(body, name="final_loss", grid=(t_tok // TM,),
                   in_specs=[row, pl.BlockSpec((1, D_MODEL), lambda i: (0, 0)), row],
                   out_specs=[row, acc, acc],
                   out_shape=[jax.ShapeDtypeStruct((t_tok, D_MODEL), F32)] + [jax.ShapeDtypeStruct((8, D_MODEL), F32)] * 2,
                   compiler_params=_cp(("arbitrary",), VMEM_MID))(x, g, target)


def _adam_math(w, g, m, v):
    m_new = ADAM_B1 * m + (1.0 - ADAM_B1) * g
    v_new = ADAM_B2 * v + (1.0 - ADAM_B2) * (g * g)
    m_hat = m_new / (1.0 - ADAM_B1 ** ADAM_STEP)
    v_hat = v_new / (1.0 - ADAM_B2 ** ADAM_STEP)
    delta = -ADAM_LR * (m_hat / (jnp.sqrt(v_hat) + ADAM_EPS) + ADAM_WD * w)
    return delta, m_new, v_new


def _sum_leading(name, x, out_dtype=F32):
    n, rows, cols = x.shape
    tr = PACK_ROWS if rows % PACK_ROWS == 0 else rows

    def body(x_ref, o_ref):
        acc = x_ref[0].astype(F32)
        for d in range(1, n):
            acc = acc + x_ref[d].astype(F32)
        o_ref[...] = acc.astype(out_dtype)

    return _pallas(body, name=name, grid=(rows // tr,),
                   in_specs=[pl.BlockSpec((n, tr, cols), lambda i: (0, i, 0))],
                   out_specs=pl.BlockSpec((tr, cols), lambda i: (i, 0)),
                   out_shape=jax.ShapeDtypeStruct((rows, cols), out_dtype),
                   compiler_params=_cp(("arbitrary",), VMEM_MID))(x)


def _adamw(name, w, m, v, g=None, parts=None):
    rows, cols = w.shape
    tr = rows if rows <= 256 else 256

    def body(*refs):
        w_ref, m_ref, v_ref, g_in, g_ref, d_ref, mo_ref, vo_ref = refs
        if parts is None:
            gv = g_in[...]
        else:
            acc = g_in[0].astype(F32)
            for d in range(1, parts.shape[0]):
                acc = acc + g_in[d].astype(F32)
            gv = acc[:, :cols]
        delta, m_new, v_new = _adam_math(w_ref[...], gv, m_ref[...], v_ref[...])
        g_ref[...] = gv
        d_ref[...] = delta
        mo_ref[...] = m_new
        vo_ref[...] = v_new

    row = pl.BlockSpec((tr, cols), lambda i: (i, 0))
    if parts is None:
        g_spec, g_arg = row, g
    else:
        g_spec, g_arg = pl.BlockSpec((parts.shape[0], tr, parts.shape[2]), lambda i: (0, i, 0)), parts
    return _pallas(body, name=name, grid=(rows // tr,), in_specs=[row, row, row, g_spec], out_specs=[row] * 4,
                   out_shape=[jax.ShapeDtypeStruct((rows, cols), F32)] * 4,
                   compiler_params=_cp(("arbitrary",), VMEM_MID))(w, m, v, g_arg)


def _adamw_many(name, groups):
    ntens = len(groups)

    def body(*refs):
        ins, outs = refs[:4 * ntens], refs[4 * ntens:]
        for k in range(ntens):
            w_ref, m_ref, v_ref, g_ref = ins[4 * k:4 * k + 4]
            gv = g_ref[...]
            delta, m_new, v_new = _adam_math(w_ref[...], gv, m_ref[...], v_ref[...])
            for o_ref, val in zip(outs[4 * k:4 * k + 4], (gv, delta, m_new, v_new)):
                o_ref[...] = val

    flat = [a for grp in groups for a in grp]
    out_shape = [jax.ShapeDtypeStruct(grp[0].shape, F32) for grp in groups for _ in range(4)]
    outs = _pallas(body, name=name, out_shape=out_shape, compiler_params=_cp(vmem=VMEM_MID))(*flat)
    return [tuple(outs[4 * k:4 * k + 4]) for k in range(ntens)]


def _pack_rows(arrs):
    rows, meta, total = [], [], 0
    for a in arrs:
        flat = a.reshape(-1)
        nrow = -(-flat.shape[0] // 1024) * 8
        rows.append(jnp.pad(flat, (0, nrow * 128 - flat.shape[0])).reshape(nrow, 128))
        meta.append((a.shape, flat.shape[0], nrow))
        total += nrow
    tail = -total % PACK_ROWS
    if tail:
        rows.append(jnp.zeros((tail, 128), F32))
    return jnp.concatenate(rows, axis=0), meta


def _unpack_rows(packed, meta):
    out, r0 = [], 0
    for shape, size, nrow in meta:
        out.append(packed[r0:r0 + nrow].reshape(-1)[:size].reshape(shape))
        r0 += nrow
    return out


WEIGHTS = ["rel_bias", "norm_g", "ada_w", "ada_b", "attn_w_in", "attn_sinks", "attn_b_f", "attn_w_out", "lru_w_in",
           "lru_conv_w", "lru_conv_b", "lru_w_a", "lru_b_a", "lru_w_x", "lru_b_x", "lru_lambda", "lru_w_out", "final_g"]
BIG = ["ada_w", "attn_w_in", "attn_w_out", "lru_w_in", "lru_w_out"]
PACK_ROWS = 256


def kernel(x, c, rel_bias, norm_g, ada_w, ada_b, attn_w_in, attn_sinks, attn_b_f, attn_w_out, lru_w_in, lru_conv_w, lru_conv_b, lru_w_a, lru_b_a, lru_w_x, lru_b_x, lru_lambda, lru_w_out, final_g, loss_target, m_rel_bias, m_norm_g, m_ada_w, m_ada_b, m_attn_w_in, m_attn_sinks, m_attn_b_f, m_attn_w_out, m_lru_w_in, m_lru_conv_w, m_lru_conv_b, m_lru_w_a, m_lru_b_a, m_lru_w_x, m_lru_b_x, m_lru_lambda, m_lru_w_out, m_final_g, v_rel_bias, v_norm_g, v_ada_w, v_ada_b, v_attn_w_in, v_attn_sinks, v_attn_b_f, v_attn_w_out, v_lru_w_in, v_lru_conv_w, v_lru_conv_b, v_lru_w_a, v_lru_b_a, v_lru_w_x, v_lru_b_x, v_lru_lambda, v_lru_w_out, v_final_g):
    nseq, seq, _ = x.shape
    t_tok = nseq * seq
    me = 4 * lax.axis_index("x") + 2 * lax.axis_index("y") + lax.axis_index("c")
    x0 = x.reshape(t_tok, D_MODEL)
    target = loss_target.reshape(t_tok, D_MODEL)

    w_in_pad = jnp.pad(attn_w_in[0].astype(BF16), ((0, 0), (0, SHARD_W_PAD - SHARD_W_IN)))
    vec_shard = jnp.concatenate([lru_conv_w[0], lru_conv_b, lru_b_a, lru_b_x, lru_lambda], axis=0)
    g_vec, g_c = _exchange("gather_small", [vec_shard, c], [])
    first_handle, first_token = _exchange_start("gather_first_start", [w_in_pad], [], after=g_vec)
    vec_full = jnp.transpose(g_vec, (1, 0, 2)).reshape(8, D_MODEL)
    conv_w, conv_b, b_a, b_x, lam = vec_full[0:4], vec_full[4:5], vec_full[5:6], vec_full[6:7], vec_full[7:8]
    c_all = g_c.reshape(N_DEV * nseq, D_MODEL)

    ncol = ada_w.shape[2]
    ada_b_slice = lax.dynamic_slice(ada_b.reshape(2, N_DEV, ncol), (0, me, 0), (2, 1, ncol))
    ada_b_slice = ada_b_slice + first_token[0:1, 0:1]
    mod_part = _ada_mod(c_all, ada_w, ada_b_slice)
    (g_mod,) = _exchange("gather_mod", [mod_part], [])
    mine = lax.dynamic_slice(g_mod, (0, 0, me * nseq, 0), (N_DEV, 2, nseq, ncol))
    mod = jnp.transpose(mine, (1, 2, 0, 3)).reshape(2, nseq, 3 * D_MODEL)
    shift = [mod[l, :, 0:D_MODEL].reshape(nseq, 1, D_MODEL) for l in range(2)]
    scale = [mod[l, :, D_MODEL:2 * D_MODEL].reshape(nseq, 1, D_MODEL) for l in range(2)]
    gmod = [mod[l, :, 2 * D_MODEL:].reshape(nseq, 1, D_MODEL) for l in range(2)]

    onehot = _bucket_onehot()
    bias = _bias_expand(jnp.transpose(rel_bias), onehot).reshape(N_HEADS, BLOCK, 2 * BLOCK)
    sinks = attn_sinks.reshape(N_HEADS)
    b_f = attn_b_f.reshape(N_HEADS, 1)
    h0 = _norm_mod("norm_mod0", x0, norm_g[0:1], shift[0], scale[0], seq)
    (g_w_in,) = _exchange_wait("gather_first_wait", first_handle, after=h0)
    g_w_in = _with_own(g_w_in, w_in_pad, me)
    later_w = [attn_w_out[0].astype(BF16), lru_w_in[0].astype(BF16), lru_w_out[0].astype(BF16)]
    later_handle, later_token = _exchange_start("gather_later_start", later_w, [], after=g_w_in)
    w_full = jnp.transpose(g_w_in[:, :, :SHARD_W_IN], (1, 0, 2)).reshape(D_MODEL, N_DEV * SHARD_W_IN)
    w_aq, w_ak, w_av = w_full[:, 0:512], w_full[:, 512:640], w_full[:, 640:768]
    w_bq, w_bk, w_bv = w_full[:, 768:1280], w_full[:, 1280:1792], w_full[:, 1792:2304]
    w_f, w_gate = w_full[:, 2304:2312], w_full[:, 2312:3336]
    w_main = jnp.concatenate([w_bq, w_bk, w_bv, w_aq, w_gate, w_ak, w_av], axis=1)
    wf_t = jnp.transpose(w_f)
    wf_t0 = wf_t + later_token[0:1, 0:1].astype(BF16)
    qkvg, fl_t = _proj("proj0", h0, w_main, wf_t0, BF16)
    f_row, f_col = _fox_prep(fl_t, b_f, seq)
    a_out, lse_a = _swa_fwd(qkvg, bias, sinks, seq)
    q_aug, k_aug, kt_aug, vt = _fox_aug(qkvg, f_col, seq)
    b_out, lse_b = _fox_fwd_t(q_aug, k_aug, vt, seq)
    g_later = _exchange_wait("gather_later_wait", later_handle, after=lse_b)
    w_out0, g_lru_in, w_out1 = (_with_own(g, w, me) for g, w in zip(g_later, later_w))
    w_out0, w_out1 = w_out0.reshape(D_MODEL, D_MODEL), w_out1.reshape(D_MODEL, D_MODEL)
    w_out0_t, w_out1_t, w_main_t = jnp.transpose(w_out0), jnp.transpose(w_out1), jnp.transpose(w_main)
    lru_in_t = jnp.transpose(g_lru_in, (0, 2, 1)).reshape(2 * D_MODEL, D_MODEL)
    yg0, y0, x1 = _out_proj("out_proj0", [a_out, b_out], qkvg, C_GATE // D_MODEL, w_out0, x0, gmod[0], seq)

    h1, proj1 = _norm_proj("norm_proj1", x1, norm_g[1:2], shift[1], scale[1], g_lru_in, seq, F32)
    hs = _lru_fwd(proj1, conv_w, conv_b, lru_w_a[0], b_a, lru_w_x[0], b_x, lam, seq)

    yg1, dx2, dy1, dyh, dgm1, loss_rows, dfinal_rows = _last_layer_tail(
        hs, proj1, w_out1, w_out1_t, x1, gmod[1], final_g.reshape(1, D_MODEL), target, seq)

    dproj1, dcw, dvec, dw_a, dw_x = _lru_bwd(proj1, hs, dyh, conv_w, conv_b, lru_w_a[0], b_a, lru_w_x[0], b_x, lam, seq)
    dx1, dss1, dg1 = _norm_bwd("norm1_bwd", [(dproj1, 0)], lru_in_t, x1, norm_g[1:2], scale[1], dx2, seq)
    (p_w_out1,) = _dw("dw_out1", yg1, [dy1])
    (p_lru_in,) = _dw("dw_lru_in", h1, [dproj1], blocked=2 * D_MODEL // N_DEV)

    rows_out = D_MODEL // N_DEV
    gpack1, gmeta1 = _pack_rows([dcw[0:4], dvec[0:4], dg1[0], dfinal_rows[0]])
    dwax = jnp.stack([dw_a, dw_x]).astype(BF16)
    own1 = [gpack1, dwax, p_lru_in, p_w_out1.reshape(N_DEV, rows_out, D_MODEL)]
    grads1_handle, grads1_token = _exchange_start("grads1_start", own1[:2], own1[2:], after=dx1)

    gmod0 = gmod[0] + grads1_token[0:1, 0:1]
    dy0, dgm0, du_a, du_b, dgate = _out_proj_bwd("out_proj0_bwd", dx1, gmod0, y0, w_out0_t, seq,
                                                  attn=(a_out, b_out, qkvg))
    dq_a, dkv_a, dbias, dsink = _swa_bwd(qkvg, du_a, a_out, lse_a, bias, sinks, seq)
    dq_b, dk_b, dv_b, df4 = _fox_bwd_t(q_aug, k_aug, kt_aug, qkvg, du_b, b_out, lse_b, seq)
    dfl_t, db_f = _fox_post(df4.reshape(N_HEADS, t_tok), fl_t, b_f, seq)
    parts0 = [(dq_b, C_BQ), (dk_b, C_BK), (dv_b, C_BV), (dq_a, C_AQ), (dgate, C_GATE), (dkv_a, C_AK)]
    (p_w_out0,) = _dw("dw_out0", yg0, [dy0])
    pw_bq, pw_bk, pw_bv, pw_aq, pw_gate, pw_akv = _dw("dw_attn_in", h0, [p for p, _ in parts0])
    pw_f = _dw_rows("dw_f", dfl_t, h0)

    p_w_in = jnp.concatenate([pw_aq, pw_akv, pw_bq, pw_bk, pw_bv, jnp.transpose(pw_f).astype(BF16), pw_gate], axis=1)
    p_w_in = jnp.transpose(p_w_in.reshape(D_MODEL, N_DEV, SHARD_W_IN), (1, 0, 2))
    p_w_in = jnp.pad(p_w_in, ((0, 0), (0, 0), (0, SHARD_W_PAD - SHARD_W_IN)))
    own0 = [p_w_in, p_w_out0.reshape(N_DEV, rows_out, D_MODEL)]
    landed1 = _exchange_wait("grads1_wait", grads1_handle, after=p_w_in)
    grads0_handle, grads0_token = _exchange_start("grads0_start", [], own0, after=landed1[0])
    scale0 = scale[0] + grads0_token[0:1, 0:1]
    dx0, dss0, dg0 = _norm_bwd("norm0_bwd", parts0, w_main_t, x0, norm_g[0:1], scale0, dx1, seq,
                               rows_part=(dfl_t, wf_t))
    dbias_t = _bias_reduce(dbias.reshape(N_HEADS, BLOCK * 2 * BLOCK), onehot)

    gpack0, gmeta0 = _pack_rows([jnp.transpose(dbias_t), dg0[0], dsink[:, 0], db_f[:, 0], loss_rows[0]])
    dmod = jnp.stack([jnp.concatenate([dss[:, 0], dss[:, 1], dgm[:, 0]], axis=1)
                      for dss, dgm in ((dss0, dgm0), (dss1, dgm1))], axis=1)
    g_small0, g_dmod = _exchange("exchange_small", [gpack0, dmod], [])
    landed0 = _exchange_wait("grads0_wait", grads0_handle, after=g_small0)
    r_w_in, r_w_out0 = (_with_own(g, lax.dynamic_index_in_dim(a, me, 0, keepdims=False), me)
                        for g, a in zip(landed0, own0))
    g_small1, g_dwax = (_with_own(g, a, me) for g, a in zip(landed1[:2], own1[:2]))
    r_lru_in, r_w_out1 = (_with_own(g, lax.dynamic_index_in_dim(a, me, 0, keepdims=False), me)
                          for g, a in zip(landed1[2:], own1[2:]))

    d_rel, d_g0, d_sinks, d_b_f, loss_cols = _unpack_rows(_sum_leading("sum_small0", g_small0), gmeta0)
    loss = jnp.sum(loss_cols)
    d_cw, d_vec, d_g1, d_final_g = _unpack_rows(_sum_leading("sum_small1", g_small1), gmeta1)
    d_norm_g = jnp.stack([d_g0, d_g1])
    d_wax = _sum_leading("sum_dwax", g_dwax.reshape(N_DEV, 2 * LRU_BLOCKS * LRU_BLOCK_W, LRU_BLOCK_W))
    d_wa, d_wx = d_wax[:LRU_BLOCKS * LRU_BLOCK_W], d_wax[LRU_BLOCKS * LRU_BLOCK_W:]
    cols = lambda a: lax.dynamic_slice(a, (0, me * LRU_BLOCK_W), (a.shape[0], LRU_BLOCK_W))
    dmod_all = g_dmod.reshape(N_DEV * nseq, 2 * 3 * D_MODEL)
    d_ada_b = _sum_leading("sum_ada_b", dmod_all.reshape(N_DEV * nseq, 2 * 3 * D_MODEL // 128, 128)).reshape(2, 3 * D_MODEL)
    dmod_slice = lax.dynamic_slice(dmod_all.reshape(N_DEV * nseq, 2, N_DEV, ncol), (0, 0, me, 0),
                                   (N_DEV * nseq, 2, 1, ncol)).reshape(N_DEV * nseq, 2, ncol)
    d_ada_w = _ada_w_grad(c_all, jnp.transpose(dmod_slice, (1, 0, 2)))

    given = dict(
        rel_bias=(rel_bias, m_rel_bias, v_rel_bias), norm_g=(norm_g, m_norm_g, v_norm_g),
        ada_w=(ada_w, m_ada_w, v_ada_w), ada_b=(ada_b, m_ada_b, v_ada_b),
        attn_w_in=(attn_w_in, m_attn_w_in, v_attn_w_in), attn_sinks=(attn_sinks, m_attn_sinks, v_attn_sinks),
        attn_b_f=(attn_b_f, m_attn_b_f, v_attn_b_f), attn_w_out=(attn_w_out, m_attn_w_out, v_attn_w_out),
        lru_w_in=(lru_w_in, m_lru_w_in, v_lru_w_in), lru_conv_w=(lru_conv_w, m_lru_conv_w, v_lru_conv_w),
        lru_conv_b=(lru_conv_b, m_lru_conv_b, v_lru_conv_b), lru_w_a=(lru_w_a, m_lru_w_a, v_lru_w_a),
        lru_b_a=(lru_b_a, m_lru_b_a, v_lru_b_a), lru_w_x=(lru_w_x, m_lru_w_x, v_lru_w_x),
        lru_b_x=(lru_b_x, m_lru_b_x, v_lru_b_x), lru_lambda=(lru_lambda, m_lru_lambda, v_lru_lambda),
        lru_w_out=(lru_w_out, m_lru_w_out, v_lru_w_out), final_g=(final_g, m_final_g, v_final_g))
    results = {}

    def big(name, shape2d, g=None, parts=None):
        w, m, v = (a.reshape(shape2d) for a in given[name])
        outs = _adamw("adamw_" + name, w, m, v, g=g, parts=parts)
        results[name] = tuple(o.reshape(given[name][0].shape) for o in outs)

    big("ada_w", (2 * D_MODEL, ncol), g=d_ada_w.reshape(2 * D_MODEL, ncol))
    big("attn_w_in", (D_MODEL, SHARD_W_IN), parts=r_w_in)
    big("attn_w_out", (rows_out, D_MODEL), parts=r_w_out0)
    big("lru_w_in", (D_MODEL, 2 * D_MODEL // N_DEV), parts=r_lru_in)
    big("lru_w_out", (rows_out, D_MODEL), parts=r_w_out1)

    small_grads = dict(
        rel_bias=d_rel, norm_g=d_norm_g, ada_b=d_ada_b, attn_sinks=d_sinks.reshape(1, N_HEADS),
        attn_b_f=d_b_f.reshape(1, N_HEADS), lru_conv_w=cols(d_cw).reshape(1, 4, LRU_BLOCK_W),
        lru_conv_b=cols(d_vec[0:1]), lru_w_a=d_wa.reshape(lru_w_a.shape), lru_b_a=cols(d_vec[1:2]),
        lru_w_x=d_wx.reshape(lru_w_x.shape), lru_b_x=cols(d_vec[2:3]), lru_lambda=cols(d_vec[3:4]),
        final_g=d_final_g)
    small = [n for n in WEIGHTS if n not in BIG]
    as2d = lambda a: a.reshape(-1, a.shape[-1])
    outs = _adamw_many("adamw_small", [tuple(as2d(a) for a in given[n]) + (as2d(small_grads[n]),) for n in small])
    for n, group in zip(small, outs):
        results[n] = tuple(o.reshape(given[n][0].shape) for o in group)

    grad_x = dx0.reshape(x.shape)
    out = [loss, grad_x]
    for j in range(4):
        out += [results[n][j] for n in WEIGHTS]
    return tuple(out)
```

```python
import functools
import math

import jax
import jax.numpy as jnp
from jax import lax
from jax.experimental import pallas as pl
from jax.experimental.pallas import tpu as pltpu

F32 = jnp.float32
BF16 = jnp.bfloat16
HI = lax.Precision.HIGHEST
MESH = pl.DeviceIdType.MESH

N_DEV = 8
D_MODEL = 1024
HEAD_DIM = 64
N_HEADS = 8
KV_GROUP = 4
BLOCK = 128
REL_BUCKETS = 32
REL_MAX_EXACT = 16
REL_MAX_DIST = 128
LRU_BLOCKS = 8
LRU_BLOCK_W = 128
LRU_C = 8.0
EPS = 1e-6
SCALE = HEAD_DIM ** -0.5
NEG = -1e30

ADAM_LR = 0.001
ADAM_B1 = 0.9
ADAM_B2 = 0.999
ADAM_EPS = 1e-08
ADAM_WD = 0.01
ADAM_STEP = 10

C_BQ, C_BK, C_BV, C_AQ, C_GATE, C_AK, C_AV = 0, 512, 1024, 1536, 2048, 3072, 3200
N_MAIN = 3328
SHARD_W_IN = 417
SHARD_W_PAD = 512

TM = 512
TQ = 256
TK = 128
TKB = 256
TC = 512
SWA_SUB = 2
VMEM_BIG = 56 * 1024 * 1024
VMEM_MID = 40 * 1024 * 1024


def _pallas(body, **kw):
    return pl.pallas_call(body, **kw)


def _cp(sem=None, vmem=None):
    kw = {}
    if sem is not None:
        kw["dimension_semantics"] = sem
    if vmem is not None:
        kw["vmem_limit_bytes"] = vmem
    return pltpu.CompilerParams(**kw)


def _nn(a, b, precision=None):
    return jnp.dot(a, b, preferred_element_type=F32, precision=precision)


def _nt(a, b, precision=None):
    return lax.dot_general(a, b, (((1,), (1,)), ((), ())), preferred_element_type=F32, precision=precision)


def _tn(a, b, precision=None):
    return lax.dot_general(a, b, (((0,), (0,)), ((), ())), preferred_element_type=F32, precision=precision)


def _sigmoid(x):
    return 1.0 / (1.0 + jnp.exp(-x))


def _silu(x):
    return x * _sigmoid(x)


def _dsilu(x):
    s = _sigmoid(x)
    return s * (1.0 + x * (1.0 - s))


def _neg_expm1(x):
    poly = x * (1.0 + x * (0.5 + x * (1.0 / 6.0 + x * (1.0 / 24.0))))
    return -jnp.where(jnp.abs(x) < 0.05, poly, jnp.exp(x) - 1.0)


def _col(tile, idx):
    lane = lax.broadcasted_iota(jnp.int32, tile.shape, 1)
    return jnp.sum(jnp.where(lane == idx, tile, 0.0), axis=1, keepdims=True)


def _row(tile, idx):
    sub = lax.broadcasted_iota(jnp.int32, tile.shape, 0)
    return jnp.sum(jnp.where(sub == idx, tile, 0.0), axis=0, keepdims=True)


def _exchange(name, gathers, scatters, axes=("x", "y", "c"), chunks=1):
    ng, n = len(gathers), len(gathers) + len(scatters)
    ins = list(gathers) + list(scatters)
    group = 2 ** len(axes)

    def body(*refs):
        in_refs, out_refs = refs[:n], refs[n:2 * n]
        send_sems, recv_sems, loc_sems = refs[2 * n:]
        coord = {a: lax.axis_index(a) for a in ("x", "y", "c")}

        def member(r):
            pc = dict(coord)
            idx = 0
            for k, a in enumerate(axes):
                if r & (1 << (len(axes) - 1 - k)):
                    pc[a] = 1 - coord[a]
                idx = 2 * idx + pc[a]
            return (pc["x"], pc["y"], pc["c"]), idx

        _, me = member(0)

        def peer(r):
            return member(r)

        local, sends, recvs = [], [], []
        for k in range(n):
            mine = in_refs[k] if k < ng else in_refs[k].at[me]
            cp = pltpu.make_async_copy(mine, out_refs[k].at[me], loc_sems.at[k])
            cp.start()
            local.append(cp)
            lead = mine.shape[0]
            nchunk = max(q for q in range(1, chunks + 1) if lead % q == 0)
            step = lead // nchunk
            for r in range(1, group):
                pid, pidx = peer(r)
                src = in_refs[k] if k < ng else in_refs[k].at[pidx]
                for q in range(nchunk):
                    rows = pl.ds(q * step, step)
                    sems = dict(send_sem=send_sems.at[r - 1, k, q], recv_sem=recv_sems.at[r - 1, k, q],
                                device_id=pid, device_id_type=MESH)
                    snd = pltpu.make_async_remote_copy(src_ref=src.at[rows], dst_ref=out_refs[k].at[me].at[rows], **sems)
                    snd.start()
                    sends.append(snd)
                    recvs.append(pltpu.make_async_remote_copy(
                        src_ref=src.at[rows], dst_ref=out_refs[k].at[pidx].at[rows], **sems))
        for rc in recvs:
            rc.wait_recv()
        for snd in sends:
            snd.wait_send()
        for cp in local:
            cp.wait()

    out_shape = [jax.ShapeDtypeStruct((group,) + a.shape, a.dtype) for a in gathers]
    out_shape += [jax.ShapeDtypeStruct(a.shape, a.dtype) for a in scatters]
    any_spec = pl.BlockSpec(memory_space=pl.ANY)
    return _pallas(
        body, name=name, out_shape=out_shape,
        in_specs=[any_spec] * n, out_specs=[any_spec] * n,
        scratch_shapes=[pltpu.SemaphoreType.DMA((group - 1, n, chunks)), pltpu.SemaphoreType.DMA((group - 1, n, chunks)),
                        pltpu.SemaphoreType.DMA((n,))],
    )(*ins)


def _peer_of(r):
    x, y, c = lax.axis_index("x"), lax.axis_index("y"), lax.axis_index("c")
    px = 1 - x if r & 4 else x
    py = 1 - y if r & 2 else y
    pc = 1 - c if r & 1 else c
    return (px, py, pc), 4 * px + 2 * py + pc


def _split_copies(in_refs, land_refs, send_sems, recv_sems, ng, with_recv):
    _, me = _peer_of(0)
    pairs = []
    for k, (src_ref, land) in enumerate(zip(in_refs, land_refs)):
        for r in range(1, N_DEV):
            pid, pidx = _peer_of(r)
            src = src_ref if k < ng else src_ref.at[pidx]
            slot = (N_DEV - 1) * k + r - 1
            sems = dict(send_sem=send_sems.at[slot], recv_sem=recv_sems.at[slot], device_id=pid, device_id_type=MESH)
            send = pltpu.make_async_remote_copy(src_ref=src, dst_ref=land.at[me], **sems)
            recv = pltpu.make_async_remote_copy(src_ref=src, dst_ref=land.at[pidx], **sems) if with_recv else None
            pairs.append((send, recv))
    return pairs


def _exchange_start(name, gathers, scatters, after):
    ng, n = len(gathers), len(gathers) + len(scatters)
    ins = list(gathers) + list(scatters)
    lands = [jax.ShapeDtypeStruct((N_DEV,) + a.shape, a.dtype) for a in gathers]
    lands += [jax.ShapeDtypeStruct(a.shape, a.dtype) for a in scatters]

    def body(*refs):
        in_refs, land_refs = refs[:n], refs[n:2 * n]
        send_sems, recv_sems = refs[2 * n + 1:2 * n + 3]
        token = refs[-1]
        for send, _ in _split_copies(in_refs, land_refs, send_sems, recv_sems, ng, False):
            send.start()
        token[...] = jnp.zeros_like(token)

    hbm = pl.BlockSpec(memory_space=pltpu.HBM)
    sem = pl.BlockSpec(memory_space=pltpu.SEMAPHORE)
    sem_shape = pltpu.SemaphoreType.DMA(((N_DEV - 1) * n,))
    out_shape = [sem_shape, sem_shape] + [pltpu.HBM(a.shape, a.dtype) for a in ins]
    out_shape += [pltpu.HBM(l.shape, l.dtype) for l in lands] + [jax.ShapeDtypeStruct((8, 128), F32)]
    args = [pltpu.with_memory_space_constraint(a, pltpu.HBM) for a in ins]
    args += [pltpu.with_memory_space_constraint(lax.empty(l.shape, l.dtype), pltpu.HBM) for l in lands]
    outs = _pallas(
        body, name=name, out_shape=out_shape,
        in_specs=[hbm] * (2 * n) + [pl.BlockSpec(memory_space=pl.ANY)],
        out_specs=[sem, sem] + [hbm] * (2 * n) + [pl.BlockSpec(memory_space=pltpu.VMEM)],
        input_output_aliases={i: 2 + i for i in range(2 * n)},
        compiler_params=pltpu.CompilerParams(has_side_effects=pltpu.SideEffectType.DATAFLOW_SIDE_EFFECTING),
    )(*args, after)
    return (outs[0], outs[1], list(outs[2:2 + n]), list(outs[2 + n:2 + 2 * n]), ng), outs[-1]


def _exchange_wait(name, handle, after):
    send_sems, recv_sems, srcs, lands, ng = handle
    n = len(srcs)

    def body(*refs):
        in_refs, land_refs = refs[:n], refs[n:2 * n]
        send_ref, recv_ref = refs[2 * n:2 * n + 2]
        for send, recv in _split_copies(in_refs, land_refs, send_ref, recv_ref, ng, True):
            send.wait_send()
            recv.wait_recv()

    hbm = pl.BlockSpec(memory_space=pltpu.HBM)
    sem = pl.BlockSpec(memory_space=pltpu.SEMAPHORE)
    outs = _pallas(
        body, name=name, out_shape=[pltpu.HBM(a.shape, a.dtype) for a in srcs + lands],
        in_specs=[hbm] * (2 * n) + [sem, sem, pl.BlockSpec(memory_space=pl.ANY)],
        out_specs=[hbm] * (2 * n), input_output_aliases={i: i for i in range(2 * n)},
        compiler_params=pltpu.CompilerParams(has_side_effects=pltpu.SideEffectType.DATAFLOW_SIDE_EFFECTING),
    )(*srcs, *lands, send_sems, recv_sems, after)
    return list(outs[n:])


def _with_own(land, own, me):
    return lax.dynamic_update_slice(land, own[None], (me,) + (0,) * own.ndim)


def _ada_mod(c_all, ada_w, ada_b_slice):
    def body(c_ref, w_ref, b_ref, o_ref):
        ca = _silu(c_ref[...])
        for l in range(2):
            o_ref[l] = _nn(ca, w_ref[l], HI) + b_ref[l]

    return _pallas(body, name="ada_mod",
                   out_shape=jax.ShapeDtypeStruct((2, c_all.shape[0], ada_w.shape[2]), F32),
                   compiler_params=_cp(vmem=VMEM_MID))(c_all, ada_w, ada_b_slice)


def _ada_w_grad(c_all, dmod_slice):
    def body(c_ref, d_ref, o_ref):
        ca = _silu(c_ref[...])
        for l in range(2):
            o_ref[l] = _tn(ca, d_ref[l], HI)

    return _pallas(body, name="ada_w_grad",
                   out_shape=jax.ShapeDtypeStruct((2, D_MODEL, dmod_slice.shape[2]), F32),
                   compiler_params=_cp(vmem=VMEM_MID))(c_all, dmod_slice)


def _bucket_onehot():
    qi = jnp.arange(BLOCK)[:, None]
    kj = jnp.arange(2 * BLOCK)[None, :]
    rel = qi - kj + BLOCK
    n = jnp.maximum(rel, 0)
    nf = jnp.maximum(n, 1).astype(F32)
    large = REL_MAX_EXACT + (jnp.log(nf / REL_MAX_EXACT) / math.log(REL_MAX_DIST / REL_MAX_EXACT)
                             * (REL_BUCKETS - REL_MAX_EXACT)).astype(jnp.int32)
    large = jnp.minimum(large, REL_BUCKETS - 1)
    bucket = jnp.where(n < REL_MAX_EXACT, n, large).reshape(1, BLOCK * 2 * BLOCK)
    return (jnp.arange(REL_BUCKETS)[:, None] == bucket).astype(F32)


def _bias_expand(rel_bias_t, onehot):
    def body(r_ref, e_ref, o_ref):
        o_ref[...] = _nn(r_ref[...], e_ref[...], HI)

    return _pallas(body, name="bias_expand",
                   out_shape=jax.ShapeDtypeStruct((N_HEADS, onehot.shape[1]), F32),
                   compiler_params=_cp(vmem=VMEM_MID))(rel_bias_t, onehot)


def _bias_reduce(dbias, onehot):
    def body(d_ref, e_ref, o_ref):
        o_ref[...] = _nt(d_ref[...], e_ref[...], HI)

    return _pallas(body, name="bias_reduce",
                   out_shape=jax.ShapeDtypeStruct((N_HEADS, REL_BUCKETS), F32),
                   compiler_params=_cp(vmem=VMEM_MID))(dbias, onehot)


def _norm_proj(name, x, g, shift, scale, w, seq, out_dtype, wf_t=None):
    t_tok = x.shape[0]
    w3d = w.ndim == 3
    n_out = w.shape[0] * w.shape[2] if w3d else w.shape[1]
    cn = w.shape[2] if w3d else 256

    def body(x_ref, g_ref, sh_ref, sc_ref, w_ref, *rest):
        if wf_t is not None:
            wf_ref, h_ref, o_ref, fl_ref = rest
        else:
            h_ref, o_ref = rest
        xv = x_ref[...]
        rstd = lax.rsqrt(jnp.mean(xv * xv, axis=-1, keepdims=True) + EPS)
        h = (xv * rstd) * g_ref[...] * (1.0 + sc_ref[...]) + sh_ref[...]
        hb = h.astype(BF16)
        h_ref[...] = hb
        for j in range(n_out // cn):
            wj = w_ref[j] if w3d else w_ref[:, j * cn:(j + 1) * cn]
            o_ref[:, j * cn:(j + 1) * cn] = _nn(hb, wj).astype(out_dtype)
        if wf_t is not None:
            fl_ref[...] = _nt(wf_ref[...], hb)

    mod_spec = pl.BlockSpec((None, 1, D_MODEL), lambda i: (i * TM // seq, 0, 0))
    w_spec = (pl.BlockSpec(w.shape, lambda i: (0, 0, 0)) if w3d else pl.BlockSpec(w.shape, lambda i: (0, 0)))
    in_specs = [pl.BlockSpec((TM, D_MODEL), lambda i: (i, 0)), pl.BlockSpec((1, D_MODEL), lambda i: (0, 0)),
                mod_spec, mod_spec, w_spec]
    out_shape = [jax.ShapeDtypeStruct((t_tok, D_MODEL), BF16), jax.ShapeDtypeStruct((t_tok, n_out), out_dtype)]
    out_specs = [pl.BlockSpec((TM, D_MODEL), lambda i: (i, 0)), pl.BlockSpec((TM, n_out), lambda i: (i, 0))]
    args = [x, g, shift, scale, w]
    if wf_t is not None:
        in_specs.append(pl.BlockSpec(wf_t.shape, lambda i: (0, 0)))
        out_shape.append(jax.ShapeDtypeStruct((wf_t.shape[0], t_tok), F32))
        out_specs.append(pl.BlockSpec((wf_t.shape[0], TM), lambda i: (0, i)))
        args.append(wf_t)
    return _pallas(body, name=name, grid=(t_tok // TM,), in_specs=in_specs, out_specs=out_specs,
                   out_shape=out_shape, compiler_params=_cp(("arbitrary",), VMEM_BIG))(*args)


def _fox_prep(fl_t, b_f, seq):
    t_tok = fl_t.shape[1]
    ch = 256

    def body(fl_ref, bf_ref, fr_ref, fc_ref):
        z = fl_ref[...] + bf_ref[...]
        logf = jnp.minimum(z, 0.0) - jnp.log(1.0 + jnp.exp(-jnp.abs(z)))
        ri = lax.broadcasted_iota(jnp.int32, (ch, ch), 0)
        ci = lax.broadcasted_iota(jnp.int32, (ch, ch), 1)
        upper = (ri <= ci).astype(F32)
        eye = (ri == ci).astype(F32)
        carry = jnp.zeros((N_HEADS, 1), F32)
        for k in range(seq // ch):
            fk = _nn(logf[:, k * ch:(k + 1) * ch], upper, HI) + carry
            carry = fk[:, ch - 1:ch]
            fr_ref[:, k * ch:(k + 1) * ch] = fk
            padded = jnp.concatenate([fk, jnp.zeros((128 - N_HEADS, ch), F32)], axis=0)
            fc_ref[k * ch:(k + 1) * ch, :] = _nt(eye, padded, HI)

    return _pallas(
        body, name="fox_prep", grid=(t_tok // seq,),
        in_specs=[pl.BlockSpec((N_HEADS, seq), lambda b: (0, b)), pl.BlockSpec((N_HEADS, 1), lambda b: (0, 0))],
        out_specs=[pl.BlockSpec((N_HEADS, seq), lambda b: (0, b)), pl.BlockSpec((seq, 128), lambda b: (b, 0))],
        out_shape=[jax.ShapeDtypeStruct((N_HEADS, t_tok), F32), jax.ShapeDtypeStruct((t_tok, 128), F32)],
        compiler_params=_cp(("arbitrary",), VMEM_MID))(fl_t, b_f)


def _fox_post(df_row, fl_t, b_f, seq):
    t_tok = fl_t.shape[1]
    ch = 256

    def body(d_ref, fl_ref, bf_ref, o_ref, db_ref):
        @pl.when(pl.program_id(0) == 0)
        def _():
            db_ref[...] = jnp.zeros_like(db_ref)

        z = fl_ref[...] + bf_ref[...]
        sig_neg = 1.0 / (1.0 + jnp.exp(z))
        ri = lax.broadcasted_iota(jnp.int32, (ch, ch), 0)
        ci = lax.broadcasted_iota(jnp.int32, (ch, ch), 1)
        lower = (ri >= ci).astype(F32)
        carry = jnp.zeros((N_HEADS, 1), F32)
        tot = jnp.zeros((N_HEADS, 1), F32)
        for k in reversed(range(seq // ch)):
            dk = _nn(d_ref[:, k * ch:(k + 1) * ch], lower, HI) + carry
            carry = dk[:, 0:1]
            dfl = dk * sig_neg[:, k * ch:(k + 1) * ch]
            o_ref[:, k * ch:(k + 1) * ch] = dfl
            tot = tot + jnp.sum(dfl, axis=1, keepdims=True)
        db_ref[...] += jnp.broadcast_to(tot, db_ref.shape)

    return _pallas(
        body, name="fox_post", grid=(t_tok // seq,),
        in_specs=[pl.BlockSpec((N_HEADS, seq), lambda b: (0, b)), pl.BlockSpec((N_HEADS, seq), lambda b: (0, b)),
                  pl.BlockSpec((N_HEADS, 1), lambda b: (0, 0))],
        out_specs=[pl.BlockSpec((N_HEADS, seq), lambda b: (0, b)), pl.BlockSpec((N_HEADS, 128), lambda b: (0, 0))],
        out_shape=[jax.ShapeDtypeStruct((N_HEADS, t_tok), F32), jax.ShapeDtypeStruct((N_HEADS, 128), F32)],
        compiler_params=_cp(("arbitrary",), VMEM_MID))(df_row, fl_t, b_f)


def _eye(n, dtype):
    return (lax.broadcasted_iota(jnp.int32, (n, n), 0) == lax.broadcasted_iota(jnp.int32, (n, n), 1)).astype(dtype)


def _fox_aug(qkvg, f_col, seq):
    t_tok = qkvg.shape[0]
    ta = 256
    nkb = ta // TK

    def body(q_ref, k_ref, v_ref, fc_ref, qa_ref, ka_ref, kt_ref, vt_ref):
        ri = lax.broadcasted_iota(jnp.int32, (128, 128), 0)
        ci = lax.broadcasted_iota(jnp.int32, (128, 128), 1)
        eye = (ri == ci).astype(BF16)
        lane = lax.broadcasted_iota(jnp.int32, (ta, 128), 1)
        ones_q = jnp.where(jnp.logical_and(lane >= 64, lane < 67), 1.0, 0.0)
        ones_k = jnp.where(jnp.logical_and(lane >= 67, lane < 70), 1.0, 0.0)
        fc_tile = fc_ref[...]
        for p in range(N_HEADS // 2):
            q2 = q_ref[:, 128 * p:128 * (p + 1)]
            k2 = k_ref[:, 128 * p:128 * (p + 1)]
            vt = _nt(eye, v_ref[:, 128 * p:128 * (p + 1)]).astype(BF16)
            for kk in range(nkb):
                vt_ref[p, kk] = vt[:, kk * TK:(kk + 1) * TK]
            for e in range(2):
                h = 2 * p + e
                sel = jnp.logical_and(ri == ci + HEAD_DIM * e, ci < HEAD_DIM)
                f = _col(fc_tile, h)
                fh = f.astype(BF16).astype(F32)
                fm = (f - fh).astype(BF16).astype(F32)
                fl = (f - fh - fm).astype(BF16).astype(F32)
                qa = (_nn(q2, jnp.where(sel, SCALE, 0.0).astype(BF16)) + ones_q + jnp.where(lane == 67, fh, 0.0)
                      + jnp.where(lane == 68, fm, 0.0) + jnp.where(lane == 69, fl, 0.0))
                ka = (_nn(k2, jnp.where(sel, 1.0, 0.0).astype(BF16)) + ones_k - jnp.where(lane == 64, fh, 0.0)
                      - jnp.where(lane == 65, fm, 0.0) - jnp.where(lane == 66, fl, 0.0))
                qa_ref[h] = qa.astype(BF16)
                kab = ka.astype(BF16)
                ka_ref[h] = kab
                kt = _nt(eye, kab).astype(BF16)
                for kk in range(ta // TKB):
                    kt_ref[h, kk] = kt[:, kk * TKB:(kk + 1) * TKB]

    aug = jax.ShapeDtypeStruct((N_HEADS, t_tok, 128), BF16)
    return _pallas(
        body, name="fox_aug", grid=(t_tok // ta,),
        in_specs=[pl.BlockSpec((ta, 512), lambda i: (i, C_BQ // 512)), pl.BlockSpec((ta, 512), lambda i: (i, C_BK // 512)),
                  pl.BlockSpec((ta, 512), lambda i: (i, C_BV // 512)), pl.BlockSpec((ta, 128), lambda i: (i, 0))],
        out_specs=[pl.BlockSpec((N_HEADS, ta, 128), lambda i: (0, i, 0)), pl.BlockSpec((N_HEADS, ta, 128), lambda i: (0, i, 0)),
                   pl.BlockSpec((N_HEADS, ta // TKB, 128, TKB), lambda i: (0, i, 0, 0)),
                   pl.BlockSpec((N_HEADS // 2, nkb, 128, TK), lambda i: (0, i, 0, 0))],
        out_shape=[aug, aug, jax.ShapeDtypeStruct((N_HEADS, t_tok // TKB, 128, TKB), BF16),
                   jax.ShapeDtypeStruct((N_HEADS // 2, t_tok // TK, 128, TK), BF16)],
        compiler_params=_cp(("arbitrary",), VMEM_MID))(qkvg, qkvg, qkvg, f_col)


def _fox_fwd_t(q_aug, k_aug, vt, seq):
    t_tok = k_aug.shape[1]
    nq = seq // TQ
    ratio = TQ // TK
    assert ratio == 2, "the two pipeline slots are addressed by the key block's parity"

    def body(qa_ref, ka_ref, vt_ref, o_ref, lse_ref, ml_s, acc_s, st_s, p_s, al_s, qt_s):
        i = pl.program_id(1)
        tpos = i * TQ + lax.broadcasted_iota(jnp.int32, (1, TQ), 1)
        eye = _eye(HEAD_DIM, BF16)
        eye2 = _eye(128, BF16)
        for h in range(N_HEADS):
            qt_s[h] = _nt(eye2, qa_ref[h]).astype(BF16)
            ml_s[0, h] = jnp.full((1, TQ), NEG, F32)
            ml_s[1, h] = jnp.zeros((1, TQ), F32)
            acc_s[h] = jnp.zeros((HEAD_DIM, TQ), F32)
            p_s[1, h] = jnp.zeros((TK, TQ), BF16)
            al_s[1, h] = jnp.ones((1, TQ), F32)

        def scores(j, slot):
            row0 = pl.multiple_of(j * TK, TK)
            for h in range(N_HEADS):
                st_s[slot, h] = _nn(ka_ref[h, pl.ds(row0, TK), :], qt_s[h])

        def softmax(j, slot, masked):
            if masked:
                keep = (j * TK + lax.broadcasted_iota(jnp.int32, (TK, 1), 0)) <= tpos
            for h in range(N_HEADS):
                st = st_s[slot, h]
                if masked:
                    st = jnp.where(keep, st, NEG)
                m = ml_s[0, h]
                m_new = jnp.maximum(m, jnp.max(st, axis=0, keepdims=True))
                alpha = jnp.exp(m - m_new)
                pe = jnp.exp(st - m_new)
                ml_s[0, h] = m_new
                ml_s[1, h] = alpha * ml_s[1, h] + jnp.sum(pe, axis=0, keepdims=True)
                al_s[slot, h] = alpha
                p_s[slot, h] = pe.astype(BF16)

        def values(j, slot):
            jv = jnp.maximum(j, 0)
            for h in range(N_HEADS):
                p, e = divmod(h, 2)
                acc_s[h] = al_s[slot, h] * acc_s[h] + _nn(vt_ref[p, jv, e * HEAD_DIM:(e + 1) * HEAD_DIM, :], p_s[slot, h])

        def step(m, carry):
            for kk in range(ratio):
                j = ratio * m + kk
                values(j - 1, 1 - kk)
                softmax(j, kk, False)
                scores(j + 1, 1 - kk)
            return carry

        scores(0, 0)
        lax.fori_loop(0, i, step, 0)
        for kk in range(ratio):
            j = ratio * i + kk
            values(j - 1, 1 - kk)
            softmax(j, kk, True)
            if kk < ratio - 1:
                scores(j + 1, 1 - kk)
        values(ratio * i + ratio - 1, ratio - 1)
        for p in range(N_HEADS // 2):
            outs = []
            for e in range(2):
                h = 2 * p + e
                l = ml_s[1, h]
                outs.append(_tn((acc_s[h] / l).astype(BF16), eye))
                lse_ref[p, e:e + 1, :] = ml_s[0, h] + jnp.log(l)
            o_ref[:, 128 * p:128 * (p + 1)] = jnp.concatenate(outs, axis=1).astype(BF16)

    return _pallas(
        body, name="fox_fwd", grid=(t_tok // seq, nq),
        in_specs=[pl.BlockSpec((N_HEADS, TQ, 128), lambda b, i: (0, b * nq + i, 0)),
                  pl.BlockSpec((N_HEADS, seq, 128), lambda b, i: (0, b, 0)),
                  pl.BlockSpec((N_HEADS // 2, seq // TK, 128, TK), lambda b, i: (0, b, 0, 0))],
        out_specs=[pl.BlockSpec((TQ, 512), lambda b, i: (b * nq + i, 0)),
                   pl.BlockSpec((N_HEADS // 2, 2, TQ), lambda b, i: (0, 0, b * nq + i))],
        out_shape=[jax.ShapeDtypeStruct((t_tok, 512), BF16), jax.ShapeDtypeStruct((N_HEADS // 2, 2, t_tok), F32)],
        scratch_shapes=[pltpu.VMEM((2, N_HEADS, 1, TQ), F32), pltpu.VMEM((N_HEADS, HEAD_DIM, TQ), F32),
                        pltpu.VMEM((2, N_HEADS, TK, TQ), F32), pltpu.VMEM((2, N_HEADS, TK, TQ), BF16),
                        pltpu.VMEM((2, N_HEADS, 1, TQ), F32), pltpu.VMEM((N_HEADS, 128, TQ), BF16)],
        compiler_params=_cp(("arbitrary", "arbitrary"), VMEM_MID))(q_aug, k_aug, vt)


def _fox_bwd_t(q_aug, k_aug, kt, qkvg, du_b, b_out, lse, seq):
    TK = TKB
    t_tok = qkvg.shape[0]
    nq = seq // TQ
    nkb = seq // TK
    ratio = TQ // TK
    hg = 4

    def body(qa_ref, ka_ref, kt_ref, v_ref, do_ref, o_ref, lse_ref, dq_ref, dk_ref, dv_ref, df_ref,
             dqt_s, row_s, dfk_s, dk_s, dv_s, dot_s, st_s, dp_s, pb_s, db_s, qt_s):
        eye = _eye(HEAD_DIM, BF16)
        eye2 = _eye(128, BF16)
        eye_k = _eye(TK, F32)
        lane8 = lax.broadcasted_iota(jnp.int32, (8, 128), 1)
        lane_k = lax.broadcasted_iota(jnp.int32, (TK, 128), 1)
        first = [lane8 < HEAD_DIM, lane8 >= HEAD_DIM]
        for pp in range(hg // 2):
            for ii in range(nq):
                dot_s[pp, ii] = _nt(eye2, do_ref[ii * TQ:(ii + 1) * TQ, 128 * pp:128 * (pp + 1)]).astype(BF16)
        for hh in range(hg):
            for ii in range(nq):
                qt_s[hh, ii] = _nt(eye2, qa_ref[hh, ii * TQ:(ii + 1) * TQ, :]).astype(BF16)
        for hh in range(hg):
            pp, e = divmod(hh, 2)
            head_lanes = jnp.where(first[e], 1.0, 0.0)
            for ii in range(nq):
                rows = slice(ii * TQ, (ii + 1) * TQ)
                prod = do_ref[rows, 128 * pp:128 * (pp + 1)].astype(F32) * o_ref[rows, 128 * pp:128 * (pp + 1)].astype(F32)
                row_s[hh, ii, 0] = _nt(head_lanes, prod, HI)
                row_s[hh, ii, 1] = jnp.broadcast_to(lse_ref[pp, e:e + 1, ii * TQ:(ii + 1) * TQ], (8, TQ))
                dqt_s[hh, ii] = jnp.zeros((128, TQ), F32)

        def kblock(j, _):
            krow = pl.multiple_of(j * TK, TK)
            spos = j * TK + lax.broadcasted_iota(jnp.int32, (TK, 1), 0)
            for hh in range(hg):
                dk_s[hh] = jnp.zeros((TK, 128), F32)
                dv_s[hh] = jnp.zeros((TK, 128), F32)

            def scores(i, slot):
                for hh in range(hg):
                    pp, e = divmod(hh, 2)
                    own = (lane_k < HEAD_DIM) if e == 0 else (lane_k >= HEAD_DIM)
                    v2 = v_ref[pl.ds(krow, TK), 128 * pp:128 * (pp + 1)]
                    vj = jnp.where(own, v2, jnp.zeros_like(v2))
                    st_s[slot, hh] = _nn(ka_ref[hh, pl.ds(krow, TK), :], qt_s[hh, i])
                    dp_s[slot, hh] = _nn(vj, dot_s[pp, i])

            def elementwise(i, slot, masked):
                if masked:
                    keep = spos <= (i * TQ + lax.broadcasted_iota(jnp.int32, (1, TQ), 1))
                for hh in range(hg):
                    pt = jnp.exp(st_s[slot, hh] - row_s[hh, i, 1][0:1, :])
                    if masked:
                        pt = jnp.where(keep, pt, 0.0)
                    dst = pt * (dp_s[slot, hh] - row_s[hh, i, 0][0:1, :])
                    pb_s[slot, hh] = pt.astype(BF16)
                    db_s[slot, hh] = dst.astype(BF16)

            def grads(i, slot):
                qrow = pl.multiple_of(i * TQ, TQ)
                for hh in range(hg):
                    dst_b = db_s[slot, hh]
                    dv_s[hh] += _nn(pb_s[slot, hh], do_ref[pl.ds(qrow, TQ), 128 * (hh // 2):128 * (hh // 2 + 1)])
                    dk_s[hh] += _nn(dst_b, qa_ref[hh, pl.ds(qrow, TQ), :])
                    dqt_s[hh, i] += _nn(kt_ref[hh, j], dst_b)

            def step(p, carry):
                i = i0 + 2 * p + 1
                grads(i - 1, 0)
                elementwise(i, 1, False)
                scores(i + 1, 0)
                grads(i, 1)
                elementwise(i + 1, 0, False)
                scores(jnp.minimum(i + 2, nq - 1), 1)
                return carry

            i0 = j // ratio
            rest = nq - 1 - i0
            scores(i0, 0)
            elementwise(i0, 0, True)
            scores(jnp.minimum(i0 + 1, nq - 1), 1)
            lax.fori_loop(0, rest // 2, step, 0)

            @pl.when(rest % 2 == 1)
            def _():
                grads(nq - 2, 0)
                elementwise(nq - 1, 1, False)
                grads(nq - 1, 1)

            @pl.when(rest % 2 == 0)
            def _():
                grads(nq - 1, 0)
            for pp in range(hg // 2):
                cols = slice(128 * pp, 128 * (pp + 1))
                dk_ref[pl.ds(krow, TK), cols] = jnp.concatenate(
                    [dk_s[2 * pp][:, :HEAD_DIM], dk_s[2 * pp + 1][:, :HEAD_DIM]], axis=1).astype(BF16)
                dv_ref[pl.ds(krow, TK), cols] = jnp.where(lane_k < HEAD_DIM, dv_s[2 * pp], dv_s[2 * pp + 1]).astype(BF16)
            for hh in range(hg):
                dfk_s[hh, j] = _tn(dk_s[hh][:, HEAD_DIM:HEAD_DIM + 8], eye_k, HI)
            return 0

        lax.fori_loop(0, nkb, kblock, 0)
        for pp in range(hg // 2):
            for ii in range(nq):
                parts = []
                for e in range(2):
                    dqt = dqt_s[2 * pp + e, ii]
                    parts.append(_tn(dqt[0:HEAD_DIM, :].astype(BF16), eye) * SCALE)
                    for kk in range(ratio):
                        jj = ii * ratio + kk
                        df_ref[pp, e:e + 1, jj * TK:(jj + 1) * TK] = (dqt[67:68, kk * TK:(kk + 1) * TK]
                                                                     - dfk_s[2 * pp + e, jj][0:1, :])
                dq_ref[ii * TQ:(ii + 1) * TQ, 128 * pp:128 * (pp + 1)] = jnp.concatenate(parts, axis=1).astype(BF16)

    aug_blk = pl.BlockSpec((hg, seq, 128), lambda b, g: (g, b, 0))
    pair_blk = pl.BlockSpec((seq, 64 * hg), lambda b, g: (b, g))
    row_blk = pl.BlockSpec((hg // 2, 2, seq), lambda b, g: (g, 0, b))
    return _pallas(
        body, name="fox_bwd", grid=(t_tok // seq, N_HEADS // hg),
        in_specs=[aug_blk, aug_blk, pl.BlockSpec((hg, nkb, 128, TK), lambda b, g: (g, b, 0, 0)),
                  pl.BlockSpec((seq, 64 * hg), lambda b, g: (b, C_BV // (64 * hg) + g)), pair_blk, pair_blk, row_blk],
        out_specs=[pair_blk, pair_blk, pair_blk, row_blk],
        out_shape=[jax.ShapeDtypeStruct((t_tok, 512), BF16)] * 3
        + [jax.ShapeDtypeStruct((N_HEADS // 2, 2, t_tok), F32)],
        scratch_shapes=[pltpu.VMEM((hg, nq, 128, TQ), F32), pltpu.VMEM((hg, nq, 2, 8, TQ), F32),
                        pltpu.VMEM((hg, nkb, 8, TK), F32), pltpu.VMEM((hg, TK, 128), F32),
                        pltpu.VMEM((hg, TK, 128), F32), pltpu.VMEM((hg // 2, nq, 128, TQ), BF16),
                        pltpu.VMEM((2, hg, TK, TQ), F32), pltpu.VMEM((2, hg, TK, TQ), F32),
                        pltpu.VMEM((2, hg, TK, TQ), BF16), pltpu.VMEM((2, hg, TK, TQ), BF16),
                        pltpu.VMEM((hg, nq, 128, TQ), BF16)],
        compiler_params=_cp(("arbitrary", "arbitrary"), VMEM_BIG))(q_aug, k_aug, kt, qkvg, du_b, b_out, lse)


def _fox_bwd_t_old(q_aug, k_aug, kt, qkvg, du_b, b_out, lse, seq):
    t_tok = qkvg.shape[0]
    nq = seq // TQ
    nkb = seq // TK
    ratio = TQ // TK

    def body(qa_ref, ka_ref, kt_ref, v_ref, do_ref, o_ref, lse_ref, dq_ref, dk_ref, dv_ref, df_ref,
             dqt_s, out_s, row_s, dfk_s):
        ones_b = jnp.ones((8, TQ), BF16)
        ones_f = jnp.ones((8, HEAD_DIM), F32)
        eye = _eye(HEAD_DIM, BF16)
        for e in range(2):
            lo, hi = e * HEAD_DIM, (e + 1) * HEAD_DIM
            for ii in range(nq):
                rows = slice(ii * TQ, (ii + 1) * TQ)
                do = do_ref[rows, :][:, lo:hi].astype(F32)
                ov = o_ref[rows, :][:, lo:hi].astype(F32)
                row_s[ii, 0] = _nt(ones_f, do * ov, HI)
                row_s[ii, 1] = jnp.broadcast_to(lse_ref[e:e + 1, ii * TQ:(ii + 1) * TQ], (8, TQ))
                dqt_s[ii] = jnp.zeros((128, TQ), F32)

            def kblock(j, _):
                krow = pl.multiple_of(j * TK, TK)
                kj = ka_ref[e, pl.ds(krow, TK), :]
                ktj = kt_ref[e, j]
                vj = v_ref[pl.ds(krow, TK), :][:, lo:hi]
                spos = j * TK + lax.broadcasted_iota(jnp.int32, (TK, 1), 0)

                def qblock(i, carry, masked):
                    dk_acc, dv_acc, dfk = carry
                    qrow = pl.multiple_of(i * TQ, TQ)
                    qa = qa_ref[e, pl.ds(qrow, TQ), :]
                    doh = do_ref[pl.ds(qrow, TQ), :][:, lo:hi]
                    pt = jnp.exp(_nt(kj, qa) - row_s[i, 1][0:1, :])
                    if masked:
                        tpos = i * TQ + lax.broadcasted_iota(jnp.int32, (1, TQ), 1)
                        pt = jnp.where(spos <= tpos, pt, 0.0)
                    dst = pt * (_nt(vj, doh) - row_s[i, 0][0:1, :])
                    dst_b = dst.astype(BF16)
                    dv_acc = dv_acc + _nn(pt.astype(BF16), doh)
                    dk_acc = dk_acc + _nn(dst_b, qa)
                    dqt_s[i] += _nn(ktj, dst_b)
                    dfk = dfk + _nt(ones_b, dst_b)
                    return dk_acc, dv_acc, dfk

                i0 = j // ratio
                carry = (jnp.zeros((TK, 128), F32), jnp.zeros((TK, HEAD_DIM), F32), jnp.zeros((8, TK), F32))
                carry = qblock(i0, carry, True)
                dk_acc, dv_acc, dfk = lax.fori_loop(i0 + 1, nq, functools.partial(qblock, masked=False), carry)
                out_s[1, e, pl.ds(krow, TK), :] = dk_acc[:, :HEAD_DIM]
                out_s[2, e, pl.ds(krow, TK), :] = dv_acc
                dfk_s[j] = dfk
                return 0

            lax.fori_loop(0, nkb, kblock, 0)
            for ii in range(nq):
                dqt = dqt_s[ii]
                out_s[0, e, ii * TQ:(ii + 1) * TQ, :] = _tn(dqt[0:HEAD_DIM, :].astype(BF16), eye) * SCALE
                for kk in range(ratio):
                    jj = ii * ratio + kk
                    df_ref[e:e + 1, jj * TK:(jj + 1) * TK] = dqt[67:68, kk * TK:(kk + 1) * TK] - dfk_s[jj][0:1, :]
        for k, ref in enumerate((dq_ref, dk_ref, dv_ref)):
            ref[...] = jnp.concatenate([out_s[k, 0], out_s[k, 1]], axis=1).astype(BF16)

    aug_blk = pl.BlockSpec((2, seq, 128), lambda b, p: (p, b, 0))
    pair_blk = pl.BlockSpec((seq, 128), lambda b, p: (b, p))
    row_blk = pl.BlockSpec((None, 2, seq), lambda b, p: (p, 0, b))
    return _pallas(
        body, name="fox_bwd", grid=(t_tok // seq, N_HEADS // 2),
        in_specs=[aug_blk, aug_blk, pl.BlockSpec((2, nkb, 128, TK), lambda b, p: (p, b, 0, 0)),
                  pl.BlockSpec((seq, 128), lambda b, p: (b, C_BV // 128 + p)), pair_blk, pair_blk, row_blk],
        out_specs=[pair_blk, pair_blk, pair_blk, row_blk],
        out_shape=[jax.ShapeDtypeStruct((t_tok, 512), BF16)] * 3
        + [jax.ShapeDtypeStruct((N_HEADS // 2, 2, t_tok), F32)],
        scratch_shapes=[pltpu.VMEM((nq, 128, TQ), F32), pltpu.VMEM((3, 2, seq, HEAD_DIM), F32),
                        pltpu.VMEM((nq, 2, 8, TQ), F32), pltpu.VMEM((nkb, 8, TK), F32)],
        compiler_params=_cp(("arbitrary", "arbitrary"), VMEM_BIG))(q_aug, k_aug, kt, qkvg, du_b, b_out, lse)


def _fox_fwd(qkvg, f_row, f_col, seq):
    t_tok = qkvg.shape[0]
    nq = seq // TQ

    def body(q_ref, k_ref, v_ref, fr_ref, fc_ref, o_ref, lse_ref, fk_s):
        i = pl.program_id(1)
        for jj in range(nq):
            fk_s[jj] = fr_ref[:, jj * TQ:(jj + 1) * TQ]
        fcol = fc_ref[...]
        tpos = i * TQ + lax.broadcasted_iota(jnp.int32, (TQ, 1), 0)
        lane = lax.broadcasted_iota(jnp.int32, (TQ, 128), 1)
        lse_tile = jnp.zeros((TQ, 128), F32)
        for p in range(N_HEADS // 2):
            q2 = q_ref[:, 128 * p:128 * (p + 1)]
            qs = [q2[:, :HEAD_DIM], q2[:, HEAD_DIM:]]
            fqs = [_col(fcol, 2 * p + e) for e in range(2)]

            def kblock(j, carry):
                row0 = pl.multiple_of(j * TQ, TQ)
                k2 = k_ref[pl.ds(row0, TQ), 128 * p:128 * (p + 1)]
                v2 = v_ref[pl.ds(row0, TQ), 128 * p:128 * (p + 1)]
                fk8 = fk_s[j]
                spos = j * TQ + lax.broadcasted_iota(jnp.int32, (1, TQ), 1)
                keep = spos <= tpos
                new = []
                for e in range(2):
                    m, l, acc = carry[3 * e:3 * e + 3]
                    kh = k2[:, e * HEAD_DIM:(e + 1) * HEAD_DIM]
                    vh = v2[:, e * HEAD_DIM:(e + 1) * HEAD_DIM]
                    s = _nt(qs[e], kh) * SCALE + (fqs[e] - fk8[2 * p + e:2 * p + e + 1, :])
                    s = jnp.where(keep, s, NEG)
                    m_new = jnp.maximum(m, jnp.max(s, axis=1, keepdims=True))
                    alpha = jnp.exp(m - m_new)
                    pe = jnp.exp(s - m_new)
                    l = alpha * l + jnp.sum(pe, axis=1, keepdims=True)
                    acc = alpha * acc + _nn(pe.astype(BF16), vh)
                    new += [m_new, l, acc]
                return tuple(new)

            init = (jnp.full((TQ, 1), NEG, F32), jnp.zeros((TQ, 1), F32), jnp.zeros((TQ, HEAD_DIM), F32)) * 2
            res = lax.fori_loop(0, i + 1, kblock, init)
            outs = []
            for e in range(2):
                m, l, acc = res[3 * e:3 * e + 3]
                outs.append(acc / l)
                lse_tile = jnp.where(lane == 2 * p + e, m + jnp.log(l), lse_tile)
            o_ref[:, 128 * p:128 * (p + 1)] = jnp.concatenate(outs, axis=1).astype(BF16)
        lse_ref[...] = lse_tile

    return _pallas(
        body, name="fox_fwd", grid=(t_tok // seq, nq),
        in_specs=[pl.BlockSpec((TQ, 512), lambda b, i: (b * nq + i, C_BQ // 512)),
                  pl.BlockSpec((seq, 512), lambda b, i: (b, C_BK // 512)),
                  pl.BlockSpec((seq, 512), lambda b, i: (b, C_BV // 512)),
                  pl.BlockSpec((N_HEADS, seq), lambda b, i: (0, b)),
                  pl.BlockSpec((TQ, 128), lambda b, i: (b * nq + i, 0))],
        out_specs=[pl.BlockSpec((TQ, 512), lambda b, i: (b * nq + i, 0)),
                   pl.BlockSpec((TQ, 128), lambda b, i: (b * nq + i, 0))],
        out_shape=[jax.ShapeDtypeStruct((t_tok, 512), BF16), jax.ShapeDtypeStruct((t_tok, 128), F32)],
        scratch_shapes=[pltpu.VMEM((nq, N_HEADS, TQ), F32)],
        compiler_params=_cp(("arbitrary", "arbitrary"), VMEM_MID))(qkvg, qkvg, qkvg, f_row, f_col)


def _fox_bwd(qkvg, du_b, b_out, lse, f_row, f_col, seq):
    t_tok = qkvg.shape[0]
    nq = seq // TQ

    def body(q_ref, k_ref, v_ref, do_ref, o_ref, lse_ref, fr_ref, fc_ref,
             dq_ref, dk_ref, dv_ref, df_ref, dq_s, dk_s, dv_s, col_s, df_s, fk_s):
        p = pl.program_id(1)
        for jj in range(nq):
            fk_s[jj] = fr_ref[:, jj * TQ:(jj + 1) * TQ]
        eye = (lax.broadcasted_iota(jnp.int32, (TQ, TQ), 0) == lax.broadcasted_iota(jnp.int32, (TQ, TQ), 1)).astype(F32)
        for e in range(2):
            h = 2 * p + e
            lo, hi = e * HEAD_DIM, (e + 1) * HEAD_DIM
            for ii in range(nq):
                rows = slice(ii * TQ, (ii + 1) * TQ)
                do = do_ref[rows, :][:, lo:hi].astype(F32)
                ov = o_ref[rows, :][:, lo:hi].astype(F32)
                col_s[0, rows, :] = jnp.sum(do * ov, axis=1, keepdims=True)
                col_s[1, rows, :] = _col(lse_ref[rows, :], h)
                col_s[2, rows, :] = _col(fc_ref[rows, :], h)
                dq_s[rows, :] = jnp.zeros((TQ, HEAD_DIM), F32)
                df_s[ii] = jnp.zeros((8, TQ), F32)
                col_s[3, rows, :] = jnp.zeros((TQ, 1), F32)

            def kblock(j, _):
                krow = pl.multiple_of(j * TQ, TQ)
                kh = k_ref[pl.ds(krow, TQ), :][:, lo:hi]
                vh = v_ref[pl.ds(krow, TQ), :][:, lo:hi]
                fk = _row(fk_s[j], h)
                spos = j * TQ + lax.broadcasted_iota(jnp.int32, (1, TQ), 1)

                def qblock(i, carry):
                    dk_acc, dv_acc, dfk = carry
                    qrow = pl.multiple_of(i * TQ, TQ)
                    qh = q_ref[pl.ds(qrow, TQ), :][:, lo:hi]
                    doh = do_ref[pl.ds(qrow, TQ), :][:, lo:hi]
                    delta = col_s[0, pl.ds(qrow, TQ), :]
                    lse_q = col_s[1, pl.ds(qrow, TQ), :]
                    fq = col_s[2, pl.ds(qrow, TQ), :]
                    tpos = i * TQ + lax.broadcasted_iota(jnp.int32, (TQ, 1), 0)
                    s = _nt(qh, kh) * SCALE + (fq - fk)
                    pr = jnp.where(spos <= tpos, jnp.exp(s - lse_q), 0.0)
                    dp = _nt(doh, vh)
                    ds = pr * (dp - delta)
                    ds_b = ds.astype(BF16)
                    dv_acc = dv_acc + _tn(pr.astype(BF16), doh)
                    dk_acc = dk_acc + _tn(ds_b, qh)
                    dq_s[pl.ds(qrow, TQ), :] += _nn(ds_b, kh)
                    col_s[3, pl.ds(qrow, TQ), :] += jnp.sum(ds, axis=1, keepdims=True)
                    dfk = dfk + jnp.sum(ds, axis=0, keepdims=True)
                    return dk_acc, dv_acc, dfk

                zero = jnp.zeros((TQ, HEAD_DIM), F32)
                dk_acc, dv_acc, dfk = lax.fori_loop(j, nq, qblock, (zero, zero, jnp.zeros((1, TQ), F32)))
                dk_s[e, pl.ds(krow, TQ), :] = dk_acc * SCALE
                dv_s[e, pl.ds(krow, TQ), :] = dv_acc
                df_s[j] -= jnp.broadcast_to(dfk, (8, TQ))
                return 0

            lax.fori_loop(0, nq, kblock, 0)
            dq_s2 = dq_s[...] * SCALE
            dk_s[2 + e] = dq_s2
            for ii in range(nq):
                dfq = jnp.broadcast_to(col_s[3, ii * TQ:(ii + 1) * TQ, :], (TQ, 128))
                df_ref[e:e + 1, ii * TQ:(ii + 1) * TQ] = _tn(dfq, eye, HI)[0:1, :] + df_s[ii][0:1, :]
        dq_ref[...] = jnp.concatenate([dk_s[2], dk_s[3]], axis=1).astype(BF16)
        dk_ref[...] = jnp.concatenate([dk_s[0], dk_s[1]], axis=1).astype(BF16)
        dv_ref[...] = jnp.concatenate([dv_s[0], dv_s[1]], axis=1).astype(BF16)

    blk = lambda off: pl.BlockSpec((seq, 128), lambda b, p: (b, off // 128 + p))
    out_blk = pl.BlockSpec((seq, 128), lambda b, p: (b, p))
    return _pallas(
        body, name="fox_bwd", grid=(t_tok // seq, N_HEADS // 2),
        in_specs=[blk(C_BQ), blk(C_BK), blk(C_BV), out_blk, out_blk,
                  pl.BlockSpec((seq, 128), lambda b, p: (b, 0)),
                  pl.BlockSpec((N_HEADS, seq), lambda b, p: (0, b)),
                  pl.BlockSpec((seq, 128), lambda b, p: (b, 0))],
        out_specs=[out_blk, out_blk, out_blk, pl.BlockSpec((None, 2, seq), lambda b, p: (p, 0, b))],
        out_shape=[jax.ShapeDtypeStruct((t_tok, 512), BF16)] * 3
        + [jax.ShapeDtypeStruct((N_HEADS // 2, 2, t_tok), F32)],
        scratch_shapes=[pltpu.VMEM((seq, HEAD_DIM), F32), pltpu.VMEM((4, seq, HEAD_DIM), F32),
                        pltpu.VMEM((2, seq, HEAD_DIM), F32), pltpu.VMEM((4, seq, 1), F32),
                        pltpu.VMEM((nq, 8, TQ), F32), pltpu.VMEM((nq, N_HEADS, TQ), F32)],
        compiler_params=_cp(("arbitrary", "arbitrary"), VMEM_BIG))(qkvg, qkvg, qkvg, du_b, b_out, lse, f_row, f_col)


def _swa_window(k_ref, v_ref, n):
    prev = pl.multiple_of(jnp.maximum(n - 1, 0) * BLOCK, BLOCK)
    cur = pl.multiple_of(n * BLOCK, BLOCK)
    kwin = jnp.concatenate([k_ref[pl.ds(prev, BLOCK), :], k_ref[pl.ds(cur, BLOCK), :]], axis=0)
    vwin = jnp.concatenate([v_ref[pl.ds(prev, BLOCK), :], v_ref[pl.ds(cur, BLOCK), :]], axis=0)
    ti = lax.broadcasted_iota(jnp.int32, (BLOCK, 2 * BLOCK), 0)
    sj = lax.broadcasted_iota(jnp.int32, (BLOCK, 2 * BLOCK), 1)
    rel = ti - sj + BLOCK
    first_key = jnp.where(n > 0, 0, BLOCK)
    mask = jnp.logical_and(jnp.logical_and(rel >= 0, rel < BLOCK), sj >= first_key)
    return kwin, vwin, mask, prev, cur


def _head_cols(ref, h):
    pair = ref[:, 128 * (h // 2):128 * (h // 2 + 1)]
    return pair[:, (h % 2) * HEAD_DIM:(h % 2 + 1) * HEAD_DIM]


def _swa_logits(q_ref, kwin, bias_ref, h, mask):
    hk = h // KV_GROUP
    s = _nt(_head_cols(q_ref, h), kwin[:, hk * HEAD_DIM:(hk + 1) * HEAD_DIM]) * SCALE + bias_ref[h]
    return jnp.where(mask, s, NEG)


def _swa_fwd(qkvg, bias, sinks, seq):
    t_tok = qkvg.shape[0]
    nb = seq // BLOCK

    def body(sink_ref, q_ref, k_ref, v_ref, bias_ref, o_ref, lse_ref, s_s, p_s, den_s):
        g = pl.program_id(1)
        subs = [pl.ds(s * BLOCK, BLOCK) for s in range(SWA_SUB)]
        wins = [_swa_window(k_ref, v_ref, SWA_SUB * g + s) for s in range(SWA_SUB)]
        for s in range(SWA_SUB):
            for h in range(N_HEADS):
                s_s[s * N_HEADS + h] = _swa_logits(q_ref.at[subs[s]], wins[s][0], bias_ref, h, wins[s][2])
        lane = lax.broadcasted_iota(jnp.int32, (BLOCK, 128), 1)
        for s in range(SWA_SUB):
            lse_tile = jnp.zeros((BLOCK, 128), F32)
            for h in range(N_HEADS):
                sc = s_s[s * N_HEADS + h]
                sink = sink_ref[h]
                m = jnp.maximum(jnp.max(sc, axis=1, keepdims=True), sink)
                pe = jnp.exp(sc - m)
                den = jnp.sum(pe, axis=1, keepdims=True) + jnp.exp(sink - m)
                p_s[s * N_HEADS + h] = pe.astype(BF16)
                den_s[s * N_HEADS + h] = den
                lse_tile = jnp.where(lane == h, m + jnp.log(den), lse_tile)
            lse_ref[subs[s], :] = lse_tile
        for s in range(SWA_SUB):
            vwin = wins[s][1]
            for pr in range(N_HEADS // 2):
                outs = []
                for h in (2 * pr, 2 * pr + 1):
                    hk = h // KV_GROUP
                    outs.append(_nn(p_s[s * N_HEADS + h], vwin[:, hk * HEAD_DIM:(hk + 1) * HEAD_DIM]) / den_s[s * N_HEADS + h])
                o_ref[subs[s], 128 * pr:128 * (pr + 1)] = jnp.concatenate(outs, axis=1).astype(BF16)

    rows = SWA_SUB * BLOCK
    steps = nb // SWA_SUB
    return _pallas(
        body, name="swa_fwd", grid=(t_tok // seq, steps),
        in_specs=[pl.BlockSpec(memory_space=pltpu.SMEM),
                  pl.BlockSpec((rows, 512), lambda b, n: (b * steps + n, C_AQ // 512)),
                  pl.BlockSpec((seq, 128), lambda b, n: (b, C_AK // 128)),
                  pl.BlockSpec((seq, 128), lambda b, n: (b, C_AV // 128)),
                  pl.BlockSpec((N_HEADS, BLOCK, 2 * BLOCK), lambda b, n: (0, 0, 0))],
        out_specs=[pl.BlockSpec((rows, 512), lambda b, n: (b * steps + n, 0)),
                   pl.BlockSpec((rows, 128), lambda b, n: (b * steps + n, 0))],
        out_shape=[jax.ShapeDtypeStruct((t_tok, 512), BF16), jax.ShapeDtypeStruct((t_tok, 128), F32)],
        scratch_shapes=[pltpu.VMEM((SWA_SUB * N_HEADS, BLOCK, 2 * BLOCK), F32),
                        pltpu.VMEM((SWA_SUB * N_HEADS, BLOCK, 2 * BLOCK), BF16),
                        pltpu.VMEM((SWA_SUB * N_HEADS, BLOCK, 1), F32)],
        compiler_params=_cp(("arbitrary", "arbitrary"), VMEM_MID))(sinks, qkvg, qkvg, qkvg, bias)


def _swa_bwd(qkvg, du_a, a_out, lse, bias, sinks, seq):
    t_tok = qkvg.shape[0]
    nb = seq // BLOCK

    def body(sink_ref, q_ref, k_ref, v_ref, do_ref, o_ref, lse_ref, bias_ref,
             dq_ref, dkv_ref, dbias_ref, dsink_ref, kv_s, s_s, dp_s, pb_s, db_s):
        b, n = pl.program_id(0), pl.program_id(1)

        @pl.when(jnp.logical_and(b == 0, n == 0))
        def _():
            dbias_ref[...] = jnp.zeros_like(dbias_ref)
            dsink_ref[...] = jnp.zeros_like(dsink_ref)

        @pl.when(n == 0)
        def _():
            kv_s[...] = jnp.zeros_like(kv_s)

        subs = [pl.ds(s * BLOCK, BLOCK) for s in range(SWA_SUB)]
        wins = [_swa_window(k_ref, v_ref, SWA_SUB * n + s) for s in range(SWA_SUB)]
        for s in range(SWA_SUB):
            kwin, vwin, mask = wins[s][:3]
            for h in range(N_HEADS):
                hk = h // KV_GROUP
                s_s[s * N_HEADS + h] = _swa_logits(q_ref.at[subs[s]], kwin, bias_ref, h, mask)
                dp_s[s * N_HEADS + h] = _nt(_head_cols(do_ref.at[subs[s]], h), vwin[:, hk * HEAD_DIM:(hk + 1) * HEAD_DIM])
        for s in range(SWA_SUB):
            lse_tile = lse_ref[subs[s], :]
            do_s, o_s = do_ref.at[subs[s]], o_ref.at[subs[s]]
            for h in range(N_HEADS):
                delta = jnp.sum(_head_cols(do_s, h).astype(F32) * _head_cols(o_s, h).astype(F32), axis=1, keepdims=True)
                lse_h = _col(lse_tile, h)
                pe = jnp.exp(s_s[s * N_HEADS + h] - lse_h)
                ds = pe * (dp_s[s * N_HEADS + h] - delta)
                dbias_ref[h] += ds
                psink = jnp.exp(sink_ref[h] - lse_h)
                dsink_ref[h:h + 1, :] += jnp.broadcast_to(jnp.sum(-psink * delta, axis=0, keepdims=True), (1, 128))
                pb_s[s * N_HEADS + h] = pe.astype(BF16)
                db_s[s * N_HEADS + h] = ds.astype(BF16)
        for s in range(SWA_SUB):
            kwin, _, _, prev, cur = wins[s]
            q_s, do_s = q_ref.at[subs[s]], do_ref.at[subs[s]]
            for pr in range(N_HEADS // 2):
                dqs = []
                for h in (2 * pr, 2 * pr + 1):
                    hk = h // KV_GROUP
                    dqs.append(_nn(db_s[s * N_HEADS + h], kwin[:, hk * HEAD_DIM:(hk + 1) * HEAD_DIM]) * SCALE)
                dq_ref[subs[s], 128 * pr:128 * (pr + 1)] = jnp.concatenate(dqs, axis=1).astype(BF16)
            dks, dvs = [], []
            for hk in range(N_HEADS // KV_GROUP):
                dk = jnp.zeros((2 * BLOCK, HEAD_DIM), F32)
                dv = jnp.zeros((2 * BLOCK, HEAD_DIM), F32)
                for h in range(hk * KV_GROUP, (hk + 1) * KV_GROUP):
                    dk = dk + _tn(db_s[s * N_HEADS + h], _head_cols(q_s, h))
                    dv = dv + _tn(pb_s[s * N_HEADS + h], _head_cols(do_s, h))
                dks.append(dk * SCALE)
                dvs.append(dv)
            upd = jnp.concatenate(dks + dvs, axis=1)
            kv_s[pl.ds(prev, BLOCK), :] += upd[:BLOCK]
            kv_s[pl.ds(cur, BLOCK), :] += upd[BLOCK:]

        @pl.when(n == steps - 1)
        def _():
            dkv_ref[...] = kv_s[...].astype(BF16)

    rows = SWA_SUB * BLOCK
    steps = nb // SWA_SUB
    tile = (SWA_SUB * N_HEADS, BLOCK, 2 * BLOCK)
    return _pallas(
        body, name="swa_bwd", grid=(t_tok // seq, steps),
        in_specs=[pl.BlockSpec(memory_space=pltpu.SMEM),
                  pl.BlockSpec((rows, 512), lambda b, n: (b * steps + n, C_AQ // 512)),
                  pl.BlockSpec((seq, 128), lambda b, n: (b, C_AK // 128)),
                  pl.BlockSpec((seq, 128), lambda b, n: (b, C_AV // 128)),
                  pl.BlockSpec((rows, 512), lambda b, n: (b * steps + n, 0)),
                  pl.BlockSpec((rows, 512), lambda b, n: (b * steps + n, 0)),
                  pl.BlockSpec((rows, 128), lambda b, n: (b * steps + n, 0)),
                  pl.BlockSpec((N_HEADS, BLOCK, 2 * BLOCK), lambda b, n: (0, 0, 0))],
        out_specs=[pl.BlockSpec((rows, 512), lambda b, n: (b * steps + n, 0)),
                   pl.BlockSpec((seq, 256), lambda b, n: (b, 0)),
                   pl.BlockSpec((N_HEADS, BLOCK, 2 * BLOCK), lambda b, n: (0, 0, 0)),
                   pl.BlockSpec((N_HEADS, 128), lambda b, n: (0, 0))],
        out_shape=[jax.ShapeDtypeStruct((t_tok, 512), BF16), jax.ShapeDtypeStruct((t_tok, 256), BF16),
                   jax.ShapeDtypeStruct((N_HEADS, BLOCK, 2 * BLOCK), F32), jax.ShapeDtypeStruct((N_HEADS, 128), F32)],
        scratch_shapes=[pltpu.VMEM((seq, 256), F32), pltpu.VMEM(tile, F32), pltpu.VMEM(tile, F32),
                        pltpu.VMEM(tile, BF16), pltpu.VMEM(tile, BF16)],
        compiler_params=_cp(("arbitrary", "arbitrary"), VMEM_MID))(sinks, qkvg, qkvg, qkvg, du_a, a_out, lse, bias)


def _out_proj(name, u_parts, gate_arr, gate_blk, w_out, x, gmod, seq):
    t_tok = x.shape[0]
    nu = len(u_parts)

    def body(*refs):
        u_refs = refs[:nu]
        g_ref, w_ref, x_ref, gm_ref, yg_ref, y_ref, xn_ref = refs[nu:]
        u = jnp.concatenate([r[...].astype(F32) for r in u_refs], axis=1) if nu > 1 else u_refs[0][...].astype(F32)
        yg = (u * _silu(g_ref[...].astype(F32))).astype(BF16)
        yg_ref[...] = yg
        y = _nn(yg, w_ref[...])
        y_ref[...] = y.astype(BF16)
        xn_ref[...] = x_ref[...] + gm_ref[...] * y

    row = lambda w: pl.BlockSpec((TM, w), lambda i: (i, 0))
    in_specs = [row(u.shape[1]) for u in u_parts]
    in_specs += [pl.BlockSpec((TM, D_MODEL), lambda i: (i, gate_blk)),
                 pl.BlockSpec((D_MODEL, D_MODEL), lambda i: (0, 0)), row(D_MODEL),
                 pl.BlockSpec((None, 1, D_MODEL), lambda i: (i * TM // seq, 0, 0))]
    return _pallas(
        body, name=name, grid=(t_tok // TM,), in_specs=in_specs,
        out_specs=[row(D_MODEL)] * 3,
        out_shape=[jax.ShapeDtypeStruct((t_tok, D_MODEL), BF16)] * 2 + [jax.ShapeDtypeStruct((t_tok, D_MODEL), F32)],
        compiler_params=_cp(("arbitrary",), VMEM_MID))(*u_parts, gate_arr, w_out, x, gmod)


def _out_proj_bwd(name, dxn, gmod, y, w_out, seq, attn=None):
    t_tok = dxn.shape[0]
    tiles_per_seq = seq // TM

    def body(*refs):
        if attn is None:
            dxn_ref, gm_ref, y_ref, w_ref, dy_ref, dgm_ref, dyg_ref = refs
        else:
            dxn_ref, gm_ref, y_ref, w_ref, a_ref, b_ref, g_ref, dy_ref, dgm_ref, dua_ref, dub_ref, dg_ref = refs
        i = pl.program_id(0)
        dxv = dxn_ref[...]
        dy = (dxv * gm_ref[...]).astype(BF16)
        dy_ref[...] = dy

        @pl.when(i % tiles_per_seq == 0)
        def _():
            dgm_ref[...] = jnp.zeros_like(dgm_ref)

        dgm_ref[...] += jnp.sum(dxv * y_ref[...].astype(F32), axis=0, keepdims=True)
        dyg = _nn(dy, w_ref[...])
        if attn is None:
            dyg_ref[...] = dyg
        else:
            gt = g_ref[...].astype(F32)
            du = dyg * _silu(gt)
            dua_ref[...] = du[:, :512].astype(BF16)
            dub_ref[...] = du[:, 512:].astype(BF16)
            u = jnp.concatenate([a_ref[...].astype(F32), b_ref[...].astype(F32)], axis=1)
            dg_ref[...] = (dyg * u * _dsilu(gt)).astype(BF16)

    row = lambda w: pl.BlockSpec((TM, w), lambda i: (i, 0))
    mod_spec = pl.BlockSpec((None, 1, D_MODEL), lambda i: (i * TM // seq, 0, 0))
    in_specs = [row(D_MODEL), mod_spec, row(D_MODEL), pl.BlockSpec((D_MODEL, D_MODEL), lambda i: (0, 0))]
    out_specs = [row(D_MODEL), mod_spec]
    out_shape = [jax.ShapeDtypeStruct((t_tok, D_MODEL), BF16), jax.ShapeDtypeStruct(gmod.shape, F32)]
    args = [dxn, gmod, y, w_out]
    if attn is None:
        out_specs.append(row(D_MODEL))
        out_shape.append(jax.ShapeDtypeStruct((t_tok, D_MODEL), F32))
    else:
        in_specs += [row(512), row(512), pl.BlockSpec((TM, D_MODEL), lambda i: (i, C_GATE // D_MODEL))]
        out_specs += [row(512), row(512), row(D_MODEL)]
        out_shape += [jax.ShapeDtypeStruct((t_tok, 512), BF16)] * 2 + [jax.ShapeDtypeStruct((t_tok, D_MODEL), BF16)]
        args += list(attn)
    return _pallas(body, name=name, grid=(t_tok // TM,), in_specs=in_specs, out_specs=out_specs,
                   out_shape=out_shape, compiler_params=_cp(("arbitrary",), VMEM_MID))(*args)


def _norm_bwd(name, parts, w, x, g, scale, dxn, seq, rows_part=None):
    t_tok = x.shape[0]
    npart = len(parts)
    tiles_per_seq = seq // TM
    nrow_in = 0 if rows_part is None else 2

    def body(*refs):
        p_refs = refs[:npart]
        w_ref, x_ref, g_ref, sc_ref, dxn_ref = refs[npart:npart + 5]
        dx_ref, dss_ref, dg_ref = refs[npart + 5 + nrow_in:]
        i = pl.program_id(0)
        dh = jnp.zeros((TM, D_MODEL), F32)
        if rows_part is not None:
            r_ref, wr_ref = refs[npart + 5:npart + 7]
            dh = dh + _tn(r_ref[...].astype(BF16), wr_ref[...])
        for (arr, off), p_ref in zip(parts, p_refs):
            dh = dh + _nn(p_ref[...], w_ref[off:off + arr.shape[1], :])
        xv = x_ref[...]
        rstd = lax.rsqrt(jnp.mean(xv * xv, axis=-1, keepdims=True) + EPS)
        xhat = xv * rstd
        gv = g_ref[...]
        nrm = xhat * gv

        @pl.when(i % tiles_per_seq == 0)
        def _():
            dss_ref[...] = jnp.zeros_like(dss_ref)

        @pl.when(i == 0)
        def _():
            dg_ref[...] = jnp.zeros_like(dg_ref)

        dss_ref[0:1, :] += jnp.sum(dh, axis=0, keepdims=True)
        dss_ref[1:2, :] += jnp.sum(dh * nrm, axis=0, keepdims=True)
        dn = dh * (1.0 + sc_ref[...])
        dg_ref[0:1, :] += jnp.sum(dn * xhat, axis=0, keepdims=True)
        dxhat = dn * gv
        dx_ref[...] = rstd * (dxhat - xhat * jnp.mean(dxhat * xhat, axis=-1, keepdims=True)) + dxn_ref[...]

    row = lambda wd: pl.BlockSpec((TM, wd), lambda i: (i, 0))
    w_spec = pl.BlockSpec(w.shape, lambda i: (0, 0))
    in_specs = [row(a.shape[1]) for a, _ in parts]
    in_specs += [w_spec, row(D_MODEL), pl.BlockSpec((1, D_MODEL), lambda i: (0, 0)),
                 pl.BlockSpec((None, 1, D_MODEL), lambda i: (i * TM // seq, 0, 0)), row(D_MODEL)]
    args = [a for a, _ in parts] + [w, x, g, scale, dxn]
    if rows_part is not None:
        in_specs += [pl.BlockSpec((8, TM), lambda i: (0, i)), pl.BlockSpec((8, D_MODEL), lambda i: (0, 0))]
        args += list(rows_part)
    nseq = t_tok // seq
    return _pallas(
        body, name=name, grid=(t_tok // TM,), in_specs=in_specs,
        out_specs=[row(D_MODEL), pl.BlockSpec((None, 8, D_MODEL), lambda i: (i * TM // seq, 0, 0)),
                   pl.BlockSpec((8, D_MODEL), lambda i: (0, 0))],
        out_shape=[jax.ShapeDtypeStruct((t_tok, D_MODEL), F32), jax.ShapeDtypeStruct((nseq, 8, D_MODEL), F32),
                   jax.ShapeDtypeStruct((8, D_MODEL), F32)],
        compiler_params=_cp(("arbitrary",), VMEM_BIG))(*args)


def _dw(name, a, parts, blocked=None):
    t_tok, ka = a.shape
    tt = min(1024, t_tok)
    npart = len(parts)
    nt = t_tok // tt

    def body(*refs):
        a_ref = refs[0]
        p_refs = refs[1:1 + npart]
        o_refs = refs[1 + npart:1 + 2 * npart]
        acc_refs = refs[1 + 2 * npart:]
        t = pl.program_id(0)
        av = a_ref[...]
        for p_ref, acc in zip(p_refs, acc_refs):
            upd = _tn(av, p_ref[...])

            @pl.when(t == 0)
            def _():
                acc[...] = upd

            @pl.when(t > 0)
            def _():
                acc[...] += upd

        @pl.when(t == nt - 1)
        def _():
            for o_ref, acc in zip(o_refs, acc_refs):
                if blocked is None:
                    o_ref[...] = acc[...].astype(BF16)
                else:
                    for j in range(o_ref.shape[0]):
                        o_ref[j] = acc[:, j * blocked:(j + 1) * blocked].astype(BF16)

    in_specs = [pl.BlockSpec((tt, ka), lambda t: (t, 0))]
    in_specs += [pl.BlockSpec((tt, p.shape[1]), lambda t: (t, 0)) for p in parts]
    if blocked is None:
        out_shape = [jax.ShapeDtypeStruct((ka, p.shape[1]), BF16) for p in parts]
        out_specs = [pl.BlockSpec((ka, p.shape[1]), lambda t: (0, 0)) for p in parts]
    else:
        out_shape = [jax.ShapeDtypeStruct((p.shape[1] // blocked, ka, blocked), BF16) for p in parts]
        out_specs = [pl.BlockSpec((p.shape[1] // blocked, ka, blocked), lambda t: (0, 0, 0)) for p in parts]
    return _pallas(body, name=name, grid=(nt,), in_specs=in_specs, out_specs=out_specs, out_shape=out_shape,
                   scratch_shapes=[pltpu.VMEM((ka, p.shape[1]), F32) for p in parts],
                   compiler_params=_cp(("arbitrary",), VMEM_BIG))(a, *parts)


def _dw_rows(name, rows_t, h):
    t_tok = h.shape[0]
    tt = 512

    def body(r_ref, h_ref, o_ref):
        @pl.when(pl.program_id(0) == 0)
        def _():
            o_ref[...] = jnp.zeros_like(o_ref)

        o_ref[...] += _nn(r_ref[...].astype(BF16), h_ref[...])

    return _pallas(body, name=name, grid=(t_tok // tt,),
                   in_specs=[pl.BlockSpec((8, tt), lambda t: (0, t)), pl.BlockSpec((tt, D_MODEL), lambda t: (t, 0))],
                   out_specs=pl.BlockSpec((8, D_MODEL), lambda t: (0, 0)),
                   out_shape=jax.ShapeDtypeStruct((8, D_MODEL), F32),
                   compiler_params=_cp(("arbitrary",), VMEM_MID))(rows_t, h)


def _lru_gates(xc, blk, wa_ref, wx_ref, ba_ref, bx_ref, sp):
    cols = slice(blk * LRU_BLOCK_W, (blk + 1) * LRU_BLOCK_W)
    xb = xc[:, cols].astype(BF16)
    r = _sigmoid(_nn(xb, wa_ref[blk].astype(BF16)) + ba_ref[:, cols])
    ig = _sigmoid(_nn(xb, wx_ref[blk].astype(BF16)) + bx_ref[:, cols])
    log_a = -LRU_C * r * sp[:, cols]
    a = jnp.exp(log_a)
    x2 = 2.0 * log_a
    series = -x2 * (1.0 + x2 * (0.5 + x2 * (1.0 / 6.0)))
    z = jnp.where(x2 > -0.01, series, 1.0 - a * a)
    mult = z * lax.rsqrt(jnp.maximum(z, 1e-30))
    return xb, r, ig, a, mult


def _softplus_neg(lam):
    return jnp.maximum(-lam, 0.0) + jnp.log(1.0 + jnp.exp(-jnp.abs(lam)))


def _conv_taps(xe_ref, cw_ref, cb_ref):
    xc = cb_ref[...] + xe_ref[8:8 + TC, :] * cw_ref[3:4, :]
    for k in range(1, 4):
        xc = xc + xe_ref[8 - k:8 - k + TC, :] * cw_ref[3 - k:4 - k, :]
    return xc


def _lru_fwd(proj, cw, cb, w_a, b_a, w_x, b_x, lam, seq):
    t_tok = proj.shape[0]
    nc = seq // TC

    def body(x_ref, cw_ref, cb_ref, wa_ref, ba_ref, wx_ref, bx_ref, lam_ref, hs_ref, xe_s, a_s, u_s, h_s):
        c = pl.program_id(1)

        @pl.when(c == 0)
        def _():
            xe_s[0:8, :] = jnp.zeros((8, D_MODEL), F32)
            h_s[...] = jnp.zeros_like(h_s)

        xe_s[8:8 + TC, :] = x_ref[...]
        xc = _conv_taps(xe_s, cw_ref, cb_ref)
        sp = _softplus_neg(lam_ref[...])
        for blk in range(LRU_BLOCKS):
            cols = slice(blk * LRU_BLOCK_W, (blk + 1) * LRU_BLOCK_W)
            _, _, ig, a, mult = _lru_gates(xc, blk, wa_ref, wx_ref, ba_ref, bx_ref, sp)
            a_s[:, cols] = a
            u_s[:, cols] = mult * ig * xc[:, cols]

        def step(t, h):
            h = a_s[pl.ds(t, 1), :] * h + u_s[pl.ds(t, 1), :]
            hs_ref[pl.ds(t, 1), :] = h
            return h

        h_s[0:1, :] = lax.fori_loop(0, TC, step, h_s[0:1, :], unroll=8)
        xe_s[0:8, :] = xe_s[TC:TC + 8, :]

    full = lambda shape: pl.BlockSpec(shape, lambda b, c: (0,) * len(shape))
    return _pallas(
        body, name="lru_fwd", grid=(t_tok // seq, nc),
        in_specs=[pl.BlockSpec((TC, D_MODEL), lambda b, c: (b * nc + c, 0)), full((4, D_MODEL)), full((1, D_MODEL)),
                  full((LRU_BLOCKS, LRU_BLOCK_W, LRU_BLOCK_W)), full((1, D_MODEL)),
                  full((LRU_BLOCKS, LRU_BLOCK_W, LRU_BLOCK_W)), full((1, D_MODEL)), full((1, D_MODEL))],
        out_specs=pl.BlockSpec((TC, D_MODEL), lambda b, c: (b * nc + c, 0)),
        out_shape=jax.ShapeDtypeStruct((t_tok, D_MODEL), F32),
        scratch_shapes=[pltpu.VMEM((TC + 8, D_MODEL), F32), pltpu.VMEM((TC, D_MODEL), F32),
                        pltpu.VMEM((TC, D_MODEL), F32), pltpu.VMEM((8, D_MODEL), F32)],
        compiler_params=_cp(("arbitrary", "arbitrary"), VMEM_BIG))(proj, cw, cb, w_a, b_a, w_x, b_x, lam)


def _lru_bwd(proj, hs, dyh, cw, cb, w_a, b_a, w_x, b_x, lam, seq):
    t_tok = proj.shape[0]
    nc = seq // TC

    def body(x_ref, xh_ref, g_ref, hs_ref, hh_ref, dy_ref, cw_ref, cb_ref, wa_ref, ba_ref, wx_ref, bx_ref, lam_ref,
             dp_ref, dcw_ref, dvec_ref, dwa_ref, dwx_ref,
             xe_s, he_s, de_s, a_s, r_s, i_s, m_s, dh_s, carry_s):
        b, cr = pl.program_id(0), pl.program_id(1)
        c = nc - 1 - cr

        @pl.when(jnp.logical_and(b == 0, cr == 0))
        def _():
            dcw_ref[...] = jnp.zeros_like(dcw_ref)
            dvec_ref[...] = jnp.zeros_like(dvec_ref)
            dwa_ref[...] = jnp.zeros_like(dwa_ref)
            dwx_ref[...] = jnp.zeros_like(dwx_ref)

        @pl.when(cr == 0)
        def _():
            carry_s[...] = jnp.zeros_like(carry_s)
            de_s[TC:TC + 8, :] = jnp.zeros((8, D_MODEL), F32)

        first = c == 0
        xe_s[0:8, :] = jnp.where(first, 0.0, xh_ref[...])
        xe_s[8:8 + TC, :] = x_ref[...]
        he_s[0:8, :] = jnp.where(first, 0.0, hh_ref[...])
        he_s[8:8 + TC, :] = hs_ref[...]
        xc = _conv_taps(xe_s, cw_ref, cb_ref)
        lam_v = lam_ref[...]
        sp = _softplus_neg(lam_v)
        for blk in range(LRU_BLOCKS):
            cols = slice(blk * LRU_BLOCK_W, (blk + 1) * LRU_BLOCK_W)
            _, r, ig, a, mult = _lru_gates(xc, blk, wa_ref, wx_ref, ba_ref, bx_ref, sp)
            a_s[:, cols], r_s[:, cols], i_s[:, cols], m_s[:, cols] = a, r, ig, mult

        gt = g_ref[...]
        dyh = dy_ref[...]
        dh_s[...] = dyh * _silu(gt)
        dp_ref[:, D_MODEL:] = (dyh * hs_ref[...] * _dsilu(gt)).astype(BF16)

        def step(k, carry):
            t = TC - 1 - k
            dh = dh_s[pl.ds(t, 1), :] + carry
            dh_s[pl.ds(t, 1), :] = dh
            return a_s[pl.ds(t, 1), :] * dh

        carry_s[0:1, :] = lax.fori_loop(0, TC, step, carry_s[0:1, :], unroll=8)

        hprev = he_s[7:7 + TC, :]
        for blk in range(LRU_BLOCKS):
            cols = slice(blk * LRU_BLOCK_W, (blk + 1) * LRU_BLOCK_W)
            xcb = xc[:, cols]
            a, r, ig, mult, dh = a_s[:, cols], r_s[:, cols], i_s[:, cols], m_s[:, cols], dh_s[:, cols]
            spb = sp[:, cols]
            dmult = dh * ig * xcb
            di = dh * mult * xcb
            dxc = dh * mult * ig
            dla = dh * hprev[:, cols] * a - dmult * (a * a) * lax.rsqrt(jnp.maximum(mult * mult, 1e-30))
            dr = dla * (-LRU_C * spb)
            dsp = jnp.sum(dla * (-LRU_C * r), axis=0, keepdims=True)
            dga = dr * r * (1.0 - r)
            dgx = di * ig * (1.0 - ig)
            dga_b, dgx_b = dga.astype(BF16), dgx.astype(BF16)
            xb = xcb.astype(BF16)
            dxc = dxc + _nt(dga_b, wa_ref[blk].astype(BF16)) + _nt(dgx_b, wx_ref[blk].astype(BF16))
            dwa_ref[blk] += _tn(xb, dga_b)
            dwx_ref[blk] += _tn(xb, dgx_b)
            dvec_ref[1:2, cols] += jnp.sum(dga, axis=0, keepdims=True)
            dvec_ref[2:3, cols] += jnp.sum(dgx, axis=0, keepdims=True)
            dvec_ref[3:4, cols] += dsp * (-1.0 / (1.0 + jnp.exp(lam_v[:, cols])))
            de_s[0:TC, cols] = dxc

        dxc = de_s[0:TC, :]
        dvec_ref[0:1, :] += jnp.sum(dxc, axis=0, keepdims=True)
        dxr = dxc * cw_ref[3:4, :]
        dcw_ref[3:4, :] += jnp.sum(dxc * xe_s[8:8 + TC, :], axis=0, keepdims=True)
        for k in range(1, 4):
            dxr = dxr + de_s[k:k + TC, :] * cw_ref[3 - k:4 - k, :]
            dcw_ref[3 - k:4 - k, :] += jnp.sum(dxc * xe_s[8 - k:8 - k + TC, :], axis=0, keepdims=True)
        dp_ref[:, :D_MODEL] = dxr.astype(BF16)
        de_s[TC:TC + 8, :] = de_s[0:8, :]

    chunk = lambda col: pl.BlockSpec((TC, D_MODEL), lambda b, cr: (b * nc + nc - 1 - cr, col))
    halo = lambda col: pl.BlockSpec(
        (8, D_MODEL), lambda b, cr: (jnp.maximum((b * nc + nc - 1 - cr) * (TC // 8) - 1, 0), col))
    full = lambda shape: pl.BlockSpec(shape, lambda b, cr: (0,) * len(shape))
    wblk = (LRU_BLOCKS, LRU_BLOCK_W, LRU_BLOCK_W)
    return _pallas(
        body, name="lru_bwd", grid=(t_tok // seq, nc),
        in_specs=[chunk(0), halo(0), chunk(1), chunk(0), halo(0), chunk(0),
                  full((4, D_MODEL)), full((1, D_MODEL)), full(wblk), full((1, D_MODEL)), full(wblk),
                  full((1, D_MODEL)), full((1, D_MODEL))],
        out_specs=[pl.BlockSpec((TC, 2 * D_MODEL), lambda b, cr: (b * nc + nc - 1 - cr, 0)),
                   full((8, D_MODEL)), full((8, D_MODEL)), full(wblk), full(wblk)],
        out_shape=[jax.ShapeDtypeStruct((t_tok, 2 * D_MODEL), BF16), jax.ShapeDtypeStruct((8, D_MODEL), F32),
                   jax.ShapeDtypeStruct((8, D_MODEL), F32), jax.ShapeDtypeStruct(wblk, F32),
                   jax.ShapeDtypeStruct(wblk, F32)],
        scratch_shapes=[pltpu.VMEM((TC + 8, D_MODEL), F32), pltpu.VMEM((TC + 8, D_MODEL), F32),
                        pltpu.VMEM((TC + 8, D_MODEL), F32)]
        + [pltpu.VMEM((TC, D_MODEL), F32)] * 5 + [pltpu.VMEM((8, D_MODEL), F32)],
        compiler_params=_cp(("arbitrary", "arbitrary"), VMEM_BIG),
    )(proj, proj, proj, hs, hs, dyh, cw, cb, w_a, b_a, w_x, b_x, lam)


def _last_layer_tail(hs, proj, w_out, w_out_t, x, gmod, final_g, target, seq):
    t_tok = x.shape[0]
    tiles_per_seq = seq // TM

    def body(hs_ref, g_ref, w_ref, wt_ref, x_ref, gm_ref, fg_ref, t_ref,
             yg_ref, dx_ref, dy_ref, dyg_ref, dgm_ref, loss_ref, dfg_ref):
        i = pl.program_id(0)

        @pl.when(i == 0)
        def _():
            loss_ref[...] = jnp.zeros_like(loss_ref)
            dfg_ref[...] = jnp.zeros_like(dfg_ref)

        @pl.when(i % tiles_per_seq == 0)
        def _():
            dgm_ref[...] = jnp.zeros_like(dgm_ref)

        gm = gm_ref[...]
        yg = (hs_ref[...] * _silu(g_ref[...])).astype(BF16)
        yg_ref[...] = yg
        y = _nn(yg, w_ref[...])
        xv = x_ref[...] + gm * y
        gv = fg_ref[...]
        rstd = lax.rsqrt(jnp.mean(xv * xv, axis=-1, keepdims=True) + EPS)
        xhat = xv * rstd
        err = xhat * gv - t_ref[...]
        loss_ref[0:1, :] += jnp.sum(err * err, axis=0, keepdims=True) * (0.5 / D_MODEL)
        dout = err * (1.0 / D_MODEL)
        dfg_ref[0:1, :] += jnp.sum(dout * xhat, axis=0, keepdims=True)
        dxhat = dout * gv
        dxv = rstd * (dxhat - xhat * jnp.mean(dxhat * xhat, axis=-1, keepdims=True))
        dx_ref[...] = dxv
        dgm_ref[...] += jnp.sum(dxv * y, axis=0, keepdims=True)
        dy = (dxv * gm).astype(BF16)
        dy_ref[...] = dy
        dyg_ref[...] = _nn(dy, wt_ref[...])

    row = pl.BlockSpec((TM, D_MODEL), lambda i: (i, 0))
    acc = pl.BlockSpec((8, D_MODEL), lambda i: (0, 0))
    mod_spec = pl.BlockSpec((None, 1, D_MODEL), lambda i: (i * TM // seq, 0, 0))
    return _pallas(
        body, name="last_layer_tail", grid=(t_tok // TM,),
        in_specs=[row, pl.BlockSpec((TM, D_MODEL), lambda i: (i, 1)), pl.BlockSpec((D_MODEL, D_MODEL), lambda i: (0, 0)),
                  pl.BlockSpec((D_MODEL, D_MODEL), lambda i: (0, 0)),
                  row, mod_spec, pl.BlockSpec((1, D_MODEL), lambda i: (0, 0)), row],
        out_specs=[row, row, row, row, mod_spec, acc, acc],
        out_shape=[jax.ShapeDtypeStruct((t_tok, D_MODEL), BF16), jax.ShapeDtypeStruct((t_tok, D_MODEL), F32),
                   jax.ShapeDtypeStruct((t_tok, D_MODEL), BF16), jax.ShapeDtypeStruct((t_tok, D_MODEL), F32),
                   jax.ShapeDtypeStruct(gmod.shape, F32), jax.ShapeDtypeStruct((8, D_MODEL), F32),
                   jax.ShapeDtypeStruct((8, D_MODEL), F32)],
        compiler_params=_cp(("arbitrary",), VMEM_BIG))(hs, proj, w_out, w_out_t, x, gmod, final_g, target)


def _final_loss(x, g, target):
    t_tok = x.shape[0]

    def body(x_ref, g_ref, t_ref, dx_ref, loss_ref, dg_ref):
        @pl.when(pl.program_id(0) == 0)
        def _():
            loss_ref[...] = jnp.zeros_like(loss_ref)
            dg_ref[...] = jnp.zeros_like(dg_ref)

        xv = x_ref[...]
        gv = g_ref[...]
        rstd = lax.rsqrt(jnp.mean(xv * xv, axis=-1, keepdims=True) + EPS)
        xhat = xv * rstd
        err = xhat * gv - t_ref[...]
        loss_ref[0:1, :] += jnp.sum(err * err, axis=0, keepdims=True) * (0.5 / D_MODEL)
        dout = err * (1.0 / D_MODEL)
        dg_ref[0:1, :] += jnp.sum(dout * xhat, axis=0, keepdims=True)
        dxhat = dout * gv
        dx_ref[...] = rstd * (dxhat - xhat * jnp.mean(dxhat * xhat, axis=-1, keepdims=True))

    row = pl.BlockSpec((TM, D_MODEL), lambda i: (i, 0))
    acc = pl.BlockSpec((8, D_MODEL), lambda i: (0, 0))
    return _pallas(body, name="final_loss", grid=(t_tok // TM,),
                   in_specs=[row, pl.BlockSpec((1, D_MODEL), lambda i: (0, 0)), row],
                   out_specs=[row, acc, acc],
                   out_shape=[jax.ShapeDtypeStruct((t_tok, D_MODEL), F32)] + [jax.ShapeDtypeStruct((8, D_MODEL), F32)] * 2,
                   compiler_params=_cp(("arbitrary",), VMEM_MID))(x, g, target)


def _adam_math(w, g, m, v):
    m_new = ADAM_B1 * m + (1.0 - ADAM_B1) * g
    v_new = ADAM_B2 * v + (1.0 - ADAM_B2) * (g * g)
    m_hat = m_new / (1.0 - ADAM_B1 ** ADAM_STEP)
    v_hat = v_new / (1.0 - ADAM_B2 ** ADAM_STEP)
    delta = -ADAM_LR * (m_hat / (jnp.sqrt(v_hat) + ADAM_EPS) + ADAM_WD * w)
    return delta, m_new, v_new


def _sum_leading(name, x, out_dtype=F32):
    n, rows, cols = x.shape
    tr = PACK_ROWS if rows % PACK_ROWS == 0 else rows

    def body(x_ref, o_ref):
        acc = x_ref[0].astype(F32)
        for d in range(1, n):
            acc = acc + x_ref[d].astype(F32)
        o_ref[...] = acc.astype(out_dtype)

    return _pallas(body, name=name, grid=(rows // tr,),
                   in_specs=[pl.BlockSpec((n, tr, cols), lambda i: (0, i, 0))],
                   out_specs=pl.BlockSpec((tr, cols), lambda i: (i, 0)),
                   out_shape=jax.ShapeDtypeStruct((rows, cols), out_dtype),
                   compiler_params=_cp(("arbitrary",), VMEM_MID))(x)


def _adamw(name, w, m, v, g=None, parts=None):
    rows, cols = w.shape
    tr = rows if rows <= 256 else 256

    def body(*refs):
        w_ref, m_ref, v_ref, g_in, g_ref, d_ref, mo_ref, vo_ref = refs
        if parts is None:
            gv = g_in[...]
        else:
            acc = g_in[0].astype(F32)
            for d in range(1, parts.shape[0]):
                acc = acc + g_in[d].astype(F32)
            gv = acc[:, :cols]
        delta, m_new, v_new = _adam_math(w_ref[...], gv, m_ref[...], v_ref[...])
        g_ref[...] = gv
        d_ref[...] = delta
        mo_ref[...] = m_new
        vo_ref[...] = v_new

    row = pl.BlockSpec((tr, cols), lambda i: (i, 0))
    if parts is None:
        g_spec, g_arg = row, g
    else:
        g_spec, g_arg = pl.BlockSpec((parts.shape[0], tr, parts.shape[2]), lambda i: (0, i, 0)), parts
    return _pallas(body, name=name, grid=(rows // tr,), in_specs=[row, row, row, g_spec], out_specs=[row] * 4,
                   out_shape=[jax.ShapeDtypeStruct((rows, cols), F32)] * 4,
                   compiler_params=_cp(("arbitrary",), VMEM_MID))(w, m, v, g_arg)


def _adamw_many(name, groups):
    ntens = len(groups)

    def body(*refs):
        ins, outs = refs[:4 * ntens], refs[4 * ntens:]
        for k in range(ntens):
            w_ref, m_ref, v_ref, g_ref = ins[4 * k:4 * k + 4]
            gv = g_ref[...]
            delta, m_new, v_new = _adam_math(w_ref[...], gv, m_ref[...], v_ref[...])
            for o_ref, val in zip(outs[4 * k:4 * k + 4], (gv, delta, m_new, v_new)):
                o_ref[...] = val

    flat = [a for grp in groups for a in grp]
    out_shape = [jax.ShapeDtypeStruct(grp[0].shape, F32) for grp in groups for _ in range(4)]
    outs = _pallas(body, name=name, out_shape=out_shape, compiler_params=_cp(vmem=VMEM_MID))(*flat)
    return [tuple(outs[4 * k:4 * k + 4]) for k in range(ntens)]


def _pack_rows(arrs):
    rows, meta, total = [], [], 0
    for a in arrs:
        flat = a.reshape(-1)
        nrow = -(-flat.shape[0] // 1024) * 8
        rows.append(jnp.pad(flat, (0, nrow * 128 - flat.shape[0])).reshape(nrow, 128))
        meta.append((a.shape, flat.shape[0], nrow))
        total += nrow
    tail = -total % PACK_ROWS
    if tail:
        rows.append(jnp.zeros((tail, 128), F32))
    return jnp.concatenate(rows, axis=0), meta


def _unpack_rows(packed, meta):
    out, r0 = [], 0
    for shape, size, nrow in meta:
        out.append(packed[r0:r0 + nrow].reshape(-1)[:size].reshape(shape))
        r0 += nrow
    return out


WEIGHTS = ["rel_bias", "norm_g", "ada_w", "ada_b", "attn_w_in", "attn_sinks", "attn_b_f", "attn_w_out", "lru_w_in",
           "lru_conv_w", "lru_conv_b", "lru_w_a", "lru_b_a", "lru_w_x", "lru_b_x", "lru_lambda", "lru_w_out", "final_g"]
BIG = ["ada_w", "attn_w_in", "attn_w_out", "lru_w_in", "lru_w_out"]
PACK_ROWS = 256


def kernel(x, c, rel_bias, norm_g, ada_w, ada_b, attn_w_in, attn_sinks, attn_b_f, attn_w_out, lru_w_in, lru_conv_w, lru_conv_b, lru_w_a, lru_b_a, lru_w_x, lru_b_x, lru_lambda, lru_w_out, final_g, loss_target, m_rel_bias, m_norm_g, m_ada_w, m_ada_b, m_attn_w_in, m_attn_sinks, m_attn_b_f, m_attn_w_out, m_lru_w_in, m_lru_conv_w, m_lru_conv_b, m_lru_w_a, m_lru_b_a, m_lru_w_x, m_lru_b_x, m_lru_lambda, m_lru_w_out, m_final_g, v_rel_bias, v_norm_g, v_ada_w, v_ada_b, v_attn_w_in, v_attn_sinks, v_attn_b_f, v_attn_w_out, v_lru_w_in, v_lru_conv_w, v_lru_conv_b, v_lru_w_a, v_lru_b_a, v_lru_w_x, v_lru_b_x, v_lru_lambda, v_lru_w_out, v_final_g):
    nseq, seq, _ = x.shape
    t_tok = nseq * seq
    me = 4 * lax.axis_index("x") + 2 * lax.axis_index("y") + lax.axis_index("c")
    x0 = x.reshape(t_tok, D_MODEL)
    target = loss_target.reshape(t_tok, D_MODEL)

    w_in_pad = jnp.pad(attn_w_in[0].astype(BF16), ((0, 0), (0, SHARD_W_PAD - SHARD_W_IN)))
    vec_shard = jnp.concatenate([lru_conv_w[0], lru_conv_b, lru_b_a, lru_b_x, lru_lambda], axis=0)
    g_w_in, g_vec, g_c = _exchange("gather_first", [w_in_pad, vec_shard, c], [])
    later_w = [attn_w_out[0].astype(BF16), lru_w_in[0].astype(BF16), lru_w_out[0].astype(BF16)]
    later_handle, later_token = _exchange_start("gather_later_start", later_w, [], after=g_vec)
    w_full = jnp.transpose(g_w_in[:, :, :SHARD_W_IN], (1, 0, 2)).reshape(D_MODEL, N_DEV * SHARD_W_IN)
    w_aq, w_ak, w_av = w_full[:, 0:512], w_full[:, 512:640], w_full[:, 640:768]
    w_bq, w_bk, w_bv = w_full[:, 768:1280], w_full[:, 1280:1792], w_full[:, 1792:2304]
    w_f, w_gate = w_full[:, 2304:2312], w_full[:, 2312:3336]
    w_main = jnp.concatenate([w_bq, w_bk, w_bv, w_aq, w_gate, w_ak, w_av], axis=1)
    wf_t = jnp.transpose(w_f)
    vec_full = jnp.transpose(g_vec, (1, 0, 2)).reshape(8, D_MODEL)
    conv_w, conv_b, b_a, b_x, lam = vec_full[0:4], vec_full[4:5], vec_full[5:6], vec_full[6:7], vec_full[7:8]
    c_all = g_c.reshape(N_DEV * nseq, D_MODEL)

    ncol = ada_w.shape[2]
    ada_b_slice = lax.dynamic_slice(ada_b.reshape(2, N_DEV, ncol), (0, me, 0), (2, 1, ncol))
    mod_part = _ada_mod(c_all, ada_w, ada_b_slice)
    (g_mod,) = _exchange("gather_mod", [mod_part], [])
    mine = lax.dynamic_slice(g_mod, (0, 0, me * nseq, 0), (N_DEV, 2, nseq, ncol))
    mod = jnp.transpose(mine, (1, 2, 0, 3)).reshape(2, nseq, 3 * D_MODEL)
    shift = [mod[l, :, 0:D_MODEL].reshape(nseq, 1, D_MODEL) for l in range(2)]
    scale = [mod[l, :, D_MODEL:2 * D_MODEL].reshape(nseq, 1, D_MODEL) for l in range(2)]
    gmod = [mod[l, :, 2 * D_MODEL:].reshape(nseq, 1, D_MODEL) for l in range(2)]

    onehot = _bucket_onehot()
    bias = _bias_expand(jnp.transpose(rel_bias), onehot).reshape(N_HEADS, BLOCK, 2 * BLOCK)
    sinks = attn_sinks.reshape(N_HEADS)
    b_f = attn_b_f.reshape(N_HEADS, 1)
    norm_g0 = norm_g[0:1] + later_token[0:1, 0:1]
    h0, qkvg, fl_t = _norm_proj("norm_proj0", x0, norm_g0, shift[0], scale[0], w_main, seq, BF16, wf_t=wf_t)
    f_row, f_col = _fox_prep(fl_t, b_f, seq)
    a_out, lse_a = _swa_fwd(qkvg, bias, sinks, seq)
    q_aug, k_aug, kt_aug, vt = _fox_aug(qkvg, f_col, seq)
    b_out, lse_b = _fox_fwd_t(q_aug, k_aug, vt, seq)
    g_later = _exchange_wait("gather_later_wait", later_handle, after=lse_b)
    w_out0, g_lru_in, w_out1 = (_with_own(g, w, me) for g, w in zip(g_later, later_w))
    w_out0, w_out1 = w_out0.reshape(D_MODEL, D_MODEL), w_out1.reshape(D_MODEL, D_MODEL)
    w_out0_t, w_out1_t, w_main_t = jnp.transpose(w_out0), jnp.transpose(w_out1), jnp.transpose(w_main)
    lru_in_t = jnp.transpose(g_lru_in, (0, 2, 1)).reshape(2 * D_MODEL, D_MODEL)
    yg0, y0, x1 = _out_proj("out_proj0", [a_out, b_out], qkvg, C_GATE // D_MODEL, w_out0, x0, gmod[0], seq)

    h1, proj1 = _norm_proj("norm_proj1", x1, norm_g[1:2], shift[1], scale[1], g_lru_in, seq, F32)
    hs = _lru_fwd(proj1, conv_w, conv_b, lru_w_a[0], b_a, lru_w_x[0], b_x, lam, seq)

    yg1, dx2, dy1, dyh, dgm1, loss_rows, dfinal_rows = _last_layer_tail(
        hs, proj1, w_out1, w_out1_t, x1, gmod[1], final_g.reshape(1, D_MODEL), target, seq)

    dproj1, dcw, dvec, dw_a, dw_x = _lru_bwd(proj1, hs, dyh, conv_w, conv_b, lru_w_a[0], b_a, lru_w_x[0], b_x, lam, seq)
    dx1, dss1, dg1 = _norm_bwd("norm1_bwd", [(dproj1, 0)], lru_in_t, x1, norm_g[1:2], scale[1], dx2, seq)
    (p_w_out1,) = _dw("dw_out1", yg1, [dy1])
    (p_lru_in,) = _dw("dw_lru_in", h1, [dproj1], blocked=2 * D_MODEL // N_DEV)

    rows_out = D_MODEL // N_DEV
    gpack1, gmeta1 = _pack_rows([dcw[0:4], dvec[0:4], dg1[0], dfinal_rows[0]])
    dwax = jnp.stack([dw_a, dw_x]).astype(BF16)
    own1 = [gpack1, dwax, p_lru_in, p_w_out1.reshape(N_DEV, rows_out, D_MODEL)]
    grads1_handle, grads1_token = _exchange_start("grads1_start", own1[:2], own1[2:], after=dx1)

    gmod0 = gmod[0] + grads1_token[0:1, 0:1]
    dy0, dgm0, du_a, du_b, dgate = _out_proj_bwd("out_proj0_bwd", dx1, gmod0, y0, w_out0_t, seq,
                                                  attn=(a_out, b_out, qkvg))
    dq_a, dkv_a, dbias, dsink = _swa_bwd(qkvg, du_a, a_out, lse_a, bias, sinks, seq)
    dq_b, dk_b, dv_b, df4 = _fox_bwd_t(q_aug, k_aug, kt_aug, qkvg, du_b, b_out, lse_b, seq)
    dfl_t, db_f = _fox_post(df4.reshape(N_HEADS, t_tok), fl_t, b_f, seq)
    parts0 = [(dq_b, C_BQ), (dk_b, C_BK), (dv_b, C_BV), (dq_a, C_AQ), (dgate, C_GATE), (dkv_a, C_AK)]
    (p_w_out0,) = _dw("dw_out0", yg0, [dy0])
    pw_bq, pw_bk, pw_bv, pw_aq, pw_gate, pw_akv = _dw("dw_attn_in", h0, [p for p, _ in parts0])
    pw_f = _dw_rows("dw_f", dfl_t, h0)

    p_w_in = jnp.concatenate([pw_aq, pw_akv, pw_bq, pw_bk, pw_bv, jnp.transpose(pw_f).astype(BF16), pw_gate], axis=1)
    p_w_in = jnp.transpose(p_w_in.reshape(D_MODEL, N_DEV, SHARD_W_IN), (1, 0, 2))
    p_w_in = jnp.pad(p_w_in, ((0, 0), (0, 0), (0, SHARD_W_PAD - SHARD_W_IN)))
    own0 = [p_w_in, p_w_out0.reshape(N_DEV, rows_out, D_MODEL)]
    landed1 = _exchange_wait("grads1_wait", grads1_handle, after=p_w_in)
    grads0_handle, grads0_token = _exchange_start("grads0_start", [], own0, after=landed1[0])
    scale0 = scale[0] + grads0_token[0:1, 0:1]
    dx0, dss0, dg0 = _norm_bwd("norm0_bwd", parts0, w_main_t, x0, norm_g[0:1], scale0, dx1, seq,
                               rows_part=(dfl_t, wf_t))
    dbias_t = _bias_reduce(dbias.reshape(N_HEADS, BLOCK * 2 * BLOCK), onehot)

    gpack0, gmeta0 = _pack_rows([jnp.transpose(dbias_t), dg0[0], dsink[:, 0], db_f[:, 0], loss_rows[0]])
    dmod = jnp.stack([jnp.concatenate([dss[:, 0], dss[:, 1], dgm[:, 0]], axis=1)
                      for dss, dgm in ((dss0, dgm0), (dss1, dgm1))], axis=1)
    g_small0, g_dmod = _exchange("exchange_small", [gpack0, dmod], [])
    landed0 = _exchange_wait("grads0_wait", grads0_handle, after=g_small0)
    r_w_in, r_w_out0 = (_with_own(g, lax.dynamic_index_in_dim(a, me, 0, keepdims=False), me)
                        for g, a in zip(landed0, own0))
    g_small1, g_dwax = (_with_own(g, a, me) for g, a in zip(landed1[:2], own1[:2]))
    r_lru_in, r_w_out1 = (_with_own(g, lax.dynamic_index_in_dim(a, me, 0, keepdims=False), me)
                          for g, a in zip(landed1[2:], own1[2:]))

    d_rel, d_g0, d_sinks, d_b_f, loss_cols = _unpack_rows(_sum_leading("sum_small0", g_small0), gmeta0)
    loss = jnp.sum(loss_cols)
    d_cw, d_vec, d_g1, d_final_g = _unpack_rows(_sum_leading("sum_small1", g_small1), gmeta1)
    d_norm_g = jnp.stack([d_g0, d_g1])
    d_wax = _sum_leading("sum_dwax", g_dwax.reshape(N_DEV, 2 * LRU_BLOCKS * LRU_BLOCK_W, LRU_BLOCK_W))
    d_wa, d_wx = d_wax[:LRU_BLOCKS * LRU_BLOCK_W], d_wax[LRU_BLOCKS * LRU_BLOCK_W:]
    cols = lambda a: lax.dynamic_slice(a, (0, me * LRU_BLOCK_W), (a.shape[0], LRU_BLOCK_W))
    dmod_all = g_dmod.reshape(N_DEV * nseq, 2 * 3 * D_MODEL)
    d_ada_b = _sum_leading("sum_ada_b", dmod_all.reshape(N_DEV * nseq, 2 * 3 * D_MODEL // 128, 128)).reshape(2, 3 * D_MODEL)
    dmod_slice = lax.dynamic_slice(dmod_all.reshape(N_DEV * nseq, 2, N_DEV, ncol), (0, 0, me, 0),
                                   (N_DEV * nseq, 2, 1, ncol)).reshape(N_DEV * nseq, 2, ncol)
    d_ada_w = _ada_w_grad(c_all, jnp.transpose(dmod_slice, (1, 0, 2)))

    given = dict(
        rel_bias=(rel_bias, m_rel_bias, v_rel_bias), norm_g=(norm_g, m_norm_g, v_norm_g),
        ada_w=(ada_w, m_ada_w, v_ada_w), ada_b=(ada_b, m_ada_b, v_ada_b),
        attn_w_in=(attn_w_in, m_attn_w_in, v_attn_w_in), attn_sinks=(attn_sinks, m_attn_sinks, v_attn_sinks),
        attn_b_f=(attn_b_f, m_attn_b_f, v_attn_b_f), attn_w_out=(attn_w_out, m_attn_w_out, v_attn_w_out),
        lru_w_in=(lru_w_in, m_lru_w_in, v_lru_w_in), lru_conv_w=(lru_conv_w, m_lru_conv_w, v_lru_conv_w),
        lru_conv_b=(lru_conv_b, m_lru_conv_b, v_lru_conv_b), lru_w_a=(lru_w_a, m_lru_w_a, v_lru_w_a),
        lru_b_a=(lru_b_a, m_lru_b_a, v_lru_b_a), lru_w_x=(lru_w_x, m_lru_w_x, v_lru_w_x),
        lru_b_x=(lru_b_x, m_lru_b_x, v_lru_b_x), lru_lambda=(lru_lambda, m_lru_lambda, v_lru_lambda),
        lru_w_out=(lru_w_out, m_lru_w_out, v_lru_w_out), final_g=(final_g, m_final_g, v_final_g))
    results = {}

    def big(name, shape2d, g=None, parts=None):
        w, m, v = (a.reshape(shape2d) for a in given[name])
        outs = _adamw("adamw_" + name, w, m, v, g=g, parts=parts)
        results[name] = tuple(o.reshape(given[name][0].shape) for o in outs)

    big("ada_w", (2 * D_MODEL, ncol), g=d_ada_w.reshape(2 * D_MODEL, ncol))
    big("attn_w_in", (D_MODEL, SHARD_W_IN), parts=r_w_in)
    big("attn_w_out", (rows_out, D_MODEL), parts=r_w_out0)
    big("lru_w_in", (D_MODEL, 2 * D_MODEL // N_DEV), parts=r_lru_in)
    big("lru_w_out", (rows_out, D_MODEL), parts=r_w_out1)

    small_grads = dict(
        rel_bias=d_rel, norm_g=d_norm_g, ada_b=d_ada_b, attn_sinks=d_sinks.reshape(1, N_HEADS),
        attn_b_f=d_b_f.reshape(1, N_HEADS), lru_conv_w=cols(d_cw).reshape(1, 4, LRU_BLOCK_W),
        lru_conv_b=cols(d_vec[0:1]), lru_w_a=d_wa.reshape(lru_w_a.shape), lru_b_a=cols(d_vec[1:2]),
        lru_w_x=d_wx.reshape(lru_w_x.shape), lru_b_x=cols(d_vec[2:3]), lru_lambda=cols(d_vec[3:4]),
        final_g=d_final_g)
    small = [n for n in WEIGHTS if n not in BIG]
    as2d = lambda a: a.reshape(-1, a.shape[-1])
    outs = _adamw_many("adamw_small", [tuple(as2d(a) for a in given[n]) + (as2d(small_grads[n]),) for n in small])
    for n, group in zip(small, outs):
        results[n] = tuple(o.reshape(given[n][0].shape) for o in group)

    grad_x = dx0.reshape(x.shape)
    out = [loss, grad_x]
    for j in range(4):
        out += [results[n][j] for n in WEIGHTS]
    return tuple(out)
```

```python
import functools
import math

import jax
import jax.numpy as jnp
from jax import lax
from jax.experimental import pallas as pl
from jax.experimental.pallas import tpu as pltpu

F32 = jnp.float32
BF16 = jnp.bfloat16
HI = lax.Precision.HIGHEST
MESH = pl.DeviceIdType.MESH

N_DEV = 8
D_MODEL = 1024
HEAD_DIM = 64
N_HEADS = 8
KV_GROUP = 4
BLOCK = 128
REL_BUCKETS = 32
REL_MAX_EXACT = 16
REL_MAX_DIST = 128
LRU_BLOCKS = 8
LRU_BLOCK_W = 128
LRU_C = 8.0
EPS = 1e-6
SCALE = HEAD_DIM ** -0.5
NEG = -1e30

ADAM_LR = 0.001
ADAM_B1 = 0.9
ADAM_B2 = 0.999
ADAM_EPS = 1e-08
ADAM_WD = 0.01
ADAM_STEP = 10

C_BQ, C_BK, C_BV, C_AQ, C_GATE, C_AK, C_AV = 0, 512, 1024, 1536, 2048, 3072, 3200
N_MAIN = 3328
SHARD_W_IN = 417
SHARD_W_PAD = 512

TM = 512
TQ = 256
TK = 128
TKB = 256
TC = 512
SWA_SUB = 2
VMEM_BIG = 56 * 1024 * 1024
VMEM_MID = 40 * 1024 * 1024


def _pallas(body, **kw):
    return pl.pallas_call(body, **kw)


def _cp(sem=None, vmem=None):
    kw = {}
    if sem is not None:
        kw["dimension_semantics"] = sem
    if vmem is not None:
        kw["vmem_limit_bytes"] = vmem
    return pltpu.CompilerParams(**kw)


def _nn(a, b, precision=None):
    return jnp.dot(a, b, preferred_element_type=F32, precision=precision)


def _nt(a, b, precision=None):
    return lax.dot_general(a, b, (((1,), (1,)), ((), ())), preferred_element_type=F32, precision=precision)


def _tn(a, b, precision=None):
    return lax.dot_general(a, b, (((0,), (0,)), ((), ())), preferred_element_type=F32, precision=precision)


def _sigmoid(x):
    return 1.0 / (1.0 + jnp.exp(-x))


def _silu(x):
    return x * _sigmoid(x)


def _dsilu(x):
    s = _sigmoid(x)
    return s * (1.0 + x * (1.0 - s))


def _col(tile, idx):
    lane = lax.broadcasted_iota(jnp.int32, tile.shape, 1)
    return jnp.sum(jnp.where(lane == idx, tile, 0.0), axis=1, keepdims=True)


def _exchange(name, gathers, scatters, axes=("x", "y", "c"), chunks=1):
    ng, n = len(gathers), len(gathers) + len(scatters)
    ins = list(gathers) + list(scatters)
    group = 2 ** len(axes)

    def body(*refs):
        in_refs, out_refs = refs[:n], refs[n:2 * n]
        send_sems, recv_sems, loc_sems = refs[2 * n:]
        coord = {a: lax.axis_index(a) for a in ("x", "y", "c")}

        def member(r):
            pc = dict(coord)
            idx = 0
            for k, a in enumerate(axes):
                if r & (1 << (len(axes) - 1 - k)):
                    pc[a] = 1 - coord[a]
                idx = 2 * idx + pc[a]
            return (pc["x"], pc["y"], pc["c"]), idx

        _, me = member(0)

        def peer(r):
            return member(r)

        local, sends, recvs = [], [], []
        for k in range(n):
            mine = in_refs[k] if k < ng else in_refs[k].at[me]
            cp = pltpu.make_async_copy(mine, out_refs[k].at[me], loc_sems.at[k])
            cp.start()
            local.append(cp)
            lead = mine.shape[0]
            nchunk = max(q for q in range(1, chunks + 1) if lead % q == 0)
            step = lead // nchunk
            for r in range(1, group):
                pid, pidx = peer(r)
                src = in_refs[k] if k < ng else in_refs[k].at[pidx]
                for q in range(nchunk):
                    rows = pl.ds(q * step, step)
                    sems = dict(send_sem=send_sems.at[r - 1, k, q], recv_sem=recv_sems.at[r - 1, k, q],
                                device_id=pid, device_id_type=MESH)
                    snd = pltpu.make_async_remote_copy(src_ref=src.at[rows], dst_ref=out_refs[k].at[me].at[rows], **sems)
                    snd.start()
                    sends.append(snd)
                    recvs.append(pltpu.make_async_remote_copy(
                        src_ref=src.at[rows], dst_ref=out_refs[k].at[pidx].at[rows], **sems))
        for rc in recvs:
            rc.wait_recv()
        for snd in sends:
            snd.wait_send()
        for cp in local:
            cp.wait()

    out_shape = [jax.ShapeDtypeStruct((group,) + a.shape, a.dtype) for a in gathers]
    out_shape += [jax.ShapeDtypeStruct(a.shape, a.dtype) for a in scatters]
    any_spec = pl.BlockSpec(memory_space=pl.ANY)
    return _pallas(
        body, name=name, out_shape=out_shape,
        in_specs=[any_spec] * n, out_specs=[any_spec] * n,
        scratch_shapes=[pltpu.SemaphoreType.DMA((group - 1, n, chunks)), pltpu.SemaphoreType.DMA((group - 1, n, chunks)),
                        pltpu.SemaphoreType.DMA((n,))],
    )(*ins)


def _peer_of(r):
    x, y, c = lax.axis_index("x"), lax.axis_index("y"), lax.axis_index("c")
    px = 1 - x if r & 4 else x
    py = 1 - y if r & 2 else y
    pc = 1 - c if r & 1 else c
    return (px, py, pc), 4 * px + 2 * py + pc


def _split_copies(in_refs, land_refs, send_sems, recv_sems, ng, with_recv):
    _, me = _peer_of(0)
    pairs = []
    for k, (src_ref, land) in enumerate(zip(in_refs, land_refs)):
        for r in range(1, N_DEV):
            pid, pidx = _peer_of(r)
            src = src_ref if k < ng else src_ref.at[pidx]
            slot = (N_DEV - 1) * k + r - 1
            sems = dict(send_sem=send_sems.at[slot], recv_sem=recv_sems.at[slot], device_id=pid, device_id_type=MESH)
            send = pltpu.make_async_remote_copy(src_ref=src, dst_ref=land.at[me], **sems)
            recv = pltpu.make_async_remote_copy(src_ref=src, dst_ref=land.at[pidx], **sems) if with_recv else None
            pairs.append((send, recv))
    return pairs


def _exchange_start(name, gathers, scatters, after):
    ng, n = len(gathers), len(gathers) + len(scatters)
    ins = list(gathers) + list(scatters)
    lands = [jax.ShapeDtypeStruct((N_DEV,) + a.shape, a.dtype) for a in gathers]
    lands += [jax.ShapeDtypeStruct(a.shape, a.dtype) for a in scatters]

    def body(*refs):
        in_refs, land_refs = refs[:n], refs[n:2 * n]
        send_sems, recv_sems = refs[2 * n + 1:2 * n + 3]
        token = refs[-1]
        for send, _ in _split_copies(in_refs, land_refs, send_sems, recv_sems, ng, False):
            send.start()
        token[...] = jnp.zeros_like(token)

    hbm = pl.BlockSpec(memory_space=pltpu.HBM)
    sem = pl.BlockSpec(memory_space=pltpu.SEMAPHORE)
    sem_shape = pltpu.SemaphoreType.DMA(((N_DEV - 1) * n,))
    out_shape = [sem_shape, sem_shape] + [pltpu.HBM(a.shape, a.dtype) for a in ins]
    out_shape += [pltpu.HBM(l.shape, l.dtype) for l in lands] + [jax.ShapeDtypeStruct((8, 128), F32)]
    args = [pltpu.with_memory_space_constraint(a, pltpu.HBM) for a in ins]
    args += [pltpu.with_memory_space_constraint(lax.empty(l.shape, l.dtype), pltpu.HBM) for l in lands]
    outs = _pallas(
        body, name=name, out_shape=out_shape,
        in_specs=[hbm] * (2 * n) + [pl.BlockSpec(memory_space=pl.ANY)],
        out_specs=[sem, sem] + [hbm] * (2 * n) + [pl.BlockSpec(memory_space=pltpu.VMEM)],
        input_output_aliases={i: 2 + i for i in range(2 * n)},
        compiler_params=pltpu.CompilerParams(has_side_effects=pltpu.SideEffectType.DATAFLOW_SIDE_EFFECTING),
    )(*args, after)
    return (outs[0], outs[1], list(outs[2:2 + n]), list(outs[2 + n:2 + 2 * n]), ng), outs[-1]


def _exchange_wait(name, handle, after):
    send_sems, recv_sems, srcs, lands, ng = handle
    n = len(srcs)

    def body(*refs):
        in_refs, land_refs = refs[:n], refs[n:2 * n]
        send_ref, recv_ref = refs[2 * n:2 * n + 2]
        for send, recv in _split_copies(in_refs, land_refs, send_ref, recv_ref, ng, True):
            send.wait_send()
            recv.wait_recv()

    hbm = pl.BlockSpec(memory_space=pltpu.HBM)
    sem = pl.BlockSpec(memory_space=pltpu.SEMAPHORE)
    outs = _pallas(
        body, name=name, out_shape=[pltpu.HBM(a.shape, a.dtype) for a in srcs + lands],
        in_specs=[hbm] * (2 * n) + [sem, sem, pl.BlockSpec(memory_space=pl.ANY)],
        out_specs=[hbm] * (2 * n), input_output_aliases={i: i for i in range(2 * n)},
        compiler_params=pltpu.CompilerParams(has_side_effects=pltpu.SideEffectType.DATAFLOW_SIDE_EFFECTING),
    )(*srcs, *lands, send_sems, recv_sems, after)
    return list(outs[n:])


def _with_own(land, own, me):
    return lax.dynamic_update_slice(land, own[None], (me,) + (0,) * own.ndim)


def _ada_mod(c_all, ada_w, ada_b_slice):
    def body(c_ref, w_ref, b_ref, o_ref):
        ca = _silu(c_ref[...])
        for l in range(2):
            o_ref[l] = _nn(ca, w_ref[l], HI) + b_ref[l]

    return _pallas(body, name="ada_mod",
                   out_shape=jax.ShapeDtypeStruct((2, c_all.shape[0], ada_w.shape[2]), F32),
                   compiler_params=_cp(vmem=VMEM_MID))(c_all, ada_w, ada_b_slice)


def _ada_w_grad(c_all, dmod_slice):
    def body(c_ref, d_ref, o_ref):
        ca = _silu(c_ref[...])
        for l in range(2):
            o_ref[l] = _tn(ca, d_ref[l], HI)

    return _pallas(body, name="ada_w_grad",
                   out_shape=jax.ShapeDtypeStruct((2, D_MODEL, dmod_slice.shape[2]), F32),
                   compiler_params=_cp(vmem=VMEM_MID))(c_all, dmod_slice)


def _bucket_onehot():
    qi = jnp.arange(BLOCK)[:, None]
    kj = jnp.arange(2 * BLOCK)[None, :]
    rel = qi - kj + BLOCK
    n = jnp.maximum(rel, 0)
    nf = jnp.maximum(n, 1).astype(F32)
    large = REL_MAX_EXACT + (jnp.log(nf / REL_MAX_EXACT) / math.log(REL_MAX_DIST / REL_MAX_EXACT)
                             * (REL_BUCKETS - REL_MAX_EXACT)).astype(jnp.int32)
    large = jnp.minimum(large, REL_BUCKETS - 1)
    bucket = jnp.where(n < REL_MAX_EXACT, n, large).reshape(1, BLOCK * 2 * BLOCK)
    return (jnp.arange(REL_BUCKETS)[:, None] == bucket).astype(F32)


def _bias_expand(rel_bias_t, onehot):
    def body(r_ref, e_ref, o_ref):
        o_ref[...] = _nn(r_ref[...], e_ref[...], HI)

    return _pallas(body, name="bias_expand",
                   out_shape=jax.ShapeDtypeStruct((N_HEADS, onehot.shape[1]), F32),
                   compiler_params=_cp(vmem=VMEM_MID))(rel_bias_t, onehot)


def _bias_reduce(dbias, onehot):
    def body(d_ref, e_ref, o_ref):
        o_ref[...] = _nt(d_ref[...], e_ref[...], HI)

    return _pallas(body, name="bias_reduce",
                   out_shape=jax.ShapeDtypeStruct((N_HEADS, REL_BUCKETS), F32),
                   compiler_params=_cp(vmem=VMEM_MID))(dbias, onehot)


def _norm_proj(name, x, g, shift, scale, w, seq, out_dtype, wf_t=None):
    t_tok = x.shape[0]
    w3d = w.ndim == 3
    n_out = w.shape[0] * w.shape[2] if w3d else w.shape[1]
    cn = w.shape[2] if w3d else 256

    def body(x_ref, g_ref, sh_ref, sc_ref, w_ref, *rest):
        if wf_t is not None:
            wf_ref, h_ref, o_ref, fl_ref = rest
        else:
            h_ref, o_ref = rest
        xv = x_ref[...]
        rstd = lax.rsqrt(jnp.mean(xv * xv, axis=-1, keepdims=True) + EPS)
        h = (xv * rstd) * g_ref[...] * (1.0 + sc_ref[...]) + sh_ref[...]
        hb = h.astype(BF16)
        h_ref[...] = hb
        for j in range(n_out // cn):
            wj = w_ref[j] if w3d else w_ref[:, j * cn:(j + 1) * cn]
            o_ref[:, j * cn:(j + 1) * cn] = _nn(hb, wj).astype(out_dtype)
        if wf_t is not None:
            fl_ref[...] = _nt(wf_ref[...], hb)

    mod_spec = pl.BlockSpec((None, 1, D_MODEL), lambda i: (i * TM // seq, 0, 0))
    w_spec = (pl.BlockSpec(w.shape, lambda i: (0, 0, 0)) if w3d else pl.BlockSpec(w.shape, lambda i: (0, 0)))
    in_specs = [pl.BlockSpec((TM, D_MODEL), lambda i: (i, 0)), pl.BlockSpec((1, D_MODEL), lambda i: (0, 0)),
                mod_spec, mod_spec, w_spec]
    out_shape = [jax.ShapeDtypeStruct((t_tok, D_MODEL), BF16), jax.ShapeDtypeStruct((t_tok, n_out), out_dtype)]
    out_specs = [pl.BlockSpec((TM, D_MODEL), lambda i: (i, 0)), pl.BlockSpec((TM, n_out), lambda i: (i, 0))]
    args = [x, g, shift, scale, w]
    if wf_t is not None:
        in_specs.append(pl.BlockSpec(wf_t.shape, lambda i: (0, 0)))
        out_shape.append(jax.ShapeDtypeStruct((wf_t.shape[0], t_tok), F32))
        out_specs.append(pl.BlockSpec((wf_t.shape[0], TM), lambda i: (0, i)))
        args.append(wf_t)
    return _pallas(body, name=name, grid=(t_tok // TM,), in_specs=in_specs, out_specs=out_specs,
                   out_shape=out_shape, compiler_params=_cp(("arbitrary",), VMEM_BIG))(*args)


def _fox_prep(fl_t, b_f, seq):
    t_tok = fl_t.shape[1]
    ch = 256

    def body(fl_ref, bf_ref, fr_ref, fc_ref):
        z = fl_ref[...] + bf_ref[...]
        logf = jnp.minimum(z, 0.0) - jnp.log(1.0 + jnp.exp(-jnp.abs(z)))
        ri = lax.broadcasted_iota(jnp.int32, (ch, ch), 0)
        ci = lax.broadcasted_iota(jnp.int32, (ch, ch), 1)
        upper = (ri <= ci).astype(F32)
        eye = (ri == ci).astype(F32)
        carry = jnp.zeros((N_HEADS, 1), F32)
        for k in range(seq // ch):
            fk = _nn(logf[:, k * ch:(k + 1) * ch], upper, HI) + carry
            carry = fk[:, ch - 1:ch]
            fr_ref[:, k * ch:(k + 1) * ch] = fk
            padded = jnp.concatenate([fk, jnp.zeros((128 - N_HEADS, ch), F32)], axis=0)
            fc_ref[k * ch:(k + 1) * ch, :] = _nt(eye, padded, HI)

    return _pallas(
        body, name="fox_prep", grid=(t_tok // seq,),
        in_specs=[pl.BlockSpec((N_HEADS, seq), lambda b: (0, b)), pl.BlockSpec((N_HEADS, 1), lambda b: (0, 0))],
        out_specs=[pl.BlockSpec((N_HEADS, seq), lambda b: (0, b)), pl.BlockSpec((seq, 128), lambda b: (b, 0))],
        out_shape=[jax.ShapeDtypeStruct((N_HEADS, t_tok), F32), jax.ShapeDtypeStruct((t_tok, 128), F32)],
        compiler_params=_cp(("arbitrary",), VMEM_MID))(fl_t, b_f)


def _fox_post(df_row, fl_t, b_f, seq):
    t_tok = fl_t.shape[1]
    ch = 256

    def body(d_ref, fl_ref, bf_ref, o_ref, db_ref):
        @pl.when(pl.program_id(0) == 0)
        def _():
            db_ref[...] = jnp.zeros_like(db_ref)

        z = fl_ref[...] + bf_ref[...]
        sig_neg = 1.0 / (1.0 + jnp.exp(z))
        ri = lax.broadcasted_iota(jnp.int32, (ch, ch), 0)
        ci = lax.broadcasted_iota(jnp.int32, (ch, ch), 1)
        lower = (ri >= ci).astype(F32)
        carry = jnp.zeros((N_HEADS, 1), F32)
        tot = jnp.zeros((N_HEADS, 1), F32)
        for k in reversed(range(seq // ch)):
            dk = _nn(d_ref[:, k * ch:(k + 1) * ch], lower, HI) + carry
            carry = dk[:, 0:1]
            dfl = dk * sig_neg[:, k * ch:(k + 1) * ch]
            o_ref[:, k * ch:(k + 1) * ch] = dfl
            tot = tot + jnp.sum(dfl, axis=1, keepdims=True)
        db_ref[...] += jnp.broadcast_to(tot, db_ref.shape)

    return _pallas(
        body, name="fox_post", grid=(t_tok // seq,),
        in_specs=[pl.BlockSpec((N_HEADS, seq), lambda b: (0, b)), pl.BlockSpec((N_HEADS, seq), lambda b: (0, b)),
                  pl.BlockSpec((N_HEADS, 1), lambda b: (0, 0))],
        out_specs=[pl.BlockSpec((N_HEADS, seq), lambda b: (0, b)), pl.BlockSpec((N_HEADS, 128), lambda b: (0, 0))],
        out_shape=[jax.ShapeDtypeStruct((N_HEADS, t_tok), F32), jax.ShapeDtypeStruct((N_HEADS, 128), F32)],
        compiler_params=_cp(("arbitrary",), VMEM_MID))(df_row, fl_t, b_f)


def _eye(n, dtype):
    return (lax.broadcasted_iota(jnp.int32, (n, n), 0) == lax.broadcasted_iota(jnp.int32, (n, n), 1)).astype(dtype)


def _fox_aug(qkvg, f_col, seq):
    t_tok = qkvg.shape[0]
    ta = 256
    nkb = ta // TK

    def body(q_ref, k_ref, v_ref, fc_ref, qa_ref, ka_ref, kt_ref, vt_ref):
        ri = lax.broadcasted_iota(jnp.int32, (128, 128), 0)
        ci = lax.broadcasted_iota(jnp.int32, (128, 128), 1)
        eye = (ri == ci).astype(BF16)
        lane = lax.broadcasted_iota(jnp.int32, (ta, 128), 1)
        ones_q = jnp.where(jnp.logical_and(lane >= 64, lane < 67), 1.0, 0.0)
        ones_k = jnp.where(jnp.logical_and(lane >= 67, lane < 70), 1.0, 0.0)
        fc_tile = fc_ref[...]
        for p in range(N_HEADS // 2):
            q2 = q_ref[:, 128 * p:128 * (p + 1)]
            k2 = k_ref[:, 128 * p:128 * (p + 1)]
            vt = _nt(eye, v_ref[:, 128 * p:128 * (p + 1)]).astype(BF16)
            for kk in range(nkb):
                vt_ref[p, kk] = vt[:, kk * TK:(kk + 1) * TK]
            for e in range(2):
                h = 2 * p + e
                sel = jnp.logical_and(ri == ci + HEAD_DIM * e, ci < HEAD_DIM)
                f = _col(fc_tile, h)
                fh = f.astype(BF16).astype(F32)
                fm = (f - fh).astype(BF16).astype(F32)
                fl = (f - fh - fm).astype(BF16).astype(F32)
                qa = (_nn(q2, jnp.where(sel, SCALE, 0.0).astype(BF16)) + ones_q + jnp.where(lane == 67, fh, 0.0)
                      + jnp.where(lane == 68, fm, 0.0) + jnp.where(lane == 69, fl, 0.0))
                ka = (_nn(k2, jnp.where(sel, 1.0, 0.0).astype(BF16)) + ones_k - jnp.where(lane == 64, fh, 0.0)
                      - jnp.where(lane == 65, fm, 0.0) - jnp.where(lane == 66, fl, 0.0))
                qa_ref[h] = qa.astype(BF16)
                kab = ka.astype(BF16)
                ka_ref[h] = kab
                kt = _nt(eye, kab).astype(BF16)
                for kk in range(ta // TKB):
                    kt_ref[h, kk] = kt[:, kk * TKB:(kk + 1) * TKB]

    aug = jax.ShapeDtypeStruct((N_HEADS, t_tok, 128), BF16)
    return _pallas(
        body, name="fox_aug", grid=(t_tok // ta,),
        in_specs=[pl.BlockSpec((ta, 512), lambda i: (i, C_BQ // 512)), pl.BlockSpec((ta, 512), lambda i: (i, C_BK // 512)),
                  pl.BlockSpec((ta, 512), lambda i: (i, C_BV // 512)), pl.BlockSpec((ta, 128), lambda i: (i, 0))],
        out_specs=[pl.BlockSpec((N_HEADS, ta, 128), lambda i: (0, i, 0)), pl.BlockSpec((N_HEADS, ta, 128), lambda i: (0, i, 0)),
                   pl.BlockSpec((N_HEADS, ta // TKB, 128, TKB), lambda i: (0, i, 0, 0)),
                   pl.BlockSpec((N_HEADS // 2, nkb, 128, TK), lambda i: (0, i, 0, 0))],
        out_shape=[aug, aug, jax.ShapeDtypeStruct((N_HEADS, t_tok // TKB, 128, TKB), BF16),
                   jax.ShapeDtypeStruct((N_HEADS // 2, t_tok // TK, 128, TK), BF16)],
        compiler_params=_cp(("arbitrary",), VMEM_MID))(qkvg, qkvg, qkvg, f_col)


def _fox_fwd_t(q_aug, k_aug, vt, seq):
    t_tok = k_aug.shape[1]
    nq = seq // TQ
    ratio = TQ // TK
    assert ratio == 2, "the two pipeline slots are addressed by the key block's parity"

    def body(qa_ref, ka_ref, vt_ref, o_ref, lse_ref, ml_s, acc_s, st_s, p_s, al_s, qt_s):
        i = pl.program_id(1)
        tpos = i * TQ + lax.broadcasted_iota(jnp.int32, (1, TQ), 1)
        eye = _eye(HEAD_DIM, BF16)
        eye2 = _eye(128, BF16)
        for h in range(N_HEADS):
            qt_s[h] = _nt(eye2, qa_ref[h]).astype(BF16)
            ml_s[0, h] = jnp.full((1, TQ), NEG, F32)
            ml_s[1, h] = jnp.zeros((1, TQ), F32)
            acc_s[h] = jnp.zeros((HEAD_DIM, TQ), F32)
            p_s[1, h] = jnp.zeros((TK, TQ), BF16)
            al_s[1, h] = jnp.ones((1, TQ), F32)

        def scores(j, slot):
            row0 = pl.multiple_of(j * TK, TK)
            for h in range(N_HEADS):
                st_s[slot, h] = _nn(ka_ref[h, pl.ds(row0, TK), :], qt_s[h])

        def softmax(j, slot, masked):
            if masked:
                keep = (j * TK + lax.broadcasted_iota(jnp.int32, (TK, 1), 0)) <= tpos
            for h in range(N_HEADS):
                st = st_s[slot, h]
                if masked:
                    st = jnp.where(keep, st, NEG)
                m = ml_s[0, h]
                m_new = jnp.maximum(m, jnp.max(st, axis=0, keepdims=True))
                alpha = jnp.exp(m - m_new)
                pe = jnp.exp(st - m_new)
                ml_s[0, h] = m_new
                ml_s[1, h] = alpha * ml_s[1, h] + jnp.sum(pe, axis=0, keepdims=True)
                al_s[slot, h] = alpha
                p_s[slot, h] = pe.astype(BF16)

        def values(j, slot):
            jv = jnp.maximum(j, 0)
            for h in range(N_HEADS):
                p, e = divmod(h, 2)
                acc_s[h] = al_s[slot, h] * acc_s[h] + _nn(vt_ref[p, jv, e * HEAD_DIM:(e + 1) * HEAD_DIM, :], p_s[slot, h])

        def step(m, carry):
            for kk in range(ratio):
                j = ratio * m + kk
                values(j - 1, 1 - kk)
                softmax(j, kk, False)
                scores(j + 1, 1 - kk)
            return carry

        scores(0, 0)
        lax.fori_loop(0, i, step, 0)
        for kk in range(ratio):
            j = ratio * i + kk
            values(j - 1, 1 - kk)
            softmax(j, kk, True)
            if kk < ratio - 1:
                scores(j + 1, 1 - kk)
        values(ratio * i + ratio - 1, ratio - 1)
        for p in range(N_HEADS // 2):
            outs = []
            for e in range(2):
                h = 2 * p + e
                l = ml_s[1, h]
                outs.append(_tn((acc_s[h] / l).astype(BF16), eye))
                lse_ref[p, e:e + 1, :] = ml_s[0, h] + jnp.log(l)
            o_ref[:, 128 * p:128 * (p + 1)] = jnp.concatenate(outs, axis=1).astype(BF16)

    return _pallas(
        body, name="fox_fwd", grid=(t_tok // seq, nq),
        in_specs=[pl.BlockSpec((N_HEADS, TQ, 128), lambda b, i: (0, b * nq + i, 0)),
                  pl.BlockSpec((N_HEADS, seq, 128), lambda b, i: (0, b, 0)),
                  pl.BlockSpec((N_HEADS // 2, seq // TK, 128, TK), lambda b, i: (0, b, 0, 0))],
        out_specs=[pl.BlockSpec((TQ, 512), lambda b, i: (b * nq + i, 0)),
                   pl.BlockSpec((N_HEADS // 2, 2, TQ), lambda b, i: (0, 0, b * nq + i))],
        out_shape=[jax.ShapeDtypeStruct((t_tok, 512), BF16), jax.ShapeDtypeStruct((N_HEADS // 2, 2, t_tok), F32)],
        scratch_shapes=[pltpu.VMEM((2, N_HEADS, 1, TQ), F32), pltpu.VMEM((N_HEADS, HEAD_DIM, TQ), F32),
                        pltpu.VMEM((2, N_HEADS, TK, TQ), F32), pltpu.VMEM((2, N_HEADS, TK, TQ), BF16),
                        pltpu.VMEM((2, N_HEADS, 1, TQ), F32), pltpu.VMEM((N_HEADS, 128, TQ), BF16)],
        compiler_params=_cp(("arbitrary", "arbitrary"), VMEM_MID))(q_aug, k_aug, vt)


def _fox_bwd_t(q_aug, k_aug, kt, qkvg, du_b, b_out, lse, seq):
    TK = TKB
    t_tok = qkvg.shape[0]
    nq = seq // TQ
    nkb = seq // TK
    ratio = TQ // TK
    hg = 4

    def body(qa_ref, ka_ref, kt_ref, v_ref, do_ref, o_ref, lse_ref, dq_ref, dk_ref, dv_ref, df_ref,
             dqt_s, row_s, dfk_s, dk_s, dv_s, dot_s, st_s, dp_s, pb_s, db_s, qt_s):
        eye = _eye(HEAD_DIM, BF16)
        eye2 = _eye(128, BF16)
        eye_k = _eye(TK, F32)
        lane8 = lax.broadcasted_iota(jnp.int32, (8, 128), 1)
        lane_k = lax.broadcasted_iota(jnp.int32, (TK, 128), 1)
        first = [lane8 < HEAD_DIM, lane8 >= HEAD_DIM]
        for pp in range(hg // 2):
            for ii in range(nq):
                dot_s[pp, ii] = _nt(eye2, do_ref[ii * TQ:(ii + 1) * TQ, 128 * pp:128 * (pp + 1)]).astype(BF16)
        for hh in range(hg):
            for ii in range(nq):
                qt_s[hh, ii] = _nt(eye2, qa_ref[hh, ii * TQ:(ii + 1) * TQ, :]).astype(BF16)
        for hh in range(hg):
            pp, e = divmod(hh, 2)
            head_lanes = jnp.where(first[e], 1.0, 0.0)
            for ii in range(nq):
                rows = slice(ii * TQ, (ii + 1) * TQ)
                prod = do_ref[rows, 128 * pp:128 * (pp + 1)].astype(F32) * o_ref[rows, 128 * pp:128 * (pp + 1)].astype(F32)
                row_s[hh, ii, 0] = _nt(head_lanes, prod, HI)
                row_s[hh, ii, 1] = jnp.broadcast_to(lse_ref[pp, e:e + 1, ii * TQ:(ii + 1) * TQ], (8, TQ))
                dqt_s[hh, ii] = jnp.zeros((128, TQ), F32)

        def kblock(j, _):
            krow = pl.multiple_of(j * TK, TK)
            spos = j * TK + lax.broadcasted_iota(jnp.int32, (TK, 1), 0)
            for hh in range(hg):
                dk_s[hh] = jnp.zeros((TK, 128), F32)
                dv_s[hh] = jnp.zeros((TK, 128), F32)

            def scores(i, slot):
                for hh in range(hg):
                    pp, e = divmod(hh, 2)
                    own = (lane_k < HEAD_DIM) if e == 0 else (lane_k >= HEAD_DIM)
                    v2 = v_ref[pl.ds(krow, TK), 128 * pp:128 * (pp + 1)]
                    vj = jnp.where(own, v2, jnp.zeros_like(v2))
                    st_s[slot, hh] = _nn(ka_ref[hh, pl.ds(krow, TK), :], qt_s[hh, i])
                    dp_s[slot, hh] = _nn(vj, dot_s[pp, i])

            def elementwise(i, slot, masked):
                if masked:
                    keep = spos <= (i * TQ + lax.broadcasted_iota(jnp.int32, (1, TQ), 1))
                for hh in range(hg):
                    pt = jnp.exp(st_s[slot, hh] - row_s[hh, i, 1][0:1, :])
                    if masked:
                        pt = jnp.where(keep, pt, 0.0)
                    dst = pt * (dp_s[slot, hh] - row_s[hh, i, 0][0:1, :])
                    pb_s[slot, hh] = pt.astype(BF16)
                    db_s[slot, hh] = dst.astype(BF16)

            def grads(i, slot):
                qrow = pl.multiple_of(i * TQ, TQ)
                for hh in range(hg):
                    dst_b = db_s[slot, hh]
                    dv_s[hh] += _nn(pb_s[slot, hh], do_ref[pl.ds(qrow, TQ), 128 * (hh // 2):128 * (hh // 2 + 1)])
                    dk_s[hh] += _nn(dst_b, qa_ref[hh, pl.ds(qrow, TQ), :])
                    dqt_s[hh, i] += _nn(kt_ref[hh, j], dst_b)

            def step(p, carry):
                i = i0 + 2 * p + 1
                grads(i - 1, 0)
                elementwise(i, 1, False)
                scores(i + 1, 0)
                grads(i, 1)
                elementwise(i + 1, 0, False)
                scores(jnp.minimum(i + 2, nq - 1), 1)
                return carry

            i0 = j // ratio
            rest = nq - 1 - i0
            scores(i0, 0)
            elementwise(i0, 0, True)
            scores(jnp.minimum(i0 + 1, nq - 1), 1)
            lax.fori_loop(0, rest // 2, step, 0)

            @pl.when(rest % 2 == 1)
            def _():
                grads(nq - 2, 0)
                elementwise(nq - 1, 1, False)
                grads(nq - 1, 1)

            @pl.when(rest % 2 == 0)
            def _():
                grads(nq - 1, 0)
            for pp in range(hg // 2):
                cols = slice(128 * pp, 128 * (pp + 1))
                dk_ref[pl.ds(krow, TK), cols] = jnp.concatenate(
                    [dk_s[2 * pp][:, :HEAD_DIM], dk_s[2 * pp + 1][:, :HEAD_DIM]], axis=1).astype(BF16)
                dv_ref[pl.ds(krow, TK), cols] = jnp.where(lane_k < HEAD_DIM, dv_s[2 * pp], dv_s[2 * pp + 1]).astype(BF16)
            for hh in range(hg):
                dfk_s[hh, j] = _tn(dk_s[hh][:, HEAD_DIM:HEAD_DIM + 8], eye_k, HI)
            return 0

        lax.fori_loop(0, nkb, kblock, 0)
        for pp in range(hg // 2):
            for ii in range(nq):
                parts = []
                for e in range(2):
                    dqt = dqt_s[2 * pp + e, ii]
                    parts.append(_tn(dqt[0:HEAD_DIM, :].astype(BF16), eye) * SCALE)
                    for kk in range(ratio):
                        jj = ii * ratio + kk
                        df_ref[pp, e:e + 1, jj * TK:(jj + 1) * TK] = (dqt[67:68, kk * TK:(kk + 1) * TK]
                                                                     - dfk_s[2 * pp + e, jj][0:1, :])
                dq_ref[ii * TQ:(ii + 1) * TQ, 128 * pp:128 * (pp + 1)] = jnp.concatenate(parts, axis=1).astype(BF16)

    aug_blk = pl.BlockSpec((hg, seq, 128), lambda b, g: (g, b, 0))
    pair_blk = pl.BlockSpec((seq, 64 * hg), lambda b, g: (b, g))
    row_blk = pl.BlockSpec((hg // 2, 2, seq), lambda b, g: (g, 0, b))
    return _pallas(
        body, name="fox_bwd", grid=(t_tok // seq, N_HEADS // hg),
        in_specs=[aug_blk, aug_blk, pl.BlockSpec((hg, nkb, 128, TK), lambda b, g: (g, b, 0, 0)),
                  pl.BlockSpec((seq, 64 * hg), lambda b, g: (b, C_BV // (64 * hg) + g)), pair_blk, pair_blk, row_blk],
        out_specs=[pair_blk, pair_blk, pair_blk, row_blk],
        out_shape=[jax.ShapeDtypeStruct((t_tok, 512), BF16)] * 3
        + [jax.ShapeDtypeStruct((N_HEADS // 2, 2, t_tok), F32)],
        scratch_shapes=[pltpu.VMEM((hg, nq, 128, TQ), F32), pltpu.VMEM((hg, nq, 2, 8, TQ), F32),
                        pltpu.VMEM((hg, nkb, 8, TK), F32), pltpu.VMEM((hg, TK, 128), F32),
                        pltpu.VMEM((hg, TK, 128), F32), pltpu.VMEM((hg // 2, nq, 128, TQ), BF16),
                        pltpu.VMEM((2, hg, TK, TQ), F32), pltpu.VMEM((2, hg, TK, TQ), F32),
                        pltpu.VMEM((2, hg, TK, TQ), BF16), pltpu.VMEM((2, hg, TK, TQ), BF16),
                        pltpu.VMEM((hg, nq, 128, TQ), BF16)],
        compiler_params=_cp(("arbitrary", "arbitrary"), VMEM_BIG))(q_aug, k_aug, kt, qkvg, du_b, b_out, lse)


def _swa_window(k_ref, v_ref, n):
    prev = pl.multiple_of(jnp.maximum(n - 1, 0) * BLOCK, BLOCK)
    cur = pl.multiple_of(n * BLOCK, BLOCK)
    kwin = jnp.concatenate([k_ref[pl.ds(prev, BLOCK), :], k_ref[pl.ds(cur, BLOCK), :]], axis=0)
    vwin = jnp.concatenate([v_ref[pl.ds(prev, BLOCK), :], v_ref[pl.ds(cur, BLOCK), :]], axis=0)
    ti = lax.broadcasted_iota(jnp.int32, (BLOCK, 2 * BLOCK), 0)
    sj = lax.broadcasted_iota(jnp.int32, (BLOCK, 2 * BLOCK), 1)
    rel = ti - sj + BLOCK
    first_key = jnp.where(n > 0, 0, BLOCK)
    mask = jnp.logical_and(jnp.logical_and(rel >= 0, rel < BLOCK), sj >= first_key)
    return kwin, vwin, mask, prev, cur


def _head_cols(ref, h):
    pair = ref[:, 128 * (h // 2):128 * (h // 2 + 1)]
    return pair[:, (h % 2) * HEAD_DIM:(h % 2 + 1) * HEAD_DIM]


def _swa_logits(q_ref, kwin, bias_ref, h, mask):
    hk = h // KV_GROUP
    s = _nt(_head_cols(q_ref, h), kwin[:, hk * HEAD_DIM:(hk + 1) * HEAD_DIM]) * SCALE + bias_ref[h]
    return jnp.where(mask, s, NEG)


def _swa_fwd(qkvg, bias, sinks, seq):
    t_tok = qkvg.shape[0]
    nb = seq // BLOCK

    def body(sink_ref, q_ref, k_ref, v_ref, bias_ref, o_ref, lse_ref, s_s, p_s, den_s):
        g = pl.program_id(1)
        subs = [pl.ds(s * BLOCK, BLOCK) for s in range(SWA_SUB)]
        wins = [_swa_window(k_ref, v_ref, SWA_SUB * g + s) for s in range(SWA_SUB)]
        for s in range(SWA_SUB):
            for h in range(N_HEADS):
                s_s[s * N_HEADS + h] = _swa_logits(q_ref.at[subs[s]], wins[s][0], bias_ref, h, wins[s][2])
        lane = lax.broadcasted_iota(jnp.int32, (BLOCK, 128), 1)
        for s in range(SWA_SUB):
            lse_tile = jnp.zeros((BLOCK, 128), F32)
            for h in range(N_HEADS):
                sc = s_s[s * N_HEADS + h]
                sink = sink_ref[h]
                m = jnp.maximum(jnp.max(sc, axis=1, keepdims=True), sink)
                pe = jnp.exp(sc - m)
                den = jnp.sum(pe, axis=1, keepdims=True) + jnp.exp(sink - m)
                p_s[s * N_HEADS + h] = pe.astype(BF16)
                den_s[s * N_HEADS + h] = den
                lse_tile = jnp.where(lane == h, m + jnp.log(den), lse_tile)
            lse_ref[subs[s], :] = lse_tile
        for s in range(SWA_SUB):
            vwin = wins[s][1]
            for pr in range(N_HEADS // 2):
                outs = []
                for h in (2 * pr, 2 * pr + 1):
                    hk = h // KV_GROUP
                    outs.append(_nn(p_s[s * N_HEADS + h], vwin[:, hk * HEAD_DIM:(hk + 1) * HEAD_DIM]) / den_s[s * N_HEADS + h])
                o_ref[subs[s], 128 * pr:128 * (pr + 1)] = jnp.concatenate(outs, axis=1).astype(BF16)

    rows = SWA_SUB * BLOCK
    steps = nb // SWA_SUB
    return _pallas(
        body, name="swa_fwd", grid=(t_tok // seq, steps),
        in_specs=[pl.BlockSpec(memory_space=pltpu.SMEM),
                  pl.BlockSpec((rows, 512), lambda b, n: (b * steps + n, C_AQ // 512)),
                  pl.BlockSpec((seq, 128), lambda b, n: (b, C_AK // 128)),
                  pl.BlockSpec((seq, 128), lambda b, n: (b, C_AV // 128)),
                  pl.BlockSpec((N_HEADS, BLOCK, 2 * BLOCK), lambda b, n: (0, 0, 0))],
        out_specs=[pl.BlockSpec((rows, 512), lambda b, n: (b * steps + n, 0)),
                   pl.BlockSpec((rows, 128), lambda b, n: (b * steps + n, 0))],
        out_shape=[jax.ShapeDtypeStruct((t_tok, 512), BF16), jax.ShapeDtypeStruct((t_tok, 128), F32)],
        scratch_shapes=[pltpu.VMEM((SWA_SUB * N_HEADS, BLOCK, 2 * BLOCK), F32),
                        pltpu.VMEM((SWA_SUB * N_HEADS, BLOCK, 2 * BLOCK), BF16),
                        pltpu.VMEM((SWA_SUB * N_HEADS, BLOCK, 1), F32)],
        compiler_params=_cp(("arbitrary", "arbitrary"), VMEM_MID))(sinks, qkvg, qkvg, qkvg, bias)


def _swa_bwd(qkvg, du_a, a_out, lse, bias, sinks, seq):
    t_tok = qkvg.shape[0]
    nb = seq // BLOCK

    def body(sink_ref, q_ref, k_ref, v_ref, do_ref, o_ref, lse_ref, bias_ref,
             dq_ref, dkv_ref, dbias_ref, dsink_ref, kv_s, s_s, dp_s, pb_s, db_s):
        b, n = pl.program_id(0), pl.program_id(1)

        @pl.when(jnp.logical_and(b == 0, n == 0))
        def _():
            dbias_ref[...] = jnp.zeros_like(dbias_ref)
            dsink_ref[...] = jnp.zeros_like(dsink_ref)

        @pl.when(n == 0)
        def _():
            kv_s[...] = jnp.zeros_like(kv_s)

        subs = [pl.ds(s * BLOCK, BLOCK) for s in range(SWA_SUB)]
        wins = [_swa_window(k_ref, v_ref, SWA_SUB * n + s) for s in range(SWA_SUB)]
        for s in range(SWA_SUB):
            kwin, vwin, mask = wins[s][:3]
            for h in range(N_HEADS):
                hk = h // KV_GROUP
                s_s[s * N_HEADS + h] = _swa_logits(q_ref.at[subs[s]], kwin, bias_ref, h, mask)
                dp_s[s * N_HEADS + h] = _nt(_head_cols(do_ref.at[subs[s]], h), vwin[:, hk * HEAD_DIM:(hk + 1) * HEAD_DIM])
        for s in range(SWA_SUB):
            lse_tile = lse_ref[subs[s], :]
            do_s, o_s = do_ref.at[subs[s]], o_ref.at[subs[s]]
            for h in range(N_HEADS):
                delta = jnp.sum(_head_cols(do_s, h).astype(F32) * _head_cols(o_s, h).astype(F32), axis=1, keepdims=True)
                lse_h = _col(lse_tile, h)
                pe = jnp.exp(s_s[s * N_HEADS + h] - lse_h)
                ds = pe * (dp_s[s * N_HEADS + h] - delta)
                dbias_ref[h] += ds
                psink = jnp.exp(sink_ref[h] - lse_h)
                dsink_ref[h:h + 1, :] += jnp.broadcast_to(jnp.sum(-psink * delta, axis=0, keepdims=True), (1, 128))
                pb_s[s * N_HEADS + h] = pe.astype(BF16)
                db_s[s * N_HEADS + h] = ds.astype(BF16)
        for s in range(SWA_SUB):
            kwin, _, _, prev, cur = wins[s]
            q_s, do_s = q_ref.at[subs[s]], do_ref.at[subs[s]]
            for pr in range(N_HEADS // 2):
                dqs = []
                for h in (2 * pr, 2 * pr + 1):
                    hk = h // KV_GROUP
                    dqs.append(_nn(db_s[s * N_HEADS + h], kwin[:, hk * HEAD_DIM:(hk + 1) * HEAD_DIM]) * SCALE)
                dq_ref[subs[s], 128 * pr:128 * (pr + 1)] = jnp.concatenate(dqs, axis=1).astype(BF16)
            dks, dvs = [], []
            for hk in range(N_HEADS // KV_GROUP):
                dk = jnp.zeros((2 * BLOCK, HEAD_DIM), F32)
                dv = jnp.zeros((2 * BLOCK, HEAD_DIM), F32)
                for h in range(hk * KV_GROUP, (hk + 1) * KV_GROUP):
                    dk = dk + _tn(db_s[s * N_HEADS + h], _head_cols(q_s, h))
                    dv = dv + _tn(pb_s[s * N_HEADS + h], _head_cols(do_s, h))
                dks.append(dk * SCALE)
                dvs.append(dv)
            upd = jnp.concatenate(dks + dvs, axis=1)
            kv_s[pl.ds(prev, BLOCK), :] += upd[:BLOCK]
            kv_s[pl.ds(cur, BLOCK), :] += upd[BLOCK:]

        @pl.when(n == steps - 1)
        def _():
            dkv_ref[...] = kv_s[...].astype(BF16)

    rows = SWA_SUB * BLOCK
    steps = nb // SWA_SUB
    tile = (SWA_SUB * N_HEADS, BLOCK, 2 * BLOCK)
    return _pallas(
        body, name="swa_bwd", grid=(t_tok // seq, steps),
        in_specs=[pl.BlockSpec(memory_space=pltpu.SMEM),
                  pl.BlockSpec((rows, 512), lambda b, n: (b * steps + n, C_AQ // 512)),
                  pl.BlockSpec((seq, 128), lambda b, n: (b, C_AK // 128)),
                  pl.BlockSpec((seq, 128), lambda b, n: (b, C_AV // 128)),
                  pl.BlockSpec((rows, 512), lambda b, n: (b * steps + n, 0)),
                  pl.BlockSpec((rows, 512), lambda b, n: (b * steps + n, 0)),
                  pl.BlockSpec((rows, 128), lambda b, n: (b * steps + n, 0)),
                  pl.BlockSpec((N_HEADS, BLOCK, 2 * BLOCK), lambda b, n: (0, 0, 0))],
        out_specs=[pl.BlockSpec((rows, 512), lambda b, n: (b * steps + n, 0)),
                   pl.BlockSpec((seq, 256), lambda b, n: (b, 0)),
                   pl.BlockSpec((N_HEADS, BLOCK, 2 * BLOCK), lambda b, n: (0, 0, 0)),
                   pl.BlockSpec((N_HEADS, 128), lambda b, n: (0, 0))],
        out_shape=[jax.ShapeDtypeStruct((t_tok, 512), BF16), jax.ShapeDtypeStruct((t_tok, 256), BF16),
                   jax.ShapeDtypeStruct((N_HEADS, BLOCK, 2 * BLOCK), F32), jax.ShapeDtypeStruct((N_HEADS, 128), F32)],
        scratch_shapes=[pltpu.VMEM((seq, 256), F32), pltpu.VMEM(tile, F32), pltpu.VMEM(tile, F32),
                        pltpu.VMEM(tile, BF16), pltpu.VMEM(tile, BF16)],
        compiler_params=_cp(("arbitrary", "arbitrary"), VMEM_MID))(sinks, qkvg, qkvg, qkvg, du_a, a_out, lse, bias)


def _out_proj(name, u_parts, gate_arr, gate_blk, w_out, x, gmod, seq):
    t_tok = x.shape[0]
    nu = len(u_parts)

    def body(*refs):
        u_refs = refs[:nu]
        g_ref, w_ref, x_ref, gm_ref, yg_ref, y_ref, xn_ref = refs[nu:]
        u = jnp.concatenate([r[...].astype(F32) for r in u_refs], axis=1) if nu > 1 else u_refs[0][...].astype(F32)
        yg = (u * _silu(g_ref[...].astype(F32))).astype(BF16)
        yg_ref[...] = yg
        y = _nn(yg, w_ref[...])
        y_ref[...] = y.astype(BF16)
        xn_ref[...] = x_ref[...] + gm_ref[...] * y

    row = lambda w: pl.BlockSpec((TM, w), lambda i: (i, 0))
    in_specs = [row(u.shape[1]) for u in u_parts]
    in_specs += [pl.BlockSpec((TM, D_MODEL), lambda i: (i, gate_blk)),
                 pl.BlockSpec((D_MODEL, D_MODEL), lambda i: (0, 0)), row(D_MODEL),
                 pl.BlockSpec((None, 1, D_MODEL), lambda i: (i * TM // seq, 0, 0))]
    return _pallas(
        body, name=name, grid=(t_tok // TM,), in_specs=in_specs,
        out_specs=[row(D_MODEL)] * 3,
        out_shape=[jax.ShapeDtypeStruct((t_tok, D_MODEL), BF16)] * 2 + [jax.ShapeDtypeStruct((t_tok, D_MODEL), F32)],
        compiler_params=_cp(("arbitrary",), VMEM_MID))(*u_parts, gate_arr, w_out, x, gmod)


def _out_proj_bwd(name, dxn, gmod, y, w_out, seq, attn=None):
    t_tok = dxn.shape[0]
    tiles_per_seq = seq // TM

    def body(*refs):
        if attn is None:
            dxn_ref, gm_ref, y_ref, w_ref, dy_ref, dgm_ref, dyg_ref = refs
        else:
            dxn_ref, gm_ref, y_ref, w_ref, a_ref, b_ref, g_ref, dy_ref, dgm_ref, dua_ref, dub_ref, dg_ref = refs
        i = pl.program_id(0)
        dxv = dxn_ref[...]
        dy = (dxv * gm_ref[...]).astype(BF16)
        dy_ref[...] = dy

        @pl.when(i % tiles_per_seq == 0)
        def _():
            dgm_ref[...] = jnp.zeros_like(dgm_ref)

        dgm_ref[...] += jnp.sum(dxv * y_ref[...].astype(F32), axis=0, keepdims=True)
        dyg = _nn(dy, w_ref[...])
        if attn is None:
            dyg_ref[...] = dyg
        else:
            gt = g_ref[...].astype(F32)
            du = dyg * _silu(gt)
            dua_ref[...] = du[:, :512].astype(BF16)
            dub_ref[...] = du[:, 512:].astype(BF16)
            u = jnp.concatenate([a_ref[...].astype(F32), b_ref[...].astype(F32)], axis=1)
            dg_ref[...] = (dyg * u * _dsilu(gt)).astype(BF16)

    row = lambda w: pl.BlockSpec((TM, w), lambda i: (i, 0))
    mod_spec = pl.BlockSpec((None, 1, D_MODEL), lambda i: (i * TM // seq, 0, 0))
    in_specs = [row(D_MODEL), mod_spec, row(D_MODEL), pl.BlockSpec((D_MODEL, D_MODEL), lambda i: (0, 0))]
    out_specs = [row(D_MODEL), mod_spec]
    out_shape = [jax.ShapeDtypeStruct((t_tok, D_MODEL), BF16), jax.ShapeDtypeStruct(gmod.shape, F32)]
    args = [dxn, gmod, y, w_out]
    if attn is None:
        out_specs.append(row(D_MODEL))
        out_shape.append(jax.ShapeDtypeStruct((t_tok, D_MODEL), F32))
    else:
        in_specs += [row(512), row(512), pl.BlockSpec((TM, D_MODEL), lambda i: (i, C_GATE // D_MODEL))]
        out_specs += [row(512), row(512), row(D_MODEL)]
        out_shape += [jax.ShapeDtypeStruct((t_tok, 512), BF16)] * 2 + [jax.ShapeDtypeStruct((t_tok, D_MODEL), BF16)]
        args += list(attn)
    return _pallas(body, name=name, grid=(t_tok // TM,), in_specs=in_specs, out_specs=out_specs,
                   out_shape=out_shape, compiler_params=_cp(("arbitrary",), VMEM_MID))(*args)


def _norm_bwd(name, parts, w, x, g, scale, dxn, seq, rows_part=None):
    t_tok = x.shape[0]
    npart = len(parts)
    tiles_per_seq = seq // TM
    nrow_in = 0 if rows_part is None else 2

    def body(*refs):
        p_refs = refs[:npart]
        w_ref, x_ref, g_ref, sc_ref, dxn_ref = refs[npart:npart + 5]
        dx_ref, dss_ref, dg_ref = refs[npart + 5 + nrow_in:]
        i = pl.program_id(0)
        dh = jnp.zeros((TM, D_MODEL), F32)
        if rows_part is not None:
            r_ref, wr_ref = refs[npart + 5:npart + 7]
            dh = dh + _tn(r_ref[...].astype(BF16), wr_ref[...])
        for (arr, off), p_ref in zip(parts, p_refs):
            dh = dh + _nn(p_ref[...], w_ref[off:off + arr.shape[1], :])
        xv = x_ref[...]
        rstd = lax.rsqrt(jnp.mean(xv * xv, axis=-1, keepdims=True) + EPS)
        xhat = xv * rstd
        gv = g_ref[...]
        nrm = xhat * gv

        @pl.when(i % tiles_per_seq == 0)
        def _():
            dss_ref[...] = jnp.zeros_like(dss_ref)

        @pl.when(i == 0)
        def _():
            dg_ref[...] = jnp.zeros_like(dg_ref)

        dss_ref[0:1, :] += jnp.sum(dh, axis=0, keepdims=True)
        dss_ref[1:2, :] += jnp.sum(dh * nrm, axis=0, keepdims=True)
        dn = dh * (1.0 + sc_ref[...])
        dg_ref[0:1, :] += jnp.sum(dn * xhat, axis=0, keepdims=True)
        dxhat = dn * gv
        dx_ref[...] = rstd * (dxhat - xhat * jnp.mean(dxhat * xhat, axis=-1, keepdims=True)) + dxn_ref[...]

    row = lambda wd: pl.BlockSpec((TM, wd), lambda i: (i, 0))
    w_spec = pl.BlockSpec(w.shape, lambda i: (0, 0))
    in_specs = [row(a.shape[1]) for a, _ in parts]
    in_specs += [w_spec, row(D_MODEL), pl.BlockSpec((1, D_MODEL), lambda i: (0, 0)),
                 pl.BlockSpec((None, 1, D_MODEL), lambda i: (i * TM // seq, 0, 0)), row(D_MODEL)]
    args = [a for a, _ in parts] + [w, x, g, scale, dxn]
    if rows_part is not None:
        in_specs += [pl.BlockSpec((8, TM), lambda i: (0, i)), pl.BlockSpec((8, D_MODEL), lambda i: (0, 0))]
        args += list(rows_part)
    nseq = t_tok // seq
    return _pallas(
        body, name=name, grid=(t_tok // TM,), in_specs=in_specs,
        out_specs=[row(D_MODEL), pl.BlockSpec((None, 8, D_MODEL), lambda i: (i * TM // seq, 0, 0)),
                   pl.BlockSpec((8, D_MODEL), lambda i: (0, 0))],
        out_shape=[jax.ShapeDtypeStruct((t_tok, D_MODEL), F32), jax.ShapeDtypeStruct((nseq, 8, D_MODEL), F32),
                   jax.ShapeDtypeStruct((8, D_MODEL), F32)],
        compiler_params=_cp(("arbitrary",), VMEM_BIG))(*args)


def _dw(name, a, parts, blocked=None):
    t_tok, ka = a.shape
    tt = min(1024, t_tok)
    npart = len(parts)
    nt = t_tok // tt

    def body(*refs):
        a_ref = refs[0]
        p_refs = refs[1:1 + npart]
        o_refs = refs[1 + npart:1 + 2 * npart]
        acc_refs = refs[1 + 2 * npart:]
        t = pl.program_id(0)
        av = a_ref[...]
        for p_ref, acc in zip(p_refs, acc_refs):
            upd = _tn(av, p_ref[...])

            @pl.when(t == 0)
            def _():
                acc[...] = upd

            @pl.when(t > 0)
            def _():
                acc[...] += upd

        @pl.when(t == nt - 1)
        def _():
            for o_ref, acc in zip(o_refs, acc_refs):
                if blocked is None:
                    o_ref[...] = acc[...].astype(BF16)
                else:
                    for j in range(o_ref.shape[0]):
                        o_ref[j] = acc[:, j * blocked:(j + 1) * blocked].astype(BF16)

    in_specs = [pl.BlockSpec((tt, ka), lambda t: (t, 0))]
    in_specs += [pl.BlockSpec((tt, p.shape[1]), lambda t: (t, 0)) for p in parts]
    if blocked is None:
        out_shape = [jax.ShapeDtypeStruct((ka, p.shape[1]), BF16) for p in parts]
        out_specs = [pl.BlockSpec((ka, p.shape[1]), lambda t: (0, 0)) for p in parts]
    else:
        out_shape = [jax.ShapeDtypeStruct((p.shape[1] // blocked, ka, blocked), BF16) for p in parts]
        out_specs = [pl.BlockSpec((p.shape[1] // blocked, ka, blocked), lambda t: (0, 0, 0)) for p in parts]
    return _pallas(body, name=name, grid=(nt,), in_specs=in_specs, out_specs=out_specs, out_shape=out_shape,
                   scratch_shapes=[pltpu.VMEM((ka, p.shape[1]), F32) for p in parts],
                   compiler_params=_cp(("arbitrary",), VMEM_BIG))(a, *parts)


def _dw_rows(name, rows_t, h):
    t_tok = h.shape[0]
    tt = 512

    def body(r_ref, h_ref, o_ref):
        @pl.when(pl.program_id(0) == 0)
        def _():
            o_ref[...] = jnp.zeros_like(o_ref)

        o_ref[...] += _nn(r_ref[...].astype(BF16), h_ref[...])

    return _pallas(body, name=name, grid=(t_tok // tt,),
                   in_specs=[pl.BlockSpec((8, tt), lambda t: (0, t)), pl.BlockSpec((tt, D_MODEL), lambda t: (t, 0))],
                   out_specs=pl.BlockSpec((8, D_MODEL), lambda t: (0, 0)),
                   out_shape=jax.ShapeDtypeStruct((8, D_MODEL), F32),
                   compiler_params=_cp(("arbitrary",), VMEM_MID))(rows_t, h)


def _lru_gates(xc, blk, wa_ref, wx_ref, ba_ref, bx_ref, sp):
    cols = slice(blk * LRU_BLOCK_W, (blk + 1) * LRU_BLOCK_W)
    xb = xc[:, cols].astype(BF16)
    r = _sigmoid(_nn(xb, wa_ref[blk].astype(BF16)) + ba_ref[:, cols])
    ig = _sigmoid(_nn(xb, wx_ref[blk].astype(BF16)) + bx_ref[:, cols])
    log_a = -LRU_C * r * sp[:, cols]
    a = jnp.exp(log_a)
    x2 = 2.0 * log_a
    series = -x2 * (1.0 + x2 * (0.5 + x2 * (1.0 / 6.0)))
    z = jnp.where(x2 > -0.01, series, 1.0 - a * a)
    mult = z * lax.rsqrt(jnp.maximum(z, 1e-30))
    return xb, r, ig, a, mult


def _softplus_neg(lam):
    return jnp.maximum(-lam, 0.0) + jnp.log(1.0 + jnp.exp(-jnp.abs(lam)))


def _conv_taps(xe_ref, cw_ref, cb_ref):
    xc = cb_ref[...] + xe_ref[8:8 + TC, :] * cw_ref[3:4, :]
    for k in range(1, 4):
        xc = xc + xe_ref[8 - k:8 - k + TC, :] * cw_ref[3 - k:4 - k, :]
    return xc


def _lru_fwd(proj, cw, cb, w_a, b_a, w_x, b_x, lam, seq):
    t_tok = proj.shape[0]
    nc = seq // TC

    def body(x_ref, cw_ref, cb_ref, wa_ref, ba_ref, wx_ref, bx_ref, lam_ref, hs_ref, xe_s, a_s, u_s, h_s):
        c = pl.program_id(1)

        @pl.when(c == 0)
        def _():
            xe_s[0:8, :] = jnp.zeros((8, D_MODEL), F32)
            h_s[...] = jnp.zeros_like(h_s)

        xe_s[8:8 + TC, :] = x_ref[...]
        xc = _conv_taps(xe_s, cw_ref, cb_ref)
        sp = _softplus_neg(lam_ref[...])
        for blk in range(LRU_BLOCKS):
            cols = slice(blk * LRU_BLOCK_W, (blk + 1) * LRU_BLOCK_W)
            _, _, ig, a, mult = _lru_gates(xc, blk, wa_ref, wx_ref, ba_ref, bx_ref, sp)
            a_s[:, cols] = a
            u_s[:, cols] = mult * ig * xc[:, cols]

        def step(t, h):
            h = a_s[pl.ds(t, 1), :] * h + u_s[pl.ds(t, 1), :]
            hs_ref[pl.ds(t, 1), :] = h
            return h

        h_s[0:1, :] = lax.fori_loop(0, TC, step, h_s[0:1, :], unroll=8)
        xe_s[0:8, :] = xe_s[TC:TC + 8, :]

    full = lambda shape: pl.BlockSpec(shape, lambda b, c: (0,) * len(shape))
    return _pallas(
        body, name="lru_fwd", grid=(t_tok // seq, nc),
        in_specs=[pl.BlockSpec((TC, D_MODEL), lambda b, c: (b * nc + c, 0)), full((4, D_MODEL)), full((1, D_MODEL)),
                  full((LRU_BLOCKS, LRU_BLOCK_W, LRU_BLOCK_W)), full((1, D_MODEL)),
                  full((LRU_BLOCKS, LRU_BLOCK_W, LRU_BLOCK_W)), full((1, D_MODEL)), full((1, D_MODEL))],
        out_specs=pl.BlockSpec((TC, D_MODEL), lambda b, c: (b * nc + c, 0)),
        out_shape=jax.ShapeDtypeStruct((t_tok, D_MODEL), F32),
        scratch_shapes=[pltpu.VMEM((TC + 8, D_MODEL), F32), pltpu.VMEM((TC, D_MODEL), F32),
                        pltpu.VMEM((TC, D_MODEL), F32), pltpu.VMEM((8, D_MODEL), F32)],
        compiler_params=_cp(("arbitrary", "arbitrary"), VMEM_BIG))(proj, cw, cb, w_a, b_a, w_x, b_x, lam)


def _lru_bwd(proj, hs, dyh, cw, cb, w_a, b_a, w_x, b_x, lam, seq):
    t_tok = proj.shape[0]
    nc = seq // TC

    def body(x_ref, xh_ref, g_ref, hs_ref, hh_ref, dy_ref, cw_ref, cb_ref, wa_ref, ba_ref, wx_ref, bx_ref, lam_ref,
             dp_ref, dcw_ref, dvec_ref, dwa_ref, dwx_ref,
             xe_s, he_s, de_s, a_s, r_s, i_s, m_s, dh_s, carry_s):
        b, cr = pl.program_id(0), pl.program_id(1)
        c = nc - 1 - cr

        @pl.when(jnp.logical_and(b == 0, cr == 0))
        def _():
            dcw_ref[...] = jnp.zeros_like(dcw_ref)
            dvec_ref[...] = jnp.zeros_like(dvec_ref)
            dwa_ref[...] = jnp.zeros_like(dwa_ref)
            dwx_ref[...] = jnp.zeros_like(dwx_ref)

        @pl.when(cr == 0)
        def _():
            carry_s[...] = jnp.zeros_like(carry_s)
            de_s[TC:TC + 8, :] = jnp.zeros((8, D_MODEL), F32)

        first = c == 0
        xe_s[0:8, :] = jnp.where(first, 0.0, xh_ref[...])
        xe_s[8:8 + TC, :] = x_ref[...]
        he_s[0:8, :] = jnp.where(first, 0.0, hh_ref[...])
        he_s[8:8 + TC, :] = hs_ref[...]
        xc = _conv_taps(xe_s, cw_ref, cb_ref)
        lam_v = lam_ref[...]
        sp = _softplus_neg(lam_v)
        for blk in range(LRU_BLOCKS):
            cols = slice(blk * LRU_BLOCK_W, (blk + 1) * LRU_BLOCK_W)
            _, r, ig, a, mult = _lru_gates(xc, blk, wa_ref, wx_ref, ba_ref, bx_ref, sp)
            a_s[:, cols], r_s[:, cols], i_s[:, cols], m_s[:, cols] = a, r, ig, mult

        gt = g_ref[...]
        dyh = dy_ref[...]
        sg = _sigmoid(gt)
        dh_s[...] = dyh * (gt * sg)
        dp_ref[:, D_MODEL:] = (dyh * hs_ref[...] * (sg * (1.0 + gt * (1.0 - sg)))).astype(BF16)

        def step(k, carry):
            t = TC - 1 - k
            dh = dh_s[pl.ds(t, 1), :] + carry
            dh_s[pl.ds(t, 1), :] = dh
            return a_s[pl.ds(t, 1), :] * dh

        carry_s[0:1, :] = lax.fori_loop(0, TC, step, carry_s[0:1, :], unroll=8)

        hprev = he_s[7:7 + TC, :]
        for blk in range(LRU_BLOCKS):
            cols = slice(blk * LRU_BLOCK_W, (blk + 1) * LRU_BLOCK_W)
            xcb = xc[:, cols]
            a, r, ig, mult, dh = a_s[:, cols], r_s[:, cols], i_s[:, cols], m_s[:, cols], dh_s[:, cols]
            spb = sp[:, cols]
            dmult = dh * ig * xcb
            di = dh * mult * xcb
            dxc = dh * mult * ig
            dla = dh * hprev[:, cols] * a - dmult * (a * a) * lax.rsqrt(jnp.maximum(mult * mult, 1e-30))
            dr = dla * (-LRU_C * spb)
            dsp = jnp.sum(dla * (-LRU_C * r), axis=0, keepdims=True)
            dga = dr * r * (1.0 - r)
            dgx = di * ig * (1.0 - ig)
            dga_b, dgx_b = dga.astype(BF16), dgx.astype(BF16)
            xb = xcb.astype(BF16)
            dxc = dxc + _nt(dga_b, wa_ref[blk].astype(BF16)) + _nt(dgx_b, wx_ref[blk].astype(BF16))
            dwa_ref[blk] += _tn(xb, dga_b)
            dwx_ref[blk] += _tn(xb, dgx_b)
            dvec_ref[1:2, cols] += jnp.sum(dga, axis=0, keepdims=True)
            dvec_ref[2:3, cols] += jnp.sum(dgx, axis=0, keepdims=True)
            dvec_ref[3:4, cols] += dsp * (-1.0 / (1.0 + jnp.exp(lam_v[:, cols])))
            de_s[0:TC, cols] = dxc

        dxc = de_s[0:TC, :]
        dvec_ref[0:1, :] += jnp.sum(dxc, axis=0, keepdims=True)
        dxr = dxc * cw_ref[3:4, :]
        dcw_ref[3:4, :] += jnp.sum(dxc * xe_s[8:8 + TC, :], axis=0, keepdims=True)
        for k in range(1, 4):
            dxr = dxr + de_s[k:k + TC, :] * cw_ref[3 - k:4 - k, :]
            dcw_ref[3 - k:4 - k, :] += jnp.sum(dxc * xe_s[8 - k:8 - k + TC, :], axis=0, keepdims=True)
        dp_ref[:, :D_MODEL] = dxr.astype(BF16)
        de_s[TC:TC + 8, :] = de_s[0:8, :]

    chunk = lambda col: pl.BlockSpec((TC, D_MODEL), lambda b, cr: (b * nc + nc - 1 - cr, col))
    halo = lambda col: pl.BlockSpec(
        (8, D_MODEL), lambda b, cr: (jnp.maximum((b * nc + nc - 1 - cr) * (TC // 8) - 1, 0), col))
    full = lambda shape: pl.BlockSpec(shape, lambda b, cr: (0,) * len(shape))
    wblk = (LRU_BLOCKS, LRU_BLOCK_W, LRU_BLOCK_W)
    return _pallas(
        body, name="lru_bwd", grid=(t_tok // seq, nc),
        in_specs=[chunk(0), halo(0), chunk(1), chunk(0), halo(0), chunk(0),
                  full((4, D_MODEL)), full((1, D_MODEL)), full(wblk), full((1, D_MODEL)), full(wblk),
                  full((1, D_MODEL)), full((1, D_MODEL))],
        out_specs=[pl.BlockSpec((TC, 2 * D_MODEL), lambda b, cr: (b * nc + nc - 1 - cr, 0)),
                   full((8, D_MODEL)), full((8, D_MODEL)), full(wblk), full(wblk)],
        out_shape=[jax.ShapeDtypeStruct((t_tok, 2 * D_MODEL), BF16), jax.ShapeDtypeStruct((8, D_MODEL), F32),
                   jax.ShapeDtypeStruct((8, D_MODEL), F32), jax.ShapeDtypeStruct(wblk, F32),
                   jax.ShapeDtypeStruct(wblk, F32)],
        scratch_shapes=[pltpu.VMEM((TC + 8, D_MODEL), F32), pltpu.VMEM((TC + 8, D_MODEL), F32),
                        pltpu.VMEM((TC + 8, D_MODEL), F32)]
        + [pltpu.VMEM((TC, D_MODEL), F32)] * 5 + [pltpu.VMEM((8, D_MODEL), F32)],
        compiler_params=_cp(("arbitrary", "arbitrary"), VMEM_BIG),
    )(proj, proj, proj, hs, hs, dyh, cw, cb, w_a, b_a, w_x, b_x, lam)


def _last_layer_tail(hs, proj, w_out, w_out_t, x, gmod, final_g, target, seq):
    t_tok = x.shape[0]
    tiles_per_seq = seq // TM

    def body(hs_ref, g_ref, w_ref, wt_ref, x_ref, gm_ref, fg_ref, t_ref,
             yg_ref, dx_ref, dy_ref, dyg_ref, dgm_ref, loss_ref, dfg_ref):
        i = pl.program_id(0)

        @pl.when(i == 0)
        def _():
            loss_ref[...] = jnp.zeros_like(loss_ref)
            dfg_ref[...] = jnp.zeros_like(dfg_ref)

        @pl.when(i % tiles_per_seq == 0)
        def _():
            dgm_ref[...] = jnp.zeros_like(dgm_ref)

        gm = gm_ref[...]
        yg = (hs_ref[...] * _silu(g_ref[...])).astype(BF16)
        yg_ref[...] = yg
        y = _nn(yg, w_ref[...])
        xv = x_ref[...] + gm * y
        gv = fg_ref[...]
        rstd = lax.rsqrt(jnp.mean(xv * xv, axis=-1, keepdims=True) + EPS)
        xhat = xv * rstd
        err = xhat * gv - t_ref[...]
        loss_ref[0:1, :] += jnp.sum(err * err, axis=0, keepdims=True) * (0.5 / D_MODEL)
        dout = err * (1.0 / D_MODEL)
        dfg_ref[0:1, :] += jnp.sum(dout * xhat, axis=0, keepdims=True)
        dxhat = dout * gv
        dxv = rstd * (dxhat - xhat * jnp.mean(dxhat * xhat, axis=-1, keepdims=True))
        dx_ref[...] = dxv
        dgm_ref[...] += jnp.sum(dxv * y, axis=0, keepdims=True)
        dy = (dxv * gm).astype(BF16)
        dy_ref[...] = dy
        dyg_ref[...] = _nn(dy, wt_ref[...])

    row = pl.BlockSpec((TM, D_MODEL), lambda i: (i, 0))
    acc = pl.BlockSpec((8, D_MODEL), lambda i: (0, 0))
    mod_spec = pl.BlockSpec((None, 1, D_MODEL), lambda i: (i * TM // seq, 0, 0))
    return _pallas(
        body, name="last_layer_tail", grid=(t_tok // TM,),
        in_specs=[row, pl.BlockSpec((TM, D_MODEL), lambda i: (i, 1)), pl.BlockSpec((D_MODEL, D_MODEL), lambda i: (0, 0)),
                  pl.BlockSpec((D_MODEL, D_MODEL), lambda i: (0, 0)),
                  row, mod_spec, pl.BlockSpec((1, D_MODEL), lambda i: (0, 0)), row],
        out_specs=[row, row, row, row, mod_spec, acc, acc],
        out_shape=[jax.ShapeDtypeStruct((t_tok, D_MODEL), BF16), jax.ShapeDtypeStruct((t_tok, D_MODEL), F32),
                   jax.ShapeDtypeStruct((t_tok, D_MODEL), BF16), jax.ShapeDtypeStruct((t_tok, D_MODEL), F32),
                   jax.ShapeDtypeStruct(gmod.shape, F32), jax.ShapeDtypeStruct((8, D_MODEL), F32),
                   jax.ShapeDtypeStruct((8, D_MODEL), F32)],
        compiler_params=_cp(("arbitrary",), VMEM_BIG))(hs, proj, w_out, w_out_t, x, gmod, final_g, target)


def _adam_math(w, g, m, v):
    m_new = ADAM_B1 * m + (1.0 - ADAM_B1) * g
    v_new = ADAM_B2 * v + (1.0 - ADAM_B2) * (g * g)
    m_hat = m_new / (1.0 - ADAM_B1 ** ADAM_STEP)
    v_hat = v_new / (1.0 - ADAM_B2 ** ADAM_STEP)
    delta = -ADAM_LR * (m_hat / (jnp.sqrt(v_hat) + ADAM_EPS) + ADAM_WD * w)
    return delta, m_new, v_new


def _sum_leading(name, x, out_dtype=F32):
    n, rows, cols = x.shape
    tr = PACK_ROWS if rows % PACK_ROWS == 0 else rows

    def body(x_ref, o_ref):
        acc = x_ref[0].astype(F32)
        for d in range(1, n):
            acc = acc + x_ref[d].astype(F32)
        o_ref[...] = acc.astype(out_dtype)

    return _pallas(body, name=name, grid=(rows // tr,),
                   in_specs=[pl.BlockSpec((n, tr, cols), lambda i: (0, i, 0))],
                   out_specs=pl.BlockSpec((tr, cols), lambda i: (i, 0)),
                   out_shape=jax.ShapeDtypeStruct((rows, cols), out_dtype),
                   compiler_params=_cp(("arbitrary",), VMEM_MID))(x)


def _adamw(name, w, m, v, g=None, parts=None):
    rows, cols = w.shape
    tr = rows if rows <= 256 else 256

    def body(*refs):
        w_ref, m_ref, v_ref, g_in, g_ref, d_ref, mo_ref, vo_ref = refs
        if parts is None:
            gv = g_in[...]
        else:
            acc = g_in[0].astype(F32)
            for d in range(1, parts.shape[0]):
                acc = acc + g_in[d].astype(F32)
            gv = acc[:, :cols]
        delta, m_new, v_new = _adam_math(w_ref[...], gv, m_ref[...], v_ref[...])
        g_ref[...] = gv
        d_ref[...] = delta
        mo_ref[...] = m_new
        vo_ref[...] = v_new

    row = pl.BlockSpec((tr, cols), lambda i: (i, 0))
    if parts is None:
        g_spec, g_arg = row, g
    else:
        g_spec, g_arg = pl.BlockSpec((parts.shape[0], tr, parts.shape[2]), lambda i: (0, i, 0)), parts
    return _pallas(body, name=name, grid=(rows // tr,), in_specs=[row, row, row, g_spec], out_specs=[row] * 4,
                   out_shape=[jax.ShapeDtypeStruct((rows, cols), F32)] * 4,
                   compiler_params=_cp(("arbitrary",), VMEM_MID))(w, m, v, g_arg)


def _adamw_many(name, groups):
    ntens = len(groups)

    def body(*refs):
        ins, outs = refs[:4 * ntens], refs[4 * ntens:]
        for k in range(ntens):
            w_ref, m_ref, v_ref, g_ref = ins[4 * k:4 * k + 4]
            gv = g_ref[...]
            delta, m_new, v_new = _adam_math(w_ref[...], gv, m_ref[...], v_ref[...])
            for o_ref, val in zip(outs[4 * k:4 * k + 4], (gv, delta, m_new, v_new)):
                o_ref[...] = val

    flat = [a for grp in groups for a in grp]
    out_shape = [jax.ShapeDtypeStruct(grp[0].shape, F32) for grp in groups for _ in range(4)]
    outs = _pallas(body, name=name, out_shape=out_shape, compiler_params=_cp(vmem=VMEM_MID))(*flat)
    return [tuple(outs[4 * k:4 * k + 4]) for k in range(ntens)]


def _pack_rows(arrs):
    rows, meta, total = [], [], 0
    for a in arrs:
        flat = a.reshape(-1)
        nrow = -(-flat.shape[0] // 1024) * 8
        rows.append(jnp.pad(flat, (0, nrow * 128 - flat.shape[0])).reshape(nrow, 128))
        meta.append((a.shape, flat.shape[0], nrow))
        total += nrow
    tail = -total % PACK_ROWS
    if tail:
        rows.append(jnp.zeros((tail, 128), F32))
    return jnp.concatenate(rows, axis=0), meta


def _unpack_rows(packed, meta):
    out, r0 = [], 0
    for shape, size, nrow in meta:
        out.append(packed[r0:r0 + nrow].reshape(-1)[:size].reshape(shape))
        r0 += nrow
    return out


WEIGHTS = ["rel_bias", "norm_g", "ada_w", "ada_b", "attn_w_in", "attn_sinks", "attn_b_f", "attn_w_out", "lru_w_in",
           "lru_conv_w", "lru_conv_b", "lru_w_a", "lru_b_a", "lru_w_x", "lru_b_x", "lru_lambda", "lru_w_out", "final_g"]
BIG = ["ada_w", "attn_w_in", "attn_w_out", "lru_w_in", "lru_w_out"]
PACK_ROWS = 256


def kernel(x, c, rel_bias, norm_g, ada_w, ada_b, attn_w_in, attn_sinks, attn_b_f, attn_w_out, lru_w_in, lru_conv_w, lru_conv_b, lru_w_a, lru_b_a, lru_w_x, lru_b_x, lru_lambda, lru_w_out, final_g, loss_target, m_rel_bias, m_norm_g, m_ada_w, m_ada_b, m_attn_w_in, m_attn_sinks, m_attn_b_f, m_attn_w_out, m_lru_w_in, m_lru_conv_w, m_lru_conv_b, m_lru_w_a, m_lru_b_a, m_lru_w_x, m_lru_b_x, m_lru_lambda, m_lru_w_out, m_final_g, v_rel_bias, v_norm_g, v_ada_w, v_ada_b, v_attn_w_in, v_attn_sinks, v_attn_b_f, v_attn_w_out, v_lru_w_in, v_lru_conv_w, v_lru_conv_b, v_lru_w_a, v_lru_b_a, v_lru_w_x, v_lru_b_x, v_lru_lambda, v_lru_w_out, v_final_g):
    nseq, seq, _ = x.shape
    t_tok = nseq * seq
    me = 4 * lax.axis_index("x") + 2 * lax.axis_index("y") + lax.axis_index("c")
    x0 = x.reshape(t_tok, D_MODEL)
    target = loss_target.reshape(t_tok, D_MODEL)

    w_in_pad = jnp.pad(attn_w_in[0].astype(BF16), ((0, 0), (0, SHARD_W_PAD - SHARD_W_IN)))
    vec_shard = jnp.concatenate([lru_conv_w[0], lru_conv_b, lru_b_a, lru_b_x, lru_lambda], axis=0)
    g_w_in, g_vec, g_c = _exchange("gather_first", [w_in_pad, vec_shard, c], [])
    later_w = [attn_w_out[0].astype(BF16), lru_w_in[0].astype(BF16), lru_w_out[0].astype(BF16)]
    later_handle, later_token = _exchange_start("gather_later_start", later_w, [], after=g_vec)
    w_full = jnp.transpose(g_w_in[:, :, :SHARD_W_IN], (1, 0, 2)).reshape(D_MODEL, N_DEV * SHARD_W_IN)
    w_aq, w_ak, w_av = w_full[:, 0:512], w_full[:, 512:640], w_full[:, 640:768]
    w_bq, w_bk, w_bv = w_full[:, 768:1280], w_full[:, 1280:1792], w_full[:, 1792:2304]
    w_f, w_gate = w_full[:, 2304:2312], w_full[:, 2312:3336]
    w_main = jnp.concatenate([w_bq, w_bk, w_bv, w_aq, w_gate, w_ak, w_av], axis=1)
    wf_t = jnp.transpose(w_f)
    vec_full = jnp.transpose(g_vec, (1, 0, 2)).reshape(8, D_MODEL)
    conv_w, conv_b, b_a, b_x, lam = vec_full[0:4], vec_full[4:5], vec_full[5:6], vec_full[6:7], vec_full[7:8]
    c_all = g_c.reshape(N_DEV * nseq, D_MODEL)

    ncol = ada_w.shape[2]
    ada_b_slice = lax.dynamic_slice(ada_b.reshape(2, N_DEV, ncol), (0, me, 0), (2, 1, ncol))
    mod_part = _ada_mod(c_all, ada_w, ada_b_slice)
    (g_mod,) = _exchange("gather_mod", [mod_part], [])
    mine = lax.dynamic_slice(g_mod, (0, 0, me * nseq, 0), (N_DEV, 2, nseq, ncol))
    mod = jnp.transpose(mine, (1, 2, 0, 3)).reshape(2, nseq, 3 * D_MODEL)
    shift = [mod[l, :, 0:D_MODEL].reshape(nseq, 1, D_MODEL) for l in range(2)]
    scale = [mod[l, :, D_MODEL:2 * D_MODEL].reshape(nseq, 1, D_MODEL) for l in range(2)]
    gmod = [mod[l, :, 2 * D_MODEL:].reshape(nseq, 1, D_MODEL) for l in range(2)]

    onehot = _bucket_onehot()
    bias = _bias_expand(jnp.transpose(rel_bias), onehot).reshape(N_HEADS, BLOCK, 2 * BLOCK)
    sinks = attn_sinks.reshape(N_HEADS)
    b_f = attn_b_f.reshape(N_HEADS, 1)
    norm_g0 = norm_g[0:1] + later_token[0:1, 0:1]
    h0, qkvg, fl_t = _norm_proj("norm_proj0", x0, norm_g0, shift[0], scale[0], w_main, seq, BF16, wf_t=wf_t)
    f_row, f_col = _fox_prep(fl_t, b_f, seq)
    a_out, lse_a = _swa_fwd(qkvg, bias, sinks, seq)
    q_aug, k_aug, kt_aug, vt = _fox_aug(qkvg, f_col, seq)
    b_out, lse_b = _fox_fwd_t(q_aug, k_aug, vt, seq)
    g_later = _exchange_wait("gather_later_wait", later_handle, after=lse_b)
    w_out0, g_lru_in, w_out1 = (_with_own(g, w, me) for g, w in zip(g_later, later_w))
    w_out0, w_out1 = w_out0.reshape(D_MODEL, D_MODEL), w_out1.reshape(D_MODEL, D_MODEL)
    w_out0_t, w_out1_t, w_main_t = jnp.transpose(w_out0), jnp.transpose(w_out1), jnp.transpose(w_main)
    lru_in_t = jnp.transpose(g_lru_in, (0, 2, 1)).reshape(2 * D_MODEL, D_MODEL)
    yg0, y0, x1 = _out_proj("out_proj0", [a_out, b_out], qkvg, C_GATE // D_MODEL, w_out0, x0, gmod[0], seq)

    h1, proj1 = _norm_proj("norm_proj1", x1, norm_g[1:2], shift[1], scale[1], g_lru_in, seq, F32)
    hs = _lru_fwd(proj1, conv_w, conv_b, lru_w_a[0], b_a, lru_w_x[0], b_x, lam, seq)

    yg1, dx2, dy1, dyh, dgm1, loss_rows, dfinal_rows = _last_layer_tail(
        hs, proj1, w_out1, w_out1_t, x1, gmod[1], final_g.reshape(1, D_MODEL), target, seq)

    dproj1, dcw, dvec, dw_a, dw_x = _lru_bwd(proj1, hs, dyh, conv_w, conv_b, lru_w_a[0], b_a, lru_w_x[0], b_x, lam, seq)
    dx1, dss1, dg1 = _norm_bwd("norm1_bwd", [(dproj1, 0)], lru_in_t, x1, norm_g[1:2], scale[1], dx2, seq)
    (p_w_out1,) = _dw("dw_out1", yg1, [dy1])
    (p_lru_in,) = _dw("dw_lru_in", h1, [dproj1], blocked=2 * D_MODEL // N_DEV)

    rows_out = D_MODEL // N_DEV
    gpack1, gmeta1 = _pack_rows([dcw[0:4], dvec[0:4], dg1[0], dfinal_rows[0]])
    dwax = jnp.stack([dw_a, dw_x]).astype(BF16)
    own1 = [gpack1, dwax, p_lru_in, p_w_out1.reshape(N_DEV, rows_out, D_MODEL)]
    grads1_handle, grads1_token = _exchange_start("grads1_start", own1[:2], own1[2:], after=dx1)

    gmod0 = gmod[0] + grads1_token[0:1, 0:1]
    dy0, dgm0, du_a, du_b, dgate = _out_proj_bwd("out_proj0_bwd", dx1, gmod0, y0, w_out0_t, seq,
                                                  attn=(a_out, b_out, qkvg))
    dq_a, dkv_a, dbias, dsink = _swa_bwd(qkvg, du_a, a_out, lse_a, bias, sinks, seq)
    dq_b, dk_b, dv_b, df4 = _fox_bwd_t(q_aug, k_aug, kt_aug, qkvg, du_b, b_out, lse_b, seq)
    dfl_t, db_f = _fox_post(df4.reshape(N_HEADS, t_tok), fl_t, b_f, seq)
    parts0 = [(dq_b, C_BQ), (dk_b, C_BK), (dv_b, C_BV), (dq_a, C_AQ), (dgate, C_GATE), (dkv_a, C_AK)]
    (p_w_out0,) = _dw("dw_out0", yg0, [dy0])
    pw_bq, pw_bk, pw_bv, pw_aq, pw_gate, pw_akv = _dw("dw_attn_in", h0, [p for p, _ in parts0])
    pw_f = _dw_rows("dw_f", dfl_t, h0)

    p_w_in = jnp.concatenate([pw_aq, pw_akv, pw_bq, pw_bk, pw_bv, jnp.transpose(pw_f).astype(BF16), pw_gate], axis=1)
    p_w_in = jnp.transpose(p_w_in.reshape(D_MODEL, N_DEV, SHARD_W_IN), (1, 0, 2))
    p_w_in = jnp.pad(p_w_in, ((0, 0), (0, 0), (0, SHARD_W_PAD - SHARD_W_IN)))
    own0 = [p_w_in, p_w_out0.reshape(N_DEV, rows_out, D_MODEL)]
    landed1 = _exchange_wait("grads1_wait", grads1_handle, after=p_w_in)
    grads0_handle, grads0_token = _exchange_start("grads0_start", [], own0, after=landed1[0])
    scale0 = scale[0] + grads0_token[0:1, 0:1]
    dx0, dss0, dg0 = _norm_bwd("norm0_bwd", parts0, w_main_t, x0, norm_g[0:1], scale0, dx1, seq,
                               rows_part=(dfl_t, wf_t))
    dbias_t = _bias_reduce(dbias.reshape(N_HEADS, BLOCK * 2 * BLOCK), onehot)

    gpack0, gmeta0 = _pack_rows([jnp.transpose(dbias_t), dg0[0], dsink[:, 0], db_f[:, 0], loss_rows[0]])
    dmod = jnp.stack([jnp.concatenate([dss[:, 0], dss[:, 1], dgm[:, 0]], axis=1)
                      for dss, dgm in ((dss0, dgm0), (dss1, dgm1))], axis=1)
    g_small0, g_dmod = _exchange("exchange_small", [gpack0, dmod], [])
    landed0 = _exchange_wait("grads0_wait", grads0_handle, after=g_small0)
    r_w_in, r_w_out0 = (_with_own(g, lax.dynamic_index_in_dim(a, me, 0, keepdims=False), me)
                        for g, a in zip(landed0, own0))
    g_small1, g_dwax = (_with_own(g, a, me) for g, a in zip(landed1[:2], own1[:2]))
    r_lru_in, r_w_out1 = (_with_own(g, lax.dynamic_index_in_dim(a, me, 0, keepdims=False), me)
                          for g, a in zip(landed1[2:], own1[2:]))

    d_rel, d_g0, d_sinks, d_b_f, loss_cols = _unpack_rows(_sum_leading("sum_small0", g_small0), gmeta0)
    loss = jnp.sum(loss_cols)
    d_cw, d_vec, d_g1, d_final_g = _unpack_rows(_sum_leading("sum_small1", g_small1), gmeta1)
    d_norm_g = jnp.stack([d_g0, d_g1])
    d_wax = _sum_leading("sum_dwax", g_dwax.reshape(N_DEV, 2 * LRU_BLOCKS * LRU_BLOCK_W, LRU_BLOCK_W))
    d_wa, d_wx = d_wax[:LRU_BLOCKS * LRU_BLOCK_W], d_wax[LRU_BLOCKS * LRU_BLOCK_W:]
    cols = lambda a: lax.dynamic_slice(a, (0, me * LRU_BLOCK_W), (a.shape[0], LRU_BLOCK_W))
    dmod_all = g_dmod.reshape(N_DEV * nseq, 2 * 3 * D_MODEL)
    d_ada_b = _sum_leading("sum_ada_b", dmod_all.reshape(N_DEV * nseq, 2 * 3 * D_MODEL // 128, 128)).reshape(2, 3 * D_MODEL)
    dmod_slice = lax.dynamic_slice(dmod_all.reshape(N_DEV * nseq, 2, N_DEV, ncol), (0, 0, me, 0),
                                   (N_DEV * nseq, 2, 1, ncol)).reshape(N_DEV * nseq, 2, ncol)
    d_ada_w = _ada_w_grad(c_all, jnp.transpose(dmod_slice, (1, 0, 2)))

    given = dict(
        rel_bias=(rel_bias, m_rel_bias, v_rel_bias), norm_g=(norm_g, m_norm_g, v_norm_g),
        ada_w=(ada_w, m_ada_w, v_ada_w), ada_b=(ada_b, m_ada_b, v_ada_b),
        attn_w_in=(attn_w_in, m_attn_w_in, v_attn_w_in), attn_sinks=(attn_sinks, m_attn_sinks, v_attn_sinks),
        attn_b_f=(attn_b_f, m_attn_b_f, v_attn_b_f), attn_w_out=(attn_w_out, m_attn_w_out, v_attn_w_out),
        lru_w_in=(lru_w_in, m_lru_w_in, v_lru_w_in), lru_conv_w=(lru_conv_w, m_lru_conv_w, v_lru_conv_w),
        lru_conv_b=(lru_conv_b, m_lru_conv_b, v_lru_conv_b), lru_w_a=(lru_w_a, m_lru_w_a, v_lru_w_a),
        lru_b_a=(lru_b_a, m_lru_b_a, v_lru_b_a), lru_w_x=(lru_w_x, m_lru_w_x, v_lru_w_x),
        lru_b_x=(lru_b_x, m_lru_b_x, v_lru_b_x), lru_lambda=(lru_lambda, m_lru_lambda, v_lru_lambda),
        lru_w_out=(lru_w_out, m_lru_w_out, v_lru_w_out), final_g=(final_g, m_final_g, v_final_g))
    results = {}

    def big(name, shape2d, g=None, parts=None):
        w, m, v = (a.reshape(shape2d) for a in given[name])
        outs = _adamw("adamw_" + name, w, m, v, g=g, parts=parts)
        results[name] = tuple(o.reshape(given[name][0].shape) for o in outs)

    big("ada_w", (2 * D_MODEL, ncol), g=d_ada_w.reshape(2 * D_MODEL, ncol))
    big("attn_w_in", (D_MODEL, SHARD_W_IN), parts=r_w_in)
    big("attn_w_out", (rows_out, D_MODEL), parts=r_w_out0)
    big("lru_w_in", (D_MODEL, 2 * D_MODEL // N_DEV), parts=r_lru_in)
    big("lru_w_out", (rows_out, D_MODEL), parts=r_w_out1)

    small_grads = dict(
        rel_bias=d_rel, norm_g=d_norm_g, ada_b=d_ada_b, attn_sinks=d_sinks.reshape(1, N_HEADS),
        attn_b_f=d_b_f.reshape(1, N_HEADS), lru_conv_w=cols(d_cw).reshape(1, 4, LRU_BLOCK_W),
        lru_conv_b=cols(d_vec[0:1]), lru_w_a=d_wa.reshape(lru_w_a.shape), lru_b_a=cols(d_vec[1:2]),
        lru_w_x=d_wx.reshape(lru_w_x.shape), lru_b_x=cols(d_vec[2:3]), lru_lambda=cols(d_vec[3:4]),
        final_g=d_final_g)
    small = [n for n in WEIGHTS if n not in BIG]
    as2d = lambda a: a.reshape(-1, a.shape[-1])
    outs = _adamw_many("adamw_small", [tuple(as2d(a) for a in given[n]) + (as2d(small_grads[n]),) for n in small])
    for n, group in zip(small, outs):
        results[n] = tuple(o.reshape(given[n][0].shape) for o in group)

    grad_x = dx0.reshape(x.shape)
    out = [loss, grad_x]
    for j in range(4):
        out += [results[n][j] for n in WEIGHTS]
    return tuple(out)
```

```python
import functools
import math

import jax
import jax.numpy as jnp
from jax import lax
from jax.experimental import pallas as pl
from jax.experimental.pallas import tpu as pltpu

F32 = jnp.float32
BF16 = jnp.bfloat16
HI = lax.Precision.HIGHEST
MESH = pl.DeviceIdType.MESH

N_DEV = 8
D_MODEL = 1024
HEAD_DIM = 64
N_HEADS = 8
KV_GROUP = 4
BLOCK = 128
REL_BUCKETS = 32
REL_MAX_EXACT = 16
REL_MAX_DIST = 128
LRU_BLOCKS = 8
LRU_BLOCK_W = 128
LRU_C = 8.0
EPS = 1e-6
SCALE = HEAD_DIM ** -0.5
NEG = -1e30

ADAM_LR = 0.001
ADAM_B1 = 0.9
ADAM_B2 = 0.999
ADAM_EPS = 1e-08
ADAM_WD = 0.01
ADAM_STEP = 10

C_BQ, C_BK, C_BV, C_AQ, C_GATE, C_AK, C_AV = 0, 512, 1024, 1536, 2048, 3072, 3200
N_MAIN = 3328
SHARD_W_IN = 417
SHARD_W_PAD = 512

TM = 512
TQ = 256
TK = 128
TKB = 256
TC = 512
SWA_SUB = 2
VMEM_BIG = 56 * 1024 * 1024
VMEM_MID = 40 * 1024 * 1024


def _pallas(body, **kw):
    return pl.pallas_call(body, **kw)


def _cp(sem=None, vmem=None):
    kw = {}
    if sem is not None:
        kw["dimension_semantics"] = sem
    if vmem is not None:
        kw["vmem_limit_bytes"] = vmem
    return pltpu.CompilerParams(**kw)


def _nn(a, b, precision=None):
    return jnp.dot(a, b, preferred_element_type=F32, precision=precision)


def _nt(a, b, precision=None):
    return lax.dot_general(a, b, (((1,), (1,)), ((), ())), preferred_element_type=F32, precision=precision)


def _tn(a, b, precision=None):
    return lax.dot_general(a, b, (((0,), (0,)), ((), ())), preferred_element_type=F32, precision=precision)


def _sigmoid(x):
    return 1.0 / (1.0 + jnp.exp(-x))


def _silu(x):
    return x * _sigmoid(x)


def _dsilu(x):
    s = _sigmoid(x)
    return s * (1.0 + x * (1.0 - s))


def _col(tile, idx):
    lane = lax.broadcasted_iota(jnp.int32, tile.shape, 1)
    return jnp.sum(jnp.where(lane == idx, tile, 0.0), axis=1, keepdims=True)


def _exchange(name, gathers, scatters, axes=("x", "y", "c"), chunks=1):
    ng, n = len(gathers), len(gathers) + len(scatters)
    ins = list(gathers) + list(scatters)
    group = 2 ** len(axes)

    def body(*refs):
        in_refs, out_refs = refs[:n], refs[n:2 * n]
        send_sems, recv_sems, loc_sems = refs[2 * n:]
        coord = {a: lax.axis_index(a) for a in ("x", "y", "c")}

        def member(r):
            pc = dict(coord)
            idx = 0
            for k, a in enumerate(axes):
                if r & (1 << (len(axes) - 1 - k)):
                    pc[a] = 1 - coord[a]
                idx = 2 * idx + pc[a]
            return (pc["x"], pc["y"], pc["c"]), idx

        _, me = member(0)

        def peer(r):
            return member(r)

        local, sends, recvs = [], [], []
        for k in range(n):
            mine = in_refs[k] if k < ng else in_refs[k].at[me]
            cp = pltpu.make_async_copy(mine, out_refs[k].at[me], loc_sems.at[k])
            cp.start()
            local.append(cp)
            lead = mine.shape[0]
            nchunk = max(q for q in range(1, chunks + 1) if lead % q == 0)
            step = lead // nchunk
            for r in range(1, group):
                pid, pidx = peer(r)
                src = in_refs[k] if k < ng else in_refs[k].at[pidx]
                for q in range(nchunk):
                    rows = pl.ds(q * step, step)
                    sems = dict(send_sem=send_sems.at[r - 1, k, q], recv_sem=recv_sems.at[r - 1, k, q],
                                device_id=pid, device_id_type=MESH)
                    snd = pltpu.make_async_remote_copy(src_ref=src.at[rows], dst_ref=out_refs[k].at[me].at[rows], **sems)
                    snd.start()
                    sends.append(snd)
                    recvs.append(pltpu.make_async_remote_copy(
                        src_ref=src.at[rows], dst_ref=out_refs[k].at[pidx].at[rows], **sems))
        for rc in recvs:
            rc.wait_recv()
        for snd in sends:
            snd.wait_send()
        for cp in local:
            cp.wait()

    out_shape = [jax.ShapeDtypeStruct((group,) + a.shape, a.dtype) for a in gathers]
    out_shape += [jax.ShapeDtypeStruct(a.shape, a.dtype) for a in scatters]
    any_spec = pl.BlockSpec(memory_space=pl.ANY)
    return _pallas(
        body, name=name, out_shape=out_shape,
        in_specs=[any_spec] * n, out_specs=[any_spec] * n,
        scratch_shapes=[pltpu.SemaphoreType.DMA((group - 1, n, chunks)), pltpu.SemaphoreType.DMA((group - 1, n, chunks)),
                        pltpu.SemaphoreType.DMA((n,))],
    )(*ins)


def _peer_of(r):
    x, y, c = lax.axis_index("x"), lax.axis_index("y"), lax.axis_index("c")
    px = 1 - x if r & 4 else x
    py = 1 - y if r & 2 else y
    pc = 1 - c if r & 1 else c
    return (px, py, pc), 4 * px + 2 * py + pc


def _split_copies(in_refs, land_refs, send_sems, recv_sems, ng, with_recv):
    _, me = _peer_of(0)
    pairs = []
    for k, (src_ref, land) in enumerate(zip(in_refs, land_refs)):
        for r in range(1, N_DEV):
            pid, pidx = _peer_of(r)
            src = src_ref if k < ng else src_ref.at[pidx]
            slot = (N_DEV - 1) * k + r - 1
            sems = dict(send_sem=send_sems.at[slot], recv_sem=recv_sems.at[slot], device_id=pid, device_id_type=MESH)
            send = pltpu.make_async_remote_copy(src_ref=src, dst_ref=land.at[me], **sems)
            recv = pltpu.make_async_remote_copy(src_ref=src, dst_ref=land.at[pidx], **sems) if with_recv else None
            pairs.append((send, recv))
    return pairs


def _exchange_start(name, gathers, scatters, after):
    ng, n = len(gathers), len(gathers) + len(scatters)
    ins = list(gathers) + list(scatters)
    lands = [jax.ShapeDtypeStruct((N_DEV,) + a.shape, a.dtype) for a in gathers]
    lands += [jax.ShapeDtypeStruct(a.shape, a.dtype) for a in scatters]

    def body(*refs):
        in_refs, land_refs = refs[:n], refs[n:2 * n]
        send_sems, recv_sems = refs[2 * n + 1:2 * n + 3]
        token = refs[-1]
        for send, _ in _split_copies(in_refs, land_refs, send_sems, recv_sems, ng, False):
            send.start()
        token[...] = jnp.zeros_like(token)

    hbm = pl.BlockSpec(memory_space=pltpu.HBM)
    sem = pl.BlockSpec(memory_space=pltpu.SEMAPHORE)
    sem_shape = pltpu.SemaphoreType.DMA(((N_DEV - 1) * n,))
    out_shape = [sem_shape, sem_shape] + [pltpu.HBM(a.shape, a.dtype) for a in ins]
    out_shape += [pltpu.HBM(l.shape, l.dtype) for l in lands] + [jax.ShapeDtypeStruct((8, 128), F32)]
    args = [pltpu.with_memory_space_constraint(a, pltpu.HBM) for a in ins]
    args += [pltpu.with_memory_space_constraint(lax.empty(l.shape, l.dtype), pltpu.HBM) for l in lands]
    outs = _pallas(
        body, name=name, out_shape=out_shape,
        in_specs=[hbm] * (2 * n) + [pl.BlockSpec(memory_space=pl.ANY)],
        out_specs=[sem, sem] + [hbm] * (2 * n) + [pl.BlockSpec(memory_space=pltpu.VMEM)],
        input_output_aliases={i: 2 + i for i in range(2 * n)},
        compiler_params=pltpu.CompilerParams(has_side_effects=pltpu.SideEffectType.DATAFLOW_SIDE_EFFECTING),
    )(*args, after)
    return (outs[0], outs[1], list(outs[2:2 + n]), list(outs[2 + n:2 + 2 * n]), ng), outs[-1]


def _exchange_wait(name, handle, after):
    send_sems, recv_sems, srcs, lands, ng = handle
    n = len(srcs)

    def body(*refs):
        in_refs, land_refs = refs[:n], refs[n:2 * n]
        send_ref, recv_ref = refs[2 * n:2 * n + 2]
        for send, recv in _split_copies(in_refs, land_refs, send_ref, recv_ref, ng, True):
            send.wait_send()
            recv.wait_recv()

    hbm = pl.BlockSpec(memory_space=pltpu.HBM)
    sem = pl.BlockSpec(memory_space=pltpu.SEMAPHORE)
    outs = _pallas(
        body, name=name, out_shape=[pltpu.HBM(a.shape, a.dtype) for a in srcs + lands],
        in_specs=[hbm] * (2 * n) + [sem, sem, pl.BlockSpec(memory_space=pl.ANY)],
        out_specs=[hbm] * (2 * n), input_output_aliases={i: i for i in range(2 * n)},
        compiler_params=pltpu.CompilerParams(has_side_effects=pltpu.SideEffectType.DATAFLOW_SIDE_EFFECTING),
    )(*srcs, *lands, send_sems, recv_sems, after)
    return list(outs[n:])


def _with_own(land, own, me):
    return lax.dynamic_update_slice(land, own[None], (me,) + (0,) * own.ndim)


def _ada_mod(c_all, ada_w, ada_b_slice):
    def body(c_ref, w_ref, b_ref, o_ref):
        ca = _silu(c_ref[...])
        for l in range(2):
            o_ref[l] = _nn(ca, w_ref[l], HI) + b_ref[l]

    return _pallas(body, name="ada_mod",
                   out_shape=jax.ShapeDtypeStruct((2, c_all.shape[0], ada_w.shape[2]), F32),
                   compiler_params=_cp(vmem=VMEM_MID))(c_all, ada_w, ada_b_slice)


def _ada_w_grad(c_all, dmod_slice):
    def body(c_ref, d_ref, o_ref):
        ca = _silu(c_ref[...])
        for l in range(2):
            o_ref[l] = _tn(ca, d_ref[l], HI)

    return _pallas(body, name="ada_w_grad",
                   out_shape=jax.ShapeDtypeStruct((2, D_MODEL, dmod_slice.shape[2]), F32),
                   compiler_params=_cp(vmem=VMEM_MID))(c_all, dmod_slice)


def _bucket_onehot():
    qi = jnp.arange(BLOCK)[:, None]
    kj = jnp.arange(2 * BLOCK)[None, :]
    rel = qi - kj + BLOCK
    n = jnp.maximum(rel, 0)
    nf = jnp.maximum(n, 1).astype(F32)
    large = REL_MAX_EXACT + (jnp.log(nf / REL_MAX_EXACT) / math.log(REL_MAX_DIST / REL_MAX_EXACT)
                             * (REL_BUCKETS - REL_MAX_EXACT)).astype(jnp.int32)
    large = jnp.minimum(large, REL_BUCKETS - 1)
    bucket = jnp.where(n < REL_MAX_EXACT, n, large).reshape(1, BLOCK * 2 * BLOCK)
    return (jnp.arange(REL_BUCKETS)[:, None] == bucket).astype(F32)


def _bias_expand(rel_bias_t, onehot):
    def body(r_ref, e_ref, o_ref):
        o_ref[...] = _nn(r_ref[...], e_ref[...], HI)

    return _pallas(body, name="bias_expand",
                   out_shape=jax.ShapeDtypeStruct((N_HEADS, onehot.shape[1]), F32),
                   compiler_params=_cp(vmem=VMEM_MID))(rel_bias_t, onehot)


def _bias_reduce(dbias, onehot):
    def body(d_ref, e_ref, o_ref):
        o_ref[...] = _nt(d_ref[...], e_ref[...], HI)

    return _pallas(body, name="bias_reduce",
                   out_shape=jax.ShapeDtypeStruct((N_HEADS, REL_BUCKETS), F32),
                   compiler_params=_cp(vmem=VMEM_MID))(dbias, onehot)


def _norm_proj(name, x, g, shift, scale, w, seq, out_dtype, wf_t=None):
    t_tok = x.shape[0]
    w3d = w.ndim == 3
    n_out = w.shape[0] * w.shape[2] if w3d else w.shape[1]
    cn = w.shape[2] if w3d else 256

    def body(x_ref, g_ref, sh_ref, sc_ref, w_ref, *rest):
        if wf_t is not None:
            wf_ref, h_ref, o_ref, fl_ref = rest
        else:
            h_ref, o_ref = rest
        xv = x_ref[...]
        rstd = lax.rsqrt(jnp.mean(xv * xv, axis=-1, keepdims=True) + EPS)
        h = (xv * rstd) * g_ref[...] * (1.0 + sc_ref[...]) + sh_ref[...]
        hb = h.astype(BF16)
        h_ref[...] = hb
        for j in range(n_out // cn):
            wj = w_ref[j] if w3d else w_ref[:, j * cn:(j + 1) * cn]
            o_ref[:, j * cn:(j + 1) * cn] = _nn(hb, wj).astype(out_dtype)
        if wf_t is not None:
            fl_ref[...] = _nt(wf_ref[...], hb)

    mod_spec = pl.BlockSpec((None, 1, D_MODEL), lambda i: (i * TM // seq, 0, 0))
    w_spec = (pl.BlockSpec(w.shape, lambda i: (0, 0, 0)) if w3d else pl.BlockSpec(w.shape, lambda i: (0, 0)))
    in_specs = [pl.BlockSpec((TM, D_MODEL), lambda i: (i, 0)), pl.BlockSpec((1, D_MODEL), lambda i: (0, 0)),
                mod_spec, mod_spec, w_spec]
    out_shape = [jax.ShapeDtypeStruct((t_tok, D_MODEL), BF16), jax.ShapeDtypeStruct((t_tok, n_out), out_dtype)]
    out_specs = [pl.BlockSpec((TM, D_MODEL), lambda i: (i, 0)), pl.BlockSpec((TM, n_out), lambda i: (i, 0))]
    args = [x, g, shift, scale, w]
    if wf_t is not None:
        in_specs.append(pl.BlockSpec(wf_t.shape, lambda i: (0, 0)))
        out_shape.append(jax.ShapeDtypeStruct((wf_t.shape[0], t_tok), F32))
        out_specs.append(pl.BlockSpec((wf_t.shape[0], TM), lambda i: (0, i)))
        args.append(wf_t)
    return _pallas(body, name=name, grid=(t_tok // TM,), in_specs=in_specs, out_specs=out_specs,
                   out_shape=out_shape, compiler_params=_cp(("arbitrary",), VMEM_BIG))(*args)


def _fox_prep(fl_t, b_f, seq):
    t_tok = fl_t.shape[1]
    ch = 256

    def body(fl_ref, bf_ref, fr_ref, fc_ref):
        z = fl_ref[...] + bf_ref[...]
        logf = jnp.minimum(z, 0.0) - jnp.log(1.0 + jnp.exp(-jnp.abs(z)))
        ri = lax.broadcasted_iota(jnp.int32, (ch, ch), 0)
        ci = lax.broadcasted_iota(jnp.int32, (ch, ch), 1)
        upper = (ri <= ci).astype(F32)
        eye = (ri == ci).astype(F32)
        carry = jnp.zeros((N_HEADS, 1), F32)
        for k in range(seq // ch):
            fk = _nn(logf[:, k * ch:(k + 1) * ch], upper, HI) + carry
            carry = fk[:, ch - 1:ch]
            fr_ref[:, k * ch:(k + 1) * ch] = fk
            padded = jnp.concatenate([fk, jnp.zeros((128 - N_HEADS, ch), F32)], axis=0)
            fc_ref[k * ch:(k + 1) * ch, :] = _nt(eye, padded, HI)

    return _pallas(
        body, name="fox_prep", grid=(t_tok // seq,),
        in_specs=[pl.BlockSpec((N_HEADS, seq), lambda b: (0, b)), pl.BlockSpec((N_HEADS, 1), lambda b: (0, 0))],
        out_specs=[pl.BlockSpec((N_HEADS, seq), lambda b: (0, b)), pl.BlockSpec((seq, 128), lambda b: (b, 0))],
        out_shape=[jax.ShapeDtypeStruct((N_HEADS, t_tok), F32), jax.ShapeDtypeStruct((t_tok, 128), F32)],
        compiler_params=_cp(("arbitrary",), VMEM_MID))(fl_t, b_f)


def _fox_post(df_row, fl_t, b_f, seq):
    t_tok = fl_t.shape[1]
    ch = 256

    def body(d_ref, fl_ref, bf_ref, o_ref, db_ref):
        @pl.when(pl.program_id(0) == 0)
        def _():
            db_ref[...] = jnp.zeros_like(db_ref)

        z = fl_ref[...] + bf_ref[...]
        sig_neg = 1.0 / (1.0 + jnp.exp(z))
        ri = lax.broadcasted_iota(jnp.int32, (ch, ch), 0)
        ci = lax.broadcasted_iota(jnp.int32, (ch, ch), 1)
        lower = (ri >= ci).astype(F32)
        carry = jnp.zeros((N_HEADS, 1), F32)
        tot = jnp.zeros((N_HEADS, 1), F32)
        for k in reversed(range(seq // ch)):
            dk = _nn(d_ref[:, k * ch:(k + 1) * ch], lower, HI) + carry
            carry = dk[:, 0:1]
            dfl = dk * sig_neg[:, k * ch:(k + 1) * ch]
            o_ref[:, k * ch:(k + 1) * ch] = dfl
            tot = tot + jnp.sum(dfl, axis=1, keepdims=True)
        db_ref[...] += jnp.broadcast_to(tot, db_ref.shape)

    return _pallas(
        body, name="fox_post", grid=(t_tok // seq,),
        in_specs=[pl.BlockSpec((N_HEADS, seq), lambda b: (0, b)), pl.BlockSpec((N_HEADS, seq), lambda b: (0, b)),
                  pl.BlockSpec((N_HEADS, 1), lambda b: (0, 0))],
        out_specs=[pl.BlockSpec((N_HEADS, seq), lambda b: (0, b)), pl.BlockSpec((N_HEADS, 128), lambda b: (0, 0))],
        out_shape=[jax.ShapeDtypeStruct((N_HEADS, t_tok), F32), jax.ShapeDtypeStruct((N_HEADS, 128), F32)],
        compiler_params=_cp(("arbitrary",), VMEM_MID))(df_row, fl_t, b_f)


def _eye(n, dtype):
    return (lax.broadcasted_iota(jnp.int32, (n, n), 0) == lax.broadcasted_iota(jnp.int32, (n, n), 1)).astype(dtype)


def _fox_aug(qkvg, f_col, seq):
    t_tok = qkvg.shape[0]
    ta = 256
    nkb = ta // TK

    def body(q_ref, k_ref, v_ref, fc_ref, qa_ref, ka_ref, kt_ref, vt_ref):
        ri = lax.broadcasted_iota(jnp.int32, (128, 128), 0)
        ci = lax.broadcasted_iota(jnp.int32, (128, 128), 1)
        eye = (ri == ci).astype(BF16)
        lane = lax.broadcasted_iota(jnp.int32, (ta, 128), 1)
        ones_q = jnp.where(jnp.logical_and(lane >= 64, lane < 67), 1.0, 0.0)
        ones_k = jnp.where(jnp.logical_and(lane >= 67, lane < 70), 1.0, 0.0)
        fc_tile = fc_ref[...]
        for p in range(N_HEADS // 2):
            q2 = q_ref[:, 128 * p:128 * (p + 1)]
            k2 = k_ref[:, 128 * p:128 * (p + 1)]
            vt = _nt(eye, v_ref[:, 128 * p:128 * (p + 1)]).astype(BF16)
            for kk in range(nkb):
                vt_ref[p, kk] = vt[:, kk * TK:(kk + 1) * TK]
            for e in range(2):
                h = 2 * p + e
                sel = jnp.logical_and(ri == ci + HEAD_DIM * e, ci < HEAD_DIM)
                f = _col(fc_tile, h)
                fh = f.astype(BF16).astype(F32)
                fm = (f - fh).astype(BF16).astype(F32)
                fl = (f - fh - fm).astype(BF16).astype(F32)
                qa = (_nn(q2, jnp.where(sel, SCALE, 0.0).astype(BF16)) + ones_q + jnp.where(lane == 67, fh, 0.0)
                      + jnp.where(lane == 68, fm, 0.0) + jnp.where(lane == 69, fl, 0.0))
                ka = (_nn(k2, jnp.where(sel, 1.0, 0.0).astype(BF16)) + ones_k - jnp.where(lane == 64, fh, 0.0)
                      - jnp.where(lane == 65, fm, 0.0) - jnp.where(lane == 66, fl, 0.0))
                qa_ref[h] = qa.astype(BF16)
                kab = ka.astype(BF16)
                ka_ref[h] = kab
                kt = _nt(eye, kab).astype(BF16)
                for kk in range(ta // TKB):
                    kt_ref[h, kk] = kt[:, kk * TKB:(kk + 1) * TKB]

    aug = jax.ShapeDtypeStruct((N_HEADS, t_tok, 128), BF16)
    return _pallas(
        body, name="fox_aug", grid=(t_tok // ta,),
        in_specs=[pl.BlockSpec((ta, 512), lambda i: (i, C_BQ // 512)), pl.BlockSpec((ta, 512), lambda i: (i, C_BK // 512)),
                  pl.BlockSpec((ta, 512), lambda i: (i, C_BV // 512)), pl.BlockSpec((ta, 128), lambda i: (i, 0))],
        out_specs=[pl.BlockSpec((N_HEADS, ta, 128), lambda i: (0, i, 0)), pl.BlockSpec((N_HEADS, ta, 128), lambda i: (0, i, 0)),
                   pl.BlockSpec((N_HEADS, ta // TKB, 128, TKB), lambda i: (0, i, 0, 0)),
                   pl.BlockSpec((N_HEADS // 2, nkb, 128, TK), lambda i: (0, i, 0, 0))],
        out_shape=[aug, aug, jax.ShapeDtypeStruct((N_HEADS, t_tok // TKB, 128, TKB), BF16),
                   jax.ShapeDtypeStruct((N_HEADS // 2, t_tok // TK, 128, TK), BF16)],
        compiler_params=_cp(("arbitrary",), VMEM_MID))(qkvg, qkvg, qkvg, f_col)


def _fox_fwd_t(q_aug, k_aug, vt, seq):
    t_tok = k_aug.shape[1]
    nq = seq // TQ
    ratio = TQ // TK
    assert ratio == 2, "the two pipeline slots are addressed by the key block's parity"

    def body(qa_ref, ka_ref, vt_ref, o_ref, lse_ref, ml_s, acc_s, st_s, p_s, al_s, qt_s):
        i = pl.program_id(1)
        tpos = i * TQ + lax.broadcasted_iota(jnp.int32, (1, TQ), 1)
        eye = _eye(HEAD_DIM, BF16)
        eye2 = _eye(128, BF16)
        for h in range(N_HEADS):
            qt_s[h] = _nt(eye2, qa_ref[h]).astype(BF16)
            ml_s[0, h] = jnp.full((1, TQ), NEG, F32)
            ml_s[1, h] = jnp.zeros((1, TQ), F32)
            acc_s[h] = jnp.zeros((HEAD_DIM, TQ), F32)
            p_s[1, h] = jnp.zeros((TK, TQ), BF16)
            al_s[1, h] = jnp.ones((1, TQ), F32)

        def scores(j, slot):
            row0 = pl.multiple_of(j * TK, TK)
            for h in range(N_HEADS):
                st_s[slot, h] = _nn(ka_ref[h, pl.ds(row0, TK), :], qt_s[h])

        def softmax(j, slot, masked):
            if masked:
                keep = (j * TK + lax.broadcasted_iota(jnp.int32, (TK, 1), 0)) <= tpos
            for h in range(N_HEADS):
                st = st_s[slot, h]
                if masked:
                    st = jnp.where(keep, st, NEG)
                m = ml_s[0, h]
                m_new = jnp.maximum(m, jnp.max(st, axis=0, keepdims=True))
                alpha = jnp.exp(m - m_new)
                pe = jnp.exp(st - m_new)
                ml_s[0, h] = m_new
                ml_s[1, h] = alpha * ml_s[1, h] + jnp.sum(pe, axis=0, keepdims=True)
                al_s[slot, h] = alpha
                p_s[slot, h] = pe.astype(BF16)

        def values(j, slot):
            jv = jnp.maximum(j, 0)
            for h in range(N_HEADS):
                p, e = divmod(h, 2)
                acc_s[h] = al_s[slot, h] * acc_s[h] + _nn(vt_ref[p, jv, e * HEAD_DIM:(e + 1) * HEAD_DIM, :], p_s[slot, h])

        def step(m, carry):
            for kk in range(ratio):
                j = ratio * m + kk
                values(j - 1, 1 - kk)
                softmax(j, kk, False)
                scores(j + 1, 1 - kk)
            return carry

        scores(0, 0)
        lax.fori_loop(0, i, step, 0)
        for kk in range(ratio):
            j = ratio * i + kk
            values(j - 1, 1 - kk)
            softmax(j, kk, True)
            if kk < ratio - 1:
                scores(j + 1, 1 - kk)
        values(ratio * i + ratio - 1, ratio - 1)
        for p in range(N_HEADS // 2):
            outs = []
            for e in range(2):
                h = 2 * p + e
                l = ml_s[1, h]
                outs.append(_tn((acc_s[h] / l).astype(BF16), eye))
                lse_ref[p, e:e + 1, :] = ml_s[0, h] + jnp.log(l)
            o_ref[:, 128 * p:128 * (p + 1)] = jnp.concatenate(outs, axis=1).astype(BF16)

    return _pallas(
        body, name="fox_fwd", grid=(t_tok // seq, nq),
        in_specs=[pl.BlockSpec((N_HEADS, TQ, 128), lambda b, i: (0, b * nq + i, 0)),
                  pl.BlockSpec((N_HEADS, seq, 128), lambda b, i: (0, b, 0)),
                  pl.BlockSpec((N_HEADS // 2, seq // TK, 128, TK), lambda b, i: (0, b, 0, 0))],
        out_specs=[pl.BlockSpec((TQ, 512), lambda b, i: (b * nq + i, 0)),
                   pl.BlockSpec((N_HEADS // 2, 2, TQ), lambda b, i: (0, 0, b * nq + i))],
        out_shape=[jax.ShapeDtypeStruct((t_tok, 512), BF16), jax.ShapeDtypeStruct((N_HEADS // 2, 2, t_tok), F32)],
        scratch_shapes=[pltpu.VMEM((2, N_HEADS, 1, TQ), F32), pltpu.VMEM((N_HEADS, HEAD_DIM, TQ), F32),
                        pltpu.VMEM((2, N_HEADS, TK, TQ), F32), pltpu.VMEM((2, N_HEADS, TK, TQ), BF16),
                        pltpu.VMEM((2, N_HEADS, 1, TQ), F32), pltpu.VMEM((N_HEADS, 128, TQ), BF16)],
        compiler_params=_cp(("arbitrary", "arbitrary"), VMEM_MID))(q_aug, k_aug, vt)


def _fox_bwd_t(q_aug, k_aug, kt, qkvg, du_b, b_out, lse, seq):
    TK = TKB
    t_tok = qkvg.shape[0]
    nq = seq // TQ
    nkb = seq // TK
    ratio = TQ // TK
    hg = 4

    def body(qa_ref, ka_ref, kt_ref, v_ref, do_ref, o_ref, lse_ref, dq_ref, dk_ref, dv_ref, df_ref,
             dqt_s, row_s, dfk_s, dk_s, dv_s, dot_s, st_s, dp_s, pb_s, db_s, qt_s):
        eye = _eye(HEAD_DIM, BF16)
        eye2 = _eye(128, BF16)
        eye_k = _eye(TK, F32)
        lane8 = lax.broadcasted_iota(jnp.int32, (8, 128), 1)
        lane_k = lax.broadcasted_iota(jnp.int32, (TK, 128), 1)
        first = [lane8 < HEAD_DIM, lane8 >= HEAD_DIM]
        for pp in range(hg // 2):
            for ii in range(nq):
                dot_s[pp, ii] = _nt(eye2, do_ref[ii * TQ:(ii + 1) * TQ, 128 * pp:128 * (pp + 1)]).astype(BF16)
        for hh in range(hg):
            for ii in range(nq):
                qt_s[hh, ii] = _nt(eye2, qa_ref[hh, ii * TQ:(ii + 1) * TQ, :]).astype(BF16)
        for hh in range(hg):
            pp, e = divmod(hh, 2)
            head_lanes = jnp.where(first[e], 1.0, 0.0)
            for ii in range(nq):
                rows = slice(ii * TQ, (ii + 1) * TQ)
                prod = do_ref[rows, 128 * pp:128 * (pp + 1)].astype(F32) * o_ref[rows, 128 * pp:128 * (pp + 1)].astype(F32)
                row_s[hh, ii, 0] = _nt(head_lanes, prod, HI)
                row_s[hh, ii, 1] = jnp.broadcast_to(lse_ref[pp, e:e + 1, ii * TQ:(ii + 1) * TQ], (8, TQ))
                dqt_s[hh, ii] = jnp.zeros((128, TQ), F32)

        def kblock(j, _):
            krow = pl.multiple_of(j * TK, TK)
            spos = j * TK + lax.broadcasted_iota(jnp.int32, (TK, 1), 0)
            for hh in range(hg):
                dk_s[hh] = jnp.zeros((TK, 128), F32)
                dv_s[hh] = jnp.zeros((TK, 128), F32)

            def scores(i, slot):
                for hh in range(hg):
                    pp, e = divmod(hh, 2)
                    own = (lane_k < HEAD_DIM) if e == 0 else (lane_k >= HEAD_DIM)
                    v2 = v_ref[pl.ds(krow, TK), 128 * pp:128 * (pp + 1)]
                    vj = jnp.where(own, v2, jnp.zeros_like(v2))
                    st_s[slot, hh] = _nn(ka_ref[hh, pl.ds(krow, TK), :], qt_s[hh, i])
                    dp_s[slot, hh] = _nn(vj, dot_s[pp, i])

            def elementwise(i, slot, masked):
                if masked:
                    keep = spos <= (i * TQ + lax.broadcasted_iota(jnp.int32, (1, TQ), 1))
                for hh in range(hg):
                    pt = jnp.exp(st_s[slot, hh] - row_s[hh, i, 1][0:1, :])
                    if masked:
                        pt = jnp.where(keep, pt, 0.0)
                    dst = pt * (dp_s[slot, hh] - row_s[hh, i, 0][0:1, :])
                    pb_s[slot, hh] = pt.astype(BF16)
                    db_s[slot, hh] = dst.astype(BF16)

            def grads(i, slot):
                qrow = pl.multiple_of(i * TQ, TQ)
                for hh in range(hg):
                    dst_b = db_s[slot, hh]
                    dv_s[hh] += _nn(pb_s[slot, hh], do_ref[pl.ds(qrow, TQ), 128 * (hh // 2):128 * (hh // 2 + 1)])
                    dk_s[hh] += _nn(dst_b, qa_ref[hh, pl.ds(qrow, TQ), :])
                    dqt_s[hh, i] += _nn(kt_ref[hh, j], dst_b)

            def step(p, carry):
                i = i0 + 2 * p + 1
                grads(i - 1, 0)
                elementwise(i, 1, False)
                scores(i + 1, 0)
                grads(i, 1)
                elementwise(i + 1, 0, False)
                scores(jnp.minimum(i + 2, nq - 1), 1)
                return carry

            i0 = j // ratio
            rest = nq - 1 - i0
            scores(i0, 0)
            elementwise(i0, 0, True)
            scores(jnp.minimum(i0 + 1, nq - 1), 1)
            lax.fori_loop(0, rest // 2, step, 0)

            @pl.when(rest % 2 == 1)
            def _():
                grads(nq - 2, 0)
                elementwise(nq - 1, 1, False)
                grads(nq - 1, 1)

            @pl.when(rest % 2 == 0)
            def _():
                grads(nq - 1, 0)
            for pp in range(hg // 2):
                cols = slice(128 * pp, 128 * (pp + 1))
                dk_ref[pl.ds(krow, TK), cols] = jnp.concatenate(
                    [dk_s[2 * pp][:, :HEAD_DIM], dk_s[2 * pp + 1][:, :HEAD_DIM]], axis=1).astype(BF16)
                dv_ref[pl.ds(krow, TK), cols] = jnp.where(lane_k < HEAD_DIM, dv_s[2 * pp], dv_s[2 * pp + 1]).astype(BF16)
            for hh in range(hg):
                dfk_s[hh, j] = _tn(dk_s[hh][:, HEAD_DIM:HEAD_DIM + 8], eye_k, HI)
            return 0

        lax.fori_loop(0, nkb, kblock, 0)
        for pp in range(hg // 2):
            for ii in range(nq):
                parts = []
                for e in range(2):
                    dqt = dqt_s[2 * pp + e, ii]
                    parts.append(_tn(dqt[0:HEAD_DIM, :].astype(BF16), eye) * SCALE)
                    for kk in range(ratio):
                        jj = ii * ratio + kk
                        df_ref[pp, e:e + 1, jj * TK:(jj + 1) * TK] = (dqt[67:68, kk * TK:(kk + 1) * TK]
                                                                     - dfk_s[2 * pp + e, jj][0:1, :])
                dq_ref[ii * TQ:(ii + 1) * TQ, 128 * pp:128 * (pp + 1)] = jnp.concatenate(parts, axis=1).astype(BF16)

    aug_blk = pl.BlockSpec((hg, seq, 128), lambda b, g: (g, b, 0))
    pair_blk = pl.BlockSpec((seq, 64 * hg), lambda b, g: (b, g))
    row_blk = pl.BlockSpec((hg // 2, 2, seq), lambda b, g: (g, 0, b))
    return _pallas(
        body, name="fox_bwd", grid=(t_tok // seq, N_HEADS // hg),
        in_specs=[aug_blk, aug_blk, pl.BlockSpec((hg, nkb, 128, TK), lambda b, g: (g, b, 0, 0)),
                  pl.BlockSpec((seq, 64 * hg), lambda b, g: (b, C_BV // (64 * hg) + g)), pair_blk, pair_blk, row_blk],
        out_specs=[pair_blk, pair_blk, pair_blk, row_blk],
        out_shape=[jax.ShapeDtypeStruct((t_tok, 512), BF16)] * 3
        + [jax.ShapeDtypeStruct((N_HEADS // 2, 2, t_tok), F32)],
        scratch_shapes=[pltpu.VMEM((hg, nq, 128, TQ), F32), pltpu.VMEM((hg, nq, 2, 8, TQ), F32),
                        pltpu.VMEM((hg, nkb, 8, TK), F32), pltpu.VMEM((hg, TK, 128), F32),
                        pltpu.VMEM((hg, TK, 128), F32), pltpu.VMEM((hg // 2, nq, 128, TQ), BF16),
                        pltpu.VMEM((2, hg, TK, TQ), F32), pltpu.VMEM((2, hg, TK, TQ), F32),
                        pltpu.VMEM((2, hg, TK, TQ), BF16), pltpu.VMEM((2, hg, TK, TQ), BF16),
                        pltpu.VMEM((hg, nq, 128, TQ), BF16)],
        compiler_params=_cp(("arbitrary", "arbitrary"), VMEM_BIG))(q_aug, k_aug, kt, qkvg, du_b, b_out, lse)


def _swa_window(k_ref, v_ref, n):
    prev = pl.multiple_of(jnp.maximum(n - 1, 0) * BLOCK, BLOCK)
    cur = pl.multiple_of(n * BLOCK, BLOCK)
    kwin = jnp.concatenate([k_ref[pl.ds(prev, BLOCK), :], k_ref[pl.ds(cur, BLOCK), :]], axis=0)
    vwin = jnp.concatenate([v_ref[pl.ds(prev, BLOCK), :], v_ref[pl.ds(cur, BLOCK), :]], axis=0)
    ti = lax.broadcasted_iota(jnp.int32, (BLOCK, 2 * BLOCK), 0)
    sj = lax.broadcasted_iota(jnp.int32, (BLOCK, 2 * BLOCK), 1)
    rel = ti - sj + BLOCK
    first_key = jnp.where(n > 0, 0, BLOCK)
    mask = jnp.logical_and(jnp.logical_and(rel >= 0, rel < BLOCK), sj >= first_key)
    return kwin, vwin, mask, prev, cur


def _head_cols(ref, h):
    pair = ref[:, 128 * (h // 2):128 * (h // 2 + 1)]
    return pair[:, (h % 2) * HEAD_DIM:(h % 2 + 1) * HEAD_DIM]


def _swa_logits(q_ref, kwin, bias_ref, h, mask):
    hk = h // KV_GROUP
    s = _nt(_head_cols(q_ref, h), kwin[:, hk * HEAD_DIM:(hk + 1) * HEAD_DIM]) * SCALE + bias_ref[h]
    return jnp.where(mask, s, NEG)


def _swa_fwd(qkvg, bias, sinks, seq):
    t_tok = qkvg.shape[0]
    nb = seq // BLOCK

    def body(sink_ref, q_ref, k_ref, v_ref, bias_ref, o_ref, lse_ref, s_s, p_s, den_s):
        g = pl.program_id(1)
        subs = [pl.ds(s * BLOCK, BLOCK) for s in range(SWA_SUB)]
        wins = [_swa_window(k_ref, v_ref, SWA_SUB * g + s) for s in range(SWA_SUB)]
        for s in range(SWA_SUB):
            for h in range(N_HEADS):
                s_s[s * N_HEADS + h] = _swa_logits(q_ref.at[subs[s]], wins[s][0], bias_ref, h, wins[s][2])
        lane = lax.broadcasted_iota(jnp.int32, (BLOCK, 128), 1)
        for s in range(SWA_SUB):
            lse_tile = jnp.zeros((BLOCK, 128), F32)
            for h in range(N_HEADS):
                sc = s_s[s * N_HEADS + h]
                sink = sink_ref[h]
                m = jnp.maximum(jnp.max(sc, axis=1, keepdims=True), sink)
                pe = jnp.exp(sc - m)
                den = jnp.sum(pe, axis=1, keepdims=True) + jnp.exp(sink - m)
                p_s[s * N_HEADS + h] = pe.astype(BF16)
                den_s[s * N_HEADS + h] = den
                lse_tile = jnp.where(lane == h, m + jnp.log(den), lse_tile)
            lse_ref[subs[s], :] = lse_tile
        for s in range(SWA_SUB):
            vwin = wins[s][1]
            for pr in range(N_HEADS // 2):
                outs = []
                for h in (2 * pr, 2 * pr + 1):
                    hk = h // KV_GROUP
                    outs.append(_nn(p_s[s * N_HEADS + h], vwin[:, hk * HEAD_DIM:(hk + 1) * HEAD_DIM]) / den_s[s * N_HEADS + h])
                o_ref[subs[s], 128 * pr:128 * (pr + 1)] = jnp.concatenate(outs, axis=1).astype(BF16)

    rows = SWA_SUB * BLOCK
    steps = nb // SWA_SUB
    return _pallas(
        body, name="swa_fwd", grid=(t_tok // seq, steps),
        in_specs=[pl.BlockSpec(memory_space=pltpu.SMEM),
                  pl.BlockSpec((rows, 512), lambda b, n: (b * steps + n, C_AQ // 512)),
                  pl.BlockSpec((seq, 128), lambda b, n: (b, C_AK // 128)),
                  pl.BlockSpec((seq, 128), lambda b, n: (b, C_AV // 128)),
                  pl.BlockSpec((N_HEADS, BLOCK, 2 * BLOCK), lambda b, n: (0, 0, 0))],
        out_specs=[pl.BlockSpec((rows, 512), lambda b, n: (b * steps + n, 0)),
                   pl.BlockSpec((rows, 128), lambda b, n: (b * steps + n, 0))],
        out_shape=[jax.ShapeDtypeStruct((t_tok, 512), BF16), jax.ShapeDtypeStruct((t_tok, 128), F32)],
        scratch_shapes=[pltpu.VMEM((SWA_SUB * N_HEADS, BLOCK, 2 * BLOCK), F32),
                        pltpu.VMEM((SWA_SUB * N_HEADS, BLOCK, 2 * BLOCK), BF16),
                        pltpu.VMEM((SWA_SUB * N_HEADS, BLOCK, 1), F32)],
        compiler_params=_cp(("arbitrary", "arbitrary"), VMEM_MID))(sinks, qkvg, qkvg, qkvg, bias)


def _swa_bwd(qkvg, du_a, a_out, lse, bias, sinks, seq):
    t_tok = qkvg.shape[0]
    nb = seq // BLOCK

    def body(sink_ref, q_ref, k_ref, v_ref, do_ref, o_ref, lse_ref, bias_ref,
             dq_ref, dkv_ref, dbias_ref, dsink_ref, kv_s, s_s, dp_s, pb_s, db_s):
        b, n = pl.program_id(0), pl.program_id(1)

        @pl.when(jnp.logical_and(b == 0, n == 0))
        def _():
            dbias_ref[...] = jnp.zeros_like(dbias_ref)
            dsink_ref[...] = jnp.zeros_like(dsink_ref)

        @pl.when(n == 0)
        def _():
            kv_s[...] = jnp.zeros_like(kv_s)

        subs = [pl.ds(s * BLOCK, BLOCK) for s in range(SWA_SUB)]
        wins = [_swa_window(k_ref, v_ref, SWA_SUB * n + s) for s in range(SWA_SUB)]
        for s in range(SWA_SUB):
            kwin, vwin, mask = wins[s][:3]
            for h in range(N_HEADS):
                hk = h // KV_GROUP
                s_s[s * N_HEADS + h] = _swa_logits(q_ref.at[subs[s]], kwin, bias_ref, h, mask)
                dp_s[s * N_HEADS + h] = _nt(_head_cols(do_ref.at[subs[s]], h), vwin[:, hk * HEAD_DIM:(hk + 1) * HEAD_DIM])
        for s in range(SWA_SUB):
            lse_tile = lse_ref[subs[s], :]
            do_s, o_s = do_ref.at[subs[s]], o_ref.at[subs[s]]
            for h in range(N_HEADS):
                delta = jnp.sum(_head_cols(do_s, h).astype(F32) * _head_cols(o_s, h).astype(F32), axis=1, keepdims=True)
                lse_h = _col(lse_tile, h)
                pe = jnp.exp(s_s[s * N_HEADS + h] - lse_h)
                ds = pe * (dp_s[s * N_HEADS + h] - delta)
                dbias_ref[h] += ds
                psink = jnp.exp(sink_ref[h] - lse_h)
                dsink_ref[h:h + 1, :] += jnp.broadcast_to(jnp.sum(-psink * delta, axis=0, keepdims=True), (1, 128))
                pb_s[s * N_HEADS + h] = pe.astype(BF16)
                db_s[s * N_HEADS + h] = ds.astype(BF16)
        for s in range(SWA_SUB):
            kwin, _, _, prev, cur = wins[s]
            q_s, do_s = q_ref.at[subs[s]], do_ref.at[subs[s]]
            for pr in range(N_HEADS // 2):
                dqs = []
                for h in (2 * pr, 2 * pr + 1):
                    hk = h // KV_GROUP
                    dqs.append(_nn(db_s[s * N_HEADS + h], kwin[:, hk * HEAD_DIM:(hk + 1) * HEAD_DIM]) * SCALE)
                dq_ref[subs[s], 128 * pr:128 * (pr + 1)] = jnp.concatenate(dqs, axis=1).astype(BF16)
            dks, dvs = [], []
            for hk in range(N_HEADS // KV_GROUP):
                dk = jnp.zeros((2 * BLOCK, HEAD_DIM), F32)
                dv = jnp.zeros((2 * BLOCK, HEAD_DIM), F32)
                for h in range(hk * KV_GROUP, (hk + 1) * KV_GROUP):
                    dk = dk + _tn(db_s[s * N_HEADS + h], _head_cols(q_s, h))
                    dv = dv + _tn(pb_s[s * N_HEADS + h], _head_cols(do_s, h))
                dks.append(dk * SCALE)
                dvs.append(dv)
            upd = jnp.concatenate(dks + dvs, axis=1)
            kv_s[pl.ds(prev, BLOCK), :] += upd[:BLOCK]
            kv_s[pl.ds(cur, BLOCK), :] += upd[BLOCK:]

        @pl.when(n == steps - 1)
        def _():
            dkv_ref[...] = kv_s[...].astype(BF16)

    rows = SWA_SUB * BLOCK
    steps = nb // SWA_SUB
    tile = (SWA_SUB * N_HEADS, BLOCK, 2 * BLOCK)
    return _pallas(
        body, name="swa_bwd", grid=(t_tok // seq, steps),
        in_specs=[pl.BlockSpec(memory_space=pltpu.SMEM),
                  pl.BlockSpec((rows, 512), lambda b, n: (b * steps + n, C_AQ // 512)),
                  pl.BlockSpec((seq, 128), lambda b, n: (b, C_AK // 128)),
                  pl.BlockSpec((seq, 128), lambda b, n: (b, C_AV // 128)),
                  pl.BlockSpec((rows, 512), lambda b, n: (b * steps + n, 0)),
                  pl.BlockSpec((rows, 512), lambda b, n: (b * steps + n, 0)),
                  pl.BlockSpec((rows, 128), lambda b, n: (b * steps + n, 0)),
                  pl.BlockSpec((N_HEADS, BLOCK, 2 * BLOCK), lambda b, n: (0, 0, 0))],
        out_specs=[pl.BlockSpec((rows, 512), lambda b, n: (b * steps + n, 0)),
                   pl.BlockSpec((seq, 256), lambda b, n: (b, 0)),
                   pl.BlockSpec((N_HEADS, BLOCK, 2 * BLOCK), lambda b, n: (0, 0, 0)),
                   pl.BlockSpec((N_HEADS, 128), lambda b, n: (0, 0))],
        out_shape=[jax.ShapeDtypeStruct((t_tok, 512), BF16), jax.ShapeDtypeStruct((t_tok, 256), BF16),
                   jax.ShapeDtypeStruct((N_HEADS, BLOCK, 2 * BLOCK), F32), jax.ShapeDtypeStruct((N_HEADS, 128), F32)],
        scratch_shapes=[pltpu.VMEM((seq, 256), F32), pltpu.VMEM(tile, F32), pltpu.VMEM(tile, F32),
                        pltpu.VMEM(tile, BF16), pltpu.VMEM(tile, BF16)],
        compiler_params=_cp(("arbitrary", "arbitrary"), VMEM_MID))(sinks, qkvg, qkvg, qkvg, du_a, a_out, lse, bias)


def _out_proj(name, u_parts, gate_arr, gate_blk, w_out, x, gmod, seq):
    t_tok = x.shape[0]
    nu = len(u_parts)

    def body(*refs):
        u_refs = refs[:nu]
        g_ref, w_ref, x_ref, gm_ref, yg_ref, y_ref, xn_ref = refs[nu:]
        u = jnp.concatenate([r[...].astype(F32) for r in u_refs], axis=1) if nu > 1 else u_refs[0][...].astype(F32)
        yg = (u * _silu(g_ref[...].astype(F32))).astype(BF16)
        yg_ref[...] = yg
        y = _nn(yg, w_ref[...])
        y_ref[...] = y.astype(BF16)
        xn_ref[...] = x_ref[...] + gm_ref[...] * y

    row = lambda w: pl.BlockSpec((TM, w), lambda i: (i, 0))
    in_specs = [row(u.shape[1]) for u in u_parts]
    in_specs += [pl.BlockSpec((TM, D_MODEL), lambda i: (i, gate_blk)),
                 pl.BlockSpec((D_MODEL, D_MODEL), lambda i: (0, 0)), row(D_MODEL),
                 pl.BlockSpec((None, 1, D_MODEL), lambda i: (i * TM // seq, 0, 0))]
    return _pallas(
        body, name=name, grid=(t_tok // TM,), in_specs=in_specs,
        out_specs=[row(D_MODEL)] * 3,
        out_shape=[jax.ShapeDtypeStruct((t_tok, D_MODEL), BF16)] * 2 + [jax.ShapeDtypeStruct((t_tok, D_MODEL), F32)],
        compiler_params=_cp(("arbitrary",), VMEM_MID))(*u_parts, gate_arr, w_out, x, gmod)


def _out_proj_bwd(name, dxn, gmod, y, w_out, seq, attn=None):
    t_tok = dxn.shape[0]
    tiles_per_seq = seq // TM

    def body(*refs):
        if attn is None:
            dxn_ref, gm_ref, y_ref, w_ref, dy_ref, dgm_ref, dyg_ref = refs
        else:
            dxn_ref, gm_ref, y_ref, w_ref, a_ref, b_ref, g_ref, dy_ref, dgm_ref, dua_ref, dub_ref, dg_ref = refs
        i = pl.program_id(0)
        dxv = dxn_ref[...]
        dy = (dxv * gm_ref[...]).astype(BF16)
        dy_ref[...] = dy

        @pl.when(i % tiles_per_seq == 0)
        def _():
            dgm_ref[...] = jnp.zeros_like(dgm_ref)

        dgm_ref[...] += jnp.sum(dxv * y_ref[...].astype(F32), axis=0, keepdims=True)
        dyg = _nn(dy, w_ref[...])
        if attn is None:
            dyg_ref[...] = dyg
        else:
            gt = g_ref[...].astype(F32)
            du = dyg * _silu(gt)
            dua_ref[...] = du[:, :512].astype(BF16)
            dub_ref[...] = du[:, 512:].astype(BF16)
            u = jnp.concatenate([a_ref[...].astype(F32), b_ref[...].astype(F32)], axis=1)
            dg_ref[...] = (dyg * u * _dsilu(gt)).astype(BF16)

    row = lambda w: pl.BlockSpec((TM, w), lambda i: (i, 0))
    mod_spec = pl.BlockSpec((None, 1, D_MODEL), lambda i: (i * TM // seq, 0, 0))
    in_specs = [row(D_MODEL), mod_spec, row(D_MODEL), pl.BlockSpec((D_MODEL, D_MODEL), lambda i: (0, 0))]
    out_specs = [row(D_MODEL), mod_spec]
    out_shape = [jax.ShapeDtypeStruct((t_tok, D_MODEL), BF16), jax.ShapeDtypeStruct(gmod.shape, F32)]
    args = [dxn, gmod, y, w_out]
    if attn is None:
        out_specs.append(row(D_MODEL))
        out_shape.append(jax.ShapeDtypeStruct((t_tok, D_MODEL), F32))
    else:
        in_specs += [row(512), row(512), pl.BlockSpec((TM, D_MODEL), lambda i: (i, C_GATE // D_MODEL))]
        out_specs += [row(512), row(512), row(D_MODEL)]
        out_shape += [jax.ShapeDtypeStruct((t_tok, 512), BF16)] * 2 + [jax.ShapeDtypeStruct((t_tok, D_MODEL), BF16)]
        args += list(attn)
    return _pallas(body, name=name, grid=(t_tok // TM,), in_specs=in_specs, out_specs=out_specs,
                   out_shape=out_shape, compiler_params=_cp(("arbitrary",), VMEM_MID))(*args)


def _norm_bwd(name, parts, w, x, g, scale, dxn, seq, rows_part=None):
    t_tok = x.shape[0]
    npart = len(parts)
    tiles_per_seq = seq // TM
    nrow_in = 0 if rows_part is None else 2

    def body(*refs):
        p_refs = refs[:npart]
        w_ref, x_ref, g_ref, sc_ref, dxn_ref = refs[npart:npart + 5]
        dx_ref, dss_ref, dg_ref = refs[npart + 5 + nrow_in:]
        i = pl.program_id(0)
        dh = jnp.zeros((TM, D_MODEL), F32)
        if rows_part is not None:
            r_ref, wr_ref = refs[npart + 5:npart + 7]
            dh = dh + _tn(r_ref[...].astype(BF16), wr_ref[...])
        for (arr, off), p_ref in zip(parts, p_refs):
            dh = dh + _nn(p_ref[...], w_ref[off:off + arr.shape[1], :])
        xv = x_ref[...]
        rstd = lax.rsqrt(jnp.mean(xv * xv, axis=-1, keepdims=True) + EPS)
        xhat = xv * rstd
        gv = g_ref[...]
        nrm = xhat * gv

        @pl.when(i % tiles_per_seq == 0)
        def _():
            dss_ref[...] = jnp.zeros_like(dss_ref)

        @pl.when(i == 0)
        def _():
            dg_ref[...] = jnp.zeros_like(dg_ref)

        dss_ref[0:1, :] += jnp.sum(dh, axis=0, keepdims=True)
        dss_ref[1:2, :] += jnp.sum(dh * nrm, axis=0, keepdims=True)
        dn = dh * (1.0 + sc_ref[...])
        dg_ref[0:1, :] += jnp.sum(dn * xhat, axis=0, keepdims=True)
        dxhat = dn * gv
        dx_ref[...] = rstd * (dxhat - xhat * jnp.mean(dxhat * xhat, axis=-1, keepdims=True)) + dxn_ref[...]

    row = lambda wd: pl.BlockSpec((TM, wd), lambda i: (i, 0))
    w_spec = pl.BlockSpec(w.shape, lambda i: (0, 0))
    in_specs = [row(a.shape[1]) for a, _ in parts]
    in_specs += [w_spec, row(D_MODEL), pl.BlockSpec((1, D_MODEL), lambda i: (0, 0)),
                 pl.BlockSpec((None, 1, D_MODEL), lambda i: (i * TM // seq, 0, 0)), row(D_MODEL)]
    args = [a for a, _ in parts] + [w, x, g, scale, dxn]
    if rows_part is not None:
        in_specs += [pl.BlockSpec((8, TM), lambda i: (0, i)), pl.BlockSpec((8, D_MODEL), lambda i: (0, 0))]
        args += list(rows_part)
    nseq = t_tok // seq
    return _pallas(
        body, name=name, grid=(t_tok // TM,), in_specs=in_specs,
        out_specs=[row(D_MODEL), pl.BlockSpec((None, 8, D_MODEL), lambda i: (i * TM // seq, 0, 0)),
                   pl.BlockSpec((8, D_MODEL), lambda i: (0, 0))],
        out_shape=[jax.ShapeDtypeStruct((t_tok, D_MODEL), F32), jax.ShapeDtypeStruct((nseq, 8, D_MODEL), F32),
                   jax.ShapeDtypeStruct((8, D_MODEL), F32)],
        compiler_params=_cp(("arbitrary",), VMEM_BIG))(*args)


def _dw(name, a, parts, blocked=None):
    t_tok, ka = a.shape
    tt = min(1024, t_tok)
    npart = len(parts)
    nt = t_tok // tt

    def body(*refs):
        a_ref = refs[0]
        p_refs = refs[1:1 + npart]
        o_refs = refs[1 + npart:1 + 2 * npart]
        acc_refs = refs[1 + 2 * npart:]
        t = pl.program_id(0)
        av = a_ref[...]
        for p_ref, acc in zip(p_refs, acc_refs):
            upd = _tn(av, p_ref[...])

            @pl.when(t == 0)
            def _():
                acc[...] = upd

            @pl.when(t > 0)
            def _():
                acc[...] += upd

        @pl.when(t == nt - 1)
        def _():
            for o_ref, acc in zip(o_refs, acc_refs):
                if blocked is None:
                    o_ref[...] = acc[...].astype(BF16)
                else:
                    for j in range(o_ref.shape[0]):
                        o_ref[j] = acc[:, j * blocked:(j + 1) * blocked].astype(BF16)

    in_specs = [pl.BlockSpec((tt, ka), lambda t: (t, 0))]
    in_specs += [pl.BlockSpec((tt, p.shape[1]), lambda t: (t, 0)) for p in parts]
    if blocked is None:
        out_shape = [jax.ShapeDtypeStruct((ka, p.shape[1]), BF16) for p in parts]
        out_specs = [pl.BlockSpec((ka, p.shape[1]), lambda t: (0, 0)) for p in parts]
    else:
        out_shape = [jax.ShapeDtypeStruct((p.shape[1] // blocked, ka, blocked), BF16) for p in parts]
        out_specs = [pl.BlockSpec((p.shape[1] // blocked, ka, blocked), lambda t: (0, 0, 0)) for p in parts]
    return _pallas(body, name=name, grid=(nt,), in_specs=in_specs, out_specs=out_specs, out_shape=out_shape,
                   scratch_shapes=[pltpu.VMEM((ka, p.shape[1]), F32) for p in parts],
                   compiler_params=_cp(("arbitrary",), VMEM_BIG))(a, *parts)


def _dw_rows(name, rows_t, h):
    t_tok = h.shape[0]
    tt = 512

    def body(r_ref, h_ref, o_ref):
        @pl.when(pl.program_id(0) == 0)
        def _():
            o_ref[...] = jnp.zeros_like(o_ref)

        o_ref[...] += _nn(r_ref[...].astype(BF16), h_ref[...])

    return _pallas(body, name=name, grid=(t_tok // tt,),
                   in_specs=[pl.BlockSpec((8, tt), lambda t: (0, t)), pl.BlockSpec((tt, D_MODEL), lambda t: (t, 0))],
                   out_specs=pl.BlockSpec((8, D_MODEL), lambda t: (0, 0)),
                   out_shape=jax.ShapeDtypeStruct((8, D_MODEL), F32),
                   compiler_params=_cp(("arbitrary",), VMEM_MID))(rows_t, h)


def _lru_gates(xc, blk, wa_ref, wx_ref, ba_ref, bx_ref, sp):
    cols = slice(blk * LRU_BLOCK_W, (blk + 1) * LRU_BLOCK_W)
    xb = xc[:, cols].astype(BF16)
    r = _sigmoid(_nn(xb, wa_ref[blk].astype(BF16)) + ba_ref[:, cols])
    ig = _sigmoid(_nn(xb, wx_ref[blk].astype(BF16)) + bx_ref[:, cols])
    log_a = -LRU_C * r * sp[:, cols]
    a = jnp.exp(log_a)
    x2 = 2.0 * log_a
    series = -x2 * (1.0 + x2 * (0.5 + x2 * (1.0 / 6.0)))
    z = jnp.where(x2 > -0.01, series, 1.0 - a * a)
    mult = z * lax.rsqrt(jnp.maximum(z, 1e-30))
    return xb, r, ig, a, mult


def _softplus_neg(lam):
    return jnp.maximum(-lam, 0.0) + jnp.log(1.0 + jnp.exp(-jnp.abs(lam)))


def _conv_taps(xe_ref, cw_ref, cb_ref):
    xc = cb_ref[...] + xe_ref[8:8 + TC, :] * cw_ref[3:4, :]
    for k in range(1, 4):
        xc = xc + xe_ref[8 - k:8 - k + TC, :] * cw_ref[3 - k:4 - k, :]
    return xc


def _lru_fwd(proj, cw, cb, w_a, b_a, w_x, b_x, lam, seq):
    t_tok = proj.shape[0]
    nc = seq // TC

    def body(x_ref, cw_ref, cb_ref, wa_ref, ba_ref, wx_ref, bx_ref, lam_ref, hs_ref, xe_s, a_s, u_s, h_s):
        c = pl.program_id(1)

        @pl.when(c == 0)
        def _():
            xe_s[0:8, :] = jnp.zeros((8, D_MODEL), F32)
            h_s[...] = jnp.zeros_like(h_s)

        xe_s[8:8 + TC, :] = x_ref[...]
        xc = _conv_taps(xe_s, cw_ref, cb_ref)
        sp = _softplus_neg(lam_ref[...])
        for blk in range(LRU_BLOCKS):
            cols = slice(blk * LRU_BLOCK_W, (blk + 1) * LRU_BLOCK_W)
            _, _, ig, a, mult = _lru_gates(xc, blk, wa_ref, wx_ref, ba_ref, bx_ref, sp)
            a_s[:, cols] = a
            u_s[:, cols] = mult * ig * xc[:, cols]

        def step(t8, h):
            base = pl.multiple_of(t8 * 8, 8)
            for q in range(8):
                h = a_s[pl.ds(base + q, 1), :] * h + u_s[pl.ds(base + q, 1), :]
                hs_ref[pl.ds(base + q, 1), :] = h
            return h

        h_s[0:1, :] = lax.fori_loop(0, TC // 8, step, h_s[0:1, :])
        xe_s[0:8, :] = xe_s[TC:TC + 8, :]

    full = lambda shape: pl.BlockSpec(shape, lambda b, c: (0,) * len(shape))
    return _pallas(
        body, name="lru_fwd", grid=(t_tok // seq, nc),
        in_specs=[pl.BlockSpec((TC, D_MODEL), lambda b, c: (b * nc + c, 0)), full((4, D_MODEL)), full((1, D_MODEL)),
                  full((LRU_BLOCKS, LRU_BLOCK_W, LRU_BLOCK_W)), full((1, D_MODEL)),
                  full((LRU_BLOCKS, LRU_BLOCK_W, LRU_BLOCK_W)), full((1, D_MODEL)), full((1, D_MODEL))],
        out_specs=pl.BlockSpec((TC, D_MODEL), lambda b, c: (b * nc + c, 0)),
        out_shape=jax.ShapeDtypeStruct((t_tok, D_MODEL), F32),
        scratch_shapes=[pltpu.VMEM((TC + 8, D_MODEL), F32), pltpu.VMEM((TC, D_MODEL), F32),
                        pltpu.VMEM((TC, D_MODEL), F32), pltpu.VMEM((8, D_MODEL), F32)],
        compiler_params=_cp(("arbitrary", "arbitrary"), VMEM_BIG))(proj, cw, cb, w_a, b_a, w_x, b_x, lam)


def _lru_bwd(proj, hs, dyh, cw, cb, w_a, b_a, w_x, b_x, lam, seq):
    t_tok = proj.shape[0]
    nc = seq // TC

    def body(x_ref, xh_ref, g_ref, hs_ref, hh_ref, dy_ref, cw_ref, cb_ref, wa_ref, ba_ref, wx_ref, bx_ref, lam_ref,
             dp_ref, dcw_ref, dvec_ref, dwa_ref, dwx_ref,
             xe_s, he_s, de_s, a_s, r_s, i_s, m_s, dhs_s, dh_s, carry_s):
        b, cr = pl.program_id(0), pl.program_id(1)
        c = nc - 1 - cr

        @pl.when(jnp.logical_and(b == 0, cr == 0))
        def _():
            dcw_ref[...] = jnp.zeros_like(dcw_ref)
            dvec_ref[...] = jnp.zeros_like(dvec_ref)
            dwa_ref[...] = jnp.zeros_like(dwa_ref)
            dwx_ref[...] = jnp.zeros_like(dwx_ref)

        @pl.when(cr == 0)
        def _():
            carry_s[...] = jnp.zeros_like(carry_s)
            de_s[TC:TC + 8, :] = jnp.zeros((8, D_MODEL), F32)

        first = c == 0
        xe_s[0:8, :] = jnp.where(first, 0.0, xh_ref[...])
        xe_s[8:8 + TC, :] = x_ref[...]
        he_s[0:8, :] = jnp.where(first, 0.0, hh_ref[...])
        he_s[8:8 + TC, :] = hs_ref[...]
        xc = _conv_taps(xe_s, cw_ref, cb_ref)
        lam_v = lam_ref[...]
        sp = _softplus_neg(lam_v)
        for blk in range(LRU_BLOCKS):
            cols = slice(blk * LRU_BLOCK_W, (blk + 1) * LRU_BLOCK_W)
            _, r, ig, a, mult = _lru_gates(xc, blk, wa_ref, wx_ref, ba_ref, bx_ref, sp)
            a_s[:, cols], r_s[:, cols], i_s[:, cols], m_s[:, cols] = a, r, ig, mult

        gt = g_ref[...]
        dyh = dy_ref[...]
        sg = _sigmoid(gt)
        dhs_s[...] = dyh * (gt * sg)
        dp_ref[:, D_MODEL:] = (dyh * hs_ref[...] * (sg * (1.0 + gt * (1.0 - sg)))).astype(BF16)

        def step(k8, carry):
            base = pl.multiple_of(TC - 8 - k8 * 8, 8)
            for q in reversed(range(8)):
                dh = dhs_s[pl.ds(base + q, 1), :] + carry
                dh_s[pl.ds(base + q, 1), :] = dh
                carry = a_s[pl.ds(base + q, 1), :] * dh
            return carry

        carry_s[0:1, :] = lax.fori_loop(0, TC // 8, step, carry_s[0:1, :])

        hprev = he_s[7:7 + TC, :]
        for blk in range(LRU_BLOCKS):
            cols = slice(blk * LRU_BLOCK_W, (blk + 1) * LRU_BLOCK_W)
            xcb = xc[:, cols]
            a, r, ig, mult, dh = a_s[:, cols], r_s[:, cols], i_s[:, cols], m_s[:, cols], dh_s[:, cols]
            spb = sp[:, cols]
            dmult = dh * ig * xcb
            di = dh * mult * xcb
            dxc = dh * mult * ig
            dla = dh * hprev[:, cols] * a - dmult * (a * a) * lax.rsqrt(jnp.maximum(mult * mult, 1e-30))
            dr = dla * (-LRU_C * spb)
            dsp = jnp.sum(dla * (-LRU_C * r), axis=0, keepdims=True)
            dga = dr * r * (1.0 - r)
            dgx = di * ig * (1.0 - ig)
            dga_b, dgx_b = dga.astype(BF16), dgx.astype(BF16)
            xb = xcb.astype(BF16)
            dxc = dxc + _nt(dga_b, wa_ref[blk].astype(BF16)) + _nt(dgx_b, wx_ref[blk].astype(BF16))
            dwa_ref[blk] += _tn(xb, dga_b)
            dwx_ref[blk] += _tn(xb, dgx_b)
            dvec_ref[1:2, cols] += jnp.sum(dga, axis=0, keepdims=True)
            dvec_ref[2:3, cols] += jnp.sum(dgx, axis=0, keepdims=True)
            dvec_ref[3:4, cols] += dsp * (-1.0 / (1.0 + jnp.exp(lam_v[:, cols])))
            de_s[0:TC, cols] = dxc

        dxc = de_s[0:TC, :]
        dvec_ref[0:1, :] += jnp.sum(dxc, axis=0, keepdims=True)
        dxr = dxc * cw_ref[3:4, :]
        dcw_ref[3:4, :] += jnp.sum(dxc * xe_s[8:8 + TC, :], axis=0, keepdims=True)
        for k in range(1, 4):
            dxr = dxr + de_s[k:k + TC, :] * cw_ref[3 - k:4 - k, :]
            dcw_ref[3 - k:4 - k, :] += jnp.sum(dxc * xe_s[8 - k:8 - k + TC, :], axis=0, keepdims=True)
        dp_ref[:, :D_MODEL] = dxr.astype(BF16)
        de_s[TC:TC + 8, :] = de_s[0:8, :]

    chunk = lambda col: pl.BlockSpec((TC, D_MODEL), lambda b, cr: (b * nc + nc - 1 - cr, col))
    halo = lambda col: pl.BlockSpec(
        (8, D_MODEL), lambda b, cr: (jnp.maximum((b * nc + nc - 1 - cr) * (TC // 8) - 1, 0), col))
    full = lambda shape: pl.BlockSpec(shape, lambda b, cr: (0,) * len(shape))
    wblk = (LRU_BLOCKS, LRU_BLOCK_W, LRU_BLOCK_W)
    return _pallas(
        body, name="lru_bwd", grid=(t_tok // seq, nc),
        in_specs=[chunk(0), halo(0), chunk(1), chunk(0), halo(0), chunk(0),
                  full((4, D_MODEL)), full((1, D_MODEL)), full(wblk), full((1, D_MODEL)), full(wblk),
                  full((1, D_MODEL)), full((1, D_MODEL))],
        out_specs=[pl.BlockSpec((TC, 2 * D_MODEL), lambda b, cr: (b * nc + nc - 1 - cr, 0)),
                   full((8, D_MODEL)), full((8, D_MODEL)), full(wblk), full(wblk)],
        out_shape=[jax.ShapeDtypeStruct((t_tok, 2 * D_MODEL), BF16), jax.ShapeDtypeStruct((8, D_MODEL), F32),
                   jax.ShapeDtypeStruct((8, D_MODEL), F32), jax.ShapeDtypeStruct(wblk, F32),
                   jax.ShapeDtypeStruct(wblk, F32)],
        scratch_shapes=[pltpu.VMEM((TC + 8, D_MODEL), F32), pltpu.VMEM((TC + 8, D_MODEL), F32),
                        pltpu.VMEM((TC + 8, D_MODEL), F32)]
        + [pltpu.VMEM((TC, D_MODEL), F32)] * 6 + [pltpu.VMEM((8, D_MODEL), F32)],
        compiler_params=_cp(("arbitrary", "arbitrary"), VMEM_BIG),
    )(proj, proj, proj, hs, hs, dyh, cw, cb, w_a, b_a, w_x, b_x, lam)


def _last_layer_tail(hs, proj, w_out, w_out_t, x, gmod, final_g, target, seq):
    t_tok = x.shape[0]
    tiles_per_seq = seq // TM

    def body(hs_ref, g_ref, w_ref, wt_ref, x_ref, gm_ref, fg_ref, t_ref,
             yg_ref, dx_ref, dy_ref, dyg_ref, dgm_ref, loss_ref, dfg_ref):
        i = pl.program_id(0)

        @pl.when(i == 0)
        def _():
            loss_ref[...] = jnp.zeros_like(loss_ref)
            dfg_ref[...] = jnp.zeros_like(dfg_ref)

        @pl.when(i % tiles_per_seq == 0)
        def _():
            dgm_ref[...] = jnp.zeros_like(dgm_ref)

        gm = gm_ref[...]
        yg = (hs_ref[...] * _silu(g_ref[...])).astype(BF16)
        yg_ref[...] = yg
        y = _nn(yg, w_ref[...])
        xv = x_ref[...] + gm * y
        gv = fg_ref[...]
        rstd = lax.rsqrt(jnp.mean(xv * xv, axis=-1, keepdims=True) + EPS)
        xhat = xv * rstd
        err = xhat * gv - t_ref[...]
        loss_ref[0:1, :] += jnp.sum(err * err, axis=0, keepdims=True) * (0.5 / D_MODEL)
        dout = err * (1.0 / D_MODEL)
        dfg_ref[0:1, :] += jnp.sum(dout * xhat, axis=0, keepdims=True)
        dxhat = dout * gv
        dxv = rstd * (dxhat - xhat * jnp.mean(dxhat * xhat, axis=-1, keepdims=True))
        dx_ref[...] = dxv
        dgm_ref[...] += jnp.sum(dxv * y, axis=0, keepdims=True)
        dy = (dxv * gm).astype(BF16)
        dy_ref[...] = dy
        dyg_ref[...] = _nn(dy, wt_ref[...])

    row = pl.BlockSpec((TM, D_MODEL), lambda i: (i, 0))
    acc = pl.BlockSpec((8, D_MODEL), lambda i: (0, 0))
    mod_spec = pl.BlockSpec((None, 1, D_MODEL), lambda i: (i * TM // seq, 0, 0))
    return _pallas(
        body, name="last_layer_tail", grid=(t_tok // TM,),
        in_specs=[row, pl.BlockSpec((TM, D_MODEL), lambda i: (i, 1)), pl.BlockSpec((D_MODEL, D_MODEL), lambda i: (0, 0)),
                  pl.BlockSpec((D_MODEL, D_MODEL), lambda i: (0, 0)),
                  row, mod_spec, pl.BlockSpec((1, D_MODEL), lambda i: (0, 0)), row],
        out_specs=[row, row, row, row, mod_spec, acc, acc],
        out_shape=[jax.ShapeDtypeStruct((t_tok, D_MODEL), BF16), jax.ShapeDtypeStruct((t_tok, D_MODEL), F32),
                   jax.ShapeDtypeStruct((t_tok, D_MODEL), BF16), jax.ShapeDtypeStruct((t_tok, D_MODEL), F32),
                   jax.ShapeDtypeStruct(gmod.shape, F32), jax.ShapeDtypeStruct((8, D_MODEL), F32),
                   jax.ShapeDtypeStruct((8, D_MODEL), F32)],
        compiler_params=_cp(("arbitrary",), VMEM_BIG))(hs, proj, w_out, w_out_t, x, gmod, final_g, target)


def _adam_math(w, g, m, v):
    m_new = ADAM_B1 * m + (1.0 - ADAM_B1) * g
    v_new = ADAM_B2 * v + (1.0 - ADAM_B2) * (g * g)
    m_hat = m_new / (1.0 - ADAM_B1 ** ADAM_STEP)
    v_hat = v_new / (1.0 - ADAM_B2 ** ADAM_STEP)
    delta = -ADAM_LR * (m_hat / (jnp.sqrt(v_hat) + ADAM_EPS) + ADAM_WD * w)
    return delta, m_new, v_new


def _sum_leading(name, x, out_dtype=F32):
    n, rows, cols = x.shape
    tr = PACK_ROWS if rows % PACK_ROWS == 0 else rows

    def body(x_ref, o_ref):
        acc = x_ref[0].astype(F32)
        for d in range(1, n):
            acc = acc + x_ref[d].astype(F32)
        o_ref[...] = acc.astype(out_dtype)

    return _pallas(body, name=name, grid=(rows // tr,),
                   in_specs=[pl.BlockSpec((n, tr, cols), lambda i: (0, i, 0))],
                   out_specs=pl.BlockSpec((tr, cols), lambda i: (i, 0)),
                   out_shape=jax.ShapeDtypeStruct((rows, cols), out_dtype),
                   compiler_params=_cp(("arbitrary",), VMEM_MID))(x)


def _adamw(name, w, m, v, g=None, parts=None):
    rows, cols = w.shape
    tr = rows if rows <= 256 else 256

    def body(*refs):
        w_ref, m_ref, v_ref, g_in, g_ref, d_ref, mo_ref, vo_ref = refs
        if parts is None:
            gv = g_in[...]
        else:
            acc = g_in[0].astype(F32)
            for d in range(1, parts.shape[0]):
                acc = acc + g_in[d].astype(F32)
            gv = acc[:, :cols]
        delta, m_new, v_new = _adam_math(w_ref[...], gv, m_ref[...], v_ref[...])
        g_ref[...] = gv
        d_ref[...] = delta
        mo_ref[...] = m_new
        vo_ref[...] = v_new

    row = pl.BlockSpec((tr, cols), lambda i: (i, 0))
    if parts is None:
        g_spec, g_arg = row, g
    else:
        g_spec, g_arg = pl.BlockSpec((parts.shape[0], tr, parts.shape[2]), lambda i: (0, i, 0)), parts
    return _pallas(body, name=name, grid=(rows // tr,), in_specs=[row, row, row, g_spec], out_specs=[row] * 4,
                   out_shape=[jax.ShapeDtypeStruct((rows, cols), F32)] * 4,
                   compiler_params=_cp(("arbitrary",), VMEM_MID))(w, m, v, g_arg)


def _adamw_many(name, groups):
    ntens = len(groups)

    def body(*refs):
        ins, outs = refs[:4 * ntens], refs[4 * ntens:]
        for k in range(ntens):
            w_ref, m_ref, v_ref, g_ref = ins[4 * k:4 * k + 4]
            gv = g_ref[...]
            delta, m_new, v_new = _adam_math(w_ref[...], gv, m_ref[...], v_ref[...])
            for o_ref, val in zip(outs[4 * k:4 * k + 4], (gv, delta, m_new, v_new)):
                o_ref[...] = val

    flat = [a for grp in groups for a in grp]
    out_shape = [jax.ShapeDtypeStruct(grp[0].shape, F32) for grp in groups for _ in range(4)]
    outs = _pallas(body, name=name, out_shape=out_shape, compiler_params=_cp(vmem=VMEM_MID))(*flat)
    return [tuple(outs[4 * k:4 * k + 4]) for k in range(ntens)]


def _pack_rows(arrs):
    rows, meta, total = [], [], 0
    for a in arrs:
        flat = a.reshape(-1)
        nrow = -(-flat.shape[0] // 1024) * 8
        rows.append(jnp.pad(flat, (0, nrow * 128 - flat.shape[0])).reshape(nrow, 128))
        meta.append((a.shape, flat.shape[0], nrow))
        total += nrow
    tail = -total % PACK_ROWS
    if tail:
        rows.append(jnp.zeros((tail, 128), F32))
    return jnp.concatenate(rows, axis=0), meta


def _unpack_rows(packed, meta):
    out, r0 = [], 0
    for shape, size, nrow in meta:
        out.append(packed[r0:r0 + nrow].reshape(-1)[:size].reshape(shape))
        r0 += nrow
    return out


WEIGHTS = ["rel_bias", "norm_g", "ada_w", "ada_b", "attn_w_in", "attn_sinks", "attn_b_f", "attn_w_out", "lru_w_in",
           "lru_conv_w", "lru_conv_b", "lru_w_a", "lru_b_a", "lru_w_x", "lru_b_x", "lru_lambda", "lru_w_out", "final_g"]
BIG = ["ada_w", "attn_w_in", "attn_w_out", "lru_w_in", "lru_w_out"]
PACK_ROWS = 256


def kernel(x, c, rel_bias, norm_g, ada_w, ada_b, attn_w_in, attn_sinks, attn_b_f, attn_w_out, lru_w_in, lru_conv_w, lru_conv_b, lru_w_a, lru_b_a, lru_w_x, lru_b_x, lru_lambda, lru_w_out, final_g, loss_target, m_rel_bias, m_norm_g, m_ada_w, m_ada_b, m_attn_w_in, m_attn_sinks, m_attn_b_f, m_attn_w_out, m_lru_w_in, m_lru_conv_w, m_lru_conv_b, m_lru_w_a, m_lru_b_a, m_lru_w_x, m_lru_b_x, m_lru_lambda, m_lru_w_out, m_final_g, v_rel_bias, v_norm_g, v_ada_w, v_ada_b, v_attn_w_in, v_attn_sinks, v_attn_b_f, v_attn_w_out, v_lru_w_in, v_lru_conv_w, v_lru_conv_b, v_lru_w_a, v_lru_b_a, v_lru_w_x, v_lru_b_x, v_lru_lambda, v_lru_w_out, v_final_g):
    nseq, seq, _ = x.shape
    t_tok = nseq * seq
    me = 4 * lax.axis_index("x") + 2 * lax.axis_index("y") + lax.axis_index("c")
    x0 = x.reshape(t_tok, D_MODEL)
    target = loss_target.reshape(t_tok, D_MODEL)

    w_in_pad = jnp.pad(attn_w_in[0].astype(BF16), ((0, 0), (0, SHARD_W_PAD - SHARD_W_IN)))
    vec_shard = jnp.concatenate([lru_conv_w[0], lru_conv_b, lru_b_a, lru_b_x, lru_lambda], axis=0)
    g_w_in, g_vec, g_c = _exchange("gather_first", [w_in_pad, vec_shard, c], [])
    later_w = [attn_w_out[0].astype(BF16), lru_w_in[0].astype(BF16), lru_w_out[0].astype(BF16)]
    later_handle, later_token = _exchange_start("gather_later_start", later_w, [], after=g_vec)
    w_full = jnp.transpose(g_w_in[:, :, :SHARD_W_IN], (1, 0, 2)).reshape(D_MODEL, N_DEV * SHARD_W_IN)
    w_aq, w_ak, w_av = w_full[:, 0:512], w_full[:, 512:640], w_full[:, 640:768]
    w_bq, w_bk, w_bv = w_full[:, 768:1280], w_full[:, 1280:1792], w_full[:, 1792:2304]
    w_f, w_gate = w_full[:, 2304:2312], w_full[:, 2312:3336]
    w_main = jnp.concatenate([w_bq, w_bk, w_bv, w_aq, w_gate, w_ak, w_av], axis=1)
    wf_t = jnp.transpose(w_f)
    vec_full = jnp.transpose(g_vec, (1, 0, 2)).reshape(8, D_MODEL)
    conv_w, conv_b, b_a, b_x, lam = vec_full[0:4], vec_full[4:5], vec_full[5:6], vec_full[6:7], vec_full[7:8]
    c_all = g_c.reshape(N_DEV * nseq, D_MODEL)

    ncol = ada_w.shape[2]
    ada_b_slice = lax.dynamic_slice(ada_b.reshape(2, N_DEV, ncol), (0, me, 0), (2, 1, ncol))
    mod_part = _ada_mod(c_all, ada_w, ada_b_slice)
    (g_mod,) = _exchange("gather_mod", [mod_part], [])
    mine = lax.dynamic_slice(g_mod, (0, 0, me * nseq, 0), (N_DEV, 2, nseq, ncol))
    mod = jnp.transpose(mine, (1, 2, 0, 3)).reshape(2, nseq, 3 * D_MODEL)
    shift = [mod[l, :, 0:D_MODEL].reshape(nseq, 1, D_MODEL) for l in range(2)]
    scale = [mod[l, :, D_MODEL:2 * D_MODEL].reshape(nseq, 1, D_MODEL) for l in range(2)]
    gmod = [mod[l, :, 2 * D_MODEL:].reshape(nseq, 1, D_MODEL) for l in range(2)]

    onehot = _bucket_onehot()
    bias = _bias_expand(jnp.transpose(rel_bias), onehot).reshape(N_HEADS, BLOCK, 2 * BLOCK)
    sinks = attn_sinks.reshape(N_HEADS)
    b_f = attn_b_f.reshape(N_HEADS, 1)
    norm_g0 = norm_g[0:1] + later_token[0:1, 0:1]
    h0, qkvg, fl_t = _norm_proj("norm_proj0", x0, norm_g0, shift[0], scale[0], w_main, seq, BF16, wf_t=wf_t)
    f_row, f_col = _fox_prep(fl_t, b_f, seq)
    a_out, lse_a = _swa_fwd(qkvg, bias, sinks, seq)
    q_aug, k_aug, kt_aug, vt = _fox_aug(qkvg, f_col, seq)
    b_out, lse_b = _fox_fwd_t(q_aug, k_aug, vt, seq)
    g_later = _exchange_wait("gather_later_wait", later_handle, after=lse_b)
    w_out0, g_lru_in, w_out1 = (_with_own(g, w, me) for g, w in zip(g_later, later_w))
    w_out0, w_out1 = w_out0.reshape(D_MODEL, D_MODEL), w_out1.reshape(D_MODEL, D_MODEL)
    w_out0_t, w_out1_t, w_main_t = jnp.transpose(w_out0), jnp.transpose(w_out1), jnp.transpose(w_main)
    lru_in_t = jnp.transpose(g_lru_in, (0, 2, 1)).reshape(2 * D_MODEL, D_MODEL)
    yg0, y0, x1 = _out_proj("out_proj0", [a_out, b_out], qkvg, C_GATE // D_MODEL, w_out0, x0, gmod[0], seq)

    h1, proj1 = _norm_proj("norm_proj1", x1, norm_g[1:2], shift[1], scale[1], g_lru_in, seq, F32)
    hs = _lru_fwd(proj1, conv_w, conv_b, lru_w_a[0], b_a, lru_w_x[0], b_x, lam, seq)

    yg1, dx2, dy1, dyh, dgm1, loss_rows, dfinal_rows = _last_layer_tail(
        hs, proj1, w_out1, w_out1_t, x1, gmod[1], final_g.reshape(1, D_MODEL), target, seq)

    dproj1, dcw, dvec, dw_a, dw_x = _lru_bwd(proj1, hs, dyh, conv_w, conv_b, lru_w_a[0], b_a, lru_w_x[0], b_x, lam, seq)
    dx1, dss1, dg1 = _norm_bwd("norm1_bwd", [(dproj1, 0)], lru_in_t, x1, norm_g[1:2], scale[1], dx2, seq)
    (p_w_out1,) = _dw("dw_out1", yg1, [dy1])
    (p_lru_in,) = _dw("dw_lru_in", h1, [dproj1], blocked=2 * D_MODEL // N_DEV)

    rows_out = D_MODEL // N_DEV
    gpack1, gmeta1 = _pack_rows([dcw[0:4], dvec[0:4], dg1[0], dfinal_rows[0]])
    dwax = jnp.stack([dw_a, dw_x]).astype(BF16)
    own1 = [gpack1, dwax, p_lru_in, p_w_out1.reshape(N_DEV, rows_out, D_MODEL)]
    grads1_handle, grads1_token = _exchange_start("grads1_start", own1[:2], own1[2:], after=dx1)

    gmod0 = gmod[0] + grads1_token[0:1, 0:1]
    dy0, dgm0, du_a, du_b, dgate = _out_proj_bwd("out_proj0_bwd", dx1, gmod0, y0, w_out0_t, seq,
                                                  attn=(a_out, b_out, qkvg))
    dq_a, dkv_a, dbias, dsink = _swa_bwd(qkvg, du_a, a_out, lse_a, bias, sinks, seq)
    dq_b, dk_b, dv_b, df4 = _fox_bwd_t(q_aug, k_aug, kt_aug, qkvg, du_b, b_out, lse_b, seq)
    dfl_t, db_f = _fox_post(df4.reshape(N_HEADS, t_tok), fl_t, b_f, seq)
    parts0 = [(dq_b, C_BQ), (dk_b, C_BK), (dv_b, C_BV), (dq_a, C_AQ), (dgate, C_GATE), (dkv_a, C_AK)]
    (p_w_out0,) = _dw("dw_out0", yg0, [dy0])
    pw_bq, pw_bk, pw_bv, pw_aq, pw_gate, pw_akv = _dw("dw_attn_in", h0, [p for p, _ in parts0])
    pw_f = _dw_rows("dw_f", dfl_t, h0)

    p_w_in = jnp.concatenate([pw_aq, pw_akv, pw_bq, pw_bk, pw_bv, jnp.transpose(pw_f).astype(BF16), pw_gate], axis=1)
    p_w_in = jnp.transpose(p_w_in.reshape(D_MODEL, N_DEV, SHARD_W_IN), (1, 0, 2))
    p_w_in = jnp.pad(p_w_in, ((0, 0), (0, 0), (0, SHARD_W_PAD - SHARD_W_IN)))
    own0 = [p_w_in, p_w_out0.reshape(N_DEV, rows_out, D_MODEL)]
    landed1 = _exchange_wait("grads1_wait", grads1_handle, after=p_w_in)
    grads0_handle, grads0_token = _exchange_start("grads0_start", [], own0, after=landed1[0])
    scale0 = scale[0] + grads0_token[0:1, 0:1]
    dx0, dss0, dg0 = _norm_bwd("norm0_bwd", parts0, w_main_t, x0, norm_g[0:1], scale0, dx1, seq,
                               rows_part=(dfl_t, wf_t))
    dbias_t = _bias_reduce(dbias.reshape(N_HEADS, BLOCK * 2 * BLOCK), onehot)

    gpack0, gmeta0 = _pack_rows([jnp.transpose(dbias_t), dg0[0], dsink[:, 0], db_f[:, 0], loss_rows[0]])
    dmod = jnp.stack([jnp.concatenate([dss[:, 0], dss[:, 1], dgm[:, 0]], axis=1)
                      for dss, dgm in ((dss0, dgm0), (dss1, dgm1))], axis=1)
    g_small0, g_dmod = _exchange("exchange_small", [gpack0, dmod], [])
    landed0 = _exchange_wait("grads0_wait", grads0_handle, after=g_small0)
    r_w_in, r_w_out0 = (_with_own(g, lax.dynamic_index_in_dim(a, me, 0, keepdims=False), me)
                        for g, a in zip(landed0, own0))
    g_small1, g_dwax = (_with_own(g, a, me) for g, a in zip(landed1[:2], own1[:2]))
    r_lru_in, r_w_out1 = (_with_own(g, lax.dynamic_index_in_dim(a, me, 0, keepdims=False), me)
                          for g, a in zip(landed1[2:], own1[2:]))

    d_rel, d_g0, d_sinks, d_b_f, loss_cols = _unpack_rows(_sum_leading("sum_small0", g_small0), gmeta0)
    loss = jnp.sum(loss_cols)
    d_cw, d_vec, d_g1, d_final_g = _unpack_rows(_sum_leading("sum_small1", g_small1), gmeta1)
    d_norm_g = jnp.stack([d_g0, d_g1])
    d_wax = _sum_leading("sum_dwax", g_dwax.reshape(N_DEV, 2 * LRU_BLOCKS * LRU_BLOCK_W, LRU_BLOCK_W))
    d_wa, d_wx = d_wax[:LRU_BLOCKS * LRU_BLOCK_W], d_wax[LRU_BLOCKS * LRU_BLOCK_W:]
    cols = lambda a: lax.dynamic_slice(a, (0, me * LRU_BLOCK_W), (a.shape[0], LRU_BLOCK_W))
    dmod_all = g_dmod.reshape(N_DEV * nseq, 2 * 3 * D_MODEL)
    d_ada_b = _sum_leading("sum_ada_b", dmod_all.reshape(N_DEV * nseq, 2 * 3 * D_MODEL // 128, 128)).reshape(2, 3 * D_MODEL)
    dmod_slice = lax.dynamic_slice(dmod_all.reshape(N_DEV * nseq, 2, N_DEV, ncol), (0, 0, me, 0),
                                   (N_DEV * nseq, 2, 1, ncol)).reshape(N_DEV * nseq, 2, ncol)
    d_ada_w = _ada_w_grad(c_all, jnp.transpose(dmod_slice, (1, 0, 2)))

    given = dict(
        rel_bias=(rel_bias, m_rel_bias, v_rel_bias), norm_g=(norm_g, m_norm_g, v_norm_g),
        ada_w=(ada_w, m_ada_w, v_ada_w), ada_b=(ada_b, m_ada_b, v_ada_b),
        attn_w_in=(attn_w_in, m_attn_w_in, v_attn_w_in), attn_sinks=(attn_sinks, m_attn_sinks, v_attn_sinks),
        attn_b_f=(attn_b_f, m_attn_b_f, v_attn_b_f), attn_w_out=(attn_w_out, m_attn_w_out, v_attn_w_out),
        lru_w_in=(lru_w_in, m_lru_w_in, v_lru_w_in), lru_conv_w=(lru_conv_w, m_lru_conv_w, v_lru_conv_w),
        lru_conv_b=(lru_conv_b, m_lru_conv_b, v_lru_conv_b), lru_w_a=(lru_w_a, m_lru_w_a, v_lru_w_a),
        lru_b_a=(lru_b_a, m_lru_b_a, v_lru_b_a), lru_w_x=(lru_w_x, m_lru_w_x, v_lru_w_x),
        lru_b_x=(lru_b_x, m_lru_b_x, v_lru_b_x), lru_lambda=(lru_lambda, m_lru_lambda, v_lru_lambda),
        lru_w_out=(lru_w_out, m_lru_w_out, v_lru_w_out), final_g=(final_g, m_final_g, v_final_g))
    results = {}

    def big(name, shape2d, g=None, parts=None):
        w, m, v = (a.reshape(shape2d) for a in given[name])
        outs = _adamw("adamw_" + name, w, m, v, g=g, parts=parts)
        results[name] = tuple(o.reshape(given[name][0].shape) for o in outs)

    big("ada_w", (2 * D_MODEL, ncol), g=d_ada_w.reshape(2 * D_MODEL, ncol))
    big("attn_w_in", (D_MODEL, SHARD_W_IN), parts=r_w_in)
    big("attn_w_out", (rows_out, D_MODEL), parts=r_w_out0)
    big("lru_w_in", (D_MODEL, 2 * D_MODEL // N_DEV), parts=r_lru_in)
    big("lru_w_out", (rows_out, D_MODEL), parts=r_w_out1)

    small_grads = dict(
        rel_bias=d_rel, norm_g=d_norm_g, ada_b=d_ada_b, attn_sinks=d_sinks.reshape(1, N_HEADS),
        attn_b_f=d_b_f.reshape(1, N_HEADS), lru_conv_w=cols(d_cw).reshape(1, 4, LRU_BLOCK_W),
        lru_conv_b=cols(d_vec[0:1]), lru_w_a=d_wa.reshape(lru_w_a.shape), lru_b_a=cols(d_vec[1:2]),
        lru_w_x=d_wx.reshape(lru_w_x.shape), lru_b_x=cols(d_vec[2:3]), lru_lambda=cols(d_vec[3:4]),
        final_g=d_final_g)
    small = [n for n in WEIGHTS if n not in BIG]
    as2d = lambda a: a.reshape(-1, a.shape[-1])
    outs = _adamw_many("adamw_small", [tuple(as2d(a) for a in given[n]) + (as2d(small_grads[n]),) for n in small])
    for n, group in zip(small, outs):
        results[n] = tuple(o.reshape(given[n][0].shape) for o in group)

    grad_x = dx0.reshape(x.shape)
    out = [loss, grad_x]
    for j in range(4):
        out += [results[n][j] for n in WEIGHTS]
    return tuple(out)
```

```python
import functools
import math

import jax
import jax.numpy as jnp
from jax import lax
from jax.experimental import pallas as pl
from jax.experimental.pallas import tpu as pltpu

F32 = jnp.float32
BF16 = jnp.bfloat16
HI = lax.Precision.HIGHEST
MESH = pl.DeviceIdType.MESH

N_DEV = 8
D_MODEL = 1024
HEAD_DIM = 64
N_HEADS = 8
KV_GROUP = 4
BLOCK = 128
REL_BUCKETS = 32
REL_MAX_EXACT = 16
REL_MAX_DIST = 128
LRU_BLOCKS = 8
LRU_BLOCK_W = 128
LRU_C = 8.0
EPS = 1e-6
SCALE = HEAD_DIM ** -0.5
NEG = -1e30

ADAM_LR = 0.001
ADAM_B1 = 0.9
ADAM_B2 = 0.999
ADAM_EPS = 1e-08
ADAM_WD = 0.01
ADAM_STEP = 10

C_BQ, C_BK, C_BV, C_AQ, C_GATE, C_AK, C_AV = 0, 512, 1024, 1536, 2048, 3072, 3200
N_MAIN = 3328
SHARD_W_IN = 417
SHARD_W_PAD = 512

TM = 512
TQ = 256
TK = 128
TKB = 256
TC = 512
SWA_SUB = 2
VMEM_BIG = 56 * 1024 * 1024
VMEM_MID = 40 * 1024 * 1024


def _pallas(body, **kw):
    return pl.pallas_call(body, **kw)


def _cp(sem=None, vmem=None):
    kw = {}
    if sem is not None:
        kw["dimension_semantics"] = sem
    if vmem is not None:
        kw["vmem_limit_bytes"] = vmem
    return pltpu.CompilerParams(**kw)


def _nn(a, b, precision=None):
    return jnp.dot(a, b, preferred_element_type=F32, precision=precision)


def _nt(a, b, precision=None):
    return lax.dot_general(a, b, (((1,), (1,)), ((), ())), preferred_element_type=F32, precision=precision)


def _tn(a, b, precision=None):
    return lax.dot_general(a, b, (((0,), (0,)), ((), ())), preferred_element_type=F32, precision=precision)


def _sigmoid(x):
    return 1.0 / (1.0 + jnp.exp(-x))


def _silu(x):
    return x * _sigmoid(x)


def _dsilu(x):
    s = _sigmoid(x)
    return s * (1.0 + x * (1.0 - s))


def _col(tile, idx):
    lane = lax.broadcasted_iota(jnp.int32, tile.shape, 1)
    return jnp.sum(jnp.where(lane == idx, tile, 0.0), axis=1, keepdims=True)


def _exchange(name, gathers, scatters, axes=("x", "y", "c"), chunks=1):
    ng, n = len(gathers), len(gathers) + len(scatters)
    ins = list(gathers) + list(scatters)
    group = 2 ** len(axes)

    def body(*refs):
        in_refs, out_refs = refs[:n], refs[n:2 * n]
        send_sems, recv_sems, loc_sems = refs[2 * n:]
        coord = {a: lax.axis_index(a) for a in ("x", "y", "c")}

        def member(r):
            pc = dict(coord)
            idx = 0
            for k, a in enumerate(axes):
                if r & (1 << (len(axes) - 1 - k)):
                    pc[a] = 1 - coord[a]
                idx = 2 * idx + pc[a]
            return (pc["x"], pc["y"], pc["c"]), idx

        _, me = member(0)

        def peer(r):
            return member(r)

        local, sends, recvs = [], [], []
        for k in range(n):
            mine = in_refs[k] if k < ng else in_refs[k].at[me]
            cp = pltpu.make_async_copy(mine, out_refs[k].at[me], loc_sems.at[k])
            cp.start()
            local.append(cp)
            lead = mine.shape[0]
            nchunk = max(q for q in range(1, chunks + 1) if lead % q == 0)
            step = lead // nchunk
            for r in range(1, group):
                pid, pidx = peer(r)
                src = in_refs[k] if k < ng else in_refs[k].at[pidx]
                for q in range(nchunk):
                    rows = pl.ds(q * step, step)
                    sems = dict(send_sem=send_sems.at[r - 1, k, q], recv_sem=recv_sems.at[r - 1, k, q],
                                device_id=pid, device_id_type=MESH)
                    snd = pltpu.make_async_remote_copy(src_ref=src.at[rows], dst_ref=out_refs[k].at[me].at[rows], **sems)
                    snd.start()
                    sends.append(snd)
                    recvs.append(pltpu.make_async_remote_copy(
                        src_ref=src.at[rows], dst_ref=out_refs[k].at[pidx].at[rows], **sems))
        for rc in recvs:
            rc.wait_recv()
        for snd in sends:
            snd.wait_send()
        for cp in local:
            cp.wait()

    out_shape = [jax.ShapeDtypeStruct((group,) + a.shape, a.dtype) for a in gathers]
    out_shape += [jax.ShapeDtypeStruct(a.shape, a.dtype) for a in scatters]
    any_spec = pl.BlockSpec(memory_space=pl.ANY)
    return _pallas(
        body, name=name, out_shape=out_shape,
        in_specs=[any_spec] * n, out_specs=[any_spec] * n,
        scratch_shapes=[pltpu.SemaphoreType.DMA((group - 1, n, chunks)), pltpu.SemaphoreType.DMA((group - 1, n, chunks)),
                        pltpu.SemaphoreType.DMA((n,))],
    )(*ins)


def _peer_of(r):
    x, y, c = lax.axis_index("x"), lax.axis_index("y"), lax.axis_index("c")
    px = 1 - x if r & 4 else x
    py = 1 - y if r & 2 else y
    pc = 1 - c if r & 1 else c
    return (px, py, pc), 4 * px + 2 * py + pc


def _split_copies(in_refs, land_refs, send_sems, recv_sems, ng, with_recv):
    _, me = _peer_of(0)
    pairs = []
    for k, (src_ref, land) in enumerate(zip(in_refs, land_refs)):
        for r in range(1, N_DEV):
            pid, pidx = _peer_of(r)
            src = src_ref if k < ng else src_ref.at[pidx]
            slot = (N_DEV - 1) * k + r - 1
            sems = dict(send_sem=send_sems.at[slot], recv_sem=recv_sems.at[slot], device_id=pid, device_id_type=MESH)
            send = pltpu.make_async_remote_copy(src_ref=src, dst_ref=land.at[me], **sems)
            recv = pltpu.make_async_remote_copy(src_ref=src, dst_ref=land.at[pidx], **sems) if with_recv else None
            pairs.append((send, recv))
    return pairs


def _exchange_start(name, gathers, scatters, after):
    ng, n = len(gathers), len(gathers) + len(scatters)
    ins = list(gathers) + list(scatters)
    lands = [jax.ShapeDtypeStruct((N_DEV,) + a.shape, a.dtype) for a in gathers]
    lands += [jax.ShapeDtypeStruct(a.shape, a.dtype) for a in scatters]

    def body(*refs):
        in_refs, land_refs = refs[:n], refs[n:2 * n]
        send_sems, recv_sems = refs[2 * n + 1:2 * n + 3]
        token = refs[-1]
        for send, _ in _split_copies(in_refs, land_refs, send_sems, recv_sems, ng, False):
            send.start()
        token[...] = jnp.zeros_like(token)

    hbm = pl.BlockSpec(memory_space=pltpu.HBM)
    sem = pl.BlockSpec(memory_space=pltpu.SEMAPHORE)
    sem_shape = pltpu.SemaphoreType.DMA(((N_DEV - 1) * n,))
    out_shape = [sem_shape, sem_shape] + [pltpu.HBM(a.shape, a.dtype) for a in ins]
    out_shape += [pltpu.HBM(l.shape, l.dtype) for l in lands] + [jax.ShapeDtypeStruct((8, 128), F32)]
    args = [pltpu.with_memory_space_constraint(a, pltpu.HBM) for a in ins]
    args += [pltpu.with_memory_space_constraint(lax.empty(l.shape, l.dtype), pltpu.HBM) for l in lands]
    outs = _pallas(
        body, name=name, out_shape=out_shape,
        in_specs=[hbm] * (2 * n) + [pl.BlockSpec(memory_space=pl.ANY)],
        out_specs=[sem, sem] + [hbm] * (2 * n) + [pl.BlockSpec(memory_space=pltpu.VMEM)],
        input_output_aliases={i: 2 + i for i in range(2 * n)},
        compiler_params=pltpu.CompilerParams(has_side_effects=pltpu.SideEffectType.DATAFLOW_SIDE_EFFECTING),
    )(*args, after)
    return (outs[0], outs[1], list(outs[2:2 + n]), list(outs[2 + n:2 + 2 * n]), ng), outs[-1]


def _exchange_wait(name, handle, after):
    send_sems, recv_sems, srcs, lands, ng = handle
    n = len(srcs)

    def body(*refs):
        in_refs, land_refs = refs[:n], refs[n:2 * n]
        send_ref, recv_ref = refs[2 * n:2 * n + 2]
        for send, recv in _split_copies(in_refs, land_refs, send_ref, recv_ref, ng, True):
            send.wait_send()
            recv.wait_recv()

    hbm = pl.BlockSpec(memory_space=pltpu.HBM)
    sem = pl.BlockSpec(memory_space=pltpu.SEMAPHORE)
    outs = _pallas(
        body, name=name, out_shape=[pltpu.HBM(a.shape, a.dtype) for a in srcs + lands],
        in_specs=[hbm] * (2 * n) + [sem, sem, pl.BlockSpec(memory_space=pl.ANY)],
        out_specs=[hbm] * (2 * n), input_output_aliases={i: i for i in range(2 * n)},
        compiler_params=pltpu.CompilerParams(has_side_effects=pltpu.SideEffectType.DATAFLOW_SIDE_EFFECTING),
    )(*srcs, *lands, send_sems, recv_sems, after)
    return list(outs[n:])


def _with_own(land, own, me):
    return lax.dynamic_update_slice(land, own[None], (me,) + (0,) * own.ndim)


def _ada_mod(c_all, ada_w, ada_b_slice):
    def body(c_ref, w_ref, b_ref, o_ref):
        ca = _silu(c_ref[...])
        for l in range(2):
            o_ref[l] = _nn(ca, w_ref[l], HI) + b_ref[l]

    return _pallas(body, name="ada_mod",
                   out_shape=jax.ShapeDtypeStruct((2, c_all.shape[0], ada_w.shape[2]), F32),
                   compiler_params=_cp(vmem=VMEM_MID))(c_all, ada_w, ada_b_slice)


def _ada_w_grad(c_all, dmod_slice):
    def body(c_ref, d_ref, o_ref):
        ca = _silu(c_ref[...])
        for l in range(2):
            o_ref[l] = _tn(ca, d_ref[l], HI)

    return _pallas(body, name="ada_w_grad",
                   out_shape=jax.ShapeDtypeStruct((2, D_MODEL, dmod_slice.shape[2]), F32),
                   compiler_params=_cp(vmem=VMEM_MID))(c_all, dmod_slice)


def _bucket_onehot():
    qi = jnp.arange(BLOCK)[:, None]
    kj = jnp.arange(2 * BLOCK)[None, :]
    rel = qi - kj + BLOCK
    n = jnp.maximum(rel, 0)
    nf = jnp.maximum(n, 1).astype(F32)
    large = REL_MAX_EXACT + (jnp.log(nf / REL_MAX_EXACT) / math.log(REL_MAX_DIST / REL_MAX_EXACT)
                             * (REL_BUCKETS - REL_MAX_EXACT)).astype(jnp.int32)
    large = jnp.minimum(large, REL_BUCKETS - 1)
    bucket = jnp.where(n < REL_MAX_EXACT, n, large).reshape(1, BLOCK * 2 * BLOCK)
    return (jnp.arange(REL_BUCKETS)[:, None] == bucket).astype(F32)


def _bias_expand(rel_bias_t, onehot):
    def body(r_ref, e_ref, o_ref):
        o_ref[...] = _nn(r_ref[...], e_ref[...], HI)

    return _pallas(body, name="bias_expand",
                   out_shape=jax.ShapeDtypeStruct((N_HEADS, onehot.shape[1]), F32),
                   compiler_params=_cp(vmem=VMEM_MID))(rel_bias_t, onehot)


def _bias_reduce(dbias, onehot):
    def body(d_ref, e_ref, o_ref):
        o_ref[...] = _nt(d_ref[...], e_ref[...], HI)

    return _pallas(body, name="bias_reduce",
                   out_shape=jax.ShapeDtypeStruct((N_HEADS, REL_BUCKETS), F32),
                   compiler_params=_cp(vmem=VMEM_MID))(dbias, onehot)


def _norm_proj(name, x, g, shift, scale, w, seq, out_dtype, wf_t=None):
    t_tok = x.shape[0]
    w3d = w.ndim == 3
    n_out = w.shape[0] * w.shape[2] if w3d else w.shape[1]
    cn = w.shape[2] if w3d else 256

    def body(x_ref, g_ref, sh_ref, sc_ref, w_ref, *rest):
        if wf_t is not None:
            wf_ref, h_ref, o_ref, fl_ref = rest
        else:
            h_ref, o_ref = rest
        xv = x_ref[...]
        rstd = lax.rsqrt(jnp.mean(xv * xv, axis=-1, keepdims=True) + EPS)
        h = (xv * rstd) * g_ref[...] * (1.0 + sc_ref[...]) + sh_ref[...]
        hb = h.astype(BF16)
        h_ref[...] = hb
        for j in range(n_out // cn):
            wj = w_ref[j] if w3d else w_ref[:, j * cn:(j + 1) * cn]
            o_ref[:, j * cn:(j + 1) * cn] = _nn(hb, wj).astype(out_dtype)
        if wf_t is not None:
            fl_ref[...] = _nt(wf_ref[...], hb)

    mod_spec = pl.BlockSpec((None, 1, D_MODEL), lambda i: (i * TM // seq, 0, 0))
    w_spec = (pl.BlockSpec(w.shape, lambda i: (0, 0, 0)) if w3d else pl.BlockSpec(w.shape, lambda i: (0, 0)))
    in_specs = [pl.BlockSpec((TM, D_MODEL), lambda i: (i, 0)), pl.BlockSpec((1, D_MODEL), lambda i: (0, 0)),
                mod_spec, mod_spec, w_spec]
    out_shape = [jax.ShapeDtypeStruct((t_tok, D_MODEL), BF16), jax.ShapeDtypeStruct((t_tok, n_out), out_dtype)]
    out_specs = [pl.BlockSpec((TM, D_MODEL), lambda i: (i, 0)), pl.BlockSpec((TM, n_out), lambda i: (i, 0))]
    args = [x, g, shift, scale, w]
    if wf_t is not None:
        in_specs.append(pl.BlockSpec(wf_t.shape, lambda i: (0, 0)))
        out_shape.append(jax.ShapeDtypeStruct((wf_t.shape[0], t_tok), F32))
        out_specs.append(pl.BlockSpec((wf_t.shape[0], TM), lambda i: (0, i)))
        args.append(wf_t)
    return _pallas(body, name=name, grid=(t_tok // TM,), in_specs=in_specs, out_specs=out_specs,
                   out_shape=out_shape, compiler_params=_cp(("arbitrary",), VMEM_BIG))(*args)


def _fox_prep(fl_t, b_f, seq):
    t_tok = fl_t.shape[1]
    ch = 256

    def body(fl_ref, bf_ref, fr_ref, fc_ref):
        z = fl_ref[...] + bf_ref[...]
        logf = jnp.minimum(z, 0.0) - jnp.log(1.0 + jnp.exp(-jnp.abs(z)))
        ri = lax.broadcasted_iota(jnp.int32, (ch, ch), 0)
        ci = lax.broadcasted_iota(jnp.int32, (ch, ch), 1)
        upper = (ri <= ci).astype(F32)
        eye = (ri == ci).astype(F32)
        carry = jnp.zeros((N_HEADS, 1), F32)
        for k in range(seq // ch):
            fk = _nn(logf[:, k * ch:(k + 1) * ch], upper, HI) + carry
            carry = fk[:, ch - 1:ch]
            fr_ref[:, k * ch:(k + 1) * ch] = fk
            padded = jnp.concatenate([fk, jnp.zeros((128 - N_HEADS, ch), F32)], axis=0)
            fc_ref[k * ch:(k + 1) * ch, :] = _nt(eye, padded, HI)

    return _pallas(
        body, name="fox_prep", grid=(t_tok // seq,),
        in_specs=[pl.BlockSpec((N_HEADS, seq), lambda b: (0, b)), pl.BlockSpec((N_HEADS, 1), lambda b: (0, 0))],
        out_specs=[pl.BlockSpec((N_HEADS, seq), lambda b: (0, b)), pl.BlockSpec((seq, 128), lambda b: (b, 0))],
        out_shape=[jax.ShapeDtypeStruct((N_HEADS, t_tok), F32), jax.ShapeDtypeStruct((t_tok, 128), F32)],
        compiler_params=_cp(("arbitrary",), VMEM_MID))(fl_t, b_f)


def _fox_post(df_row, fl_t, b_f, seq):
    t_tok = fl_t.shape[1]
    ch = 256

    def body(d_ref, fl_ref, bf_ref, o_ref, db_ref):
        @pl.when(pl.program_id(0) == 0)
        def _():
            db_ref[...] = jnp.zeros_like(db_ref)

        z = fl_ref[...] + bf_ref[...]
        sig_neg = 1.0 / (1.0 + jnp.exp(z))
        ri = lax.broadcasted_iota(jnp.int32, (ch, ch), 0)
        ci = lax.broadcasted_iota(jnp.int32, (ch, ch), 1)
        lower = (ri >= ci).astype(F32)
        carry = jnp.zeros((N_HEADS, 1), F32)
        tot = jnp.zeros((N_HEADS, 1), F32)
        for k in reversed(range(seq // ch)):
            dk = _nn(d_ref[:, k * ch:(k + 1) * ch], lower, HI) + carry
            carry = dk[:, 0:1]
            dfl = dk * sig_neg[:, k * ch:(k + 1) * ch]
            o_ref[:, k * ch:(k + 1) * ch] = dfl
            tot = tot + jnp.sum(dfl, axis=1, keepdims=True)
        db_ref[...] += jnp.broadcast_to(tot, db_ref.shape)

    return _pallas(
        body, name="fox_post", grid=(t_tok // seq,),
        in_specs=[pl.BlockSpec((N_HEADS, seq), lambda b: (0, b)), pl.BlockSpec((N_HEADS, seq), lambda b: (0, b)),
                  pl.BlockSpec((N_HEADS, 1), lambda b: (0, 0))],
        out_specs=[pl.BlockSpec((N_HEADS, seq), lambda b: (0, b)), pl.BlockSpec((N_HEADS, 128), lambda b: (0, 0))],
        out_shape=[jax.ShapeDtypeStruct((N_HEADS, t_tok), F32), jax.ShapeDtypeStruct((N_HEADS, 128), F32)],
        compiler_params=_cp(("arbitrary",), VMEM_MID))(df_row, fl_t, b_f)


def _eye(n, dtype):
    return (lax.broadcasted_iota(jnp.int32, (n, n), 0) == lax.broadcasted_iota(jnp.int32, (n, n), 1)).astype(dtype)


def _fox_aug(qkvg, f_col, seq):
    t_tok = qkvg.shape[0]
    ta = 256
    nkb = ta // TK

    def body(q_ref, k_ref, v_ref, fc_ref, qa_ref, ka_ref, kt_ref, vt_ref):
        ri = lax.broadcasted_iota(jnp.int32, (128, 128), 0)
        ci = lax.broadcasted_iota(jnp.int32, (128, 128), 1)
        eye = (ri == ci).astype(BF16)
        lane = lax.broadcasted_iota(jnp.int32, (ta, 128), 1)
        ones_q = jnp.where(jnp.logical_and(lane >= 64, lane < 67), 1.0, 0.0)
        ones_k = jnp.where(jnp.logical_and(lane >= 67, lane < 70), 1.0, 0.0)
        fc_tile = fc_ref[...]
        for p in range(N_HEADS // 2):
            q2 = q_ref[:, 128 * p:128 * (p + 1)]
            k2 = k_ref[:, 128 * p:128 * (p + 1)]
            vt = _nt(eye, v_ref[:, 128 * p:128 * (p + 1)]).astype(BF16)
            for kk in range(nkb):
                vt_ref[p, kk] = vt[:, kk * TK:(kk + 1) * TK]
            for e in range(2):
                h = 2 * p + e
                sel = jnp.logical_and(ri == ci + HEAD_DIM * e, ci < HEAD_DIM)
                f = _col(fc_tile, h)
                fh = f.astype(BF16).astype(F32)
                fm = (f - fh).astype(BF16).astype(F32)
                fl = (f - fh - fm).astype(BF16).astype(F32)
                qa = (_nn(q2, jnp.where(sel, SCALE, 0.0).astype(BF16)) + ones_q + jnp.where(lane == 67, fh, 0.0)
                      + jnp.where(lane == 68, fm, 0.0) + jnp.where(lane == 69, fl, 0.0))
                ka = (_nn(k2, jnp.where(sel, 1.0, 0.0).astype(BF16)) + ones_k - jnp.where(lane == 64, fh, 0.0)
                      - jnp.where(lane == 65, fm, 0.0) - jnp.where(lane == 66, fl, 0.0))
                qa_ref[h] = qa.astype(BF16)
                kab = ka.astype(BF16)
                ka_ref[h] = kab
                kt = _nt(eye, kab).astype(BF16)
                for kk in range(ta // TKB):
                    kt_ref[h, kk] = kt[:, kk * TKB:(kk + 1) * TKB]

    aug = jax.ShapeDtypeStruct((N_HEADS, t_tok, 128), BF16)
    return _pallas(
        body, name="fox_aug", grid=(t_tok // ta,),
        in_specs=[pl.BlockSpec((ta, 512), lambda i: (i, C_BQ // 512)), pl.BlockSpec((ta, 512), lambda i: (i, C_BK // 512)),
                  pl.BlockSpec((ta, 512), lambda i: (i, C_BV // 512)), pl.BlockSpec((ta, 128), lambda i: (i, 0))],
        out_specs=[pl.BlockSpec((N_HEADS, ta, 128), lambda i: (0, i, 0)), pl.BlockSpec((N_HEADS, ta, 128), lambda i: (0, i, 0)),
                   pl.BlockSpec((N_HEADS, ta // TKB, 128, TKB), lambda i: (0, i, 0, 0)),
                   pl.BlockSpec((N_HEADS // 2, nkb, 128, TK), lambda i: (0, i, 0, 0))],
        out_shape=[aug, aug, jax.ShapeDtypeStruct((N_HEADS, t_tok // TKB, 128, TKB), BF16),
                   jax.ShapeDtypeStruct((N_HEADS // 2, t_tok // TK, 128, TK), BF16)],
        compiler_params=_cp(("arbitrary",), VMEM_MID))(qkvg, qkvg, qkvg, f_col)


def _fox_fwd_t(q_aug, k_aug, vt, seq):
    t_tok = k_aug.shape[1]
    nq = seq // TQ
    ratio = TQ // TK
    assert ratio == 2, "the two pipeline slots are addressed by the key block's parity"

    def body(qa_ref, ka_ref, vt_ref, o_ref, lse_ref, ml_s, acc_s, st_s, p_s, al_s, qt_s):
        i = pl.program_id(1)
        tpos = i * TQ + lax.broadcasted_iota(jnp.int32, (1, TQ), 1)
        eye = _eye(HEAD_DIM, BF16)
        eye2 = _eye(128, BF16)
        for h in range(N_HEADS):
            qt_s[h] = _nt(eye2, qa_ref[h]).astype(BF16)
            ml_s[0, h] = jnp.full((1, TQ), NEG, F32)
            ml_s[1, h] = jnp.zeros((1, TQ), F32)
            acc_s[h] = jnp.zeros((HEAD_DIM, TQ), F32)
            p_s[1, h] = jnp.zeros((TK, TQ), BF16)
            al_s[1, h] = jnp.ones((1, TQ), F32)

        def scores(j, slot):
            row0 = pl.multiple_of(j * TK, TK)
            for h in range(N_HEADS):
                st_s[slot, h] = _nn(ka_ref[h, pl.ds(row0, TK), :], qt_s[h])

        def softmax(j, slot, masked):
            if masked:
                keep = (j * TK + lax.broadcasted_iota(jnp.int32, (TK, 1), 0)) <= tpos
            for h in range(N_HEADS):
                st = st_s[slot, h]
                if masked:
                    st = jnp.where(keep, st, NEG)
                m = ml_s[0, h]
                m_new = jnp.maximum(m, jnp.max(st, axis=0, keepdims=True))
                alpha = jnp.exp(m - m_new)
                pe = jnp.exp(st - m_new)
                ml_s[0, h] = m_new
                ml_s[1, h] = alpha * ml_s[1, h] + jnp.sum(pe, axis=0, keepdims=True)
                al_s[slot, h] = alpha
                p_s[slot, h] = pe.astype(BF16)

        def values(j, slot):
            jv = jnp.maximum(j, 0)
            for h in range(N_HEADS):
                p, e = divmod(h, 2)
                acc_s[h] = al_s[slot, h] * acc_s[h] + _nn(vt_ref[p, jv, e * HEAD_DIM:(e + 1) * HEAD_DIM, :], p_s[slot, h])

        def step(m, carry):
            for kk in range(ratio):
                j = ratio * m + kk
                values(j - 1, 1 - kk)
                softmax(j, kk, False)
                scores(j + 1, 1 - kk)
            return carry

        scores(0, 0)
        lax.fori_loop(0, i, step, 0)
        for kk in range(ratio):
            j = ratio * i + kk
            values(j - 1, 1 - kk)
            softmax(j, kk, True)
            if kk < ratio - 1:
                scores(j + 1, 1 - kk)
        values(ratio * i + ratio - 1, ratio - 1)
        for p in range(N_HEADS // 2):
            outs = []
            for e in range(2):
                h = 2 * p + e
                l = ml_s[1, h]
                outs.append(_tn((acc_s[h] / l).astype(BF16), eye))
                lse_ref[p, e:e + 1, :] = ml_s[0, h] + jnp.log(l)
            o_ref[:, 128 * p:128 * (p + 1)] = jnp.concatenate(outs, axis=1).astype(BF16)

    return _pallas(
        body, name="fox_fwd", grid=(t_tok // seq, nq),
        in_specs=[pl.BlockSpec((N_HEADS, TQ, 128), lambda b, i: (0, b * nq + i, 0)),
                  pl.BlockSpec((N_HEADS, seq, 128), lambda b, i: (0, b, 0)),
                  pl.BlockSpec((N_HEADS // 2, seq // TK, 128, TK), lambda b, i: (0, b, 0, 0))],
        out_specs=[pl.BlockSpec((TQ, 512), lambda b, i: (b * nq + i, 0)),
                   pl.BlockSpec((N_HEADS // 2, 2, TQ), lambda b, i: (0, 0, b * nq + i))],
        out_shape=[jax.ShapeDtypeStruct((t_tok, 512), BF16), jax.ShapeDtypeStruct((N_HEADS // 2, 2, t_tok), F32)],
        scratch_shapes=[pltpu.VMEM((2, N_HEADS, 1, TQ), F32), pltpu.VMEM((N_HEADS, HEAD_DIM, TQ), F32),
                        pltpu.VMEM((2, N_HEADS, TK, TQ), F32), pltpu.VMEM((2, N_HEADS, TK, TQ), BF16),
                        pltpu.VMEM((2, N_HEADS, 1, TQ), F32), pltpu.VMEM((N_HEADS, 128, TQ), BF16)],
        compiler_params=_cp(("arbitrary", "arbitrary"), VMEM_MID))(q_aug, k_aug, vt)


def _fox_bwd_t(q_aug, k_aug, kt, qkvg, du_b, b_out, lse, seq):
    TK = TKB
    t_tok = qkvg.shape[0]
    nq = seq // TQ
    nkb = seq // TK
    ratio = TQ // TK
    hg = 4

    def body(qa_ref, ka_ref, kt_ref, v_ref, do_ref, o_ref, lse_ref, dq_ref, dk_ref, dv_ref, df_ref,
             dqt_s, row_s, dfk_s, dk_s, dv_s, dot_s, st_s, dp_s, pb_s, db_s, qt_s):
        eye = _eye(HEAD_DIM, BF16)
        eye2 = _eye(128, BF16)
        eye_k = _eye(TK, F32)
        lane8 = lax.broadcasted_iota(jnp.int32, (8, 128), 1)
        lane_k = lax.broadcasted_iota(jnp.int32, (TK, 128), 1)
        first = [lane8 < HEAD_DIM, lane8 >= HEAD_DIM]
        for pp in range(hg // 2):
            for ii in range(nq):
                dot_s[pp, ii] = _nt(eye2, do_ref[ii * TQ:(ii + 1) * TQ, 128 * pp:128 * (pp + 1)]).astype(BF16)
        for hh in range(hg):
            for ii in range(nq):
                qt_s[hh, ii] = _nt(eye2, qa_ref[hh, ii * TQ:(ii + 1) * TQ, :]).astype(BF16)
        for hh in range(hg):
            pp, e = divmod(hh, 2)
            head_lanes = jnp.where(first[e], 1.0, 0.0)
            for ii in range(nq):
                rows = slice(ii * TQ, (ii + 1) * TQ)
                prod = do_ref[rows, 128 * pp:128 * (pp + 1)].astype(F32) * o_ref[rows, 128 * pp:128 * (pp + 1)].astype(F32)
                row_s[hh, ii, 0] = _nt(head_lanes, prod, HI)
                row_s[hh, ii, 1] = jnp.broadcast_to(lse_ref[pp, e:e + 1, ii * TQ:(ii + 1) * TQ], (8, TQ))
                dqt_s[hh, ii] = jnp.zeros((128, TQ), F32)

        def scores_at(krow, i, slot):
            for hh in range(hg):
                pp, e = divmod(hh, 2)
                own = (lane_k < HEAD_DIM) if e == 0 else (lane_k >= HEAD_DIM)
                v2 = v_ref[pl.ds(krow, TK), 128 * pp:128 * (pp + 1)]
                vj = jnp.where(own, v2, jnp.zeros_like(v2))
                st_s[slot, hh] = _nn(ka_ref[hh, pl.ds(krow, TK), :], qt_s[hh, i])
                dp_s[slot, hh] = _nn(vj, dot_s[pp, i])

        def kblock(j, _):
            krow = pl.multiple_of(j * TK, TK)
            spos = j * TK + lax.broadcasted_iota(jnp.int32, (TK, 1), 0)
            for hh in range(hg):
                dk_s[hh] = jnp.zeros((TK, 128), F32)
                dv_s[hh] = jnp.zeros((TK, 128), F32)

            scores = functools.partial(scores_at, krow)

            def elementwise(i, slot, masked):
                if masked:
                    keep = spos <= (i * TQ + lax.broadcasted_iota(jnp.int32, (1, TQ), 1))
                for hh in range(hg):
                    pt = jnp.exp(st_s[slot, hh] - row_s[hh, i, 1][0:1, :])
                    if masked:
                        pt = jnp.where(keep, pt, 0.0)
                    dst = pt * (dp_s[slot, hh] - row_s[hh, i, 0][0:1, :])
                    pb_s[slot, hh] = pt.astype(BF16)
                    db_s[slot, hh] = dst.astype(BF16)

            def grads(i, slot):
                qrow = pl.multiple_of(i * TQ, TQ)
                for hh in range(hg):
                    dst_b = db_s[slot, hh]
                    dv_s[hh] += _nn(pb_s[slot, hh], do_ref[pl.ds(qrow, TQ), 128 * (hh // 2):128 * (hh // 2 + 1)])
                    dk_s[hh] += _nn(dst_b, qa_ref[hh, pl.ds(qrow, TQ), :])
                    dqt_s[hh, i] += _nn(kt_ref[hh, j], dst_b)

            def step(p, carry):
                i = i0 + 2 * p + 1
                grads(i - 1, 0)
                elementwise(i, 1, False)
                scores(i + 1, 0)
                grads(i, 1)
                elementwise(i + 1, 0, False)
                scores(jnp.minimum(i + 2, nq - 1), 1)
                return carry

            i0 = j // ratio
            rest = nq - 1 - i0
            elementwise(i0, 0, True)
            scores(jnp.minimum(i0 + 1, nq - 1), 1)
            lax.fori_loop(0, rest // 2, step, 0)
            j_next = jnp.minimum(j + 1, nkb - 1)
            scores_at(pl.multiple_of(j_next * TK, TK), j_next // ratio, 0)

            @pl.when(rest % 2 == 1)
            def _():
                grads(nq - 2, 0)
                elementwise(nq - 1, 1, False)
                grads(nq - 1, 1)

            @pl.when(rest % 2 == 0)
            def _():
                grads(nq - 1, 0)
            for pp in range(hg // 2):
                cols = slice(128 * pp, 128 * (pp + 1))
                dk_ref[pl.ds(krow, TK), cols] = jnp.concatenate(
                    [dk_s[2 * pp][:, :HEAD_DIM], dk_s[2 * pp + 1][:, :HEAD_DIM]], axis=1).astype(BF16)
                dv_ref[pl.ds(krow, TK), cols] = jnp.where(lane_k < HEAD_DIM, dv_s[2 * pp], dv_s[2 * pp + 1]).astype(BF16)
            for hh in range(hg):
                dfk_s[hh, j] = _tn(dk_s[hh][:, HEAD_DIM:HEAD_DIM + 8], eye_k, HI)
            return 0

        scores_at(0, 0, 0)
        lax.fori_loop(0, nkb, kblock, 0)
        for pp in range(hg // 2):
            for ii in range(nq):
                parts = []
                for e in range(2):
                    dqt = dqt_s[2 * pp + e, ii]
                    parts.append(_tn(dqt[0:HEAD_DIM, :].astype(BF16), eye) * SCALE)
                    for kk in range(ratio):
                        jj = ii * ratio + kk
                        df_ref[pp, e:e + 1, jj * TK:(jj + 1) * TK] = (dqt[67:68, kk * TK:(kk + 1) * TK]
                                                                     - dfk_s[2 * pp + e, jj][0:1, :])
                dq_ref[ii * TQ:(ii + 1) * TQ, 128 * pp:128 * (pp + 1)] = jnp.concatenate(parts, axis=1).astype(BF16)

    aug_blk = pl.BlockSpec((hg, seq, 128), lambda b, g: (g, b, 0))
    pair_blk = pl.BlockSpec((seq, 64 * hg), lambda b, g: (b, g))
    row_blk = pl.BlockSpec((hg // 2, 2, seq), lambda b, g: (g, 0, b))
    return _pallas(
        body, name="fox_bwd", grid=(t_tok // seq, N_HEADS // hg),
        in_specs=[aug_blk, aug_blk, pl.BlockSpec((hg, nkb, 128, TK), lambda b, g: (g, b, 0, 0)),
                  pl.BlockSpec((seq, 64 * hg), lambda b, g: (b, C_BV // (64 * hg) + g)), pair_blk, pair_blk, row_blk],
        out_specs=[pair_blk, pair_blk, pair_blk, row_blk],
        out_shape=[jax.ShapeDtypeStruct((t_tok, 512), BF16)] * 3
        + [jax.ShapeDtypeStruct((N_HEADS // 2, 2, t_tok), F32)],
        scratch_shapes=[pltpu.VMEM((hg, nq, 128, TQ), F32), pltpu.VMEM((hg, nq, 2, 8, TQ), F32),
                        pltpu.VMEM((hg, nkb, 8, TK), F32), pltpu.VMEM((hg, TK, 128), F32),
                        pltpu.VMEM((hg, TK, 128), F32), pltpu.VMEM((hg // 2, nq, 128, TQ), BF16),
                        pltpu.VMEM((2, hg, TK, TQ), F32), pltpu.VMEM((2, hg, TK, TQ), F32),
                        pltpu.VMEM((2, hg, TK, TQ), BF16), pltpu.VMEM((2, hg, TK, TQ), BF16),
                        pltpu.VMEM((hg, nq, 128, TQ), BF16)],
        compiler_params=_cp(("arbitrary", "arbitrary"), VMEM_BIG))(q_aug, k_aug, kt, qkvg, du_b, b_out, lse)


def _swa_window(k_ref, v_ref, n):
    prev = pl.multiple_of(jnp.maximum(n - 1, 0) * BLOCK, BLOCK)
    cur = pl.multiple_of(n * BLOCK, BLOCK)
    kwin = jnp.concatenate([k_ref[pl.ds(prev, BLOCK), :], k_ref[pl.ds(cur, BLOCK), :]], axis=0)
    vwin = jnp.concatenate([v_ref[pl.ds(prev, BLOCK), :], v_ref[pl.ds(cur, BLOCK), :]], axis=0)
    ti = lax.broadcasted_iota(jnp.int32, (BLOCK, 2 * BLOCK), 0)
    sj = lax.broadcasted_iota(jnp.int32, (BLOCK, 2 * BLOCK), 1)
    rel = ti - sj + BLOCK
    first_key = jnp.where(n > 0, 0, BLOCK)
    mask = jnp.logical_and(jnp.logical_and(rel >= 0, rel < BLOCK), sj >= first_key)
    return kwin, vwin, mask, prev, cur


def _head_cols(ref, h):
    pair = ref[:, 128 * (h // 2):128 * (h // 2 + 1)]
    return pair[:, (h % 2) * HEAD_DIM:(h % 2 + 1) * HEAD_DIM]


def _swa_logits(q_ref, kwin, bias_ref, h, mask):
    hk = h // KV_GROUP
    s = _nt(_head_cols(q_ref, h), kwin[:, hk * HEAD_DIM:(hk + 1) * HEAD_DIM]) * SCALE + bias_ref[h]
    return jnp.where(mask, s, NEG)


def _swa_fwd(qkvg, bias, sinks, seq):
    t_tok = qkvg.shape[0]
    nb = seq // BLOCK

    def body(sink_ref, q_ref, k_ref, v_ref, bias_ref, o_ref, lse_ref, s_s, p_s, den_s):
        g = pl.program_id(1)
        subs = [pl.ds(s * BLOCK, BLOCK) for s in range(SWA_SUB)]
        wins = [_swa_window(k_ref, v_ref, SWA_SUB * g + s) for s in range(SWA_SUB)]
        for s in range(SWA_SUB):
            for h in range(N_HEADS):
                s_s[s * N_HEADS + h] = _swa_logits(q_ref.at[subs[s]], wins[s][0], bias_ref, h, wins[s][2])
        lane = lax.broadcasted_iota(jnp.int32, (BLOCK, 128), 1)
        for s in range(SWA_SUB):
            lse_tile = jnp.zeros((BLOCK, 128), F32)
            for h in range(N_HEADS):
                sc = s_s[s * N_HEADS + h]
                sink = sink_ref[h]
                m = jnp.maximum(jnp.max(sc, axis=1, keepdims=True), sink)
                pe = jnp.exp(sc - m)
                den = jnp.sum(pe, axis=1, keepdims=True) + jnp.exp(sink - m)
                p_s[s * N_HEADS + h] = pe.astype(BF16)
                den_s[s * N_HEADS + h] = den
                lse_tile = jnp.where(lane == h, m + jnp.log(den), lse_tile)
            lse_ref[subs[s], :] = lse_tile
        for s in range(SWA_SUB):
            vwin = wins[s][1]
            for pr in range(N_HEADS // 2):
                outs = []
                for h in (2 * pr, 2 * pr + 1):
                    hk = h // KV_GROUP
                    outs.append(_nn(p_s[s * N_HEADS + h], vwin[:, hk * HEAD_DIM:(hk + 1) * HEAD_DIM]) / den_s[s * N_HEADS + h])
                o_ref[subs[s], 128 * pr:128 * (pr + 1)] = jnp.concatenate(outs, axis=1).astype(BF16)

    rows = SWA_SUB * BLOCK
    steps = nb // SWA_SUB
    return _pallas(
        body, name="swa_fwd", grid=(t_tok // seq, steps),
        in_specs=[pl.BlockSpec(memory_space=pltpu.SMEM),
                  pl.BlockSpec((rows, 512), lambda b, n: (b * steps + n, C_AQ // 512)),
                  pl.BlockSpec((seq, 128), lambda b, n: (b, C_AK // 128)),
                  pl.BlockSpec((seq, 128), lambda b, n: (b, C_AV // 128)),
                  pl.BlockSpec((N_HEADS, BLOCK, 2 * BLOCK), lambda b, n: (0, 0, 0))],
        out_specs=[pl.BlockSpec((rows, 512), lambda b, n: (b * steps + n, 0)),
                   pl.BlockSpec((rows, 128), lambda b, n: (b * steps + n, 0))],
        out_shape=[jax.ShapeDtypeStruct((t_tok, 512), BF16), jax.ShapeDtypeStruct((t_tok, 128), F32)],
        scratch_shapes=[pltpu.VMEM((SWA_SUB * N_HEADS, BLOCK, 2 * BLOCK), F32),
                        pltpu.VMEM((SWA_SUB * N_HEADS, BLOCK, 2 * BLOCK), BF16),
                        pltpu.VMEM((SWA_SUB * N_HEADS, BLOCK, 1), F32)],
        compiler_params=_cp(("arbitrary", "arbitrary"), VMEM_MID))(sinks, qkvg, qkvg, qkvg, bias)


def _swa_bwd(qkvg, du_a, a_out, lse, bias, sinks, seq):
    t_tok = qkvg.shape[0]
    nb = seq // BLOCK

    def body(sink_ref, q_ref, k_ref, v_ref, do_ref, o_ref, lse_ref, bias_ref,
             dq_ref, dkv_ref, dbias_ref, dsink_ref, kv_s, s_s, dp_s, pb_s, db_s):
        b, n = pl.program_id(0), pl.program_id(1)

        @pl.when(jnp.logical_and(b == 0, n == 0))
        def _():
            dbias_ref[...] = jnp.zeros_like(dbias_ref)
            dsink_ref[...] = jnp.zeros_like(dsink_ref)

        @pl.when(n == 0)
        def _():
            kv_s[...] = jnp.zeros_like(kv_s)

        subs = [pl.ds(s * BLOCK, BLOCK) for s in range(SWA_SUB)]
        wins = [_swa_window(k_ref, v_ref, SWA_SUB * n + s) for s in range(SWA_SUB)]
        for s in range(SWA_SUB):
            kwin, vwin, mask = wins[s][:3]
            for h in range(N_HEADS):
                hk = h // KV_GROUP
                s_s[s * N_HEADS + h] = _swa_logits(q_ref.at[subs[s]], kwin, bias_ref, h, mask)
                dp_s[s * N_HEADS + h] = _nt(_head_cols(do_ref.at[subs[s]], h), vwin[:, hk * HEAD_DIM:(hk + 1) * HEAD_DIM])
        for s in range(SWA_SUB):
            lse_tile = lse_ref[subs[s], :]
            do_s, o_s = do_ref.at[subs[s]], o_ref.at[subs[s]]
            for h in range(N_HEADS):
                delta = jnp.sum(_head_cols(do_s, h).astype(F32) * _head_cols(o_s, h).astype(F32), axis=1, keepdims=True)
                lse_h = _col(lse_tile, h)
                pe = jnp.exp(s_s[s * N_HEADS + h] - lse_h)
                ds = pe * (dp_s[s * N_HEADS + h] - delta)
                dbias_ref[h] += ds
                psink = jnp.exp(sink_ref[h] - lse_h)
                dsink_ref[h:h + 1, :] += jnp.broadcast_to(jnp.sum(-psink * delta, axis=0, keepdims=True), (1, 128))
                pb_s[s * N_HEADS + h] = pe.astype(BF16)
                db_s[s * N_HEADS + h] = ds.astype(BF16)
        for s in range(SWA_SUB):
            kwin, _, _, prev, cur = wins[s]
            q_s, do_s = q_ref.at[subs[s]], do_ref.at[subs[s]]
            for pr in range(N_HEADS // 2):
                dqs = []
                for h in (2 * pr, 2 * pr + 1):
                    hk = h // KV_GROUP
                    dqs.append(_nn(db_s[s * N_HEADS + h], kwin[:, hk * HEAD_DIM:(hk + 1) * HEAD_DIM]) * SCALE)
                dq_ref[subs[s], 128 * pr:128 * (pr + 1)] = jnp.concatenate(dqs, axis=1).astype(BF16)
            dks, dvs = [], []
            for hk in range(N_HEADS // KV_GROUP):
                dk = jnp.zeros((2 * BLOCK, HEAD_DIM), F32)
                dv = jnp.zeros((2 * BLOCK, HEAD_DIM), F32)
                for h in range(hk * KV_GROUP, (hk + 1) * KV_GROUP):
                    dk = dk + _tn(db_s[s * N_HEADS + h], _head_cols(q_s, h))
                    dv = dv + _tn(pb_s[s * N_HEADS + h], _head_cols(do_s, h))
                dks.append(dk * SCALE)
                dvs.append(dv)
            upd = jnp.concatenate(dks + dvs, axis=1)
            kv_s[pl.ds(prev, BLOCK), :] += upd[:BLOCK]
            kv_s[pl.ds(cur, BLOCK), :] += upd[BLOCK:]

        @pl.when(n == steps - 1)
        def _():
            dkv_ref[...] = kv_s[...].astype(BF16)

    rows = SWA_SUB * BLOCK
    steps = nb // SWA_SUB
    tile = (SWA_SUB * N_HEADS, BLOCK, 2 * BLOCK)
    return _pallas(
        body, name="swa_bwd", grid=(t_tok // seq, steps),
        in_specs=[pl.BlockSpec(memory_space=pltpu.SMEM),
                  pl.BlockSpec((rows, 512), lambda b, n: (b * steps + n, C_AQ // 512)),
                  pl.BlockSpec((seq, 128), lambda b, n: (b, C_AK // 128)),
                  pl.BlockSpec((seq, 128), lambda b, n: (b, C_AV // 128)),
                  pl.BlockSpec((rows, 512), lambda b, n: (b * steps + n, 0)),
                  pl.BlockSpec((rows, 512), lambda b, n: (b * steps + n, 0)),
                  pl.BlockSpec((rows, 128), lambda b, n: (b * steps + n, 0)),
                  pl.BlockSpec((N_HEADS, BLOCK, 2 * BLOCK), lambda b, n: (0, 0, 0))],
        out_specs=[pl.BlockSpec((rows, 512), lambda b, n: (b * steps + n, 0)),
                   pl.BlockSpec((seq, 256), lambda b, n: (b, 0)),
                   pl.BlockSpec((N_HEADS, BLOCK, 2 * BLOCK), lambda b, n: (0, 0, 0)),
                   pl.BlockSpec((N_HEADS, 128), lambda b, n: (0, 0))],
        out_shape=[jax.ShapeDtypeStruct((t_tok, 512), BF16), jax.ShapeDtypeStruct((t_tok, 256), BF16),
                   jax.ShapeDtypeStruct((N_HEADS, BLOCK, 2 * BLOCK), F32), jax.ShapeDtypeStruct((N_HEADS, 128), F32)],
        scratch_shapes=[pltpu.VMEM((seq, 256), F32), pltpu.VMEM(tile, F32), pltpu.VMEM(tile, F32),
                        pltpu.VMEM(tile, BF16), pltpu.VMEM(tile, BF16)],
        compiler_params=_cp(("arbitrary", "arbitrary"), VMEM_MID))(sinks, qkvg, qkvg, qkvg, du_a, a_out, lse, bias)


def _out_proj(name, u_parts, gate_arr, gate_blk, w_out, x, gmod, seq):
    t_tok = x.shape[0]
    nu = len(u_parts)

    def body(*refs):
        u_refs = refs[:nu]
        g_ref, w_ref, x_ref, gm_ref, yg_ref, y_ref, xn_ref = refs[nu:]
        u = jnp.concatenate([r[...].astype(F32) for r in u_refs], axis=1) if nu > 1 else u_refs[0][...].astype(F32)
        yg = (u * _silu(g_ref[...].astype(F32))).astype(BF16)
        yg_ref[...] = yg
        y = _nn(yg, w_ref[...])
        y_ref[...] = y.astype(BF16)
        xn_ref[...] = x_ref[...] + gm_ref[...] * y

    row = lambda w: pl.BlockSpec((TM, w), lambda i: (i, 0))
    in_specs = [row(u.shape[1]) for u in u_parts]
    in_specs += [pl.BlockSpec((TM, D_MODEL), lambda i: (i, gate_blk)),
                 pl.BlockSpec((D_MODEL, D_MODEL), lambda i: (0, 0)), row(D_MODEL),
                 pl.BlockSpec((None, 1, D_MODEL), lambda i: (i * TM // seq, 0, 0))]
    return _pallas(
        body, name=name, grid=(t_tok // TM,), in_specs=in_specs,
        out_specs=[row(D_MODEL)] * 3,
        out_shape=[jax.ShapeDtypeStruct((t_tok, D_MODEL), BF16)] * 2 + [jax.ShapeDtypeStruct((t_tok, D_MODEL), F32)],
        compiler_params=_cp(("arbitrary",), VMEM_MID))(*u_parts, gate_arr, w_out, x, gmod)


def _out_proj_bwd(name, dxn, gmod, y, w_out, seq, attn=None):
    t_tok = dxn.shape[0]
    tiles_per_seq = seq // TM

    def body(*refs):
        if attn is None:
            dxn_ref, gm_ref, y_ref, w_ref, dy_ref, dgm_ref, dyg_ref = refs
        else:
            dxn_ref, gm_ref, y_ref, w_ref, a_ref, b_ref, g_ref, dy_ref, dgm_ref, dua_ref, dub_ref, dg_ref = refs
        i = pl.program_id(0)
        dxv = dxn_ref[...]
        dy = (dxv * gm_ref[...]).astype(BF16)
        dy_ref[...] = dy

        @pl.when(i % tiles_per_seq == 0)
        def _():
            dgm_ref[...] = jnp.zeros_like(dgm_ref)

        dgm_ref[...] += jnp.sum(dxv * y_ref[...].astype(F32), axis=0, keepdims=True)
        dyg = _nn(dy, w_ref[...])
        if attn is None:
            dyg_ref[...] = dyg
        else:
            gt = g_ref[...].astype(F32)
            du = dyg * _silu(gt)
            dua_ref[...] = du[:, :512].astype(BF16)
            dub_ref[...] = du[:, 512:].astype(BF16)
            u = jnp.concatenate([a_ref[...].astype(F32), b_ref[...].astype(F32)], axis=1)
            dg_ref[...] = (dyg * u * _dsilu(gt)).astype(BF16)

    row = lambda w: pl.BlockSpec((TM, w), lambda i: (i, 0))
    mod_spec = pl.BlockSpec((None, 1, D_MODEL), lambda i: (i * TM // seq, 0, 0))
    in_specs = [row(D_MODEL), mod_spec, row(D_MODEL), pl.BlockSpec((D_MODEL, D_MODEL), lambda i: (0, 0))]
    out_specs = [row(D_MODEL), mod_spec]
    out_shape = [jax.ShapeDtypeStruct((t_tok, D_MODEL), BF16), jax.ShapeDtypeStruct(gmod.shape, F32)]
    args = [dxn, gmod, y, w_out]
    if attn is None:
        out_specs.append(row(D_MODEL))
        out_shape.append(jax.ShapeDtypeStruct((t_tok, D_MODEL), F32))
    else:
        in_specs += [row(512), row(512), pl.BlockSpec((TM, D_MODEL), lambda i: (i, C_GATE // D_MODEL))]
        out_specs += [row(512), row(512), row(D_MODEL)]
        out_shape += [jax.ShapeDtypeStruct((t_tok, 512), BF16)] * 2 + [jax.ShapeDtypeStruct((t_tok, D_MODEL), BF16)]
        args += list(attn)
    return _pallas(body, name=name, grid=(t_tok // TM,), in_specs=in_specs, out_specs=out_specs,
                   out_shape=out_shape, compiler_params=_cp(("arbitrary",), VMEM_MID))(*args)


def _norm_bwd(name, parts, w, x, g, scale, dxn, seq, rows_part=None):
    t_tok = x.shape[0]
    npart = len(parts)
    tiles_per_seq = seq // TM
    nrow_in = 0 if rows_part is None else 2

    def body(*refs):
        p_refs = refs[:npart]
        w_ref, x_ref, g_ref, sc_ref, dxn_ref = refs[npart:npart + 5]
        dx_ref, dss_ref, dg_ref = refs[npart + 5 + nrow_in:]
        i = pl.program_id(0)
        dh = jnp.zeros((TM, D_MODEL), F32)
        if rows_part is not None:
            r_ref, wr_ref = refs[npart + 5:npart + 7]
            dh = dh + _tn(r_ref[...].astype(BF16), wr_ref[...])
        for (arr, off), p_ref in zip(parts, p_refs):
            dh = dh + _nn(p_ref[...], w_ref[off:off + arr.shape[1], :])
        xv = x_ref[...]
        rstd = lax.rsqrt(jnp.mean(xv * xv, axis=-1, keepdims=True) + EPS)
        xhat = xv * rstd
        gv = g_ref[...]
        nrm = xhat * gv

        @pl.when(i % tiles_per_seq == 0)
        def _():
            dss_ref[...] = jnp.zeros_like(dss_ref)

        @pl.when(i == 0)
        def _():
            dg_ref[...] = jnp.zeros_like(dg_ref)

        dss_ref[0:1, :] += jnp.sum(dh, axis=0, keepdims=True)
        dss_ref[1:2, :] += jnp.sum(dh * nrm, axis=0, keepdims=True)
        dn = dh * (1.0 + sc_ref[...])
        dg_ref[0:1, :] += jnp.sum(dn * xhat, axis=0, keepdims=True)
        dxhat = dn * gv
        dx_ref[...] = rstd * (dxhat - xhat * jnp.mean(dxhat * xhat, axis=-1, keepdims=True)) + dxn_ref[...]

    row = lambda wd: pl.BlockSpec((TM, wd), lambda i: (i, 0))
    w_spec = pl.BlockSpec(w.shape, lambda i: (0, 0))
    in_specs = [row(a.shape[1]) for a, _ in parts]
    in_specs += [w_spec, row(D_MODEL), pl.BlockSpec((1, D_MODEL), lambda i: (0, 0)),
                 pl.BlockSpec((None, 1, D_MODEL), lambda i: (i * TM // seq, 0, 0)), row(D_MODEL)]
    args = [a for a, _ in parts] + [w, x, g, scale, dxn]
    if rows_part is not None:
        in_specs += [pl.BlockSpec((8, TM), lambda i: (0, i)), pl.BlockSpec((8, D_MODEL), lambda i: (0, 0))]
        args += list(rows_part)
    nseq = t_tok // seq
    return _pallas(
        body, name=name, grid=(t_tok // TM,), in_specs=in_specs,
        out_specs=[row(D_MODEL), pl.BlockSpec((None, 8, D_MODEL), lambda i: (i * TM // seq, 0, 0)),
                   pl.BlockSpec((8, D_MODEL), lambda i: (0, 0))],
        out_shape=[jax.ShapeDtypeStruct((t_tok, D_MODEL), F32), jax.ShapeDtypeStruct((nseq, 8, D_MODEL), F32),
                   jax.ShapeDtypeStruct((8, D_MODEL), F32)],
        compiler_params=_cp(("arbitrary",), VMEM_BIG))(*args)


def _dw(name, a, parts, blocked=None):
    t_tok, ka = a.shape
    tt = min(1024, t_tok)
    npart = len(parts)
    nt = t_tok // tt

    def body(*refs):
        a_ref = refs[0]
        p_refs = refs[1:1 + npart]
        o_refs = refs[1 + npart:1 + 2 * npart]
        acc_refs = refs[1 + 2 * npart:]
        t = pl.program_id(0)
        av = a_ref[...]
        for p_ref, acc in zip(p_refs, acc_refs):
            upd = _tn(av, p_ref[...])

            @pl.when(t == 0)
            def _():
                acc[...] = upd

            @pl.when(t > 0)
            def _():
                acc[...] += upd

        @pl.when(t == nt - 1)
        def _():
            for o_ref, acc in zip(o_refs, acc_refs):
                if blocked is None:
                    o_ref[...] = acc[...].astype(BF16)
                else:
                    for j in range(o_ref.shape[0]):
                        o_ref[j] = acc[:, j * blocked:(j + 1) * blocked].astype(BF16)

    in_specs = [pl.BlockSpec((tt, ka), lambda t: (t, 0))]
    in_specs += [pl.BlockSpec((tt, p.shape[1]), lambda t: (t, 0)) for p in parts]
    if blocked is None:
        out_shape = [jax.ShapeDtypeStruct((ka, p.shape[1]), BF16) for p in parts]
        out_specs = [pl.BlockSpec((ka, p.shape[1]), lambda t: (0, 0)) for p in parts]
    else:
        out_shape = [jax.ShapeDtypeStruct((p.shape[1] // blocked, ka, blocked), BF16) for p in parts]
        out_specs = [pl.BlockSpec((p.shape[1] // blocked, ka, blocked), lambda t: (0, 0, 0)) for p in parts]
    return _pallas(body, name=name, grid=(nt,), in_specs=in_specs, out_specs=out_specs, out_shape=out_shape,
                   scratch_shapes=[pltpu.VMEM((ka, p.shape[1]), F32) for p in parts],
                   compiler_params=_cp(("arbitrary",), VMEM_BIG))(a, *parts)


def _dw_rows(name, rows_t, h):
    t_tok = h.shape[0]
    tt = 512

    def body(r_ref, h_ref, o_ref):
        @pl.when(pl.program_id(0) == 0)
        def _():
            o_ref[...] = jnp.zeros_like(o_ref)

        o_ref[...] += _nn(r_ref[...].astype(BF16), h_ref[...])

    return _pallas(body, name=name, grid=(t_tok // tt,),
                   in_specs=[pl.BlockSpec((8, tt), lambda t: (0, t)), pl.BlockSpec((tt, D_MODEL), lambda t: (t, 0))],
                   out_specs=pl.BlockSpec((8, D_MODEL), lambda t: (0, 0)),
                   out_shape=jax.ShapeDtypeStruct((8, D_MODEL), F32),
                   compiler_params=_cp(("arbitrary",), VMEM_MID))(rows_t, h)


def _lru_gates(xc, blk, wa_ref, wx_ref, ba_ref, bx_ref, sp):
    cols = slice(blk * LRU_BLOCK_W, (blk + 1) * LRU_BLOCK_W)
    xb = xc[:, cols].astype(BF16)
    r = _sigmoid(_nn(xb, wa_ref[blk].astype(BF16)) + ba_ref[:, cols])
    ig = _sigmoid(_nn(xb, wx_ref[blk].astype(BF16)) + bx_ref[:, cols])
    log_a = -LRU_C * r * sp[:, cols]
    a = jnp.exp(log_a)
    x2 = 2.0 * log_a
    series = -x2 * (1.0 + x2 * (0.5 + x2 * (1.0 / 6.0)))
    z = jnp.where(x2 > -0.01, series, 1.0 - a * a)
    mult = z * lax.rsqrt(jnp.maximum(z, 1e-30))
    return xb, r, ig, a, mult


def _softplus_neg(lam):
    return jnp.maximum(-lam, 0.0) + jnp.log(1.0 + jnp.exp(-jnp.abs(lam)))


def _conv_taps(xe_ref, cw_ref, cb_ref):
    xc = cb_ref[...] + xe_ref[8:8 + TC, :] * cw_ref[3:4, :]
    for k in range(1, 4):
        xc = xc + xe_ref[8 - k:8 - k + TC, :] * cw_ref[3 - k:4 - k, :]
    return xc


def _lru_fwd(proj, cw, cb, w_a, b_a, w_x, b_x, lam, seq):
    t_tok = proj.shape[0]
    nc = seq // TC

    def body(x_ref, cw_ref, cb_ref, wa_ref, ba_ref, wx_ref, bx_ref, lam_ref, hs_ref, xe_s, a_s, u_s, h_s):
        c = pl.program_id(1)

        @pl.when(c == 0)
        def _():
            xe_s[0:8, :] = jnp.zeros((8, D_MODEL), F32)
            h_s[...] = jnp.zeros_like(h_s)

        xe_s[8:8 + TC, :] = x_ref[...]
        xc = _conv_taps(xe_s, cw_ref, cb_ref)
        sp = _softplus_neg(lam_ref[...])
        for blk in range(LRU_BLOCKS):
            cols = slice(blk * LRU_BLOCK_W, (blk + 1) * LRU_BLOCK_W)
            _, _, ig, a, mult = _lru_gates(xc, blk, wa_ref, wx_ref, ba_ref, bx_ref, sp)
            a_s[:, cols] = a
            u_s[:, cols] = mult * ig * xc[:, cols]

        def step(t8, h):
            base = pl.multiple_of(t8 * 8, 8)
            for q in range(8):
                h = a_s[pl.ds(base + q, 1), :] * h + u_s[pl.ds(base + q, 1), :]
                hs_ref[pl.ds(base + q, 1), :] = h
            return h

        h_s[0:1, :] = lax.fori_loop(0, TC // 8, step, h_s[0:1, :])
        xe_s[0:8, :] = xe_s[TC:TC + 8, :]

    full = lambda shape: pl.BlockSpec(shape, lambda b, c: (0,) * len(shape))
    return _pallas(
        body, name="lru_fwd", grid=(t_tok // seq, nc),
        in_specs=[pl.BlockSpec((TC, D_MODEL), lambda b, c: (b * nc + c, 0)), full((4, D_MODEL)), full((1, D_MODEL)),
                  full((LRU_BLOCKS, LRU_BLOCK_W, LRU_BLOCK_W)), full((1, D_MODEL)),
                  full((LRU_BLOCKS, LRU_BLOCK_W, LRU_BLOCK_W)), full((1, D_MODEL)), full((1, D_MODEL))],
        out_specs=pl.BlockSpec((TC, D_MODEL), lambda b, c: (b * nc + c, 0)),
        out_shape=jax.ShapeDtypeStruct((t_tok, D_MODEL), F32),
        scratch_shapes=[pltpu.VMEM((TC + 8, D_MODEL), F32), pltpu.VMEM((TC, D_MODEL), F32),
                        pltpu.VMEM((TC, D_MODEL), F32), pltpu.VMEM((8, D_MODEL), F32)],
        compiler_params=_cp(("arbitrary", "arbitrary"), VMEM_BIG))(proj, cw, cb, w_a, b_a, w_x, b_x, lam)


def _lru_bwd(proj, hs, dyh, cw, cb, w_a, b_a, w_x, b_x, lam, seq):
    t_tok = proj.shape[0]
    nc = seq // TC

    def body(x_ref, xh_ref, g_ref, hs_ref, hh_ref, dy_ref, cw_ref, cb_ref, wa_ref, ba_ref, wx_ref, bx_ref, lam_ref,
             dp_ref, dcw_ref, dvec_ref, dwa_ref, dwx_ref,
             xe_s, he_s, de_s, a_s, r_s, i_s, m_s, dhs_s, dh_s, carry_s):
        b, cr = pl.program_id(0), pl.program_id(1)
        c = nc - 1 - cr

        @pl.when(jnp.logical_and(b == 0, cr == 0))
        def _():
            dcw_ref[...] = jnp.zeros_like(dcw_ref)
            dvec_ref[...] = jnp.zeros_like(dvec_ref)
            dwa_ref[...] = jnp.zeros_like(dwa_ref)
            dwx_ref[...] = jnp.zeros_like(dwx_ref)

        @pl.when(cr == 0)
        def _():
            carry_s[...] = jnp.zeros_like(carry_s)
            de_s[TC:TC + 8, :] = jnp.zeros((8, D_MODEL), F32)

        first = c == 0
        xe_s[0:8, :] = jnp.where(first, 0.0, xh_ref[...])
        xe_s[8:8 + TC, :] = x_ref[...]
        he_s[0:8, :] = jnp.where(first, 0.0, hh_ref[...])
        he_s[8:8 + TC, :] = hs_ref[...]
        xc = _conv_taps(xe_s, cw_ref, cb_ref)
        lam_v = lam_ref[...]
        sp = _softplus_neg(lam_v)
        for blk in range(LRU_BLOCKS):
            cols = slice(blk * LRU_BLOCK_W, (blk + 1) * LRU_BLOCK_W)
            _, r, ig, a, mult = _lru_gates(xc, blk, wa_ref, wx_ref, ba_ref, bx_ref, sp)
            a_s[:, cols], r_s[:, cols], i_s[:, cols], m_s[:, cols] = a, r, ig, mult

        gt = g_ref[...]
        dyh = dy_ref[...]
        sg = _sigmoid(gt)
        dhs_s[...] = dyh * (gt * sg)
        dp_ref[:, D_MODEL:] = (dyh * hs_ref[...] * (sg * (1.0 + gt * (1.0 - sg)))).astype(BF16)

        def step(k8, carry):
            base = pl.multiple_of(TC - 8 - k8 * 8, 8)
            for q in reversed(range(8)):
                dh = dhs_s[pl.ds(base + q, 1), :] + carry
                dh_s[pl.ds(base + q, 1), :] = dh
                carry = a_s[pl.ds(base + q, 1), :] * dh
            return carry

        carry_s[0:1, :] = lax.fori_loop(0, TC // 8, step, carry_s[0:1, :])

        hprev = he_s[7:7 + TC, :]
        for blk in range(LRU_BLOCKS):
            cols = slice(blk * LRU_BLOCK_W, (blk + 1) * LRU_BLOCK_W)
            xcb = xc[:, cols]
            a, r, ig, mult, dh = a_s[:, cols], r_s[:, cols], i_s[:, cols], m_s[:, cols], dh_s[:, cols]
            spb = sp[:, cols]
            dmult = dh * ig * xcb
            di = dh * mult * xcb
            dxc = dh * mult * ig
            dla = dh * hprev[:, cols] * a - dmult * (a * a) * lax.rsqrt(jnp.maximum(mult * mult, 1e-30))
            dr = dla * (-LRU_C * spb)
            dsp = jnp.sum(dla * (-LRU_C * r), axis=0, keepdims=True)
            dga = dr * r * (1.0 - r)
            dgx = di * ig * (1.0 - ig)
            dga_b, dgx_b = dga.astype(BF16), dgx.astype(BF16)
            xb = xcb.astype(BF16)
            dxc = dxc + _nt(dga_b, wa_ref[blk].astype(BF16)) + _nt(dgx_b, wx_ref[blk].astype(BF16))
            dwa_ref[blk] += _tn(xb, dga_b)
            dwx_ref[blk] += _tn(xb, dgx_b)
            dvec_ref[1:2, cols] += jnp.sum(dga, axis=0, keepdims=True)
            dvec_ref[2:3, cols] += jnp.sum(dgx, axis=0, keepdims=True)
            dvec_ref[3:4, cols] += dsp * (-1.0 / (1.0 + jnp.exp(lam_v[:, cols])))
            de_s[0:TC, cols] = dxc

        dxc = de_s[0:TC, :]
        dvec_ref[0:1, :] += jnp.sum(dxc, axis=0, keepdims=True)
        dxr = dxc * cw_ref[3:4, :]
        dcw_ref[3:4, :] += jnp.sum(dxc * xe_s[8:8 + TC, :], axis=0, keepdims=True)
        for k in range(1, 4):
            dxr = dxr + de_s[k:k + TC, :] * cw_ref[3 - k:4 - k, :]
            dcw_ref[3 - k:4 - k, :] += jnp.sum(dxc * xe_s[8 - k:8 - k + TC, :], axis=0, keepdims=True)
        dp_ref[:, :D_MODEL] = dxr.astype(BF16)
        de_s[TC:TC + 8, :] = de_s[0:8, :]

    chunk = lambda col: pl.BlockSpec((TC, D_MODEL), lambda b, cr: (b * nc + nc - 1 - cr, col))
    halo = lambda col: pl.BlockSpec(
        (8, D_MODEL), lambda b, cr: (jnp.maximum((b * nc + nc - 1 - cr) * (TC // 8) - 1, 0), col))
    full = lambda shape: pl.BlockSpec(shape, lambda b, cr: (0,) * len(shape))
    wblk = (LRU_BLOCKS, LRU_BLOCK_W, LRU_BLOCK_W)
    return _pallas(
        body, name="lru_bwd", grid=(t_tok // seq, nc),
        in_specs=[chunk(0), halo(0), chunk(1), chunk(0), halo(0), chunk(0),
                  full((4, D_MODEL)), full((1, D_MODEL)), full(wblk), full((1, D_MODEL)), full(wblk),
                  full((1, D_MODEL)), full((1, D_MODEL))],
        out_specs=[pl.BlockSpec((TC, 2 * D_MODEL), lambda b, cr: (b * nc + nc - 1 - cr, 0)),
                   full((8, D_MODEL)), full((8, D_MODEL)), full(wblk), full(wblk)],
        out_shape=[jax.ShapeDtypeStruct((t_tok, 2 * D_MODEL), BF16), jax.ShapeDtypeStruct((8, D_MODEL), F32),
                   jax.ShapeDtypeStruct((8, D_MODEL), F32), jax.ShapeDtypeStruct(wblk, F32),
                   jax.ShapeDtypeStruct(wblk, F32)],
        scratch_shapes=[pltpu.VMEM((TC + 8, D_MODEL), F32), pltpu.VMEM((TC + 8, D_MODEL), F32),
                        pltpu.VMEM((TC + 8, D_MODEL), F32)]
        + [pltpu.VMEM((TC, D_MODEL), F32)] * 6 + [pltpu.VMEM((8, D_MODEL), F32)],
        compiler_params=_cp(("arbitrary", "arbitrary"), VMEM_BIG),
    )(proj, proj, proj, hs, hs, dyh, cw, cb, w_a, b_a, w_x, b_x, lam)


def _last_layer_tail(hs, proj, w_out, w_out_t, x, gmod, final_g, target, seq):
    t_tok = x.shape[0]
    tiles_per_seq = seq // TM

    def body(hs_ref, g_ref, w_ref, wt_ref, x_ref, gm_ref, fg_ref, t_ref,
             yg_ref, dx_ref, dy_ref, dyg_ref, dgm_ref, loss_ref, dfg_ref):
        i = pl.program_id(0)

        @pl.when(i == 0)
        def _():
            loss_ref[...] = jnp.zeros_like(loss_ref)
            dfg_ref[...] = jnp.zeros_like(dfg_ref)

        @pl.when(i % tiles_per_seq == 0)
        def _():
            dgm_ref[...] = jnp.zeros_like(dgm_ref)

        gm = gm_ref[...]
        yg = (hs_ref[...] * _silu(g_ref[...])).astype(BF16)
        yg_ref[...] = yg
        y = _nn(yg, w_ref[...])
        xv = x_ref[...] + gm * y
        gv = fg_ref[...]
        rstd = lax.rsqrt(jnp.mean(xv * xv, axis=-1, keepdims=True) + EPS)
        xhat = xv * rstd
        err = xhat * gv - t_ref[...]
        loss_ref[0:1, :] += jnp.sum(err * err, axis=0, keepdims=True) * (0.5 / D_MODEL)
        dout = err * (1.0 / D_MODEL)
        dfg_ref[0:1, :] += jnp.sum(dout * xhat, axis=0, keepdims=True)
        dxhat = dout * gv
        dxv = rstd * (dxhat - xhat * jnp.mean(dxhat * xhat, axis=-1, keepdims=True))
        dx_ref[...] = dxv
        dgm_ref[...] += jnp.sum(dxv * y, axis=0, keepdims=True)
        dy = (dxv * gm).astype(BF16)
        dy_ref[...] = dy
        dyg_ref[...] = _nn(dy, wt_ref[...])

    row = pl.BlockSpec((TM, D_MODEL), lambda i: (i, 0))
    acc = pl.BlockSpec((8, D_MODEL), lambda i: (0, 0))
    mod_spec = pl.BlockSpec((None, 1, D_MODEL), lambda i: (i * TM // seq, 0, 0))
    return _pallas(
        body, name="last_layer_tail", grid=(t_tok // TM,),
        in_specs=[row, pl.BlockSpec((TM, D_MODEL), lambda i: (i, 1)), pl.BlockSpec((D_MODEL, D_MODEL), lambda i: (0, 0)),
                  pl.BlockSpec((D_MODEL, D_MODEL), lambda i: (0, 0)),
                  row, mod_spec, pl.BlockSpec((1, D_MODEL), lambda i: (0, 0)), row],
        out_specs=[row, row, row, row, mod_spec, acc, acc],
        out_shape=[jax.ShapeDtypeStruct((t_tok, D_MODEL), BF16), jax.ShapeDtypeStruct((t_tok, D_MODEL), F32),
                   jax.ShapeDtypeStruct((t_tok, D_MODEL), BF16), jax.ShapeDtypeStruct((t_tok, D_MODEL), F32),
                   jax.ShapeDtypeStruct(gmod.shape, F32), jax.ShapeDtypeStruct((8, D_MODEL), F32),
                   jax.ShapeDtypeStruct((8, D_MODEL), F32)],
        compiler_params=_cp(("arbitrary",), VMEM_BIG))(hs, proj, w_out, w_out_t, x, gmod, final_g, target)


def _adam_math(w, g, m, v):
    m_new = ADAM_B1 * m + (1.0 - ADAM_B1) * g
    v_new = ADAM_B2 * v + (1.0 - ADAM_B2) * (g * g)
    m_hat = m_new / (1.0 - ADAM_B1 ** ADAM_STEP)
    v_hat = v_new / (1.0 - ADAM_B2 ** ADAM_STEP)
    delta = -ADAM_LR * (m_hat / (jnp.sqrt(v_hat) + ADAM_EPS) + ADAM_WD * w)
    return delta, m_new, v_new


def _sum_leading(name, x, out_dtype=F32):
    n, rows, cols = x.shape
    tr = PACK_ROWS if rows % PACK_ROWS == 0 else rows

    def body(x_ref, o_ref):
        acc = x_ref[0].astype(F32)
        for d in range(1, n):
            acc = acc + x_ref[d].astype(F32)
        o_ref[...] = acc.astype(out_dtype)

    return _pallas(body, name=name, grid=(rows // tr,),
                   in_specs=[pl.BlockSpec((n, tr, cols), lambda i: (0, i, 0))],
                   out_specs=pl.BlockSpec((tr, cols), lambda i: (i, 0)),
                   out_shape=jax.ShapeDtypeStruct((rows, cols), out_dtype),
                   compiler_params=_cp(("arbitrary",), VMEM_MID))(x)


def _adamw(name, w, m, v, g=None, parts=None):
    rows, cols = w.shape
    tr = rows if rows <= 256 else 256

    def body(*refs):
        w_ref, m_ref, v_ref, g_in, g_ref, d_ref, mo_ref, vo_ref = refs
        if parts is None:
            gv = g_in[...]
        else:
            acc = g_in[0].astype(F32)
            for d in range(1, parts.shape[0]):
                acc = acc + g_in[d].astype(F32)
            gv = acc[:, :cols]
        delta, m_new, v_new = _adam_math(w_ref[...], gv, m_ref[...], v_ref[...])
        g_ref[...] = gv
        d_ref[...] = delta
        mo_ref[...] = m_new
        vo_ref[...] = v_new

    row = pl.BlockSpec((tr, cols), lambda i: (i, 0))
    if parts is None:
        g_spec, g_arg = row, g
    else:
        g_spec, g_arg = pl.BlockSpec((parts.shape[0], tr, parts.shape[2]), lambda i: (0, i, 0)), parts
    return _pallas(body, name=name, grid=(rows // tr,), in_specs=[row, row, row, g_spec], out_specs=[row] * 4,
                   out_shape=[jax.ShapeDtypeStruct((rows, cols), F32)] * 4,
                   compiler_params=_cp(("arbitrary",), VMEM_MID))(w, m, v, g_arg)


def _adamw_many(name, groups):
    ntens = len(groups)

    def body(*refs):
        ins, outs = refs[:4 * ntens], refs[4 * ntens:]
        for k in range(ntens):
            w_ref, m_ref, v_ref, g_ref = ins[4 * k:4 * k + 4]
            gv = g_ref[...]
            delta, m_new, v_new = _adam_math(w_ref[...], gv, m_ref[...], v_ref[...])
            for o_ref, val in zip(outs[4 * k:4 * k + 4], (gv, delta, m_new, v_new)):
                o_ref[...] = val

    flat = [a for grp in groups for a in grp]
    out_shape = [jax.ShapeDtypeStruct(grp[0].shape, F32) for grp in groups for _ in range(4)]
    outs = _pallas(body, name=name, out_shape=out_shape, compiler_params=_cp(vmem=VMEM_MID))(*flat)
    return [tuple(outs[4 * k:4 * k + 4]) for k in range(ntens)]


def _pack_rows(arrs):
    rows, meta, total = [], [], 0
    for a in arrs:
        flat = a.reshape(-1)
        nrow = -(-flat.shape[0] // 1024) * 8
        rows.append(jnp.pad(flat, (0, nrow * 128 - flat.shape[0])).reshape(nrow, 128))
        meta.append((a.shape, flat.shape[0], nrow))
        total += nrow
    tail = -total % PACK_ROWS
    if tail:
        rows.append(jnp.zeros((tail, 128), F32))
    return jnp.concatenate(rows, axis=0), meta


def _unpack_rows(packed, meta):
    out, r0 = [], 0
    for shape, size, nrow in meta:
        out.append(packed[r0:r0 + nrow].reshape(-1)[:size].reshape(shape))
        r0 += nrow
    return out


WEIGHTS = ["rel_bias", "norm_g", "ada_w", "ada_b", "attn_w_in", "attn_sinks", "attn_b_f", "attn_w_out", "lru_w_in",
           "lru_conv_w", "lru_conv_b", "lru_w_a", "lru_b_a", "lru_w_x", "lru_b_x", "lru_lambda", "lru_w_out", "final_g"]
BIG = ["ada_w", "attn_w_in", "attn_w_out", "lru_w_in", "lru_w_out"]
PACK_ROWS = 256


def kernel(x, c, rel_bias, norm_g, ada_w, ada_b, attn_w_in, attn_sinks, attn_b_f, attn_w_out, lru_w_in, lru_conv_w, lru_conv_b, lru_w_a, lru_b_a, lru_w_x, lru_b_x, lru_lambda, lru_w_out, final_g, loss_target, m_rel_bias, m_norm_g, m_ada_w, m_ada_b, m_attn_w_in, m_attn_sinks, m_attn_b_f, m_attn_w_out, m_lru_w_in, m_lru_conv_w, m_lru_conv_b, m_lru_w_a, m_lru_b_a, m_lru_w_x, m_lru_b_x, m_lru_lambda, m_lru_w_out, m_final_g, v_rel_bias, v_norm_g, v_ada_w, v_ada_b, v_attn_w_in, v_attn_sinks, v_attn_b_f, v_attn_w_out, v_lru_w_in, v_lru_conv_w, v_lru_conv_b, v_lru_w_a, v_lru_b_a, v_lru_w_x, v_lru_b_x, v_lru_lambda, v_lru_w_out, v_final_g):
    nseq, seq, _ = x.shape
    t_tok = nseq * seq
    me = 4 * lax.axis_index("x") + 2 * lax.axis_index("y") + lax.axis_index("c")
    x0 = x.reshape(t_tok, D_MODEL)
    target = loss_target.reshape(t_tok, D_MODEL)

    w_in_pad = jnp.pad(attn_w_in[0].astype(BF16), ((0, 0), (0, SHARD_W_PAD - SHARD_W_IN)))
    vec_shard = jnp.concatenate([lru_conv_w[0], lru_conv_b, lru_b_a, lru_b_x, lru_lambda], axis=0)
    g_w_in, g_vec, g_c = _exchange("gather_first", [w_in_pad, vec_shard, c], [])
    later_w = [attn_w_out[0].astype(BF16), lru_w_in[0].astype(BF16), lru_w_out[0].astype(BF16)]
    later_handle, later_token = _exchange_start("gather_later_start", later_w, [], after=g_vec)
    w_full = jnp.transpose(g_w_in[:, :, :SHARD_W_IN], (1, 0, 2)).reshape(D_MODEL, N_DEV * SHARD_W_IN)
    w_aq, w_ak, w_av = w_full[:, 0:512], w_full[:, 512:640], w_full[:, 640:768]
    w_bq, w_bk, w_bv = w_full[:, 768:1280], w_full[:, 1280:1792], w_full[:, 1792:2304]
    w_f, w_gate = w_full[:, 2304:2312], w_full[:, 2312:3336]
    w_main = jnp.concatenate([w_bq, w_bk, w_bv, w_aq, w_gate, w_ak, w_av], axis=1)
    wf_t = jnp.transpose(w_f)
    vec_full = jnp.transpose(g_vec, (1, 0, 2)).reshape(8, D_MODEL)
    conv_w, conv_b, b_a, b_x, lam = vec_full[0:4], vec_full[4:5], vec_full[5:6], vec_full[6:7], vec_full[7:8]
    c_all = g_c.reshape(N_DEV * nseq, D_MODEL)

    ncol = ada_w.shape[2]
    ada_b_slice = lax.dynamic_slice(ada_b.reshape(2, N_DEV, ncol), (0, me, 0), (2, 1, ncol))
    mod_part = _ada_mod(c_all, ada_w, ada_b_slice)
    (g_mod,) = _exchange("gather_mod", [mod_part], [])
    mine = lax.dynamic_slice(g_mod, (0, 0, me * nseq, 0), (N_DEV, 2, nseq, ncol))
    mod = jnp.transpose(mine, (1, 2, 0, 3)).reshape(2, nseq, 3 * D_MODEL)
    shift = [mod[l, :, 0:D_MODEL].reshape(nseq, 1, D_MODEL) for l in range(2)]
    scale = [mod[l, :, D_MODEL:2 * D_MODEL].reshape(nseq, 1, D_MODEL) for l in range(2)]
    gmod = [mod[l, :, 2 * D_MODEL:].reshape(nseq, 1, D_MODEL) for l in range(2)]

    onehot = _bucket_onehot()
    bias = _bias_expand(jnp.transpose(rel_bias), onehot).reshape(N_HEADS, BLOCK, 2 * BLOCK)
    sinks = attn_sinks.reshape(N_HEADS)
    b_f = attn_b_f.reshape(N_HEADS, 1)
    norm_g0 = norm_g[0:1] + later_token[0:1, 0:1]
    h0, qkvg, fl_t = _norm_proj("norm_proj0", x0, norm_g0, shift[0], scale[0], w_main, seq, BF16, wf_t=wf_t)
    f_row, f_col = _fox_prep(fl_t, b_f, seq)
    a_out, lse_a = _swa_fwd(qkvg, bias, sinks, seq)
    q_aug, k_aug, kt_aug, vt = _fox_aug(qkvg, f_col, seq)
    b_out, lse_b = _fox_fwd_t(q_aug, k_aug, vt, seq)
    g_later = _exchange_wait("gather_later_wait", later_handle, after=lse_b)
    w_out0, g_lru_in, w_out1 = (_with_own(g, w, me) for g, w in zip(g_later, later_w))
    w_out0, w_out1 = w_out0.reshape(D_MODEL, D_MODEL), w_out1.reshape(D_MODEL, D_MODEL)
    w_out0_t, w_out1_t, w_main_t = jnp.transpose(w_out0), jnp.transpose(w_out1), jnp.transpose(w_main)
    lru_in_t = jnp.transpose(g_lru_in, (0, 2, 1)).reshape(2 * D_MODEL, D_MODEL)
    yg0, y0, x1 = _out_proj("out_proj0", [a_out, b_out], qkvg, C_GATE // D_MODEL, w_out0, x0, gmod[0], seq)

    h1, proj1 = _norm_proj("norm_proj1", x1, norm_g[1:2], shift[1], scale[1], g_lru_in, seq, F32)
    hs = _lru_fwd(proj1, conv_w, conv_b, lru_w_a[0], b_a, lru_w_x[0], b_x, lam, seq)

    yg1, dx2, dy1, dyh, dgm1, loss_rows, dfinal_rows = _last_layer_tail(
        hs, proj1, w_out1, w_out1_t, x1, gmod[1], final_g.reshape(1, D_MODEL), target, seq)

    dproj1, dcw, dvec, dw_a, dw_x = _lru_bwd(proj1, hs, dyh, conv_w, conv_b, lru_w_a[0], b_a, lru_w_x[0], b_x, lam, seq)
    dx1, dss1, dg1 = _norm_bwd("norm1_bwd", [(dproj1, 0)], lru_in_t, x1, norm_g[1:2], scale[1], dx2, seq)
    (p_w_out1,) = _dw("dw_out1", yg1, [dy1])
    (p_lru_in,) = _dw("dw_lru_in", h1, [dproj1], blocked=2 * D_MODEL // N_DEV)

    rows_out = D_MODEL // N_DEV
    gpack1, gmeta1 = _pack_rows([dcw[0:4], dvec[0:4], dg1[0], dfinal_rows[0]])
    dwax = jnp.stack([dw_a, dw_x]).astype(BF16)
    own1 = [gpack1, dwax, p_lru_in, p_w_out1.reshape(N_DEV, rows_out, D_MODEL)]
    grads1_handle, grads1_token = _exchange_start("grads1_start", own1[:2], own1[2:], after=dx1)

    gmod0 = gmod[0] + grads1_token[0:1, 0:1]
    dy0, dgm0, du_a, du_b, dgate = _out_proj_bwd("out_proj0_bwd", dx1, gmod0, y0, w_out0_t, seq,
                                                  attn=(a_out, b_out, qkvg))
    dq_a, dkv_a, dbias, dsink = _swa_bwd(qkvg, du_a, a_out, lse_a, bias, sinks, seq)
    dq_b, dk_b, dv_b, df4 = _fox_bwd_t(q_aug, k_aug, kt_aug, qkvg, du_b, b_out, lse_b, seq)
    dfl_t, db_f = _fox_post(df4.reshape(N_HEADS, t_tok), fl_t, b_f, seq)
    parts0 = [(dq_b, C_BQ), (dk_b, C_BK), (dv_b, C_BV), (dq_a, C_AQ), (dgate, C_GATE), (dkv_a, C_AK)]
    (p_w_out0,) = _dw("dw_out0", yg0, [dy0])
    pw_bq, pw_bk, pw_bv, pw_aq, pw_gate, pw_akv = _dw("dw_attn_in", h0, [p for p, _ in parts0])
    pw_f = _dw_rows("dw_f", dfl_t, h0)

    p_w_in = jnp.concatenate([pw_aq, pw_akv, pw_bq, pw_bk, pw_bv, jnp.transpose(pw_f).astype(BF16), pw_gate], axis=1)
    p_w_in = jnp.transpose(p_w_in.reshape(D_MODEL, N_DEV, SHARD_W_IN), (1, 0, 2))
    p_w_in = jnp.pad(p_w_in, ((0, 0), (0, 0), (0, SHARD_W_PAD - SHARD_W_IN)))
    own0 = [p_w_in, p_w_out0.reshape(N_DEV, rows_out, D_MODEL)]
    landed1 = _exchange_wait("grads1_wait", grads1_handle, after=p_w_in)
    grads0_handle, grads0_token = _exchange_start("grads0_start", [], own0, after=landed1[0])
    scale0 = scale[0] + grads0_token[0:1, 0:1]
    dx0, dss0, dg0 = _norm_bwd("norm0_bwd", parts0, w_main_t, x0, norm_g[0:1], scale0, dx1, seq,
                               rows_part=(dfl_t, wf_t))
    dbias_t = _bias_reduce(dbias.reshape(N_HEADS, BLOCK * 2 * BLOCK), onehot)

    gpack0, gmeta0 = _pack_rows([jnp.transpose(dbias_t), dg0[0], dsink[:, 0], db_f[:, 0], loss_rows[0]])
    dmod = jnp.stack([jnp.concatenate([dss[:, 0], dss[:, 1], dgm[:, 0]], axis=1)
                      for dss, dgm in ((dss0, dgm0), (dss1, dgm1))], axis=1)
    g_small0, g_dmod = _exchange("exchange_small", [gpack0, dmod], [])
    landed0 = _exchange_wait("grads0_wait", grads0_handle, after=g_small0)
    r_w_in, r_w_out0 = (_with_own(g, lax.dynamic_index_in_dim(a, me, 0, keepdims=False), me)
                        for g, a in zip(landed0, own0))
    g_small1, g_dwax = (_with_own(g, a, me) for g, a in zip(landed1[:2], own1[:2]))
    r_lru_in, r_w_out1 = (_with_own(g, lax.dynamic_index_in_dim(a, me, 0, keepdims=False), me)
                          for g, a in zip(landed1[2:], own1[2:]))

    d_rel, d_g0, d_sinks, d_b_f, loss_cols = _unpack_rows(_sum_leading("sum_small0", g_small0), gmeta0)
    loss = jnp.sum(loss_cols)
    d_cw, d_vec, d_g1, d_final_g = _unpack_rows(_sum_leading("sum_small1", g_small1), gmeta1)
    d_norm_g = jnp.stack([d_g0, d_g1])
    d_wax = _sum_leading("sum_dwax", g_dwax.reshape(N_DEV, 2 * LRU_BLOCKS * LRU_BLOCK_W, LRU_BLOCK_W))
    d_wa, d_wx = d_wax[:LRU_BLOCKS * LRU_BLOCK_W], d_wax[LRU_BLOCKS * LRU_BLOCK_W:]
    cols = lambda a: lax.dynamic_slice(a, (0, me * LRU_BLOCK_W), (a.shape[0], LRU_BLOCK_W))
    dmod_all = g_dmod.reshape(N_DEV * nseq, 2 * 3 * D_MODEL)
    d_ada_b = _sum_leading("sum_ada_b", dmod_all.reshape(N_DEV * nseq, 2 * 3 * D_MODEL // 128, 128)).reshape(2, 3 * D_MODEL)
    dmod_slice = lax.dynamic_slice(dmod_all.reshape(N_DEV * nseq, 2, N_DEV, ncol), (0, 0, me, 0),
                                   (N_DEV * nseq, 2, 1, ncol)).reshape(N_DEV * nseq, 2, ncol)
    d_ada_w = _ada_w_grad(c_all, jnp.transpose(dmod_slice, (1, 0, 2)))

    given = dict(
        rel_bias=(rel_bias, m_rel_bias, v_rel_bias), norm_g=(norm_g, m_norm_g, v_norm_g),
        ada_w=(ada_w, m_ada_w, v_ada_w), ada_b=(ada_b, m_ada_b, v_ada_b),
        attn_w_in=(attn_w_in, m_attn_w_in, v_attn_w_in), attn_sinks=(attn_sinks, m_attn_sinks, v_attn_sinks),
        attn_b_f=(attn_b_f, m_attn_b_f, v_attn_b_f), attn_w_out=(attn_w_out, m_attn_w_out, v_attn_w_out),
        lru_w_in=(lru_w_in, m_lru_w_in, v_lru_w_in), lru_conv_w=(lru_conv_w, m_lru_conv_w, v_lru_conv_w),
        lru_conv_b=(lru_conv_b, m_lru_conv_b, v_lru_conv_b), lru_w_a=(lru_w_a, m_lru_w_a, v_lru_w_a),
        lru_b_a=(lru_b_a, m_lru_b_a, v_lru_b_a), lru_w_x=(lru_w_x, m_lru_w_x, v_lru_w_x),
        lru_b_x=(lru_b_x, m_lru_b_x, v_lru_b_x), lru_lambda=(lru_lambda, m_lru_lambda, v_lru_lambda),
        lru_w_out=(lru_w_out, m_lru_w_out, v_lru_w_out), final_g=(final_g, m_final_g, v_final_g))
    results = {}

    def big(name, shape2d, g=None, parts=None):
        w, m, v = (a.reshape(shape2d) for a in given[name])
        outs = _adamw("adamw_" + name, w, m, v, g=g, parts=parts)
        results[name] = tuple(o.reshape(given[name][0].shape) for o in outs)

    big("ada_w", (2 * D_MODEL, ncol), g=d_ada_w.reshape(2 * D_MODEL, ncol))
    big("attn_w_in", (D_MODEL, SHARD_W_IN), parts=r_w_in)
    big("attn_w_out", (rows_out, D_MODEL), parts=r_w_out0)
    big("lru_w_in", (D_MODEL, 2 * D_MODEL // N_DEV), parts=r_lru_in)
    big("lru_w_out", (rows_out, D_MODEL), parts=r_w_out1)

    small_grads = dict(
        rel_bias=d_rel, norm_g=d_norm_g, ada_b=d_ada_b, attn_sinks=d_sinks.reshape(1, N_HEADS),
        attn_b_f=d_b_f.reshape(1, N_HEADS), lru_conv_w=cols(d_cw).reshape(1, 4, LRU_BLOCK_W),
        lru_conv_b=cols(d_vec[0:1]), lru_w_a=d_wa.reshape(lru_w_a.shape), lru_b_a=cols(d_vec[1:2]),
        lru_w_x=d_wx.reshape(lru_w_x.shape), lru_b_x=cols(d_vec[2:3]), lru_lambda=cols(d_vec[3:4]),
        final_g=d_final_g)
    small = [n for n in WEIGHTS if n not in BIG]
    as2d = lambda a: a.reshape(-1, a.shape[-1])
    outs = _adamw_many("adamw_small", [tuple(as2d(a) for a in given[n]) + (as2d(small_grads[n]),) for n in small])
    for n, group in zip(small, outs):
        results[n] = tuple(o.reshape(given[n][0].shape) for o in group)

    grad_x = dx0.reshape(x.shape)
    out = [loss, grad_x]
    for j in range(4):
        out += [results[n][j] for n in WEIGHTS]
    return tuple(out)
```

```python
import functools
import math

import jax
import jax.numpy as jnp
from jax import lax
from jax.experimental import pallas as pl
from jax.experimental.pallas import tpu as pltpu

F32 = jnp.float32
BF16 = jnp.bfloat16
HI = lax.Precision.HIGHEST
MESH = pl.DeviceIdType.MESH

N_DEV = 8
D_MODEL = 1024
HEAD_DIM = 64
N_HEADS = 8
KV_GROUP = 4
BLOCK = 128
REL_BUCKETS = 32
REL_MAX_EXACT = 16
REL_MAX_DIST = 128
LRU_BLOCKS = 8
LRU_BLOCK_W = 128
LRU_C = 8.0
EPS = 1e-6
SCALE = HEAD_DIM ** -0.5
NEG = -1e30

ADAM_LR = 0.001
ADAM_B1 = 0.9
ADAM_B2 = 0.999
ADAM_EPS = 1e-08
ADAM_WD = 0.01
ADAM_STEP = 10

C_BQ, C_BK, C_BV, C_AQ, C_GATE, C_AK, C_AV = 0, 512, 1024, 1536, 2048, 3072, 3200
N_MAIN = 3328
SHARD_W_IN = 417
SHARD_W_PAD = 512

TM = 512
TQ = 256
TK = 128
TKB = 256
TC = 512
SWA_SUB = 2
VMEM_BIG = 56 * 1024 * 1024
VMEM_MID = 40 * 1024 * 1024


def _pallas(body, **kw):
    return pl.pallas_call(body, **kw)


def _cp(sem=None, vmem=None):
    kw = {}
    if sem is not None:
        kw["dimension_semantics"] = sem
    if vmem is not None:
        kw["vmem_limit_bytes"] = vmem
    return pltpu.CompilerParams(**kw)


def _nn(a, b, precision=None):
    return jnp.dot(a, b, preferred_element_type=F32, precision=precision)


def _nt(a, b, precision=None):
    return lax.dot_general(a, b, (((1,), (1,)), ((), ())), preferred_element_type=F32, precision=precision)


def _tn(a, b, precision=None):
    return lax.dot_general(a, b, (((0,), (0,)), ((), ())), preferred_element_type=F32, precision=precision)


def _sigmoid(x):
    return 1.0 / (1.0 + jnp.exp(-x))


def _silu(x):
    return x * _sigmoid(x)


def _dsilu(x):
    s = _sigmoid(x)
    return s * (1.0 + x * (1.0 - s))


def _col(tile, idx):
    lane = lax.broadcasted_iota(jnp.int32, tile.shape, 1)
    return jnp.sum(jnp.where(lane == idx, tile, 0.0), axis=1, keepdims=True)


def _exchange(name, gathers, scatters, axes=("x", "y", "c"), chunks=1):
    ng, n = len(gathers), len(gathers) + len(scatters)
    ins = list(gathers) + list(scatters)
    group = 2 ** len(axes)

    def body(*refs):
        in_refs, out_refs = refs[:n], refs[n:2 * n]
        send_sems, recv_sems, loc_sems = refs[2 * n:]
        coord = {a: lax.axis_index(a) for a in ("x", "y", "c")}

        def member(r):
            pc = dict(coord)
            idx = 0
            for k, a in enumerate(axes):
                if r & (1 << (len(axes) - 1 - k)):
                    pc[a] = 1 - coord[a]
                idx = 2 * idx + pc[a]
            return (pc["x"], pc["y"], pc["c"]), idx

        _, me = member(0)

        def peer(r):
            return member(r)

        local, sends, recvs = [], [], []
        for k in range(n):
            mine = in_refs[k] if k < ng else in_refs[k].at[me]
            cp = pltpu.make_async_copy(mine, out_refs[k].at[me], loc_sems.at[k])
            cp.start()
            local.append(cp)
            lead = mine.shape[0]
            nchunk = max(q for q in range(1, chunks + 1) if lead % q == 0)
            step = lead // nchunk
            for r in range(1, group):
                pid, pidx = peer(r)
                src = in_refs[k] if k < ng else in_refs[k].at[pidx]
                for q in range(nchunk):
                    rows = pl.ds(q * step, step)
                    sems = dict(send_sem=send_sems.at[r - 1, k, q], recv_sem=recv_sems.at[r - 1, k, q],
                                device_id=pid, device_id_type=MESH)
                    snd = pltpu.make_async_remote_copy(src_ref=src.at[rows], dst_ref=out_refs[k].at[me].at[rows], **sems)
                    snd.start()
                    sends.append(snd)
                    recvs.append(pltpu.make_async_remote_copy(
                        src_ref=src.at[rows], dst_ref=out_refs[k].at[pidx].at[rows], **sems))
        for rc in recvs:
            rc.wait_recv()
        for snd in sends:
            snd.wait_send()
        for cp in local:
            cp.wait()

    out_shape = [jax.ShapeDtypeStruct((group,) + a.shape, a.dtype) for a in gathers]
    out_shape += [jax.ShapeDtypeStruct(a.shape, a.dtype) for a in scatters]
    any_spec = pl.BlockSpec(memory_space=pl.ANY)
    return _pallas(
        body, name=name, out_shape=out_shape,
        in_specs=[any_spec] * n, out_specs=[any_spec] * n,
        scratch_shapes=[pltpu.SemaphoreType.DMA((group - 1, n, chunks)), pltpu.SemaphoreType.DMA((group - 1, n, chunks)),
                        pltpu.SemaphoreType.DMA((n,))],
    )(*ins)


def _peer_of(r):
    x, y, c = lax.axis_index("x"), lax.axis_index("y"), lax.axis_index("c")
    px = 1 - x if r & 4 else x
    py = 1 - y if r & 2 else y
    pc = 1 - c if r & 1 else c
    return (px, py, pc), 4 * px + 2 * py + pc


def _split_copies(in_refs, land_refs, send_sems, recv_sems, ng, with_recv):
    _, me = _peer_of(0)
    pairs = []
    for k, (src_ref, land) in enumerate(zip(in_refs, land_refs)):
        for r in range(1, N_DEV):
            pid, pidx = _peer_of(r)
            src = src_ref if k < ng else src_ref.at[pidx]
            slot = (N_DEV - 1) * k + r - 1
            sems = dict(send_sem=send_sems.at[slot], recv_sem=recv_sems.at[slot], device_id=pid, device_id_type=MESH)
            send = pltpu.make_async_remote_copy(src_ref=src, dst_ref=land.at[me], **sems)
            recv = pltpu.make_async_remote_copy(src_ref=src, dst_ref=land.at[pidx], **sems) if with_recv else None
            pairs.append((send, recv))
    return pairs


def _exchange_start(name, gathers, scatters, after):
    ng, n = len(gathers), len(gathers) + len(scatters)
    ins = list(gathers) + list(scatters)
    lands = [jax.ShapeDtypeStruct((N_DEV,) + a.shape, a.dtype) for a in gathers]
    lands += [jax.ShapeDtypeStruct(a.shape, a.dtype) for a in scatters]

    def body(*refs):
        in_refs, land_refs = refs[:n], refs[n:2 * n]
        send_sems, recv_sems = refs[2 * n + 1:2 * n + 3]
        token = refs[-1]
        for send, _ in _split_copies(in_refs, land_refs, send_sems, recv_sems, ng, False):
            send.start()
        token[...] = jnp.zeros_like(token)

    hbm = pl.BlockSpec(memory_space=pltpu.HBM)
    sem = pl.BlockSpec(memory_space=pltpu.SEMAPHORE)
    sem_shape = pltpu.SemaphoreType.DMA(((N_DEV - 1) * n,))
    out_shape = [sem_shape, sem_shape] + [pltpu.HBM(a.shape, a.dtype) for a in ins]
    out_shape += [pltpu.HBM(l.shape, l.dtype) for l in lands] + [jax.ShapeDtypeStruct((8, 128), F32)]
    args = [pltpu.with_memory_space_constraint(a, pltpu.HBM) for a in ins]
    args += [pltpu.with_memory_space_constraint(lax.empty(l.shape, l.dtype), pltpu.HBM) for l in lands]
    outs = _pallas(
        body, name=name, out_shape=out_shape,
        in_specs=[hbm] * (2 * n) + [pl.BlockSpec(memory_space=pl.ANY)],
        out_specs=[sem, sem] + [hbm] * (2 * n) + [pl.BlockSpec(memory_space=pltpu.VMEM)],
        input_output_aliases={i: 2 + i for i in range(2 * n)},
        compiler_params=pltpu.CompilerParams(has_side_effects=pltpu.SideEffectType.DATAFLOW_SIDE_EFFECTING),
    )(*args, after)
    return (outs[0], outs[1], list(outs[2:2 + n]), list(outs[2 + n:2 + 2 * n]), ng), outs[-1]


def _exchange_wait(name, handle, after):
    send_sems, recv_sems, srcs, lands, ng = handle
    n = len(srcs)

    def body(*refs):
        in_refs, land_refs = refs[:n], refs[n:2 * n]
        send_ref, recv_ref = refs[2 * n:2 * n + 2]
        for send, recv in _split_copies(in_refs, land_refs, send_ref, recv_ref, ng, True):
            send.wait_send()
            recv.wait_recv()

    hbm = pl.BlockSpec(memory_space=pltpu.HBM)
    sem = pl.BlockSpec(memory_space=pltpu.SEMAPHORE)
    outs = _pallas(
        body, name=name, out_shape=[pltpu.HBM(a.shape, a.dtype) for a in srcs + lands],
        in_specs=[hbm] * (2 * n) + [sem, sem, pl.BlockSpec(memory_space=pl.ANY)],
        out_specs=[hbm] * (2 * n), input_output_aliases={i: i for i in range(2 * n)},
        compiler_params=pltpu.CompilerParams(has_side_effects=pltpu.SideEffectType.DATAFLOW_SIDE_EFFECTING),
    )(*srcs, *lands, send_sems, recv_sems, after)
    return list(outs[n:])


def _with_own(land, own, me):
    return lax.dynamic_update_slice(land, own[None], (me,) + (0,) * own.ndim)


def _ada_mod(c_all, ada_w, ada_b_slice):
    def body(c_ref, w_ref, b_ref, o_ref):
        ca = _silu(c_ref[...])
        for l in range(2):
            o_ref[l] = _nn(ca, w_ref[l], HI) + b_ref[l]

    return _pallas(body, name="ada_mod",
                   out_shape=jax.ShapeDtypeStruct((2, c_all.shape[0], ada_w.shape[2]), F32),
                   compiler_params=_cp(vmem=VMEM_MID))(c_all, ada_w, ada_b_slice)


def _ada_w_grad(c_all, dmod_slice):
    def body(c_ref, d_ref, o_ref):
        ca = _silu(c_ref[...])
        for l in range(2):
            o_ref[l] = _tn(ca, d_ref[l], HI)

    return _pallas(body, name="ada_w_grad",
                   out_shape=jax.ShapeDtypeStruct((2, D_MODEL, dmod_slice.shape[2]), F32),
                   compiler_params=_cp(vmem=VMEM_MID))(c_all, dmod_slice)


def _bucket_onehot():
    qi = jnp.arange(BLOCK)[:, None]
    kj = jnp.arange(2 * BLOCK)[None, :]
    rel = qi - kj + BLOCK
    n = jnp.maximum(rel, 0)
    nf = jnp.maximum(n, 1).astype(F32)
    large = REL_MAX_EXACT + (jnp.log(nf / REL_MAX_EXACT) / math.log(REL_MAX_DIST / REL_MAX_EXACT)
                             * (REL_BUCKETS - REL_MAX_EXACT)).astype(jnp.int32)
    large = jnp.minimum(large, REL_BUCKETS - 1)
    bucket = jnp.where(n < REL_MAX_EXACT, n, large).reshape(1, BLOCK * 2 * BLOCK)
    return (jnp.arange(REL_BUCKETS)[:, None] == bucket).astype(F32)


def _bias_expand(rel_bias_t, onehot):
    def body(r_ref, e_ref, o_ref):
        o_ref[...] = _nn(r_ref[...], e_ref[...], HI)

    return _pallas(body, name="bias_expand",
                   out_shape=jax.ShapeDtypeStruct((N_HEADS, onehot.shape[1]), F32),
                   compiler_params=_cp(vmem=VMEM_MID))(rel_bias_t, onehot)


def _bias_reduce(dbias, onehot):
    def body(d_ref, e_ref, o_ref):
        o_ref[...] = _nt(d_ref[...], e_ref[...], HI)

    return _pallas(body, name="bias_reduce",
                   out_shape=jax.ShapeDtypeStruct((N_HEADS, REL_BUCKETS), F32),
                   compiler_params=_cp(vmem=VMEM_MID))(dbias, onehot)


def _norm_proj(name, x, g, shift, scale, w, seq, out_dtype, wf_t=None):
    t_tok = x.shape[0]
    w3d = w.ndim == 3
    n_out = w.shape[0] * w.shape[2] if w3d else w.shape[1]
    cn = w.shape[2] if w3d else 256

    def body(x_ref, g_ref, sh_ref, sc_ref, w_ref, *rest):
        if wf_t is not None:
            wf_ref, h_ref, o_ref, fl_ref = rest
        else:
            h_ref, o_ref = rest
        xv = x_ref[...]
        rstd = lax.rsqrt(jnp.mean(xv * xv, axis=-1, keepdims=True) + EPS)
        h = (xv * rstd) * g_ref[...] * (1.0 + sc_ref[...]) + sh_ref[...]
        hb = h.astype(BF16)
        h_ref[...] = hb
        for j in range(n_out // cn):
            wj = w_ref[j] if w3d else w_ref[:, j * cn:(j + 1) * cn]
            o_ref[:, j * cn:(j + 1) * cn] = _nn(hb, wj).astype(out_dtype)
        if wf_t is not None:
            fl_ref[...] = _nt(wf_ref[...], hb)

    mod_spec = pl.BlockSpec((None, 1, D_MODEL), lambda i: (i * TM // seq, 0, 0))
    w_spec = (pl.BlockSpec(w.shape, lambda i: (0, 0, 0)) if w3d else pl.BlockSpec(w.shape, lambda i: (0, 0)))
    in_specs = [pl.BlockSpec((TM, D_MODEL), lambda i: (i, 0)), pl.BlockSpec((1, D_MODEL), lambda i: (0, 0)),
                mod_spec, mod_spec, w_spec]
    out_shape = [jax.ShapeDtypeStruct((t_tok, D_MODEL), BF16), jax.ShapeDtypeStruct((t_tok, n_out), out_dtype)]
    out_specs = [pl.BlockSpec((TM, D_MODEL), lambda i: (i, 0)), pl.BlockSpec((TM, n_out), lambda i: (i, 0))]
    args = [x, g, shift, scale, w]
    if wf_t is not None:
        in_specs.append(pl.BlockSpec(wf_t.shape, lambda i: (0, 0)))
        out_shape.append(jax.ShapeDtypeStruct((wf_t.shape[0], t_tok), F32))
        out_specs.append(pl.BlockSpec((wf_t.shape[0], TM), lambda i: (0, i)))
        args.append(wf_t)
    return _pallas(body, name=name, grid=(t_tok // TM,), in_specs=in_specs, out_specs=out_specs,
                   out_shape=out_shape, compiler_params=_cp(("arbitrary",), VMEM_BIG))(*args)


def _fox_prep(fl_t, b_f, seq):
    t_tok = fl_t.shape[1]
    ch = 256

    def body(fl_ref, bf_ref, fr_ref, fc_ref):
        z = fl_ref[...] + bf_ref[...]
        logf = jnp.minimum(z, 0.0) - jnp.log(1.0 + jnp.exp(-jnp.abs(z)))
        ri = lax.broadcasted_iota(jnp.int32, (ch, ch), 0)
        ci = lax.broadcasted_iota(jnp.int32, (ch, ch), 1)
        upper = (ri <= ci).astype(F32)
        eye = (ri == ci).astype(F32)
        carry = jnp.zeros((N_HEADS, 1), F32)
        for k in range(seq // ch):
            fk = _nn(logf[:, k * ch:(k + 1) * ch], upper, HI) + carry
            carry = fk[:, ch - 1:ch]
            fr_ref[:, k * ch:(k + 1) * ch] = fk
            padded = jnp.concatenate([fk, jnp.zeros((128 - N_HEADS, ch), F32)], axis=0)
            fc_ref[k * ch:(k + 1) * ch, :] = _nt(eye, padded, HI)

    return _pallas(
        body, name="fox_prep", grid=(t_tok // seq,),
        in_specs=[pl.BlockSpec((N_HEADS, seq), lambda b: (0, b)), pl.BlockSpec((N_HEADS, 1), lambda b: (0, 0))],
        out_specs=[pl.BlockSpec((N_HEADS, seq), lambda b: (0, b)), pl.BlockSpec((seq, 128), lambda b: (b, 0))],
        out_shape=[jax.ShapeDtypeStruct((N_HEADS, t_tok), F32), jax.ShapeDtypeStruct((t_tok, 128), F32)],
        compiler_params=_cp(("arbitrary",), VMEM_MID))(fl_t, b_f)


def _fox_post(df_row, fl_t, b_f, seq):
    t_tok = fl_t.shape[1]
    ch = 256

    def body(d_ref, fl_ref, bf_ref, o_ref, db_ref):
        @pl.when(pl.program_id(0) == 0)
        def _():
            db_ref[...] = jnp.zeros_like(db_ref)

        z = fl_ref[...] + bf_ref[...]
        sig_neg = 1.0 / (1.0 + jnp.exp(z))
        ri = lax.broadcasted_iota(jnp.int32, (ch, ch), 0)
        ci = lax.broadcasted_iota(jnp.int32, (ch, ch), 1)
        lower = (ri >= ci).astype(F32)
        carry = jnp.zeros((N_HEADS, 1), F32)
        tot = jnp.zeros((N_HEADS, 1), F32)
        for k in reversed(range(seq // ch)):
            dk = _nn(d_ref[:, k * ch:(k + 1) * ch], lower, HI) + carry
            carry = dk[:, 0:1]
            dfl = dk * sig_neg[:, k * ch:(k + 1) * ch]
            o_ref[:, k * ch:(k + 1) * ch] = dfl
            tot = tot + jnp.sum(dfl, axis=1, keepdims=True)
        db_ref[...] += jnp.broadcast_to(tot, db_ref.shape)

    return _pallas(
        body, name="fox_post", grid=(t_tok // seq,),
        in_specs=[pl.BlockSpec((N_HEADS, seq), lambda b: (0, b)), pl.BlockSpec((N_HEADS, seq), lambda b: (0, b)),
                  pl.BlockSpec((N_HEADS, 1), lambda b: (0, 0))],
        out_specs=[pl.BlockSpec((N_HEADS, seq), lambda b: (0, b)), pl.BlockSpec((N_HEADS, 128), lambda b: (0, 0))],
        out_shape=[jax.ShapeDtypeStruct((N_HEADS, t_tok), F32), jax.ShapeDtypeStruct((N_HEADS, 128), F32)],
        compiler_params=_cp(("arbitrary",), VMEM_MID))(df_row, fl_t, b_f)


def _eye(n, dtype):
    return (lax.broadcasted_iota(jnp.int32, (n, n), 0) == lax.broadcasted_iota(jnp.int32, (n, n), 1)).astype(dtype)


def _fox_aug(qkvg, f_col, seq):
    t_tok = qkvg.shape[0]
    ta = 256
    nkb = ta // TK

    def body(q_ref, k_ref, v_ref, fc_ref, qa_ref, ka_ref, kt_ref, vt_ref):
        ri = lax.broadcasted_iota(jnp.int32, (128, 128), 0)
        ci = lax.broadcasted_iota(jnp.int32, (128, 128), 1)
        eye = (ri == ci).astype(BF16)
        lane = lax.broadcasted_iota(jnp.int32, (ta, 128), 1)
        ones_q = jnp.where(jnp.logical_and(lane >= 64, lane < 67), 1.0, 0.0)
        ones_k = jnp.where(jnp.logical_and(lane >= 67, lane < 70), 1.0, 0.0)
        fc_tile = fc_ref[...]
        for p in range(N_HEADS // 2):
            q2 = q_ref[:, 128 * p:128 * (p + 1)]
            k2 = k_ref[:, 128 * p:128 * (p + 1)]
            vt = _nt(eye, v_ref[:, 128 * p:128 * (p + 1)]).astype(BF16)
            for kk in range(nkb):
                vt_ref[p, kk] = vt[:, kk * TK:(kk + 1) * TK]
            for e in range(2):
                h = 2 * p + e
                sel = jnp.logical_and(ri == ci + HEAD_DIM * e, ci < HEAD_DIM)
                f = _col(fc_tile, h)
                fh = f.astype(BF16).astype(F32)
                fm = (f - fh).astype(BF16).astype(F32)
                fl = (f - fh - fm).astype(BF16).astype(F32)
                qa = (_nn(q2, jnp.where(sel, SCALE, 0.0).astype(BF16)) + ones_q + jnp.where(lane == 67, fh, 0.0)
                      + jnp.where(lane == 68, fm, 0.0) + jnp.where(lane == 69, fl, 0.0))
                ka = (_nn(k2, jnp.where(sel, 1.0, 0.0).astype(BF16)) + ones_k - jnp.where(lane == 64, fh, 0.0)
                      - jnp.where(lane == 65, fm, 0.0) - jnp.where(lane == 66, fl, 0.0))
                qa_ref[h] = qa.astype(BF16)
                kab = ka.astype(BF16)
                ka_ref[h] = kab
                kt = _nt(eye, kab).astype(BF16)
                for kk in range(ta // TKB):
                    kt_ref[h, kk] = kt[:, kk * TKB:(kk + 1) * TKB]

    aug = jax.ShapeDtypeStruct((N_HEADS, t_tok, 128), BF16)
    return _pallas(
        body, name="fox_aug", grid=(t_tok // ta,),
        in_specs=[pl.BlockSpec((ta, 512), lambda i: (i, C_BQ // 512)), pl.BlockSpec((ta, 512), lambda i: (i, C_BK // 512)),
                  pl.BlockSpec((ta, 512), lambda i: (i, C_BV // 512)), pl.BlockSpec((ta, 128), lambda i: (i, 0))],
        out_specs=[pl.BlockSpec((N_HEADS, ta, 128), lambda i: (0, i, 0)), pl.BlockSpec((N_HEADS, ta, 128), lambda i: (0, i, 0)),
                   pl.BlockSpec((N_HEADS, ta // TKB, 128, TKB), lambda i: (0, i, 0, 0)),
                   pl.BlockSpec((N_HEADS // 2, nkb, 128, TK), lambda i: (0, i, 0, 0))],
        out_shape=[aug, aug, jax.ShapeDtypeStruct((N_HEADS, t_tok // TKB, 128, TKB), BF16),
                   jax.ShapeDtypeStruct((N_HEADS // 2, t_tok // TK, 128, TK), BF16)],
        compiler_params=_cp(("arbitrary",), VMEM_MID))(qkvg, qkvg, qkvg, f_col)


def _fox_fwd_t(q_aug, k_aug, vt, seq):
    t_tok = k_aug.shape[1]
    nq = seq // TQ
    ratio = TQ // TK
    assert ratio == 2, "the two pipeline slots are addressed by the key block's parity"

    def body(qa_ref, ka_ref, vt_ref, o_ref, lse_ref, ml_s, acc_s, st_s, p_s, al_s, qt_s):
        i = pl.program_id(1)
        tpos = i * TQ + lax.broadcasted_iota(jnp.int32, (1, TQ), 1)
        eye = _eye(HEAD_DIM, BF16)
        eye2 = _eye(128, BF16)
        for h in range(N_HEADS):
            qt_s[h] = _nt(eye2, qa_ref[h]).astype(BF16)
            ml_s[0, h] = jnp.full((1, TQ), NEG, F32)
            ml_s[1, h] = jnp.zeros((1, TQ), F32)
            acc_s[h] = jnp.zeros((HEAD_DIM, TQ), F32)
            p_s[1, h] = jnp.zeros((TK, TQ), BF16)
            al_s[1, h] = jnp.ones((1, TQ), F32)

        def scores(j, slot):
            row0 = pl.multiple_of(j * TK, TK)
            for h in range(N_HEADS):
                st_s[slot, h] = _nn(ka_ref[h, pl.ds(row0, TK), :], qt_s[h])

        def softmax(j, slot, masked):
            if masked:
                keep = (j * TK + lax.broadcasted_iota(jnp.int32, (TK, 1), 0)) <= tpos
            for h in range(N_HEADS):
                st = st_s[slot, h]
                if masked:
                    st = jnp.where(keep, st, NEG)
                m = ml_s[0, h]
                m_new = jnp.maximum(m, jnp.max(st, axis=0, keepdims=True))
                alpha = jnp.exp(m - m_new)
                pe = jnp.exp(st - m_new)
                ml_s[0, h] = m_new
                ml_s[1, h] = alpha * ml_s[1, h] + jnp.sum(pe, axis=0, keepdims=True)
                al_s[slot, h] = alpha
                p_s[slot, h] = pe.astype(BF16)

        def values(j, slot):
            jv = jnp.maximum(j, 0)
            for h in range(N_HEADS):
                p, e = divmod(h, 2)
                acc_s[h] = al_s[slot, h] * acc_s[h] + _nn(vt_ref[p, jv, e * HEAD_DIM:(e + 1) * HEAD_DIM, :], p_s[slot, h])

        def step(m, carry):
            for kk in range(ratio):
                j = ratio * m + kk
                values(j - 1, 1 - kk)
                softmax(j, kk, False)
                scores(j + 1, 1 - kk)
            return carry

        scores(0, 0)
        lax.fori_loop(0, i, step, 0)
        for kk in range(ratio):
            j = ratio * i + kk
            values(j - 1, 1 - kk)
            softmax(j, kk, True)
            if kk < ratio - 1:
                scores(j + 1, 1 - kk)
        values(ratio * i + ratio - 1, ratio - 1)
        for p in range(N_HEADS // 2):
            outs = []
            for e in range(2):
                h = 2 * p + e
                l = ml_s[1, h]
                outs.append(_tn((acc_s[h] / l).astype(BF16), eye))
                lse_ref[p, e:e + 1, :] = ml_s[0, h] + jnp.log(l)
            o_ref[:, 128 * p:128 * (p + 1)] = jnp.concatenate(outs, axis=1).astype(BF16)

    return _pallas(
        body, name="fox_fwd", grid=(t_tok // seq, nq),
        in_specs=[pl.BlockSpec((N_HEADS, TQ, 128), lambda b, i: (0, b * nq + i, 0)),
                  pl.BlockSpec((N_HEADS, seq, 128), lambda b, i: (0, b, 0)),
                  pl.BlockSpec((N_HEADS // 2, seq // TK, 128, TK), lambda b, i: (0, b, 0, 0))],
        out_specs=[pl.BlockSpec((TQ, 512), lambda b, i: (b * nq + i, 0)),
                   pl.BlockSpec((N_HEADS // 2, 2, TQ), lambda b, i: (0, 0, b * nq + i))],
        out_shape=[jax.ShapeDtypeStruct((t_tok, 512), BF16), jax.ShapeDtypeStruct((N_HEADS // 2, 2, t_tok), F32)],
        scratch_shapes=[pltpu.VMEM((2, N_HEADS, 1, TQ), F32), pltpu.VMEM((N_HEADS, HEAD_DIM, TQ), F32),
                        pltpu.VMEM((2, N_HEADS, TK, TQ), F32), pltpu.VMEM((2, N_HEADS, TK, TQ), BF16),
                        pltpu.VMEM((2, N_HEADS, 1, TQ), F32), pltpu.VMEM((N_HEADS, 128, TQ), BF16)],
        compiler_params=_cp(("arbitrary", "arbitrary"), VMEM_MID))(q_aug, k_aug, vt)


def _fox_bwd_t(q_aug, k_aug, kt, qkvg, du_b, b_out, lse, seq):
    TK = TKB
    t_tok = qkvg.shape[0]
    nq = seq // TQ
    nkb = seq // TK
    ratio = TQ // TK
    hg = 4

    def body(qa_ref, ka_ref, kt_ref, v_ref, do_ref, o_ref, lse_ref, dq_ref, dk_ref, dv_ref, df_ref,
             dqt_s, row_s, dfk_s, dk_s, dv_s, dot_s, st_s, dp_s, pb_s, db_s, qt_s):
        eye = _eye(HEAD_DIM, BF16)
        eye2 = _eye(128, BF16)
        eye_k = _eye(TK, F32)
        lane8 = lax.broadcasted_iota(jnp.int32, (8, 128), 1)
        lane_k = lax.broadcasted_iota(jnp.int32, (TK, 128), 1)
        first = [lane8 < HEAD_DIM, lane8 >= HEAD_DIM]
        for pp in range(hg // 2):
            for ii in range(nq):
                dot_s[pp, ii] = _nt(eye2, do_ref[ii * TQ:(ii + 1) * TQ, 128 * pp:128 * (pp + 1)]).astype(BF16)
        for hh in range(hg):
            for ii in range(nq):
                qt_s[hh, ii] = _nt(eye2, qa_ref[hh, ii * TQ:(ii + 1) * TQ, :]).astype(BF16)
        for hh in range(hg):
            pp, e = divmod(hh, 2)
            head_lanes = jnp.where(first[e], 1.0, 0.0)
            for ii in range(nq):
                rows = slice(ii * TQ, (ii + 1) * TQ)
                prod = do_ref[rows, 128 * pp:128 * (pp + 1)].astype(F32) * o_ref[rows, 128 * pp:128 * (pp + 1)].astype(F32)
                row_s[hh, ii, 0] = _nt(head_lanes, prod, HI)
                row_s[hh, ii, 1] = jnp.broadcast_to(lse_ref[pp, e:e + 1, ii * TQ:(ii + 1) * TQ], (8, TQ))
                dqt_s[hh, ii] = jnp.zeros((128, TQ), F32)

        def kblock(j, _):
            krow = pl.multiple_of(j * TK, TK)
            spos = j * TK + lax.broadcasted_iota(jnp.int32, (TK, 1), 0)
            for hh in range(hg):
                dk_s[hh] = jnp.zeros((TK, 128), F32)
                dv_s[hh] = jnp.zeros((TK, 128), F32)

            def scores(i, slot):
                for hh in range(hg):
                    pp, e = divmod(hh, 2)
                    own = (lane_k < HEAD_DIM) if e == 0 else (lane_k >= HEAD_DIM)
                    v2 = v_ref[pl.ds(krow, TK), 128 * pp:128 * (pp + 1)]
                    vj = jnp.where(own, v2, jnp.zeros_like(v2))
                    st_s[slot, hh] = _nn(ka_ref[hh, pl.ds(krow, TK), :], qt_s[hh, i])
                    dp_s[slot, hh] = _nn(vj, dot_s[pp, i])

            def elementwise(i, slot, masked):
                if masked:
                    keep = spos <= (i * TQ + lax.broadcasted_iota(jnp.int32, (1, TQ), 1))
                for hh in range(hg):
                    pt = jnp.exp(st_s[slot, hh] - row_s[hh, i, 1][0:1, :])
                    if masked:
                        pt = jnp.where(keep, pt, 0.0)
                    dst = pt * (dp_s[slot, hh] - row_s[hh, i, 0][0:1, :])
                    pb_s[slot, hh] = pt.astype(BF16)
                    db_s[slot, hh] = dst.astype(BF16)

            def grads(i, slot):
                qrow = pl.multiple_of(i * TQ, TQ)
                for hh in range(hg):
                    dst_b = db_s[slot, hh]
                    dv_s[hh] += _nn(pb_s[slot, hh], do_ref[pl.ds(qrow, TQ), 128 * (hh // 2):128 * (hh // 2 + 1)])
                    dk_s[hh] += _nn(dst_b, qa_ref[hh, pl.ds(qrow, TQ), :])
                    dqt_s[hh, i] += _nn(kt_ref[hh, j], dst_b)

            def step(p, carry):
                i = i0 + 2 * p + 1
                grads(i - 1, 0)
                elementwise(i, 1, False)
                scores(i + 1, 0)
                grads(i, 1)
                elementwise(i + 1, 0, False)
                scores(jnp.minimum(i + 2, nq - 1), 1)
                return carry

            i0 = j // ratio
            rest = nq - 1 - i0
            scores(i0, 0)
            elementwise(i0, 0, True)
            scores(jnp.minimum(i0 + 1, nq - 1), 1)
            lax.fori_loop(0, rest // 2, step, 0)

            @pl.when(rest % 2 == 1)
            def _():
                grads(nq - 2, 0)
                elementwise(nq - 1, 1, False)
                grads(nq - 1, 1)

            @pl.when(rest % 2 == 0)
            def _():
                grads(nq - 1, 0)
            for pp in range(hg // 2):
                cols = slice(128 * pp, 128 * (pp + 1))
                dk_ref[pl.ds(krow, TK), cols] = jnp.concatenate(
                    [dk_s[2 * pp][:, :HEAD_DIM], dk_s[2 * pp + 1][:, :HEAD_DIM]], axis=1).astype(BF16)
                dv_ref[pl.ds(krow, TK), cols] = jnp.where(lane_k < HEAD_DIM, dv_s[2 * pp], dv_s[2 * pp + 1]).astype(BF16)
            for hh in range(hg):
                dfk_s[hh, j] = _tn(dk_s[hh][:, HEAD_DIM:HEAD_DIM + 8], eye_k, HI)
            return 0

        lax.fori_loop(0, nkb, kblock, 0)
        for pp in range(hg // 2):
            for ii in range(nq):
                parts = []
                for e in range(2):
                    dqt = dqt_s[2 * pp + e, ii]
                    parts.append(_tn(dqt[0:HEAD_DIM, :].astype(BF16), eye) * SCALE)
                    for kk in range(ratio):
                        jj = ii * ratio + kk
                        df_ref[pp, e:e + 1, jj * TK:(jj + 1) * TK] = (dqt[67:68, kk * TK:(kk + 1) * TK]
                                                                     - dfk_s[2 * pp + e, jj][0:1, :])
                dq_ref[ii * TQ:(ii + 1) * TQ, 128 * pp:128 * (pp + 1)] = jnp.concatenate(parts, axis=1).astype(BF16)

    aug_blk = pl.BlockSpec((hg, seq, 128), lambda b, g: (g, b, 0))
    pair_blk = pl.BlockSpec((seq, 64 * hg), lambda b, g: (b, g))
    row_blk = pl.BlockSpec((hg // 2, 2, seq), lambda b, g: (g, 0, b))
    return _pallas(
        body, name="fox_bwd", grid=(t_tok // seq, N_HEADS // hg),
        in_specs=[aug_blk, aug_blk, pl.BlockSpec((hg, nkb, 128, TK), lambda b, g: (g, b, 0, 0)),
                  pl.BlockSpec((seq, 64 * hg), lambda b, g: (b, C_BV // (64 * hg) + g)), pair_blk, pair_blk, row_blk],
        out_specs=[pair_blk, pair_blk, pair_blk, row_blk],
        out_shape=[jax.ShapeDtypeStruct((t_tok, 512), BF16)] * 3
        + [jax.ShapeDtypeStruct((N_HEADS // 2, 2, t_tok), F32)],
        scratch_shapes=[pltpu.VMEM((hg, nq, 128, TQ), F32), pltpu.VMEM((hg, nq, 2, 8, TQ), F32),
                        pltpu.VMEM((hg, nkb, 8, TK), F32), pltpu.VMEM((hg, TK, 128), F32),
                        pltpu.VMEM((hg, TK, 128), F32), pltpu.VMEM((hg // 2, nq, 128, TQ), BF16),
                        pltpu.VMEM((2, hg, TK, TQ), F32), pltpu.VMEM((2, hg, TK, TQ), F32),
                        pltpu.VMEM((2, hg, TK, TQ), BF16), pltpu.VMEM((2, hg, TK, TQ), BF16),
                        pltpu.VMEM((hg, nq, 128, TQ), BF16)],
        compiler_params=_cp(("arbitrary", "arbitrary"), VMEM_BIG))(q_aug, k_aug, kt, qkvg, du_b, b_out, lse)


def _swa_window(k_ref, v_ref, n):
    prev = pl.multiple_of(jnp.maximum(n - 1, 0) * BLOCK, BLOCK)
    cur = pl.multiple_of(n * BLOCK, BLOCK)
    kwin = jnp.concatenate([k_ref[pl.ds(prev, BLOCK), :], k_ref[pl.ds(cur, BLOCK), :]], axis=0)
    vwin = jnp.concatenate([v_ref[pl.ds(prev, BLOCK), :], v_ref[pl.ds(cur, BLOCK), :]], axis=0)
    ti = lax.broadcasted_iota(jnp.int32, (BLOCK, 2 * BLOCK), 0)
    sj = lax.broadcasted_iota(jnp.int32, (BLOCK, 2 * BLOCK), 1)
    rel = ti - sj + BLOCK
    first_key = jnp.where(n > 0, 0, BLOCK)
    mask = jnp.logical_and(jnp.logical_and(rel >= 0, rel < BLOCK), sj >= first_key)
    return kwin, vwin, mask, prev, cur


def _head_cols(ref, h):
    pair = ref[:, 128 * (h // 2):128 * (h // 2 + 1)]
    return pair[:, (h % 2) * HEAD_DIM:(h % 2 + 1) * HEAD_DIM]


def _swa_logits(q_ref, kwin, bias_ref, h, mask):
    hk = h // KV_GROUP
    s = _nt(_head_cols(q_ref, h), kwin[:, hk * HEAD_DIM:(hk + 1) * HEAD_DIM]) * SCALE + bias_ref[h]
    return jnp.where(mask, s, NEG)


def _swa_fwd(qkvg, bias, sinks, seq):
    t_tok = qkvg.shape[0]
    nb = seq // BLOCK

    def body(sink_ref, q_ref, k_ref, v_ref, bias_ref, o_ref, lse_ref, s_s, p_s, den_s):
        g = pl.program_id(1)
        subs = [pl.ds(s * BLOCK, BLOCK) for s in range(SWA_SUB)]
        wins = [_swa_window(k_ref, v_ref, SWA_SUB * g + s) for s in range(SWA_SUB)]
        for s in range(SWA_SUB):
            for h in range(N_HEADS):
                s_s[s * N_HEADS + h] = _swa_logits(q_ref.at[subs[s]], wins[s][0], bias_ref, h, wins[s][2])
        lane = lax.broadcasted_iota(jnp.int32, (BLOCK, 128), 1)
        for s in range(SWA_SUB):
            lse_tile = jnp.zeros((BLOCK, 128), F32)
            for h in range(N_HEADS):
                sc = s_s[s * N_HEADS + h]
                sink = sink_ref[h]
                m = jnp.maximum(jnp.max(sc, axis=1, keepdims=True), sink)
                pe = jnp.exp(sc - m)
                den = jnp.sum(pe, axis=1, keepdims=True) + jnp.exp(sink - m)
                p_s[s * N_HEADS + h] = pe.astype(BF16)
                den_s[s * N_HEADS + h] = den
                lse_tile = jnp.where(lane == h, m + jnp.log(den), lse_tile)
            lse_ref[subs[s], :] = lse_tile
        for s in range(SWA_SUB):
            vwin = wins[s][1]
            for pr in range(N_HEADS // 2):
                outs = []
                for h in (2 * pr, 2 * pr + 1):
                    hk = h // KV_GROUP
                    outs.append(_nn(p_s[s * N_HEADS + h], vwin[:, hk * HEAD_DIM:(hk + 1) * HEAD_DIM]) / den_s[s * N_HEADS + h])
                o_ref[subs[s], 128 * pr:128 * (pr + 1)] = jnp.concatenate(outs, axis=1).astype(BF16)

    rows = SWA_SUB * BLOCK
    steps = nb // SWA_SUB
    return _pallas(
        body, name="swa_fwd", grid=(t_tok // seq, steps),
        in_specs=[pl.BlockSpec(memory_space=pltpu.SMEM),
                  pl.BlockSpec((rows, 512), lambda b, n: (b * steps + n, C_AQ // 512)),
                  pl.BlockSpec((seq, 128), lambda b, n: (b, C_AK // 128)),
                  pl.BlockSpec((seq, 128), lambda b, n: (b, C_AV // 128)),
                  pl.BlockSpec((N_HEADS, BLOCK, 2 * BLOCK), lambda b, n: (0, 0, 0))],
        out_specs=[pl.BlockSpec((rows, 512), lambda b, n: (b * steps + n, 0)),
                   pl.BlockSpec((rows, 128), lambda b, n: (b * steps + n, 0))],
        out_shape=[jax.ShapeDtypeStruct((t_tok, 512), BF16), jax.ShapeDtypeStruct((t_tok, 128), F32)],
        scratch_shapes=[pltpu.VMEM((SWA_SUB * N_HEADS, BLOCK, 2 * BLOCK), F32),
                        pltpu.VMEM((SWA_SUB * N_HEADS, BLOCK, 2 * BLOCK), BF16),
                        pltpu.VMEM((SWA_SUB * N_HEADS, BLOCK, 1), F32)],
        compiler_params=_cp(("arbitrary", "arbitrary"), VMEM_MID))(sinks, qkvg, qkvg, qkvg, bias)


def _swa_bwd(qkvg, du_a, a_out, lse, bias, sinks, seq):
    t_tok = qkvg.shape[0]
    nb = seq // BLOCK

    def body(sink_ref, q_ref, k_ref, v_ref, do_ref, o_ref, lse_ref, bias_ref,
             dq_ref, dkv_ref, dbias_ref, dsink_ref, kv_s, s_s, dp_s, pb_s, db_s):
        b, n = pl.program_id(0), pl.program_id(1)

        @pl.when(jnp.logical_and(b == 0, n == 0))
        def _():
            dbias_ref[...] = jnp.zeros_like(dbias_ref)
            dsink_ref[...] = jnp.zeros_like(dsink_ref)

        @pl.when(n == 0)
        def _():
            kv_s[...] = jnp.zeros_like(kv_s)

        subs = [pl.ds(s * BLOCK, BLOCK) for s in range(SWA_SUB)]
        wins = [_swa_window(k_ref, v_ref, SWA_SUB * n + s) for s in range(SWA_SUB)]
        for s in range(SWA_SUB):
            kwin, vwin, mask = wins[s][:3]
            for h in range(N_HEADS):
                hk = h // KV_GROUP
                s_s[s * N_HEADS + h] = _swa_logits(q_ref.at[subs[s]], kwin, bias_ref, h, mask)
                dp_s[s * N_HEADS + h] = _nt(_head_cols(do_ref.at[subs[s]], h), vwin[:, hk * HEAD_DIM:(hk + 1) * HEAD_DIM])
        for s in range(SWA_SUB):
            lse_tile = lse_ref[subs[s], :]
            do_s, o_s = do_ref.at[subs[s]], o_ref.at[subs[s]]
            for h in range(N_HEADS):
                delta = jnp.sum(_head_cols(do_s, h).astype(F32) * _head_cols(o_s, h).astype(F32), axis=1, keepdims=True)
                lse_h = _col(lse_tile, h)
                pe = jnp.exp(s_s[s * N_HEADS + h] - lse_h)
                ds = pe * (dp_s[s * N_HEADS + h] - delta)
                dbias_ref[h] += ds
                psink = jnp.exp(sink_ref[h] - lse_h)
                dsink_ref[h:h + 1, :] += jnp.broadcast_to(jnp.sum(-psink * delta, axis=0, keepdims=True), (1, 128))
                pb_s[s * N_HEADS + h] = pe.astype(BF16)
                db_s[s * N_HEADS + h] = ds.astype(BF16)
        for s in range(SWA_SUB):
            kwin, _, _, prev, cur = wins[s]
            q_s, do_s = q_ref.at[subs[s]], do_ref.at[subs[s]]
            for pr in range(N_HEADS // 2):
                dqs = []
                for h in (2 * pr, 2 * pr + 1):
                    hk = h // KV_GROUP
                    dqs.append(_nn(db_s[s * N_HEADS + h], kwin[:, hk * HEAD_DIM:(hk + 1) * HEAD_DIM]) * SCALE)
                dq_ref[subs[s], 128 * pr:128 * (pr + 1)] = jnp.concatenate(dqs, axis=1).astype(BF16)
            dks, dvs = [], []
            for hk in range(N_HEADS // KV_GROUP):
                dk = jnp.zeros((2 * BLOCK, HEAD_DIM), F32)
                dv = jnp.zeros((2 * BLOCK, HEAD_DIM), F32)
                for h in range(hk * KV_GROUP, (hk + 1) * KV_GROUP):
                    dk = dk + _tn(db_s[s * N_HEADS + h], _head_cols(q_s, h))
                    dv = dv + _tn(pb_s[s * N_HEADS + h], _head_cols(do_s, h))
                dks.append(dk * SCALE)
                dvs.append(dv)
            upd = jnp.concatenate(dks + dvs, axis=1)
            kv_s[pl.ds(prev, BLOCK), :] += upd[:BLOCK]
            kv_s[pl.ds(cur, BLOCK), :] += upd[BLOCK:]

        @pl.when(n == steps - 1)
        def _():
            dkv_ref[...] = kv_s[...].astype(BF16)

    rows = SWA_SUB * BLOCK
    steps = nb // SWA_SUB
    tile = (SWA_SUB * N_HEADS, BLOCK, 2 * BLOCK)
    return _pallas(
        body, name="swa_bwd", grid=(t_tok // seq, steps),
        in_specs=[pl.BlockSpec(memory_space=pltpu.SMEM),
                  pl.BlockSpec((rows, 512), lambda b, n: (b * steps + n, C_AQ // 512)),
                  pl.BlockSpec((seq, 128), lambda b, n: (b, C_AK // 128)),
                  pl.BlockSpec((seq, 128), lambda b, n: (b, C_AV // 128)),
                  pl.BlockSpec((rows, 512), lambda b, n: (b * steps + n, 0)),
                  pl.BlockSpec((rows, 512), lambda b, n: (b * steps + n, 0)),
                  pl.BlockSpec((rows, 128), lambda b, n: (b * steps + n, 0)),
                  pl.BlockSpec((N_HEADS, BLOCK, 2 * BLOCK), lambda b, n: (0, 0, 0))],
        out_specs=[pl.BlockSpec((rows, 512), lambda b, n: (b * steps + n, 0)),
                   pl.BlockSpec((seq, 256), lambda b, n: (b, 0)),
                   pl.BlockSpec((N_HEADS, BLOCK, 2 * BLOCK), lambda b, n: (0, 0, 0)),
                   pl.BlockSpec((N_HEADS, 128), lambda b, n: (0, 0))],
        out_shape=[jax.ShapeDtypeStruct((t_tok, 512), BF16), jax.ShapeDtypeStruct((t_tok, 256), BF16),
                   jax.ShapeDtypeStruct((N_HEADS, BLOCK, 2 * BLOCK), F32), jax.ShapeDtypeStruct((N_HEADS, 128), F32)],
        scratch_shapes=[pltpu.VMEM((seq, 256), F32), pltpu.VMEM(tile, F32), pltpu.VMEM(tile, F32),
                        pltpu.VMEM(tile, BF16), pltpu.VMEM(tile, BF16)],
        compiler_params=_cp(("arbitrary", "arbitrary"), VMEM_MID))(sinks, qkvg, qkvg, qkvg, du_a, a_out, lse, bias)


def _out_proj(name, u_parts, gate_arr, gate_blk, w_out, x, gmod, seq):
    t_tok = x.shape[0]
    nu = len(u_parts)

    def body(*refs):
        u_refs = refs[:nu]
        g_ref, w_ref, x_ref, gm_ref, yg_ref, y_ref, xn_ref = refs[nu:]
        u = jnp.concatenate([r[...].astype(F32) for r in u_refs], axis=1) if nu > 1 else u_refs[0][...].astype(F32)
        yg = (u * _silu(g_ref[...].astype(F32))).astype(BF16)
        yg_ref[...] = yg
        y = _nn(yg, w_ref[...])
        y_ref[...] = y.astype(BF16)
        xn_ref[...] = x_ref[...] + gm_ref[...] * y

    row = lambda w: pl.BlockSpec((TM, w), lambda i: (i, 0))
    in_specs = [row(u.shape[1]) for u in u_parts]
    in_specs += [pl.BlockSpec((TM, D_MODEL), lambda i: (i, gate_blk)),
                 pl.BlockSpec((D_MODEL, D_MODEL), lambda i: (0, 0)), row(D_MODEL),
                 pl.BlockSpec((None, 1, D_MODEL), lambda i: (i * TM // seq, 0, 0))]
    return _pallas(
        body, name=name, grid=(t_tok // TM,), in_specs=in_specs,
        out_specs=[row(D_MODEL)] * 3,
        out_shape=[jax.ShapeDtypeStruct((t_tok, D_MODEL), BF16)] * 2 + [jax.ShapeDtypeStruct((t_tok, D_MODEL), F32)],
        compiler_params=_cp(("arbitrary",), VMEM_MID))(*u_parts, gate_arr, w_out, x, gmod)


def _out_proj_bwd(name, dxn, gmod, y, w_out, seq, attn=None):
    t_tok = dxn.shape[0]
    tiles_per_seq = seq // TM

    def body(*refs):
        if attn is None:
            dxn_ref, gm_ref, y_ref, w_ref, dy_ref, dgm_ref, dyg_ref = refs
        else:
            dxn_ref, gm_ref, y_ref, w_ref, a_ref, b_ref, g_ref, dy_ref, dgm_ref, dua_ref, dub_ref, dg_ref = refs
        i = pl.program_id(0)
        dxv = dxn_ref[...]
        dy = (dxv * gm_ref[...]).astype(BF16)
        dy_ref[...] = dy

        @pl.when(i % tiles_per_seq == 0)
        def _():
            dgm_ref[...] = jnp.zeros_like(dgm_ref)

        dgm_ref[...] += jnp.sum(dxv * y_ref[...].astype(F32), axis=0, keepdims=True)
        dyg = _nn(dy, w_ref[...])
        if attn is None:
            dyg_ref[...] = dyg
        else:
            gt = g_ref[...].astype(F32)
            du = dyg * _silu(gt)
            dua_ref[...] = du[:, :512].astype(BF16)
            dub_ref[...] = du[:, 512:].astype(BF16)
            u = jnp.concatenate([a_ref[...].astype(F32), b_ref[...].astype(F32)], axis=1)
            dg_ref[...] = (dyg * u * _dsilu(gt)).astype(BF16)

    row = lambda w: pl.BlockSpec((TM, w), lambda i: (i, 0))
    mod_spec = pl.BlockSpec((None, 1, D_MODEL), lambda i: (i * TM // seq, 0, 0))
    in_specs = [row(D_MODEL), mod_spec, row(D_MODEL), pl.BlockSpec((D_MODEL, D_MODEL), lambda i: (0, 0))]
    out_specs = [row(D_MODEL), mod_spec]
    out_shape = [jax.ShapeDtypeStruct((t_tok, D_MODEL), BF16), jax.ShapeDtypeStruct(gmod.shape, F32)]
    args = [dxn, gmod, y, w_out]
    if attn is None:
        out_specs.append(row(D_MODEL))
        out_shape.append(jax.ShapeDtypeStruct((t_tok, D_MODEL), F32))
    else:
        in_specs += [row(512), row(512), pl.BlockSpec((TM, D_MODEL), lambda i: (i, C_GATE // D_MODEL))]
        out_specs += [row(512), row(512), row(D_MODEL)]
        out_shape += [jax.ShapeDtypeStruct((t_tok, 512), BF16)] * 2 + [jax.ShapeDtypeStruct((t_tok, D_MODEL), BF16)]
        args += list(attn)
    return _pallas(body, name=name, grid=(t_tok // TM,), in_specs=in_specs, out_specs=out_specs,
                   out_shape=out_shape, compiler_params=_cp(("arbitrary",), VMEM_MID))(*args)


def _norm_bwd(name, parts, w, x, g, scale, dxn, seq, rows_part=None):
    t_tok = x.shape[0]
    npart = len(parts)
    tiles_per_seq = seq // TM
    nrow_in = 0 if rows_part is None else 2

    def body(*refs):
        p_refs = refs[:npart]
        w_ref, x_ref, g_ref, sc_ref, dxn_ref = refs[npart:npart + 5]
        dx_ref, dss_ref, dg_ref = refs[npart + 5 + nrow_in:]
        i = pl.program_id(0)
        dh = jnp.zeros((TM, D_MODEL), F32)
        if rows_part is not None:
            r_ref, wr_ref = refs[npart + 5:npart + 7]
            dh = dh + _tn(r_ref[...].astype(BF16), wr_ref[...])
        for (arr, off), p_ref in zip(parts, p_refs):
            dh = dh + _nn(p_ref[...], w_ref[off:off + arr.shape[1], :])
        xv = x_ref[...]
        rstd = lax.rsqrt(jnp.mean(xv * xv, axis=-1, keepdims=True) + EPS)
        xhat = xv * rstd
        gv = g_ref[...]
        nrm = xhat * gv

        @pl.when(i % tiles_per_seq == 0)
        def _():
            dss_ref[...] = jnp.zeros_like(dss_ref)

        @pl.when(i == 0)
        def _():
            dg_ref[...] = jnp.zeros_like(dg_ref)

        dss_ref[0:1, :] += jnp.sum(dh, axis=0, keepdims=True)
        dss_ref[1:2, :] += jnp.sum(dh * nrm, axis=0, keepdims=True)
        dn = dh * (1.0 + sc_ref[...])
        dg_ref[0:1, :] += jnp.sum(dn * xhat, axis=0, keepdims=True)
        dxhat = dn * gv
        dx_ref[...] = rstd * (dxhat - xhat * jnp.mean(dxhat * xhat, axis=-1, keepdims=True)) + dxn_ref[...]

    row = lambda wd: pl.BlockSpec((TM, wd), lambda i: (i, 0))
    w_spec = pl.BlockSpec(w.shape, lambda i: (0, 0))
    in_specs = [row(a.shape[1]) for a, _ in parts]
    in_specs += [w_spec, row(D_MODEL), pl.BlockSpec((1, D_MODEL), lambda i: (0, 0)),
                 pl.BlockSpec((None, 1, D_MODEL), lambda i: (i * TM // seq, 0, 0)), row(D_MODEL)]
    args = [a for a, _ in parts] + [w, x, g, scale, dxn]
    if rows_part is not None:
        in_specs += [pl.BlockSpec((8, TM), lambda i: (0, i)), pl.BlockSpec((8, D_MODEL), lambda i: (0, 0))]
        args += list(rows_part)
    nseq = t_tok // seq
    return _pallas(
        body, name=name, grid=(t_tok // TM,), in_specs=in_specs,
        out_specs=[row(D_MODEL), pl.BlockSpec((None, 8, D_MODEL), lambda i: (i * TM // seq, 0, 0)),
                   pl.BlockSpec((8, D_MODEL), lambda i: (0, 0))],
        out_shape=[jax.ShapeDtypeStruct((t_tok, D_MODEL), F32), jax.ShapeDtypeStruct((nseq, 8, D_MODEL), F32),
                   jax.ShapeDtypeStruct((8, D_MODEL), F32)],
        compiler_params=_cp(("arbitrary",), VMEM_BIG))(*args)


def _dw(name, a, parts, blocked=None):
    t_tok, ka = a.shape
    tt = min(1024, t_tok)
    npart = len(parts)
    nt = t_tok // tt

    def body(*refs):
        a_ref = refs[0]
        p_refs = refs[1:1 + npart]
        o_refs = refs[1 + npart:1 + 2 * npart]
        acc_refs = refs[1 + 2 * npart:]
        t = pl.program_id(0)
        at = a_ref[...].T
        for p_ref, acc in zip(p_refs, acc_refs):
            upd = _nn(at, p_ref[...])

            @pl.when(t == 0)
            def _():
                acc[...] = upd

            @pl.when(t > 0)
            def _():
                acc[...] += upd

        @pl.when(t == nt - 1)
        def _():
            for o_ref, acc in zip(o_refs, acc_refs):
                if blocked is None:
                    o_ref[...] = acc[...].astype(BF16)
                else:
                    for j in range(o_ref.shape[0]):
                        o_ref[j] = acc[:, j * blocked:(j + 1) * blocked].astype(BF16)

    in_specs = [pl.BlockSpec((tt, ka), lambda t: (t, 0))]
    in_specs += [pl.BlockSpec((tt, p.shape[1]), lambda t: (t, 0)) for p in parts]
    if blocked is None:
        out_shape = [jax.ShapeDtypeStruct((ka, p.shape[1]), BF16) for p in parts]
        out_specs = [pl.BlockSpec((ka, p.shape[1]), lambda t: (0, 0)) for p in parts]
    else:
        out_shape = [jax.ShapeDtypeStruct((p.shape[1] // blocked, ka, blocked), BF16) for p in parts]
        out_specs = [pl.BlockSpec((p.shape[1] // blocked, ka, blocked), lambda t: (0, 0, 0)) for p in parts]
    return _pallas(body, name=name, grid=(nt,), in_specs=in_specs, out_specs=out_specs, out_shape=out_shape,
                   scratch_shapes=[pltpu.VMEM((ka, p.shape[1]), F32) for p in parts],
                   compiler_params=_cp(("arbitrary",), VMEM_BIG))(a, *parts)


def _dw_rows(name, rows_t, h):
    t_tok = h.shape[0]
    tt = 512

    def body(r_ref, h_ref, o_ref):
        @pl.when(pl.program_id(0) == 0)
        def _():
            o_ref[...] = jnp.zeros_like(o_ref)

        o_ref[...] += _nn(r_ref[...].astype(BF16), h_ref[...])

    return _pallas(body, name=name, grid=(t_tok // tt,),
                   in_specs=[pl.BlockSpec((8, tt), lambda t: (0, t)), pl.BlockSpec((tt, D_MODEL), lambda t: (t, 0))],
                   out_specs=pl.BlockSpec((8, D_MODEL), lambda t: (0, 0)),
                   out_shape=jax.ShapeDtypeStruct((8, D_MODEL), F32),
                   compiler_params=_cp(("arbitrary",), VMEM_MID))(rows_t, h)


def _lru_gates(xc, blk, wa_ref, wx_ref, ba_ref, bx_ref, sp):
    cols = slice(blk * LRU_BLOCK_W, (blk + 1) * LRU_BLOCK_W)
    xb = xc[:, cols].astype(BF16)
    r = _sigmoid(_nn(xb, wa_ref[blk].astype(BF16)) + ba_ref[:, cols])
    ig = _sigmoid(_nn(xb, wx_ref[blk].astype(BF16)) + bx_ref[:, cols])
    log_a = -LRU_C * r * sp[:, cols]
    a = jnp.exp(log_a)
    x2 = 2.0 * log_a
    series = -x2 * (1.0 + x2 * (0.5 + x2 * (1.0 / 6.0)))
    z = jnp.where(x2 > -0.01, series, 1.0 - a * a)
    mult = z * lax.rsqrt(jnp.maximum(z, 1e-30))
    return xb, r, ig, a, mult


def _softplus_neg(lam):
    return jnp.maximum(-lam, 0.0) + jnp.log(1.0 + jnp.exp(-jnp.abs(lam)))


def _conv_taps(xe_ref, cw_ref, cb_ref):
    xc = cb_ref[...] + xe_ref[8:8 + TC, :] * cw_ref[3:4, :]
    for k in range(1, 4):
        xc = xc + xe_ref[8 - k:8 - k + TC, :] * cw_ref[3 - k:4 - k, :]
    return xc


def _lru_fwd(proj, cw, cb, w_a, b_a, w_x, b_x, lam, seq):
    t_tok = proj.shape[0]
    nc = seq // TC

    def body(x_ref, cw_ref, cb_ref, wa_ref, ba_ref, wx_ref, bx_ref, lam_ref, hs_ref, xe_s, a_s, u_s, h_s):
        c = pl.program_id(1)

        @pl.when(c == 0)
        def _():
            xe_s[0:8, :] = jnp.zeros((8, D_MODEL), F32)
            h_s[...] = jnp.zeros_like(h_s)

        xe_s[8:8 + TC, :] = x_ref[...]
        xc = _conv_taps(xe_s, cw_ref, cb_ref)
        sp = _softplus_neg(lam_ref[...])
        for blk in range(LRU_BLOCKS):
            cols = slice(blk * LRU_BLOCK_W, (blk + 1) * LRU_BLOCK_W)
            _, _, ig, a, mult = _lru_gates(xc, blk, wa_ref, wx_ref, ba_ref, bx_ref, sp)
            a_s[:, cols] = a
            u_s[:, cols] = mult * ig * xc[:, cols]

        def step(t8, h):
            base = pl.multiple_of(t8 * 8, 8)
            for q in range(8):
                h = a_s[pl.ds(base + q, 1), :] * h + u_s[pl.ds(base + q, 1), :]
                hs_ref[pl.ds(base + q, 1), :] = h
            return h

        h_s[0:1, :] = lax.fori_loop(0, TC // 8, step, h_s[0:1, :])
        xe_s[0:8, :] = xe_s[TC:TC + 8, :]

    full = lambda shape: pl.BlockSpec(shape, lambda b, c: (0,) * len(shape))
    return _pallas(
        body, name="lru_fwd", grid=(t_tok // seq, nc),
        in_specs=[pl.BlockSpec((TC, D_MODEL), lambda b, c: (b * nc + c, 0)), full((4, D_MODEL)), full((1, D_MODEL)),
                  full((LRU_BLOCKS, LRU_BLOCK_W, LRU_BLOCK_W)), full((1, D_MODEL)),
                  full((LRU_BLOCKS, LRU_BLOCK_W, LRU_BLOCK_W)), full((1, D_MODEL)), full((1, D_MODEL))],
        out_specs=pl.BlockSpec((TC, D_MODEL), lambda b, c: (b * nc + c, 0)),
        out_shape=jax.ShapeDtypeStruct((t_tok, D_MODEL), F32),
        scratch_shapes=[pltpu.VMEM((TC + 8, D_MODEL), F32), pltpu.VMEM((TC, D_MODEL), F32),
                        pltpu.VMEM((TC, D_MODEL), F32), pltpu.VMEM((8, D_MODEL), F32)],
        compiler_params=_cp(("arbitrary", "arbitrary"), VMEM_BIG))(proj, cw, cb, w_a, b_a, w_x, b_x, lam)


def _lru_bwd(proj, hs, dyh, cw, cb, w_a, b_a, w_x, b_x, lam, seq):
    t_tok = proj.shape[0]
    nc = seq // TC

    def body(x_ref, xh_ref, g_ref, hs_ref, hh_ref, dy_ref, cw_ref, cb_ref, wa_ref, ba_ref, wx_ref, bx_ref, lam_ref,
             dp_ref, dcw_ref, dvec_ref, dwa_ref, dwx_ref,
             xe_s, he_s, de_s, a_s, r_s, i_s, m_s, dhs_s, dh_s, carry_s):
        b, cr = pl.program_id(0), pl.program_id(1)
        c = nc - 1 - cr

        @pl.when(jnp.logical_and(b == 0, cr == 0))
        def _():
            dcw_ref[...] = jnp.zeros_like(dcw_ref)
            dvec_ref[...] = jnp.zeros_like(dvec_ref)
            dwa_ref[...] = jnp.zeros_like(dwa_ref)
            dwx_ref[...] = jnp.zeros_like(dwx_ref)

        @pl.when(cr == 0)
        def _():
            carry_s[...] = jnp.zeros_like(carry_s)
            de_s[TC:TC + 8, :] = jnp.zeros((8, D_MODEL), F32)

        first = c == 0
        xe_s[0:8, :] = jnp.where(first, 0.0, xh_ref[...])
        xe_s[8:8 + TC, :] = x_ref[...]
        he_s[0:8, :] = jnp.where(first, 0.0, hh_ref[...])
        he_s[8:8 + TC, :] = hs_ref[...]
        xc = _conv_taps(xe_s, cw_ref, cb_ref)
        lam_v = lam_ref[...]
        sp = _softplus_neg(lam_v)
        for blk in range(LRU_BLOCKS):
            cols = slice(blk * LRU_BLOCK_W, (blk + 1) * LRU_BLOCK_W)
            _, r, ig, a, mult = _lru_gates(xc, blk, wa_ref, wx_ref, ba_ref, bx_ref, sp)
            a_s[:, cols], r_s[:, cols], i_s[:, cols], m_s[:, cols] = a, r, ig, mult

        gt = g_ref[...]
        dyh = dy_ref[...]
        sg = _sigmoid(gt)
        dhs_s[...] = dyh * (gt * sg)
        dp_ref[:, D_MODEL:] = (dyh * hs_ref[...] * (sg * (1.0 + gt * (1.0 - sg)))).astype(BF16)

        def step(k8, carry):
            base = pl.multiple_of(TC - 8 - k8 * 8, 8)
            for q in reversed(range(8)):
                dh = dhs_s[pl.ds(base + q, 1), :] + carry
                dh_s[pl.ds(base + q, 1), :] = dh
                carry = a_s[pl.ds(base + q, 1), :] * dh
            return carry

        carry_s[0:1, :] = lax.fori_loop(0, TC // 8, step, carry_s[0:1, :])

        hprev = he_s[7:7 + TC, :]
        for blk in range(LRU_BLOCKS):
            cols = slice(blk * LRU_BLOCK_W, (blk + 1) * LRU_BLOCK_W)
            xcb = xc[:, cols]
            a, r, ig, mult, dh = a_s[:, cols], r_s[:, cols], i_s[:, cols], m_s[:, cols], dh_s[:, cols]
            spb = sp[:, cols]
            dmult = dh * ig * xcb
            di = dh * mult * xcb
            dxc = dh * mult * ig
            dla = dh * hprev[:, cols] * a - dmult * (a * a) * lax.rsqrt(jnp.maximum(mult * mult, 1e-30))
            dr = dla * (-LRU_C * spb)
            dsp = jnp.sum(dla * (-LRU_C * r), axis=0, keepdims=True)
            dga = dr * r * (1.0 - r)
            dgx = di * ig * (1.0 - ig)
            dga_b, dgx_b = dga.astype(BF16), dgx.astype(BF16)
            xb = xcb.astype(BF16)
            dxc = dxc + _nt(dga_b, wa_ref[blk].astype(BF16)) + _nt(dgx_b, wx_ref[blk].astype(BF16))
            dwa_ref[blk] += _tn(xb, dga_b)
            dwx_ref[blk] += _tn(xb, dgx_b)
            dvec_ref[1:2, cols] += jnp.sum(dga, axis=0, keepdims=True)
            dvec_ref[2:3, cols] += jnp.sum(dgx, axis=0, keepdims=True)
            dvec_ref[3:4, cols] += dsp * (-1.0 / (1.0 + jnp.exp(lam_v[:, cols])))
            de_s[0:TC, cols] = dxc

        dxc = de_s[0:TC, :]
        dvec_ref[0:1, :] += jnp.sum(dxc, axis=0, keepdims=True)
        dxr = dxc * cw_ref[3:4, :]
        dcw_ref[3:4, :] += jnp.sum(dxc * xe_s[8:8 + TC, :], axis=0, keepdims=True)
        for k in range(1, 4):
            dxr = dxr + de_s[k:k + TC, :] * cw_ref[3 - k:4 - k, :]
            dcw_ref[3 - k:4 - k, :] += jnp.sum(dxc * xe_s[8 - k:8 - k + TC, :], axis=0, keepdims=True)
        dp_ref[:, :D_MODEL] = dxr.astype(BF16)
        de_s[TC:TC + 8, :] = de_s[0:8, :]

    chunk = lambda col: pl.BlockSpec((TC, D_MODEL), lambda b, cr: (b * nc + nc - 1 - cr, col))
    halo = lambda col: pl.BlockSpec(
        (8, D_MODEL), lambda b, cr: (jnp.maximum((b * nc + nc - 1 - cr) * (TC // 8) - 1, 0), col))
    full = lambda shape: pl.BlockSpec(shape, lambda b, cr: (0,) * len(shape))
    wblk = (LRU_BLOCKS, LRU_BLOCK_W, LRU_BLOCK_W)
    return _pallas(
        body, name="lru_bwd", grid=(t_tok // seq, nc),
        in_specs=[chunk(0), halo(0), chunk(1), chunk(0), halo(0), chunk(0),
                  full((4, D_MODEL)), full((1, D_MODEL)), full(wblk), full((1, D_MODEL)), full(wblk),
                  full((1, D_MODEL)), full((1, D_MODEL))],
        out_specs=[pl.BlockSpec((TC, 2 * D_MODEL), lambda b, cr: (b * nc + nc - 1 - cr, 0)),
                   full((8, D_MODEL)), full((8, D_MODEL)), full(wblk), full(wblk)],
        out_shape=[jax.ShapeDtypeStruct((t_tok, 2 * D_MODEL), BF16), jax.ShapeDtypeStruct((8, D_MODEL), F32),
                   jax.ShapeDtypeStruct((8, D_MODEL), F32), jax.ShapeDtypeStruct(wblk, F32),
                   jax.ShapeDtypeStruct(wblk, F32)],
        scratch_shapes=[pltpu.VMEM((TC + 8, D_MODEL), F32), pltpu.VMEM((TC + 8, D_MODEL), F32),
                        pltpu.VMEM((TC + 8, D_MODEL), F32)]
        + [pltpu.VMEM((TC, D_MODEL), F32)] * 6 + [pltpu.VMEM((8, D_MODEL), F32)],
        compiler_params=_cp(("arbitrary", "arbitrary"), VMEM_BIG),
    )(proj, proj, proj, hs, hs, dyh, cw, cb, w_a, b_a, w_x, b_x, lam)


def _last_layer_tail(hs, proj, w_out, w_out_t, x, gmod, final_g, target, seq):
    t_tok = x.shape[0]
    tiles_per_seq = seq // TM

    def body(hs_ref, g_ref, w_ref, wt_ref, x_ref, gm_ref, fg_ref, t_ref,
             yg_ref, dx_ref, dy_ref, dyg_ref, dgm_ref, loss_ref, dfg_ref):
        i = pl.program_id(0)

        @pl.when(i == 0)
        def _():
            loss_ref[...] = jnp.zeros_like(loss_ref)
            dfg_ref[...] = jnp.zeros_like(dfg_ref)

        @pl.when(i % tiles_per_seq == 0)
        def _():
            dgm_ref[...] = jnp.zeros_like(dgm_ref)

        gm = gm_ref[...]
        yg = (hs_ref[...] * _silu(g_ref[...])).astype(BF16)
        yg_ref[...] = yg
        y = _nn(yg, w_ref[...])
        xv = x_ref[...] + gm * y
        gv = fg_ref[...]
        rstd = lax.rsqrt(jnp.mean(xv * xv, axis=-1, keepdims=True) + EPS)
        xhat = xv * rstd
        err = xhat * gv - t_ref[...]
        loss_ref[0:1, :] += jnp.sum(err * err, axis=0, keepdims=True) * (0.5 / D_MODEL)
        dout = err * (1.0 / D_MODEL)
        dfg_ref[0:1, :] += jnp.sum(dout * xhat, axis=0, keepdims=True)
        dxhat = dout * gv
        dxv = rstd * (dxhat - xhat * jnp.mean(dxhat * xhat, axis=-1, keepdims=True))
        dx_ref[...] = dxv
        dgm_ref[...] += jnp.sum(dxv * y, axis=0, keepdims=True)
        dy = (dxv * gm).astype(BF16)
        dy_ref[...] = dy
        dyg_ref[...] = _nn(dy, wt_ref[...])

    row = pl.BlockSpec((TM, D_MODEL), lambda i: (i, 0))
    acc = pl.BlockSpec((8, D_MODEL), lambda i: (0, 0))
    mod_spec = pl.BlockSpec((None, 1, D_MODEL), lambda i: (i * TM // seq, 0, 0))
    return _pallas(
        body, name="last_layer_tail", grid=(t_tok // TM,),
        in_specs=[row, pl.BlockSpec((TM, D_MODEL), lambda i: (i, 1)), pl.BlockSpec((D_MODEL, D_MODEL), lambda i: (0, 0)),
                  pl.BlockSpec((D_MODEL, D_MODEL), lambda i: (0, 0)),
                  row, mod_spec, pl.BlockSpec((1, D_MODEL), lambda i: (0, 0)), row],
        out_specs=[row, row, row, row, mod_spec, acc, acc],
        out_shape=[jax.ShapeDtypeStruct((t_tok, D_MODEL), BF16), jax.ShapeDtypeStruct((t_tok, D_MODEL), F32),
                   jax.ShapeDtypeStruct((t_tok, D_MODEL), BF16), jax.ShapeDtypeStruct((t_tok, D_MODEL), F32),
                   jax.ShapeDtypeStruct(gmod.shape, F32), jax.ShapeDtypeStruct((8, D_MODEL), F32),
                   jax.ShapeDtypeStruct((8, D_MODEL), F32)],
        compiler_params=_cp(("arbitrary",), VMEM_BIG))(hs, proj, w_out, w_out_t, x, gmod, final_g, target)


def _adam_math(w, g, m, v):
    m_new = ADAM_B1 * m + (1.0 - ADAM_B1) * g
    v_new = ADAM_B2 * v + (1.0 - ADAM_B2) * (g * g)
    m_hat = m_new / (1.0 - ADAM_B1 ** ADAM_STEP)
    v_hat = v_new / (1.0 - ADAM_B2 ** ADAM_STEP)
    delta = -ADAM_LR * (m_hat / (jnp.sqrt(v_hat) + ADAM_EPS) + ADAM_WD * w)
    return delta, m_new, v_new


def _sum_leading(name, x, out_dtype=F32):
    n, rows, cols = x.shape
    tr = PACK_ROWS if rows % PACK_ROWS == 0 else rows

    def body(x_ref, o_ref):
        acc = x_ref[0].astype(F32)
        for d in range(1, n):
            acc = acc + x_ref[d].astype(F32)
        o_ref[...] = acc.astype(out_dtype)

    return _pallas(body, name=name, grid=(rows // tr,),
                   in_specs=[pl.BlockSpec((n, tr, cols), lambda i: (0, i, 0))],
                   out_specs=pl.BlockSpec((tr, cols), lambda i: (i, 0)),
                   out_shape=jax.ShapeDtypeStruct((rows, cols), out_dtype),
                   compiler_params=_cp(("arbitrary",), VMEM_MID))(x)


def _adamw(name, w, m, v, g=None, parts=None):
    rows, cols = w.shape
    tr = rows if rows <= 256 else 256

    def body(*refs):
        w_ref, m_ref, v_ref, g_in, g_ref, d_ref, mo_ref, vo_ref = refs
        if parts is None:
            gv = g_in[...]
        else:
            acc = g_in[0].astype(F32)
            for d in range(1, parts.shape[0]):
                acc = acc + g_in[d].astype(F32)
            gv = acc[:, :cols]
        delta, m_new, v_new = _adam_math(w_ref[...], gv, m_ref[...], v_ref[...])
        g_ref[...] = gv
        d_ref[...] = delta
        mo_ref[...] = m_new
        vo_ref[...] = v_new

    row = pl.BlockSpec((tr, cols), lambda i: (i, 0))
    if parts is None:
        g_spec, g_arg = row, g
    else:
        g_spec, g_arg = pl.BlockSpec((parts.shape[0], tr, parts.shape[2]), lambda i: (0, i, 0)), parts
    return _pallas(body, name=name, grid=(rows // tr,), in_specs=[row, row, row, g_spec], out_specs=[row] * 4,
                   out_shape=[jax.ShapeDtypeStruct((rows, cols), F32)] * 4,
                   compiler_params=_cp(("arbitrary",), VMEM_MID))(w, m, v, g_arg)


def _adamw_many(name, groups):
    ntens = len(groups)

    def body(*refs):
        ins, outs = refs[:4 * ntens], refs[4 * ntens:]
        for k in range(ntens):
            w_ref, m_ref, v_ref, g_ref = ins[4 * k:4 * k + 4]
            gv = g_ref[...]
            delta, m_new, v_new = _adam_math(w_ref[...], gv, m_ref[...], v_ref[...])
            for o_ref, val in zip(outs[4 * k:4 * k + 4], (gv, delta, m_new, v_new)):
                o_ref[...] = val

    flat = [a for grp in groups for a in grp]
    out_shape = [jax.ShapeDtypeStruct(grp[0].shape, F32) for grp in groups for _ in range(4)]
    outs = _pallas(body, name=name, out_shape=out_shape, compiler_params=_cp(vmem=VMEM_MID))(*flat)
    return [tuple(outs[4 * k:4 * k + 4]) for k in range(ntens)]


def _pack_rows(arrs):
    rows, meta, total = [], [], 0
    for a in arrs:
        flat = a.reshape(-1)
        nrow = -(-flat.shape[0] // 1024) * 8
        rows.append(jnp.pad(flat, (0, nrow * 128 - flat.shape[0])).reshape(nrow, 128))
        meta.append((a.shape, flat.shape[0], nrow))
        total += nrow
    tail = -total % PACK_ROWS
    if tail:
        rows.append(jnp.zeros((tail, 128), F32))
    return jnp.concatenate(rows, axis=0), meta


def _unpack_rows(packed, meta):
    out, r0 = [], 0
    for shape, size, nrow in meta:
        out.append(packed[r0:r0 + nrow].reshape(-1)[:size].reshape(shape))
        r0 += nrow
    return out


WEIGHTS = ["rel_bias", "norm_g", "ada_w", "ada_b", "attn_w_in", "attn_sinks", "attn_b_f", "attn_w_out", "lru_w_in",
           "lru_conv_w", "lru_conv_b", "lru_w_a", "lru_b_a", "lru_w_x", "lru_b_x", "lru_lambda", "lru_w_out", "final_g"]
BIG = ["ada_w", "attn_w_in", "attn_w_out", "lru_w_in", "lru_w_out"]
PACK_ROWS = 256


def kernel(x, c, rel_bias, norm_g, ada_w, ada_b, attn_w_in, attn_sinks, attn_b_f, attn_w_out, lru_w_in, lru_conv_w, lru_conv_b, lru_w_a, lru_b_a, lru_w_x, lru_b_x, lru_lambda, lru_w_out, final_g, loss_target, m_rel_bias, m_norm_g, m_ada_w, m_ada_b, m_attn_w_in, m_attn_sinks, m_attn_b_f, m_attn_w_out, m_lru_w_in, m_lru_conv_w, m_lru_conv_b, m_lru_w_a, m_lru_b_a, m_lru_w_x, m_lru_b_x, m_lru_lambda, m_lru_w_out, m_final_g, v_rel_bias, v_norm_g, v_ada_w, v_ada_b, v_attn_w_in, v_attn_sinks, v_attn_b_f, v_attn_w_out, v_lru_w_in, v_lru_conv_w, v_lru_conv_b, v_lru_w_a, v_lru_b_a, v_lru_w_x, v_lru_b_x, v_lru_lambda, v_lru_w_out, v_final_g):
    nseq, seq, _ = x.shape
    t_tok = nseq * seq
    me = 4 * lax.axis_index("x") + 2 * lax.axis_index("y") + lax.axis_index("c")
    x0 = x.reshape(t_tok, D_MODEL)
    target = loss_target.reshape(t_tok, D_MODEL)

    w_in_pad = jnp.pad(attn_w_in[0].astype(BF16), ((0, 0), (0, SHARD_W_PAD - SHARD_W_IN)))
    vec_shard = jnp.concatenate([lru_conv_w[0], lru_conv_b, lru_b_a, lru_b_x, lru_lambda], axis=0)
    g_w_in, g_vec, g_c = _exchange("gather_first", [w_in_pad, vec_shard, c], [])
    later_w = [attn_w_out[0].astype(BF16), lru_w_in[0].astype(BF16), lru_w_out[0].astype(BF16)]
    later_handle, later_token = _exchange_start("gather_later_start", later_w, [], after=g_vec)
    w_full = jnp.transpose(g_w_in[:, :, :SHARD_W_IN], (1, 0, 2)).reshape(D_MODEL, N_DEV * SHARD_W_IN)
    w_aq, w_ak, w_av = w_full[:, 0:512], w_full[:, 512:640], w_full[:, 640:768]
    w_bq, w_bk, w_bv = w_full[:, 768:1280], w_full[:, 1280:1792], w_full[:, 1792:2304]
    w_f, w_gate = w_full[:, 2304:2312], w_full[:, 2312:3336]
    w_main = jnp.concatenate([w_bq, w_bk, w_bv, w_aq, w_gate, w_ak, w_av], axis=1)
    wf_t = jnp.transpose(w_f)
    vec_full = jnp.transpose(g_vec, (1, 0, 2)).reshape(8, D_MODEL)
    conv_w, conv_b, b_a, b_x, lam = vec_full[0:4], vec_full[4:5], vec_full[5:6], vec_full[6:7], vec_full[7:8]
    c_all = g_c.reshape(N_DEV * nseq, D_MODEL)

    ncol = ada_w.shape[2]
    ada_b_slice = lax.dynamic_slice(ada_b.reshape(2, N_DEV, ncol), (0, me, 0), (2, 1, ncol))
    mod_part = _ada_mod(c_all, ada_w, ada_b_slice)
    (g_mod,) = _exchange("gather_mod", [mod_part], [])
    mine = lax.dynamic_slice(g_mod, (0, 0, me * nseq, 0), (N_DEV, 2, nseq, ncol))
    mod = jnp.transpose(mine, (1, 2, 0, 3)).reshape(2, nseq, 3 * D_MODEL)
    shift = [mod[l, :, 0:D_MODEL].reshape(nseq, 1, D_MODEL) for l in range(2)]
    scale = [mod[l, :, D_MODEL:2 * D_MODEL].reshape(nseq, 1, D_MODEL) for l in range(2)]
    gmod = [mod[l, :, 2 * D_MODEL:].reshape(nseq, 1, D_MODEL) for l in range(2)]

    onehot = _bucket_onehot()
    bias = _bias_expand(jnp.transpose(rel_bias), onehot).reshape(N_HEADS, BLOCK, 2 * BLOCK)
    sinks = attn_sinks.reshape(N_HEADS)
    b_f = attn_b_f.reshape(N_HEADS, 1)
    norm_g0 = norm_g[0:1] + later_token[0:1, 0:1]
    h0, qkvg, fl_t = _norm_proj("norm_proj0", x0, norm_g0, shift[0], scale[0], w_main, seq, BF16, wf_t=wf_t)
    f_row, f_col = _fox_prep(fl_t, b_f, seq)
    a_out, lse_a = _swa_fwd(qkvg, bias, sinks, seq)
    q_aug, k_aug, kt_aug, vt = _fox_aug(qkvg, f_col, seq)
    b_out, lse_b = _fox_fwd_t(q_aug, k_aug, vt, seq)
    g_later = _exchange_wait("gather_later_wait", later_handle, after=lse_b)
    w_out0, g_lru_in, w_out1 = (_with_own(g, w, me) for g, w in zip(g_later, later_w))
    w_out0, w_out1 = w_out0.reshape(D_MODEL, D_MODEL), w_out1.reshape(D_MODEL, D_MODEL)
    w_out0_t, w_out1_t, w_main_t = jnp.transpose(w_out0), jnp.transpose(w_out1), jnp.transpose(w_main)
    lru_in_t = jnp.transpose(g_lru_in, (0, 2, 1)).reshape(2 * D_MODEL, D_MODEL)
    yg0, y0, x1 = _out_proj("out_proj0", [a_out, b_out], qkvg, C_GATE // D_MODEL, w_out0, x0, gmod[0], seq)

    h1, proj1 = _norm_proj("norm_proj1", x1, norm_g[1:2], shift[1], scale[1], g_lru_in, seq, F32)
    hs = _lru_fwd(proj1, conv_w, conv_b, lru_w_a[0], b_a, lru_w_x[0], b_x, lam, seq)

    yg1, dx2, dy1, dyh, dgm1, loss_rows, dfinal_rows = _last_layer_tail(
        hs, proj1, w_out1, w_out1_t, x1, gmod[1], final_g.reshape(1, D_MODEL), target, seq)

    dproj1, dcw, dvec, dw_a, dw_x = _lru_bwd(proj1, hs, dyh, conv_w, conv_b, lru_w_a[0], b_a, lru_w_x[0], b_x, lam, seq)
    dx1, dss1, dg1 = _norm_bwd("norm1_bwd", [(dproj1, 0)], lru_in_t, x1, norm_g[1:2], scale[1], dx2, seq)
    (p_w_out1,) = _dw("dw_out1", yg1, [dy1])
    (p_lru_in,) = _dw("dw_lru_in", h1, [dproj1], blocked=2 * D_MODEL // N_DEV)

    rows_out = D_MODEL // N_DEV
    gpack1, gmeta1 = _pack_rows([dcw[0:4], dvec[0:4], dg1[0], dfinal_rows[0]])
    dwax = jnp.stack([dw_a, dw_x]).astype(BF16)
    own1 = [gpack1, dwax, p_lru_in, p_w_out1.reshape(N_DEV, rows_out, D_MODEL)]
    grads1_handle, grads1_token = _exchange_start("grads1_start", own1[:2], own1[2:], after=dx1)

    gmod0 = gmod[0] + grads1_token[0:1, 0:1]
    dy0, dgm0, du_a, du_b, dgate = _out_proj_bwd("out_proj0_bwd", dx1, gmod0, y0, w_out0_t, seq,
                                                  attn=(a_out, b_out, qkvg))
    dq_a, dkv_a, dbias, dsink = _swa_bwd(qkvg, du_a, a_out, lse_a, bias, sinks, seq)
    dq_b, dk_b, dv_b, df4 = _fox_bwd_t(q_aug, k_aug, kt_aug, qkvg, du_b, b_out, lse_b, seq)
    dfl_t, db_f = _fox_post(df4.reshape(N_HEADS, t_tok), fl_t, b_f, seq)
    parts0 = [(dq_b, C_BQ), (dk_b, C_BK), (dv_b, C_BV), (dq_a, C_AQ), (dgate, C_GATE), (dkv_a, C_AK)]
    (p_w_out0,) = _dw("dw_out0", yg0, [dy0])
    pw_bq, pw_bk, pw_bv, pw_aq, pw_gate, pw_akv = _dw("dw_attn_in", h0, [p for p, _ in parts0])
    pw_f = _dw_rows("dw_f", dfl_t, h0)

    p_w_in = jnp.concatenate([pw_aq, pw_akv, pw_bq, pw_bk, pw_bv, jnp.transpose(pw_f).astype(BF16), pw_gate], axis=1)
    p_w_in = jnp.transpose(p_w_in.reshape(D_MODEL, N_DEV, SHARD_W_IN), (1, 0, 2))
    p_w_in = jnp.pad(p_w_in, ((0, 0), (0, 0), (0, SHARD_W_PAD - SHARD_W_IN)))
    own0 = [p_w_in, p_w_out0.reshape(N_DEV, rows_out, D_MODEL)]
    landed1 = _exchange_wait("grads1_wait", grads1_handle, after=p_w_in)
    grads0_handle, grads0_token = _exchange_start("grads0_start", [], own0, after=landed1[0])
    scale0 = scale[0] + grads0_token[0:1, 0:1]
    dx0, dss0, dg0 = _norm_bwd("norm0_bwd", parts0, w_main_t, x0, norm_g[0:1], scale0, dx1, seq,
                               rows_part=(dfl_t, wf_t))
    dbias_t = _bias_reduce(dbias.reshape(N_HEADS, BLOCK * 2 * BLOCK), onehot)

    gpack0, gmeta0 = _pack_rows([jnp.transpose(dbias_t), dg0[0], dsink[:, 0], db_f[:, 0], loss_rows[0]])
    dmod = jnp.stack([jnp.concatenate([dss[:, 0], dss[:, 1], dgm[:, 0]], axis=1)
                      for dss, dgm in ((dss0, dgm0), (dss1, dgm1))], axis=1)
    g_small0, g_dmod = _exchange("exchange_small", [gpack0, dmod], [])
    landed0 = _exchange_wait("grads0_wait", grads0_handle, after=g_small0)
    r_w_in, r_w_out0 = (_with_own(g, lax.dynamic_index_in_dim(a, me, 0, keepdims=False), me)
                        for g, a in zip(landed0, own0))
    g_small1, g_dwax = (_with_own(g, a, me) for g, a in zip(landed1[:2], own1[:2]))
    r_lru_in, r_w_out1 = (_with_own(g, lax.dynamic_index_in_dim(a, me, 0, keepdims=False), me)
                          for g, a in zip(landed1[2:], own1[2:]))

    d_rel, d_g0, d_sinks, d_b_f, loss_cols = _unpack_rows(_sum_leading("sum_small0", g_small0), gmeta0)
    loss = jnp.sum(loss_cols)
    d_cw, d_vec, d_g1, d_final_g = _unpack_rows(_sum_leading("sum_small1", g_small1), gmeta1)
    d_norm_g = jnp.stack([d_g0, d_g1])
    d_wax = _sum_leading("sum_dwax", g_dwax.reshape(N_DEV, 2 * LRU_BLOCKS * LRU_BLOCK_W, LRU_BLOCK_W))
    d_wa, d_wx = d_wax[:LRU_BLOCKS * LRU_BLOCK_W], d_wax[LRU_BLOCKS * LRU_BLOCK_W:]
    cols = lambda a: lax.dynamic_slice(a, (0, me * LRU_BLOCK_W), (a.shape[0], LRU_BLOCK_W))
    dmod_all = g_dmod.reshape(N_DEV * nseq, 2 * 3 * D_MODEL)
    d_ada_b = _sum_leading("sum_ada_b", dmod_all.reshape(N_DEV * nseq, 2 * 3 * D_MODEL // 128, 128)).reshape(2, 3 * D_MODEL)
    dmod_slice = lax.dynamic_slice(dmod_all.reshape(N_DEV * nseq, 2, N_DEV, ncol), (0, 0, me, 0),
                                   (N_DEV * nseq, 2, 1, ncol)).reshape(N_DEV * nseq, 2, ncol)
    d_ada_w = _ada_w_grad(c_all, jnp.transpose(dmod_slice, (1, 0, 2)))

    given = dict(
        rel_bias=(rel_bias, m_rel_bias, v_rel_bias), norm_g=(norm_g, m_norm_g, v_norm_g),
        ada_w=(ada_w, m_ada_w, v_ada_w), ada_b=(ada_b, m_ada_b, v_ada_b),
        attn_w_in=(attn_w_in, m_attn_w_in, v_attn_w_in), attn_sinks=(attn_sinks, m_attn_sinks, v_attn_sinks),
        attn_b_f=(attn_b_f, m_attn_b_f, v_attn_b_f), attn_w_out=(attn_w_out, m_attn_w_out, v_attn_w_out),
        lru_w_in=(lru_w_in, m_lru_w_in, v_lru_w_in), lru_conv_w=(lru_conv_w, m_lru_conv_w, v_lru_conv_w),
        lru_conv_b=(lru_conv_b, m_lru_conv_b, v_lru_conv_b), lru_w_a=(lru_w_a, m_lru_w_a, v_lru_w_a),
        lru_b_a=(lru_b_a, m_lru_b_a, v_lru_b_a), lru_w_x=(lru_w_x, m_lru_w_x, v_lru_w_x),
        lru_b_x=(lru_b_x, m_lru_b_x, v_lru_b_x), lru_lambda=(lru_lambda, m_lru_lambda, v_lru_lambda),
        lru_w_out=(lru_w_out, m_lru_w_out, v_lru_w_out), final_g=(final_g, m_final_g, v_final_g))
    results = {}

    def big(name, shape2d, g=None, parts=None):
        w, m, v = (a.reshape(shape2d) for a in given[name])
        outs = _adamw("adamw_" + name, w, m, v, g=g, parts=parts)
        results[name] = tuple(o.reshape(given[name][0].shape) for o in outs)

    big("ada_w", (2 * D_MODEL, ncol), g=d_ada_w.reshape(2 * D_MODEL, ncol))
    big("attn_w_in", (D_MODEL, SHARD_W_IN), parts=r_w_in)
    big("attn_w_out", (rows_out, D_MODEL), parts=r_w_out0)
    big("lru_w_in", (D_MODEL, 2 * D_MODEL // N_DEV), parts=r_lru_in)
    big("lru_w_out", (rows_out, D_MODEL), parts=r_w_out1)

    small_grads = dict(
        rel_bias=d_rel, norm_g=d_norm_g, ada_b=d_ada_b, attn_sinks=d_sinks.reshape(1, N_HEADS),
        attn_b_f=d_b_f.reshape(1, N_HEADS), lru_conv_w=cols(d_cw).reshape(1, 4, LRU_BLOCK_W),
        lru_conv_b=cols(d_vec[0:1]), lru_w_a=d_wa.reshape(lru_w_a.shape), lru_b_a=cols(d_vec[1:2]),
        lru_w_x=d_wx.reshape(lru_w_x.shape), lru_b_x=cols(d_vec[2:3]), lru_lambda=cols(d_vec[3:4]),
        final_g=d_final_g)
    small = [n for n in WEIGHTS if n not in BIG]
    as2d = lambda a: a.reshape(-1, a.shape[-1])
    outs = _adamw_many("adamw_small", [tuple(as2d(a) for a in given[n]) + (as2d(small_grads[n]),) for n in small])
    for n, group in zip(small, outs):
        results[n] = tuple(o.reshape(given[n][0].shape) for o in group)

    grad_x = dx0.reshape(x.shape)
    out = [loss, grad_x]
    for j in range(4):
        out += [results[n][j] for n in WEIGHTS]
    return tuple(out)
```

```python
import functools
import math

import jax
import jax.numpy as jnp
from jax import lax
from jax.experimental import pallas as pl
from jax.experimental.pallas import tpu as pltpu

F32 = jnp.float32
BF16 = jnp.bfloat16
HI = lax.Precision.HIGHEST
MESH = pl.DeviceIdType.MESH

N_DEV = 8
D_MODEL = 1024
HEAD_DIM = 64
N_HEADS = 8
KV_GROUP = 4
BLOCK = 128
REL_BUCKETS = 32
REL_MAX_EXACT = 16
REL_MAX_DIST = 128
LRU_BLOCKS = 8
LRU_BLOCK_W = 128
LRU_C = 8.0
EPS = 1e-6
SCALE = HEAD_DIM ** -0.5
NEG = -1e30

ADAM_LR = 0.001
ADAM_B1 = 0.9
ADAM_B2 = 0.999
ADAM_EPS = 1e-08
ADAM_WD = 0.01
ADAM_STEP = 10

C_BQ, C_BK, C_BV, C_AQ, C_GATE, C_AK, C_AV = 0, 512, 1024, 1536, 2048, 3072, 3200
N_MAIN = 3328
SHARD_W_IN = 417
SHARD_W_PAD = 512

TM = 512
TQ = 256
TK = 128
TKB = 256
TC = 512
SWA_SUB = 2
VMEM_BIG = 56 * 1024 * 1024
VMEM_MID = 40 * 1024 * 1024


def _pallas(body, **kw):
    return pl.pallas_call(body, **kw)


def _cp(sem=None, vmem=None):
    kw = {}
    if sem is not None:
        kw["dimension_semantics"] = sem
    if vmem is not None:
        kw["vmem_limit_bytes"] = vmem
    return pltpu.CompilerParams(**kw)


def _nn(a, b, precision=None):
    return jnp.dot(a, b, preferred_element_type=F32, precision=precision)


def _nt(a, b, precision=None):
    return lax.dot_general(a, b, (((1,), (1,)), ((), ())), preferred_element_type=F32, precision=precision)


def _tn(a, b, precision=None):
    return lax.dot_general(a, b, (((0,), (0,)), ((), ())), preferred_element_type=F32, precision=precision)


def _sigmoid(x):
    return 1.0 / (1.0 + jnp.exp(-x))


def _silu(x):
    return x * _sigmoid(x)


def _dsilu(x):
    s = _sigmoid(x)
    return s * (1.0 + x * (1.0 - s))


def _col(tile, idx):
    lane = lax.broadcasted_iota(jnp.int32, tile.shape, 1)
    return jnp.sum(jnp.where(lane == idx, tile, 0.0), axis=1, keepdims=True)


def _exchange(name, gathers, scatters, axes=("x", "y", "c"), chunks=1):
    ng, n = len(gathers), len(gathers) + len(scatters)
    ins = list(gathers) + list(scatters)
    group = 2 ** len(axes)

    def body(*refs):
        in_refs, out_refs = refs[:n], refs[n:2 * n]
        send_sems, recv_sems, loc_sems = refs[2 * n:]
        coord = {a: lax.axis_index(a) for a in ("x", "y", "c")}

        def member(r):
            pc = dict(coord)
            idx = 0
            for k, a in enumerate(axes):
                if r & (1 << (len(axes) - 1 - k)):
                    pc[a] = 1 - coord[a]
                idx = 2 * idx + pc[a]
            return (pc["x"], pc["y"], pc["c"]), idx

        _, me = member(0)

        def peer(r):
            return member(r)

        local, sends, recvs = [], [], []
        for k in range(n):
            mine = in_refs[k] if k < ng else in_refs[k].at[me]
            cp = pltpu.make_async_copy(mine, out_refs[k].at[me], loc_sems.at[k])
            cp.start()
            local.append(cp)
            lead = mine.shape[0]
            nchunk = max(q for q in range(1, chunks + 1) if lead % q == 0)
            step = lead // nchunk
            for r in range(1, group):
                pid, pidx = peer(r)
                src = in_refs[k] if k < ng else in_refs[k].at[pidx]
                for q in range(nchunk):
                    rows = pl.ds(q * step, step)
                    sems = dict(send_sem=send_sems.at[r - 1, k, q], recv_sem=recv_sems.at[r - 1, k, q],
                                device_id=pid, device_id_type=MESH)
                    snd = pltpu.make_async_remote_copy(src_ref=src.at[rows], dst_ref=out_refs[k].at[me].at[rows], **sems)
                    snd.start()
                    sends.append(snd)
                    recvs.append(pltpu.make_async_remote_copy(
                        src_ref=src.at[rows], dst_ref=out_refs[k].at[pidx].at[rows], **sems))
        for rc in recvs:
            rc.wait_recv()
        for snd in sends:
            snd.wait_send()
        for cp in local:
            cp.wait()

    out_shape = [jax.ShapeDtypeStruct((group,) + a.shape, a.dtype) for a in gathers]
    out_shape += [jax.ShapeDtypeStruct(a.shape, a.dtype) for a in scatters]
    any_spec = pl.BlockSpec(memory_space=pl.ANY)
    return _pallas(
        body, name=name, out_shape=out_shape,
        in_specs=[any_spec] * n, out_specs=[any_spec] * n,
        scratch_shapes=[pltpu.SemaphoreType.DMA((group - 1, n, chunks)), pltpu.SemaphoreType.DMA((group - 1, n, chunks)),
                        pltpu.SemaphoreType.DMA((n,))],
    )(*ins)


def _peer_of(r):
    x, y, c = lax.axis_index("x"), lax.axis_index("y"), lax.axis_index("c")
    px = 1 - x if r & 4 else x
    py = 1 - y if r & 2 else y
    pc = 1 - c if r & 1 else c
    return (px, py, pc), 4 * px + 2 * py + pc


def _split_copies(in_refs, land_refs, send_sems, recv_sems, ng, with_recv):
    _, me = _peer_of(0)
    pairs = []
    for k, (src_ref, land) in enumerate(zip(in_refs, land_refs)):
        for r in range(1, N_DEV):
            pid, pidx = _peer_of(r)
            src = src_ref if k < ng else src_ref.at[pidx]
            slot = (N_DEV - 1) * k + r - 1
            sems = dict(send_sem=send_sems.at[slot], recv_sem=recv_sems.at[slot], device_id=pid, device_id_type=MESH)
            send = pltpu.make_async_remote_copy(src_ref=src, dst_ref=land.at[me], **sems)
            recv = pltpu.make_async_remote_copy(src_ref=src, dst_ref=land.at[pidx], **sems) if with_recv else None
            pairs.append((send, recv))
    return pairs


def _exchange_start(name, gathers, scatters, after):
    ng, n = len(gathers), len(gathers) + len(scatters)
    ins = list(gathers) + list(scatters)
    lands = [jax.ShapeDtypeStruct((N_DEV,) + a.shape, a.dtype) for a in gathers]
    lands += [jax.ShapeDtypeStruct(a.shape, a.dtype) for a in scatters]

    def body(*refs):
        in_refs, land_refs = refs[:n], refs[n:2 * n]
        send_sems, recv_sems = refs[2 * n + 1:2 * n + 3]
        token = refs[-1]
        for send, _ in _split_copies(in_refs, land_refs, send_sems, recv_sems, ng, False):
            send.start()
        token[...] = jnp.zeros_like(token)

    hbm = pl.BlockSpec(memory_space=pltpu.HBM)
    sem = pl.BlockSpec(memory_space=pltpu.SEMAPHORE)
    sem_shape = pltpu.SemaphoreType.DMA(((N_DEV - 1) * n,))
    out_shape = [sem_shape, sem_shape] + [pltpu.HBM(a.shape, a.dtype) for a in ins]
    out_shape += [pltpu.HBM(l.shape, l.dtype) for l in lands] + [jax.ShapeDtypeStruct((8, 128), F32)]
    args = [pltpu.with_memory_space_constraint(a, pltpu.HBM) for a in ins]
    args += [pltpu.with_memory_space_constraint(lax.empty(l.shape, l.dtype), pltpu.HBM) for l in lands]
    outs = _pallas(
        body, name=name, out_shape=out_shape,
        in_specs=[hbm] * (2 * n) + [pl.BlockSpec(memory_space=pl.ANY)],
        out_specs=[sem, sem] + [hbm] * (2 * n) + [pl.BlockSpec(memory_space=pltpu.VMEM)],
        input_output_aliases={i: 2 + i for i in range(2 * n)},
        compiler_params=pltpu.CompilerParams(has_side_effects=pltpu.SideEffectType.DATAFLOW_SIDE_EFFECTING),
    )(*args, after)
    return (outs[0], outs[1], list(outs[2:2 + n]), list(outs[2 + n:2 + 2 * n]), ng), outs[-1]


def _exchange_wait(name, handle, after):
    send_sems, recv_sems, srcs, lands, ng = handle
    n = len(srcs)

    def body(*refs):
        in_refs, land_refs = refs[:n], refs[n:2 * n]
        send_ref, recv_ref = refs[2 * n:2 * n + 2]
        for send, recv in _split_copies(in_refs, land_refs, send_ref, recv_ref, ng, True):
            send.wait_send()
            recv.wait_recv()

    hbm = pl.BlockSpec(memory_space=pltpu.HBM)
    sem = pl.BlockSpec(memory_space=pltpu.SEMAPHORE)
    outs = _pallas(
        body, name=name, out_shape=[pltpu.HBM(a.shape, a.dtype) for a in srcs + lands],
        in_specs=[hbm] * (2 * n) + [sem, sem, pl.BlockSpec(memory_space=pl.ANY)],
        out_specs=[hbm] * (2 * n), input_output_aliases={i: i for i in range(2 * n)},
        compiler_params=pltpu.CompilerParams(has_side_effects=pltpu.SideEffectType.DATAFLOW_SIDE_EFFECTING),
    )(*srcs, *lands, send_sems, recv_sems, after)
    return list(outs[n:])


def _with_own(land, own, me):
    return lax.dynamic_update_slice(land, own[None], (me,) + (0,) * own.ndim)


def _ada_mod(c_all, ada_w, ada_b_slice):
    def body(c_ref, w_ref, b_ref, o_ref):
        ca = _silu(c_ref[...])
        for l in range(2):
            o_ref[l] = _nn(ca, w_ref[l], HI) + b_ref[l]

    return _pallas(body, name="ada_mod",
                   out_shape=jax.ShapeDtypeStruct((2, c_all.shape[0], ada_w.shape[2]), F32),
                   compiler_params=_cp(vmem=VMEM_MID))(c_all, ada_w, ada_b_slice)


def _ada_w_grad(c_all, dmod_slice):
    def body(c_ref, d_ref, o_ref):
        ca = _silu(c_ref[...])
        for l in range(2):
            o_ref[l] = _tn(ca, d_ref[l], HI)

    return _pallas(body, name="ada_w_grad",
                   out_shape=jax.ShapeDtypeStruct((2, D_MODEL, dmod_slice.shape[2]), F32),
                   compiler_params=_cp(vmem=VMEM_MID))(c_all, dmod_slice)


def _bucket_onehot():
    qi = jnp.arange(BLOCK)[:, None]
    kj = jnp.arange(2 * BLOCK)[None, :]
    rel = qi - kj + BLOCK
    n = jnp.maximum(rel, 0)
    nf = jnp.maximum(n, 1).astype(F32)
    large = REL_MAX_EXACT + (jnp.log(nf / REL_MAX_EXACT) / math.log(REL_MAX_DIST / REL_MAX_EXACT)
                             * (REL_BUCKETS - REL_MAX_EXACT)).astype(jnp.int32)
    large = jnp.minimum(large, REL_BUCKETS - 1)
    bucket = jnp.where(n < REL_MAX_EXACT, n, large).reshape(1, BLOCK * 2 * BLOCK)
    return (jnp.arange(REL_BUCKETS)[:, None] == bucket).astype(F32)


def _bias_expand(rel_bias_t, onehot):
    def body(r_ref, e_ref, o_ref):
        o_ref[...] = _nn(r_ref[...], e_ref[...], HI)

    return _pallas(body, name="bias_expand",
                   out_shape=jax.ShapeDtypeStruct((N_HEADS, onehot.shape[1]), F32),
                   compiler_params=_cp(vmem=VMEM_MID))(rel_bias_t, onehot)


def _bias_reduce(dbias, onehot):
    def body(d_ref, e_ref, o_ref):
        o_ref[...] = _nt(d_ref[...], e_ref[...], HI)

    return _pallas(body, name="bias_reduce",
                   out_shape=jax.ShapeDtypeStruct((N_HEADS, REL_BUCKETS), F32),
                   compiler_params=_cp(vmem=VMEM_MID))(dbias, onehot)


def _norm_proj(name, x, g, shift, scale, w, seq, out_dtype, wf_t=None):
    t_tok = x.shape[0]
    w3d = w.ndim == 3
    n_out = w.shape[0] * w.shape[2] if w3d else w.shape[1]
    cn = w.shape[2] if w3d else 256

    def body(x_ref, g_ref, sh_ref, sc_ref, w_ref, *rest):
        if wf_t is not None:
            wf_ref, h_ref, o_ref, fl_ref = rest
        else:
            h_ref, o_ref = rest
        xv = x_ref[...]
        rstd = lax.rsqrt(jnp.mean(xv * xv, axis=-1, keepdims=True) + EPS)
        h = (xv * rstd) * g_ref[...] * (1.0 + sc_ref[...]) + sh_ref[...]
        hb = h.astype(BF16)
        h_ref[...] = hb
        for j in range(n_out // cn):
            wj = w_ref[j] if w3d else w_ref[:, j * cn:(j + 1) * cn]
            o_ref[:, j * cn:(j + 1) * cn] = _nn(hb, wj).astype(out_dtype)
        if wf_t is not None:
            fl_ref[...] = _nt(wf_ref[...], hb)

    mod_spec = pl.BlockSpec((None, 1, D_MODEL), lambda i: (i * TM // seq, 0, 0))
    w_spec = (pl.BlockSpec(w.shape, lambda i: (0, 0, 0)) if w3d else pl.BlockSpec(w.shape, lambda i: (0, 0)))
    in_specs = [pl.BlockSpec((TM, D_MODEL), lambda i: (i, 0)), pl.BlockSpec((1, D_MODEL), lambda i: (0, 0)),
                mod_spec, mod_spec, w_spec]
    out_shape = [jax.ShapeDtypeStruct((t_tok, D_MODEL), BF16), jax.ShapeDtypeStruct((t_tok, n_out), out_dtype)]
    out_specs = [pl.BlockSpec((TM, D_MODEL), lambda i: (i, 0)), pl.BlockSpec((TM, n_out), lambda i: (i, 0))]
    args = [x, g, shift, scale, w]
    if wf_t is not None:
        in_specs.append(pl.BlockSpec(wf_t.shape, lambda i: (0, 0)))
        out_shape.append(jax.ShapeDtypeStruct((wf_t.shape[0], t_tok), F32))
        out_specs.append(pl.BlockSpec((wf_t.shape[0], TM), lambda i: (0, i)))
        args.append(wf_t)
    return _pallas(body, name=name, grid=(t_tok // TM,), in_specs=in_specs, out_specs=out_specs,
                   out_shape=out_shape, compiler_params=_cp(("arbitrary",), VMEM_BIG))(*args)


def _fox_prep(fl_t, b_f, seq):
    t_tok = fl_t.shape[1]
    ch = 256

    def body(fl_ref, bf_ref, fr_ref, fc_ref):
        z = fl_ref[...] + bf_ref[...]
        logf = jnp.minimum(z, 0.0) - jnp.log(1.0 + jnp.exp(-jnp.abs(z)))
        ri = lax.broadcasted_iota(jnp.int32, (ch, ch), 0)
        ci = lax.broadcasted_iota(jnp.int32, (ch, ch), 1)
        upper = (ri <= ci).astype(F32)
        eye = (ri == ci).astype(F32)
        carry = jnp.zeros((N_HEADS, 1), F32)
        for k in range(seq // ch):
            fk = _nn(logf[:, k * ch:(k + 1) * ch], upper, HI) + carry
            carry = fk[:, ch - 1:ch]
            fr_ref[:, k * ch:(k + 1) * ch] = fk
            padded = jnp.concatenate([fk, jnp.zeros((128 - N_HEADS, ch), F32)], axis=0)
            fc_ref[k * ch:(k + 1) * ch, :] = _nt(eye, padded, HI)

    return _pallas(
        body, name="fox_prep", grid=(t_tok // seq,),
        in_specs=[pl.BlockSpec((N_HEADS, seq), lambda b: (0, b)), pl.BlockSpec((N_HEADS, 1), lambda b: (0, 0))],
        out_specs=[pl.BlockSpec((N_HEADS, seq), lambda b: (0, b)), pl.BlockSpec((seq, 128), lambda b: (b, 0))],
        out_shape=[jax.ShapeDtypeStruct((N_HEADS, t_tok), F32), jax.ShapeDtypeStruct((t_tok, 128), F32)],
        compiler_params=_cp(("arbitrary",), VMEM_MID))(fl_t, b_f)


def _fox_post(df_row, fl_t, b_f, seq):
    t_tok = fl_t.shape[1]
    ch = 256

    def body(d_ref, fl_ref, bf_ref, o_ref, db_ref):
        @pl.when(pl.program_id(0) == 0)
        def _():
            db_ref[...] = jnp.zeros_like(db_ref)

        z = fl_ref[...] + bf_ref[...]
        sig_neg = 1.0 / (1.0 + jnp.exp(z))
        ri = lax.broadcasted_iota(jnp.int32, (ch, ch), 0)
        ci = lax.broadcasted_iota(jnp.int32, (ch, ch), 1)
        lower = (ri >= ci).astype(F32)
        carry = jnp.zeros((N_HEADS, 1), F32)
        tot = jnp.zeros((N_HEADS, 1), F32)
        for k in reversed(range(seq // ch)):
            dk = _nn(d_ref[:, k * ch:(k + 1) * ch], lower, HI) + carry
            carry = dk[:, 0:1]
            dfl = dk * sig_neg[:, k * ch:(k + 1) * ch]
            o_ref[:, k * ch:(k + 1) * ch] = dfl
            tot = tot + jnp.sum(dfl, axis=1, keepdims=True)
        db_ref[...] += jnp.broadcast_to(tot, db_ref.shape)

    return _pallas(
        body, name="fox_post", grid=(t_tok // seq,),
        in_specs=[pl.BlockSpec((N_HEADS, seq), lambda b: (0, b)), pl.BlockSpec((N_HEADS, seq), lambda b: (0, b)),
                  pl.BlockSpec((N_HEADS, 1), lambda b: (0, 0))],
        out_specs=[pl.BlockSpec((N_HEADS, seq), lambda b: (0, b)), pl.BlockSpec((N_HEADS, 128), lambda b: (0, 0))],
        out_shape=[jax.ShapeDtypeStruct((N_HEADS, t_tok), F32), jax.ShapeDtypeStruct((N_HEADS, 128), F32)],
        compiler_params=_cp(("arbitrary",), VMEM_MID))(df_row, fl_t, b_f)


def _eye(n, dtype):
    return (lax.broadcasted_iota(jnp.int32, (n, n), 0) == lax.broadcasted_iota(jnp.int32, (n, n), 1)).astype(dtype)


def _fox_aug(qkvg, f_col, seq):
    t_tok = qkvg.shape[0]
    ta = 256
    nkb = ta // TK

    def body(q_ref, k_ref, v_ref, fc_ref, qa_ref, ka_ref, kt_ref, vt_ref):
        ri = lax.broadcasted_iota(jnp.int32, (128, 128), 0)
        ci = lax.broadcasted_iota(jnp.int32, (128, 128), 1)
        eye = (ri == ci).astype(BF16)
        lane = lax.broadcasted_iota(jnp.int32, (ta, 128), 1)
        ones_q = jnp.where(jnp.logical_and(lane >= 64, lane < 67), 1.0, 0.0)
        ones_k = jnp.where(jnp.logical_and(lane >= 67, lane < 70), 1.0, 0.0)
        fc_tile = fc_ref[...]
        for p in range(N_HEADS // 2):
            q2 = q_ref[:, 128 * p:128 * (p + 1)]
            k2 = k_ref[:, 128 * p:128 * (p + 1)]
            vt = _nt(eye, v_ref[:, 128 * p:128 * (p + 1)]).astype(BF16)
            for kk in range(nkb):
                vt_ref[p, kk] = vt[:, kk * TK:(kk + 1) * TK]
            for e in range(2):
                h = 2 * p + e
                sel = jnp.logical_and(ri == ci + HEAD_DIM * e, ci < HEAD_DIM)
                f = _col(fc_tile, h)
                fh = f.astype(BF16).astype(F32)
                fm = (f - fh).astype(BF16).astype(F32)
                fl = (f - fh - fm).astype(BF16).astype(F32)
                qa = (_nn(q2, jnp.where(sel, SCALE, 0.0).astype(BF16)) + ones_q + jnp.where(lane == 67, fh, 0.0)
                      + jnp.where(lane == 68, fm, 0.0) + jnp.where(lane == 69, fl, 0.0))
                ka = (_nn(k2, jnp.where(sel, 1.0, 0.0).astype(BF16)) + ones_k - jnp.where(lane == 64, fh, 0.0)
                      - jnp.where(lane == 65, fm, 0.0) - jnp.where(lane == 66, fl, 0.0))
                qa_ref[h] = qa.astype(BF16)
                kab = ka.astype(BF16)
                ka_ref[h] = kab
                kt = _nt(eye, kab).astype(BF16)
                for kk in range(ta // TKB):
                    kt_ref[h, kk] = kt[:, kk * TKB:(kk + 1) * TKB]

    aug = jax.ShapeDtypeStruct((N_HEADS, t_tok, 128), BF16)
    return _pallas(
        body, name="fox_aug", grid=(t_tok // ta,),
        in_specs=[pl.BlockSpec((ta, 512), lambda i: (i, C_BQ // 512)), pl.BlockSpec((ta, 512), lambda i: (i, C_BK // 512)),
                  pl.BlockSpec((ta, 512), lambda i: (i, C_BV // 512)), pl.BlockSpec((ta, 128), lambda i: (i, 0))],
        out_specs=[pl.BlockSpec((N_HEADS, ta, 128), lambda i: (0, i, 0)), pl.BlockSpec((N_HEADS, ta, 128), lambda i: (0, i, 0)),
                   pl.BlockSpec((N_HEADS, ta // TKB, 128, TKB), lambda i: (0, i, 0, 0)),
                   pl.BlockSpec((N_HEADS // 2, nkb, 128, TK), lambda i: (0, i, 0, 0))],
        out_shape=[aug, aug, jax.ShapeDtypeStruct((N_HEADS, t_tok // TKB, 128, TKB), BF16),
                   jax.ShapeDtypeStruct((N_HEADS // 2, t_tok // TK, 128, TK), BF16)],
        compiler_params=_cp(("arbitrary",), VMEM_MID))(qkvg, qkvg, qkvg, f_col)


def _fox_fwd_t(q_aug, k_aug, vt, seq):
    t_tok = k_aug.shape[1]
    nq = seq // TQ
    ratio = TQ // TK
    assert ratio == 2, "the two pipeline slots are addressed by the key block's parity"

    def body(qa_ref, ka_ref, vt_ref, o_ref, lse_ref, ml_s, acc_s, st_s, p_s, al_s, qt_s):
        i = pl.program_id(1)
        tpos = i * TQ + lax.broadcasted_iota(jnp.int32, (1, TQ), 1)
        eye = _eye(HEAD_DIM, BF16)
        eye2 = _eye(128, BF16)
        for h in range(N_HEADS):
            qt_s[h] = _nt(eye2, qa_ref[h]).astype(BF16)
            ml_s[0, h] = jnp.full((1, TQ), NEG, F32)
            ml_s[1, h] = jnp.zeros((1, TQ), F32)
            acc_s[h] = jnp.zeros((HEAD_DIM, TQ), F32)
            p_s[1, h] = jnp.zeros((TK, TQ), BF16)
            al_s[1, h] = jnp.ones((1, TQ), F32)

        def scores(j, slot):
            row0 = pl.multiple_of(j * TK, TK)
            for h in range(N_HEADS):
                st_s[slot, h] = _nn(ka_ref[h, pl.ds(row0, TK), :], qt_s[h])

        def softmax(j, slot, masked):
            if masked:
                keep = (j * TK + lax.broadcasted_iota(jnp.int32, (TK, 1), 0)) <= tpos
            for h in range(N_HEADS):
                st = st_s[slot, h]
                if masked:
                    st = jnp.where(keep, st, NEG)
                m = ml_s[0, h]
                m_new = jnp.maximum(m, jnp.max(st, axis=0, keepdims=True))
                alpha = jnp.exp(m - m_new)
                pe = jnp.exp(st - m_new)
                ml_s[0, h] = m_new
                ml_s[1, h] = alpha * ml_s[1, h] + jnp.sum(pe, axis=0, keepdims=True)
                al_s[slot, h] = alpha
                p_s[slot, h] = pe.astype(BF16)

        def values(j, slot):
            jv = jnp.maximum(j, 0)
            for h in range(N_HEADS):
                p, e = divmod(h, 2)
                acc_s[h] = al_s[slot, h] * acc_s[h] + _nn(vt_ref[p, jv, e * HEAD_DIM:(e + 1) * HEAD_DIM, :], p_s[slot, h])

        def step(m, carry):
            for kk in range(ratio):
                j = ratio * m + kk
                values(j - 1, 1 - kk)
                softmax(j, kk, False)
                scores(j + 1, 1 - kk)
            return carry

        scores(0, 0)
        lax.fori_loop(0, i, step, 0)
        for kk in range(ratio):
            j = ratio * i + kk
            values(j - 1, 1 - kk)
            softmax(j, kk, True)
            if kk < ratio - 1:
                scores(j + 1, 1 - kk)
        values(ratio * i + ratio - 1, ratio - 1)
        for p in range(N_HEADS // 2):
            outs = []
            for e in range(2):
                h = 2 * p + e
                l = ml_s[1, h]
                outs.append(_tn((acc_s[h] / l).astype(BF16), eye))
                lse_ref[p, e:e + 1, :] = ml_s[0, h] + jnp.log(l)
            o_ref[:, 128 * p:128 * (p + 1)] = jnp.concatenate(outs, axis=1).astype(BF16)

    return _pallas(
        body, name="fox_fwd", grid=(t_tok // seq, nq),
        in_specs=[pl.BlockSpec((N_HEADS, TQ, 128), lambda b, i: (0, b * nq + i, 0)),
                  pl.BlockSpec((N_HEADS, seq, 128), lambda b, i: (0, b, 0)),
                  pl.BlockSpec((N_HEADS // 2, seq // TK, 128, TK), lambda b, i: (0, b, 0, 0))],
        out_specs=[pl.BlockSpec((TQ, 512), lambda b, i: (b * nq + i, 0)),
                   pl.BlockSpec((N_HEADS // 2, 2, TQ), lambda b, i: (0, 0, b * nq + i))],
        out_shape=[jax.ShapeDtypeStruct((t_tok, 512), BF16), jax.ShapeDtypeStruct((N_HEADS // 2, 2, t_tok), F32)],
        scratch_shapes=[pltpu.VMEM((2, N_HEADS, 1, TQ), F32), pltpu.VMEM((N_HEADS, HEAD_DIM, TQ), F32),
                        pltpu.VMEM((2, N_HEADS, TK, TQ), F32), pltpu.VMEM((2, N_HEADS, TK, TQ), BF16),
                        pltpu.VMEM((2, N_HEADS, 1, TQ), F32), pltpu.VMEM((N_HEADS, 128, TQ), BF16)],
        compiler_params=_cp(("arbitrary", "arbitrary"), VMEM_MID))(q_aug, k_aug, vt)


def _fox_bwd_t(q_aug, k_aug, kt, qkvg, du_b, b_out, lse, seq):
    TK = TKB
    t_tok = qkvg.shape[0]
    nq = seq // TQ
    nkb = seq // TK
    ratio = TQ // TK
    hg = 4

    def body(qa_ref, ka_ref, kt_ref, v_ref, do_ref, o_ref, lse_ref, dq_ref, dk_ref, dv_ref, df_ref,
             dqt_s, row_s, dfk_s, dk_s, dv_s, dot_s, st_s, dp_s, pb_s, db_s, qt_s):
        eye = _eye(HEAD_DIM, BF16)
        eye2 = _eye(128, BF16)
        eye_k = _eye(TK, F32)
        lane8 = lax.broadcasted_iota(jnp.int32, (8, 128), 1)
        lane_k = lax.broadcasted_iota(jnp.int32, (TK, 128), 1)
        first = [lane8 < HEAD_DIM, lane8 >= HEAD_DIM]
        for pp in range(hg // 2):
            for ii in range(nq):
                dot_s[pp, ii] = _nt(eye2, do_ref[ii * TQ:(ii + 1) * TQ, 128 * pp:128 * (pp + 1)]).astype(BF16)
        for hh in range(hg):
            for ii in range(nq):
                qt_s[hh, ii] = _nt(eye2, qa_ref[hh, ii * TQ:(ii + 1) * TQ, :]).astype(BF16)
        for hh in range(hg):
            pp, e = divmod(hh, 2)
            head_lanes = jnp.where(first[e], 1.0, 0.0)
            for ii in range(nq):
                rows = slice(ii * TQ, (ii + 1) * TQ)
                prod = do_ref[rows, 128 * pp:128 * (pp + 1)].astype(F32) * o_ref[rows, 128 * pp:128 * (pp + 1)].astype(F32)
                row_s[hh, ii, 0] = _nt(head_lanes, prod, HI)
                row_s[hh, ii, 1] = jnp.broadcast_to(lse_ref[pp, e:e + 1, ii * TQ:(ii + 1) * TQ], (8, TQ))
                dqt_s[hh, ii] = jnp.zeros((128, TQ), F32)

        def kblock(j, _):
            krow = pl.multiple_of(j * TK, TK)
            spos = j * TK + lax.broadcasted_iota(jnp.int32, (TK, 1), 0)
            for hh in range(hg):
                dk_s[hh] = jnp.zeros((TK, 128), F32)
                dv_s[hh] = jnp.zeros((TK, 128), F32)

            def scores(i, slot):
                for hh in range(hg):
                    pp, e = divmod(hh, 2)
                    own = (lane_k < HEAD_DIM) if e == 0 else (lane_k >= HEAD_DIM)
                    v2 = v_ref[pl.ds(krow, TK), 128 * pp:128 * (pp + 1)]
                    vj = jnp.where(own, v2, jnp.zeros_like(v2))
                    st_s[slot, hh] = _nn(ka_ref[hh, pl.ds(krow, TK), :], qt_s[hh, i])
                    dp_s[slot, hh] = _nn(vj, dot_s[pp, i])

            def elementwise(i, slot, masked):
                if masked:
                    keep = spos <= (i * TQ + lax.broadcasted_iota(jnp.int32, (1, TQ), 1))
                for hh in range(hg):
                    pt = jnp.exp(st_s[slot, hh] - row_s[hh, i, 1][0:1, :])
                    if masked:
                        pt = jnp.where(keep, pt, 0.0)
                    dst = pt * (dp_s[slot, hh] - row_s[hh, i, 0][0:1, :])
                    pb_s[slot, hh] = pt.astype(BF16)
                    db_s[slot, hh] = dst.astype(BF16)

            def grads(i, slot):
                qrow = pl.multiple_of(i * TQ, TQ)
                for hh in range(hg):
                    dst_b = db_s[slot, hh]
                    dv_s[hh] += _nn(pb_s[slot, hh], do_ref[pl.ds(qrow, TQ), 128 * (hh // 2):128 * (hh // 2 + 1)])
                    dk_s[hh] += _nn(dst_b, qa_ref[hh, pl.ds(qrow, TQ), :])
                    dqt_s[hh, i] += _nn(kt_ref[hh, j], dst_b)

            def step(p, carry):
                i = i0 + 2 * p + 1
                grads(i - 1, 0)
                elementwise(i, 1, False)
                scores(i + 1, 0)
                grads(i, 1)
                elementwise(i + 1, 0, False)
                scores(jnp.minimum(i + 2, nq - 1), 1)
                return carry

            i0 = j // ratio
            rest = nq - 1 - i0
            scores(i0, 0)
            elementwise(i0, 0, True)
            scores(jnp.minimum(i0 + 1, nq - 1), 1)
            lax.fori_loop(0, rest // 2, step, 0)

            @pl.when(rest % 2 == 1)
            def _():
                grads(nq - 2, 0)
                elementwise(nq - 1, 1, False)
                grads(nq - 1, 1)

            @pl.when(rest % 2 == 0)
            def _():
                grads(nq - 1, 0)
            for pp in range(hg // 2):
                cols = slice(128 * pp, 128 * (pp + 1))
                dk_ref[pl.ds(krow, TK), cols] = jnp.concatenate(
                    [dk_s[2 * pp][:, :HEAD_DIM], dk_s[2 * pp + 1][:, :HEAD_DIM]], axis=1).astype(BF16)
                dv_ref[pl.ds(krow, TK), cols] = jnp.where(lane_k < HEAD_DIM, dv_s[2 * pp], dv_s[2 * pp + 1]).astype(BF16)
            for hh in range(hg):
                dfk_s[hh, j] = _tn(dk_s[hh][:, HEAD_DIM:HEAD_DIM + 8], eye_k, HI)
            return 0

        lax.fori_loop(0, nkb, kblock, 0)
        for pp in range(hg // 2):
            for ii in range(nq):
                parts = []
                for e in range(2):
                    dqt = dqt_s[2 * pp + e, ii]
                    parts.append(_tn(dqt[0:HEAD_DIM, :].astype(BF16), eye) * SCALE)
                    for kk in range(ratio):
                        jj = ii * ratio + kk
                        df_ref[pp, e:e + 1, jj * TK:(jj + 1) * TK] = (dqt[67:68, kk * TK:(kk + 1) * TK]
                                                                     - dfk_s[2 * pp + e, jj][0:1, :])
                dq_ref[ii * TQ:(ii + 1) * TQ, 128 * pp:128 * (pp + 1)] = jnp.concatenate(parts, axis=1).astype(BF16)

    aug_blk = pl.BlockSpec((hg, seq, 128), lambda b, g: (g, b, 0))
    pair_blk = pl.BlockSpec((seq, 64 * hg), lambda b, g: (b, g))
    row_blk = pl.BlockSpec((hg // 2, 2, seq), lambda b, g: (g, 0, b))
    return _pallas(
        body, name="fox_bwd", grid=(t_tok // seq, N_HEADS // hg),
        in_specs=[aug_blk, aug_blk, pl.BlockSpec((hg, nkb, 128, TK), lambda b, g: (g, b, 0, 0)),
                  pl.BlockSpec((seq, 64 * hg), lambda b, g: (b, C_BV // (64 * hg) + g)), pair_blk, pair_blk, row_blk],
        out_specs=[pair_blk, pair_blk, pair_blk, row_blk],
        out_shape=[jax.ShapeDtypeStruct((t_tok, 512), BF16)] * 3
        + [jax.ShapeDtypeStruct((N_HEADS // 2, 2, t_tok), F32)],
        scratch_shapes=[pltpu.VMEM((hg, nq, 128, TQ), F32), pltpu.VMEM((hg, nq, 2, 8, TQ), F32),
                        pltpu.VMEM((hg, nkb, 8, TK), F32), pltpu.VMEM((hg, TK, 128), F32),
                        pltpu.VMEM((hg, TK, 128), F32), pltpu.VMEM((hg // 2, nq, 128, TQ), BF16),
                        pltpu.VMEM((2, hg, TK, TQ), F32), pltpu.VMEM((2, hg, TK, TQ), F32),
                        pltpu.VMEM((2, hg, TK, TQ), BF16), pltpu.VMEM((2, hg, TK, TQ), BF16),
                        pltpu.VMEM((hg, nq, 128, TQ), BF16)],
        compiler_params=_cp(("arbitrary", "arbitrary"), VMEM_BIG))(q_aug, k_aug, kt, qkvg, du_b, b_out, lse)


def _swa_window(k_ref, v_ref, n):
    prev = pl.multiple_of(jnp.maximum(n - 1, 0) * BLOCK, BLOCK)
    cur = pl.multiple_of(n * BLOCK, BLOCK)
    kwin = jnp.concatenate([k_ref[pl.ds(prev, BLOCK), :], k_ref[pl.ds(cur, BLOCK), :]], axis=0)
    vwin = jnp.concatenate([v_ref[pl.ds(prev, BLOCK), :], v_ref[pl.ds(cur, BLOCK), :]], axis=0)
    ti = lax.broadcasted_iota(jnp.int32, (BLOCK, 2 * BLOCK), 0)
    sj = lax.broadcasted_iota(jnp.int32, (BLOCK, 2 * BLOCK), 1)
    rel = ti - sj + BLOCK
    first_key = jnp.where(n > 0, 0, BLOCK)
    mask = jnp.logical_and(jnp.logical_and(rel >= 0, rel < BLOCK), sj >= first_key)
    return kwin, vwin, mask, prev, cur


def _head_cols(ref, h):
    pair = ref[:, 128 * (h // 2):128 * (h // 2 + 1)]
    return pair[:, (h % 2) * HEAD_DIM:(h % 2 + 1) * HEAD_DIM]


def _swa_logits(q_ref, kwin, bias_ref, h, mask):
    hk = h // KV_GROUP
    s = _nt(_head_cols(q_ref, h), kwin[:, hk * HEAD_DIM:(hk + 1) * HEAD_DIM]) * SCALE + bias_ref[h]
    return jnp.where(mask, s, NEG)


def _swa_fwd(qkvg, bias, sinks, seq):
    t_tok = qkvg.shape[0]
    nb = seq // BLOCK

    def body(sink_ref, q_ref, k_ref, v_ref, bias_ref, o_ref, lse_ref, s_s, p_s, den_s):
        g = pl.program_id(1)
        subs = [pl.ds(s * BLOCK, BLOCK) for s in range(SWA_SUB)]
        wins = [_swa_window(k_ref, v_ref, SWA_SUB * g + s) for s in range(SWA_SUB)]
        for s in range(SWA_SUB):
            for h in range(N_HEADS):
                s_s[s * N_HEADS + h] = _swa_logits(q_ref.at[subs[s]], wins[s][0], bias_ref, h, wins[s][2])
        lane = lax.broadcasted_iota(jnp.int32, (BLOCK, 128), 1)
        for s in range(SWA_SUB):
            lse_tile = jnp.zeros((BLOCK, 128), F32)
            for h in range(N_HEADS):
                sc = s_s[s * N_HEADS + h]
                sink = sink_ref[h]
                m = jnp.maximum(jnp.max(sc, axis=1, keepdims=True), sink)
                pe = jnp.exp(sc - m)
                den = jnp.sum(pe, axis=1, keepdims=True) + jnp.exp(sink - m)
                p_s[s * N_HEADS + h] = pe.astype(BF16)
                den_s[s * N_HEADS + h] = den
                lse_tile = jnp.where(lane == h, m + jnp.log(den), lse_tile)
            lse_ref[subs[s], :] = lse_tile
        for s in range(SWA_SUB):
            vwin = wins[s][1]
            for pr in range(N_HEADS // 2):
                outs = []
                for h in (2 * pr, 2 * pr + 1):
                    hk = h // KV_GROUP
                    outs.append(_nn(p_s[s * N_HEADS + h], vwin[:, hk * HEAD_DIM:(hk + 1) * HEAD_DIM]) / den_s[s * N_HEADS + h])
                o_ref[subs[s], 128 * pr:128 * (pr + 1)] = jnp.concatenate(outs, axis=1).astype(BF16)

    rows = SWA_SUB * BLOCK
    steps = nb // SWA_SUB
    return _pallas(
        body, name="swa_fwd", grid=(t_tok // seq, steps),
        in_specs=[pl.BlockSpec(memory_space=pltpu.SMEM),
                  pl.BlockSpec((rows, 512), lambda b, n: (b * steps + n, C_AQ // 512)),
                  pl.BlockSpec((seq, 128), lambda b, n: (b, C_AK // 128)),
                  pl.BlockSpec((seq, 128), lambda b, n: (b, C_AV // 128)),
                  pl.BlockSpec((N_HEADS, BLOCK, 2 * BLOCK), lambda b, n: (0, 0, 0))],
        out_specs=[pl.BlockSpec((rows, 512), lambda b, n: (b * steps + n, 0)),
                   pl.BlockSpec((rows, 128), lambda b, n: (b * steps + n, 0))],
        out_shape=[jax.ShapeDtypeStruct((t_tok, 512), BF16), jax.ShapeDtypeStruct((t_tok, 128), F32)],
        scratch_shapes=[pltpu.VMEM((SWA_SUB * N_HEADS, BLOCK, 2 * BLOCK), F32),
                        pltpu.VMEM((SWA_SUB * N_HEADS, BLOCK, 2 * BLOCK), BF16),
                        pltpu.VMEM((SWA_SUB * N_HEADS, BLOCK, 1), F32)],
        compiler_params=_cp(("arbitrary", "arbitrary"), VMEM_MID))(sinks, qkvg, qkvg, qkvg, bias)


def _swa_bwd(qkvg, du_a, a_out, lse, bias, sinks, seq):
    t_tok = qkvg.shape[0]
    nb = seq // BLOCK

    def body(sink_ref, q_ref, k_ref, v_ref, do_ref, o_ref, lse_ref, bias_ref,
             dq_ref, dkv_ref, dbias_ref, dsink_ref, kv_s, s_s, dp_s, pb_s, db_s):
        b, n = pl.program_id(0), pl.program_id(1)

        @pl.when(jnp.logical_and(b == 0, n == 0))
        def _():
            dbias_ref[...] = jnp.zeros_like(dbias_ref)
            dsink_ref[...] = jnp.zeros_like(dsink_ref)

        @pl.when(n == 0)
        def _():
            kv_s[...] = jnp.zeros_like(kv_s)

        subs = [pl.ds(s * BLOCK, BLOCK) for s in range(SWA_SUB)]
        wins = [_swa_window(k_ref, v_ref, SWA_SUB * n + s) for s in range(SWA_SUB)]
        for s in range(SWA_SUB):
            kwin, vwin, mask = wins[s][:3]
            for h in range(N_HEADS):
                hk = h // KV_GROUP
                s_s[s * N_HEADS + h] = _swa_logits(q_ref.at[subs[s]], kwin, bias_ref, h, mask)
                dp_s[s * N_HEADS + h] = _nt(_head_cols(do_ref.at[subs[s]], h), vwin[:, hk * HEAD_DIM:(hk + 1) * HEAD_DIM])
        for s in range(SWA_SUB):
            lse_tile = lse_ref[subs[s], :]
            do_s, o_s = do_ref.at[subs[s]], o_ref.at[subs[s]]
            for h in range(N_HEADS):
                delta = jnp.sum(_head_cols(do_s, h).astype(F32) * _head_cols(o_s, h).astype(F32), axis=1, keepdims=True)
                lse_h = _col(lse_tile, h)
                pe = jnp.exp(s_s[s * N_HEADS + h] - lse_h)
                ds = pe * (dp_s[s * N_HEADS + h] - delta)
                dbias_ref[h] += ds
                psink = jnp.exp(sink_ref[h] - lse_h)
                dsink_ref[h:h + 1, :] += jnp.broadcast_to(jnp.sum(-psink * delta, axis=0, keepdims=True), (1, 128))
                pb_s[s * N_HEADS + h] = pe.astype(BF16)
                db_s[s * N_HEADS + h] = ds.astype(BF16)
        for s in range(SWA_SUB):
            kwin, _, _, prev, cur = wins[s]
            q_s, do_s = q_ref.at[subs[s]], do_ref.at[subs[s]]
            for pr in range(N_HEADS // 2):
                dqs = []
                for h in (2 * pr, 2 * pr + 1):
                    hk = h // KV_GROUP
                    dqs.append(_nn(db_s[s * N_HEADS + h], kwin[:, hk * HEAD_DIM:(hk + 1) * HEAD_DIM]) * SCALE)
                dq_ref[subs[s], 128 * pr:128 * (pr + 1)] = jnp.concatenate(dqs, axis=1).astype(BF16)
            dks, dvs = [], []
            for hk in range(N_HEADS // KV_GROUP):
                dk = jnp.zeros((2 * BLOCK, HEAD_DIM), F32)
                dv = jnp.zeros((2 * BLOCK, HEAD_DIM), F32)
                for h in range(hk * KV_GROUP, (hk + 1) * KV_GROUP):
                    dk = dk + _tn(db_s[s * N_HEADS + h], _head_cols(q_s, h))
                    dv = dv + _tn(pb_s[s * N_HEADS + h], _head_cols(do_s, h))
                dks.append(dk * SCALE)
                dvs.append(dv)
            upd = jnp.concatenate(dks + dvs, axis=1)
            kv_s[pl.ds(prev, BLOCK), :] += upd[:BLOCK]
            kv_s[pl.ds(cur, BLOCK), :] += upd[BLOCK:]

        @pl.when(n == steps - 1)
        def _():
            dkv_ref[...] = kv_s[...].astype(BF16)

    rows = SWA_SUB * BLOCK
    steps = nb // SWA_SUB
    tile = (SWA_SUB * N_HEADS, BLOCK, 2 * BLOCK)
    return _pallas(
        body, name="swa_bwd", grid=(t_tok // seq, steps),
        in_specs=[pl.BlockSpec(memory_space=pltpu.SMEM),
                  pl.BlockSpec((rows, 512), lambda b, n: (b * steps + n, C_AQ // 512)),
                  pl.BlockSpec((seq, 128), lambda b, n: (b, C_AK // 128)),
                  pl.BlockSpec((seq, 128), lambda b, n: (b, C_AV // 128)),
                  pl.BlockSpec((rows, 512), lambda b, n: (b * steps + n, 0)),
                  pl.BlockSpec((rows, 512), lambda b, n: (b * steps + n, 0)),
                  pl.BlockSpec((rows, 128), lambda b, n: (b * steps + n, 0)),
                  pl.BlockSpec((N_HEADS, BLOCK, 2 * BLOCK), lambda b, n: (0, 0, 0))],
        out_specs=[pl.BlockSpec((rows, 512), lambda b, n: (b * steps + n, 0)),
                   pl.BlockSpec((seq, 256), lambda b, n: (b, 0)),
                   pl.BlockSpec((N_HEADS, BLOCK, 2 * BLOCK), lambda b, n: (0, 0, 0)),
                   pl.BlockSpec((N_HEADS, 128), lambda b, n: (0, 0))],
        out_shape=[jax.ShapeDtypeStruct((t_tok, 512), BF16), jax.ShapeDtypeStruct((t_tok, 256), BF16),
                   jax.ShapeDtypeStruct((N_HEADS, BLOCK, 2 * BLOCK), F32), jax.ShapeDtypeStruct((N_HEADS, 128), F32)],
        scratch_shapes=[pltpu.VMEM((seq, 256), F32), pltpu.VMEM(tile, F32), pltpu.VMEM(tile, F32),
                        pltpu.VMEM(tile, BF16), pltpu.VMEM(tile, BF16)],
        compiler_params=_cp(("arbitrary", "arbitrary"), VMEM_MID))(sinks, qkvg, qkvg, qkvg, du_a, a_out, lse, bias)


def _out_proj(name, u_parts, gate_arr, gate_blk, w_out, x, gmod, seq):
    t_tok = x.shape[0]
    nu = len(u_parts)

    def body(*refs):
        u_refs = refs[:nu]
        g_ref, w_ref, x_ref, gm_ref, yg_ref, y_ref, xn_ref = refs[nu:]
        u = jnp.concatenate([r[...].astype(F32) for r in u_refs], axis=1) if nu > 1 else u_refs[0][...].astype(F32)
        yg = (u * _silu(g_ref[...].astype(F32))).astype(BF16)
        yg_ref[...] = yg
        y = _nn(yg, w_ref[...])
        y_ref[...] = y.astype(BF16)
        xn_ref[...] = x_ref[...] + gm_ref[...] * y

    row = lambda w: pl.BlockSpec((TM, w), lambda i: (i, 0))
    in_specs = [row(u.shape[1]) for u in u_parts]
    in_specs += [pl.BlockSpec((TM, D_MODEL), lambda i: (i, gate_blk)),
                 pl.BlockSpec((D_MODEL, D_MODEL), lambda i: (0, 0)), row(D_MODEL),
                 pl.BlockSpec((None, 1, D_MODEL), lambda i: (i * TM // seq, 0, 0))]
    return _pallas(
        body, name=name, grid=(t_tok // TM,), in_specs=in_specs,
        out_specs=[row(D_MODEL)] * 3,
        out_shape=[jax.ShapeDtypeStruct((t_tok, D_MODEL), BF16)] * 2 + [jax.ShapeDtypeStruct((t_tok, D_MODEL), F32)],
        compiler_params=_cp(("arbitrary",), VMEM_MID))(*u_parts, gate_arr, w_out, x, gmod)


def _out_proj_bwd(name, dxn, gmod, y, w_out, seq, attn=None):
    t_tok = dxn.shape[0]
    tiles_per_seq = seq // TM

    def body(*refs):
        if attn is None:
            dxn_ref, gm_ref, y_ref, w_ref, dy_ref, dgm_ref, dyg_ref = refs
        else:
            dxn_ref, gm_ref, y_ref, w_ref, a_ref, b_ref, g_ref, dy_ref, dgm_ref, dua_ref, dub_ref, dg_ref = refs
        i = pl.program_id(0)
        dxv = dxn_ref[...]
        dy = (dxv * gm_ref[...]).astype(BF16)
        dy_ref[...] = dy

        @pl.when(i % tiles_per_seq == 0)
        def _():
            dgm_ref[...] = jnp.zeros_like(dgm_ref)

        dgm_ref[...] += jnp.sum(dxv * y_ref[...].astype(F32), axis=0, keepdims=True)
        dyg = _nn(dy, w_ref[...])
        if attn is None:
            dyg_ref[...] = dyg
        else:
            gt = g_ref[...].astype(F32)
            du = dyg * _silu(gt)
            dua_ref[...] = du[:, :512].astype(BF16)
            dub_ref[...] = du[:, 512:].astype(BF16)
            u = jnp.concatenate([a_ref[...].astype(F32), b_ref[...].astype(F32)], axis=1)
            dg_ref[...] = (dyg * u * _dsilu(gt)).astype(BF16)

    row = lambda w: pl.BlockSpec((TM, w), lambda i: (i, 0))
    mod_spec = pl.BlockSpec((None, 1, D_MODEL), lambda i: (i * TM // seq, 0, 0))
    in_specs = [row(D_MODEL), mod_spec, row(D_MODEL), pl.BlockSpec((D_MODEL, D_MODEL), lambda i: (0, 0))]
    out_specs = [row(D_MODEL), mod_spec]
    out_shape = [jax.ShapeDtypeStruct((t_tok, D_MODEL), BF16), jax.ShapeDtypeStruct(gmod.shape, F32)]
    args = [dxn, gmod, y, w_out]
    if attn is None:
        out_specs.append(row(D_MODEL))
        out_shape.append(jax.ShapeDtypeStruct((t_tok, D_MODEL), F32))
    else:
        in_specs += [row(512), row(512), pl.BlockSpec((TM, D_MODEL), lambda i: (i, C_GATE // D_MODEL))]
        out_specs += [row(512), row(512), row(D_MODEL)]
        out_shape += [jax.ShapeDtypeStruct((t_tok, 512), BF16)] * 2 + [jax.ShapeDtypeStruct((t_tok, D_MODEL), BF16)]
        args += list(attn)
    return _pallas(body, name=name, grid=(t_tok // TM,), in_specs=in_specs, out_specs=out_specs,
                   out_shape=out_shape, compiler_params=_cp(("arbitrary",), VMEM_MID))(*args)


def _norm_bwd(name, parts, w, x, g, scale, dxn, seq, rows_part=None):
    t_tok = x.shape[0]
    npart = len(parts)
    tiles_per_seq = seq // TM
    nrow_in = 0 if rows_part is None else 2

    def body(*refs):
        p_refs = refs[:npart]
        w_ref, x_ref, g_ref, sc_ref, dxn_ref = refs[npart:npart + 5]
        dx_ref, dss_ref, dg_ref = refs[npart + 5 + nrow_in:]
        i = pl.program_id(0)
        dh = jnp.zeros((TM, D_MODEL), F32)
        if rows_part is not None:
            r_ref, wr_ref = refs[npart + 5:npart + 7]
            dh = dh + _tn(r_ref[...].astype(BF16), wr_ref[...])
        for (arr, off), p_ref in zip(parts, p_refs):
            dh = dh + _nn(p_ref[...], w_ref[off:off + arr.shape[1], :])
        xv = x_ref[...]
        rstd = lax.rsqrt(jnp.mean(xv * xv, axis=-1, keepdims=True) + EPS)
        xhat = xv * rstd
        gv = g_ref[...]
        nrm = xhat * gv

        @pl.when(i % tiles_per_seq == 0)
        def _():
            dss_ref[...] = jnp.zeros_like(dss_ref)

        @pl.when(i == 0)
        def _():
            dg_ref[...] = jnp.zeros_like(dg_ref)

        dss_ref[0:1, :] += jnp.sum(dh, axis=0, keepdims=True)
        dss_ref[1:2, :] += jnp.sum(dh * nrm, axis=0, keepdims=True)
        dn = dh * (1.0 + sc_ref[...])
        dg_ref[0:1, :] += jnp.sum(dn * xhat, axis=0, keepdims=True)
        dxhat = dn * gv
        dx_ref[...] = rstd * (dxhat - xhat * jnp.mean(dxhat * xhat, axis=-1, keepdims=True)) + dxn_ref[...]

    row = lambda wd: pl.BlockSpec((TM, wd), lambda i: (i, 0))
    w_spec = pl.BlockSpec(w.shape, lambda i: (0, 0))
    in_specs = [row(a.shape[1]) for a, _ in parts]
    in_specs += [w_spec, row(D_MODEL), pl.BlockSpec((1, D_MODEL), lambda i: (0, 0)),
                 pl.BlockSpec((None, 1, D_MODEL), lambda i: (i * TM // seq, 0, 0)), row(D_MODEL)]
    args = [a for a, _ in parts] + [w, x, g, scale, dxn]
    if rows_part is not None:
        in_specs += [pl.BlockSpec((8, TM), lambda i: (0, i)), pl.BlockSpec((8, D_MODEL), lambda i: (0, 0))]
        args += list(rows_part)
    nseq = t_tok // seq
    return _pallas(
        body, name=name, grid=(t_tok // TM,), in_specs=in_specs,
        out_specs=[row(D_MODEL), pl.BlockSpec((None, 8, D_MODEL), lambda i: (i * TM // seq, 0, 0)),
                   pl.BlockSpec((8, D_MODEL), lambda i: (0, 0))],
        out_shape=[jax.ShapeDtypeStruct((t_tok, D_MODEL), F32), jax.ShapeDtypeStruct((nseq, 8, D_MODEL), F32),
                   jax.ShapeDtypeStruct((8, D_MODEL), F32)],
        compiler_params=_cp(("arbitrary",), VMEM_BIG))(*args)


def _dw(name, a, parts, blocked=None):
    t_tok, ka = a.shape
    tt = min(1024, t_tok)
    npart = len(parts)
    nt = t_tok // tt

    def body(*refs):
        a_ref = refs[0]
        p_refs = refs[1:1 + npart]
        o_refs = refs[1 + npart:1 + 2 * npart]
        acc_refs = refs[1 + 2 * npart:]
        t = pl.program_id(0)
        at = a_ref[...].T
        for p_ref, acc in zip(p_refs, acc_refs):
            upd = _nn(at, p_ref[...])

            @pl.when(t == 0)
            def _():
                acc[...] = upd

            @pl.when(t > 0)
            def _():
                acc[...] += upd

        @pl.when(t == nt - 1)
        def _():
            for o_ref, acc in zip(o_refs, acc_refs):
                if blocked is None:
                    o_ref[...] = acc[...].astype(BF16)
                else:
                    for j in range(o_ref.shape[0]):
                        o_ref[j] = acc[:, j * blocked:(j + 1) * blocked].astype(BF16)

    in_specs = [pl.BlockSpec((tt, ka), lambda t: (t, 0))]
    in_specs += [pl.BlockSpec((tt, p.shape[1]), lambda t: (t, 0)) for p in parts]
    if blocked is None:
        out_shape = [jax.ShapeDtypeStruct((ka, p.shape[1]), BF16) for p in parts]
        out_specs = [pl.BlockSpec((ka, p.shape[1]), lambda t: (0, 0)) for p in parts]
    else:
        out_shape = [jax.ShapeDtypeStruct((p.shape[1] // blocked, ka, blocked), BF16) for p in parts]
        out_specs = [pl.BlockSpec((p.shape[1] // blocked, ka, blocked), lambda t: (0, 0, 0)) for p in parts]
    return _pallas(body, name=name, grid=(nt,), in_specs=in_specs, out_specs=out_specs, out_shape=out_shape,
                   scratch_shapes=[pltpu.VMEM((ka, p.shape[1]), F32) for p in parts],
                   compiler_params=_cp(("arbitrary",), VMEM_BIG))(a, *parts)


def _dw_rows(name, rows_t, h):
    t_tok = h.shape[0]
    tt = 512

    def body(r_ref, h_ref, o_ref):
        @pl.when(pl.program_id(0) == 0)
        def _():
            o_ref[...] = jnp.zeros_like(o_ref)

        o_ref[...] += _nn(r_ref[...].astype(BF16), h_ref[...])

    return _pallas(body, name=name, grid=(t_tok // tt,),
                   in_specs=[pl.BlockSpec((8, tt), lambda t: (0, t)), pl.BlockSpec((tt, D_MODEL), lambda t: (t, 0))],
                   out_specs=pl.BlockSpec((8, D_MODEL), lambda t: (0, 0)),
                   out_shape=jax.ShapeDtypeStruct((8, D_MODEL), F32),
                   compiler_params=_cp(("arbitrary",), VMEM_MID))(rows_t, h)


def _lru_gates(xc, blk, wa_ref, wx_ref, ba_ref, bx_ref, sp):
    cols = slice(blk * LRU_BLOCK_W, (blk + 1) * LRU_BLOCK_W)
    xb = xc[:, cols].astype(BF16)
    r = _sigmoid(_nn(xb, wa_ref[blk].astype(BF16)) + ba_ref[:, cols])
    ig = _sigmoid(_nn(xb, wx_ref[blk].astype(BF16)) + bx_ref[:, cols])
    log_a = -LRU_C * r * sp[:, cols]
    a = jnp.exp(log_a)
    x2 = 2.0 * log_a
    series = -x2 * (1.0 + x2 * (0.5 + x2 * (1.0 / 6.0)))
    z = jnp.where(x2 > -0.01, series, 1.0 - a * a)
    mult = z * lax.rsqrt(jnp.maximum(z, 1e-30))
    return xb, r, ig, a, mult


def _softplus_neg(lam):
    return jnp.maximum(-lam, 0.0) + jnp.log(1.0 + jnp.exp(-jnp.abs(lam)))


def _conv_taps(xe_ref, cw_ref, cb_ref):
    xc = cb_ref[...] + xe_ref[8:8 + TC, :] * cw_ref[3:4, :]
    for k in range(1, 4):
        xc = xc + xe_ref[8 - k:8 - k + TC, :] * cw_ref[3 - k:4 - k, :]
    return xc


def _lru_fwd(proj, cw, cb, w_a, b_a, w_x, b_x, lam, seq):
    t_tok = proj.shape[0]
    nc = seq // TC

    def body(x_ref, cw_ref, cb_ref, wa_ref, ba_ref, wx_ref, bx_ref, lam_ref, hs_ref, xe_s, a_s, u_s, h_s):
        c = pl.program_id(1)

        @pl.when(c == 0)
        def _():
            xe_s[0:8, :] = jnp.zeros((8, D_MODEL), F32)
            h_s[...] = jnp.zeros_like(h_s)

        xe_s[8:8 + TC, :] = x_ref[...]
        xc = _conv_taps(xe_s, cw_ref, cb_ref)
        sp = _softplus_neg(lam_ref[...])
        for blk in range(LRU_BLOCKS):
            cols = slice(blk * LRU_BLOCK_W, (blk + 1) * LRU_BLOCK_W)
            _, _, ig, a, mult = _lru_gates(xc, blk, wa_ref, wx_ref, ba_ref, bx_ref, sp)
            a_s[:, cols] = a
            u_s[:, cols] = mult * ig * xc[:, cols]

        def step(t8, h):
            base = pl.multiple_of(t8 * 8, 8)
            for q in range(8):
                h = a_s[pl.ds(base + q, 1), :] * h + u_s[pl.ds(base + q, 1), :]
                hs_ref[pl.ds(base + q, 1), :] = h
            return h

        h_s[0:1, :] = lax.fori_loop(0, TC // 8, step, h_s[0:1, :])
        xe_s[0:8, :] = xe_s[TC:TC + 8, :]

    full = lambda shape: pl.BlockSpec(shape, lambda b, c: (0,) * len(shape))
    return _pallas(
        body, name="lru_fwd", grid=(t_tok // seq, nc),
        in_specs=[pl.BlockSpec((TC, D_MODEL), lambda b, c: (b * nc + c, 0)), full((4, D_MODEL)), full((1, D_MODEL)),
                  full((LRU_BLOCKS, LRU_BLOCK_W, LRU_BLOCK_W)), full((1, D_MODEL)),
                  full((LRU_BLOCKS, LRU_BLOCK_W, LRU_BLOCK_W)), full((1, D_MODEL)), full((1, D_MODEL))],
        out_specs=pl.BlockSpec((TC, D_MODEL), lambda b, c: (b * nc + c, 0)),
        out_shape=jax.ShapeDtypeStruct((t_tok, D_MODEL), F32),
        scratch_shapes=[pltpu.VMEM((TC + 8, D_MODEL), F32), pltpu.VMEM((TC, D_MODEL), F32),
                        pltpu.VMEM((TC, D_MODEL), F32), pltpu.VMEM((8, D_MODEL), F32)],
        compiler_params=_cp(("arbitrary", "arbitrary"), VMEM_BIG))(proj, cw, cb, w_a, b_a, w_x, b_x, lam)


def _lru_bwd(proj, hs, dyh, cw, cb, w_a, b_a, w_x, b_x, lam, seq):
    t_tok = proj.shape[0]
    nc = seq // TC

    def body(x_ref, xh_ref, g_ref, hs_ref, hh_ref, dy_ref, cw_ref, cb_ref, wa_ref, ba_ref, wx_ref, bx_ref, lam_ref,
             dp_ref, dcw_ref, dvec_ref, dwa_ref, dwx_ref,
             xe_s, he_s, de_s, a_s, r_s, i_s, m_s, dhs_s, dh_s, carry_s):
        b, cr = pl.program_id(0), pl.program_id(1)
        c = nc - 1 - cr

        @pl.when(jnp.logical_and(b == 0, cr == 0))
        def _():
            dcw_ref[...] = jnp.zeros_like(dcw_ref)
            dvec_ref[...] = jnp.zeros_like(dvec_ref)
            dwa_ref[...] = jnp.zeros_like(dwa_ref)
            dwx_ref[...] = jnp.zeros_like(dwx_ref)

        @pl.when(cr == 0)
        def _():
            carry_s[...] = jnp.zeros_like(carry_s)
            de_s[TC:TC + 8, :] = jnp.zeros((8, D_MODEL), F32)

        first = c == 0
        xe_s[0:8, :] = jnp.where(first, 0.0, xh_ref[...])
        xe_s[8:8 + TC, :] = x_ref[...]
        he_s[0:8, :] = jnp.where(first, 0.0, hh_ref[...])
        he_s[8:8 + TC, :] = hs_ref[...]
        xc = _conv_taps(xe_s, cw_ref, cb_ref)
        lam_v = lam_ref[...]
        sp = _softplus_neg(lam_v)
        for blk in range(LRU_BLOCKS):
            cols = slice(blk * LRU_BLOCK_W, (blk + 1) * LRU_BLOCK_W)
            _, r, ig, a, mult = _lru_gates(xc, blk, wa_ref, wx_ref, ba_ref, bx_ref, sp)
            a_s[:, cols], r_s[:, cols], i_s[:, cols], m_s[:, cols] = a, r, ig, mult

        gt = g_ref[...]
        dyh = dy_ref[...]
        sg = _sigmoid(gt)
        dhs_s[...] = dyh * (gt * sg)
        dp_ref[:, D_MODEL:] = (dyh * hs_ref[...] * (sg * (1.0 + gt * (1.0 - sg)))).astype(BF16)

        def step(k8, carry):
            base = pl.multiple_of(TC - 8 - k8 * 8, 8)
            for q in reversed(range(8)):
                dh = dhs_s[pl.ds(base + q, 1), :] + carry
                dh_s[pl.ds(base + q, 1), :] = dh
                carry = a_s[pl.ds(base + q, 1), :] * dh
            return carry

        carry_s[0:1, :] = lax.fori_loop(0, TC // 8, step, carry_s[0:1, :])

        hprev = he_s[7:7 + TC, :]
        for blk in range(LRU_BLOCKS):
            cols = slice(blk * LRU_BLOCK_W, (blk + 1) * LRU_BLOCK_W)
            xcb = xc[:, cols]
            a, r, ig, mult, dh = a_s[:, cols], r_s[:, cols], i_s[:, cols], m_s[:, cols], dh_s[:, cols]
            spb = sp[:, cols]
            dmult = dh * ig * xcb
            di = dh * mult * xcb
            dxc = dh * mult * ig
            dla = dh * hprev[:, cols] * a - dmult * (a * a) * lax.rsqrt(jnp.maximum(mult * mult, 1e-30))
            dr = dla * (-LRU_C * spb)
            dsp = jnp.sum(dla * (-LRU_C * r), axis=0, keepdims=True)
            dga = dr * r * (1.0 - r)
            dgx = di * ig * (1.0 - ig)
            dga_b, dgx_b = dga.astype(BF16), dgx.astype(BF16)
            xb = xcb.astype(BF16)
            dxc = dxc + _nt(dga_b, wa_ref[blk].astype(BF16)) + _nt(dgx_b, wx_ref[blk].astype(BF16))
            dwa_ref[blk] += _tn(xb, dga_b)
            dwx_ref[blk] += _tn(xb, dgx_b)
            dvec_ref[1:2, cols] += jnp.sum(dga, axis=0, keepdims=True)
            dvec_ref[2:3, cols] += jnp.sum(dgx, axis=0, keepdims=True)
            dvec_ref[3:4, cols] += dsp * (-1.0 / (1.0 + jnp.exp(lam_v[:, cols])))
            de_s[0:TC, cols] = dxc

        dxc = de_s[0:TC, :]
        dvec_ref[0:1, :] += jnp.sum(dxc, axis=0, keepdims=True)
        dxr = dxc * cw_ref[3:4, :]
        dcw_ref[3:4, :] += jnp.sum(dxc * xe_s[8:8 + TC, :], axis=0, keepdims=True)
        for k in range(1, 4):
            dxr = dxr + de_s[k:k + TC, :] * cw_ref[3 - k:4 - k, :]
            dcw_ref[3 - k:4 - k, :] += jnp.sum(dxc * xe_s[8 - k:8 - k + TC, :], axis=0, keepdims=True)
        dp_ref[:, :D_MODEL] = dxr.astype(BF16)
        de_s[TC:TC + 8, :] = de_s[0:8, :]

    chunk = lambda col: pl.BlockSpec((TC, D_MODEL), lambda b, cr: (b * nc + nc - 1 - cr, col))
    halo = lambda col: pl.BlockSpec(
        (8, D_MODEL), lambda b, cr: (jnp.maximum((b * nc + nc - 1 - cr) * (TC // 8) - 1, 0), col))
    full = lambda shape: pl.BlockSpec(shape, lambda b, cr: (0,) * len(shape))
    wblk = (LRU_BLOCKS, LRU_BLOCK_W, LRU_BLOCK_W)
    return _pallas(
        body, name="lru_bwd", grid=(t_tok // seq, nc),
        in_specs=[chunk(0), halo(0), chunk(1), chunk(0), halo(0), chunk(0),
                  full((4, D_MODEL)), full((1, D_MODEL)), full(wblk), full((1, D_MODEL)), full(wblk),
                  full((1, D_MODEL)), full((1, D_MODEL))],
        out_specs=[pl.BlockSpec((TC, 2 * D_MODEL), lambda b, cr: (b * nc + nc - 1 - cr, 0)),
                   full((8, D_MODEL)), full((8, D_MODEL)), full(wblk), full(wblk)],
        out_shape=[jax.ShapeDtypeStruct((t_tok, 2 * D_MODEL), BF16), jax.ShapeDtypeStruct((8, D_MODEL), F32),
                   jax.ShapeDtypeStruct((8, D_MODEL), F32), jax.ShapeDtypeStruct(wblk, F32),
                   jax.ShapeDtypeStruct(wblk, F32)],
        scratch_shapes=[pltpu.VMEM((TC + 8, D_MODEL), F32), pltpu.VMEM((TC + 8, D_MODEL), F32),
                        pltpu.VMEM((TC + 8, D_MODEL), F32)]
        + [pltpu.VMEM((TC, D_MODEL), F32)] * 6 + [pltpu.VMEM((8, D_MODEL), F32)],
        compiler_params=_cp(("arbitrary", "arbitrary"), VMEM_BIG),
    )(proj, proj, proj, hs, hs, dyh, cw, cb, w_a, b_a, w_x, b_x, lam)


def _last_layer_tail(hs, proj, w_out, w_out_t, x, gmod, final_g, target, seq):
    t_tok = x.shape[0]
    tiles_per_seq = seq // TM

    def body(hs_ref, g_ref, w_ref, wt_ref, x_ref, gm_ref, fg_ref, t_ref,
             yg_ref, dx_ref, dy_ref, dyg_ref, dgm_ref, loss_ref, dfg_ref):
        i = pl.program_id(0)

        @pl.when(i == 0)
        def _():
            loss_ref[...] = jnp.zeros_like(loss_ref)
            dfg_ref[...] = jnp.zeros_like(dfg_ref)

        @pl.when(i % tiles_per_seq == 0)
        def _():
            dgm_ref[...] = jnp.zeros_like(dgm_ref)

        gm = gm_ref[...]
        yg = (hs_ref[...] * _silu(g_ref[...])).astype(BF16)
        yg_ref[...] = yg
        y = _nn(yg, w_ref[...])
        xv = x_ref[...] + gm * y
        gv = fg_ref[...]
        rstd = lax.rsqrt(jnp.mean(xv * xv, axis=-1, keepdims=True) + EPS)
        xhat = xv * rstd
        err = xhat * gv - t_ref[...]
        loss_ref[0:1, :] += jnp.sum(err * err, axis=0, keepdims=True) * (0.5 / D_MODEL)
        dout = err * (1.0 / D_MODEL)
        dfg_ref[0:1, :] += jnp.sum(dout * xhat, axis=0, keepdims=True)
        dxhat = dout * gv
        dxv = rstd * (dxhat - xhat * jnp.mean(dxhat * xhat, axis=-1, keepdims=True))
        dx_ref[...] = dxv
        dgm_ref[...] += jnp.sum(dxv * y, axis=0, keepdims=True)
        dy = (dxv * gm).astype(BF16)
        dy_ref[...] = dy
        dyg_ref[...] = _nn(dy, wt_ref[...])

    row = pl.BlockSpec((TM, D_MODEL), lambda i: (i, 0))
    acc = pl.BlockSpec((8, D_MODEL), lambda i: (0, 0))
    mod_spec = pl.BlockSpec((None, 1, D_MODEL), lambda i: (i * TM // seq, 0, 0))
    return _pallas(
        body, name="last_layer_tail", grid=(t_tok // TM,),
        in_specs=[row, pl.BlockSpec((TM, D_MODEL), lambda i: (i, 1)), pl.BlockSpec((D_MODEL, D_MODEL), lambda i: (0, 0)),
                  pl.BlockSpec((D_MODEL, D_MODEL), lambda i: (0, 0)),
                  row, mod_spec, pl.BlockSpec((1, D_MODEL), lambda i: (0, 0)), row],
        out_specs=[row, row, row, row, mod_spec, acc, acc],
        out_shape=[jax.ShapeDtypeStruct((t_tok, D_MODEL), BF16), jax.ShapeDtypeStruct((t_tok, D_MODEL), F32),
                   jax.ShapeDtypeStruct((t_tok, D_MODEL), BF16), jax.ShapeDtypeStruct((t_tok, D_MODEL), F32),
                   jax.ShapeDtypeStruct(gmod.shape, F32), jax.ShapeDtypeStruct((8, D_MODEL), F32),
                   jax.ShapeDtypeStruct((8, D_MODEL), F32)],
        compiler_params=_cp(("arbitrary",), VMEM_BIG))(hs, proj, w_out, w_out_t, x, gmod, final_g, target)


def _adam_math(w, g, m, v):
    m_new = ADAM_B1 * m + (1.0 - ADAM_B1) * g
    v_new = ADAM_B2 * v + (1.0 - ADAM_B2) * (g * g)
    m_hat = m_new / (1.0 - ADAM_B1 ** ADAM_STEP)
    v_hat = v_new / (1.0 - ADAM_B2 ** ADAM_STEP)
    delta = -ADAM_LR * (m_hat / (jnp.sqrt(v_hat) + ADAM_EPS) + ADAM_WD * w)
    return delta, m_new, v_new


def _sum_leading(name, x, out_dtype=F32):
    n, rows, cols = x.shape
    tr = PACK_ROWS if rows % PACK_ROWS == 0 else rows

    def body(x_ref, o_ref):
        acc = x_ref[0].astype(F32)
        for d in range(1, n):
            acc = acc + x_ref[d].astype(F32)
        o_ref[...] = acc.astype(out_dtype)

    return _pallas(body, name=name, grid=(rows // tr,),
                   in_specs=[pl.BlockSpec((n, tr, cols), lambda i: (0, i, 0))],
                   out_specs=pl.BlockSpec((tr, cols), lambda i: (i, 0)),
                   out_shape=jax.ShapeDtypeStruct((rows, cols), out_dtype),
                   compiler_params=_cp(("arbitrary",), VMEM_MID))(x)


def _adamw(name, w, m, v, g=None, parts=None):
    rows, cols = w.shape
    tr = rows if rows <= 256 else 256

    def body(*refs):
        w_ref, m_ref, v_ref, g_in, g_ref, d_ref, mo_ref, vo_ref = refs
        if parts is None:
            gv = g_in[...]
        else:
            acc = g_in[0].astype(F32)
            for d in range(1, parts.shape[0]):
                acc = acc + g_in[d].astype(F32)
            gv = acc[:, :cols]
        delta, m_new, v_new = _adam_math(w_ref[...], gv, m_ref[...], v_ref[...])
        g_ref[...] = gv
        d_ref[...] = delta
        mo_ref[...] = m_new
        vo_ref[...] = v_new

    row = pl.BlockSpec((tr, cols), lambda i: (i, 0))
    if parts is None:
        g_spec, g_arg = row, g
    else:
        g_spec, g_arg = pl.BlockSpec((parts.shape[0], tr, parts.shape[2]), lambda i: (0, i, 0)), parts
    return _pallas(body, name=name, grid=(rows // tr,), in_specs=[row, row, row, g_spec], out_specs=[row] * 4,
                   out_shape=[jax.ShapeDtypeStruct((rows, cols), F32)] * 4,
                   compiler_params=_cp(("arbitrary",), VMEM_MID))(w, m, v, g_arg)


def _adamw_many(name, groups):
    ntens = len(groups)

    def body(*refs):
        ins, outs = refs[:4 * ntens], refs[4 * ntens:]
        for k in range(ntens):
            w_ref, m_ref, v_ref, g_ref = ins[4 * k:4 * k + 4]
            gv = g_ref[...]
            delta, m_new, v_new = _adam_math(w_ref[...], gv, m_ref[...], v_ref[...])
            for o_ref, val in zip(outs[4 * k:4 * k + 4], (gv, delta, m_new, v_new)):
                o_ref[...] = val

    flat = [a for grp in groups for a in grp]
    out_shape = [jax.ShapeDtypeStruct(grp[0].shape, F32) for grp in groups for _ in range(4)]
    outs = _pallas(body, name=name, out_shape=out_shape, compiler_params=_cp(vmem=VMEM_MID))(*flat)
    return [tuple(outs[4 * k:4 * k + 4]) for k in range(ntens)]


def _pack_rows(arrs):
    rows, meta, total = [], [], 0
    for a in arrs:
        flat = a.reshape(-1)
        nrow = -(-flat.shape[0] // 1024) * 8
        rows.append(jnp.pad(flat, (0, nrow * 128 - flat.shape[0])).reshape(nrow, 128))
        meta.append((a.shape, flat.shape[0], nrow))
        total += nrow
    tail = -total % PACK_ROWS
    if tail:
        rows.append(jnp.zeros((tail, 128), F32))
    return jnp.concatenate(rows, axis=0), meta


def _unpack_rows(packed, meta):
    out, r0 = [], 0
    for shape, size, nrow in meta:
        out.append(packed[r0:r0 + nrow].reshape(-1)[:size].reshape(shape))
        r0 += nrow
    return out


WEIGHTS = ["rel_bias", "norm_g", "ada_w", "ada_b", "attn_w_in", "attn_sinks", "attn_b_f", "attn_w_out", "lru_w_in",
           "lru_conv_w", "lru_conv_b", "lru_w_a", "lru_b_a", "lru_w_x", "lru_b_x", "lru_lambda", "lru_w_out", "final_g"]
BIG = ["ada_w", "attn_w_in", "attn_w_out", "lru_w_in", "lru_w_out"]
PACK_ROWS = 256


def kernel(x, c, rel_bias, norm_g, ada_w, ada_b, attn_w_in, attn_sinks, attn_b_f, attn_w_out, lru_w_in, lru_conv_w, lru_conv_b, lru_w_a, lru_b_a, lru_w_x, lru_b_x, lru_lambda, lru_w_out, final_g, loss_target, m_rel_bias, m_norm_g, m_ada_w, m_ada_b, m_attn_w_in, m_attn_sinks, m_attn_b_f, m_attn_w_out, m_lru_w_in, m_lru_conv_w, m_lru_conv_b, m_lru_w_a, m_lru_b_a, m_lru_w_x, m_lru_b_x, m_lru_lambda, m_lru_w_out, m_final_g, v_rel_bias, v_norm_g, v_ada_w, v_ada_b, v_attn_w_in, v_attn_sinks, v_attn_b_f, v_attn_w_out, v_lru_w_in, v_lru_conv_w, v_lru_conv_b, v_lru_w_a, v_lru_b_a, v_lru_w_x, v_lru_b_x, v_lru_lambda, v_lru_w_out, v_final_g):
    nseq, seq, _ = x.shape
    t_tok = nseq * seq
    me = 4 * lax.axis_index("x") + 2 * lax.axis_index("y") + lax.axis_index("c")
    x0 = x.reshape(t_tok, D_MODEL)
    target = loss_target.reshape(t_tok, D_MODEL)

    w_in_pad = jnp.pad(attn_w_in[0].astype(BF16), ((0, 0), (0, SHARD_W_PAD - SHARD_W_IN)))
    vec_shard = jnp.concatenate([lru_conv_w[0], lru_conv_b, lru_b_a, lru_b_x, lru_lambda], axis=0)
    g_w_in, g_vec, g_c = _exchange("gather_first", [w_in_pad, vec_shard, c], [])
    later_w = [attn_w_out[0].astype(BF16), lru_w_in[0].astype(BF16), lru_w_out[0].astype(BF16)]
    later_handle, later_token = _exchange_start("gather_later_start", later_w, [], after=g_vec)
    w_full = jnp.transpose(g_w_in[:, :, :SHARD_W_IN], (1, 0, 2)).reshape(D_MODEL, N_DEV * SHARD_W_IN)
    w_aq, w_ak, w_av = w_full[:, 0:512], w_full[:, 512:640], w_full[:, 640:768]
    w_bq, w_bk, w_bv = w_full[:, 768:1280], w_full[:, 1280:1792], w_full[:, 1792:2304]
    w_f, w_gate = w_full[:, 2304:2312], w_full[:, 2312:3336]
    w_main = jnp.concatenate([w_bq, w_bk, w_bv, w_aq, w_gate, w_ak, w_av], axis=1)
    wf_t = jnp.transpose(w_f)
    vec_full = jnp.transpose(g_vec, (1, 0, 2)).reshape(8, D_MODEL)
    conv_w, conv_b, b_a, b_x, lam = vec_full[0:4], vec_full[4:5], vec_full[5:6], vec_full[6:7], vec_full[7:8]
    c_all = g_c.reshape(N_DEV * nseq, D_MODEL)

    ncol = ada_w.shape[2]
    ada_b_slice = lax.dynamic_slice(ada_b.reshape(2, N_DEV, ncol), (0, me, 0), (2, 1, ncol))
    mod_part = _ada_mod(c_all, ada_w, ada_b_slice)
    (g_mod,) = _exchange("gather_mod", [mod_part], [])
    mine = lax.dynamic_slice(g_mod, (0, 0, me * nseq, 0), (N_DEV, 2, nseq, ncol))
    mod = jnp.transpose(mine, (1, 2, 0, 3)).reshape(2, nseq, 3 * D_MODEL)
    shift = [mod[l, :, 0:D_MODEL].reshape(nseq, 1, D_MODEL) for l in range(2)]
    scale = [mod[l, :, D_MODEL:2 * D_MODEL].reshape(nseq, 1, D_MODEL) for l in range(2)]
    gmod = [mod[l, :, 2 * D_MODEL:].reshape(nseq, 1, D_MODEL) for l in range(2)]

    onehot = _bucket_onehot()
    bias = _bias_expand(jnp.transpose(rel_bias), onehot).reshape(N_HEADS, BLOCK, 2 * BLOCK)
    sinks = attn_sinks.reshape(N_HEADS)
    b_f = attn_b_f.reshape(N_HEADS, 1)
    norm_g0 = norm_g[0:1] + later_token[0:1, 0:1]
    h0, qkvg, fl_t = _norm_proj("norm_proj0", x0, norm_g0, shift[0], scale[0], w_main, seq, BF16, wf_t=wf_t)
    f_row, f_col = _fox_prep(fl_t, b_f, seq)
    a_out, lse_a = _swa_fwd(qkvg, bias, sinks, seq)
    q_aug, k_aug, kt_aug, vt = _fox_aug(qkvg, f_col, seq)
    b_out, lse_b = _fox_fwd_t(q_aug, k_aug, vt, seq)
    g_later = _exchange_wait("gather_later_wait", later_handle, after=lse_b)
    w_out0, g_lru_in, w_out1 = (_with_own(g, w, me) for g, w in zip(g_later, later_w))
    w_out0, w_out1 = w_out0.reshape(D_MODEL, D_MODEL), w_out1.reshape(D_MODEL, D_MODEL)
    w_out0_t, w_out1_t, w_main_t = jnp.transpose(w_out0), jnp.transpose(w_out1), jnp.transpose(w_main)
    lru_in_t = jnp.transpose(g_lru_in, (0, 2, 1)).reshape(2 * D_MODEL, D_MODEL)
    yg0, y0, x1 = _out_proj("out_proj0", [a_out, b_out], qkvg, C_GATE // D_MODEL, w_out0, x0, gmod[0], seq)

    h1, proj1 = _norm_proj("norm_proj1", x1, norm_g[1:2], shift[1], scale[1], g_lru_in, seq, F32)
    hs = _lru_fwd(proj1, conv_w, conv_b, lru_w_a[0], b_a, lru_w_x[0], b_x, lam, seq)

    yg1, dx2, dy1, dyh, dgm1, loss_rows, dfinal_rows = _last_layer_tail(
        hs, proj1, w_out1, w_out1_t, x1, gmod[1], final_g.reshape(1, D_MODEL), target, seq)

    dproj1, dcw, dvec, dw_a, dw_x = _lru_bwd(proj1, hs, dyh, conv_w, conv_b, lru_w_a[0], b_a, lru_w_x[0], b_x, lam, seq)
    dx1, dss1, dg1 = _norm_bwd("norm1_bwd", [(dproj1, 0)], lru_in_t, x1, norm_g[1:2], scale[1], dx2, seq)
    (p_w_out1,) = _dw("dw_out1", yg1, [dy1])
    (p_lru_in,) = _dw("dw_lru_in", h1, [dproj1], blocked=2 * D_MODEL // N_DEV)

    dy0, dgm0, du_a, du_b, dgate = _out_proj_bwd("out_proj0_bwd", dx1, gmod[0], y0, w_out0_t, seq,
                                                  attn=(a_out, b_out, qkvg))
    (p_w_out0,) = _dw("dw_out0", yg0, [dy0])

    rows_out = D_MODEL // N_DEV
    gpack1, gmeta1 = _pack_rows([dcw[0:4], dvec[0:4], dg1[0], dfinal_rows[0]])
    dwax = jnp.stack([dw_a, dw_x]).astype(BF16)
    own1 = [gpack1, dwax, p_lru_in, p_w_out1.reshape(N_DEV, rows_out, D_MODEL),
            p_w_out0.reshape(N_DEV, rows_out, D_MODEL)]
    grads1_handle, grads1_token = _exchange_start("grads1_start", own1[:2], own1[2:], after=p_w_out0)
    sinks_after = sinks + grads1_token[0, 0]
    dq_a, dkv_a, dbias, dsink = _swa_bwd(qkvg, du_a, a_out, lse_a, bias, sinks_after, seq)
    dq_b, dk_b, dv_b, df4 = _fox_bwd_t(q_aug, k_aug, kt_aug, qkvg, du_b, b_out, lse_b, seq)
    dfl_t, db_f = _fox_post(df4.reshape(N_HEADS, t_tok), fl_t, b_f, seq)
    parts0 = [(dq_b, C_BQ), (dk_b, C_BK), (dv_b, C_BV), (dq_a, C_AQ), (dgate, C_GATE), (dkv_a, C_AK)]
    pw_bq, pw_bk, pw_bv, pw_aq, pw_gate, pw_akv = _dw("dw_attn_in", h0, [p for p, _ in parts0])
    pw_f = _dw_rows("dw_f", dfl_t, h0)

    p_w_in = jnp.concatenate([pw_aq, pw_akv, pw_bq, pw_bk, pw_bv, jnp.transpose(pw_f).astype(BF16), pw_gate], axis=1)
    p_w_in = jnp.transpose(p_w_in.reshape(D_MODEL, N_DEV, SHARD_W_IN), (1, 0, 2))
    p_w_in = jnp.pad(p_w_in, ((0, 0), (0, 0), (0, SHARD_W_PAD - SHARD_W_IN)))
    own0 = [p_w_in]
    landed1 = _exchange_wait("grads1_wait", grads1_handle, after=p_w_in)
    grads0_handle, grads0_token = _exchange_start("grads0_start", [], own0, after=landed1[0])
    scale0 = scale[0] + grads0_token[0:1, 0:1]
    dx0, dss0, dg0 = _norm_bwd("norm0_bwd", parts0, w_main_t, x0, norm_g[0:1], scale0, dx1, seq,
                               rows_part=(dfl_t, wf_t))
    dbias_t = _bias_reduce(dbias.reshape(N_HEADS, BLOCK * 2 * BLOCK), onehot)

    gpack0, gmeta0 = _pack_rows([jnp.transpose(dbias_t), dg0[0], dsink[:, 0], db_f[:, 0], loss_rows[0]])
    dmod = jnp.stack([jnp.concatenate([dss[:, 0], dss[:, 1], dgm[:, 0]], axis=1)
                      for dss, dgm in ((dss0, dgm0), (dss1, dgm1))], axis=1)
    g_small0, g_dmod = _exchange("exchange_small", [gpack0, dmod], [])
    landed0 = _exchange_wait("grads0_wait", grads0_handle, after=g_small0)
    (r_w_in,) = (_with_own(g, lax.dynamic_index_in_dim(a, me, 0, keepdims=False), me)
                 for g, a in zip(landed0, own0))
    g_small1, g_dwax = (_with_own(g, a, me) for g, a in zip(landed1[:2], own1[:2]))
    r_lru_in, r_w_out1, r_w_out0 = (_with_own(g, lax.dynamic_index_in_dim(a, me, 0, keepdims=False), me)
                                    for g, a in zip(landed1[2:], own1[2:]))

    d_rel, d_g0, d_sinks, d_b_f, loss_cols = _unpack_rows(_sum_leading("sum_small0", g_small0), gmeta0)
    loss = jnp.sum(loss_cols)
    d_cw, d_vec, d_g1, d_final_g = _unpack_rows(_sum_leading("sum_small1", g_small1), gmeta1)
    d_norm_g = jnp.stack([d_g0, d_g1])
    d_wax = _sum_leading("sum_dwax", g_dwax.reshape(N_DEV, 2 * LRU_BLOCKS * LRU_BLOCK_W, LRU_BLOCK_W))
    d_wa, d_wx = d_wax[:LRU_BLOCKS * LRU_BLOCK_W], d_wax[LRU_BLOCKS * LRU_BLOCK_W:]
    cols = lambda a: lax.dynamic_slice(a, (0, me * LRU_BLOCK_W), (a.shape[0], LRU_BLOCK_W))
    dmod_all = g_dmod.reshape(N_DEV * nseq, 2 * 3 * D_MODEL)
    d_ada_b = _sum_leading("sum_ada_b", dmod_all.reshape(N_DEV * nseq, 2 * 3 * D_MODEL // 128, 128)).reshape(2, 3 * D_MODEL)
    dmod_slice = lax.dynamic_slice(dmod_all.reshape(N_DEV * nseq, 2, N_DEV, ncol), (0, 0, me, 0),
                                   (N_DEV * nseq, 2, 1, ncol)).reshape(N_DEV * nseq, 2, ncol)
    d_ada_w = _ada_w_grad(c_all, jnp.transpose(dmod_slice, (1, 0, 2)))

    given = dict(
        rel_bias=(rel_bias, m_rel_bias, v_rel_bias), norm_g=(norm_g, m_norm_g, v_norm_g),
        ada_w=(ada_w, m_ada_w, v_ada_w), ada_b=(ada_b, m_ada_b, v_ada_b),
        attn_w_in=(attn_w_in, m_attn_w_in, v_attn_w_in), attn_sinks=(attn_sinks, m_attn_sinks, v_attn_sinks),
        attn_b_f=(attn_b_f, m_attn_b_f, v_attn_b_f), attn_w_out=(attn_w_out, m_attn_w_out, v_attn_w_out),
        lru_w_in=(lru_w_in, m_lru_w_in, v_lru_w_in), lru_conv_w=(lru_conv_w, m_lru_conv_w, v_lru_conv_w),
        lru_conv_b=(lru_conv_b, m_lru_conv_b, v_lru_conv_b), lru_w_a=(lru_w_a, m_lru_w_a, v_lru_w_a),
        lru_b_a=(lru_b_a, m_lru_b_a, v_lru_b_a), lru_w_x=(lru_w_x, m_lru_w_x, v_lru_w_x),
        lru_b_x=(lru_b_x, m_lru_b_x, v_lru_b_x), lru_lambda=(lru_lambda, m_lru_lambda, v_lru_lambda),
        lru_w_out=(lru_w_out, m_lru_w_out, v_lru_w_out), final_g=(final_g, m_final_g, v_final_g))
    results = {}

    def big(name, shape2d, g=None, parts=None):
        w, m, v = (a.reshape(shape2d) for a in given[name])
        outs = _adamw("adamw_" + name, w, m, v, g=g, parts=parts)
        results[name] = tuple(o.reshape(given[name][0].shape) for o in outs)

    big("ada_w", (2 * D_MODEL, ncol), g=d_ada_w.reshape(2 * D_MODEL, ncol))
    big("attn_w_in", (D_MODEL, SHARD_W_IN), parts=r_w_in)
    big("attn_w_out", (rows_out, D_MODEL), parts=r_w_out0)
    big("lru_w_in", (D_MODEL, 2 * D_MODEL // N_DEV), parts=r_lru_in)
    big("lru_w_out", (rows_out, D_MODEL), parts=r_w_out1)

    small_grads = dict(
        rel_bias=d_rel, norm_g=d_norm_g, ada_b=d_ada_b, attn_sinks=d_sinks.reshape(1, N_HEADS),
        attn_b_f=d_b_f.reshape(1, N_HEADS), lru_conv_w=cols(d_cw).reshape(1, 4, LRU_BLOCK_W),
        lru_conv_b=cols(d_vec[0:1]), lru_w_a=d_wa.reshape(lru_w_a.shape), lru_b_a=cols(d_vec[1:2]),
        lru_w_x=d_wx.reshape(lru_w_x.shape), lru_b_x=cols(d_vec[2:3]), lru_lambda=cols(d_vec[3:4]),
        final_g=d_final_g)
    small = [n for n in WEIGHTS if n not in BIG]
    as2d = lambda a: a.reshape(-1, a.shape[-1])
    outs = _adamw_many("adamw_small", [tuple(as2d(a) for a in given[n]) + (as2d(small_grads[n]),) for n in small])
    for n, group in zip(small, outs):
        results[n] = tuple(o.reshape(given[n][0].shape) for o in group)

    grad_x = dx0.reshape(x.shape)
    out = [loss, grad_x]
    for j in range(4):
        out += [results[n][j] for n in WEIGHTS]
    return tuple(out)
```

```python
import functools
import math

import jax
import jax.numpy as jnp
from jax import lax
from jax.experimental import pallas as pl
from jax.experimental.pallas import tpu as pltpu

F32 = jnp.float32
BF16 = jnp.bfloat16
HI = lax.Precision.HIGHEST
MESH = pl.DeviceIdType.MESH

N_DEV = 8
D_MODEL = 1024
HEAD_DIM = 64
N_HEADS = 8
KV_GROUP = 4
BLOCK = 128
REL_BUCKETS = 32
REL_MAX_EXACT = 16
REL_MAX_DIST = 128
LRU_BLOCKS = 8
LRU_BLOCK_W = 128
LRU_C = 8.0
EPS = 1e-6
SCALE = HEAD_DIM ** -0.5
NEG = -1e30

ADAM_LR = 0.001
ADAM_B1 = 0.9
ADAM_B2 = 0.999
ADAM_EPS = 1e-08
ADAM_WD = 0.01
ADAM_STEP = 10

C_BQ, C_BK, C_BV, C_AQ, C_GATE, C_AK, C_AV = 0, 512, 1024, 1536, 2048, 3072, 3200
N_MAIN = 3328
SHARD_W_IN = 417
SHARD_W_PAD = 512

TM = 512
TQ = 256
TK = 128
TKB = 256
TC = 512
SWA_SUB = 2
VMEM_BIG = 56 * 1024 * 1024
VMEM_MID = 40 * 1024 * 1024


def _pallas(body, **kw):
    return pl.pallas_call(body, **kw)


def _cp(sem=None, vmem=None):
    kw = {}
    if sem is not None:
        kw["dimension_semantics"] = sem
    if vmem is not None:
        kw["vmem_limit_bytes"] = vmem
    return pltpu.CompilerParams(**kw)


def _nn(a, b, precision=None):
    return jnp.dot(a, b, preferred_element_type=F32, precision=precision)


def _nt(a, b, precision=None):
    return lax.dot_general(a, b, (((1,), (1,)), ((), ())), preferred_element_type=F32, precision=precision)


def _tn(a, b, precision=None):
    return lax.dot_general(a, b, (((0,), (0,)), ((), ())), preferred_element_type=F32, precision=precision)


def _sigmoid(x):
    return 1.0 / (1.0 + jnp.exp(-x))


def _silu(x):
    return x * _sigmoid(x)


def _dsilu(x):
    s = _sigmoid(x)
    return s * (1.0 + x * (1.0 - s))


def _col(tile, idx):
    lane = lax.broadcasted_iota(jnp.int32, tile.shape, 1)
    return jnp.sum(jnp.where(lane == idx, tile, 0.0), axis=1, keepdims=True)


def _exchange(name, gathers, scatters, axes=("x", "y", "c"), chunks=1):
    ng, n = len(gathers), len(gathers) + len(scatters)
    ins = list(gathers) + list(scatters)
    group = 2 ** len(axes)

    def body(*refs):
        in_refs, out_refs = refs[:n], refs[n:2 * n]
        send_sems, recv_sems, loc_sems = refs[2 * n:]
        coord = {a: lax.axis_index(a) for a in ("x", "y", "c")}

        def member(r):
            pc = dict(coord)
            idx = 0
            for k, a in enumerate(axes):
                if r & (1 << (len(axes) - 1 - k)):
                    pc[a] = 1 - coord[a]
                idx = 2 * idx + pc[a]
            return (pc["x"], pc["y"], pc["c"]), idx

        _, me = member(0)

        def peer(r):
            return member(r)

        local, sends, recvs = [], [], []
        for k in range(n):
            mine = in_refs[k] if k < ng else in_refs[k].at[me]
            cp = pltpu.make_async_copy(mine, out_refs[k].at[me], loc_sems.at[k])
            cp.start()
            local.append(cp)
            lead = mine.shape[0]
            nchunk = max(q for q in range(1, chunks + 1) if lead % q == 0)
            step = lead // nchunk
            for r in range(1, group):
                pid, pidx = peer(r)
                src = in_refs[k] if k < ng else in_refs[k].at[pidx]
                for q in range(nchunk):
                    rows = pl.ds(q * step, step)
                    sems = dict(send_sem=send_sems.at[r - 1, k, q], recv_sem=recv_sems.at[r - 1, k, q],
                                device_id=pid, device_id_type=MESH)
                    snd = pltpu.make_async_remote_copy(src_ref=src.at[rows], dst_ref=out_refs[k].at[me].at[rows], **sems)
                    snd.start()
                    sends.append(snd)
                    recvs.append(pltpu.make_async_remote_copy(
                        src_ref=src.at[rows], dst_ref=out_refs[k].at[pidx].at[rows], **sems))
        for rc in recvs:
            rc.wait_recv()
        for snd in sends:
            snd.wait_send()
        for cp in local:
            cp.wait()

    out_shape = [jax.ShapeDtypeStruct((group,) + a.shape, a.dtype) for a in gathers]
    out_shape += [jax.ShapeDtypeStruct(a.shape, a.dtype) for a in scatters]
    any_spec = pl.BlockSpec(memory_space=pl.ANY)
    return _pallas(
        body, name=name, out_shape=out_shape,
        in_specs=[any_spec] * n, out_specs=[any_spec] * n,
        scratch_shapes=[pltpu.SemaphoreType.DMA((group - 1, n, chunks)), pltpu.SemaphoreType.DMA((group - 1, n, chunks)),
                        pltpu.SemaphoreType.DMA((n,))],
    )(*ins)


def _peer_of(r):
    x, y, c = lax.axis_index("x"), lax.axis_index("y"), lax.axis_index("c")
    px = 1 - x if r & 4 else x
    py = 1 - y if r & 2 else y
    pc = 1 - c if r & 1 else c
    return (px, py, pc), 4 * px + 2 * py + pc


def _split_copies(in_refs, land_refs, send_sems, recv_sems, ng, with_recv):
    _, me = _peer_of(0)
    pairs = []
    for k, (src_ref, land) in enumerate(zip(in_refs, land_refs)):
        for r in range(1, N_DEV):
            pid, pidx = _peer_of(r)
            src = src_ref if k < ng else src_ref.at[pidx]
            slot = (N_DEV - 1) * k + r - 1
            sems = dict(send_sem=send_sems.at[slot], recv_sem=recv_sems.at[slot], device_id=pid, device_id_type=MESH)
            send = pltpu.make_async_remote_copy(src_ref=src, dst_ref=land.at[me], **sems)
            recv = pltpu.make_async_remote_copy(src_ref=src, dst_ref=land.at[pidx], **sems) if with_recv else None
            pairs.append((send, recv))
    return pairs


def _exchange_start(name, gathers, scatters, after):
    ng, n = len(gathers), len(gathers) + len(scatters)
    ins = list(gathers) + list(scatters)
    lands = [jax.ShapeDtypeStruct((N_DEV,) + a.shape, a.dtype) for a in gathers]
    lands += [jax.ShapeDtypeStruct(a.shape, a.dtype) for a in scatters]

    def body(*refs):
        in_refs, land_refs = refs[:n], refs[n:2 * n]
        send_sems, recv_sems = refs[2 * n + 1:2 * n + 3]
        token = refs[-1]
        for send, _ in _split_copies(in_refs, land_refs, send_sems, recv_sems, ng, False):
            send.start()
        token[...] = jnp.zeros_like(token)

    hbm = pl.BlockSpec(memory_space=pltpu.HBM)
    sem = pl.BlockSpec(memory_space=pltpu.SEMAPHORE)
    sem_shape = pltpu.SemaphoreType.DMA(((N_DEV - 1) * n,))
    out_shape = [sem_shape, sem_shape] + [pltpu.HBM(a.shape, a.dtype) for a in ins]
    out_shape += [pltpu.HBM(l.shape, l.dtype) for l in lands] + [jax.ShapeDtypeStruct((8, 128), F32)]
    args = [pltpu.with_memory_space_constraint(a, pltpu.HBM) for a in ins]
    args += [pltpu.with_memory_space_constraint(lax.empty(l.shape, l.dtype), pltpu.HBM) for l in lands]
    outs = _pallas(
        body, name=name, out_shape=out_shape,
        in_specs=[hbm] * (2 * n) + [pl.BlockSpec(memory_space=pl.ANY)],
        out_specs=[sem, sem] + [hbm] * (2 * n) + [pl.BlockSpec(memory_space=pltpu.VMEM)],
        input_output_aliases={i: 2 + i for i in range(2 * n)},
        compiler_params=pltpu.CompilerParams(has_side_effects=pltpu.SideEffectType.DATAFLOW_SIDE_EFFECTING),
    )(*args, after)
    return (outs[0], outs[1], list(outs[2:2 + n]), list(outs[2 + n:2 + 2 * n]), ng), outs[-1]


def _exchange_wait(name, handle, after):
    send_sems, recv_sems, srcs, lands, ng = handle
    n = len(srcs)

    def body(*refs):
        in_refs, land_refs = refs[:n], refs[n:2 * n]
        send_ref, recv_ref = refs[2 * n:2 * n + 2]
        for send, recv in _split_copies(in_refs, land_refs, send_ref, recv_ref, ng, True):
            send.wait_send()
            recv.wait_recv()

    hbm = pl.BlockSpec(memory_space=pltpu.HBM)
    sem = pl.BlockSpec(memory_space=pltpu.SEMAPHORE)
    outs = _pallas(
        body, name=name, out_shape=[pltpu.HBM(a.shape, a.dtype) for a in srcs + lands],
        in_specs=[hbm] * (2 * n) + [sem, sem, pl.BlockSpec(memory_space=pl.ANY)],
        out_specs=[hbm] * (2 * n), input_output_aliases={i: i for i in range(2 * n)},
        compiler_params=pltpu.CompilerParams(has_side_effects=pltpu.SideEffectType.DATAFLOW_SIDE_EFFECTING),
    )(*srcs, *lands, send_sems, recv_sems, after)
    return list(outs[n:])


def _with_own(land, own, me):
    return lax.dynamic_update_slice(land, own[None], (me,) + (0,) * own.ndim)


def _ada_mod(c_all, ada_w, ada_b_slice):
    def body(c_ref, w_ref, b_ref, o_ref):
        ca = _silu(c_ref[...])
        for l in range(2):
            o_ref[l] = _nn(ca, w_ref[l], HI) + b_ref[l]

    return _pallas(body, name="ada_mod",
                   out_shape=jax.ShapeDtypeStruct((2, c_all.shape[0], ada_w.shape[2]), F32),
                   compiler_params=_cp(vmem=VMEM_MID))(c_all, ada_w, ada_b_slice)


def _ada_w_grad(c_all, dmod_slice):
    def body(c_ref, d_ref, o_ref):
        ca = _silu(c_ref[...])
        for l in range(2):
            o_ref[l] = _tn(ca, d_ref[l], HI)

    return _pallas(body, name="ada_w_grad",
                   out_shape=jax.ShapeDtypeStruct((2, D_MODEL, dmod_slice.shape[2]), F32),
                   compiler_params=_cp(vmem=VMEM_MID))(c_all, dmod_slice)


def _bucket_onehot():
    qi = jnp.arange(BLOCK)[:, None]
    kj = jnp.arange(2 * BLOCK)[None, :]
    rel = qi - kj + BLOCK
    n = jnp.maximum(rel, 0)
    nf = jnp.maximum(n, 1).astype(F32)
    large = REL_MAX_EXACT + (jnp.log(nf / REL_MAX_EXACT) / math.log(REL_MAX_DIST / REL_MAX_EXACT)
                             * (REL_BUCKETS - REL_MAX_EXACT)).astype(jnp.int32)
    large = jnp.minimum(large, REL_BUCKETS - 1)
    bucket = jnp.where(n < REL_MAX_EXACT, n, large).reshape(1, BLOCK * 2 * BLOCK)
    return (jnp.arange(REL_BUCKETS)[:, None] == bucket).astype(F32)


def _bias_expand(rel_bias_t, onehot):
    def body(r_ref, e_ref, o_ref):
        o_ref[...] = _nn(r_ref[...], e_ref[...], HI)

    return _pallas(body, name="bias_expand",
                   out_shape=jax.ShapeDtypeStruct((N_HEADS, onehot.shape[1]), F32),
                   compiler_params=_cp(vmem=VMEM_MID))(rel_bias_t, onehot)


def _bias_reduce(dbias, onehot):
    def body(d_ref, e_ref, o_ref):
        o_ref[...] = _nt(d_ref[...], e_ref[...], HI)

    return _pallas(body, name="bias_reduce",
                   out_shape=jax.ShapeDtypeStruct((N_HEADS, REL_BUCKETS), F32),
                   compiler_params=_cp(vmem=VMEM_MID))(dbias, onehot)


def _norm_proj(name, x, g, shift, scale, w, seq, out_dtype, wf_t=None):
    t_tok = x.shape[0]
    w3d = w.ndim == 3
    n_out = w.shape[0] * w.shape[2] if w3d else w.shape[1]
    cn = w.shape[2] if w3d else 256

    def body(x_ref, g_ref, sh_ref, sc_ref, w_ref, *rest):
        if wf_t is not None:
            wf_ref, h_ref, o_ref, fl_ref = rest
        else:
            h_ref, o_ref = rest
        xv = x_ref[...]
        rstd = lax.rsqrt(jnp.mean(xv * xv, axis=-1, keepdims=True) + EPS)
        h = (xv * rstd) * g_ref[...] * (1.0 + sc_ref[...]) + sh_ref[...]
        hb = h.astype(BF16)
        h_ref[...] = hb
        for j in range(n_out // cn):
            wj = w_ref[j] if w3d else w_ref[:, j * cn:(j + 1) * cn]
            o_ref[:, j * cn:(j + 1) * cn] = _nn(hb, wj).astype(out_dtype)
        if wf_t is not None:
            fl_ref[...] = _nt(wf_ref[...], hb)

    mod_spec = pl.BlockSpec((None, 1, D_MODEL), lambda i: (i * TM // seq, 0, 0))
    w_spec = (pl.BlockSpec(w.shape, lambda i: (0, 0, 0)) if w3d else pl.BlockSpec(w.shape, lambda i: (0, 0)))
    in_specs = [pl.BlockSpec((TM, D_MODEL), lambda i: (i, 0)), pl.BlockSpec((1, D_MODEL), lambda i: (0, 0)),
                mod_spec, mod_spec, w_spec]
    out_shape = [jax.ShapeDtypeStruct((t_tok, D_MODEL), BF16), jax.ShapeDtypeStruct((t_tok, n_out), out_dtype)]
    out_specs = [pl.BlockSpec((TM, D_MODEL), lambda i: (i, 0)), pl.BlockSpec((TM, n_out), lambda i: (i, 0))]
    args = [x, g, shift, scale, w]
    if wf_t is not None:
        in_specs.append(pl.BlockSpec(wf_t.shape, lambda i: (0, 0)))
        out_shape.append(jax.ShapeDtypeStruct((wf_t.shape[0], t_tok), F32))
        out_specs.append(pl.BlockSpec((wf_t.shape[0], TM), lambda i: (0, i)))
        args.append(wf_t)
    return _pallas(body, name=name, grid=(t_tok // TM,), in_specs=in_specs, out_specs=out_specs,
                   out_shape=out_shape, compiler_params=_cp(("arbitrary",), VMEM_BIG))(*args)


def _fox_prep(fl_t, b_f, seq):
    t_tok = fl_t.shape[1]
    ch = 256

    def body(fl_ref, bf_ref, fr_ref, fc_ref):
        z = fl_ref[...] + bf_ref[...]
        logf = jnp.minimum(z, 0.0) - jnp.log(1.0 + jnp.exp(-jnp.abs(z)))
        ri = lax.broadcasted_iota(jnp.int32, (ch, ch), 0)
        ci = lax.broadcasted_iota(jnp.int32, (ch, ch), 1)
        upper = (ri <= ci).astype(F32)
        eye = (ri == ci).astype(F32)
        carry = jnp.zeros((N_HEADS, 1), F32)
        for k in range(seq // ch):
            fk = _nn(logf[:, k * ch:(k + 1) * ch], upper, HI) + carry
            carry = fk[:, ch - 1:ch]
            fr_ref[:, k * ch:(k + 1) * ch] = fk
            padded = jnp.concatenate([fk, jnp.zeros((128 - N_HEADS, ch), F32)], axis=0)
            fc_ref[k * ch:(k + 1) * ch, :] = _nt(eye, padded, HI)

    return _pallas(
        body, name="fox_prep", grid=(t_tok // seq,),
        in_specs=[pl.BlockSpec((N_HEADS, seq), lambda b: (0, b)), pl.BlockSpec((N_HEADS, 1), lambda b: (0, 0))],
        out_specs=[pl.BlockSpec((N_HEADS, seq), lambda b: (0, b)), pl.BlockSpec((seq, 128), lambda b: (b, 0))],
        out_shape=[jax.ShapeDtypeStruct((N_HEADS, t_tok), F32), jax.ShapeDtypeStruct((t_tok, 128), F32)],
        compiler_params=_cp(("arbitrary",), VMEM_MID))(fl_t, b_f)


def _fox_post(df_row, fl_t, b_f, seq):
    t_tok = fl_t.shape[1]
    ch = 256

    def body(d_ref, fl_ref, bf_ref, o_ref, db_ref):
        @pl.when(pl.program_id(0) == 0)
        def _():
            db_ref[...] = jnp.zeros_like(db_ref)

        z = fl_ref[...] + bf_ref[...]
        sig_neg = 1.0 / (1.0 + jnp.exp(z))
        ri = lax.broadcasted_iota(jnp.int32, (ch, ch), 0)
        ci = lax.broadcasted_iota(jnp.int32, (ch, ch), 1)
        lower = (ri >= ci).astype(F32)
        carry = jnp.zeros((N_HEADS, 1), F32)
        tot = jnp.zeros((N_HEADS, 1), F32)
        for k in reversed(range(seq // ch)):
            dk = _nn(d_ref[:, k * ch:(k + 1) * ch], lower, HI) + carry
            carry = dk[:, 0:1]
            dfl = dk * sig_neg[:, k * ch:(k + 1) * ch]
            o_ref[:, k * ch:(k + 1) * ch] = dfl
            tot = tot + jnp.sum(dfl, axis=1, keepdims=True)
        db_ref[...] += jnp.broadcast_to(tot, db_ref.shape)

    return _pallas(
        body, name="fox_post", grid=(t_tok // seq,),
        in_specs=[pl.BlockSpec((N_HEADS, seq), lambda b: (0, b)), pl.BlockSpec((N_HEADS, seq), lambda b: (0, b)),
                  pl.BlockSpec((N_HEADS, 1), lambda b: (0, 0))],
        out_specs=[pl.BlockSpec((N_HEADS, seq), lambda b: (0, b)), pl.BlockSpec((N_HEADS, 128), lambda b: (0, 0))],
        out_shape=[jax.ShapeDtypeStruct((N_HEADS, t_tok), F32), jax.ShapeDtypeStruct((N_HEADS, 128), F32)],
        compiler_params=_cp(("arbitrary",), VMEM_MID))(df_row, fl_t, b_f)


def _eye(n, dtype):
    return (lax.broadcasted_iota(jnp.int32, (n, n), 0) == lax.broadcasted_iota(jnp.int32, (n, n), 1)).astype(dtype)


def _fox_aug(qkvg, f_col, seq):
    t_tok = qkvg.shape[0]
    ta = 256
    nkb = ta // TK

    def body(q_ref, k_ref, v_ref, fc_ref, qa_ref, ka_ref, kt_ref, vt_ref):
        ri = lax.broadcasted_iota(jnp.int32, (128, 128), 0)
        ci = lax.broadcasted_iota(jnp.int32, (128, 128), 1)
        eye = (ri == ci).astype(BF16)
        lane = lax.broadcasted_iota(jnp.int32, (ta, 128), 1)
        ones_q = jnp.where(jnp.logical_and(lane >= 64, lane < 67), 1.0, 0.0)
        ones_k = jnp.where(jnp.logical_and(lane >= 67, lane < 70), 1.0, 0.0)
        fc_tile = fc_ref[...]
        for p in range(N_HEADS // 2):
            q2 = q_ref[:, 128 * p:128 * (p + 1)]
            k2 = k_ref[:, 128 * p:128 * (p + 1)]
            vt = _nt(eye, v_ref[:, 128 * p:128 * (p + 1)]).astype(BF16)
            for kk in range(nkb):
                vt_ref[p, kk] = vt[:, kk * TK:(kk + 1) * TK]
            for e in range(2):
                h = 2 * p + e
                sel = jnp.logical_and(ri == ci + HEAD_DIM * e, ci < HEAD_DIM)
                f = _col(fc_tile, h)
                fh = f.astype(BF16).astype(F32)
                fm = (f - fh).astype(BF16).astype(F32)
                fl = (f - fh - fm).astype(BF16).astype(F32)
                qa = (_nn(q2, jnp.where(sel, SCALE, 0.0).astype(BF16)) + ones_q + jnp.where(lane == 67, fh, 0.0)
                      + jnp.where(lane == 68, fm, 0.0) + jnp.where(lane == 69, fl, 0.0))
                ka = (_nn(k2, jnp.where(sel, 1.0, 0.0).astype(BF16)) + ones_k - jnp.where(lane == 64, fh, 0.0)
                      - jnp.where(lane == 65, fm, 0.0) - jnp.where(lane == 66, fl, 0.0))
                qa_ref[h] = qa.astype(BF16)
                kab = ka.astype(BF16)
                ka_ref[h] = kab
                kt = _nt(eye, kab).astype(BF16)
                for kk in range(ta // TKB):
                    kt_ref[h, kk] = kt[:, kk * TKB:(kk + 1) * TKB]

    aug = jax.ShapeDtypeStruct((N_HEADS, t_tok, 128), BF16)
    return _pallas(
        body, name="fox_aug", grid=(t_tok // ta,),
        in_specs=[pl.BlockSpec((ta, 512), lambda i: (i, C_BQ // 512)), pl.BlockSpec((ta, 512), lambda i: (i, C_BK // 512)),
                  pl.BlockSpec((ta, 512), lambda i: (i, C_BV // 512)), pl.BlockSpec((ta, 128), lambda i: (i, 0))],
        out_specs=[pl.BlockSpec((N_HEADS, ta, 128), lambda i: (0, i, 0)), pl.BlockSpec((N_HEADS, ta, 128), lambda i: (0, i, 0)),
                   pl.BlockSpec((N_HEADS, ta // TKB, 128, TKB), lambda i: (0, i, 0, 0)),
                   pl.BlockSpec((N_HEADS // 2, nkb, 128, TK), lambda i: (0, i, 0, 0))],
        out_shape=[aug, aug, jax.ShapeDtypeStruct((N_HEADS, t_tok // TKB, 128, TKB), BF16),
                   jax.ShapeDtypeStruct((N_HEADS // 2, t_tok // TK, 128, TK), BF16)],
        compiler_params=_cp(("arbitrary",), VMEM_MID))(qkvg, qkvg, qkvg, f_col)


def _fox_fwd_t(q_aug, k_aug, vt, seq):
    t_tok = k_aug.shape[1]
    nq = seq // TQ
    ratio = TQ // TK
    assert ratio == 2, "the two pipeline slots are addressed by the key block's parity"

    def body(qa_ref, ka_ref, vt_ref, o_ref, lse_ref, ml_s, acc_s, st_s, p_s, al_s, qt_s):
        i = pl.program_id(1)
        tpos = i * TQ + lax.broadcasted_iota(jnp.int32, (1, TQ), 1)
        eye = _eye(HEAD_DIM, BF16)
        eye2 = _eye(128, BF16)
        for h in range(N_HEADS):
            qt_s[h] = _nt(eye2, qa_ref[h]).astype(BF16)
            ml_s[0, h] = jnp.full((1, TQ), NEG, F32)
            ml_s[1, h] = jnp.zeros((1, TQ), F32)
            acc_s[h] = jnp.zeros((HEAD_DIM, TQ), F32)
            p_s[1, h] = jnp.zeros((TK, TQ), BF16)
            al_s[1, h] = jnp.ones((1, TQ), F32)

        def scores(j, slot):
            row0 = pl.multiple_of(j * TK, TK)
            for h in range(N_HEADS):
                st_s[slot, h] = _nn(ka_ref[h, pl.ds(row0, TK), :], qt_s[h])

        def softmax(j, slot, masked):
            if masked:
                keep = (j * TK + lax.broadcasted_iota(jnp.int32, (TK, 1), 0)) <= tpos
            for h in range(N_HEADS):
                st = st_s[slot, h]
                if masked:
                    st = jnp.where(keep, st, NEG)
                m = ml_s[0, h]
                m_new = jnp.maximum(m, jnp.max(st, axis=0, keepdims=True))
                alpha = jnp.exp(m - m_new)
                pe = jnp.exp(st - m_new)
                ml_s[0, h] = m_new
                ml_s[1, h] = alpha * ml_s[1, h] + jnp.sum(pe, axis=0, keepdims=True)
                al_s[slot, h] = alpha
                p_s[slot, h] = pe.astype(BF16)

        def values(j, slot):
            jv = jnp.maximum(j, 0)
            for h in range(N_HEADS):
                p, e = divmod(h, 2)
                acc_s[h] = al_s[slot, h] * acc_s[h] + _nn(vt_ref[p, jv, e * HEAD_DIM:(e + 1) * HEAD_DIM, :], p_s[slot, h])

        def step(m, carry):
            for kk in range(ratio):
                j = ratio * m + kk
                values(j - 1, 1 - kk)
                softmax(j, kk, False)
                scores(j + 1, 1 - kk)
            return carry

        scores(0, 0)
        lax.fori_loop(0, i, step, 0)
        for kk in range(ratio):
            j = ratio * i + kk
            values(j - 1, 1 - kk)
            softmax(j, kk, True)
            if kk < ratio - 1:
                scores(j + 1, 1 - kk)
        values(ratio * i + ratio - 1, ratio - 1)
        for p in range(N_HEADS // 2):
            outs = []
            for e in range(2):
                h = 2 * p + e
                l = ml_s[1, h]
                outs.append(_tn((acc_s[h] / l).astype(BF16), eye))
                lse_ref[p, e:e + 1, :] = ml_s[0, h] + jnp.log(l)
            o_ref[:, 128 * p:128 * (p + 1)] = jnp.concatenate(outs, axis=1).astype(BF16)

    return _pallas(
        body, name="fox_fwd", grid=(t_tok // seq, nq),
        in_specs=[pl.BlockSpec((N_HEADS, TQ, 128), lambda b, i: (0, b * nq + i, 0)),
                  pl.BlockSpec((N_HEADS, seq, 128), lambda b, i: (0, b, 0)),
                  pl.BlockSpec((N_HEADS // 2, seq // TK, 128, TK), lambda b, i: (0, b, 0, 0))],
        out_specs=[pl.BlockSpec((TQ, 512), lambda b, i: (b * nq + i, 0)),
                   pl.BlockSpec((N_HEADS // 2, 2, TQ), lambda b, i: (0, 0, b * nq + i))],
        out_shape=[jax.ShapeDtypeStruct((t_tok, 512), BF16), jax.ShapeDtypeStruct((N_HEADS // 2, 2, t_tok), F32)],
        scratch_shapes=[pltpu.VMEM((2, N_HEADS, 1, TQ), F32), pltpu.VMEM((N_HEADS, HEAD_DIM, TQ), F32),
                        pltpu.VMEM((2, N_HEADS, TK, TQ), F32), pltpu.VMEM((2, N_HEADS, TK, TQ), BF16),
                        pltpu.VMEM((2, N_HEADS, 1, TQ), F32), pltpu.VMEM((N_HEADS, 128, TQ), BF16)],
        compiler_params=_cp(("arbitrary", "arbitrary"), VMEM_MID))(q_aug, k_aug, vt)


def _fox_bwd_t(q_aug, k_aug, kt, qkvg, du_b, b_out, lse, seq):
    TK = TKB
    t_tok = qkvg.shape[0]
    nq = seq // TQ
    nkb = seq // TK
    ratio = TQ // TK
    hg = 4

    def body(qa_ref, ka_ref, kt_ref, v_ref, do_ref, o_ref, lse_ref, dq_ref, dk_ref, dv_ref, df_ref,
             dqt_s, row_s, dfk_s, dk_s, dv_s, dot_s, st_s, dp_s, pb_s, db_s, qt_s):
        eye = _eye(HEAD_DIM, BF16)
        eye2 = _eye(128, BF16)
        eye_k = _eye(TK, F32)
        lane8 = lax.broadcasted_iota(jnp.int32, (8, 128), 1)
        lane_k = lax.broadcasted_iota(jnp.int32, (TK, 128), 1)
        first = [lane8 < HEAD_DIM, lane8 >= HEAD_DIM]
        for pp in range(hg // 2):
            for ii in range(nq):
                dot_s[pp, ii] = _nt(eye2, do_ref[ii * TQ:(ii + 1) * TQ, 128 * pp:128 * (pp + 1)]).astype(BF16)
        for hh in range(hg):
            for ii in range(nq):
                qt_s[hh, ii] = _nt(eye2, qa_ref[hh, ii * TQ:(ii + 1) * TQ, :]).astype(BF16)
        for hh in range(hg):
            pp, e = divmod(hh, 2)
            head_lanes = jnp.where(first[e], 1.0, 0.0)
            for ii in range(nq):
                rows = slice(ii * TQ, (ii + 1) * TQ)
                prod = do_ref[rows, 128 * pp:128 * (pp + 1)].astype(F32) * o_ref[rows, 128 * pp:128 * (pp + 1)].astype(F32)
                row_s[hh, ii, 0] = _nt(head_lanes, prod, HI)
                row_s[hh, ii, 1] = jnp.broadcast_to(lse_ref[pp, e:e + 1, ii * TQ:(ii + 1) * TQ], (8, TQ))
                dqt_s[hh, ii] = jnp.zeros((128, TQ), F32)

        def kblock(j, _):
            krow = pl.multiple_of(j * TK, TK)
            spos = j * TK + lax.broadcasted_iota(jnp.int32, (TK, 1), 0)
            for hh in range(hg):
                dk_s[hh] = jnp.zeros((TK, 128), F32)
                dv_s[hh] = jnp.zeros((TK, 128), F32)

            def scores(i, slot):
                for hh in range(hg):
                    pp, e = divmod(hh, 2)
                    own = (lane_k < HEAD_DIM) if e == 0 else (lane_k >= HEAD_DIM)
                    v2 = v_ref[pl.ds(krow, TK), 128 * pp:128 * (pp + 1)]
                    vj = jnp.where(own, v2, jnp.zeros_like(v2))
                    st_s[slot, hh] = _nn(ka_ref[hh, pl.ds(krow, TK), :], qt_s[hh, i])
                    dp_s[slot, hh] = _nn(vj, dot_s[pp, i])

            def elementwise(i, slot, masked):
                if masked:
                    keep = spos <= (i * TQ + lax.broadcasted_iota(jnp.int32, (1, TQ), 1))
                for hh in range(hg):
                    pt = jnp.exp(st_s[slot, hh] - row_s[hh, i, 1][0:1, :])
                    if masked:
                        pt = jnp.where(keep, pt, 0.0)
                    dst = pt * (dp_s[slot, hh] - row_s[hh, i, 0][0:1, :])
                    pb_s[slot, hh] = pt.astype(BF16)
                    db_s[slot, hh] = dst.astype(BF16)

            def grads(i, slot):
                qrow = pl.multiple_of(i * TQ, TQ)
                for hh in range(hg):
                    dst_b = db_s[slot, hh]
                    dv_s[hh] += _nn(pb_s[slot, hh], do_ref[pl.ds(qrow, TQ), 128 * (hh // 2):128 * (hh // 2 + 1)])
                    dk_s[hh] += _nn(dst_b, qa_ref[hh, pl.ds(qrow, TQ), :])
                    dqt_s[hh, i] += _nn(kt_ref[hh, j], dst_b)

            def step(p, carry):
                i = i0 + 2 * p + 1
                grads(i - 1, 0)
                elementwise(i, 1, False)
                scores(i + 1, 0)
                grads(i, 1)
                elementwise(i + 1, 0, False)
                scores(jnp.minimum(i + 2, nq - 1), 1)
                return carry

            i0 = j // ratio
            rest = nq - 1 - i0
            scores(i0, 0)
            elementwise(i0, 0, True)
            scores(jnp.minimum(i0 + 1, nq - 1), 1)
            lax.fori_loop(0, rest // 2, step, 0)

            @pl.when(rest % 2 == 1)
            def _():
                grads(nq - 2, 0)
                elementwise(nq - 1, 1, False)
                grads(nq - 1, 1)

            @pl.when(rest % 2 == 0)
            def _():
                grads(nq - 1, 0)
            for pp in range(hg // 2):
                cols = slice(128 * pp, 128 * (pp + 1))
                dk_ref[pl.ds(krow, TK), cols] = jnp.concatenate(
                    [dk_s[2 * pp][:, :HEAD_DIM], dk_s[2 * pp + 1][:, :HEAD_DIM]], axis=1).astype(BF16)
                dv_ref[pl.ds(krow, TK), cols] = jnp.where(lane_k < HEAD_DIM, dv_s[2 * pp], dv_s[2 * pp + 1]).astype(BF16)
            for hh in range(hg):
                dfk_s[hh, j] = _tn(dk_s[hh][:, HEAD_DIM:HEAD_DIM + 8], eye_k, HI)
            return 0

        lax.fori_loop(0, nkb, kblock, 0)
        for pp in range(hg // 2):
            for ii in range(nq):
                parts = []
                for e in range(2):
                    dqt = dqt_s[2 * pp + e, ii]
                    parts.append(_tn(dqt[0:HEAD_DIM, :].astype(BF16), eye) * SCALE)
                    for kk in range(ratio):
                        jj = ii * ratio + kk
                        df_ref[pp, e:e + 1, jj * TK:(jj + 1) * TK] = (dqt[67:68, kk * TK:(kk + 1) * TK]
                                                                     - dfk_s[2 * pp + e, jj][0:1, :])
                dq_ref[ii * TQ:(ii + 1) * TQ, 128 * pp:128 * (pp + 1)] = jnp.concatenate(parts, axis=1).astype(BF16)

    aug_blk = pl.BlockSpec((hg, seq, 128), lambda b, g: (g, b, 0))
    pair_blk = pl.BlockSpec((seq, 64 * hg), lambda b, g: (b, g))
    row_blk = pl.BlockSpec((hg // 2, 2, seq), lambda b, g: (g, 0, b))
    return _pallas(
        body, name="fox_bwd", grid=(t_tok // seq, N_HEADS // hg),
        in_specs=[aug_blk, aug_blk, pl.BlockSpec((hg, nkb, 128, TK), lambda b, g: (g, b, 0, 0)),
                  pl.BlockSpec((seq, 64 * hg), lambda b, g: (b, C_BV // (64 * hg) + g)), pair_blk, pair_blk, row_blk],
        out_specs=[pair_blk, pair_blk, pair_blk, row_blk],
        out_shape=[jax.ShapeDtypeStruct((t_tok, 512), BF16)] * 3
        + [jax.ShapeDtypeStruct((N_HEADS // 2, 2, t_tok), F32)],
        scratch_shapes=[pltpu.VMEM((hg, nq, 128, TQ), F32), pltpu.VMEM((hg, nq, 2, 8, TQ), F32),
                        pltpu.VMEM((hg, nkb, 8, TK), F32), pltpu.VMEM((hg, TK, 128), F32),
                        pltpu.VMEM((hg, TK, 128), F32), pltpu.VMEM((hg // 2, nq, 128, TQ), BF16),
                        pltpu.VMEM((2, hg, TK, TQ), F32), pltpu.VMEM((2, hg, TK, TQ), F32),
                        pltpu.VMEM((2, hg, TK, TQ), BF16), pltpu.VMEM((2, hg, TK, TQ), BF16),
                        pltpu.VMEM((hg, nq, 128, TQ), BF16)],
        compiler_params=_cp(("arbitrary", "arbitrary"), VMEM_BIG))(q_aug, k_aug, kt, qkvg, du_b, b_out, lse)


def _swa_window(k_ref, v_ref, n):
    prev = pl.multiple_of(jnp.maximum(n - 1, 0) * BLOCK, BLOCK)
    cur = pl.multiple_of(n * BLOCK, BLOCK)
    kwin = jnp.concatenate([k_ref[pl.ds(prev, BLOCK), :], k_ref[pl.ds(cur, BLOCK), :]], axis=0)
    vwin = jnp.concatenate([v_ref[pl.ds(prev, BLOCK), :], v_ref[pl.ds(cur, BLOCK), :]], axis=0)
    ti = lax.broadcasted_iota(jnp.int32, (BLOCK, 2 * BLOCK), 0)
    sj = lax.broadcasted_iota(jnp.int32, (BLOCK, 2 * BLOCK), 1)
    rel = ti - sj + BLOCK
    first_key = jnp.where(n > 0, 0, BLOCK)
    mask = jnp.logical_and(jnp.logical_and(rel >= 0, rel < BLOCK), sj >= first_key)
    return kwin, vwin, mask, prev, cur


def _head_cols(ref, h):
    pair = ref[:, 128 * (h // 2):128 * (h // 2 + 1)]
    return pair[:, (h % 2) * HEAD_DIM:(h % 2 + 1) * HEAD_DIM]


def _swa_logits(q_ref, kwin, bias_ref, h, mask):
    hk = h // KV_GROUP
    s = _nt(_head_cols(q_ref, h), kwin[:, hk * HEAD_DIM:(hk + 1) * HEAD_DIM]) * SCALE + bias_ref[h]
    return jnp.where(mask, s, NEG)


def _swa_fwd(qkvg, bias, sinks, seq):
    t_tok = qkvg.shape[0]
    nb = seq // BLOCK

    def body(sink_ref, q_ref, k_ref, v_ref, bias_ref, o_ref, lse_ref, s_s, p_s, den_s):
        g = pl.program_id(1)
        subs = [pl.ds(s * BLOCK, BLOCK) for s in range(SWA_SUB)]
        wins = [_swa_window(k_ref, v_ref, SWA_SUB * g + s) for s in range(SWA_SUB)]
        for s in range(SWA_SUB):
            for h in range(N_HEADS):
                s_s[s * N_HEADS + h] = _swa_logits(q_ref.at[subs[s]], wins[s][0], bias_ref, h, wins[s][2])
        lane = lax.broadcasted_iota(jnp.int32, (BLOCK, 128), 1)
        for s in range(SWA_SUB):
            lse_tile = jnp.zeros((BLOCK, 128), F32)
            for h in range(N_HEADS):
                sc = s_s[s * N_HEADS + h]
                sink = sink_ref[h]
                m = jnp.maximum(jnp.max(sc, axis=1, keepdims=True), sink)
                pe = jnp.exp(sc - m)
                den = jnp.sum(pe, axis=1, keepdims=True) + jnp.exp(sink - m)
                p_s[s * N_HEADS + h] = pe.astype(BF16)
                den_s[s * N_HEADS + h] = den
                lse_tile = jnp.where(lane == h, m + jnp.log(den), lse_tile)
            lse_ref[subs[s], :] = lse_tile
        for s in range(SWA_SUB):
            vwin = wins[s][1]
            for pr in range(N_HEADS // 2):
                outs = []
                for h in (2 * pr, 2 * pr + 1):
                    hk = h // KV_GROUP
                    outs.append(_nn(p_s[s * N_HEADS + h], vwin[:, hk * HEAD_DIM:(hk + 1) * HEAD_DIM]) / den_s[s * N_HEADS + h])
                o_ref[subs[s], 128 * pr:128 * (pr + 1)] = jnp.concatenate(outs, axis=1).astype(BF16)

    rows = SWA_SUB * BLOCK
    steps = nb // SWA_SUB
    return _pallas(
        body, name="swa_fwd", grid=(t_tok // seq, steps),
        in_specs=[pl.BlockSpec(memory_space=pltpu.SMEM),
                  pl.BlockSpec((rows, 512), lambda b, n: (b * steps + n, C_AQ // 512)),
                  pl.BlockSpec((seq, 128), lambda b, n: (b, C_AK // 128)),
                  pl.BlockSpec((seq, 128), lambda b, n: (b, C_AV // 128)),
                  pl.BlockSpec((N_HEADS, BLOCK, 2 * BLOCK), lambda b, n: (0, 0, 0))],
        out_specs=[pl.BlockSpec((rows, 512), lambda b, n: (b * steps + n, 0)),
                   pl.BlockSpec((rows, 128), lambda b, n: (b * steps + n, 0))],
        out_shape=[jax.ShapeDtypeStruct((t_tok, 512), BF16), jax.ShapeDtypeStruct((t_tok, 128), F32)],
        scratch_shapes=[pltpu.VMEM((SWA_SUB * N_HEADS, BLOCK, 2 * BLOCK), F32),
                        pltpu.VMEM((SWA_SUB * N_HEADS, BLOCK, 2 * BLOCK), BF16),
                        pltpu.VMEM((SWA_SUB * N_HEADS, BLOCK, 1), F32)],
        compiler_params=_cp(("arbitrary", "arbitrary"), VMEM_MID))(sinks, qkvg, qkvg, qkvg, bias)


def _swa_bwd(qkvg, du_a, a_out, lse, bias, sinks, seq):
    t_tok = qkvg.shape[0]
    nb = seq // BLOCK

    def body(sink_ref, q_ref, k_ref, v_ref, do_ref, o_ref, lse_ref, bias_ref,
             dq_ref, dkv_ref, dbias_ref, dsink_ref, kv_s, s_s, dp_s, pb_s, db_s):
        b, n = pl.program_id(0), pl.program_id(1)

        @pl.when(jnp.logical_and(b == 0, n == 0))
        def _():
            dbias_ref[...] = jnp.zeros_like(dbias_ref)
            dsink_ref[...] = jnp.zeros_like(dsink_ref)

        @pl.when(n == 0)
        def _():
            kv_s[...] = jnp.zeros_like(kv_s)

        subs = [pl.ds(s * BLOCK, BLOCK) for s in range(SWA_SUB)]
        wins = [_swa_window(k_ref, v_ref, SWA_SUB * n + s) for s in range(SWA_SUB)]
        for s in range(SWA_SUB):
            kwin, vwin, mask = wins[s][:3]
            for h in range(N_HEADS):
                hk = h // KV_GROUP
                s_s[s * N_HEADS + h] = _swa_logits(q_ref.at[subs[s]], kwin, bias_ref, h, mask)
                dp_s[s * N_HEADS + h] = _nt(_head_cols(do_ref.at[subs[s]], h), vwin[:, hk * HEAD_DIM:(hk + 1) * HEAD_DIM])
        for s in range(SWA_SUB):
            lse_tile = lse_ref[subs[s], :]
            do_s, o_s = do_ref.at[subs[s]], o_ref.at[subs[s]]
            for h in range(N_HEADS):
                delta = jnp.sum(_head_cols(do_s, h).astype(F32) * _head_cols(o_s, h).astype(F32), axis=1, keepdims=True)
                lse_h = _col(lse_tile, h)
                pe = jnp.exp(s_s[s * N_HEADS + h] - lse_h)
                ds = pe * (dp_s[s * N_HEADS + h] - delta)
                dbias_ref[h] += ds
                psink = jnp.exp(sink_ref[h] - lse_h)
                dsink_ref[h:h + 1, :] += jnp.broadcast_to(jnp.sum(-psink * delta, axis=0, keepdims=True), (1, 128))
                pb_s[s * N_HEADS + h] = pe.astype(BF16)
                db_s[s * N_HEADS + h] = ds.astype(BF16)
        for s in range(SWA_SUB):
            kwin, _, _, prev, cur = wins[s]
            q_s, do_s = q_ref.at[subs[s]], do_ref.at[subs[s]]
            for pr in range(N_HEADS // 2):
                dqs = []
                for h in (2 * pr, 2 * pr + 1):
                    hk = h // KV_GROUP
                    dqs.append(_nn(db_s[s * N_HEADS + h], kwin[:, hk * HEAD_DIM:(hk + 1) * HEAD_DIM]) * SCALE)
                dq_ref[subs[s], 128 * pr:128 * (pr + 1)] = jnp.concatenate(dqs, axis=1).astype(BF16)
            dks, dvs = [], []
            for hk in range(N_HEADS // KV_GROUP):
                dk = jnp.zeros((2 * BLOCK, HEAD_DIM), F32)
                dv = jnp.zeros((2 * BLOCK, HEAD_DIM), F32)
                for h in range(hk * KV_GROUP, (hk + 1) * KV_GROUP):
                    dk = dk + _tn(db_s[s * N_HEADS + h], _head_cols(q_s, h))
                    dv = dv + _tn(pb_s[s * N_HEADS + h], _head_cols(do_s, h))
                dks.append(dk * SCALE)
                dvs.append(dv)
            upd = jnp.concatenate(dks + dvs, axis=1)
            kv_s[pl.ds(prev, BLOCK), :] += upd[:BLOCK]
            kv_s[pl.ds(cur, BLOCK), :] += upd[BLOCK:]

        @pl.when(n == steps - 1)
        def _():
            dkv_ref[...] = kv_s[...].astype(BF16)

    rows = SWA_SUB * BLOCK
    steps = nb // SWA_SUB
    tile = (SWA_SUB * N_HEADS, BLOCK, 2 * BLOCK)
    return _pallas(
        body, name="swa_bwd", grid=(t_tok // seq, steps),
        in_specs=[pl.BlockSpec(memory_space=pltpu.SMEM),
                  pl.BlockSpec((rows, 512), lambda b, n: (b * steps + n, C_AQ // 512)),
                  pl.BlockSpec((seq, 128), lambda b, n: (b, C_AK // 128)),
                  pl.BlockSpec((seq, 128), lambda b, n: (b, C_AV // 128)),
                  pl.BlockSpec((rows, 512), lambda b, n: (b * steps + n, 0)),
                  pl.BlockSpec((rows, 512), lambda b, n: (b * steps + n, 0)),
                  pl.BlockSpec((rows, 128), lambda b, n: (b * steps + n, 0)),
                  pl.BlockSpec((N_HEADS, BLOCK, 2 * BLOCK), lambda b, n: (0, 0, 0))],
        out_specs=[pl.BlockSpec((rows, 512), lambda b, n: (b * steps + n, 0)),
                   pl.BlockSpec((seq, 256), lambda b, n: (b, 0)),
                   pl.BlockSpec((N_HEADS, BLOCK, 2 * BLOCK), lambda b, n: (0, 0, 0)),
                   pl.BlockSpec((N_HEADS, 128), lambda b, n: (0, 0))],
        out_shape=[jax.ShapeDtypeStruct((t_tok, 512), BF16), jax.ShapeDtypeStruct((t_tok, 256), BF16),
                   jax.ShapeDtypeStruct((N_HEADS, BLOCK, 2 * BLOCK), F32), jax.ShapeDtypeStruct((N_HEADS, 128), F32)],
        scratch_shapes=[pltpu.VMEM((seq, 256), F32), pltpu.VMEM(tile, F32), pltpu.VMEM(tile, F32),
                        pltpu.VMEM(tile, BF16), pltpu.VMEM(tile, BF16)],
        compiler_params=_cp(("arbitrary", "arbitrary"), VMEM_MID))(sinks, qkvg, qkvg, qkvg, du_a, a_out, lse, bias)


def _out_proj(name, u_parts, gate_arr, gate_blk, w_out, x, gmod, seq):
    t_tok = x.shape[0]
    nu = len(u_parts)

    def body(*refs):
        u_refs = refs[:nu]
        g_ref, w_ref, x_ref, gm_ref, yg_ref, y_ref, xn_ref = refs[nu:]
        u = jnp.concatenate([r[...].astype(F32) for r in u_refs], axis=1) if nu > 1 else u_refs[0][...].astype(F32)
        yg = (u * _silu(g_ref[...].astype(F32))).astype(BF16)
        yg_ref[...] = yg
        y = _nn(yg, w_ref[...])
        y_ref[...] = y.astype(BF16)
        xn_ref[...] = x_ref[...] + gm_ref[...] * y

    row = lambda w: pl.BlockSpec((TM, w), lambda i: (i, 0))
    in_specs = [row(u.shape[1]) for u in u_parts]
    in_specs += [pl.BlockSpec((TM, D_MODEL), lambda i: (i, gate_blk)),
                 pl.BlockSpec((D_MODEL, D_MODEL), lambda i: (0, 0)), row(D_MODEL),
                 pl.BlockSpec((None, 1, D_MODEL), lambda i: (i * TM // seq, 0, 0))]
    return _pallas(
        body, name=name, grid=(t_tok // TM,), in_specs=in_specs,
        out_specs=[row(D_MODEL)] * 3,
        out_shape=[jax.ShapeDtypeStruct((t_tok, D_MODEL), BF16)] * 2 + [jax.ShapeDtypeStruct((t_tok, D_MODEL), F32)],
        compiler_params=_cp(("arbitrary",), VMEM_MID))(*u_parts, gate_arr, w_out, x, gmod)


def _out_proj_bwd(name, dxn, gmod, y, w_out, seq, attn=None):
    t_tok = dxn.shape[0]
    tiles_per_seq = seq // TM

    def body(*refs):
        if attn is None:
            dxn_ref, gm_ref, y_ref, w_ref, dy_ref, dgm_ref, dyg_ref = refs
        else:
            dxn_ref, gm_ref, y_ref, w_ref, a_ref, b_ref, g_ref, dy_ref, dgm_ref, dua_ref, dub_ref, dg_ref = refs
        i = pl.program_id(0)
        dxv = dxn_ref[...]
        dy = (dxv * gm_ref[...]).astype(BF16)
        dy_ref[...] = dy

        @pl.when(i % tiles_per_seq == 0)
        def _():
            dgm_ref[...] = jnp.zeros_like(dgm_ref)

        dgm_ref[...] += jnp.sum(dxv * y_ref[...].astype(F32), axis=0, keepdims=True)
        dyg = _nn(dy, w_ref[...])
        if attn is None:
            dyg_ref[...] = dyg
        else:
            gt = g_ref[...].astype(F32)
            du = dyg * _silu(gt)
            dua_ref[...] = du[:, :512].astype(BF16)
            dub_ref[...] = du[:, 512:].astype(BF16)
            u = jnp.concatenate([a_ref[...].astype(F32), b_ref[...].astype(F32)], axis=1)
            dg_ref[...] = (dyg * u * _dsilu(gt)).astype(BF16)

    row = lambda w: pl.BlockSpec((TM, w), lambda i: (i, 0))
    mod_spec = pl.BlockSpec((None, 1, D_MODEL), lambda i: (i * TM // seq, 0, 0))
    in_specs = [row(D_MODEL), mod_spec, row(D_MODEL), pl.BlockSpec((D_MODEL, D_MODEL), lambda i: (0, 0))]
    out_specs = [row(D_MODEL), mod_spec]
    out_shape = [jax.ShapeDtypeStruct((t_tok, D_MODEL), BF16), jax.ShapeDtypeStruct(gmod.shape, F32)]
    args = [dxn, gmod, y, w_out]
    if attn is None:
        out_specs.append(row(D_MODEL))
        out_shape.append(jax.ShapeDtypeStruct((t_tok, D_MODEL), F32))
    else:
        in_specs += [row(512), row(512), pl.BlockSpec((TM, D_MODEL), lambda i: (i, C_GATE // D_MODEL))]
        out_specs += [row(512), row(512), row(D_MODEL)]
        out_shape += [jax.ShapeDtypeStruct((t_tok, 512), BF16)] * 2 + [jax.ShapeDtypeStruct((t_tok, D_MODEL), BF16)]
        args += list(attn)
    return _pallas(body, name=name, grid=(t_tok // TM,), in_specs=in_specs, out_specs=out_specs,
                   out_shape=out_shape, compiler_params=_cp(("arbitrary",), VMEM_MID))(*args)


def _norm_bwd(name, parts, w, x, g, scale, dxn, seq, rows_part=None):
    t_tok = x.shape[0]
    npart = len(parts)
    tiles_per_seq = seq // TM
    nrow_in = 0 if rows_part is None else 2

    def body(*refs):
        p_refs = refs[:npart]
        w_ref, x_ref, g_ref, sc_ref, dxn_ref = refs[npart:npart + 5]
        dx_ref, dss_ref, dg_ref = refs[npart + 5 + nrow_in:]
        i = pl.program_id(0)
        dh = jnp.zeros((TM, D_MODEL), F32)
        if rows_part is not None:
            r_ref, wr_ref = refs[npart + 5:npart + 7]
            dh = dh + _tn(r_ref[...].astype(BF16), wr_ref[...])
        for (arr, off), p_ref in zip(parts, p_refs):
            dh = dh + _nn(p_ref[...], w_ref[off:off + arr.shape[1], :])
        xv = x_ref[...]
        rstd = lax.rsqrt(jnp.mean(xv * xv, axis=-1, keepdims=True) + EPS)
        xhat = xv * rstd
        gv = g_ref[...]
        nrm = xhat * gv

        @pl.when(i % tiles_per_seq == 0)
        def _():
            dss_ref[...] = jnp.zeros_like(dss_ref)

        @pl.when(i == 0)
        def _():
            dg_ref[...] = jnp.zeros_like(dg_ref)

        dss_ref[0:1, :] += jnp.sum(dh, axis=0, keepdims=True)
        dss_ref[1:2, :] += jnp.sum(dh * nrm, axis=0, keepdims=True)
        dn = dh * (1.0 + sc_ref[...])
        dg_ref[0:1, :] += jnp.sum(dn * xhat, axis=0, keepdims=True)
        dxhat = dn * gv
        dx_ref[...] = rstd * (dxhat - xhat * jnp.mean(dxhat * xhat, axis=-1, keepdims=True)) + dxn_ref[...]

    row = lambda wd: pl.BlockSpec((TM, wd), lambda i: (i, 0))
    w_spec = pl.BlockSpec(w.shape, lambda i: (0, 0))
    in_specs = [row(a.shape[1]) for a, _ in parts]
    in_specs += [w_spec, row(D_MODEL), pl.BlockSpec((1, D_MODEL), lambda i: (0, 0)),
                 pl.BlockSpec((None, 1, D_MODEL), lambda i: (i * TM // seq, 0, 0)), row(D_MODEL)]
    args = [a for a, _ in parts] + [w, x, g, scale, dxn]
    if rows_part is not None:
        in_specs += [pl.BlockSpec((8, TM), lambda i: (0, i)), pl.BlockSpec((8, D_MODEL), lambda i: (0, 0))]
        args += list(rows_part)
    nseq = t_tok // seq
    return _pallas(
        body, name=name, grid=(t_tok // TM,), in_specs=in_specs,
        out_specs=[row(D_MODEL), pl.BlockSpec((None, 8, D_MODEL), lambda i: (i * TM // seq, 0, 0)),
                   pl.BlockSpec((8, D_MODEL), lambda i: (0, 0))],
        out_shape=[jax.ShapeDtypeStruct((t_tok, D_MODEL), F32), jax.ShapeDtypeStruct((nseq, 8, D_MODEL), F32),
                   jax.ShapeDtypeStruct((8, D_MODEL), F32)],
        compiler_params=_cp(("arbitrary",), VMEM_BIG))(*args)


def _dw(name, a, parts, blocked=None, rows_t=None):
    t_tok, ka = a.shape
    tt = min(1024, t_tok)
    npart = len(parts)
    nrow = 0 if rows_t is None else 1
    nt = t_tok // tt

    def body(*refs):
        a_ref = refs[0]
        p_refs = refs[1:1 + npart]
        o_refs = refs[1 + npart + nrow:1 + 2 * npart + nrow]
        acc_refs = refs[1 + 2 * npart + 2 * nrow:]
        t = pl.program_id(0)
        if nrow:
            r_ref, ro_ref = refs[1 + npart], refs[1 + 2 * npart + nrow]

            @pl.when(t == 0)
            def _():
                ro_ref[...] = jnp.zeros_like(ro_ref)

            ro_ref[...] += _nn(r_ref[...].astype(BF16), a_ref[...])
        at = a_ref[...].T
        for p_ref, acc in zip(p_refs, acc_refs):
            upd = _nn(at, p_ref[...])

            @pl.when(t == 0)
            def _():
                acc[...] = upd

            @pl.when(t > 0)
            def _():
                acc[...] += upd

        @pl.when(t == nt - 1)
        def _():
            for o_ref, acc in zip(o_refs, acc_refs):
                if blocked is None:
                    o_ref[...] = acc[...].astype(BF16)
                else:
                    for j in range(o_ref.shape[0]):
                        o_ref[j] = acc[:, j * blocked:(j + 1) * blocked].astype(BF16)

    in_specs = [pl.BlockSpec((tt, ka), lambda t: (t, 0))]
    in_specs += [pl.BlockSpec((tt, p.shape[1]), lambda t: (t, 0)) for p in parts]
    if blocked is None:
        out_shape = [jax.ShapeDtypeStruct((ka, p.shape[1]), BF16) for p in parts]
        out_specs = [pl.BlockSpec((ka, p.shape[1]), lambda t: (0, 0)) for p in parts]
    else:
        out_shape = [jax.ShapeDtypeStruct((p.shape[1] // blocked, ka, blocked), BF16) for p in parts]
        out_specs = [pl.BlockSpec((p.shape[1] // blocked, ka, blocked), lambda t: (0, 0, 0)) for p in parts]
    args = [a, *parts]
    if nrow:
        in_specs.append(pl.BlockSpec((8, tt), lambda t: (0, t)))
        out_shape.append(jax.ShapeDtypeStruct((8, ka), F32))
        out_specs.append(pl.BlockSpec((8, ka), lambda t: (0, 0)))
        args.append(rows_t)
    return _pallas(body, name=name, grid=(nt,), in_specs=in_specs, out_specs=out_specs, out_shape=out_shape,
                   scratch_shapes=[pltpu.VMEM((ka, p.shape[1]), F32) for p in parts],
                   compiler_params=_cp(("arbitrary",), VMEM_BIG))(*args)


def _dw_rows(name, rows_t, h):
    t_tok = h.shape[0]
    tt = 512

    def body(r_ref, h_ref, o_ref):
        @pl.when(pl.program_id(0) == 0)
        def _():
            o_ref[...] = jnp.zeros_like(o_ref)

        o_ref[...] += _nn(r_ref[...].astype(BF16), h_ref[...])

    return _pallas(body, name=name, grid=(t_tok // tt,),
                   in_specs=[pl.BlockSpec((8, tt), lambda t: (0, t)), pl.BlockSpec((tt, D_MODEL), lambda t: (t, 0))],
                   out_specs=pl.BlockSpec((8, D_MODEL), lambda t: (0, 0)),
                   out_shape=jax.ShapeDtypeStruct((8, D_MODEL), F32),
                   compiler_params=_cp(("arbitrary",), VMEM_MID))(rows_t, h)


def _lru_gates(xc, blk, wa_ref, wx_ref, ba_ref, bx_ref, sp):
    cols = slice(blk * LRU_BLOCK_W, (blk + 1) * LRU_BLOCK_W)
    xb = xc[:, cols].astype(BF16)
    r = _sigmoid(_nn(xb, wa_ref[blk].astype(BF16)) + ba_ref[:, cols])
    ig = _sigmoid(_nn(xb, wx_ref[blk].astype(BF16)) + bx_ref[:, cols])
    log_a = -LRU_C * r * sp[:, cols]
    a = jnp.exp(log_a)
    x2 = 2.0 * log_a
    series = -x2 * (1.0 + x2 * (0.5 + x2 * (1.0 / 6.0)))
    z = jnp.where(x2 > -0.01, series, 1.0 - a * a)
    mult = z * lax.rsqrt(jnp.maximum(z, 1e-30))
    return xb, r, ig, a, mult


def _softplus_neg(lam):
    return jnp.maximum(-lam, 0.0) + jnp.log(1.0 + jnp.exp(-jnp.abs(lam)))


def _conv_taps(xe_ref, cw_ref, cb_ref):
    xc = cb_ref[...] + xe_ref[8:8 + TC, :] * cw_ref[3:4, :]
    for k in range(1, 4):
        xc = xc + xe_ref[8 - k:8 - k + TC, :] * cw_ref[3 - k:4 - k, :]
    return xc


def _lru_fwd(proj, cw, cb, w_a, b_a, w_x, b_x, lam, seq):
    t_tok = proj.shape[0]
    nc = seq // TC

    def body(x_ref, cw_ref, cb_ref, wa_ref, ba_ref, wx_ref, bx_ref, lam_ref, hs_ref, xe_s, a_s, u_s, h_s):
        c = pl.program_id(1)

        @pl.when(c == 0)
        def _():
            xe_s[0:8, :] = jnp.zeros((8, D_MODEL), F32)
            h_s[...] = jnp.zeros_like(h_s)

        xe_s[8:8 + TC, :] = x_ref[...]
        xc = _conv_taps(xe_s, cw_ref, cb_ref)
        sp = _softplus_neg(lam_ref[...])
        for blk in range(LRU_BLOCKS):
            cols = slice(blk * LRU_BLOCK_W, (blk + 1) * LRU_BLOCK_W)
            _, _, ig, a, mult = _lru_gates(xc, blk, wa_ref, wx_ref, ba_ref, bx_ref, sp)
            a_s[:, cols] = a
            u_s[:, cols] = mult * ig * xc[:, cols]

        def step(t8, h):
            base = pl.multiple_of(t8 * 8, 8)
            for q in range(8):
                h = a_s[pl.ds(base + q, 1), :] * h + u_s[pl.ds(base + q, 1), :]
                hs_ref[pl.ds(base + q, 1), :] = h
            return h

        h_s[0:1, :] = lax.fori_loop(0, TC // 8, step, h_s[0:1, :])
        xe_s[0:8, :] = xe_s[TC:TC + 8, :]

    full = lambda shape: pl.BlockSpec(shape, lambda b, c: (0,) * len(shape))
    return _pallas(
        body, name="lru_fwd", grid=(t_tok // seq, nc),
        in_specs=[pl.BlockSpec((TC, D_MODEL), lambda b, c: (b * nc + c, 0)), full((4, D_MODEL)), full((1, D_MODEL)),
                  full((LRU_BLOCKS, LRU_BLOCK_W, LRU_BLOCK_W)), full((1, D_MODEL)),
                  full((LRU_BLOCKS, LRU_BLOCK_W, LRU_BLOCK_W)), full((1, D_MODEL)), full((1, D_MODEL))],
        out_specs=pl.BlockSpec((TC, D_MODEL), lambda b, c: (b * nc + c, 0)),
        out_shape=jax.ShapeDtypeStruct((t_tok, D_MODEL), F32),
        scratch_shapes=[pltpu.VMEM((TC + 8, D_MODEL), F32), pltpu.VMEM((TC, D_MODEL), F32),
                        pltpu.VMEM((TC, D_MODEL), F32), pltpu.VMEM((8, D_MODEL), F32)],
        compiler_params=_cp(("arbitrary", "arbitrary"), VMEM_BIG))(proj, cw, cb, w_a, b_a, w_x, b_x, lam)


def _lru_bwd(proj, hs, dyh, cw, cb, w_a, b_a, w_x, b_x, lam, seq):
    t_tok = proj.shape[0]
    nc = seq // TC

    def body(x_ref, xh_ref, g_ref, hs_ref, hh_ref, dy_ref, cw_ref, cb_ref, wa_ref, ba_ref, wx_ref, bx_ref, lam_ref,
             dp_ref, dcw_ref, dvec_ref, dwa_ref, dwx_ref,
             xe_s, he_s, de_s, a_s, r_s, i_s, m_s, dhs_s, dh_s, carry_s):
        b, cr = pl.program_id(0), pl.program_id(1)
        c = nc - 1 - cr

        @pl.when(jnp.logical_and(b == 0, cr == 0))
        def _():
            dcw_ref[...] = jnp.zeros_like(dcw_ref)
            dvec_ref[...] = jnp.zeros_like(dvec_ref)
            dwa_ref[...] = jnp.zeros_like(dwa_ref)
            dwx_ref[...] = jnp.zeros_like(dwx_ref)

        @pl.when(cr == 0)
        def _():
            carry_s[...] = jnp.zeros_like(carry_s)
            de_s[TC:TC + 8, :] = jnp.zeros((8, D_MODEL), F32)

        first = c == 0
        xe_s[0:8, :] = jnp.where(first, 0.0, xh_ref[...])
        xe_s[8:8 + TC, :] = x_ref[...]
        he_s[0:8, :] = jnp.where(first, 0.0, hh_ref[...])
        he_s[8:8 + TC, :] = hs_ref[...]
        xc = _conv_taps(xe_s, cw_ref, cb_ref)
        lam_v = lam_ref[...]
        sp = _softplus_neg(lam_v)
        for blk in range(LRU_BLOCKS):
            cols = slice(blk * LRU_BLOCK_W, (blk + 1) * LRU_BLOCK_W)
            _, r, ig, a, mult = _lru_gates(xc, blk, wa_ref, wx_ref, ba_ref, bx_ref, sp)
            a_s[:, cols], r_s[:, cols], i_s[:, cols], m_s[:, cols] = a, r, ig, mult

        gt = g_ref[...]
        dyh = dy_ref[...]
        sg = _sigmoid(gt)
        dhs_s[...] = dyh * (gt * sg)
        dp_ref[:, D_MODEL:] = (dyh * hs_ref[...] * (sg * (1.0 + gt * (1.0 - sg)))).astype(BF16)

        def step(k8, carry):
            base = pl.multiple_of(TC - 8 - k8 * 8, 8)
            for q in reversed(range(8)):
                dh = dhs_s[pl.ds(base + q, 1), :] + carry
                dh_s[pl.ds(base + q, 1), :] = dh
                carry = a_s[pl.ds(base + q, 1), :] * dh
            return carry

        carry_s[0:1, :] = lax.fori_loop(0, TC // 8, step, carry_s[0:1, :])

        hprev = he_s[7:7 + TC, :]
        for blk in range(LRU_BLOCKS):
            cols = slice(blk * LRU_BLOCK_W, (blk + 1) * LRU_BLOCK_W)
            xcb = xc[:, cols]
            a, r, ig, mult, dh = a_s[:, cols], r_s[:, cols], i_s[:, cols], m_s[:, cols], dh_s[:, cols]
            spb = sp[:, cols]
            dmult = dh * ig * xcb
            di = dh * mult * xcb
            dxc = dh * mult * ig
            dla = dh * hprev[:, cols] * a - dmult * (a * a) * lax.rsqrt(jnp.maximum(mult * mult, 1e-30))
            dr = dla * (-LRU_C * spb)
            dsp = jnp.sum(dla * (-LRU_C * r), axis=0, keepdims=True)
            dga = dr * r * (1.0 - r)
            dgx = di * ig * (1.0 - ig)
            dga_b, dgx_b = dga.astype(BF16), dgx.astype(BF16)
            xb = xcb.astype(BF16)
            dxc = dxc + _nt(dga_b, wa_ref[blk].astype(BF16)) + _nt(dgx_b, wx_ref[blk].astype(BF16))
            dwa_ref[blk] += _tn(xb, dga_b)
            dwx_ref[blk] += _tn(xb, dgx_b)
            dvec_ref[1:2, cols] += jnp.sum(dga, axis=0, keepdims=True)
            dvec_ref[2:3, cols] += jnp.sum(dgx, axis=0, keepdims=True)
            dvec_ref[3:4, cols] += dsp * (-1.0 / (1.0 + jnp.exp(lam_v[:, cols])))
            de_s[0:TC, cols] = dxc

        dxc = de_s[0:TC, :]
        dvec_ref[0:1, :] += jnp.sum(dxc, axis=0, keepdims=True)
        dxr = dxc * cw_ref[3:4, :]
        dcw_ref[3:4, :] += jnp.sum(dxc * xe_s[8:8 + TC, :], axis=0, keepdims=True)
        for k in range(1, 4):
            dxr = dxr + de_s[k:k + TC, :] * cw_ref[3 - k:4 - k, :]
            dcw_ref[3 - k:4 - k, :] += jnp.sum(dxc * xe_s[8 - k:8 - k + TC, :], axis=0, keepdims=True)
        dp_ref[:, :D_MODEL] = dxr.astype(BF16)
        de_s[TC:TC + 8, :] = de_s[0:8, :]

    chunk = lambda col: pl.BlockSpec((TC, D_MODEL), lambda b, cr: (b * nc + nc - 1 - cr, col))
    halo = lambda col: pl.BlockSpec(
        (8, D_MODEL), lambda b, cr: (jnp.maximum((b * nc + nc - 1 - cr) * (TC // 8) - 1, 0), col))
    full = lambda shape: pl.BlockSpec(shape, lambda b, cr: (0,) * len(shape))
    wblk = (LRU_BLOCKS, LRU_BLOCK_W, LRU_BLOCK_W)
    return _pallas(
        body, name="lru_bwd", grid=(t_tok // seq, nc),
        in_specs=[chunk(0), halo(0), chunk(1), chunk(0), halo(0), chunk(0),
                  full((4, D_MODEL)), full((1, D_MODEL)), full(wblk), full((1, D_MODEL)), full(wblk),
                  full((1, D_MODEL)), full((1, D_MODEL))],
        out_specs=[pl.BlockSpec((TC, 2 * D_MODEL), lambda b, cr: (b * nc + nc - 1 - cr, 0)),
                   full((8, D_MODEL)), full((8, D_MODEL)), full(wblk), full(wblk)],
        out_shape=[jax.ShapeDtypeStruct((t_tok, 2 * D_MODEL), BF16), jax.ShapeDtypeStruct((8, D_MODEL), F32),
                   jax.ShapeDtypeStruct((8, D_MODEL), F32), jax.ShapeDtypeStruct(wblk, F32),
                   jax.ShapeDtypeStruct(wblk, F32)],
        scratch_shapes=[pltpu.VMEM((TC + 8, D_MODEL), F32), pltpu.VMEM((TC + 8, D_MODEL), F32),
                        pltpu.VMEM((TC + 8, D_MODEL), F32)]
        + [pltpu.VMEM((TC, D_MODEL), F32)] * 6 + [pltpu.VMEM((8, D_MODEL), F32)],
        compiler_params=_cp(("arbitrary", "arbitrary"), VMEM_BIG),
    )(proj, proj, proj, hs, hs, dyh, cw, cb, w_a, b_a, w_x, b_x, lam)


def _last_layer_tail(hs, proj, w_out, w_out_t, x, gmod, final_g, target, seq):
    t_tok = x.shape[0]
    tiles_per_seq = seq // TM

    def body(hs_ref, g_ref, w_ref, wt_ref, x_ref, gm_ref, fg_ref, t_ref,
             yg_ref, dx_ref, dy_ref, dyg_ref, dgm_ref, loss_ref, dfg_ref):
        i = pl.program_id(0)

        @pl.when(i == 0)
        def _():
            loss_ref[...] = jnp.zeros_like(loss_ref)
            dfg_ref[...] = jnp.zeros_like(dfg_ref)

        @pl.when(i % tiles_per_seq == 0)
        def _():
            dgm_ref[...] = jnp.zeros_like(dgm_ref)

        gm = gm_ref[...]
        yg = (hs_ref[...] * _silu(g_ref[...])).astype(BF16)
        yg_ref[...] = yg
        y = _nn(yg, w_ref[...])
        xv = x_ref[...] + gm * y
        gv = fg_ref[...]
        rstd = lax.rsqrt(jnp.mean(xv * xv, axis=-1, keepdims=True) + EPS)
        xhat = xv * rstd
        err = xhat * gv - t_ref[...]
        loss_ref[0:1, :] += jnp.sum(err * err, axis=0, keepdims=True) * (0.5 / D_MODEL)
        dout = err * (1.0 / D_MODEL)
        dfg_ref[0:1, :] += jnp.sum(dout * xhat, axis=0, keepdims=True)
        dxhat = dout * gv
        dxv = rstd * (dxhat - xhat * jnp.mean(dxhat * xhat, axis=-1, keepdims=True))
        dx_ref[...] = dxv
        dgm_ref[...] += jnp.sum(dxv * y, axis=0, keepdims=True)
        dy = (dxv * gm).astype(BF16)
        dy_ref[...] = dy
        dyg_ref[...] = _nn(dy, wt_ref[...])

    row = pl.BlockSpec((TM, D_MODEL), lambda i: (i, 0))
    acc = pl.BlockSpec((8, D_MODEL), lambda i: (0, 0))
    mod_spec = pl.BlockSpec((None, 1, D_MODEL), lambda i: (i * TM // seq, 0, 0))
    return _pallas(
        body, name="last_layer_tail", grid=(t_tok // TM,),
        in_specs=[row, pl.BlockSpec((TM, D_MODEL), lambda i: (i, 1)), pl.BlockSpec((D_MODEL, D_MODEL), lambda i: (0, 0)),
                  pl.BlockSpec((D_MODEL, D_MODEL), lambda i: (0, 0)),
                  row, mod_spec, pl.BlockSpec((1, D_MODEL), lambda i: (0, 0)), row],
        out_specs=[row, row, row, row, mod_spec, acc, acc],
        out_shape=[jax.ShapeDtypeStruct((t_tok, D_MODEL), BF16), jax.ShapeDtypeStruct((t_tok, D_MODEL), F32),
                   jax.ShapeDtypeStruct((t_tok, D_MODEL), BF16), jax.ShapeDtypeStruct((t_tok, D_MODEL), F32),
                   jax.ShapeDtypeStruct(gmod.shape, F32), jax.ShapeDtypeStruct((8, D_MODEL), F32),
                   jax.ShapeDtypeStruct((8, D_MODEL), F32)],
        compiler_params=_cp(("arbitrary",), VMEM_BIG))(hs, proj, w_out, w_out_t, x, gmod, final_g, target)


def _adam_math(w, g, m, v):
    m_new = ADAM_B1 * m + (1.0 - ADAM_B1) * g
    v_new = ADAM_B2 * v + (1.0 - ADAM_B2) * (g * g)
    m_hat = m_new / (1.0 - ADAM_B1 ** ADAM_STEP)
    v_hat = v_new / (1.0 - ADAM_B2 ** ADAM_STEP)
    delta = -ADAM_LR * (m_hat / (jnp.sqrt(v_hat) + ADAM_EPS) + ADAM_WD * w)
    return delta, m_new, v_new


def _sum_leading(name, x, out_dtype=F32):
    n, rows, cols = x.shape
    tr = PACK_ROWS if rows % PACK_ROWS == 0 else rows

    def body(x_ref, o_ref):
        acc = x_ref[0].astype(F32)
        for d in range(1, n):
            acc = acc + x_ref[d].astype(F32)
        o_ref[...] = acc.astype(out_dtype)

    return _pallas(body, name=name, grid=(rows // tr,),
                   in_specs=[pl.BlockSpec((n, tr, cols), lambda i: (0, i, 0))],
                   out_specs=pl.BlockSpec((tr, cols), lambda i: (i, 0)),
                   out_shape=jax.ShapeDtypeStruct((rows, cols), out_dtype),
                   compiler_params=_cp(("arbitrary",), VMEM_MID))(x)


def _adamw(name, w, m, v, g=None, parts=None):
    rows, cols = w.shape
    tr = rows if rows <= 256 else 256

    def body(*refs):
        w_ref, m_ref, v_ref, g_in, g_ref, d_ref, mo_ref, vo_ref = refs
        if parts is None:
            gv = g_in[...]
        else:
            acc = g_in[0].astype(F32)
            for d in range(1, parts.shape[0]):
                acc = acc + g_in[d].astype(F32)
            gv = acc[:, :cols]
        delta, m_new, v_new = _adam_math(w_ref[...], gv, m_ref[...], v_ref[...])
        g_ref[...] = gv
        d_ref[...] = delta
        mo_ref[...] = m_new
        vo_ref[...] = v_new

    row = pl.BlockSpec((tr, cols), lambda i: (i, 0))
    if parts is None:
        g_spec, g_arg = row, g
    else:
        g_spec, g_arg = pl.BlockSpec((parts.shape[0], tr, parts.shape[2]), lambda i: (0, i, 0)), parts
    return _pallas(body, name=name, grid=(rows // tr,), in_specs=[row, row, row, g_spec], out_specs=[row] * 4,
                   out_shape=[jax.ShapeDtypeStruct((rows, cols), F32)] * 4,
                   compiler_params=_cp(("arbitrary",), VMEM_MID))(w, m, v, g_arg)


def _adamw_many(name, groups):
    ntens = len(groups)

    def body(*refs):
        ins, outs = refs[:4 * ntens], refs[4 * ntens:]
        for k in range(ntens):
            w_ref, m_ref, v_ref, g_ref = ins[4 * k:4 * k + 4]
            gv = g_ref[...]
            delta, m_new, v_new = _adam_math(w_ref[...], gv, m_ref[...], v_ref[...])
            for o_ref, val in zip(outs[4 * k:4 * k + 4], (gv, delta, m_new, v_new)):
                o_ref[...] = val

    flat = [a for grp in groups for a in grp]
    out_shape = [jax.ShapeDtypeStruct(grp[0].shape, F32) for grp in groups for _ in range(4)]
    outs = _pallas(body, name=name, out_shape=out_shape, compiler_params=_cp(vmem=VMEM_MID))(*flat)
    return [tuple(outs[4 * k:4 * k + 4]) for k in range(ntens)]


def _pack_rows(arrs):
    rows, meta, total = [], [], 0
    for a in arrs:
        flat = a.reshape(-1)
        nrow = -(-flat.shape[0] // 1024) * 8
        rows.append(jnp.pad(flat, (0, nrow * 128 - flat.shape[0])).reshape(nrow, 128))
        meta.append((a.shape, flat.shape[0], nrow))
        total += nrow
    tail = -total % PACK_ROWS
    if tail:
        rows.append(jnp.zeros((tail, 128), F32))
    return jnp.concatenate(rows, axis=0), meta


def _unpack_rows(packed, meta):
    out, r0 = [], 0
    for shape, size, nrow in meta:
        out.append(packed[r0:r0 + nrow].reshape(-1)[:size].reshape(shape))
        r0 += nrow
    return out


WEIGHTS = ["rel_bias", "norm_g", "ada_w", "ada_b", "attn_w_in", "attn_sinks", "attn_b_f", "attn_w_out", "lru_w_in",
           "lru_conv_w", "lru_conv_b", "lru_w_a", "lru_b_a", "lru_w_x", "lru_b_x", "lru_lambda", "lru_w_out", "final_g"]
BIG = ["ada_w", "attn_w_in", "attn_w_out", "lru_w_in", "lru_w_out"]
PACK_ROWS = 256


def kernel(x, c, rel_bias, norm_g, ada_w, ada_b, attn_w_in, attn_sinks, attn_b_f, attn_w_out, lru_w_in, lru_conv_w, lru_conv_b, lru_w_a, lru_b_a, lru_w_x, lru_b_x, lru_lambda, lru_w_out, final_g, loss_target, m_rel_bias, m_norm_g, m_ada_w, m_ada_b, m_attn_w_in, m_attn_sinks, m_attn_b_f, m_attn_w_out, m_lru_w_in, m_lru_conv_w, m_lru_conv_b, m_lru_w_a, m_lru_b_a, m_lru_w_x, m_lru_b_x, m_lru_lambda, m_lru_w_out, m_final_g, v_rel_bias, v_norm_g, v_ada_w, v_ada_b, v_attn_w_in, v_attn_sinks, v_attn_b_f, v_attn_w_out, v_lru_w_in, v_lru_conv_w, v_lru_conv_b, v_lru_w_a, v_lru_b_a, v_lru_w_x, v_lru_b_x, v_lru_lambda, v_lru_w_out, v_final_g):
    nseq, seq, _ = x.shape
    t_tok = nseq * seq
    me = 4 * lax.axis_index("x") + 2 * lax.axis_index("y") + lax.axis_index("c")
    x0 = x.reshape(t_tok, D_MODEL)
    target = loss_target.reshape(t_tok, D_MODEL)

    w_in_pad = jnp.pad(attn_w_in[0].astype(BF16), ((0, 0), (0, SHARD_W_PAD - SHARD_W_IN)))
    vec_shard = jnp.concatenate([lru_conv_w[0], lru_conv_b, lru_b_a, lru_b_x, lru_lambda], axis=0)
    g_w_in, g_vec, g_c = _exchange("gather_first", [w_in_pad, vec_shard, c], [])
    later_w = [attn_w_out[0].astype(BF16), lru_w_in[0].astype(BF16), lru_w_out[0].astype(BF16)]
    later_handle, later_token = _exchange_start("gather_later_start", later_w, [], after=g_vec)
    w_full = jnp.transpose(g_w_in[:, :, :SHARD_W_IN], (1, 0, 2)).reshape(D_MODEL, N_DEV * SHARD_W_IN)
    w_aq, w_ak, w_av = w_full[:, 0:512], w_full[:, 512:640], w_full[:, 640:768]
    w_bq, w_bk, w_bv = w_full[:, 768:1280], w_full[:, 1280:1792], w_full[:, 1792:2304]
    w_f, w_gate = w_full[:, 2304:2312], w_full[:, 2312:3336]
    w_main = jnp.concatenate([w_bq, w_bk, w_bv, w_aq, w_gate, w_ak, w_av], axis=1)
    wf_t = jnp.transpose(w_f)
    vec_full = jnp.transpose(g_vec, (1, 0, 2)).reshape(8, D_MODEL)
    conv_w, conv_b, b_a, b_x, lam = vec_full[0:4], vec_full[4:5], vec_full[5:6], vec_full[6:7], vec_full[7:8]
    c_all = g_c.reshape(N_DEV * nseq, D_MODEL)

    ncol = ada_w.shape[2]
    ada_b_slice = lax.dynamic_slice(ada_b.reshape(2, N_DEV, ncol), (0, me, 0), (2, 1, ncol))
    mod_part = _ada_mod(c_all, ada_w, ada_b_slice)
    (g_mod,) = _exchange("gather_mod", [mod_part], [])
    mine = lax.dynamic_slice(g_mod, (0, 0, me * nseq, 0), (N_DEV, 2, nseq, ncol))
    mod = jnp.transpose(mine, (1, 2, 0, 3)).reshape(2, nseq, 3 * D_MODEL)
    shift = [mod[l, :, 0:D_MODEL].reshape(nseq, 1, D_MODEL) for l in range(2)]
    scale = [mod[l, :, D_MODEL:2 * D_MODEL].reshape(nseq, 1, D_MODEL) for l in range(2)]
    gmod = [mod[l, :, 2 * D_MODEL:].reshape(nseq, 1, D_MODEL) for l in range(2)]

    onehot = _bucket_onehot()
    bias = _bias_expand(jnp.transpose(rel_bias), onehot).reshape(N_HEADS, BLOCK, 2 * BLOCK)
    sinks = attn_sinks.reshape(N_HEADS)
    b_f = attn_b_f.reshape(N_HEADS, 1)
    norm_g0 = norm_g[0:1] + later_token[0:1, 0:1]
    h0, qkvg, fl_t = _norm_proj("norm_proj0", x0, norm_g0, shift[0], scale[0], w_main, seq, BF16, wf_t=wf_t)
    f_row, f_col = _fox_prep(fl_t, b_f, seq)
    a_out, lse_a = _swa_fwd(qkvg, bias, sinks, seq)
    q_aug, k_aug, kt_aug, vt = _fox_aug(qkvg, f_col, seq)
    b_out, lse_b = _fox_fwd_t(q_aug, k_aug, vt, seq)
    g_later = _exchange_wait("gather_later_wait", later_handle, after=lse_b)
    w_out0, g_lru_in, w_out1 = (_with_own(g, w, me) for g, w in zip(g_later, later_w))
    w_out0, w_out1 = w_out0.reshape(D_MODEL, D_MODEL), w_out1.reshape(D_MODEL, D_MODEL)
    w_out0_t, w_out1_t, w_main_t = jnp.transpose(w_out0), jnp.transpose(w_out1), jnp.transpose(w_main)
    lru_in_t = jnp.transpose(g_lru_in, (0, 2, 1)).reshape(2 * D_MODEL, D_MODEL)
    yg0, y0, x1 = _out_proj("out_proj0", [a_out, b_out], qkvg, C_GATE // D_MODEL, w_out0, x0, gmod[0], seq)

    h1, proj1 = _norm_proj("norm_proj1", x1, norm_g[1:2], shift[1], scale[1], g_lru_in, seq, F32)
    hs = _lru_fwd(proj1, conv_w, conv_b, lru_w_a[0], b_a, lru_w_x[0], b_x, lam, seq)

    yg1, dx2, dy1, dyh, dgm1, loss_rows, dfinal_rows = _last_layer_tail(
        hs, proj1, w_out1, w_out1_t, x1, gmod[1], final_g.reshape(1, D_MODEL), target, seq)

    dproj1, dcw, dvec, dw_a, dw_x = _lru_bwd(proj1, hs, dyh, conv_w, conv_b, lru_w_a[0], b_a, lru_w_x[0], b_x, lam, seq)
    dx1, dss1, dg1 = _norm_bwd("norm1_bwd", [(dproj1, 0)], lru_in_t, x1, norm_g[1:2], scale[1], dx2, seq)
    (p_w_out1,) = _dw("dw_out1", yg1, [dy1])
    (p_lru_in,) = _dw("dw_lru_in", h1, [dproj1], blocked=2 * D_MODEL // N_DEV)

    dy0, dgm0, du_a, du_b, dgate = _out_proj_bwd("out_proj0_bwd", dx1, gmod[0], y0, w_out0_t, seq,
                                                  attn=(a_out, b_out, qkvg))
    (p_w_out0,) = _dw("dw_out0", yg0, [dy0])

    rows_out = D_MODEL // N_DEV
    gpack1, gmeta1 = _pack_rows([dcw[0:4], dvec[0:4], dg1[0], dfinal_rows[0]])
    dwax = jnp.stack([dw_a, dw_x]).astype(BF16)
    own1 = [gpack1, dwax, p_lru_in, p_w_out1.reshape(N_DEV, rows_out, D_MODEL),
            p_w_out0.reshape(N_DEV, rows_out, D_MODEL)]
    grads1_handle, grads1_token = _exchange_start("grads1_start", own1[:2], own1[2:], after=p_w_out0)
    sinks_after = sinks + grads1_token[0, 0]
    dq_a, dkv_a, dbias, dsink = _swa_bwd(qkvg, du_a, a_out, lse_a, bias, sinks_after, seq)
    dq_b, dk_b, dv_b, df4 = _fox_bwd_t(q_aug, k_aug, kt_aug, qkvg, du_b, b_out, lse_b, seq)
    dfl_t, db_f = _fox_post(df4.reshape(N_HEADS, t_tok), fl_t, b_f, seq)
    parts0 = [(dq_b, C_BQ), (dk_b, C_BK), (dv_b, C_BV), (dq_a, C_AQ), (dgate, C_GATE), (dkv_a, C_AK)]
    pw_bq, pw_bk, pw_bv, pw_aq, pw_gate, pw_akv, pw_f = _dw("dw_attn_in", h0, [p for p, _ in parts0], rows_t=dfl_t)

    p_w_in = jnp.concatenate([pw_aq, pw_akv, pw_bq, pw_bk, pw_bv, jnp.transpose(pw_f).astype(BF16), pw_gate], axis=1)
    p_w_in = jnp.transpose(p_w_in.reshape(D_MODEL, N_DEV, SHARD_W_IN), (1, 0, 2))
    p_w_in = jnp.pad(p_w_in, ((0, 0), (0, 0), (0, SHARD_W_PAD - SHARD_W_IN)))
    own0 = [p_w_in]
    landed1 = _exchange_wait("grads1_wait", grads1_handle, after=p_w_in)
    grads0_handle, grads0_token = _exchange_start("grads0_start", [], own0, after=landed1[0])
    scale0 = scale[0] + grads0_token[0:1, 0:1]
    dx0, dss0, dg0 = _norm_bwd("norm0_bwd", parts0, w_main_t, x0, norm_g[0:1], scale0, dx1, seq,
                               rows_part=(dfl_t, wf_t))
    dbias_t = _bias_reduce(dbias.reshape(N_HEADS, BLOCK * 2 * BLOCK), onehot)

    gpack0, gmeta0 = _pack_rows([jnp.transpose(dbias_t), dg0[0], dsink[:, 0], db_f[:, 0], loss_rows[0]])
    dmod = jnp.stack([jnp.concatenate([dss[:, 0], dss[:, 1], dgm[:, 0]], axis=1)
                      for dss, dgm in ((dss0, dgm0), (dss1, dgm1))], axis=1)
    g_small0, g_dmod = _exchange("exchange_small", [gpack0, dmod], [])
    landed0 = _exchange_wait("grads0_wait", grads0_handle, after=g_small0)
    (r_w_in,) = (_with_own(g, lax.dynamic_index_in_dim(a, me, 0, keepdims=False), me)
                 for g, a in zip(landed0, own0))
    g_small1, g_dwax = (_with_own(g, a, me) for g, a in zip(landed1[:2], own1[:2]))
    r_lru_in, r_w_out1, r_w_out0 = (_with_own(g, lax.dynamic_index_in_dim(a, me, 0, keepdims=False), me)
                                    for g, a in zip(landed1[2:], own1[2:]))

    d_rel, d_g0, d_sinks, d_b_f, loss_cols = _unpack_rows(_sum_leading("sum_small0", g_small0), gmeta0)
    loss = jnp.sum(loss_cols)
    d_cw, d_vec, d_g1, d_final_g = _unpack_rows(_sum_leading("sum_small1", g_small1), gmeta1)
    d_norm_g = jnp.stack([d_g0, d_g1])
    d_wax = _sum_leading("sum_dwax", g_dwax.reshape(N_DEV, 2 * LRU_BLOCKS * LRU_BLOCK_W, LRU_BLOCK_W))
    d_wa, d_wx = d_wax[:LRU_BLOCKS * LRU_BLOCK_W], d_wax[LRU_BLOCKS * LRU_BLOCK_W:]
    cols = lambda a: lax.dynamic_slice(a, (0, me * LRU_BLOCK_W), (a.shape[0], LRU_BLOCK_W))
    dmod_all = g_dmod.reshape(N_DEV * nseq, 2 * 3 * D_MODEL)
    d_ada_b = _sum_leading("sum_ada_b", dmod_all.reshape(N_DEV * nseq, 2 * 3 * D_MODEL // 128, 128)).reshape(2, 3 * D_MODEL)
    dmod_slice = lax.dynamic_slice(dmod_all.reshape(N_DEV * nseq, 2, N_DEV, ncol), (0, 0, me, 0),
                                   (N_DEV * nseq, 2, 1, ncol)).reshape(N_DEV * nseq, 2, ncol)
    d_ada_w = _ada_w_grad(c_all, jnp.transpose(dmod_slice, (1, 0, 2)))

    given = dict(
        rel_bias=(rel_bias, m_rel_bias, v_rel_bias), norm_g=(norm_g, m_norm_g, v_norm_g),
        ada_w=(ada_w, m_ada_w, v_ada_w), ada_b=(ada_b, m_ada_b, v_ada_b),
        attn_w_in=(attn_w_in, m_attn_w_in, v_attn_w_in), attn_sinks=(attn_sinks, m_attn_sinks, v_attn_sinks),
        attn_b_f=(attn_b_f, m_attn_b_f, v_attn_b_f), attn_w_out=(attn_w_out, m_attn_w_out, v_attn_w_out),
        lru_w_in=(lru_w_in, m_lru_w_in, v_lru_w_in), lru_conv_w=(lru_conv_w, m_lru_conv_w, v_lru_conv_w),
        lru_conv_b=(lru_conv_b, m_lru_conv_b, v_lru_conv_b), lru_w_a=(lru_w_a, m_lru_w_a, v_lru_w_a),
        lru_b_a=(lru_b_a, m_lru_b_a, v_lru_b_a), lru_w_x=(lru_w_x, m_lru_w_x, v_lru_w_x),
        lru_b_x=(lru_b_x, m_lru_b_x, v_lru_b_x), lru_lambda=(lru_lambda, m_lru_lambda, v_lru_lambda),
        lru_w_out=(lru_w_out, m_lru_w_out, v_lru_w_out), final_g=(final_g, m_final_g, v_final_g))
    results = {}

    def big(name, shape2d, g=None, parts=None):
        w, m, v = (a.reshape(shape2d) for a in given[name])
        outs = _adamw("adamw_" + name, w, m, v, g=g, parts=parts)
        results[name] = tuple(o.reshape(given[name][0].shape) for o in outs)

    big("ada_w", (2 * D_MODEL, ncol), g=d_ada_w.reshape(2 * D_MODEL, ncol))
    big("attn_w_in", (D_MODEL, SHARD_W_IN), parts=r_w_in)
    big("attn_w_out", (rows_out, D_MODEL), parts=r_w_out0)
    big("lru_w_in", (D_MODEL, 2 * D_MODEL // N_DEV), parts=r_lru_in)
    big("lru_w_out", (rows_out, D_MODEL), parts=r_w_out1)

    small_grads = dict(
        rel_bias=d_rel, norm_g=d_norm_g, ada_b=d_ada_b, attn_sinks=d_sinks.reshape(1, N_HEADS),
        attn_b_f=d_b_f.reshape(1, N_HEADS), lru_conv_w=cols(d_cw).reshape(1, 4, LRU_BLOCK_W),
        lru_conv_b=cols(d_vec[0:1]), lru_w_a=d_wa.reshape(lru_w_a.shape), lru_b_a=cols(d_vec[1:2]),
        lru_w_x=d_wx.reshape(lru_w_x.shape), lru_b_x=cols(d_vec[2:3]), lru_lambda=cols(d_vec[3:4]),
        final_g=d_final_g)
    small = [n for n in WEIGHTS if n not in BIG]
    as2d = lambda a: a.reshape(-1, a.shape[-1])
    outs = _adamw_many("adamw_small", [tuple(as2d(a) for a in given[n]) + (as2d(small_grads[n]),) for n in small])
    for n, group in zip(small, outs):
        results[n] = tuple(o.reshape(given[n][0].shape) for o in group)

    grad_x = dx0.reshape(x.shape)
    out = [loss, grad_x]
    for j in range(4):
        out += [results[n][j] for n in WEIGHTS]
    return tuple(out)
```
